```python
import math
import jax, jax.numpy as jnp
from jax import lax
import numpy as np

D_MODEL = 1024
BATCH = 8
SEQ = 4096
DEPTH = 2

HEAD_DIM = 64
N_HEADS = D_MODEL // HEAD_DIM
SB_HEADS = N_HEADS // 4
DIL_HEADS = N_HEADS - SB_HEADS
DIL_PATTERNS = ((128, 1), (512, 4), (2048, 16))
DIL_GROUP_HEADS = DIL_HEADS // len(DIL_PATTERNS)
OUT_WIDTH = (SB_HEADS + DIL_GROUP_HEADS) * HEAD_DIM
BLOCK = 128
N_BUCKETS = 32
MAX_DISTANCE = 2048
D_FF = 2816
RWKV_HEAD = 64
RWKV_HEADS = D_MODEL // RWKV_HEAD
D_DECAY_LORA = 64
D_AAA_LORA = 64
D_GATE_LORA = 160
NORM_EPS = 1e-6
GN_EPS = 64e-5
NEG_INF = -1e30
N_EVEN = (DEPTH + 1) // 2
N_ODD = DEPTH // 2

kernel_name = "hybrid_stickbreak_dilated_rwkv7_macaron"


def rms_norm(x, g, eps=NORM_EPS):
    x32 = x.astype(jnp.float32)
    y = x32 * lax.rsqrt(jnp.mean(x32 * x32, axis=-1, keepdims=True) + eps)
    return (y * g.astype(jnp.float32)).astype(x.dtype)


def swiglu(h, w_gate, w_up, w_down):
    return (jax.nn.silu(h @ w_gate) * (h @ w_up)) @ w_down


def t5_bucket(dist):
    max_exact = N_BUCKETS // 2
    d = jnp.maximum(dist, 1).astype(jnp.float32)
    large = max_exact + (jnp.log(d / max_exact) / math.log(MAX_DISTANCE / max_exact)
                         * (N_BUCKETS - max_exact)).astype(jnp.int32)
    large = jnp.minimum(large, N_BUCKETS - 1)
    return jnp.where(dist < max_exact, dist, large)


def stick_breaking_attention(q, k, v):
    B, S, H, Dh = q.shape
    nb = S // BLOCK
    scale = Dh ** -0.5
    qb = q.reshape(B, nb, BLOCK, H, Dh).transpose(1, 0, 3, 2, 4)
    kpos = jnp.arange(S)

    def one_block(args):
        qi, n = args
        z = jnp.einsum('bhqd,bshd->bhqs', qi, k).astype(jnp.float32) * scale
        qpos = n * BLOCK + jnp.arange(BLOCK)
        strict = kpos[None, :] < qpos[:, None]
        log_keep = jnp.where(strict, jax.nn.log_sigmoid(-z), 0.0)
        after = lax.cumsum(log_keep, axis=3, reverse=True) - log_keep
        weight = jnp.where(strict, jnp.exp(jax.nn.log_sigmoid(z) + after), 0.0)
        return jnp.einsum('bhqs,bshd->bqhd', weight.astype(v.dtype), v)

    out = lax.map(one_block, (qb, jnp.arange(nb)))
    return out.transpose(1, 0, 2, 3, 4).reshape(B, S, H, Dh)


def dilated_window_attention(q, k, v, bias_table, window, dilation):
    B, S, G, Dh = q.shape
    r = dilation
    L = S // r
    span = window // r
    nb = -(-L // BLOCK)
    Lp = nb * BLOCK

    def to_sub(t):
        t = t.reshape(B, L, r, G, Dh).transpose(0, 2, 1, 3, 4)
        t = jnp.pad(t, ((0, 0), (0, 0), (0, Lp - L), (0, 0), (0, 0)))
        return t.reshape(B, r, nb, BLOCK, G, Dh)

    def with_prev(t):
        prev = jnp.pad(t, ((0, 0), (0, 0), (1, 0), (0, 0), (0, 0), (0, 0)))[:, :, :-1]
        return jnp.concatenate([prev, t], axis=3)

    qs = to_sub(q)
    kw, vw = with_prev(to_sub(k)), with_prev(to_sub(v))
    logits = jnp.einsum('brnqgd,brnkgd->brngqk', qs, kw).astype(jnp.float32) * (Dh ** -0.5)
    qi = jnp.arange(BLOCK)[:, None]
    kj = jnp.arange(2 * BLOCK)[None, :] - BLOCK
    dist = qi - kj
    in_window = (dist >= 0) & (dist <= span)
    has_key = (jnp.arange(nb)[:, None, None] > 0) | (kj[None] >= 0)
    mask = (in_window[None] & has_key)[:, None]
    bias = bias_table[t5_bucket(jnp.maximum(dist, 0) * r)].astype(jnp.float32).transpose(2, 0, 1)
    logits = jnp.where(mask, logits + bias, NEG_INF)
    m = jnp.max(logits, axis=-1, keepdims=True)
    p = jnp.exp(logits - m)
    den = jnp.sum(p, axis=-1, keepdims=True)
    o = jnp.einsum('brngqk,brnkgd->brnqgd', (p / den).astype(v.dtype), vw)
    lse = (m + jnp.log(den))[..., 0].transpose(0, 1, 2, 4, 3)

    def from_sub(t):
        t = t.reshape((B, r, Lp) + t.shape[4:])[:, :, :L]
        return jnp.swapaxes(t, 1, 2).reshape((B, S) + t.shape[3:])

    return from_sub(o), from_sub(lse)


def parallel_attention_mixer(h, w_in, q_norm, k_norm, w_out, rel_bias):
    B, S, _ = h.shape
    proj = h @ w_in
    a_cols = 3 * SB_HEADS * HEAD_DIM
    sb = proj[..., :a_cols].reshape(B, S, 3, SB_HEADS, HEAD_DIM)
    dl = proj[..., a_cols:].reshape(B, S, 3, DIL_HEADS, HEAD_DIM)
    out_a = stick_breaking_attention(sb[:, :, 0], sb[:, :, 1], sb[:, :, 2])
    q = rms_norm(dl[:, :, 0], q_norm)
    k = rms_norm(dl[:, :, 1], k_norm)
    v = dl[:, :, 2]
    outs, lses = [], []
    for g, (window, dilation) in enumerate(DIL_PATTERNS):
        sl = slice(g * DIL_GROUP_HEADS, (g + 1) * DIL_GROUP_HEADS)
        o, l = dilated_window_attention(q[:, :, sl], k[:, :, sl], v[:, :, sl],
                                        rel_bias[:, sl], window, dilation)
        outs.append(o)
        lses.append(l)
    alpha = jax.nn.softmax(jnp.stack(lses, axis=0), axis=0)
    out_b = jnp.sum(alpha[..., None].astype(v.dtype) * jnp.stack(outs, axis=0), axis=0)
    merged = jnp.concatenate([out_a.reshape(B, S, -1), out_b.reshape(B, S, -1)], axis=-1)
    return merged @ w_out


def rwkv7_step(state, inp):
    r_t, w_t, k_t, v_t, a_t, b_t = inp
    sa = jnp.einsum('bhvk,bhk->bhv', state, a_t)
    state = (state * w_t[:, :, None, :] + sa[..., None] * b_t[:, :, None, :]
             + v_t[..., None] * k_t[:, :, None, :])
    return state, jnp.einsum('bhvk,bhk->bhv', state, r_t)


def rwkv7_time_mix(h, mix, w0, w1, w2, a0, a1, a2, g1, g2, k_k, k_a, r_k,
                   w_r, w_k, w_v, w_o, lnx_g, lnx_b):
    B, S, D = h.shape
    H, N = RWKV_HEADS, RWKV_HEAD
    f32 = jnp.float32
    xx = jnp.pad(h, ((0, 0), (1, 0), (0, 0)))[:, :-1] - h
    xr, xw, xk, xv, xa, xg = [h + xx * mix[i] for i in range(6)]
    r = (xr @ w_r).astype(f32)
    k = (xk @ w_k).astype(f32)
    v = (xv @ w_v).astype(f32)
    w_log = -jax.nn.softplus(-(w0 + jnp.tanh(xw @ w1) @ w2).astype(f32)) - 0.5
    decay = jnp.exp(-jnp.exp(w_log))
    a = jax.nn.sigmoid((a0 + (xa @ a1) @ a2).astype(f32))
    g = jax.nn.sigmoid(xg @ g1) @ g2
    heads = lambda t: t.reshape(B, S, H, N)
    kk = heads(k * k_k.astype(f32))
    kk = kk / jnp.maximum(jnp.sqrt(jnp.sum(kk * kk, axis=-1, keepdims=True)), 1e-12)
    k = k * (1.0 + (a - 1.0) * k_a.astype(f32))
    r_h, k_h, v_h, a_h, w_h = heads(r), heads(k), heads(v), heads(a), heads(decay)
    tm = lambda t: jnp.swapaxes(t, 0, 1)
    seqs = (tm(r_h), tm(w_h), tm(k_h), tm(v_h), tm(-kk), tm(kk * a_h))
    _, y = lax.scan(rwkv7_step, jnp.zeros((B, H, N, N), f32), seqs)
    y = jnp.swapaxes(y, 0, 1)
    mu = jnp.mean(y, axis=-1, keepdims=True)
    var = jnp.mean(jnp.square(y - mu), axis=-1, keepdims=True)
    y = ((y - mu) * lax.rsqrt(var + GN_EPS)).reshape(B, S, D)
    y = y * lnx_g.astype(f32) + lnx_b.astype(f32)
    y = y + (jnp.sum(r_h * k_h * r_k.astype(f32), axis=-1, keepdims=True) * v_h).reshape(B, S, D)
    return (y.astype(h.dtype) * g) @ w_o


def _fwd_setup_inputs(seed: int = 0) -> dict:
    key = jax.random.key(seed)
    ks = iter(jax.random.split(key, 48))
    D = D_MODEL
    nrm = lambda shape, scale: jax.random.normal(next(ks), shape, jnp.float32) * scale
    uni = lambda shape, lo, hi: jax.random.uniform(next(ks), shape, jnp.float32, lo, hi)
    return {
        "x": nrm((BATCH, SEQ, D), 1.0),
        "ffn_norm": 1.0 + nrm((DEPTH, 2, D), 0.05),
        "ffn_w_gate": nrm((DEPTH, 2, D, D_FF), D ** -0.5),
        "ffn_w_up": nrm((DEPTH, 2, D, D_FF), D ** -0.5),
        "ffn_w_down": nrm((DEPTH, 2, D_FF, D), D_FF ** -0.5),
        "mix_norm": 1.0 + nrm((DEPTH, D), 0.05),
        "rel_bias": nrm((N_BUCKETS, DIL_HEADS), 0.3),
        "attn_w_in": nrm((N_EVEN, D, 3 * N_HEADS * HEAD_DIM), D ** -0.5),
        "attn_q_norm": 1.0 + nrm((N_EVEN, HEAD_DIM), 0.05),
        "attn_k_norm": 1.0 + nrm((N_EVEN, HEAD_DIM), 0.05),
        "attn_w_out": nrm((N_EVEN, OUT_WIDTH, D), OUT_WIDTH ** -0.5),
        "rw_mix": uni((N_ODD, 6, D), 0.0, 1.0),
        "rw_w0": uni((N_ODD, D), -4.0, 0.0),
        "rw_w1": nrm((N_ODD, D, D_DECAY_LORA), D ** -0.5),
        "rw_w2": nrm((N_ODD, D_DECAY_LORA, D), 0.1 * D_DECAY_LORA ** -0.5),
        "rw_a0": nrm((N_ODD, D), 0.1),
        "rw_a1": nrm((N_ODD, D, D_AAA_LORA), D ** -0.5),
        "rw_a2": nrm((N_ODD, D_AAA_LORA, D), 0.1 * D_AAA_LORA ** -0.5),
        "rw_g1": nrm((N_ODD, D, D_GATE_LORA), D ** -0.5),
        "rw_g2": nrm((N_ODD, D_GATE_LORA, D), D_GATE_LORA ** -0.5),
        "rw_kk": 0.85 + nrm((N_ODD, D), 0.05),
        "rw_ka": 1.0 + nrm((N_ODD, D), 0.05),
        "rw_rk": nrm((N_ODD, RWKV_HEADS, RWKV_HEAD), 0.1),
        "rw_wr": nrm((N_ODD, D, D), D ** -0.5),
        "rw_wk": nrm((N_ODD, D, D), D ** -0.5),
        "rw_wv": nrm((N_ODD, D, D), D ** -0.5),
        "rw_wo": nrm((N_ODD, D, D), D ** -0.5),
        "rw_lnx_g": 1.0 + nrm((N_ODD, D), 0.05),
        "rw_lnx_b": nrm((N_ODD, D), 0.01),
    }


def _fwd_reference(x, ffn_norm, ffn_w_gate, ffn_w_up, ffn_w_down, mix_norm, rel_bias,
              attn_w_in, attn_q_norm, attn_k_norm, attn_w_out,
              rw_mix, rw_w0, rw_w1, rw_w2, rw_a0, rw_a1, rw_a2, rw_g1, rw_g2,
              rw_kk, rw_ka, rw_rk, rw_wr, rw_wk, rw_wv, rw_wo, rw_lnx_g, rw_lnx_b):
    for layer in range(DEPTH):
        x = x + 0.5 * swiglu(rms_norm(x, ffn_norm[layer, 0]), ffn_w_gate[layer, 0],
                             ffn_w_up[layer, 0], ffn_w_down[layer, 0])
        h = rms_norm(x, mix_norm[layer])
        if layer % 2 == 0:
            e = layer // 2
            x = x + parallel_attention_mixer(h, attn_w_in[e], attn_q_norm[e], attn_k_norm[e],
                                             attn_w_out[e], rel_bias)
        else:
            o = layer // 2
            x = x + rwkv7_time_mix(h, rw_mix[o], rw_w0[o], rw_w1[o], rw_w2[o], rw_a0[o],
                                   rw_a1[o], rw_a2[o], rw_g1[o], rw_g2[o], rw_kk[o], rw_ka[o],
                                   rw_rk[o], rw_wr[o], rw_wk[o], rw_wv[o], rw_wo[o],
                                   rw_lnx_g[o], rw_lnx_b[o])
        x = x + 0.5 * swiglu(rms_norm(x, ffn_norm[layer, 1]), ffn_w_gate[layer, 1],
                             ffn_w_up[layer, 1], ffn_w_down[layer, 1])
    return x


import jax as _jax
import jax.numpy as _jnp

TWIN_FORMAT = 'train_step'
FWD_PARAMS = ['x', 'ffn_norm', 'ffn_w_gate', 'ffn_w_up', 'ffn_w_down', 'mix_norm', 'rel_bias', 'attn_w_in', 'attn_q_norm', 'attn_k_norm', 'attn_w_out', 'rw_mix', 'rw_w0', 'rw_w1', 'rw_w2', 'rw_a0', 'rw_a1', 'rw_a2', 'rw_g1', 'rw_g2', 'rw_kk', 'rw_ka', 'rw_rk', 'rw_wr', 'rw_wk', 'rw_wv', 'rw_wo', 'rw_lnx_g', 'rw_lnx_b']
TWIN_WEIGHTS = ['ffn_norm', 'ffn_w_gate', 'ffn_w_up', 'ffn_w_down', 'mix_norm', 'rel_bias', 'attn_w_in', 'attn_q_norm', 'attn_k_norm', 'attn_w_out', 'rw_mix', 'rw_w0', 'rw_w1', 'rw_w2', 'rw_a0', 'rw_a1', 'rw_a2', 'rw_g1', 'rw_g2', 'rw_kk', 'rw_ka', 'rw_rk', 'rw_wr', 'rw_wk', 'rw_wv', 'rw_wo', 'rw_lnx_g', 'rw_lnx_b']
TWIN_DIFF_INPUT = 'x'
TWIN_INPUTS = ['x', 'ffn_norm', 'ffn_w_gate', 'ffn_w_up', 'ffn_w_down', 'mix_norm', 'rel_bias', 'attn_w_in', 'attn_q_norm', 'attn_k_norm', 'attn_w_out', 'rw_mix', 'rw_w0', 'rw_w1', 'rw_w2', 'rw_a0', 'rw_a1', 'rw_a2', 'rw_g1', 'rw_g2', 'rw_kk', 'rw_ka', 'rw_rk', 'rw_wr', 'rw_wk', 'rw_wv', 'rw_wo', 'rw_lnx_g', 'rw_lnx_b', 'loss_target', 'm_ffn_norm', 'm_ffn_w_gate', 'm_ffn_w_up', 'm_ffn_w_down', 'm_mix_norm', 'm_rel_bias', 'm_attn_w_in', 'm_attn_q_norm', 'm_attn_k_norm', 'm_attn_w_out', 'm_rw_mix', 'm_rw_w0', 'm_rw_w1', 'm_rw_w2', 'm_rw_a0', 'm_rw_a1', 'm_rw_a2', 'm_rw_g1', 'm_rw_g2', 'm_rw_kk', 'm_rw_ka', 'm_rw_rk', 'm_rw_wr', 'm_rw_wk', 'm_rw_wv', 'm_rw_wo', 'm_rw_lnx_g', 'm_rw_lnx_b', 'v_ffn_norm', 'v_ffn_w_gate', 'v_ffn_w_up', 'v_ffn_w_down', 'v_mix_norm', 'v_rel_bias', 'v_attn_w_in', 'v_attn_q_norm', 'v_attn_k_norm', 'v_attn_w_out', 'v_rw_mix', 'v_rw_w0', 'v_rw_w1', 'v_rw_w2', 'v_rw_a0', 'v_rw_a1', 'v_rw_a2', 'v_rw_g1', 'v_rw_g2', 'v_rw_kk', 'v_rw_ka', 'v_rw_rk', 'v_rw_wr', 'v_rw_wk', 'v_rw_wv', 'v_rw_wo', 'v_rw_lnx_g', 'v_rw_lnx_b']
TWIN_OUTPUTS = ['loss', 'grad_x', 'grad_ffn_norm', 'grad_ffn_w_gate', 'grad_ffn_w_up', 'grad_ffn_w_down', 'grad_mix_norm', 'grad_rel_bias', 'grad_attn_w_in', 'grad_attn_q_norm', 'grad_attn_k_norm', 'grad_attn_w_out', 'grad_rw_mix', 'grad_rw_w0', 'grad_rw_w1', 'grad_rw_w2', 'grad_rw_a0', 'grad_rw_a1', 'grad_rw_a2', 'grad_rw_g1', 'grad_rw_g2', 'grad_rw_kk', 'grad_rw_ka', 'grad_rw_rk', 'grad_rw_wr', 'grad_rw_wk', 'grad_rw_wv', 'grad_rw_wo', 'grad_rw_lnx_g', 'grad_rw_lnx_b', 'delta_ffn_norm', 'delta_ffn_w_gate', 'delta_ffn_w_up', 'delta_ffn_w_down', 'delta_mix_norm', 'delta_rel_bias', 'delta_attn_w_in', 'delta_attn_q_norm', 'delta_attn_k_norm', 'delta_attn_w_out', 'delta_rw_mix', 'delta_rw_w0', 'delta_rw_w1', 'delta_rw_w2', 'delta_rw_a0', 'delta_rw_a1', 'delta_rw_a2', 'delta_rw_g1', 'delta_rw_g2', 'delta_rw_kk', 'delta_rw_ka', 'delta_rw_rk', 'delta_rw_wr', 'delta_rw_wk', 'delta_rw_wv', 'delta_rw_wo', 'delta_rw_lnx_g', 'delta_rw_lnx_b', 'new_m_ffn_norm', 'new_m_ffn_w_gate', 'new_m_ffn_w_up', 'new_m_ffn_w_down', 'new_m_mix_norm', 'new_m_rel_bias', 'new_m_attn_w_in', 'new_m_attn_q_norm', 'new_m_attn_k_norm', 'new_m_attn_w_out', 'new_m_rw_mix', 'new_m_rw_w0', 'new_m_rw_w1', 'new_m_rw_w2', 'new_m_rw_a0', 'new_m_rw_a1', 'new_m_rw_a2', 'new_m_rw_g1', 'new_m_rw_g2', 'new_m_rw_kk', 'new_m_rw_ka', 'new_m_rw_rk', 'new_m_rw_wr', 'new_m_rw_wk', 'new_m_rw_wv', 'new_m_rw_wo', 'new_m_rw_lnx_g', 'new_m_rw_lnx_b', 'new_v_ffn_norm', 'new_v_ffn_w_gate', 'new_v_ffn_w_up', 'new_v_ffn_w_down', 'new_v_mix_norm', 'new_v_rel_bias', 'new_v_attn_w_in', 'new_v_attn_q_norm', 'new_v_attn_k_norm', 'new_v_attn_w_out', 'new_v_rw_mix', 'new_v_rw_w0', 'new_v_rw_w1', 'new_v_rw_w2', 'new_v_rw_a0', 'new_v_rw_a1', 'new_v_rw_a2', 'new_v_rw_g1', 'new_v_rw_g2', 'new_v_rw_kk', 'new_v_rw_ka', 'new_v_rw_rk', 'new_v_rw_wr', 'new_v_rw_wk', 'new_v_rw_wv', 'new_v_rw_wo', 'new_v_rw_lnx_g', 'new_v_rw_lnx_b']
TWIN_LEAF_KINDS = {'loss': 'loss', 'grad_x': 'grad_x', 'grad_ffn_norm': 'grad_w', 'grad_ffn_w_gate': 'grad_w', 'grad_ffn_w_up': 'grad_w', 'grad_ffn_w_down': 'grad_w', 'grad_mix_norm': 'grad_w', 'grad_rel_bias': 'grad_w', 'grad_attn_w_in': 'grad_w', 'grad_attn_q_norm': 'grad_w', 'grad_attn_k_norm': 'grad_w', 'grad_attn_w_out': 'grad_w', 'grad_rw_mix': 'grad_w', 'grad_rw_w0': 'grad_w', 'grad_rw_w1': 'grad_w', 'grad_rw_w2': 'grad_w', 'grad_rw_a0': 'grad_w', 'grad_rw_a1': 'grad_w', 'grad_rw_a2': 'grad_w', 'grad_rw_g1': 'grad_w', 'grad_rw_g2': 'grad_w', 'grad_rw_kk': 'grad_w', 'grad_rw_ka': 'grad_w', 'grad_rw_rk': 'grad_w', 'grad_rw_wr': 'grad_w', 'grad_rw_wk': 'grad_w', 'grad_rw_wv': 'grad_w', 'grad_rw_wo': 'grad_w', 'grad_rw_lnx_g': 'grad_w', 'grad_rw_lnx_b': 'grad_w', 'delta_ffn_norm': 'delta_w', 'delta_ffn_w_gate': 'delta_w', 'delta_ffn_w_up': 'delta_w', 'delta_ffn_w_down': 'delta_w', 'delta_mix_norm': 'delta_w', 'delta_rel_bias': 'delta_w', 'delta_attn_w_in': 'delta_w', 'delta_attn_q_norm': 'delta_w', 'delta_attn_k_norm': 'delta_w', 'delta_attn_w_out': 'delta_w', 'delta_rw_mix': 'delta_w', 'delta_rw_w0': 'delta_w', 'delta_rw_w1': 'delta_w', 'delta_rw_w2': 'delta_w', 'delta_rw_a0': 'delta_w', 'delta_rw_a1': 'delta_w', 'delta_rw_a2': 'delta_w', 'delta_rw_g1': 'delta_w', 'delta_rw_g2': 'delta_w', 'delta_rw_kk': 'delta_w', 'delta_rw_ka': 'delta_w', 'delta_rw_rk': 'delta_w', 'delta_rw_wr': 'delta_w', 'delta_rw_wk': 'delta_w', 'delta_rw_wv': 'delta_w', 'delta_rw_wo': 'delta_w', 'delta_rw_lnx_g': 'delta_w', 'delta_rw_lnx_b': 'delta_w', 'new_m_ffn_norm': 'new_m', 'new_m_ffn_w_gate': 'new_m', 'new_m_ffn_w_up': 'new_m', 'new_m_ffn_w_down': 'new_m', 'new_m_mix_norm': 'new_m', 'new_m_rel_bias': 'new_m', 'new_m_attn_w_in': 'new_m', 'new_m_attn_q_norm': 'new_m', 'new_m_attn_k_norm': 'new_m', 'new_m_attn_w_out': 'new_m', 'new_m_rw_mix': 'new_m', 'new_m_rw_w0': 'new_m', 'new_m_rw_w1': 'new_m', 'new_m_rw_w2': 'new_m', 'new_m_rw_a0': 'new_m', 'new_m_rw_a1': 'new_m', 'new_m_rw_a2': 'new_m', 'new_m_rw_g1': 'new_m', 'new_m_rw_g2': 'new_m', 'new_m_rw_kk': 'new_m', 'new_m_rw_ka': 'new_m', 'new_m_rw_rk': 'new_m', 'new_m_rw_wr': 'new_m', 'new_m_rw_wk': 'new_m', 'new_m_rw_wv': 'new_m', 'new_m_rw_wo': 'new_m', 'new_m_rw_lnx_g': 'new_m', 'new_m_rw_lnx_b': 'new_m', 'new_v_ffn_norm': 'new_v', 'new_v_ffn_w_gate': 'new_v', 'new_v_ffn_w_up': 'new_v', 'new_v_ffn_w_down': 'new_v', 'new_v_mix_norm': 'new_v', 'new_v_rel_bias': 'new_v', 'new_v_attn_w_in': 'new_v', 'new_v_attn_q_norm': 'new_v', 'new_v_attn_k_norm': 'new_v', 'new_v_attn_w_out': 'new_v', 'new_v_rw_mix': 'new_v', 'new_v_rw_w0': 'new_v', 'new_v_rw_w1': 'new_v', 'new_v_rw_w2': 'new_v', 'new_v_rw_a0': 'new_v', 'new_v_rw_a1': 'new_v', 'new_v_rw_a2': 'new_v', 'new_v_rw_g1': 'new_v', 'new_v_rw_g2': 'new_v', 'new_v_rw_kk': 'new_v', 'new_v_rw_ka': 'new_v', 'new_v_rw_rk': 'new_v', 'new_v_rw_wr': 'new_v', 'new_v_rw_wk': 'new_v', 'new_v_rw_wv': 'new_v', 'new_v_rw_wo': 'new_v', 'new_v_rw_lnx_g': 'new_v', 'new_v_rw_lnx_b': 'new_v'}


def _forward(args):
    return _fwd_reference(*[args[k] for k in FWD_PARAMS])


def _output_shape():
    out = _jax.eval_shape(lambda: _forward(_fwd_setup_inputs(0)))
    return out.shape, out.dtype

N_MICROBATCH = 1
ADAM_LR = 0.001
ADAM_B1 = 0.9
ADAM_B2 = 0.999
ADAM_EPS = 1e-08
ADAM_WD = 0.01
ADAM_STEP = 10
PER_EXAMPLE_BATCH_AXIS = {'x': 0, 'loss_target': 0}
SHARED_INPUTS = []
_WEIGHT_DTYPES = {'ffn_norm': _jnp.float32, 'ffn_w_gate': _jnp.float32, 'ffn_w_up': _jnp.float32, 'ffn_w_down': _jnp.float32, 'mix_norm': _jnp.float32, 'rel_bias': _jnp.float32, 'attn_w_in': _jnp.float32, 'attn_q_norm': _jnp.float32, 'attn_k_norm': _jnp.float32, 'attn_w_out': _jnp.float32, 'rw_mix': _jnp.float32, 'rw_w0': _jnp.float32, 'rw_w1': _jnp.float32, 'rw_w2': _jnp.float32, 'rw_a0': _jnp.float32, 'rw_a1': _jnp.float32, 'rw_a2': _jnp.float32, 'rw_g1': _jnp.float32, 'rw_g2': _jnp.float32, 'rw_kk': _jnp.float32, 'rw_ka': _jnp.float32, 'rw_rk': _jnp.float32, 'rw_wr': _jnp.float32, 'rw_wk': _jnp.float32, 'rw_wv': _jnp.float32, 'rw_wo': _jnp.float32, 'rw_lnx_g': _jnp.float32, 'rw_lnx_b': _jnp.float32}
MOMENT_SCALE = {'ffn_norm': 6.144318e+00, 'ffn_w_gate': 9.702471e-02, 'ffn_w_up': 9.514949e-02, 'ffn_w_down': 1.558608e-01, 'mix_norm': 6.050438e+00, 'rel_bias': 1.478720e-01, 'attn_w_in': 2.408539e-01, 'attn_q_norm': 1.179091e+00, 'attn_k_norm': 1.172657e+00, 'attn_w_out': 3.569325e-01, 'rw_mix': 2.961066e-01, 'rw_w0': 1.359884e-01, 'rw_w1': 7.540794e-03, 'rw_w2': 1.904624e-02, 'rw_a0': 6.124247e-01, 'rw_a1': 4.015769e-02, 'rw_a2': 9.882648e-02, 'rw_g1': 3.367319e-01, 'rw_g2': 8.917382e+00, 'rw_kk': 1.548095e-01, 'rw_ka': 1.222373e+00, 'rw_rk': 7.784981e+00, 'rw_wr': 2.646789e-01, 'rw_wk': 2.829811e-01, 'rw_wv': 3.531217e-01, 'rw_wo': 3.853783e-01, 'rw_lnx_g': 1.569225e+01, 'rw_lnx_b': 5.589051e-01}


def _to_microbatches(a, axis):
    t = _jnp.moveaxis(a, axis, 0)
    t = t.reshape((N_MICROBATCH, t.shape[0] // N_MICROBATCH) + t.shape[1:])
    return _jnp.moveaxis(t, 1, axis + 1)


def setup_inputs(seed: int = 0) -> dict:
    inp = _fwd_setup_inputs(seed)
    key = _jax.random.fold_in(_jax.random.key(seed), 7919)
    shape, _ = _output_shape()
    out = dict(inp)
    out["loss_target"] = _jax.random.normal(_jax.random.fold_in(key, 0), shape, _jnp.float32)
    for i, name in enumerate(TWIN_WEIGHTS):
        w = inp[name].astype(_jnp.float32)
        if MOMENT_SCALE is None:
            s = _jnp.sqrt(_jnp.mean(_jnp.square(w)) + 1e-30)
        else:
            s = MOMENT_SCALE[name]
        km, kv = _jax.random.split(_jax.random.fold_in(key, i + 1))
        out[name] = w
        out["m_" + name] = s * _jax.random.normal(km, w.shape, _jnp.float32)
        out["v_" + name] = (s * s) * _jax.random.uniform(kv, w.shape, _jnp.float32, 0.5, 1.5)
    if N_MICROBATCH > 1:
        for name, axis in PER_EXAMPLE_BATCH_AXIS.items():
            out[name] = _to_microbatches(out[name], axis)
    return {'x': out['x'], 'ffn_norm': out['ffn_norm'], 'ffn_w_gate': out['ffn_w_gate'], 'ffn_w_up': out['ffn_w_up'], 'ffn_w_down': out['ffn_w_down'], 'mix_norm': out['mix_norm'], 'rel_bias': out['rel_bias'], 'attn_w_in': out['attn_w_in'], 'attn_q_norm': out['attn_q_norm'], 'attn_k_norm': out['attn_k_norm'], 'attn_w_out': out['attn_w_out'], 'rw_mix': out['rw_mix'], 'rw_w0': out['rw_w0'], 'rw_w1': out['rw_w1'], 'rw_w2': out['rw_w2'], 'rw_a0': out['rw_a0'], 'rw_a1': out['rw_a1'], 'rw_a2': out['rw_a2'], 'rw_g1': out['rw_g1'], 'rw_g2': out['rw_g2'], 'rw_kk': out['rw_kk'], 'rw_ka': out['rw_ka'], 'rw_rk': out['rw_rk'], 'rw_wr': out['rw_wr'], 'rw_wk': out['rw_wk'], 'rw_wv': out['rw_wv'], 'rw_wo': out['rw_wo'], 'rw_lnx_g': out['rw_lnx_g'], 'rw_lnx_b': out['rw_lnx_b'], 'loss_target': out['loss_target'], 'm_ffn_norm': out['m_ffn_norm'], 'm_ffn_w_gate': out['m_ffn_w_gate'], 'm_ffn_w_up': out['m_ffn_w_up'], 'm_ffn_w_down': out['m_ffn_w_down'], 'm_mix_norm': out['m_mix_norm'], 'm_rel_bias': out['m_rel_bias'], 'm_attn_w_in': out['m_attn_w_in'], 'm_attn_q_norm': out['m_attn_q_norm'], 'm_attn_k_norm': out['m_attn_k_norm'], 'm_attn_w_out': out['m_attn_w_out'], 'm_rw_mix': out['m_rw_mix'], 'm_rw_w0': out['m_rw_w0'], 'm_rw_w1': out['m_rw_w1'], 'm_rw_w2': out['m_rw_w2'], 'm_rw_a0': out['m_rw_a0'], 'm_rw_a1': out['m_rw_a1'], 'm_rw_a2': out['m_rw_a2'], 'm_rw_g1': out['m_rw_g1'], 'm_rw_g2': out['m_rw_g2'], 'm_rw_kk': out['m_rw_kk'], 'm_rw_ka': out['m_rw_ka'], 'm_rw_rk': out['m_rw_rk'], 'm_rw_wr': out['m_rw_wr'], 'm_rw_wk': out['m_rw_wk'], 'm_rw_wv': out['m_rw_wv'], 'm_rw_wo': out['m_rw_wo'], 'm_rw_lnx_g': out['m_rw_lnx_g'], 'm_rw_lnx_b': out['m_rw_lnx_b'], 'v_ffn_norm': out['v_ffn_norm'], 'v_ffn_w_gate': out['v_ffn_w_gate'], 'v_ffn_w_up': out['v_ffn_w_up'], 'v_ffn_w_down': out['v_ffn_w_down'], 'v_mix_norm': out['v_mix_norm'], 'v_rel_bias': out['v_rel_bias'], 'v_attn_w_in': out['v_attn_w_in'], 'v_attn_q_norm': out['v_attn_q_norm'], 'v_attn_k_norm': out['v_attn_k_norm'], 'v_attn_w_out': out['v_attn_w_out'], 'v_rw_mix': out['v_rw_mix'], 'v_rw_w0': out['v_rw_w0'], 'v_rw_w1': out['v_rw_w1'], 'v_rw_w2': out['v_rw_w2'], 'v_rw_a0': out['v_rw_a0'], 'v_rw_a1': out['v_rw_a1'], 'v_rw_a2': out['v_rw_a2'], 'v_rw_g1': out['v_rw_g1'], 'v_rw_g2': out['v_rw_g2'], 'v_rw_kk': out['v_rw_kk'], 'v_rw_ka': out['v_rw_ka'], 'v_rw_rk': out['v_rw_rk'], 'v_rw_wr': out['v_rw_wr'], 'v_rw_wk': out['v_rw_wk'], 'v_rw_wv': out['v_rw_wv'], 'v_rw_wo': out['v_rw_wo'], 'v_rw_lnx_g': out['v_rw_lnx_g'], 'v_rw_lnx_b': out['v_rw_lnx_b']}


def _loss(weights, diff, rest, loss_target):
    with _jax.named_scope("forward"):
        args = {**rest, TWIN_DIFF_INPUT: diff, **{k: w.astype(_WEIGHT_DTYPES[k]) for k, w in weights.items()}}
        y = _forward(args)
    with _jax.named_scope("loss_head"):
        err = _jnp.square(y.astype(_jnp.float32) - loss_target)
        return 0.5 * _jnp.sum(_jnp.mean(err, axis=-1)) if err.ndim else 0.5 * err


def _adamw(w, g, m, v):
    m = ADAM_B1 * m + (1.0 - ADAM_B1) * g
    v = ADAM_B2 * v + (1.0 - ADAM_B2) * _jnp.square(g)
    m_hat = m / (1.0 - ADAM_B1 ** ADAM_STEP)
    v_hat = v / (1.0 - ADAM_B2 ** ADAM_STEP)
    delta = -ADAM_LR * (m_hat / (_jnp.sqrt(v_hat) + ADAM_EPS) + ADAM_WD * w)
    return delta, m, v


def reference(x, ffn_norm, ffn_w_gate, ffn_w_up, ffn_w_down, mix_norm, rel_bias, attn_w_in, attn_q_norm, attn_k_norm, attn_w_out, rw_mix, rw_w0, rw_w1, rw_w2, rw_a0, rw_a1, rw_a2, rw_g1, rw_g2, rw_kk, rw_ka, rw_rk, rw_wr, rw_wk, rw_wv, rw_wo, rw_lnx_g, rw_lnx_b, loss_target, m_ffn_norm, m_ffn_w_gate, m_ffn_w_up, m_ffn_w_down, m_mix_norm, m_rel_bias, m_attn_w_in, m_attn_q_norm, m_attn_k_norm, m_attn_w_out, m_rw_mix, m_rw_w0, m_rw_w1, m_rw_w2, m_rw_a0, m_rw_a1, m_rw_a2, m_rw_g1, m_rw_g2, m_rw_kk, m_rw_ka, m_rw_rk, m_rw_wr, m_rw_wk, m_rw_wv, m_rw_wo, m_rw_lnx_g, m_rw_lnx_b, v_ffn_norm, v_ffn_w_gate, v_ffn_w_up, v_ffn_w_down, v_mix_norm, v_rel_bias, v_attn_w_in, v_attn_q_norm, v_attn_k_norm, v_attn_w_out, v_rw_mix, v_rw_w0, v_rw_w1, v_rw_w2, v_rw_a0, v_rw_a1, v_rw_a2, v_rw_g1, v_rw_g2, v_rw_kk, v_rw_ka, v_rw_rk, v_rw_wr, v_rw_wk, v_rw_wv, v_rw_wo, v_rw_lnx_g, v_rw_lnx_b):
    given = dict(x=x, ffn_norm=ffn_norm, ffn_w_gate=ffn_w_gate, ffn_w_up=ffn_w_up, ffn_w_down=ffn_w_down, mix_norm=mix_norm, rel_bias=rel_bias, attn_w_in=attn_w_in, attn_q_norm=attn_q_norm, attn_k_norm=attn_k_norm, attn_w_out=attn_w_out, rw_mix=rw_mix, rw_w0=rw_w0, rw_w1=rw_w1, rw_w2=rw_w2, rw_a0=rw_a0, rw_a1=rw_a1, rw_a2=rw_a2, rw_g1=rw_g1, rw_g2=rw_g2, rw_kk=rw_kk, rw_ka=rw_ka, rw_rk=rw_rk, rw_wr=rw_wr, rw_wk=rw_wk, rw_wv=rw_wv, rw_wo=rw_wo, rw_lnx_g=rw_lnx_g, rw_lnx_b=rw_lnx_b, loss_target=loss_target, m_ffn_norm=m_ffn_norm, m_ffn_w_gate=m_ffn_w_gate, m_ffn_w_up=m_ffn_w_up, m_ffn_w_down=m_ffn_w_down, m_mix_norm=m_mix_norm, m_rel_bias=m_rel_bias, m_attn_w_in=m_attn_w_in, m_attn_q_norm=m_attn_q_norm, m_attn_k_norm=m_attn_k_norm, m_attn_w_out=m_attn_w_out, m_rw_mix=m_rw_mix, m_rw_w0=m_rw_w0, m_rw_w1=m_rw_w1, m_rw_w2=m_rw_w2, m_rw_a0=m_rw_a0, m_rw_a1=m_rw_a1, m_rw_a2=m_rw_a2, m_rw_g1=m_rw_g1, m_rw_g2=m_rw_g2, m_rw_kk=m_rw_kk, m_rw_ka=m_rw_ka, m_rw_rk=m_rw_rk, m_rw_wr=m_rw_wr, m_rw_wk=m_rw_wk, m_rw_wv=m_rw_wv, m_rw_wo=m_rw_wo, m_rw_lnx_g=m_rw_lnx_g, m_rw_lnx_b=m_rw_lnx_b, v_ffn_norm=v_ffn_norm, v_ffn_w_gate=v_ffn_w_gate, v_ffn_w_up=v_ffn_w_up, v_ffn_w_down=v_ffn_w_down, v_mix_norm=v_mix_norm, v_rel_bias=v_rel_bias, v_attn_w_in=v_attn_w_in, v_attn_q_norm=v_attn_q_norm, v_attn_k_norm=v_attn_k_norm, v_attn_w_out=v_attn_w_out, v_rw_mix=v_rw_mix, v_rw_w0=v_rw_w0, v_rw_w1=v_rw_w1, v_rw_w2=v_rw_w2, v_rw_a0=v_rw_a0, v_rw_a1=v_rw_a1, v_rw_a2=v_rw_a2, v_rw_g1=v_rw_g1, v_rw_g2=v_rw_g2, v_rw_kk=v_rw_kk, v_rw_ka=v_rw_ka, v_rw_rk=v_rw_rk, v_rw_wr=v_rw_wr, v_rw_wk=v_rw_wk, v_rw_wv=v_rw_wv, v_rw_wo=v_rw_wo, v_rw_lnx_g=v_rw_lnx_g, v_rw_lnx_b=v_rw_lnx_b)
    weights = {n: given[n] for n in TWIN_WEIGHTS}
    shared = {n: given[n] for n in SHARED_INPUTS}
    per_example = {n: given[n] for n in ['x']}
    grad_fn = _jax.value_and_grad(_loss, argnums=(0, 1))

    def one_microbatch(ex, loss_target):
        ex = dict(ex)
        diff = ex.pop(TWIN_DIFF_INPUT)
        return grad_fn(weights, diff, {**shared, **ex}, loss_target)

    if N_MICROBATCH == 1:
        loss, (grad_w, grad_x) = one_microbatch(per_example, given["loss_target"])
    else:
        def body(carry, xs):
            loss_sum, grad_sum = carry
            l_k, (gw_k, gx_k) = one_microbatch(xs[0], xs[1])
            with _jax.named_scope("update"):
                return (loss_sum + l_k, _jax.tree.map(_jnp.add, grad_sum, gw_k)), gx_k

        init = (_jnp.zeros((), _jnp.float32), _jax.tree.map(_jnp.zeros_like, weights))
        (loss, grad_w), grad_x = _jax.lax.scan(body, init, (per_example, given["loss_target"]))
    with _jax.named_scope("update"):
        delta_w, new_m, new_v = {}, {}, {}
        for n in TWIN_WEIGHTS:
            delta_w[n], new_m[n], new_v[n] = _adamw(weights[n], grad_w[n], given["m_" + n], given["v_" + n])
    return (loss, grad_x, *[grad_w[n] for n in TWIN_WEIGHTS], *[delta_w[n] for n in TWIN_WEIGHTS],
            *[new_m[n] for n in TWIN_WEIGHTS], *[new_v[n] for n in TWIN_WEIGHTS])
```

```python
import functools
import math

import jax
import jax.numpy as jnp
from jax import lax
from jax.experimental import pallas as pl
from jax.experimental.pallas import tpu as pltpu

F32, BF16 = jnp.float32, jnp.bfloat16
HI = lax.Precision.HIGHEST
MESH = pl.DeviceIdType.MESH

D = 1024
HEAD = 64
N_CHIPS = 4
FF_SHARD = 704
SB_W = 256
DL_HEADS = 12
DL_PAIRS = 6
DIL = (1, 4, 16)
QBLK = 128
N_BUCKETS = 32
MAX_DISTANCE = 2048
RW_H = 16
RW_CHUNK = 64
NORM_EPS = 1e-6
GN_EPS = 64e-5
NEG_INF = -1e30
VMEM_LIMIT = 56 * 1024 * 1024

ADAM_LR, ADAM_B1, ADAM_B2, ADAM_EPS, ADAM_WD, ADAM_STEP = 0.001, 0.9, 0.999, 1e-08, 0.01, 10


def _cp(sem):
    return pltpu.CompilerParams(dimension_semantics=sem, vmem_limit_bytes=VMEM_LIMIT)


def _dg(a, b, dims, prec=None):
    return lax.dot_general(a, b, (dims, ((), ())), precision=prec, preferred_element_type=F32)


def _bdot(a, b, dims):
    return _dg(a.astype(BF16), b.astype(BF16), dims)


@jax.custom_vjp
def mm(a, b):
    return _bdot(a, b, ((1,), (0,)))


def _mm_fwd(a, b):
    return _bdot(a, b, ((1,), (0,))), (a, b)


def _mm_bwd(res, g):
    a, b = res
    return _bdot(g, b, ((1,), (1,))), _bdot(a, g, ((0,), (0,)))


mm.defvjp(_mm_fwd, _mm_bwd)


def rms(x, g):
    return x * lax.rsqrt(jnp.mean(x * x, axis=-1, keepdims=True) + NORM_EPS) * g


def group_sum(x, nh):
    w = x.shape[-1]
    e = (lax.broadcasted_iota(jnp.int32, (w, nh), 0) // HEAD == lax.broadcasted_iota(jnp.int32, (w, nh), 1)).astype(F32)
    s = _dg(x, e, ((1,), (0,)), HI)
    return _dg(s, e, ((1,), (1,)), HI)


def softplus(u):
    return jnp.maximum(u, 0.0) + jnp.log1p(jnp.exp(-jnp.abs(u)))


def to_heads(t, nh=RW_H):
    return jnp.stack([t[:, HEAD * h:HEAD * (h + 1)] for h in range(nh)])


def from_heads(t):
    return jnp.concatenate([t[h] for h in range(t.shape[0])], axis=-1)


def _tile_spec(shape, tm):
    if len(shape) == 2:
        return pl.BlockSpec((tm, shape[1]), lambda t: (t, 0))
    return pl.BlockSpec((shape[0], tm, shape[2]), lambda t: (0, t, 0))


def _full_spec(shape):
    nd = len(shape)
    return pl.BlockSpec(tuple(shape), lambda t: (0,) * nd)


def _rows(a):
    return a.shape[0] if a.ndim == 2 else a.shape[1]


def tile_fwd(f, name, tiles, weights, outs, tm):
    nt, nw = len(tiles), len(weights)

    def body(*refs):
        tv = [r[...] for r in refs[:nt]]
        wv = [r[...].astype(F32) for r in refs[nt:nt + nw]]
        res = f(*tv, *wv)
        if not isinstance(res, (tuple, list)):
            res = (res,)
        for o, v in zip(refs[nt + nw:], res):
            o[...] = v.astype(o.dtype)

    return pl.pallas_call(
        body, name=name, grid=(_rows(tiles[0]) // tm,),
        in_specs=[_tile_spec(a.shape, tm) for a in tiles] + [_full_spec(w.shape) for w in weights],
        out_specs=[_tile_spec(o.shape, tm) for o in outs],
        out_shape=list(outs),
        compiler_params=_cp(("parallel",)),
    )(*tiles, *weights)


def tile_bwd(f, name, tiles, weights, cts, tm, dt, dw, acc=None):
    acc = acc or {}
    nt, nw, nc = len(tiles), len(weights), len(cts)
    acc_idx = sorted(acc)
    na = len(acc_idx)
    dti = [i for i in range(nt) if dt[i]]
    dwi = [i for i in range(nw) if dw[i]]

    def body(*refs):
        tv = [r[...] for r in refs[:nt]]
        wv = [r[...].astype(F32) for r in refs[nt:nt + nw]]
        cv = [r[...] for r in refs[nt + nw:nt + nw + nc]]
        av = {i: r[...] for i, r in zip(acc_idx, refs[nt + nw + nc:nt + nw + nc + na])}
        orefs = refs[nt + nw + nc + na:]

        def g(*diff):
            t2, w2 = list(tv), list(wv)
            for i, v in zip(dti, diff[:len(dti)]):
                t2[i] = v
            for i, v in zip(dwi, diff[len(dti):]):
                w2[i] = v
            res = f(*t2, *w2)
            return tuple(res) if isinstance(res, (tuple, list)) else (res,)

        _, vjp = jax.vjp(g, *[tv[i] for i in dti], *[wv[i] for i in dwi])
        grads = vjp(tuple(cv))
        for k, i in enumerate(dti):
            gt = grads[k]
            if i in av:
                gt = gt + av[i]
            orefs[k][...] = gt
        first = pl.program_id(0) == 0
        for k, i in enumerate(dwi):
            o = orefs[len(dti) + k]
            gw = grads[len(dti) + k]

            @pl.when(first)
            def _(o=o, gw=gw):
                o[...] = gw

            @pl.when(jnp.logical_not(first))
            def _(o=o, gw=gw):
                o[...] += gw

    out_shape = [jax.ShapeDtypeStruct(tiles[i].shape, F32) for i in dti] + \
                [jax.ShapeDtypeStruct(weights[i].shape, F32) for i in dwi]
    res = pl.pallas_call(
        body, name=name, grid=(_rows(tiles[0]) // tm,),
        in_specs=[_tile_spec(a.shape, tm) for a in tiles] + [_full_spec(w.shape) for w in weights] +
                 [_tile_spec(c.shape, tm) for c in cts] + [_tile_spec(tiles[i].shape, tm) for i in acc_idx],
        out_specs=[_tile_spec(tiles[i].shape, tm) for i in dti] + [_full_spec(weights[i].shape) for i in dwi],
        out_shape=out_shape,
        compiler_params=_cp(("arbitrary",)),
    )(*tiles, *weights, *cts, *[acc[i] for i in acc_idx])
    return list(res[:len(dti)]), list(res[len(dti):])


def _ffn_wspec(l, j, rows, cols, cfirst):
    if cfirst:
        return pl.BlockSpec((1, 1, 1, rows, cols), lambda c, t: (c, l, j, 0, 0))
    return pl.BlockSpec((1, 1, 1, rows, cols), lambda t, c: (c, l, j, 0, 0))


def ffn_fwd(x, g, wg, wu, wd, l, j, tm=512):
    S = x.shape[0]

    def body(x_ref, g_ref, wg_ref, wu_ref, wd_ref, o_ref, h_ref, acc_ref):
        c = pl.program_id(1)

        @pl.when(c == 0)
        def _():
            h_ref[...] = rms(x_ref[...], g_ref[...]).astype(BF16)
            acc_ref[...] = jnp.zeros_like(acc_ref)

        h = h_ref[...]
        a = _bdot(h, wg_ref[0, 0, 0], ((1,), (0,)))
        b = _bdot(h, wu_ref[0, 0, 0], ((1,), (0,)))
        y = a * jax.nn.sigmoid(a) * b
        acc_ref[...] += _bdot(y, wd_ref[0, 0, 0], ((1,), (0,)))

        @pl.when(c == N_CHIPS - 1)
        def _():
            o_ref[...] = x_ref[...] + 0.5 * acc_ref[...]

    return pl.pallas_call(
        body, name=f"ffn_fwd_{l}{j}", grid=(S // tm, N_CHIPS),
        in_specs=[pl.BlockSpec((tm, D), lambda t, c: (t, 0)), pl.BlockSpec((1, D), lambda t, c: (0, 0)),
                  _ffn_wspec(l, j, D, FF_SHARD, False), _ffn_wspec(l, j, D, FF_SHARD, False),
                  _ffn_wspec(l, j, FF_SHARD, D, False)],
        out_specs=pl.BlockSpec((tm, D), lambda t, c: (t, 0)),
        out_shape=jax.ShapeDtypeStruct((S, D), F32),
        scratch_shapes=[pltpu.VMEM((tm, D), BF16), pltpu.VMEM((tm, D), F32)],
        compiler_params=_cp(("parallel", "arbitrary")),
    )(x, g, wg, wu, wd)


def ffn_bwd(x, g, wg, wu, wd, dout, l, j, tm=256):
    S = x.shape[0]

    def body(x_ref, g_ref, wg_ref, wu_ref, wd_ref, do_ref, dh_ref, dwg_ref, dwu_ref, dwd_ref):
        t = pl.program_id(1)
        h = rms(x_ref[...], g_ref[...]).astype(BF16)
        wgv, wuv, wdv = wg_ref[0, 0, 0], wu_ref[0, 0, 0], wd_ref[0, 0, 0]
        a = _bdot(h, wgv, ((1,), (0,)))
        b = _bdot(h, wuv, ((1,), (0,)))
        sig = jax.nn.sigmoid(a)
        s = a * sig
        dyd = 0.5 * do_ref[...]
        dy = _bdot(dyd, wdv, ((1,), (1,)))
        dwd = _bdot(s * b, dyd, ((0,), (0,)))
        db = dy * s
        da = dy * b * (sig * (1.0 + a * (1.0 - sig)))
        dwg = _bdot(h, da, ((0,), (0,)))
        dwu = _bdot(h, db, ((0,), (0,)))
        dh_ref[0] = _bdot(da, wgv, ((1,), (1,))) + _bdot(db, wuv, ((1,), (1,)))

        @pl.when(t == 0)
        def _():
            dwg_ref[0] = dwg
            dwu_ref[0] = dwu
            dwd_ref[0] = dwd

        @pl.when(t != 0)
        def _():
            dwg_ref[0] += dwg
            dwu_ref[0] += dwu
            dwd_ref[0] += dwd

    return pl.pallas_call(
        body, name=f"ffn_bwd_{l}{j}", grid=(N_CHIPS, S // tm),
        in_specs=[pl.BlockSpec((tm, D), lambda c, t: (t, 0)), pl.BlockSpec((1, D), lambda c, t: (0, 0)),
                  _ffn_wspec(l, j, D, FF_SHARD, True), _ffn_wspec(l, j, D, FF_SHARD, True),
                  _ffn_wspec(l, j, FF_SHARD, D, True), pl.BlockSpec((tm, D), lambda c, t: (t, 0))],
        out_specs=[pl.BlockSpec((1, tm, D), lambda c, t: (c, t, 0)),
                   pl.BlockSpec((1, D, FF_SHARD), lambda c, t: (c, 0, 0)),
                   pl.BlockSpec((1, D, FF_SHARD), lambda c, t: (c, 0, 0)),
                   pl.BlockSpec((1, FF_SHARD, D), lambda c, t: (c, 0, 0))],
        out_shape=[jax.ShapeDtypeStruct((N_CHIPS, S, D), F32), jax.ShapeDtypeStruct((N_CHIPS, D, FF_SHARD), F32),
                   jax.ShapeDtypeStruct((N_CHIPS, D, FF_SHARD), F32), jax.ShapeDtypeStruct((N_CHIPS, FF_SHARD, D), F32)],
        compiler_params=_cp(("parallel", "arbitrary")),
    )(x, g, wg, wu, wd, dout)


def norm_bwd(name, x, g, dh_parts, dres, tm=256):
    S = x.shape[0]
    P = dh_parts.shape[0]

    def body(x_ref, g_ref, dh_ref, dr_ref, dx_ref, dg_ref):
        dh = dh_ref[0]
        for p in range(1, P):
            dh = dh + dh_ref[p]
        _, vjp = jax.vjp(rms, x_ref[...], g_ref[...])
        dx, dg = vjp(dh)
        dx_ref[...] = dr_ref[...] + dx

        @pl.when(pl.program_id(0) == 0)
        def _():
            dg_ref[...] = dg

        @pl.when(pl.program_id(0) != 0)
        def _():
            dg_ref[...] += dg

    return pl.pallas_call(
        body, name=name, grid=(S // tm,),
        in_specs=[pl.BlockSpec((tm, D), lambda t: (t, 0)), pl.BlockSpec((1, D), lambda t: (0, 0)),
                  pl.BlockSpec((P, tm, D), lambda t: (0, t, 0)), pl.BlockSpec((tm, D), lambda t: (t, 0))],
        out_specs=[pl.BlockSpec((tm, D), lambda t: (t, 0)), pl.BlockSpec((1, D), lambda t: (0, 0))],
        out_shape=[jax.ShapeDtypeStruct((S, D), F32), jax.ShapeDtypeStruct((1, D), F32)],
        compiler_params=_cp(("arbitrary",)),
    )(x, g, dh_parts, dres)


def f_attn_sb(x, g, w):
    pr = mm(rms(x, g), w)
    return pr[:, :SB_W], pr[:, SB_W:2 * SB_W], pr[:, 2 * SB_W:]


def _pairs(y):
    return jnp.stack([y[:, 128 * j:128 * (j + 1)] for j in range(DL_PAIRS)])


def f_attn_qk(x, g, w, nrm):
    pr = mm(rms(x, g), w)
    ms = group_sum(pr * pr, DL_HEADS) * (1.0 / HEAD)
    return _pairs(pr * lax.rsqrt(ms + NORM_EPS) * jnp.concatenate([nrm] * DL_HEADS, axis=1))


def f_attn_v(x, g, w):
    return _pairs(mm(rms(x, g), w))


def _sb_tiles(q, k, qpos, kpos):
    z = _bdot(q, k, ((1,), (1,))) * (HEAD ** -0.5)
    strict = kpos < qpos
    keep = jnp.where(strict, -softplus(z), 0.0)
    return z, strict, keep


def _tri(n, upper):
    r = lax.broadcasted_iota(jnp.int32, (n, n), 0)
    c = lax.broadcasted_iota(jnp.int32, (n, n), 1)
    return ((r > c) if upper else (r < c)).astype(F32)


def sb_fwd(q, k, v, tb=QBLK):
    S = q.shape[0]
    nh = SB_W // HEAD

    def body(q_ref, k_ref, v_ref, o_ref):
        qb = pl.program_id(0)
        qpos = qb * tb + lax.broadcasted_iota(jnp.int32, (tb, tb), 0)
        col = lax.broadcasted_iota(jnp.int32, (tb, tb), 1)
        after_mat = _tri(tb, True)
        for h in range(nh):
            sl = slice(HEAD * h, HEAD * (h + 1))
            qh = q_ref[:, sl]

            def step(i, carry):
                acc, run = carry
                kb = qb - i
                rows = pl.ds(pl.multiple_of(kb * tb, tb), tb)
                z, strict, keep = _sb_tiles(qh, k_ref[rows, sl], qpos, kb * tb + col)
                after = _dg(keep, after_mat, ((1,), (0,)), HI) + run
                w = jnp.where(strict, jnp.exp(z + keep + after), 0.0)
                acc = acc + _bdot(w, v_ref[rows, sl], ((1,), (0,)))
                return acc, run + jnp.sum(keep, axis=1, keepdims=True)

            acc, _ = lax.fori_loop(0, qb + 1, step, (jnp.zeros((tb, HEAD), F32), jnp.zeros((tb, 1), F32)))
            o_ref[:, sl] = acc

    return pl.pallas_call(
        body, name="sb_fwd", grid=(S // tb,),
        in_specs=[pl.BlockSpec((tb, SB_W), lambda i: (i, 0)), pl.BlockSpec((S, SB_W), lambda i: (0, 0)),
                  pl.BlockSpec((S, SB_W), lambda i: (0, 0))],
        out_specs=pl.BlockSpec((tb, SB_W), lambda i: (i, 0)),
        out_shape=jax.ShapeDtypeStruct((S, SB_W), F32),
        compiler_params=_cp(("parallel",)),
    )(q, k, v)


def sb_bwd(q, k, v, do, tb=QBLK):
    S = q.shape[0]
    nh = SB_W // HEAD
    scale = HEAD ** -0.5

    def body(q_ref, k_ref, v_ref, do_ref, dq_ref, dk_ref, dv_ref, z_scr, g_scr):
        qb = pl.program_id(0)

        @pl.when(qb == 0)
        def _():
            dk_ref[...] = jnp.zeros_like(dk_ref)
            dv_ref[...] = jnp.zeros_like(dv_ref)

        qpos = qb * tb + lax.broadcasted_iota(jnp.int32, (tb, tb), 0)
        col = lax.broadcasted_iota(jnp.int32, (tb, tb), 1)
        after_mat = _tri(tb, True)
        before_mat = _tri(tb, False)
        for h in range(nh):
            sl = slice(HEAD * h, HEAD * (h + 1))
            qh = q_ref[:, sl]
            doh = do_ref[:, sl]

            def right_to_left(i, run):
                kb = qb - i
                rows = pl.ds(pl.multiple_of(kb * tb, tb), tb)
                z, strict, keep = _sb_tiles(qh, k_ref[rows, sl], qpos, kb * tb + col)
                after = _dg(keep, after_mat, ((1,), (0,)), HI) + run
                w = jnp.where(strict, jnp.exp(z + keep + after), 0.0)
                gw = _bdot(doh, v_ref[rows, sl], ((1,), (1,))) * w
                z_scr[kb] = z
                g_scr[kb] = gw
                dv_ref[rows, sl] += _bdot(w, doh, ((0,), (0,)))
                return run + jnp.sum(keep, axis=1, keepdims=True)

            lax.fori_loop(0, qb + 1, right_to_left, jnp.zeros((tb, 1), F32))

            def left_to_right(kb, carry):
                dq, run = carry
                rows = pl.ds(pl.multiple_of(kb * tb, tb), tb)
                z = z_scr[kb]
                gw = g_scr[kb]
                strict = (kb * tb + col) < qpos
                dkeep = jnp.where(strict, _dg(gw, before_mat, ((1,), (0,)), HI) + run, 0.0)
                sig = jax.nn.sigmoid(z)
                dz = (gw * (1.0 - sig) - dkeep * sig) * scale
                dq = dq + _bdot(dz, k_ref[rows, sl], ((1,), (0,)))
                dk_ref[rows, sl] += _bdot(dz, qh, ((0,), (0,)))
                return dq, run + jnp.sum(gw, axis=1, keepdims=True)

            dq, _ = lax.fori_loop(0, qb + 1, left_to_right, (jnp.zeros((tb, HEAD), F32), jnp.zeros((tb, 1), F32)))
            dq_ref[:, sl] = dq

    whole = pl.BlockSpec((S, SB_W), lambda i: (0, 0))
    blk = pl.BlockSpec((tb, SB_W), lambda i: (i, 0))
    return pl.pallas_call(
        body, name="sb_bwd", grid=(S // tb,),
        in_specs=[blk, whole, whole, blk], out_specs=[blk, whole, whole],
        out_shape=[jax.ShapeDtypeStruct((S, SB_W), F32)] * 3,
        scratch_shapes=[pltpu.VMEM((S // tb, tb, tb), F32), pltpu.VMEM((S // tb, tb, tb), F32)],
        compiler_params=_cp(("arbitrary",)),
    )(q, k, v, do)


def reorder(name, x, groups, inverse):
    S = x.shape[1]
    out = x
    for gi, r in enumerate(groups):
        if r > 1:
            out = _reorder_call(f"{name}_{r}", x, out, gi, r, S // r, inverse)
    return out


def _reorder_call(name, x, prev, gi, r, L, inverse):
    S = x.shape[1]
    whole = pl.BlockSpec((None, S, 128), lambda p, c: (2 * gi + p, 0, 0))
    part = pl.BlockSpec((None, L, 128), lambda p, c: (2 * gi + p, c, 0))

    def body(x_ref, prev_ref, o_ref):
        c = pl.program_id(1)
        if inverse:
            o_ref[pl.ds(c, L, stride=r), :] = x_ref[...]
        else:
            o_ref[...] = x_ref[pl.ds(c, L, stride=r), :]

    return pl.pallas_call(
        body, name=name, grid=(2, r),
        in_specs=[part if inverse else whole, pl.BlockSpec(memory_space=pl.ANY)],
        out_specs=whole if inverse else part,
        out_shape=jax.ShapeDtypeStruct(x.shape, x.dtype),
        input_output_aliases={1: 0},
        compiler_params=_cp(("parallel", "arbitrary")),
    )(x, prev)


def _dil_blocks(S):
    return S // QBLK


def _dil_mask(n_in_stream):
    qi = lax.broadcasted_iota(jnp.int32, (QBLK, 2 * QBLK), 0)
    kj = lax.broadcasted_iota(jnp.int32, (QBLK, 2 * QBLK), 1) - QBLK
    dist = qi - kj
    return (dist >= 0) & (dist <= QBLK) & ((n_in_stream > 0) | (kj >= 0))


def _stream_pos(gi, i, S):
    nb = jnp.where(gi == 0, S // (QBLK * DIL[0]), jnp.where(gi == 1, S // (QBLK * DIL[1]), S // (QBLK * DIL[2])))
    return i % nb


def dil_fwd(q, k, v, bias):
    S = q.shape[1]
    nblk = _dil_blocks(S)

    def body(q_ref, kc_ref, kp_ref, vc_ref, vp_ref, b_ref, o_ref, l_ref):
        gi, i = pl.program_id(0), pl.program_id(1)
        mask = _dil_mask(_stream_pos(gi, i, S))
        for j in range(2):
            q2, kc, kp, vc, vp = q_ref[j], kc_ref[j], kp_ref[j], vc_ref[j], vp_ref[j]
            os_, ls_ = [], []
            for hh in range(2):
                sl = slice(HEAD * hh, HEAD * (hh + 1))
                kw = jnp.concatenate([kp[:, sl], kc[:, sl]], axis=0)
                vw = jnp.concatenate([vp[:, sl], vc[:, sl]], axis=0)
                lg = _bdot(q2[:, sl], kw, ((1,), (1,))) * (HEAD ** -0.5) + b_ref[2 * j + hh]
                lg = jnp.where(mask, lg, NEG_INF)
                m = jnp.max(lg, axis=-1, keepdims=True)
                p = jnp.exp(lg - m)
                den = jnp.sum(p, axis=-1, keepdims=True)
                os_.append(_bdot(p / den, vw, ((1,), (0,))))
                ls_.append(jnp.broadcast_to(m + jnp.log(den), (QBLK, HEAD)))
            o_ref[j] = jnp.concatenate(os_, axis=1)
            l_ref[j] = jnp.concatenate(ls_, axis=1)

    cur = pl.BlockSpec((2, QBLK, 128), lambda g, i: (g, i, 0))
    prev = pl.BlockSpec((2, QBLK, 128), lambda g, i: (g, jnp.maximum(i - 1, 0), 0))
    return pl.pallas_call(
        body, name="dil_fwd", grid=(len(DIL), nblk),
        in_specs=[cur, cur, prev, cur, prev, pl.BlockSpec((4, QBLK, 2 * QBLK), lambda g, i: (g, 0, 0))],
        out_specs=[cur, cur],
        out_shape=[jax.ShapeDtypeStruct(q.shape, F32)] * 2,
        compiler_params=_cp(("parallel", "parallel")),
    )(q, k, k, v, v, bias)


def dil_bwd(q, k, v, bias, o, lse, do, dlse):
    S = q.shape[1]
    nblk = _dil_blocks(S)

    def body(q_ref, kc_ref, kp_ref, vc_ref, vp_ref, b_ref, o_ref, l_ref, do_ref, dl_ref,
             dq_ref, dk_ref, dv_ref, ds_ref, dk_car, dv_car):
        gi, i = pl.program_id(0), pl.program_id(1)

        @pl.when(i == 0)
        def _():
            ds_ref[...] = jnp.zeros_like(ds_ref)
            dk_car[...] = jnp.zeros_like(dk_car)
            dv_car[...] = jnp.zeros_like(dv_car)

        @pl.when(i < nblk)
        def _():
            mask = _dil_mask(_stream_pos(gi, i, S))
            for j in range(2):
                q2, kc, kp, vc, vp = q_ref[j], kc_ref[j], kp_ref[j], vc_ref[j], vp_ref[j]
                o2, l2, do2, dl2 = o_ref[j], l_ref[j], do_ref[j], dl_ref[j]
                dqs, dkps, dkcs, dvps, dvcs = [], [], [], [], []
                for hh in range(2):
                    sl = slice(HEAD * hh, HEAD * (hh + 1))
                    qh, doh = q2[:, sl], do2[:, sl]
                    kw = jnp.concatenate([kp[:, sl], kc[:, sl]], axis=0)
                    vw = jnp.concatenate([vp[:, sl], vc[:, sl]], axis=0)
                    lg = _bdot(qh, kw, ((1,), (1,))) * (HEAD ** -0.5) + b_ref[2 * j + hh]
                    p = jnp.where(mask, jnp.exp(lg - l2[:, HEAD * hh:HEAD * hh + 1]), 0.0)
                    dp = _bdot(doh, vw, ((1,), (1,)))
                    delta = jnp.sum(doh * o2[:, sl], axis=-1, keepdims=True)
                    dl = jnp.sum(dl2[:, sl], axis=-1, keepdims=True)
                    ds = p * (dp - delta + dl)
                    ds_ref[2 * j + hh] += ds
                    dsq = ds * (HEAD ** -0.5)
                    dqs.append(_bdot(dsq, kw, ((1,), (0,))))
                    dkw = _bdot(dsq, qh, ((0,), (0,)))
                    dvw = _bdot(p, doh, ((0,), (0,)))
                    dkps.append(dkw[:QBLK])
                    dkcs.append(dkw[QBLK:])
                    dvps.append(dvw[:QBLK])
                    dvcs.append(dvw[QBLK:])
                dq_ref[j] = jnp.concatenate(dqs, axis=1)
                dk_ref[j] = dk_car[j] + jnp.concatenate(dkps, axis=1)
                dv_ref[j] = dv_car[j] + jnp.concatenate(dvps, axis=1)
                dk_car[j] = jnp.concatenate(dkcs, axis=1)
                dv_car[j] = jnp.concatenate(dvcs, axis=1)

        @pl.when(i == nblk)
        def _():
            dk_ref[...] = dk_car[...]
            dv_ref[...] = dv_car[...]

    cur = pl.BlockSpec((2, QBLK, 128), lambda g, i: (g, jnp.minimum(i, nblk - 1), 0))
    prev = pl.BlockSpec((2, QBLK, 128), lambda g, i: (g, jnp.clip(i - 1, 0, nblk - 1), 0))
    bspec = pl.BlockSpec((4, QBLK, 2 * QBLK), lambda g, i: (g, 0, 0))
    return pl.pallas_call(
        body, name="dil_bwd", grid=(len(DIL), nblk + 1),
        in_specs=[cur, cur, prev, cur, prev, bspec, cur, cur, cur, cur],
        out_specs=[cur, prev, prev, bspec],
        out_shape=[jax.ShapeDtypeStruct(q.shape, F32)] * 3 + [jax.ShapeDtypeStruct(bias.shape, F32)],
        scratch_shapes=[pltpu.VMEM((2, QBLK, 128), F32), pltpu.VMEM((2, QBLK, 128), F32)],
        compiler_params=_cp(("arbitrary", "arbitrary")),
    )(q, k, k, v, v, bias, o, lse, do, dlse)


def _t5_bucket(dist):
    max_exact = N_BUCKETS // 2
    d = jnp.maximum(dist, 1).astype(F32)
    large = max_exact + (jnp.log(d / max_exact) / math.log(MAX_DISTANCE / max_exact)
                         * (N_BUCKETS - max_exact)).astype(jnp.int32)
    large = jnp.minimum(large, N_BUCKETS - 1)
    return jnp.where(dist < max_exact, dist, large)


def _bucket_maps():
    qi = jnp.arange(QBLK)[:, None]
    kj = jnp.arange(2 * QBLK)[None, :] - QBLK
    dist = jnp.maximum(qi - kj, 0)
    return jnp.stack([_t5_bucket(dist * r) for r in DIL])


def bias_grad(ds, buckets):
    def body(ds_ref, bk_ref, o_ref):
        lane = lax.broadcasted_iota(jnp.int32, (1, 128), 1)
        for h in range(DL_HEADS):
            dsv = ds_ref[h]
            bk = bk_ref[h // 4]

            def step(b, row):
                return jnp.where(lane == b, jnp.sum(jnp.where(bk == b, dsv, 0.0)), row)

            o_ref[h:h + 1, :] = lax.fori_loop(0, N_BUCKETS, step, jnp.zeros((1, 128), F32))

    return pl.pallas_call(
        body, name="bias_grad", out_shape=jax.ShapeDtypeStruct((DL_HEADS, 128), F32),
        in_specs=[pl.BlockSpec(memory_space=pltpu.VMEM)] * 2, out_specs=pl.BlockSpec(memory_space=pltpu.VMEM),
    )(ds, buckets)


def f_attn_out(x, oa, o, lse, w):
    og = [jnp.concatenate([o[2 * g], o[2 * g + 1]], axis=1) for g in range(3)]
    lg = [jnp.concatenate([lse[2 * g], lse[2 * g + 1]], axis=1) for g in range(3)]
    m = jnp.maximum(jnp.maximum(lg[0], lg[1]), lg[2])
    e = [jnp.exp(l - m) for l in lg]
    den = e[0] + e[1] + e[2]
    ob = (e[0] * og[0] + e[1] * og[1] + e[2] * og[2]) / den
    return x + mm(jnp.concatenate([oa, ob], axis=1), w)


def norm_shift_fwd(x, g, tm=256):
    S = x.shape[0]

    def body(x_ref, xp_ref, g_ref, h_ref, hs_ref):
        h = rms(x_ref[...], g_ref[...])
        hp = rms(xp_ref[7:8, :], g_ref[...])
        hp = jnp.where(pl.program_id(0) == 0, 0.0, hp)
        row = lax.broadcasted_iota(jnp.int32, (tm, D), 0)
        h_ref[...] = h
        hs_ref[...] = jnp.where(row == 0, hp, pltpu.roll(h, 1, 0))

    return pl.pallas_call(
        body, name="rw_norm_shift", grid=(S // tm,),
        in_specs=[pl.BlockSpec((tm, D), lambda t: (t, 0)),
                  pl.BlockSpec((8, D), lambda t: (jnp.maximum(t * (tm // 8) - 1, 0), 0)),
                  pl.BlockSpec((1, D), lambda t: (0, 0))],
        out_specs=[pl.BlockSpec((tm, D), lambda t: (t, 0))] * 2,
        out_shape=[jax.ShapeDtypeStruct((S, D), F32)] * 2,
        compiler_params=_cp(("parallel",)),
    )(x, x, g)


def norm_shift_bwd(x, g, dh, dhs, dres, tm=256):
    S = x.shape[0]
    nt = S // tm

    def body(x_ref, g_ref, dh_ref, dhs_ref, dhn_ref, dr_ref, dx_ref, dg_ref):
        t = pl.program_id(0)
        nxt = jnp.where(t == nt - 1, 0.0, dhn_ref[0:1, :])
        row = lax.broadcasted_iota(jnp.int32, (tm, D), 0)
        tot = dh_ref[...] + jnp.where(row == tm - 1, nxt, pltpu.roll(dhs_ref[...], tm - 1, 0))
        _, vjp = jax.vjp(rms, x_ref[...], g_ref[...])
        dx, dg = vjp(tot)
        dx_ref[...] = dr_ref[...] + dx

        @pl.when(t == 0)
        def _():
            dg_ref[...] = dg

        @pl.when(t != 0)
        def _():
            dg_ref[...] += dg

    tile = pl.BlockSpec((tm, D), lambda t: (t, 0))
    return pl.pallas_call(
        body, name="rw_norm_shift_bwd", grid=(nt,),
        in_specs=[tile, pl.BlockSpec((1, D), lambda t: (0, 0)), tile, tile,
                  pl.BlockSpec((8, D), lambda t: (jnp.minimum((t + 1) * (tm // 8), S // 8 - 1), 0)), tile],
        out_specs=[tile, pl.BlockSpec((1, D), lambda t: (0, 0))],
        out_shape=[jax.ShapeDtypeStruct((S, D), F32), jax.ShapeDtypeStruct((1, D), F32)],
        compiler_params=_cp(("arbitrary",)),
    )(x, g, dh, dhs, dhs, dres)


def f_rw_proj(h, hs, mix, w):
    return mm(h + (hs - h) * mix, w)


def f_rw_mid(h, hs, r, k, v, mix3, w0, a0, kkw, kaw, w1, w2, a1, a2, g1, g2):
    xx = hs - h
    xw, xa, xg = h + xx * mix3[0:1], h + xx * mix3[1:2], h + xx * mix3[2:3]
    w_log = -softplus(-(w0 + mm(jnp.tanh(mm(xw, w1)), w2))) - 0.5
    lw = -jnp.exp(w_log)
    ag = jax.nn.sigmoid(a0 + mm(mm(xa, a1), a2))
    gate = mm(jax.nn.sigmoid(mm(xg, g1)), g2)
    kk = k * kkw
    kk = kk / jnp.maximum(jnp.sqrt(group_sum(kk * kk, RW_H)), 1e-12)
    kmod = k * (1.0 + (ag - 1.0) * kaw)
    return (to_heads(r), to_heads(lw), to_heads(kmod), to_heads(v), to_heads(-kk), to_heads(kk * ag), gate)


def f_rw_post(yh, rh, kh, vh, gate, x, lng, lnb, rk, wo):
    mu = jnp.mean(yh, axis=-1, keepdims=True)
    var = jnp.mean(jnp.square(yh - mu), axis=-1, keepdims=True)
    yn = (yh - mu) * lax.rsqrt(var + GN_EPS)
    bonus = jnp.sum(rh * kh * rk, axis=-1, keepdims=True) * vh
    y = from_heads(yn) * lng + lnb + from_heads(bonus)
    return x + mm(y * gate, wo)


def _bmm(x, y, cx, cy):
    return lax.dot_general(x, y, (((cx,), (cy,)), ((0,), (0,))), precision=HI, preferred_element_type=F32)


def rwkv_chunk(S0, r, lw, k, v, a, b):
    H, C, _ = r.shape
    ii = lax.broadcasted_iota(jnp.int32, (C, C), 0)
    jj = lax.broadcasted_iota(jnp.int32, (C, C), 1)
    incl, strict = jj <= ii, jj < ii
    g = _bmm(jnp.broadcast_to(incl.astype(F32), (H, C, C)), lw, 2, 1)
    at, rt = a * jnp.exp(g - lw), r * jnp.exp(g)
    ig = jnp.exp(-g)
    bt, kt = b * ig, k * ig
    a_ab = jnp.where(strict, _bmm(at, bt, 2, 2), 0.0)
    a_ak = jnp.where(strict, _bmm(at, kt, 2, 2), 0.0)
    b_rb = jnp.where(incl, _bmm(rt, bt, 2, 2), 0.0)
    b_rk = jnp.where(incl, _bmm(rt, kt, 2, 2), 0.0)
    u = _bmm(at, S0, 2, 2) + _bmm(a_ak, v, 2, 1)
    nmat, n = a_ab, 1
    while n < C:
        u = u + _bmm(nmat, u, 2, 1)
        n *= 2
        if n < C:
            nmat = _bmm(nmat, nmat, 2, 1)
    y = _bmm(rt, S0, 2, 2) + _bmm(b_rb, u, 2, 1) + _bmm(b_rk, v, 2, 1)
    g_end = g[:, C - 1:C, :]
    dec = jnp.exp(g_end - g)
    s_new = S0 * jnp.exp(g_end) + _bmm(u, b * dec, 1, 1) + _bmm(v, k * dec, 1, 1)
    return y, s_new


def rwkv_fwd(r, lw, k, v, a, b):
    H, S, _ = r.shape
    C = RW_CHUNK

    def body(r_ref, lw_ref, k_ref, v_ref, a_ref, b_ref, y_ref, s_ref, s_scr):
        @pl.when(pl.program_id(0) == 0)
        def _():
            s_scr[...] = jnp.zeros_like(s_scr)

        s0 = s_scr[...]
        s_ref[0] = s0
        y, s1 = rwkv_chunk(s0, r_ref[...], lw_ref[...], k_ref[...], v_ref[...], a_ref[...], b_ref[...])
        y_ref[...] = y
        s_scr[...] = s1

    bs = pl.BlockSpec((H, C, HEAD), lambda c: (0, c, 0))
    return pl.pallas_call(
        body, name="rwkv_fwd", grid=(S // C,), in_specs=[bs] * 6,
        out_specs=[bs, pl.BlockSpec((1, H, HEAD, HEAD), lambda c: (c, 0, 0, 0))],
        out_shape=[jax.ShapeDtypeStruct((H, S, HEAD), F32), jax.ShapeDtypeStruct((S // C, H, HEAD, HEAD), F32)],
        scratch_shapes=[pltpu.VMEM((H, HEAD, HEAD), F32)],
        compiler_params=_cp(("arbitrary",)),
    )(r, lw, k, v, a, b)


def rwkv_bwd(r, lw, k, v, a, b, states, dy):
    H, S, _ = r.shape
    C = RW_CHUNK
    nc = S // C

    def body(r_ref, lw_ref, k_ref, v_ref, a_ref, b_ref, s_ref, dy_ref, dr, dlw, dk, dv, da, db, ds_scr):
        @pl.when(pl.program_id(0) == 0)
        def _():
            ds_scr[...] = jnp.zeros_like(ds_scr)

        _, vjp = jax.vjp(rwkv_chunk, s_ref[0], r_ref[...], lw_ref[...], k_ref[...], v_ref[...], a_ref[...], b_ref[...])
        grads = vjp((dy_ref[...], ds_scr[...]))
        ds_scr[...] = grads[0]
        for o, gv in zip((dr, dlw, dk, dv, da, db), grads[1:]):
            o[...] = gv

    bs = pl.BlockSpec((H, C, HEAD), lambda c: (0, nc - 1 - c, 0))
    return pl.pallas_call(
        body, name="rwkv_bwd", grid=(nc,),
        in_specs=[bs] * 6 + [pl.BlockSpec((1, H, HEAD, HEAD), lambda c: (nc - 1 - c, 0, 0, 0)), bs],
        out_specs=[bs] * 6, out_shape=[jax.ShapeDtypeStruct((H, S, HEAD), F32)] * 6,
        scratch_shapes=[pltpu.VMEM((H, HEAD, HEAD), F32)],
        compiler_params=_cp(("arbitrary",)),
    )(r, lw, k, v, a, b, states, dy)


def loss_head(y, target, tm=512):
    S = y.shape[0]

    def body(y_ref, t_ref, dy_ref, l_ref):
        e = y_ref[...] - t_ref[...]
        dy_ref[...] = e * (1.0 / D)
        part = jnp.broadcast_to(0.5 * jnp.sum(jnp.mean(e * e, axis=-1, keepdims=True)), (1, 128))

        @pl.when(pl.program_id(0) == 0)
        def _():
            l_ref[...] = part

        @pl.when(pl.program_id(0) != 0)
        def _():
            l_ref[...] += part

    tile = pl.BlockSpec((tm, D), lambda t: (t, 0))
    return pl.pallas_call(
        body, name="loss_head", grid=(S // tm,), in_specs=[tile, tile],
        out_specs=[tile, pl.BlockSpec((1, 128), lambda t: (0, 0))],
        out_shape=[jax.ShapeDtypeStruct((S, D), F32), jax.ShapeDtypeStruct((1, 128), F32)],
        compiler_params=_cp(("arbitrary",)),
    )(y, target)


def _row_tile(rows, cols, budget=1 << 19):
    best = None
    for tr in range(8, rows + 1, 8):
        if rows % tr == 0 and tr * cols <= budget:
            best = tr
    return best or rows


def _adam(w, g, m, v):
    m = ADAM_B1 * m + (1.0 - ADAM_B1) * g
    v = ADAM_B2 * v + (1.0 - ADAM_B2) * jnp.square(g)
    m_hat = m / (1.0 - ADAM_B1 ** ADAM_STEP)
    v_hat = v / (1.0 - ADAM_B2 ** ADAM_STEP)
    return -ADAM_LR * (m_hat / (jnp.sqrt(v_hat) + ADAM_EPS) + ADAM_WD * w), m, v


def sum_slots(name, parts):
    n, R, C = parts.shape
    tr = _row_tile(R, C * n)

    def body(p_ref, o_ref):
        s = p_ref[0]
        for i in range(1, n):
            s = s + p_ref[i]
        o_ref[...] = s

    return pl.pallas_call(
        body, name=name, grid=(R // tr,),
        in_specs=[pl.BlockSpec((n, tr, C), lambda t: (0, t, 0))], out_specs=pl.BlockSpec((tr, C), lambda t: (t, 0)),
        out_shape=jax.ShapeDtypeStruct((R, C), F32), compiler_params=_cp(("parallel",)),
    )(parts)


def adam_step(name, ga, gb, w, m, v):
    R, C = w.shape
    tr = _row_tile(R, C, 1 << 17)
    ins = [ga] + ([gb] if gb is not None else []) + [w, m, v]

    def body(*refs):
        g = refs[0][...]
        if gb is not None:
            g = g + refs[1][...]
        w_ref, m_ref, v_ref, g_out, d_out, m_out, v_out = refs[len(ins) - 3:]
        d, m2, v2 = _adam(w_ref[...], g, m_ref[...], v_ref[...])
        g_out[...] = g
        d_out[...] = d
        m_out[...] = m2
        v_out[...] = v2

    tile = pl.BlockSpec((tr, C), lambda t: (t, 0))
    return pl.pallas_call(
        body, name=name, grid=(R // tr,), in_specs=[tile] * len(ins), out_specs=[tile] * 4,
        out_shape=[jax.ShapeDtypeStruct((R, C), F32)] * 4, compiler_params=_cp(("parallel",)),
    )(*ins)


def _place():
    return lax.axis_index("x"), lax.axis_index("y"), lax.axis_index("c")


def _flip(me, mask):
    return tuple(1 - v if mk else v for v, mk in zip(me, mask))


CHIP_MASKS = ((1, 0, 0), (0, 1, 0), (1, 1, 0))
ALL_MASKS = tuple((a, b, c) for a in (0, 1) for b in (0, 1) for c in (0, 1) if (a, b, c) != (0, 0, 0))


def _chip(dev):
    return 2 * dev[0] + dev[1]


def _devno(dev):
    return 4 * dev[0] + 2 * dev[1] + dev[2]


def exchange(name, arrays, out_shapes, masks, src_of, dst_of, local_of):
    n, npeer = len(arrays), len(masks)

    def body(*refs):
        ins, outs = refs[:n], refs[n:2 * n]
        send_sems, recv_sems, local_sems = refs[2 * n:]
        me = _place()
        peers = [_flip(me, mk) for mk in masks]
        locals_ = []
        for i in range(n):
            lc = local_of(ins[i], outs[i], me)
            if lc is not None:
                cp = pltpu.make_async_copy(lc[0], lc[1], local_sems.at[i])
                cp.start()
                locals_.append(cp)
        sends = []
        for i in range(n):
            for j, peer in enumerate(peers):
                cp = pltpu.make_async_remote_copy(
                    src_ref=src_of(ins[i], me, peer), dst_ref=dst_of(outs[i], me, j),
                    send_sem=send_sems.at[i * npeer + j], recv_sem=recv_sems.at[i * npeer + j],
                    device_id=peer, device_id_type=MESH)
                cp.start()
                sends.append(cp)
        for i in range(n):
            for j, peer in enumerate(peers):
                land = dst_of(outs[i], peer, j)
                pltpu.make_async_remote_copy(
                    src_ref=land, dst_ref=land, send_sem=send_sems.at[i * npeer + j],
                    recv_sem=recv_sems.at[i * npeer + j], device_id=peer, device_id_type=MESH).wait_recv()
        for cp in sends:
            cp.wait_send()
        for cp in locals_:
            cp.wait()

    hbm = pl.BlockSpec(memory_space=pl.ANY)
    return pl.pallas_call(
        body, name=name, in_specs=[hbm] * n, out_specs=[hbm] * n, out_shape=list(out_shapes),
        scratch_shapes=[pltpu.SemaphoreType.DMA((n * npeer,)), pltpu.SemaphoreType.DMA((n * npeer,)),
                        pltpu.SemaphoreType.DMA((n,))],
    )(*arrays)


def gather_chips(arrays):
    outs = [jax.ShapeDtypeStruct((N_CHIPS,) + a.shape, a.dtype) for a in arrays]
    return exchange("gather_weights", arrays, outs, CHIP_MASKS,
                    src_of=lambda r, me, peer: r,
                    dst_of=lambda o, sender, j: o.at[_chip(sender)],
                    local_of=lambda r, o, me: (r, o.at[_chip(me)]))


def scatter_chips(arrays):
    outs = [jax.ShapeDtypeStruct(a.shape, a.dtype) for a in arrays]
    return exchange("scatter_grads", arrays, outs, CHIP_MASKS,
                    src_of=lambda r, me, peer: r.at[_chip(peer)],
                    dst_of=lambda o, sender, j: o.at[j],
                    local_of=lambda r, o, me: (r.at[_chip(me)], o.at[3]))


def swap_cores(arrays):
    outs = [jax.ShapeDtypeStruct(a.shape, a.dtype) for a in arrays]
    return exchange("swap_cores", arrays, outs, ((0, 0, 1),),
                    src_of=lambda r, me, peer: r, dst_of=lambda o, sender, j: o, local_of=lambda r, o, me: None)


def gather_all(arrays):
    outs = [jax.ShapeDtypeStruct((8,) + a.shape, a.dtype) for a in arrays]
    return exchange("gather_replicated", arrays, outs, ALL_MASKS,
                    src_of=lambda r, me, peer: r,
                    dst_of=lambda o, sender, j: o.at[_devno(sender)],
                    local_of=lambda r, o, me: (r, o.at[_devno(me)]))


def _unshard_cols(g):
    return jnp.transpose(g, (1, 0, 2)).reshape(g.shape[1], -1)


def _shard_cols(a):
    return jnp.transpose(a.reshape(a.shape[0], N_CHIPS, -1), (1, 0, 2))


def _forward_backward(x, tgt, W):
    S = x.shape[0]
    G = {}
    sd = jax.ShapeDtypeStruct

    def ffn(xin, l, j):
        return ffn_fwd(xin, W["ffn_norm"][l][j], W["ffn_w_gate"], W["ffn_w_up"], W["ffn_w_down"], l, j)

    def ffn_back(xin, dout, l, j):
        gn = W["ffn_norm"][l][j]
        dh, dwg, dwu, dwd = ffn_bwd(xin, gn, W["ffn_w_gate"], W["ffn_w_up"], W["ffn_w_down"], dout, l, j)
        dx, dg = norm_bwd(f"ffn_norm_bwd_{l}{j}", xin, gn, dh, dout)
        G[("ffn", l, j)] = (dg, dwg, dwu, dwd)
        return dx

    x0 = x
    x1 = ffn(x0, 0, 0)
    g0 = W["mix_norm"][0]
    sbq, sbk, sbv = tile_fwd(f_attn_sb, "attn_in_sb", [x1], [g0, W["attn_w_in"][0]], [sd((S, SB_W), F32)] * 3, 256)
    dl_shape = sd((DL_PAIRS, S, 128), F32)
    qn, = tile_fwd(f_attn_qk, "attn_in_q", [x1], [g0, W["attn_w_in"][1], W["attn_q_norm"]], [dl_shape], 256)
    kn, = tile_fwd(f_attn_qk, "attn_in_k", [x1], [g0, W["attn_w_in"][2], W["attn_k_norm"]], [dl_shape], 256)
    vv, = tile_fwd(f_attn_v, "attn_in_v", [x1], [g0, W["attn_w_in"][3]], [dl_shape], 256)
    oa = sb_fwd(sbq, sbk, sbv)
    qs, ks, vs = (reorder(nm, t, DIL, False) for nm, t in (("sub_q", qn), ("sub_k", kn), ("sub_v", vv)))
    o_s, lse_s = dil_fwd(qs, ks, vs, W["bias_mat"])
    o_n, lse_n = reorder("nat_o", o_s, DIL, True), reorder("nat_lse", lse_s, DIL, True)
    x2, = tile_fwd(f_attn_out, "attn_out", [x1, oa, o_n, lse_n], [W["attn_w_out"]], [sd((S, D), F32)], 256)
    x3 = ffn(x2, 0, 1)
    x4 = ffn(x3, 1, 0)
    g1 = W["mix_norm"][1]
    h, hs = norm_shift_fwd(x4, g1)
    mix = W["rw_mix"]
    r, = tile_fwd(f_rw_proj, "rw_proj_r", [h, hs], [mix[0:1], W["rw_wr"]], [sd((S, D), F32)], 256)
    k, = tile_fwd(f_rw_proj, "rw_proj_k", [h, hs], [mix[2:3], W["rw_wk"]], [sd((S, D), F32)], 256)
    v, = tile_fwd(f_rw_proj, "rw_proj_v", [h, hs], [mix[3:4], W["rw_wv"]], [sd((S, D), F32)], 256)
    mix3 = jnp.concatenate([mix[1:2], mix[4:5], mix[5:6]], axis=0)
    mid_w = [mix3, W["rw_w0"], W["rw_a0"], W["rw_kk"], W["rw_ka"], W["rw_w1"], W["rw_w2"], W["rw_a1"], W["rw_a2"],
             W["rw_g1"], W["rw_g2"]]
    hshape = sd((RW_H, S, HEAD), F32)
    mid_tiles = [h, hs, r, k, v]
    rh, lwh, kh, vh, ah, bh, gate = tile_fwd(f_rw_mid, "rw_mid", mid_tiles, mid_w, [hshape] * 6 + [sd((S, D), F32)], 128)
    yh, states = rwkv_fwd(rh, lwh, kh, vh, ah, bh)
    post_w = [W["rw_lnx_g"], W["rw_lnx_b"], W["rw_rk"], W["rw_wo"]]
    post_tiles = [yh, rh, kh, vh, gate, x4]
    x5, = tile_fwd(f_rw_post, "rw_post", post_tiles, post_w, [sd((S, D), F32)], 128)
    x6 = ffn(x5, 1, 1)
    dx6, loss_part = loss_head(x6, tgt)

    dx5 = ffn_back(x5, dx6, 1, 1)
    (dyh, drh, dkh, dvh, dgate, dx4), (d_lng, d_lnb, d_rk, d_wo) = tile_bwd(
        f_rw_post, "rw_post_bwd", post_tiles, post_w, [dx5], 128, [True] * 6, [True] * 4)
    drh2, dlwh, dkh2, dvh2, dah, dbh = rwkv_bwd(rh, lwh, kh, vh, ah, bh, states, dyh)
    mid_cts = [drh + drh2, dlwh, dkh + dkh2, dvh + dvh2, dah, dbh, dgate]
    (dh, dhs, dr, dk, dv), dmid_w = tile_bwd(f_rw_mid, "rw_mid_bwd", mid_tiles, mid_w, mid_cts, 128,
                                             [True] * 5, [True] * len(mid_w))
    dmix = {}
    for nm, ct, row, wname in (("r", dr, 0, "rw_wr"), ("k", dk, 2, "rw_wk"), ("v", dv, 3, "rw_wv")):
        (dh, dhs), (dmix[row], G[wname]) = tile_bwd(
            f_rw_proj, f"rw_proj_{nm}_bwd", [h, hs], [mix[row:row + 1], W[wname]], [ct], 256,
            [True, True], [True, True], acc={0: dh, 1: dhs})
    dx4, G[("mix_norm", 1)] = norm_shift_bwd(x4, g1, dh, dhs, dx4)
    dmix3 = dmid_w[0]
    G["rw_mix"] = jnp.concatenate([dmix[0], dmix3[0:1], dmix[2], dmix[3], dmix3[1:2], dmix3[2:3]], axis=0)
    for nm, gv in zip(("rw_w0", "rw_a0", "rw_kk", "rw_ka", "rw_w1", "rw_w2", "rw_a1", "rw_a2", "rw_g1", "rw_g2"), dmid_w[1:]):
        G[nm] = gv
    G["rw_lnx_g"], G["rw_lnx_b"], G["rw_rk"], G["rw_wo"] = d_lng, d_lnb, d_rk, d_wo
    dx3 = ffn_back(x3, dx4, 1, 0)
    dx2 = ffn_back(x2, dx3, 0, 1)
    (dx1, doa, do_n, dlse_n), (G["attn_w_out"],) = tile_bwd(
        f_attn_out, "attn_out_bwd", [x1, oa, o_n, lse_n], [W["attn_w_out"]], [dx2], 256, [True] * 4, [True])
    do_s, dlse_s = reorder("sub_do", do_n, DIL, False), reorder("sub_dlse", dlse_n, DIL, False)
    dqs, dks, dvs, dsum = dil_bwd(qs, ks, vs, W["bias_mat"], o_s, lse_s, do_s, dlse_s)
    G["rel_bias"] = bias_grad(dsum, W["buckets"])
    dqn, dkn, dvv = (reorder(nm, t, DIL, True) for nm, t in (("nat_dq", dqs), ("nat_dk", dks), ("nat_dv", dvs)))
    dsbq, dsbk, dsbv = sb_bwd(sbq, sbk, sbv, doa)
    dg0 = []
    dwin = []
    (dx1,), (dg, dw) = tile_bwd(f_attn_sb, "attn_in_sb_bwd", [x1], [g0, W["attn_w_in"][0]], [dsbq, dsbk, dsbv], 256,
                                [True], [True, True], acc={0: dx1})
    dg0.append(dg), dwin.append(dw)
    (dx1,), (dg, dw, G["attn_q_norm"]) = tile_bwd(f_attn_qk, "attn_in_q_bwd", [x1], [g0, W["attn_w_in"][1], W["attn_q_norm"]],
                                                  [dqn], 256, [True], [True] * 3, acc={0: dx1})
    dg0.append(dg), dwin.append(dw)
    (dx1,), (dg, dw, G["attn_k_norm"]) = tile_bwd(f_attn_qk, "attn_in_k_bwd", [x1], [g0, W["attn_w_in"][2], W["attn_k_norm"]],
                                                  [dkn], 256, [True], [True] * 3, acc={0: dx1})
    dg0.append(dg), dwin.append(dw)
    (dx1,), (dg, dw) = tile_bwd(f_attn_v, "attn_in_v_bwd", [x1], [g0, W["attn_w_in"][3]], [dvv], 256,
                                [True], [True, True], acc={0: dx1})
    dg0.append(dg), dwin.append(dw)
    G[("mix_norm", 0)] = dg0
    G["attn_w_in"] = dwin
    dx0 = ffn_back(x0, dx1, 0, 0)
    return loss_part, dx0, G


VEC_ROWS = ("ffn_norm", "rw_mix", "rw_w0", "rw_a0", "rw_kk", "rw_ka", "rw_lnx_g", "rw_lnx_b")


def kernel(x, ffn_norm, ffn_w_gate, ffn_w_up, ffn_w_down, mix_norm, rel_bias, attn_w_in, attn_q_norm, attn_k_norm, attn_w_out, rw_mix, rw_w0, rw_w1, rw_w2, rw_a0, rw_a1, rw_a2, rw_g1, rw_g2, rw_kk, rw_ka, rw_rk, rw_wr, rw_wk, rw_wv, rw_wo, rw_lnx_g, rw_lnx_b, loss_target, m_ffn_norm, m_ffn_w_gate, m_ffn_w_up, m_ffn_w_down, m_mix_norm, m_rel_bias, m_attn_w_in, m_attn_q_norm, m_attn_k_norm, m_attn_w_out, m_rw_mix, m_rw_w0, m_rw_w1, m_rw_w2, m_rw_a0, m_rw_a1, m_rw_a2, m_rw_g1, m_rw_g2, m_rw_kk, m_rw_ka, m_rw_rk, m_rw_wr, m_rw_wk, m_rw_wv, m_rw_wo, m_rw_lnx_g, m_rw_lnx_b, v_ffn_norm, v_ffn_w_gate, v_ffn_w_up, v_ffn_w_down, v_mix_norm, v_rel_bias, v_attn_w_in, v_attn_q_norm, v_attn_k_norm, v_attn_w_out, v_rw_mix, v_rw_w0, v_rw_w1, v_rw_w2, v_rw_a0, v_rw_a1, v_rw_a2, v_rw_g1, v_rw_g2, v_rw_kk, v_rw_ka, v_rw_rk, v_rw_wr, v_rw_wk, v_rw_wv, v_rw_wo, v_rw_lnx_g, v_rw_lnx_b):
    names = ["ffn_norm", "ffn_w_gate", "ffn_w_up", "ffn_w_down", "mix_norm", "rel_bias", "attn_w_in", "attn_q_norm",
             "attn_k_norm", "attn_w_out", "rw_mix", "rw_w0", "rw_w1", "rw_w2", "rw_a0", "rw_a1", "rw_a2", "rw_g1", "rw_g2",
             "rw_kk", "rw_ka", "rw_rk", "rw_wr", "rw_wk", "rw_wv", "rw_wo", "rw_lnx_g", "rw_lnx_b"]
    loc = locals()
    w = {n: loc[n] for n in names}
    mom = {n: loc["m_" + n] for n in names}
    vel = {n: loc["v_" + n] for n in names}
    S = x.shape[1]

    vec_shard = jnp.concatenate([w[n].reshape(-1, 256) for n in VEC_ROWS], axis=0)
    mats = ["ffn_w_gate", "ffn_w_up", "ffn_w_down", "attn_w_in", "attn_w_out", "rw_w1", "rw_w2", "rw_a1", "rw_a2",
            "rw_g1", "rw_g2", "rw_wr", "rw_wk", "rw_wv", "rw_wo"]
    send = [vec_shard] + [(w[n] if w[n].shape[0] != 1 else w[n][0]).astype(BF16) for n in mats]
    got = gather_chips(send)
    vec_full = _unshard_cols(got[0])
    gm = dict(zip(mats, got[1:]))
    W = {
        "ffn_norm": [[vec_full[2 * l + j][None] for j in range(2)] for l in range(2)],
        "ffn_w_gate": gm["ffn_w_gate"], "ffn_w_up": gm["ffn_w_up"], "ffn_w_down": gm["ffn_w_down"],
        "mix_norm": [mix_norm[0:1], mix_norm[1:2]],
        "attn_w_in": [gm["attn_w_in"][p] for p in range(N_CHIPS)],
        "attn_q_norm": attn_q_norm, "attn_k_norm": attn_k_norm,
        "attn_w_out": _unshard_cols(gm["attn_w_out"]),
        "rw_mix": vec_full[4:10],
        "rw_w1": gm["rw_w1"].reshape(D, -1), "rw_a1": gm["rw_a1"].reshape(D, -1), "rw_g1": gm["rw_g1"].reshape(D, -1),
        "rw_w2": _unshard_cols(gm["rw_w2"]), "rw_a2": _unshard_cols(gm["rw_a2"]), "rw_g2": _unshard_cols(gm["rw_g2"]),
        "rw_wr": gm["rw_wr"].reshape(D, D), "rw_wk": gm["rw_wk"].reshape(D, D), "rw_wv": gm["rw_wv"].reshape(D, D),
        "rw_wo": gm["rw_wo"].reshape(D, D),
        "rw_rk": rw_rk[0][:, None, :],
    }
    for i, n in enumerate(("rw_w0", "rw_a0", "rw_kk", "rw_ka", "rw_lnx_g", "rw_lnx_b")):
        W[n] = vec_full[10 + i][None]
    buckets = _bucket_maps()
    W["buckets"] = buckets
    W["bias_mat"] = jnp.concatenate([jnp.transpose(rel_bias[buckets[g], 4 * g:4 * g + 4], (2, 0, 1)) for g in range(3)], axis=0)

    loss_part, dx, G = _forward_backward(x[0], loss_target[0], W)
    loss = lax.psum(loss_part[0, 0], ("x", "y", "c"))

    def ffn_stack(idx):
        return jnp.stack([jnp.stack([G[("ffn", l, j)][idx] for j in range(2)], axis=1) for l in range(2)], axis=1)

    vec_rows = [G[("ffn", l, j)][0] for l in range(2) for j in range(2)] + [G["rw_mix"]] + \
               [G[n] for n in ("rw_w0", "rw_a0", "rw_kk", "rw_ka", "rw_lnx_g", "rw_lnx_b")]
    full = {
        "vec": _shard_cols(jnp.concatenate(vec_rows, axis=0)),
        "ffn_w_gate": ffn_stack(1), "ffn_w_up": ffn_stack(2), "ffn_w_down": ffn_stack(3),
        "attn_w_in": jnp.stack(G["attn_w_in"]),
        "attn_w_out": _shard_cols(G["attn_w_out"]),
        "rw_w1": G["rw_w1"].reshape(N_CHIPS, 256, -1), "rw_a1": G["rw_a1"].reshape(N_CHIPS, 256, -1),
        "rw_g1": G["rw_g1"].reshape(N_CHIPS, 256, -1),
        "rw_w2": _shard_cols(G["rw_w2"]), "rw_a2": _shard_cols(G["rw_a2"]), "rw_g2": _shard_cols(G["rw_g2"]),
        "rw_wr": G["rw_wr"].reshape(N_CHIPS, 256, D), "rw_wk": G["rw_wk"].reshape(N_CHIPS, 256, D),
        "rw_wv": G["rw_wv"].reshape(N_CHIPS, 256, D), "rw_wo": G["rw_wo"].reshape(N_CHIPS, 256, D),
    }
    order = ["vec"] + mats
    landed = scatter_chips([full[n] for n in order])
    mine = [sum_slots(f"sum_{n}", p.reshape(N_CHIPS, -1, p.shape[-1])) for n, p in zip(order, landed)]
    theirs = swap_cores(mine)

    rep = jnp.concatenate([G[("mix_norm", 0)][0] + G[("mix_norm", 0)][1] + G[("mix_norm", 0)][2] + G[("mix_norm", 0)][3],
                           G[("mix_norm", 1)]], axis=0).reshape(16, 128)
    rep = jnp.concatenate([rep, G["rel_bias"], jnp.pad(G["attn_q_norm"], ((0, 0), (0, 64))),
                           jnp.pad(G["attn_k_norm"], ((0, 0), (0, 64))), G["rw_rk"].reshape(8, 128),
                           jnp.zeros((2, 128), F32)], axis=0)
    rep_sum = sum_slots("sum_replicated", gather_all([rep])[0])
    g_rep = {
        "mix_norm": rep_sum[0:16].reshape(2, D),
        "rel_bias": jnp.transpose(rep_sum[16:28, :N_BUCKETS]),
        "attn_q_norm": rep_sum[28:29, :HEAD], "attn_k_norm": rep_sum[29:30, :HEAD],
        "rw_rk": rep_sum[30:38].reshape(1, RW_H, HEAD),
    }

    out = {}

    def adam(n, ga, gb):
        shp = w[n].shape
        to2 = lambda a: a.reshape(-1, shp[-1])
        res = adam_step(f"adam_{n}", to2(ga), None if gb is None else to2(gb), to2(w[n]), to2(mom[n]), to2(vel[n]))
        out[n] = tuple(r.reshape(shp) for r in res)

    part = dict(zip(order, zip(mine, theirs)))
    for n in mats:
        adam(n, *part[n])
    va, vb = part["vec"]
    rows = {"ffn_norm": (0, 4), "rw_mix": (4, 10), "rw_w0": (10, 11), "rw_a0": (11, 12), "rw_kk": (12, 13),
            "rw_ka": (13, 14), "rw_lnx_g": (14, 15), "rw_lnx_b": (15, 16)}
    for n, (lo, hi) in rows.items():
        adam(n, va[lo:hi], vb[lo:hi])
    for n, gv in g_rep.items():
        adam(n, gv, None)

    grads = [out[n][0] for n in names]
    deltas = [out[n][1] for n in names]
    new_m = [out[n][2] for n in names]
    new_v = [out[n][3] for n in names]
    return (loss, dx[None], *grads, *deltas, *new_m, *new_v)
```

```python
import functools
import math

import jax
import jax.numpy as jnp
from jax import lax
from jax.experimental import pallas as pl
from jax.experimental.pallas import tpu as pltpu

F32, BF16 = jnp.float32, jnp.bfloat16
HI = lax.Precision.HIGHEST
MESH = pl.DeviceIdType.MESH

D = 1024
HEAD = 64
N_CHIPS = 4
FF_SHARD = 704
SB_W = 256
DL_HEADS = 12
DL_PAIRS = 6
DIL = (1, 4, 16)
QBLK = 128
N_BUCKETS = 32
MAX_DISTANCE = 2048
RW_H = 16
RW_CHUNK = 64
NORM_EPS = 1e-6
GN_EPS = 64e-5
NEG_INF = -1e30
VMEM_LIMIT = 56 * 1024 * 1024

ADAM_LR, ADAM_B1, ADAM_B2, ADAM_EPS, ADAM_WD, ADAM_STEP = 0.001, 0.9, 0.999, 1e-08, 0.01, 10


def _cp(sem):
    return pltpu.CompilerParams(dimension_semantics=sem, vmem_limit_bytes=VMEM_LIMIT)


def _dg(a, b, dims, prec=None):
    return lax.dot_general(a, b, (dims, ((), ())), precision=prec, preferred_element_type=F32)


def _bdot(a, b, dims):
    return _dg(a.astype(BF16), b.astype(BF16), dims)


@jax.custom_vjp
def mm(a, b):
    return _bdot(a, b, ((1,), (0,)))


def _mm_fwd(a, b):
    return _bdot(a, b, ((1,), (0,))), (a, b)


def _mm_bwd(res, g):
    a, b = res
    return _bdot(g, b, ((1,), (1,))), _bdot(a, g, ((0,), (0,)))


mm.defvjp(_mm_fwd, _mm_bwd)


def rms(x, g):
    return x * lax.rsqrt(jnp.mean(x * x, axis=-1, keepdims=True) + NORM_EPS) * g


def group_sum(x, nh):
    w = x.shape[-1]
    e = (lax.broadcasted_iota(jnp.int32, (w, nh), 0) // HEAD == lax.broadcasted_iota(jnp.int32, (w, nh), 1)).astype(F32)
    s = _dg(x, e, ((1,), (0,)), HI)
    return _dg(s, e, ((1,), (1,)), HI)


def softplus(u):
    return jnp.maximum(u, 0.0) + jnp.log1p(jnp.exp(-jnp.abs(u)))


def to_heads(t, nh=RW_H):
    return jnp.stack([t[:, HEAD * h:HEAD * (h + 1)] for h in range(nh)])


def from_heads(t):
    return jnp.concatenate([t[h] for h in range(t.shape[0])], axis=-1)


def _tile_spec(shape, tm):
    if len(shape) == 2:
        return pl.BlockSpec((tm, shape[1]), lambda t: (t, 0))
    return pl.BlockSpec((shape[0], tm, shape[2]), lambda t: (0, t, 0))


def _full_spec(shape):
    nd = len(shape)
    return pl.BlockSpec(tuple(shape), lambda t: (0,) * nd)


def _rows(a):
    return a.shape[0] if a.ndim == 2 else a.shape[1]


def tile_fwd(f, name, tiles, weights, outs, tm):
    nt, nw = len(tiles), len(weights)

    def body(*refs):
        tv = [r[...] for r in refs[:nt]]
        wv = [r[...].astype(F32) for r in refs[nt:nt + nw]]
        res = f(*tv, *wv)
        if not isinstance(res, (tuple, list)):
            res = (res,)
        for o, v in zip(refs[nt + nw:], res):
            o[...] = v.astype(o.dtype)

    return pl.pallas_call(
        body, name=name, grid=(_rows(tiles[0]) // tm,),
        in_specs=[_tile_spec(a.shape, tm) for a in tiles] + [_full_spec(w.shape) for w in weights],
        out_specs=[_tile_spec(o.shape, tm) for o in outs],
        out_shape=list(outs),
        compiler_params=_cp(("parallel",)),
    )(*tiles, *weights)


def tile_bwd(f, name, tiles, weights, cts, tm, dt, dw, acc=None):
    acc = acc or {}
    nt, nw, nc = len(tiles), len(weights), len(cts)
    acc_idx = sorted(acc)
    na = len(acc_idx)
    dti = [i for i in range(nt) if dt[i]]
    dwi = [i for i in range(nw) if dw[i]]

    def body(*refs):
        tv = [r[...] for r in refs[:nt]]
        wv = [r[...].astype(F32) for r in refs[nt:nt + nw]]
        cv = [r[...] for r in refs[nt + nw:nt + nw + nc]]
        av = {i: r[...] for i, r in zip(acc_idx, refs[nt + nw + nc:nt + nw + nc + na])}
        orefs = refs[nt + nw + nc + na:]

        def g(*diff):
            t2, w2 = list(tv), list(wv)
            for i, v in zip(dti, diff[:len(dti)]):
                t2[i] = v
            for i, v in zip(dwi, diff[len(dti):]):
                w2[i] = v
            res = f(*t2, *w2)
            return tuple(res) if isinstance(res, (tuple, list)) else (res,)

        _, vjp = jax.vjp(g, *[tv[i] for i in dti], *[wv[i] for i in dwi])
        grads = vjp(tuple(cv))
        for k, i in enumerate(dti):
            gt = grads[k]
            if i in av:
                gt = gt + av[i]
            orefs[k][...] = gt
        first = pl.program_id(0) == 0
        for k, i in enumerate(dwi):
            o = orefs[len(dti) + k]
            gw = grads[len(dti) + k]

            @pl.when(first)
            def _(o=o, gw=gw):
                o[...] = gw

            @pl.when(jnp.logical_not(first))
            def _(o=o, gw=gw):
                o[...] += gw

    out_shape = [jax.ShapeDtypeStruct(tiles[i].shape, F32) for i in dti] + \
                [jax.ShapeDtypeStruct(weights[i].shape, F32) for i in dwi]
    res = pl.pallas_call(
        body, name=name, grid=(_rows(tiles[0]) // tm,),
        in_specs=[_tile_spec(a.shape, tm) for a in tiles] + [_full_spec(w.shape) for w in weights] +
                 [_tile_spec(c.shape, tm) for c in cts] + [_tile_spec(tiles[i].shape, tm) for i in acc_idx],
        out_specs=[_tile_spec(tiles[i].shape, tm) for i in dti] + [_full_spec(weights[i].shape) for i in dwi],
        out_shape=out_shape,
        compiler_params=_cp(("arbitrary",)),
    )(*tiles, *weights, *cts, *[acc[i] for i in acc_idx])
    return list(res[:len(dti)]), list(res[len(dti):])


def _ffn_wspec(l, j, rows, cols, cfirst):
    if cfirst:
        return pl.BlockSpec((1, 1, 1, rows, cols), lambda c, t: (c, l, j, 0, 0))
    return pl.BlockSpec((1, 1, 1, rows, cols), lambda t, c: (c, l, j, 0, 0))


def ffn_fwd(x, g, wg, wu, wd, l, j, tm=512):
    S = x.shape[0]

    def body(x_ref, g_ref, wg_ref, wu_ref, wd_ref, o_ref, h_ref, acc_ref):
        c = pl.program_id(1)

        @pl.when(c == 0)
        def _():
            h_ref[...] = rms(x_ref[...], g_ref[...]).astype(BF16)
            acc_ref[...] = jnp.zeros_like(acc_ref)

        h = h_ref[...]
        a = _bdot(h, wg_ref[0, 0, 0], ((1,), (0,)))
        b = _bdot(h, wu_ref[0, 0, 0], ((1,), (0,)))
        y = a * jax.nn.sigmoid(a) * b
        acc_ref[...] += _bdot(y, wd_ref[0, 0, 0], ((1,), (0,)))

        @pl.when(c == N_CHIPS - 1)
        def _():
            o_ref[...] = x_ref[...] + 0.5 * acc_ref[...]

    return pl.pallas_call(
        body, name=f"ffn_fwd_{l}{j}", grid=(S // tm, N_CHIPS),
        in_specs=[pl.BlockSpec((tm, D), lambda t, c: (t, 0)), pl.BlockSpec((1, D), lambda t, c: (0, 0)),
                  _ffn_wspec(l, j, D, FF_SHARD, False), _ffn_wspec(l, j, D, FF_SHARD, False),
                  _ffn_wspec(l, j, FF_SHARD, D, False)],
        out_specs=pl.BlockSpec((tm, D), lambda t, c: (t, 0)),
        out_shape=jax.ShapeDtypeStruct((S, D), F32),
        scratch_shapes=[pltpu.VMEM((tm, D), BF16), pltpu.VMEM((tm, D), F32)],
        compiler_params=_cp(("parallel", "arbitrary")),
    )(x, g, wg, wu, wd)


def ffn_bwd(x, g, wg, wu, wd, dout, l, j, tm=256):
    S = x.shape[0]

    def body(x_ref, g_ref, wg_ref, wu_ref, wd_ref, do_ref, dh_ref, dwg_ref, dwu_ref, dwd_ref):
        t = pl.program_id(1)
        h = rms(x_ref[...], g_ref[...]).astype(BF16)
        wgv, wuv, wdv = wg_ref[0, 0, 0], wu_ref[0, 0, 0], wd_ref[0, 0, 0]
        a = _bdot(h, wgv, ((1,), (0,)))
        b = _bdot(h, wuv, ((1,), (0,)))
        sig = jax.nn.sigmoid(a)
        s = a * sig
        dyd = 0.5 * do_ref[...]
        dy = _bdot(dyd, wdv, ((1,), (1,)))
        dwd = _bdot(s * b, dyd, ((0,), (0,)))
        db = dy * s
        da = dy * b * (sig * (1.0 + a * (1.0 - sig)))
        dwg = _bdot(h, da, ((0,), (0,)))
        dwu = _bdot(h, db, ((0,), (0,)))
        dh_ref[0] = _bdot(da, wgv, ((1,), (1,))) + _bdot(db, wuv, ((1,), (1,)))

        @pl.when(t == 0)
        def _():
            dwg_ref[0] = dwg
            dwu_ref[0] = dwu
            dwd_ref[0] = dwd

        @pl.when(t != 0)
        def _():
            dwg_ref[0] += dwg
            dwu_ref[0] += dwu
            dwd_ref[0] += dwd

    return pl.pallas_call(
        body, name=f"ffn_bwd_{l}{j}", grid=(N_CHIPS, S // tm),
        in_specs=[pl.BlockSpec((tm, D), lambda c, t: (t, 0)), pl.BlockSpec((1, D), lambda c, t: (0, 0)),
                  _ffn_wspec(l, j, D, FF_SHARD, True), _ffn_wspec(l, j, D, FF_SHARD, True),
                  _ffn_wspec(l, j, FF_SHARD, D, True), pl.BlockSpec((tm, D), lambda c, t: (t, 0))],
        out_specs=[pl.BlockSpec((1, tm, D), lambda c, t: (c, t, 0)),
                   pl.BlockSpec((1, D, FF_SHARD), lambda c, t: (c, 0, 0)),
                   pl.BlockSpec((1, D, FF_SHARD), lambda c, t: (c, 0, 0)),
                   pl.BlockSpec((1, FF_SHARD, D), lambda c, t: (c, 0, 0))],
        out_shape=[jax.ShapeDtypeStruct((N_CHIPS, S, D), F32), jax.ShapeDtypeStruct((N_CHIPS, D, FF_SHARD), F32),
                   jax.ShapeDtypeStruct((N_CHIPS, D, FF_SHARD), F32), jax.ShapeDtypeStruct((N_CHIPS, FF_SHARD, D), F32)],
        compiler_params=_cp(("parallel", "arbitrary")),
    )(x, g, wg, wu, wd, dout)


def norm_bwd(name, x, g, dh_parts, dres, tm=256):
    S = x.shape[0]
    P = dh_parts.shape[0]

    def body(x_ref, g_ref, dh_ref, dr_ref, dx_ref, dg_ref):
        dh = dh_ref[0]
        for p in range(1, P):
            dh = dh + dh_ref[p]
        _, vjp = jax.vjp(rms, x_ref[...], g_ref[...])
        dx, dg = vjp(dh)
        dx_ref[...] = dr_ref[...] + dx

        @pl.when(pl.program_id(0) == 0)
        def _():
            dg_ref[...] = dg

        @pl.when(pl.program_id(0) != 0)
        def _():
            dg_ref[...] += dg

    return pl.pallas_call(
        body, name=name, grid=(S // tm,),
        in_specs=[pl.BlockSpec((tm, D), lambda t: (t, 0)), pl.BlockSpec((1, D), lambda t: (0, 0)),
                  pl.BlockSpec((P, tm, D), lambda t: (0, t, 0)), pl.BlockSpec((tm, D), lambda t: (t, 0))],
        out_specs=[pl.BlockSpec((tm, D), lambda t: (t, 0)), pl.BlockSpec((1, D), lambda t: (0, 0))],
        out_shape=[jax.ShapeDtypeStruct((S, D), F32), jax.ShapeDtypeStruct((1, D), F32)],
        compiler_params=_cp(("arbitrary",)),
    )(x, g, dh_parts, dres)


def f_attn_sb(x, g, w):
    pr = mm(rms(x, g), w)
    return pr[:, :SB_W], pr[:, SB_W:2 * SB_W], pr[:, 2 * SB_W:]


def _pairs(y):
    return jnp.stack([y[:, 128 * j:128 * (j + 1)] for j in range(DL_PAIRS)])


def f_attn_qk(x, g, w, nrm):
    pr = mm(rms(x, g), w)
    ms = group_sum(pr * pr, DL_HEADS) * (1.0 / HEAD)
    return _pairs(pr * lax.rsqrt(ms + NORM_EPS) * jnp.concatenate([nrm] * DL_HEADS, axis=1))


def f_attn_v(x, g, w):
    return _pairs(mm(rms(x, g), w))


def _sb_tiles(q, k, qpos, kpos):
    z = _bdot(q, k, ((1,), (1,))) * (HEAD ** -0.5)
    strict = kpos < qpos
    keep = jnp.where(strict, -softplus(z), 0.0)
    return z, strict, keep


def _tri(n, upper):
    r = lax.broadcasted_iota(jnp.int32, (n, n), 0)
    c = lax.broadcasted_iota(jnp.int32, (n, n), 1)
    return ((r > c) if upper else (r < c)).astype(BF16)


def _tri_sums(xs, tri):
    x = jnp.concatenate(xs, axis=0)
    hi, lo = _split2(x)
    y = _dg(hi, tri, ((1,), (0,))) + _dg(lo, tri, ((1,), (0,)))
    n = xs[0].shape[0]
    return [y[n * i:n * (i + 1)] for i in range(len(xs))]


def sb_fwd(q, k, v, tb=QBLK):
    S = q.shape[0]
    nh = SB_W // HEAD

    def body(q_ref, k_ref, v_ref, o_ref):
        qb = pl.program_id(0)
        qpos = qb * tb + lax.broadcasted_iota(jnp.int32, (tb, tb), 0)
        col = lax.broadcasted_iota(jnp.int32, (tb, tb), 1)
        after_mat = _tri(tb, True)
        sls = [slice(HEAD * h, HEAD * (h + 1)) for h in range(nh)]
        qs = [q_ref[:, sl] for sl in sls]

        def step(i, carry):
            accs, runs = carry
            kb = qb - i
            rows = pl.ds(pl.multiple_of(kb * tb, tb), tb)
            kblk, vblk = k_ref[rows, :], v_ref[rows, :]
            tiles = [_sb_tiles(qs[h], kblk[:, sls[h]], qpos, kb * tb + col) for h in range(nh)]
            afters = _tri_sums([t[2] for t in tiles], after_mat)
            new_accs, new_runs = [], []
            for h, (z, strict, keep) in enumerate(tiles):
                w = jnp.where(strict, jnp.exp(z + keep + afters[h] + runs[h]), 0.0)
                new_accs.append(accs[h] + _bdot(w, vblk[:, sls[h]], ((1,), (0,))))
                new_runs.append(runs[h] + jnp.sum(keep, axis=1, keepdims=True))
            return tuple(new_accs), tuple(new_runs)

        init = (tuple(jnp.zeros((tb, HEAD), F32) for _ in range(nh)), tuple(jnp.zeros((tb, 1), F32) for _ in range(nh)))
        accs, _ = lax.fori_loop(0, qb + 1, step, init)
        o_ref[...] = jnp.concatenate(accs, axis=1)

    return pl.pallas_call(
        body, name="sb_fwd", grid=(S // tb,),
        in_specs=[pl.BlockSpec((tb, SB_W), lambda i: (i, 0)), pl.BlockSpec((S, SB_W), lambda i: (0, 0)),
                  pl.BlockSpec((S, SB_W), lambda i: (0, 0))],
        out_specs=pl.BlockSpec((tb, SB_W), lambda i: (i, 0)),
        out_shape=jax.ShapeDtypeStruct((S, SB_W), F32),
        compiler_params=_cp(("parallel",)),
    )(q, k, v)


def sb_bwd(q, k, v, do, tb=QBLK):
    S = q.shape[0]
    nh = SB_W // HEAD
    scale = HEAD ** -0.5

    def body(q_ref, k_ref, v_ref, do_ref, dq_ref, dk_ref, dv_ref, g_scr):
        qb = pl.program_id(0)

        @pl.when(qb == 0)
        def _():
            dk_ref[...] = jnp.zeros_like(dk_ref)
            dv_ref[...] = jnp.zeros_like(dv_ref)

        qpos = qb * tb + lax.broadcasted_iota(jnp.int32, (tb, tb), 0)
        col = lax.broadcasted_iota(jnp.int32, (tb, tb), 1)
        after_mat = _tri(tb, True)
        before_mat = _tri(tb, False)
        sls = [slice(HEAD * h, HEAD * (h + 1)) for h in range(nh)]
        qs = [q_ref[:, sl] for sl in sls]
        dos = [do_ref[:, sl] for sl in sls]

        def right_to_left(i, runs):
            kb = qb - i
            rows = pl.ds(pl.multiple_of(kb * tb, tb), tb)
            kblk, vblk = k_ref[rows, :], v_ref[rows, :]
            tiles = [_sb_tiles(qs[h], kblk[:, sls[h]], qpos, kb * tb + col) for h in range(nh)]
            afters = _tri_sums([t[2] for t in tiles], after_mat)
            dvs, new_runs = [], []
            for h, (z, strict, keep) in enumerate(tiles):
                w = jnp.where(strict, jnp.exp(z + keep + afters[h] + runs[h]), 0.0)
                g_scr[h, kb] = _bdot(dos[h], vblk[:, sls[h]], ((1,), (1,))) * w
                dvs.append(_bdot(w, dos[h], ((0,), (0,))))
                new_runs.append(runs[h] + jnp.sum(keep, axis=1, keepdims=True))
            dv_ref[rows, :] += jnp.concatenate(dvs, axis=1)
            return tuple(new_runs)

        zero_runs = tuple(jnp.zeros((tb, 1), F32) for _ in range(nh))
        lax.fori_loop(0, qb + 1, right_to_left, zero_runs)

        def left_to_right(kb, carry):
            dqs, runs = carry
            rows = pl.ds(pl.multiple_of(kb * tb, tb), tb)
            kblk = k_ref[rows, :]
            strict = (kb * tb + col) < qpos
            gws = [g_scr[h, kb] for h in range(nh)]
            befores = _tri_sums(gws, before_mat)
            new_dqs, new_runs, dks = [], [], []
            for h in range(nh):
                kh = kblk[:, sls[h]]
                sig = jax.nn.sigmoid(_bdot(qs[h], kh, ((1,), (1,))) * scale)
                dkeep = jnp.where(strict, befores[h] + runs[h], 0.0)
                dz = (gws[h] * (1.0 - sig) - dkeep * sig) * scale
                new_dqs.append(dqs[h] + _bdot(dz, kh, ((1,), (0,))))
                dks.append(_bdot(dz, qs[h], ((0,), (0,))))
                new_runs.append(runs[h] + jnp.sum(gws[h], axis=1, keepdims=True))
            dk_ref[rows, :] += jnp.concatenate(dks, axis=1)
            return tuple(new_dqs), tuple(new_runs)

        dqs, _ = lax.fori_loop(0, qb + 1, left_to_right,
                               (tuple(jnp.zeros((tb, HEAD), F32) for _ in range(nh)), zero_runs))
        dq_ref[...] = jnp.concatenate(dqs, axis=1)

    whole = pl.BlockSpec((S, SB_W), lambda i: (0, 0))
    blk = pl.BlockSpec((tb, SB_W), lambda i: (i, 0))
    return pl.pallas_call(
        body, name="sb_bwd", grid=(S // tb,),
        in_specs=[blk, whole, whole, blk], out_specs=[blk, whole, whole],
        out_shape=[jax.ShapeDtypeStruct((S, SB_W), F32)] * 3,
        scratch_shapes=[pltpu.VMEM((nh, S // tb, tb, tb), F32)],
        compiler_params=_cp(("arbitrary",)),
    )(q, k, v, do)


def reorder(name, x, groups, inverse):
    S = x.shape[1]
    out = x
    for gi, r in enumerate(groups):
        if r > 1:
            out = _reorder_call(f"{name}_{r}", x, out, gi, r, S // r, inverse)
    return out


def _reorder_call(name, x, prev, gi, r, L, inverse):
    S = x.shape[1]
    whole = pl.BlockSpec((None, S, 128), lambda p, c: (2 * gi + p, 0, 0))
    part = pl.BlockSpec((None, L, 128), lambda p, c: (2 * gi + p, c, 0))

    def body(x_ref, prev_ref, o_ref):
        c = pl.program_id(1)
        if inverse:
            o_ref[pl.ds(c, L, stride=r), :] = x_ref[...]
        else:
            o_ref[...] = x_ref[pl.ds(c, L, stride=r), :]

    return pl.pallas_call(
        body, name=name, grid=(2, r),
        in_specs=[part if inverse else whole, pl.BlockSpec(memory_space=pl.ANY)],
        out_specs=whole if inverse else part,
        out_shape=jax.ShapeDtypeStruct(x.shape, x.dtype),
        input_output_aliases={1: 0},
        compiler_params=_cp(("parallel", "arbitrary")),
    )(x, prev)


def _dil_blocks(S):
    return S // QBLK


def _dil_mask(n_in_stream):
    qi = lax.broadcasted_iota(jnp.int32, (QBLK, 2 * QBLK), 0)
    kj = lax.broadcasted_iota(jnp.int32, (QBLK, 2 * QBLK), 1) - QBLK
    dist = qi - kj
    return (dist >= 0) & (dist <= QBLK) & ((n_in_stream > 0) | (kj >= 0))


def _stream_pos(gi, i, S):
    nb = jnp.where(gi == 0, S // (QBLK * DIL[0]), jnp.where(gi == 1, S // (QBLK * DIL[1]), S // (QBLK * DIL[2])))
    return i % nb


def dil_fwd(q, k, v, bias):
    S = q.shape[1]
    nblk = _dil_blocks(S)

    def body(q_ref, kc_ref, kp_ref, vc_ref, vp_ref, b_ref, o_ref, l_ref):
        gi, i = pl.program_id(0), pl.program_id(1)
        mask = _dil_mask(_stream_pos(gi, i, S))
        for j in range(2):
            q2, kc, kp, vc, vp = q_ref[j], kc_ref[j], kp_ref[j], vc_ref[j], vp_ref[j]
            os_, ls_ = [], []
            for hh in range(2):
                sl = slice(HEAD * hh, HEAD * (hh + 1))
                kw = jnp.concatenate([kp[:, sl], kc[:, sl]], axis=0)
                vw = jnp.concatenate([vp[:, sl], vc[:, sl]], axis=0)
                lg = _bdot(q2[:, sl], kw, ((1,), (1,))) * (HEAD ** -0.5) + b_ref[2 * j + hh]
                lg = jnp.where(mask, lg, NEG_INF)
                m = jnp.max(lg, axis=-1, keepdims=True)
                p = jnp.exp(lg - m)
                den = jnp.sum(p, axis=-1, keepdims=True)
                os_.append(_bdot(p / den, vw, ((1,), (0,))))
                ls_.append(jnp.broadcast_to(m + jnp.log(den), (QBLK, HEAD)))
            o_ref[j] = jnp.concatenate(os_, axis=1)
            l_ref[j] = jnp.concatenate(ls_, axis=1)

    cur = pl.BlockSpec((2, QBLK, 128), lambda g, i: (g, i, 0))
    prev = pl.BlockSpec((2, QBLK, 128), lambda g, i: (g, jnp.maximum(i - 1, 0), 0))
    return pl.pallas_call(
        body, name="dil_fwd", grid=(len(DIL), nblk),
        in_specs=[cur, cur, prev, cur, prev, pl.BlockSpec((4, QBLK, 2 * QBLK), lambda g, i: (g, 0, 0))],
        out_specs=[cur, cur],
        out_shape=[jax.ShapeDtypeStruct(q.shape, F32)] * 2,
        compiler_params=_cp(("parallel", "parallel")),
    )(q, k, k, v, v, bias)


def dil_bwd(q, k, v, bias, o, lse, do, dlse):
    S = q.shape[1]
    nblk = _dil_blocks(S)

    def body(q_ref, kc_ref, kp_ref, vc_ref, vp_ref, b_ref, o_ref, l_ref, do_ref, dl_ref,
             dq_ref, dk_ref, dv_ref, ds_ref, dk_car, dv_car):
        gi, i = pl.program_id(0), pl.program_id(1)

        @pl.when(i == 0)
        def _():
            ds_ref[...] = jnp.zeros_like(ds_ref)
            dk_car[...] = jnp.zeros_like(dk_car)
            dv_car[...] = jnp.zeros_like(dv_car)

        @pl.when(i < nblk)
        def _():
            mask = _dil_mask(_stream_pos(gi, i, S))
            for j in range(2):
                q2, kc, kp, vc, vp = q_ref[j], kc_ref[j], kp_ref[j], vc_ref[j], vp_ref[j]
                o2, l2, do2, dl2 = o_ref[j], l_ref[j], do_ref[j], dl_ref[j]
                dqs, dkps, dkcs, dvps, dvcs = [], [], [], [], []
                for hh in range(2):
                    sl = slice(HEAD * hh, HEAD * (hh + 1))
                    qh, doh = q2[:, sl], do2[:, sl]
                    kw = jnp.concatenate([kp[:, sl], kc[:, sl]], axis=0)
                    vw = jnp.concatenate([vp[:, sl], vc[:, sl]], axis=0)
                    lg = _bdot(qh, kw, ((1,), (1,))) * (HEAD ** -0.5) + b_ref[2 * j + hh]
                    p = jnp.where(mask, jnp.exp(lg - l2[:, HEAD * hh:HEAD * hh + 1]), 0.0)
                    dp = _bdot(doh, vw, ((1,), (1,)))
                    delta = jnp.sum(doh * o2[:, sl], axis=-1, keepdims=True)
                    dl = jnp.sum(dl2[:, sl], axis=-1, keepdims=True)
                    ds = p * (dp - delta + dl)
                    ds_ref[2 * j + hh] += ds
                    dsq = ds * (HEAD ** -0.5)
                    dqs.append(_bdot(dsq, kw, ((1,), (0,))))
                    dkw = _bdot(dsq, qh, ((0,), (0,)))
                    dvw = _bdot(p, doh, ((0,), (0,)))
                    dkps.append(dkw[:QBLK])
                    dkcs.append(dkw[QBLK:])
                    dvps.append(dvw[:QBLK])
                    dvcs.append(dvw[QBLK:])
                dq_ref[j] = jnp.concatenate(dqs, axis=1)
                dk_ref[j] = dk_car[j] + jnp.concatenate(dkps, axis=1)
                dv_ref[j] = dv_car[j] + jnp.concatenate(dvps, axis=1)
                dk_car[j] = jnp.concatenate(dkcs, axis=1)
                dv_car[j] = jnp.concatenate(dvcs, axis=1)

        @pl.when(i == nblk)
        def _():
            dk_ref[...] = dk_car[...]
            dv_ref[...] = dv_car[...]

    cur = pl.BlockSpec((2, QBLK, 128), lambda g, i: (g, jnp.minimum(i, nblk - 1), 0))
    prev = pl.BlockSpec((2, QBLK, 128), lambda g, i: (g, jnp.clip(i - 1, 0, nblk - 1), 0))
    bspec = pl.BlockSpec((4, QBLK, 2 * QBLK), lambda g, i: (g, 0, 0))
    return pl.pallas_call(
        body, name="dil_bwd", grid=(len(DIL), nblk + 1),
        in_specs=[cur, cur, prev, cur, prev, bspec, cur, cur, cur, cur],
        out_specs=[cur, prev, prev, bspec],
        out_shape=[jax.ShapeDtypeStruct(q.shape, F32)] * 3 + [jax.ShapeDtypeStruct(bias.shape, F32)],
        scratch_shapes=[pltpu.VMEM((2, QBLK, 128), F32), pltpu.VMEM((2, QBLK, 128), F32)],
        compiler_params=_cp(("arbitrary", "arbitrary")),
    )(q, k, k, v, v, bias, o, lse, do, dlse)


def _t5_bucket(dist):
    max_exact = N_BUCKETS // 2
    d = jnp.maximum(dist, 1).astype(F32)
    large = max_exact + (jnp.log(d / max_exact) / math.log(MAX_DISTANCE / max_exact)
                         * (N_BUCKETS - max_exact)).astype(jnp.int32)
    large = jnp.minimum(large, N_BUCKETS - 1)
    return jnp.where(dist < max_exact, dist, large)


def _bucket_maps():
    qi = jnp.arange(QBLK)[:, None]
    kj = jnp.arange(2 * QBLK)[None, :] - QBLK
    dist = jnp.maximum(qi - kj, 0)
    return jnp.stack([_t5_bucket(dist * r) for r in DIL])


def bias_table(rel_bias, buckets):
    def body(tbl_ref, bk_ref, o_ref):
        for h in range(DL_HEADS):
            bk = bk_ref[h // 4]

            def step(b, acc):
                return jnp.where(bk == b, tbl_ref[b, h], acc)

            o_ref[h] = lax.fori_loop(0, N_BUCKETS, step, jnp.zeros(bk.shape, F32))

    return pl.pallas_call(
        body, name="bias_table", out_shape=jax.ShapeDtypeStruct((DL_HEADS,) + buckets.shape[1:], F32),
        in_specs=[pl.BlockSpec(memory_space=pltpu.SMEM), pl.BlockSpec(memory_space=pltpu.VMEM)],
        out_specs=pl.BlockSpec(memory_space=pltpu.VMEM),
    )(rel_bias, buckets)


def bias_grad(ds, buckets):
    def body(ds_ref, bk_ref, o_ref):
        lane = lax.broadcasted_iota(jnp.int32, (1, 128), 1)
        for h in range(DL_HEADS):
            dsv = ds_ref[h]
            bk = bk_ref[h // 4]

            def step(b, row):
                return jnp.where(lane == b, jnp.sum(jnp.where(bk == b, dsv, 0.0)), row)

            o_ref[h:h + 1, :] = lax.fori_loop(0, N_BUCKETS, step, jnp.zeros((1, 128), F32))

    return pl.pallas_call(
        body, name="bias_grad", out_shape=jax.ShapeDtypeStruct((DL_HEADS, 128), F32),
        in_specs=[pl.BlockSpec(memory_space=pltpu.VMEM)] * 2, out_specs=pl.BlockSpec(memory_space=pltpu.VMEM),
    )(ds, buckets)


def f_attn_out(x, oa, o, lse, w):
    og = [jnp.concatenate([o[2 * g], o[2 * g + 1]], axis=1) for g in range(3)]
    lg = [jnp.concatenate([lse[2 * g], lse[2 * g + 1]], axis=1) for g in range(3)]
    m = jnp.maximum(jnp.maximum(lg[0], lg[1]), lg[2])
    e = [jnp.exp(l - m) for l in lg]
    den = e[0] + e[1] + e[2]
    ob = (e[0] * og[0] + e[1] * og[1] + e[2] * og[2]) / den
    return x + mm(jnp.concatenate([oa, ob], axis=1), w)


def norm_shift_fwd(x, g, tm=256):
    S = x.shape[0]

    def body(x_ref, xp_ref, g_ref, h_ref, hs_ref):
        h = rms(x_ref[...], g_ref[...])
        hp = rms(xp_ref[7:8, :], g_ref[...])
        hp = jnp.where(pl.program_id(0) == 0, 0.0, hp)
        row = lax.broadcasted_iota(jnp.int32, (tm, D), 0)
        h_ref[...] = h
        hs_ref[...] = jnp.where(row == 0, hp, pltpu.roll(h, 1, 0))

    return pl.pallas_call(
        body, name="rw_norm_shift", grid=(S // tm,),
        in_specs=[pl.BlockSpec((tm, D), lambda t: (t, 0)),
                  pl.BlockSpec((8, D), lambda t: (jnp.maximum(t * (tm // 8) - 1, 0), 0)),
                  pl.BlockSpec((1, D), lambda t: (0, 0))],
        out_specs=[pl.BlockSpec((tm, D), lambda t: (t, 0))] * 2,
        out_shape=[jax.ShapeDtypeStruct((S, D), F32)] * 2,
        compiler_params=_cp(("parallel",)),
    )(x, x, g)


def norm_shift_bwd(x, g, dh, dhs, dres, tm=256):
    S = x.shape[0]
    nt = S // tm

    def body(x_ref, g_ref, dh_ref, dhs_ref, dhn_ref, dr_ref, dx_ref, dg_ref):
        t = pl.program_id(0)
        nxt = jnp.where(t == nt - 1, 0.0, dhn_ref[0:1, :])
        row = lax.broadcasted_iota(jnp.int32, (tm, D), 0)
        tot = dh_ref[...] + jnp.where(row == tm - 1, nxt, pltpu.roll(dhs_ref[...], tm - 1, 0))
        _, vjp = jax.vjp(rms, x_ref[...], g_ref[...])
        dx, dg = vjp(tot)
        dx_ref[...] = dr_ref[...] + dx

        @pl.when(t == 0)
        def _():
            dg_ref[...] = dg

        @pl.when(t != 0)
        def _():
            dg_ref[...] += dg

    tile = pl.BlockSpec((tm, D), lambda t: (t, 0))
    return pl.pallas_call(
        body, name="rw_norm_shift_bwd", grid=(nt,),
        in_specs=[tile, pl.BlockSpec((1, D), lambda t: (0, 0)), tile, tile,
                  pl.BlockSpec((8, D), lambda t: (jnp.minimum((t + 1) * (tm // 8), S // 8 - 1), 0)), tile],
        out_specs=[tile, pl.BlockSpec((1, D), lambda t: (0, 0))],
        out_shape=[jax.ShapeDtypeStruct((S, D), F32), jax.ShapeDtypeStruct((1, D), F32)],
        compiler_params=_cp(("arbitrary",)),
    )(x, g, dh, dhs, dhs, dres)


def f_rw_proj(h, hs, mix, w):
    return mm(h + (hs - h) * mix, w)


def f_rw_mid(h, hs, r, k, v, mix3, w0, a0, kkw, kaw, w1, w2, a1, a2, g1, g2):
    xx = hs - h
    xw, xa, xg = h + xx * mix3[0:1], h + xx * mix3[1:2], h + xx * mix3[2:3]
    w_log = -softplus(-(w0 + mm(jnp.tanh(mm(xw, w1)), w2))) - 0.5
    lw = -jnp.exp(w_log)
    ag = jax.nn.sigmoid(a0 + mm(mm(xa, a1), a2))
    gate = mm(jax.nn.sigmoid(mm(xg, g1)), g2)
    kk = k * kkw
    kk = kk / jnp.maximum(jnp.sqrt(group_sum(kk * kk, RW_H)), 1e-12)
    kmod = k * (1.0 + (ag - 1.0) * kaw)
    return (to_heads(r), to_heads(lw), to_heads(kmod), to_heads(v), to_heads(-kk), to_heads(kk * ag), gate)


def f_rw_post(yh, rh, kh, vh, gate, x, lng, lnb, rk, wo):
    mu = jnp.mean(yh, axis=-1, keepdims=True)
    var = jnp.mean(jnp.square(yh - mu), axis=-1, keepdims=True)
    yn = (yh - mu) * lax.rsqrt(var + GN_EPS)
    bonus = jnp.sum(rh * kh * rk, axis=-1, keepdims=True) * vh
    y = from_heads(yn) * lng + lnb + from_heads(bonus)
    return x + mm(y * gate, wo)


def _split2(x):
    hi = x.astype(BF16)
    return hi, (x - hi.astype(F32)).astype(BF16)


def _b3(x, y, cx, cy):
    dn = (((cx,), (cy,)), ((0,), (0,)))
    xh, xl = _split2(x)
    yh, yl = _split2(y)
    d = lambda p, q: lax.dot_general(p, q, dn, preferred_element_type=F32)
    return d(xh, yh) + (d(xh, yl) + d(xl, yh))


@jax.custom_vjp
def b_nt(x, y):
    return _b3(x, y, 2, 2)


@jax.custom_vjp
def b_nn(x, y):
    return _b3(x, y, 2, 1)


@jax.custom_vjp
def b_tn(x, y):
    return _b3(x, y, 1, 1)


b_nt.defvjp(lambda x, y: (b_nt(x, y), (x, y)), lambda r, g: (b_nn(g, r[1]), b_tn(g, r[0])))
b_nn.defvjp(lambda x, y: (b_nn(x, y), (x, y)), lambda r, g: (b_nt(g, r[1]), b_tn(r[0], g)))
b_tn.defvjp(lambda x, y: (b_tn(x, y), (x, y)), lambda r, g: (b_nt(r[1], g), b_nn(r[0], g)))


def _tri_apply(x, lower):
    H, C, _ = x.shape
    ii = lax.broadcasted_iota(jnp.int32, (C, C), 0)
    jj = lax.broadcasted_iota(jnp.int32, (C, C), 1)
    m = jnp.broadcast_to(((jj <= ii) if lower else (jj >= ii)).astype(BF16), (H, C, C))
    x1 = x.astype(BF16)
    r1 = x - x1.astype(F32)
    x2 = r1.astype(BF16)
    x3 = (r1 - x2.astype(F32)).astype(BF16)
    d = lambda q: lax.dot_general(m, q, (((2,), (1,)), ((0,), (0,))), preferred_element_type=F32)
    return d(x1) + (d(x2) + d(x3))


@jax.custom_vjp
def run_sum(x):
    return _tri_apply(x, True)


run_sum.defvjp(lambda x: (run_sum(x), None), lambda _, g: (_tri_apply(g, False),))


def rwkv_chunk(S0, r, lw, k, v, a, b):
    H, C, _ = r.shape
    V = S0.shape[1]
    ii = lax.broadcasted_iota(jnp.int32, (C, C), 0)
    jj = lax.broadcasted_iota(jnp.int32, (C, C), 1)
    strict = jj < ii
    i2 = lax.broadcasted_iota(jnp.int32, (C, 2 * C), 0)
    j2 = lax.broadcasted_iota(jnp.int32, (C, 2 * C), 1)
    incl2 = jnp.where(j2 >= C, j2 - C, j2) <= i2
    g = run_sum(lw)
    ig = jnp.exp(-g)
    ar = jnp.concatenate([a * jnp.exp(g - lw), r * jnp.exp(g)], axis=1)
    bk = jnp.concatenate([b * ig, k * ig], axis=1)
    m = b_nt(ar, bk)
    a_ab = jnp.where(strict, m[:, :C, :C], 0.0)
    a_ak = jnp.where(strict, m[:, :C, C:], 0.0)
    b_r = jnp.where(incl2, m[:, C:, :], 0.0)
    p = b_nt(ar, S0)
    u = p[:, :C] + b_nn(a_ak, v)
    nmat, n = a_ab, 1
    while n < C:
        n *= 2
        if n < C:
            z = b_nn(nmat, jnp.concatenate([u, nmat], axis=2))
            u, nmat = u + z[:, :, :V], z[:, :, V:]
        else:
            u = u + b_nn(nmat, u)
    uv = jnp.concatenate([u, v], axis=1)
    y = p[:, C:] + b_nn(b_r, uv)
    g_end = g[:, C - 1:C, :]
    dec = jnp.exp(g_end - g)
    s_new = S0 * jnp.exp(g_end) + b_tn(uv, jnp.concatenate([b * dec, k * dec], axis=1))
    return y, s_new


def rwkv_fwd(r, lw, k, v, a, b):
    H, S, _ = r.shape
    C = RW_CHUNK

    def body(r_ref, lw_ref, k_ref, v_ref, a_ref, b_ref, y_ref, s_ref, s_scr):
        @pl.when(pl.program_id(0) == 0)
        def _():
            s_scr[...] = jnp.zeros_like(s_scr)

        s0 = s_scr[...]
        s_ref[0] = s0
        y, s1 = rwkv_chunk(s0, r_ref[...], lw_ref[...], k_ref[...], v_ref[...], a_ref[...], b_ref[...])
        y_ref[...] = y
        s_scr[...] = s1

    bs = pl.BlockSpec((H, C, HEAD), lambda c: (0, c, 0))
    return pl.pallas_call(
        body, name="rwkv_fwd", grid=(S // C,), in_specs=[bs] * 6,
        out_specs=[bs, pl.BlockSpec((1, H, HEAD, HEAD), lambda c: (c, 0, 0, 0))],
        out_shape=[jax.ShapeDtypeStruct((H, S, HEAD), F32), jax.ShapeDtypeStruct((S // C, H, HEAD, HEAD), F32)],
        scratch_shapes=[pltpu.VMEM((H, HEAD, HEAD), F32)],
        compiler_params=_cp(("arbitrary",)),
    )(r, lw, k, v, a, b)


def rwkv_bwd(r, lw, k, v, a, b, states, dy):
    H, S, _ = r.shape
    C = RW_CHUNK
    nc = S // C

    def body(r_ref, lw_ref, k_ref, v_ref, a_ref, b_ref, s_ref, dy_ref, dr, dlw, dk, dv, da, db, ds_scr):
        @pl.when(pl.program_id(0) == 0)
        def _():
            ds_scr[...] = jnp.zeros_like(ds_scr)

        _, vjp = jax.vjp(rwkv_chunk, s_ref[0], r_ref[...], lw_ref[...], k_ref[...], v_ref[...], a_ref[...], b_ref[...])
        grads = vjp((dy_ref[...], ds_scr[...]))
        ds_scr[...] = grads[0]
        for o, gv in zip((dr, dlw, dk, dv, da, db), grads[1:]):
            o[...] = gv

    bs = pl.BlockSpec((H, C, HEAD), lambda c: (0, nc - 1 - c, 0))
    return pl.pallas_call(
        body, name="rwkv_bwd", grid=(nc,),
        in_specs=[bs] * 6 + [pl.BlockSpec((1, H, HEAD, HEAD), lambda c: (nc - 1 - c, 0, 0, 0)), bs],
        out_specs=[bs] * 6, out_shape=[jax.ShapeDtypeStruct((H, S, HEAD), F32)] * 6,
        scratch_shapes=[pltpu.VMEM((H, HEAD, HEAD), F32)],
        compiler_params=_cp(("arbitrary",)),
    )(r, lw, k, v, a, b, states, dy)


def loss_head(y, target, tm=512):
    S = y.shape[0]

    def body(y_ref, t_ref, dy_ref, l_ref):
        e = y_ref[...] - t_ref[...]
        dy_ref[...] = e * (1.0 / D)
        part = jnp.broadcast_to(0.5 * jnp.sum(jnp.mean(e * e, axis=-1, keepdims=True)), (1, 128))

        @pl.when(pl.program_id(0) == 0)
        def _():
            l_ref[...] = part

        @pl.when(pl.program_id(0) != 0)
        def _():
            l_ref[...] += part

    tile = pl.BlockSpec((tm, D), lambda t: (t, 0))
    return pl.pallas_call(
        body, name="loss_head", grid=(S // tm,), in_specs=[tile, tile],
        out_specs=[tile, pl.BlockSpec((1, 128), lambda t: (0, 0))],
        out_shape=[jax.ShapeDtypeStruct((S, D), F32), jax.ShapeDtypeStruct((1, 128), F32)],
        compiler_params=_cp(("arbitrary",)),
    )(y, target)


def _row_tile(rows, cols, budget=1 << 19):
    best = None
    for tr in range(8, rows + 1, 8):
        if rows % tr == 0 and tr * cols <= budget:
            best = tr
    return best or rows


def _adam(w, g, m, v):
    m = ADAM_B1 * m + (1.0 - ADAM_B1) * g
    v = ADAM_B2 * v + (1.0 - ADAM_B2) * jnp.square(g)
    m_hat = m / (1.0 - ADAM_B1 ** ADAM_STEP)
    v_hat = v / (1.0 - ADAM_B2 ** ADAM_STEP)
    return -ADAM_LR * (m_hat / (jnp.sqrt(v_hat) + ADAM_EPS) + ADAM_WD * w), m, v


def sum_slots(name, parts):
    n, R, C = parts.shape
    tr = _row_tile(R, C * n)

    def body(p_ref, o_ref):
        s = p_ref[0]
        for i in range(1, n):
            s = s + p_ref[i]
        o_ref[...] = s

    return pl.pallas_call(
        body, name=name, grid=(R // tr,),
        in_specs=[pl.BlockSpec((n, tr, C), lambda t: (0, t, 0))], out_specs=pl.BlockSpec((tr, C), lambda t: (t, 0)),
        out_shape=jax.ShapeDtypeStruct((R, C), F32), compiler_params=_cp(("parallel",)),
    )(parts)


def adam_step(name, ga, gb, w, m, v):
    R, C = w.shape
    tr = _row_tile(R, C, 1 << 17)
    ins = [ga] + ([gb] if gb is not None else []) + [w, m, v]

    def body(*refs):
        g = refs[0][...]
        if gb is not None:
            g = g + refs[1][...]
        w_ref, m_ref, v_ref, g_out, d_out, m_out, v_out = refs[len(ins) - 3:]
        d, m2, v2 = _adam(w_ref[...], g, m_ref[...], v_ref[...])
        g_out[...] = g
        d_out[...] = d
        m_out[...] = m2
        v_out[...] = v2

    tile = pl.BlockSpec((tr, C), lambda t: (t, 0))
    return pl.pallas_call(
        body, name=name, grid=(R // tr,), in_specs=[tile] * len(ins), out_specs=[tile] * 4,
        out_shape=[jax.ShapeDtypeStruct((R, C), F32)] * 4, compiler_params=_cp(("parallel",)),
    )(*ins)


def _place():
    return lax.axis_index("x"), lax.axis_index("y"), lax.axis_index("c")


def _flip(me, mask):
    return tuple(1 - v if mk else v for v, mk in zip(me, mask))


CHIP_MASKS = ((1, 0, 0), (0, 1, 0), (1, 1, 0))
ALL_MASKS = tuple((a, b, c) for a in (0, 1) for b in (0, 1) for c in (0, 1) if (a, b, c) != (0, 0, 0))


def _chip(dev):
    return 2 * dev[0] + dev[1]


def _devno(dev):
    return 4 * dev[0] + 2 * dev[1] + dev[2]


def exchange(name, arrays, out_shapes, masks, src_of, dst_of, local_of):
    n, npeer = len(arrays), len(masks)

    def body(*refs):
        ins, outs = refs[:n], refs[n:2 * n]
        send_sems, recv_sems, local_sems = refs[2 * n:]
        me = _place()
        peers = [_flip(me, mk) for mk in masks]
        locals_ = []
        for i in range(n):
            lc = local_of(ins[i], outs[i], me)
            if lc is not None:
                cp = pltpu.make_async_copy(lc[0], lc[1], local_sems.at[i])
                cp.start()
                locals_.append(cp)
        sends = []
        for i in range(n):
            for j, peer in enumerate(peers):
                cp = pltpu.make_async_remote_copy(
                    src_ref=src_of(ins[i], me, peer), dst_ref=dst_of(outs[i], me, j),
                    send_sem=send_sems.at[i * npeer + j], recv_sem=recv_sems.at[i * npeer + j],
                    device_id=peer, device_id_type=MESH)
                cp.start()
                sends.append(cp)
        for i in range(n):
            for j, peer in enumerate(peers):
                land = dst_of(outs[i], peer, j)
                pltpu.make_async_remote_copy(
                    src_ref=land, dst_ref=land, send_sem=send_sems.at[i * npeer + j],
                    recv_sem=recv_sems.at[i * npeer + j], device_id=peer, device_id_type=MESH).wait_recv()
        for cp in sends:
            cp.wait_send()
        for cp in locals_:
            cp.wait()

    hbm = pl.BlockSpec(memory_space=pl.ANY)
    return pl.pallas_call(
        body, name=name, in_specs=[hbm] * n, out_specs=[hbm] * n, out_shape=list(out_shapes),
        scratch_shapes=[pltpu.SemaphoreType.DMA((n * npeer,)), pltpu.SemaphoreType.DMA((n * npeer,)),
                        pltpu.SemaphoreType.DMA((n,))],
    )(*arrays)


def gather_chips(arrays):
    outs = [jax.ShapeDtypeStruct((N_CHIPS,) + a.shape, a.dtype) for a in arrays]
    return exchange("gather_weights", arrays, outs, CHIP_MASKS,
                    src_of=lambda r, me, peer: r,
                    dst_of=lambda o, sender, j: o.at[_chip(sender)],
                    local_of=lambda r, o, me: (r, o.at[_chip(me)]))


def scatter_chips(arrays):
    outs = [jax.ShapeDtypeStruct(a.shape, a.dtype) for a in arrays]
    return exchange("scatter_grads", arrays, outs, CHIP_MASKS,
                    src_of=lambda r, me, peer: r.at[_chip(peer)],
                    dst_of=lambda o, sender, j: o.at[j],
                    local_of=lambda r, o, me: (r.at[_chip(me)], o.at[3]))


def swap_cores(arrays):
    outs = [jax.ShapeDtypeStruct(a.shape, a.dtype) for a in arrays]
    return exchange("swap_cores", arrays, outs, ((0, 0, 1),),
                    src_of=lambda r, me, peer: r, dst_of=lambda o, sender, j: o, local_of=lambda r, o, me: None)


def gather_all(arrays):
    outs = [jax.ShapeDtypeStruct((8,) + a.shape, a.dtype) for a in arrays]
    return exchange("gather_replicated", arrays, outs, ALL_MASKS,
                    src_of=lambda r, me, peer: r,
                    dst_of=lambda o, sender, j: o.at[_devno(sender)],
                    local_of=lambda r, o, me: (r, o.at[_devno(me)]))


def _unshard_cols(g):
    return jnp.transpose(g, (1, 0, 2)).reshape(g.shape[1], -1)


def _shard_cols(a):
    return jnp.transpose(a.reshape(a.shape[0], N_CHIPS, -1), (1, 0, 2))


def _forward_backward(x, tgt, W):
    S = x.shape[0]
    G = {}
    sd = jax.ShapeDtypeStruct

    def ffn(xin, l, j):
        return ffn_fwd(xin, W["ffn_norm"][l][j], W["ffn_w_gate"], W["ffn_w_up"], W["ffn_w_down"], l, j)

    def ffn_back(xin, dout, l, j):
        gn = W["ffn_norm"][l][j]
        dh, dwg, dwu, dwd = ffn_bwd(xin, gn, W["ffn_w_gate"], W["ffn_w_up"], W["ffn_w_down"], dout, l, j)
        dx, dg = norm_bwd(f"ffn_norm_bwd_{l}{j}", xin, gn, dh, dout)
        G[("ffn", l, j)] = (dg, dwg, dwu, dwd)
        return dx

    x0 = x
    x1 = ffn(x0, 0, 0)
    g0 = W["mix_norm"][0]
    sbq, sbk, sbv = tile_fwd(f_attn_sb, "attn_in_sb", [x1], [g0, W["attn_w_in"][0]], [sd((S, SB_W), F32)] * 3, 256)
    dl_shape = sd((DL_PAIRS, S, 128), F32)
    qn, = tile_fwd(f_attn_qk, "attn_in_q", [x1], [g0, W["attn_w_in"][1], W["attn_q_norm"]], [dl_shape], 256)
    kn, = tile_fwd(f_attn_qk, "attn_in_k", [x1], [g0, W["attn_w_in"][2], W["attn_k_norm"]], [dl_shape], 256)
    vv, = tile_fwd(f_attn_v, "attn_in_v", [x1], [g0, W["attn_w_in"][3]], [dl_shape], 256)
    oa = sb_fwd(sbq, sbk, sbv)
    qs, ks, vs = (reorder(nm, t, DIL, False) for nm, t in (("sub_q", qn), ("sub_k", kn), ("sub_v", vv)))
    o_s, lse_s = dil_fwd(qs, ks, vs, W["bias_mat"])
    o_n, lse_n = reorder("nat_o", o_s, DIL, True), reorder("nat_lse", lse_s, DIL, True)
    x2, = tile_fwd(f_attn_out, "attn_out", [x1, oa, o_n, lse_n], [W["attn_w_out"]], [sd((S, D), F32)], 256)
    x3 = ffn(x2, 0, 1)
    x4 = ffn(x3, 1, 0)
    g1 = W["mix_norm"][1]
    h, hs = norm_shift_fwd(x4, g1)
    mix = W["rw_mix"]
    r, = tile_fwd(f_rw_proj, "rw_proj_r", [h, hs], [mix[0:1], W["rw_wr"]], [sd((S, D), F32)], 256)
    k, = tile_fwd(f_rw_proj, "rw_proj_k", [h, hs], [mix[2:3], W["rw_wk"]], [sd((S, D), F32)], 256)
    v, = tile_fwd(f_rw_proj, "rw_proj_v", [h, hs], [mix[3:4], W["rw_wv"]], [sd((S, D), F32)], 256)
    mix3 = jnp.concatenate([mix[1:2], mix[4:5], mix[5:6]], axis=0)
    mid_w = [mix3, W["rw_w0"], W["rw_a0"], W["rw_kk"], W["rw_ka"], W["rw_w1"], W["rw_w2"], W["rw_a1"], W["rw_a2"],
             W["rw_g1"], W["rw_g2"]]
    hshape = sd((RW_H, S, HEAD), F32)
    mid_tiles = [h, hs, r, k, v]
    rh, lwh, kh, vh, ah, bh, gate = tile_fwd(f_rw_mid, "rw_mid", mid_tiles, mid_w, [hshape] * 6 + [sd((S, D), F32)], 128)
    yh, states = rwkv_fwd(rh, lwh, kh, vh, ah, bh)
    post_w = [W["rw_lnx_g"], W["rw_lnx_b"], W["rw_rk"], W["rw_wo"]]
    post_tiles = [yh, rh, kh, vh, gate, x4]
    x5, = tile_fwd(f_rw_post, "rw_post", post_tiles, post_w, [sd((S, D), F32)], 128)
    x6 = ffn(x5, 1, 1)
    dx6, loss_part = loss_head(x6, tgt)

    dx5 = ffn_back(x5, dx6, 1, 1)
    (dyh, drh, dkh, dvh, dgate, dx4), (d_lng, d_lnb, d_rk, d_wo) = tile_bwd(
        f_rw_post, "rw_post_bwd", post_tiles, post_w, [dx5], 128, [True] * 6, [True] * 4)
    drh2, dlwh, dkh2, dvh2, dah, dbh = rwkv_bwd(rh, lwh, kh, vh, ah, bh, states, dyh)
    mid_cts = [drh + drh2, dlwh, dkh + dkh2, dvh + dvh2, dah, dbh, dgate]
    (dh, dhs, dr, dk, dv), dmid_w = tile_bwd(f_rw_mid, "rw_mid_bwd", mid_tiles, mid_w, mid_cts, 128,
                                             [True] * 5, [True] * len(mid_w))
    dmix = {}
    for nm, ct, row, wname in (("r", dr, 0, "rw_wr"), ("k", dk, 2, "rw_wk"), ("v", dv, 3, "rw_wv")):
        (dh, dhs), (dmix[row], G[wname]) = tile_bwd(
            f_rw_proj, f"rw_proj_{nm}_bwd", [h, hs], [mix[row:row + 1], W[wname]], [ct], 256,
            [True, True], [True, True], acc={0: dh, 1: dhs})
    dx4, G[("mix_norm", 1)] = norm_shift_bwd(x4, g1, dh, dhs, dx4)
    dmix3 = dmid_w[0]
    G["rw_mix"] = jnp.concatenate([dmix[0], dmix3[0:1], dmix[2], dmix[3], dmix3[1:2], dmix3[2:3]], axis=0)
    for nm, gv in zip(("rw_w0", "rw_a0", "rw_kk", "rw_ka", "rw_w1", "rw_w2", "rw_a1", "rw_a2", "rw_g1", "rw_g2"), dmid_w[1:]):
        G[nm] = gv
    G["rw_lnx_g"], G["rw_lnx_b"], G["rw_rk"], G["rw_wo"] = d_lng, d_lnb, d_rk, d_wo
    dx3 = ffn_back(x3, dx4, 1, 0)
    dx2 = ffn_back(x2, dx3, 0, 1)
    (dx1, doa, do_n, dlse_n), (G["attn_w_out"],) = tile_bwd(
        f_attn_out, "attn_out_bwd", [x1, oa, o_n, lse_n], [W["attn_w_out"]], [dx2], 256, [True] * 4, [True])
    do_s, dlse_s = reorder("sub_do", do_n, DIL, False), reorder("sub_dlse", dlse_n, DIL, False)
    dqs, dks, dvs, dsum = dil_bwd(qs, ks, vs, W["bias_mat"], o_s, lse_s, do_s, dlse_s)
    G["rel_bias"] = bias_grad(dsum, W["buckets"])
    dqn, dkn, dvv = (reorder(nm, t, DIL, True) for nm, t in (("nat_dq", dqs), ("nat_dk", dks), ("nat_dv", dvs)))
    dsbq, dsbk, dsbv = sb_bwd(sbq, sbk, sbv, doa)
    dg0 = []
    dwin = []
    (dx1,), (dg, dw) = tile_bwd(f_attn_sb, "attn_in_sb_bwd", [x1], [g0, W["attn_w_in"][0]], [dsbq, dsbk, dsbv], 256,
                                [True], [True, True], acc={0: dx1})
    dg0.append(dg), dwin.append(dw)
    (dx1,), (dg, dw, G["attn_q_norm"]) = tile_bwd(f_attn_qk, "attn_in_q_bwd", [x1], [g0, W["attn_w_in"][1], W["attn_q_norm"]],
                                                  [dqn], 256, [True], [True] * 3, acc={0: dx1})
    dg0.append(dg), dwin.append(dw)
    (dx1,), (dg, dw, G["attn_k_norm"]) = tile_bwd(f_attn_qk, "attn_in_k_bwd", [x1], [g0, W["attn_w_in"][2], W["attn_k_norm"]],
                                                  [dkn], 256, [True], [True] * 3, acc={0: dx1})
    dg0.append(dg), dwin.append(dw)
    (dx1,), (dg, dw) = tile_bwd(f_attn_v, "attn_in_v_bwd", [x1], [g0, W["attn_w_in"][3]], [dvv], 256,
                                [True], [True, True], acc={0: dx1})
    dg0.append(dg), dwin.append(dw)
    G[("mix_norm", 0)] = dg0
    G["attn_w_in"] = dwin
    dx0 = ffn_back(x0, dx1, 0, 0)
    return loss_part, dx0, G


VEC_ROWS = ("ffn_norm", "rw_mix", "rw_w0", "rw_a0", "rw_kk", "rw_ka", "rw_lnx_g", "rw_lnx_b")


def kernel(x, ffn_norm, ffn_w_gate, ffn_w_up, ffn_w_down, mix_norm, rel_bias, attn_w_in, attn_q_norm, attn_k_norm, attn_w_out, rw_mix, rw_w0, rw_w1, rw_w2, rw_a0, rw_a1, rw_a2, rw_g1, rw_g2, rw_kk, rw_ka, rw_rk, rw_wr, rw_wk, rw_wv, rw_wo, rw_lnx_g, rw_lnx_b, loss_target, m_ffn_norm, m_ffn_w_gate, m_ffn_w_up, m_ffn_w_down, m_mix_norm, m_rel_bias, m_attn_w_in, m_attn_q_norm, m_attn_k_norm, m_attn_w_out, m_rw_mix, m_rw_w0, m_rw_w1, m_rw_w2, m_rw_a0, m_rw_a1, m_rw_a2, m_rw_g1, m_rw_g2, m_rw_kk, m_rw_ka, m_rw_rk, m_rw_wr, m_rw_wk, m_rw_wv, m_rw_wo, m_rw_lnx_g, m_rw_lnx_b, v_ffn_norm, v_ffn_w_gate, v_ffn_w_up, v_ffn_w_down, v_mix_norm, v_rel_bias, v_attn_w_in, v_attn_q_norm, v_attn_k_norm, v_attn_w_out, v_rw_mix, v_rw_w0, v_rw_w1, v_rw_w2, v_rw_a0, v_rw_a1, v_rw_a2, v_rw_g1, v_rw_g2, v_rw_kk, v_rw_ka, v_rw_rk, v_rw_wr, v_rw_wk, v_rw_wv, v_rw_wo, v_rw_lnx_g, v_rw_lnx_b):
    names = ["ffn_norm", "ffn_w_gate", "ffn_w_up", "ffn_w_down", "mix_norm", "rel_bias", "attn_w_in", "attn_q_norm",
             "attn_k_norm", "attn_w_out", "rw_mix", "rw_w0", "rw_w1", "rw_w2", "rw_a0", "rw_a1", "rw_a2", "rw_g1", "rw_g2",
             "rw_kk", "rw_ka", "rw_rk", "rw_wr", "rw_wk", "rw_wv", "rw_wo", "rw_lnx_g", "rw_lnx_b"]
    loc = locals()
    w = {n: loc[n] for n in names}
    mom = {n: loc["m_" + n] for n in names}
    vel = {n: loc["v_" + n] for n in names}
    S = x.shape[1]

    vec_shard = jnp.concatenate([w[n].reshape(-1, 256) for n in VEC_ROWS], axis=0)
    mats = ["ffn_w_gate", "ffn_w_up", "ffn_w_down", "attn_w_in", "attn_w_out", "rw_w1", "rw_w2", "rw_a1", "rw_a2",
            "rw_g1", "rw_g2", "rw_wr", "rw_wk", "rw_wv", "rw_wo"]
    send = [vec_shard] + [(w[n] if w[n].shape[0] != 1 else w[n][0]).astype(BF16) for n in mats]
    got = gather_chips(send)
    vec_full = _unshard_cols(got[0])
    gm = dict(zip(mats, got[1:]))
    W = {
        "ffn_norm": [[vec_full[2 * l + j][None] for j in range(2)] for l in range(2)],
        "ffn_w_gate": gm["ffn_w_gate"], "ffn_w_up": gm["ffn_w_up"], "ffn_w_down": gm["ffn_w_down"],
        "mix_norm": [mix_norm[0:1], mix_norm[1:2]],
        "attn_w_in": [gm["attn_w_in"][p] for p in range(N_CHIPS)],
        "attn_q_norm": attn_q_norm, "attn_k_norm": attn_k_norm,
        "attn_w_out": _unshard_cols(gm["attn_w_out"]),
        "rw_mix": vec_full[4:10],
        "rw_w1": gm["rw_w1"].reshape(D, -1), "rw_a1": gm["rw_a1"].reshape(D, -1), "rw_g1": gm["rw_g1"].reshape(D, -1),
        "rw_w2": _unshard_cols(gm["rw_w2"]), "rw_a2": _unshard_cols(gm["rw_a2"]), "rw_g2": _unshard_cols(gm["rw_g2"]),
        "rw_wr": gm["rw_wr"].reshape(D, D), "rw_wk": gm["rw_wk"].reshape(D, D), "rw_wv": gm["rw_wv"].reshape(D, D),
        "rw_wo": gm["rw_wo"].reshape(D, D),
        "rw_rk": rw_rk[0][:, None, :],
    }
    for i, n in enumerate(("rw_w0", "rw_a0", "rw_kk", "rw_ka", "rw_lnx_g", "rw_lnx_b")):
        W[n] = vec_full[10 + i][None]
    buckets = _bucket_maps()
    W["buckets"] = buckets
    W["bias_mat"] = bias_table(rel_bias, buckets)

    loss_part, dx, G = _forward_backward(x[0], loss_target[0], W)
    loss = lax.psum(loss_part[0, 0], ("x", "y", "c"))

    def ffn_stack(idx):
        return jnp.stack([jnp.stack([G[("ffn", l, j)][idx] for j in range(2)], axis=1) for l in range(2)], axis=1)

    vec_rows = [G[("ffn", l, j)][0] for l in range(2) for j in range(2)] + [G["rw_mix"]] + \
               [G[n] for n in ("rw_w0", "rw_a0", "rw_kk", "rw_ka", "rw_lnx_g", "rw_lnx_b")]
    full = {
        "vec": _shard_cols(jnp.concatenate(vec_rows, axis=0)),
        "ffn_w_gate": ffn_stack(1), "ffn_w_up": ffn_stack(2), "ffn_w_down": ffn_stack(3),
        "attn_w_in": jnp.stack(G["attn_w_in"]),
        "attn_w_out": _shard_cols(G["attn_w_out"]),
        "rw_w1": G["rw_w1"].reshape(N_CHIPS, 256, -1), "rw_a1": G["rw_a1"].reshape(N_CHIPS, 256, -1),
        "rw_g1": G["rw_g1"].reshape(N_CHIPS, 256, -1),
        "rw_w2": _shard_cols(G["rw_w2"]), "rw_a2": _shard_cols(G["rw_a2"]), "rw_g2": _shard_cols(G["rw_g2"]),
        "rw_wr": G["rw_wr"].reshape(N_CHIPS, 256, D), "rw_wk": G["rw_wk"].reshape(N_CHIPS, 256, D),
        "rw_wv": G["rw_wv"].reshape(N_CHIPS, 256, D), "rw_wo": G["rw_wo"].reshape(N_CHIPS, 256, D),
    }
    order = ["vec"] + mats
    landed = scatter_chips([full[n] for n in order])
    mine = [sum_slots(f"sum_{n}", p.reshape(N_CHIPS, -1, p.shape[-1])) for n, p in zip(order, landed)]
    theirs = swap_cores(mine)

    rep = jnp.concatenate([G[("mix_norm", 0)][0] + G[("mix_norm", 0)][1] + G[("mix_norm", 0)][2] + G[("mix_norm", 0)][3],
                           G[("mix_norm", 1)]], axis=0).reshape(16, 128)
    rep = jnp.concatenate([rep, G["rel_bias"], jnp.pad(G["attn_q_norm"], ((0, 0), (0, 64))),
                           jnp.pad(G["attn_k_norm"], ((0, 0), (0, 64))), G["rw_rk"].reshape(8, 128),
                           jnp.zeros((2, 128), F32)], axis=0)
    rep_sum = sum_slots("sum_replicated", gather_all([rep])[0])
    g_rep = {
        "mix_norm": rep_sum[0:16].reshape(2, D),
        "rel_bias": jnp.transpose(rep_sum[16:28, :N_BUCKETS]),
        "attn_q_norm": rep_sum[28:29, :HEAD], "attn_k_norm": rep_sum[29:30, :HEAD],
        "rw_rk": rep_sum[30:38].reshape(1, RW_H, HEAD),
    }

    out = {}

    def adam(n, ga, gb):
        shp = w[n].shape
        to2 = lambda a: a.reshape(-1, shp[-1])
        res = adam_step(f"adam_{n}", to2(ga), None if gb is None else to2(gb), to2(w[n]), to2(mom[n]), to2(vel[n]))
        out[n] = tuple(r.reshape(shp) for r in res)

    part = dict(zip(order, zip(mine, theirs)))
    for n in mats:
        adam(n, *part[n])
    va, vb = part["vec"]
    rows = {"ffn_norm": (0, 4), "rw_mix": (4, 10), "rw_w0": (10, 11), "rw_a0": (11, 12), "rw_kk": (12, 13),
            "rw_ka": (13, 14), "rw_lnx_g": (14, 15), "rw_lnx_b": (15, 16)}
    for n, (lo, hi) in rows.items():
        adam(n, va[lo:hi], vb[lo:hi])
    for n, gv in g_rep.items():
        adam(n, gv, None)

    grads = [out[n][0] for n in names]
    deltas = [out[n][1] for n in names]
    new_m = [out[n][2] for n in names]
    new_v = [out[n][3] for n in names]
    return (loss, dx[None], *grads, *deltas, *new_m, *new_v)
```

```python
import functools
import math

import jax
import jax.numpy as jnp
from jax import lax
from jax.experimental import pallas as pl
from jax.experimental.pallas import tpu as pltpu

F32, BF16 = jnp.float32, jnp.bfloat16
HI = lax.Precision.HIGHEST
MESH = pl.DeviceIdType.MESH

D = 1024
HEAD = 64
N_CHIPS = 4
FF_SHARD = 704
SB_W = 256
DL_HEADS = 12
DL_PAIRS = 6
DIL = (1, 4, 16)
QBLK = 128
N_BUCKETS = 32
MAX_DISTANCE = 2048
RW_H = 16
RW_CHUNK = 64
NORM_EPS = 1e-6
GN_EPS = 64e-5
NEG_INF = -1e30
VMEM_LIMIT = 56 * 1024 * 1024

ADAM_LR, ADAM_B1, ADAM_B2, ADAM_EPS, ADAM_WD, ADAM_STEP = 0.001, 0.9, 0.999, 1e-08, 0.01, 10


def _cp(sem):
    return pltpu.CompilerParams(dimension_semantics=sem, vmem_limit_bytes=VMEM_LIMIT)


def _dg(a, b, dims, prec=None):
    return lax.dot_general(a, b, (dims, ((), ())), precision=prec, preferred_element_type=F32)


def _bdot(a, b, dims):
    return _dg(a.astype(BF16), b.astype(BF16), dims)


@jax.custom_vjp
def mm(a, b):
    return _bdot(a, b, ((1,), (0,)))


def _mm_fwd(a, b):
    return _bdot(a, b, ((1,), (0,))), (a, b)


def _mm_bwd(res, g):
    a, b = res
    return _bdot(g, b, ((1,), (1,))), _bdot(a, g, ((0,), (0,)))


mm.defvjp(_mm_fwd, _mm_bwd)


def rms(x, g):
    return x * lax.rsqrt(jnp.mean(x * x, axis=-1, keepdims=True) + NORM_EPS) * g


def group_sum(x, nh):
    w = x.shape[-1]
    e = (lax.broadcasted_iota(jnp.int32, (w, nh), 0) // HEAD == lax.broadcasted_iota(jnp.int32, (w, nh), 1)).astype(F32)
    s = _dg(x, e, ((1,), (0,)), HI)
    return _dg(s, e, ((1,), (1,)), HI)


def softplus(u):
    return jnp.maximum(u, 0.0) + jnp.log1p(jnp.exp(-jnp.abs(u)))


def to_heads(t, nh=RW_H):
    return jnp.stack([t[:, HEAD * h:HEAD * (h + 1)] for h in range(nh)])


def from_heads(t):
    return jnp.concatenate([t[h] for h in range(t.shape[0])], axis=-1)


def _tile_spec(shape, tm):
    if len(shape) == 2:
        return pl.BlockSpec((tm, shape[1]), lambda t: (t, 0))
    return pl.BlockSpec((shape[0], tm, shape[2]), lambda t: (0, t, 0))


def _full_spec(shape):
    nd = len(shape)
    return pl.BlockSpec(tuple(shape), lambda t: (0,) * nd)


def _rows(a):
    return a.shape[0] if a.ndim == 2 else a.shape[1]


def tile_fwd(f, name, tiles, weights, outs, tm):
    nt, nw = len(tiles), len(weights)

    def body(*refs):
        tv = [r[...] for r in refs[:nt]]
        wv = [r[...].astype(F32) for r in refs[nt:nt + nw]]
        res = f(*tv, *wv)
        if not isinstance(res, (tuple, list)):
            res = (res,)
        for o, v in zip(refs[nt + nw:], res):
            o[...] = v.astype(o.dtype)

    return pl.pallas_call(
        body, name=name, grid=(_rows(tiles[0]) // tm,),
        in_specs=[_tile_spec(a.shape, tm) for a in tiles] + [_full_spec(w.shape) for w in weights],
        out_specs=[_tile_spec(o.shape, tm) for o in outs],
        out_shape=list(outs),
        compiler_params=_cp(("parallel",)),
    )(*tiles, *weights)


def tile_bwd(f, name, tiles, weights, cts, tm, dt, dw, acc=None):
    acc = acc or {}
    nt, nw, nc = len(tiles), len(weights), len(cts)
    acc_idx = sorted(acc)
    na = len(acc_idx)
    dti = [i for i in range(nt) if dt[i]]
    dwi = [i for i in range(nw) if dw[i]]

    def body(*refs):
        tv = [r[...] for r in refs[:nt]]
        wv = [r[...].astype(F32) for r in refs[nt:nt + nw]]
        cv = [r[...] for r in refs[nt + nw:nt + nw + nc]]
        av = {i: r[...] for i, r in zip(acc_idx, refs[nt + nw + nc:nt + nw + nc + na])}
        orefs = refs[nt + nw + nc + na:]

        def g(*diff):
            t2, w2 = list(tv), list(wv)
            for i, v in zip(dti, diff[:len(dti)]):
                t2[i] = v
            for i, v in zip(dwi, diff[len(dti):]):
                w2[i] = v
            res = f(*t2, *w2)
            return tuple(res) if isinstance(res, (tuple, list)) else (res,)

        _, vjp = jax.vjp(g, *[tv[i] for i in dti], *[wv[i] for i in dwi])
        grads = vjp(tuple(cv))
        for k, i in enumerate(dti):
            gt = grads[k]
            if i in av:
                gt = gt + av[i]
            orefs[k][...] = gt
        first = pl.program_id(0) == 0
        for k, i in enumerate(dwi):
            o = orefs[len(dti) + k]
            gw = grads[len(dti) + k]

            @pl.when(first)
            def _(o=o, gw=gw):
                o[...] = gw

            @pl.when(jnp.logical_not(first))
            def _(o=o, gw=gw):
                o[...] += gw

    out_shape = [jax.ShapeDtypeStruct(tiles[i].shape, F32) for i in dti] + \
                [jax.ShapeDtypeStruct(weights[i].shape, F32) for i in dwi]
    res = pl.pallas_call(
        body, name=name, grid=(_rows(tiles[0]) // tm,),
        in_specs=[_tile_spec(a.shape, tm) for a in tiles] + [_full_spec(w.shape) for w in weights] +
                 [_tile_spec(c.shape, tm) for c in cts] + [_tile_spec(tiles[i].shape, tm) for i in acc_idx],
        out_specs=[_tile_spec(tiles[i].shape, tm) for i in dti] + [_full_spec(weights[i].shape) for i in dwi],
        out_shape=out_shape,
        compiler_params=_cp(("arbitrary",)),
    )(*tiles, *weights, *cts, *[acc[i] for i in acc_idx])
    return list(res[:len(dti)]), list(res[len(dti):])


def _ffn_wspec(l, j, rows, cols, cfirst):
    if cfirst:
        return pl.BlockSpec((1, 1, 1, rows, cols), lambda c, t: (c, l, j, 0, 0))
    return pl.BlockSpec((1, 1, 1, rows, cols), lambda t, c: (c, l, j, 0, 0))


def ffn_fwd(x, g, wg, wu, wd, l, j, tm=512):
    S = x.shape[0]

    def body(x_ref, g_ref, wg_ref, wu_ref, wd_ref, o_ref, h_ref, acc_ref):
        c = pl.program_id(1)

        @pl.when(c == 0)
        def _():
            h_ref[...] = rms(x_ref[...], g_ref[...]).astype(BF16)
            acc_ref[...] = jnp.zeros_like(acc_ref)

        h = h_ref[...]
        a = _bdot(h, wg_ref[0, 0, 0], ((1,), (0,)))
        b = _bdot(h, wu_ref[0, 0, 0], ((1,), (0,)))
        y = a * jax.nn.sigmoid(a) * b
        acc_ref[...] += _bdot(y, wd_ref[0, 0, 0], ((1,), (0,)))

        @pl.when(c == N_CHIPS - 1)
        def _():
            o_ref[...] = x_ref[...] + 0.5 * acc_ref[...]

    return pl.pallas_call(
        body, name=f"ffn_fwd_{l}{j}", grid=(S // tm, N_CHIPS),
        in_specs=[pl.BlockSpec((tm, D), lambda t, c: (t, 0)), pl.BlockSpec((1, D), lambda t, c: (0, 0)),
                  _ffn_wspec(l, j, D, FF_SHARD, False), _ffn_wspec(l, j, D, FF_SHARD, False),
                  _ffn_wspec(l, j, FF_SHARD, D, False)],
        out_specs=pl.BlockSpec((tm, D), lambda t, c: (t, 0)),
        out_shape=jax.ShapeDtypeStruct((S, D), F32),
        scratch_shapes=[pltpu.VMEM((tm, D), BF16), pltpu.VMEM((tm, D), F32)],
        compiler_params=_cp(("parallel", "arbitrary")),
    )(x, g, wg, wu, wd)


def ffn_bwd(x, g, wg, wu, wd, dout, l, j, tm=256):
    S = x.shape[0]

    def body(x_ref, g_ref, wg_ref, wu_ref, wd_ref, do_ref, dh_ref, dwg_ref, dwu_ref, dwd_ref):
        t = pl.program_id(1)
        h = rms(x_ref[...], g_ref[...]).astype(BF16)
        wgv, wuv, wdv = wg_ref[0, 0, 0], wu_ref[0, 0, 0], wd_ref[0, 0, 0]
        a = _bdot(h, wgv, ((1,), (0,)))
        b = _bdot(h, wuv, ((1,), (0,)))
        sig = jax.nn.sigmoid(a)
        s = a * sig
        dyd = 0.5 * do_ref[...]
        dy = _bdot(dyd, wdv, ((1,), (1,)))
        dwd = _bdot(s * b, dyd, ((0,), (0,)))
        db = dy * s
        da = dy * b * (sig * (1.0 + a * (1.0 - sig)))
        dwg = _bdot(h, da, ((0,), (0,)))
        dwu = _bdot(h, db, ((0,), (0,)))
        dh_ref[0] = _bdot(da, wgv, ((1,), (1,))) + _bdot(db, wuv, ((1,), (1,)))

        @pl.when(t == 0)
        def _():
            dwg_ref[0] = dwg
            dwu_ref[0] = dwu
            dwd_ref[0] = dwd

        @pl.when(t != 0)
        def _():
            dwg_ref[0] += dwg
            dwu_ref[0] += dwu
            dwd_ref[0] += dwd

    return pl.pallas_call(
        body, name=f"ffn_bwd_{l}{j}", grid=(N_CHIPS, S // tm),
        in_specs=[pl.BlockSpec((tm, D), lambda c, t: (t, 0)), pl.BlockSpec((1, D), lambda c, t: (0, 0)),
                  _ffn_wspec(l, j, D, FF_SHARD, True), _ffn_wspec(l, j, D, FF_SHARD, True),
                  _ffn_wspec(l, j, FF_SHARD, D, True), pl.BlockSpec((tm, D), lambda c, t: (t, 0))],
        out_specs=[pl.BlockSpec((1, tm, D), lambda c, t: (c, t, 0)),
                   pl.BlockSpec((1, D, FF_SHARD), lambda c, t: (c, 0, 0)),
                   pl.BlockSpec((1, D, FF_SHARD), lambda c, t: (c, 0, 0)),
                   pl.BlockSpec((1, FF_SHARD, D), lambda c, t: (c, 0, 0))],
        out_shape=[jax.ShapeDtypeStruct((N_CHIPS, S, D), F32), jax.ShapeDtypeStruct((N_CHIPS, D, FF_SHARD), F32),
                   jax.ShapeDtypeStruct((N_CHIPS, D, FF_SHARD), F32), jax.ShapeDtypeStruct((N_CHIPS, FF_SHARD, D), F32)],
        compiler_params=_cp(("parallel", "arbitrary")),
    )(x, g, wg, wu, wd, dout)


def norm_bwd(name, x, g, dh_parts, dres, tm=256):
    S = x.shape[0]
    P = dh_parts.shape[0]

    def body(x_ref, g_ref, dh_ref, dr_ref, dx_ref, dg_ref):
        dh = dh_ref[0]
        for p in range(1, P):
            dh = dh + dh_ref[p]
        _, vjp = jax.vjp(rms, x_ref[...], g_ref[...])
        dx, dg = vjp(dh)
        dx_ref[...] = dr_ref[...] + dx

        @pl.when(pl.program_id(0) == 0)
        def _():
            dg_ref[...] = dg

        @pl.when(pl.program_id(0) != 0)
        def _():
            dg_ref[...] += dg

    return pl.pallas_call(
        body, name=name, grid=(S // tm,),
        in_specs=[pl.BlockSpec((tm, D), lambda t: (t, 0)), pl.BlockSpec((1, D), lambda t: (0, 0)),
                  pl.BlockSpec((P, tm, D), lambda t: (0, t, 0)), pl.BlockSpec((tm, D), lambda t: (t, 0))],
        out_specs=[pl.BlockSpec((tm, D), lambda t: (t, 0)), pl.BlockSpec((1, D), lambda t: (0, 0))],
        out_shape=[jax.ShapeDtypeStruct((S, D), F32), jax.ShapeDtypeStruct((1, D), F32)],
        compiler_params=_cp(("arbitrary",)),
    )(x, g, dh_parts, dres)


def f_attn_sb(x, g, w):
    pr = mm(rms(x, g), w)
    return pr[:, :SB_W], pr[:, SB_W:2 * SB_W], pr[:, 2 * SB_W:]


def _pairs(y):
    return jnp.stack([y[:, 128 * j:128 * (j + 1)] for j in range(DL_PAIRS)])


def f_attn_qk(x, g, w, nrm):
    pr = mm(rms(x, g), w)
    ms = group_sum(pr * pr, DL_HEADS) * (1.0 / HEAD)
    return _pairs(pr * lax.rsqrt(ms + NORM_EPS) * jnp.concatenate([nrm] * DL_HEADS, axis=1))


def f_attn_v(x, g, w):
    return _pairs(mm(rms(x, g), w))


def _sb_tiles(q, k, qpos, kpos):
    z = _bdot(q, k, ((1,), (1,))) * (HEAD ** -0.5)
    strict = kpos < qpos
    keep = jnp.where(strict, -softplus(z), 0.0)
    return z, strict, keep


def _tri(n, upper):
    r = lax.broadcasted_iota(jnp.int32, (n, n), 0)
    c = lax.broadcasted_iota(jnp.int32, (n, n), 1)
    return ((r > c) if upper else (r < c)).astype(BF16)


def _tri_sums(xs, tri):
    x = jnp.concatenate(xs, axis=0)
    hi, lo = _split2(x)
    y = _dg(hi, tri, ((1,), (0,))) + _dg(lo, tri, ((1,), (0,)))
    n = xs[0].shape[0]
    return [y[n * i:n * (i + 1)] for i in range(len(xs))]


def sb_fwd(q, k, v, tb=QBLK):
    S = q.shape[0]
    nh = SB_W // HEAD

    def body(q_ref, k_ref, v_ref, o_ref):
        qb = pl.program_id(0)
        qpos = qb * tb + lax.broadcasted_iota(jnp.int32, (tb, tb), 0)
        col = lax.broadcasted_iota(jnp.int32, (tb, tb), 1)
        after_mat = _tri(tb, True)
        sls = [slice(HEAD * h, HEAD * (h + 1)) for h in range(nh)]
        qs = [q_ref[:, sl] for sl in sls]

        def step(i, carry):
            accs, runs = carry
            kb = qb - i
            rows = pl.ds(pl.multiple_of(kb * tb, tb), tb)
            kblk, vblk = k_ref[rows, :], v_ref[rows, :]
            tiles = [_sb_tiles(qs[h], kblk[:, sls[h]], qpos, kb * tb + col) for h in range(nh)]
            afters = _tri_sums([t[2] for t in tiles], after_mat)
            new_accs, new_runs = [], []
            for h, (z, strict, keep) in enumerate(tiles):
                w = jnp.where(strict, jnp.exp(z + keep + afters[h] + runs[h]), 0.0)
                new_accs.append(accs[h] + _bdot(w, vblk[:, sls[h]], ((1,), (0,))))
                new_runs.append(runs[h] + jnp.sum(keep, axis=1, keepdims=True))
            return tuple(new_accs), tuple(new_runs)

        init = (tuple(jnp.zeros((tb, HEAD), F32) for _ in range(nh)), tuple(jnp.zeros((tb, 1), F32) for _ in range(nh)))
        accs, _ = lax.fori_loop(0, qb + 1, step, init)
        o_ref[...] = jnp.concatenate(accs, axis=1)

    return pl.pallas_call(
        body, name="sb_fwd", grid=(S // tb,),
        in_specs=[pl.BlockSpec((tb, SB_W), lambda i: (i, 0)), pl.BlockSpec((S, SB_W), lambda i: (0, 0)),
                  pl.BlockSpec((S, SB_W), lambda i: (0, 0))],
        out_specs=pl.BlockSpec((tb, SB_W), lambda i: (i, 0)),
        out_shape=jax.ShapeDtypeStruct((S, SB_W), F32),
        compiler_params=_cp(("parallel",)),
    )(q, k, v)


def sb_bwd(q, k, v, do, tb=QBLK):
    S = q.shape[0]
    nh = SB_W // HEAD
    scale = HEAD ** -0.5

    def body(q_ref, k_ref, v_ref, do_ref, dq_ref, dk_ref, dv_ref, g_scr):
        qb = pl.program_id(0)

        @pl.when(qb == 0)
        def _():
            dk_ref[...] = jnp.zeros_like(dk_ref)
            dv_ref[...] = jnp.zeros_like(dv_ref)

        qpos = qb * tb + lax.broadcasted_iota(jnp.int32, (tb, tb), 0)
        col = lax.broadcasted_iota(jnp.int32, (tb, tb), 1)
        after_mat = _tri(tb, True)
        before_mat = _tri(tb, False)
        sls = [slice(HEAD * h, HEAD * (h + 1)) for h in range(nh)]
        qs = [q_ref[:, sl] for sl in sls]
        dos = [do_ref[:, sl] for sl in sls]

        def right_to_left(i, runs):
            kb = qb - i
            rows = pl.ds(pl.multiple_of(kb * tb, tb), tb)
            kblk, vblk = k_ref[rows, :], v_ref[rows, :]
            tiles = [_sb_tiles(qs[h], kblk[:, sls[h]], qpos, kb * tb + col) for h in range(nh)]
            afters = _tri_sums([t[2] for t in tiles], after_mat)
            dvs, new_runs = [], []
            for h, (z, strict, keep) in enumerate(tiles):
                w = jnp.where(strict, jnp.exp(z + keep + afters[h] + runs[h]), 0.0)
                g_scr[h, kb] = _bdot(dos[h], vblk[:, sls[h]], ((1,), (1,))) * w
                dvs.append(_bdot(w, dos[h], ((0,), (0,))))
                new_runs.append(runs[h] + jnp.sum(keep, axis=1, keepdims=True))
            dv_ref[rows, :] += jnp.concatenate(dvs, axis=1)
            return tuple(new_runs)

        zero_runs = tuple(jnp.zeros((tb, 1), F32) for _ in range(nh))
        lax.fori_loop(0, qb + 1, right_to_left, zero_runs)

        def left_to_right(kb, carry):
            dqs, runs = carry
            rows = pl.ds(pl.multiple_of(kb * tb, tb), tb)
            kblk = k_ref[rows, :]
            strict = (kb * tb + col) < qpos
            gws = [g_scr[h, kb] for h in range(nh)]
            befores = _tri_sums(gws, before_mat)
            new_dqs, new_runs, dks = [], [], []
            for h in range(nh):
                kh = kblk[:, sls[h]]
                sig = jax.nn.sigmoid(_bdot(qs[h], kh, ((1,), (1,))) * scale)
                dkeep = jnp.where(strict, befores[h] + runs[h], 0.0)
                dz = (gws[h] * (1.0 - sig) - dkeep * sig) * scale
                new_dqs.append(dqs[h] + _bdot(dz, kh, ((1,), (0,))))
                dks.append(_bdot(dz, qs[h], ((0,), (0,))))
                new_runs.append(runs[h] + jnp.sum(gws[h], axis=1, keepdims=True))
            dk_ref[rows, :] += jnp.concatenate(dks, axis=1)
            return tuple(new_dqs), tuple(new_runs)

        dqs, _ = lax.fori_loop(0, qb + 1, left_to_right,
                               (tuple(jnp.zeros((tb, HEAD), F32) for _ in range(nh)), zero_runs))
        dq_ref[...] = jnp.concatenate(dqs, axis=1)

    whole = pl.BlockSpec((S, SB_W), lambda i: (0, 0))
    blk = pl.BlockSpec((tb, SB_W), lambda i: (i, 0))
    return pl.pallas_call(
        body, name="sb_bwd", grid=(S // tb,),
        in_specs=[blk, whole, whole, blk], out_specs=[blk, whole, whole],
        out_shape=[jax.ShapeDtypeStruct((S, SB_W), F32)] * 3,
        scratch_shapes=[pltpu.VMEM((nh, S // tb, tb, tb), F32)],
        compiler_params=_cp(("arbitrary",)),
    )(q, k, v, do)


def reorder(name, x, groups, inverse):
    S = x.shape[1]
    out = x
    for gi, r in enumerate(groups):
        if r > 1:
            out = _reorder_call(f"{name}_{r}", x, out, gi, r, S // r, inverse)
    return out


def _reorder_call(name, x, prev, gi, r, L, inverse):
    S = x.shape[1]
    whole = pl.BlockSpec((None, S, 128), lambda p, c: (2 * gi + p, 0, 0))
    part = pl.BlockSpec((None, L, 128), lambda p, c: (2 * gi + p, c, 0))

    def body(x_ref, prev_ref, o_ref):
        c = pl.program_id(1)
        if inverse:
            o_ref[pl.ds(c, L, stride=r), :] = x_ref[...]
        else:
            o_ref[...] = x_ref[pl.ds(c, L, stride=r), :]

    return pl.pallas_call(
        body, name=name, grid=(2, r),
        in_specs=[part if inverse else whole, pl.BlockSpec(memory_space=pl.ANY)],
        out_specs=whole if inverse else part,
        out_shape=jax.ShapeDtypeStruct(x.shape, x.dtype),
        input_output_aliases={1: 0},
        compiler_params=_cp(("parallel", "arbitrary")),
    )(x, prev)


def _dil_blocks(S):
    return S // QBLK


def _dil_mask(n_in_stream):
    qi = lax.broadcasted_iota(jnp.int32, (QBLK, 2 * QBLK), 0)
    kj = lax.broadcasted_iota(jnp.int32, (QBLK, 2 * QBLK), 1) - QBLK
    dist = qi - kj
    return (dist >= 0) & (dist <= QBLK) & ((n_in_stream > 0) | (kj >= 0))


def _stream_pos(gi, i, S):
    nb = jnp.where(gi == 0, S // (QBLK * DIL[0]), jnp.where(gi == 1, S // (QBLK * DIL[1]), S // (QBLK * DIL[2])))
    return i % nb


def dil_fwd(q, k, v, bias):
    S = q.shape[1]
    nblk = _dil_blocks(S)

    def body(q_ref, kc_ref, kp_ref, vc_ref, vp_ref, b_ref, o_ref, l_ref):
        gi, i = pl.program_id(0), pl.program_id(1)
        mask = _dil_mask(_stream_pos(gi, i, S))
        for j in range(2):
            q2, kc, kp, vc, vp = q_ref[j], kc_ref[j], kp_ref[j], vc_ref[j], vp_ref[j]
            os_, ls_ = [], []
            for hh in range(2):
                sl = slice(HEAD * hh, HEAD * (hh + 1))
                kw = jnp.concatenate([kp[:, sl], kc[:, sl]], axis=0)
                vw = jnp.concatenate([vp[:, sl], vc[:, sl]], axis=0)
                lg = _bdot(q2[:, sl], kw, ((1,), (1,))) * (HEAD ** -0.5) + b_ref[2 * j + hh]
                lg = jnp.where(mask, lg, NEG_INF)
                m = jnp.max(lg, axis=-1, keepdims=True)
                p = jnp.exp(lg - m)
                den = jnp.sum(p, axis=-1, keepdims=True)
                os_.append(_bdot(p / den, vw, ((1,), (0,))))
                ls_.append(jnp.broadcast_to(m + jnp.log(den), (QBLK, HEAD)))
            o_ref[j] = jnp.concatenate(os_, axis=1)
            l_ref[j] = jnp.concatenate(ls_, axis=1)

    cur = pl.BlockSpec((2, QBLK, 128), lambda g, i: (g, i, 0))
    prev = pl.BlockSpec((2, QBLK, 128), lambda g, i: (g, jnp.maximum(i - 1, 0), 0))
    return pl.pallas_call(
        body, name="dil_fwd", grid=(len(DIL), nblk),
        in_specs=[cur, cur, prev, cur, prev, pl.BlockSpec((4, QBLK, 2 * QBLK), lambda g, i: (g, 0, 0))],
        out_specs=[cur, cur],
        out_shape=[jax.ShapeDtypeStruct(q.shape, F32)] * 2,
        compiler_params=_cp(("parallel", "parallel")),
    )(q, k, k, v, v, bias)


def dil_bwd(q, k, v, bias, o, lse, do, dlse):
    S = q.shape[1]
    nblk = _dil_blocks(S)

    def body(q_ref, kc_ref, kp_ref, vc_ref, vp_ref, b_ref, o_ref, l_ref, do_ref, dl_ref,
             dq_ref, dk_ref, dv_ref, ds_ref, dk_car, dv_car):
        gi, i = pl.program_id(0), pl.program_id(1)

        @pl.when(i == 0)
        def _():
            ds_ref[...] = jnp.zeros_like(ds_ref)
            dk_car[...] = jnp.zeros_like(dk_car)
            dv_car[...] = jnp.zeros_like(dv_car)

        @pl.when(i < nblk)
        def _():
            mask = _dil_mask(_stream_pos(gi, i, S))
            for j in range(2):
                q2, kc, kp, vc, vp = q_ref[j], kc_ref[j], kp_ref[j], vc_ref[j], vp_ref[j]
                o2, l2, do2, dl2 = o_ref[j], l_ref[j], do_ref[j], dl_ref[j]
                dqs, dkps, dkcs, dvps, dvcs = [], [], [], [], []
                for hh in range(2):
                    sl = slice(HEAD * hh, HEAD * (hh + 1))
                    qh, doh = q2[:, sl], do2[:, sl]
                    kw = jnp.concatenate([kp[:, sl], kc[:, sl]], axis=0)
                    vw = jnp.concatenate([vp[:, sl], vc[:, sl]], axis=0)
                    lg = _bdot(qh, kw, ((1,), (1,))) * (HEAD ** -0.5) + b_ref[2 * j + hh]
                    p = jnp.where(mask, jnp.exp(lg - l2[:, HEAD * hh:HEAD * hh + 1]), 0.0)
                    dp = _bdot(doh, vw, ((1,), (1,)))
                    delta = jnp.sum(doh * o2[:, sl], axis=-1, keepdims=True)
                    dl = jnp.sum(dl2[:, sl], axis=-1, keepdims=True)
                    ds = p * (dp - delta + dl)
                    ds_ref[2 * j + hh] += ds
                    dsq = ds * (HEAD ** -0.5)
                    dqs.append(_bdot(dsq, kw, ((1,), (0,))))
                    dkw = _bdot(dsq, qh, ((0,), (0,)))
                    dvw = _bdot(p, doh, ((0,), (0,)))
                    dkps.append(dkw[:QBLK])
                    dkcs.append(dkw[QBLK:])
                    dvps.append(dvw[:QBLK])
                    dvcs.append(dvw[QBLK:])
                dq_ref[j] = jnp.concatenate(dqs, axis=1)
                dk_ref[j] = dk_car[j] + jnp.concatenate(dkps, axis=1)
                dv_ref[j] = dv_car[j] + jnp.concatenate(dvps, axis=1)
                dk_car[j] = jnp.concatenate(dkcs, axis=1)
                dv_car[j] = jnp.concatenate(dvcs, axis=1)

        @pl.when(i == nblk)
        def _():
            dk_ref[...] = dk_car[...]
            dv_ref[...] = dv_car[...]

    cur = pl.BlockSpec((2, QBLK, 128), lambda g, i: (g, jnp.minimum(i, nblk - 1), 0))
    prev = pl.BlockSpec((2, QBLK, 128), lambda g, i: (g, jnp.clip(i - 1, 0, nblk - 1), 0))
    bspec = pl.BlockSpec((4, QBLK, 2 * QBLK), lambda g, i: (g, 0, 0))
    return pl.pallas_call(
        body, name="dil_bwd", grid=(len(DIL), nblk + 1),
        in_specs=[cur, cur, prev, cur, prev, bspec, cur, cur, cur, cur],
        out_specs=[cur, prev, prev, bspec],
        out_shape=[jax.ShapeDtypeStruct(q.shape, F32)] * 3 + [jax.ShapeDtypeStruct(bias.shape, F32)],
        scratch_shapes=[pltpu.VMEM((2, QBLK, 128), F32), pltpu.VMEM((2, QBLK, 128), F32)],
        compiler_params=_cp(("arbitrary", "arbitrary")),
    )(q, k, k, v, v, bias, o, lse, do, dlse)


def _t5_bucket(dist):
    max_exact = N_BUCKETS // 2
    d = jnp.maximum(dist, 1).astype(F32)
    large = max_exact + (jnp.log(d / max_exact) / math.log(MAX_DISTANCE / max_exact)
                         * (N_BUCKETS - max_exact)).astype(jnp.int32)
    large = jnp.minimum(large, N_BUCKETS - 1)
    return jnp.where(dist < max_exact, dist, large)


def _bucket_maps():
    qi = jnp.arange(QBLK)[:, None]
    kj = jnp.arange(2 * QBLK)[None, :] - QBLK
    dist = jnp.maximum(qi - kj, 0)
    return jnp.stack([_t5_bucket(dist * r) for r in DIL])


def bias_table(rel_bias, buckets):
    def body(tbl_ref, bk_ref, o_ref):
        for h in range(DL_HEADS):
            bk = bk_ref[h // 4]

            def step(b, acc):
                return jnp.where(bk == b, tbl_ref[b, h], acc)

            o_ref[h] = lax.fori_loop(0, N_BUCKETS, step, jnp.zeros(bk.shape, F32))

    return pl.pallas_call(
        body, name="bias_table", out_shape=jax.ShapeDtypeStruct((DL_HEADS,) + buckets.shape[1:], F32),
        in_specs=[pl.BlockSpec(memory_space=pltpu.SMEM), pl.BlockSpec(memory_space=pltpu.VMEM)],
        out_specs=pl.BlockSpec(memory_space=pltpu.VMEM),
    )(rel_bias, buckets)


def bias_grad(ds, buckets):
    def body(ds_ref, bk_ref, o_ref):
        lane = lax.broadcasted_iota(jnp.int32, (1, 128), 1)
        for h in range(DL_HEADS):
            dsv = ds_ref[h]
            bk = bk_ref[h // 4]

            def step(b, row):
                return jnp.where(lane == b, jnp.sum(jnp.where(bk == b, dsv, 0.0)), row)

            o_ref[h:h + 1, :] = lax.fori_loop(0, N_BUCKETS, step, jnp.zeros((1, 128), F32))

    return pl.pallas_call(
        body, name="bias_grad", out_shape=jax.ShapeDtypeStruct((DL_HEADS, 128), F32),
        in_specs=[pl.BlockSpec(memory_space=pltpu.VMEM)] * 2, out_specs=pl.BlockSpec(memory_space=pltpu.VMEM),
    )(ds, buckets)


def f_attn_out(x, oa, o, lse, w):
    og = [jnp.concatenate([o[2 * g], o[2 * g + 1]], axis=1) for g in range(3)]
    lg = [jnp.concatenate([lse[2 * g], lse[2 * g + 1]], axis=1) for g in range(3)]
    m = jnp.maximum(jnp.maximum(lg[0], lg[1]), lg[2])
    e = [jnp.exp(l - m) for l in lg]
    den = e[0] + e[1] + e[2]
    ob = (e[0] * og[0] + e[1] * og[1] + e[2] * og[2]) / den
    return x + mm(jnp.concatenate([oa, ob], axis=1), w)


def norm_shift_fwd(x, g, tm=256):
    S = x.shape[0]

    def body(x_ref, xp_ref, g_ref, h_ref, hs_ref):
        h = rms(x_ref[...], g_ref[...])
        hp = rms(xp_ref[7:8, :], g_ref[...])
        hp = jnp.where(pl.program_id(0) == 0, 0.0, hp)
        row = lax.broadcasted_iota(jnp.int32, (tm, D), 0)
        h_ref[...] = h
        hs_ref[...] = jnp.where(row == 0, hp, pltpu.roll(h, 1, 0))

    return pl.pallas_call(
        body, name="rw_norm_shift", grid=(S // tm,),
        in_specs=[pl.BlockSpec((tm, D), lambda t: (t, 0)),
                  pl.BlockSpec((8, D), lambda t: (jnp.maximum(t * (tm // 8) - 1, 0), 0)),
                  pl.BlockSpec((1, D), lambda t: (0, 0))],
        out_specs=[pl.BlockSpec((tm, D), lambda t: (t, 0))] * 2,
        out_shape=[jax.ShapeDtypeStruct((S, D), F32)] * 2,
        compiler_params=_cp(("parallel",)),
    )(x, x, g)


def norm_shift_bwd(x, g, dh, dhs, dres, tm=256):
    S = x.shape[0]
    nt = S // tm

    def body(x_ref, g_ref, dh_ref, dhs_ref, dhn_ref, dr_ref, dx_ref, dg_ref):
        t = pl.program_id(0)
        nxt = jnp.where(t == nt - 1, 0.0, dhn_ref[0:1, :])
        row = lax.broadcasted_iota(jnp.int32, (tm, D), 0)
        tot = dh_ref[...] + jnp.where(row == tm - 1, nxt, pltpu.roll(dhs_ref[...], tm - 1, 0))
        _, vjp = jax.vjp(rms, x_ref[...], g_ref[...])
        dx, dg = vjp(tot)
        dx_ref[...] = dr_ref[...] + dx

        @pl.when(t == 0)
        def _():
            dg_ref[...] = dg

        @pl.when(t != 0)
        def _():
            dg_ref[...] += dg

    tile = pl.BlockSpec((tm, D), lambda t: (t, 0))
    return pl.pallas_call(
        body, name="rw_norm_shift_bwd", grid=(nt,),
        in_specs=[tile, pl.BlockSpec((1, D), lambda t: (0, 0)), tile, tile,
                  pl.BlockSpec((8, D), lambda t: (jnp.minimum((t + 1) * (tm // 8), S // 8 - 1), 0)), tile],
        out_specs=[tile, pl.BlockSpec((1, D), lambda t: (0, 0))],
        out_shape=[jax.ShapeDtypeStruct((S, D), F32), jax.ShapeDtypeStruct((1, D), F32)],
        compiler_params=_cp(("arbitrary",)),
    )(x, g, dh, dhs, dhs, dres)


def f_rw_proj(h, hs, mix, w):
    return mm(h + (hs - h) * mix, w)


def f_rw_mid(h, hs, r, k, v, mix3, w0, a0, kkw, kaw, w1, w2, a1, a2, g1, g2):
    xx = hs - h
    xw, xa, xg = h + xx * mix3[0:1], h + xx * mix3[1:2], h + xx * mix3[2:3]
    w_log = -softplus(-(w0 + mm(jnp.tanh(mm(xw, w1)), w2))) - 0.5
    lw = -jnp.exp(w_log)
    ag = jax.nn.sigmoid(a0 + mm(mm(xa, a1), a2))
    gate = mm(jax.nn.sigmoid(mm(xg, g1)), g2)
    kk = k * kkw
    kk = kk / jnp.maximum(jnp.sqrt(group_sum(kk * kk, RW_H)), 1e-12)
    kmod = k * (1.0 + (ag - 1.0) * kaw)
    return (to_heads(r), to_heads(lw), to_heads(kmod), to_heads(v), to_heads(-kk), to_heads(kk * ag), gate)


def f_rw_post(yh, rh, kh, vh, gate, x, lng, lnb, rk, wo):
    mu = jnp.mean(yh, axis=-1, keepdims=True)
    var = jnp.mean(jnp.square(yh - mu), axis=-1, keepdims=True)
    yn = (yh - mu) * lax.rsqrt(var + GN_EPS)
    bonus = jnp.sum(rh * kh * rk, axis=-1, keepdims=True) * vh
    y = from_heads(yn) * lng + lnb + from_heads(bonus)
    return x + mm(y * gate, wo)


def _split2(x):
    hi = x.astype(BF16)
    return hi, (x - hi.astype(F32)).astype(BF16)


def _b3(x, y, cx, cy):
    dn = (((cx,), (cy,)), ((0,), (0,)))
    xh, xl = _split2(x)
    yh, yl = _split2(y)
    d = lambda p, q: lax.dot_general(p, q, dn, preferred_element_type=F32)
    return d(xh, yh) + (d(xh, yl) + d(xl, yh))


@jax.custom_vjp
def b_nt(x, y):
    return _b3(x, y, 2, 2)


@jax.custom_vjp
def b_nn(x, y):
    return _b3(x, y, 2, 1)


@jax.custom_vjp
def b_tn(x, y):
    return _b3(x, y, 1, 1)


b_nt.defvjp(lambda x, y: (b_nt(x, y), (x, y)), lambda r, g: (b_nn(g, r[1]), b_tn(g, r[0])))
b_nn.defvjp(lambda x, y: (b_nn(x, y), (x, y)), lambda r, g: (b_nt(g, r[1]), b_tn(r[0], g)))
b_tn.defvjp(lambda x, y: (b_tn(x, y), (x, y)), lambda r, g: (b_nt(r[1], g), b_nn(r[0], g)))


def _tri_apply(x, lower):
    H, C, _ = x.shape
    ii = lax.broadcasted_iota(jnp.int32, (C, C), 0)
    jj = lax.broadcasted_iota(jnp.int32, (C, C), 1)
    m = jnp.broadcast_to(((jj <= ii) if lower else (jj >= ii)).astype(BF16), (H, C, C))
    x1 = x.astype(BF16)
    r1 = x - x1.astype(F32)
    x2 = r1.astype(BF16)
    x3 = (r1 - x2.astype(F32)).astype(BF16)
    d = lambda q: lax.dot_general(m, q, (((2,), (1,)), ((0,), (0,))), preferred_element_type=F32)
    return d(x1) + (d(x2) + d(x3))


@jax.custom_vjp
def run_sum(x):
    return _tri_apply(x, True)


run_sum.defvjp(lambda x: (run_sum(x), None), lambda _, g: (_tri_apply(g, False),))


def rwkv_chunk(S0, r, lw, k, v, a, b):
    H, C, _ = r.shape
    V = S0.shape[1]
    ii = lax.broadcasted_iota(jnp.int32, (C, C), 0)
    jj = lax.broadcasted_iota(jnp.int32, (C, C), 1)
    strict = jj < ii
    i2 = lax.broadcasted_iota(jnp.int32, (C, 2 * C), 0)
    j2 = lax.broadcasted_iota(jnp.int32, (C, 2 * C), 1)
    incl2 = jnp.where(j2 >= C, j2 - C, j2) <= i2
    g = run_sum(lw)
    ig = jnp.exp(-g)
    ar = jnp.concatenate([a * jnp.exp(g - lw), r * jnp.exp(g)], axis=1)
    bk = jnp.concatenate([b * ig, k * ig], axis=1)
    m = b_nt(ar, bk)
    a_ab = jnp.where(strict, m[:, :C, :C], 0.0)
    a_ak = jnp.where(strict, m[:, :C, C:], 0.0)
    b_r = jnp.where(incl2, m[:, C:, :], 0.0)
    p = b_nt(ar, S0)
    u = p[:, :C] + b_nn(a_ak, v)
    nmat, n = a_ab, 1
    while n < C:
        n *= 2
        if n < C:
            z = b_nn(nmat, jnp.concatenate([u, nmat], axis=2))
            u, nmat = u + z[:, :, :V], z[:, :, V:]
        else:
            u = u + b_nn(nmat, u)
    uv = jnp.concatenate([u, v], axis=1)
    y = p[:, C:] + b_nn(b_r, uv)
    g_end = g[:, C - 1:C, :]
    dec = jnp.exp(g_end - g)
    s_new = S0 * jnp.exp(g_end) + b_tn(uv, jnp.concatenate([b * dec, k * dec], axis=1))
    return y, s_new


def rwkv_fwd(r, lw, k, v, a, b):
    H, S, _ = r.shape
    C = RW_CHUNK

    def body(r_ref, lw_ref, k_ref, v_ref, a_ref, b_ref, y_ref, s_ref, s_scr):
        @pl.when(pl.program_id(0) == 0)
        def _():
            s_scr[...] = jnp.zeros_like(s_scr)

        s0 = s_scr[...]
        s_ref[0] = s0
        y, s1 = rwkv_chunk(s0, r_ref[...], lw_ref[...], k_ref[...], v_ref[...], a_ref[...], b_ref[...])
        y_ref[...] = y
        s_scr[...] = s1

    bs = pl.BlockSpec((H, C, HEAD), lambda c: (0, c, 0))
    return pl.pallas_call(
        body, name="rwkv_fwd", grid=(S // C,), in_specs=[bs] * 6,
        out_specs=[bs, pl.BlockSpec((1, H, HEAD, HEAD), lambda c: (c, 0, 0, 0))],
        out_shape=[jax.ShapeDtypeStruct((H, S, HEAD), F32), jax.ShapeDtypeStruct((S // C, H, HEAD, HEAD), F32)],
        scratch_shapes=[pltpu.VMEM((H, HEAD, HEAD), F32)],
        compiler_params=_cp(("arbitrary",)),
    )(r, lw, k, v, a, b)


def rwkv_bwd(r, lw, k, v, a, b, states, dy):
    H, S, _ = r.shape
    C = RW_CHUNK
    nc = S // C

    def body(r_ref, lw_ref, k_ref, v_ref, a_ref, b_ref, s_ref, dy_ref, dr, dlw, dk, dv, da, db, ds_scr):
        @pl.when(pl.program_id(0) == 0)
        def _():
            ds_scr[...] = jnp.zeros_like(ds_scr)

        _, vjp = jax.vjp(rwkv_chunk, s_ref[0], r_ref[...], lw_ref[...], k_ref[...], v_ref[...], a_ref[...], b_ref[...])
        grads = vjp((dy_ref[...], ds_scr[...]))
        ds_scr[...] = grads[0]
        for o, gv in zip((dr, dlw, dk, dv, da, db), grads[1:]):
            o[...] = gv

    bs = pl.BlockSpec((H, C, HEAD), lambda c: (0, nc - 1 - c, 0))
    return pl.pallas_call(
        body, name="rwkv_bwd", grid=(nc,),
        in_specs=[bs] * 6 + [pl.BlockSpec((1, H, HEAD, HEAD), lambda c: (nc - 1 - c, 0, 0, 0)), bs],
        out_specs=[bs] * 6, out_shape=[jax.ShapeDtypeStruct((H, S, HEAD), F32)] * 6,
        scratch_shapes=[pltpu.VMEM((H, HEAD, HEAD), F32)],
        compiler_params=_cp(("arbitrary",)),
    )(r, lw, k, v, a, b, states, dy)


def loss_head(y, target, tm=512):
    S = y.shape[0]

    def body(y_ref, t_ref, dy_ref, l_ref):
        e = y_ref[...] - t_ref[...]
        dy_ref[...] = e * (1.0 / D)
        part = jnp.broadcast_to(0.5 * jnp.sum(jnp.mean(e * e, axis=-1, keepdims=True)), (1, 128))

        @pl.when(pl.program_id(0) == 0)
        def _():
            l_ref[...] = part

        @pl.when(pl.program_id(0) != 0)
        def _():
            l_ref[...] += part

    tile = pl.BlockSpec((tm, D), lambda t: (t, 0))
    return pl.pallas_call(
        body, name="loss_head", grid=(S // tm,), in_specs=[tile, tile],
        out_specs=[tile, pl.BlockSpec((1, 128), lambda t: (0, 0))],
        out_shape=[jax.ShapeDtypeStruct((S, D), F32), jax.ShapeDtypeStruct((1, 128), F32)],
        compiler_params=_cp(("arbitrary",)),
    )(y, target)


def _row_tile(rows, cols, budget=1 << 19):
    best = None
    for tr in range(8, rows + 1, 8):
        if rows % tr == 0 and tr * cols <= budget:
            best = tr
    return best or rows


def _adam(w, g, m, v):
    m = ADAM_B1 * m + (1.0 - ADAM_B1) * g
    v = ADAM_B2 * v + (1.0 - ADAM_B2) * jnp.square(g)
    m_hat = m / (1.0 - ADAM_B1 ** ADAM_STEP)
    v_hat = v / (1.0 - ADAM_B2 ** ADAM_STEP)
    return -ADAM_LR * (m_hat / (jnp.sqrt(v_hat) + ADAM_EPS) + ADAM_WD * w), m, v


def sum_slots(name, parts, dtype=F32):
    n, R, C = parts.shape
    tr = _row_tile(R, C * n)

    def body(p_ref, o_ref):
        s = p_ref[0].astype(F32)
        for i in range(1, n):
            s = s + p_ref[i].astype(F32)
        o_ref[...] = s.astype(dtype)

    return pl.pallas_call(
        body, name=name, grid=(R // tr,),
        in_specs=[pl.BlockSpec((n, tr, C), lambda t: (0, t, 0))], out_specs=pl.BlockSpec((tr, C), lambda t: (t, 0)),
        out_shape=jax.ShapeDtypeStruct((R, C), dtype), compiler_params=_cp(("parallel",)),
    )(parts)


def adam_step(name, ga, gb, w, m, v):
    R, C = w.shape
    tr = _row_tile(R, C, 1 << 17)
    ins = [ga] + ([gb] if gb is not None else []) + [w, m, v]

    def body(*refs):
        g = refs[0][...]
        if gb is not None:
            g = g + refs[1][...]
        w_ref, m_ref, v_ref, g_out, d_out, m_out, v_out = refs[len(ins) - 3:]
        d, m2, v2 = _adam(w_ref[...], g, m_ref[...], v_ref[...])
        g_out[...] = g
        d_out[...] = d
        m_out[...] = m2
        v_out[...] = v2

    tile = pl.BlockSpec((tr, C), lambda t: (t, 0))
    return pl.pallas_call(
        body, name=name, grid=(R // tr,), in_specs=[tile] * len(ins), out_specs=[tile] * 4,
        out_shape=[jax.ShapeDtypeStruct((R, C), F32)] * 4, compiler_params=_cp(("parallel",)),
    )(*ins)


def _place():
    return lax.axis_index("x"), lax.axis_index("y"), lax.axis_index("c")


def _flip(me, mask):
    return tuple(1 - v if mk else v for v, mk in zip(me, mask))


CHIP_MASKS = ((1, 0, 0), (0, 1, 0), (1, 1, 0))
ALL_MASKS = tuple((a, b, c) for a in (0, 1) for b in (0, 1) for c in (0, 1) if (a, b, c) != (0, 0, 0))


def _chip(dev):
    return 2 * dev[0] + dev[1]


def _devno(dev):
    return 4 * dev[0] + 2 * dev[1] + dev[2]


def exchange(name, arrays, out_shapes, masks, copies, src_of, dst_of, local_of, alias=False):
    n, npeer = len(arrays), len(masks)
    nloc = 1

    def body(*refs):
        ins, outs = refs[:n], refs[n:2 * n]
        send_sems, recv_sems, local_sems = refs[2 * n:]
        me = _place()
        peers = [_flip(me, mk) for mk in masks]
        locals_ = []
        for i in range(n):
            for q, (src, dst) in enumerate(local_of(ins[i], outs[i], me)):
                cp = pltpu.make_async_copy(src, dst, local_sems.at[i * nloc + q])
                cp.start()
                locals_.append(cp)
        sends = []
        for i in range(n):
            for j, peer in enumerate(peers):
                srcs, dsts = src_of(ins[i], me, peer), dst_of(outs[i], me, j)
                for q in range(copies):
                    sem = (i * npeer + j) * copies + q
                    cp = pltpu.make_async_remote_copy(
                        src_ref=srcs[q], dst_ref=dsts[q], send_sem=send_sems.at[sem], recv_sem=recv_sems.at[sem],
                        device_id=peer, device_id_type=MESH)
                    cp.start()
                    sends.append(cp)
        for i in range(n):
            for j, peer in enumerate(peers):
                lands = dst_of(outs[i], peer, j)
                for q in range(copies):
                    sem = (i * npeer + j) * copies + q
                    pltpu.make_async_remote_copy(
                        src_ref=lands[q], dst_ref=lands[q], send_sem=send_sems.at[sem], recv_sem=recv_sems.at[sem],
                        device_id=peer, device_id_type=MESH).wait_recv()
        for cp in sends:
            cp.wait_send()
        for cp in locals_:
            cp.wait()

    hbm = pl.BlockSpec(memory_space=pl.ANY)
    return pl.pallas_call(
        body, name=name, in_specs=[hbm] * n, out_specs=[hbm] * n, out_shape=list(out_shapes),
        scratch_shapes=[pltpu.SemaphoreType.DMA((n * npeer * copies,)), pltpu.SemaphoreType.DMA((n * npeer * copies,)),
                        pltpu.SemaphoreType.DMA((n * nloc,))],
        input_output_aliases={i: i for i in range(n)} if alias else {},
    )(*arrays)


def _half(c, rows):
    return pl.ds(c * (rows // 2), rows // 2)


def gather_chips(arrays):
    outs = [jax.ShapeDtypeStruct((N_CHIPS,) + a.shape, a.dtype) for a in arrays]
    got = exchange("gather_weights", arrays, outs, CHIP_MASKS, 1,
                   src_of=lambda r, me, peer: [r.at[_half(me[2], r.shape[0])]],
                   dst_of=lambda o, sender, j: [o.at[_chip(sender), _half(sender[2], o.shape[1])]],
                   local_of=lambda r, o, me: [(r, o.at[_chip(me)])])
    return exchange("gather_swap", got, outs, ((0, 0, 1),), len(CHIP_MASKS),
                    src_of=lambda r, me, peer: [r.at[_chip(_flip(me, mk)), _half(me[2], r.shape[1])] for mk in CHIP_MASKS],
                    dst_of=lambda o, sender, j: [o.at[_chip(_flip(sender, mk)), _half(sender[2], o.shape[1])]
                                                 for mk in CHIP_MASKS],
                    local_of=lambda r, o, me: [], alias=True)


def reduce_chips(names, arrays, wire):
    pre = exchange("grad_pre_swap", arrays,
                   [jax.ShapeDtypeStruct((2, N_CHIPS, a.shape[1] // 2, a.shape[2]), F32) for a in arrays],
                   ((0, 0, 1),), 1,
                   src_of=lambda r, me, peer: [r.at[:, _half(1 - me[2], r.shape[1])]],
                   dst_of=lambda o, sender, j: [o.at[1]],
                   local_of=lambda r, o, me: [(r.at[:, _half(me[2], r.shape[1])], o.at[0])])
    chip_sum = [sum_slots(f"sum2_{nm}", p.reshape(2, -1, p.shape[-1]), dt).reshape(p.shape[1:])
                for nm, p, dt in zip(names, pre, wire)]
    landed = exchange("scatter_grads", chip_sum, [jax.ShapeDtypeStruct(a.shape, a.dtype) for a in chip_sum],
                      CHIP_MASKS, 1,
                      src_of=lambda r, me, peer: [r.at[_chip(peer)]],
                      dst_of=lambda o, sender, j: [o.at[j]],
                      local_of=lambda r, o, me: [(r.at[_chip(me)], o.at[3])])
    halves = [sum_slots(f"sum4_{nm}", p, F32) for nm, p in zip(names, landed)]
    return exchange("grad_final_swap", halves,
                    [jax.ShapeDtypeStruct((2 * a.shape[0], a.shape[1]), F32) for a in halves], ((0, 0, 1),), 1,
                    src_of=lambda r, me, peer: [r],
                    dst_of=lambda o, sender, j: [o.at[_half(sender[2], o.shape[0])]],
                    local_of=lambda r, o, me: [(r, o.at[_half(me[2], o.shape[0])])])


def gather_all(arrays):
    outs = [jax.ShapeDtypeStruct((8,) + a.shape, a.dtype) for a in arrays]
    return exchange("gather_replicated", arrays, outs, ALL_MASKS, 1,
                    src_of=lambda r, me, peer: [r],
                    dst_of=lambda o, sender, j: [o.at[_devno(sender)]],
                    local_of=lambda r, o, me: [(r, o.at[_devno(me)])])


def _unshard_cols(g):
    return jnp.transpose(g, (1, 0, 2)).reshape(g.shape[1], -1)


def _shard_cols(a):
    return jnp.transpose(a.reshape(a.shape[0], N_CHIPS, -1), (1, 0, 2))


def _forward_backward(x, tgt, W):
    S = x.shape[0]
    G = {}
    sd = jax.ShapeDtypeStruct

    def ffn(xin, l, j):
        return ffn_fwd(xin, W["ffn_norm"][l][j], W["ffn_w_gate"], W["ffn_w_up"], W["ffn_w_down"], l, j)

    def ffn_back(xin, dout, l, j):
        gn = W["ffn_norm"][l][j]
        dh, dwg, dwu, dwd = ffn_bwd(xin, gn, W["ffn_w_gate"], W["ffn_w_up"], W["ffn_w_down"], dout, l, j)
        dx, dg = norm_bwd(f"ffn_norm_bwd_{l}{j}", xin, gn, dh, dout)
        G[("ffn", l, j)] = (dg, dwg, dwu, dwd)
        return dx

    x0 = x
    x1 = ffn(x0, 0, 0)
    g0 = W["mix_norm"][0]
    sbq, sbk, sbv = tile_fwd(f_attn_sb, "attn_in_sb", [x1], [g0, W["attn_w_in"][0]], [sd((S, SB_W), F32)] * 3, 256)
    dl_shape = sd((DL_PAIRS, S, 128), F32)
    qn, = tile_fwd(f_attn_qk, "attn_in_q", [x1], [g0, W["attn_w_in"][1], W["attn_q_norm"]], [dl_shape], 256)
    kn, = tile_fwd(f_attn_qk, "attn_in_k", [x1], [g0, W["attn_w_in"][2], W["attn_k_norm"]], [dl_shape], 256)
    vv, = tile_fwd(f_attn_v, "attn_in_v", [x1], [g0, W["attn_w_in"][3]], [dl_shape], 256)
    oa = sb_fwd(sbq, sbk, sbv)
    qs, ks, vs = (reorder(nm, t, DIL, False) for nm, t in (("sub_q", qn), ("sub_k", kn), ("sub_v", vv)))
    o_s, lse_s = dil_fwd(qs, ks, vs, W["bias_mat"])
    o_n, lse_n = reorder("nat_o", o_s, DIL, True), reorder("nat_lse", lse_s, DIL, True)
    x2, = tile_fwd(f_attn_out, "attn_out", [x1, oa, o_n, lse_n], [W["attn_w_out"]], [sd((S, D), F32)], 256)
    x3 = ffn(x2, 0, 1)
    x4 = ffn(x3, 1, 0)
    g1 = W["mix_norm"][1]
    h, hs = norm_shift_fwd(x4, g1)
    mix = W["rw_mix"]
    r, = tile_fwd(f_rw_proj, "rw_proj_r", [h, hs], [mix[0:1], W["rw_wr"]], [sd((S, D), F32)], 256)
    k, = tile_fwd(f_rw_proj, "rw_proj_k", [h, hs], [mix[2:3], W["rw_wk"]], [sd((S, D), F32)], 256)
    v, = tile_fwd(f_rw_proj, "rw_proj_v", [h, hs], [mix[3:4], W["rw_wv"]], [sd((S, D), F32)], 256)
    mix3 = jnp.concatenate([mix[1:2], mix[4:5], mix[5:6]], axis=0)
    mid_w = [mix3, W["rw_w0"], W["rw_a0"], W["rw_kk"], W["rw_ka"], W["rw_w1"], W["rw_w2"], W["rw_a1"], W["rw_a2"],
             W["rw_g1"], W["rw_g2"]]
    hshape = sd((RW_H, S, HEAD), F32)
    mid_tiles = [h, hs, r, k, v]
    rh, lwh, kh, vh, ah, bh, gate = tile_fwd(f_rw_mid, "rw_mid", mid_tiles, mid_w, [hshape] * 6 + [sd((S, D), F32)], 128)
    yh, states = rwkv_fwd(rh, lwh, kh, vh, ah, bh)
    post_w = [W["rw_lnx_g"], W["rw_lnx_b"], W["rw_rk"], W["rw_wo"]]
    post_tiles = [yh, rh, kh, vh, gate, x4]
    x5, = tile_fwd(f_rw_post, "rw_post", post_tiles, post_w, [sd((S, D), F32)], 128)
    x6 = ffn(x5, 1, 1)
    dx6, loss_part = loss_head(x6, tgt)

    dx5 = ffn_back(x5, dx6, 1, 1)
    (dyh, drh, dkh, dvh, dgate, dx4), (d_lng, d_lnb, d_rk, d_wo) = tile_bwd(
        f_rw_post, "rw_post_bwd", post_tiles, post_w, [dx5], 128, [True] * 6, [True] * 4)
    drh2, dlwh, dkh2, dvh2, dah, dbh = rwkv_bwd(rh, lwh, kh, vh, ah, bh, states, dyh)
    mid_cts = [drh + drh2, dlwh, dkh + dkh2, dvh + dvh2, dah, dbh, dgate]
    (dh, dhs, dr, dk, dv), dmid_w = tile_bwd(f_rw_mid, "rw_mid_bwd", mid_tiles, mid_w, mid_cts, 128,
                                             [True] * 5, [True] * len(mid_w))
    dmix = {}
    for nm, ct, row, wname in (("r", dr, 0, "rw_wr"), ("k", dk, 2, "rw_wk"), ("v", dv, 3, "rw_wv")):
        (dh, dhs), (dmix[row], G[wname]) = tile_bwd(
            f_rw_proj, f"rw_proj_{nm}_bwd", [h, hs], [mix[row:row + 1], W[wname]], [ct], 256,
            [True, True], [True, True], acc={0: dh, 1: dhs})
    dx4, G[("mix_norm", 1)] = norm_shift_bwd(x4, g1, dh, dhs, dx4)
    dmix3 = dmid_w[0]
    G["rw_mix"] = jnp.concatenate([dmix[0], dmix3[0:1], dmix[2], dmix[3], dmix3[1:2], dmix3[2:3]], axis=0)
    for nm, gv in zip(("rw_w0", "rw_a0", "rw_kk", "rw_ka", "rw_w1", "rw_w2", "rw_a1", "rw_a2", "rw_g1", "rw_g2"), dmid_w[1:]):
        G[nm] = gv
    G["rw_lnx_g"], G["rw_lnx_b"], G["rw_rk"], G["rw_wo"] = d_lng, d_lnb, d_rk, d_wo
    dx3 = ffn_back(x3, dx4, 1, 0)
    dx2 = ffn_back(x2, dx3, 0, 1)
    (dx1, doa, do_n, dlse_n), (G["attn_w_out"],) = tile_bwd(
        f_attn_out, "attn_out_bwd", [x1, oa, o_n, lse_n], [W["attn_w_out"]], [dx2], 256, [True] * 4, [True])
    do_s, dlse_s = reorder("sub_do", do_n, DIL, False), reorder("sub_dlse", dlse_n, DIL, False)
    dqs, dks, dvs, dsum = dil_bwd(qs, ks, vs, W["bias_mat"], o_s, lse_s, do_s, dlse_s)
    G["rel_bias"] = bias_grad(dsum, W["buckets"])
    dqn, dkn, dvv = (reorder(nm, t, DIL, True) for nm, t in (("nat_dq", dqs), ("nat_dk", dks), ("nat_dv", dvs)))
    dsbq, dsbk, dsbv = sb_bwd(sbq, sbk, sbv, doa)
    dg0 = []
    dwin = []
    (dx1,), (dg, dw) = tile_bwd(f_attn_sb, "attn_in_sb_bwd", [x1], [g0, W["attn_w_in"][0]], [dsbq, dsbk, dsbv], 256,
                                [True], [True, True], acc={0: dx1})
    dg0.append(dg), dwin.append(dw)
    (dx1,), (dg, dw, G["attn_q_norm"]) = tile_bwd(f_attn_qk, "attn_in_q_bwd", [x1], [g0, W["attn_w_in"][1], W["attn_q_norm"]],
                                                  [dqn], 256, [True], [True] * 3, acc={0: dx1})
    dg0.append(dg), dwin.append(dw)
    (dx1,), (dg, dw, G["attn_k_norm"]) = tile_bwd(f_attn_qk, "attn_in_k_bwd", [x1], [g0, W["attn_w_in"][2], W["attn_k_norm"]],
                                                  [dkn], 256, [True], [True] * 3, acc={0: dx1})
    dg0.append(dg), dwin.append(dw)
    (dx1,), (dg, dw) = tile_bwd(f_attn_v, "attn_in_v_bwd", [x1], [g0, W["attn_w_in"][3]], [dvv], 256,
                                [True], [True, True], acc={0: dx1})
    dg0.append(dg), dwin.append(dw)
    G[("mix_norm", 0)] = dg0
    G["attn_w_in"] = dwin
    dx0 = ffn_back(x0, dx1, 0, 0)
    return loss_part, dx0, G


VEC_ROWS = ("ffn_norm", "rw_mix", "rw_w0", "rw_a0", "rw_kk", "rw_ka", "rw_lnx_g", "rw_lnx_b")


def kernel(x, ffn_norm, ffn_w_gate, ffn_w_up, ffn_w_down, mix_norm, rel_bias, attn_w_in, attn_q_norm, attn_k_norm, attn_w_out, rw_mix, rw_w0, rw_w1, rw_w2, rw_a0, rw_a1, rw_a2, rw_g1, rw_g2, rw_kk, rw_ka, rw_rk, rw_wr, rw_wk, rw_wv, rw_wo, rw_lnx_g, rw_lnx_b, loss_target, m_ffn_norm, m_ffn_w_gate, m_ffn_w_up, m_ffn_w_down, m_mix_norm, m_rel_bias, m_attn_w_in, m_attn_q_norm, m_attn_k_norm, m_attn_w_out, m_rw_mix, m_rw_w0, m_rw_w1, m_rw_w2, m_rw_a0, m_rw_a1, m_rw_a2, m_rw_g1, m_rw_g2, m_rw_kk, m_rw_ka, m_rw_rk, m_rw_wr, m_rw_wk, m_rw_wv, m_rw_wo, m_rw_lnx_g, m_rw_lnx_b, v_ffn_norm, v_ffn_w_gate, v_ffn_w_up, v_ffn_w_down, v_mix_norm, v_rel_bias, v_attn_w_in, v_attn_q_norm, v_attn_k_norm, v_attn_w_out, v_rw_mix, v_rw_w0, v_rw_w1, v_rw_w2, v_rw_a0, v_rw_a1, v_rw_a2, v_rw_g1, v_rw_g2, v_rw_kk, v_rw_ka, v_rw_rk, v_rw_wr, v_rw_wk, v_rw_wv, v_rw_wo, v_rw_lnx_g, v_rw_lnx_b):
    names = ["ffn_norm", "ffn_w_gate", "ffn_w_up", "ffn_w_down", "mix_norm", "rel_bias", "attn_w_in", "attn_q_norm",
             "attn_k_norm", "attn_w_out", "rw_mix", "rw_w0", "rw_w1", "rw_w2", "rw_a0", "rw_a1", "rw_a2", "rw_g1", "rw_g2",
             "rw_kk", "rw_ka", "rw_rk", "rw_wr", "rw_wk", "rw_wv", "rw_wo", "rw_lnx_g", "rw_lnx_b"]
    loc = locals()
    w = {n: loc[n] for n in names}
    mom = {n: loc["m_" + n] for n in names}
    vel = {n: loc["v_" + n] for n in names}
    S = x.shape[1]

    vec_shard = jnp.concatenate([w[n].reshape(-1, 256) for n in VEC_ROWS], axis=0)
    mats = ["ffn_w_gate", "ffn_w_up", "ffn_w_down", "attn_w_in", "attn_w_out", "rw_w1", "rw_w2", "rw_a1", "rw_a2",
            "rw_g1", "rw_g2", "rw_wr", "rw_wk", "rw_wv", "rw_wo"]
    send = [vec_shard] + [w[n].reshape(-1, w[n].shape[-1]).astype(BF16) for n in mats]
    got = gather_chips(send)
    vec_full = _unshard_cols(got[0])
    gm = dict(zip(mats, got[1:]))
    W = {
        "ffn_norm": [[vec_full[2 * l + j][None] for j in range(2)] for l in range(2)],
        "ffn_w_gate": gm["ffn_w_gate"].reshape(N_CHIPS, 2, 2, D, FF_SHARD),
        "ffn_w_up": gm["ffn_w_up"].reshape(N_CHIPS, 2, 2, D, FF_SHARD),
        "ffn_w_down": gm["ffn_w_down"].reshape(N_CHIPS, 2, 2, FF_SHARD, D),
        "mix_norm": [mix_norm[0:1], mix_norm[1:2]],
        "attn_w_in": [gm["attn_w_in"][p] for p in range(N_CHIPS)],
        "attn_q_norm": attn_q_norm, "attn_k_norm": attn_k_norm,
        "attn_w_out": _unshard_cols(gm["attn_w_out"]),
        "rw_mix": vec_full[4:10],
        "rw_w1": gm["rw_w1"].reshape(D, -1), "rw_a1": gm["rw_a1"].reshape(D, -1), "rw_g1": gm["rw_g1"].reshape(D, -1),
        "rw_w2": _unshard_cols(gm["rw_w2"]), "rw_a2": _unshard_cols(gm["rw_a2"]), "rw_g2": _unshard_cols(gm["rw_g2"]),
        "rw_wr": gm["rw_wr"].reshape(D, D), "rw_wk": gm["rw_wk"].reshape(D, D), "rw_wv": gm["rw_wv"].reshape(D, D),
        "rw_wo": gm["rw_wo"].reshape(D, D),
        "rw_rk": rw_rk[0][:, None, :],
    }
    for i, n in enumerate(("rw_w0", "rw_a0", "rw_kk", "rw_ka", "rw_lnx_g", "rw_lnx_b")):
        W[n] = vec_full[10 + i][None]
    buckets = _bucket_maps()
    W["buckets"] = buckets
    W["bias_mat"] = bias_table(rel_bias, buckets)

    loss_part, dx, G = _forward_backward(x[0], loss_target[0], W)
    loss = lax.psum(loss_part[0, 0], ("x", "y", "c"))

    def ffn_stack(idx):
        return jnp.stack([jnp.stack([G[("ffn", l, j)][idx] for j in range(2)], axis=1) for l in range(2)], axis=1)

    vec_rows = [G[("ffn", l, j)][0] for l in range(2) for j in range(2)] + [G["rw_mix"]] + \
               [G[n] for n in ("rw_w0", "rw_a0", "rw_kk", "rw_ka", "rw_lnx_g", "rw_lnx_b")]
    full = {
        "vec": _shard_cols(jnp.concatenate(vec_rows, axis=0)),
        "ffn_w_gate": ffn_stack(1), "ffn_w_up": ffn_stack(2), "ffn_w_down": ffn_stack(3),
        "attn_w_in": jnp.stack(G["attn_w_in"]),
        "attn_w_out": _shard_cols(G["attn_w_out"]),
        "rw_w1": G["rw_w1"].reshape(N_CHIPS, 256, -1), "rw_a1": G["rw_a1"].reshape(N_CHIPS, 256, -1),
        "rw_g1": G["rw_g1"].reshape(N_CHIPS, 256, -1),
        "rw_w2": _shard_cols(G["rw_w2"]), "rw_a2": _shard_cols(G["rw_a2"]), "rw_g2": _shard_cols(G["rw_g2"]),
        "rw_wr": G["rw_wr"].reshape(N_CHIPS, 256, D), "rw_wk": G["rw_wk"].reshape(N_CHIPS, 256, D),
        "rw_wv": G["rw_wv"].reshape(N_CHIPS, 256, D), "rw_wo": G["rw_wo"].reshape(N_CHIPS, 256, D),
    }
    order = ["vec"] + mats
    summed = reduce_chips(order, [full[n].reshape(N_CHIPS, -1, full[n].shape[-1]) for n in order],
                          [F32] + [BF16] * len(mats))

    rep = jnp.concatenate([G[("mix_norm", 0)][0] + G[("mix_norm", 0)][1] + G[("mix_norm", 0)][2] + G[("mix_norm", 0)][3],
                           G[("mix_norm", 1)]], axis=0).reshape(16, 128)
    rep = jnp.concatenate([rep, G["rel_bias"], jnp.pad(G["attn_q_norm"], ((0, 0), (0, 64))),
                           jnp.pad(G["attn_k_norm"], ((0, 0), (0, 64))), G["rw_rk"].reshape(8, 128),
                           jnp.zeros((2, 128), F32)], axis=0)
    rep_sum = sum_slots("sum_replicated", gather_all([rep])[0])
    g_rep = {
        "mix_norm": rep_sum[0:16].reshape(2, D),
        "rel_bias": jnp.transpose(rep_sum[16:28, :N_BUCKETS]),
        "attn_q_norm": rep_sum[28:29, :HEAD], "attn_k_norm": rep_sum[29:30, :HEAD],
        "rw_rk": rep_sum[30:38].reshape(1, RW_H, HEAD),
    }

    out = {}

    def adam(n, ga, gb):
        shp = w[n].shape
        to2 = lambda a: a.reshape(-1, shp[-1])
        res = adam_step(f"adam_{n}", to2(ga), None if gb is None else to2(gb), to2(w[n]), to2(mom[n]), to2(vel[n]))
        out[n] = tuple(r.reshape(shp) for r in res)

    part = dict(zip(order, summed))
    for n in mats:
        adam(n, part[n], None)
    rows = {"ffn_norm": (0, 4), "rw_mix": (4, 10), "rw_w0": (10, 11), "rw_a0": (11, 12), "rw_kk": (12, 13),
            "rw_ka": (13, 14), "rw_lnx_g": (14, 15), "rw_lnx_b": (15, 16)}
    for n, (lo, hi) in rows.items():
        adam(n, part["vec"][lo:hi], None)
    for n, gv in g_rep.items():
        adam(n, gv, None)

    grads = [out[n][0] for n in names]
    deltas = [out[n][1] for n in names]
    new_m = [out[n][2] for n in names]
    new_v = [out[n][3] for n in names]
    return (loss, dx[None], *grads, *deltas, *new_m, *new_v)
```

```python
import functools
import math

import jax
import jax.numpy as jnp
from jax import lax
from jax.experimental import pallas as pl
from jax.experimental.pallas import tpu as pltpu

F32, BF16 = jnp.float32, jnp.bfloat16
HI = lax.Precision.HIGHEST
MESH = pl.DeviceIdType.MESH

D = 1024
HEAD = 64
N_CHIPS = 4
FF_SHARD = 704
SB_W = 256
DL_HEADS = 12
DL_PAIRS = 6
DIL = (1, 4, 16)
QBLK = 128
N_BUCKETS = 32
MAX_DISTANCE = 2048
RW_H = 16
RW_CHUNK = 64
NORM_EPS = 1e-6
GN_EPS = 64e-5
NEG_INF = -1e30
VMEM_LIMIT = 56 * 1024 * 1024

ADAM_LR, ADAM_B1, ADAM_B2, ADAM_EPS, ADAM_WD, ADAM_STEP = 0.001, 0.9, 0.999, 1e-08, 0.01, 10


def _cp(sem):
    return pltpu.CompilerParams(dimension_semantics=sem, vmem_limit_bytes=VMEM_LIMIT)


def _dg(a, b, dims, prec=None):
    return lax.dot_general(a, b, (dims, ((), ())), precision=prec, preferred_element_type=F32)


def _bdot(a, b, dims):
    return _dg(a.astype(BF16), b.astype(BF16), dims)


@jax.custom_vjp
def mm(a, b):
    return _bdot(a, b, ((1,), (0,)))


def _mm_fwd(a, b):
    return _bdot(a, b, ((1,), (0,))), (a, b)


def _mm_bwd(res, g):
    a, b = res
    return _bdot(g, b, ((1,), (1,))), _bdot(a, g, ((0,), (0,)))


mm.defvjp(_mm_fwd, _mm_bwd)


def rms(x, g):
    return x * lax.rsqrt(jnp.mean(x * x, axis=-1, keepdims=True) + NORM_EPS) * g


def group_sum(x, nh):
    w = x.shape[-1]
    e = (lax.broadcasted_iota(jnp.int32, (w, nh), 0) // HEAD == lax.broadcasted_iota(jnp.int32, (w, nh), 1)).astype(F32)
    s = _dg(x, e, ((1,), (0,)), HI)
    return _dg(s, e, ((1,), (1,)), HI)


def softplus(u):
    return jnp.maximum(u, 0.0) + jnp.log1p(jnp.exp(-jnp.abs(u)))


def to_heads(t, nh=RW_H):
    return jnp.stack([t[:, HEAD * h:HEAD * (h + 1)] for h in range(nh)])


def from_heads(t):
    return jnp.concatenate([t[h] for h in range(t.shape[0])], axis=-1)


def _tile_spec(shape, tm):
    if len(shape) == 2:
        return pl.BlockSpec((tm, shape[1]), lambda t: (t, 0))
    return pl.BlockSpec((shape[0], tm, shape[2]), lambda t: (0, t, 0))


def _full_spec(shape):
    nd = len(shape)
    return pl.BlockSpec(tuple(shape), lambda t: (0,) * nd)


def _rows(a):
    return a.shape[0] if a.ndim == 2 else a.shape[1]


def tile_fwd(f, name, tiles, weights, outs, tm):
    nt, nw = len(tiles), len(weights)

    def body(*refs):
        tv = [r[...] for r in refs[:nt]]
        wv = [r[...].astype(F32) for r in refs[nt:nt + nw]]
        res = f(*tv, *wv)
        if not isinstance(res, (tuple, list)):
            res = (res,)
        for o, v in zip(refs[nt + nw:], res):
            o[...] = v.astype(o.dtype)

    return pl.pallas_call(
        body, name=name, grid=(_rows(tiles[0]) // tm,),
        in_specs=[_tile_spec(a.shape, tm) for a in tiles] + [_full_spec(w.shape) for w in weights],
        out_specs=[_tile_spec(o.shape, tm) for o in outs],
        out_shape=list(outs),
        compiler_params=_cp(("parallel",)),
    )(*tiles, *weights)


def tile_bwd(f, name, tiles, weights, cts, tm, dt, dw, acc=None):
    acc = acc or {}
    nt, nw, nc = len(tiles), len(weights), len(cts)
    acc_idx = sorted(acc)
    na = len(acc_idx)
    dti = [i for i in range(nt) if dt[i]]
    dwi = [i for i in range(nw) if dw[i]]

    def body(*refs):
        tv = [r[...] for r in refs[:nt]]
        wv = [r[...].astype(F32) for r in refs[nt:nt + nw]]
        cv = [r[...] for r in refs[nt + nw:nt + nw + nc]]
        av = {i: r[...] for i, r in zip(acc_idx, refs[nt + nw + nc:nt + nw + nc + na])}
        orefs = refs[nt + nw + nc + na:]

        def g(*diff):
            t2, w2 = list(tv), list(wv)
            for i, v in zip(dti, diff[:len(dti)]):
                t2[i] = v
            for i, v in zip(dwi, diff[len(dti):]):
                w2[i] = v
            res = f(*t2, *w2)
            return tuple(res) if isinstance(res, (tuple, list)) else (res,)

        _, vjp = jax.vjp(g, *[tv[i] for i in dti], *[wv[i] for i in dwi])
        grads = vjp(tuple(cv))
        for k, i in enumerate(dti):
            gt = grads[k]
            if i in av:
                gt = gt + av[i]
            orefs[k][...] = gt
        first = pl.program_id(0) == 0
        for k, i in enumerate(dwi):
            o = orefs[len(dti) + k]
            gw = grads[len(dti) + k]

            @pl.when(first)
            def _(o=o, gw=gw):
                o[...] = gw

            @pl.when(jnp.logical_not(first))
            def _(o=o, gw=gw):
                o[...] += gw

    out_shape = [jax.ShapeDtypeStruct(tiles[i].shape, F32) for i in dti] + \
                [jax.ShapeDtypeStruct(weights[i].shape, F32) for i in dwi]
    res = pl.pallas_call(
        body, name=name, grid=(_rows(tiles[0]) // tm,),
        in_specs=[_tile_spec(a.shape, tm) for a in tiles] + [_full_spec(w.shape) for w in weights] +
                 [_tile_spec(c.shape, tm) for c in cts] + [_tile_spec(tiles[i].shape, tm) for i in acc_idx],
        out_specs=[_tile_spec(tiles[i].shape, tm) for i in dti] + [_full_spec(weights[i].shape) for i in dwi],
        out_shape=out_shape,
        compiler_params=_cp(("arbitrary",)),
    )(*tiles, *weights, *cts, *[acc[i] for i in acc_idx])
    return list(res[:len(dti)]), list(res[len(dti):])


def _ffn_wspec(l, j, rows, cols, cfirst):
    if cfirst:
        return pl.BlockSpec((1, 1, 1, rows, cols), lambda c, t: (c, l, j, 0, 0))
    return pl.BlockSpec((1, 1, 1, rows, cols), lambda t, c: (c, l, j, 0, 0))


def ffn_fwd(x, g, wg, wu, wd, l, j, tm=512):
    S = x.shape[0]

    def body(x_ref, g_ref, wg_ref, wu_ref, wd_ref, o_ref, h_ref, acc_ref):
        c = pl.program_id(1)

        @pl.when(c == 0)
        def _():
            h_ref[...] = rms(x_ref[...], g_ref[...]).astype(BF16)
            acc_ref[...] = jnp.zeros_like(acc_ref)

        h = h_ref[...]
        a = _bdot(h, wg_ref[0, 0, 0], ((1,), (0,)))
        b = _bdot(h, wu_ref[0, 0, 0], ((1,), (0,)))
        y = a * jax.nn.sigmoid(a) * b
        acc_ref[...] += _bdot(y, wd_ref[0, 0, 0], ((1,), (0,)))

        @pl.when(c == N_CHIPS - 1)
        def _():
            o_ref[...] = x_ref[...] + 0.5 * acc_ref[...]

    return pl.pallas_call(
        body, name=f"ffn_fwd_{l}{j}", grid=(S // tm, N_CHIPS),
        in_specs=[pl.BlockSpec((tm, D), lambda t, c: (t, 0)), pl.BlockSpec((1, D), lambda t, c: (0, 0)),
                  _ffn_wspec(l, j, D, FF_SHARD, False), _ffn_wspec(l, j, D, FF_SHARD, False),
                  _ffn_wspec(l, j, FF_SHARD, D, False)],
        out_specs=pl.BlockSpec((tm, D), lambda t, c: (t, 0)),
        out_shape=jax.ShapeDtypeStruct((S, D), F32),
        scratch_shapes=[pltpu.VMEM((tm, D), BF16), pltpu.VMEM((tm, D), F32)],
        compiler_params=_cp(("parallel", "arbitrary")),
    )(x, g, wg, wu, wd)


def ffn_bwd(x, g, wg, wu, wd, dout, l, j, tm=256):
    S = x.shape[0]

    def body(x_ref, g_ref, wg_ref, wu_ref, wd_ref, do_ref, dh_ref, dwg_ref, dwu_ref, dwd_ref):
        t = pl.program_id(1)
        h = rms(x_ref[...], g_ref[...]).astype(BF16)
        wgv, wuv, wdv = wg_ref[0, 0, 0], wu_ref[0, 0, 0], wd_ref[0, 0, 0]
        a = _bdot(h, wgv, ((1,), (0,)))
        b = _bdot(h, wuv, ((1,), (0,)))
        sig = jax.nn.sigmoid(a)
        s = a * sig
        dyd = 0.5 * do_ref[...]
        dy = _bdot(dyd, wdv, ((1,), (1,)))
        dwd = _bdot(s * b, dyd, ((0,), (0,)))
        db = dy * s
        da = dy * b * (sig * (1.0 + a * (1.0 - sig)))
        dwg = _bdot(h, da, ((0,), (0,)))
        dwu = _bdot(h, db, ((0,), (0,)))
        dh_ref[0] = _bdot(da, wgv, ((1,), (1,))) + _bdot(db, wuv, ((1,), (1,)))

        @pl.when(t == 0)
        def _():
            dwg_ref[0] = dwg
            dwu_ref[0] = dwu
            dwd_ref[0] = dwd

        @pl.when(t != 0)
        def _():
            dwg_ref[0] += dwg
            dwu_ref[0] += dwu
            dwd_ref[0] += dwd

    return pl.pallas_call(
        body, name=f"ffn_bwd_{l}{j}", grid=(N_CHIPS, S // tm),
        in_specs=[pl.BlockSpec((tm, D), lambda c, t: (t, 0)), pl.BlockSpec((1, D), lambda c, t: (0, 0)),
                  _ffn_wspec(l, j, D, FF_SHARD, True), _ffn_wspec(l, j, D, FF_SHARD, True),
                  _ffn_wspec(l, j, FF_SHARD, D, True), pl.BlockSpec((tm, D), lambda c, t: (t, 0))],
        out_specs=[pl.BlockSpec((1, tm, D), lambda c, t: (c, t, 0)),
                   pl.BlockSpec((1, D, FF_SHARD), lambda c, t: (c, 0, 0)),
                   pl.BlockSpec((1, D, FF_SHARD), lambda c, t: (c, 0, 0)),
                   pl.BlockSpec((1, FF_SHARD, D), lambda c, t: (c, 0, 0))],
        out_shape=[jax.ShapeDtypeStruct((N_CHIPS, S, D), F32), jax.ShapeDtypeStruct((N_CHIPS, D, FF_SHARD), F32),
                   jax.ShapeDtypeStruct((N_CHIPS, D, FF_SHARD), F32), jax.ShapeDtypeStruct((N_CHIPS, FF_SHARD, D), F32)],
        compiler_params=_cp(("parallel", "arbitrary")),
    )(x, g, wg, wu, wd, dout)


def norm_bwd(name, x, g, dh_parts, dres, tm=256):
    S = x.shape[0]
    P = dh_parts.shape[0]

    def body(x_ref, g_ref, dh_ref, dr_ref, dx_ref, dg_ref):
        dh = dh_ref[0]
        for p in range(1, P):
            dh = dh + dh_ref[p]
        _, vjp = jax.vjp(rms, x_ref[...], g_ref[...])
        dx, dg = vjp(dh)
        dx_ref[...] = dr_ref[...] + dx

        @pl.when(pl.program_id(0) == 0)
        def _():
            dg_ref[...] = dg

        @pl.when(pl.program_id(0) != 0)
        def _():
            dg_ref[...] += dg

    return pl.pallas_call(
        body, name=name, grid=(S // tm,),
        in_specs=[pl.BlockSpec((tm, D), lambda t: (t, 0)), pl.BlockSpec((1, D), lambda t: (0, 0)),
                  pl.BlockSpec((P, tm, D), lambda t: (0, t, 0)), pl.BlockSpec((tm, D), lambda t: (t, 0))],
        out_specs=[pl.BlockSpec((tm, D), lambda t: (t, 0)), pl.BlockSpec((1, D), lambda t: (0, 0))],
        out_shape=[jax.ShapeDtypeStruct((S, D), F32), jax.ShapeDtypeStruct((1, D), F32)],
        compiler_params=_cp(("arbitrary",)),
    )(x, g, dh_parts, dres)


def f_attn_sb(x, g, w):
    pr = mm(rms(x, g), w)
    return pr[:, :SB_W], pr[:, SB_W:2 * SB_W], pr[:, 2 * SB_W:]


def _pairs(y):
    return jnp.stack([y[:, 128 * j:128 * (j + 1)] for j in range(DL_PAIRS)])


def f_attn_qk(x, g, w, nrm):
    pr = mm(rms(x, g), w)
    ms = group_sum(pr * pr, DL_HEADS) * (1.0 / HEAD)
    return _pairs(pr * lax.rsqrt(ms + NORM_EPS) * jnp.concatenate([nrm] * DL_HEADS, axis=1))


def f_attn_v(x, g, w):
    return _pairs(mm(rms(x, g), w))


def _sb_tiles(q, k, qpos, kpos):
    z = _bdot(q, k, ((1,), (1,))) * (HEAD ** -0.5)
    strict = kpos < qpos
    keep = jnp.where(strict, -softplus(z), 0.0)
    return z, strict, keep


def _tri(n, upper):
    r = lax.broadcasted_iota(jnp.int32, (n, n), 0)
    c = lax.broadcasted_iota(jnp.int32, (n, n), 1)
    return ((r > c) if upper else (r < c)).astype(BF16)


def _tri_sums(xs, tri):
    x = jnp.concatenate(xs, axis=0)
    hi, lo = _split2(x)
    y = _dg(hi, tri, ((1,), (0,))) + _dg(lo, tri, ((1,), (0,)))
    n = xs[0].shape[0]
    return [y[n * i:n * (i + 1)] for i in range(len(xs))]


def sb_fwd(q, k, v, tb=QBLK):
    S = q.shape[0]
    nh = SB_W // HEAD

    def body(q_ref, k_ref, v_ref, o_ref):
        qb = pl.program_id(0)
        qpos = qb * tb + lax.broadcasted_iota(jnp.int32, (tb, tb), 0)
        col = lax.broadcasted_iota(jnp.int32, (tb, tb), 1)
        after_mat = _tri(tb, True)
        sls = [slice(HEAD * h, HEAD * (h + 1)) for h in range(nh)]
        qs = [q_ref[:, sl] for sl in sls]

        def step(i, carry):
            accs, runs = carry
            kb = qb - i
            rows = pl.ds(pl.multiple_of(kb * tb, tb), tb)
            kblk, vblk = k_ref[rows, :], v_ref[rows, :]
            tiles = [_sb_tiles(qs[h], kblk[:, sls[h]], qpos, kb * tb + col) for h in range(nh)]
            afters = _tri_sums([t[2] for t in tiles], after_mat)
            new_accs, new_runs = [], []
            for h, (z, strict, keep) in enumerate(tiles):
                w = jnp.where(strict, jnp.exp(z + keep + afters[h] + runs[h]), 0.0)
                new_accs.append(accs[h] + _bdot(w, vblk[:, sls[h]], ((1,), (0,))))
                new_runs.append(runs[h] + jnp.sum(keep, axis=1, keepdims=True))
            return tuple(new_accs), tuple(new_runs)

        init = (tuple(jnp.zeros((tb, HEAD), F32) for _ in range(nh)), tuple(jnp.zeros((tb, 1), F32) for _ in range(nh)))
        accs, _ = lax.fori_loop(0, qb + 1, step, init)
        o_ref[...] = jnp.concatenate(accs, axis=1)

    return pl.pallas_call(
        body, name="sb_fwd", grid=(S // tb,),
        in_specs=[pl.BlockSpec((tb, SB_W), lambda i: (i, 0)), pl.BlockSpec((S, SB_W), lambda i: (0, 0)),
                  pl.BlockSpec((S, SB_W), lambda i: (0, 0))],
        out_specs=pl.BlockSpec((tb, SB_W), lambda i: (i, 0)),
        out_shape=jax.ShapeDtypeStruct((S, SB_W), F32),
        compiler_params=_cp(("parallel",)),
    )(q, k, v)


def sb_bwd(q, k, v, do, tb=QBLK):
    S = q.shape[0]
    nh = SB_W // HEAD
    scale = HEAD ** -0.5

    def body(q_ref, k_ref, v_ref, do_ref, dq_ref, dk_ref, dv_ref, g_scr):
        qb = pl.program_id(0)

        @pl.when(qb == 0)
        def _():
            dk_ref[...] = jnp.zeros_like(dk_ref)
            dv_ref[...] = jnp.zeros_like(dv_ref)

        qpos = qb * tb + lax.broadcasted_iota(jnp.int32, (tb, tb), 0)
        col = lax.broadcasted_iota(jnp.int32, (tb, tb), 1)
        after_mat = _tri(tb, True)
        before_mat = _tri(tb, False)
        sls = [slice(HEAD * h, HEAD * (h + 1)) for h in range(nh)]
        qs = [q_ref[:, sl] for sl in sls]
        dos = [do_ref[:, sl] for sl in sls]

        def right_to_left(i, runs):
            kb = qb - i
            rows = pl.ds(pl.multiple_of(kb * tb, tb), tb)
            kblk, vblk = k_ref[rows, :], v_ref[rows, :]
            tiles = [_sb_tiles(qs[h], kblk[:, sls[h]], qpos, kb * tb + col) for h in range(nh)]
            afters = _tri_sums([t[2] for t in tiles], after_mat)
            dvs, new_runs = [], []
            for h, (z, strict, keep) in enumerate(tiles):
                w = jnp.where(strict, jnp.exp(z + keep + afters[h] + runs[h]), 0.0)
                g_scr[h, kb] = _bdot(dos[h], vblk[:, sls[h]], ((1,), (1,))) * w
                dvs.append(_bdot(w, dos[h], ((0,), (0,))))
                new_runs.append(runs[h] + jnp.sum(keep, axis=1, keepdims=True))
            dv_ref[rows, :] += jnp.concatenate(dvs, axis=1)
            return tuple(new_runs)

        zero_runs = tuple(jnp.zeros((tb, 1), F32) for _ in range(nh))
        lax.fori_loop(0, qb + 1, right_to_left, zero_runs)

        def left_to_right(kb, carry):
            dqs, runs = carry
            rows = pl.ds(pl.multiple_of(kb * tb, tb), tb)
            kblk = k_ref[rows, :]
            strict = (kb * tb + col) < qpos
            gws = [g_scr[h, kb] for h in range(nh)]
            befores = _tri_sums(gws, before_mat)
            new_dqs, new_runs, dks = [], [], []
            for h in range(nh):
                kh = kblk[:, sls[h]]
                sig = jax.nn.sigmoid(_bdot(qs[h], kh, ((1,), (1,))) * scale)
                dkeep = jnp.where(strict, befores[h] + runs[h], 0.0)
                dz = (gws[h] * (1.0 - sig) - dkeep * sig) * scale
                new_dqs.append(dqs[h] + _bdot(dz, kh, ((1,), (0,))))
                dks.append(_bdot(dz, qs[h], ((0,), (0,))))
                new_runs.append(runs[h] + jnp.sum(gws[h], axis=1, keepdims=True))
            dk_ref[rows, :] += jnp.concatenate(dks, axis=1)
            return tuple(new_dqs), tuple(new_runs)

        dqs, _ = lax.fori_loop(0, qb + 1, left_to_right,
                               (tuple(jnp.zeros((tb, HEAD), F32) for _ in range(nh)), zero_runs))
        dq_ref[...] = jnp.concatenate(dqs, axis=1)

    whole = pl.BlockSpec((S, SB_W), lambda i: (0, 0))
    blk = pl.BlockSpec((tb, SB_W), lambda i: (i, 0))
    return pl.pallas_call(
        body, name="sb_bwd", grid=(S // tb,),
        in_specs=[blk, whole, whole, blk], out_specs=[blk, whole, whole],
        out_shape=[jax.ShapeDtypeStruct((S, SB_W), F32)] * 3,
        scratch_shapes=[pltpu.VMEM((nh, S // tb, tb, tb), F32)],
        compiler_params=_cp(("arbitrary",)),
    )(q, k, v, do)


def reorder(name, x, groups, inverse):
    S = x.shape[1]
    out = x
    for gi, r in enumerate(groups):
        if r > 1:
            out = _reorder_call(f"{name}_{r}", x, out, gi, r, S // r, inverse)
    return out


def _reorder_call(name, x, prev, gi, r, L, inverse):
    S = x.shape[1]
    whole = pl.BlockSpec((None, S, 128), lambda p, c: (2 * gi + p, 0, 0))
    part = pl.BlockSpec((None, L, 128), lambda p, c: (2 * gi + p, c, 0))

    def body(x_ref, prev_ref, o_ref):
        c = pl.program_id(1)
        if inverse:
            o_ref[pl.ds(c, L, stride=r), :] = x_ref[...]
        else:
            o_ref[...] = x_ref[pl.ds(c, L, stride=r), :]

    return pl.pallas_call(
        body, name=name, grid=(2, r),
        in_specs=[part if inverse else whole, pl.BlockSpec(memory_space=pl.ANY)],
        out_specs=whole if inverse else part,
        out_shape=jax.ShapeDtypeStruct(x.shape, x.dtype),
        input_output_aliases={1: 0},
        compiler_params=_cp(("parallel", "arbitrary")),
    )(x, prev)


def _dil_blocks(S):
    return S // QBLK


def _dil_mask(n_in_stream):
    qi = lax.broadcasted_iota(jnp.int32, (QBLK, 2 * QBLK), 0)
    kj = lax.broadcasted_iota(jnp.int32, (QBLK, 2 * QBLK), 1) - QBLK
    dist = qi - kj
    return (dist >= 0) & (dist <= QBLK) & ((n_in_stream > 0) | (kj >= 0))


def _stream_pos(gi, i, S):
    nb = jnp.where(gi == 0, S // (QBLK * DIL[0]), jnp.where(gi == 1, S // (QBLK * DIL[1]), S // (QBLK * DIL[2])))
    return i % nb


def dil_fwd(q, k, v, bias):
    S = q.shape[1]
    nblk = _dil_blocks(S)

    def body(q_ref, kc_ref, kp_ref, vc_ref, vp_ref, b_ref, o_ref, l_ref):
        gi, i = pl.program_id(0), pl.program_id(1)
        mask = _dil_mask(_stream_pos(gi, i, S))
        for j in range(2):
            q2, kc, kp, vc, vp = q_ref[j], kc_ref[j], kp_ref[j], vc_ref[j], vp_ref[j]
            os_, ls_ = [], []
            for hh in range(2):
                sl = slice(HEAD * hh, HEAD * (hh + 1))
                kw = jnp.concatenate([kp[:, sl], kc[:, sl]], axis=0)
                vw = jnp.concatenate([vp[:, sl], vc[:, sl]], axis=0)
                lg = _bdot(q2[:, sl], kw, ((1,), (1,))) * (HEAD ** -0.5) + b_ref[2 * j + hh]
                lg = jnp.where(mask, lg, NEG_INF)
                m = jnp.max(lg, axis=-1, keepdims=True)
                p = jnp.exp(lg - m)
                den = jnp.sum(p, axis=-1, keepdims=True)
                os_.append(_bdot(p / den, vw, ((1,), (0,))))
                ls_.append(jnp.broadcast_to(m + jnp.log(den), (QBLK, HEAD)))
            o_ref[j] = jnp.concatenate(os_, axis=1)
            l_ref[j] = jnp.concatenate(ls_, axis=1)

    cur = pl.BlockSpec((2, QBLK, 128), lambda g, i: (g, i, 0))
    prev = pl.BlockSpec((2, QBLK, 128), lambda g, i: (g, jnp.maximum(i - 1, 0), 0))
    return pl.pallas_call(
        body, name="dil_fwd", grid=(len(DIL), nblk),
        in_specs=[cur, cur, prev, cur, prev, pl.BlockSpec((4, QBLK, 2 * QBLK), lambda g, i: (g, 0, 0))],
        out_specs=[cur, cur],
        out_shape=[jax.ShapeDtypeStruct(q.shape, F32)] * 2,
        compiler_params=_cp(("parallel", "parallel")),
    )(q, k, k, v, v, bias)


def dil_bwd(q, k, v, bias, o, lse, do, dlse):
    S = q.shape[1]
    nblk = _dil_blocks(S)

    def body(q_ref, kc_ref, kp_ref, vc_ref, vp_ref, b_ref, o_ref, l_ref, do_ref, dl_ref,
             dq_ref, dk_ref, dv_ref, ds_ref, dk_car, dv_car):
        gi, i = pl.program_id(0), pl.program_id(1)

        @pl.when(i == 0)
        def _():
            ds_ref[...] = jnp.zeros_like(ds_ref)
            dk_car[...] = jnp.zeros_like(dk_car)
            dv_car[...] = jnp.zeros_like(dv_car)

        @pl.when(i < nblk)
        def _():
            mask = _dil_mask(_stream_pos(gi, i, S))
            for j in range(2):
                q2, kc, kp, vc, vp = q_ref[j], kc_ref[j], kp_ref[j], vc_ref[j], vp_ref[j]
                o2, l2, do2, dl2 = o_ref[j], l_ref[j], do_ref[j], dl_ref[j]
                dqs, dkps, dkcs, dvps, dvcs = [], [], [], [], []
                for hh in range(2):
                    sl = slice(HEAD * hh, HEAD * (hh + 1))
                    qh, doh = q2[:, sl], do2[:, sl]
                    kw = jnp.concatenate([kp[:, sl], kc[:, sl]], axis=0)
                    vw = jnp.concatenate([vp[:, sl], vc[:, sl]], axis=0)
                    lg = _bdot(qh, kw, ((1,), (1,))) * (HEAD ** -0.5) + b_ref[2 * j + hh]
                    p = jnp.where(mask, jnp.exp(lg - l2[:, HEAD * hh:HEAD * hh + 1]), 0.0)
                    dp = _bdot(doh, vw, ((1,), (1,)))
                    delta = jnp.sum(doh * o2[:, sl], axis=-1, keepdims=True)
                    dl = jnp.sum(dl2[:, sl], axis=-1, keepdims=True)
                    ds = p * (dp - delta + dl)
                    ds_ref[2 * j + hh] += ds
                    dsq = ds * (HEAD ** -0.5)
                    dqs.append(_bdot(dsq, kw, ((1,), (0,))))
                    dkw = _bdot(dsq, qh, ((0,), (0,)))
                    dvw = _bdot(p, doh, ((0,), (0,)))
                    dkps.append(dkw[:QBLK])
                    dkcs.append(dkw[QBLK:])
                    dvps.append(dvw[:QBLK])
                    dvcs.append(dvw[QBLK:])
                dq_ref[j] = jnp.concatenate(dqs, axis=1)
                dk_ref[j] = dk_car[j] + jnp.concatenate(dkps, axis=1)
                dv_ref[j] = dv_car[j] + jnp.concatenate(dvps, axis=1)
                dk_car[j] = jnp.concatenate(dkcs, axis=1)
                dv_car[j] = jnp.concatenate(dvcs, axis=1)

        @pl.when(i == nblk)
        def _():
            dk_ref[...] = dk_car[...]
            dv_ref[...] = dv_car[...]

    cur = pl.BlockSpec((2, QBLK, 128), lambda g, i: (g, jnp.minimum(i, nblk - 1), 0))
    prev = pl.BlockSpec((2, QBLK, 128), lambda g, i: (g, jnp.clip(i - 1, 0, nblk - 1), 0))
    bspec = pl.BlockSpec((4, QBLK, 2 * QBLK), lambda g, i: (g, 0, 0))
    return pl.pallas_call(
        body, name="dil_bwd", grid=(len(DIL), nblk + 1),
        in_specs=[cur, cur, prev, cur, prev, bspec, cur, cur, cur, cur],
        out_specs=[cur, prev, prev, bspec],
        out_shape=[jax.ShapeDtypeStruct(q.shape, F32)] * 3 + [jax.ShapeDtypeStruct(bias.shape, F32)],
        scratch_shapes=[pltpu.VMEM((2, QBLK, 128), F32), pltpu.VMEM((2, QBLK, 128), F32)],
        compiler_params=_cp(("arbitrary", "arbitrary")),
    )(q, k, k, v, v, bias, o, lse, do, dlse)


def _t5_bucket(dist):
    max_exact = N_BUCKETS // 2
    d = jnp.maximum(dist, 1).astype(F32)
    large = max_exact + (jnp.log(d / max_exact) / math.log(MAX_DISTANCE / max_exact)
                         * (N_BUCKETS - max_exact)).astype(jnp.int32)
    large = jnp.minimum(large, N_BUCKETS - 1)
    return jnp.where(dist < max_exact, dist, large)


def _bucket_maps():
    qi = jnp.arange(QBLK)[:, None]
    kj = jnp.arange(2 * QBLK)[None, :] - QBLK
    dist = jnp.maximum(qi - kj, 0)
    return jnp.stack([_t5_bucket(dist * r) for r in DIL])


def bias_table(rel_bias, buckets):
    def body(tbl_ref, bk_ref, o_ref):
        for h in range(DL_HEADS):
            bk = bk_ref[h // 4]

            def step(b, acc):
                return jnp.where(bk == b, tbl_ref[b, h], acc)

            o_ref[h] = lax.fori_loop(0, N_BUCKETS, step, jnp.zeros(bk.shape, F32))

    return pl.pallas_call(
        body, name="bias_table", out_shape=jax.ShapeDtypeStruct((DL_HEADS,) + buckets.shape[1:], F32),
        in_specs=[pl.BlockSpec(memory_space=pltpu.SMEM), pl.BlockSpec(memory_space=pltpu.VMEM)],
        out_specs=pl.BlockSpec(memory_space=pltpu.VMEM),
    )(rel_bias, buckets)


def bias_grad(ds, buckets):
    def body(ds_ref, bk_ref, o_ref):
        lane = lax.broadcasted_iota(jnp.int32, (1, 128), 1)
        for h in range(DL_HEADS):
            dsv = ds_ref[h]
            bk = bk_ref[h // 4]

            def step(b, row):
                return jnp.where(lane == b, jnp.sum(jnp.where(bk == b, dsv, 0.0)), row)

            o_ref[h:h + 1, :] = lax.fori_loop(0, N_BUCKETS, step, jnp.zeros((1, 128), F32))

    return pl.pallas_call(
        body, name="bias_grad", out_shape=jax.ShapeDtypeStruct((DL_HEADS, 128), F32),
        in_specs=[pl.BlockSpec(memory_space=pltpu.VMEM)] * 2, out_specs=pl.BlockSpec(memory_space=pltpu.VMEM),
    )(ds, buckets)


def f_attn_out(x, oa, o, lse, w):
    og = [jnp.concatenate([o[2 * g], o[2 * g + 1]], axis=1) for g in range(3)]
    lg = [jnp.concatenate([lse[2 * g], lse[2 * g + 1]], axis=1) for g in range(3)]
    m = jnp.maximum(jnp.maximum(lg[0], lg[1]), lg[2])
    e = [jnp.exp(l - m) for l in lg]
    den = e[0] + e[1] + e[2]
    ob = (e[0] * og[0] + e[1] * og[1] + e[2] * og[2]) / den
    return x + mm(jnp.concatenate([oa, ob], axis=1), w)


def norm_shift_fwd(x, g, tm=256):
    S = x.shape[0]

    def body(x_ref, xp_ref, g_ref, h_ref, hs_ref):
        h = rms(x_ref[...], g_ref[...])
        hp = rms(xp_ref[7:8, :], g_ref[...])
        hp = jnp.where(pl.program_id(0) == 0, 0.0, hp)
        row = lax.broadcasted_iota(jnp.int32, (tm, D), 0)
        h_ref[...] = h
        hs_ref[...] = jnp.where(row == 0, hp, pltpu.roll(h, 1, 0))

    return pl.pallas_call(
        body, name="rw_norm_shift", grid=(S // tm,),
        in_specs=[pl.BlockSpec((tm, D), lambda t: (t, 0)),
                  pl.BlockSpec((8, D), lambda t: (jnp.maximum(t * (tm // 8) - 1, 0), 0)),
                  pl.BlockSpec((1, D), lambda t: (0, 0))],
        out_specs=[pl.BlockSpec((tm, D), lambda t: (t, 0))] * 2,
        out_shape=[jax.ShapeDtypeStruct((S, D), F32)] * 2,
        compiler_params=_cp(("parallel",)),
    )(x, x, g)


def norm_shift_bwd(x, g, dh, dhs, dres, tm=256):
    S = x.shape[0]
    nt = S // tm

    def body(x_ref, g_ref, dh_ref, dhs_ref, dhn_ref, dr_ref, dx_ref, dg_ref):
        t = pl.program_id(0)
        nxt = jnp.where(t == nt - 1, 0.0, dhn_ref[0:1, :])
        row = lax.broadcasted_iota(jnp.int32, (tm, D), 0)
        tot = dh_ref[...] + jnp.where(row == tm - 1, nxt, pltpu.roll(dhs_ref[...], tm - 1, 0))
        _, vjp = jax.vjp(rms, x_ref[...], g_ref[...])
        dx, dg = vjp(tot)
        dx_ref[...] = dr_ref[...] + dx

        @pl.when(t == 0)
        def _():
            dg_ref[...] = dg

        @pl.when(t != 0)
        def _():
            dg_ref[...] += dg

    tile = pl.BlockSpec((tm, D), lambda t: (t, 0))
    return pl.pallas_call(
        body, name="rw_norm_shift_bwd", grid=(nt,),
        in_specs=[tile, pl.BlockSpec((1, D), lambda t: (0, 0)), tile, tile,
                  pl.BlockSpec((8, D), lambda t: (jnp.minimum((t + 1) * (tm // 8), S // 8 - 1), 0)), tile],
        out_specs=[tile, pl.BlockSpec((1, D), lambda t: (0, 0))],
        out_shape=[jax.ShapeDtypeStruct((S, D), F32), jax.ShapeDtypeStruct((1, D), F32)],
        compiler_params=_cp(("arbitrary",)),
    )(x, g, dh, dhs, dhs, dres)


def f_rw_proj(h, hs, mix, w):
    return mm(h + (hs - h) * mix, w)


def f_rw_mid(h, hs, r, k, v, mix3, w0, a0, kkw, kaw, w1, w2, a1, a2, g1, g2):
    xx = hs - h
    xw, xa, xg = h + xx * mix3[0:1], h + xx * mix3[1:2], h + xx * mix3[2:3]
    w_log = -softplus(-(w0 + mm(jnp.tanh(mm(xw, w1)), w2))) - 0.5
    lw = -jnp.exp(w_log)
    ag = jax.nn.sigmoid(a0 + mm(mm(xa, a1), a2))
    gate = mm(jax.nn.sigmoid(mm(xg, g1)), g2)
    kk = k * kkw
    kk = kk / jnp.maximum(jnp.sqrt(group_sum(kk * kk, RW_H)), 1e-12)
    kmod = k * (1.0 + (ag - 1.0) * kaw)
    return (to_heads(r), to_heads(lw), to_heads(kmod), to_heads(v), to_heads(-kk), to_heads(kk * ag), gate)


def f_rw_post(yh, rh, kh, vh, gate, x, lng, lnb, rk, wo):
    mu = jnp.mean(yh, axis=-1, keepdims=True)
    var = jnp.mean(jnp.square(yh - mu), axis=-1, keepdims=True)
    yn = (yh - mu) * lax.rsqrt(var + GN_EPS)
    bonus = jnp.sum(rh * kh * rk, axis=-1, keepdims=True) * vh
    y = from_heads(yn) * lng + lnb + from_heads(bonus)
    return x + mm(y * gate, wo)


def _split2(x):
    hi = x.astype(BF16)
    return hi, (x - hi.astype(F32)).astype(BF16)


def _b3(x, y, cx, cy):
    dn = (((cx,), (cy,)), ((0,), (0,)))
    xh, xl = _split2(x)
    yh, yl = _split2(y)
    d = lambda p, q: lax.dot_general(p, q, dn, preferred_element_type=F32)
    return d(xh, yh) + (d(xh, yl) + d(xl, yh))


@jax.custom_vjp
def b_nt(x, y):
    return _b3(x, y, 2, 2)


@jax.custom_vjp
def b_nn(x, y):
    return _b3(x, y, 2, 1)


@jax.custom_vjp
def b_tn(x, y):
    return _b3(x, y, 1, 1)


b_nt.defvjp(lambda x, y: (b_nt(x, y), (x, y)), lambda r, g: (b_nn(g, r[1]), b_tn(g, r[0])))
b_nn.defvjp(lambda x, y: (b_nn(x, y), (x, y)), lambda r, g: (b_nt(g, r[1]), b_tn(r[0], g)))
b_tn.defvjp(lambda x, y: (b_tn(x, y), (x, y)), lambda r, g: (b_nt(r[1], g), b_nn(r[0], g)))


def _tri_apply(x, lower):
    H, C, _ = x.shape
    ii = lax.broadcasted_iota(jnp.int32, (C, C), 0)
    jj = lax.broadcasted_iota(jnp.int32, (C, C), 1)
    m = jnp.broadcast_to(((jj <= ii) if lower else (jj >= ii)).astype(BF16), (H, C, C))
    x1 = x.astype(BF16)
    r1 = x - x1.astype(F32)
    x2 = r1.astype(BF16)
    x3 = (r1 - x2.astype(F32)).astype(BF16)
    d = lambda q: lax.dot_general(m, q, (((2,), (1,)), ((0,), (0,))), preferred_element_type=F32)
    return d(x1) + (d(x2) + d(x3))


@jax.custom_vjp
def run_sum(x):
    return _tri_apply(x, True)


run_sum.defvjp(lambda x: (run_sum(x), None), lambda _, g: (_tri_apply(g, False),))


def rwkv_chunk(S0, r, lw, k, v, a, b):
    H, C, _ = r.shape
    V = S0.shape[1]
    ii = lax.broadcasted_iota(jnp.int32, (C, C), 0)
    jj = lax.broadcasted_iota(jnp.int32, (C, C), 1)
    strict = jj < ii
    i2 = lax.broadcasted_iota(jnp.int32, (C, 2 * C), 0)
    j2 = lax.broadcasted_iota(jnp.int32, (C, 2 * C), 1)
    incl2 = jnp.where(j2 >= C, j2 - C, j2) <= i2
    g = run_sum(lw)
    ig = jnp.exp(-g)
    ar = jnp.concatenate([a * jnp.exp(g - lw), r * jnp.exp(g)], axis=1)
    bk = jnp.concatenate([b * ig, k * ig], axis=1)
    m = b_nt(ar, bk)
    a_ab = jnp.where(strict, m[:, :C, :C], 0.0)
    a_ak = jnp.where(strict, m[:, :C, C:], 0.0)
    b_r = jnp.where(incl2, m[:, C:, :], 0.0)
    p = b_nt(ar, S0)
    u = p[:, :C] + b_nn(a_ak, v)
    nmat, n = a_ab, 1
    while n < C:
        n *= 2
        if n < C:
            z = b_nn(nmat, jnp.concatenate([u, nmat], axis=2))
            u, nmat = u + z[:, :, :V], z[:, :, V:]
        else:
            u = u + b_nn(nmat, u)
    uv = jnp.concatenate([u, v], axis=1)
    y = p[:, C:] + b_nn(b_r, uv)
    g_end = g[:, C - 1:C, :]
    dec = jnp.exp(g_end - g)
    s_new = S0 * jnp.exp(g_end) + b_tn(uv, jnp.concatenate([b * dec, k * dec], axis=1))
    return y, s_new


def rwkv_fwd(r, lw, k, v, a, b):
    H, S, _ = r.shape
    C = RW_CHUNK

    def body(r_ref, lw_ref, k_ref, v_ref, a_ref, b_ref, y_ref, s_ref, s_scr):
        @pl.when(pl.program_id(0) == 0)
        def _():
            s_scr[...] = jnp.zeros_like(s_scr)

        s0 = s_scr[...]
        s_ref[0] = s0
        y, s1 = rwkv_chunk(s0, r_ref[...], lw_ref[...], k_ref[...], v_ref[...], a_ref[...], b_ref[...])
        y_ref[...] = y
        s_scr[...] = s1

    bs = pl.BlockSpec((H, C, HEAD), lambda c: (0, c, 0))
    return pl.pallas_call(
        body, name="rwkv_fwd", grid=(S // C,), in_specs=[bs] * 6,
        out_specs=[bs, pl.BlockSpec((1, H, HEAD, HEAD), lambda c: (c, 0, 0, 0))],
        out_shape=[jax.ShapeDtypeStruct((H, S, HEAD), F32), jax.ShapeDtypeStruct((S // C, H, HEAD, HEAD), F32)],
        scratch_shapes=[pltpu.VMEM((H, HEAD, HEAD), F32)],
        compiler_params=_cp(("arbitrary",)),
    )(r, lw, k, v, a, b)


def rwkv_bwd(r, lw, k, v, a, b, states, dy):
    H, S, _ = r.shape
    C = RW_CHUNK
    nc = S // C

    def body(r_ref, lw_ref, k_ref, v_ref, a_ref, b_ref, s_ref, dy_ref, dr, dlw, dk, dv, da, db, ds_scr):
        @pl.when(pl.program_id(0) == 0)
        def _():
            ds_scr[...] = jnp.zeros_like(ds_scr)

        _, vjp = jax.vjp(rwkv_chunk, s_ref[0], r_ref[...], lw_ref[...], k_ref[...], v_ref[...], a_ref[...], b_ref[...])
        grads = vjp((dy_ref[...], ds_scr[...]))
        ds_scr[...] = grads[0]
        for o, gv in zip((dr, dlw, dk, dv, da, db), grads[1:]):
            o[...] = gv

    bs = pl.BlockSpec((H, C, HEAD), lambda c: (0, nc - 1 - c, 0))
    return pl.pallas_call(
        body, name="rwkv_bwd", grid=(nc,),
        in_specs=[bs] * 6 + [pl.BlockSpec((1, H, HEAD, HEAD), lambda c: (nc - 1 - c, 0, 0, 0)), bs],
        out_specs=[bs] * 6, out_shape=[jax.ShapeDtypeStruct((H, S, HEAD), F32)] * 6,
        scratch_shapes=[pltpu.VMEM((H, HEAD, HEAD), F32)],
        compiler_params=_cp(("arbitrary",)),
    )(r, lw, k, v, a, b, states, dy)


def loss_head(y, target, tm=512):
    S = y.shape[0]

    def body(y_ref, t_ref, dy_ref, l_ref):
        e = y_ref[...] - t_ref[...]
        dy_ref[...] = e * (1.0 / D)
        part = jnp.broadcast_to(0.5 * jnp.sum(jnp.mean(e * e, axis=-1, keepdims=True)), (1, 128))

        @pl.when(pl.program_id(0) == 0)
        def _():
            l_ref[...] = part

        @pl.when(pl.program_id(0) != 0)
        def _():
            l_ref[...] += part

    tile = pl.BlockSpec((tm, D), lambda t: (t, 0))
    return pl.pallas_call(
        body, name="loss_head", grid=(S // tm,), in_specs=[tile, tile],
        out_specs=[tile, pl.BlockSpec((1, 128), lambda t: (0, 0))],
        out_shape=[jax.ShapeDtypeStruct((S, D), F32), jax.ShapeDtypeStruct((1, 128), F32)],
        compiler_params=_cp(("arbitrary",)),
    )(y, target)


def _row_tile(rows, cols, budget=1 << 19):
    best = None
    for tr in range(8, rows + 1, 8):
        if rows % tr == 0 and tr * cols <= budget:
            best = tr
    return best or rows


def _adam(w, g, m, v):
    m = ADAM_B1 * m + (1.0 - ADAM_B1) * g
    v = ADAM_B2 * v + (1.0 - ADAM_B2) * jnp.square(g)
    m_hat = m / (1.0 - ADAM_B1 ** ADAM_STEP)
    v_hat = v / (1.0 - ADAM_B2 ** ADAM_STEP)
    return -ADAM_LR * (m_hat / (jnp.sqrt(v_hat) + ADAM_EPS) + ADAM_WD * w), m, v


def sum_slots(name, parts, dtype=F32, extras=()):
    n = 0 if parts is None else parts.shape[0]
    R, C = extras[0].shape if parts is None else parts.shape[1:]
    tr = _row_tile(R, C * (n + len(extras)))
    ins = ([] if parts is None else [parts]) + list(extras)

    def body(*refs):
        terms = [] if parts is None else [refs[0][i] for i in range(n)]
        terms += [r[...] for r in refs[len(ins) - len(extras):len(ins)]]
        s = terms[0].astype(F32)
        for t in terms[1:]:
            s = s + t.astype(F32)
        refs[len(ins)][...] = s.astype(dtype)

    tile = pl.BlockSpec((tr, C), lambda t: (t, 0))
    return pl.pallas_call(
        body, name=name, grid=(R // tr,),
        in_specs=([] if parts is None else [pl.BlockSpec((n, tr, C), lambda t: (0, t, 0))]) + [tile] * len(extras),
        out_specs=tile, out_shape=jax.ShapeDtypeStruct((R, C), dtype), compiler_params=_cp(("parallel",)),
    )(*ins)


def adam_step(name, ga, gb, w, m, v):
    R, C = w.shape
    tr = _row_tile(R, C, 1 << 17)
    ins = [ga] + ([gb] if gb is not None else []) + [w, m, v]

    def body(*refs):
        g = refs[0][...]
        if gb is not None:
            g = g + refs[1][...]
        w_ref, m_ref, v_ref, g_out, d_out, m_out, v_out = refs[len(ins) - 3:]
        d, m2, v2 = _adam(w_ref[...], g, m_ref[...], v_ref[...])
        g_out[...] = g
        d_out[...] = d
        m_out[...] = m2
        v_out[...] = v2

    tile = pl.BlockSpec((tr, C), lambda t: (t, 0))
    return pl.pallas_call(
        body, name=name, grid=(R // tr,), in_specs=[tile] * len(ins), out_specs=[tile] * 4,
        out_shape=[jax.ShapeDtypeStruct((R, C), F32)] * 4, compiler_params=_cp(("parallel",)),
    )(*ins)


def _place():
    return lax.axis_index("x"), lax.axis_index("y"), lax.axis_index("c")


def _flip(me, mask):
    return tuple(1 - v if mk else v for v, mk in zip(me, mask))


CHIP_MASKS = ((1, 0, 0), (0, 1, 0), (1, 1, 0))
ALL_MASKS = tuple((a, b, c) for a in (0, 1) for b in (0, 1) for c in (0, 1) if (a, b, c) != (0, 0, 0))


def _chip(dev):
    return 2 * dev[0] + dev[1]


def _devno(dev):
    return 4 * dev[0] + 2 * dev[1] + dev[2]


def exchange(name, arrays, out_shapes, masks, copies, src_of, dst_of, local_of, alias=False):
    n, npeer = len(arrays), len(masks)
    nloc = 1

    def body(*refs):
        ins, outs = refs[:n], refs[n:2 * n]
        send_sems, recv_sems, local_sems = refs[2 * n:]
        me = _place()
        peers = [_flip(me, mk) for mk in masks]
        locals_ = []
        for i in range(n):
            for q, (src, dst) in enumerate(local_of(ins[i], outs[i], me)):
                cp = pltpu.make_async_copy(src, dst, local_sems.at[i * nloc + q])
                cp.start()
                locals_.append(cp)
        sends = []
        for i in range(n):
            for j, peer in enumerate(peers):
                srcs, dsts = src_of(ins[i], me, j), dst_of(outs[i], me, j)
                for q in range(copies):
                    sem = (i * npeer + j) * copies + q
                    cp = pltpu.make_async_remote_copy(
                        src_ref=srcs[q], dst_ref=dsts[q], send_sem=send_sems.at[sem], recv_sem=recv_sems.at[sem],
                        device_id=peer, device_id_type=MESH)
                    cp.start()
                    sends.append(cp)
        for i in range(n):
            for j, peer in enumerate(peers):
                lands = dst_of(outs[i], peer, j)
                for q in range(copies):
                    sem = (i * npeer + j) * copies + q
                    pltpu.make_async_remote_copy(
                        src_ref=lands[q], dst_ref=lands[q], send_sem=send_sems.at[sem], recv_sem=recv_sems.at[sem],
                        device_id=peer, device_id_type=MESH).wait_recv()
        for cp in sends:
            cp.wait_send()
        for cp in locals_:
            cp.wait()

    hbm = pl.BlockSpec(memory_space=pl.ANY)
    return pl.pallas_call(
        body, name=name, in_specs=[hbm] * n, out_specs=[hbm] * n, out_shape=list(out_shapes),
        scratch_shapes=[pltpu.SemaphoreType.DMA((n * npeer * copies,)), pltpu.SemaphoreType.DMA((n * npeer * copies,)),
                        pltpu.SemaphoreType.DMA((n * nloc,))],
        input_output_aliases={i: i for i in range(n)} if alias else {},
    )(*arrays)


def _half(c, rows):
    return pl.ds(c * (rows // 2), rows // 2)


def gather_chips(arrays):
    outs = [jax.ShapeDtypeStruct((N_CHIPS,) + a.shape, a.dtype) for a in arrays]
    sib = len(CHIP_MASKS)
    got = exchange("gather_weights", arrays, outs, CHIP_MASKS + ((0, 0, 1),), 1,
                   src_of=lambda r, me, j: [r] if j == sib else [r.at[_half(me[2], r.shape[0])]],
                   dst_of=lambda o, sender, j: [o.at[_chip(sender)]] if j == sib else
                   [o.at[_chip(sender), _half(sender[2], o.shape[1])]],
                   local_of=lambda r, o, me: [])
    return exchange("gather_swap", got, outs, ((0, 0, 1),), len(CHIP_MASKS),
                    src_of=lambda r, me, j: [r.at[_chip(_flip(me, mk)), _half(me[2], r.shape[1])] for mk in CHIP_MASKS],
                    dst_of=lambda o, sender, j: [o.at[_chip(_flip(sender, mk)), _half(sender[2], o.shape[1])]
                                                 for mk in CHIP_MASKS],
                    local_of=lambda r, o, me: [], alias=True)


def reduce_chips(names, arrays, wire):
    x, y, c = _place()
    split = [a.reshape(N_CHIPS, 2, a.shape[1] // 2, a.shape[2]) for a in arrays]
    half_shapes = [jax.ShapeDtypeStruct((N_CHIPS,) + a.shape[2:], F32) for a in split]
    theirs = exchange("grad_pre_swap", split, half_shapes, ((0, 0, 1),), 1,
                      src_of=lambda r, me, j: [r.at[:, 1 - me[2]]], dst_of=lambda o, sender, j: [o],
                      local_of=lambda r, o, me: [])
    chip_sum = []
    for nm, a, t, dt in zip(names, split, theirs, wire):
        own = lax.dynamic_index_in_dim(a, c, axis=1, keepdims=False)
        flat = lambda v: v.reshape(-1, v.shape[-1])
        chip_sum.append(sum_slots(f"sum2_{nm}", None, dt, [flat(own), flat(t)]).reshape(t.shape))
    landed = exchange("scatter_grads", chip_sum,
                      [jax.ShapeDtypeStruct((len(CHIP_MASKS),) + a.shape[1:], a.dtype) for a in chip_sum], CHIP_MASKS, 1,
                      src_of=lambda r, me, j: [r.at[_chip(_flip(me, CHIP_MASKS[j]))]],
                      dst_of=lambda o, sender, j: [o.at[j]], local_of=lambda r, o, me: [])
    halves = [sum_slots(f"sum4_{nm}", p, F32, [lax.dynamic_index_in_dim(a, _chip((x, y, c)), axis=0, keepdims=False)])
              for nm, p, a in zip(names, landed, chip_sum)]
    others = exchange("grad_final_swap", halves, [jax.ShapeDtypeStruct(a.shape, F32) for a in halves], ((0, 0, 1),), 1,
                      src_of=lambda r, me, j: [r], dst_of=lambda o, sender, j: [o], local_of=lambda r, o, me: [])
    return [jnp.concatenate([jnp.where(c == 0, h, o), jnp.where(c == 0, o, h)], axis=0) for h, o in zip(halves, others)]


def gather_all(arrays):
    outs = [jax.ShapeDtypeStruct((8,) + a.shape, a.dtype) for a in arrays]
    return exchange("gather_replicated", arrays, outs, ALL_MASKS, 1,
                    src_of=lambda r, me, peer: [r],
                    dst_of=lambda o, sender, j: [o.at[_devno(sender)]],
                    local_of=lambda r, o, me: [(r, o.at[_devno(me)])])


def _unshard_cols(g):
    return jnp.transpose(g, (1, 0, 2)).reshape(g.shape[1], -1)


def _shard_cols(a):
    return jnp.transpose(a.reshape(a.shape[0], N_CHIPS, -1), (1, 0, 2))


def _forward_backward(x, tgt, W):
    S = x.shape[0]
    G = {}
    sd = jax.ShapeDtypeStruct

    def ffn(xin, l, j):
        return ffn_fwd(xin, W["ffn_norm"][l][j], W["ffn_w_gate"], W["ffn_w_up"], W["ffn_w_down"], l, j)

    def ffn_back(xin, dout, l, j):
        gn = W["ffn_norm"][l][j]
        dh, dwg, dwu, dwd = ffn_bwd(xin, gn, W["ffn_w_gate"], W["ffn_w_up"], W["ffn_w_down"], dout, l, j)
        dx, dg = norm_bwd(f"ffn_norm_bwd_{l}{j}", xin, gn, dh, dout)
        G[("ffn", l, j)] = (dg, dwg, dwu, dwd)
        return dx

    x0 = x
    x1 = ffn(x0, 0, 0)
    g0 = W["mix_norm"][0]
    sbq, sbk, sbv = tile_fwd(f_attn_sb, "attn_in_sb", [x1], [g0, W["attn_w_in"][0]], [sd((S, SB_W), F32)] * 3, 256)
    dl_shape = sd((DL_PAIRS, S, 128), F32)
    qn, = tile_fwd(f_attn_qk, "attn_in_q", [x1], [g0, W["attn_w_in"][1], W["attn_q_norm"]], [dl_shape], 256)
    kn, = tile_fwd(f_attn_qk, "attn_in_k", [x1], [g0, W["attn_w_in"][2], W["attn_k_norm"]], [dl_shape], 256)
    vv, = tile_fwd(f_attn_v, "attn_in_v", [x1], [g0, W["attn_w_in"][3]], [dl_shape], 256)
    oa = sb_fwd(sbq, sbk, sbv)
    qs, ks, vs = (reorder(nm, t, DIL, False) for nm, t in (("sub_q", qn), ("sub_k", kn), ("sub_v", vv)))
    o_s, lse_s = dil_fwd(qs, ks, vs, W["bias_mat"])
    o_n, lse_n = reorder("nat_o", o_s, DIL, True), reorder("nat_lse", lse_s, DIL, True)
    x2, = tile_fwd(f_attn_out, "attn_out", [x1, oa, o_n, lse_n], [W["attn_w_out"]], [sd((S, D), F32)], 256)
    x3 = ffn(x2, 0, 1)
    x4 = ffn(x3, 1, 0)
    g1 = W["mix_norm"][1]
    h, hs = norm_shift_fwd(x4, g1)
    mix = W["rw_mix"]
    r, = tile_fwd(f_rw_proj, "rw_proj_r", [h, hs], [mix[0:1], W["rw_wr"]], [sd((S, D), F32)], 256)
    k, = tile_fwd(f_rw_proj, "rw_proj_k", [h, hs], [mix[2:3], W["rw_wk"]], [sd((S, D), F32)], 256)
    v, = tile_fwd(f_rw_proj, "rw_proj_v", [h, hs], [mix[3:4], W["rw_wv"]], [sd((S, D), F32)], 256)
    mix3 = jnp.concatenate([mix[1:2], mix[4:5], mix[5:6]], axis=0)
    mid_w = [mix3, W["rw_w0"], W["rw_a0"], W["rw_kk"], W["rw_ka"], W["rw_w1"], W["rw_w2"], W["rw_a1"], W["rw_a2"],
             W["rw_g1"], W["rw_g2"]]
    hshape = sd((RW_H, S, HEAD), F32)
    mid_tiles = [h, hs, r, k, v]
    rh, lwh, kh, vh, ah, bh, gate = tile_fwd(f_rw_mid, "rw_mid", mid_tiles, mid_w, [hshape] * 6 + [sd((S, D), F32)], 128)
    yh, states = rwkv_fwd(rh, lwh, kh, vh, ah, bh)
    post_w = [W["rw_lnx_g"], W["rw_lnx_b"], W["rw_rk"], W["rw_wo"]]
    post_tiles = [yh, rh, kh, vh, gate, x4]
    x5, = tile_fwd(f_rw_post, "rw_post", post_tiles, post_w, [sd((S, D), F32)], 128)
    x6 = ffn(x5, 1, 1)
    dx6, loss_part = loss_head(x6, tgt)

    dx5 = ffn_back(x5, dx6, 1, 1)
    (dyh, drh, dkh, dvh, dgate, dx4), (d_lng, d_lnb, d_rk, d_wo) = tile_bwd(
        f_rw_post, "rw_post_bwd", post_tiles, post_w, [dx5], 128, [True] * 6, [True] * 4)
    drh2, dlwh, dkh2, dvh2, dah, dbh = rwkv_bwd(rh, lwh, kh, vh, ah, bh, states, dyh)
    mid_cts = [drh + drh2, dlwh, dkh + dkh2, dvh + dvh2, dah, dbh, dgate]
    (dh, dhs, dr, dk, dv), dmid_w = tile_bwd(f_rw_mid, "rw_mid_bwd", mid_tiles, mid_w, mid_cts, 128,
                                             [True] * 5, [True] * len(mid_w))
    dmix = {}
    for nm, ct, row, wname in (("r", dr, 0, "rw_wr"), ("k", dk, 2, "rw_wk"), ("v", dv, 3, "rw_wv")):
        (dh, dhs), (dmix[row], G[wname]) = tile_bwd(
            f_rw_proj, f"rw_proj_{nm}_bwd", [h, hs], [mix[row:row + 1], W[wname]], [ct], 256,
            [True, True], [True, True], acc={0: dh, 1: dhs})
    dx4, G[("mix_norm", 1)] = norm_shift_bwd(x4, g1, dh, dhs, dx4)
    dmix3 = dmid_w[0]
    G["rw_mix"] = jnp.concatenate([dmix[0], dmix3[0:1], dmix[2], dmix[3], dmix3[1:2], dmix3[2:3]], axis=0)
    for nm, gv in zip(("rw_w0", "rw_a0", "rw_kk", "rw_ka", "rw_w1", "rw_w2", "rw_a1", "rw_a2", "rw_g1", "rw_g2"), dmid_w[1:]):
        G[nm] = gv
    G["rw_lnx_g"], G["rw_lnx_b"], G["rw_rk"], G["rw_wo"] = d_lng, d_lnb, d_rk, d_wo
    dx3 = ffn_back(x3, dx4, 1, 0)
    dx2 = ffn_back(x2, dx3, 0, 1)
    (dx1, doa, do_n, dlse_n), (G["attn_w_out"],) = tile_bwd(
        f_attn_out, "attn_out_bwd", [x1, oa, o_n, lse_n], [W["attn_w_out"]], [dx2], 256, [True] * 4, [True])
    do_s, dlse_s = reorder("sub_do", do_n, DIL, False), reorder("sub_dlse", dlse_n, DIL, False)
    dqs, dks, dvs, dsum = dil_bwd(qs, ks, vs, W["bias_mat"], o_s, lse_s, do_s, dlse_s)
    G["rel_bias"] = bias_grad(dsum, W["buckets"])
    dqn, dkn, dvv = (reorder(nm, t, DIL, True) for nm, t in (("nat_dq", dqs), ("nat_dk", dks), ("nat_dv", dvs)))
    dsbq, dsbk, dsbv = sb_bwd(sbq, sbk, sbv, doa)
    dg0 = []
    dwin = []
    (dx1,), (dg, dw) = tile_bwd(f_attn_sb, "attn_in_sb_bwd", [x1], [g0, W["attn_w_in"][0]], [dsbq, dsbk, dsbv], 256,
                                [True], [True, True], acc={0: dx1})
    dg0.append(dg), dwin.append(dw)
    (dx1,), (dg, dw, G["attn_q_norm"]) = tile_bwd(f_attn_qk, "attn_in_q_bwd", [x1], [g0, W["attn_w_in"][1], W["attn_q_norm"]],
                                                  [dqn], 256, [True], [True] * 3, acc={0: dx1})
    dg0.append(dg), dwin.append(dw)
    (dx1,), (dg, dw, G["attn_k_norm"]) = tile_bwd(f_attn_qk, "attn_in_k_bwd", [x1], [g0, W["attn_w_in"][2], W["attn_k_norm"]],
                                                  [dkn], 256, [True], [True] * 3, acc={0: dx1})
    dg0.append(dg), dwin.append(dw)
    (dx1,), (dg, dw) = tile_bwd(f_attn_v, "attn_in_v_bwd", [x1], [g0, W["attn_w_in"][3]], [dvv], 256,
                                [True], [True, True], acc={0: dx1})
    dg0.append(dg), dwin.append(dw)
    G[("mix_norm", 0)] = dg0
    G["attn_w_in"] = dwin
    dx0 = ffn_back(x0, dx1, 0, 0)
    return loss_part, dx0, G


VEC_ROWS = ("ffn_norm", "rw_mix", "rw_w0", "rw_a0", "rw_kk", "rw_ka", "rw_lnx_g", "rw_lnx_b")


def kernel(x, ffn_norm, ffn_w_gate, ffn_w_up, ffn_w_down, mix_norm, rel_bias, attn_w_in, attn_q_norm, attn_k_norm, attn_w_out, rw_mix, rw_w0, rw_w1, rw_w2, rw_a0, rw_a1, rw_a2, rw_g1, rw_g2, rw_kk, rw_ka, rw_rk, rw_wr, rw_wk, rw_wv, rw_wo, rw_lnx_g, rw_lnx_b, loss_target, m_ffn_norm, m_ffn_w_gate, m_ffn_w_up, m_ffn_w_down, m_mix_norm, m_rel_bias, m_attn_w_in, m_attn_q_norm, m_attn_k_norm, m_attn_w_out, m_rw_mix, m_rw_w0, m_rw_w1, m_rw_w2, m_rw_a0, m_rw_a1, m_rw_a2, m_rw_g1, m_rw_g2, m_rw_kk, m_rw_ka, m_rw_rk, m_rw_wr, m_rw_wk, m_rw_wv, m_rw_wo, m_rw_lnx_g, m_rw_lnx_b, v_ffn_norm, v_ffn_w_gate, v_ffn_w_up, v_ffn_w_down, v_mix_norm, v_rel_bias, v_attn_w_in, v_attn_q_norm, v_attn_k_norm, v_attn_w_out, v_rw_mix, v_rw_w0, v_rw_w1, v_rw_w2, v_rw_a0, v_rw_a1, v_rw_a2, v_rw_g1, v_rw_g2, v_rw_kk, v_rw_ka, v_rw_rk, v_rw_wr, v_rw_wk, v_rw_wv, v_rw_wo, v_rw_lnx_g, v_rw_lnx_b):
    names = ["ffn_norm", "ffn_w_gate", "ffn_w_up", "ffn_w_down", "mix_norm", "rel_bias", "attn_w_in", "attn_q_norm",
             "attn_k_norm", "attn_w_out", "rw_mix", "rw_w0", "rw_w1", "rw_w2", "rw_a0", "rw_a1", "rw_a2", "rw_g1", "rw_g2",
             "rw_kk", "rw_ka", "rw_rk", "rw_wr", "rw_wk", "rw_wv", "rw_wo", "rw_lnx_g", "rw_lnx_b"]
    loc = locals()
    w = {n: loc[n] for n in names}
    mom = {n: loc["m_" + n] for n in names}
    vel = {n: loc["v_" + n] for n in names}
    S = x.shape[1]

    vec_shard = jnp.concatenate([w[n].reshape(-1, 256) for n in VEC_ROWS], axis=0)
    mats = ["ffn_w_gate", "ffn_w_up", "ffn_w_down", "attn_w_in", "attn_w_out", "rw_w1", "rw_w2", "rw_a1", "rw_a2",
            "rw_g1", "rw_g2", "rw_wr", "rw_wk", "rw_wv", "rw_wo"]
    send = [vec_shard] + [w[n].reshape(-1, w[n].shape[-1]).astype(BF16) for n in mats]
    got = gather_chips(send)
    vec_full = _unshard_cols(got[0])
    gm = dict(zip(mats, got[1:]))
    W = {
        "ffn_norm": [[vec_full[2 * l + j][None] for j in range(2)] for l in range(2)],
        "ffn_w_gate": gm["ffn_w_gate"].reshape(N_CHIPS, 2, 2, D, FF_SHARD),
        "ffn_w_up": gm["ffn_w_up"].reshape(N_CHIPS, 2, 2, D, FF_SHARD),
        "ffn_w_down": gm["ffn_w_down"].reshape(N_CHIPS, 2, 2, FF_SHARD, D),
        "mix_norm": [mix_norm[0:1], mix_norm[1:2]],
        "attn_w_in": [gm["attn_w_in"][p] for p in range(N_CHIPS)],
        "attn_q_norm": attn_q_norm, "attn_k_norm": attn_k_norm,
        "attn_w_out": _unshard_cols(gm["attn_w_out"]),
        "rw_mix": vec_full[4:10],
        "rw_w1": gm["rw_w1"].reshape(D, -1), "rw_a1": gm["rw_a1"].reshape(D, -1), "rw_g1": gm["rw_g1"].reshape(D, -1),
        "rw_w2": _unshard_cols(gm["rw_w2"]), "rw_a2": _unshard_cols(gm["rw_a2"]), "rw_g2": _unshard_cols(gm["rw_g2"]),
        "rw_wr": gm["rw_wr"].reshape(D, D), "rw_wk": gm["rw_wk"].reshape(D, D), "rw_wv": gm["rw_wv"].reshape(D, D),
        "rw_wo": gm["rw_wo"].reshape(D, D),
        "rw_rk": rw_rk[0][:, None, :],
    }
    for i, n in enumerate(("rw_w0", "rw_a0", "rw_kk", "rw_ka", "rw_lnx_g", "rw_lnx_b")):
        W[n] = vec_full[10 + i][None]
    buckets = _bucket_maps()
    W["buckets"] = buckets
    W["bias_mat"] = bias_table(rel_bias, buckets)

    loss_part, dx, G = _forward_backward(x[0], loss_target[0], W)
    loss = lax.psum(loss_part[0, 0], ("x", "y", "c"))

    def ffn_stack(idx):
        return jnp.stack([jnp.stack([G[("ffn", l, j)][idx] for j in range(2)], axis=1) for l in range(2)], axis=1)

    vec_rows = [G[("ffn", l, j)][0] for l in range(2) for j in range(2)] + [G["rw_mix"]] + \
               [G[n] for n in ("rw_w0", "rw_a0", "rw_kk", "rw_ka", "rw_lnx_g", "rw_lnx_b")]
    full = {
        "vec": _shard_cols(jnp.concatenate(vec_rows, axis=0)),
        "ffn_w_gate": ffn_stack(1), "ffn_w_up": ffn_stack(2), "ffn_w_down": ffn_stack(3),
        "attn_w_in": jnp.stack(G["attn_w_in"]),
        "attn_w_out": _shard_cols(G["attn_w_out"]),
        "rw_w1": G["rw_w1"].reshape(N_CHIPS, 256, -1), "rw_a1": G["rw_a1"].reshape(N_CHIPS, 256, -1),
        "rw_g1": G["rw_g1"].reshape(N_CHIPS, 256, -1),
        "rw_w2": _shard_cols(G["rw_w2"]), "rw_a2": _shard_cols(G["rw_a2"]), "rw_g2": _shard_cols(G["rw_g2"]),
        "rw_wr": G["rw_wr"].reshape(N_CHIPS, 256, D), "rw_wk": G["rw_wk"].reshape(N_CHIPS, 256, D),
        "rw_wv": G["rw_wv"].reshape(N_CHIPS, 256, D), "rw_wo": G["rw_wo"].reshape(N_CHIPS, 256, D),
    }
    order = ["vec"] + mats
    summed = reduce_chips(order, [full[n].reshape(N_CHIPS, -1, full[n].shape[-1]) for n in order],
                          [F32] + [BF16] * len(mats))

    rep = jnp.concatenate([G[("mix_norm", 0)][0] + G[("mix_norm", 0)][1] + G[("mix_norm", 0)][2] + G[("mix_norm", 0)][3],
                           G[("mix_norm", 1)]], axis=0).reshape(16, 128)
    rep = jnp.concatenate([rep, G["rel_bias"], jnp.pad(G["attn_q_norm"], ((0, 0), (0, 64))),
                           jnp.pad(G["attn_k_norm"], ((0, 0), (0, 64))), G["rw_rk"].reshape(8, 128),
                           jnp.zeros((2, 128), F32)], axis=0)
    rep_sum = sum_slots("sum_replicated", gather_all([rep])[0])
    g_rep = {
        "mix_norm": rep_sum[0:16].reshape(2, D),
        "rel_bias": jnp.transpose(rep_sum[16:28, :N_BUCKETS]),
        "attn_q_norm": rep_sum[28:29, :HEAD], "attn_k_norm": rep_sum[29:30, :HEAD],
        "rw_rk": rep_sum[30:38].reshape(1, RW_H, HEAD),
    }

    out = {}

    def adam(n, ga, gb):
        shp = w[n].shape
        to2 = lambda a: a.reshape(-1, shp[-1])
        res = adam_step(f"adam_{n}", to2(ga), None if gb is None else to2(gb), to2(w[n]), to2(mom[n]), to2(vel[n]))
        out[n] = tuple(r.reshape(shp) for r in res)

    part = dict(zip(order, summed))
    for n in mats:
        adam(n, part[n], None)
    rows = {"ffn_norm": (0, 4), "rw_mix": (4, 10), "rw_w0": (10, 11), "rw_a0": (11, 12), "rw_kk": (12, 13),
            "rw_ka": (13, 14), "rw_lnx_g": (14, 15), "rw_lnx_b": (15, 16)}
    for n, (lo, hi) in rows.items():
        adam(n, part["vec"][lo:hi], None)
    for n, gv in g_rep.items():
        adam(n, gv, None)

    grads = [out[n][0] for n in names]
    deltas = [out[n][1] for n in names]
    new_m = [out[n][2] for n in names]
    new_v = [out[n][3] for n in names]
    return (loss, dx[None], *grads, *deltas, *new_m, *new_v)
```

```python
import functools
import math

import jax
import jax.numpy as jnp
from jax import lax
from jax.experimental import pallas as pl
from jax.experimental.pallas import tpu as pltpu

F32, BF16 = jnp.float32, jnp.bfloat16
HI = lax.Precision.HIGHEST
MESH = pl.DeviceIdType.MESH

D = 1024
HEAD = 64
N_CHIPS = 4
FF_SHARD = 704
SB_W = 256
DL_HEADS = 12
DL_PAIRS = 6
DIL = (1, 4, 16)
QBLK = 128
N_BUCKETS = 32
MAX_DISTANCE = 2048
RW_H = 16
RW_CHUNK = 64
NORM_EPS = 1e-6
GN_EPS = 64e-5
NEG_INF = -1e30
VMEM_LIMIT = 56 * 1024 * 1024

ADAM_LR, ADAM_B1, ADAM_B2, ADAM_EPS, ADAM_WD, ADAM_STEP = 0.001, 0.9, 0.999, 1e-08, 0.01, 10


def _cp(sem):
    return pltpu.CompilerParams(dimension_semantics=sem, vmem_limit_bytes=VMEM_LIMIT)


def _dg(a, b, dims, prec=None):
    return lax.dot_general(a, b, (dims, ((), ())), precision=prec, preferred_element_type=F32)


def _bdot(a, b, dims):
    return _dg(a.astype(BF16), b.astype(BF16), dims)


@jax.custom_vjp
def mm(a, b):
    return _bdot(a, b, ((1,), (0,)))


def _mm_fwd(a, b):
    return _bdot(a, b, ((1,), (0,))), (a, b)


def _mm_bwd(res, g):
    a, b = res
    return _bdot(g, b, ((1,), (1,))), _bdot(a, g, ((0,), (0,)))


mm.defvjp(_mm_fwd, _mm_bwd)


def rms(x, g):
    return x * lax.rsqrt(jnp.mean(x * x, axis=-1, keepdims=True) + NORM_EPS) * g


def group_sum(x, nh):
    w = x.shape[-1]
    e = (lax.broadcasted_iota(jnp.int32, (w, nh), 0) // HEAD == lax.broadcasted_iota(jnp.int32, (w, nh), 1)).astype(F32)
    s = _dg(x, e, ((1,), (0,)), HI)
    return _dg(s, e, ((1,), (1,)), HI)


def softplus(u):
    return jnp.maximum(u, 0.0) + jnp.log1p(jnp.exp(-jnp.abs(u)))


def to_heads(t, nh=RW_H):
    return jnp.stack([t[:, HEAD * h:HEAD * (h + 1)] for h in range(nh)])


def from_heads(t):
    return jnp.concatenate([t[h] for h in range(t.shape[0])], axis=-1)


def _tile_spec(shape, tm):
    if len(shape) == 2:
        return pl.BlockSpec((tm, shape[1]), lambda t: (t, 0))
    return pl.BlockSpec((shape[0], tm, shape[2]), lambda t: (0, t, 0))


def _full_spec(shape):
    nd = len(shape)
    return pl.BlockSpec(tuple(shape), lambda t: (0,) * nd)


def _rows(a):
    return a.shape[0] if a.ndim == 2 else a.shape[1]


def tile_fwd(f, name, tiles, weights, outs, tm):
    nt, nw = len(tiles), len(weights)

    def body(*refs):
        tv = [r[...] for r in refs[:nt]]
        wv = [r[...].astype(F32) for r in refs[nt:nt + nw]]
        res = f(*tv, *wv)
        if not isinstance(res, (tuple, list)):
            res = (res,)
        for o, v in zip(refs[nt + nw:], res):
            o[...] = v.astype(o.dtype)

    return pl.pallas_call(
        body, name=name, grid=(_rows(tiles[0]) // tm,),
        in_specs=[_tile_spec(a.shape, tm) for a in tiles] + [_full_spec(w.shape) for w in weights],
        out_specs=[_tile_spec(o.shape, tm) for o in outs],
        out_shape=list(outs),
        compiler_params=_cp(("parallel",)),
    )(*tiles, *weights)


def tile_bwd(f, name, tiles, weights, cts, tm, dt, dw, acc=None):
    acc = acc or {}
    groups = [c if isinstance(c, tuple) else (c,) for c in cts]
    cts = [a for grp in groups for a in grp]
    nt, nw, nc = len(tiles), len(weights), len(cts)
    acc_idx = sorted(acc)
    na = len(acc_idx)
    dti = [i for i in range(nt) if dt[i]]
    dwi = [i for i in range(nw) if dw[i]]

    def body(*refs):
        tv = [r[...] for r in refs[:nt]]
        wv = [r[...].astype(F32) for r in refs[nt:nt + nw]]
        crefs = list(refs[nt + nw:nt + nw + nc])
        cv = []
        for grp in groups:
            terms = [crefs.pop(0)[...] for _ in grp]
            cv.append(functools.reduce(lambda a, b: a + b, terms))
        av = {i: r[...] for i, r in zip(acc_idx, refs[nt + nw + nc:nt + nw + nc + na])}
        orefs = refs[nt + nw + nc + na:]

        def g(*diff):
            t2, w2 = list(tv), list(wv)
            for i, v in zip(dti, diff[:len(dti)]):
                t2[i] = v
            for i, v in zip(dwi, diff[len(dti):]):
                w2[i] = v
            res = f(*t2, *w2)
            return tuple(res) if isinstance(res, (tuple, list)) else (res,)

        _, vjp = jax.vjp(g, *[tv[i] for i in dti], *[wv[i] for i in dwi])
        grads = vjp(tuple(cv))
        for k, i in enumerate(dti):
            gt = grads[k]
            if i in av:
                gt = gt + av[i]
            orefs[k][...] = gt
        first = pl.program_id(0) == 0
        for k, i in enumerate(dwi):
            o = orefs[len(dti) + k]
            gw = grads[len(dti) + k]

            @pl.when(first)
            def _(o=o, gw=gw):
                o[...] = gw

            @pl.when(jnp.logical_not(first))
            def _(o=o, gw=gw):
                o[...] += gw

    out_shape = [jax.ShapeDtypeStruct(tiles[i].shape, F32) for i in dti] + \
                [jax.ShapeDtypeStruct(weights[i].shape, F32) for i in dwi]
    res = pl.pallas_call(
        body, name=name, grid=(_rows(tiles[0]) // tm,),
        in_specs=[_tile_spec(a.shape, tm) for a in tiles] + [_full_spec(w.shape) for w in weights] +
                 [_tile_spec(c.shape, tm) for c in cts] + [_tile_spec(tiles[i].shape, tm) for i in acc_idx],
        out_specs=[_tile_spec(tiles[i].shape, tm) for i in dti] + [_full_spec(weights[i].shape) for i in dwi],
        out_shape=out_shape,
        compiler_params=_cp(("arbitrary",)),
    )(*tiles, *weights, *cts, *[acc[i] for i in acc_idx])
    return list(res[:len(dti)]), list(res[len(dti):])


def _ffn_wspec(l, j, rows, cols, cfirst):
    if cfirst:
        return pl.BlockSpec((1, 1, 1, rows, cols), lambda c, t: (c, l, j, 0, 0))
    return pl.BlockSpec((1, 1, 1, rows, cols), lambda t, c: (c, l, j, 0, 0))


def ffn_fwd(x, g, wg, wu, wd, l, j, tm=512):
    S = x.shape[0]

    def body(x_ref, g_ref, wg_ref, wu_ref, wd_ref, o_ref, h_ref, acc_ref):
        c = pl.program_id(1)

        @pl.when(c == 0)
        def _():
            h_ref[...] = rms(x_ref[...], g_ref[...]).astype(BF16)
            acc_ref[...] = jnp.zeros_like(acc_ref)

        h = h_ref[...]
        a = _bdot(h, wg_ref[0, 0, 0], ((1,), (0,)))
        b = _bdot(h, wu_ref[0, 0, 0], ((1,), (0,)))
        y = a * jax.nn.sigmoid(a) * b
        acc_ref[...] += _bdot(y, wd_ref[0, 0, 0], ((1,), (0,)))

        @pl.when(c == N_CHIPS - 1)
        def _():
            o_ref[...] = x_ref[...] + 0.5 * acc_ref[...]

    return pl.pallas_call(
        body, name=f"ffn_fwd_{l}{j}", grid=(S // tm, N_CHIPS),
        in_specs=[pl.BlockSpec((tm, D), lambda t, c: (t, 0)), pl.BlockSpec((1, D), lambda t, c: (0, 0)),
                  _ffn_wspec(l, j, D, FF_SHARD, False), _ffn_wspec(l, j, D, FF_SHARD, False),
                  _ffn_wspec(l, j, FF_SHARD, D, False)],
        out_specs=pl.BlockSpec((tm, D), lambda t, c: (t, 0)),
        out_shape=jax.ShapeDtypeStruct((S, D), F32),
        scratch_shapes=[pltpu.VMEM((tm, D), BF16), pltpu.VMEM((tm, D), F32)],
        compiler_params=_cp(("parallel", "arbitrary")),
    )(x, g, wg, wu, wd)


def ffn_bwd(x, g, wg, wu, wd, dout, dws, l, j, tm=512):
    S = x.shape[0]

    def body(x_ref, g_ref, wg_ref, wu_ref, wd_ref, do_ref, _g, _u, _d, dh_ref, dwg_ref, dwu_ref, dwd_ref):
        dwg_ref, dwu_ref, dwd_ref = dwg_ref.at[0, 0], dwu_ref.at[0, 0], dwd_ref.at[0, 0]
        t = pl.program_id(1)
        h = rms(x_ref[...], g_ref[...]).astype(BF16)
        wgv, wuv, wdv = wg_ref[0, 0, 0], wu_ref[0, 0, 0], wd_ref[0, 0, 0]
        a = _bdot(h, wgv, ((1,), (0,)))
        b = _bdot(h, wuv, ((1,), (0,)))
        sig = jax.nn.sigmoid(a)
        s = a * sig
        dyd = 0.5 * do_ref[...]
        dy = _bdot(dyd, wdv, ((1,), (1,)))
        dwd = _bdot(s * b, dyd, ((0,), (0,)))
        db = dy * s
        da = dy * b * (sig * (1.0 + a * (1.0 - sig)))
        dwg = _bdot(h, da, ((0,), (0,)))
        dwu = _bdot(h, db, ((0,), (0,)))
        dh_ref[0] = _bdot(da, wgv, ((1,), (1,))) + _bdot(db, wuv, ((1,), (1,)))

        @pl.when(t == 0)
        def _():
            dwg_ref[0] = dwg
            dwu_ref[0] = dwu
            dwd_ref[0] = dwd

        @pl.when(t != 0)
        def _():
            dwg_ref[0] += dwg
            dwu_ref[0] += dwu
            dwd_ref[0] += dwd

    return pl.pallas_call(
        body, name=f"ffn_bwd_{l}{j}", grid=(N_CHIPS, S // tm),
        in_specs=[pl.BlockSpec((tm, D), lambda c, t: (t, 0)), pl.BlockSpec((1, D), lambda c, t: (0, 0)),
                  _ffn_wspec(l, j, D, FF_SHARD, True), _ffn_wspec(l, j, D, FF_SHARD, True),
                  _ffn_wspec(l, j, FF_SHARD, D, True), pl.BlockSpec((tm, D), lambda c, t: (t, 0))] +
                 [pl.BlockSpec(memory_space=pl.ANY)] * 3,
        out_specs=[pl.BlockSpec((1, tm, D), lambda c, t: (c, t, 0)),
                   _ffn_wspec(l, j, D, FF_SHARD, True), _ffn_wspec(l, j, D, FF_SHARD, True),
                   _ffn_wspec(l, j, FF_SHARD, D, True)],
        out_shape=[jax.ShapeDtypeStruct((N_CHIPS, S, D), F32)] + [jax.ShapeDtypeStruct(a.shape, F32) for a in dws],
        input_output_aliases={6: 1, 7: 2, 8: 3},
        compiler_params=_cp(("parallel", "arbitrary")),
    )(x, g, wg, wu, wd, dout, *dws)


def norm_bwd(name, x, g, dh_parts, dres, tm=256):
    S = x.shape[0]
    P = dh_parts.shape[0]

    def body(x_ref, g_ref, dh_ref, dr_ref, dx_ref, dg_ref):
        dh = dh_ref[0]
        for p in range(1, P):
            dh = dh + dh_ref[p]
        _, vjp = jax.vjp(rms, x_ref[...], g_ref[...])
        dx, dg = vjp(dh)
        dx_ref[...] = dr_ref[...] + dx

        @pl.when(pl.program_id(0) == 0)
        def _():
            dg_ref[...] = dg

        @pl.when(pl.program_id(0) != 0)
        def _():
            dg_ref[...] += dg

    return pl.pallas_call(
        body, name=name, grid=(S // tm,),
        in_specs=[pl.BlockSpec((tm, D), lambda t: (t, 0)), pl.BlockSpec((1, D), lambda t: (0, 0)),
                  pl.BlockSpec((P, tm, D), lambda t: (0, t, 0)), pl.BlockSpec((tm, D), lambda t: (t, 0))],
        out_specs=[pl.BlockSpec((tm, D), lambda t: (t, 0)), pl.BlockSpec((1, D), lambda t: (0, 0))],
        out_shape=[jax.ShapeDtypeStruct((S, D), F32), jax.ShapeDtypeStruct((1, D), F32)],
        compiler_params=_cp(("arbitrary",)),
    )(x, g, dh_parts, dres)


def f_attn_sb(x, g, w):
    pr = mm(rms(x, g), w)
    return pr[:, :SB_W], pr[:, SB_W:2 * SB_W], pr[:, 2 * SB_W:]


def _pairs(y):
    return jnp.stack([y[:, 128 * j:128 * (j + 1)] for j in range(DL_PAIRS)])


def f_attn_qk(x, g, w, nrm):
    pr = mm(rms(x, g), w)
    ms = group_sum(pr * pr, DL_HEADS) * (1.0 / HEAD)
    return _pairs(pr * lax.rsqrt(ms + NORM_EPS) * jnp.concatenate([nrm] * DL_HEADS, axis=1))


def f_attn_v(x, g, w):
    return _pairs(mm(rms(x, g), w))


def _masked(strict, x):
    return x if strict is None else jnp.where(strict, x, 0.0)


def _sb_tiles(q, k, strict):
    z = _bdot(q, k, ((1,), (1,))) * (HEAD ** -0.5)
    return z, _masked(strict, -softplus(z))


def _tri(n, upper):
    r = lax.broadcasted_iota(jnp.int32, (n, n), 0)
    c = lax.broadcasted_iota(jnp.int32, (n, n), 1)
    return ((r > c) if upper else (r < c)).astype(BF16)


def _tri_sums(xs, tri):
    x = jnp.concatenate(xs, axis=0)
    hi, lo = _split2(x)
    y = _dg(hi, tri, ((1,), (0,))) + _dg(lo, tri, ((1,), (0,)))
    n = xs[0].shape[0]
    return [y[n * i:n * (i + 1)] for i in range(len(xs))]


def sb_fwd(q, k, v, tb=QBLK):
    S = q.shape[0]
    nh = SB_W // HEAD

    def body(q_ref, k_ref, v_ref, o_ref):
        qb = pl.program_id(0)
        diag = lax.broadcasted_iota(jnp.int32, (tb, tb), 1) < lax.broadcasted_iota(jnp.int32, (tb, tb), 0)
        after_mat = _tri(tb, True)
        sls = [slice(HEAD * h, HEAD * (h + 1)) for h in range(nh)]
        qs = [q_ref[:, sl] for sl in sls]

        def step(kb, carry, strict):
            accs, runs = carry
            rows = pl.ds(pl.multiple_of(kb * tb, tb), tb)
            kblk, vblk = k_ref[rows, :], v_ref[rows, :]
            tiles = [_sb_tiles(qs[h], kblk[:, sls[h]], strict) for h in range(nh)]
            afters = _tri_sums([t[1] for t in tiles], after_mat)
            new_accs, new_runs = [], []
            for h, (z, keep) in enumerate(tiles):
                w = _masked(strict, jnp.exp(z + keep + afters[h] + runs[h]))
                new_accs.append(accs[h] + _bdot(w, vblk[:, sls[h]], ((1,), (0,))))
                new_runs.append(runs[h] + jnp.sum(keep, axis=1, keepdims=True))
            return tuple(new_accs), tuple(new_runs)

        init = (tuple(jnp.zeros((tb, HEAD), F32) for _ in range(nh)), tuple(jnp.zeros((tb, 1), F32) for _ in range(nh)))
        carry = step(qb, init, diag)
        accs, _ = lax.fori_loop(1, qb + 1, lambda i, c: step(qb - i, c, None), carry)
        o_ref[...] = jnp.concatenate(accs, axis=1)

    return pl.pallas_call(
        body, name="sb_fwd", grid=(S // tb,),
        in_specs=[pl.BlockSpec((tb, SB_W), lambda i: (i, 0)), pl.BlockSpec((S, SB_W), lambda i: (0, 0)),
                  pl.BlockSpec((S, SB_W), lambda i: (0, 0))],
        out_specs=pl.BlockSpec((tb, SB_W), lambda i: (i, 0)),
        out_shape=jax.ShapeDtypeStruct((S, SB_W), F32),
        compiler_params=_cp(("parallel",)),
    )(q, k, v)


def sb_bwd(q, k, v, do, tb=QBLK):
    S = q.shape[0]
    nh = SB_W // HEAD
    scale = HEAD ** -0.5

    def body(q_ref, k_ref, v_ref, do_ref, dq_ref, dk_ref, dv_ref, g_scr):
        qb = pl.program_id(0)

        @pl.when(qb == 0)
        def _():
            dk_ref[...] = jnp.zeros_like(dk_ref)
            dv_ref[...] = jnp.zeros_like(dv_ref)

        diag = lax.broadcasted_iota(jnp.int32, (tb, tb), 1) < lax.broadcasted_iota(jnp.int32, (tb, tb), 0)
        after_mat = _tri(tb, True)
        before_mat = _tri(tb, False)
        sls = [slice(HEAD * h, HEAD * (h + 1)) for h in range(nh)]
        qs = [q_ref[:, sl] for sl in sls]
        dos = [do_ref[:, sl] for sl in sls]

        def right_to_left(kb, runs, strict):
            rows = pl.ds(pl.multiple_of(kb * tb, tb), tb)
            kblk, vblk = k_ref[rows, :], v_ref[rows, :]
            tiles = [_sb_tiles(qs[h], kblk[:, sls[h]], strict) for h in range(nh)]
            afters = _tri_sums([t[1] for t in tiles], after_mat)
            dvs, new_runs = [], []
            for h, (z, keep) in enumerate(tiles):
                w = _masked(strict, jnp.exp(z + keep + afters[h] + runs[h]))
                g_scr[h, kb] = _bdot(dos[h], vblk[:, sls[h]], ((1,), (1,))) * w
                dvs.append(_bdot(w, dos[h], ((0,), (0,))))
                new_runs.append(runs[h] + jnp.sum(keep, axis=1, keepdims=True))
            dv_ref[rows, :] += jnp.concatenate(dvs, axis=1)
            return tuple(new_runs)

        zero_runs = tuple(jnp.zeros((tb, 1), F32) for _ in range(nh))
        runs = right_to_left(qb, zero_runs, diag)
        lax.fori_loop(1, qb + 1, lambda i, r: right_to_left(qb - i, r, None), runs)

        def left_to_right(kb, carry, strict):
            dqs, runs = carry
            rows = pl.ds(pl.multiple_of(kb * tb, tb), tb)
            kblk = k_ref[rows, :]
            gws = [g_scr[h, kb] for h in range(nh)]
            befores = _tri_sums(gws, before_mat)
            new_dqs, new_runs, dks = [], [], []
            for h in range(nh):
                kh = kblk[:, sls[h]]
                sig = jax.nn.sigmoid(_bdot(qs[h], kh, ((1,), (1,))) * scale)
                dkeep = _masked(strict, befores[h] + runs[h])
                dz = (gws[h] * (1.0 - sig) - dkeep * sig) * scale
                new_dqs.append(dqs[h] + _bdot(dz, kh, ((1,), (0,))))
                dks.append(_bdot(dz, qs[h], ((0,), (0,))))
                new_runs.append(runs[h] + jnp.sum(gws[h], axis=1, keepdims=True))
            dk_ref[rows, :] += jnp.concatenate(dks, axis=1)
            return tuple(new_dqs), tuple(new_runs)

        carry = lax.fori_loop(0, qb, lambda kb, c: left_to_right(kb, c, None),
                              (tuple(jnp.zeros((tb, HEAD), F32) for _ in range(nh)), zero_runs))
        dqs, _ = left_to_right(qb, carry, diag)
        dq_ref[...] = jnp.concatenate(dqs, axis=1)

    whole = pl.BlockSpec((S, SB_W), lambda i: (0, 0))
    blk = pl.BlockSpec((tb, SB_W), lambda i: (i, 0))
    return pl.pallas_call(
        body, name="sb_bwd", grid=(S // tb,),
        in_specs=[blk, whole, whole, blk], out_specs=[blk, whole, whole],
        out_shape=[jax.ShapeDtypeStruct((S, SB_W), F32)] * 3,
        scratch_shapes=[pltpu.VMEM((nh, S // tb, tb, tb), F32)],
        compiler_params=_cp(("arbitrary",)),
    )(q, k, v, do)


def reorder(name, x, groups, inverse):
    S = x.shape[1]
    out = x
    for gi, r in enumerate(groups):
        if r > 1:
            out = _reorder_call(f"{name}_{r}", x, out, gi, r, S // r, inverse)
    return out


def _reorder_call(name, x, prev, gi, r, L, inverse):
    S = x.shape[1]
    whole = pl.BlockSpec((None, S, 128), lambda p, c: (2 * gi + p, 0, 0))
    part = pl.BlockSpec((None, L, 128), lambda p, c: (2 * gi + p, c, 0))

    def body(x_ref, prev_ref, o_ref):
        c = pl.program_id(1)
        if inverse:
            o_ref[pl.ds(c, L, stride=r), :] = x_ref[...]
        else:
            o_ref[...] = x_ref[pl.ds(c, L, stride=r), :]

    return pl.pallas_call(
        body, name=name, grid=(2, r),
        in_specs=[part if inverse else whole, pl.BlockSpec(memory_space=pl.ANY)],
        out_specs=whole if inverse else part,
        out_shape=jax.ShapeDtypeStruct(x.shape, x.dtype),
        input_output_aliases={1: 0},
        compiler_params=_cp(("parallel", "arbitrary")),
    )(x, prev)


def _dil_blocks(S):
    return S // QBLK


def _dil_mask(n_in_stream):
    qi = lax.broadcasted_iota(jnp.int32, (QBLK, 2 * QBLK), 0)
    kj = lax.broadcasted_iota(jnp.int32, (QBLK, 2 * QBLK), 1) - QBLK
    dist = qi - kj
    return (dist >= 0) & (dist <= QBLK) & ((n_in_stream > 0) | (kj >= 0))


def _stream_pos(gi, i, S):
    nb = jnp.where(gi == 0, S // (QBLK * DIL[0]), jnp.where(gi == 1, S // (QBLK * DIL[1]), S // (QBLK * DIL[2])))
    return i % nb


def dil_fwd(q, k, v, bias):
    S = q.shape[1]
    nblk = _dil_blocks(S)

    def body(q_ref, kc_ref, kp_ref, vc_ref, vp_ref, b_ref, o_ref, l_ref):
        gi, i = pl.program_id(0), pl.program_id(1)
        mask = _dil_mask(_stream_pos(gi, i, S))
        for j in range(2):
            q2, kc, kp, vc, vp = q_ref[j], kc_ref[j], kp_ref[j], vc_ref[j], vp_ref[j]
            os_, ls_ = [], []
            for hh in range(2):
                sl = slice(HEAD * hh, HEAD * (hh + 1))
                kw = jnp.concatenate([kp[:, sl], kc[:, sl]], axis=0)
                vw = jnp.concatenate([vp[:, sl], vc[:, sl]], axis=0)
                lg = _bdot(q2[:, sl], kw, ((1,), (1,))) * (HEAD ** -0.5) + b_ref[2 * j + hh]
                lg = jnp.where(mask, lg, NEG_INF)
                m = jnp.max(lg, axis=-1, keepdims=True)
                p = jnp.exp(lg - m)
                den = jnp.sum(p, axis=-1, keepdims=True)
                os_.append(_bdot(p / den, vw, ((1,), (0,))))
                ls_.append(jnp.broadcast_to(m + jnp.log(den), (QBLK, HEAD)))
            o_ref[j] = jnp.concatenate(os_, axis=1)
            l_ref[j] = jnp.concatenate(ls_, axis=1)

    cur = pl.BlockSpec((2, QBLK, 128), lambda g, i: (g, i, 0))
    prev = pl.BlockSpec((2, QBLK, 128), lambda g, i: (g, jnp.maximum(i - 1, 0), 0))
    return pl.pallas_call(
        body, name="dil_fwd", grid=(len(DIL), nblk),
        in_specs=[cur, cur, prev, cur, prev, pl.BlockSpec((4, QBLK, 2 * QBLK), lambda g, i: (g, 0, 0))],
        out_specs=[cur, cur],
        out_shape=[jax.ShapeDtypeStruct(q.shape, F32)] * 2,
        compiler_params=_cp(("parallel", "parallel")),
    )(q, k, k, v, v, bias)


def dil_bwd(q, k, v, bias, o, lse, do, dlse):
    S = q.shape[1]
    nblk = _dil_blocks(S)

    def body(q_ref, kc_ref, kp_ref, vc_ref, vp_ref, b_ref, o_ref, l_ref, do_ref, dl_ref,
             dq_ref, dk_ref, dv_ref, ds_ref, dk_car, dv_car):
        gi, i = pl.program_id(0), pl.program_id(1)

        @pl.when(i == 0)
        def _():
            ds_ref[...] = jnp.zeros_like(ds_ref)
            dk_car[...] = jnp.zeros_like(dk_car)
            dv_car[...] = jnp.zeros_like(dv_car)

        @pl.when(i < nblk)
        def _():
            mask = _dil_mask(_stream_pos(gi, i, S))
            for j in range(2):
                q2, kc, kp, vc, vp = q_ref[j], kc_ref[j], kp_ref[j], vc_ref[j], vp_ref[j]
                o2, l2, do2, dl2 = o_ref[j], l_ref[j], do_ref[j], dl_ref[j]
                dqs, dkps, dkcs, dvps, dvcs = [], [], [], [], []
                for hh in range(2):
                    sl = slice(HEAD * hh, HEAD * (hh + 1))
                    qh, doh = q2[:, sl], do2[:, sl]
                    kw = jnp.concatenate([kp[:, sl], kc[:, sl]], axis=0)
                    vw = jnp.concatenate([vp[:, sl], vc[:, sl]], axis=0)
                    lg = _bdot(qh, kw, ((1,), (1,))) * (HEAD ** -0.5) + b_ref[2 * j + hh]
                    p = jnp.where(mask, jnp.exp(lg - l2[:, HEAD * hh:HEAD * hh + 1]), 0.0)
                    dp = _bdot(doh, vw, ((1,), (1,)))
                    delta = jnp.sum(doh * o2[:, sl], axis=-1, keepdims=True)
                    dl = jnp.sum(dl2[:, sl], axis=-1, keepdims=True)
                    ds = p * (dp - delta + dl)
                    ds_ref[2 * j + hh] += ds
                    dsq = ds * (HEAD ** -0.5)
                    dqs.append(_bdot(dsq, kw, ((1,), (0,))))
                    dkw = _bdot(dsq, qh, ((0,), (0,)))
                    dvw = _bdot(p, doh, ((0,), (0,)))
                    dkps.append(dkw[:QBLK])
                    dkcs.append(dkw[QBLK:])
                    dvps.append(dvw[:QBLK])
                    dvcs.append(dvw[QBLK:])
                dq_ref[j] = jnp.concatenate(dqs, axis=1)
                dk_ref[j] = dk_car[j] + jnp.concatenate(dkps, axis=1)
                dv_ref[j] = dv_car[j] + jnp.concatenate(dvps, axis=1)
                dk_car[j] = jnp.concatenate(dkcs, axis=1)
                dv_car[j] = jnp.concatenate(dvcs, axis=1)

        @pl.when(i == nblk)
        def _():
            dk_ref[...] = dk_car[...]
            dv_ref[...] = dv_car[...]

    cur = pl.BlockSpec((2, QBLK, 128), lambda g, i: (g, jnp.minimum(i, nblk - 1), 0))
    prev = pl.BlockSpec((2, QBLK, 128), lambda g, i: (g, jnp.clip(i - 1, 0, nblk - 1), 0))
    bspec = pl.BlockSpec((4, QBLK, 2 * QBLK), lambda g, i: (g, 0, 0))
    return pl.pallas_call(
        body, name="dil_bwd", grid=(len(DIL), nblk + 1),
        in_specs=[cur, cur, prev, cur, prev, bspec, cur, cur, cur, cur],
        out_specs=[cur, prev, prev, bspec],
        out_shape=[jax.ShapeDtypeStruct(q.shape, F32)] * 3 + [jax.ShapeDtypeStruct(bias.shape, F32)],
        scratch_shapes=[pltpu.VMEM((2, QBLK, 128), F32), pltpu.VMEM((2, QBLK, 128), F32)],
        compiler_params=_cp(("arbitrary", "arbitrary")),
    )(q, k, k, v, v, bias, o, lse, do, dlse)


def _t5_bucket(dist):
    max_exact = N_BUCKETS // 2
    d = jnp.maximum(dist, 1).astype(F32)
    large = max_exact + (jnp.log(d / max_exact) / math.log(MAX_DISTANCE / max_exact)
                         * (N_BUCKETS - max_exact)).astype(jnp.int32)
    large = jnp.minimum(large, N_BUCKETS - 1)
    return jnp.where(dist < max_exact, dist, large)


def _bucket_maps():
    qi = jnp.arange(QBLK)[:, None]
    kj = jnp.arange(2 * QBLK)[None, :] - QBLK
    dist = jnp.maximum(qi - kj, 0)
    return jnp.stack([_t5_bucket(dist * r) for r in DIL])


def bias_table(rel_bias, buckets):
    def body(tbl_ref, bk_ref, o_ref):
        for h in range(DL_HEADS):
            bk = bk_ref[h // 4]

            def step(b, acc):
                return jnp.where(bk == b, tbl_ref[b, h], acc)

            o_ref[h] = lax.fori_loop(0, N_BUCKETS, step, jnp.zeros(bk.shape, F32))

    return pl.pallas_call(
        body, name="bias_table", out_shape=jax.ShapeDtypeStruct((DL_HEADS,) + buckets.shape[1:], F32),
        in_specs=[pl.BlockSpec(memory_space=pltpu.SMEM), pl.BlockSpec(memory_space=pltpu.VMEM)],
        out_specs=pl.BlockSpec(memory_space=pltpu.VMEM),
    )(rel_bias, buckets)


def bias_grad(ds, buckets):
    def body(ds_ref, bk_ref, o_ref):
        lane = lax.broadcasted_iota(jnp.int32, (1, 128), 1)
        for h in range(DL_HEADS):
            dsv = ds_ref[h]
            bk = bk_ref[h // 4]

            def step(b, row):
                return jnp.where(lane == b, jnp.sum(jnp.where(bk == b, dsv, 0.0)), row)

            o_ref[h:h + 1, :] = lax.fori_loop(0, N_BUCKETS, step, jnp.zeros((1, 128), F32))

    return pl.pallas_call(
        body, name="bias_grad", out_shape=jax.ShapeDtypeStruct((DL_HEADS, 128), F32),
        in_specs=[pl.BlockSpec(memory_space=pltpu.VMEM)] * 2, out_specs=pl.BlockSpec(memory_space=pltpu.VMEM),
    )(ds, buckets)


def f_attn_out(x, oa, o, lse, w):
    og = [jnp.concatenate([o[2 * g], o[2 * g + 1]], axis=1) for g in range(3)]
    lg = [jnp.concatenate([lse[2 * g], lse[2 * g + 1]], axis=1) for g in range(3)]
    m = jnp.maximum(jnp.maximum(lg[0], lg[1]), lg[2])
    e = [jnp.exp(l - m) for l in lg]
    den = e[0] + e[1] + e[2]
    ob = (e[0] * og[0] + e[1] * og[1] + e[2] * og[2]) / den
    return x + mm(jnp.concatenate([oa, ob], axis=1), w)


def norm_shift_fwd(x, g, tm=256):
    S = x.shape[0]

    def body(x_ref, xp_ref, g_ref, h_ref, hs_ref):
        h = rms(x_ref[...], g_ref[...])
        hp = rms(xp_ref[7:8, :], g_ref[...])
        hp = jnp.where(pl.program_id(0) == 0, 0.0, hp)
        row = lax.broadcasted_iota(jnp.int32, (tm, D), 0)
        h_ref[...] = h
        hs_ref[...] = jnp.where(row == 0, hp, pltpu.roll(h, 1, 0))

    return pl.pallas_call(
        body, name="rw_norm_shift", grid=(S // tm,),
        in_specs=[pl.BlockSpec((tm, D), lambda t: (t, 0)),
                  pl.BlockSpec((8, D), lambda t: (jnp.maximum(t * (tm // 8) - 1, 0), 0)),
                  pl.BlockSpec((1, D), lambda t: (0, 0))],
        out_specs=[pl.BlockSpec((tm, D), lambda t: (t, 0))] * 2,
        out_shape=[jax.ShapeDtypeStruct((S, D), F32)] * 2,
        compiler_params=_cp(("parallel",)),
    )(x, x, g)


def norm_shift_bwd(x, g, dh, dhs, dres, tm=256):
    S = x.shape[0]
    nt = S // tm

    def body(x_ref, g_ref, dh_ref, dhs_ref, dhn_ref, dr_ref, dx_ref, dg_ref):
        t = pl.program_id(0)
        nxt = jnp.where(t == nt - 1, 0.0, dhn_ref[0:1, :])
        row = lax.broadcasted_iota(jnp.int32, (tm, D), 0)
        tot = dh_ref[...] + jnp.where(row == tm - 1, nxt, pltpu.roll(dhs_ref[...], tm - 1, 0))
        _, vjp = jax.vjp(rms, x_ref[...], g_ref[...])
        dx, dg = vjp(tot)
        dx_ref[...] = dr_ref[...] + dx

        @pl.when(t == 0)
        def _():
            dg_ref[...] = dg

        @pl.when(t != 0)
        def _():
            dg_ref[...] += dg

    tile = pl.BlockSpec((tm, D), lambda t: (t, 0))
    return pl.pallas_call(
        body, name="rw_norm_shift_bwd", grid=(nt,),
        in_specs=[tile, pl.BlockSpec((1, D), lambda t: (0, 0)), tile, tile,
                  pl.BlockSpec((8, D), lambda t: (jnp.minimum((t + 1) * (tm // 8), S // 8 - 1), 0)), tile],
        out_specs=[tile, pl.BlockSpec((1, D), lambda t: (0, 0))],
        out_shape=[jax.ShapeDtypeStruct((S, D), F32), jax.ShapeDtypeStruct((1, D), F32)],
        compiler_params=_cp(("arbitrary",)),
    )(x, g, dh, dhs, dhs, dres)


def f_rw_proj(h, hs, mix, w):
    return mm(h + (hs - h) * mix, w)


def f_rw_mid(h, hs, r, k, v, mix3, w0, a0, kkw, kaw, w1, w2, a1, a2, g1, g2):
    xx = hs - h
    xw, xa, xg = h + xx * mix3[0:1], h + xx * mix3[1:2], h + xx * mix3[2:3]
    w_log = -softplus(-(w0 + mm(jnp.tanh(mm(xw, w1)), w2))) - 0.5
    lw = -jnp.exp(w_log)
    ag = jax.nn.sigmoid(a0 + mm(mm(xa, a1), a2))
    gate = mm(jax.nn.sigmoid(mm(xg, g1)), g2)
    kk = k * kkw
    kk = kk / jnp.maximum(jnp.sqrt(group_sum(kk * kk, RW_H)), 1e-12)
    kmod = k * (1.0 + (ag - 1.0) * kaw)
    return (to_heads(r), to_heads(lw), to_heads(kmod), to_heads(v), to_heads(-kk), to_heads(kk * ag), gate)


def f_rw_post(yh, rh, kh, vh, gate, x, lng, lnb, rk, wo):
    mu = jnp.mean(yh, axis=-1, keepdims=True)
    var = jnp.mean(jnp.square(yh - mu), axis=-1, keepdims=True)
    yn = (yh - mu) * lax.rsqrt(var + GN_EPS)
    bonus = jnp.sum(rh * kh * rk, axis=-1, keepdims=True) * vh
    y = from_heads(yn) * lng + lnb + from_heads(bonus)
    return x + mm(y * gate, wo)


def _split2(x):
    hi = x.astype(BF16)
    return hi, (x - hi.astype(F32)).astype(BF16)


def _b3(x, y, cx, cy):
    dn = (((cx,), (cy,)), ((0,), (0,)))
    xh, xl = _split2(x)
    yh, yl = _split2(y)
    d = lambda p, q: lax.dot_general(p, q, dn, preferred_element_type=F32)
    return d(xh, yh) + (d(xh, yl) + d(xl, yh))


@jax.custom_vjp
def b_nt(x, y):
    return _b3(x, y, 2, 2)


@jax.custom_vjp
def b_nn(x, y):
    return _b3(x, y, 2, 1)


@jax.custom_vjp
def b_tn(x, y):
    return _b3(x, y, 1, 1)


b_nt.defvjp(lambda x, y: (b_nt(x, y), (x, y)), lambda r, g: (b_nn(g, r[1]), b_tn(g, r[0])))
b_nn.defvjp(lambda x, y: (b_nn(x, y), (x, y)), lambda r, g: (b_nt(g, r[1]), b_tn(r[0], g)))
b_tn.defvjp(lambda x, y: (b_tn(x, y), (x, y)), lambda r, g: (b_nt(r[1], g), b_nn(r[0], g)))


def _tri_apply(x, lower):
    H, C, _ = x.shape
    ii = lax.broadcasted_iota(jnp.int32, (C, C), 0)
    jj = lax.broadcasted_iota(jnp.int32, (C, C), 1)
    m = jnp.broadcast_to(((jj <= ii) if lower else (jj >= ii)).astype(BF16), (H, C, C))
    x1 = x.astype(BF16)
    r1 = x - x1.astype(F32)
    x2 = r1.astype(BF16)
    x3 = (r1 - x2.astype(F32)).astype(BF16)
    d = lambda q: lax.dot_general(m, q, (((2,), (1,)), ((0,), (0,))), preferred_element_type=F32)
    return d(x1) + (d(x2) + d(x3))


@jax.custom_vjp
def run_sum(x):
    return _tri_apply(x, True)


run_sum.defvjp(lambda x: (run_sum(x), None), lambda _, g: (_tri_apply(g, False),))


def rwkv_chunk(S0, r, lw, k, v, a, b):
    H, C, _ = r.shape
    V = S0.shape[1]
    ii = lax.broadcasted_iota(jnp.int32, (C, C), 0)
    jj = lax.broadcasted_iota(jnp.int32, (C, C), 1)
    strict = jj < ii
    i2 = lax.broadcasted_iota(jnp.int32, (C, 2 * C), 0)
    j2 = lax.broadcasted_iota(jnp.int32, (C, 2 * C), 1)
    incl2 = jnp.where(j2 >= C, j2 - C, j2) <= i2
    g = run_sum(lw)
    ig = jnp.exp(-g)
    ar = jnp.concatenate([a * jnp.exp(g - lw), r * jnp.exp(g)], axis=1)
    bk = jnp.concatenate([b * ig, k * ig], axis=1)
    m = b_nt(ar, bk)
    a_ab = jnp.where(strict, m[:, :C, :C], 0.0)
    a_ak = jnp.where(strict, m[:, :C, C:], 0.0)
    b_r = jnp.where(incl2, m[:, C:, :], 0.0)
    p = b_nt(ar, S0)
    u = p[:, :C] + b_nn(a_ak, v)
    nmat, n = a_ab, 1
    while n < C:
        n *= 2
        if n < C:
            z = b_nn(nmat, jnp.concatenate([u, nmat], axis=2))
            u, nmat = u + z[:, :, :V], z[:, :, V:]
        else:
            u = u + b_nn(nmat, u)
    uv = jnp.concatenate([u, v], axis=1)
    y = p[:, C:] + b_nn(b_r, uv)
    g_end = g[:, C - 1:C, :]
    dec = jnp.exp(g_end - g)
    s_new = S0 * jnp.exp(g_end) + b_tn(uv, jnp.concatenate([b * dec, k * dec], axis=1))
    return y, s_new


def rwkv_fwd(r, lw, k, v, a, b):
    H, S, _ = r.shape
    C = RW_CHUNK

    def body(r_ref, lw_ref, k_ref, v_ref, a_ref, b_ref, y_ref, s_ref, s_scr):
        @pl.when(pl.program_id(0) == 0)
        def _():
            s_scr[...] = jnp.zeros_like(s_scr)

        s0 = s_scr[...]
        s_ref[0] = s0
        y, s1 = rwkv_chunk(s0, r_ref[...], lw_ref[...], k_ref[...], v_ref[...], a_ref[...], b_ref[...])
        y_ref[...] = y
        s_scr[...] = s1

    bs = pl.BlockSpec((H, C, HEAD), lambda c: (0, c, 0))
    return pl.pallas_call(
        body, name="rwkv_fwd", grid=(S // C,), in_specs=[bs] * 6,
        out_specs=[bs, pl.BlockSpec((1, H, HEAD, HEAD), lambda c: (c, 0, 0, 0))],
        out_shape=[jax.ShapeDtypeStruct((H, S, HEAD), F32), jax.ShapeDtypeStruct((S // C, H, HEAD, HEAD), F32)],
        scratch_shapes=[pltpu.VMEM((H, HEAD, HEAD), F32)],
        compiler_params=_cp(("arbitrary",)),
    )(r, lw, k, v, a, b)


def rwkv_bwd(r, lw, k, v, a, b, states, dy):
    H, S, _ = r.shape
    C = RW_CHUNK
    nc = S // C

    def body(r_ref, lw_ref, k_ref, v_ref, a_ref, b_ref, s_ref, dy_ref, dr, dlw, dk, dv, da, db, ds_scr):
        @pl.when(pl.program_id(0) == 0)
        def _():
            ds_scr[...] = jnp.zeros_like(ds_scr)

        _, vjp = jax.vjp(rwkv_chunk, s_ref[0], r_ref[...], lw_ref[...], k_ref[...], v_ref[...], a_ref[...], b_ref[...])
        grads = vjp((dy_ref[...], ds_scr[...]))
        ds_scr[...] = grads[0]
        for o, gv in zip((dr, dlw, dk, dv, da, db), grads[1:]):
            o[...] = gv

    bs = pl.BlockSpec((H, C, HEAD), lambda c: (0, nc - 1 - c, 0))
    return pl.pallas_call(
        body, name="rwkv_bwd", grid=(nc,),
        in_specs=[bs] * 6 + [pl.BlockSpec((1, H, HEAD, HEAD), lambda c: (nc - 1 - c, 0, 0, 0)), bs],
        out_specs=[bs] * 6, out_shape=[jax.ShapeDtypeStruct((H, S, HEAD), F32)] * 6,
        scratch_shapes=[pltpu.VMEM((H, HEAD, HEAD), F32)],
        compiler_params=_cp(("arbitrary",)),
    )(r, lw, k, v, a, b, states, dy)


def loss_head(y, target, tm=512):
    S = y.shape[0]

    def body(y_ref, t_ref, dy_ref, l_ref):
        e = y_ref[...] - t_ref[...]
        dy_ref[...] = e * (1.0 / D)
        part = jnp.broadcast_to(0.5 * jnp.sum(jnp.mean(e * e, axis=-1, keepdims=True)), (1, 128))

        @pl.when(pl.program_id(0) == 0)
        def _():
            l_ref[...] = part

        @pl.when(pl.program_id(0) != 0)
        def _():
            l_ref[...] += part

    tile = pl.BlockSpec((tm, D), lambda t: (t, 0))
    return pl.pallas_call(
        body, name="loss_head", grid=(S // tm,), in_specs=[tile, tile],
        out_specs=[tile, pl.BlockSpec((1, 128), lambda t: (0, 0))],
        out_shape=[jax.ShapeDtypeStruct((S, D), F32), jax.ShapeDtypeStruct((1, 128), F32)],
        compiler_params=_cp(("arbitrary",)),
    )(y, target)


def _row_tile(rows, cols, budget=1 << 19):
    best = None
    for tr in range(8, rows + 1, 8):
        if rows % tr == 0 and tr * cols <= budget:
            best = tr
    return best or rows


def _adam(w, g, m, v):
    m = ADAM_B1 * m + (1.0 - ADAM_B1) * g
    v = ADAM_B2 * v + (1.0 - ADAM_B2) * jnp.square(g)
    m_hat = m / (1.0 - ADAM_B1 ** ADAM_STEP)
    v_hat = v / (1.0 - ADAM_B2 ** ADAM_STEP)
    return -ADAM_LR * (m_hat / (jnp.sqrt(v_hat) + ADAM_EPS) + ADAM_WD * w), m, v


def sum_slots(name, parts, dtype=F32, extras=()):
    n = 0 if parts is None else parts.shape[0]
    R, C = extras[0].shape if parts is None else parts.shape[1:]
    tr = _row_tile(R, C * (n + len(extras)))
    ins = ([] if parts is None else [parts]) + list(extras)

    def body(*refs):
        terms = [] if parts is None else [refs[0][i] for i in range(n)]
        terms += [r[...] for r in refs[len(ins) - len(extras):len(ins)]]
        s = terms[0].astype(F32)
        for t in terms[1:]:
            s = s + t.astype(F32)
        refs[len(ins)][...] = s.astype(dtype)

    tile = pl.BlockSpec((tr, C), lambda t: (t, 0))
    return pl.pallas_call(
        body, name=name, grid=(R // tr,),
        in_specs=([] if parts is None else [pl.BlockSpec((n, tr, C), lambda t: (0, t, 0))]) + [tile] * len(extras),
        out_specs=tile, out_shape=jax.ShapeDtypeStruct((R, C), dtype), compiler_params=_cp(("parallel",)),
    )(*ins)


def adam_step(name, ga, gb, w, m, v):
    R, C = w.shape
    tr = _row_tile(R, C, 1 << 17)
    ins = [ga] + ([gb] if gb is not None else []) + [w, m, v]

    def body(*refs):
        g = refs[0][...]
        if gb is not None:
            g = g + refs[1][...]
        w_ref, m_ref, v_ref, g_out, d_out, m_out, v_out = refs[len(ins) - 3:]
        d, m2, v2 = _adam(w_ref[...], g, m_ref[...], v_ref[...])
        g_out[...] = g
        d_out[...] = d
        m_out[...] = m2
        v_out[...] = v2

    tile = pl.BlockSpec((tr, C), lambda t: (t, 0))
    return pl.pallas_call(
        body, name=name, grid=(R // tr,), in_specs=[tile] * len(ins), out_specs=[tile] * 4,
        out_shape=[jax.ShapeDtypeStruct((R, C), F32)] * 4, compiler_params=_cp(("parallel",)),
    )(*ins)


def _place():
    return lax.axis_index("x"), lax.axis_index("y"), lax.axis_index("c")


def _flip(me, mask):
    return tuple(1 - v if mk else v for v, mk in zip(me, mask))


CHIP_MASKS = ((1, 0, 0), (0, 1, 0), (1, 1, 0))
ALL_MASKS = tuple((a, b, c) for a in (0, 1) for b in (0, 1) for c in (0, 1) if (a, b, c) != (0, 0, 0))


def _chip(dev):
    return 2 * dev[0] + dev[1]


def _devno(dev):
    return 4 * dev[0] + 2 * dev[1] + dev[2]


def exchange(name, arrays, out_shapes, masks, copies, src_of, dst_of, local_of, alias=False):
    n, npeer = len(arrays), len(masks)
    nloc = 1

    def body(*refs):
        ins, outs = refs[:n], refs[n:2 * n]
        send_sems, recv_sems, local_sems = refs[2 * n:]
        me = _place()
        peers = [_flip(me, mk) for mk in masks]
        locals_ = []
        for i in range(n):
            for q, (src, dst) in enumerate(local_of(ins[i], outs[i], me)):
                cp = pltpu.make_async_copy(src, dst, local_sems.at[i * nloc + q])
                cp.start()
                locals_.append(cp)
        sends = []
        for i in range(n):
            for j, peer in enumerate(peers):
                srcs, dsts = src_of(ins[i], me, j), dst_of(outs[i], me, j)
                for q in range(copies):
                    sem = (i * npeer + j) * copies + q
                    cp = pltpu.make_async_remote_copy(
                        src_ref=srcs[q], dst_ref=dsts[q], send_sem=send_sems.at[sem], recv_sem=recv_sems.at[sem],
                        device_id=peer, device_id_type=MESH)
                    cp.start()
                    sends.append(cp)
        for i in range(n):
            for j, peer in enumerate(peers):
                lands = dst_of(outs[i], peer, j)
                for q in range(copies):
                    sem = (i * npeer + j) * copies + q
                    pltpu.make_async_remote_copy(
                        src_ref=lands[q], dst_ref=lands[q], send_sem=send_sems.at[sem], recv_sem=recv_sems.at[sem],
                        device_id=peer, device_id_type=MESH).wait_recv()
        for cp in sends:
            cp.wait_send()
        for cp in locals_:
            cp.wait()

    hbm = pl.BlockSpec(memory_space=pl.ANY)
    return pl.pallas_call(
        body, name=name, in_specs=[hbm] * n, out_specs=[hbm] * n, out_shape=list(out_shapes),
        scratch_shapes=[pltpu.SemaphoreType.DMA((n * npeer * copies,)), pltpu.SemaphoreType.DMA((n * npeer * copies,)),
                        pltpu.SemaphoreType.DMA((n * nloc,))],
        input_output_aliases={i: i for i in range(n)} if alias else {},
    )(*arrays)


def _half(c, rows):
    return pl.ds(c * (rows // 2), rows // 2)


def gather_chips(arrays):
    outs = [jax.ShapeDtypeStruct((N_CHIPS,) + a.shape, a.dtype) for a in arrays]
    sib = len(CHIP_MASKS)
    got = exchange("gather_weights", arrays, outs, CHIP_MASKS + ((0, 0, 1),), 1,
                   src_of=lambda r, me, j: [r] if j == sib else [r.at[_half(me[2], r.shape[0])]],
                   dst_of=lambda o, sender, j: [o.at[_chip(sender)]] if j == sib else
                   [o.at[_chip(sender), _half(sender[2], o.shape[1])]],
                   local_of=lambda r, o, me: [])
    return exchange("gather_swap", got, outs, ((0, 0, 1),), len(CHIP_MASKS),
                    src_of=lambda r, me, j: [r.at[_chip(_flip(me, mk)), _half(me[2], r.shape[1])] for mk in CHIP_MASKS],
                    dst_of=lambda o, sender, j: [o.at[_chip(_flip(sender, mk)), _half(sender[2], o.shape[1])]
                                                 for mk in CHIP_MASKS],
                    local_of=lambda r, o, me: [], alias=True)


def reduce_chips(names, arrays, wire):
    x, y, c = _place()
    split = [a.reshape(N_CHIPS, 2, a.shape[1] // 2, a.shape[2]) for a in arrays]
    half_shapes = [jax.ShapeDtypeStruct((N_CHIPS,) + a.shape[2:], F32) for a in split]
    theirs = exchange("grad_pre_swap", split, half_shapes, ((0, 0, 1),), 1,
                      src_of=lambda r, me, j: [r.at[:, 1 - me[2]]], dst_of=lambda o, sender, j: [o],
                      local_of=lambda r, o, me: [])
    chip_sum = []
    for nm, a, t, dt in zip(names, split, theirs, wire):
        own = lax.dynamic_index_in_dim(a, c, axis=1, keepdims=False)
        flat = lambda v: v.reshape(-1, v.shape[-1])
        chip_sum.append(sum_slots(f"sum2_{nm}", None, dt, [flat(own), flat(t)]).reshape(t.shape))
    landed = exchange("scatter_grads", chip_sum,
                      [jax.ShapeDtypeStruct((len(CHIP_MASKS),) + a.shape[1:], a.dtype) for a in chip_sum], CHIP_MASKS, 1,
                      src_of=lambda r, me, j: [r.at[_chip(_flip(me, CHIP_MASKS[j]))]],
                      dst_of=lambda o, sender, j: [o.at[j]], local_of=lambda r, o, me: [])
    halves = [sum_slots(f"sum4_{nm}", p, F32, [lax.dynamic_index_in_dim(a, _chip((x, y, c)), axis=0, keepdims=False)])
              for nm, p, a in zip(names, landed, chip_sum)]
    others = exchange("grad_final_swap", halves, [jax.ShapeDtypeStruct(a.shape, F32) for a in halves], ((0, 0, 1),), 1,
                      src_of=lambda r, me, j: [r], dst_of=lambda o, sender, j: [o], local_of=lambda r, o, me: [])
    return [jnp.concatenate([jnp.where(c == 0, h, o), jnp.where(c == 0, o, h)], axis=0) for h, o in zip(halves, others)]


def gather_all(arrays):
    outs = [jax.ShapeDtypeStruct((8,) + a.shape, a.dtype) for a in arrays]
    return exchange("gather_replicated", arrays, outs, ALL_MASKS, 1,
                    src_of=lambda r, me, peer: [r],
                    dst_of=lambda o, sender, j: [o.at[_devno(sender)]],
                    local_of=lambda r, o, me: [(r, o.at[_devno(me)])])


def _unshard_cols(g):
    return jnp.transpose(g, (1, 0, 2)).reshape(g.shape[1], -1)


def _shard_cols(a):
    return jnp.transpose(a.reshape(a.shape[0], N_CHIPS, -1), (1, 0, 2))


def _forward_backward(x, tgt, W):
    S = x.shape[0]
    G = {}
    sd = jax.ShapeDtypeStruct

    def ffn(xin, l, j):
        return ffn_fwd(xin, W["ffn_norm"][l][j], W["ffn_w_gate"], W["ffn_w_up"], W["ffn_w_down"], l, j)

    ffn_dw = [tuple(lax.empty(W[n].shape, F32) for n in ("ffn_w_gate", "ffn_w_up", "ffn_w_down"))]

    def ffn_back(xin, dout, l, j):
        gn = W["ffn_norm"][l][j]
        dh, *dws = ffn_bwd(xin, gn, W["ffn_w_gate"], W["ffn_w_up"], W["ffn_w_down"], dout, ffn_dw[0], l, j)
        ffn_dw[0] = tuple(dws)
        dx, G[("ffn_norm", l, j)] = norm_bwd(f"ffn_norm_bwd_{l}{j}", xin, gn, dh, dout)
        return dx

    x0 = x
    x1 = ffn(x0, 0, 0)
    g0 = W["mix_norm"][0]
    sbq, sbk, sbv = tile_fwd(f_attn_sb, "attn_in_sb", [x1], [g0, W["attn_w_in"][0]], [sd((S, SB_W), F32)] * 3, 256)
    dl_shape = sd((DL_PAIRS, S, 128), F32)
    qn, = tile_fwd(f_attn_qk, "attn_in_q", [x1], [g0, W["attn_w_in"][1], W["attn_q_norm"]], [dl_shape], 256)
    kn, = tile_fwd(f_attn_qk, "attn_in_k", [x1], [g0, W["attn_w_in"][2], W["attn_k_norm"]], [dl_shape], 256)
    vv, = tile_fwd(f_attn_v, "attn_in_v", [x1], [g0, W["attn_w_in"][3]], [dl_shape], 256)
    oa = sb_fwd(sbq, sbk, sbv)
    qs, ks, vs = (reorder(nm, t, DIL, False) for nm, t in (("sub_q", qn), ("sub_k", kn), ("sub_v", vv)))
    o_s, lse_s = dil_fwd(qs, ks, vs, W["bias_mat"])
    o_n, lse_n = reorder("nat_o", o_s, DIL, True), reorder("nat_lse", lse_s, DIL, True)
    x2, = tile_fwd(f_attn_out, "attn_out", [x1, oa, o_n, lse_n], [W["attn_w_out"]], [sd((S, D), F32)], 256)
    x3 = ffn(x2, 0, 1)
    x4 = ffn(x3, 1, 0)
    g1 = W["mix_norm"][1]
    h, hs = norm_shift_fwd(x4, g1)
    mix = W["rw_mix"]
    r, = tile_fwd(f_rw_proj, "rw_proj_r", [h, hs], [mix[0:1], W["rw_wr"]], [sd((S, D), F32)], 256)
    k, = tile_fwd(f_rw_proj, "rw_proj_k", [h, hs], [mix[2:3], W["rw_wk"]], [sd((S, D), F32)], 256)
    v, = tile_fwd(f_rw_proj, "rw_proj_v", [h, hs], [mix[3:4], W["rw_wv"]], [sd((S, D), F32)], 256)
    mix3 = jnp.concatenate([mix[1:2], mix[4:5], mix[5:6]], axis=0)
    mid_w = [mix3, W["rw_w0"], W["rw_a0"], W["rw_kk"], W["rw_ka"], W["rw_w1"], W["rw_w2"], W["rw_a1"], W["rw_a2"],
             W["rw_g1"], W["rw_g2"]]
    hshape = sd((RW_H, S, HEAD), F32)
    mid_tiles = [h, hs, r, k, v]
    rh, lwh, kh, vh, ah, bh, gate = tile_fwd(f_rw_mid, "rw_mid", mid_tiles, mid_w, [hshape] * 6 + [sd((S, D), F32)], 128)
    yh, states = rwkv_fwd(rh, lwh, kh, vh, ah, bh)
    post_w = [W["rw_lnx_g"], W["rw_lnx_b"], W["rw_rk"], W["rw_wo"]]
    post_tiles = [yh, rh, kh, vh, gate, x4]
    x5, = tile_fwd(f_rw_post, "rw_post", post_tiles, post_w, [sd((S, D), F32)], 128)
    x6 = ffn(x5, 1, 1)
    dx6, loss_part = loss_head(x6, tgt)

    dx5 = ffn_back(x5, dx6, 1, 1)
    (dyh, drh, dkh, dvh, dgate, dx4), (d_lng, d_lnb, d_rk, d_wo) = tile_bwd(
        f_rw_post, "rw_post_bwd", post_tiles, post_w, [dx5], 128, [True] * 6, [True] * 4)
    drh2, dlwh, dkh2, dvh2, dah, dbh = rwkv_bwd(rh, lwh, kh, vh, ah, bh, states, dyh)
    mid_cts = [(drh, drh2), dlwh, (dkh, dkh2), (dvh, dvh2), dah, dbh, dgate]
    (dh, dhs, dr, dk, dv), dmid_w = tile_bwd(f_rw_mid, "rw_mid_bwd", mid_tiles, mid_w, mid_cts, 128,
                                             [True] * 5, [True] * len(mid_w))
    dmix = {}
    for nm, ct, row, wname in (("r", dr, 0, "rw_wr"), ("k", dk, 2, "rw_wk"), ("v", dv, 3, "rw_wv")):
        (dh, dhs), (dmix[row], G[wname]) = tile_bwd(
            f_rw_proj, f"rw_proj_{nm}_bwd", [h, hs], [mix[row:row + 1], W[wname]], [ct], 256,
            [True, True], [True, True], acc={0: dh, 1: dhs})
    dx4, G[("mix_norm", 1)] = norm_shift_bwd(x4, g1, dh, dhs, dx4)
    dmix3 = dmid_w[0]
    G["rw_mix"] = jnp.concatenate([dmix[0], dmix3[0:1], dmix[2], dmix[3], dmix3[1:2], dmix3[2:3]], axis=0)
    for nm, gv in zip(("rw_w0", "rw_a0", "rw_kk", "rw_ka", "rw_w1", "rw_w2", "rw_a1", "rw_a2", "rw_g1", "rw_g2"), dmid_w[1:]):
        G[nm] = gv
    G["rw_lnx_g"], G["rw_lnx_b"], G["rw_rk"], G["rw_wo"] = d_lng, d_lnb, d_rk, d_wo
    dx3 = ffn_back(x3, dx4, 1, 0)
    dx2 = ffn_back(x2, dx3, 0, 1)
    (dx1, doa, do_n, dlse_n), (G["attn_w_out"],) = tile_bwd(
        f_attn_out, "attn_out_bwd", [x1, oa, o_n, lse_n], [W["attn_w_out"]], [dx2], 256, [True] * 4, [True])
    do_s, dlse_s = reorder("sub_do", do_n, DIL, False), reorder("sub_dlse", dlse_n, DIL, False)
    dqs, dks, dvs, dsum = dil_bwd(qs, ks, vs, W["bias_mat"], o_s, lse_s, do_s, dlse_s)
    G["rel_bias"] = bias_grad(dsum, W["buckets"])
    dqn, dkn, dvv = (reorder(nm, t, DIL, True) for nm, t in (("nat_dq", dqs), ("nat_dk", dks), ("nat_dv", dvs)))
    dsbq, dsbk, dsbv = sb_bwd(sbq, sbk, sbv, doa)
    dg0 = []
    dwin = []
    (dx1,), (dg, dw) = tile_bwd(f_attn_sb, "attn_in_sb_bwd", [x1], [g0, W["attn_w_in"][0]], [dsbq, dsbk, dsbv], 256,
                                [True], [True, True], acc={0: dx1})
    dg0.append(dg), dwin.append(dw)
    (dx1,), (dg, dw, G["attn_q_norm"]) = tile_bwd(f_attn_qk, "attn_in_q_bwd", [x1], [g0, W["attn_w_in"][1], W["attn_q_norm"]],
                                                  [dqn], 256, [True], [True] * 3, acc={0: dx1})
    dg0.append(dg), dwin.append(dw)
    (dx1,), (dg, dw, G["attn_k_norm"]) = tile_bwd(f_attn_qk, "attn_in_k_bwd", [x1], [g0, W["attn_w_in"][2], W["attn_k_norm"]],
                                                  [dkn], 256, [True], [True] * 3, acc={0: dx1})
    dg0.append(dg), dwin.append(dw)
    (dx1,), (dg, dw) = tile_bwd(f_attn_v, "attn_in_v_bwd", [x1], [g0, W["attn_w_in"][3]], [dvv], 256,
                                [True], [True, True], acc={0: dx1})
    dg0.append(dg), dwin.append(dw)
    G[("mix_norm", 0)] = dg0
    G["attn_w_in"] = dwin
    dx0 = ffn_back(x0, dx1, 0, 0)
    G["ffn_w_gate"], G["ffn_w_up"], G["ffn_w_down"] = ffn_dw[0]
    return loss_part, dx0, G


VEC_ROWS = ("ffn_norm", "rw_mix", "rw_w0", "rw_a0", "rw_kk", "rw_ka", "rw_lnx_g", "rw_lnx_b")


def kernel(x, ffn_norm, ffn_w_gate, ffn_w_up, ffn_w_down, mix_norm, rel_bias, attn_w_in, attn_q_norm, attn_k_norm, attn_w_out, rw_mix, rw_w0, rw_w1, rw_w2, rw_a0, rw_a1, rw_a2, rw_g1, rw_g2, rw_kk, rw_ka, rw_rk, rw_wr, rw_wk, rw_wv, rw_wo, rw_lnx_g, rw_lnx_b, loss_target, m_ffn_norm, m_ffn_w_gate, m_ffn_w_up, m_ffn_w_down, m_mix_norm, m_rel_bias, m_attn_w_in, m_attn_q_norm, m_attn_k_norm, m_attn_w_out, m_rw_mix, m_rw_w0, m_rw_w1, m_rw_w2, m_rw_a0, m_rw_a1, m_rw_a2, m_rw_g1, m_rw_g2, m_rw_kk, m_rw_ka, m_rw_rk, m_rw_wr, m_rw_wk, m_rw_wv, m_rw_wo, m_rw_lnx_g, m_rw_lnx_b, v_ffn_norm, v_ffn_w_gate, v_ffn_w_up, v_ffn_w_down, v_mix_norm, v_rel_bias, v_attn_w_in, v_attn_q_norm, v_attn_k_norm, v_attn_w_out, v_rw_mix, v_rw_w0, v_rw_w1, v_rw_w2, v_rw_a0, v_rw_a1, v_rw_a2, v_rw_g1, v_rw_g2, v_rw_kk, v_rw_ka, v_rw_rk, v_rw_wr, v_rw_wk, v_rw_wv, v_rw_wo, v_rw_lnx_g, v_rw_lnx_b):
    names = ["ffn_norm", "ffn_w_gate", "ffn_w_up", "ffn_w_down", "mix_norm", "rel_bias", "attn_w_in", "attn_q_norm",
             "attn_k_norm", "attn_w_out", "rw_mix", "rw_w0", "rw_w1", "rw_w2", "rw_a0", "rw_a1", "rw_a2", "rw_g1", "rw_g2",
             "rw_kk", "rw_ka", "rw_rk", "rw_wr", "rw_wk", "rw_wv", "rw_wo", "rw_lnx_g", "rw_lnx_b"]
    loc = locals()
    w = {n: loc[n] for n in names}
    mom = {n: loc["m_" + n] for n in names}
    vel = {n: loc["v_" + n] for n in names}
    S = x.shape[1]

    vec_shard = jnp.concatenate([w[n].reshape(-1, 256) for n in VEC_ROWS], axis=0)
    mats = ["ffn_w_gate", "ffn_w_up", "ffn_w_down", "attn_w_in", "attn_w_out", "rw_w1", "rw_w2", "rw_a1", "rw_a2",
            "rw_g1", "rw_g2", "rw_wr", "rw_wk", "rw_wv", "rw_wo"]
    send = [vec_shard] + [w[n].reshape(-1, w[n].shape[-1]).astype(BF16) for n in mats]
    got = gather_chips(send)
    vec_full = _unshard_cols(got[0])
    gm = dict(zip(mats, got[1:]))
    W = {
        "ffn_norm": [[vec_full[2 * l + j][None] for j in range(2)] for l in range(2)],
        "ffn_w_gate": gm["ffn_w_gate"].reshape(N_CHIPS, 2, 2, D, FF_SHARD),
        "ffn_w_up": gm["ffn_w_up"].reshape(N_CHIPS, 2, 2, D, FF_SHARD),
        "ffn_w_down": gm["ffn_w_down"].reshape(N_CHIPS, 2, 2, FF_SHARD, D),
        "mix_norm": [mix_norm[0:1], mix_norm[1:2]],
        "attn_w_in": [gm["attn_w_in"][p] for p in range(N_CHIPS)],
        "attn_q_norm": attn_q_norm, "attn_k_norm": attn_k_norm,
        "attn_w_out": _unshard_cols(gm["attn_w_out"]),
        "rw_mix": vec_full[4:10],
        "rw_w1": gm["rw_w1"].reshape(D, -1), "rw_a1": gm["rw_a1"].reshape(D, -1), "rw_g1": gm["rw_g1"].reshape(D, -1),
        "rw_w2": _unshard_cols(gm["rw_w2"]), "rw_a2": _unshard_cols(gm["rw_a2"]), "rw_g2": _unshard_cols(gm["rw_g2"]),
        "rw_wr": gm["rw_wr"].reshape(D, D), "rw_wk": gm["rw_wk"].reshape(D, D), "rw_wv": gm["rw_wv"].reshape(D, D),
        "rw_wo": gm["rw_wo"].reshape(D, D),
        "rw_rk": rw_rk[0][:, None, :],
    }
    for i, n in enumerate(("rw_w0", "rw_a0", "rw_kk", "rw_ka", "rw_lnx_g", "rw_lnx_b")):
        W[n] = vec_full[10 + i][None]
    buckets = _bucket_maps()
    W["buckets"] = buckets
    W["bias_mat"] = bias_table(rel_bias, buckets)

    loss_part, dx, G = _forward_backward(x[0], loss_target[0], W)
    loss = lax.psum(loss_part[0, 0], ("x", "y", "c"))

    vec_rows = [G[("ffn_norm", l, j)] for l in range(2) for j in range(2)] + [G["rw_mix"]] + \
               [G[n] for n in ("rw_w0", "rw_a0", "rw_kk", "rw_ka", "rw_lnx_g", "rw_lnx_b")]
    full = {
        "vec": _shard_cols(jnp.concatenate(vec_rows, axis=0)),
        "ffn_w_gate": G["ffn_w_gate"], "ffn_w_up": G["ffn_w_up"], "ffn_w_down": G["ffn_w_down"],
        "attn_w_in": jnp.stack(G["attn_w_in"]),
        "attn_w_out": _shard_cols(G["attn_w_out"]),
        "rw_w1": G["rw_w1"].reshape(N_CHIPS, 256, -1), "rw_a1": G["rw_a1"].reshape(N_CHIPS, 256, -1),
        "rw_g1": G["rw_g1"].reshape(N_CHIPS, 256, -1),
        "rw_w2": _shard_cols(G["rw_w2"]), "rw_a2": _shard_cols(G["rw_a2"]), "rw_g2": _shard_cols(G["rw_g2"]),
        "rw_wr": G["rw_wr"].reshape(N_CHIPS, 256, D), "rw_wk": G["rw_wk"].reshape(N_CHIPS, 256, D),
        "rw_wv": G["rw_wv"].reshape(N_CHIPS, 256, D), "rw_wo": G["rw_wo"].reshape(N_CHIPS, 256, D),
    }
    order = ["vec"] + mats
    summed = reduce_chips(order, [full[n].reshape(N_CHIPS, -1, full[n].shape[-1]) for n in order],
                          [F32] + [BF16] * len(mats))

    rep = jnp.concatenate([G[("mix_norm", 0)][0] + G[("mix_norm", 0)][1] + G[("mix_norm", 0)][2] + G[("mix_norm", 0)][3],
                           G[("mix_norm", 1)]], axis=0).reshape(16, 128)
    rep = jnp.concatenate([rep, G["rel_bias"], jnp.pad(G["attn_q_norm"], ((0, 0), (0, 64))),
                           jnp.pad(G["attn_k_norm"], ((0, 0), (0, 64))), G["rw_rk"].reshape(8, 128),
                           jnp.zeros((2, 128), F32)], axis=0)
    rep_sum = sum_slots("sum_replicated", gather_all([rep])[0])
    g_rep = {
        "mix_norm": rep_sum[0:16].reshape(2, D),
        "rel_bias": jnp.transpose(rep_sum[16:28, :N_BUCKETS]),
        "attn_q_norm": rep_sum[28:29, :HEAD], "attn_k_norm": rep_sum[29:30, :HEAD],
        "rw_rk": rep_sum[30:38].reshape(1, RW_H, HEAD),
    }

    out = {}

    def adam(n, ga, gb):
        shp = w[n].shape
        to2 = lambda a: a.reshape(-1, shp[-1])
        res = adam_step(f"adam_{n}", to2(ga), None if gb is None else to2(gb), to2(w[n]), to2(mom[n]), to2(vel[n]))
        out[n] = tuple(r.reshape(shp) for r in res)

    part = dict(zip(order, summed))
    for n in mats:
        adam(n, part[n], None)
    rows = {"ffn_norm": (0, 4), "rw_mix": (4, 10), "rw_w0": (10, 11), "rw_a0": (11, 12), "rw_kk": (12, 13),
            "rw_ka": (13, 14), "rw_lnx_g": (14, 15), "rw_lnx_b": (15, 16)}
    for n, (lo, hi) in rows.items():
        adam(n, part["vec"][lo:hi], None)
    for n, gv in g_rep.items():
        adam(n, gv, None)

    grads = [out[n][0] for n in names]
    deltas = [out[n][1] for n in names]
    new_m = [out[n][2] for n in names]
    new_v = [out[n][3] for n in names]
    return (loss, dx[None], *grads, *deltas, *new_m, *new_v)
```

```python
import functools
import math

import jax
import jax.numpy as jnp
from jax import lax
from jax.experimental import pallas as pl
from jax.experimental.pallas import tpu as pltpu

F32, BF16 = jnp.float32, jnp.bfloat16
HI = lax.Precision.HIGHEST
MESH = pl.DeviceIdType.MESH

D = 1024
HEAD = 64
N_CHIPS = 4
FF_SHARD = 704
SB_W = 256
DL_HEADS = 12
DL_PAIRS = 6
DIL = (1, 4, 16)
QBLK = 128
N_BUCKETS = 32
MAX_DISTANCE = 2048
RW_H = 16
RW_CHUNK = 64
NORM_EPS = 1e-6
GN_EPS = 64e-5
NEG_INF = -1e30
VMEM_LIMIT = 56 * 1024 * 1024

ADAM_LR, ADAM_B1, ADAM_B2, ADAM_EPS, ADAM_WD, ADAM_STEP = 0.001, 0.9, 0.999, 1e-08, 0.01, 10


def _cp(sem):
    return pltpu.CompilerParams(dimension_semantics=sem, vmem_limit_bytes=VMEM_LIMIT)


def _dg(a, b, dims, prec=None):
    return lax.dot_general(a, b, (dims, ((), ())), precision=prec, preferred_element_type=F32)


def _bdot(a, b, dims):
    return _dg(a.astype(BF16), b.astype(BF16), dims)


@jax.custom_vjp
def mm(a, b):
    return _bdot(a, b, ((1,), (0,)))


def _mm_fwd(a, b):
    return _bdot(a, b, ((1,), (0,))), (a, b)


def _mm_bwd(res, g):
    a, b = res
    return _bdot(g, b, ((1,), (1,))), _bdot(a, g, ((0,), (0,)))


mm.defvjp(_mm_fwd, _mm_bwd)


def rms(x, g):
    return x * lax.rsqrt(jnp.mean(x * x, axis=-1, keepdims=True) + NORM_EPS) * g


def group_sum(x, nh):
    w = x.shape[-1]
    e = (lax.broadcasted_iota(jnp.int32, (w, nh), 0) // HEAD == lax.broadcasted_iota(jnp.int32, (w, nh), 1)).astype(F32)
    s = _dg(x, e, ((1,), (0,)), HI)
    return _dg(s, e, ((1,), (1,)), HI)


def softplus(u):
    return jnp.maximum(u, 0.0) + jnp.log1p(jnp.exp(-jnp.abs(u)))


def to_heads(t, nh=RW_H):
    return jnp.stack([t[:, HEAD * h:HEAD * (h + 1)] for h in range(nh)])


def from_heads(t):
    return jnp.concatenate([t[h] for h in range(t.shape[0])], axis=-1)


def _tile_spec(shape, tm):
    if len(shape) == 2:
        return pl.BlockSpec((tm, shape[1]), lambda t: (t, 0))
    return pl.BlockSpec((shape[0], tm, shape[2]), lambda t: (0, t, 0))


def _full_spec(shape):
    nd = len(shape)
    return pl.BlockSpec(tuple(shape), lambda t: (0,) * nd)


def _rows(a):
    return a.shape[0] if a.ndim == 2 else a.shape[1]


def tile_fwd(f, name, tiles, weights, outs, tm):
    nt, nw = len(tiles), len(weights)

    def body(*refs):
        tv = [r[...] for r in refs[:nt]]
        wv = [r[...].astype(F32) for r in refs[nt:nt + nw]]
        res = f(*tv, *wv)
        if not isinstance(res, (tuple, list)):
            res = (res,)
        for o, v in zip(refs[nt + nw:], res):
            o[...] = v.astype(o.dtype)

    return pl.pallas_call(
        body, name=name, grid=(_rows(tiles[0]) // tm,),
        in_specs=[_tile_spec(a.shape, tm) for a in tiles] + [_full_spec(w.shape) for w in weights],
        out_specs=[_tile_spec(o.shape, tm) for o in outs],
        out_shape=list(outs),
        compiler_params=_cp(("parallel",)),
    )(*tiles, *weights)


def tile_bwd(f, name, tiles, weights, cts, tm, dt, dw, acc=None):
    acc = acc or {}
    groups = [c if isinstance(c, tuple) else (c,) for c in cts]
    cts = [a for grp in groups for a in grp]
    nt, nw, nc = len(tiles), len(weights), len(cts)
    acc_idx = sorted(acc)
    na = len(acc_idx)
    dti = [i for i in range(nt) if dt[i]]
    dwi = [i for i in range(nw) if dw[i]]

    def body(*refs):
        tv = [r[...] for r in refs[:nt]]
        wv = [r[...].astype(F32) for r in refs[nt:nt + nw]]
        crefs = list(refs[nt + nw:nt + nw + nc])
        cv = []
        for grp in groups:
            terms = [crefs.pop(0)[...] for _ in grp]
            cv.append(functools.reduce(lambda a, b: a + b, terms))
        av = {i: r[...] for i, r in zip(acc_idx, refs[nt + nw + nc:nt + nw + nc + na])}
        orefs = refs[nt + nw + nc + na:]

        def g(*diff):
            t2, w2 = list(tv), list(wv)
            for i, v in zip(dti, diff[:len(dti)]):
                t2[i] = v
            for i, v in zip(dwi, diff[len(dti):]):
                w2[i] = v
            res = f(*t2, *w2)
            return tuple(res) if isinstance(res, (tuple, list)) else (res,)

        _, vjp = jax.vjp(g, *[tv[i] for i in dti], *[wv[i] for i in dwi])
        grads = vjp(tuple(cv))
        for k, i in enumerate(dti):
            gt = grads[k]
            if i in av:
                gt = gt + av[i]
            orefs[k][...] = gt
        first = pl.program_id(0) == 0
        for k, i in enumerate(dwi):
            o = orefs[len(dti) + k]
            gw = grads[len(dti) + k]

            @pl.when(first)
            def _(o=o, gw=gw):
                o[...] = gw

            @pl.when(jnp.logical_not(first))
            def _(o=o, gw=gw):
                o[...] += gw

    out_shape = [jax.ShapeDtypeStruct(tiles[i].shape, F32) for i in dti] + \
                [jax.ShapeDtypeStruct(weights[i].shape, F32) for i in dwi]
    res = pl.pallas_call(
        body, name=name, grid=(_rows(tiles[0]) // tm,),
        in_specs=[_tile_spec(a.shape, tm) for a in tiles] + [_full_spec(w.shape) for w in weights] +
                 [_tile_spec(c.shape, tm) for c in cts] + [_tile_spec(tiles[i].shape, tm) for i in acc_idx],
        out_specs=[_tile_spec(tiles[i].shape, tm) for i in dti] + [_full_spec(weights[i].shape) for i in dwi],
        out_shape=out_shape,
        compiler_params=_cp(("arbitrary",)),
    )(*tiles, *weights, *cts, *[acc[i] for i in acc_idx])
    return list(res[:len(dti)]), list(res[len(dti):])


def _ffn_wspec(l, j, rows, cols, cfirst):
    if cfirst:
        return pl.BlockSpec((1, 1, 1, rows, cols), lambda c, t: (c, l, j, 0, 0))
    return pl.BlockSpec((1, 1, 1, rows, cols), lambda t, c: (c, l, j, 0, 0))


def ffn_fwd(x, g, wg, wu, wd, l, j, tm=512):
    S = x.shape[0]

    def body(x_ref, g_ref, wg_ref, wu_ref, wd_ref, o_ref, h_ref, acc_ref):
        c = pl.program_id(1)

        @pl.when(c == 0)
        def _():
            h_ref[...] = rms(x_ref[...], g_ref[...]).astype(BF16)
            acc_ref[...] = jnp.zeros_like(acc_ref)

        h = h_ref[...]
        a = _bdot(h, wg_ref[0, 0, 0], ((1,), (0,)))
        b = _bdot(h, wu_ref[0, 0, 0], ((1,), (0,)))
        y = a * jax.nn.sigmoid(a) * b
        acc_ref[...] += _bdot(y, wd_ref[0, 0, 0], ((1,), (0,)))

        @pl.when(c == N_CHIPS - 1)
        def _():
            o_ref[...] = x_ref[...] + 0.5 * acc_ref[...]

    return pl.pallas_call(
        body, name=f"ffn_fwd_{l}{j}", grid=(S // tm, N_CHIPS),
        in_specs=[pl.BlockSpec((tm, D), lambda t, c: (t, 0)), pl.BlockSpec((1, D), lambda t, c: (0, 0)),
                  _ffn_wspec(l, j, D, FF_SHARD, False), _ffn_wspec(l, j, D, FF_SHARD, False),
                  _ffn_wspec(l, j, FF_SHARD, D, False)],
        out_specs=pl.BlockSpec((tm, D), lambda t, c: (t, 0)),
        out_shape=jax.ShapeDtypeStruct((S, D), F32),
        scratch_shapes=[pltpu.VMEM((tm, D), BF16), pltpu.VMEM((tm, D), F32)],
        compiler_params=_cp(("parallel", "arbitrary")),
    )(x, g, wg, wu, wd)


def ffn_bwd(x, g, wg, wu, wd, dout, dws, l, j, tm=512):
    S = x.shape[0]

    def body(x_ref, g_ref, wg_ref, wu_ref, wd_ref, do_ref, _g, _u, _d, dh_ref, dwg_ref, dwu_ref, dwd_ref):
        dwg_ref, dwu_ref, dwd_ref = dwg_ref.at[0, 0], dwu_ref.at[0, 0], dwd_ref.at[0, 0]
        t = pl.program_id(1)
        h = rms(x_ref[...], g_ref[...]).astype(BF16)
        wgv, wuv, wdv = wg_ref[0, 0, 0], wu_ref[0, 0, 0], wd_ref[0, 0, 0]
        a = _bdot(h, wgv, ((1,), (0,)))
        b = _bdot(h, wuv, ((1,), (0,)))
        sig = jax.nn.sigmoid(a)
        s = a * sig
        dyd = 0.5 * do_ref[...]
        dy = _bdot(dyd, wdv, ((1,), (1,)))
        dwd = _bdot(s * b, dyd, ((0,), (0,)))
        db = dy * s
        da = dy * b * (sig * (1.0 + a * (1.0 - sig)))
        dwg = _bdot(h, da, ((0,), (0,)))
        dwu = _bdot(h, db, ((0,), (0,)))
        dh_ref[0] = _bdot(da, wgv, ((1,), (1,))) + _bdot(db, wuv, ((1,), (1,)))

        @pl.when(t == 0)
        def _():
            dwg_ref[0] = dwg
            dwu_ref[0] = dwu
            dwd_ref[0] = dwd

        @pl.when(t != 0)
        def _():
            dwg_ref[0] += dwg
            dwu_ref[0] += dwu
            dwd_ref[0] += dwd

    return pl.pallas_call(
        body, name=f"ffn_bwd_{l}{j}", grid=(N_CHIPS, S // tm),
        in_specs=[pl.BlockSpec((tm, D), lambda c, t: (t, 0)), pl.BlockSpec((1, D), lambda c, t: (0, 0)),
                  _ffn_wspec(l, j, D, FF_SHARD, True), _ffn_wspec(l, j, D, FF_SHARD, True),
                  _ffn_wspec(l, j, FF_SHARD, D, True), pl.BlockSpec((tm, D), lambda c, t: (t, 0))] +
                 [pl.BlockSpec(memory_space=pl.ANY)] * 3,
        out_specs=[pl.BlockSpec((1, tm, D), lambda c, t: (c, t, 0)),
                   _ffn_wspec(l, j, D, FF_SHARD, True), _ffn_wspec(l, j, D, FF_SHARD, True),
                   _ffn_wspec(l, j, FF_SHARD, D, True)],
        out_shape=[jax.ShapeDtypeStruct((N_CHIPS, S, D), F32)] + [jax.ShapeDtypeStruct(a.shape, F32) for a in dws],
        input_output_aliases={6: 1, 7: 2, 8: 3},
        compiler_params=_cp(("parallel", "arbitrary")),
    )(x, g, wg, wu, wd, dout, *dws)


def norm_bwd(name, x, g, dh_parts, dres, tm=256):
    S = x.shape[0]
    P = dh_parts.shape[0]

    def body(x_ref, g_ref, dh_ref, dr_ref, dx_ref, dg_ref):
        dh = dh_ref[0]
        for p in range(1, P):
            dh = dh + dh_ref[p]
        _, vjp = jax.vjp(rms, x_ref[...], g_ref[...])
        dx, dg = vjp(dh)
        dx_ref[...] = dr_ref[...] + dx

        @pl.when(pl.program_id(0) == 0)
        def _():
            dg_ref[...] = dg

        @pl.when(pl.program_id(0) != 0)
        def _():
            dg_ref[...] += dg

    return pl.pallas_call(
        body, name=name, grid=(S // tm,),
        in_specs=[pl.BlockSpec((tm, D), lambda t: (t, 0)), pl.BlockSpec((1, D), lambda t: (0, 0)),
                  pl.BlockSpec((P, tm, D), lambda t: (0, t, 0)), pl.BlockSpec((tm, D), lambda t: (t, 0))],
        out_specs=[pl.BlockSpec((tm, D), lambda t: (t, 0)), pl.BlockSpec((1, D), lambda t: (0, 0))],
        out_shape=[jax.ShapeDtypeStruct((S, D), F32), jax.ShapeDtypeStruct((1, D), F32)],
        compiler_params=_cp(("arbitrary",)),
    )(x, g, dh_parts, dres)


def f_attn_sb(x, g, w):
    pr = mm(rms(x, g), w)
    return pr[:, :SB_W], pr[:, SB_W:2 * SB_W], pr[:, 2 * SB_W:]


def _pairs(y):
    return jnp.stack([y[:, 128 * j:128 * (j + 1)] for j in range(DL_PAIRS)])


def f_attn_qk(x, g, w, nrm):
    pr = mm(rms(x, g), w)
    ms = group_sum(pr * pr, DL_HEADS) * (1.0 / HEAD)
    return _pairs(pr * lax.rsqrt(ms + NORM_EPS) * jnp.concatenate([nrm] * DL_HEADS, axis=1))


def f_attn_v(x, g, w):
    return _pairs(mm(rms(x, g), w))


def _masked(strict, x):
    return x if strict is None else jnp.where(strict, x, 0.0)


def _sb_tiles(q, k, strict):
    z = _bdot(q, k, ((1,), (1,))) * (HEAD ** -0.5)
    return z, _masked(strict, -softplus(z))


def _tri(n, upper):
    r = lax.broadcasted_iota(jnp.int32, (n, n), 0)
    c = lax.broadcasted_iota(jnp.int32, (n, n), 1)
    return ((r > c) if upper else (r < c)).astype(BF16)


def _tri_sums(xs, tri):
    x = jnp.concatenate(xs, axis=0)
    hi, lo = _split2(x)
    y = _dg(hi, tri, ((1,), (0,))) + _dg(lo, tri, ((1,), (0,)))
    n = xs[0].shape[0]
    return [y[n * i:n * (i + 1)] for i in range(len(xs))]


SB_UNROLL = 4


def _sb_sweep(step, first, count, carry, direction):
    rem = count % SB_UNROLL
    carry = lax.fori_loop(0, rem, lambda i, c: step(first + direction * i, c), carry)

    def group(g, c):
        base = first + direction * (rem + SB_UNROLL * g)
        for u in range(SB_UNROLL):
            c = step(base + direction * u, c)
        return c

    return lax.fori_loop(0, count // SB_UNROLL, group, carry)


def sb_fwd(q, k, v, tb=QBLK):
    S = q.shape[0]
    nh = SB_W // HEAD

    def body(q_ref, k_ref, v_ref, o_ref):
        qb = pl.program_id(0)
        diag = lax.broadcasted_iota(jnp.int32, (tb, tb), 1) < lax.broadcasted_iota(jnp.int32, (tb, tb), 0)
        after_mat = _tri(tb, True)
        sls = [slice(HEAD * h, HEAD * (h + 1)) for h in range(nh)]
        qs = [q_ref[:, sl] for sl in sls]

        def step(kb, carry, strict):
            accs, runs = carry
            rows = pl.ds(pl.multiple_of(kb * tb, tb), tb)
            kblk, vblk = k_ref[rows, :], v_ref[rows, :]
            tiles = [_sb_tiles(qs[h], kblk[:, sls[h]], strict) for h in range(nh)]
            afters = _tri_sums([t[1] for t in tiles], after_mat)
            new_accs, new_runs = [], []
            for h, (z, keep) in enumerate(tiles):
                w = _masked(strict, jnp.exp(z + keep + afters[h] + runs[h]))
                new_accs.append(accs[h] + _bdot(w, vblk[:, sls[h]], ((1,), (0,))))
                new_runs.append(runs[h] + jnp.sum(keep, axis=1, keepdims=True))
            return tuple(new_accs), tuple(new_runs)

        init = (tuple(jnp.zeros((tb, HEAD), F32) for _ in range(nh)), tuple(jnp.zeros((tb, 1), F32) for _ in range(nh)))
        carry = step(qb, init, diag)
        accs, _ = _sb_sweep(lambda kb, c: step(kb, c, None), qb - 1, qb, carry, -1)
        o_ref[...] = jnp.concatenate(accs, axis=1)

    return pl.pallas_call(
        body, name="sb_fwd", grid=(S // tb,),
        in_specs=[pl.BlockSpec((tb, SB_W), lambda i: (i, 0)), pl.BlockSpec((S, SB_W), lambda i: (0, 0)),
                  pl.BlockSpec((S, SB_W), lambda i: (0, 0))],
        out_specs=pl.BlockSpec((tb, SB_W), lambda i: (i, 0)),
        out_shape=jax.ShapeDtypeStruct((S, SB_W), F32),
        compiler_params=_cp(("parallel",)),
    )(q, k, v)


def sb_bwd(q, k, v, do, tb=QBLK):
    S = q.shape[0]
    nh = SB_W // HEAD
    scale = HEAD ** -0.5

    def body(q_ref, k_ref, v_ref, do_ref, dq_ref, dk_ref, dv_ref, g_scr):
        qb = pl.program_id(0)

        @pl.when(qb == 0)
        def _():
            dk_ref[...] = jnp.zeros_like(dk_ref)
            dv_ref[...] = jnp.zeros_like(dv_ref)

        diag = lax.broadcasted_iota(jnp.int32, (tb, tb), 1) < lax.broadcasted_iota(jnp.int32, (tb, tb), 0)
        after_mat = _tri(tb, True)
        before_mat = _tri(tb, False)
        sls = [slice(HEAD * h, HEAD * (h + 1)) for h in range(nh)]
        qs = [q_ref[:, sl] for sl in sls]
        dos = [do_ref[:, sl] for sl in sls]

        def right_to_left(kb, runs, strict):
            rows = pl.ds(pl.multiple_of(kb * tb, tb), tb)
            kblk, vblk = k_ref[rows, :], v_ref[rows, :]
            tiles = [_sb_tiles(qs[h], kblk[:, sls[h]], strict) for h in range(nh)]
            afters = _tri_sums([t[1] for t in tiles], after_mat)
            dvs, new_runs = [], []
            for h, (z, keep) in enumerate(tiles):
                w = _masked(strict, jnp.exp(z + keep + afters[h] + runs[h]))
                g_scr[h, kb] = _bdot(dos[h], vblk[:, sls[h]], ((1,), (1,))) * w
                dvs.append(_bdot(w, dos[h], ((0,), (0,))))
                new_runs.append(runs[h] + jnp.sum(keep, axis=1, keepdims=True))
            dv_ref[rows, :] += jnp.concatenate(dvs, axis=1)
            return tuple(new_runs)

        zero_runs = tuple(jnp.zeros((tb, 1), F32) for _ in range(nh))
        runs = right_to_left(qb, zero_runs, diag)
        _sb_sweep(lambda kb, r: right_to_left(kb, r, None), qb - 1, qb, runs, -1)

        def left_to_right(kb, carry, strict):
            dqs, runs = carry
            rows = pl.ds(pl.multiple_of(kb * tb, tb), tb)
            kblk = k_ref[rows, :]
            gws = [g_scr[h, kb] for h in range(nh)]
            befores = _tri_sums(gws, before_mat)
            new_dqs, new_runs, dks = [], [], []
            for h in range(nh):
                kh = kblk[:, sls[h]]
                sig = jax.nn.sigmoid(_bdot(qs[h], kh, ((1,), (1,))) * scale)
                dkeep = _masked(strict, befores[h] + runs[h])
                dz = (gws[h] * (1.0 - sig) - dkeep * sig) * scale
                new_dqs.append(dqs[h] + _bdot(dz, kh, ((1,), (0,))))
                dks.append(_bdot(dz, qs[h], ((0,), (0,))))
                new_runs.append(runs[h] + jnp.sum(gws[h], axis=1, keepdims=True))
            dk_ref[rows, :] += jnp.concatenate(dks, axis=1)
            return tuple(new_dqs), tuple(new_runs)

        carry = _sb_sweep(lambda kb, c: left_to_right(kb, c, None), 0, qb,
                          (tuple(jnp.zeros((tb, HEAD), F32) for _ in range(nh)), zero_runs), 1)
        dqs, _ = left_to_right(qb, carry, diag)
        dq_ref[...] = jnp.concatenate(dqs, axis=1)

    whole = pl.BlockSpec((S, SB_W), lambda i: (0, 0))
    blk = pl.BlockSpec((tb, SB_W), lambda i: (i, 0))
    return pl.pallas_call(
        body, name="sb_bwd", grid=(S // tb,),
        in_specs=[blk, whole, whole, blk], out_specs=[blk, whole, whole],
        out_shape=[jax.ShapeDtypeStruct((S, SB_W), F32)] * 3,
        scratch_shapes=[pltpu.VMEM((nh, S // tb, tb, tb), F32)],
        compiler_params=_cp(("arbitrary",)),
    )(q, k, v, do)


def reorder(name, x, groups, inverse):
    P, S, _ = x.shape

    def body(x_ref, o_ref):
        p = pl.program_id(0)
        for gi, r in enumerate(groups):
            @pl.when(p // 2 == gi)
            def _(r=r):
                L = S // r
                if r == 1:
                    o_ref[...] = x_ref[...]
                for c in range(r if r > 1 else 0):
                    if inverse:
                        o_ref[pl.ds(c, L, stride=r), :] = x_ref[c * L:(c + 1) * L, :]
                    else:
                        o_ref[c * L:(c + 1) * L, :] = x_ref[pl.ds(c, L, stride=r), :]

    slab = pl.BlockSpec((None, S, 128), lambda p: (p, 0, 0))
    return pl.pallas_call(
        body, name=name, grid=(P,), in_specs=[slab], out_specs=slab,
        out_shape=jax.ShapeDtypeStruct(x.shape, x.dtype), compiler_params=_cp(("parallel",)),
    )(x)


def _dil_blocks(S):
    return S // QBLK


def _dil_mask(n_in_stream):
    qi = lax.broadcasted_iota(jnp.int32, (QBLK, 2 * QBLK), 0)
    kj = lax.broadcasted_iota(jnp.int32, (QBLK, 2 * QBLK), 1) - QBLK
    dist = qi - kj
    return (dist >= 0) & (dist <= QBLK) & ((n_in_stream > 0) | (kj >= 0))


def _stream_pos(gi, i, S):
    nb = jnp.where(gi == 0, S // (QBLK * DIL[0]), jnp.where(gi == 1, S // (QBLK * DIL[1]), S // (QBLK * DIL[2])))
    return i % nb


def dil_fwd(q, k, v, bias):
    S = q.shape[1]
    nblk = _dil_blocks(S)

    def body(q_ref, kc_ref, kp_ref, vc_ref, vp_ref, b_ref, o_ref, l_ref):
        gi, i = pl.program_id(0), pl.program_id(1)
        mask = _dil_mask(_stream_pos(gi, i, S))
        for j in range(2):
            q2, kc, kp, vc, vp = q_ref[j], kc_ref[j], kp_ref[j], vc_ref[j], vp_ref[j]
            os_, ls_ = [], []
            for hh in range(2):
                sl = slice(HEAD * hh, HEAD * (hh + 1))
                kw = jnp.concatenate([kp[:, sl], kc[:, sl]], axis=0)
                vw = jnp.concatenate([vp[:, sl], vc[:, sl]], axis=0)
                lg = _bdot(q2[:, sl], kw, ((1,), (1,))) * (HEAD ** -0.5) + b_ref[2 * j + hh]
                lg = jnp.where(mask, lg, NEG_INF)
                m = jnp.max(lg, axis=-1, keepdims=True)
                p = jnp.exp(lg - m)
                den = jnp.sum(p, axis=-1, keepdims=True)
                os_.append(_bdot(p / den, vw, ((1,), (0,))))
                ls_.append(jnp.broadcast_to(m + jnp.log(den), (QBLK, HEAD)))
            o_ref[j] = jnp.concatenate(os_, axis=1)
            l_ref[j] = jnp.concatenate(ls_, axis=1)

    cur = pl.BlockSpec((2, QBLK, 128), lambda g, i: (g, i, 0))
    prev = pl.BlockSpec((2, QBLK, 128), lambda g, i: (g, jnp.maximum(i - 1, 0), 0))
    return pl.pallas_call(
        body, name="dil_fwd", grid=(len(DIL), nblk),
        in_specs=[cur, cur, prev, cur, prev, pl.BlockSpec((4, QBLK, 2 * QBLK), lambda g, i: (g, 0, 0))],
        out_specs=[cur, cur],
        out_shape=[jax.ShapeDtypeStruct(q.shape, F32)] * 2,
        compiler_params=_cp(("parallel", "parallel")),
    )(q, k, k, v, v, bias)


def dil_bwd(q, k, v, bias, o, lse, do, dlse):
    S = q.shape[1]
    nblk = _dil_blocks(S)

    def body(q_ref, kc_ref, kp_ref, vc_ref, vp_ref, b_ref, o_ref, l_ref, do_ref, dl_ref,
             dq_ref, dk_ref, dv_ref, ds_ref, dk_car, dv_car):
        gi, i = pl.program_id(0), pl.program_id(1)

        @pl.when(i == 0)
        def _():
            ds_ref[...] = jnp.zeros_like(ds_ref)
            dk_car[...] = jnp.zeros_like(dk_car)
            dv_car[...] = jnp.zeros_like(dv_car)

        @pl.when(i < nblk)
        def _():
            mask = _dil_mask(_stream_pos(gi, i, S))
            for j in range(2):
                q2, kc, kp, vc, vp = q_ref[j], kc_ref[j], kp_ref[j], vc_ref[j], vp_ref[j]
                o2, l2, do2, dl2 = o_ref[j], l_ref[j], do_ref[j], dl_ref[j]
                dqs, dkps, dkcs, dvps, dvcs = [], [], [], [], []
                for hh in range(2):
                    sl = slice(HEAD * hh, HEAD * (hh + 1))
                    qh, doh = q2[:, sl], do2[:, sl]
                    kw = jnp.concatenate([kp[:, sl], kc[:, sl]], axis=0)
                    vw = jnp.concatenate([vp[:, sl], vc[:, sl]], axis=0)
                    lg = _bdot(qh, kw, ((1,), (1,))) * (HEAD ** -0.5) + b_ref[2 * j + hh]
                    p = jnp.where(mask, jnp.exp(lg - l2[:, HEAD * hh:HEAD * hh + 1]), 0.0)
                    dp = _bdot(doh, vw, ((1,), (1,)))
                    delta = jnp.sum(doh * o2[:, sl], axis=-1, keepdims=True)
                    dl = jnp.sum(dl2[:, sl], axis=-1, keepdims=True)
                    ds = p * (dp - delta + dl)
                    ds_ref[2 * j + hh] += ds
                    dsq = ds * (HEAD ** -0.5)
                    dqs.append(_bdot(dsq, kw, ((1,), (0,))))
                    dkw = _bdot(dsq, qh, ((0,), (0,)))
                    dvw = _bdot(p, doh, ((0,), (0,)))
                    dkps.append(dkw[:QBLK])
                    dkcs.append(dkw[QBLK:])
                    dvps.append(dvw[:QBLK])
                    dvcs.append(dvw[QBLK:])
                dq_ref[j] = jnp.concatenate(dqs, axis=1)
                dk_ref[j] = dk_car[j] + jnp.concatenate(dkps, axis=1)
                dv_ref[j] = dv_car[j] + jnp.concatenate(dvps, axis=1)
                dk_car[j] = jnp.concatenate(dkcs, axis=1)
                dv_car[j] = jnp.concatenate(dvcs, axis=1)

        @pl.when(i == nblk)
        def _():
            dk_ref[...] = dk_car[...]
            dv_ref[...] = dv_car[...]

    cur = pl.BlockSpec((2, QBLK, 128), lambda g, i: (g, jnp.minimum(i, nblk - 1), 0))
    prev = pl.BlockSpec((2, QBLK, 128), lambda g, i: (g, jnp.clip(i - 1, 0, nblk - 1), 0))
    bspec = pl.BlockSpec((4, QBLK, 2 * QBLK), lambda g, i: (g, 0, 0))
    return pl.pallas_call(
        body, name="dil_bwd", grid=(len(DIL), nblk + 1),
        in_specs=[cur, cur, prev, cur, prev, bspec, cur, cur, cur, cur],
        out_specs=[cur, prev, prev, bspec],
        out_shape=[jax.ShapeDtypeStruct(q.shape, F32)] * 3 + [jax.ShapeDtypeStruct(bias.shape, F32)],
        scratch_shapes=[pltpu.VMEM((2, QBLK, 128), F32), pltpu.VMEM((2, QBLK, 128), F32)],
        compiler_params=_cp(("arbitrary", "arbitrary")),
    )(q, k, k, v, v, bias, o, lse, do, dlse)


def _t5_bucket(dist):
    max_exact = N_BUCKETS // 2
    d = jnp.maximum(dist, 1).astype(F32)
    large = max_exact + (jnp.log(d / max_exact) / math.log(MAX_DISTANCE / max_exact)
                         * (N_BUCKETS - max_exact)).astype(jnp.int32)
    large = jnp.minimum(large, N_BUCKETS - 1)
    return jnp.where(dist < max_exact, dist, large)


def _bucket_maps():
    qi = jnp.arange(QBLK)[:, None]
    kj = jnp.arange(2 * QBLK)[None, :] - QBLK
    dist = jnp.maximum(qi - kj, 0)
    return jnp.stack([_t5_bucket(dist * r) for r in DIL])


def bias_table(rel_bias, buckets):
    def body(tbl_ref, bk_ref, o_ref):
        for h in range(DL_HEADS):
            bk = bk_ref[h // 4]

            def step(b, acc):
                return jnp.where(bk == b, tbl_ref[b, h], acc)

            o_ref[h] = lax.fori_loop(0, N_BUCKETS, step, jnp.zeros(bk.shape, F32))

    return pl.pallas_call(
        body, name="bias_table", out_shape=jax.ShapeDtypeStruct((DL_HEADS,) + buckets.shape[1:], F32),
        in_specs=[pl.BlockSpec(memory_space=pltpu.SMEM), pl.BlockSpec(memory_space=pltpu.VMEM)],
        out_specs=pl.BlockSpec(memory_space=pltpu.VMEM),
    )(rel_bias, buckets)


def bias_grad(ds, buckets):
    def body(ds_ref, bk_ref, o_ref):
        lane = lax.broadcasted_iota(jnp.int32, (1, 128), 1)
        for h in range(DL_HEADS):
            dsv = ds_ref[h]
            bk = bk_ref[h // 4]

            def step(b, row):
                return jnp.where(lane == b, jnp.sum(jnp.where(bk == b, dsv, 0.0)), row)

            o_ref[h:h + 1, :] = lax.fori_loop(0, N_BUCKETS, step, jnp.zeros((1, 128), F32))

    return pl.pallas_call(
        body, name="bias_grad", out_shape=jax.ShapeDtypeStruct((DL_HEADS, 128), F32),
        in_specs=[pl.BlockSpec(memory_space=pltpu.VMEM)] * 2, out_specs=pl.BlockSpec(memory_space=pltpu.VMEM),
    )(ds, buckets)


def f_attn_out(x, oa, o, lse, w):
    og = [jnp.concatenate([o[2 * g], o[2 * g + 1]], axis=1) for g in range(3)]
    lg = [jnp.concatenate([lse[2 * g], lse[2 * g + 1]], axis=1) for g in range(3)]
    m = jnp.maximum(jnp.maximum(lg[0], lg[1]), lg[2])
    e = [jnp.exp(l - m) for l in lg]
    den = e[0] + e[1] + e[2]
    ob = (e[0] * og[0] + e[1] * og[1] + e[2] * og[2]) / den
    return x + mm(jnp.concatenate([oa, ob], axis=1), w)


def norm_shift_fwd(x, g, tm=256):
    S = x.shape[0]

    def body(x_ref, xp_ref, g_ref, h_ref, hs_ref):
        h = rms(x_ref[...], g_ref[...])
        hp = rms(xp_ref[7:8, :], g_ref[...])
        hp = jnp.where(pl.program_id(0) == 0, 0.0, hp)
        row = lax.broadcasted_iota(jnp.int32, (tm, D), 0)
        h_ref[...] = h
        hs_ref[...] = jnp.where(row == 0, hp, pltpu.roll(h, 1, 0))

    return pl.pallas_call(
        body, name="rw_norm_shift", grid=(S // tm,),
        in_specs=[pl.BlockSpec((tm, D), lambda t: (t, 0)),
                  pl.BlockSpec((8, D), lambda t: (jnp.maximum(t * (tm // 8) - 1, 0), 0)),
                  pl.BlockSpec((1, D), lambda t: (0, 0))],
        out_specs=[pl.BlockSpec((tm, D), lambda t: (t, 0))] * 2,
        out_shape=[jax.ShapeDtypeStruct((S, D), F32)] * 2,
        compiler_params=_cp(("parallel",)),
    )(x, x, g)


def norm_shift_bwd(x, g, dh, dhs, dres, tm=256):
    S = x.shape[0]
    nt = S // tm

    def body(x_ref, g_ref, dh_ref, dhs_ref, dhn_ref, dr_ref, dx_ref, dg_ref):
        t = pl.program_id(0)
        nxt = jnp.where(t == nt - 1, 0.0, dhn_ref[0:1, :])
        row = lax.broadcasted_iota(jnp.int32, (tm, D), 0)
        tot = dh_ref[...] + jnp.where(row == tm - 1, nxt, pltpu.roll(dhs_ref[...], tm - 1, 0))
        _, vjp = jax.vjp(rms, x_ref[...], g_ref[...])
        dx, dg = vjp(tot)
        dx_ref[...] = dr_ref[...] + dx

        @pl.when(t == 0)
        def _():
            dg_ref[...] = dg

        @pl.when(t != 0)
        def _():
            dg_ref[...] += dg

    tile = pl.BlockSpec((tm, D), lambda t: (t, 0))
    return pl.pallas_call(
        body, name="rw_norm_shift_bwd", grid=(nt,),
        in_specs=[tile, pl.BlockSpec((1, D), lambda t: (0, 0)), tile, tile,
                  pl.BlockSpec((8, D), lambda t: (jnp.minimum((t + 1) * (tm // 8), S // 8 - 1), 0)), tile],
        out_specs=[tile, pl.BlockSpec((1, D), lambda t: (0, 0))],
        out_shape=[jax.ShapeDtypeStruct((S, D), F32), jax.ShapeDtypeStruct((1, D), F32)],
        compiler_params=_cp(("arbitrary",)),
    )(x, g, dh, dhs, dhs, dres)


def f_rw_proj(h, hs, mix, w):
    return mm(h + (hs - h) * mix, w)


def f_rw_mid(h, hs, r, k, v, mix3, w0, a0, kkw, kaw, w1, w2, a1, a2, g1, g2):
    xx = hs - h
    xw, xa, xg = h + xx * mix3[0:1], h + xx * mix3[1:2], h + xx * mix3[2:3]
    w_log = -softplus(-(w0 + mm(jnp.tanh(mm(xw, w1)), w2))) - 0.5
    lw = -jnp.exp(w_log)
    ag = jax.nn.sigmoid(a0 + mm(mm(xa, a1), a2))
    gate = mm(jax.nn.sigmoid(mm(xg, g1)), g2)
    kk = k * kkw
    kk = kk / jnp.maximum(jnp.sqrt(group_sum(kk * kk, RW_H)), 1e-12)
    kmod = k * (1.0 + (ag - 1.0) * kaw)
    return (to_heads(r), to_heads(lw), to_heads(kmod), to_heads(v), to_heads(-kk), to_heads(kk * ag), gate)


def f_rw_post(yh, rh, kh, vh, gate, x, lng, lnb, rk, wo):
    mu = jnp.mean(yh, axis=-1, keepdims=True)
    var = jnp.mean(jnp.square(yh - mu), axis=-1, keepdims=True)
    yn = (yh - mu) * lax.rsqrt(var + GN_EPS)
    bonus = jnp.sum(rh * kh * rk, axis=-1, keepdims=True) * vh
    y = from_heads(yn) * lng + lnb + from_heads(bonus)
    return x + mm(y * gate, wo)


def _split2(x):
    hi = x.astype(BF16)
    return hi, (x - hi.astype(F32)).astype(BF16)


def _b3(x, y, cx, cy):
    dn = (((cx,), (cy,)), ((0,), (0,)))
    xh, xl = _split2(x)
    yh, yl = _split2(y)
    d = lambda p, q: lax.dot_general(p, q, dn, preferred_element_type=F32)
    return d(xh, yh) + (d(xh, yl) + d(xl, yh))


@jax.custom_vjp
def b_nt(x, y):
    return _b3(x, y, 2, 2)


@jax.custom_vjp
def b_nn(x, y):
    return _b3(x, y, 2, 1)


@jax.custom_vjp
def b_tn(x, y):
    return _b3(x, y, 1, 1)


b_nt.defvjp(lambda x, y: (b_nt(x, y), (x, y)), lambda r, g: (b_nn(g, r[1]), b_tn(g, r[0])))
b_nn.defvjp(lambda x, y: (b_nn(x, y), (x, y)), lambda r, g: (b_nt(g, r[1]), b_tn(r[0], g)))
b_tn.defvjp(lambda x, y: (b_tn(x, y), (x, y)), lambda r, g: (b_nt(r[1], g), b_nn(r[0], g)))


def _tri_apply(x, lower):
    H, C, _ = x.shape
    ii = lax.broadcasted_iota(jnp.int32, (C, C), 0)
    jj = lax.broadcasted_iota(jnp.int32, (C, C), 1)
    m = jnp.broadcast_to(((jj <= ii) if lower else (jj >= ii)).astype(BF16), (H, C, C))
    x1 = x.astype(BF16)
    r1 = x - x1.astype(F32)
    x2 = r1.astype(BF16)
    x3 = (r1 - x2.astype(F32)).astype(BF16)
    d = lambda q: lax.dot_general(m, q, (((2,), (1,)), ((0,), (0,))), preferred_element_type=F32)
    return d(x1) + (d(x2) + d(x3))


@jax.custom_vjp
def run_sum(x):
    return _tri_apply(x, True)


run_sum.defvjp(lambda x: (run_sum(x), None), lambda _, g: (_tri_apply(g, False),))


def rwkv_chunk(S0, r, lw, k, v, a, b):
    H, C, _ = r.shape
    V = S0.shape[1]
    ii = lax.broadcasted_iota(jnp.int32, (C, C), 0)
    jj = lax.broadcasted_iota(jnp.int32, (C, C), 1)
    strict = jj < ii
    i2 = lax.broadcasted_iota(jnp.int32, (C, 2 * C), 0)
    j2 = lax.broadcasted_iota(jnp.int32, (C, 2 * C), 1)
    incl2 = jnp.where(j2 >= C, j2 - C, j2) <= i2
    g = run_sum(lw)
    ig = jnp.exp(-g)
    ar = jnp.concatenate([a * jnp.exp(g - lw), r * jnp.exp(g)], axis=1)
    bk = jnp.concatenate([b * ig, k * ig], axis=1)
    m = b_nt(ar, bk)
    a_ab = jnp.where(strict, m[:, :C, :C], 0.0)
    a_ak = jnp.where(strict, m[:, :C, C:], 0.0)
    b_r = jnp.where(incl2, m[:, C:, :], 0.0)
    p = b_nt(ar, S0)
    u = p[:, :C] + b_nn(a_ak, v)
    nmat, n = a_ab, 1
    while n < C:
        n *= 2
        if n < C:
            z = b_nn(nmat, jnp.concatenate([u, nmat], axis=2))
            u, nmat = u + z[:, :, :V], z[:, :, V:]
        else:
            u = u + b_nn(nmat, u)
    uv = jnp.concatenate([u, v], axis=1)
    y = p[:, C:] + b_nn(b_r, uv)
    g_end = g[:, C - 1:C, :]
    dec = jnp.exp(g_end - g)
    s_new = S0 * jnp.exp(g_end) + b_tn(uv, jnp.concatenate([b * dec, k * dec], axis=1))
    return y, s_new


def rwkv_fwd(r, lw, k, v, a, b):
    H, S, _ = r.shape
    C = RW_CHUNK

    def body(r_ref, lw_ref, k_ref, v_ref, a_ref, b_ref, y_ref, s_ref, s_scr):
        @pl.when(pl.program_id(0) == 0)
        def _():
            s_scr[...] = jnp.zeros_like(s_scr)

        s0 = s_scr[...]
        s_ref[0] = s0
        y, s1 = rwkv_chunk(s0, r_ref[...], lw_ref[...], k_ref[...], v_ref[...], a_ref[...], b_ref[...])
        y_ref[...] = y
        s_scr[...] = s1

    bs = pl.BlockSpec((H, C, HEAD), lambda c: (0, c, 0))
    return pl.pallas_call(
        body, name="rwkv_fwd", grid=(S // C,), in_specs=[bs] * 6,
        out_specs=[bs, pl.BlockSpec((1, H, HEAD, HEAD), lambda c: (c, 0, 0, 0))],
        out_shape=[jax.ShapeDtypeStruct((H, S, HEAD), F32), jax.ShapeDtypeStruct((S // C, H, HEAD, HEAD), F32)],
        scratch_shapes=[pltpu.VMEM((H, HEAD, HEAD), F32)],
        compiler_params=_cp(("arbitrary",)),
    )(r, lw, k, v, a, b)


def rwkv_bwd(r, lw, k, v, a, b, states, dy):
    H, S, _ = r.shape
    C = RW_CHUNK
    nc = S // C

    def body(r_ref, lw_ref, k_ref, v_ref, a_ref, b_ref, s_ref, dy_ref, dr, dlw, dk, dv, da, db, ds_scr):
        @pl.when(pl.program_id(0) == 0)
        def _():
            ds_scr[...] = jnp.zeros_like(ds_scr)

        _, vjp = jax.vjp(rwkv_chunk, s_ref[0], r_ref[...], lw_ref[...], k_ref[...], v_ref[...], a_ref[...], b_ref[...])
        grads = vjp((dy_ref[...], ds_scr[...]))
        ds_scr[...] = grads[0]
        for o, gv in zip((dr, dlw, dk, dv, da, db), grads[1:]):
            o[...] = gv

    bs = pl.BlockSpec((H, C, HEAD), lambda c: (0, nc - 1 - c, 0))
    return pl.pallas_call(
        body, name="rwkv_bwd", grid=(nc,),
        in_specs=[bs] * 6 + [pl.BlockSpec((1, H, HEAD, HEAD), lambda c: (nc - 1 - c, 0, 0, 0)), bs],
        out_specs=[bs] * 6, out_shape=[jax.ShapeDtypeStruct((H, S, HEAD), F32)] * 6,
        scratch_shapes=[pltpu.VMEM((H, HEAD, HEAD), F32)],
        compiler_params=_cp(("arbitrary",)),
    )(r, lw, k, v, a, b, states, dy)


def loss_head(y, target, tm=512):
    S = y.shape[0]

    def body(y_ref, t_ref, dy_ref, l_ref):
        e = y_ref[...] - t_ref[...]
        dy_ref[...] = e * (1.0 / D)
        part = jnp.broadcast_to(0.5 * jnp.sum(jnp.mean(e * e, axis=-1, keepdims=True)), (1, 128))

        @pl.when(pl.program_id(0) == 0)
        def _():
            l_ref[...] = part

        @pl.when(pl.program_id(0) != 0)
        def _():
            l_ref[...] += part

    tile = pl.BlockSpec((tm, D), lambda t: (t, 0))
    return pl.pallas_call(
        body, name="loss_head", grid=(S // tm,), in_specs=[tile, tile],
        out_specs=[tile, pl.BlockSpec((1, 128), lambda t: (0, 0))],
        out_shape=[jax.ShapeDtypeStruct((S, D), F32), jax.ShapeDtypeStruct((1, 128), F32)],
        compiler_params=_cp(("arbitrary",)),
    )(y, target)


def _row_tile(rows, cols, budget=1 << 19):
    best = None
    for tr in range(8, rows + 1, 8):
        if rows % tr == 0 and tr * cols <= budget:
            best = tr
    return best or rows


def _adam(w, g, m, v):
    m = ADAM_B1 * m + (1.0 - ADAM_B1) * g
    v = ADAM_B2 * v + (1.0 - ADAM_B2) * jnp.square(g)
    m_hat = m / (1.0 - ADAM_B1 ** ADAM_STEP)
    v_hat = v / (1.0 - ADAM_B2 ** ADAM_STEP)
    return -ADAM_LR * (m_hat / (jnp.sqrt(v_hat) + ADAM_EPS) + ADAM_WD * w), m, v


def sum_slots(name, parts, dtype=F32, extras=()):
    n = 0 if parts is None else parts.shape[0]
    R, C = extras[0].shape if parts is None else parts.shape[1:]
    tr = _row_tile(R, C * (n + len(extras)))
    ins = ([] if parts is None else [parts]) + list(extras)

    def body(*refs):
        terms = [] if parts is None else [refs[0][i] for i in range(n)]
        terms += [r[...] for r in refs[len(ins) - len(extras):len(ins)]]
        s = terms[0].astype(F32)
        for t in terms[1:]:
            s = s + t.astype(F32)
        refs[len(ins)][...] = s.astype(dtype)

    tile = pl.BlockSpec((tr, C), lambda t: (t, 0))
    return pl.pallas_call(
        body, name=name, grid=(R // tr,),
        in_specs=([] if parts is None else [pl.BlockSpec((n, tr, C), lambda t: (0, t, 0))]) + [tile] * len(extras),
        out_specs=tile, out_shape=jax.ShapeDtypeStruct((R, C), dtype), compiler_params=_cp(("parallel",)),
    )(*ins)


def adam_step(name, ga, gb, w, m, v):
    R, C = w.shape
    tr = _row_tile(R, C, 1 << 17)
    ins = [ga] + ([gb] if gb is not None else []) + [w, m, v]

    def body(*refs):
        g = refs[0][...]
        if gb is not None:
            g = g + refs[1][...]
        w_ref, m_ref, v_ref, g_out, d_out, m_out, v_out = refs[len(ins) - 3:]
        d, m2, v2 = _adam(w_ref[...], g, m_ref[...], v_ref[...])
        g_out[...] = g
        d_out[...] = d
        m_out[...] = m2
        v_out[...] = v2

    tile = pl.BlockSpec((tr, C), lambda t: (t, 0))
    return pl.pallas_call(
        body, name=name, grid=(R // tr,), in_specs=[tile] * len(ins), out_specs=[tile] * 4,
        out_shape=[jax.ShapeDtypeStruct((R, C), F32)] * 4, compiler_params=_cp(("parallel",)),
    )(*ins)


def _place():
    return lax.axis_index("x"), lax.axis_index("y"), lax.axis_index("c")


def _flip(me, mask):
    return tuple(1 - v if mk else v for v, mk in zip(me, mask))


CHIP_MASKS = ((1, 0, 0), (0, 1, 0), (1, 1, 0))
ALL_MASKS = tuple((a, b, c) for a in (0, 1) for b in (0, 1) for c in (0, 1) if (a, b, c) != (0, 0, 0))


def _chip(dev):
    return 2 * dev[0] + dev[1]


def _devno(dev):
    return 4 * dev[0] + 2 * dev[1] + dev[2]


def exchange(name, arrays, out_shapes, masks, copies, src_of, dst_of, local_of, alias=False):
    n, npeer = len(arrays), len(masks)
    nloc = 1

    def body(*refs):
        ins, outs = refs[:n], refs[n:2 * n]
        send_sems, recv_sems, local_sems = refs[2 * n:]
        me = _place()
        peers = [_flip(me, mk) for mk in masks]
        locals_ = []
        for i in range(n):
            for q, (src, dst) in enumerate(local_of(ins[i], outs[i], me)):
                cp = pltpu.make_async_copy(src, dst, local_sems.at[i * nloc + q])
                cp.start()
                locals_.append(cp)
        sends = []
        for i in range(n):
            for j, peer in enumerate(peers):
                srcs, dsts = src_of(ins[i], me, j), dst_of(outs[i], me, j)
                for q in range(copies):
                    sem = (i * npeer + j) * copies + q
                    cp = pltpu.make_async_remote_copy(
                        src_ref=srcs[q], dst_ref=dsts[q], send_sem=send_sems.at[sem], recv_sem=recv_sems.at[sem],
                        device_id=peer, device_id_type=MESH)
                    cp.start()
                    sends.append(cp)
        for i in range(n):
            for j, peer in enumerate(peers):
                lands = dst_of(outs[i], peer, j)
                for q in range(copies):
                    sem = (i * npeer + j) * copies + q
                    pltpu.make_async_remote_copy(
                        src_ref=lands[q], dst_ref=lands[q], send_sem=send_sems.at[sem], recv_sem=recv_sems.at[sem],
                        device_id=peer, device_id_type=MESH).wait_recv()
        for cp in sends:
            cp.wait_send()
        for cp in locals_:
            cp.wait()

    hbm = pl.BlockSpec(memory_space=pl.ANY)
    return pl.pallas_call(
        body, name=name, in_specs=[hbm] * n, out_specs=[hbm] * n, out_shape=list(out_shapes),
        scratch_shapes=[pltpu.SemaphoreType.DMA((n * npeer * copies,)), pltpu.SemaphoreType.DMA((n * npeer * copies,)),
                        pltpu.SemaphoreType.DMA((n * nloc,))],
        input_output_aliases={i: i for i in range(n)} if alias else {},
    )(*arrays)


def _half(c, rows):
    return pl.ds(c * (rows // 2), rows // 2)


def gather_chips(arrays):
    outs = [jax.ShapeDtypeStruct((N_CHIPS,) + a.shape, a.dtype) for a in arrays]
    sib = len(CHIP_MASKS)
    got = exchange("gather_weights", arrays, outs, CHIP_MASKS + ((0, 0, 1),), 1,
                   src_of=lambda r, me, j: [r] if j == sib else [r.at[_half(me[2], r.shape[0])]],
                   dst_of=lambda o, sender, j: [o.at[_chip(sender)]] if j == sib else
                   [o.at[_chip(sender), _half(sender[2], o.shape[1])]],
                   local_of=lambda r, o, me: [])
    return exchange("gather_swap", got, outs, ((0, 0, 1),), len(CHIP_MASKS),
                    src_of=lambda r, me, j: [r.at[_chip(_flip(me, mk)), _half(me[2], r.shape[1])] for mk in CHIP_MASKS],
                    dst_of=lambda o, sender, j: [o.at[_chip(_flip(sender, mk)), _half(sender[2], o.shape[1])]
                                                 for mk in CHIP_MASKS],
                    local_of=lambda r, o, me: [], alias=True)


def reduce_chips(names, arrays, wire):
    x, y, c = _place()
    split = [a.reshape(N_CHIPS, 2, a.shape[1] // 2, a.shape[2]) for a in arrays]
    half_shapes = [jax.ShapeDtypeStruct((N_CHIPS,) + a.shape[2:], F32) for a in split]
    theirs = exchange("grad_pre_swap", split, half_shapes, ((0, 0, 1),), 1,
                      src_of=lambda r, me, j: [r.at[:, 1 - me[2]]], dst_of=lambda o, sender, j: [o],
                      local_of=lambda r, o, me: [])
    chip_sum = []
    for nm, a, t, dt in zip(names, split, theirs, wire):
        own = lax.dynamic_index_in_dim(a, c, axis=1, keepdims=False)
        flat = lambda v: v.reshape(-1, v.shape[-1])
        chip_sum.append(sum_slots(f"sum2_{nm}", None, dt, [flat(own), flat(t)]).reshape(t.shape))
    landed = exchange("scatter_grads", chip_sum,
                      [jax.ShapeDtypeStruct((len(CHIP_MASKS),) + a.shape[1:], a.dtype) for a in chip_sum], CHIP_MASKS, 1,
                      src_of=lambda r, me, j: [r.at[_chip(_flip(me, CHIP_MASKS[j]))]],
                      dst_of=lambda o, sender, j: [o.at[j]], local_of=lambda r, o, me: [])
    halves = [sum_slots(f"sum4_{nm}", p, F32, [lax.dynamic_index_in_dim(a, _chip((x, y, c)), axis=0, keepdims=False)])
              for nm, p, a in zip(names, landed, chip_sum)]
    others = exchange("grad_final_swap", halves, [jax.ShapeDtypeStruct(a.shape, F32) for a in halves], ((0, 0, 1),), 1,
                      src_of=lambda r, me, j: [r], dst_of=lambda o, sender, j: [o], local_of=lambda r, o, me: [])
    return [jnp.concatenate([jnp.where(c == 0, h, o), jnp.where(c == 0, o, h)], axis=0) for h, o in zip(halves, others)]


def gather_all(arrays):
    outs = [jax.ShapeDtypeStruct((8,) + a.shape, a.dtype) for a in arrays]
    return exchange("gather_replicated", arrays, outs, ALL_MASKS, 1,
                    src_of=lambda r, me, peer: [r],
                    dst_of=lambda o, sender, j: [o.at[_devno(sender)]],
                    local_of=lambda r, o, me: [(r, o.at[_devno(me)])])


def _unshard_cols(g):
    return jnp.transpose(g, (1, 0, 2)).reshape(g.shape[1], -1)


def _shard_cols(a):
    return jnp.transpose(a.reshape(a.shape[0], N_CHIPS, -1), (1, 0, 2))


def _forward_backward(x, tgt, W):
    S = x.shape[0]
    G = {}
    sd = jax.ShapeDtypeStruct

    def ffn(xin, l, j):
        return ffn_fwd(xin, W["ffn_norm"][l][j], W["ffn_w_gate"], W["ffn_w_up"], W["ffn_w_down"], l, j)

    ffn_dw = [tuple(lax.empty(W[n].shape, F32) for n in ("ffn_w_gate", "ffn_w_up", "ffn_w_down"))]

    def ffn_back(xin, dout, l, j):
        gn = W["ffn_norm"][l][j]
        dh, *dws = ffn_bwd(xin, gn, W["ffn_w_gate"], W["ffn_w_up"], W["ffn_w_down"], dout, ffn_dw[0], l, j)
        ffn_dw[0] = tuple(dws)
        dx, G[("ffn_norm", l, j)] = norm_bwd(f"ffn_norm_bwd_{l}{j}", xin, gn, dh, dout)
        return dx

    x0 = x
    x1 = ffn(x0, 0, 0)
    g0 = W["mix_norm"][0]
    sbq, sbk, sbv = tile_fwd(f_attn_sb, "attn_in_sb", [x1], [g0, W["attn_w_in"][0]], [sd((S, SB_W), F32)] * 3, 256)
    dl_shape = sd((DL_PAIRS, S, 128), F32)
    qn, = tile_fwd(f_attn_qk, "attn_in_q", [x1], [g0, W["attn_w_in"][1], W["attn_q_norm"]], [dl_shape], 256)
    kn, = tile_fwd(f_attn_qk, "attn_in_k", [x1], [g0, W["attn_w_in"][2], W["attn_k_norm"]], [dl_shape], 256)
    vv, = tile_fwd(f_attn_v, "attn_in_v", [x1], [g0, W["attn_w_in"][3]], [dl_shape], 256)
    oa = sb_fwd(sbq, sbk, sbv)
    qs, ks, vs = (reorder(nm, t, DIL, False) for nm, t in (("sub_q", qn), ("sub_k", kn), ("sub_v", vv)))
    o_s, lse_s = dil_fwd(qs, ks, vs, W["bias_mat"])
    o_n, lse_n = reorder("nat_o", o_s, DIL, True), reorder("nat_lse", lse_s, DIL, True)
    x2, = tile_fwd(f_attn_out, "attn_out", [x1, oa, o_n, lse_n], [W["attn_w_out"]], [sd((S, D), F32)], 256)
    x3 = ffn(x2, 0, 1)
    x4 = ffn(x3, 1, 0)
    g1 = W["mix_norm"][1]
    h, hs = norm_shift_fwd(x4, g1)
    mix = W["rw_mix"]
    r, = tile_fwd(f_rw_proj, "rw_proj_r", [h, hs], [mix[0:1], W["rw_wr"]], [sd((S, D), F32)], 256)
    k, = tile_fwd(f_rw_proj, "rw_proj_k", [h, hs], [mix[2:3], W["rw_wk"]], [sd((S, D), F32)], 256)
    v, = tile_fwd(f_rw_proj, "rw_proj_v", [h, hs], [mix[3:4], W["rw_wv"]], [sd((S, D), F32)], 256)
    mix3 = jnp.concatenate([mix[1:2], mix[4:5], mix[5:6]], axis=0)
    mid_w = [mix3, W["rw_w0"], W["rw_a0"], W["rw_kk"], W["rw_ka"], W["rw_w1"], W["rw_w2"], W["rw_a1"], W["rw_a2"],
             W["rw_g1"], W["rw_g2"]]
    hshape = sd((RW_H, S, HEAD), F32)
    mid_tiles = [h, hs, r, k, v]
    rh, lwh, kh, vh, ah, bh, gate = tile_fwd(f_rw_mid, "rw_mid", mid_tiles, mid_w, [hshape] * 6 + [sd((S, D), F32)], 128)
    yh, states = rwkv_fwd(rh, lwh, kh, vh, ah, bh)
    post_w = [W["rw_lnx_g"], W["rw_lnx_b"], W["rw_rk"], W["rw_wo"]]
    post_tiles = [yh, rh, kh, vh, gate, x4]
    x5, = tile_fwd(f_rw_post, "rw_post", post_tiles, post_w, [sd((S, D), F32)], 128)
    x6 = ffn(x5, 1, 1)
    dx6, loss_part = loss_head(x6, tgt)

    dx5 = ffn_back(x5, dx6, 1, 1)
    (dyh, drh, dkh, dvh, dgate, dx4), (d_lng, d_lnb, d_rk, d_wo) = tile_bwd(
        f_rw_post, "rw_post_bwd", post_tiles, post_w, [dx5], 128, [True] * 6, [True] * 4)
    drh2, dlwh, dkh2, dvh2, dah, dbh = rwkv_bwd(rh, lwh, kh, vh, ah, bh, states, dyh)
    mid_cts = [(drh, drh2), dlwh, (dkh, dkh2), (dvh, dvh2), dah, dbh, dgate]
    (dh, dhs, dr, dk, dv), dmid_w = tile_bwd(f_rw_mid, "rw_mid_bwd", mid_tiles, mid_w, mid_cts, 128,
                                             [True] * 5, [True] * len(mid_w))
    dmix = {}
    for nm, ct, row, wname in (("r", dr, 0, "rw_wr"), ("k", dk, 2, "rw_wk"), ("v", dv, 3, "rw_wv")):
        (dh, dhs), (dmix[row], G[wname]) = tile_bwd(
            f_rw_proj, f"rw_proj_{nm}_bwd", [h, hs], [mix[row:row + 1], W[wname]], [ct], 256,
            [True, True], [True, True], acc={0: dh, 1: dhs})
    dx4, G[("mix_norm", 1)] = norm_shift_bwd(x4, g1, dh, dhs, dx4)
    dmix3 = dmid_w[0]
    G["rw_mix"] = jnp.concatenate([dmix[0], dmix3[0:1], dmix[2], dmix[3], dmix3[1:2], dmix3[2:3]], axis=0)
    for nm, gv in zip(("rw_w0", "rw_a0", "rw_kk", "rw_ka", "rw_w1", "rw_w2", "rw_a1", "rw_a2", "rw_g1", "rw_g2"), dmid_w[1:]):
        G[nm] = gv
    G["rw_lnx_g"], G["rw_lnx_b"], G["rw_rk"], G["rw_wo"] = d_lng, d_lnb, d_rk, d_wo
    dx3 = ffn_back(x3, dx4, 1, 0)
    dx2 = ffn_back(x2, dx3, 0, 1)
    (dx1, doa, do_n, dlse_n), (G["attn_w_out"],) = tile_bwd(
        f_attn_out, "attn_out_bwd", [x1, oa, o_n, lse_n], [W["attn_w_out"]], [dx2], 256, [True] * 4, [True])
    do_s, dlse_s = reorder("sub_do", do_n, DIL, False), reorder("sub_dlse", dlse_n, DIL, False)
    dqs, dks, dvs, dsum = dil_bwd(qs, ks, vs, W["bias_mat"], o_s, lse_s, do_s, dlse_s)
    G["rel_bias"] = bias_grad(dsum, W["buckets"])
    dqn, dkn, dvv = (reorder(nm, t, DIL, True) for nm, t in (("nat_dq", dqs), ("nat_dk", dks), ("nat_dv", dvs)))
    dsbq, dsbk, dsbv = sb_bwd(sbq, sbk, sbv, doa)
    dg0 = []
    dwin = []
    (dx1,), (dg, dw) = tile_bwd(f_attn_sb, "attn_in_sb_bwd", [x1], [g0, W["attn_w_in"][0]], [dsbq, dsbk, dsbv], 256,
                                [True], [True, True], acc={0: dx1})
    dg0.append(dg), dwin.append(dw)
    (dx1,), (dg, dw, G["attn_q_norm"]) = tile_bwd(f_attn_qk, "attn_in_q_bwd", [x1], [g0, W["attn_w_in"][1], W["attn_q_norm"]],
                                                  [dqn], 256, [True], [True] * 3, acc={0: dx1})
    dg0.append(dg), dwin.append(dw)
    (dx1,), (dg, dw, G["attn_k_norm"]) = tile_bwd(f_attn_qk, "attn_in_k_bwd", [x1], [g0, W["attn_w_in"][2], W["attn_k_norm"]],
                                                  [dkn], 256, [True], [True] * 3, acc={0: dx1})
    dg0.append(dg), dwin.append(dw)
    (dx1,), (dg, dw) = tile_bwd(f_attn_v, "attn_in_v_bwd", [x1], [g0, W["attn_w_in"][3]], [dvv], 256,
                                [True], [True, True], acc={0: dx1})
    dg0.append(dg), dwin.append(dw)
    G[("mix_norm", 0)] = dg0
    G["attn_w_in"] = dwin
    dx0 = ffn_back(x0, dx1, 0, 0)
    G["ffn_w_gate"], G["ffn_w_up"], G["ffn_w_down"] = ffn_dw[0]
    return loss_part, dx0, G


VEC_ROWS = ("ffn_norm", "rw_mix", "rw_w0", "rw_a0", "rw_kk", "rw_ka", "rw_lnx_g", "rw_lnx_b")


def kernel(x, ffn_norm, ffn_w_gate, ffn_w_up, ffn_w_down, mix_norm, rel_bias, attn_w_in, attn_q_norm, attn_k_norm, attn_w_out, rw_mix, rw_w0, rw_w1, rw_w2, rw_a0, rw_a1, rw_a2, rw_g1, rw_g2, rw_kk, rw_ka, rw_rk, rw_wr, rw_wk, rw_wv, rw_wo, rw_lnx_g, rw_lnx_b, loss_target, m_ffn_norm, m_ffn_w_gate, m_ffn_w_up, m_ffn_w_down, m_mix_norm, m_rel_bias, m_attn_w_in, m_attn_q_norm, m_attn_k_norm, m_attn_w_out, m_rw_mix, m_rw_w0, m_rw_w1, m_rw_w2, m_rw_a0, m_rw_a1, m_rw_a2, m_rw_g1, m_rw_g2, m_rw_kk, m_rw_ka, m_rw_rk, m_rw_wr, m_rw_wk, m_rw_wv, m_rw_wo, m_rw_lnx_g, m_rw_lnx_b, v_ffn_norm, v_ffn_w_gate, v_ffn_w_up, v_ffn_w_down, v_mix_norm, v_rel_bias, v_attn_w_in, v_attn_q_norm, v_attn_k_norm, v_attn_w_out, v_rw_mix, v_rw_w0, v_rw_w1, v_rw_w2, v_rw_a0, v_rw_a1, v_rw_a2, v_rw_g1, v_rw_g2, v_rw_kk, v_rw_ka, v_rw_rk, v_rw_wr, v_rw_wk, v_rw_wv, v_rw_wo, v_rw_lnx_g, v_rw_lnx_b):
    names = ["ffn_norm", "ffn_w_gate", "ffn_w_up", "ffn_w_down", "mix_norm", "rel_bias", "attn_w_in", "attn_q_norm",
             "attn_k_norm", "attn_w_out", "rw_mix", "rw_w0", "rw_w1", "rw_w2", "rw_a0", "rw_a1", "rw_a2", "rw_g1", "rw_g2",
             "rw_kk", "rw_ka", "rw_rk", "rw_wr", "rw_wk", "rw_wv", "rw_wo", "rw_lnx_g", "rw_lnx_b"]
    loc = locals()
    w = {n: loc[n] for n in names}
    mom = {n: loc["m_" + n] for n in names}
    vel = {n: loc["v_" + n] for n in names}
    S = x.shape[1]

    vec_shard = jnp.concatenate([w[n].reshape(-1, 256) for n in VEC_ROWS], axis=0)
    mats = ["ffn_w_gate", "ffn_w_up", "ffn_w_down", "attn_w_in", "attn_w_out", "rw_w1", "rw_w2", "rw_a1", "rw_a2",
            "rw_g1", "rw_g2", "rw_wr", "rw_wk", "rw_wv", "rw_wo"]
    send = [vec_shard] + [w[n].reshape(-1, w[n].shape[-1]).astype(BF16) for n in mats]
    got = gather_chips(send)
    vec_full = _unshard_cols(got[0])
    gm = dict(zip(mats, got[1:]))
    W = {
        "ffn_norm": [[vec_full[2 * l + j][None] for j in range(2)] for l in range(2)],
        "ffn_w_gate": gm["ffn_w_gate"].reshape(N_CHIPS, 2, 2, D, FF_SHARD),
        "ffn_w_up": gm["ffn_w_up"].reshape(N_CHIPS, 2, 2, D, FF_SHARD),
        "ffn_w_down": gm["ffn_w_down"].reshape(N_CHIPS, 2, 2, FF_SHARD, D),
        "mix_norm": [mix_norm[0:1], mix_norm[1:2]],
        "attn_w_in": [gm["attn_w_in"][p] for p in range(N_CHIPS)],
        "attn_q_norm": attn_q_norm, "attn_k_norm": attn_k_norm,
        "attn_w_out": _unshard_cols(gm["attn_w_out"]),
        "rw_mix": vec_full[4:10],
        "rw_w1": gm["rw_w1"].reshape(D, -1), "rw_a1": gm["rw_a1"].reshape(D, -1), "rw_g1": gm["rw_g1"].reshape(D, -1),
        "rw_w2": _unshard_cols(gm["rw_w2"]), "rw_a2": _unshard_cols(gm["rw_a2"]), "rw_g2": _unshard_cols(gm["rw_g2"]),
        "rw_wr": gm["rw_wr"].reshape(D, D), "rw_wk": gm["rw_wk"].reshape(D, D), "rw_wv": gm["rw_wv"].reshape(D, D),
        "rw_wo": gm["rw_wo"].reshape(D, D),
        "rw_rk": rw_rk[0][:, None, :],
    }
    for i, n in enumerate(("rw_w0", "rw_a0", "rw_kk", "rw_ka", "rw_lnx_g", "rw_lnx_b")):
        W[n] = vec_full[10 + i][None]
    buckets = _bucket_maps()
    W["buckets"] = buckets
    W["bias_mat"] = bias_table(rel_bias, buckets)

    loss_part, dx, G = _forward_backward(x[0], loss_target[0], W)
    loss = lax.psum(loss_part[0, 0], ("x", "y", "c"))

    vec_rows = [G[("ffn_norm", l, j)] for l in range(2) for j in range(2)] + [G["rw_mix"]] + \
               [G[n] for n in ("rw_w0", "rw_a0", "rw_kk", "rw_ka", "rw_lnx_g", "rw_lnx_b")]
    full = {
        "vec": _shard_cols(jnp.concatenate(vec_rows, axis=0)),
        "ffn_w_gate": G["ffn_w_gate"], "ffn_w_up": G["ffn_w_up"], "ffn_w_down": G["ffn_w_down"],
        "attn_w_in": jnp.stack(G["attn_w_in"]),
        "attn_w_out": _shard_cols(G["attn_w_out"]),
        "rw_w1": G["rw_w1"].reshape(N_CHIPS, 256, -1), "rw_a1": G["rw_a1"].reshape(N_CHIPS, 256, -1),
        "rw_g1": G["rw_g1"].reshape(N_CHIPS, 256, -1),
        "rw_w2": _shard_cols(G["rw_w2"]), "rw_a2": _shard_cols(G["rw_a2"]), "rw_g2": _shard_cols(G["rw_g2"]),
        "rw_wr": G["rw_wr"].reshape(N_CHIPS, 256, D), "rw_wk": G["rw_wk"].reshape(N_CHIPS, 256, D),
        "rw_wv": G["rw_wv"].reshape(N_CHIPS, 256, D), "rw_wo": G["rw_wo"].reshape(N_CHIPS, 256, D),
    }
    order = ["vec"] + mats
    summed = reduce_chips(order, [full[n].reshape(N_CHIPS, -1, full[n].shape[-1]) for n in order],
                          [F32] + [BF16] * len(mats))

    rep = jnp.concatenate([G[("mix_norm", 0)][0] + G[("mix_norm", 0)][1] + G[("mix_norm", 0)][2] + G[("mix_norm", 0)][3],
                           G[("mix_norm", 1)]], axis=0).reshape(16, 128)
    rep = jnp.concatenate([rep, G["rel_bias"], jnp.pad(G["attn_q_norm"], ((0, 0), (0, 64))),
                           jnp.pad(G["attn_k_norm"], ((0, 0), (0, 64))), G["rw_rk"].reshape(8, 128),
                           jnp.zeros((2, 128), F32)], axis=0)
    rep_sum = sum_slots("sum_replicated", gather_all([rep])[0])
    g_rep = {
        "mix_norm": rep_sum[0:16].reshape(2, D),
        "rel_bias": jnp.transpose(rep_sum[16:28, :N_BUCKETS]),
        "attn_q_norm": rep_sum[28:29, :HEAD], "attn_k_norm": rep_sum[29:30, :HEAD],
        "rw_rk": rep_sum[30:38].reshape(1, RW_H, HEAD),
    }

    out = {}

    def adam(n, ga, gb):
        shp = w[n].shape
        to2 = lambda a: a.reshape(-1, shp[-1])
        res = adam_step(f"adam_{n}", to2(ga), None if gb is None else to2(gb), to2(w[n]), to2(mom[n]), to2(vel[n]))
        out[n] = tuple(r.reshape(shp) for r in res)

    part = dict(zip(order, summed))
    for n in mats:
        adam(n, part[n], None)
    rows = {"ffn_norm": (0, 4), "rw_mix": (4, 10), "rw_w0": (10, 11), "rw_a0": (11, 12), "rw_kk": (12, 13),
            "rw_ka": (13, 14), "rw_lnx_g": (14, 15), "rw_lnx_b": (15, 16)}
    for n, (lo, hi) in rows.items():
        adam(n, part["vec"][lo:hi], None)
    for n, gv in g_rep.items():
        adam(n, gv, None)

    grads = [out[n][0] for n in names]
    deltas = [out[n][1] for n in names]
    new_m = [out[n][2] for n in names]
    new_v = [out[n][3] for n in names]
    return (loss, dx[None], *grads, *deltas, *new_m, *new_v)
```

```python
import functools
import math

import jax
import jax.numpy as jnp
from jax import lax
from jax.experimental import pallas as pl
from jax.experimental.pallas import tpu as pltpu

F32, BF16 = jnp.float32, jnp.bfloat16
HI = lax.Precision.HIGHEST
MESH = pl.DeviceIdType.MESH

D = 1024
HEAD = 64
N_CHIPS = 4
FF_SHARD = 704
SB_W = 256
DL_HEADS = 12
DL_PAIRS = 6
DIL = (1, 4, 16)
QBLK = 128
N_BUCKETS = 32
MAX_DISTANCE = 2048
RW_H = 16
RW_CHUNK = 64
NORM_EPS = 1e-6
GN_EPS = 64e-5
NEG_INF = -1e30
VMEM_LIMIT = 56 * 1024 * 1024

ADAM_LR, ADAM_B1, ADAM_B2, ADAM_EPS, ADAM_WD, ADAM_STEP = 0.001, 0.9, 0.999, 1e-08, 0.01, 10


def _cp(sem):
    return pltpu.CompilerParams(dimension_semantics=sem, vmem_limit_bytes=VMEM_LIMIT)


def _dg(a, b, dims, prec=None):
    return lax.dot_general(a, b, (dims, ((), ())), precision=prec, preferred_element_type=F32)


def _bdot(a, b, dims):
    return _dg(a.astype(BF16), b.astype(BF16), dims)


@jax.custom_vjp
def mm(a, b):
    return _bdot(a, b, ((1,), (0,)))


def _mm_fwd(a, b):
    return _bdot(a, b, ((1,), (0,))), (a, b)


def _mm_bwd(res, g):
    a, b = res
    return _bdot(g, b, ((1,), (1,))), _bdot(a, g, ((0,), (0,)))


mm.defvjp(_mm_fwd, _mm_bwd)


def rms(x, g):
    return x * lax.rsqrt(jnp.mean(x * x, axis=-1, keepdims=True) + NORM_EPS) * g


def group_sum(x, nh):
    w = x.shape[-1]
    e = (lax.broadcasted_iota(jnp.int32, (w, nh), 0) // HEAD == lax.broadcasted_iota(jnp.int32, (w, nh), 1)).astype(F32)
    s = _dg(x, e, ((1,), (0,)), HI)
    return _dg(s, e, ((1,), (1,)), HI)


def softplus(u):
    return jnp.maximum(u, 0.0) + jnp.log1p(jnp.exp(-jnp.abs(u)))


def to_heads(t, nh=RW_H):
    return jnp.stack([t[:, HEAD * h:HEAD * (h + 1)] for h in range(nh)])


def from_heads(t):
    return jnp.concatenate([t[h] for h in range(t.shape[0])], axis=-1)


def _tile_spec(shape, tm):
    if len(shape) == 2:
        return pl.BlockSpec((tm, shape[1]), lambda t: (t, 0))
    return pl.BlockSpec((shape[0], tm, shape[2]), lambda t: (0, t, 0))


def _full_spec(shape):
    nd = len(shape)
    return pl.BlockSpec(tuple(shape), lambda t: (0,) * nd)


def _rows(a):
    return a.shape[0] if a.ndim == 2 else a.shape[1]


def tile_fwd(f, name, tiles, weights, outs, tm):
    nt, nw = len(tiles), len(weights)

    def body(*refs):
        tv = [r[...] for r in refs[:nt]]
        wv = [r[...].astype(F32) for r in refs[nt:nt + nw]]
        res = f(*tv, *wv)
        if not isinstance(res, (tuple, list)):
            res = (res,)
        for o, v in zip(refs[nt + nw:], res):
            o[...] = v.astype(o.dtype)

    return pl.pallas_call(
        body, name=name, grid=(_rows(tiles[0]) // tm,),
        in_specs=[_tile_spec(a.shape, tm) for a in tiles] + [_full_spec(w.shape) for w in weights],
        out_specs=[_tile_spec(o.shape, tm) for o in outs],
        out_shape=list(outs),
        compiler_params=_cp(("parallel",)),
    )(*tiles, *weights)


def tile_bwd(f, name, tiles, weights, cts, tm, dt, dw, acc=None):
    acc = acc or {}
    groups = [c if isinstance(c, tuple) else (c,) for c in cts]
    cts = [a for grp in groups for a in grp]
    nt, nw, nc = len(tiles), len(weights), len(cts)
    acc_idx = sorted(acc)
    na = len(acc_idx)
    dti = [i for i in range(nt) if dt[i]]
    dwi = [i for i in range(nw) if dw[i]]

    def body(*refs):
        tv = [r[...] for r in refs[:nt]]
        wv = [r[...].astype(F32) for r in refs[nt:nt + nw]]
        crefs = list(refs[nt + nw:nt + nw + nc])
        cv = []
        for grp in groups:
            terms = [crefs.pop(0)[...] for _ in grp]
            cv.append(functools.reduce(lambda a, b: a + b, terms))
        av = {i: r[...] for i, r in zip(acc_idx, refs[nt + nw + nc:nt + nw + nc + na])}
        orefs = refs[nt + nw + nc + na:]

        def g(*diff):
            t2, w2 = list(tv), list(wv)
            for i, v in zip(dti, diff[:len(dti)]):
                t2[i] = v
            for i, v in zip(dwi, diff[len(dti):]):
                w2[i] = v
            res = f(*t2, *w2)
            return tuple(res) if isinstance(res, (tuple, list)) else (res,)

        _, vjp = jax.vjp(g, *[tv[i] for i in dti], *[wv[i] for i in dwi])
        grads = vjp(tuple(cv))
        for k, i in enumerate(dti):
            gt = grads[k]
            if i in av:
                gt = gt + av[i]
            orefs[k][...] = gt
        first = pl.program_id(0) == 0
        for k, i in enumerate(dwi):
            o = orefs[len(dti) + k]
            gw = grads[len(dti) + k]

            @pl.when(first)
            def _(o=o, gw=gw):
                o[...] = gw

            @pl.when(jnp.logical_not(first))
            def _(o=o, gw=gw):
                o[...] += gw

    out_shape = [jax.ShapeDtypeStruct(tiles[i].shape, F32) for i in dti] + \
                [jax.ShapeDtypeStruct(weights[i].shape, F32) for i in dwi]
    res = pl.pallas_call(
        body, name=name, grid=(_rows(tiles[0]) // tm,),
        in_specs=[_tile_spec(a.shape, tm) for a in tiles] + [_full_spec(w.shape) for w in weights] +
                 [_tile_spec(c.shape, tm) for c in cts] + [_tile_spec(tiles[i].shape, tm) for i in acc_idx],
        out_specs=[_tile_spec(tiles[i].shape, tm) for i in dti] + [_full_spec(weights[i].shape) for i in dwi],
        out_shape=out_shape,
        compiler_params=_cp(("arbitrary",)),
    )(*tiles, *weights, *cts, *[acc[i] for i in acc_idx])
    return list(res[:len(dti)]), list(res[len(dti):])


def _ffn_wspec(l, j, rows, cols, cfirst):
    if cfirst:
        return pl.BlockSpec((1, 1, 1, rows, cols), lambda c, t: (c, l, j, 0, 0))
    return pl.BlockSpec((1, 1, 1, rows, cols), lambda t, c: (c, l, j, 0, 0))


def ffn_fwd(x, g, wg, wu, wd, l, j, tm=512):
    S = x.shape[0]

    def body(x_ref, g_ref, wg_ref, wu_ref, wd_ref, o_ref, h_ref, acc_ref):
        c = pl.program_id(1)

        @pl.when(c == 0)
        def _():
            h_ref[...] = rms(x_ref[...], g_ref[...]).astype(BF16)
            acc_ref[...] = jnp.zeros_like(acc_ref)

        h = h_ref[...]
        a = _bdot(h, wg_ref[0, 0, 0], ((1,), (0,)))
        b = _bdot(h, wu_ref[0, 0, 0], ((1,), (0,)))
        y = a * jax.nn.sigmoid(a) * b
        acc_ref[...] += _bdot(y, wd_ref[0, 0, 0], ((1,), (0,)))

        @pl.when(c == N_CHIPS - 1)
        def _():
            o_ref[...] = x_ref[...] + 0.5 * acc_ref[...]

    return pl.pallas_call(
        body, name=f"ffn_fwd_{l}{j}", grid=(S // tm, N_CHIPS),
        in_specs=[pl.BlockSpec((tm, D), lambda t, c: (t, 0)), pl.BlockSpec((1, D), lambda t, c: (0, 0)),
                  _ffn_wspec(l, j, D, FF_SHARD, False), _ffn_wspec(l, j, D, FF_SHARD, False),
                  _ffn_wspec(l, j, FF_SHARD, D, False)],
        out_specs=pl.BlockSpec((tm, D), lambda t, c: (t, 0)),
        out_shape=jax.ShapeDtypeStruct((S, D), F32),
        scratch_shapes=[pltpu.VMEM((tm, D), BF16), pltpu.VMEM((tm, D), F32)],
        compiler_params=_cp(("parallel", "arbitrary")),
    )(x, g, wg, wu, wd)


def ffn_bwd(x, g, wg, wu, wd, dout, dws, l, j, tm=512):
    S = x.shape[0]

    def body(x_ref, g_ref, wg_ref, wu_ref, wd_ref, do_ref, _g, _u, _d, dh_ref, dwg_ref, dwu_ref, dwd_ref):
        dwg_ref, dwu_ref, dwd_ref = dwg_ref.at[0, 0], dwu_ref.at[0, 0], dwd_ref.at[0, 0]
        t = pl.program_id(1)
        h = rms(x_ref[...], g_ref[...]).astype(BF16)
        wgv, wuv, wdv = wg_ref[0, 0, 0], wu_ref[0, 0, 0], wd_ref[0, 0, 0]
        a = _bdot(h, wgv, ((1,), (0,)))
        b = _bdot(h, wuv, ((1,), (0,)))
        sig = jax.nn.sigmoid(a)
        s = a * sig
        dyd = 0.5 * do_ref[...]
        dy = _bdot(dyd, wdv, ((1,), (1,)))
        dwd = _bdot(s * b, dyd, ((0,), (0,)))
        db = dy * s
        da = dy * b * (sig * (1.0 + a * (1.0 - sig)))
        dwg = _bdot(h, da, ((0,), (0,)))
        dwu = _bdot(h, db, ((0,), (0,)))
        dh_ref[0] = _bdot(da, wgv, ((1,), (1,))) + _bdot(db, wuv, ((1,), (1,)))

        @pl.when(t == 0)
        def _():
            dwg_ref[0] = dwg
            dwu_ref[0] = dwu
            dwd_ref[0] = dwd

        @pl.when(t != 0)
        def _():
            dwg_ref[0] += dwg
            dwu_ref[0] += dwu
            dwd_ref[0] += dwd

    return pl.pallas_call(
        body, name=f"ffn_bwd_{l}{j}", grid=(N_CHIPS, S // tm),
        in_specs=[pl.BlockSpec((tm, D), lambda c, t: (t, 0)), pl.BlockSpec((1, D), lambda c, t: (0, 0)),
                  _ffn_wspec(l, j, D, FF_SHARD, True), _ffn_wspec(l, j, D, FF_SHARD, True),
                  _ffn_wspec(l, j, FF_SHARD, D, True), pl.BlockSpec((tm, D), lambda c, t: (t, 0))] +
                 [pl.BlockSpec(memory_space=pl.ANY)] * 3,
        out_specs=[pl.BlockSpec((1, tm, D), lambda c, t: (c, t, 0)),
                   _ffn_wspec(l, j, D, FF_SHARD, True), _ffn_wspec(l, j, D, FF_SHARD, True),
                   _ffn_wspec(l, j, FF_SHARD, D, True)],
        out_shape=[jax.ShapeDtypeStruct((N_CHIPS, S, D), F32)] + [jax.ShapeDtypeStruct(a.shape, F32) for a in dws],
        input_output_aliases={6: 1, 7: 2, 8: 3},
        compiler_params=_cp(("parallel", "arbitrary")),
    )(x, g, wg, wu, wd, dout, *dws)


def norm_bwd(name, x, g, dh_parts, dres, tm=256):
    S = x.shape[0]
    P = dh_parts.shape[0]

    def body(x_ref, g_ref, dh_ref, dr_ref, dx_ref, dg_ref):
        dh = dh_ref[0]
        for p in range(1, P):
            dh = dh + dh_ref[p]
        _, vjp = jax.vjp(rms, x_ref[...], g_ref[...])
        dx, dg = vjp(dh)
        dx_ref[...] = dr_ref[...] + dx

        @pl.when(pl.program_id(0) == 0)
        def _():
            dg_ref[...] = dg

        @pl.when(pl.program_id(0) != 0)
        def _():
            dg_ref[...] += dg

    return pl.pallas_call(
        body, name=name, grid=(S // tm,),
        in_specs=[pl.BlockSpec((tm, D), lambda t: (t, 0)), pl.BlockSpec((1, D), lambda t: (0, 0)),
                  pl.BlockSpec((P, tm, D), lambda t: (0, t, 0)), pl.BlockSpec((tm, D), lambda t: (t, 0))],
        out_specs=[pl.BlockSpec((tm, D), lambda t: (t, 0)), pl.BlockSpec((1, D), lambda t: (0, 0))],
        out_shape=[jax.ShapeDtypeStruct((S, D), F32), jax.ShapeDtypeStruct((1, D), F32)],
        compiler_params=_cp(("arbitrary",)),
    )(x, g, dh_parts, dres)


def f_attn_sb(x, g, w):
    pr = mm(rms(x, g), w)
    return pr[:, :SB_W], pr[:, SB_W:2 * SB_W], pr[:, 2 * SB_W:]


def _pairs(y):
    return jnp.stack([y[:, 128 * j:128 * (j + 1)] for j in range(DL_PAIRS)])


def f_attn_qk(x, g, w, nrm):
    pr = mm(rms(x, g), w)
    ms = group_sum(pr * pr, DL_HEADS) * (1.0 / HEAD)
    return _pairs(pr * lax.rsqrt(ms + NORM_EPS) * jnp.concatenate([nrm] * DL_HEADS, axis=1))


def f_attn_v(x, g, w):
    return _pairs(mm(rms(x, g), w))


def _masked(strict, x):
    return x if strict is None else jnp.where(strict, x, 0.0)


def _head_stack(x):
    nh = x.shape[1] // HEAD
    lane_head = lax.broadcasted_iota(jnp.int32, (1, x.shape[1]), 1) // HEAD
    return jnp.concatenate([jnp.where(lane_head == h, x, 0.0) for h in range(nh)], axis=0).astype(BF16)


def _head_pick(xs):
    nh = xs.shape[1] // HEAD
    rows = xs.shape[0] // nh
    lane_head = lax.broadcasted_iota(jnp.int32, (1, xs.shape[1]), 1) // HEAD
    out = xs[:rows]
    for h in range(1, nh):
        out = jnp.where(lane_head == h, xs[rows * h:rows * (h + 1)], out)
    return out


def _sb_tiles(qs, kblk, strict):
    z = _dg(qs, kblk, ((1,), (1,))) * (HEAD ** -0.5)
    keep = -(jnp.maximum(z, 0.0) + jnp.log(1.0 + jnp.exp(-jnp.abs(z))))
    return z, _masked(strict, keep)


def _tri(n, upper):
    r = lax.broadcasted_iota(jnp.int32, (n, n), 0)
    c = lax.broadcasted_iota(jnp.int32, (n, n), 1)
    return ((r > c) if upper else (r < c)).astype(BF16)


def _tri_sums(x, tri):
    hi, lo = _split2(x)
    return _dg(hi, tri, ((1,), (0,))) + _dg(lo, tri, ((1,), (0,)))


SB_UNROLL = 4


def _sb_diag(tb, nh):
    r = lax.broadcasted_iota(jnp.int32, (nh * tb, tb), 0)
    return lax.broadcasted_iota(jnp.int32, (nh * tb, tb), 1) < lax.rem(r, tb)


def _sb_sweep(step, first, count, carry, direction, commit=None):
    def run(kbs, c):
        outs = []
        for kb in kbs:
            c, out = step(kb, c)
            outs.append(out)
        if commit is not None:
            for kb, out in zip(kbs, outs):
                commit(kb, out)
        return c

    rem = count % SB_UNROLL
    carry = lax.fori_loop(0, rem, lambda i, c: run([first + direction * i], c), carry)
    return lax.fori_loop(
        0, count // SB_UNROLL,
        lambda g, c: run([first + direction * (rem + SB_UNROLL * g + u) for u in range(SB_UNROLL)], c), carry)


def sb_fwd(q, k, v, tb=QBLK):
    S = q.shape[0]
    nh = SB_W // HEAD

    def body(q_ref, k_ref, v_ref, o_ref):
        qb = pl.program_id(0)
        diag = _sb_diag(tb, nh)
        after_mat = _tri(tb, True)
        qs = _head_stack(q_ref[...])

        def step(kb, carry, strict):
            acc, run = carry
            rows = pl.ds(pl.multiple_of(kb * tb, tb), tb)
            z, keep = _sb_tiles(qs, k_ref[rows, :].astype(BF16), strict)
            w = _masked(strict, jnp.exp(z + keep + _tri_sums(keep, after_mat) + run))
            acc = acc + _dg(w.astype(BF16), v_ref[rows, :].astype(BF16), ((1,), (0,)))
            return acc, run + jnp.sum(keep, axis=1, keepdims=True)

        init = (jnp.zeros((nh * tb, SB_W), F32), jnp.zeros((nh * tb, 1), F32))
        carry = step(qb, init, diag)
        acc, _ = _sb_sweep(lambda kb, c: (step(kb, c, None), None), qb - 1, qb, carry, -1)
        o_ref[...] = _head_pick(acc)

    return pl.pallas_call(
        body, name="sb_fwd", grid=(S // tb,),
        in_specs=[pl.BlockSpec((tb, SB_W), lambda i: (i, 0)), pl.BlockSpec((S, SB_W), lambda i: (0, 0)),
                  pl.BlockSpec((S, SB_W), lambda i: (0, 0))],
        out_specs=pl.BlockSpec((tb, SB_W), lambda i: (i, 0)),
        out_shape=jax.ShapeDtypeStruct((S, SB_W), F32),
        compiler_params=_cp(("parallel",)),
    )(q, k, v)


def sb_bwd(q, k, v, do, tb=QBLK):
    S = q.shape[0]
    nh = SB_W // HEAD
    scale = HEAD ** -0.5

    def body(q_ref, k_ref, v_ref, do_ref, dq_ref, dk_ref, dv_ref, g_scr):
        qb = pl.program_id(0)

        @pl.when(qb == 0)
        def _():
            dk_ref[...] = jnp.zeros_like(dk_ref)
            dv_ref[...] = jnp.zeros_like(dv_ref)

        diag = _sb_diag(tb, nh)
        after_mat = _tri(tb, True)
        before_mat = _tri(tb, False)
        qs = _head_stack(q_ref[...])
        dos = _head_stack(do_ref[...])

        def right_to_left(kb, run, strict):
            rows = pl.ds(pl.multiple_of(kb * tb, tb), tb)
            z, keep = _sb_tiles(qs, k_ref[rows, :].astype(BF16), strict)
            w = _masked(strict, jnp.exp(z + keep + _tri_sums(keep, after_mat) + run))
            g_scr[kb] = _dg(dos, v_ref[rows, :].astype(BF16), ((1,), (1,))) * w
            dv = _dg(w.astype(BF16), dos, ((0,), (0,)))
            return run + jnp.sum(keep, axis=1, keepdims=True), dv

        def add_rows(ref):
            def commit(kb, val):
                ref[pl.ds(pl.multiple_of(kb * tb, tb), tb), :] += val
            return commit

        zero_run = jnp.zeros((nh * tb, 1), F32)
        run, dv_diag = right_to_left(qb, zero_run, diag)
        add_rows(dv_ref)(qb, dv_diag)
        _sb_sweep(lambda kb, r: right_to_left(kb, r, None), qb - 1, qb, run, -1, add_rows(dv_ref))

        def left_to_right(kb, carry, strict):
            dq, run = carry
            rows = pl.ds(pl.multiple_of(kb * tb, tb), tb)
            kblk = k_ref[rows, :].astype(BF16)
            gw = g_scr[kb]
            sig = jax.nn.sigmoid(_dg(qs, kblk, ((1,), (1,))) * scale)
            dkeep = _masked(strict, _tri_sums(gw, before_mat) + run)
            dz = ((gw * (1.0 - sig) - dkeep * sig) * scale).astype(BF16)
            dq = dq + _dg(dz, kblk, ((1,), (0,)))
            return (dq, run + jnp.sum(gw, axis=1, keepdims=True)), _dg(dz, qs, ((0,), (0,)))

        carry = _sb_sweep(lambda kb, c: left_to_right(kb, c, None), 0, qb,
                          (jnp.zeros((nh * tb, SB_W), F32), zero_run), 1, add_rows(dk_ref))
        (dq, _), dk_diag = left_to_right(qb, carry, diag)
        add_rows(dk_ref)(qb, dk_diag)
        dq_ref[...] = _head_pick(dq)

    whole = pl.BlockSpec((S, SB_W), lambda i: (0, 0))
    blk = pl.BlockSpec((tb, SB_W), lambda i: (i, 0))
    return pl.pallas_call(
        body, name="sb_bwd", grid=(S // tb,),
        in_specs=[blk, whole, whole, blk], out_specs=[blk, whole, whole],
        out_shape=[jax.ShapeDtypeStruct((S, SB_W), F32)] * 3,
        scratch_shapes=[pltpu.VMEM((S // tb, nh * tb, tb), F32)],
        compiler_params=_cp(("arbitrary",)),
    )(q, k, v, do)


def reorder(name, x, groups, inverse):
    P, S, _ = x.shape

    def body(x_ref, o_ref):
        p = pl.program_id(0)
        for gi, r in enumerate(groups):
            @pl.when(p // 2 == gi)
            def _(r=r):
                L = S // r
                if r == 1:
                    o_ref[...] = x_ref[...]
                for c in range(r if r > 1 else 0):
                    if inverse:
                        o_ref[pl.ds(c, L, stride=r), :] = x_ref[c * L:(c + 1) * L, :]
                    else:
                        o_ref[c * L:(c + 1) * L, :] = x_ref[pl.ds(c, L, stride=r), :]

    slab = pl.BlockSpec((None, S, 128), lambda p: (p, 0, 0))
    return pl.pallas_call(
        body, name=name, grid=(P,), in_specs=[slab], out_specs=slab,
        out_shape=jax.ShapeDtypeStruct(x.shape, x.dtype), compiler_params=_cp(("parallel",)),
    )(x)


def _dil_blocks(S):
    return S // QBLK


def _dil_mask(n_in_stream):
    qi = lax.broadcasted_iota(jnp.int32, (QBLK, 2 * QBLK), 0)
    kj = lax.broadcasted_iota(jnp.int32, (QBLK, 2 * QBLK), 1) - QBLK
    dist = qi - kj
    return (dist >= 0) & (dist <= QBLK) & ((n_in_stream > 0) | (kj >= 0))


def _stream_pos(gi, i, S):
    nb = jnp.where(gi == 0, S // (QBLK * DIL[0]), jnp.where(gi == 1, S // (QBLK * DIL[1]), S // (QBLK * DIL[2])))
    return i % nb


def dil_fwd(q, k, v, bias):
    S = q.shape[1]
    nblk = _dil_blocks(S)

    def body(q_ref, kc_ref, kp_ref, vc_ref, vp_ref, b_ref, o_ref, l_ref):
        gi, i = pl.program_id(0), pl.program_id(1)
        mask = _dil_mask(_stream_pos(gi, i, S))
        for j in range(2):
            q2, kc, kp, vc, vp = q_ref[j], kc_ref[j], kp_ref[j], vc_ref[j], vp_ref[j]
            os_, ls_ = [], []
            for hh in range(2):
                sl = slice(HEAD * hh, HEAD * (hh + 1))
                kw = jnp.concatenate([kp[:, sl], kc[:, sl]], axis=0)
                vw = jnp.concatenate([vp[:, sl], vc[:, sl]], axis=0)
                lg = _bdot(q2[:, sl], kw, ((1,), (1,))) * (HEAD ** -0.5) + b_ref[2 * j + hh]
                lg = jnp.where(mask, lg, NEG_INF)
                m = jnp.max(lg, axis=-1, keepdims=True)
                p = jnp.exp(lg - m)
                den = jnp.sum(p, axis=-1, keepdims=True)
                os_.append(_bdot(p / den, vw, ((1,), (0,))))
                ls_.append(jnp.broadcast_to(m + jnp.log(den), (QBLK, HEAD)))
            o_ref[j] = jnp.concatenate(os_, axis=1)
            l_ref[j] = jnp.concatenate(ls_, axis=1)

    cur = pl.BlockSpec((2, QBLK, 128), lambda g, i: (g, i, 0))
    prev = pl.BlockSpec((2, QBLK, 128), lambda g, i: (g, jnp.maximum(i - 1, 0), 0))
    return pl.pallas_call(
        body, name="dil_fwd", grid=(len(DIL), nblk),
        in_specs=[cur, cur, prev, cur, prev, pl.BlockSpec((4, QBLK, 2 * QBLK), lambda g, i: (g, 0, 0))],
        out_specs=[cur, cur],
        out_shape=[jax.ShapeDtypeStruct(q.shape, F32)] * 2,
        compiler_params=_cp(("parallel", "parallel")),
    )(q, k, k, v, v, bias)


def dil_bwd(q, k, v, bias, o, lse, do, dlse):
    S = q.shape[1]
    nblk = _dil_blocks(S)

    def body(q_ref, kc_ref, kp_ref, vc_ref, vp_ref, b_ref, o_ref, l_ref, do_ref, dl_ref,
             dq_ref, dk_ref, dv_ref, ds_ref, dk_car, dv_car):
        gi, i = pl.program_id(0), pl.program_id(1)

        @pl.when(i == 0)
        def _():
            ds_ref[...] = jnp.zeros_like(ds_ref)
            dk_car[...] = jnp.zeros_like(dk_car)
            dv_car[...] = jnp.zeros_like(dv_car)

        @pl.when(i < nblk)
        def _():
            mask = _dil_mask(_stream_pos(gi, i, S))
            for j in range(2):
                q2, kc, kp, vc, vp = q_ref[j], kc_ref[j], kp_ref[j], vc_ref[j], vp_ref[j]
                o2, l2, do2, dl2 = o_ref[j], l_ref[j], do_ref[j], dl_ref[j]
                dqs, dkps, dkcs, dvps, dvcs = [], [], [], [], []
                for hh in range(2):
                    sl = slice(HEAD * hh, HEAD * (hh + 1))
                    qh, doh = q2[:, sl], do2[:, sl]
                    kw = jnp.concatenate([kp[:, sl], kc[:, sl]], axis=0)
                    vw = jnp.concatenate([vp[:, sl], vc[:, sl]], axis=0)
                    lg = _bdot(qh, kw, ((1,), (1,))) * (HEAD ** -0.5) + b_ref[2 * j + hh]
                    p = jnp.where(mask, jnp.exp(lg - l2[:, HEAD * hh:HEAD * hh + 1]), 0.0)
                    dp = _bdot(doh, vw, ((1,), (1,)))
                    delta = jnp.sum(doh * o2[:, sl], axis=-1, keepdims=True)
                    dl = jnp.sum(dl2[:, sl], axis=-1, keepdims=True)
                    ds = p * (dp - delta + dl)
                    ds_ref[2 * j + hh] += ds
                    dsq = ds * (HEAD ** -0.5)
                    dqs.append(_bdot(dsq, kw, ((1,), (0,))))
                    dkw = _bdot(dsq, qh, ((0,), (0,)))
                    dvw = _bdot(p, doh, ((0,), (0,)))
                    dkps.append(dkw[:QBLK])
                    dkcs.append(dkw[QBLK:])
                    dvps.append(dvw[:QBLK])
                    dvcs.append(dvw[QBLK:])
                dq_ref[j] = jnp.concatenate(dqs, axis=1)
                dk_ref[j] = dk_car[j] + jnp.concatenate(dkps, axis=1)
                dv_ref[j] = dv_car[j] + jnp.concatenate(dvps, axis=1)
                dk_car[j] = jnp.concatenate(dkcs, axis=1)
                dv_car[j] = jnp.concatenate(dvcs, axis=1)

        @pl.when(i == nblk)
        def _():
            dk_ref[...] = dk_car[...]
            dv_ref[...] = dv_car[...]

    cur = pl.BlockSpec((2, QBLK, 128), lambda g, i: (g, jnp.minimum(i, nblk - 1), 0))
    prev = pl.BlockSpec((2, QBLK, 128), lambda g, i: (g, jnp.clip(i - 1, 0, nblk - 1), 0))
    bspec = pl.BlockSpec((4, QBLK, 2 * QBLK), lambda g, i: (g, 0, 0))
    return pl.pallas_call(
        body, name="dil_bwd", grid=(len(DIL), nblk + 1),
        in_specs=[cur, cur, prev, cur, prev, bspec, cur, cur, cur, cur],
        out_specs=[cur, prev, prev, bspec],
        out_shape=[jax.ShapeDtypeStruct(q.shape, F32)] * 3 + [jax.ShapeDtypeStruct(bias.shape, F32)],
        scratch_shapes=[pltpu.VMEM((2, QBLK, 128), F32), pltpu.VMEM((2, QBLK, 128), F32)],
        compiler_params=_cp(("arbitrary", "arbitrary")),
    )(q, k, k, v, v, bias, o, lse, do, dlse)


def _t5_bucket(dist):
    max_exact = N_BUCKETS // 2
    d = jnp.maximum(dist, 1).astype(F32)
    large = max_exact + (jnp.log(d / max_exact) / math.log(MAX_DISTANCE / max_exact)
                         * (N_BUCKETS - max_exact)).astype(jnp.int32)
    large = jnp.minimum(large, N_BUCKETS - 1)
    return jnp.where(dist < max_exact, dist, large)


def _bucket_maps():
    qi = jnp.arange(QBLK)[:, None]
    kj = jnp.arange(2 * QBLK)[None, :] - QBLK
    dist = jnp.maximum(qi - kj, 0)
    return jnp.stack([_t5_bucket(dist * r) for r in DIL])


def bias_table(rel_bias, buckets):
    def body(tbl_ref, bk_ref, o_ref):
        for h in range(DL_HEADS):
            bk = bk_ref[h // 4]

            def step(b, acc):
                return jnp.where(bk == b, tbl_ref[b, h], acc)

            o_ref[h] = lax.fori_loop(0, N_BUCKETS, step, jnp.zeros(bk.shape, F32))

    return pl.pallas_call(
        body, name="bias_table", out_shape=jax.ShapeDtypeStruct((DL_HEADS,) + buckets.shape[1:], F32),
        in_specs=[pl.BlockSpec(memory_space=pltpu.SMEM), pl.BlockSpec(memory_space=pltpu.VMEM)],
        out_specs=pl.BlockSpec(memory_space=pltpu.VMEM),
    )(rel_bias, buckets)


def bias_grad(ds, buckets):
    def body(ds_ref, bk_ref, o_ref):
        lane = lax.broadcasted_iota(jnp.int32, (1, 128), 1)
        for h in range(DL_HEADS):
            dsv = ds_ref[h]
            bk = bk_ref[h // 4]

            def step(b, row):
                return jnp.where(lane == b, jnp.sum(jnp.where(bk == b, dsv, 0.0)), row)

            o_ref[h:h + 1, :] = lax.fori_loop(0, N_BUCKETS, step, jnp.zeros((1, 128), F32))

    return pl.pallas_call(
        body, name="bias_grad", out_shape=jax.ShapeDtypeStruct((DL_HEADS, 128), F32),
        in_specs=[pl.BlockSpec(memory_space=pltpu.VMEM)] * 2, out_specs=pl.BlockSpec(memory_space=pltpu.VMEM),
    )(ds, buckets)


def f_attn_out(x, oa, o, lse, w):
    og = [jnp.concatenate([o[2 * g], o[2 * g + 1]], axis=1) for g in range(3)]
    lg = [jnp.concatenate([lse[2 * g], lse[2 * g + 1]], axis=1) for g in range(3)]
    m = jnp.maximum(jnp.maximum(lg[0], lg[1]), lg[2])
    e = [jnp.exp(l - m) for l in lg]
    den = e[0] + e[1] + e[2]
    ob = (e[0] * og[0] + e[1] * og[1] + e[2] * og[2]) / den
    return x + mm(jnp.concatenate([oa, ob], axis=1), w)


def norm_shift_fwd(x, g, tm=256):
    S = x.shape[0]

    def body(x_ref, xp_ref, g_ref, h_ref, hs_ref):
        h = rms(x_ref[...], g_ref[...])
        hp = rms(xp_ref[7:8, :], g_ref[...])
        hp = jnp.where(pl.program_id(0) == 0, 0.0, hp)
        row = lax.broadcasted_iota(jnp.int32, (tm, D), 0)
        h_ref[...] = h
        hs_ref[...] = jnp.where(row == 0, hp, pltpu.roll(h, 1, 0))

    return pl.pallas_call(
        body, name="rw_norm_shift", grid=(S // tm,),
        in_specs=[pl.BlockSpec((tm, D), lambda t: (t, 0)),
                  pl.BlockSpec((8, D), lambda t: (jnp.maximum(t * (tm // 8) - 1, 0), 0)),
                  pl.BlockSpec((1, D), lambda t: (0, 0))],
        out_specs=[pl.BlockSpec((tm, D), lambda t: (t, 0))] * 2,
        out_shape=[jax.ShapeDtypeStruct((S, D), F32)] * 2,
        compiler_params=_cp(("parallel",)),
    )(x, x, g)


def norm_shift_bwd(x, g, dh, dhs, dres, tm=256):
    S = x.shape[0]
    nt = S // tm

    def body(x_ref, g_ref, dh_ref, dhs_ref, dhn_ref, dr_ref, dx_ref, dg_ref):
        t = pl.program_id(0)
        nxt = jnp.where(t == nt - 1, 0.0, dhn_ref[0:1, :])
        row = lax.broadcasted_iota(jnp.int32, (tm, D), 0)
        tot = dh_ref[...] + jnp.where(row == tm - 1, nxt, pltpu.roll(dhs_ref[...], tm - 1, 0))
        _, vjp = jax.vjp(rms, x_ref[...], g_ref[...])
        dx, dg = vjp(tot)
        dx_ref[...] = dr_ref[...] + dx

        @pl.when(t == 0)
        def _():
            dg_ref[...] = dg

        @pl.when(t != 0)
        def _():
            dg_ref[...] += dg

    tile = pl.BlockSpec((tm, D), lambda t: (t, 0))
    return pl.pallas_call(
        body, name="rw_norm_shift_bwd", grid=(nt,),
        in_specs=[tile, pl.BlockSpec((1, D), lambda t: (0, 0)), tile, tile,
                  pl.BlockSpec((8, D), lambda t: (jnp.minimum((t + 1) * (tm // 8), S // 8 - 1), 0)), tile],
        out_specs=[tile, pl.BlockSpec((1, D), lambda t: (0, 0))],
        out_shape=[jax.ShapeDtypeStruct((S, D), F32), jax.ShapeDtypeStruct((1, D), F32)],
        compiler_params=_cp(("arbitrary",)),
    )(x, g, dh, dhs, dhs, dres)


def f_rw_proj(h, hs, mix, w):
    return mm(h + (hs - h) * mix, w)


def f_rw_mid(h, hs, r, k, v, mix3, w0, a0, kkw, kaw, w1, w2, a1, a2, g1, g2):
    xx = hs - h
    xw, xa, xg = h + xx * mix3[0:1], h + xx * mix3[1:2], h + xx * mix3[2:3]
    w_log = -softplus(-(w0 + mm(jnp.tanh(mm(xw, w1)), w2))) - 0.5
    lw = -jnp.exp(w_log)
    ag = jax.nn.sigmoid(a0 + mm(mm(xa, a1), a2))
    gate = mm(jax.nn.sigmoid(mm(xg, g1)), g2)
    kk = k * kkw
    kk = kk / jnp.maximum(jnp.sqrt(group_sum(kk * kk, RW_H)), 1e-12)
    kmod = k * (1.0 + (ag - 1.0) * kaw)
    return (to_heads(r), to_heads(lw), to_heads(kmod), to_heads(v), to_heads(-kk), to_heads(kk * ag), gate)


def f_rw_post(yh, rh, kh, vh, gate, x, lng, lnb, rk, wo):
    mu = jnp.mean(yh, axis=-1, keepdims=True)
    var = jnp.mean(jnp.square(yh - mu), axis=-1, keepdims=True)
    yn = (yh - mu) * lax.rsqrt(var + GN_EPS)
    bonus = jnp.sum(rh * kh * rk, axis=-1, keepdims=True) * vh
    y = from_heads(yn) * lng + lnb + from_heads(bonus)
    return x + mm(y * gate, wo)


def _split2(x):
    hi = x.astype(BF16)
    return hi, (x - hi.astype(F32)).astype(BF16)


def _b3(x, y, cx, cy):
    dn = (((cx,), (cy,)), ((0,), (0,)))
    xh, xl = _split2(x)
    yh, yl = _split2(y)
    d = lambda p, q: lax.dot_general(p, q, dn, preferred_element_type=F32)
    return d(xh, yh) + (d(xh, yl) + d(xl, yh))


@jax.custom_vjp
def b_nt(x, y):
    return _b3(x, y, 2, 2)


@jax.custom_vjp
def b_nn(x, y):
    return _b3(x, y, 2, 1)


@jax.custom_vjp
def b_tn(x, y):
    return _b3(x, y, 1, 1)


b_nt.defvjp(lambda x, y: (b_nt(x, y), (x, y)), lambda r, g: (b_nn(g, r[1]), b_tn(g, r[0])))
b_nn.defvjp(lambda x, y: (b_nn(x, y), (x, y)), lambda r, g: (b_nt(g, r[1]), b_tn(r[0], g)))
b_tn.defvjp(lambda x, y: (b_tn(x, y), (x, y)), lambda r, g: (b_nt(r[1], g), b_nn(r[0], g)))


def _tri_apply(x, lower):
    H, C, _ = x.shape
    ii = lax.broadcasted_iota(jnp.int32, (C, C), 0)
    jj = lax.broadcasted_iota(jnp.int32, (C, C), 1)
    m = jnp.broadcast_to(((jj <= ii) if lower else (jj >= ii)).astype(BF16), (H, C, C))
    x1 = x.astype(BF16)
    r1 = x - x1.astype(F32)
    x2 = r1.astype(BF16)
    x3 = (r1 - x2.astype(F32)).astype(BF16)
    d = lambda q: lax.dot_general(m, q, (((2,), (1,)), ((0,), (0,))), preferred_element_type=F32)
    return d(x1) + (d(x2) + d(x3))


@jax.custom_vjp
def run_sum(x):
    return _tri_apply(x, True)


run_sum.defvjp(lambda x: (run_sum(x), None), lambda _, g: (_tri_apply(g, False),))


def rwkv_chunk(S0, r, lw, k, v, a, b):
    H, C, _ = r.shape
    V = S0.shape[1]
    ii = lax.broadcasted_iota(jnp.int32, (C, C), 0)
    jj = lax.broadcasted_iota(jnp.int32, (C, C), 1)
    strict = jj < ii
    i2 = lax.broadcasted_iota(jnp.int32, (C, 2 * C), 0)
    j2 = lax.broadcasted_iota(jnp.int32, (C, 2 * C), 1)
    incl2 = jnp.where(j2 >= C, j2 - C, j2) <= i2
    g = run_sum(lw)
    ig = jnp.exp(-g)
    ar = jnp.concatenate([a * jnp.exp(g - lw), r * jnp.exp(g)], axis=1)
    bk = jnp.concatenate([b * ig, k * ig], axis=1)
    m = b_nt(ar, bk)
    a_ab = jnp.where(strict, m[:, :C, :C], 0.0)
    a_ak = jnp.where(strict, m[:, :C, C:], 0.0)
    b_r = jnp.where(incl2, m[:, C:, :], 0.0)
    p = b_nt(ar, S0)
    u = p[:, :C] + b_nn(a_ak, v)
    nmat, n = a_ab, 1
    while n < C:
        n *= 2
        if n < C:
            z = b_nn(nmat, jnp.concatenate([u, nmat], axis=2))
            u, nmat = u + z[:, :, :V], z[:, :, V:]
        else:
            u = u + b_nn(nmat, u)
    uv = jnp.concatenate([u, v], axis=1)
    y = p[:, C:] + b_nn(b_r, uv)
    g_end = g[:, C - 1:C, :]
    dec = jnp.exp(g_end - g)
    s_new = S0 * jnp.exp(g_end) + b_tn(uv, jnp.concatenate([b * dec, k * dec], axis=1))
    return y, s_new


def rwkv_fwd(r, lw, k, v, a, b):
    H, S, _ = r.shape
    C = RW_CHUNK

    def body(r_ref, lw_ref, k_ref, v_ref, a_ref, b_ref, y_ref, s_ref, s_scr):
        @pl.when(pl.program_id(0) == 0)
        def _():
            s_scr[...] = jnp.zeros_like(s_scr)

        s0 = s_scr[...]
        s_ref[0] = s0
        y, s1 = rwkv_chunk(s0, r_ref[...], lw_ref[...], k_ref[...], v_ref[...], a_ref[...], b_ref[...])
        y_ref[...] = y
        s_scr[...] = s1

    bs = pl.BlockSpec((H, C, HEAD), lambda c: (0, c, 0))
    return pl.pallas_call(
        body, name="rwkv_fwd", grid=(S // C,), in_specs=[bs] * 6,
        out_specs=[bs, pl.BlockSpec((1, H, HEAD, HEAD), lambda c: (c, 0, 0, 0))],
        out_shape=[jax.ShapeDtypeStruct((H, S, HEAD), F32), jax.ShapeDtypeStruct((S // C, H, HEAD, HEAD), F32)],
        scratch_shapes=[pltpu.VMEM((H, HEAD, HEAD), F32)],
        compiler_params=_cp(("arbitrary",)),
    )(r, lw, k, v, a, b)


def rwkv_bwd(r, lw, k, v, a, b, states, dy):
    H, S, _ = r.shape
    C = RW_CHUNK
    nc = S // C

    def body(r_ref, lw_ref, k_ref, v_ref, a_ref, b_ref, s_ref, dy_ref, dr, dlw, dk, dv, da, db, ds_scr):
        @pl.when(pl.program_id(0) == 0)
        def _():
            ds_scr[...] = jnp.zeros_like(ds_scr)

        _, vjp = jax.vjp(rwkv_chunk, s_ref[0], r_ref[...], lw_ref[...], k_ref[...], v_ref[...], a_ref[...], b_ref[...])
        grads = vjp((dy_ref[...], ds_scr[...]))
        ds_scr[...] = grads[0]
        for o, gv in zip((dr, dlw, dk, dv, da, db), grads[1:]):
            o[...] = gv

    bs = pl.BlockSpec((H, C, HEAD), lambda c: (0, nc - 1 - c, 0))
    return pl.pallas_call(
        body, name="rwkv_bwd", grid=(nc,),
        in_specs=[bs] * 6 + [pl.BlockSpec((1, H, HEAD, HEAD), lambda c: (nc - 1 - c, 0, 0, 0)), bs],
        out_specs=[bs] * 6, out_shape=[jax.ShapeDtypeStruct((H, S, HEAD), F32)] * 6,
        scratch_shapes=[pltpu.VMEM((H, HEAD, HEAD), F32)],
        compiler_params=_cp(("arbitrary",)),
    )(r, lw, k, v, a, b, states, dy)


def loss_head(y, target, tm=512):
    S = y.shape[0]

    def body(y_ref, t_ref, dy_ref, l_ref):
        e = y_ref[...] - t_ref[...]
        dy_ref[...] = e * (1.0 / D)
        part = jnp.broadcast_to(0.5 * jnp.sum(jnp.mean(e * e, axis=-1, keepdims=True)), (1, 128))

        @pl.when(pl.program_id(0) == 0)
        def _():
            l_ref[...] = part

        @pl.when(pl.program_id(0) != 0)
        def _():
            l_ref[...] += part

    tile = pl.BlockSpec((tm, D), lambda t: (t, 0))
    return pl.pallas_call(
        body, name="loss_head", grid=(S // tm,), in_specs=[tile, tile],
        out_specs=[tile, pl.BlockSpec((1, 128), lambda t: (0, 0))],
        out_shape=[jax.ShapeDtypeStruct((S, D), F32), jax.ShapeDtypeStruct((1, 128), F32)],
        compiler_params=_cp(("arbitrary",)),
    )(y, target)


def _row_tile(rows, cols, budget=1 << 19):
    best = None
    for tr in range(8, rows + 1, 8):
        if rows % tr == 0 and tr * cols <= budget:
            best = tr
    return best or rows


def _adam(w, g, m, v):
    m = ADAM_B1 * m + (1.0 - ADAM_B1) * g
    v = ADAM_B2 * v + (1.0 - ADAM_B2) * jnp.square(g)
    m_hat = m / (1.0 - ADAM_B1 ** ADAM_STEP)
    v_hat = v / (1.0 - ADAM_B2 ** ADAM_STEP)
    return -ADAM_LR * (m_hat / (jnp.sqrt(v_hat) + ADAM_EPS) + ADAM_WD * w), m, v


def sum_slots(name, parts, dtype=F32, extras=()):
    n = 0 if parts is None else parts.shape[0]
    R, C = extras[0].shape if parts is None else parts.shape[1:]
    tr = _row_tile(R, C * (n + len(extras)))
    ins = ([] if parts is None else [parts]) + list(extras)

    def body(*refs):
        terms = [] if parts is None else [refs[0][i] for i in range(n)]
        terms += [r[...] for r in refs[len(ins) - len(extras):len(ins)]]
        s = terms[0].astype(F32)
        for t in terms[1:]:
            s = s + t.astype(F32)
        refs[len(ins)][...] = s.astype(dtype)

    tile = pl.BlockSpec((tr, C), lambda t: (t, 0))
    return pl.pallas_call(
        body, name=name, grid=(R // tr,),
        in_specs=([] if parts is None else [pl.BlockSpec((n, tr, C), lambda t: (0, t, 0))]) + [tile] * len(extras),
        out_specs=tile, out_shape=jax.ShapeDtypeStruct((R, C), dtype), compiler_params=_cp(("parallel",)),
    )(*ins)


def adam_step(name, ga, gb, w, m, v):
    R, C = w.shape
    tr = _row_tile(R, C, 1 << 17)
    ins = [ga] + ([gb] if gb is not None else []) + [w, m, v]

    def body(*refs):
        g = refs[0][...]
        if gb is not None:
            g = g + refs[1][...]
        w_ref, m_ref, v_ref, g_out, d_out, m_out, v_out = refs[len(ins) - 3:]
        d, m2, v2 = _adam(w_ref[...], g, m_ref[...], v_ref[...])
        g_out[...] = g
        d_out[...] = d
        m_out[...] = m2
        v_out[...] = v2

    tile = pl.BlockSpec((tr, C), lambda t: (t, 0))
    return pl.pallas_call(
        body, name=name, grid=(R // tr,), in_specs=[tile] * len(ins), out_specs=[tile] * 4,
        out_shape=[jax.ShapeDtypeStruct((R, C), F32)] * 4, compiler_params=_cp(("parallel",)),
    )(*ins)


def _place():
    return lax.axis_index("x"), lax.axis_index("y"), lax.axis_index("c")


def _flip(me, mask):
    return tuple(1 - v if mk else v for v, mk in zip(me, mask))


CHIP_MASKS = ((1, 0, 0), (0, 1, 0), (1, 1, 0))
ALL_MASKS = tuple((a, b, c) for a in (0, 1) for b in (0, 1) for c in (0, 1) if (a, b, c) != (0, 0, 0))


def _chip(dev):
    return 2 * dev[0] + dev[1]


def _devno(dev):
    return 4 * dev[0] + 2 * dev[1] + dev[2]


def exchange(name, arrays, out_shapes, masks, copies, src_of, dst_of, local_of, alias=False):
    n, npeer = len(arrays), len(masks)
    nloc = 1

    def body(*refs):
        ins, outs = refs[:n], refs[n:2 * n]
        send_sems, recv_sems, local_sems = refs[2 * n:]
        me = _place()
        peers = [_flip(me, mk) for mk in masks]
        locals_ = []
        for i in range(n):
            for q, (src, dst) in enumerate(local_of(ins[i], outs[i], me)):
                cp = pltpu.make_async_copy(src, dst, local_sems.at[i * nloc + q])
                cp.start()
                locals_.append(cp)
        sends = []
        for i in range(n):
            for j, peer in enumerate(peers):
                srcs, dsts = src_of(ins[i], me, j), dst_of(outs[i], me, j)
                for q in range(copies):
                    sem = (i * npeer + j) * copies + q
                    cp = pltpu.make_async_remote_copy(
                        src_ref=srcs[q], dst_ref=dsts[q], send_sem=send_sems.at[sem], recv_sem=recv_sems.at[sem],
                        device_id=peer, device_id_type=MESH)
                    cp.start()
                    sends.append(cp)
        for i in range(n):
            for j, peer in enumerate(peers):
                lands = dst_of(outs[i], peer, j)
                for q in range(copies):
                    sem = (i * npeer + j) * copies + q
                    pltpu.make_async_remote_copy(
                        src_ref=lands[q], dst_ref=lands[q], send_sem=send_sems.at[sem], recv_sem=recv_sems.at[sem],
                        device_id=peer, device_id_type=MESH).wait_recv()
        for cp in sends:
            cp.wait_send()
        for cp in locals_:
            cp.wait()

    hbm = pl.BlockSpec(memory_space=pl.ANY)
    return pl.pallas_call(
        body, name=name, in_specs=[hbm] * n, out_specs=[hbm] * n, out_shape=list(out_shapes),
        scratch_shapes=[pltpu.SemaphoreType.DMA((n * npeer * copies,)), pltpu.SemaphoreType.DMA((n * npeer * copies,)),
                        pltpu.SemaphoreType.DMA((n * nloc,))],
        input_output_aliases={i: i for i in range(n)} if alias else {},
    )(*arrays)


def _half(c, rows):
    return pl.ds(c * (rows // 2), rows // 2)


def gather_chips(arrays):
    outs = [jax.ShapeDtypeStruct((N_CHIPS,) + a.shape, a.dtype) for a in arrays]
    sib = len(CHIP_MASKS)
    got = exchange("gather_weights", arrays, outs, CHIP_MASKS + ((0, 0, 1),), 1,
                   src_of=lambda r, me, j: [r] if j == sib else [r.at[_half(me[2], r.shape[0])]],
                   dst_of=lambda o, sender, j: [o.at[_chip(sender)]] if j == sib else
                   [o.at[_chip(sender), _half(sender[2], o.shape[1])]],
                   local_of=lambda r, o, me: [])
    return exchange("gather_swap", got, outs, ((0, 0, 1),), len(CHIP_MASKS),
                    src_of=lambda r, me, j: [r.at[_chip(_flip(me, mk)), _half(me[2], r.shape[1])] for mk in CHIP_MASKS],
                    dst_of=lambda o, sender, j: [o.at[_chip(_flip(sender, mk)), _half(sender[2], o.shape[1])]
                                                 for mk in CHIP_MASKS],
                    local_of=lambda r, o, me: [], alias=True)


def reduce_chips(names, arrays, wire):
    x, y, c = _place()
    split = [a.reshape(N_CHIPS, 2, a.shape[1] // 2, a.shape[2]) for a in arrays]
    half_shapes = [jax.ShapeDtypeStruct((N_CHIPS,) + a.shape[2:], F32) for a in split]
    theirs = exchange("grad_pre_swap", split, half_shapes, ((0, 0, 1),), 1,
                      src_of=lambda r, me, j: [r.at[:, 1 - me[2]]], dst_of=lambda o, sender, j: [o],
                      local_of=lambda r, o, me: [])
    chip_sum = []
    for nm, a, t, dt in zip(names, split, theirs, wire):
        own = lax.dynamic_index_in_dim(a, c, axis=1, keepdims=False)
        flat = lambda v: v.reshape(-1, v.shape[-1])
        chip_sum.append(sum_slots(f"sum2_{nm}", None, dt, [flat(own), flat(t)]).reshape(t.shape))
    landed = exchange("scatter_grads", chip_sum,
                      [jax.ShapeDtypeStruct((len(CHIP_MASKS),) + a.shape[1:], a.dtype) for a in chip_sum], CHIP_MASKS, 1,
                      src_of=lambda r, me, j: [r.at[_chip(_flip(me, CHIP_MASKS[j]))]],
                      dst_of=lambda o, sender, j: [o.at[j]], local_of=lambda r, o, me: [])
    halves = [sum_slots(f"sum4_{nm}", p, F32, [lax.dynamic_index_in_dim(a, _chip((x, y, c)), axis=0, keepdims=False)])
              for nm, p, a in zip(names, landed, chip_sum)]
    others = exchange("grad_final_swap", halves, [jax.ShapeDtypeStruct(a.shape, F32) for a in halves], ((0, 0, 1),), 1,
                      src_of=lambda r, me, j: [r], dst_of=lambda o, sender, j: [o], local_of=lambda r, o, me: [])
    return [jnp.concatenate([jnp.where(c == 0, h, o), jnp.where(c == 0, o, h)], axis=0) for h, o in zip(halves, others)]


def gather_all(arrays):
    outs = [jax.ShapeDtypeStruct((8,) + a.shape, a.dtype) for a in arrays]
    return exchange("gather_replicated", arrays, outs, ALL_MASKS, 1,
                    src_of=lambda r, me, peer: [r],
                    dst_of=lambda o, sender, j: [o.at[_devno(sender)]],
                    local_of=lambda r, o, me: [(r, o.at[_devno(me)])])


def _unshard_cols(g):
    return jnp.transpose(g, (1, 0, 2)).reshape(g.shape[1], -1)


def _shard_cols(a):
    return jnp.transpose(a.reshape(a.shape[0], N_CHIPS, -1), (1, 0, 2))


def _forward_backward(x, tgt, W):
    S = x.shape[0]
    G = {}
    sd = jax.ShapeDtypeStruct

    def ffn(xin, l, j):
        return ffn_fwd(xin, W["ffn_norm"][l][j], W["ffn_w_gate"], W["ffn_w_up"], W["ffn_w_down"], l, j)

    ffn_dw = [tuple(lax.empty(W[n].shape, F32) for n in ("ffn_w_gate", "ffn_w_up", "ffn_w_down"))]

    def ffn_back(xin, dout, l, j):
        gn = W["ffn_norm"][l][j]
        dh, *dws = ffn_bwd(xin, gn, W["ffn_w_gate"], W["ffn_w_up"], W["ffn_w_down"], dout, ffn_dw[0], l, j)
        ffn_dw[0] = tuple(dws)
        dx, G[("ffn_norm", l, j)] = norm_bwd(f"ffn_norm_bwd_{l}{j}", xin, gn, dh, dout)
        return dx

    x0 = x
    x1 = ffn(x0, 0, 0)
    g0 = W["mix_norm"][0]
    sbq, sbk, sbv = tile_fwd(f_attn_sb, "attn_in_sb", [x1], [g0, W["attn_w_in"][0]], [sd((S, SB_W), F32)] * 3, 256)
    dl_shape = sd((DL_PAIRS, S, 128), F32)
    qn, = tile_fwd(f_attn_qk, "attn_in_q", [x1], [g0, W["attn_w_in"][1], W["attn_q_norm"]], [dl_shape], 256)
    kn, = tile_fwd(f_attn_qk, "attn_in_k", [x1], [g0, W["attn_w_in"][2], W["attn_k_norm"]], [dl_shape], 256)
    vv, = tile_fwd(f_attn_v, "attn_in_v", [x1], [g0, W["attn_w_in"][3]], [dl_shape], 256)
    oa = sb_fwd(sbq, sbk, sbv)
    qs, ks, vs = (reorder(nm, t, DIL, False) for nm, t in (("sub_q", qn), ("sub_k", kn), ("sub_v", vv)))
    o_s, lse_s = dil_fwd(qs, ks, vs, W["bias_mat"])
    o_n, lse_n = reorder("nat_o", o_s, DIL, True), reorder("nat_lse", lse_s, DIL, True)
    x2, = tile_fwd(f_attn_out, "attn_out", [x1, oa, o_n, lse_n], [W["attn_w_out"]], [sd((S, D), F32)], 256)
    x3 = ffn(x2, 0, 1)
    x4 = ffn(x3, 1, 0)
    g1 = W["mix_norm"][1]
    h, hs = norm_shift_fwd(x4, g1)
    mix = W["rw_mix"]
    r, = tile_fwd(f_rw_proj, "rw_proj_r", [h, hs], [mix[0:1], W["rw_wr"]], [sd((S, D), F32)], 256)
    k, = tile_fwd(f_rw_proj, "rw_proj_k", [h, hs], [mix[2:3], W["rw_wk"]], [sd((S, D), F32)], 256)
    v, = tile_fwd(f_rw_proj, "rw_proj_v", [h, hs], [mix[3:4], W["rw_wv"]], [sd((S, D), F32)], 256)
    mix3 = jnp.concatenate([mix[1:2], mix[4:5], mix[5:6]], axis=0)
    mid_w = [mix3, W["rw_w0"], W["rw_a0"], W["rw_kk"], W["rw_ka"], W["rw_w1"], W["rw_w2"], W["rw_a1"], W["rw_a2"],
             W["rw_g1"], W["rw_g2"]]
    hshape = sd((RW_H, S, HEAD), F32)
    mid_tiles = [h, hs, r, k, v]
    rh, lwh, kh, vh, ah, bh, gate = tile_fwd(f_rw_mid, "rw_mid", mid_tiles, mid_w, [hshape] * 6 + [sd((S, D), F32)], 128)
    yh, states = rwkv_fwd(rh, lwh, kh, vh, ah, bh)
    post_w = [W["rw_lnx_g"], W["rw_lnx_b"], W["rw_rk"], W["rw_wo"]]
    post_tiles = [yh, rh, kh, vh, gate, x4]
    x5, = tile_fwd(f_rw_post, "rw_post", post_tiles, post_w, [sd((S, D), F32)], 128)
    x6 = ffn(x5, 1, 1)
    dx6, loss_part = loss_head(x6, tgt)

    dx5 = ffn_back(x5, dx6, 1, 1)
    (dyh, drh, dkh, dvh, dgate, dx4), (d_lng, d_lnb, d_rk, d_wo) = tile_bwd(
        f_rw_post, "rw_post_bwd", post_tiles, post_w, [dx5], 128, [True] * 6, [True] * 4)
    drh2, dlwh, dkh2, dvh2, dah, dbh = rwkv_bwd(rh, lwh, kh, vh, ah, bh, states, dyh)
    mid_cts = [(drh, drh2), dlwh, (dkh, dkh2), (dvh, dvh2), dah, dbh, dgate]
    (dh, dhs, dr, dk, dv), dmid_w = tile_bwd(f_rw_mid, "rw_mid_bwd", mid_tiles, mid_w, mid_cts, 128,
                                             [True] * 5, [True] * len(mid_w))
    dmix = {}
    for nm, ct, row, wname in (("r", dr, 0, "rw_wr"), ("k", dk, 2, "rw_wk"), ("v", dv, 3, "rw_wv")):
        (dh, dhs), (dmix[row], G[wname]) = tile_bwd(
            f_rw_proj, f"rw_proj_{nm}_bwd", [h, hs], [mix[row:row + 1], W[wname]], [ct], 256,
            [True, True], [True, True], acc={0: dh, 1: dhs})
    dx4, G[("mix_norm", 1)] = norm_shift_bwd(x4, g1, dh, dhs, dx4)
    dmix3 = dmid_w[0]
    G["rw_mix"] = jnp.concatenate([dmix[0], dmix3[0:1], dmix[2], dmix[3], dmix3[1:2], dmix3[2:3]], axis=0)
    for nm, gv in zip(("rw_w0", "rw_a0", "rw_kk", "rw_ka", "rw_w1", "rw_w2", "rw_a1", "rw_a2", "rw_g1", "rw_g2"), dmid_w[1:]):
        G[nm] = gv
    G["rw_lnx_g"], G["rw_lnx_b"], G["rw_rk"], G["rw_wo"] = d_lng, d_lnb, d_rk, d_wo
    dx3 = ffn_back(x3, dx4, 1, 0)
    dx2 = ffn_back(x2, dx3, 0, 1)
    (dx1, doa, do_n, dlse_n), (G["attn_w_out"],) = tile_bwd(
        f_attn_out, "attn_out_bwd", [x1, oa, o_n, lse_n], [W["attn_w_out"]], [dx2], 256, [True] * 4, [True])
    do_s, dlse_s = reorder("sub_do", do_n, DIL, False), reorder("sub_dlse", dlse_n, DIL, False)
    dqs, dks, dvs, dsum = dil_bwd(qs, ks, vs, W["bias_mat"], o_s, lse_s, do_s, dlse_s)
    G["rel_bias"] = bias_grad(dsum, W["buckets"])
    dqn, dkn, dvv = (reorder(nm, t, DIL, True) for nm, t in (("nat_dq", dqs), ("nat_dk", dks), ("nat_dv", dvs)))
    dsbq, dsbk, dsbv = sb_bwd(sbq, sbk, sbv, doa)
    dg0 = []
    dwin = []
    (dx1,), (dg, dw) = tile_bwd(f_attn_sb, "attn_in_sb_bwd", [x1], [g0, W["attn_w_in"][0]], [dsbq, dsbk, dsbv], 256,
                                [True], [True, True], acc={0: dx1})
    dg0.append(dg), dwin.append(dw)
    (dx1,), (dg, dw, G["attn_q_norm"]) = tile_bwd(f_attn_qk, "attn_in_q_bwd", [x1], [g0, W["attn_w_in"][1], W["attn_q_norm"]],
                                                  [dqn], 256, [True], [True] * 3, acc={0: dx1})
    dg0.append(dg), dwin.append(dw)
    (dx1,), (dg, dw, G["attn_k_norm"]) = tile_bwd(f_attn_qk, "attn_in_k_bwd", [x1], [g0, W["attn_w_in"][2], W["attn_k_norm"]],
                                                  [dkn], 256, [True], [True] * 3, acc={0: dx1})
    dg0.append(dg), dwin.append(dw)
    (dx1,), (dg, dw) = tile_bwd(f_attn_v, "attn_in_v_bwd", [x1], [g0, W["attn_w_in"][3]], [dvv], 256,
                                [True], [True, True], acc={0: dx1})
    dg0.append(dg), dwin.append(dw)
    G[("mix_norm", 0)] = dg0
    G["attn_w_in"] = dwin
    dx0 = ffn_back(x0, dx1, 0, 0)
    G["ffn_w_gate"], G["ffn_w_up"], G["ffn_w_down"] = ffn_dw[0]
    return loss_part, dx0, G


VEC_ROWS = ("ffn_norm", "rw_mix", "rw_w0", "rw_a0", "rw_kk", "rw_ka", "rw_lnx_g", "rw_lnx_b")


def kernel(x, ffn_norm, ffn_w_gate, ffn_w_up, ffn_w_down, mix_norm, rel_bias, attn_w_in, attn_q_norm, attn_k_norm, attn_w_out, rw_mix, rw_w0, rw_w1, rw_w2, rw_a0, rw_a1, rw_a2, rw_g1, rw_g2, rw_kk, rw_ka, rw_rk, rw_wr, rw_wk, rw_wv, rw_wo, rw_lnx_g, rw_lnx_b, loss_target, m_ffn_norm, m_ffn_w_gate, m_ffn_w_up, m_ffn_w_down, m_mix_norm, m_rel_bias, m_attn_w_in, m_attn_q_norm, m_attn_k_norm, m_attn_w_out, m_rw_mix, m_rw_w0, m_rw_w1, m_rw_w2, m_rw_a0, m_rw_a1, m_rw_a2, m_rw_g1, m_rw_g2, m_rw_kk, m_rw_ka, m_rw_rk, m_rw_wr, m_rw_wk, m_rw_wv, m_rw_wo, m_rw_lnx_g, m_rw_lnx_b, v_ffn_norm, v_ffn_w_gate, v_ffn_w_up, v_ffn_w_down, v_mix_norm, v_rel_bias, v_attn_w_in, v_attn_q_norm, v_attn_k_norm, v_attn_w_out, v_rw_mix, v_rw_w0, v_rw_w1, v_rw_w2, v_rw_a0, v_rw_a1, v_rw_a2, v_rw_g1, v_rw_g2, v_rw_kk, v_rw_ka, v_rw_rk, v_rw_wr, v_rw_wk, v_rw_wv, v_rw_wo, v_rw_lnx_g, v_rw_lnx_b):
    names = ["ffn_norm", "ffn_w_gate", "ffn_w_up", "ffn_w_down", "mix_norm", "rel_bias", "attn_w_in", "attn_q_norm",
             "attn_k_norm", "attn_w_out", "rw_mix", "rw_w0", "rw_w1", "rw_w2", "rw_a0", "rw_a1", "rw_a2", "rw_g1", "rw_g2",
             "rw_kk", "rw_ka", "rw_rk", "rw_wr", "rw_wk", "rw_wv", "rw_wo", "rw_lnx_g", "rw_lnx_b"]
    loc = locals()
    w = {n: loc[n] for n in names}
    mom = {n: loc["m_" + n] for n in names}
    vel = {n: loc["v_" + n] for n in names}
    S = x.shape[1]

    vec_shard = jnp.concatenate([w[n].reshape(-1, 256) for n in VEC_ROWS], axis=0)
    mats = ["ffn_w_gate", "ffn_w_up", "ffn_w_down", "attn_w_in", "attn_w_out", "rw_w1", "rw_w2", "rw_a1", "rw_a2",
            "rw_g1", "rw_g2", "rw_wr", "rw_wk", "rw_wv", "rw_wo"]
    send = [vec_shard] + [w[n].reshape(-1, w[n].shape[-1]).astype(BF16) for n in mats]
    got = gather_chips(send)
    vec_full = _unshard_cols(got[0])
    gm = dict(zip(mats, got[1:]))
    W = {
        "ffn_norm": [[vec_full[2 * l + j][None] for j in range(2)] for l in range(2)],
        "ffn_w_gate": gm["ffn_w_gate"].reshape(N_CHIPS, 2, 2, D, FF_SHARD),
        "ffn_w_up": gm["ffn_w_up"].reshape(N_CHIPS, 2, 2, D, FF_SHARD),
        "ffn_w_down": gm["ffn_w_down"].reshape(N_CHIPS, 2, 2, FF_SHARD, D),
        "mix_norm": [mix_norm[0:1], mix_norm[1:2]],
        "attn_w_in": [gm["attn_w_in"][p] for p in range(N_CHIPS)],
        "attn_q_norm": attn_q_norm, "attn_k_norm": attn_k_norm,
        "attn_w_out": _unshard_cols(gm["attn_w_out"]),
        "rw_mix": vec_full[4:10],
        "rw_w1": gm["rw_w1"].reshape(D, -1), "rw_a1": gm["rw_a1"].reshape(D, -1), "rw_g1": gm["rw_g1"].reshape(D, -1),
        "rw_w2": _unshard_cols(gm["rw_w2"]), "rw_a2": _unshard_cols(gm["rw_a2"]), "rw_g2": _unshard_cols(gm["rw_g2"]),
        "rw_wr": gm["rw_wr"].reshape(D, D), "rw_wk": gm["rw_wk"].reshape(D, D), "rw_wv": gm["rw_wv"].reshape(D, D),
        "rw_wo": gm["rw_wo"].reshape(D, D),
        "rw_rk": rw_rk[0][:, None, :],
    }
    for i, n in enumerate(("rw_w0", "rw_a0", "rw_kk", "rw_ka", "rw_lnx_g", "rw_lnx_b")):
        W[n] = vec_full[10 + i][None]
    buckets = _bucket_maps()
    W["buckets"] = buckets
    W["bias_mat"] = bias_table(rel_bias, buckets)

    loss_part, dx, G = _forward_backward(x[0], loss_target[0], W)
    loss = lax.psum(loss_part[0, 0], ("x", "y", "c"))

    vec_rows = [G[("ffn_norm", l, j)] for l in range(2) for j in range(2)] + [G["rw_mix"]] + \
               [G[n] for n in ("rw_w0", "rw_a0", "rw_kk", "rw_ka", "rw_lnx_g", "rw_lnx_b")]
    full = {
        "vec": _shard_cols(jnp.concatenate(vec_rows, axis=0)),
        "ffn_w_gate": G["ffn_w_gate"], "ffn_w_up": G["ffn_w_up"], "ffn_w_down": G["ffn_w_down"],
        "attn_w_in": jnp.stack(G["attn_w_in"]),
        "attn_w_out": _shard_cols(G["attn_w_out"]),
        "rw_w1": G["rw_w1"].reshape(N_CHIPS, 256, -1), "rw_a1": G["rw_a1"].reshape(N_CHIPS, 256, -1),
        "rw_g1": G["rw_g1"].reshape(N_CHIPS, 256, -1),
        "rw_w2": _shard_cols(G["rw_w2"]), "rw_a2": _shard_cols(G["rw_a2"]), "rw_g2": _shard_cols(G["rw_g2"]),
        "rw_wr": G["rw_wr"].reshape(N_CHIPS, 256, D), "rw_wk": G["rw_wk"].reshape(N_CHIPS, 256, D),
        "rw_wv": G["rw_wv"].reshape(N_CHIPS, 256, D), "rw_wo": G["rw_wo"].reshape(N_CHIPS, 256, D),
    }
    order = ["vec"] + mats
    summed = reduce_chips(order, [full[n].reshape(N_CHIPS, -1, full[n].shape[-1]) for n in order],
                          [F32] + [BF16] * len(mats))

    rep = jnp.concatenate([G[("mix_norm", 0)][0] + G[("mix_norm", 0)][1] + G[("mix_norm", 0)][2] + G[("mix_norm", 0)][3],
                           G[("mix_norm", 1)]], axis=0).reshape(16, 128)
    rep = jnp.concatenate([rep, G["rel_bias"], jnp.pad(G["attn_q_norm"], ((0, 0), (0, 64))),
                           jnp.pad(G["attn_k_norm"], ((0, 0), (0, 64))), G["rw_rk"].reshape(8, 128),
                           jnp.zeros((2, 128), F32)], axis=0)
    rep_sum = sum_slots("sum_replicated", gather_all([rep])[0])
    g_rep = {
        "mix_norm": rep_sum[0:16].reshape(2, D),
        "rel_bias": jnp.transpose(rep_sum[16:28, :N_BUCKETS]),
        "attn_q_norm": rep_sum[28:29, :HEAD], "attn_k_norm": rep_sum[29:30, :HEAD],
        "rw_rk": rep_sum[30:38].reshape(1, RW_H, HEAD),
    }

    out = {}

    def adam(n, ga, gb):
        shp = w[n].shape
        to2 = lambda a: a.reshape(-1, shp[-1])
        res = adam_step(f"adam_{n}", to2(ga), None if gb is None else to2(gb), to2(w[n]), to2(mom[n]), to2(vel[n]))
        out[n] = tuple(r.reshape(shp) for r in res)

    part = dict(zip(order, summed))
    for n in mats:
        adam(n, part[n], None)
    rows = {"ffn_norm": (0, 4), "rw_mix": (4, 10), "rw_w0": (10, 11), "rw_a0": (11, 12), "rw_kk": (12, 13),
            "rw_ka": (13, 14), "rw_lnx_g": (14, 15), "rw_lnx_b": (15, 16)}
    for n, (lo, hi) in rows.items():
        adam(n, part["vec"][lo:hi], None)
    for n, gv in g_rep.items():
        adam(n, gv, None)

    grads = [out[n][0] for n in names]
    deltas = [out[n][1] for n in names]
    new_m = [out[n][2] for n in names]
    new_v = [out[n][3] for n in names]
    return (loss, dx[None], *grads, *deltas, *new_m, *new_v)
```

```python
import functools
import math

import jax
import jax.numpy as jnp
from jax import lax
from jax.experimental import pallas as pl
from jax.experimental.pallas import tpu as pltpu

F32, BF16 = jnp.float32, jnp.bfloat16
HI = lax.Precision.HIGHEST
MESH = pl.DeviceIdType.MESH

D = 1024
HEAD = 64
N_CHIPS = 4
FF_SHARD = 704
SB_W = 256
DL_HEADS = 12
DL_PAIRS = 6
DIL = (1, 4, 16)
QBLK = 128
N_BUCKETS = 32
MAX_DISTANCE = 2048
RW_H = 16
RW_CHUNK = 64
NORM_EPS = 1e-6
GN_EPS = 64e-5
NEG_INF = -1e30
VMEM_LIMIT = 56 * 1024 * 1024

ADAM_LR, ADAM_B1, ADAM_B2, ADAM_EPS, ADAM_WD, ADAM_STEP = 0.001, 0.9, 0.999, 1e-08, 0.01, 10


def _cp(sem):
    return pltpu.CompilerParams(dimension_semantics=sem, vmem_limit_bytes=VMEM_LIMIT)


def _dg(a, b, dims, prec=None):
    return lax.dot_general(a, b, (dims, ((), ())), precision=prec, preferred_element_type=F32)


def _bdot(a, b, dims):
    return _dg(a.astype(BF16), b.astype(BF16), dims)


@jax.custom_vjp
def mm(a, b):
    return _bdot(a, b, ((1,), (0,)))


def _mm_fwd(a, b):
    return _bdot(a, b, ((1,), (0,))), (a, b)


def _mm_bwd(res, g):
    a, b = res
    return _bdot(g, b, ((1,), (1,))), _bdot(a, g, ((0,), (0,)))


mm.defvjp(_mm_fwd, _mm_bwd)


def rms(x, g):
    return x * lax.rsqrt(jnp.mean(x * x, axis=-1, keepdims=True) + NORM_EPS) * g


def group_sum(x, nh):
    w = x.shape[-1]
    e = (lax.broadcasted_iota(jnp.int32, (w, nh), 0) // HEAD == lax.broadcasted_iota(jnp.int32, (w, nh), 1)).astype(F32)
    s = _dg(x, e, ((1,), (0,)), HI)
    return _dg(s, e, ((1,), (1,)), HI)


def softplus(u):
    return jnp.maximum(u, 0.0) + jnp.log1p(jnp.exp(-jnp.abs(u)))


def to_heads(t, nh=RW_H):
    return jnp.stack([t[:, HEAD * h:HEAD * (h + 1)] for h in range(nh)])


def from_heads(t):
    return jnp.concatenate([t[h] for h in range(t.shape[0])], axis=-1)


def _tile_spec(shape, tm):
    if len(shape) == 2:
        return pl.BlockSpec((tm, shape[1]), lambda t: (t, 0))
    return pl.BlockSpec((shape[0], tm, shape[2]), lambda t: (0, t, 0))


def _full_spec(shape):
    nd = len(shape)
    return pl.BlockSpec(tuple(shape), lambda t: (0,) * nd)


def _rows(a):
    return a.shape[0] if a.ndim == 2 else a.shape[1]


def tile_fwd(f, name, tiles, weights, outs, tm):
    nt, nw = len(tiles), len(weights)

    def body(*refs):
        tv = [r[...] for r in refs[:nt]]
        wv = [r[...].astype(F32) for r in refs[nt:nt + nw]]
        res = f(*tv, *wv)
        if not isinstance(res, (tuple, list)):
            res = (res,)
        for o, v in zip(refs[nt + nw:], res):
            o[...] = v.astype(o.dtype)

    return pl.pallas_call(
        body, name=name, grid=(_rows(tiles[0]) // tm,),
        in_specs=[_tile_spec(a.shape, tm) for a in tiles] + [_full_spec(w.shape) for w in weights],
        out_specs=[_tile_spec(o.shape, tm) for o in outs],
        out_shape=list(outs),
        compiler_params=_cp(("parallel",)),
    )(*tiles, *weights)


def tile_bwd(f, name, tiles, weights, cts, tm, dt, dw, acc=None):
    acc = acc or {}
    groups = [c if isinstance(c, tuple) else (c,) for c in cts]
    cts = [a for grp in groups for a in grp]
    nt, nw, nc = len(tiles), len(weights), len(cts)
    acc_idx = sorted(acc)
    na = len(acc_idx)
    dti = [i for i in range(nt) if dt[i]]
    dwi = [i for i in range(nw) if dw[i]]

    def body(*refs):
        tv = [r[...] for r in refs[:nt]]
        wv = [r[...].astype(F32) for r in refs[nt:nt + nw]]
        crefs = list(refs[nt + nw:nt + nw + nc])
        cv = []
        for grp in groups:
            terms = [crefs.pop(0)[...] for _ in grp]
            cv.append(functools.reduce(lambda a, b: a + b, terms))
        av = {i: r[...] for i, r in zip(acc_idx, refs[nt + nw + nc:nt + nw + nc + na])}
        orefs = refs[nt + nw + nc + na:]

        def g(*diff):
            t2, w2 = list(tv), list(wv)
            for i, v in zip(dti, diff[:len(dti)]):
                t2[i] = v
            for i, v in zip(dwi, diff[len(dti):]):
                w2[i] = v
            res = f(*t2, *w2)
            return tuple(res) if isinstance(res, (tuple, list)) else (res,)

        _, vjp = jax.vjp(g, *[tv[i] for i in dti], *[wv[i] for i in dwi])
        grads = vjp(tuple(cv))
        for k, i in enumerate(dti):
            gt = grads[k]
            if i in av:
                gt = gt + av[i]
            orefs[k][...] = gt
        first = pl.program_id(0) == 0
        for k, i in enumerate(dwi):
            o = orefs[len(dti) + k]
            gw = grads[len(dti) + k]

            @pl.when(first)
            def _(o=o, gw=gw):
                o[...] = gw

            @pl.when(jnp.logical_not(first))
            def _(o=o, gw=gw):
                o[...] += gw

    out_shape = [jax.ShapeDtypeStruct(tiles[i].shape, F32) for i in dti] + \
                [jax.ShapeDtypeStruct(weights[i].shape, F32) for i in dwi]
    res = pl.pallas_call(
        body, name=name, grid=(_rows(tiles[0]) // tm,),
        in_specs=[_tile_spec(a.shape, tm) for a in tiles] + [_full_spec(w.shape) for w in weights] +
                 [_tile_spec(c.shape, tm) for c in cts] + [_tile_spec(tiles[i].shape, tm) for i in acc_idx],
        out_specs=[_tile_spec(tiles[i].shape, tm) for i in dti] + [_full_spec(weights[i].shape) for i in dwi],
        out_shape=out_shape,
        compiler_params=_cp(("arbitrary",)),
    )(*tiles, *weights, *cts, *[acc[i] for i in acc_idx])
    return list(res[:len(dti)]), list(res[len(dti):])


def _ffn_wspec(rows, cols, cfirst):
    if cfirst:
        return pl.BlockSpec((1, rows, cols), lambda c, t: (c, 0, 0))
    return pl.BlockSpec((1, rows, cols), lambda t, c: (c, 0, 0))


def ffn_fwd(x, g, wg, wu, wd, l, j, tm=512):
    S = x.shape[0]

    def body(x_ref, g_ref, wg_ref, wu_ref, wd_ref, o_ref, h_ref, acc_ref):
        c = pl.program_id(1)

        @pl.when(c == 0)
        def _():
            h_ref[...] = rms(x_ref[...], g_ref[...]).astype(BF16)
            acc_ref[...] = jnp.zeros_like(acc_ref)

        h = h_ref[...]
        a = _bdot(h, wg_ref[0], ((1,), (0,)))
        b = _bdot(h, wu_ref[0], ((1,), (0,)))
        y = a * jax.nn.sigmoid(a) * b
        acc_ref[...] += _bdot(y, wd_ref[0], ((1,), (0,)))

        @pl.when(c == N_CHIPS - 1)
        def _():
            o_ref[...] = x_ref[...] + 0.5 * acc_ref[...]

    return pl.pallas_call(
        body, name=f"ffn_fwd_{l}{j}", grid=(S // tm, N_CHIPS),
        in_specs=[pl.BlockSpec((tm, D), lambda t, c: (t, 0)), pl.BlockSpec((1, D), lambda t, c: (0, 0)),
                  _ffn_wspec(D, FF_SHARD, False), _ffn_wspec(D, FF_SHARD, False), _ffn_wspec(FF_SHARD, D, False)],
        out_specs=pl.BlockSpec((tm, D), lambda t, c: (t, 0)),
        out_shape=jax.ShapeDtypeStruct((S, D), F32),
        scratch_shapes=[pltpu.VMEM((tm, D), BF16), pltpu.VMEM((tm, D), F32)],
        compiler_params=_cp(("parallel", "arbitrary")),
    )(x, g, wg, wu, wd)


def ffn_bwd(x, g, wg, wu, wd, dout, l, j, tm=512):
    S = x.shape[0]

    def body(x_ref, g_ref, wg_ref, wu_ref, wd_ref, do_ref, dh_ref, dwg_ref, dwu_ref, dwd_ref):
        t = pl.program_id(1)
        h = rms(x_ref[...], g_ref[...]).astype(BF16)
        wgv, wuv, wdv = wg_ref[0], wu_ref[0], wd_ref[0]
        a = _bdot(h, wgv, ((1,), (0,)))
        b = _bdot(h, wuv, ((1,), (0,)))
        sig = jax.nn.sigmoid(a)
        s = a * sig
        dyd = 0.5 * do_ref[...]
        dy = _bdot(dyd, wdv, ((1,), (1,)))
        dwd = _bdot(s * b, dyd, ((0,), (0,)))
        db = dy * s
        da = dy * b * (sig * (1.0 + a * (1.0 - sig)))
        dwg = _bdot(h, da, ((0,), (0,)))
        dwu = _bdot(h, db, ((0,), (0,)))
        dh_ref[0] = _bdot(da, wgv, ((1,), (1,))) + _bdot(db, wuv, ((1,), (1,)))

        @pl.when(t == 0)
        def _():
            dwg_ref[0] = dwg
            dwu_ref[0] = dwu
            dwd_ref[0] = dwd

        @pl.when(t != 0)
        def _():
            dwg_ref[0] += dwg
            dwu_ref[0] += dwu
            dwd_ref[0] += dwd

    return pl.pallas_call(
        body, name=f"ffn_bwd_{l}{j}", grid=(N_CHIPS, S // tm),
        in_specs=[pl.BlockSpec((tm, D), lambda c, t: (t, 0)), pl.BlockSpec((1, D), lambda c, t: (0, 0)),
                  _ffn_wspec(D, FF_SHARD, True), _ffn_wspec(D, FF_SHARD, True), _ffn_wspec(FF_SHARD, D, True),
                  pl.BlockSpec((tm, D), lambda c, t: (t, 0))],
        out_specs=[pl.BlockSpec((1, tm, D), lambda c, t: (c, t, 0)),
                   _ffn_wspec(D, FF_SHARD, True), _ffn_wspec(D, FF_SHARD, True), _ffn_wspec(FF_SHARD, D, True)],
        out_shape=[jax.ShapeDtypeStruct((N_CHIPS, S, D), F32)] + [jax.ShapeDtypeStruct(a.shape, F32) for a in (wg, wu, wd)],
        compiler_params=_cp(("parallel", "arbitrary")),
    )(x, g, wg, wu, wd, dout)


def norm_bwd(name, x, g, dh_parts, dres, tm=256):
    S = x.shape[0]
    P = dh_parts.shape[0]

    def body(x_ref, g_ref, dh_ref, dr_ref, dx_ref, dg_ref):
        dh = dh_ref[0]
        for p in range(1, P):
            dh = dh + dh_ref[p]
        _, vjp = jax.vjp(rms, x_ref[...], g_ref[...])
        dx, dg = vjp(dh)
        dx_ref[...] = dr_ref[...] + dx

        @pl.when(pl.program_id(0) == 0)
        def _():
            dg_ref[...] = dg

        @pl.when(pl.program_id(0) != 0)
        def _():
            dg_ref[...] += dg

    return pl.pallas_call(
        body, name=name, grid=(S // tm,),
        in_specs=[pl.BlockSpec((tm, D), lambda t: (t, 0)), pl.BlockSpec((1, D), lambda t: (0, 0)),
                  pl.BlockSpec((P, tm, D), lambda t: (0, t, 0)), pl.BlockSpec((tm, D), lambda t: (t, 0))],
        out_specs=[pl.BlockSpec((tm, D), lambda t: (t, 0)), pl.BlockSpec((1, D), lambda t: (0, 0))],
        out_shape=[jax.ShapeDtypeStruct((S, D), F32), jax.ShapeDtypeStruct((1, D), F32)],
        compiler_params=_cp(("arbitrary",)),
    )(x, g, dh_parts, dres)


def f_attn_sb(x, g, w):
    pr = mm(rms(x, g), w)
    return pr[:, :SB_W], pr[:, SB_W:2 * SB_W], pr[:, 2 * SB_W:]


def _pairs(y):
    return jnp.stack([y[:, 128 * j:128 * (j + 1)] for j in range(DL_PAIRS)])


def f_attn_qk(x, g, w, nrm):
    pr = mm(rms(x, g), w)
    ms = group_sum(pr * pr, DL_HEADS) * (1.0 / HEAD)
    return _pairs(pr * lax.rsqrt(ms + NORM_EPS) * jnp.concatenate([nrm] * DL_HEADS, axis=1))


def f_attn_v(x, g, w):
    return _pairs(mm(rms(x, g), w))


def _masked(strict, x):
    return x if strict is None else jnp.where(strict, x, 0.0)


def _head_stack(x):
    nh = x.shape[1] // HEAD
    lane_head = lax.broadcasted_iota(jnp.int32, (1, x.shape[1]), 1) // HEAD
    return jnp.concatenate([jnp.where(lane_head == h, x, 0.0) for h in range(nh)], axis=0).astype(BF16)


def _head_pick(xs):
    nh = xs.shape[1] // HEAD
    rows = xs.shape[0] // nh
    lane_head = lax.broadcasted_iota(jnp.int32, (1, xs.shape[1]), 1) // HEAD
    out = xs[:rows]
    for h in range(1, nh):
        out = jnp.where(lane_head == h, xs[rows * h:rows * (h + 1)], out)
    return out


def _sb_tiles(qs, kblk, strict):
    z = _dg(qs, kblk, ((1,), (1,))) * (HEAD ** -0.5)
    keep = -(jnp.maximum(z, 0.0) + jnp.log(1.0 + jnp.exp(-jnp.abs(z))))
    return z, _masked(strict, keep)


def _tri(n, upper):
    r = lax.broadcasted_iota(jnp.int32, (n, n), 0)
    c = lax.broadcasted_iota(jnp.int32, (n, n), 1)
    return ((r > c) if upper else (r < c)).astype(BF16)


def _tri_sums(x, tri):
    hi, lo = _split2(x)
    return _dg(hi, tri, ((1,), (0,))) + _dg(lo, tri, ((1,), (0,)))


SB_UNROLL = 4


def _sb_diag(tb, nh):
    r = lax.broadcasted_iota(jnp.int32, (nh * tb, tb), 0)
    return lax.broadcasted_iota(jnp.int32, (nh * tb, tb), 1) < lax.rem(r, tb)


def _sb_sweep(step, first, count, carry, direction, commit=None):
    def run(kbs, c):
        outs = []
        for kb in kbs:
            c, out = step(kb, c)
            outs.append(out)
        if commit is not None:
            for kb, out in zip(kbs, outs):
                commit(kb, out)
        return c

    rem = count % SB_UNROLL
    carry = lax.fori_loop(0, rem, lambda i, c: run([first + direction * i], c), carry)
    return lax.fori_loop(
        0, count // SB_UNROLL,
        lambda g, c: run([first + direction * (rem + SB_UNROLL * g + u) for u in range(SB_UNROLL)], c), carry)


def _riding(ride, refs, n_in, n_out, first, last):
    if ride is None:
        return refs, lambda: None
    n = ride.n
    own = refs[:n_in] + refs[n_in + n:n_in + n + n_out] + refs[n_in + 2 * n + n_out:len(refs) - 2]
    start, wait = ride.ops(refs[n_in:n_in + n], refs[n_in + n + n_out:n_in + 2 * n + n_out], refs[-2], refs[-1])
    pl.when(first)(start)
    return own, lambda: pl.when(last)(wait)


def _ride_specs(ride):
    if ride is None:
        return [], [], [], [], []
    return [_HBM] * ride.n, [_HBM] * ride.n, ride.out_shapes, ride.sem_shapes(), ride.arrays


def sb_fwd(q, k, v, ride=None, tb=QBLK):
    S = q.shape[0]
    nh = SB_W // HEAD
    r_in, r_out, r_shape, r_scr, r_args = _ride_specs(ride)

    def body(*refs):
        qb = pl.program_id(0)
        (q_ref, k_ref, v_ref, o_ref), finish = _riding(ride, refs, 3, 1, qb == 0, qb == S // tb - 1)
        diag = _sb_diag(tb, nh)
        after_mat = _tri(tb, True)
        qs = _head_stack(q_ref[...])

        def step(kb, carry, strict):
            acc, run = carry
            rows = pl.ds(pl.multiple_of(kb * tb, tb), tb)
            z, keep = _sb_tiles(qs, k_ref[rows, :].astype(BF16), strict)
            w = _masked(strict, jnp.exp(z + keep + _tri_sums(keep, after_mat) + run))
            acc = acc + _dg(w.astype(BF16), v_ref[rows, :].astype(BF16), ((1,), (0,)))
            return acc, run + jnp.sum(keep, axis=1, keepdims=True)

        init = (jnp.zeros((nh * tb, SB_W), F32), jnp.zeros((nh * tb, 1), F32))
        carry = step(qb, init, diag)
        acc, _ = _sb_sweep(lambda kb, c: (step(kb, c, None), None), qb - 1, qb, carry, -1)
        o_ref[...] = _head_pick(acc)
        finish()

    return pl.pallas_call(
        body, name="sb_fwd", grid=(S // tb,),
        in_specs=[pl.BlockSpec((tb, SB_W), lambda i: (i, 0)), pl.BlockSpec((S, SB_W), lambda i: (0, 0)),
                  pl.BlockSpec((S, SB_W), lambda i: (0, 0))] + r_in,
        out_specs=[pl.BlockSpec((tb, SB_W), lambda i: (i, 0))] + r_out,
        out_shape=[jax.ShapeDtypeStruct((S, SB_W), F32)] + r_shape,
        scratch_shapes=r_scr,
        compiler_params=_cp(("arbitrary",)),
    )(q, k, v, *r_args)


def sb_bwd(q, k, v, do, ride=None, tb=QBLK):
    S = q.shape[0]
    nh = SB_W // HEAD
    scale = HEAD ** -0.5
    r_in, r_out, r_shape, r_scr, r_args = _ride_specs(ride)

    def body(*refs):
        qb = pl.program_id(0)
        (q_ref, k_ref, v_ref, do_ref, dq_ref, dk_ref, dv_ref, g_scr), finish = _riding(
            ride, refs, 4, 3, qb == 0, qb == S // tb - 1)

        @pl.when(qb == 0)
        def _():
            dk_ref[...] = jnp.zeros_like(dk_ref)
            dv_ref[...] = jnp.zeros_like(dv_ref)

        diag = _sb_diag(tb, nh)
        after_mat = _tri(tb, True)
        before_mat = _tri(tb, False)
        qs = _head_stack(q_ref[...])
        dos = _head_stack(do_ref[...])

        def right_to_left(kb, run, strict):
            rows = pl.ds(pl.multiple_of(kb * tb, tb), tb)
            z, keep = _sb_tiles(qs, k_ref[rows, :].astype(BF16), strict)
            w = _masked(strict, jnp.exp(z + keep + _tri_sums(keep, after_mat) + run))
            g_scr[kb] = _dg(dos, v_ref[rows, :].astype(BF16), ((1,), (1,))) * w
            dv = _dg(w.astype(BF16), dos, ((0,), (0,)))
            return run + jnp.sum(keep, axis=1, keepdims=True), dv

        def add_rows(ref):
            def commit(kb, val):
                ref[pl.ds(pl.multiple_of(kb * tb, tb), tb), :] += val
            return commit

        zero_run = jnp.zeros((nh * tb, 1), F32)
        run, dv_diag = right_to_left(qb, zero_run, diag)
        add_rows(dv_ref)(qb, dv_diag)
        _sb_sweep(lambda kb, r: right_to_left(kb, r, None), qb - 1, qb, run, -1, add_rows(dv_ref))

        def left_to_right(kb, carry, strict):
            dq, run = carry
            rows = pl.ds(pl.multiple_of(kb * tb, tb), tb)
            kblk = k_ref[rows, :].astype(BF16)
            gw = g_scr[kb]
            sig = jax.nn.sigmoid(_dg(qs, kblk, ((1,), (1,))) * scale)
            dkeep = _masked(strict, _tri_sums(gw, before_mat) + run)
            dz = ((gw * (1.0 - sig) - dkeep * sig) * scale).astype(BF16)
            dq = dq + _dg(dz, kblk, ((1,), (0,)))
            return (dq, run + jnp.sum(gw, axis=1, keepdims=True)), _dg(dz, qs, ((0,), (0,)))

        carry = _sb_sweep(lambda kb, c: left_to_right(kb, c, None), 0, qb,
                          (jnp.zeros((nh * tb, SB_W), F32), zero_run), 1, add_rows(dk_ref))
        (dq, _), dk_diag = left_to_right(qb, carry, diag)
        add_rows(dk_ref)(qb, dk_diag)
        dq_ref[...] = _head_pick(dq)
        finish()

    whole = pl.BlockSpec((S, SB_W), lambda i: (0, 0))
    blk = pl.BlockSpec((tb, SB_W), lambda i: (i, 0))
    return pl.pallas_call(
        body, name="sb_bwd", grid=(S // tb,),
        in_specs=[blk, whole, whole, blk] + r_in, out_specs=[blk, whole, whole] + r_out,
        out_shape=[jax.ShapeDtypeStruct((S, SB_W), F32)] * 3 + r_shape,
        scratch_shapes=[pltpu.VMEM((S // tb, nh * tb, tb), F32)] + r_scr,
        compiler_params=_cp(("arbitrary",)),
    )(q, k, v, do, *r_args)


def reorder(name, x, groups, inverse):
    P, S, _ = x.shape

    def body(x_ref, o_ref):
        p = pl.program_id(0)
        for gi, r in enumerate(groups):
            @pl.when(p // 2 == gi)
            def _(r=r):
                L = S // r
                if r == 1:
                    o_ref[...] = x_ref[...]
                for c in range(r if r > 1 else 0):
                    if inverse:
                        o_ref[pl.ds(c, L, stride=r), :] = x_ref[c * L:(c + 1) * L, :]
                    else:
                        o_ref[c * L:(c + 1) * L, :] = x_ref[pl.ds(c, L, stride=r), :]

    slab = pl.BlockSpec((None, S, 128), lambda p: (p, 0, 0))
    return pl.pallas_call(
        body, name=name, grid=(P,), in_specs=[slab], out_specs=slab,
        out_shape=jax.ShapeDtypeStruct(x.shape, x.dtype), compiler_params=_cp(("parallel",)),
    )(x)


def _dil_blocks(S):
    return S // QBLK


def _dil_mask(n_in_stream):
    qi = lax.broadcasted_iota(jnp.int32, (QBLK, 2 * QBLK), 0)
    kj = lax.broadcasted_iota(jnp.int32, (QBLK, 2 * QBLK), 1) - QBLK
    dist = qi - kj
    return (dist >= 0) & (dist <= QBLK) & ((n_in_stream > 0) | (kj >= 0))


def _stream_pos(gi, i, S):
    nb = jnp.where(gi == 0, S // (QBLK * DIL[0]), jnp.where(gi == 1, S // (QBLK * DIL[1]), S // (QBLK * DIL[2])))
    return i % nb


def dil_fwd(q, k, v, bias, ride=None):
    S = q.shape[1]
    nblk = _dil_blocks(S)
    r_in, r_out, r_shape, r_scr, r_args = _ride_specs(ride)

    def body(*refs):
        gi, i = pl.program_id(0), pl.program_id(1)
        (q_ref, kc_ref, kp_ref, vc_ref, vp_ref, b_ref, o_ref, l_ref), finish = _riding(
            ride, refs, 6, 2, (gi == 0) & (i == 0), (gi == len(DIL) - 1) & (i == nblk - 1))
        mask = _dil_mask(_stream_pos(gi, i, S))
        for j in range(2):
            q2, kc, kp, vc, vp = q_ref[j], kc_ref[j], kp_ref[j], vc_ref[j], vp_ref[j]
            os_, ls_ = [], []
            for hh in range(2):
                sl = slice(HEAD * hh, HEAD * (hh + 1))
                kw = jnp.concatenate([kp[:, sl], kc[:, sl]], axis=0)
                vw = jnp.concatenate([vp[:, sl], vc[:, sl]], axis=0)
                lg = _bdot(q2[:, sl], kw, ((1,), (1,))) * (HEAD ** -0.5) + b_ref[2 * j + hh]
                lg = jnp.where(mask, lg, NEG_INF)
                m = jnp.max(lg, axis=-1, keepdims=True)
                p = jnp.exp(lg - m)
                den = jnp.sum(p, axis=-1, keepdims=True)
                os_.append(_bdot(p / den, vw, ((1,), (0,))))
                ls_.append(jnp.broadcast_to(m + jnp.log(den), (QBLK, HEAD)))
            o_ref[j] = jnp.concatenate(os_, axis=1)
            l_ref[j] = jnp.concatenate(ls_, axis=1)
        finish()

    cur = pl.BlockSpec((2, QBLK, 128), lambda g, i: (g, i, 0))
    prev = pl.BlockSpec((2, QBLK, 128), lambda g, i: (g, jnp.maximum(i - 1, 0), 0))
    return pl.pallas_call(
        body, name="dil_fwd", grid=(len(DIL), nblk),
        in_specs=[cur, cur, prev, cur, prev, pl.BlockSpec((4, QBLK, 2 * QBLK), lambda g, i: (g, 0, 0))] + r_in,
        out_specs=[cur, cur] + r_out,
        out_shape=[jax.ShapeDtypeStruct(q.shape, F32)] * 2 + r_shape,
        scratch_shapes=r_scr,
        compiler_params=_cp(("arbitrary", "arbitrary")),
    )(q, k, k, v, v, bias, *r_args)


def dil_bwd(q, k, v, bias, o, lse, do, dlse):
    S = q.shape[1]
    nblk = _dil_blocks(S)

    def body(q_ref, kc_ref, kp_ref, vc_ref, vp_ref, b_ref, o_ref, l_ref, do_ref, dl_ref,
             dq_ref, dk_ref, dv_ref, ds_ref, dk_car, dv_car):
        gi, i = pl.program_id(0), pl.program_id(1)

        @pl.when(i == 0)
        def _():
            ds_ref[...] = jnp.zeros_like(ds_ref)
            dk_car[...] = jnp.zeros_like(dk_car)
            dv_car[...] = jnp.zeros_like(dv_car)

        @pl.when(i < nblk)
        def _():
            mask = _dil_mask(_stream_pos(gi, i, S))
            for j in range(2):
                q2, kc, kp, vc, vp = q_ref[j], kc_ref[j], kp_ref[j], vc_ref[j], vp_ref[j]
                o2, l2, do2, dl2 = o_ref[j], l_ref[j], do_ref[j], dl_ref[j]
                dqs, dkps, dkcs, dvps, dvcs = [], [], [], [], []
                for hh in range(2):
                    sl = slice(HEAD * hh, HEAD * (hh + 1))
                    qh, doh = q2[:, sl], do2[:, sl]
                    kw = jnp.concatenate([kp[:, sl], kc[:, sl]], axis=0)
                    vw = jnp.concatenate([vp[:, sl], vc[:, sl]], axis=0)
                    lg = _bdot(qh, kw, ((1,), (1,))) * (HEAD ** -0.5) + b_ref[2 * j + hh]
                    p = jnp.where(mask, jnp.exp(lg - l2[:, HEAD * hh:HEAD * hh + 1]), 0.0)
                    dp = _bdot(doh, vw, ((1,), (1,)))
                    delta = jnp.sum(doh * o2[:, sl], axis=-1, keepdims=True)
                    dl = jnp.sum(dl2[:, sl], axis=-1, keepdims=True)
                    ds = p * (dp - delta + dl)
                    ds_ref[2 * j + hh] += ds
                    dsq = ds * (HEAD ** -0.5)
                    dqs.append(_bdot(dsq, kw, ((1,), (0,))))
                    dkw = _bdot(dsq, qh, ((0,), (0,)))
                    dvw = _bdot(p, doh, ((0,), (0,)))
                    dkps.append(dkw[:QBLK])
                    dkcs.append(dkw[QBLK:])
                    dvps.append(dvw[:QBLK])
                    dvcs.append(dvw[QBLK:])
                dq_ref[j] = jnp.concatenate(dqs, axis=1)
                dk_ref[j] = dk_car[j] + jnp.concatenate(dkps, axis=1)
                dv_ref[j] = dv_car[j] + jnp.concatenate(dvps, axis=1)
                dk_car[j] = jnp.concatenate(dkcs, axis=1)
                dv_car[j] = jnp.concatenate(dvcs, axis=1)

        @pl.when(i == nblk)
        def _():
            dk_ref[...] = dk_car[...]
            dv_ref[...] = dv_car[...]

    cur = pl.BlockSpec((2, QBLK, 128), lambda g, i: (g, jnp.minimum(i, nblk - 1), 0))
    prev = pl.BlockSpec((2, QBLK, 128), lambda g, i: (g, jnp.clip(i - 1, 0, nblk - 1), 0))
    bspec = pl.BlockSpec((4, QBLK, 2 * QBLK), lambda g, i: (g, 0, 0))
    return pl.pallas_call(
        body, name="dil_bwd", grid=(len(DIL), nblk + 1),
        in_specs=[cur, cur, prev, cur, prev, bspec, cur, cur, cur, cur],
        out_specs=[cur, prev, prev, bspec],
        out_shape=[jax.ShapeDtypeStruct(q.shape, F32)] * 3 + [jax.ShapeDtypeStruct(bias.shape, F32)],
        scratch_shapes=[pltpu.VMEM((2, QBLK, 128), F32), pltpu.VMEM((2, QBLK, 128), F32)],
        compiler_params=_cp(("arbitrary", "arbitrary")),
    )(q, k, k, v, v, bias, o, lse, do, dlse)


def _t5_bucket(dist):
    max_exact = N_BUCKETS // 2
    d = jnp.maximum(dist, 1).astype(F32)
    large = max_exact + (jnp.log(d / max_exact) / math.log(MAX_DISTANCE / max_exact)
                         * (N_BUCKETS - max_exact)).astype(jnp.int32)
    large = jnp.minimum(large, N_BUCKETS - 1)
    return jnp.where(dist < max_exact, dist, large)


def _bucket_maps():
    qi = jnp.arange(QBLK)[:, None]
    kj = jnp.arange(2 * QBLK)[None, :] - QBLK
    dist = jnp.maximum(qi - kj, 0)
    return jnp.stack([_t5_bucket(dist * r) for r in DIL])


def bias_table(rel_bias, buckets):
    def body(tbl_ref, bk_ref, o_ref):
        for h in range(DL_HEADS):
            bk = bk_ref[h // 4]

            def step(b, acc):
                return jnp.where(bk == b, tbl_ref[b, h], acc)

            o_ref[h] = lax.fori_loop(0, N_BUCKETS, step, jnp.zeros(bk.shape, F32))

    return pl.pallas_call(
        body, name="bias_table", out_shape=jax.ShapeDtypeStruct((DL_HEADS,) + buckets.shape[1:], F32),
        in_specs=[pl.BlockSpec(memory_space=pltpu.SMEM), pl.BlockSpec(memory_space=pltpu.VMEM)],
        out_specs=pl.BlockSpec(memory_space=pltpu.VMEM),
    )(rel_bias, buckets)


def bias_grad(ds, buckets):
    def body(ds_ref, bk_ref, o_ref):
        lane = lax.broadcasted_iota(jnp.int32, (1, 128), 1)
        for h in range(DL_HEADS):
            dsv = ds_ref[h]
            bk = bk_ref[h // 4]

            def step(b, row):
                return jnp.where(lane == b, jnp.sum(jnp.where(bk == b, dsv, 0.0)), row)

            o_ref[h:h + 1, :] = lax.fori_loop(0, N_BUCKETS, step, jnp.zeros((1, 128), F32))

    return pl.pallas_call(
        body, name="bias_grad", out_shape=jax.ShapeDtypeStruct((DL_HEADS, 128), F32),
        in_specs=[pl.BlockSpec(memory_space=pltpu.VMEM)] * 2, out_specs=pl.BlockSpec(memory_space=pltpu.VMEM),
    )(ds, buckets)


def f_attn_out(x, oa, o, lse, w):
    og = [jnp.concatenate([o[2 * g], o[2 * g + 1]], axis=1) for g in range(3)]
    lg = [jnp.concatenate([lse[2 * g], lse[2 * g + 1]], axis=1) for g in range(3)]
    m = jnp.maximum(jnp.maximum(lg[0], lg[1]), lg[2])
    e = [jnp.exp(l - m) for l in lg]
    den = e[0] + e[1] + e[2]
    ob = (e[0] * og[0] + e[1] * og[1] + e[2] * og[2]) / den
    return x + mm(jnp.concatenate([oa, ob], axis=1), w)


def norm_shift_fwd(x, g, tm=256):
    S = x.shape[0]

    def body(x_ref, xp_ref, g_ref, h_ref, hs_ref):
        h = rms(x_ref[...], g_ref[...])
        hp = rms(xp_ref[7:8, :], g_ref[...])
        hp = jnp.where(pl.program_id(0) == 0, 0.0, hp)
        row = lax.broadcasted_iota(jnp.int32, (tm, D), 0)
        h_ref[...] = h
        hs_ref[...] = jnp.where(row == 0, hp, pltpu.roll(h, 1, 0))

    return pl.pallas_call(
        body, name="rw_norm_shift", grid=(S // tm,),
        in_specs=[pl.BlockSpec((tm, D), lambda t: (t, 0)),
                  pl.BlockSpec((8, D), lambda t: (jnp.maximum(t * (tm // 8) - 1, 0), 0)),
                  pl.BlockSpec((1, D), lambda t: (0, 0))],
        out_specs=[pl.BlockSpec((tm, D), lambda t: (t, 0))] * 2,
        out_shape=[jax.ShapeDtypeStruct((S, D), F32)] * 2,
        compiler_params=_cp(("parallel",)),
    )(x, x, g)


def norm_shift_bwd(x, g, dh, dhs, dres, tm=256):
    S = x.shape[0]
    nt = S // tm

    def body(x_ref, g_ref, dh_ref, dhs_ref, dhn_ref, dr_ref, dx_ref, dg_ref):
        t = pl.program_id(0)
        nxt = jnp.where(t == nt - 1, 0.0, dhn_ref[0:1, :])
        row = lax.broadcasted_iota(jnp.int32, (tm, D), 0)
        tot = dh_ref[...] + jnp.where(row == tm - 1, nxt, pltpu.roll(dhs_ref[...], tm - 1, 0))
        _, vjp = jax.vjp(rms, x_ref[...], g_ref[...])
        dx, dg = vjp(tot)
        dx_ref[...] = dr_ref[...] + dx

        @pl.when(t == 0)
        def _():
            dg_ref[...] = dg

        @pl.when(t != 0)
        def _():
            dg_ref[...] += dg

    tile = pl.BlockSpec((tm, D), lambda t: (t, 0))
    return pl.pallas_call(
        body, name="rw_norm_shift_bwd", grid=(nt,),
        in_specs=[tile, pl.BlockSpec((1, D), lambda t: (0, 0)), tile, tile,
                  pl.BlockSpec((8, D), lambda t: (jnp.minimum((t + 1) * (tm // 8), S // 8 - 1), 0)), tile],
        out_specs=[tile, pl.BlockSpec((1, D), lambda t: (0, 0))],
        out_shape=[jax.ShapeDtypeStruct((S, D), F32), jax.ShapeDtypeStruct((1, D), F32)],
        compiler_params=_cp(("arbitrary",)),
    )(x, g, dh, dhs, dhs, dres)


def f_rw_proj(h, hs, mix, w):
    return mm(h + (hs - h) * mix, w)


def f_rw_mid(h, hs, r, k, v, mix3, w0, a0, kkw, kaw, w1, w2, a1, a2, g1, g2):
    xx = hs - h
    xw, xa, xg = h + xx * mix3[0:1], h + xx * mix3[1:2], h + xx * mix3[2:3]
    w_log = -softplus(-(w0 + mm(jnp.tanh(mm(xw, w1)), w2))) - 0.5
    lw = -jnp.exp(w_log)
    ag = jax.nn.sigmoid(a0 + mm(mm(xa, a1), a2))
    gate = mm(jax.nn.sigmoid(mm(xg, g1)), g2)
    kk = k * kkw
    kk = kk / jnp.maximum(jnp.sqrt(group_sum(kk * kk, RW_H)), 1e-12)
    kmod = k * (1.0 + (ag - 1.0) * kaw)
    return (to_heads(r), to_heads(lw), to_heads(kmod), to_heads(v), to_heads(-kk), to_heads(kk * ag), gate)


def f_rw_post(yh, rh, kh, vh, gate, x, lng, lnb, rk, wo):
    mu = jnp.mean(yh, axis=-1, keepdims=True)
    var = jnp.mean(jnp.square(yh - mu), axis=-1, keepdims=True)
    yn = (yh - mu) * lax.rsqrt(var + GN_EPS)
    bonus = jnp.sum(rh * kh * rk, axis=-1, keepdims=True) * vh
    y = from_heads(yn) * lng + lnb + from_heads(bonus)
    return x + mm(y * gate, wo)


def _split2(x):
    hi = x.astype(BF16)
    return hi, (x - hi.astype(F32)).astype(BF16)


def _b3(x, y, cx, cy):
    dn = (((cx,), (cy,)), ((0,), (0,)))
    xh, xl = _split2(x)
    yh, yl = _split2(y)
    d = lambda p, q: lax.dot_general(p, q, dn, preferred_element_type=F32)
    return d(xh, yh) + (d(xh, yl) + d(xl, yh))


@jax.custom_vjp
def b_nt(x, y):
    return _b3(x, y, 2, 2)


@jax.custom_vjp
def b_nn(x, y):
    return _b3(x, y, 2, 1)


@jax.custom_vjp
def b_tn(x, y):
    return _b3(x, y, 1, 1)


b_nt.defvjp(lambda x, y: (b_nt(x, y), (x, y)), lambda r, g: (b_nn(g, r[1]), b_tn(g, r[0])))
b_nn.defvjp(lambda x, y: (b_nn(x, y), (x, y)), lambda r, g: (b_nt(g, r[1]), b_tn(r[0], g)))
b_tn.defvjp(lambda x, y: (b_tn(x, y), (x, y)), lambda r, g: (b_nt(r[1], g), b_nn(r[0], g)))


def _tri_apply(x, lower):
    H, C, _ = x.shape
    ii = lax.broadcasted_iota(jnp.int32, (C, C), 0)
    jj = lax.broadcasted_iota(jnp.int32, (C, C), 1)
    m = jnp.broadcast_to(((jj <= ii) if lower else (jj >= ii)).astype(BF16), (H, C, C))
    x1 = x.astype(BF16)
    r1 = x - x1.astype(F32)
    x2 = r1.astype(BF16)
    x3 = (r1 - x2.astype(F32)).astype(BF16)
    d = lambda q: lax.dot_general(m, q, (((2,), (1,)), ((0,), (0,))), preferred_element_type=F32)
    return d(x1) + (d(x2) + d(x3))


@jax.custom_vjp
def run_sum(x):
    return _tri_apply(x, True)


run_sum.defvjp(lambda x: (run_sum(x), None), lambda _, g: (_tri_apply(g, False),))


def rwkv_chunk(S0, r, lw, k, v, a, b):
    H, C, _ = r.shape
    V = S0.shape[1]
    ii = lax.broadcasted_iota(jnp.int32, (C, C), 0)
    jj = lax.broadcasted_iota(jnp.int32, (C, C), 1)
    strict = jj < ii
    i2 = lax.broadcasted_iota(jnp.int32, (C, 2 * C), 0)
    j2 = lax.broadcasted_iota(jnp.int32, (C, 2 * C), 1)
    incl2 = jnp.where(j2 >= C, j2 - C, j2) <= i2
    g = run_sum(lw)
    ig = jnp.exp(-g)
    ar = jnp.concatenate([a * jnp.exp(g - lw), r * jnp.exp(g)], axis=1)
    bk = jnp.concatenate([b * ig, k * ig], axis=1)
    m = b_nt(ar, bk)
    a_ab = jnp.where(strict, m[:, :C, :C], 0.0)
    a_ak = jnp.where(strict, m[:, :C, C:], 0.0)
    b_r = jnp.where(incl2, m[:, C:, :], 0.0)
    p = b_nt(ar, S0)
    u = p[:, :C] + b_nn(a_ak, v)
    nmat, n = a_ab, 1
    while n < C:
        n *= 2
        if n < C:
            z = b_nn(nmat, jnp.concatenate([u, nmat], axis=2))
            u, nmat = u + z[:, :, :V], z[:, :, V:]
        else:
            u = u + b_nn(nmat, u)
    uv = jnp.concatenate([u, v], axis=1)
    y = p[:, C:] + b_nn(b_r, uv)
    g_end = g[:, C - 1:C, :]
    dec = jnp.exp(g_end - g)
    s_new = S0 * jnp.exp(g_end) + b_tn(uv, jnp.concatenate([b * dec, k * dec], axis=1))
    return y, s_new


def rwkv_fwd(r, lw, k, v, a, b):
    H, S, _ = r.shape
    C = RW_CHUNK

    def body(r_ref, lw_ref, k_ref, v_ref, a_ref, b_ref, y_ref, s_ref, s_scr):
        @pl.when(pl.program_id(0) == 0)
        def _():
            s_scr[...] = jnp.zeros_like(s_scr)

        s0 = s_scr[...]
        s_ref[0] = s0
        y, s1 = rwkv_chunk(s0, r_ref[...], lw_ref[...], k_ref[...], v_ref[...], a_ref[...], b_ref[...])
        y_ref[...] = y
        s_scr[...] = s1

    bs = pl.BlockSpec((H, C, HEAD), lambda c: (0, c, 0))
    return pl.pallas_call(
        body, name="rwkv_fwd", grid=(S // C,), in_specs=[bs] * 6,
        out_specs=[bs, pl.BlockSpec((1, H, HEAD, HEAD), lambda c: (c, 0, 0, 0))],
        out_shape=[jax.ShapeDtypeStruct((H, S, HEAD), F32), jax.ShapeDtypeStruct((S // C, H, HEAD, HEAD), F32)],
        scratch_shapes=[pltpu.VMEM((H, HEAD, HEAD), F32)],
        compiler_params=_cp(("arbitrary",)),
    )(r, lw, k, v, a, b)


def rwkv_bwd(r, lw, k, v, a, b, states, dy):
    H, S, _ = r.shape
    C = RW_CHUNK
    nc = S // C

    def body(r_ref, lw_ref, k_ref, v_ref, a_ref, b_ref, s_ref, dy_ref, dr, dlw, dk, dv, da, db, ds_scr):
        @pl.when(pl.program_id(0) == 0)
        def _():
            ds_scr[...] = jnp.zeros_like(ds_scr)

        _, vjp = jax.vjp(rwkv_chunk, s_ref[0], r_ref[...], lw_ref[...], k_ref[...], v_ref[...], a_ref[...], b_ref[...])
        grads = vjp((dy_ref[...], ds_scr[...]))
        ds_scr[...] = grads[0]
        for o, gv in zip((dr, dlw, dk, dv, da, db), grads[1:]):
            o[...] = gv

    bs = pl.BlockSpec((H, C, HEAD), lambda c: (0, nc - 1 - c, 0))
    return pl.pallas_call(
        body, name="rwkv_bwd", grid=(nc,),
        in_specs=[bs] * 6 + [pl.BlockSpec((1, H, HEAD, HEAD), lambda c: (nc - 1 - c, 0, 0, 0)), bs],
        out_specs=[bs] * 6, out_shape=[jax.ShapeDtypeStruct((H, S, HEAD), F32)] * 6,
        scratch_shapes=[pltpu.VMEM((H, HEAD, HEAD), F32)],
        compiler_params=_cp(("arbitrary",)),
    )(r, lw, k, v, a, b, states, dy)


def loss_head(y, target, tm=512):
    S = y.shape[0]

    def body(y_ref, t_ref, dy_ref, l_ref):
        e = y_ref[...] - t_ref[...]
        dy_ref[...] = e * (1.0 / D)
        part = jnp.broadcast_to(0.5 * jnp.sum(jnp.mean(e * e, axis=-1, keepdims=True)), (1, 128))

        @pl.when(pl.program_id(0) == 0)
        def _():
            l_ref[...] = part

        @pl.when(pl.program_id(0) != 0)
        def _():
            l_ref[...] += part

    tile = pl.BlockSpec((tm, D), lambda t: (t, 0))
    return pl.pallas_call(
        body, name="loss_head", grid=(S // tm,), in_specs=[tile, tile],
        out_specs=[tile, pl.BlockSpec((1, 128), lambda t: (0, 0))],
        out_shape=[jax.ShapeDtypeStruct((S, D), F32), jax.ShapeDtypeStruct((1, 128), F32)],
        compiler_params=_cp(("arbitrary",)),
    )(y, target)


def _row_tile(rows, cols, budget=1 << 19):
    best = None
    for tr in range(8, rows + 1, 8):
        if rows % tr == 0 and tr * cols <= budget:
            best = tr
    return best or rows


def _adam(w, g, m, v):
    m = ADAM_B1 * m + (1.0 - ADAM_B1) * g
    v = ADAM_B2 * v + (1.0 - ADAM_B2) * jnp.square(g)
    m_hat = m / (1.0 - ADAM_B1 ** ADAM_STEP)
    v_hat = v / (1.0 - ADAM_B2 ** ADAM_STEP)
    return -ADAM_LR * (m_hat / (jnp.sqrt(v_hat) + ADAM_EPS) + ADAM_WD * w), m, v


def sum_slots(name, parts, dtype=F32, extras=()):
    n = 0 if parts is None else parts.shape[0]
    R, C = extras[0].shape if parts is None else parts.shape[1:]
    tr = _row_tile(R, C * (n + len(extras)))
    ins = ([] if parts is None else [parts]) + list(extras)

    def body(*refs):
        terms = [] if parts is None else [refs[0][i] for i in range(n)]
        terms += [r[...] for r in refs[len(ins) - len(extras):len(ins)]]
        s = terms[0].astype(F32)
        for t in terms[1:]:
            s = s + t.astype(F32)
        refs[len(ins)][...] = s.astype(dtype)

    tile = pl.BlockSpec((tr, C), lambda t: (t, 0))
    return pl.pallas_call(
        body, name=name, grid=(R // tr,),
        in_specs=([] if parts is None else [pl.BlockSpec((n, tr, C), lambda t: (0, t, 0))]) + [tile] * len(extras),
        out_specs=tile, out_shape=jax.ShapeDtypeStruct((R, C), dtype), compiler_params=_cp(("parallel",)),
    )(*ins)


def adam_step(name, ga, gb, w, m, v):
    R, C = w.shape
    tr = _row_tile(R, C, 1 << 17)
    ins = [ga] + ([gb] if gb is not None else []) + [w, m, v]

    def body(*refs):
        g = refs[0][...]
        if gb is not None:
            g = g + refs[1][...]
        w_ref, m_ref, v_ref, g_out, d_out, m_out, v_out = refs[len(ins) - 3:]
        d, m2, v2 = _adam(w_ref[...], g, m_ref[...], v_ref[...])
        g_out[...] = g
        d_out[...] = d
        m_out[...] = m2
        v_out[...] = v2

    tile = pl.BlockSpec((tr, C), lambda t: (t, 0))
    return pl.pallas_call(
        body, name=name, grid=(R // tr,), in_specs=[tile] * len(ins), out_specs=[tile] * 4,
        out_shape=[jax.ShapeDtypeStruct((R, C), F32)] * 4, compiler_params=_cp(("parallel",)),
    )(*ins)


def _place():
    return lax.axis_index("x"), lax.axis_index("y"), lax.axis_index("c")


def _flip(me, mask):
    return tuple(1 - v if mk else v for v, mk in zip(me, mask))


CHIP_MASKS = ((1, 0, 0), (0, 1, 0), (1, 1, 0))
ALL_MASKS = tuple((a, b, c) for a in (0, 1) for b in (0, 1) for c in (0, 1) if (a, b, c) != (0, 0, 0))


def _chip(dev):
    return 2 * dev[0] + dev[1]


def _devno(dev):
    return 4 * dev[0] + 2 * dev[1] + dev[2]


class Pushes:
    def __init__(self, arrays, out_shapes, masks, copies, src_of, dst_of, alias=False):
        self.arrays, self.out_shapes, self.masks, self.copies = list(arrays), list(out_shapes), masks, copies
        self.src_of, self.dst_of, self.alias = src_of, dst_of, alias
        self.n = len(self.arrays)

    def sem_shapes(self):
        k = self.n * len(self.masks) * self.copies
        return [pltpu.SemaphoreType.DMA((k,)), pltpu.SemaphoreType.DMA((k,))]

    def ops(self, ins, outs, send_sems, recv_sems):
        me = _place()
        sends, lands = [], []
        for i in range(self.n):
            for j, mk in enumerate(self.masks):
                peer = _flip(me, mk)
                srcs, dsts = self.src_of(ins[i], me, j), self.dst_of(outs[i], me, j)
                here = self.dst_of(outs[i], peer, j)
                for q in range(self.copies):
                    sem = (i * len(self.masks) + j) * self.copies + q
                    sends.append(pltpu.make_async_remote_copy(
                        src_ref=srcs[q], dst_ref=dsts[q], send_sem=send_sems.at[sem], recv_sem=recv_sems.at[sem],
                        device_id=peer, device_id_type=MESH))
                    lands.append(pltpu.make_async_remote_copy(
                        src_ref=here[q], dst_ref=here[q], send_sem=send_sems.at[sem], recv_sem=recv_sems.at[sem],
                        device_id=peer, device_id_type=MESH))

        def start():
            for cp in sends:
                cp.start()

        def wait():
            for cp in lands:
                cp.wait_recv()
            for cp in sends:
                cp.wait_send()

        return start, wait


_HBM = pl.BlockSpec(memory_space=pl.ANY)


def exchange(name, p, local_of=None):
    n = p.n

    def body(*refs):
        ins, outs = refs[:n], refs[n:2 * n]
        start, wait = p.ops(ins, outs, refs[2 * n], refs[2 * n + 1])
        locals_ = []
        if local_of is not None:
            for i in range(n):
                src, dst = local_of(ins[i], outs[i], _place())
                locals_.append(pltpu.make_async_copy(src, dst, refs[2 * n + 2].at[i]))
                locals_[-1].start()
        start()
        wait()
        for cp in locals_:
            cp.wait()

    return pl.pallas_call(
        body, name=name, in_specs=[_HBM] * n, out_specs=[_HBM] * n, out_shape=p.out_shapes,
        scratch_shapes=p.sem_shapes() + ([pltpu.SemaphoreType.DMA((n,))] if local_of is not None else []),
        input_output_aliases={i: i for i in range(n)} if p.alias else {},
    )(*p.arrays)


def _half(c, rows):
    return pl.ds(c * (rows // 2), rows // 2)


def gather_pushes(arrays):
    outs = [jax.ShapeDtypeStruct((N_CHIPS,) + a.shape, a.dtype) for a in arrays]
    sib = len(CHIP_MASKS)
    return Pushes(arrays, outs, CHIP_MASKS + ((0, 0, 1),), 1,
                  src_of=lambda r, me, j: [r] if j == sib else [r.at[_half(me[2], r.shape[0])]],
                  dst_of=lambda o, sender, j: [o.at[_chip(sender)]] if j == sib else
                  [o.at[_chip(sender), _half(sender[2], o.shape[1])]])


def gather_swap(name, got):
    outs = [jax.ShapeDtypeStruct(a.shape, a.dtype) for a in got]
    return exchange(name, Pushes(
        got, outs, ((0, 0, 1),), len(CHIP_MASKS),
        src_of=lambda r, me, j: [r.at[_chip(_flip(me, mk)), _half(me[2], r.shape[1])] for mk in CHIP_MASKS],
        dst_of=lambda o, sender, j: [o.at[_chip(_flip(sender, mk)), _half(sender[2], o.shape[1])] for mk in CHIP_MASKS],
        alias=True))


def reduce_begin(tag, names, arrays, wire):
    c = lax.axis_index("c")
    split = [a.reshape(N_CHIPS, 2, a.shape[1] // 2, a.shape[2]) for a in arrays]
    half_shapes = [jax.ShapeDtypeStruct((N_CHIPS,) + a.shape[2:], F32) for a in split]
    theirs = exchange(f"grad_pre_swap_{tag}", Pushes(
        split, half_shapes, ((0, 0, 1),), 1,
        src_of=lambda r, me, j: [r.at[:, 1 - me[2]]], dst_of=lambda o, sender, j: [o]))
    chip_sum = []
    for nm, a, t, dt in zip(names, split, theirs, wire):
        own = lax.dynamic_index_in_dim(a, c, axis=1, keepdims=False)
        flat = lambda v: v.reshape(-1, v.shape[-1])
        chip_sum.append(sum_slots(f"sum2_{nm}", None, dt, [flat(own), flat(t)]).reshape(t.shape))
    pushes = Pushes(chip_sum, [jax.ShapeDtypeStruct((len(CHIP_MASKS),) + a.shape[1:], a.dtype) for a in chip_sum],
                    CHIP_MASKS, 1,
                    src_of=lambda r, me, j: [r.at[_chip(_flip(me, CHIP_MASKS[j]))]],
                    dst_of=lambda o, sender, j: [o.at[j]])
    return chip_sum, pushes


def reduce_end(tag, names, chip_sum, landed):
    x, y, c = _place()
    halves = [sum_slots(f"sum4_{nm}", p, F32, [lax.dynamic_index_in_dim(a, _chip((x, y, c)), axis=0, keepdims=False)])
              for nm, p, a in zip(names, landed, chip_sum)]
    others = exchange(f"grad_final_swap_{tag}", Pushes(
        halves, [jax.ShapeDtypeStruct(a.shape, F32) for a in halves], ((0, 0, 1),), 1,
        src_of=lambda r, me, j: [r], dst_of=lambda o, sender, j: [o]))
    return [jnp.concatenate([jnp.where(c == 0, h, o), jnp.where(c == 0, o, h)], axis=0) for h, o in zip(halves, others)]


def gather_all(arrays):
    outs = [jax.ShapeDtypeStruct((8,) + a.shape, a.dtype) for a in arrays]
    return exchange("gather_replicated", Pushes(
        arrays, outs, ALL_MASKS, 1, src_of=lambda r, me, j: [r], dst_of=lambda o, sender, j: [o.at[_devno(sender)]]),
        local_of=lambda r, o, me: (r, o.at[_devno(me)]))


def _unshard_cols(g):
    return jnp.transpose(g, (1, 0, 2)).reshape(g.shape[1], -1)


def _shard_cols(a):
    return jnp.transpose(a.reshape(a.shape[0], N_CHIPS, -1), (1, 0, 2))


class Weights(dict):
    def ride(self, kernel_name):
        return None

    def arrived(self, kernel_name, outs):
        pass


def _forward_backward(x, tgt, W, grads_early=None):
    S = x.shape[0]
    G = {}
    sd = jax.ShapeDtypeStruct

    def ffn(xin, l, j):
        return ffn_fwd(xin, W["ffn_norm"][l][j], W["ffn_w_gate", l, j], W["ffn_w_up", l, j], W["ffn_w_down", l, j], l, j)

    def ffn_back(xin, dout, l, j):
        gn = W["ffn_norm"][l][j]
        dh, G["ffn_w_gate", l, j], G["ffn_w_up", l, j], G["ffn_w_down", l, j] = ffn_bwd(
            xin, gn, W["ffn_w_gate", l, j], W["ffn_w_up", l, j], W["ffn_w_down", l, j], dout, l, j)
        dx, G[("ffn_norm", l, j)] = norm_bwd(f"ffn_norm_bwd_{l}{j}", xin, gn, dh, dout)
        return dx

    x0 = x
    x1 = ffn(x0, 0, 0)
    g0 = W["mix_norm"][0]
    sbq, sbk, sbv = tile_fwd(f_attn_sb, "attn_in_sb", [x1], [g0, W["attn_w_in"][0]], [sd((S, SB_W), F32)] * 3, 256)
    dl_shape = sd((DL_PAIRS, S, 128), F32)
    qn, = tile_fwd(f_attn_qk, "attn_in_q", [x1], [g0, W["attn_w_in"][1], W["attn_q_norm"]], [dl_shape], 256)
    kn, = tile_fwd(f_attn_qk, "attn_in_k", [x1], [g0, W["attn_w_in"][2], W["attn_k_norm"]], [dl_shape], 256)
    vv, = tile_fwd(f_attn_v, "attn_in_v", [x1], [g0, W["attn_w_in"][3]], [dl_shape], 256)
    oa, *rode = sb_fwd(sbq, sbk, sbv, W.ride("sb_fwd"))
    W.arrived("sb_fwd", rode)
    qs, ks, vs = (reorder(nm, t, DIL, False) for nm, t in (("sub_q", qn), ("sub_k", kn), ("sub_v", vv)))
    o_s, lse_s, *rode = dil_fwd(qs, ks, vs, W["bias_mat"], W.ride("dil_fwd"))
    W.arrived("dil_fwd", rode)
    o_n, lse_n = reorder("nat_o", o_s, DIL, True), reorder("nat_lse", lse_s, DIL, True)
    x2, = tile_fwd(f_attn_out, "attn_out", [x1, oa, o_n, lse_n], [W["attn_w_out"]], [sd((S, D), F32)], 256)
    x3 = ffn(x2, 0, 1)
    x4 = ffn(x3, 1, 0)
    g1 = W["mix_norm"][1]
    h, hs = norm_shift_fwd(x4, g1)
    mix = W["rw_mix"]
    r, = tile_fwd(f_rw_proj, "rw_proj_r", [h, hs], [mix[0:1], W["rw_wr"]], [sd((S, D), F32)], 256)
    k, = tile_fwd(f_rw_proj, "rw_proj_k", [h, hs], [mix[2:3], W["rw_wk"]], [sd((S, D), F32)], 256)
    v, = tile_fwd(f_rw_proj, "rw_proj_v", [h, hs], [mix[3:4], W["rw_wv"]], [sd((S, D), F32)], 256)
    mix3 = jnp.concatenate([mix[1:2], mix[4:5], mix[5:6]], axis=0)
    mid_w = [mix3, W["rw_w0"], W["rw_a0"], W["rw_kk"], W["rw_ka"], W["rw_w1"], W["rw_w2"], W["rw_a1"], W["rw_a2"],
             W["rw_g1"], W["rw_g2"]]
    hshape = sd((RW_H, S, HEAD), F32)
    mid_tiles = [h, hs, r, k, v]
    rh, lwh, kh, vh, ah, bh, gate = tile_fwd(f_rw_mid, "rw_mid", mid_tiles, mid_w, [hshape] * 6 + [sd((S, D), F32)], 128)
    yh, states = rwkv_fwd(rh, lwh, kh, vh, ah, bh)
    post_w = [W["rw_lnx_g"], W["rw_lnx_b"], W["rw_rk"], W["rw_wo"]]
    post_tiles = [yh, rh, kh, vh, gate, x4]
    x5, = tile_fwd(f_rw_post, "rw_post", post_tiles, post_w, [sd((S, D), F32)], 128)
    x6 = ffn(x5, 1, 1)
    dx6, loss_part = loss_head(x6, tgt)

    dx5 = ffn_back(x5, dx6, 1, 1)
    (dyh, drh, dkh, dvh, dgate, dx4), (d_lng, d_lnb, d_rk, d_wo) = tile_bwd(
        f_rw_post, "rw_post_bwd", post_tiles, post_w, [dx5], 128, [True] * 6, [True] * 4)
    drh2, dlwh, dkh2, dvh2, dah, dbh = rwkv_bwd(rh, lwh, kh, vh, ah, bh, states, dyh)
    mid_cts = [(drh, drh2), dlwh, (dkh, dkh2), (dvh, dvh2), dah, dbh, dgate]
    (dh, dhs, dr, dk, dv), dmid_w = tile_bwd(f_rw_mid, "rw_mid_bwd", mid_tiles, mid_w, mid_cts, 128,
                                             [True] * 5, [True] * len(mid_w))
    dmix = {}
    for nm, ct, row, wname in (("r", dr, 0, "rw_wr"), ("k", dk, 2, "rw_wk"), ("v", dv, 3, "rw_wv")):
        (dh, dhs), (dmix[row], G[wname]) = tile_bwd(
            f_rw_proj, f"rw_proj_{nm}_bwd", [h, hs], [mix[row:row + 1], W[wname]], [ct], 256,
            [True, True], [True, True], acc={0: dh, 1: dhs})
    dx4, G[("mix_norm", 1)] = norm_shift_bwd(x4, g1, dh, dhs, dx4)
    dmix3 = dmid_w[0]
    G["rw_mix"] = jnp.concatenate([dmix[0], dmix3[0:1], dmix[2], dmix[3], dmix3[1:2], dmix3[2:3]], axis=0)
    for nm, gv in zip(("rw_w0", "rw_a0", "rw_kk", "rw_ka", "rw_w1", "rw_w2", "rw_a1", "rw_a2", "rw_g1", "rw_g2"), dmid_w[1:]):
        G[nm] = gv
    G["rw_lnx_g"], G["rw_lnx_b"], G["rw_rk"], G["rw_wo"] = d_lng, d_lnb, d_rk, d_wo
    dx3 = ffn_back(x3, dx4, 1, 0)
    dx2 = ffn_back(x2, dx3, 0, 1)
    (dx1, doa, do_n, dlse_n), (G["attn_w_out"],) = tile_bwd(
        f_attn_out, "attn_out_bwd", [x1, oa, o_n, lse_n], [W["attn_w_out"]], [dx2], 256, [True] * 4, [True])
    do_s, dlse_s = reorder("sub_do", do_n, DIL, False), reorder("sub_dlse", dlse_n, DIL, False)
    dqs, dks, dvs, dsum = dil_bwd(qs, ks, vs, W["bias_mat"], o_s, lse_s, do_s, dlse_s)
    G["rel_bias"] = bias_grad(dsum, W["buckets"])
    dqn, dkn, dvv = (reorder(nm, t, DIL, True) for nm, t in (("nat_dq", dqs), ("nat_dk", dks), ("nat_dv", dvs)))
    ride, landed = grads_early(G) if grads_early is not None else (None, None)
    dsbq, dsbk, dsbv, *rode = sb_bwd(sbq, sbk, sbv, doa, ride)
    if landed is not None:
        landed(rode)
    dg0 = []
    dwin = []
    (dx1,), (dg, dw) = tile_bwd(f_attn_sb, "attn_in_sb_bwd", [x1], [g0, W["attn_w_in"][0]], [dsbq, dsbk, dsbv], 256,
                                [True], [True, True], acc={0: dx1})
    dg0.append(dg), dwin.append(dw)
    (dx1,), (dg, dw, G["attn_q_norm"]) = tile_bwd(f_attn_qk, "attn_in_q_bwd", [x1], [g0, W["attn_w_in"][1], W["attn_q_norm"]],
                                                  [dqn], 256, [True], [True] * 3, acc={0: dx1})
    dg0.append(dg), dwin.append(dw)
    (dx1,), (dg, dw, G["attn_k_norm"]) = tile_bwd(f_attn_qk, "attn_in_k_bwd", [x1], [g0, W["attn_w_in"][2], W["attn_k_norm"]],
                                                  [dkn], 256, [True], [True] * 3, acc={0: dx1})
    dg0.append(dg), dwin.append(dw)
    (dx1,), (dg, dw) = tile_bwd(f_attn_v, "attn_in_v_bwd", [x1], [g0, W["attn_w_in"][3]], [dvv], 256,
                                [True], [True, True], acc={0: dx1})
    dg0.append(dg), dwin.append(dw)
    G[("mix_norm", 0)] = dg0
    G["attn_w_in"] = dwin
    dx0 = ffn_back(x0, dx1, 0, 0)
    return loss_part, dx0, G


VEC_ROWS = ("ffn_norm", "rw_mix", "rw_w0", "rw_a0", "rw_kk", "rw_ka", "rw_lnx_g", "rw_lnx_b")


def kernel(x, ffn_norm, ffn_w_gate, ffn_w_up, ffn_w_down, mix_norm, rel_bias, attn_w_in, attn_q_norm, attn_k_norm, attn_w_out, rw_mix, rw_w0, rw_w1, rw_w2, rw_a0, rw_a1, rw_a2, rw_g1, rw_g2, rw_kk, rw_ka, rw_rk, rw_wr, rw_wk, rw_wv, rw_wo, rw_lnx_g, rw_lnx_b, loss_target, m_ffn_norm, m_ffn_w_gate, m_ffn_w_up, m_ffn_w_down, m_mix_norm, m_rel_bias, m_attn_w_in, m_attn_q_norm, m_attn_k_norm, m_attn_w_out, m_rw_mix, m_rw_w0, m_rw_w1, m_rw_w2, m_rw_a0, m_rw_a1, m_rw_a2, m_rw_g1, m_rw_g2, m_rw_kk, m_rw_ka, m_rw_rk, m_rw_wr, m_rw_wk, m_rw_wv, m_rw_wo, m_rw_lnx_g, m_rw_lnx_b, v_ffn_norm, v_ffn_w_gate, v_ffn_w_up, v_ffn_w_down, v_mix_norm, v_rel_bias, v_attn_w_in, v_attn_q_norm, v_attn_k_norm, v_attn_w_out, v_rw_mix, v_rw_w0, v_rw_w1, v_rw_w2, v_rw_a0, v_rw_a1, v_rw_a2, v_rw_g1, v_rw_g2, v_rw_kk, v_rw_ka, v_rw_rk, v_rw_wr, v_rw_wk, v_rw_wv, v_rw_wo, v_rw_lnx_g, v_rw_lnx_b):
    names = ["ffn_norm", "ffn_w_gate", "ffn_w_up", "ffn_w_down", "mix_norm", "rel_bias", "attn_w_in", "attn_q_norm",
             "attn_k_norm", "attn_w_out", "rw_mix", "rw_w0", "rw_w1", "rw_w2", "rw_a0", "rw_a1", "rw_a2", "rw_g1", "rw_g2",
             "rw_kk", "rw_ka", "rw_rk", "rw_wr", "rw_wk", "rw_wv", "rw_wo", "rw_lnx_g", "rw_lnx_b"]
    loc = locals()
    w = {n: loc[n] for n in names}
    mom = {n: loc["m_" + n] for n in names}
    vel = {n: loc["v_" + n] for n in names}
    S = x.shape[1]

    ffn3 = ("ffn_w_gate", "ffn_w_up", "ffn_w_down")
    rw_mats = ("rw_w1", "rw_w2", "rw_a1", "rw_a2", "rw_g1", "rw_g2", "rw_wr", "rw_wk", "rw_wv", "rw_wo")
    cols_split = ("attn_w_out", "rw_w2", "rw_a2", "rw_g2")
    shard = {"vec": jnp.concatenate([w[n].reshape(-1, 256) for n in VEC_ROWS], axis=0)}
    for n in ffn3:
        for l in range(2):
            for j in range(2):
                shard[n, l, j] = w[n][l, j].astype(BF16)
    for n in ("attn_w_in", "attn_w_out") + rw_mats:
        shard[n] = w[n].reshape(-1, w[n].shape[-1]).astype(BF16)
    ffn_keys = lambda l, j: [(n, l, j) for n in ffn3]
    w_groups = {"first": ["vec"] + ffn_keys(0, 0) + ["attn_w_in", "attn_w_out"],
                "sb_fwd": ffn_keys(0, 1) + ffn_keys(1, 0) + list(rw_mats),
                "dil_fwd": ffn_keys(1, 1)}
    label = lambda key: key if isinstance(key, str) else f"{key[0]}_{key[1]}{key[2]}"

    class Streamed(Weights):
        def ride(self, kernel_name):
            return gather_pushes([shard[k] for k in w_groups[kernel_name]])

        def arrived(self, kernel_name, outs):
            for key, g in zip(w_groups[kernel_name], gather_swap(f"gather_swap_{kernel_name}", outs)):
                if key == "vec":
                    vec_full = _unshard_cols(g)
                    self["ffn_norm"] = [[vec_full[2 * l + j][None] for j in range(2)] for l in range(2)]
                    self["rw_mix"] = vec_full[4:10]
                    for i, n in enumerate(("rw_w0", "rw_a0", "rw_kk", "rw_ka", "rw_lnx_g", "rw_lnx_b")):
                        self[n] = vec_full[10 + i][None]
                elif key == "attn_w_in":
                    self[key] = [g[p] for p in range(N_CHIPS)]
                elif key in cols_split:
                    self[key] = _unshard_cols(g)
                elif isinstance(key, str):
                    self[key] = g.reshape(D, -1)
                else:
                    self[key] = g

    buckets = _bucket_maps()
    W = Streamed({"mix_norm": [mix_norm[0:1], mix_norm[1:2]], "attn_q_norm": attn_q_norm, "attn_k_norm": attn_k_norm,
                  "rw_rk": rw_rk[0][:, None, :], "buckets": buckets, "bias_mat": bias_table(rel_bias, buckets)})
    W.arrived("first", exchange("gather_weights", W.ride("first")))

    def slots(key, G):
        if key == "vec":
            rows = [G[("ffn_norm", l, j)] for l in range(2) for j in range(2)] + [G["rw_mix"]] + \
                   [G[n] for n in ("rw_w0", "rw_a0", "rw_kk", "rw_ka", "rw_lnx_g", "rw_lnx_b")]
            return _shard_cols(jnp.concatenate(rows, axis=0))
        if key == "attn_w_in":
            return jnp.stack(G[key])
        if key in cols_split:
            return _shard_cols(G[key])
        if isinstance(key, str):
            return G[key].reshape(N_CHIPS, D // N_CHIPS, -1)
        return G[key]

    g_groups = {"early": ffn_keys(1, 1) + ffn_keys(1, 0) + ffn_keys(0, 1) + list(rw_mats) + ["attn_w_out"],
                "late": ["vec", "attn_w_in"] + ffn_keys(0, 0)}
    wire = lambda keys: [F32 if k == "vec" else BF16 for k in keys]
    part = {}

    def grads_early(G):
        keys = g_groups["early"]
        chip_sum, pushes = reduce_begin("early", [label(k) for k in keys], [slots(k, G) for k in keys], wire(keys))
        return pushes, lambda landed: part.update(zip(keys, reduce_end("early", [label(k) for k in keys], chip_sum, landed)))

    loss_part, dx, G = _forward_backward(x[0], loss_target[0], W, grads_early)
    loss = lax.psum(loss_part[0, 0], ("x", "y", "c"))
    keys = g_groups["late"]
    chip_sum, pushes = reduce_begin("late", [label(k) for k in keys], [slots(k, G) for k in keys], wire(keys))
    part.update(zip(keys, reduce_end("late", [label(k) for k in keys], chip_sum, exchange("scatter_grads", pushes))))
    for n in ffn3:
        part[n] = jnp.stack([jnp.stack([part[n, l, j] for j in range(2)]) for l in range(2)])

    rep = jnp.concatenate([G[("mix_norm", 0)][0] + G[("mix_norm", 0)][1] + G[("mix_norm", 0)][2] + G[("mix_norm", 0)][3],
                           G[("mix_norm", 1)]], axis=0).reshape(16, 128)
    rep = jnp.concatenate([rep, G["rel_bias"], jnp.pad(G["attn_q_norm"], ((0, 0), (0, 64))),
                           jnp.pad(G["attn_k_norm"], ((0, 0), (0, 64))), G["rw_rk"].reshape(8, 128),
                           jnp.zeros((2, 128), F32)], axis=0)
    rep_sum = sum_slots("sum_replicated", gather_all([rep])[0])
    g_rep = {
        "mix_norm": rep_sum[0:16].reshape(2, D),
        "rel_bias": jnp.transpose(rep_sum[16:28, :N_BUCKETS]),
        "attn_q_norm": rep_sum[28:29, :HEAD], "attn_k_norm": rep_sum[29:30, :HEAD],
        "rw_rk": rep_sum[30:38].reshape(1, RW_H, HEAD),
    }

    out = {}

    def adam(n, ga, gb):
        shp = w[n].shape
        to2 = lambda a: a.reshape(-1, shp[-1])
        res = adam_step(f"adam_{n}", to2(ga), None if gb is None else to2(gb), to2(w[n]), to2(mom[n]), to2(vel[n]))
        out[n] = tuple(r.reshape(shp) for r in res)

    for n in ffn3 + ("attn_w_in", "attn_w_out") + rw_mats:
        adam(n, part[n], None)
    rows = {"ffn_norm": (0, 4), "rw_mix": (4, 10), "rw_w0": (10, 11), "rw_a0": (11, 12), "rw_kk": (12, 13),
            "rw_ka": (13, 14), "rw_lnx_g": (14, 15), "rw_lnx_b": (15, 16)}
    for n, (lo, hi) in rows.items():
        adam(n, part["vec"][lo:hi], None)
    for n, gv in g_rep.items():
        adam(n, gv, None)

    grads = [out[n][0] for n in names]
    deltas = [out[n][1] for n in names]
    new_m = [out[n][2] for n in names]
    new_v = [out[n][3] for n in names]
    return (loss, dx[None], *grads, *deltas, *new_m, *new_v)
```

```python
import functools
import math

import jax
import jax.numpy as jnp
from jax import lax
from jax.experimental import pallas as pl
from jax.experimental.pallas import tpu as pltpu

F32, BF16 = jnp.float32, jnp.bfloat16
HI = lax.Precision.HIGHEST
MESH = pl.DeviceIdType.MESH

D = 1024
HEAD = 64
N_CHIPS = 4
FF_SHARD = 704
SB_W = 256
DL_HEADS = 12
DL_PAIRS = 6
DIL = (1, 4, 16)
QBLK = 128
N_BUCKETS = 32
MAX_DISTANCE = 2048
RW_H = 16
RW_CHUNK = 64
NORM_EPS = 1e-6
GN_EPS = 64e-5
NEG_INF = -1e30
VMEM_LIMIT = 56 * 1024 * 1024

ADAM_LR, ADAM_B1, ADAM_B2, ADAM_EPS, ADAM_WD, ADAM_STEP = 0.001, 0.9, 0.999, 1e-08, 0.01, 10


def _cp(sem):
    return pltpu.CompilerParams(dimension_semantics=sem, vmem_limit_bytes=VMEM_LIMIT)


def _dg(a, b, dims, prec=None):
    return lax.dot_general(a, b, (dims, ((), ())), precision=prec, preferred_element_type=F32)


def _bdot(a, b, dims):
    return _dg(a.astype(BF16), b.astype(BF16), dims)


@jax.custom_vjp
def mm(a, b):
    return _bdot(a, b, ((1,), (0,)))


def _mm_fwd(a, b):
    return _bdot(a, b, ((1,), (0,))), (a, b)


def _mm_bwd(res, g):
    a, b = res
    return _bdot(g, b, ((1,), (1,))), _bdot(a, g, ((0,), (0,)))


mm.defvjp(_mm_fwd, _mm_bwd)


def rms(x, g):
    return x * lax.rsqrt(jnp.mean(x * x, axis=-1, keepdims=True) + NORM_EPS) * g


def group_sum(x, nh):
    w = x.shape[-1]
    e = (lax.broadcasted_iota(jnp.int32, (w, nh), 0) // HEAD == lax.broadcasted_iota(jnp.int32, (w, nh), 1)).astype(F32)
    s = _dg(x, e, ((1,), (0,)), HI)
    return _dg(s, e, ((1,), (1,)), HI)


def softplus(u):
    return jnp.maximum(u, 0.0) + jnp.log1p(jnp.exp(-jnp.abs(u)))


def to_heads(t, nh=RW_H):
    return jnp.stack([t[:, HEAD * h:HEAD * (h + 1)] for h in range(nh)])


def from_heads(t):
    return jnp.concatenate([t[h] for h in range(t.shape[0])], axis=-1)


def _tile_spec(shape, tm):
    if len(shape) == 2:
        return pl.BlockSpec((tm, shape[1]), lambda t: (t, 0))
    return pl.BlockSpec((shape[0], tm, shape[2]), lambda t: (0, t, 0))


def _full_spec(shape):
    nd = len(shape)
    return pl.BlockSpec(tuple(shape), lambda t: (0,) * nd)


def _rows(a):
    return a.shape[0] if a.ndim == 2 else a.shape[1]


def tile_fwd(f, name, tiles, weights, outs, tm):
    nt, nw = len(tiles), len(weights)

    def body(*refs):
        tv = [r[...] for r in refs[:nt]]
        wv = [r[...].astype(F32) for r in refs[nt:nt + nw]]
        res = f(*tv, *wv)
        if not isinstance(res, (tuple, list)):
            res = (res,)
        for o, v in zip(refs[nt + nw:], res):
            o[...] = v.astype(o.dtype)

    return pl.pallas_call(
        body, name=name, grid=(_rows(tiles[0]) // tm,),
        in_specs=[_tile_spec(a.shape, tm) for a in tiles] + [_full_spec(w.shape) for w in weights],
        out_specs=[_tile_spec(o.shape, tm) for o in outs],
        out_shape=list(outs),
        compiler_params=_cp(("parallel",)),
    )(*tiles, *weights)


def tile_bwd(f, name, tiles, weights, cts, tm, dt, dw, acc=None):
    acc = acc or {}
    groups = [c if isinstance(c, tuple) else (c,) for c in cts]
    cts = [a for grp in groups for a in grp]
    nt, nw, nc = len(tiles), len(weights), len(cts)
    acc_idx = sorted(acc)
    na = len(acc_idx)
    dti = [i for i in range(nt) if dt[i]]
    dwi = [i for i in range(nw) if dw[i]]

    def body(*refs):
        tv = [r[...] for r in refs[:nt]]
        wv = [r[...].astype(F32) for r in refs[nt:nt + nw]]
        crefs = list(refs[nt + nw:nt + nw + nc])
        cv = []
        for grp in groups:
            terms = [crefs.pop(0)[...] for _ in grp]
            cv.append(functools.reduce(lambda a, b: a + b, terms))
        av = {i: r[...] for i, r in zip(acc_idx, refs[nt + nw + nc:nt + nw + nc + na])}
        orefs = refs[nt + nw + nc + na:]

        def g(*diff):
            t2, w2 = list(tv), list(wv)
            for i, v in zip(dti, diff[:len(dti)]):
                t2[i] = v
            for i, v in zip(dwi, diff[len(dti):]):
                w2[i] = v
            res = f(*t2, *w2)
            return tuple(res) if isinstance(res, (tuple, list)) else (res,)

        _, vjp = jax.vjp(g, *[tv[i] for i in dti], *[wv[i] for i in dwi])
        grads = vjp(tuple(cv))
        for k, i in enumerate(dti):
            gt = grads[k]
            if i in av:
                gt = gt + av[i]
            orefs[k][...] = gt
        first = pl.program_id(0) == 0
        for k, i in enumerate(dwi):
            o = orefs[len(dti) + k]
            gw = grads[len(dti) + k]

            @pl.when(first)
            def _(o=o, gw=gw):
                o[...] = gw

            @pl.when(jnp.logical_not(first))
            def _(o=o, gw=gw):
                o[...] += gw

    out_shape = [jax.ShapeDtypeStruct(tiles[i].shape, F32) for i in dti] + \
                [jax.ShapeDtypeStruct(weights[i].shape, F32) for i in dwi]
    res = pl.pallas_call(
        body, name=name, grid=(_rows(tiles[0]) // tm,),
        in_specs=[_tile_spec(a.shape, tm) for a in tiles] + [_full_spec(w.shape) for w in weights] +
                 [_tile_spec(c.shape, tm) for c in cts] + [_tile_spec(tiles[i].shape, tm) for i in acc_idx],
        out_specs=[_tile_spec(tiles[i].shape, tm) for i in dti] + [_full_spec(weights[i].shape) for i in dwi],
        out_shape=out_shape,
        compiler_params=_cp(("arbitrary",)),
    )(*tiles, *weights, *cts, *[acc[i] for i in acc_idx])
    return list(res[:len(dti)]), list(res[len(dti):])


def _ffn_wspec(rows, cols, cfirst):
    if cfirst:
        return pl.BlockSpec((1, rows, cols), lambda c, t: (c, 0, 0))
    return pl.BlockSpec((1, rows, cols), lambda t, c: (c, 0, 0))


def ffn_fwd(x, g, wg, wu, wd, l, j, tm=512):
    S = x.shape[0]

    def body(x_ref, g_ref, wg_ref, wu_ref, wd_ref, o_ref, h_ref, acc_ref):
        c = pl.program_id(1)

        @pl.when(c == 0)
        def _():
            h_ref[...] = rms(x_ref[...], g_ref[...]).astype(BF16)
            acc_ref[...] = jnp.zeros_like(acc_ref)

        h = h_ref[...]
        a = _bdot(h, wg_ref[0], ((1,), (0,)))
        b = _bdot(h, wu_ref[0], ((1,), (0,)))
        y = a * jax.nn.sigmoid(a) * b
        acc_ref[...] += _bdot(y, wd_ref[0], ((1,), (0,)))

        @pl.when(c == N_CHIPS - 1)
        def _():
            o_ref[...] = x_ref[...] + 0.5 * acc_ref[...]

    return pl.pallas_call(
        body, name=f"ffn_fwd_{l}{j}", grid=(S // tm, N_CHIPS),
        in_specs=[pl.BlockSpec((tm, D), lambda t, c: (t, 0)), pl.BlockSpec((1, D), lambda t, c: (0, 0)),
                  _ffn_wspec(D, FF_SHARD, False), _ffn_wspec(D, FF_SHARD, False), _ffn_wspec(FF_SHARD, D, False)],
        out_specs=pl.BlockSpec((tm, D), lambda t, c: (t, 0)),
        out_shape=jax.ShapeDtypeStruct((S, D), F32),
        scratch_shapes=[pltpu.VMEM((tm, D), BF16), pltpu.VMEM((tm, D), F32)],
        compiler_params=_cp(("parallel", "arbitrary")),
    )(x, g, wg, wu, wd)


def ffn_bwd(x, g, wg, wu, wd, dout, l, j, tm=512):
    S = x.shape[0]

    def body(x_ref, g_ref, wg_ref, wu_ref, wd_ref, do_ref, dh_ref, dwg_ref, dwu_ref, dwd_ref):
        t = pl.program_id(1)
        h = rms(x_ref[...], g_ref[...]).astype(BF16)
        wgv, wuv, wdv = wg_ref[0], wu_ref[0], wd_ref[0]
        a = _bdot(h, wgv, ((1,), (0,)))
        b = _bdot(h, wuv, ((1,), (0,)))
        sig = jax.nn.sigmoid(a)
        s = a * sig
        dyd = 0.5 * do_ref[...]
        dy = _bdot(dyd, wdv, ((1,), (1,)))
        dwd = _bdot(s * b, dyd, ((0,), (0,)))
        db = dy * s
        da = dy * b * (sig * (1.0 + a * (1.0 - sig)))
        dwg = _bdot(h, da, ((0,), (0,)))
        dwu = _bdot(h, db, ((0,), (0,)))
        dh_ref[0] = _bdot(da, wgv, ((1,), (1,))) + _bdot(db, wuv, ((1,), (1,)))

        @pl.when(t == 0)
        def _():
            dwg_ref[0] = dwg
            dwu_ref[0] = dwu
            dwd_ref[0] = dwd

        @pl.when(t != 0)
        def _():
            dwg_ref[0] += dwg
            dwu_ref[0] += dwu
            dwd_ref[0] += dwd

    return pl.pallas_call(
        body, name=f"ffn_bwd_{l}{j}", grid=(N_CHIPS, S // tm),
        in_specs=[pl.BlockSpec((tm, D), lambda c, t: (t, 0)), pl.BlockSpec((1, D), lambda c, t: (0, 0)),
                  _ffn_wspec(D, FF_SHARD, True), _ffn_wspec(D, FF_SHARD, True), _ffn_wspec(FF_SHARD, D, True),
                  pl.BlockSpec((tm, D), lambda c, t: (t, 0))],
        out_specs=[pl.BlockSpec((1, tm, D), lambda c, t: (c, t, 0)),
                   _ffn_wspec(D, FF_SHARD, True), _ffn_wspec(D, FF_SHARD, True), _ffn_wspec(FF_SHARD, D, True)],
        out_shape=[jax.ShapeDtypeStruct((N_CHIPS, S, D), F32)] + [jax.ShapeDtypeStruct(a.shape, F32) for a in (wg, wu, wd)],
        compiler_params=_cp(("parallel", "arbitrary")),
    )(x, g, wg, wu, wd, dout)


def norm_bwd(name, x, g, dh_parts, dres, tm=256):
    S = x.shape[0]
    P = dh_parts.shape[0]

    def body(x_ref, g_ref, dh_ref, dr_ref, dx_ref, dg_ref):
        dh = dh_ref[0]
        for p in range(1, P):
            dh = dh + dh_ref[p]
        _, vjp = jax.vjp(rms, x_ref[...], g_ref[...])
        dx, dg = vjp(dh)
        dx_ref[...] = dr_ref[...] + dx

        @pl.when(pl.program_id(0) == 0)
        def _():
            dg_ref[...] = dg

        @pl.when(pl.program_id(0) != 0)
        def _():
            dg_ref[...] += dg

    return pl.pallas_call(
        body, name=name, grid=(S // tm,),
        in_specs=[pl.BlockSpec((tm, D), lambda t: (t, 0)), pl.BlockSpec((1, D), lambda t: (0, 0)),
                  pl.BlockSpec((P, tm, D), lambda t: (0, t, 0)), pl.BlockSpec((tm, D), lambda t: (t, 0))],
        out_specs=[pl.BlockSpec((tm, D), lambda t: (t, 0)), pl.BlockSpec((1, D), lambda t: (0, 0))],
        out_shape=[jax.ShapeDtypeStruct((S, D), F32), jax.ShapeDtypeStruct((1, D), F32)],
        compiler_params=_cp(("arbitrary",)),
    )(x, g, dh_parts, dres)


def f_attn_sb(x, g, w):
    pr = mm(rms(x, g), w)
    return pr[:, :SB_W], pr[:, SB_W:2 * SB_W], pr[:, 2 * SB_W:]


def _pairs(y):
    return jnp.stack([y[:, 128 * j:128 * (j + 1)] for j in range(DL_PAIRS)])


def f_attn_qk(x, g, w, nrm):
    pr = mm(rms(x, g), w)
    ms = group_sum(pr * pr, DL_HEADS) * (1.0 / HEAD)
    return _pairs(pr * lax.rsqrt(ms + NORM_EPS) * jnp.concatenate([nrm] * DL_HEADS, axis=1))


def f_attn_v(x, g, w):
    return _pairs(mm(rms(x, g), w))


def _masked(strict, x):
    return x if strict is None else jnp.where(strict, x, 0.0)


def _head_stack(x):
    nh = x.shape[1] // HEAD
    lane_head = lax.broadcasted_iota(jnp.int32, (1, x.shape[1]), 1) // HEAD
    return jnp.concatenate([jnp.where(lane_head == h, x, 0.0) for h in range(nh)], axis=0).astype(BF16)


def _head_pick(xs):
    nh = xs.shape[1] // HEAD
    rows = xs.shape[0] // nh
    lane_head = lax.broadcasted_iota(jnp.int32, (1, xs.shape[1]), 1) // HEAD
    out = xs[:rows]
    for h in range(1, nh):
        out = jnp.where(lane_head == h, xs[rows * h:rows * (h + 1)], out)
    return out


def _sb_tiles(qs, kblk, strict):
    z = _dg(qs, kblk, ((1,), (1,))) * (HEAD ** -0.5)
    keep = -(jnp.maximum(z, 0.0) + jnp.log(1.0 + jnp.exp(-jnp.abs(z))))
    return z, _masked(strict, keep)


def _tri(n, upper):
    r = lax.broadcasted_iota(jnp.int32, (n, n), 0)
    c = lax.broadcasted_iota(jnp.int32, (n, n), 1)
    return ((r > c) if upper else (r < c)).astype(BF16)


def _tri_sums(x, tri):
    hi, lo = _split2(x)
    return _dg(hi, tri, ((1,), (0,))) + _dg(lo, tri, ((1,), (0,)))


SB_UNROLL = 4


def _sb_diag(tb, nh):
    r = lax.broadcasted_iota(jnp.int32, (nh * tb, tb), 0)
    return lax.broadcasted_iota(jnp.int32, (nh * tb, tb), 1) < lax.rem(r, tb)


def _sb_sweep(step, first, count, carry, direction, commit=None):
    def run(kbs, c):
        outs = []
        for kb in kbs:
            c, out = step(kb, c)
            outs.append(out)
        if commit is not None:
            for kb, out in zip(kbs, outs):
                commit(kb, out)
        return c

    rem = count % SB_UNROLL
    carry = lax.fori_loop(0, rem, lambda i, c: run([first + direction * i], c), carry)
    return lax.fori_loop(
        0, count // SB_UNROLL,
        lambda g, c: run([first + direction * (rem + SB_UNROLL * g + u) for u in range(SB_UNROLL)], c), carry)


def _riding(ride, refs, n_in, n_out, first, last):
    if ride is None:
        return refs, lambda: None
    n = ride.n
    own = refs[:n_in] + refs[n_in + n:n_in + n + n_out] + refs[n_in + 2 * n + n_out:len(refs) - 2]
    start, wait = ride.ops(refs[n_in:n_in + n], refs[n_in + n + n_out:n_in + 2 * n + n_out], refs[-2], refs[-1])
    pl.when(first)(start)
    return own, lambda: pl.when(last)(wait)


def _ride_specs(ride):
    if ride is None:
        return [], [], [], [], []
    return [_HBM] * ride.n, [_HBM] * ride.n, ride.out_shapes, ride.sem_shapes(), ride.arrays


def sb_fwd(q, k, v, ride=None, tb=QBLK):
    S = q.shape[0]
    nh = SB_W // HEAD
    r_in, r_out, r_shape, r_scr, r_args = _ride_specs(ride)

    def body(*refs):
        qb = pl.program_id(0)
        (q_ref, k_ref, v_ref, o_ref), finish = _riding(ride, refs, 3, 1, qb == 0, qb == S // tb - 1)
        diag = _sb_diag(tb, nh)
        after_mat = _tri(tb, True)
        qs = _head_stack(q_ref[...])

        def step(kb, carry, strict):
            acc, run = carry
            rows = pl.ds(pl.multiple_of(kb * tb, tb), tb)
            z, keep = _sb_tiles(qs, k_ref[rows, :].astype(BF16), strict)
            w = _masked(strict, jnp.exp(z + keep + _tri_sums(keep, after_mat) + run))
            acc = acc + _dg(w.astype(BF16), v_ref[rows, :].astype(BF16), ((1,), (0,)))
            return acc, run + jnp.sum(keep, axis=1, keepdims=True)

        init = (jnp.zeros((nh * tb, SB_W), F32), jnp.zeros((nh * tb, 1), F32))
        carry = step(qb, init, diag)
        acc, _ = _sb_sweep(lambda kb, c: (step(kb, c, None), None), qb - 1, qb, carry, -1)
        o_ref[...] = _head_pick(acc)
        finish()

    return pl.pallas_call(
        body, name="sb_fwd", grid=(S // tb,),
        in_specs=[pl.BlockSpec((tb, SB_W), lambda i: (i, 0)), pl.BlockSpec((S, SB_W), lambda i: (0, 0)),
                  pl.BlockSpec((S, SB_W), lambda i: (0, 0))] + r_in,
        out_specs=[pl.BlockSpec((tb, SB_W), lambda i: (i, 0))] + r_out,
        out_shape=[jax.ShapeDtypeStruct((S, SB_W), F32)] + r_shape,
        scratch_shapes=r_scr,
        compiler_params=_cp(("arbitrary",)),
    )(q, k, v, *r_args)


def sb_bwd(q, k, v, do, ride=None, tb=QBLK):
    S = q.shape[0]
    nh = SB_W // HEAD
    scale = HEAD ** -0.5
    r_in, r_out, r_shape, r_scr, r_args = _ride_specs(ride)

    def body(*refs):
        qb = pl.program_id(0)
        (q_ref, k_ref, v_ref, do_ref, dq_ref, dk_ref, dv_ref, g_scr), finish = _riding(
            ride, refs, 4, 3, qb == 0, qb == S // tb - 1)

        @pl.when(qb == 0)
        def _():
            dk_ref[...] = jnp.zeros_like(dk_ref)
            dv_ref[...] = jnp.zeros_like(dv_ref)

        diag = _sb_diag(tb, nh)
        after_mat = _tri(tb, True)
        before_mat = _tri(tb, False)
        qs = _head_stack(q_ref[...])
        dos = _head_stack(do_ref[...])

        def right_to_left(kb, run, strict):
            rows = pl.ds(pl.multiple_of(kb * tb, tb), tb)
            z, keep = _sb_tiles(qs, k_ref[rows, :].astype(BF16), strict)
            w = _masked(strict, jnp.exp(z + keep + _tri_sums(keep, after_mat) + run))
            g_scr[kb] = _dg(dos, v_ref[rows, :].astype(BF16), ((1,), (1,))) * w
            dv = _dg(w.astype(BF16), dos, ((0,), (0,)))
            return run + jnp.sum(keep, axis=1, keepdims=True), dv

        def add_rows(ref):
            def commit(kb, val):
                ref[pl.ds(pl.multiple_of(kb * tb, tb), tb), :] += val
            return commit

        zero_run = jnp.zeros((nh * tb, 1), F32)
        run, dv_diag = right_to_left(qb, zero_run, diag)
        add_rows(dv_ref)(qb, dv_diag)
        _sb_sweep(lambda kb, r: right_to_left(kb, r, None), qb - 1, qb, run, -1, add_rows(dv_ref))

        def left_to_right(kb, carry, strict):
            dq, run = carry
            rows = pl.ds(pl.multiple_of(kb * tb, tb), tb)
            kblk = k_ref[rows, :].astype(BF16)
            gw = g_scr[kb]
            sig = jax.nn.sigmoid(_dg(qs, kblk, ((1,), (1,))) * scale)
            dkeep = _masked(strict, _tri_sums(gw, before_mat) + run)
            dz = ((gw * (1.0 - sig) - dkeep * sig) * scale).astype(BF16)
            dq = dq + _dg(dz, kblk, ((1,), (0,)))
            return (dq, run + jnp.sum(gw, axis=1, keepdims=True)), _dg(dz, qs, ((0,), (0,)))

        carry = _sb_sweep(lambda kb, c: left_to_right(kb, c, None), 0, qb,
                          (jnp.zeros((nh * tb, SB_W), F32), zero_run), 1, add_rows(dk_ref))
        (dq, _), dk_diag = left_to_right(qb, carry, diag)
        add_rows(dk_ref)(qb, dk_diag)
        dq_ref[...] = _head_pick(dq)
        finish()

    whole = pl.BlockSpec((S, SB_W), lambda i: (0, 0))
    blk = pl.BlockSpec((tb, SB_W), lambda i: (i, 0))
    return pl.pallas_call(
        body, name="sb_bwd", grid=(S // tb,),
        in_specs=[blk, whole, whole, blk] + r_in, out_specs=[blk, whole, whole] + r_out,
        out_shape=[jax.ShapeDtypeStruct((S, SB_W), F32)] * 3 + r_shape,
        scratch_shapes=[pltpu.VMEM((S // tb, nh * tb, tb), F32)] + r_scr,
        compiler_params=_cp(("arbitrary",)),
    )(q, k, v, do, *r_args)


def reorder(name, x, groups, inverse):
    P, S, _ = x.shape

    def body(x_ref, o_ref):
        p = pl.program_id(0)
        for gi, r in enumerate(groups):
            @pl.when(p // 2 == gi)
            def _(r=r):
                L = S // r
                if r == 1:
                    o_ref[...] = x_ref[...]
                for c in range(r if r > 1 else 0):
                    if inverse:
                        o_ref[pl.ds(c, L, stride=r), :] = x_ref[c * L:(c + 1) * L, :]
                    else:
                        o_ref[c * L:(c + 1) * L, :] = x_ref[pl.ds(c, L, stride=r), :]

    slab = pl.BlockSpec((None, S, 128), lambda p: (p, 0, 0))
    return pl.pallas_call(
        body, name=name, grid=(P,), in_specs=[slab], out_specs=slab,
        out_shape=jax.ShapeDtypeStruct(x.shape, x.dtype), compiler_params=_cp(("parallel",)),
    )(x)


def _dil_blocks(S):
    return S // QBLK


def _dil_mask(n_in_stream):
    qi = lax.broadcasted_iota(jnp.int32, (QBLK, 2 * QBLK), 0)
    kj = lax.broadcasted_iota(jnp.int32, (QBLK, 2 * QBLK), 1) - QBLK
    dist = qi - kj
    return (dist >= 0) & (dist <= QBLK) & ((n_in_stream > 0) | (kj >= 0))


def _stream_pos(gi, i, S):
    nb = jnp.where(gi == 0, S // (QBLK * DIL[0]), jnp.where(gi == 1, S // (QBLK * DIL[1]), S // (QBLK * DIL[2])))
    return i % nb


def dil_fwd(q, k, v, bias, ride=None):
    S = q.shape[1]
    nblk = _dil_blocks(S)
    r_in, r_out, r_shape, r_scr, r_args = _ride_specs(ride)

    def body(*refs):
        gi, i = pl.program_id(0), pl.program_id(1)
        (q_ref, kc_ref, kp_ref, vc_ref, vp_ref, b_ref, o_ref, l_ref), finish = _riding(
            ride, refs, 6, 2, (gi == 0) & (i == 0), (gi == len(DIL) - 1) & (i == nblk - 1))
        mask = _dil_mask(_stream_pos(gi, i, S))
        for j in range(2):
            q2, kc, kp, vc, vp = q_ref[j], kc_ref[j], kp_ref[j], vc_ref[j], vp_ref[j]
            os_, ls_ = [], []
            for hh in range(2):
                sl = slice(HEAD * hh, HEAD * (hh + 1))
                kw = jnp.concatenate([kp[:, sl], kc[:, sl]], axis=0)
                vw = jnp.concatenate([vp[:, sl], vc[:, sl]], axis=0)
                lg = _bdot(q2[:, sl], kw, ((1,), (1,))) * (HEAD ** -0.5) + b_ref[2 * j + hh]
                lg = jnp.where(mask, lg, NEG_INF)
                m = jnp.max(lg, axis=-1, keepdims=True)
                p = jnp.exp(lg - m)
                den = jnp.sum(p, axis=-1, keepdims=True)
                os_.append(_bdot(p / den, vw, ((1,), (0,))))
                ls_.append(jnp.broadcast_to(m + jnp.log(den), (QBLK, HEAD)))
            o_ref[j] = jnp.concatenate(os_, axis=1)
            l_ref[j] = jnp.concatenate(ls_, axis=1)
        finish()

    cur = pl.BlockSpec((2, QBLK, 128), lambda g, i: (g, i, 0))
    prev = pl.BlockSpec((2, QBLK, 128), lambda g, i: (g, jnp.maximum(i - 1, 0), 0))
    return pl.pallas_call(
        body, name="dil_fwd", grid=(len(DIL), nblk),
        in_specs=[cur, cur, prev, cur, prev, pl.BlockSpec((4, QBLK, 2 * QBLK), lambda g, i: (g, 0, 0))] + r_in,
        out_specs=[cur, cur] + r_out,
        out_shape=[jax.ShapeDtypeStruct(q.shape, F32)] * 2 + r_shape,
        scratch_shapes=r_scr,
        compiler_params=_cp(("arbitrary", "arbitrary")),
    )(q, k, k, v, v, bias, *r_args)


def dil_bwd(q, k, v, bias, o, lse, do, dlse):
    S = q.shape[1]
    nblk = _dil_blocks(S)

    def body(q_ref, kc_ref, kp_ref, vc_ref, vp_ref, b_ref, o_ref, l_ref, do_ref, dl_ref,
             dq_ref, dk_ref, dv_ref, ds_ref, dk_car, dv_car):
        gi, i = pl.program_id(0), pl.program_id(1)

        @pl.when(i == 0)
        def _():
            ds_ref[...] = jnp.zeros_like(ds_ref)
            dk_car[...] = jnp.zeros_like(dk_car)
            dv_car[...] = jnp.zeros_like(dv_car)

        @pl.when(i < nblk)
        def _():
            mask = _dil_mask(_stream_pos(gi, i, S))
            for j in range(2):
                q2, kc, kp, vc, vp = q_ref[j], kc_ref[j], kp_ref[j], vc_ref[j], vp_ref[j]
                o2, l2, do2, dl2 = o_ref[j], l_ref[j], do_ref[j], dl_ref[j]
                dqs, dkps, dkcs, dvps, dvcs = [], [], [], [], []
                for hh in range(2):
                    sl = slice(HEAD * hh, HEAD * (hh + 1))
                    qh, doh = q2[:, sl], do2[:, sl]
                    kw = jnp.concatenate([kp[:, sl], kc[:, sl]], axis=0)
                    vw = jnp.concatenate([vp[:, sl], vc[:, sl]], axis=0)
                    lg = _bdot(qh, kw, ((1,), (1,))) * (HEAD ** -0.5) + b_ref[2 * j + hh]
                    p = jnp.where(mask, jnp.exp(lg - l2[:, HEAD * hh:HEAD * hh + 1]), 0.0)
                    dp = _bdot(doh, vw, ((1,), (1,)))
                    delta = jnp.sum(doh * o2[:, sl], axis=-1, keepdims=True)
                    dl = jnp.sum(dl2[:, sl], axis=-1, keepdims=True)
                    ds = p * (dp - delta + dl)
                    ds_ref[2 * j + hh] += ds
                    dsq = ds * (HEAD ** -0.5)
                    dqs.append(_bdot(dsq, kw, ((1,), (0,))))
                    dkw = _bdot(dsq, qh, ((0,), (0,)))
                    dvw = _bdot(p, doh, ((0,), (0,)))
                    dkps.append(dkw[:QBLK])
                    dkcs.append(dkw[QBLK:])
                    dvps.append(dvw[:QBLK])
                    dvcs.append(dvw[QBLK:])
                dq_ref[j] = jnp.concatenate(dqs, axis=1)
                dk_ref[j] = dk_car[j] + jnp.concatenate(dkps, axis=1)
                dv_ref[j] = dv_car[j] + jnp.concatenate(dvps, axis=1)
                dk_car[j] = jnp.concatenate(dkcs, axis=1)
                dv_car[j] = jnp.concatenate(dvcs, axis=1)

        @pl.when(i == nblk)
        def _():
            dk_ref[...] = dk_car[...]
            dv_ref[...] = dv_car[...]

    cur = pl.BlockSpec((2, QBLK, 128), lambda g, i: (g, jnp.minimum(i, nblk - 1), 0))
    prev = pl.BlockSpec((2, QBLK, 128), lambda g, i: (g, jnp.clip(i - 1, 0, nblk - 1), 0))
    bspec = pl.BlockSpec((4, QBLK, 2 * QBLK), lambda g, i: (g, 0, 0))
    return pl.pallas_call(
        body, name="dil_bwd", grid=(len(DIL), nblk + 1),
        in_specs=[cur, cur, prev, cur, prev, bspec, cur, cur, cur, cur],
        out_specs=[cur, prev, prev, bspec],
        out_shape=[jax.ShapeDtypeStruct(q.shape, F32)] * 3 + [jax.ShapeDtypeStruct(bias.shape, F32)],
        scratch_shapes=[pltpu.VMEM((2, QBLK, 128), F32), pltpu.VMEM((2, QBLK, 128), F32)],
        compiler_params=_cp(("arbitrary", "arbitrary")),
    )(q, k, k, v, v, bias, o, lse, do, dlse)


def _t5_bucket(dist):
    max_exact = N_BUCKETS // 2
    d = jnp.maximum(dist, 1).astype(F32)
    large = max_exact + (jnp.log(d / max_exact) / math.log(MAX_DISTANCE / max_exact)
                         * (N_BUCKETS - max_exact)).astype(jnp.int32)
    large = jnp.minimum(large, N_BUCKETS - 1)
    return jnp.where(dist < max_exact, dist, large)


def _bucket_maps():
    qi = jnp.arange(QBLK)[:, None]
    kj = jnp.arange(2 * QBLK)[None, :] - QBLK
    dist = jnp.maximum(qi - kj, 0)
    return jnp.stack([_t5_bucket(dist * r) for r in DIL])


def bias_table(rel_bias, buckets):
    def body(tbl_ref, bk_ref, o_ref):
        for h in range(DL_HEADS):
            bk = bk_ref[h // 4]

            def step(b, acc):
                return jnp.where(bk == b, tbl_ref[b, h], acc)

            o_ref[h] = lax.fori_loop(0, N_BUCKETS, step, jnp.zeros(bk.shape, F32))

    return pl.pallas_call(
        body, name="bias_table", out_shape=jax.ShapeDtypeStruct((DL_HEADS,) + buckets.shape[1:], F32),
        in_specs=[pl.BlockSpec(memory_space=pltpu.SMEM), pl.BlockSpec(memory_space=pltpu.VMEM)],
        out_specs=pl.BlockSpec(memory_space=pltpu.VMEM),
    )(rel_bias, buckets)


def bias_grad(ds, buckets):
    def body(ds_ref, bk_ref, o_ref):
        lane = lax.broadcasted_iota(jnp.int32, (1, 128), 1)
        for h in range(DL_HEADS):
            dsv = ds_ref[h]
            bk = bk_ref[h // 4]

            def step(b, row):
                return jnp.where(lane == b, jnp.sum(jnp.where(bk == b, dsv, 0.0)), row)

            o_ref[h:h + 1, :] = lax.fori_loop(0, N_BUCKETS, step, jnp.zeros((1, 128), F32))

    return pl.pallas_call(
        body, name="bias_grad", out_shape=jax.ShapeDtypeStruct((DL_HEADS, 128), F32),
        in_specs=[pl.BlockSpec(memory_space=pltpu.VMEM)] * 2, out_specs=pl.BlockSpec(memory_space=pltpu.VMEM),
    )(ds, buckets)


def f_attn_out(x, oa, o, lse, w):
    og = [jnp.concatenate([o[2 * g], o[2 * g + 1]], axis=1) for g in range(3)]
    lg = [jnp.concatenate([lse[2 * g], lse[2 * g + 1]], axis=1) for g in range(3)]
    m = jnp.maximum(jnp.maximum(lg[0], lg[1]), lg[2])
    e = [jnp.exp(l - m) for l in lg]
    den = e[0] + e[1] + e[2]
    ob = (e[0] * og[0] + e[1] * og[1] + e[2] * og[2]) / den
    return x + mm(jnp.concatenate([oa, ob], axis=1), w)


def norm_shift_fwd(x, g, tm=256):
    S = x.shape[0]

    def body(x_ref, xp_ref, g_ref, h_ref, hs_ref):
        h = rms(x_ref[...], g_ref[...])
        hp = rms(xp_ref[7:8, :], g_ref[...])
        hp = jnp.where(pl.program_id(0) == 0, 0.0, hp)
        row = lax.broadcasted_iota(jnp.int32, (tm, D), 0)
        h_ref[...] = h
        hs_ref[...] = jnp.where(row == 0, hp, pltpu.roll(h, 1, 0))

    return pl.pallas_call(
        body, name="rw_norm_shift", grid=(S // tm,),
        in_specs=[pl.BlockSpec((tm, D), lambda t: (t, 0)),
                  pl.BlockSpec((8, D), lambda t: (jnp.maximum(t * (tm // 8) - 1, 0), 0)),
                  pl.BlockSpec((1, D), lambda t: (0, 0))],
        out_specs=[pl.BlockSpec((tm, D), lambda t: (t, 0))] * 2,
        out_shape=[jax.ShapeDtypeStruct((S, D), F32)] * 2,
        compiler_params=_cp(("parallel",)),
    )(x, x, g)


def norm_shift_bwd(x, g, dh, dhs, dres, tm=256):
    S = x.shape[0]
    nt = S // tm

    def body(x_ref, g_ref, dh_ref, dhs_ref, dhn_ref, dr_ref, dx_ref, dg_ref):
        t = pl.program_id(0)
        nxt = jnp.where(t == nt - 1, 0.0, dhn_ref[0:1, :])
        row = lax.broadcasted_iota(jnp.int32, (tm, D), 0)
        tot = dh_ref[...] + jnp.where(row == tm - 1, nxt, pltpu.roll(dhs_ref[...], tm - 1, 0))
        _, vjp = jax.vjp(rms, x_ref[...], g_ref[...])
        dx, dg = vjp(tot)
        dx_ref[...] = dr_ref[...] + dx

        @pl.when(t == 0)
        def _():
            dg_ref[...] = dg

        @pl.when(t != 0)
        def _():
            dg_ref[...] += dg

    tile = pl.BlockSpec((tm, D), lambda t: (t, 0))
    return pl.pallas_call(
        body, name="rw_norm_shift_bwd", grid=(nt,),
        in_specs=[tile, pl.BlockSpec((1, D), lambda t: (0, 0)), tile, tile,
                  pl.BlockSpec((8, D), lambda t: (jnp.minimum((t + 1) * (tm // 8), S // 8 - 1), 0)), tile],
        out_specs=[tile, pl.BlockSpec((1, D), lambda t: (0, 0))],
        out_shape=[jax.ShapeDtypeStruct((S, D), F32), jax.ShapeDtypeStruct((1, D), F32)],
        compiler_params=_cp(("arbitrary",)),
    )(x, g, dh, dhs, dhs, dres)


def f_rw_proj(h, hs, mix, w):
    return mm(h + (hs - h) * mix, w)


def f_rw_mid(h, hs, r, k, v, mix3, w0, a0, kkw, kaw, w1, w2, a1, a2, g1, g2):
    xx = hs - h
    xw, xa, xg = h + xx * mix3[0:1], h + xx * mix3[1:2], h + xx * mix3[2:3]
    w_log = -softplus(-(w0 + mm(jnp.tanh(mm(xw, w1)), w2))) - 0.5
    lw = -jnp.exp(w_log)
    ag = jax.nn.sigmoid(a0 + mm(mm(xa, a1), a2))
    gate = mm(jax.nn.sigmoid(mm(xg, g1)), g2)
    kk = k * kkw
    kk = kk / jnp.maximum(jnp.sqrt(group_sum(kk * kk, RW_H)), 1e-12)
    kmod = k * (1.0 + (ag - 1.0) * kaw)
    return (to_heads(r), to_heads(lw), to_heads(kmod), to_heads(v), to_heads(-kk), to_heads(kk * ag), gate)


def f_rw_post(yh, rh, kh, vh, gate, x, lng, lnb, rk, wo):
    mu = jnp.mean(yh, axis=-1, keepdims=True)
    var = jnp.mean(jnp.square(yh - mu), axis=-1, keepdims=True)
    yn = (yh - mu) * lax.rsqrt(var + GN_EPS)
    bonus = jnp.sum(rh * kh * rk, axis=-1, keepdims=True) * vh
    y = from_heads(yn) * lng + lnb + from_heads(bonus)
    return x + mm(y * gate, wo)


def _split2(x):
    hi = x.astype(BF16)
    return hi, (x - hi.astype(F32)).astype(BF16)


def _b3(x, y, cx, cy):
    dn = (((cx,), (cy,)), ((0,), (0,)))
    xh, xl = _split2(x)
    yh, yl = _split2(y)
    d = lambda p, q: lax.dot_general(p, q, dn, preferred_element_type=F32)
    return d(xh, yh) + (d(xh, yl) + d(xl, yh))


@jax.custom_vjp
def b_nt(x, y):
    return _b3(x, y, 2, 2)


@jax.custom_vjp
def b_nn(x, y):
    return _b3(x, y, 2, 1)


@jax.custom_vjp
def b_tn(x, y):
    return _b3(x, y, 1, 1)


def _b1(x, y, cx, cy):
    return lax.dot_general(x.astype(BF16), y.astype(BF16), (((cx,), (cy,)), ((0,), (0,))), preferred_element_type=F32)


b_nt.defvjp(lambda x, y: (b_nt(x, y), (x, y)), lambda r, g: (_b1(g, r[1], 2, 1), _b1(g, r[0], 1, 1)))
b_nn.defvjp(lambda x, y: (b_nn(x, y), (x, y)), lambda r, g: (_b1(g, r[1], 2, 2), _b1(r[0], g, 1, 1)))
b_tn.defvjp(lambda x, y: (b_tn(x, y), (x, y)), lambda r, g: (_b1(r[1], g, 2, 2), _b1(r[0], g, 2, 1)))


def _tri_apply(x, lower):
    H, C, _ = x.shape
    ii = lax.broadcasted_iota(jnp.int32, (C, C), 0)
    jj = lax.broadcasted_iota(jnp.int32, (C, C), 1)
    m = jnp.broadcast_to(((jj <= ii) if lower else (jj >= ii)).astype(BF16), (H, C, C))
    x1 = x.astype(BF16)
    r1 = x - x1.astype(F32)
    x2 = r1.astype(BF16)
    x3 = (r1 - x2.astype(F32)).astype(BF16)
    d = lambda q: lax.dot_general(m, q, (((2,), (1,)), ((0,), (0,))), preferred_element_type=F32)
    return d(x1) + (d(x2) + d(x3))


@jax.custom_vjp
def run_sum(x):
    return _tri_apply(x, True)


run_sum.defvjp(lambda x: (run_sum(x), None), lambda _, g: (_tri_apply(g, False),))


def rwkv_chunk(S0, r, lw, k, v, a, b):
    H, C, _ = r.shape
    V = S0.shape[1]
    ii = lax.broadcasted_iota(jnp.int32, (C, C), 0)
    jj = lax.broadcasted_iota(jnp.int32, (C, C), 1)
    strict = jj < ii
    i2 = lax.broadcasted_iota(jnp.int32, (C, 2 * C), 0)
    j2 = lax.broadcasted_iota(jnp.int32, (C, 2 * C), 1)
    incl2 = jnp.where(j2 >= C, j2 - C, j2) <= i2
    g = run_sum(lw)
    ig = jnp.exp(-g)
    ar = jnp.concatenate([a * jnp.exp(g - lw), r * jnp.exp(g)], axis=1)
    bk = jnp.concatenate([b * ig, k * ig], axis=1)
    m = b_nt(ar, bk)
    a_ab = jnp.where(strict, m[:, :C, :C], 0.0)
    a_ak = jnp.where(strict, m[:, :C, C:], 0.0)
    b_r = jnp.where(incl2, m[:, C:, :], 0.0)
    p = b_nt(ar, S0)
    u = p[:, :C] + b_nn(a_ak, v)
    nmat, n = a_ab, 1
    while n < C:
        n *= 2
        if n < C:
            z = b_nn(nmat, jnp.concatenate([u, nmat], axis=2))
            u, nmat = u + z[:, :, :V], z[:, :, V:]
        else:
            u = u + b_nn(nmat, u)
    uv = jnp.concatenate([u, v], axis=1)
    y = p[:, C:] + b_nn(b_r, uv)
    g_end = g[:, C - 1:C, :]
    dec = jnp.exp(g_end - g)
    s_new = S0 * jnp.exp(g_end) + b_tn(uv, jnp.concatenate([b * dec, k * dec], axis=1))
    return y, s_new


def rwkv_fwd(r, lw, k, v, a, b):
    H, S, _ = r.shape
    C = RW_CHUNK

    def body(r_ref, lw_ref, k_ref, v_ref, a_ref, b_ref, y_ref, s_ref, s_scr):
        @pl.when(pl.program_id(0) == 0)
        def _():
            s_scr[...] = jnp.zeros_like(s_scr)

        s0 = s_scr[...]
        s_ref[0] = s0
        y, s1 = rwkv_chunk(s0, r_ref[...], lw_ref[...], k_ref[...], v_ref[...], a_ref[...], b_ref[...])
        y_ref[...] = y
        s_scr[...] = s1

    bs = pl.BlockSpec((H, C, HEAD), lambda c: (0, c, 0))
    return pl.pallas_call(
        body, name="rwkv_fwd", grid=(S // C,), in_specs=[bs] * 6,
        out_specs=[bs, pl.BlockSpec((1, H, HEAD, HEAD), lambda c: (c, 0, 0, 0))],
        out_shape=[jax.ShapeDtypeStruct((H, S, HEAD), F32), jax.ShapeDtypeStruct((S // C, H, HEAD, HEAD), F32)],
        scratch_shapes=[pltpu.VMEM((H, HEAD, HEAD), F32)],
        compiler_params=_cp(("arbitrary",)),
    )(r, lw, k, v, a, b)


def rwkv_bwd(r, lw, k, v, a, b, states, dy):
    H, S, _ = r.shape
    C = RW_CHUNK
    nc = S // C

    def body(r_ref, lw_ref, k_ref, v_ref, a_ref, b_ref, s_ref, dy_ref, dr, dlw, dk, dv, da, db, ds_scr):
        @pl.when(pl.program_id(0) == 0)
        def _():
            ds_scr[...] = jnp.zeros_like(ds_scr)

        _, vjp = jax.vjp(rwkv_chunk, s_ref[0], r_ref[...], lw_ref[...], k_ref[...], v_ref[...], a_ref[...], b_ref[...])
        grads = vjp((dy_ref[...], ds_scr[...]))
        ds_scr[...] = grads[0]
        for o, gv in zip((dr, dlw, dk, dv, da, db), grads[1:]):
            o[...] = gv

    bs = pl.BlockSpec((H, C, HEAD), lambda c: (0, nc - 1 - c, 0))
    return pl.pallas_call(
        body, name="rwkv_bwd", grid=(nc,),
        in_specs=[bs] * 6 + [pl.BlockSpec((1, H, HEAD, HEAD), lambda c: (nc - 1 - c, 0, 0, 0)), bs],
        out_specs=[bs] * 6, out_shape=[jax.ShapeDtypeStruct((H, S, HEAD), F32)] * 6,
        scratch_shapes=[pltpu.VMEM((H, HEAD, HEAD), F32)],
        compiler_params=_cp(("arbitrary",)),
    )(r, lw, k, v, a, b, states, dy)


def loss_head(y, target, tm=512):
    S = y.shape[0]

    def body(y_ref, t_ref, dy_ref, l_ref):
        e = y_ref[...] - t_ref[...]
        dy_ref[...] = e * (1.0 / D)
        part = jnp.broadcast_to(0.5 * jnp.sum(jnp.mean(e * e, axis=-1, keepdims=True)), (1, 128))

        @pl.when(pl.program_id(0) == 0)
        def _():
            l_ref[...] = part

        @pl.when(pl.program_id(0) != 0)
        def _():
            l_ref[...] += part

    tile = pl.BlockSpec((tm, D), lambda t: (t, 0))
    return pl.pallas_call(
        body, name="loss_head", grid=(S // tm,), in_specs=[tile, tile],
        out_specs=[tile, pl.BlockSpec((1, 128), lambda t: (0, 0))],
        out_shape=[jax.ShapeDtypeStruct((S, D), F32), jax.ShapeDtypeStruct((1, 128), F32)],
        compiler_params=_cp(("arbitrary",)),
    )(y, target)


def _row_tile(rows, cols, budget=1 << 19):
    best = None
    for tr in range(8, rows + 1, 8):
        if rows % tr == 0 and tr * cols <= budget:
            best = tr
    return best or rows


def _adam(w, g, m, v):
    m = ADAM_B1 * m + (1.0 - ADAM_B1) * g
    v = ADAM_B2 * v + (1.0 - ADAM_B2) * jnp.square(g)
    m_hat = m / (1.0 - ADAM_B1 ** ADAM_STEP)
    v_hat = v / (1.0 - ADAM_B2 ** ADAM_STEP)
    return -ADAM_LR * (m_hat / (jnp.sqrt(v_hat) + ADAM_EPS) + ADAM_WD * w), m, v


def sum_slots(name, parts, dtype=F32, extras=()):
    n = 0 if parts is None else parts.shape[0]
    R, C = extras[0].shape if parts is None else parts.shape[1:]
    tr = _row_tile(R, C * (n + len(extras)))
    ins = ([] if parts is None else [parts]) + list(extras)

    def body(*refs):
        terms = [] if parts is None else [refs[0][i] for i in range(n)]
        terms += [r[...] for r in refs[len(ins) - len(extras):len(ins)]]
        s = terms[0].astype(F32)
        for t in terms[1:]:
            s = s + t.astype(F32)
        refs[len(ins)][...] = s.astype(dtype)

    tile = pl.BlockSpec((tr, C), lambda t: (t, 0))
    return pl.pallas_call(
        body, name=name, grid=(R // tr,),
        in_specs=([] if parts is None else [pl.BlockSpec((n, tr, C), lambda t: (0, t, 0))]) + [tile] * len(extras),
        out_specs=tile, out_shape=jax.ShapeDtypeStruct((R, C), dtype), compiler_params=_cp(("parallel",)),
    )(*ins)


def adam_step(name, ga, gb, w, m, v):
    R, C = w.shape
    tr = _row_tile(R, C, 1 << 17)
    ins = [ga] + ([gb] if gb is not None else []) + [w, m, v]

    def body(*refs):
        g = refs[0][...]
        if gb is not None:
            g = g + refs[1][...]
        w_ref, m_ref, v_ref, g_out, d_out, m_out, v_out = refs[len(ins) - 3:]
        d, m2, v2 = _adam(w_ref[...], g, m_ref[...], v_ref[...])
        g_out[...] = g
        d_out[...] = d
        m_out[...] = m2
        v_out[...] = v2

    tile = pl.BlockSpec((tr, C), lambda t: (t, 0))
    return pl.pallas_call(
        body, name=name, grid=(R // tr,), in_specs=[tile] * len(ins), out_specs=[tile] * 4,
        out_shape=[jax.ShapeDtypeStruct((R, C), F32)] * 4, compiler_params=_cp(("parallel",)),
    )(*ins)


def _place():
    return lax.axis_index("x"), lax.axis_index("y"), lax.axis_index("c")


def _flip(me, mask):
    return tuple(1 - v if mk else v for v, mk in zip(me, mask))


CHIP_MASKS = ((1, 0, 0), (0, 1, 0), (1, 1, 0))
ALL_MASKS = tuple((a, b, c) for a in (0, 1) for b in (0, 1) for c in (0, 1) if (a, b, c) != (0, 0, 0))


def _chip(dev):
    return 2 * dev[0] + dev[1]


def _devno(dev):
    return 4 * dev[0] + 2 * dev[1] + dev[2]


class Pushes:
    def __init__(self, arrays, out_shapes, masks, copies, src_of, dst_of, alias=False):
        self.arrays, self.out_shapes, self.masks, self.copies = list(arrays), list(out_shapes), masks, copies
        self.src_of, self.dst_of, self.alias = src_of, dst_of, alias
        self.n = len(self.arrays)

    def sem_shapes(self):
        k = self.n * len(self.masks) * self.copies
        return [pltpu.SemaphoreType.DMA((k,)), pltpu.SemaphoreType.DMA((k,))]

    def ops(self, ins, outs, send_sems, recv_sems):
        me = _place()
        sends, lands = [], []
        for i in range(self.n):
            for j, mk in enumerate(self.masks):
                peer = _flip(me, mk)
                srcs, dsts = self.src_of(ins[i], me, j), self.dst_of(outs[i], me, j)
                here = self.dst_of(outs[i], peer, j)
                for q in range(self.copies):
                    sem = (i * len(self.masks) + j) * self.copies + q
                    sends.append(pltpu.make_async_remote_copy(
                        src_ref=srcs[q], dst_ref=dsts[q], send_sem=send_sems.at[sem], recv_sem=recv_sems.at[sem],
                        device_id=peer, device_id_type=MESH))
                    lands.append(pltpu.make_async_remote_copy(
                        src_ref=here[q], dst_ref=here[q], send_sem=send_sems.at[sem], recv_sem=recv_sems.at[sem],
                        device_id=peer, device_id_type=MESH))

        def start():
            for cp in sends:
                cp.start()

        def wait():
            for cp in lands:
                cp.wait_recv()
            for cp in sends:
                cp.wait_send()

        return start, wait


_HBM = pl.BlockSpec(memory_space=pl.ANY)


def exchange(name, p, local_of=None):
    n = p.n

    def body(*refs):
        ins, outs = refs[:n], refs[n:2 * n]
        start, wait = p.ops(ins, outs, refs[2 * n], refs[2 * n + 1])
        locals_ = []
        if local_of is not None:
            for i in range(n):
                src, dst = local_of(ins[i], outs[i], _place())
                locals_.append(pltpu.make_async_copy(src, dst, refs[2 * n + 2].at[i]))
                locals_[-1].start()
        start()
        wait()
        for cp in locals_:
            cp.wait()

    return pl.pallas_call(
        body, name=name, in_specs=[_HBM] * n, out_specs=[_HBM] * n, out_shape=p.out_shapes,
        scratch_shapes=p.sem_shapes() + ([pltpu.SemaphoreType.DMA((n,))] if local_of is not None else []),
        input_output_aliases={i: i for i in range(n)} if p.alias else {},
    )(*p.arrays)


def _half(c, rows):
    return pl.ds(c * (rows // 2), rows // 2)


def gather_pushes(arrays):
    outs = [jax.ShapeDtypeStruct((N_CHIPS,) + a.shape, a.dtype) for a in arrays]
    sib = len(CHIP_MASKS)
    return Pushes(arrays, outs, CHIP_MASKS + ((0, 0, 1),), 1,
                  src_of=lambda r, me, j: [r] if j == sib else [r.at[_half(me[2], r.shape[0])]],
                  dst_of=lambda o, sender, j: [o.at[_chip(sender)]] if j == sib else
                  [o.at[_chip(sender), _half(sender[2], o.shape[1])]])


def gather_swap(name, got):
    outs = [jax.ShapeDtypeStruct(a.shape, a.dtype) for a in got]
    return exchange(name, Pushes(
        got, outs, ((0, 0, 1),), len(CHIP_MASKS),
        src_of=lambda r, me, j: [r.at[_chip(_flip(me, mk)), _half(me[2], r.shape[1])] for mk in CHIP_MASKS],
        dst_of=lambda o, sender, j: [o.at[_chip(_flip(sender, mk)), _half(sender[2], o.shape[1])] for mk in CHIP_MASKS],
        alias=True))


def reduce_begin(tag, names, arrays, wire):
    c = lax.axis_index("c")
    split = [a.reshape(N_CHIPS, 2, a.shape[1] // 2, a.shape[2]) for a in arrays]
    half_shapes = [jax.ShapeDtypeStruct((N_CHIPS,) + a.shape[2:], F32) for a in split]
    theirs = exchange(f"grad_pre_swap_{tag}", Pushes(
        split, half_shapes, ((0, 0, 1),), 1,
        src_of=lambda r, me, j: [r.at[:, 1 - me[2]]], dst_of=lambda o, sender, j: [o]))
    chip_sum = []
    for nm, a, t, dt in zip(names, split, theirs, wire):
        own = lax.dynamic_index_in_dim(a, c, axis=1, keepdims=False)
        flat = lambda v: v.reshape(-1, v.shape[-1])
        chip_sum.append(sum_slots(f"sum2_{nm}", None, dt, [flat(own), flat(t)]).reshape(t.shape))
    pushes = Pushes(chip_sum, [jax.ShapeDtypeStruct((len(CHIP_MASKS),) + a.shape[1:], a.dtype) for a in chip_sum],
                    CHIP_MASKS, 1,
                    src_of=lambda r, me, j: [r.at[_chip(_flip(me, CHIP_MASKS[j]))]],
                    dst_of=lambda o, sender, j: [o.at[j]])
    return chip_sum, pushes


def reduce_end(tag, names, chip_sum, landed):
    x, y, c = _place()
    halves = [sum_slots(f"sum4_{nm}", p, F32, [lax.dynamic_index_in_dim(a, _chip((x, y, c)), axis=0, keepdims=False)])
              for nm, p, a in zip(names, landed, chip_sum)]
    others = exchange(f"grad_final_swap_{tag}", Pushes(
        halves, [jax.ShapeDtypeStruct(a.shape, F32) for a in halves], ((0, 0, 1),), 1,
        src_of=lambda r, me, j: [r], dst_of=lambda o, sender, j: [o]))
    return [jnp.concatenate([jnp.where(c == 0, h, o), jnp.where(c == 0, o, h)], axis=0) for h, o in zip(halves, others)]


def gather_all(arrays):
    outs = [jax.ShapeDtypeStruct((8,) + a.shape, a.dtype) for a in arrays]
    return exchange("gather_replicated", Pushes(
        arrays, outs, ALL_MASKS, 1, src_of=lambda r, me, j: [r], dst_of=lambda o, sender, j: [o.at[_devno(sender)]]),
        local_of=lambda r, o, me: (r, o.at[_devno(me)]))


def _unshard_cols(g):
    return jnp.transpose(g, (1, 0, 2)).reshape(g.shape[1], -1)


def _shard_cols(a):
    return jnp.transpose(a.reshape(a.shape[0], N_CHIPS, -1), (1, 0, 2))


class Weights(dict):
    def ride(self, kernel_name):
        return None

    def arrived(self, kernel_name, outs):
        pass


def _forward_backward(x, tgt, W, grads_early=None):
    S = x.shape[0]
    G = {}
    sd = jax.ShapeDtypeStruct

    def ffn(xin, l, j):
        return ffn_fwd(xin, W["ffn_norm"][l][j], W["ffn_w_gate", l, j], W["ffn_w_up", l, j], W["ffn_w_down", l, j], l, j)

    def ffn_back(xin, dout, l, j):
        gn = W["ffn_norm"][l][j]
        dh, G["ffn_w_gate", l, j], G["ffn_w_up", l, j], G["ffn_w_down", l, j] = ffn_bwd(
            xin, gn, W["ffn_w_gate", l, j], W["ffn_w_up", l, j], W["ffn_w_down", l, j], dout, l, j)
        dx, G[("ffn_norm", l, j)] = norm_bwd(f"ffn_norm_bwd_{l}{j}", xin, gn, dh, dout)
        return dx

    x0 = x
    x1 = ffn(x0, 0, 0)
    g0 = W["mix_norm"][0]
    sbq, sbk, sbv = tile_fwd(f_attn_sb, "attn_in_sb", [x1], [g0, W["attn_w_in"][0]], [sd((S, SB_W), F32)] * 3, 256)
    dl_shape = sd((DL_PAIRS, S, 128), F32)
    qn, = tile_fwd(f_attn_qk, "attn_in_q", [x1], [g0, W["attn_w_in"][1], W["attn_q_norm"]], [dl_shape], 256)
    kn, = tile_fwd(f_attn_qk, "attn_in_k", [x1], [g0, W["attn_w_in"][2], W["attn_k_norm"]], [dl_shape], 256)
    vv, = tile_fwd(f_attn_v, "attn_in_v", [x1], [g0, W["attn_w_in"][3]], [dl_shape], 256)
    oa, *rode = sb_fwd(sbq, sbk, sbv, W.ride("sb_fwd"))
    W.arrived("sb_fwd", rode)
    qs, ks, vs = (reorder(nm, t, DIL, False) for nm, t in (("sub_q", qn), ("sub_k", kn), ("sub_v", vv)))
    o_s, lse_s, *rode = dil_fwd(qs, ks, vs, W["bias_mat"], W.ride("dil_fwd"))
    W.arrived("dil_fwd", rode)
    o_n, lse_n = reorder("nat_o", o_s, DIL, True), reorder("nat_lse", lse_s, DIL, True)
    x2, = tile_fwd(f_attn_out, "attn_out", [x1, oa, o_n, lse_n], [W["attn_w_out"]], [sd((S, D), F32)], 256)
    x3 = ffn(x2, 0, 1)
    x4 = ffn(x3, 1, 0)
    g1 = W["mix_norm"][1]
    h, hs = norm_shift_fwd(x4, g1)
    mix = W["rw_mix"]
    r, = tile_fwd(f_rw_proj, "rw_proj_r", [h, hs], [mix[0:1], W["rw_wr"]], [sd((S, D), F32)], 256)
    k, = tile_fwd(f_rw_proj, "rw_proj_k", [h, hs], [mix[2:3], W["rw_wk"]], [sd((S, D), F32)], 256)
    v, = tile_fwd(f_rw_proj, "rw_proj_v", [h, hs], [mix[3:4], W["rw_wv"]], [sd((S, D), F32)], 256)
    mix3 = jnp.concatenate([mix[1:2], mix[4:5], mix[5:6]], axis=0)
    mid_w = [mix3, W["rw_w0"], W["rw_a0"], W["rw_kk"], W["rw_ka"], W["rw_w1"], W["rw_w2"], W["rw_a1"], W["rw_a2"],
             W["rw_g1"], W["rw_g2"]]
    hshape = sd((RW_H, S, HEAD), F32)
    mid_tiles = [h, hs, r, k, v]
    rh, lwh, kh, vh, ah, bh, gate = tile_fwd(f_rw_mid, "rw_mid", mid_tiles, mid_w, [hshape] * 6 + [sd((S, D), F32)], 128)
    yh, states = rwkv_fwd(rh, lwh, kh, vh, ah, bh)
    post_w = [W["rw_lnx_g"], W["rw_lnx_b"], W["rw_rk"], W["rw_wo"]]
    post_tiles = [yh, rh, kh, vh, gate, x4]
    x5, = tile_fwd(f_rw_post, "rw_post", post_tiles, post_w, [sd((S, D), F32)], 128)
    x6 = ffn(x5, 1, 1)
    dx6, loss_part = loss_head(x6, tgt)

    dx5 = ffn_back(x5, dx6, 1, 1)
    (dyh, drh, dkh, dvh, dgate, dx4), (d_lng, d_lnb, d_rk, d_wo) = tile_bwd(
        f_rw_post, "rw_post_bwd", post_tiles, post_w, [dx5], 128, [True] * 6, [True] * 4)
    drh2, dlwh, dkh2, dvh2, dah, dbh = rwkv_bwd(rh, lwh, kh, vh, ah, bh, states, dyh)
    mid_cts = [(drh, drh2), dlwh, (dkh, dkh2), (dvh, dvh2), dah, dbh, dgate]
    (dh, dhs, dr, dk, dv), dmid_w = tile_bwd(f_rw_mid, "rw_mid_bwd", mid_tiles, mid_w, mid_cts, 128,
                                             [True] * 5, [True] * len(mid_w))
    dmix = {}
    for nm, ct, row, wname in (("r", dr, 0, "rw_wr"), ("k", dk, 2, "rw_wk"), ("v", dv, 3, "rw_wv")):
        (dh, dhs), (dmix[row], G[wname]) = tile_bwd(
            f_rw_proj, f"rw_proj_{nm}_bwd", [h, hs], [mix[row:row + 1], W[wname]], [ct], 256,
            [True, True], [True, True], acc={0: dh, 1: dhs})
    dx4, G[("mix_norm", 1)] = norm_shift_bwd(x4, g1, dh, dhs, dx4)
    dmix3 = dmid_w[0]
    G["rw_mix"] = jnp.concatenate([dmix[0], dmix3[0:1], dmix[2], dmix[3], dmix3[1:2], dmix3[2:3]], axis=0)
    for nm, gv in zip(("rw_w0", "rw_a0", "rw_kk", "rw_ka", "rw_w1", "rw_w2", "rw_a1", "rw_a2", "rw_g1", "rw_g2"), dmid_w[1:]):
        G[nm] = gv
    G["rw_lnx_g"], G["rw_lnx_b"], G["rw_rk"], G["rw_wo"] = d_lng, d_lnb, d_rk, d_wo
    dx3 = ffn_back(x3, dx4, 1, 0)
    dx2 = ffn_back(x2, dx3, 0, 1)
    (dx1, doa, do_n, dlse_n), (G["attn_w_out"],) = tile_bwd(
        f_attn_out, "attn_out_bwd", [x1, oa, o_n, lse_n], [W["attn_w_out"]], [dx2], 256, [True] * 4, [True])
    do_s, dlse_s = reorder("sub_do", do_n, DIL, False), reorder("sub_dlse", dlse_n, DIL, False)
    dqs, dks, dvs, dsum = dil_bwd(qs, ks, vs, W["bias_mat"], o_s, lse_s, do_s, dlse_s)
    G["rel_bias"] = bias_grad(dsum, W["buckets"])
    dqn, dkn, dvv = (reorder(nm, t, DIL, True) for nm, t in (("nat_dq", dqs), ("nat_dk", dks), ("nat_dv", dvs)))
    ride, landed = grads_early(G) if grads_early is not None else (None, None)
    dsbq, dsbk, dsbv, *rode = sb_bwd(sbq, sbk, sbv, doa, ride)
    if landed is not None:
        landed(rode)
    dg0 = []
    dwin = []
    (dx1,), (dg, dw) = tile_bwd(f_attn_sb, "attn_in_sb_bwd", [x1], [g0, W["attn_w_in"][0]], [dsbq, dsbk, dsbv], 256,
                                [True], [True, True], acc={0: dx1})
    dg0.append(dg), dwin.append(dw)
    (dx1,), (dg, dw, G["attn_q_norm"]) = tile_bwd(f_attn_qk, "attn_in_q_bwd", [x1], [g0, W["attn_w_in"][1], W["attn_q_norm"]],
                                                  [dqn], 256, [True], [True] * 3, acc={0: dx1})
    dg0.append(dg), dwin.append(dw)
    (dx1,), (dg, dw, G["attn_k_norm"]) = tile_bwd(f_attn_qk, "attn_in_k_bwd", [x1], [g0, W["attn_w_in"][2], W["attn_k_norm"]],
                                                  [dkn], 256, [True], [True] * 3, acc={0: dx1})
    dg0.append(dg), dwin.append(dw)
    (dx1,), (dg, dw) = tile_bwd(f_attn_v, "attn_in_v_bwd", [x1], [g0, W["attn_w_in"][3]], [dvv], 256,
                                [True], [True, True], acc={0: dx1})
    dg0.append(dg), dwin.append(dw)
    G[("mix_norm", 0)] = dg0
    G["attn_w_in"] = dwin
    dx0 = ffn_back(x0, dx1, 0, 0)
    return loss_part, dx0, G


VEC_ROWS = ("ffn_norm", "rw_mix", "rw_w0", "rw_a0", "rw_kk", "rw_ka", "rw_lnx_g", "rw_lnx_b")


def kernel(x, ffn_norm, ffn_w_gate, ffn_w_up, ffn_w_down, mix_norm, rel_bias, attn_w_in, attn_q_norm, attn_k_norm, attn_w_out, rw_mix, rw_w0, rw_w1, rw_w2, rw_a0, rw_a1, rw_a2, rw_g1, rw_g2, rw_kk, rw_ka, rw_rk, rw_wr, rw_wk, rw_wv, rw_wo, rw_lnx_g, rw_lnx_b, loss_target, m_ffn_norm, m_ffn_w_gate, m_ffn_w_up, m_ffn_w_down, m_mix_norm, m_rel_bias, m_attn_w_in, m_attn_q_norm, m_attn_k_norm, m_attn_w_out, m_rw_mix, m_rw_w0, m_rw_w1, m_rw_w2, m_rw_a0, m_rw_a1, m_rw_a2, m_rw_g1, m_rw_g2, m_rw_kk, m_rw_ka, m_rw_rk, m_rw_wr, m_rw_wk, m_rw_wv, m_rw_wo, m_rw_lnx_g, m_rw_lnx_b, v_ffn_norm, v_ffn_w_gate, v_ffn_w_up, v_ffn_w_down, v_mix_norm, v_rel_bias, v_attn_w_in, v_attn_q_norm, v_attn_k_norm, v_attn_w_out, v_rw_mix, v_rw_w0, v_rw_w1, v_rw_w2, v_rw_a0, v_rw_a1, v_rw_a2, v_rw_g1, v_rw_g2, v_rw_kk, v_rw_ka, v_rw_rk, v_rw_wr, v_rw_wk, v_rw_wv, v_rw_wo, v_rw_lnx_g, v_rw_lnx_b):
    names = ["ffn_norm", "ffn_w_gate", "ffn_w_up", "ffn_w_down", "mix_norm", "rel_bias", "attn_w_in", "attn_q_norm",
             "attn_k_norm", "attn_w_out", "rw_mix", "rw_w0", "rw_w1", "rw_w2", "rw_a0", "rw_a1", "rw_a2", "rw_g1", "rw_g2",
             "rw_kk", "rw_ka", "rw_rk", "rw_wr", "rw_wk", "rw_wv", "rw_wo", "rw_lnx_g", "rw_lnx_b"]
    loc = locals()
    w = {n: loc[n] for n in names}
    mom = {n: loc["m_" + n] for n in names}
    vel = {n: loc["v_" + n] for n in names}
    S = x.shape[1]

    ffn3 = ("ffn_w_gate", "ffn_w_up", "ffn_w_down")
    rw_mats = ("rw_w1", "rw_w2", "rw_a1", "rw_a2", "rw_g1", "rw_g2", "rw_wr", "rw_wk", "rw_wv", "rw_wo")
    cols_split = ("attn_w_out", "rw_w2", "rw_a2", "rw_g2")
    shard = {"vec": jnp.concatenate([w[n].reshape(-1, 256) for n in VEC_ROWS], axis=0)}
    for n in ffn3:
        for l in range(2):
            for j in range(2):
                shard[n, l, j] = w[n][l, j].astype(BF16)
    for n in ("attn_w_in", "attn_w_out") + rw_mats:
        shard[n] = w[n].reshape(-1, w[n].shape[-1]).astype(BF16)
    ffn_keys = lambda l, j: [(n, l, j) for n in ffn3]
    w_groups = {"first": ["vec"] + ffn_keys(0, 0) + ["attn_w_in", "attn_w_out"],
                "sb_fwd": ffn_keys(0, 1) + ffn_keys(1, 0) + list(rw_mats),
                "dil_fwd": ffn_keys(1, 1)}
    label = lambda key: key if isinstance(key, str) else f"{key[0]}_{key[1]}{key[2]}"

    class Streamed(Weights):
        def ride(self, kernel_name):
            return gather_pushes([shard[k] for k in w_groups[kernel_name]])

        def arrived(self, kernel_name, outs):
            for key, g in zip(w_groups[kernel_name], gather_swap(f"gather_swap_{kernel_name}", outs)):
                if key == "vec":
                    vec_full = _unshard_cols(g)
                    self["ffn_norm"] = [[vec_full[2 * l + j][None] for j in range(2)] for l in range(2)]
                    self["rw_mix"] = vec_full[4:10]
                    for i, n in enumerate(("rw_w0", "rw_a0", "rw_kk", "rw_ka", "rw_lnx_g", "rw_lnx_b")):
                        self[n] = vec_full[10 + i][None]
                elif key == "attn_w_in":
                    self[key] = [g[p] for p in range(N_CHIPS)]
                elif key in cols_split:
                    self[key] = _unshard_cols(g)
                elif isinstance(key, str):
                    self[key] = g.reshape(D, -1)
                else:
                    self[key] = g

    buckets = _bucket_maps()
    W = Streamed({"mix_norm": [mix_norm[0:1], mix_norm[1:2]], "attn_q_norm": attn_q_norm, "attn_k_norm": attn_k_norm,
                  "rw_rk": rw_rk[0][:, None, :], "buckets": buckets, "bias_mat": bias_table(rel_bias, buckets)})
    W.arrived("first", exchange("gather_weights", W.ride("first")))

    def slots(key, G):
        if key == "vec":
            rows = [G[("ffn_norm", l, j)] for l in range(2) for j in range(2)] + [G["rw_mix"]] + \
                   [G[n] for n in ("rw_w0", "rw_a0", "rw_kk", "rw_ka", "rw_lnx_g", "rw_lnx_b")]
            return _shard_cols(jnp.concatenate(rows, axis=0))
        if key == "attn_w_in":
            return jnp.stack(G[key])
        if key in cols_split:
            return _shard_cols(G[key])
        if isinstance(key, str):
            return G[key].reshape(N_CHIPS, D // N_CHIPS, -1)
        return G[key]

    g_groups = {"early": ffn_keys(1, 1) + ffn_keys(1, 0) + ffn_keys(0, 1) + list(rw_mats) + ["attn_w_out"],
                "late": ["vec", "attn_w_in"] + ffn_keys(0, 0)}
    wire = lambda keys: [F32 if k == "vec" else BF16 for k in keys]
    part = {}

    def grads_early(G):
        keys = g_groups["early"]
        chip_sum, pushes = reduce_begin("early", [label(k) for k in keys], [slots(k, G) for k in keys], wire(keys))
        return pushes, lambda landed: part.update(zip(keys, reduce_end("early", [label(k) for k in keys], chip_sum, landed)))

    loss_part, dx, G = _forward_backward(x[0], loss_target[0], W, grads_early)
    loss = lax.psum(loss_part[0, 0], ("x", "y", "c"))
    keys = g_groups["late"]
    chip_sum, pushes = reduce_begin("late", [label(k) for k in keys], [slots(k, G) for k in keys], wire(keys))
    part.update(zip(keys, reduce_end("late", [label(k) for k in keys], chip_sum, exchange("scatter_grads", pushes))))
    for n in ffn3:
        part[n] = jnp.stack([jnp.stack([part[n, l, j] for j in range(2)]) for l in range(2)])

    rep = jnp.concatenate([G[("mix_norm", 0)][0] + G[("mix_norm", 0)][1] + G[("mix_norm", 0)][2] + G[("mix_norm", 0)][3],
                           G[("mix_norm", 1)]], axis=0).reshape(16, 128)
    rep = jnp.concatenate([rep, G["rel_bias"], jnp.pad(G["attn_q_norm"], ((0, 0), (0, 64))),
                           jnp.pad(G["attn_k_norm"], ((0, 0), (0, 64))), G["rw_rk"].reshape(8, 128),
                           jnp.zeros((2, 128), F32)], axis=0)
    rep_sum = sum_slots("sum_replicated", gather_all([rep])[0])
    g_rep = {
        "mix_norm": rep_sum[0:16].reshape(2, D),
        "rel_bias": jnp.transpose(rep_sum[16:28, :N_BUCKETS]),
        "attn_q_norm": rep_sum[28:29, :HEAD], "attn_k_norm": rep_sum[29:30, :HEAD],
        "rw_rk": rep_sum[30:38].reshape(1, RW_H, HEAD),
    }

    out = {}

    def adam(n, ga, gb):
        shp = w[n].shape
        to2 = lambda a: a.reshape(-1, shp[-1])
        res = adam_step(f"adam_{n}", to2(ga), None if gb is None else to2(gb), to2(w[n]), to2(mom[n]), to2(vel[n]))
        out[n] = tuple(r.reshape(shp) for r in res)

    for n in ffn3 + ("attn_w_in", "attn_w_out") + rw_mats:
        adam(n, part[n], None)
    rows = {"ffn_norm": (0, 4), "rw_mix": (4, 10), "rw_w0": (10, 11), "rw_a0": (11, 12), "rw_kk": (12, 13),
            "rw_ka": (13, 14), "rw_lnx_g": (14, 15), "rw_lnx_b": (15, 16)}
    for n, (lo, hi) in rows.items():
        adam(n, part["vec"][lo:hi], None)
    for n, gv in g_rep.items():
        adam(n, gv, None)

    grads = [out[n][0] for n in names]
    deltas = [out[n][1] for n in names]
    new_m = [out[n][2] for n in names]
    new_v = [out[n][3] for n in names]
    return (loss, dx[None], *grads, *deltas, *new_m, *new_v)
```

```python
import functools
import math

import jax
import jax.numpy as jnp
from jax import lax
from jax.experimental import pallas as pl
from jax.experimental.pallas import tpu as pltpu

F32, BF16 = jnp.float32, jnp.bfloat16
HI = lax.Precision.HIGHEST
MESH = pl.DeviceIdType.MESH

D = 1024
HEAD = 64
N_CHIPS = 4
FF_SHARD = 704
SB_W = 256
DL_HEADS = 12
DL_PAIRS = 6
DIL = (1, 4, 16)
QBLK = 128
N_BUCKETS = 32
MAX_DISTANCE = 2048
RW_H = 16
RW_CHUNK = 64
NORM_EPS = 1e-6
GN_EPS = 64e-5
NEG_INF = -1e30
VMEM_LIMIT = 56 * 1024 * 1024

ADAM_LR, ADAM_B1, ADAM_B2, ADAM_EPS, ADAM_WD, ADAM_STEP = 0.001, 0.9, 0.999, 1e-08, 0.01, 10


def _cp(sem):
    return pltpu.CompilerParams(dimension_semantics=sem, vmem_limit_bytes=VMEM_LIMIT)


def _dg(a, b, dims, prec=None):
    return lax.dot_general(a, b, (dims, ((), ())), precision=prec, preferred_element_type=F32)


def _bdot(a, b, dims):
    return _dg(a.astype(BF16), b.astype(BF16), dims)


@jax.custom_vjp
def mm(a, b):
    return _bdot(a, b, ((1,), (0,)))


def _mm_fwd(a, b):
    return _bdot(a, b, ((1,), (0,))), (a, b)


def _mm_bwd(res, g):
    a, b = res
    return _bdot(g, b, ((1,), (1,))), _bdot(a, g, ((0,), (0,)))


mm.defvjp(_mm_fwd, _mm_bwd)


def rms(x, g):
    return x * lax.rsqrt(jnp.mean(x * x, axis=-1, keepdims=True) + NORM_EPS) * g


def group_sum(x, nh):
    w = x.shape[-1]
    e = (lax.broadcasted_iota(jnp.int32, (w, nh), 0) // HEAD == lax.broadcasted_iota(jnp.int32, (w, nh), 1)).astype(F32)
    s = _dg(x, e, ((1,), (0,)), HI)
    return _dg(s, e, ((1,), (1,)), HI)


def softplus(u):
    return jnp.maximum(u, 0.0) + jnp.log1p(jnp.exp(-jnp.abs(u)))


def to_heads(t, nh=RW_H):
    return jnp.stack([t[:, HEAD * h:HEAD * (h + 1)] for h in range(nh)])


def from_heads(t):
    return jnp.concatenate([t[h] for h in range(t.shape[0])], axis=-1)


def _tile_spec(shape, tm):
    if len(shape) == 2:
        return pl.BlockSpec((tm, shape[1]), lambda t: (t, 0))
    return pl.BlockSpec((shape[0], tm, shape[2]), lambda t: (0, t, 0))


def _full_spec(shape):
    nd = len(shape)
    return pl.BlockSpec(tuple(shape), lambda t: (0,) * nd)


def _rows(a):
    return a.shape[0] if a.ndim == 2 else a.shape[1]


def tile_fwd(f, name, tiles, weights, outs, tm):
    nt, nw = len(tiles), len(weights)

    def body(*refs):
        tv = [r[...] for r in refs[:nt]]
        wv = [r[...].astype(F32) for r in refs[nt:nt + nw]]
        res = f(*tv, *wv)
        if not isinstance(res, (tuple, list)):
            res = (res,)
        for o, v in zip(refs[nt + nw:], res):
            o[...] = v.astype(o.dtype)

    return pl.pallas_call(
        body, name=name, grid=(_rows(tiles[0]) // tm,),
        in_specs=[_tile_spec(a.shape, tm) for a in tiles] + [_full_spec(w.shape) for w in weights],
        out_specs=[_tile_spec(o.shape, tm) for o in outs],
        out_shape=list(outs),
        compiler_params=_cp(("parallel",)),
    )(*tiles, *weights)


def tile_bwd(f, name, tiles, weights, cts, tm, dt, dw, acc=None):
    acc = acc or {}
    groups = [c if isinstance(c, tuple) else (c,) for c in cts]
    cts = [a for grp in groups for a in grp]
    nt, nw, nc = len(tiles), len(weights), len(cts)
    acc_idx = sorted(acc)
    na = len(acc_idx)
    dti = [i for i in range(nt) if dt[i]]
    dwi = [i for i in range(nw) if dw[i]]

    def body(*refs):
        tv = [r[...] for r in refs[:nt]]
        wv = [r[...].astype(F32) for r in refs[nt:nt + nw]]
        crefs = list(refs[nt + nw:nt + nw + nc])
        cv = []
        for grp in groups:
            terms = [crefs.pop(0)[...] for _ in grp]
            cv.append(functools.reduce(lambda a, b: a + b, terms))
        av = {i: r[...] for i, r in zip(acc_idx, refs[nt + nw + nc:nt + nw + nc + na])}
        orefs = refs[nt + nw + nc + na:]

        def g(*diff):
            t2, w2 = list(tv), list(wv)
            for i, v in zip(dti, diff[:len(dti)]):
                t2[i] = v
            for i, v in zip(dwi, diff[len(dti):]):
                w2[i] = v
            res = f(*t2, *w2)
            return tuple(res) if isinstance(res, (tuple, list)) else (res,)

        _, vjp = jax.vjp(g, *[tv[i] for i in dti], *[wv[i] for i in dwi])
        grads = vjp(tuple(cv))
        for k, i in enumerate(dti):
            gt = grads[k]
            if i in av:
                gt = gt + av[i]
            orefs[k][...] = gt
        first = pl.program_id(0) == 0
        for k, i in enumerate(dwi):
            o = orefs[len(dti) + k]
            gw = grads[len(dti) + k]

            @pl.when(first)
            def _(o=o, gw=gw):
                o[...] = gw

            @pl.when(jnp.logical_not(first))
            def _(o=o, gw=gw):
                o[...] += gw

    out_shape = [jax.ShapeDtypeStruct(tiles[i].shape, F32) for i in dti] + \
                [jax.ShapeDtypeStruct(weights[i].shape, F32) for i in dwi]
    res = pl.pallas_call(
        body, name=name, grid=(_rows(tiles[0]) // tm,),
        in_specs=[_tile_spec(a.shape, tm) for a in tiles] + [_full_spec(w.shape) for w in weights] +
                 [_tile_spec(c.shape, tm) for c in cts] + [_tile_spec(tiles[i].shape, tm) for i in acc_idx],
        out_specs=[_tile_spec(tiles[i].shape, tm) for i in dti] + [_full_spec(weights[i].shape) for i in dwi],
        out_shape=out_shape,
        compiler_params=_cp(("arbitrary",)),
    )(*tiles, *weights, *cts, *[acc[i] for i in acc_idx])
    return list(res[:len(dti)]), list(res[len(dti):])


def _ffn_wspec(rows, cols, cfirst):
    if cfirst:
        return pl.BlockSpec((1, rows, cols), lambda c, t: (c, 0, 0))
    return pl.BlockSpec((1, rows, cols), lambda t, c: (c, 0, 0))


def ffn_fwd(x, g, wg, wu, wd, l, j, tm=512):
    S = x.shape[0]

    def body(x_ref, g_ref, wg_ref, wu_ref, wd_ref, o_ref, h_ref, acc_ref):
        c = pl.program_id(1)

        @pl.when(c == 0)
        def _():
            h_ref[...] = rms(x_ref[...], g_ref[...]).astype(BF16)
            acc_ref[...] = jnp.zeros_like(acc_ref)

        h = h_ref[...]
        a = _bdot(h, wg_ref[0], ((1,), (0,)))
        b = _bdot(h, wu_ref[0], ((1,), (0,)))
        y = a * jax.nn.sigmoid(a) * b
        acc_ref[...] += _bdot(y, wd_ref[0], ((1,), (0,)))

        @pl.when(c == N_CHIPS - 1)
        def _():
            o_ref[...] = x_ref[...] + 0.5 * acc_ref[...]

    return pl.pallas_call(
        body, name=f"ffn_fwd_{l}{j}", grid=(S // tm, N_CHIPS),
        in_specs=[pl.BlockSpec((tm, D), lambda t, c: (t, 0)), pl.BlockSpec((1, D), lambda t, c: (0, 0)),
                  _ffn_wspec(D, FF_SHARD, False), _ffn_wspec(D, FF_SHARD, False), _ffn_wspec(FF_SHARD, D, False)],
        out_specs=pl.BlockSpec((tm, D), lambda t, c: (t, 0)),
        out_shape=jax.ShapeDtypeStruct((S, D), F32),
        scratch_shapes=[pltpu.VMEM((tm, D), BF16), pltpu.VMEM((tm, D), F32)],
        compiler_params=_cp(("parallel", "arbitrary")),
    )(x, g, wg, wu, wd)


def ffn_bwd(x, g, wg, wu, wd, dout, l, j, tm=512):
    S = x.shape[0]

    def body(x_ref, g_ref, wg_ref, wu_ref, wd_ref, do_ref, dh_ref, dwg_ref, dwu_ref, dwd_ref):
        t = pl.program_id(1)
        h = rms(x_ref[...], g_ref[...]).astype(BF16)
        wgv, wuv, wdv = wg_ref[0], wu_ref[0], wd_ref[0]
        a = _bdot(h, wgv, ((1,), (0,)))
        b = _bdot(h, wuv, ((1,), (0,)))
        sig = jax.nn.sigmoid(a)
        s = a * sig
        dyd = 0.5 * do_ref[...]
        dy = _bdot(dyd, wdv, ((1,), (1,)))
        dwd = _bdot(s * b, dyd, ((0,), (0,)))
        db = dy * s
        da = dy * b * (sig * (1.0 + a * (1.0 - sig)))
        dwg = _bdot(h, da, ((0,), (0,)))
        dwu = _bdot(h, db, ((0,), (0,)))
        dh_ref[0] = _bdot(da, wgv, ((1,), (1,))) + _bdot(db, wuv, ((1,), (1,)))

        @pl.when(t == 0)
        def _():
            dwg_ref[0] = dwg
            dwu_ref[0] = dwu
            dwd_ref[0] = dwd

        @pl.when(t != 0)
        def _():
            dwg_ref[0] += dwg
            dwu_ref[0] += dwu
            dwd_ref[0] += dwd

    return pl.pallas_call(
        body, name=f"ffn_bwd_{l}{j}", grid=(N_CHIPS, S // tm),
        in_specs=[pl.BlockSpec((tm, D), lambda c, t: (t, 0)), pl.BlockSpec((1, D), lambda c, t: (0, 0)),
                  _ffn_wspec(D, FF_SHARD, True), _ffn_wspec(D, FF_SHARD, True), _ffn_wspec(FF_SHARD, D, True),
                  pl.BlockSpec((tm, D), lambda c, t: (t, 0))],
        out_specs=[pl.BlockSpec((1, tm, D), lambda c, t: (c, t, 0)),
                   _ffn_wspec(D, FF_SHARD, True), _ffn_wspec(D, FF_SHARD, True), _ffn_wspec(FF_SHARD, D, True)],
        out_shape=[jax.ShapeDtypeStruct((N_CHIPS, S, D), F32)] + [jax.ShapeDtypeStruct(a.shape, F32) for a in (wg, wu, wd)],
        compiler_params=_cp(("parallel", "arbitrary")),
    )(x, g, wg, wu, wd, dout)


def norm_bwd(name, x, g, dh_parts, dres, tm=256):
    S = x.shape[0]
    P = dh_parts.shape[0]

    def body(x_ref, g_ref, dh_ref, dr_ref, dx_ref, dg_ref):
        dh = dh_ref[0]
        for p in range(1, P):
            dh = dh + dh_ref[p]
        _, vjp = jax.vjp(rms, x_ref[...], g_ref[...])
        dx, dg = vjp(dh)
        dx_ref[...] = dr_ref[...] + dx

        @pl.when(pl.program_id(0) == 0)
        def _():
            dg_ref[...] = dg

        @pl.when(pl.program_id(0) != 0)
        def _():
            dg_ref[...] += dg

    return pl.pallas_call(
        body, name=name, grid=(S // tm,),
        in_specs=[pl.BlockSpec((tm, D), lambda t: (t, 0)), pl.BlockSpec((1, D), lambda t: (0, 0)),
                  pl.BlockSpec((P, tm, D), lambda t: (0, t, 0)), pl.BlockSpec((tm, D), lambda t: (t, 0))],
        out_specs=[pl.BlockSpec((tm, D), lambda t: (t, 0)), pl.BlockSpec((1, D), lambda t: (0, 0))],
        out_shape=[jax.ShapeDtypeStruct((S, D), F32), jax.ShapeDtypeStruct((1, D), F32)],
        compiler_params=_cp(("arbitrary",)),
    )(x, g, dh_parts, dres)


def f_attn_sb(x, g, w):
    pr = mm(rms(x, g), w)
    return pr[:, :SB_W], pr[:, SB_W:2 * SB_W], pr[:, 2 * SB_W:]


def _pairs(y):
    return jnp.stack([y[:, 128 * j:128 * (j + 1)] for j in range(DL_PAIRS)])


def f_attn_qk(x, g, w, nrm):
    pr = mm(rms(x, g), w)
    ms = group_sum(pr * pr, DL_HEADS) * (1.0 / HEAD)
    return _pairs(pr * lax.rsqrt(ms + NORM_EPS) * jnp.concatenate([nrm] * DL_HEADS, axis=1))


def f_attn_v(x, g, w):
    return _pairs(mm(rms(x, g), w))


def _masked(strict, x):
    return x if strict is None else jnp.where(strict, x, 0.0)


def _head_stack(x):
    nh = x.shape[1] // HEAD
    lane_head = lax.broadcasted_iota(jnp.int32, (1, x.shape[1]), 1) // HEAD
    return jnp.concatenate([jnp.where(lane_head == h, x, 0.0) for h in range(nh)], axis=0).astype(BF16)


def _head_pick(xs):
    nh = xs.shape[1] // HEAD
    rows = xs.shape[0] // nh
    lane_head = lax.broadcasted_iota(jnp.int32, (1, xs.shape[1]), 1) // HEAD
    out = xs[:rows]
    for h in range(1, nh):
        out = jnp.where(lane_head == h, xs[rows * h:rows * (h + 1)], out)
    return out


def _sb_tiles(qs, kblk, strict):
    z = _dg(qs, kblk, ((1,), (1,))) * (HEAD ** -0.5)
    keep = -(jnp.maximum(z, 0.0) + jnp.log(1.0 + jnp.exp(-jnp.abs(z))))
    return z, _masked(strict, keep)


def _tri(n, upper):
    r = lax.broadcasted_iota(jnp.int32, (n, n), 0)
    c = lax.broadcasted_iota(jnp.int32, (n, n), 1)
    return ((r > c) if upper else (r < c)).astype(BF16)


def _tri_sums(x, tri):
    hi, lo = _split2(x)
    return _dg(hi, tri, ((1,), (0,))) + _dg(lo, tri, ((1,), (0,)))


SB_UNROLL = 4


def _sb_diag(tb, nh):
    r = lax.broadcasted_iota(jnp.int32, (nh * tb, tb), 0)
    return lax.broadcasted_iota(jnp.int32, (nh * tb, tb), 1) < lax.rem(r, tb)


def _sb_sweep(step, first, count, carry, direction, commit=None):
    def run(kbs, c):
        outs = []
        for kb in kbs:
            c, out = step(kb, c)
            outs.append(out)
        if commit is not None:
            for kb, out in zip(kbs, outs):
                commit(kb, out)
        return c

    rem = count % SB_UNROLL
    carry = lax.fori_loop(0, rem, lambda i, c: run([first + direction * i], c), carry)
    return lax.fori_loop(
        0, count // SB_UNROLL,
        lambda g, c: run([first + direction * (rem + SB_UNROLL * g + u) for u in range(SB_UNROLL)], c), carry)


def _riding(ride, refs, n_in, n_out, first, last):
    if ride is None:
        return refs, lambda: None
    n = ride.n
    own = refs[:n_in] + refs[n_in + n:n_in + n + n_out] + refs[n_in + 2 * n + n_out:len(refs) - 2]
    start, wait = ride.ops(refs[n_in:n_in + n], refs[n_in + n + n_out:n_in + 2 * n + n_out], refs[-2], refs[-1])
    pl.when(first)(start)
    return own, lambda: pl.when(last)(wait)


def _ride_specs(ride):
    if ride is None:
        return [], [], [], [], []
    return [_HBM] * ride.n, [_HBM] * ride.n, ride.out_shapes, ride.sem_shapes(), ride.arrays


def sb_fwd(q, k, v, ride=None, tb=QBLK):
    S = q.shape[0]
    nh = SB_W // HEAD
    nb = S // tb
    r_in, r_out, r_shape, r_scr, r_args = _ride_specs(ride)

    def body(*refs):
        qb = pl.program_id(0)
        (q_ref, k_ref, v_ref, o_ref, w_ref), finish = _riding(ride, refs, 3, 2, qb == 0, qb == nb - 1)
        diag = _sb_diag(tb, nh)
        after_mat = _tri(tb, True)
        qs = _head_stack(q_ref[...])

        def step(kb, carry, strict):
            acc, run = carry
            rows = pl.ds(pl.multiple_of(kb * tb, tb), tb)
            z, keep = _sb_tiles(qs, k_ref[rows, :].astype(BF16), strict)
            w = _masked(strict, jnp.exp(z + keep + _tri_sums(keep, after_mat) + run)).astype(BF16)
            w_ref[0, kb] = w
            acc = acc + _dg(w, v_ref[rows, :].astype(BF16), ((1,), (0,)))
            return acc, run + jnp.sum(keep, axis=1, keepdims=True)

        init = (jnp.zeros((nh * tb, SB_W), F32), jnp.zeros((nh * tb, 1), F32))
        carry = step(qb, init, diag)
        acc, _ = _sb_sweep(lambda kb, c: (step(kb, c, None), None), qb - 1, qb, carry, -1)
        o_ref[...] = _head_pick(acc)
        finish()

    return pl.pallas_call(
        body, name="sb_fwd", grid=(S // tb,),
        in_specs=[pl.BlockSpec((tb, SB_W), lambda i: (i, 0)), pl.BlockSpec((S, SB_W), lambda i: (0, 0)),
                  pl.BlockSpec((S, SB_W), lambda i: (0, 0))] + r_in,
        out_specs=[pl.BlockSpec((tb, SB_W), lambda i: (i, 0)),
                   pl.BlockSpec((1, nb, nh * tb, tb), lambda i: (i, 0, 0, 0))] + r_out,
        out_shape=[jax.ShapeDtypeStruct((S, SB_W), F32), jax.ShapeDtypeStruct((nb, nb, nh * tb, tb), BF16)] + r_shape,
        scratch_shapes=r_scr,
        compiler_params=_cp(("arbitrary",)),
    )(q, k, v, *r_args)


def sb_bwd(q, k, v, do, wts, ride=None, tb=QBLK):
    S = q.shape[0]
    nh = SB_W // HEAD
    nb = S // tb
    scale = HEAD ** -0.5
    r_in, r_out, r_shape, r_scr, r_args = _ride_specs(ride)

    def body(*refs):
        qb = pl.program_id(0)
        (q_ref, k_ref, v_ref, do_ref, w_ref, dq_ref, dk_ref, dv_ref, g_scr), finish = _riding(
            ride, refs, 5, 3, qb == 0, qb == nb - 1)

        @pl.when(qb == 0)
        def _():
            dk_ref[...] = jnp.zeros_like(dk_ref)
            dv_ref[...] = jnp.zeros_like(dv_ref)

        diag = _sb_diag(tb, nh)
        before_mat = _tri(tb, False)
        qs = _head_stack(q_ref[...])
        dos = _head_stack(do_ref[...])

        def weights_pass(kb, carry):
            rows = pl.ds(pl.multiple_of(kb * tb, tb), tb)
            w = w_ref[0, kb]
            g_scr[kb] = _dg(dos, v_ref[rows, :].astype(BF16), ((1,), (1,))) * w.astype(F32)
            return carry, _dg(w, dos, ((0,), (0,)))

        def add_rows(ref):
            def commit(kb, val):
                ref[pl.ds(pl.multiple_of(kb * tb, tb), tb), :] += val
            return commit

        zero_run = jnp.zeros((nh * tb, 1), F32)
        _sb_sweep(weights_pass, 0, qb + 1, 0, 1, add_rows(dv_ref))

        def left_to_right(kb, carry, strict):
            dq, run = carry
            rows = pl.ds(pl.multiple_of(kb * tb, tb), tb)
            kblk = k_ref[rows, :].astype(BF16)
            gw = g_scr[kb]
            sig = jax.nn.sigmoid(_dg(qs, kblk, ((1,), (1,))) * scale)
            dkeep = _masked(strict, _tri_sums(gw, before_mat) + run)
            dz = ((gw * (1.0 - sig) - dkeep * sig) * scale).astype(BF16)
            dq = dq + _dg(dz, kblk, ((1,), (0,)))
            return (dq, run + jnp.sum(gw, axis=1, keepdims=True)), _dg(dz, qs, ((0,), (0,)))

        carry = _sb_sweep(lambda kb, c: left_to_right(kb, c, None), 0, qb,
                          (jnp.zeros((nh * tb, SB_W), F32), zero_run), 1, add_rows(dk_ref))
        (dq, _), dk_diag = left_to_right(qb, carry, diag)
        add_rows(dk_ref)(qb, dk_diag)
        dq_ref[...] = _head_pick(dq)
        finish()

    whole = pl.BlockSpec((S, SB_W), lambda i: (0, 0))
    blk = pl.BlockSpec((tb, SB_W), lambda i: (i, 0))
    return pl.pallas_call(
        body, name="sb_bwd", grid=(S // tb,),
        in_specs=[blk, whole, whole, blk, pl.BlockSpec((1, nb, nh * tb, tb), lambda i: (i, 0, 0, 0))] + r_in,
        out_specs=[blk, whole, whole] + r_out,
        out_shape=[jax.ShapeDtypeStruct((S, SB_W), F32)] * 3 + r_shape,
        scratch_shapes=[pltpu.VMEM((S // tb, nh * tb, tb), F32)] + r_scr,
        compiler_params=_cp(("arbitrary",)),
    )(q, k, v, do, wts, *r_args)


def reorder(name, x, groups, inverse):
    P, S, _ = x.shape

    def body(x_ref, o_ref):
        p = pl.program_id(0)
        for gi, r in enumerate(groups):
            @pl.when(p // 2 == gi)
            def _(r=r):
                L = S // r
                if r == 1:
                    o_ref[...] = x_ref[...]
                for c in range(r if r > 1 else 0):
                    if inverse:
                        o_ref[pl.ds(c, L, stride=r), :] = x_ref[c * L:(c + 1) * L, :]
                    else:
                        o_ref[c * L:(c + 1) * L, :] = x_ref[pl.ds(c, L, stride=r), :]

    slab = pl.BlockSpec((None, S, 128), lambda p: (p, 0, 0))
    return pl.pallas_call(
        body, name=name, grid=(P,), in_specs=[slab], out_specs=slab,
        out_shape=jax.ShapeDtypeStruct(x.shape, x.dtype), compiler_params=_cp(("parallel",)),
    )(x)


def _dil_blocks(S):
    return S // QBLK


def _dil_mask(n_in_stream):
    qi = lax.broadcasted_iota(jnp.int32, (QBLK, 2 * QBLK), 0)
    kj = lax.broadcasted_iota(jnp.int32, (QBLK, 2 * QBLK), 1) - QBLK
    dist = qi - kj
    return (dist >= 0) & (dist <= QBLK) & ((n_in_stream > 0) | (kj >= 0))


def _stream_pos(gi, i, S):
    nb = jnp.where(gi == 0, S // (QBLK * DIL[0]), jnp.where(gi == 1, S // (QBLK * DIL[1]), S // (QBLK * DIL[2])))
    return i % nb


def dil_fwd(q, k, v, bias, ride=None):
    S = q.shape[1]
    nblk = _dil_blocks(S)
    r_in, r_out, r_shape, r_scr, r_args = _ride_specs(ride)

    def body(*refs):
        gi, i = pl.program_id(0), pl.program_id(1)
        (q_ref, kc_ref, kp_ref, vc_ref, vp_ref, b_ref, o_ref, l_ref), finish = _riding(
            ride, refs, 6, 2, (gi == 0) & (i == 0), (gi == len(DIL) - 1) & (i == nblk - 1))
        mask = _dil_mask(_stream_pos(gi, i, S))
        for j in range(2):
            q2, kc, kp, vc, vp = q_ref[j], kc_ref[j], kp_ref[j], vc_ref[j], vp_ref[j]
            os_, ls_ = [], []
            for hh in range(2):
                sl = slice(HEAD * hh, HEAD * (hh + 1))
                kw = jnp.concatenate([kp[:, sl], kc[:, sl]], axis=0)
                vw = jnp.concatenate([vp[:, sl], vc[:, sl]], axis=0)
                lg = _bdot(q2[:, sl], kw, ((1,), (1,))) * (HEAD ** -0.5) + b_ref[2 * j + hh]
                lg = jnp.where(mask, lg, NEG_INF)
                m = jnp.max(lg, axis=-1, keepdims=True)
                p = jnp.exp(lg - m)
                den = jnp.sum(p, axis=-1, keepdims=True)
                os_.append(_bdot(p / den, vw, ((1,), (0,))))
                ls_.append(jnp.broadcast_to(m + jnp.log(den), (QBLK, HEAD)))
            o_ref[j] = jnp.concatenate(os_, axis=1)
            l_ref[j] = jnp.concatenate(ls_, axis=1)
        finish()

    cur = pl.BlockSpec((2, QBLK, 128), lambda g, i: (g, i, 0))
    prev = pl.BlockSpec((2, QBLK, 128), lambda g, i: (g, jnp.maximum(i - 1, 0), 0))
    return pl.pallas_call(
        body, name="dil_fwd", grid=(len(DIL), nblk),
        in_specs=[cur, cur, prev, cur, prev, pl.BlockSpec((4, QBLK, 2 * QBLK), lambda g, i: (g, 0, 0))] + r_in,
        out_specs=[cur, cur] + r_out,
        out_shape=[jax.ShapeDtypeStruct(q.shape, F32)] * 2 + r_shape,
        scratch_shapes=r_scr,
        compiler_params=_cp(("arbitrary", "arbitrary")),
    )(q, k, k, v, v, bias, *r_args)


def dil_bwd(q, k, v, bias, o, lse, do, dlse, ride=None):
    S = q.shape[1]
    nblk = _dil_blocks(S)
    r_in, r_out, r_shape, r_scr, r_args = _ride_specs(ride)

    def body(*refs):
        gi, i = pl.program_id(0), pl.program_id(1)
        (q_ref, kc_ref, kp_ref, vc_ref, vp_ref, b_ref, o_ref, l_ref, do_ref, dl_ref,
         dq_ref, dk_ref, dv_ref, ds_ref, dk_car, dv_car), finish = _riding(
            ride, refs, 10, 4, (gi == 0) & (i == 0), (gi == len(DIL) - 1) & (i == nblk))

        @pl.when(i == 0)
        def _():
            ds_ref[...] = jnp.zeros_like(ds_ref)
            dk_car[...] = jnp.zeros_like(dk_car)
            dv_car[...] = jnp.zeros_like(dv_car)

        @pl.when(i < nblk)
        def _():
            mask = _dil_mask(_stream_pos(gi, i, S))
            for j in range(2):
                q2, kc, kp, vc, vp = q_ref[j], kc_ref[j], kp_ref[j], vc_ref[j], vp_ref[j]
                o2, l2, do2, dl2 = o_ref[j], l_ref[j], do_ref[j], dl_ref[j]
                dqs, dkps, dkcs, dvps, dvcs = [], [], [], [], []
                for hh in range(2):
                    sl = slice(HEAD * hh, HEAD * (hh + 1))
                    qh, doh = q2[:, sl], do2[:, sl]
                    kw = jnp.concatenate([kp[:, sl], kc[:, sl]], axis=0)
                    vw = jnp.concatenate([vp[:, sl], vc[:, sl]], axis=0)
                    lg = _bdot(qh, kw, ((1,), (1,))) * (HEAD ** -0.5) + b_ref[2 * j + hh]
                    p = jnp.where(mask, jnp.exp(lg - l2[:, HEAD * hh:HEAD * hh + 1]), 0.0)
                    dp = _bdot(doh, vw, ((1,), (1,)))
                    delta = jnp.sum(doh * o2[:, sl], axis=-1, keepdims=True)
                    dl = jnp.sum(dl2[:, sl], axis=-1, keepdims=True)
                    ds = p * (dp - delta + dl)
                    ds_ref[2 * j + hh] += ds
                    dsq = ds * (HEAD ** -0.5)
                    dqs.append(_bdot(dsq, kw, ((1,), (0,))))
                    dkw = _bdot(dsq, qh, ((0,), (0,)))
                    dvw = _bdot(p, doh, ((0,), (0,)))
                    dkps.append(dkw[:QBLK])
                    dkcs.append(dkw[QBLK:])
                    dvps.append(dvw[:QBLK])
                    dvcs.append(dvw[QBLK:])
                dq_ref[j] = jnp.concatenate(dqs, axis=1)
                dk_ref[j] = dk_car[j] + jnp.concatenate(dkps, axis=1)
                dv_ref[j] = dv_car[j] + jnp.concatenate(dvps, axis=1)
                dk_car[j] = jnp.concatenate(dkcs, axis=1)
                dv_car[j] = jnp.concatenate(dvcs, axis=1)

        @pl.when(i == nblk)
        def _():
            dk_ref[...] = dk_car[...]
            dv_ref[...] = dv_car[...]

        finish()

    cur = pl.BlockSpec((2, QBLK, 128), lambda g, i: (g, jnp.minimum(i, nblk - 1), 0))
    prev = pl.BlockSpec((2, QBLK, 128), lambda g, i: (g, jnp.clip(i - 1, 0, nblk - 1), 0))
    bspec = pl.BlockSpec((4, QBLK, 2 * QBLK), lambda g, i: (g, 0, 0))
    return pl.pallas_call(
        body, name="dil_bwd", grid=(len(DIL), nblk + 1),
        in_specs=[cur, cur, prev, cur, prev, bspec, cur, cur, cur, cur] + r_in,
        out_specs=[cur, prev, prev, bspec] + r_out,
        out_shape=[jax.ShapeDtypeStruct(q.shape, F32)] * 3 + [jax.ShapeDtypeStruct(bias.shape, F32)] + r_shape,
        scratch_shapes=[pltpu.VMEM((2, QBLK, 128), F32), pltpu.VMEM((2, QBLK, 128), F32)] + r_scr,
        compiler_params=_cp(("arbitrary", "arbitrary")),
    )(q, k, k, v, v, bias, o, lse, do, dlse, *r_args)


def _t5_bucket(dist):
    max_exact = N_BUCKETS // 2
    d = jnp.maximum(dist, 1).astype(F32)
    large = max_exact + (jnp.log(d / max_exact) / math.log(MAX_DISTANCE / max_exact)
                         * (N_BUCKETS - max_exact)).astype(jnp.int32)
    large = jnp.minimum(large, N_BUCKETS - 1)
    return jnp.where(dist < max_exact, dist, large)


def _bucket_maps():
    qi = jnp.arange(QBLK)[:, None]
    kj = jnp.arange(2 * QBLK)[None, :] - QBLK
    dist = jnp.maximum(qi - kj, 0)
    return jnp.stack([_t5_bucket(dist * r) for r in DIL])


def bias_table(rel_bias, buckets):
    def body(tbl_ref, bk_ref, o_ref):
        for h in range(DL_HEADS):
            bk = bk_ref[h // 4]

            def step(b, acc):
                return jnp.where(bk == b, tbl_ref[b, h], acc)

            o_ref[h] = lax.fori_loop(0, N_BUCKETS, step, jnp.zeros(bk.shape, F32))

    return pl.pallas_call(
        body, name="bias_table", out_shape=jax.ShapeDtypeStruct((DL_HEADS,) + buckets.shape[1:], F32),
        in_specs=[pl.BlockSpec(memory_space=pltpu.SMEM), pl.BlockSpec(memory_space=pltpu.VMEM)],
        out_specs=pl.BlockSpec(memory_space=pltpu.VMEM),
    )(rel_bias, buckets)


def bias_grad(ds, buckets):
    def body(ds_ref, bk_ref, o_ref):
        lane = lax.broadcasted_iota(jnp.int32, (1, 128), 1)
        for h in range(DL_HEADS):
            dsv = ds_ref[h]
            bk = bk_ref[h // 4]

            def step(b, row):
                return jnp.where(lane == b, jnp.sum(jnp.where(bk == b, dsv, 0.0)), row)

            o_ref[h:h + 1, :] = lax.fori_loop(0, N_BUCKETS, step, jnp.zeros((1, 128), F32))

    return pl.pallas_call(
        body, name="bias_grad", out_shape=jax.ShapeDtypeStruct((DL_HEADS, 128), F32),
        in_specs=[pl.BlockSpec(memory_space=pltpu.VMEM)] * 2, out_specs=pl.BlockSpec(memory_space=pltpu.VMEM),
    )(ds, buckets)


def f_attn_out(x, oa, o, lse, w):
    og = [jnp.concatenate([o[2 * g], o[2 * g + 1]], axis=1) for g in range(3)]
    lg = [jnp.concatenate([lse[2 * g], lse[2 * g + 1]], axis=1) for g in range(3)]
    m = jnp.maximum(jnp.maximum(lg[0], lg[1]), lg[2])
    e = [jnp.exp(l - m) for l in lg]
    den = e[0] + e[1] + e[2]
    ob = (e[0] * og[0] + e[1] * og[1] + e[2] * og[2]) / den
    return x + mm(jnp.concatenate([oa, ob], axis=1), w)


def norm_shift_fwd(x, g, tm=256):
    S = x.shape[0]

    def body(x_ref, xp_ref, g_ref, h_ref, hs_ref):
        h = rms(x_ref[...], g_ref[...])
        hp = rms(xp_ref[7:8, :], g_ref[...])
        hp = jnp.where(pl.program_id(0) == 0, 0.0, hp)
        row = lax.broadcasted_iota(jnp.int32, (tm, D), 0)
        h_ref[...] = h
        hs_ref[...] = jnp.where(row == 0, hp, pltpu.roll(h, 1, 0))

    return pl.pallas_call(
        body, name="rw_norm_shift", grid=(S // tm,),
        in_specs=[pl.BlockSpec((tm, D), lambda t: (t, 0)),
                  pl.BlockSpec((8, D), lambda t: (jnp.maximum(t * (tm // 8) - 1, 0), 0)),
                  pl.BlockSpec((1, D), lambda t: (0, 0))],
        out_specs=[pl.BlockSpec((tm, D), lambda t: (t, 0))] * 2,
        out_shape=[jax.ShapeDtypeStruct((S, D), F32)] * 2,
        compiler_params=_cp(("parallel",)),
    )(x, x, g)


def norm_shift_bwd(x, g, dh, dhs, dres, tm=256):
    S = x.shape[0]
    nt = S // tm

    def body(x_ref, g_ref, dh_ref, dhs_ref, dhn_ref, dr_ref, dx_ref, dg_ref):
        t = pl.program_id(0)
        nxt = jnp.where(t == nt - 1, 0.0, dhn_ref[0:1, :])
        row = lax.broadcasted_iota(jnp.int32, (tm, D), 0)
        tot = dh_ref[...] + jnp.where(row == tm - 1, nxt, pltpu.roll(dhs_ref[...], tm - 1, 0))
        _, vjp = jax.vjp(rms, x_ref[...], g_ref[...])
        dx, dg = vjp(tot)
        dx_ref[...] = dr_ref[...] + dx

        @pl.when(t == 0)
        def _():
            dg_ref[...] = dg

        @pl.when(t != 0)
        def _():
            dg_ref[...] += dg

    tile = pl.BlockSpec((tm, D), lambda t: (t, 0))
    return pl.pallas_call(
        body, name="rw_norm_shift_bwd", grid=(nt,),
        in_specs=[tile, pl.BlockSpec((1, D), lambda t: (0, 0)), tile, tile,
                  pl.BlockSpec((8, D), lambda t: (jnp.minimum((t + 1) * (tm // 8), S // 8 - 1), 0)), tile],
        out_specs=[tile, pl.BlockSpec((1, D), lambda t: (0, 0))],
        out_shape=[jax.ShapeDtypeStruct((S, D), F32), jax.ShapeDtypeStruct((1, D), F32)],
        compiler_params=_cp(("arbitrary",)),
    )(x, g, dh, dhs, dhs, dres)


def f_rw_proj(h, hs, mix, w):
    return mm(h + (hs - h) * mix, w)


def f_rw_mid(h, hs, r, k, v, mix3, w0, a0, kkw, kaw, w1, w2, a1, a2, g1, g2):
    xx = hs - h
    xw, xa, xg = h + xx * mix3[0:1], h + xx * mix3[1:2], h + xx * mix3[2:3]
    w_log = -softplus(-(w0 + mm(jnp.tanh(mm(xw, w1)), w2))) - 0.5
    lw = -jnp.exp(w_log)
    ag = jax.nn.sigmoid(a0 + mm(mm(xa, a1), a2))
    gate = mm(jax.nn.sigmoid(mm(xg, g1)), g2)
    kk = k * kkw
    kk = kk / jnp.maximum(jnp.sqrt(group_sum(kk * kk, RW_H)), 1e-12)
    kmod = k * (1.0 + (ag - 1.0) * kaw)
    return (to_heads(r), to_heads(lw), to_heads(kmod), to_heads(v), to_heads(-kk), to_heads(kk * ag), gate)


def f_rw_post(yh, rh, kh, vh, gate, x, lng, lnb, rk, wo):
    mu = jnp.mean(yh, axis=-1, keepdims=True)
    var = jnp.mean(jnp.square(yh - mu), axis=-1, keepdims=True)
    yn = (yh - mu) * lax.rsqrt(var + GN_EPS)
    bonus = jnp.sum(rh * kh * rk, axis=-1, keepdims=True) * vh
    y = from_heads(yn) * lng + lnb + from_heads(bonus)
    return x + mm(y * gate, wo)


def _split2(x):
    hi = x.astype(BF16)
    return hi, (x - hi.astype(F32)).astype(BF16)


def _b3(x, y, cx, cy):
    dn = (((cx,), (cy,)), ((0,), (0,)))
    xh, xl = _split2(x)
    yh, yl = _split2(y)
    d = lambda p, q: lax.dot_general(p, q, dn, preferred_element_type=F32)
    return d(xh, yh) + (d(xh, yl) + d(xl, yh))


@jax.custom_vjp
def b_nt(x, y):
    return _b3(x, y, 2, 2)


@jax.custom_vjp
def b_nn(x, y):
    return _b3(x, y, 2, 1)


@jax.custom_vjp
def b_tn(x, y):
    return _b3(x, y, 1, 1)


def _b1(x, y, cx, cy):
    return lax.dot_general(x.astype(BF16), y.astype(BF16), (((cx,), (cy,)), ((0,), (0,))), preferred_element_type=F32)


b_nt.defvjp(lambda x, y: (b_nt(x, y), (x, y)), lambda r, g: (_b1(g, r[1], 2, 1), _b1(g, r[0], 1, 1)))
b_nn.defvjp(lambda x, y: (b_nn(x, y), (x, y)), lambda r, g: (_b1(g, r[1], 2, 2), _b1(r[0], g, 1, 1)))
b_tn.defvjp(lambda x, y: (b_tn(x, y), (x, y)), lambda r, g: (_b1(r[1], g, 2, 2), _b1(r[0], g, 2, 1)))


def _tri_apply(x, lower):
    H, C, _ = x.shape
    ii = lax.broadcasted_iota(jnp.int32, (C, C), 0)
    jj = lax.broadcasted_iota(jnp.int32, (C, C), 1)
    m = jnp.broadcast_to(((jj <= ii) if lower else (jj >= ii)).astype(BF16), (H, C, C))
    x1 = x.astype(BF16)
    r1 = x - x1.astype(F32)
    x2 = r1.astype(BF16)
    x3 = (r1 - x2.astype(F32)).astype(BF16)
    d = lambda q: lax.dot_general(m, q, (((2,), (1,)), ((0,), (0,))), preferred_element_type=F32)
    return d(x1) + (d(x2) + d(x3))


@jax.custom_vjp
def run_sum(x):
    return _tri_apply(x, True)


run_sum.defvjp(lambda x: (run_sum(x), None), lambda _, g: (_tri_apply(g, False),))


def rwkv_chunk(S0, r, lw, k, v, a, b):
    H, C, _ = r.shape
    V = S0.shape[1]
    ii = lax.broadcasted_iota(jnp.int32, (C, C), 0)
    jj = lax.broadcasted_iota(jnp.int32, (C, C), 1)
    strict = jj < ii
    i2 = lax.broadcasted_iota(jnp.int32, (C, 2 * C), 0)
    j2 = lax.broadcasted_iota(jnp.int32, (C, 2 * C), 1)
    incl2 = jnp.where(j2 >= C, j2 - C, j2) <= i2
    g = run_sum(lw)
    ig = jnp.exp(-g)
    ar = jnp.concatenate([a * jnp.exp(g - lw), r * jnp.exp(g)], axis=1)
    bk = jnp.concatenate([b * ig, k * ig], axis=1)
    m = b_nt(ar, bk)
    a_ab = jnp.where(strict, m[:, :C, :C], 0.0)
    a_ak = jnp.where(strict, m[:, :C, C:], 0.0)
    b_r = jnp.where(incl2, m[:, C:, :], 0.0)
    p = b_nt(ar, S0)
    u = p[:, :C] + b_nn(a_ak, v)
    nmat, n = a_ab, 1
    while n < C:
        n *= 2
        if n < C:
            z = b_nn(nmat, jnp.concatenate([u, nmat], axis=2))
            u, nmat = u + z[:, :, :V], z[:, :, V:]
        else:
            u = u + b_nn(nmat, u)
    uv = jnp.concatenate([u, v], axis=1)
    y = p[:, C:] + b_nn(b_r, uv)
    g_end = g[:, C - 1:C, :]
    dec = jnp.exp(g_end - g)
    s_new = S0 * jnp.exp(g_end) + b_tn(uv, jnp.concatenate([b * dec, k * dec], axis=1))
    return y, s_new


def rwkv_fwd(r, lw, k, v, a, b):
    H, S, _ = r.shape
    C = RW_CHUNK

    def body(r_ref, lw_ref, k_ref, v_ref, a_ref, b_ref, y_ref, s_ref, s_scr):
        @pl.when(pl.program_id(0) == 0)
        def _():
            s_scr[...] = jnp.zeros_like(s_scr)

        s0 = s_scr[...]
        s_ref[0] = s0
        y, s1 = rwkv_chunk(s0, r_ref[...], lw_ref[...], k_ref[...], v_ref[...], a_ref[...], b_ref[...])
        y_ref[...] = y
        s_scr[...] = s1

    bs = pl.BlockSpec((H, C, HEAD), lambda c: (0, c, 0))
    return pl.pallas_call(
        body, name="rwkv_fwd", grid=(S // C,), in_specs=[bs] * 6,
        out_specs=[bs, pl.BlockSpec((1, H, HEAD, HEAD), lambda c: (c, 0, 0, 0))],
        out_shape=[jax.ShapeDtypeStruct((H, S, HEAD), F32), jax.ShapeDtypeStruct((S // C, H, HEAD, HEAD), F32)],
        scratch_shapes=[pltpu.VMEM((H, HEAD, HEAD), F32)],
        compiler_params=_cp(("arbitrary",)),
    )(r, lw, k, v, a, b)


def rwkv_bwd(r, lw, k, v, a, b, states, dy):
    H, S, _ = r.shape
    C = RW_CHUNK
    nc = S // C

    def body(r_ref, lw_ref, k_ref, v_ref, a_ref, b_ref, s_ref, dy_ref, dr, dlw, dk, dv, da, db, ds_scr):
        @pl.when(pl.program_id(0) == 0)
        def _():
            ds_scr[...] = jnp.zeros_like(ds_scr)

        _, vjp = jax.vjp(rwkv_chunk, s_ref[0], r_ref[...], lw_ref[...], k_ref[...], v_ref[...], a_ref[...], b_ref[...])
        grads = vjp((dy_ref[...], ds_scr[...]))
        ds_scr[...] = grads[0]
        for o, gv in zip((dr, dlw, dk, dv, da, db), grads[1:]):
            o[...] = gv

    bs = pl.BlockSpec((H, C, HEAD), lambda c: (0, nc - 1 - c, 0))
    return pl.pallas_call(
        body, name="rwkv_bwd", grid=(nc,),
        in_specs=[bs] * 6 + [pl.BlockSpec((1, H, HEAD, HEAD), lambda c: (nc - 1 - c, 0, 0, 0)), bs],
        out_specs=[bs] * 6, out_shape=[jax.ShapeDtypeStruct((H, S, HEAD), F32)] * 6,
        scratch_shapes=[pltpu.VMEM((H, HEAD, HEAD), F32)],
        compiler_params=_cp(("arbitrary",)),
    )(r, lw, k, v, a, b, states, dy)


def loss_head(y, target, tm=512):
    S = y.shape[0]

    def body(y_ref, t_ref, dy_ref, l_ref):
        e = y_ref[...] - t_ref[...]
        dy_ref[...] = e * (1.0 / D)
        part = jnp.broadcast_to(0.5 * jnp.sum(jnp.mean(e * e, axis=-1, keepdims=True)), (1, 128))

        @pl.when(pl.program_id(0) == 0)
        def _():
            l_ref[...] = part

        @pl.when(pl.program_id(0) != 0)
        def _():
            l_ref[...] += part

    tile = pl.BlockSpec((tm, D), lambda t: (t, 0))
    return pl.pallas_call(
        body, name="loss_head", grid=(S // tm,), in_specs=[tile, tile],
        out_specs=[tile, pl.BlockSpec((1, 128), lambda t: (0, 0))],
        out_shape=[jax.ShapeDtypeStruct((S, D), F32), jax.ShapeDtypeStruct((1, 128), F32)],
        compiler_params=_cp(("arbitrary",)),
    )(y, target)


def _row_tile(rows, cols, budget=1 << 19):
    best = None
    for tr in range(8, rows + 1, 8):
        if rows % tr == 0 and tr * cols <= budget:
            best = tr
    return best or rows


def _adam(w, g, m, v):
    m = ADAM_B1 * m + (1.0 - ADAM_B1) * g
    v = ADAM_B2 * v + (1.0 - ADAM_B2) * jnp.square(g)
    m_hat = m / (1.0 - ADAM_B1 ** ADAM_STEP)
    v_hat = v / (1.0 - ADAM_B2 ** ADAM_STEP)
    return -ADAM_LR * (m_hat / (jnp.sqrt(v_hat) + ADAM_EPS) + ADAM_WD * w), m, v


def sum_slots(name, parts, dtype=F32, extras=()):
    n = 0 if parts is None else parts.shape[0]
    R, C = extras[0].shape if parts is None else parts.shape[1:]
    tr = _row_tile(R, C * (n + len(extras)))
    ins = ([] if parts is None else [parts]) + list(extras)

    def body(*refs):
        terms = [] if parts is None else [refs[0][i] for i in range(n)]
        terms += [r[...] for r in refs[len(ins) - len(extras):len(ins)]]
        s = terms[0].astype(F32)
        for t in terms[1:]:
            s = s + t.astype(F32)
        refs[len(ins)][...] = s.astype(dtype)

    tile = pl.BlockSpec((tr, C), lambda t: (t, 0))
    return pl.pallas_call(
        body, name=name, grid=(R // tr,),
        in_specs=([] if parts is None else [pl.BlockSpec((n, tr, C), lambda t: (0, t, 0))]) + [tile] * len(extras),
        out_specs=tile, out_shape=jax.ShapeDtypeStruct((R, C), dtype), compiler_params=_cp(("parallel",)),
    )(*ins)


def adam_step(name, ga, gb, w, m, v):
    R, C = w.shape
    tr = _row_tile(R, C, 1 << 17)
    ins = [ga] + ([gb] if gb is not None else []) + [w, m, v]

    def body(*refs):
        g = refs[0][...]
        if gb is not None:
            g = g + refs[1][...]
        w_ref, m_ref, v_ref, g_out, d_out, m_out, v_out = refs[len(ins) - 3:]
        d, m2, v2 = _adam(w_ref[...], g, m_ref[...], v_ref[...])
        g_out[...] = g
        d_out[...] = d
        m_out[...] = m2
        v_out[...] = v2

    tile = pl.BlockSpec((tr, C), lambda t: (t, 0))
    return pl.pallas_call(
        body, name=name, grid=(R // tr,), in_specs=[tile] * len(ins), out_specs=[tile] * 4,
        out_shape=[jax.ShapeDtypeStruct((R, C), F32)] * 4, compiler_params=_cp(("parallel",)),
    )(*ins)


def _place():
    return lax.axis_index("x"), lax.axis_index("y"), lax.axis_index("c")


def _flip(me, mask):
    return tuple(1 - v if mk else v for v, mk in zip(me, mask))


CHIP_MASKS = ((1, 0, 0), (0, 1, 0), (1, 1, 0))
ALL_MASKS = tuple((a, b, c) for a in (0, 1) for b in (0, 1) for c in (0, 1) if (a, b, c) != (0, 0, 0))


def _chip(dev):
    return 2 * dev[0] + dev[1]


def _devno(dev):
    return 4 * dev[0] + 2 * dev[1] + dev[2]


class Pushes:
    def __init__(self, arrays, out_shapes, masks, copies, src_of, dst_of, alias=False):
        self.arrays, self.out_shapes, self.masks, self.copies = list(arrays), list(out_shapes), masks, copies
        self.src_of, self.dst_of, self.alias = src_of, dst_of, alias
        self.n = len(self.arrays)

    def sem_shapes(self):
        k = self.n * len(self.masks) * self.copies
        return [pltpu.SemaphoreType.DMA((k,)), pltpu.SemaphoreType.DMA((k,))]

    def ops(self, ins, outs, send_sems, recv_sems):
        me = _place()
        sends, lands = [], []
        for i in range(self.n):
            for j, mk in enumerate(self.masks):
                peer = _flip(me, mk)
                srcs, dsts = self.src_of(ins[i], me, j), self.dst_of(outs[i], me, j)
                here = self.dst_of(outs[i], peer, j)
                for q in range(self.copies):
                    sem = (i * len(self.masks) + j) * self.copies + q
                    sends.append(pltpu.make_async_remote_copy(
                        src_ref=srcs[q], dst_ref=dsts[q], send_sem=send_sems.at[sem], recv_sem=recv_sems.at[sem],
                        device_id=peer, device_id_type=MESH))
                    lands.append(pltpu.make_async_remote_copy(
                        src_ref=here[q], dst_ref=here[q], send_sem=send_sems.at[sem], recv_sem=recv_sems.at[sem],
                        device_id=peer, device_id_type=MESH))

        def start():
            for cp in sends:
                cp.start()

        def wait():
            for cp in lands:
                cp.wait_recv()
            for cp in sends:
                cp.wait_send()

        return start, wait


_HBM = pl.BlockSpec(memory_space=pl.ANY)


def exchange(name, p, local_of=None):
    n = p.n

    def body(*refs):
        ins, outs = refs[:n], refs[n:2 * n]
        start, wait = p.ops(ins, outs, refs[2 * n], refs[2 * n + 1])
        locals_ = []
        if local_of is not None:
            for i in range(n):
                src, dst = local_of(ins[i], outs[i], _place())
                locals_.append(pltpu.make_async_copy(src, dst, refs[2 * n + 2].at[i]))
                locals_[-1].start()
        start()
        wait()
        for cp in locals_:
            cp.wait()

    return pl.pallas_call(
        body, name=name, in_specs=[_HBM] * n, out_specs=[_HBM] * n, out_shape=p.out_shapes,
        scratch_shapes=p.sem_shapes() + ([pltpu.SemaphoreType.DMA((n,))] if local_of is not None else []),
        input_output_aliases={i: i for i in range(n)} if p.alias else {},
    )(*p.arrays)


def _half(c, rows):
    return pl.ds(c * (rows // 2), rows // 2)


def gather_pushes(arrays):
    outs = [jax.ShapeDtypeStruct((N_CHIPS,) + a.shape, a.dtype) for a in arrays]
    sib = len(CHIP_MASKS)
    return Pushes(arrays, outs, CHIP_MASKS + ((0, 0, 1),), 1,
                  src_of=lambda r, me, j: [r] if j == sib else [r.at[_half(me[2], r.shape[0])]],
                  dst_of=lambda o, sender, j: [o.at[_chip(sender)]] if j == sib else
                  [o.at[_chip(sender), _half(sender[2], o.shape[1])]])


def gather_swap(name, got):
    outs = [jax.ShapeDtypeStruct(a.shape, a.dtype) for a in got]
    return exchange(name, Pushes(
        got, outs, ((0, 0, 1),), len(CHIP_MASKS),
        src_of=lambda r, me, j: [r.at[_chip(_flip(me, mk)), _half(me[2], r.shape[1])] for mk in CHIP_MASKS],
        dst_of=lambda o, sender, j: [o.at[_chip(_flip(sender, mk)), _half(sender[2], o.shape[1])] for mk in CHIP_MASKS],
        alias=True))


def reduce_swap(arrays):
    split = [a.reshape(N_CHIPS, 2, a.shape[1] // 2, a.shape[2]) for a in arrays]
    half_shapes = [jax.ShapeDtypeStruct((N_CHIPS,) + a.shape[2:], F32) for a in split]
    return split, Pushes(split, half_shapes, ((0, 0, 1),), 1,
                         src_of=lambda r, me, j: [r.at[:, 1 - me[2]]], dst_of=lambda o, sender, j: [o])


def reduce_begin(tag, names, arrays, wire):
    split, pushes = reduce_swap(arrays)
    return reduce_sum(names, split, exchange(f"grad_pre_swap_{tag}", pushes), wire)


def reduce_sum(names, split, theirs, wire):
    c = lax.axis_index("c")
    chip_sum = []
    for nm, a, t, dt in zip(names, split, theirs, wire):
        own = lax.dynamic_index_in_dim(a, c, axis=1, keepdims=False)
        flat = lambda v: v.reshape(-1, v.shape[-1])
        chip_sum.append(sum_slots(f"sum2_{nm}", None, dt, [flat(own), flat(t)]).reshape(t.shape))
    pushes = Pushes(chip_sum, [jax.ShapeDtypeStruct((len(CHIP_MASKS),) + a.shape[1:], a.dtype) for a in chip_sum],
                    CHIP_MASKS, 1,
                    src_of=lambda r, me, j: [r.at[_chip(_flip(me, CHIP_MASKS[j]))]],
                    dst_of=lambda o, sender, j: [o.at[j]])
    return chip_sum, pushes


def reduce_end(tag, names, chip_sum, landed):
    x, y, c = _place()
    halves = [sum_slots(f"sum4_{nm}", p, F32, [lax.dynamic_index_in_dim(a, _chip((x, y, c)), axis=0, keepdims=False)])
              for nm, p, a in zip(names, landed, chip_sum)]
    others = exchange(f"grad_final_swap_{tag}", Pushes(
        halves, [jax.ShapeDtypeStruct(a.shape, F32) for a in halves], ((0, 0, 1),), 1,
        src_of=lambda r, me, j: [r], dst_of=lambda o, sender, j: [o]))
    return [jnp.concatenate([jnp.where(c == 0, h, o), jnp.where(c == 0, o, h)], axis=0) for h, o in zip(halves, others)]


def gather_all(arrays):
    outs = [jax.ShapeDtypeStruct((8,) + a.shape, a.dtype) for a in arrays]
    return exchange("gather_replicated", Pushes(
        arrays, outs, ALL_MASKS, 1, src_of=lambda r, me, j: [r], dst_of=lambda o, sender, j: [o.at[_devno(sender)]]),
        local_of=lambda r, o, me: (r, o.at[_devno(me)]))


def _unshard_cols(g):
    return jnp.transpose(g, (1, 0, 2)).reshape(g.shape[1], -1)


def _shard_cols(a):
    return jnp.transpose(a.reshape(a.shape[0], N_CHIPS, -1), (1, 0, 2))


class Weights(dict):
    def ride(self, kernel_name):
        return None

    def arrived(self, kernel_name, outs):
        pass


def _forward_backward(x, tgt, W, grads_early=None):
    S = x.shape[0]
    G = {}
    sd = jax.ShapeDtypeStruct

    def ffn(xin, l, j):
        return ffn_fwd(xin, W["ffn_norm"][l][j], W["ffn_w_gate", l, j], W["ffn_w_up", l, j], W["ffn_w_down", l, j], l, j)

    def ffn_back(xin, dout, l, j):
        gn = W["ffn_norm"][l][j]
        dh, G["ffn_w_gate", l, j], G["ffn_w_up", l, j], G["ffn_w_down", l, j] = ffn_bwd(
            xin, gn, W["ffn_w_gate", l, j], W["ffn_w_up", l, j], W["ffn_w_down", l, j], dout, l, j)
        dx, G[("ffn_norm", l, j)] = norm_bwd(f"ffn_norm_bwd_{l}{j}", xin, gn, dh, dout)
        return dx

    x0 = x
    x1 = ffn(x0, 0, 0)
    g0 = W["mix_norm"][0]
    sbq, sbk, sbv = tile_fwd(f_attn_sb, "attn_in_sb", [x1], [g0, W["attn_w_in"][0]], [sd((S, SB_W), F32)] * 3, 256)
    dl_shape = sd((DL_PAIRS, S, 128), F32)
    qn, = tile_fwd(f_attn_qk, "attn_in_q", [x1], [g0, W["attn_w_in"][1], W["attn_q_norm"]], [dl_shape], 256)
    kn, = tile_fwd(f_attn_qk, "attn_in_k", [x1], [g0, W["attn_w_in"][2], W["attn_k_norm"]], [dl_shape], 256)
    vv, = tile_fwd(f_attn_v, "attn_in_v", [x1], [g0, W["attn_w_in"][3]], [dl_shape], 256)
    oa, sb_wts, *rode = sb_fwd(sbq, sbk, sbv, W.ride("sb_fwd"))
    W.arrived("sb_fwd", rode)
    qs, ks, vs = (reorder(nm, t, DIL, False) for nm, t in (("sub_q", qn), ("sub_k", kn), ("sub_v", vv)))
    o_s, lse_s, *rode = dil_fwd(qs, ks, vs, W["bias_mat"], W.ride("dil_fwd"))
    W.arrived("dil_fwd", rode)
    o_n, lse_n = reorder("nat_o", o_s, DIL, True), reorder("nat_lse", lse_s, DIL, True)
    x2, = tile_fwd(f_attn_out, "attn_out", [x1, oa, o_n, lse_n], [W["attn_w_out"]], [sd((S, D), F32)], 256)
    x3 = ffn(x2, 0, 1)
    x4 = ffn(x3, 1, 0)
    g1 = W["mix_norm"][1]
    h, hs = norm_shift_fwd(x4, g1)
    mix = W["rw_mix"]
    r, = tile_fwd(f_rw_proj, "rw_proj_r", [h, hs], [mix[0:1], W["rw_wr"]], [sd((S, D), F32)], 256)
    k, = tile_fwd(f_rw_proj, "rw_proj_k", [h, hs], [mix[2:3], W["rw_wk"]], [sd((S, D), F32)], 256)
    v, = tile_fwd(f_rw_proj, "rw_proj_v", [h, hs], [mix[3:4], W["rw_wv"]], [sd((S, D), F32)], 256)
    mix3 = jnp.concatenate([mix[1:2], mix[4:5], mix[5:6]], axis=0)
    mid_w = [mix3, W["rw_w0"], W["rw_a0"], W["rw_kk"], W["rw_ka"], W["rw_w1"], W["rw_w2"], W["rw_a1"], W["rw_a2"],
             W["rw_g1"], W["rw_g2"]]
    hshape = sd((RW_H, S, HEAD), F32)
    mid_tiles = [h, hs, r, k, v]
    rh, lwh, kh, vh, ah, bh, gate = tile_fwd(f_rw_mid, "rw_mid", mid_tiles, mid_w, [hshape] * 6 + [sd((S, D), F32)], 128)
    yh, states = rwkv_fwd(rh, lwh, kh, vh, ah, bh)
    post_w = [W["rw_lnx_g"], W["rw_lnx_b"], W["rw_rk"], W["rw_wo"]]
    post_tiles = [yh, rh, kh, vh, gate, x4]
    x5, = tile_fwd(f_rw_post, "rw_post", post_tiles, post_w, [sd((S, D), F32)], 128)
    x6 = ffn(x5, 1, 1)
    dx6, loss_part = loss_head(x6, tgt)

    dx5 = ffn_back(x5, dx6, 1, 1)
    (dyh, drh, dkh, dvh, dgate, dx4), (d_lng, d_lnb, d_rk, d_wo) = tile_bwd(
        f_rw_post, "rw_post_bwd", post_tiles, post_w, [dx5], 128, [True] * 6, [True] * 4)
    drh2, dlwh, dkh2, dvh2, dah, dbh = rwkv_bwd(rh, lwh, kh, vh, ah, bh, states, dyh)
    mid_cts = [(drh, drh2), dlwh, (dkh, dkh2), (dvh, dvh2), dah, dbh, dgate]
    (dh, dhs, dr, dk, dv), dmid_w = tile_bwd(f_rw_mid, "rw_mid_bwd", mid_tiles, mid_w, mid_cts, 128,
                                             [True] * 5, [True] * len(mid_w))
    dmix = {}
    for nm, ct, row, wname in (("r", dr, 0, "rw_wr"), ("k", dk, 2, "rw_wk"), ("v", dv, 3, "rw_wv")):
        (dh, dhs), (dmix[row], G[wname]) = tile_bwd(
            f_rw_proj, f"rw_proj_{nm}_bwd", [h, hs], [mix[row:row + 1], W[wname]], [ct], 256,
            [True, True], [True, True], acc={0: dh, 1: dhs})
    dx4, G[("mix_norm", 1)] = norm_shift_bwd(x4, g1, dh, dhs, dx4)
    dmix3 = dmid_w[0]
    G["rw_mix"] = jnp.concatenate([dmix[0], dmix3[0:1], dmix[2], dmix[3], dmix3[1:2], dmix3[2:3]], axis=0)
    for nm, gv in zip(("rw_w0", "rw_a0", "rw_kk", "rw_ka", "rw_w1", "rw_w2", "rw_a1", "rw_a2", "rw_g1", "rw_g2"), dmid_w[1:]):
        G[nm] = gv
    G["rw_lnx_g"], G["rw_lnx_b"], G["rw_rk"], G["rw_wo"] = d_lng, d_lnb, d_rk, d_wo
    dx3 = ffn_back(x3, dx4, 1, 0)
    dx2 = ffn_back(x2, dx3, 0, 1)
    (dx1, doa, do_n, dlse_n), (G["attn_w_out"],) = tile_bwd(
        f_attn_out, "attn_out_bwd", [x1, oa, o_n, lse_n], [W["attn_w_out"]], [dx2], 256, [True] * 4, [True])
    do_s, dlse_s = reorder("sub_do", do_n, DIL, False), reorder("sub_dlse", dlse_n, DIL, False)
    ride, swapped = grads_early(G) if grads_early is not None else (None, None)
    dqs, dks, dvs, dsum, *rode = dil_bwd(qs, ks, vs, W["bias_mat"], o_s, lse_s, do_s, dlse_s, ride)
    ride, landed = swapped(rode) if swapped is not None else (None, None)
    G["rel_bias"] = bias_grad(dsum, W["buckets"])
    dqn, dkn, dvv = (reorder(nm, t, DIL, True) for nm, t in (("nat_dq", dqs), ("nat_dk", dks), ("nat_dv", dvs)))
    dsbq, dsbk, dsbv, *rode = sb_bwd(sbq, sbk, sbv, doa, sb_wts, ride)
    if landed is not None:
        landed(rode)
    dg0 = []
    dwin = []
    (dx1,), (dg, dw) = tile_bwd(f_attn_sb, "attn_in_sb_bwd", [x1], [g0, W["attn_w_in"][0]], [dsbq, dsbk, dsbv], 256,
                                [True], [True, True], acc={0: dx1})
    dg0.append(dg), dwin.append(dw)
    (dx1,), (dg, dw, G["attn_q_norm"]) = tile_bwd(f_attn_qk, "attn_in_q_bwd", [x1], [g0, W["attn_w_in"][1], W["attn_q_norm"]],
                                                  [dqn], 256, [True], [True] * 3, acc={0: dx1})
    dg0.append(dg), dwin.append(dw)
    (dx1,), (dg, dw, G["attn_k_norm"]) = tile_bwd(f_attn_qk, "attn_in_k_bwd", [x1], [g0, W["attn_w_in"][2], W["attn_k_norm"]],
                                                  [dkn], 256, [True], [True] * 3, acc={0: dx1})
    dg0.append(dg), dwin.append(dw)
    (dx1,), (dg, dw) = tile_bwd(f_attn_v, "attn_in_v_bwd", [x1], [g0, W["attn_w_in"][3]], [dvv], 256,
                                [True], [True, True], acc={0: dx1})
    dg0.append(dg), dwin.append(dw)
    G[("mix_norm", 0)] = dg0
    G["attn_w_in"] = dwin
    dx0 = ffn_back(x0, dx1, 0, 0)
    return loss_part, dx0, G


VEC_ROWS = ("ffn_norm", "rw_mix", "rw_w0", "rw_a0", "rw_kk", "rw_ka", "rw_lnx_g", "rw_lnx_b")


def kernel(x, ffn_norm, ffn_w_gate, ffn_w_up, ffn_w_down, mix_norm, rel_bias, attn_w_in, attn_q_norm, attn_k_norm, attn_w_out, rw_mix, rw_w0, rw_w1, rw_w2, rw_a0, rw_a1, rw_a2, rw_g1, rw_g2, rw_kk, rw_ka, rw_rk, rw_wr, rw_wk, rw_wv, rw_wo, rw_lnx_g, rw_lnx_b, loss_target, m_ffn_norm, m_ffn_w_gate, m_ffn_w_up, m_ffn_w_down, m_mix_norm, m_rel_bias, m_attn_w_in, m_attn_q_norm, m_attn_k_norm, m_attn_w_out, m_rw_mix, m_rw_w0, m_rw_w1, m_rw_w2, m_rw_a0, m_rw_a1, m_rw_a2, m_rw_g1, m_rw_g2, m_rw_kk, m_rw_ka, m_rw_rk, m_rw_wr, m_rw_wk, m_rw_wv, m_rw_wo, m_rw_lnx_g, m_rw_lnx_b, v_ffn_norm, v_ffn_w_gate, v_ffn_w_up, v_ffn_w_down, v_mix_norm, v_rel_bias, v_attn_w_in, v_attn_q_norm, v_attn_k_norm, v_attn_w_out, v_rw_mix, v_rw_w0, v_rw_w1, v_rw_w2, v_rw_a0, v_rw_a1, v_rw_a2, v_rw_g1, v_rw_g2, v_rw_kk, v_rw_ka, v_rw_rk, v_rw_wr, v_rw_wk, v_rw_wv, v_rw_wo, v_rw_lnx_g, v_rw_lnx_b):
    names = ["ffn_norm", "ffn_w_gate", "ffn_w_up", "ffn_w_down", "mix_norm", "rel_bias", "attn_w_in", "attn_q_norm",
             "attn_k_norm", "attn_w_out", "rw_mix", "rw_w0", "rw_w1", "rw_w2", "rw_a0", "rw_a1", "rw_a2", "rw_g1", "rw_g2",
             "rw_kk", "rw_ka", "rw_rk", "rw_wr", "rw_wk", "rw_wv", "rw_wo", "rw_lnx_g", "rw_lnx_b"]
    loc = locals()
    w = {n: loc[n] for n in names}
    mom = {n: loc["m_" + n] for n in names}
    vel = {n: loc["v_" + n] for n in names}
    S = x.shape[1]

    ffn3 = ("ffn_w_gate", "ffn_w_up", "ffn_w_down")
    rw_mats = ("rw_w1", "rw_w2", "rw_a1", "rw_a2", "rw_g1", "rw_g2", "rw_wr", "rw_wk", "rw_wv", "rw_wo")
    cols_split = ("attn_w_out", "rw_w2", "rw_a2", "rw_g2")
    shard = {"vec": jnp.concatenate([w[n].reshape(-1, 256) for n in VEC_ROWS], axis=0)}
    for n in ffn3:
        for l in range(2):
            for j in range(2):
                shard[n, l, j] = w[n][l, j].astype(BF16)
    for n in ("attn_w_in", "attn_w_out") + rw_mats:
        shard[n] = w[n].reshape(-1, w[n].shape[-1]).astype(BF16)
    ffn_keys = lambda l, j: [(n, l, j) for n in ffn3]
    w_groups = {"first": ["vec"] + ffn_keys(0, 0) + ["attn_w_in", "attn_w_out"],
                "sb_fwd": ffn_keys(0, 1) + ffn_keys(1, 0) + list(rw_mats),
                "dil_fwd": ffn_keys(1, 1)}
    label = lambda key: key if isinstance(key, str) else f"{key[0]}_{key[1]}{key[2]}"

    class Streamed(Weights):
        def ride(self, kernel_name):
            return gather_pushes([shard[k] for k in w_groups[kernel_name]])

        def arrived(self, kernel_name, outs):
            for key, g in zip(w_groups[kernel_name], gather_swap(f"gather_swap_{kernel_name}", outs)):
                if key == "vec":
                    vec_full = _unshard_cols(g)
                    self["ffn_norm"] = [[vec_full[2 * l + j][None] for j in range(2)] for l in range(2)]
                    self["rw_mix"] = vec_full[4:10]
                    for i, n in enumerate(("rw_w0", "rw_a0", "rw_kk", "rw_ka", "rw_lnx_g", "rw_lnx_b")):
                        self[n] = vec_full[10 + i][None]
                elif key == "attn_w_in":
                    self[key] = [g[p] for p in range(N_CHIPS)]
                elif key in cols_split:
                    self[key] = _unshard_cols(g)
                elif isinstance(key, str):
                    self[key] = g.reshape(D, -1)
                else:
                    self[key] = g

    buckets = _bucket_maps()
    W = Streamed({"mix_norm": [mix_norm[0:1], mix_norm[1:2]], "attn_q_norm": attn_q_norm, "attn_k_norm": attn_k_norm,
                  "rw_rk": rw_rk[0][:, None, :], "buckets": buckets, "bias_mat": bias_table(rel_bias, buckets)})
    W.arrived("first", exchange("gather_weights", W.ride("first")))

    def slots(key, G):
        if key == "vec":
            rows = [G[("ffn_norm", l, j)] for l in range(2) for j in range(2)] + [G["rw_mix"]] + \
                   [G[n] for n in ("rw_w0", "rw_a0", "rw_kk", "rw_ka", "rw_lnx_g", "rw_lnx_b")]
            return _shard_cols(jnp.concatenate(rows, axis=0))
        if key == "attn_w_in":
            return jnp.stack(G[key])
        if key in cols_split:
            return _shard_cols(G[key])
        if isinstance(key, str):
            return G[key].reshape(N_CHIPS, D // N_CHIPS, -1)
        return G[key]

    g_groups = {"early": ffn_keys(1, 1) + ffn_keys(1, 0) + ffn_keys(0, 1) + list(rw_mats) + ["attn_w_out"],
                "late": ["vec", "attn_w_in"] + ffn_keys(0, 0)}
    wire = lambda keys: [F32 if k == "vec" else BF16 for k in keys]
    part = {}

    def grads_early(G):
        keys = g_groups["early"]
        names_ = [label(k) for k in keys]
        split, swap_pushes = reduce_swap([slots(k, G) for k in keys])

        def swapped(theirs):
            chip_sum, pushes = reduce_sum(names_, split, theirs, wire(keys))
            return pushes, lambda landed: part.update(zip(keys, reduce_end("early", names_, chip_sum, landed)))

        return swap_pushes, swapped

    loss_part, dx, G = _forward_backward(x[0], loss_target[0], W, grads_early)
    loss = lax.psum(loss_part[0, 0], ("x", "y", "c"))
    keys = g_groups["late"]
    chip_sum, pushes = reduce_begin("late", [label(k) for k in keys], [slots(k, G) for k in keys], wire(keys))
    part.update(zip(keys, reduce_end("late", [label(k) for k in keys], chip_sum, exchange("scatter_grads", pushes))))
    for n in ffn3:
        part[n] = jnp.stack([jnp.stack([part[n, l, j] for j in range(2)]) for l in range(2)])

    rep = jnp.concatenate([G[("mix_norm", 0)][0] + G[("mix_norm", 0)][1] + G[("mix_norm", 0)][2] + G[("mix_norm", 0)][3],
                           G[("mix_norm", 1)]], axis=0).reshape(16, 128)
    rep = jnp.concatenate([rep, G["rel_bias"], jnp.pad(G["attn_q_norm"], ((0, 0), (0, 64))),
                           jnp.pad(G["attn_k_norm"], ((0, 0), (0, 64))), G["rw_rk"].reshape(8, 128),
                           jnp.zeros((2, 128), F32)], axis=0)
    rep_sum = sum_slots("sum_replicated", gather_all([rep])[0])
    g_rep = {
        "mix_norm": rep_sum[0:16].reshape(2, D),
        "rel_bias": jnp.transpose(rep_sum[16:28, :N_BUCKETS]),
        "attn_q_norm": rep_sum[28:29, :HEAD], "attn_k_norm": rep_sum[29:30, :HEAD],
        "rw_rk": rep_sum[30:38].reshape(1, RW_H, HEAD),
    }

    out = {}

    def adam(n, ga, gb):
        shp = w[n].shape
        to2 = lambda a: a.reshape(-1, shp[-1])
        res = adam_step(f"adam_{n}", to2(ga), None if gb is None else to2(gb), to2(w[n]), to2(mom[n]), to2(vel[n]))
        out[n] = tuple(r.reshape(shp) for r in res)

    for n in ffn3 + ("attn_w_in", "attn_w_out") + rw_mats:
        adam(n, part[n], None)
    rows = {"ffn_norm": (0, 4), "rw_mix": (4, 10), "rw_w0": (10, 11), "rw_a0": (11, 12), "rw_kk": (12, 13),
            "rw_ka": (13, 14), "rw_lnx_g": (14, 15), "rw_lnx_b": (15, 16)}
    for n, (lo, hi) in rows.items():
        adam(n, part["vec"][lo:hi], None)
    for n, gv in g_rep.items():
        adam(n, gv, None)

    grads = [out[n][0] for n in names]
    deltas = [out[n][1] for n in names]
    new_m = [out[n][2] for n in names]
    new_v = [out[n][3] for n in names]
    return (loss, dx[None], *grads, *deltas, *new_m, *new_v)
```

```python
import functools
import math

import jax
import jax.numpy as jnp
from jax import lax
from jax.experimental import pallas as pl
from jax.experimental.pallas import tpu as pltpu

F32, BF16 = jnp.float32, jnp.bfloat16
HI = lax.Precision.HIGHEST
MESH = pl.DeviceIdType.MESH

D = 1024
HEAD = 64
N_CHIPS = 4
FF_SHARD = 704
SB_W = 256
DL_HEADS = 12
DL_PAIRS = 6
DIL = (1, 4, 16)
QBLK = 128
N_BUCKETS = 32
MAX_DISTANCE = 2048
RW_H = 16
RW_CHUNK = 64
NORM_EPS = 1e-6
GN_EPS = 64e-5
NEG_INF = -1e30
VMEM_LIMIT = 56 * 1024 * 1024

ADAM_LR, ADAM_B1, ADAM_B2, ADAM_EPS, ADAM_WD, ADAM_STEP = 0.001, 0.9, 0.999, 1e-08, 0.01, 10


def _cp(sem):
    return pltpu.CompilerParams(dimension_semantics=sem, vmem_limit_bytes=VMEM_LIMIT)


def _dg(a, b, dims, prec=None):
    return lax.dot_general(a, b, (dims, ((), ())), precision=prec, preferred_element_type=F32)


def _bdot(a, b, dims):
    return _dg(a.astype(BF16), b.astype(BF16), dims)


@jax.custom_vjp
def mm(a, b):
    return _bdot(a, b, ((1,), (0,)))


def _mm_fwd(a, b):
    return _bdot(a, b, ((1,), (0,))), (a, b)


def _mm_bwd(res, g):
    a, b = res
    return _bdot(g, b, ((1,), (1,))), _bdot(a, g, ((0,), (0,)))


mm.defvjp(_mm_fwd, _mm_bwd)


def rms(x, g):
    return x * lax.rsqrt(jnp.mean(x * x, axis=-1, keepdims=True) + NORM_EPS) * g


def group_sum(x, nh):
    w = x.shape[-1]
    e = (lax.broadcasted_iota(jnp.int32, (w, nh), 0) // HEAD == lax.broadcasted_iota(jnp.int32, (w, nh), 1)).astype(F32)
    s = _dg(x, e, ((1,), (0,)), HI)
    return _dg(s, e, ((1,), (1,)), HI)


def softplus(u):
    return jnp.maximum(u, 0.0) + jnp.log1p(jnp.exp(-jnp.abs(u)))


def to_heads(t, nh=RW_H):
    return jnp.stack([t[:, HEAD * h:HEAD * (h + 1)] for h in range(nh)])


def from_heads(t):
    return jnp.concatenate([t[h] for h in range(t.shape[0])], axis=-1)


def _tile_spec(shape, tm):
    if len(shape) == 2:
        return pl.BlockSpec((tm, shape[1]), lambda t: (t, 0))
    return pl.BlockSpec((shape[0], tm, shape[2]), lambda t: (0, t, 0))


def _full_spec(shape):
    nd = len(shape)
    return pl.BlockSpec(tuple(shape), lambda t: (0,) * nd)


def _rows(a):
    return a.shape[0] if a.ndim == 2 else a.shape[1]


def tile_fwd(f, name, tiles, weights, outs, tm):
    nt, nw = len(tiles), len(weights)

    def body(*refs):
        tv = [r[...] for r in refs[:nt]]
        wv = [r[...].astype(F32) for r in refs[nt:nt + nw]]
        res = f(*tv, *wv)
        if not isinstance(res, (tuple, list)):
            res = (res,)
        for o, v in zip(refs[nt + nw:], res):
            o[...] = v.astype(o.dtype)

    return pl.pallas_call(
        body, name=name, grid=(_rows(tiles[0]) // tm,),
        in_specs=[_tile_spec(a.shape, tm) for a in tiles] + [_full_spec(w.shape) for w in weights],
        out_specs=[_tile_spec(o.shape, tm) for o in outs],
        out_shape=list(outs),
        compiler_params=_cp(("parallel",)),
    )(*tiles, *weights)


def tile_bwd(f, name, tiles, weights, cts, tm, dt, dw, acc=None):
    acc = acc or {}
    groups = [c if isinstance(c, tuple) else (c,) for c in cts]
    cts = [a for grp in groups for a in grp]
    nt, nw, nc = len(tiles), len(weights), len(cts)
    acc_idx = sorted(acc)
    na = len(acc_idx)
    dti = [i for i in range(nt) if dt[i]]
    dwi = [i for i in range(nw) if dw[i]]

    def body(*refs):
        tv = [r[...] for r in refs[:nt]]
        wv = [r[...].astype(F32) for r in refs[nt:nt + nw]]
        crefs = list(refs[nt + nw:nt + nw + nc])
        cv = []
        for grp in groups:
            terms = [crefs.pop(0)[...] for _ in grp]
            cv.append(functools.reduce(lambda a, b: a + b, terms))
        av = {i: r[...] for i, r in zip(acc_idx, refs[nt + nw + nc:nt + nw + nc + na])}
        orefs = refs[nt + nw + nc + na:]

        def g(*diff):
            t2, w2 = list(tv), list(wv)
            for i, v in zip(dti, diff[:len(dti)]):
                t2[i] = v
            for i, v in zip(dwi, diff[len(dti):]):
                w2[i] = v
            res = f(*t2, *w2)
            return tuple(res) if isinstance(res, (tuple, list)) else (res,)

        _, vjp = jax.vjp(g, *[tv[i] for i in dti], *[wv[i] for i in dwi])
        grads = vjp(tuple(cv))
        for k, i in enumerate(dti):
            gt = grads[k]
            if i in av:
                gt = gt + av[i]
            orefs[k][...] = gt
        first = pl.program_id(0) == 0
        for k, i in enumerate(dwi):
            o = orefs[len(dti) + k]
            gw = grads[len(dti) + k]

            @pl.when(first)
            def _(o=o, gw=gw):
                o[...] = gw

            @pl.when(jnp.logical_not(first))
            def _(o=o, gw=gw):
                o[...] += gw

    out_shape = [jax.ShapeDtypeStruct(tiles[i].shape, F32) for i in dti] + \
                [jax.ShapeDtypeStruct(weights[i].shape, F32) for i in dwi]
    res = pl.pallas_call(
        body, name=name, grid=(_rows(tiles[0]) // tm,),
        in_specs=[_tile_spec(a.shape, tm) for a in tiles] + [_full_spec(w.shape) for w in weights] +
                 [_tile_spec(c.shape, tm) for c in cts] + [_tile_spec(tiles[i].shape, tm) for i in acc_idx],
        out_specs=[_tile_spec(tiles[i].shape, tm) for i in dti] + [_full_spec(weights[i].shape) for i in dwi],
        out_shape=out_shape,
        compiler_params=_cp(("arbitrary",)),
    )(*tiles, *weights, *cts, *[acc[i] for i in acc_idx])
    return list(res[:len(dti)]), list(res[len(dti):])


def _ffn_wspec(rows, cols, cfirst):
    if cfirst:
        return pl.BlockSpec((1, rows, cols), lambda c, t: (c, 0, 0))
    return pl.BlockSpec((1, rows, cols), lambda t, c: (c, 0, 0))


def ffn_fwd(x, g, wg, wu, wd, l, j, tm=512):
    S = x.shape[0]

    def body(x_ref, g_ref, wg_ref, wu_ref, wd_ref, o_ref, h_ref, acc_ref):
        c = pl.program_id(1)

        @pl.when(c == 0)
        def _():
            h_ref[...] = rms(x_ref[...], g_ref[...]).astype(BF16)
            acc_ref[...] = jnp.zeros_like(acc_ref)

        h = h_ref[...]
        a = _bdot(h, wg_ref[0], ((1,), (0,)))
        b = _bdot(h, wu_ref[0], ((1,), (0,)))
        y = a * jax.nn.sigmoid(a) * b
        acc_ref[...] += _bdot(y, wd_ref[0], ((1,), (0,)))

        @pl.when(c == N_CHIPS - 1)
        def _():
            o_ref[...] = x_ref[...] + 0.5 * acc_ref[...]

    return pl.pallas_call(
        body, name=f"ffn_fwd_{l}{j}", grid=(S // tm, N_CHIPS),
        in_specs=[pl.BlockSpec((tm, D), lambda t, c: (t, 0)), pl.BlockSpec((1, D), lambda t, c: (0, 0)),
                  _ffn_wspec(D, FF_SHARD, False), _ffn_wspec(D, FF_SHARD, False), _ffn_wspec(FF_SHARD, D, False)],
        out_specs=pl.BlockSpec((tm, D), lambda t, c: (t, 0)),
        out_shape=jax.ShapeDtypeStruct((S, D), F32),
        scratch_shapes=[pltpu.VMEM((tm, D), BF16), pltpu.VMEM((tm, D), F32)],
        compiler_params=_cp(("parallel", "arbitrary")),
    )(x, g, wg, wu, wd)


def ffn_bwd(x, g, wg, wu, wd, dout, l, j, tm=512):
    S = x.shape[0]

    def body(x_ref, g_ref, wg_ref, wu_ref, wd_ref, do_ref, dh_ref, dwg_ref, dwu_ref, dwd_ref):
        t = pl.program_id(1)
        h = rms(x_ref[...], g_ref[...]).astype(BF16)
        wgv, wuv, wdv = wg_ref[0], wu_ref[0], wd_ref[0]
        a = _bdot(h, wgv, ((1,), (0,)))
        b = _bdot(h, wuv, ((1,), (0,)))
        sig = jax.nn.sigmoid(a)
        s = a * sig
        dyd = 0.5 * do_ref[...]
        dy = _bdot(dyd, wdv, ((1,), (1,)))
        dwd = _bdot(s * b, dyd, ((0,), (0,)))
        db = dy * s
        da = dy * b * (sig * (1.0 + a * (1.0 - sig)))
        dwg = _bdot(h, da, ((0,), (0,)))
        dwu = _bdot(h, db, ((0,), (0,)))
        dh_ref[0] = _bdot(da, wgv, ((1,), (1,))) + _bdot(db, wuv, ((1,), (1,)))

        @pl.when(t == 0)
        def _():
            dwg_ref[0] = dwg
            dwu_ref[0] = dwu
            dwd_ref[0] = dwd

        @pl.when(t != 0)
        def _():
            dwg_ref[0] += dwg
            dwu_ref[0] += dwu
            dwd_ref[0] += dwd

    return pl.pallas_call(
        body, name=f"ffn_bwd_{l}{j}", grid=(N_CHIPS, S // tm),
        in_specs=[pl.BlockSpec((tm, D), lambda c, t: (t, 0)), pl.BlockSpec((1, D), lambda c, t: (0, 0)),
                  _ffn_wspec(D, FF_SHARD, True), _ffn_wspec(D, FF_SHARD, True), _ffn_wspec(FF_SHARD, D, True),
                  pl.BlockSpec((tm, D), lambda c, t: (t, 0))],
        out_specs=[pl.BlockSpec((1, tm, D), lambda c, t: (c, t, 0)),
                   _ffn_wspec(D, FF_SHARD, True), _ffn_wspec(D, FF_SHARD, True), _ffn_wspec(FF_SHARD, D, True)],
        out_shape=[jax.ShapeDtypeStruct((N_CHIPS, S, D), F32)] + [jax.ShapeDtypeStruct(a.shape, F32) for a in (wg, wu, wd)],
        compiler_params=_cp(("parallel", "arbitrary")),
    )(x, g, wg, wu, wd, dout)


def norm_bwd(name, x, g, dh_parts, dres, tm=256):
    S = x.shape[0]
    P = dh_parts.shape[0]

    def body(x_ref, g_ref, dh_ref, dr_ref, dx_ref, dg_ref):
        dh = dh_ref[0]
        for p in range(1, P):
            dh = dh + dh_ref[p]
        _, vjp = jax.vjp(rms, x_ref[...], g_ref[...])
        dx, dg = vjp(dh)
        dx_ref[...] = dr_ref[...] + dx

        @pl.when(pl.program_id(0) == 0)
        def _():
            dg_ref[...] = dg

        @pl.when(pl.program_id(0) != 0)
        def _():
            dg_ref[...] += dg

    return pl.pallas_call(
        body, name=name, grid=(S // tm,),
        in_specs=[pl.BlockSpec((tm, D), lambda t: (t, 0)), pl.BlockSpec((1, D), lambda t: (0, 0)),
                  pl.BlockSpec((P, tm, D), lambda t: (0, t, 0)), pl.BlockSpec((tm, D), lambda t: (t, 0))],
        out_specs=[pl.BlockSpec((tm, D), lambda t: (t, 0)), pl.BlockSpec((1, D), lambda t: (0, 0))],
        out_shape=[jax.ShapeDtypeStruct((S, D), F32), jax.ShapeDtypeStruct((1, D), F32)],
        compiler_params=_cp(("arbitrary",)),
    )(x, g, dh_parts, dres)


def f_attn_sb(x, g, w):
    pr = mm(rms(x, g), w)
    return pr[:, :SB_W], pr[:, SB_W:2 * SB_W], pr[:, 2 * SB_W:]


def _pairs(y):
    return jnp.stack([y[:, 128 * j:128 * (j + 1)] for j in range(DL_PAIRS)])


def f_attn_qk(x, g, w, nrm):
    pr = mm(rms(x, g), w)
    ms = group_sum(pr * pr, DL_HEADS) * (1.0 / HEAD)
    return _pairs(pr * lax.rsqrt(ms + NORM_EPS) * jnp.concatenate([nrm] * DL_HEADS, axis=1))


def f_attn_v(x, g, w):
    return _pairs(mm(rms(x, g), w))


def _masked(strict, x):
    return x if strict is None else jnp.where(strict, x, 0.0)


def _head_stack(x):
    nh = x.shape[1] // HEAD
    lane_head = lax.broadcasted_iota(jnp.int32, (1, x.shape[1]), 1) // HEAD
    return jnp.concatenate([jnp.where(lane_head == h, x, 0.0) for h in range(nh)], axis=0).astype(BF16)


def _head_pick(xs):
    nh = xs.shape[1] // HEAD
    rows = xs.shape[0] // nh
    lane_head = lax.broadcasted_iota(jnp.int32, (1, xs.shape[1]), 1) // HEAD
    out = xs[:rows]
    for h in range(1, nh):
        out = jnp.where(lane_head == h, xs[rows * h:rows * (h + 1)], out)
    return out


def _sb_tiles(qs, kblk, strict):
    z = _dg(qs, kblk, ((1,), (1,))) * (HEAD ** -0.5)
    keep = -(jnp.maximum(z, 0.0) + jnp.log(1.0 + jnp.exp(-jnp.abs(z))))
    return z, _masked(strict, keep)


def _tri(n, upper):
    r = lax.broadcasted_iota(jnp.int32, (n, n), 0)
    c = lax.broadcasted_iota(jnp.int32, (n, n), 1)
    return ((r > c) if upper else (r < c)).astype(BF16)


def _tri_sums(x, tri):
    hi, lo = _split2(x)
    return _dg(jnp.concatenate([hi, lo], axis=1), jnp.concatenate([tri, tri], axis=0), ((1,), (0,)))


SB_UNROLL = 4


def _sb_diag(tb, nh):
    r = lax.broadcasted_iota(jnp.int32, (nh * tb, tb), 0)
    return lax.broadcasted_iota(jnp.int32, (nh * tb, tb), 1) < lax.rem(r, tb)


def _sb_sweep(step, first, count, carry, direction, commit=None):
    def run(kbs, c):
        outs = []
        for kb in kbs:
            c, out = step(kb, c)
            outs.append(out)
        if commit is not None:
            for kb, out in zip(kbs, outs):
                commit(kb, out)
        return c

    rem = count % SB_UNROLL
    carry = lax.fori_loop(0, rem, lambda i, c: run([first + direction * i], c), carry)
    return lax.fori_loop(
        0, count // SB_UNROLL,
        lambda g, c: run([first + direction * (rem + SB_UNROLL * g + u) for u in range(SB_UNROLL)], c), carry)


def _riding(ride, refs, n_in, n_out, first, last):
    if ride is None:
        return refs, lambda: None
    n = ride.n
    own = refs[:n_in] + refs[n_in + n:n_in + n + n_out] + refs[n_in + 2 * n + n_out:len(refs) - 2]
    start, wait = ride.ops(refs[n_in:n_in + n], refs[n_in + n + n_out:n_in + 2 * n + n_out], refs[-2], refs[-1])
    pl.when(first)(start)
    return own, lambda: pl.when(last)(wait)


def _ride_specs(ride):
    if ride is None:
        return [], [], [], [], []
    return [_HBM] * ride.n, [_HBM] * ride.n, ride.out_shapes, ride.sem_shapes(), ride.arrays


def sb_fwd(q, k, v, ride=None, tb=QBLK):
    S = q.shape[0]
    nh = SB_W // HEAD
    nb = S // tb
    r_in, r_out, r_shape, r_scr, r_args = _ride_specs(ride)

    def body(*refs):
        qb = pl.program_id(0)
        (q_ref, k_ref, v_ref, o_ref, w_ref), finish = _riding(ride, refs, 3, 2, qb == 0, qb == nb - 1)
        diag = _sb_diag(tb, nh)
        after_mat = _tri(tb, True)
        qs = _head_stack(q_ref[...])

        def step(kb, carry, strict):
            acc, run = carry
            rows = pl.ds(pl.multiple_of(kb * tb, tb), tb)
            z, keep = _sb_tiles(qs, k_ref[rows, :].astype(BF16), strict)
            w = _masked(strict, jnp.exp(z + keep + _tri_sums(keep, after_mat) + run)).astype(BF16)
            w_ref[0, kb] = w
            acc = acc + _dg(w, v_ref[rows, :].astype(BF16), ((1,), (0,)))
            return acc, run + jnp.sum(keep, axis=1, keepdims=True)

        init = (jnp.zeros((nh * tb, SB_W), F32), jnp.zeros((nh * tb, 1), F32))
        carry = step(qb, init, diag)
        acc, _ = _sb_sweep(lambda kb, c: (step(kb, c, None), None), qb - 1, qb, carry, -1)
        o_ref[...] = _head_pick(acc)
        finish()

    return pl.pallas_call(
        body, name="sb_fwd", grid=(S // tb,),
        in_specs=[pl.BlockSpec((tb, SB_W), lambda i: (i, 0)), pl.BlockSpec((S, SB_W), lambda i: (0, 0)),
                  pl.BlockSpec((S, SB_W), lambda i: (0, 0))] + r_in,
        out_specs=[pl.BlockSpec((tb, SB_W), lambda i: (i, 0)),
                   pl.BlockSpec((1, nb, nh * tb, tb), lambda i: (i, 0, 0, 0))] + r_out,
        out_shape=[jax.ShapeDtypeStruct((S, SB_W), F32), jax.ShapeDtypeStruct((nb, nb, nh * tb, tb), BF16)] + r_shape,
        scratch_shapes=r_scr,
        compiler_params=_cp(("arbitrary",)),
    )(q, k, v, *r_args)


def sb_bwd(q, k, v, do, wts, ride=None, tb=QBLK):
    S = q.shape[0]
    nh = SB_W // HEAD
    nb = S // tb
    scale = HEAD ** -0.5
    r_in, r_out, r_shape, r_scr, r_args = _ride_specs(ride)

    def body(*refs):
        qb = pl.program_id(0)
        (q_ref, k_ref, v_ref, do_ref, w_ref, dq_ref, dk_ref, dv_ref, g_scr), finish = _riding(
            ride, refs, 5, 3, qb == 0, qb == nb - 1)

        @pl.when(qb == 0)
        def _():
            dk_ref[...] = jnp.zeros_like(dk_ref)
            dv_ref[...] = jnp.zeros_like(dv_ref)

        diag = _sb_diag(tb, nh)
        before_mat = _tri(tb, False)
        qs = _head_stack(q_ref[...])
        dos = _head_stack(do_ref[...])

        def weights_pass(kb, carry):
            rows = pl.ds(pl.multiple_of(kb * tb, tb), tb)
            w = w_ref[0, kb]
            g_scr[kb] = _dg(dos, v_ref[rows, :].astype(BF16), ((1,), (1,))) * w.astype(F32)
            return carry, _dg(w, dos, ((0,), (0,)))

        def add_rows(ref):
            def commit(kb, val):
                ref[pl.ds(pl.multiple_of(kb * tb, tb), tb), :] += val
            return commit

        zero_run = jnp.zeros((nh * tb, 1), F32)
        _sb_sweep(weights_pass, 0, qb + 1, 0, 1, add_rows(dv_ref))

        def left_to_right(kb, carry, strict):
            dq, run = carry
            rows = pl.ds(pl.multiple_of(kb * tb, tb), tb)
            kblk = k_ref[rows, :].astype(BF16)
            gw = g_scr[kb]
            sig = jax.nn.sigmoid(_dg(qs, kblk, ((1,), (1,))) * scale)
            dkeep = _masked(strict, _tri_sums(gw, before_mat) + run)
            dz = ((gw * (1.0 - sig) - dkeep * sig) * scale).astype(BF16)
            dq = dq + _dg(dz, kblk, ((1,), (0,)))
            return (dq, run + jnp.sum(gw, axis=1, keepdims=True)), _dg(dz, qs, ((0,), (0,)))

        carry = _sb_sweep(lambda kb, c: left_to_right(kb, c, None), 0, qb,
                          (jnp.zeros((nh * tb, SB_W), F32), zero_run), 1, add_rows(dk_ref))
        (dq, _), dk_diag = left_to_right(qb, carry, diag)
        add_rows(dk_ref)(qb, dk_diag)
        dq_ref[...] = _head_pick(dq)
        finish()

    whole = pl.BlockSpec((S, SB_W), lambda i: (0, 0))
    blk = pl.BlockSpec((tb, SB_W), lambda i: (i, 0))
    return pl.pallas_call(
        body, name="sb_bwd", grid=(S // tb,),
        in_specs=[blk, whole, whole, blk, pl.BlockSpec((1, nb, nh * tb, tb), lambda i: (i, 0, 0, 0))] + r_in,
        out_specs=[blk, whole, whole] + r_out,
        out_shape=[jax.ShapeDtypeStruct((S, SB_W), F32)] * 3 + r_shape,
        scratch_shapes=[pltpu.VMEM((S // tb, nh * tb, tb), F32)] + r_scr,
        compiler_params=_cp(("arbitrary",)),
    )(q, k, v, do, wts, *r_args)


def reorder(name, x, groups, inverse):
    P, S, _ = x.shape

    def body(x_ref, o_ref):
        p = pl.program_id(0)
        for gi, r in enumerate(groups):
            @pl.when(p // 2 == gi)
            def _(r=r):
                L = S // r
                if r == 1:
                    o_ref[...] = x_ref[...]
                for c in range(r if r > 1 else 0):
                    if inverse:
                        o_ref[pl.ds(c, L, stride=r), :] = x_ref[c * L:(c + 1) * L, :]
                    else:
                        o_ref[c * L:(c + 1) * L, :] = x_ref[pl.ds(c, L, stride=r), :]

    slab = pl.BlockSpec((None, S, 128), lambda p: (p, 0, 0))
    return pl.pallas_call(
        body, name=name, grid=(P,), in_specs=[slab], out_specs=slab,
        out_shape=jax.ShapeDtypeStruct(x.shape, x.dtype), compiler_params=_cp(("parallel",)),
    )(x)


def _dil_blocks(S):
    return S // QBLK


def _dil_mask(n_in_stream):
    qi = lax.broadcasted_iota(jnp.int32, (QBLK, 2 * QBLK), 0)
    kj = lax.broadcasted_iota(jnp.int32, (QBLK, 2 * QBLK), 1) - QBLK
    dist = qi - kj
    return (dist >= 0) & (dist <= QBLK) & ((n_in_stream > 0) | (kj >= 0))


def _stream_pos(gi, i, S):
    nb = jnp.where(gi == 0, S // (QBLK * DIL[0]), jnp.where(gi == 1, S // (QBLK * DIL[1]), S // (QBLK * DIL[2])))
    return i % nb


def dil_fwd(q, k, v, bias, ride=None):
    S = q.shape[1]
    nblk = _dil_blocks(S)
    r_in, r_out, r_shape, r_scr, r_args = _ride_specs(ride)

    def body(*refs):
        gi, i = pl.program_id(0), pl.program_id(1)
        (q_ref, kc_ref, kp_ref, vc_ref, vp_ref, b_ref, o_ref, l_ref), finish = _riding(
            ride, refs, 6, 2, (gi == 0) & (i == 0), (gi == len(DIL) - 1) & (i == nblk - 1))
        mask = _dil_mask(_stream_pos(gi, i, S))
        for j in range(2):
            q2, kc, kp, vc, vp = q_ref[j], kc_ref[j], kp_ref[j], vc_ref[j], vp_ref[j]
            os_, ls_ = [], []
            for hh in range(2):
                sl = slice(HEAD * hh, HEAD * (hh + 1))
                kw = jnp.concatenate([kp[:, sl], kc[:, sl]], axis=0)
                vw = jnp.concatenate([vp[:, sl], vc[:, sl]], axis=0)
                lg = _bdot(q2[:, sl], kw, ((1,), (1,))) * (HEAD ** -0.5) + b_ref[2 * j + hh]
                lg = jnp.where(mask, lg, NEG_INF)
                m = jnp.max(lg, axis=-1, keepdims=True)
                p = jnp.exp(lg - m)
                den = jnp.sum(p, axis=-1, keepdims=True)
                os_.append(_bdot(p / den, vw, ((1,), (0,))))
                ls_.append(jnp.broadcast_to(m + jnp.log(den), (QBLK, HEAD)))
            o_ref[j] = jnp.concatenate(os_, axis=1)
            l_ref[j] = jnp.concatenate(ls_, axis=1)
        finish()

    cur = pl.BlockSpec((2, QBLK, 128), lambda g, i: (g, i, 0))
    prev = pl.BlockSpec((2, QBLK, 128), lambda g, i: (g, jnp.maximum(i - 1, 0), 0))
    return pl.pallas_call(
        body, name="dil_fwd", grid=(len(DIL), nblk),
        in_specs=[cur, cur, prev, cur, prev, pl.BlockSpec((4, QBLK, 2 * QBLK), lambda g, i: (g, 0, 0))] + r_in,
        out_specs=[cur, cur] + r_out,
        out_shape=[jax.ShapeDtypeStruct(q.shape, F32)] * 2 + r_shape,
        scratch_shapes=r_scr,
        compiler_params=_cp(("arbitrary", "arbitrary")),
    )(q, k, k, v, v, bias, *r_args)


def dil_bwd(q, k, v, bias, o, lse, do, dlse, ride=None):
    S = q.shape[1]
    nblk = _dil_blocks(S)
    r_in, r_out, r_shape, r_scr, r_args = _ride_specs(ride)

    def body(*refs):
        gi, i = pl.program_id(0), pl.program_id(1)
        (q_ref, kc_ref, kp_ref, vc_ref, vp_ref, b_ref, o_ref, l_ref, do_ref, dl_ref,
         dq_ref, dk_ref, dv_ref, ds_ref, dk_car, dv_car), finish = _riding(
            ride, refs, 10, 4, (gi == 0) & (i == 0), (gi == len(DIL) - 1) & (i == nblk))

        @pl.when(i == 0)
        def _():
            ds_ref[...] = jnp.zeros_like(ds_ref)
            dk_car[...] = jnp.zeros_like(dk_car)
            dv_car[...] = jnp.zeros_like(dv_car)

        @pl.when(i < nblk)
        def _():
            mask = _dil_mask(_stream_pos(gi, i, S))
            for j in range(2):
                q2, kc, kp, vc, vp = q_ref[j], kc_ref[j], kp_ref[j], vc_ref[j], vp_ref[j]
                o2, l2, do2, dl2 = o_ref[j], l_ref[j], do_ref[j], dl_ref[j]
                dqs, dkps, dkcs, dvps, dvcs = [], [], [], [], []
                for hh in range(2):
                    sl = slice(HEAD * hh, HEAD * (hh + 1))
                    qh, doh = q2[:, sl], do2[:, sl]
                    kw = jnp.concatenate([kp[:, sl], kc[:, sl]], axis=0)
                    vw = jnp.concatenate([vp[:, sl], vc[:, sl]], axis=0)
                    lg = _bdot(qh, kw, ((1,), (1,))) * (HEAD ** -0.5) + b_ref[2 * j + hh]
                    p = jnp.where(mask, jnp.exp(lg - l2[:, HEAD * hh:HEAD * hh + 1]), 0.0)
                    dp = _bdot(doh, vw, ((1,), (1,)))
                    delta = jnp.sum(doh * o2[:, sl], axis=-1, keepdims=True)
                    dl = jnp.sum(dl2[:, sl], axis=-1, keepdims=True)
                    ds = p * (dp - delta + dl)
                    ds_ref[2 * j + hh] += ds
                    dsq = ds * (HEAD ** -0.5)
                    dqs.append(_bdot(dsq, kw, ((1,), (0,))))
                    dkw = _bdot(dsq, qh, ((0,), (0,)))
                    dvw = _bdot(p, doh, ((0,), (0,)))
                    dkps.append(dkw[:QBLK])
                    dkcs.append(dkw[QBLK:])
                    dvps.append(dvw[:QBLK])
                    dvcs.append(dvw[QBLK:])
                dq_ref[j] = jnp.concatenate(dqs, axis=1)
                dk_ref[j] = dk_car[j] + jnp.concatenate(dkps, axis=1)
                dv_ref[j] = dv_car[j] + jnp.concatenate(dvps, axis=1)
                dk_car[j] = jnp.concatenate(dkcs, axis=1)
                dv_car[j] = jnp.concatenate(dvcs, axis=1)

        @pl.when(i == nblk)
        def _():
            dk_ref[...] = dk_car[...]
            dv_ref[...] = dv_car[...]

        finish()

    cur = pl.BlockSpec((2, QBLK, 128), lambda g, i: (g, jnp.minimum(i, nblk - 1), 0))
    prev = pl.BlockSpec((2, QBLK, 128), lambda g, i: (g, jnp.clip(i - 1, 0, nblk - 1), 0))
    bspec = pl.BlockSpec((4, QBLK, 2 * QBLK), lambda g, i: (g, 0, 0))
    return pl.pallas_call(
        body, name="dil_bwd", grid=(len(DIL), nblk + 1),
        in_specs=[cur, cur, prev, cur, prev, bspec, cur, cur, cur, cur] + r_in,
        out_specs=[cur, prev, prev, bspec] + r_out,
        out_shape=[jax.ShapeDtypeStruct(q.shape, F32)] * 3 + [jax.ShapeDtypeStruct(bias.shape, F32)] + r_shape,
        scratch_shapes=[pltpu.VMEM((2, QBLK, 128), F32), pltpu.VMEM((2, QBLK, 128), F32)] + r_scr,
        compiler_params=_cp(("arbitrary", "arbitrary")),
    )(q, k, k, v, v, bias, o, lse, do, dlse, *r_args)


def _t5_bucket(dist):
    max_exact = N_BUCKETS // 2
    d = jnp.maximum(dist, 1).astype(F32)
    large = max_exact + (jnp.log(d / max_exact) / math.log(MAX_DISTANCE / max_exact)
                         * (N_BUCKETS - max_exact)).astype(jnp.int32)
    large = jnp.minimum(large, N_BUCKETS - 1)
    return jnp.where(dist < max_exact, dist, large)


def _bucket_maps():
    qi = jnp.arange(QBLK)[:, None]
    kj = jnp.arange(2 * QBLK)[None, :] - QBLK
    dist = jnp.maximum(qi - kj, 0)
    return jnp.stack([_t5_bucket(dist * r) for r in DIL])


def bias_table(rel_bias, buckets):
    def body(tbl_ref, bk_ref, o_ref):
        for h in range(DL_HEADS):
            bk = bk_ref[h // 4]

            def step(b, acc):
                return jnp.where(bk == b, tbl_ref[b, h], acc)

            o_ref[h] = lax.fori_loop(0, N_BUCKETS, step, jnp.zeros(bk.shape, F32))

    return pl.pallas_call(
        body, name="bias_table", out_shape=jax.ShapeDtypeStruct((DL_HEADS,) + buckets.shape[1:], F32),
        in_specs=[pl.BlockSpec(memory_space=pltpu.SMEM), pl.BlockSpec(memory_space=pltpu.VMEM)],
        out_specs=pl.BlockSpec(memory_space=pltpu.VMEM),
    )(rel_bias, buckets)


def bias_grad(ds, buckets):
    def body(ds_ref, bk_ref, o_ref):
        lane = lax.broadcasted_iota(jnp.int32, (1, 128), 1)
        for h in range(DL_HEADS):
            dsv = ds_ref[h]
            bk = bk_ref[h // 4]

            def step(b, row):
                return jnp.where(lane == b, jnp.sum(jnp.where(bk == b, dsv, 0.0)), row)

            o_ref[h:h + 1, :] = lax.fori_loop(0, N_BUCKETS, step, jnp.zeros((1, 128), F32))

    return pl.pallas_call(
        body, name="bias_grad", out_shape=jax.ShapeDtypeStruct((DL_HEADS, 128), F32),
        in_specs=[pl.BlockSpec(memory_space=pltpu.VMEM)] * 2, out_specs=pl.BlockSpec(memory_space=pltpu.VMEM),
    )(ds, buckets)


def f_attn_out(x, oa, o, lse, w):
    og = [jnp.concatenate([o[2 * g], o[2 * g + 1]], axis=1) for g in range(3)]
    lg = [jnp.concatenate([lse[2 * g], lse[2 * g + 1]], axis=1) for g in range(3)]
    m = jnp.maximum(jnp.maximum(lg[0], lg[1]), lg[2])
    e = [jnp.exp(l - m) for l in lg]
    den = e[0] + e[1] + e[2]
    ob = (e[0] * og[0] + e[1] * og[1] + e[2] * og[2]) / den
    return x + mm(jnp.concatenate([oa, ob], axis=1), w)


def norm_shift_fwd(x, g, tm=256):
    S = x.shape[0]

    def body(x_ref, xp_ref, g_ref, h_ref, hs_ref):
        h = rms(x_ref[...], g_ref[...])
        hp = rms(xp_ref[7:8, :], g_ref[...])
        hp = jnp.where(pl.program_id(0) == 0, 0.0, hp)
        row = lax.broadcasted_iota(jnp.int32, (tm, D), 0)
        h_ref[...] = h
        hs_ref[...] = jnp.where(row == 0, hp, pltpu.roll(h, 1, 0))

    return pl.pallas_call(
        body, name="rw_norm_shift", grid=(S // tm,),
        in_specs=[pl.BlockSpec((tm, D), lambda t: (t, 0)),
                  pl.BlockSpec((8, D), lambda t: (jnp.maximum(t * (tm // 8) - 1, 0), 0)),
                  pl.BlockSpec((1, D), lambda t: (0, 0))],
        out_specs=[pl.BlockSpec((tm, D), lambda t: (t, 0))] * 2,
        out_shape=[jax.ShapeDtypeStruct((S, D), F32)] * 2,
        compiler_params=_cp(("parallel",)),
    )(x, x, g)


def norm_shift_bwd(x, g, dh, dhs, dres, tm=256):
    S = x.shape[0]
    nt = S // tm

    def body(x_ref, g_ref, dh_ref, dhs_ref, dhn_ref, dr_ref, dx_ref, dg_ref):
        t = pl.program_id(0)
        nxt = jnp.where(t == nt - 1, 0.0, dhn_ref[0:1, :])
        row = lax.broadcasted_iota(jnp.int32, (tm, D), 0)
        tot = dh_ref[...] + jnp.where(row == tm - 1, nxt, pltpu.roll(dhs_ref[...], tm - 1, 0))
        _, vjp = jax.vjp(rms, x_ref[...], g_ref[...])
        dx, dg = vjp(tot)
        dx_ref[...] = dr_ref[...] + dx

        @pl.when(t == 0)
        def _():
            dg_ref[...] = dg

        @pl.when(t != 0)
        def _():
            dg_ref[...] += dg

    tile = pl.BlockSpec((tm, D), lambda t: (t, 0))
    return pl.pallas_call(
        body, name="rw_norm_shift_bwd", grid=(nt,),
        in_specs=[tile, pl.BlockSpec((1, D), lambda t: (0, 0)), tile, tile,
                  pl.BlockSpec((8, D), lambda t: (jnp.minimum((t + 1) * (tm // 8), S // 8 - 1), 0)), tile],
        out_specs=[tile, pl.BlockSpec((1, D), lambda t: (0, 0))],
        out_shape=[jax.ShapeDtypeStruct((S, D), F32), jax.ShapeDtypeStruct((1, D), F32)],
        compiler_params=_cp(("arbitrary",)),
    )(x, g, dh, dhs, dhs, dres)


def f_rw_proj(h, hs, mix, w):
    return mm(h + (hs - h) * mix, w)


def f_rw_mid(h, hs, r, k, v, mix3, w0, a0, kkw, kaw, w1, w2, a1, a2, g1, g2):
    xx = hs - h
    xw, xa, xg = h + xx * mix3[0:1], h + xx * mix3[1:2], h + xx * mix3[2:3]
    w_log = -softplus(-(w0 + mm(jnp.tanh(mm(xw, w1)), w2))) - 0.5
    lw = -jnp.exp(w_log)
    ag = jax.nn.sigmoid(a0 + mm(mm(xa, a1), a2))
    gate = mm(jax.nn.sigmoid(mm(xg, g1)), g2)
    kk = k * kkw
    kk = kk / jnp.maximum(jnp.sqrt(group_sum(kk * kk, RW_H)), 1e-12)
    kmod = k * (1.0 + (ag - 1.0) * kaw)
    return (to_heads(r), to_heads(lw), to_heads(kmod), to_heads(v), to_heads(-kk), to_heads(kk * ag), gate)


def f_rw_post(yh, rh, kh, vh, gate, x, lng, lnb, rk, wo):
    mu = jnp.mean(yh, axis=-1, keepdims=True)
    var = jnp.mean(jnp.square(yh - mu), axis=-1, keepdims=True)
    yn = (yh - mu) * lax.rsqrt(var + GN_EPS)
    bonus = jnp.sum(rh * kh * rk, axis=-1, keepdims=True) * vh
    y = from_heads(yn) * lng + lnb + from_heads(bonus)
    return x + mm(y * gate, wo)


def _split2(x):
    hi = x.astype(BF16)
    return hi, (x - hi.astype(F32)).astype(BF16)


def _b3(x, y, cx, cy):
    xh, xl = _split2(x)
    yh, yl = _split2(y)
    x3 = jnp.concatenate([xh, xh, xl], axis=cx)
    y3 = jnp.concatenate([yh, yl, yh], axis=cy)
    return lax.dot_general(x3, y3, (((cx,), (cy,)), ((0,), (0,))), preferred_element_type=F32)


@jax.custom_vjp
def b_nt(x, y):
    return _b3(x, y, 2, 2)


@jax.custom_vjp
def b_nn(x, y):
    return _b3(x, y, 2, 1)


@jax.custom_vjp
def b_tn(x, y):
    return _b3(x, y, 1, 1)


def _b1(x, y, cx, cy):
    return lax.dot_general(x.astype(BF16), y.astype(BF16), (((cx,), (cy,)), ((0,), (0,))), preferred_element_type=F32)


b_nt.defvjp(lambda x, y: (b_nt(x, y), (x, y)), lambda r, g: (_b1(g, r[1], 2, 1), _b1(g, r[0], 1, 1)))
b_nn.defvjp(lambda x, y: (b_nn(x, y), (x, y)), lambda r, g: (_b1(g, r[1], 2, 2), _b1(r[0], g, 1, 1)))
b_tn.defvjp(lambda x, y: (b_tn(x, y), (x, y)), lambda r, g: (_b1(r[1], g, 2, 2), _b1(r[0], g, 2, 1)))


def _tri_apply(x, lower):
    H, C, _ = x.shape
    ii = lax.broadcasted_iota(jnp.int32, (C, C), 0)
    jj = lax.broadcasted_iota(jnp.int32, (C, C), 1)
    m = jnp.broadcast_to(((jj <= ii) if lower else (jj >= ii)).astype(BF16), (H, C, C))
    x1 = x.astype(BF16)
    r1 = x - x1.astype(F32)
    x2 = r1.astype(BF16)
    x3 = (r1 - x2.astype(F32)).astype(BF16)
    return lax.dot_general(jnp.concatenate([m, m, m], axis=2), jnp.concatenate([x1, x2, x3], axis=1),
                           (((2,), (1,)), ((0,), (0,))), preferred_element_type=F32)


@jax.custom_vjp
def run_sum(x):
    return _tri_apply(x, True)


run_sum.defvjp(lambda x: (run_sum(x), None), lambda _, g: (_tri_apply(g, False),))


def rwkv_chunk(S0, r, lw, k, v, a, b):
    H, C, _ = r.shape
    V = S0.shape[1]
    ii = lax.broadcasted_iota(jnp.int32, (C, C), 0)
    jj = lax.broadcasted_iota(jnp.int32, (C, C), 1)
    strict = jj < ii
    i2 = lax.broadcasted_iota(jnp.int32, (C, 2 * C), 0)
    j2 = lax.broadcasted_iota(jnp.int32, (C, 2 * C), 1)
    incl2 = jnp.where(j2 >= C, j2 - C, j2) <= i2
    g = run_sum(lw)
    ig = jnp.exp(-g)
    ar = jnp.concatenate([a * jnp.exp(g - lw), r * jnp.exp(g)], axis=1)
    bk = jnp.concatenate([b * ig, k * ig], axis=1)
    m = b_nt(ar, bk)
    a_ab = jnp.where(strict, m[:, :C, :C], 0.0)
    a_ak = jnp.where(strict, m[:, :C, C:], 0.0)
    b_r = jnp.where(incl2, m[:, C:, :], 0.0)
    p = b_nt(ar, S0)
    u = p[:, :C] + b_nn(a_ak, v)
    nmat, n = a_ab, 1
    while n < C:
        n *= 2
        if n < C:
            z = b_nn(nmat, jnp.concatenate([u, nmat], axis=2))
            u, nmat = u + z[:, :, :V], z[:, :, V:]
        else:
            u = u + b_nn(nmat, u)
    uv = jnp.concatenate([u, v], axis=1)
    y = p[:, C:] + b_nn(b_r, uv)
    g_end = g[:, C - 1:C, :]
    dec = jnp.exp(g_end - g)
    s_new = S0 * jnp.exp(g_end) + b_tn(uv, jnp.concatenate([b * dec, k * dec], axis=1))
    return y, s_new


def rwkv_fwd(r, lw, k, v, a, b):
    H, S, _ = r.shape
    C = RW_CHUNK

    def body(r_ref, lw_ref, k_ref, v_ref, a_ref, b_ref, y_ref, s_ref, s_scr):
        @pl.when(pl.program_id(0) == 0)
        def _():
            s_scr[...] = jnp.zeros_like(s_scr)

        s0 = s_scr[...]
        s_ref[0] = s0
        y, s1 = rwkv_chunk(s0, r_ref[...], lw_ref[...], k_ref[...], v_ref[...], a_ref[...], b_ref[...])
        y_ref[...] = y
        s_scr[...] = s1

    bs = pl.BlockSpec((H, C, HEAD), lambda c: (0, c, 0))
    return pl.pallas_call(
        body, name="rwkv_fwd", grid=(S // C,), in_specs=[bs] * 6,
        out_specs=[bs, pl.BlockSpec((1, H, HEAD, HEAD), lambda c: (c, 0, 0, 0))],
        out_shape=[jax.ShapeDtypeStruct((H, S, HEAD), F32), jax.ShapeDtypeStruct((S // C, H, HEAD, HEAD), F32)],
        scratch_shapes=[pltpu.VMEM((H, HEAD, HEAD), F32)],
        compiler_params=_cp(("arbitrary",)),
    )(r, lw, k, v, a, b)


def rwkv_bwd(r, lw, k, v, a, b, states, dy):
    H, S, _ = r.shape
    C = RW_CHUNK
    nc = S // C

    def body(r_ref, lw_ref, k_ref, v_ref, a_ref, b_ref, s_ref, dy_ref, dr, dlw, dk, dv, da, db, ds_scr):
        @pl.when(pl.program_id(0) == 0)
        def _():
            ds_scr[...] = jnp.zeros_like(ds_scr)

        _, vjp = jax.vjp(rwkv_chunk, s_ref[0], r_ref[...], lw_ref[...], k_ref[...], v_ref[...], a_ref[...], b_ref[...])
        grads = vjp((dy_ref[...], ds_scr[...]))
        ds_scr[...] = grads[0]
        for o, gv in zip((dr, dlw, dk, dv, da, db), grads[1:]):
            o[...] = gv

    bs = pl.BlockSpec((H, C, HEAD), lambda c: (0, nc - 1 - c, 0))
    return pl.pallas_call(
        body, name="rwkv_bwd", grid=(nc,),
        in_specs=[bs] * 6 + [pl.BlockSpec((1, H, HEAD, HEAD), lambda c: (nc - 1 - c, 0, 0, 0)), bs],
        out_specs=[bs] * 6, out_shape=[jax.ShapeDtypeStruct((H, S, HEAD), F32)] * 6,
        scratch_shapes=[pltpu.VMEM((H, HEAD, HEAD), F32)],
        compiler_params=_cp(("arbitrary",)),
    )(r, lw, k, v, a, b, states, dy)


def loss_head(y, target, tm=512):
    S = y.shape[0]

    def body(y_ref, t_ref, dy_ref, l_ref):
        e = y_ref[...] - t_ref[...]
        dy_ref[...] = e * (1.0 / D)
        part = jnp.broadcast_to(0.5 * jnp.sum(jnp.mean(e * e, axis=-1, keepdims=True)), (1, 128))

        @pl.when(pl.program_id(0) == 0)
        def _():
            l_ref[...] = part

        @pl.when(pl.program_id(0) != 0)
        def _():
            l_ref[...] += part

    tile = pl.BlockSpec((tm, D), lambda t: (t, 0))
    return pl.pallas_call(
        body, name="loss_head", grid=(S // tm,), in_specs=[tile, tile],
        out_specs=[tile, pl.BlockSpec((1, 128), lambda t: (0, 0))],
        out_shape=[jax.ShapeDtypeStruct((S, D), F32), jax.ShapeDtypeStruct((1, 128), F32)],
        compiler_params=_cp(("arbitrary",)),
    )(y, target)


def _row_tile(rows, cols, budget=1 << 19):
    best = None
    for tr in range(8, rows + 1, 8):
        if rows % tr == 0 and tr * cols <= budget:
            best = tr
    return best or rows


def _adam(w, g, m, v):
    m = ADAM_B1 * m + (1.0 - ADAM_B1) * g
    v = ADAM_B2 * v + (1.0 - ADAM_B2) * jnp.square(g)
    m_hat = m / (1.0 - ADAM_B1 ** ADAM_STEP)
    v_hat = v / (1.0 - ADAM_B2 ** ADAM_STEP)
    return -ADAM_LR * (m_hat / (jnp.sqrt(v_hat) + ADAM_EPS) + ADAM_WD * w), m, v


def sum_slots(name, parts, dtype=F32, extras=()):
    n = 0 if parts is None else parts.shape[0]
    R, C = extras[0].shape if parts is None else parts.shape[1:]
    tr = _row_tile(R, C * (n + len(extras)))
    ins = ([] if parts is None else [parts]) + list(extras)

    def body(*refs):
        terms = [] if parts is None else [refs[0][i] for i in range(n)]
        terms += [r[...] for r in refs[len(ins) - len(extras):len(ins)]]
        s = terms[0].astype(F32)
        for t in terms[1:]:
            s = s + t.astype(F32)
        refs[len(ins)][...] = s.astype(dtype)

    tile = pl.BlockSpec((tr, C), lambda t: (t, 0))
    return pl.pallas_call(
        body, name=name, grid=(R // tr,),
        in_specs=([] if parts is None else [pl.BlockSpec((n, tr, C), lambda t: (0, t, 0))]) + [tile] * len(extras),
        out_specs=tile, out_shape=jax.ShapeDtypeStruct((R, C), dtype), compiler_params=_cp(("parallel",)),
    )(*ins)


def adam_step(name, ga, gb, w, m, v):
    R, C = w.shape
    tr = _row_tile(R, C, 1 << 17)
    ins = [ga] + ([gb] if gb is not None else []) + [w, m, v]

    def body(*refs):
        g = refs[0][...]
        if gb is not None:
            g = g + refs[1][...]
        w_ref, m_ref, v_ref, g_out, d_out, m_out, v_out = refs[len(ins) - 3:]
        d, m2, v2 = _adam(w_ref[...], g, m_ref[...], v_ref[...])
        g_out[...] = g
        d_out[...] = d
        m_out[...] = m2
        v_out[...] = v2

    tile = pl.BlockSpec((tr, C), lambda t: (t, 0))
    return pl.pallas_call(
        body, name=name, grid=(R // tr,), in_specs=[tile] * len(ins), out_specs=[tile] * 4,
        out_shape=[jax.ShapeDtypeStruct((R, C), F32)] * 4, compiler_params=_cp(("parallel",)),
    )(*ins)


def _place():
    return lax.axis_index("x"), lax.axis_index("y"), lax.axis_index("c")


def _flip(me, mask):
    return tuple(1 - v if mk else v for v, mk in zip(me, mask))


CHIP_MASKS = ((1, 0, 0), (0, 1, 0), (1, 1, 0))
ALL_MASKS = tuple((a, b, c) for a in (0, 1) for b in (0, 1) for c in (0, 1) if (a, b, c) != (0, 0, 0))


def _chip(dev):
    return 2 * dev[0] + dev[1]


def _devno(dev):
    return 4 * dev[0] + 2 * dev[1] + dev[2]


class Pushes:
    def __init__(self, arrays, out_shapes, masks, copies, src_of, dst_of, alias=False):
        self.arrays, self.out_shapes, self.masks, self.copies = list(arrays), list(out_shapes), masks, copies
        self.src_of, self.dst_of, self.alias = src_of, dst_of, alias
        self.n = len(self.arrays)

    def sem_shapes(self):
        k = self.n * len(self.masks) * self.copies
        return [pltpu.SemaphoreType.DMA((k,)), pltpu.SemaphoreType.DMA((k,))]

    def ops(self, ins, outs, send_sems, recv_sems):
        me = _place()
        sends, lands = [], []
        for i in range(self.n):
            for j, mk in enumerate(self.masks):
                peer = _flip(me, mk)
                srcs, dsts = self.src_of(ins[i], me, j), self.dst_of(outs[i], me, j)
                here = self.dst_of(outs[i], peer, j)
                for q in range(self.copies):
                    sem = (i * len(self.masks) + j) * self.copies + q
                    sends.append(pltpu.make_async_remote_copy(
                        src_ref=srcs[q], dst_ref=dsts[q], send_sem=send_sems.at[sem], recv_sem=recv_sems.at[sem],
                        device_id=peer, device_id_type=MESH))
                    lands.append(pltpu.make_async_remote_copy(
                        src_ref=here[q], dst_ref=here[q], send_sem=send_sems.at[sem], recv_sem=recv_sems.at[sem],
                        device_id=peer, device_id_type=MESH))

        def start():
            for cp in sends:
                cp.start()

        def wait():
            for cp in lands:
                cp.wait_recv()
            for cp in sends:
                cp.wait_send()

        return start, wait


_HBM = pl.BlockSpec(memory_space=pl.ANY)


def exchange(name, p, local_of=None):
    n = p.n

    def body(*refs):
        ins, outs = refs[:n], refs[n:2 * n]
        start, wait = p.ops(ins, outs, refs[2 * n], refs[2 * n + 1])
        locals_ = []
        if local_of is not None:
            for i in range(n):
                src, dst = local_of(ins[i], outs[i], _place())
                locals_.append(pltpu.make_async_copy(src, dst, refs[2 * n + 2].at[i]))
                locals_[-1].start()
        start()
        wait()
        for cp in locals_:
            cp.wait()

    return pl.pallas_call(
        body, name=name, in_specs=[_HBM] * n, out_specs=[_HBM] * n, out_shape=p.out_shapes,
        scratch_shapes=p.sem_shapes() + ([pltpu.SemaphoreType.DMA((n,))] if local_of is not None else []),
        input_output_aliases={i: i for i in range(n)} if p.alias else {},
    )(*p.arrays)


def _half(c, rows):
    return pl.ds(c * (rows // 2), rows // 2)


def gather_pushes(arrays):
    outs = [jax.ShapeDtypeStruct((N_CHIPS,) + a.shape, a.dtype) for a in arrays]
    sib = len(CHIP_MASKS)
    return Pushes(arrays, outs, CHIP_MASKS + ((0, 0, 1),), 1,
                  src_of=lambda r, me, j: [r] if j == sib else [r.at[_half(me[2], r.shape[0])]],
                  dst_of=lambda o, sender, j: [o.at[_chip(sender)]] if j == sib else
                  [o.at[_chip(sender), _half(sender[2], o.shape[1])]])


def gather_swap(name, got):
    outs = [jax.ShapeDtypeStruct(a.shape, a.dtype) for a in got]
    return exchange(name, Pushes(
        got, outs, ((0, 0, 1),), len(CHIP_MASKS),
        src_of=lambda r, me, j: [r.at[_chip(_flip(me, mk)), _half(me[2], r.shape[1])] for mk in CHIP_MASKS],
        dst_of=lambda o, sender, j: [o.at[_chip(_flip(sender, mk)), _half(sender[2], o.shape[1])] for mk in CHIP_MASKS],
        alias=True))


def reduce_swap(arrays):
    split = [a.reshape(N_CHIPS, 2, a.shape[1] // 2, a.shape[2]) for a in arrays]
    half_shapes = [jax.ShapeDtypeStruct((N_CHIPS,) + a.shape[2:], F32) for a in split]
    return split, Pushes(split, half_shapes, ((0, 0, 1),), 1,
                         src_of=lambda r, me, j: [r.at[:, 1 - me[2]]], dst_of=lambda o, sender, j: [o])


def reduce_begin(tag, names, arrays, wire):
    split, pushes = reduce_swap(arrays)
    return reduce_sum(names, split, exchange(f"grad_pre_swap_{tag}", pushes), wire)


def reduce_sum(names, split, theirs, wire):
    c = lax.axis_index("c")
    chip_sum = []
    for nm, a, t, dt in zip(names, split, theirs, wire):
        own = lax.dynamic_index_in_dim(a, c, axis=1, keepdims=False)
        flat = lambda v: v.reshape(-1, v.shape[-1])
        chip_sum.append(sum_slots(f"sum2_{nm}", None, dt, [flat(own), flat(t)]).reshape(t.shape))
    pushes = Pushes(chip_sum, [jax.ShapeDtypeStruct((len(CHIP_MASKS),) + a.shape[1:], a.dtype) for a in chip_sum],
                    CHIP_MASKS, 1,
                    src_of=lambda r, me, j: [r.at[_chip(_flip(me, CHIP_MASKS[j]))]],
                    dst_of=lambda o, sender, j: [o.at[j]])
    return chip_sum, pushes


def reduce_end(tag, names, chip_sum, landed):
    x, y, c = _place()
    halves = [sum_slots(f"sum4_{nm}", p, F32, [lax.dynamic_index_in_dim(a, _chip((x, y, c)), axis=0, keepdims=False)])
              for nm, p, a in zip(names, landed, chip_sum)]
    others = exchange(f"grad_final_swap_{tag}", Pushes(
        halves, [jax.ShapeDtypeStruct(a.shape, F32) for a in halves], ((0, 0, 1),), 1,
        src_of=lambda r, me, j: [r], dst_of=lambda o, sender, j: [o]))
    return [jnp.concatenate([jnp.where(c == 0, h, o), jnp.where(c == 0, o, h)], axis=0) for h, o in zip(halves, others)]


def gather_all(arrays):
    outs = [jax.ShapeDtypeStruct((8,) + a.shape, a.dtype) for a in arrays]
    return exchange("gather_replicated", Pushes(
        arrays, outs, ALL_MASKS, 1, src_of=lambda r, me, j: [r], dst_of=lambda o, sender, j: [o.at[_devno(sender)]]),
        local_of=lambda r, o, me: (r, o.at[_devno(me)]))


def _unshard_cols(g):
    return jnp.transpose(g, (1, 0, 2)).reshape(g.shape[1], -1)


def _shard_cols(a):
    return jnp.transpose(a.reshape(a.shape[0], N_CHIPS, -1), (1, 0, 2))


class Weights(dict):
    def ride(self, kernel_name):
        return None

    def arrived(self, kernel_name, outs):
        pass


def _forward_backward(x, tgt, W, grads_early=None):
    S = x.shape[0]
    G = {}
    sd = jax.ShapeDtypeStruct

    def ffn(xin, l, j):
        return ffn_fwd(xin, W["ffn_norm"][l][j], W["ffn_w_gate", l, j], W["ffn_w_up", l, j], W["ffn_w_down", l, j], l, j)

    def ffn_back(xin, dout, l, j):
        gn = W["ffn_norm"][l][j]
        dh, G["ffn_w_gate", l, j], G["ffn_w_up", l, j], G["ffn_w_down", l, j] = ffn_bwd(
            xin, gn, W["ffn_w_gate", l, j], W["ffn_w_up", l, j], W["ffn_w_down", l, j], dout, l, j)
        dx, G[("ffn_norm", l, j)] = norm_bwd(f"ffn_norm_bwd_{l}{j}", xin, gn, dh, dout)
        return dx

    x0 = x
    x1 = ffn(x0, 0, 0)
    g0 = W["mix_norm"][0]
    sbq, sbk, sbv = tile_fwd(f_attn_sb, "attn_in_sb", [x1], [g0, W["attn_w_in"][0]], [sd((S, SB_W), F32)] * 3, 256)
    dl_shape = sd((DL_PAIRS, S, 128), F32)
    qn, = tile_fwd(f_attn_qk, "attn_in_q", [x1], [g0, W["attn_w_in"][1], W["attn_q_norm"]], [dl_shape], 256)
    kn, = tile_fwd(f_attn_qk, "attn_in_k", [x1], [g0, W["attn_w_in"][2], W["attn_k_norm"]], [dl_shape], 256)
    vv, = tile_fwd(f_attn_v, "attn_in_v", [x1], [g0, W["attn_w_in"][3]], [dl_shape], 256)
    oa, sb_wts, *rode = sb_fwd(sbq, sbk, sbv, W.ride("sb_fwd"))
    W.arrived("sb_fwd", rode)
    qs, ks, vs = (reorder(nm, t, DIL, False) for nm, t in (("sub_q", qn), ("sub_k", kn), ("sub_v", vv)))
    o_s, lse_s, *rode = dil_fwd(qs, ks, vs, W["bias_mat"], W.ride("dil_fwd"))
    W.arrived("dil_fwd", rode)
    o_n, lse_n = reorder("nat_o", o_s, DIL, True), reorder("nat_lse", lse_s, DIL, True)
    x2, = tile_fwd(f_attn_out, "attn_out", [x1, oa, o_n, lse_n], [W["attn_w_out"]], [sd((S, D), F32)], 256)
    x3 = ffn(x2, 0, 1)
    x4 = ffn(x3, 1, 0)
    g1 = W["mix_norm"][1]
    h, hs = norm_shift_fwd(x4, g1)
    mix = W["rw_mix"]
    r, = tile_fwd(f_rw_proj, "rw_proj_r", [h, hs], [mix[0:1], W["rw_wr"]], [sd((S, D), F32)], 256)
    k, = tile_fwd(f_rw_proj, "rw_proj_k", [h, hs], [mix[2:3], W["rw_wk"]], [sd((S, D), F32)], 256)
    v, = tile_fwd(f_rw_proj, "rw_proj_v", [h, hs], [mix[3:4], W["rw_wv"]], [sd((S, D), F32)], 256)
    mix3 = jnp.concatenate([mix[1:2], mix[4:5], mix[5:6]], axis=0)
    mid_w = [mix3, W["rw_w0"], W["rw_a0"], W["rw_kk"], W["rw_ka"], W["rw_w1"], W["rw_w2"], W["rw_a1"], W["rw_a2"],
             W["rw_g1"], W["rw_g2"]]
    hshape = sd((RW_H, S, HEAD), F32)
    mid_tiles = [h, hs, r, k, v]
    rh, lwh, kh, vh, ah, bh, gate = tile_fwd(f_rw_mid, "rw_mid", mid_tiles, mid_w, [hshape] * 6 + [sd((S, D), F32)], 128)
    yh, states = rwkv_fwd(rh, lwh, kh, vh, ah, bh)
    post_w = [W["rw_lnx_g"], W["rw_lnx_b"], W["rw_rk"], W["rw_wo"]]
    post_tiles = [yh, rh, kh, vh, gate, x4]
    x5, = tile_fwd(f_rw_post, "rw_post", post_tiles, post_w, [sd((S, D), F32)], 128)
    x6 = ffn(x5, 1, 1)
    dx6, loss_part = loss_head(x6, tgt)

    dx5 = ffn_back(x5, dx6, 1, 1)
    (dyh, drh, dkh, dvh, dgate, dx4), (d_lng, d_lnb, d_rk, d_wo) = tile_bwd(
        f_rw_post, "rw_post_bwd", post_tiles, post_w, [dx5], 128, [True] * 6, [True] * 4)
    drh2, dlwh, dkh2, dvh2, dah, dbh = rwkv_bwd(rh, lwh, kh, vh, ah, bh, states, dyh)
    mid_cts = [(drh, drh2), dlwh, (dkh, dkh2), (dvh, dvh2), dah, dbh, dgate]
    (dh, dhs, dr, dk, dv), dmid_w = tile_bwd(f_rw_mid, "rw_mid_bwd", mid_tiles, mid_w, mid_cts, 128,
                                             [True] * 5, [True] * len(mid_w))
    dmix = {}
    for nm, ct, row, wname in (("r", dr, 0, "rw_wr"), ("k", dk, 2, "rw_wk"), ("v", dv, 3, "rw_wv")):
        (dh, dhs), (dmix[row], G[wname]) = tile_bwd(
            f_rw_proj, f"rw_proj_{nm}_bwd", [h, hs], [mix[row:row + 1], W[wname]], [ct], 256,
            [True, True], [True, True], acc={0: dh, 1: dhs})
    dx4, G[("mix_norm", 1)] = norm_shift_bwd(x4, g1, dh, dhs, dx4)
    dmix3 = dmid_w[0]
    G["rw_mix"] = jnp.concatenate([dmix[0], dmix3[0:1], dmix[2], dmix[3], dmix3[1:2], dmix3[2:3]], axis=0)
    for nm, gv in zip(("rw_w0", "rw_a0", "rw_kk", "rw_ka", "rw_w1", "rw_w2", "rw_a1", "rw_a2", "rw_g1", "rw_g2"), dmid_w[1:]):
        G[nm] = gv
    G["rw_lnx_g"], G["rw_lnx_b"], G["rw_rk"], G["rw_wo"] = d_lng, d_lnb, d_rk, d_wo
    dx3 = ffn_back(x3, dx4, 1, 0)
    dx2 = ffn_back(x2, dx3, 0, 1)
    (dx1, doa, do_n, dlse_n), (G["attn_w_out"],) = tile_bwd(
        f_attn_out, "attn_out_bwd", [x1, oa, o_n, lse_n], [W["attn_w_out"]], [dx2], 256, [True] * 4, [True])
    do_s, dlse_s = reorder("sub_do", do_n, DIL, False), reorder("sub_dlse", dlse_n, DIL, False)
    ride, swapped = grads_early(G) if grads_early is not None else (None, None)
    dqs, dks, dvs, dsum, *rode = dil_bwd(qs, ks, vs, W["bias_mat"], o_s, lse_s, do_s, dlse_s, ride)
    ride, landed = swapped(rode) if swapped is not None else (None, None)
    G["rel_bias"] = bias_grad(dsum, W["buckets"])
    dqn, dkn, dvv = (reorder(nm, t, DIL, True) for nm, t in (("nat_dq", dqs), ("nat_dk", dks), ("nat_dv", dvs)))
    dsbq, dsbk, dsbv, *rode = sb_bwd(sbq, sbk, sbv, doa, sb_wts, ride)
    if landed is not None:
        landed(rode)
    dg0 = []
    dwin = []
    (dx1,), (dg, dw) = tile_bwd(f_attn_sb, "attn_in_sb_bwd", [x1], [g0, W["attn_w_in"][0]], [dsbq, dsbk, dsbv], 256,
                                [True], [True, True], acc={0: dx1})
    dg0.append(dg), dwin.append(dw)
    (dx1,), (dg, dw, G["attn_q_norm"]) = tile_bwd(f_attn_qk, "attn_in_q_bwd", [x1], [g0, W["attn_w_in"][1], W["attn_q_norm"]],
                                                  [dqn], 256, [True], [True] * 3, acc={0: dx1})
    dg0.append(dg), dwin.append(dw)
    (dx1,), (dg, dw, G["attn_k_norm"]) = tile_bwd(f_attn_qk, "attn_in_k_bwd", [x1], [g0, W["attn_w_in"][2], W["attn_k_norm"]],
                                                  [dkn], 256, [True], [True] * 3, acc={0: dx1})
    dg0.append(dg), dwin.append(dw)
    (dx1,), (dg, dw) = tile_bwd(f_attn_v, "attn_in_v_bwd", [x1], [g0, W["attn_w_in"][3]], [dvv], 256,
                                [True], [True, True], acc={0: dx1})
    dg0.append(dg), dwin.append(dw)
    G[("mix_norm", 0)] = dg0
    G["attn_w_in"] = dwin
    dx0 = ffn_back(x0, dx1, 0, 0)
    return loss_part, dx0, G


VEC_ROWS = ("ffn_norm", "rw_mix", "rw_w0", "rw_a0", "rw_kk", "rw_ka", "rw_lnx_g", "rw_lnx_b")


def kernel(x, ffn_norm, ffn_w_gate, ffn_w_up, ffn_w_down, mix_norm, rel_bias, attn_w_in, attn_q_norm, attn_k_norm, attn_w_out, rw_mix, rw_w0, rw_w1, rw_w2, rw_a0, rw_a1, rw_a2, rw_g1, rw_g2, rw_kk, rw_ka, rw_rk, rw_wr, rw_wk, rw_wv, rw_wo, rw_lnx_g, rw_lnx_b, loss_target, m_ffn_norm, m_ffn_w_gate, m_ffn_w_up, m_ffn_w_down, m_mix_norm, m_rel_bias, m_attn_w_in, m_attn_q_norm, m_attn_k_norm, m_attn_w_out, m_rw_mix, m_rw_w0, m_rw_w1, m_rw_w2, m_rw_a0, m_rw_a1, m_rw_a2, m_rw_g1, m_rw_g2, m_rw_kk, m_rw_ka, m_rw_rk, m_rw_wr, m_rw_wk, m_rw_wv, m_rw_wo, m_rw_lnx_g, m_rw_lnx_b, v_ffn_norm, v_ffn_w_gate, v_ffn_w_up, v_ffn_w_down, v_mix_norm, v_rel_bias, v_attn_w_in, v_attn_q_norm, v_attn_k_norm, v_attn_w_out, v_rw_mix, v_rw_w0, v_rw_w1, v_rw_w2, v_rw_a0, v_rw_a1, v_rw_a2, v_rw_g1, v_rw_g2, v_rw_kk, v_rw_ka, v_rw_rk, v_rw_wr, v_rw_wk, v_rw_wv, v_rw_wo, v_rw_lnx_g, v_rw_lnx_b):
    names = ["ffn_norm", "ffn_w_gate", "ffn_w_up", "ffn_w_down", "mix_norm", "rel_bias", "attn_w_in", "attn_q_norm",
             "attn_k_norm", "attn_w_out", "rw_mix", "rw_w0", "rw_w1", "rw_w2", "rw_a0", "rw_a1", "rw_a2", "rw_g1", "rw_g2",
             "rw_kk", "rw_ka", "rw_rk", "rw_wr", "rw_wk", "rw_wv", "rw_wo", "rw_lnx_g", "rw_lnx_b"]
    loc = locals()
    w = {n: loc[n] for n in names}
    mom = {n: loc["m_" + n] for n in names}
    vel = {n: loc["v_" + n] for n in names}
    S = x.shape[1]

    ffn3 = ("ffn_w_gate", "ffn_w_up", "ffn_w_down")
    rw_mats = ("rw_w1", "rw_w2", "rw_a1", "rw_a2", "rw_g1", "rw_g2", "rw_wr", "rw_wk", "rw_wv", "rw_wo")
    cols_split = ("attn_w_out", "rw_w2", "rw_a2", "rw_g2")
    shard = {"vec": jnp.concatenate([w[n].reshape(-1, 256) for n in VEC_ROWS], axis=0)}
    for n in ffn3:
        for l in range(2):
            for j in range(2):
                shard[n, l, j] = w[n][l, j].astype(BF16)
    for n in ("attn_w_in", "attn_w_out") + rw_mats:
        shard[n] = w[n].reshape(-1, w[n].shape[-1]).astype(BF16)
    ffn_keys = lambda l, j: [(n, l, j) for n in ffn3]
    w_groups = {"first": ["vec"] + ffn_keys(0, 0) + ["attn_w_in", "attn_w_out"],
                "sb_fwd": ffn_keys(0, 1) + ffn_keys(1, 0) + list(rw_mats),
                "dil_fwd": ffn_keys(1, 1)}
    label = lambda key: key if isinstance(key, str) else f"{key[0]}_{key[1]}{key[2]}"

    class Streamed(Weights):
        def ride(self, kernel_name):
            return gather_pushes([shard[k] for k in w_groups[kernel_name]])

        def arrived(self, kernel_name, outs):
            for key, g in zip(w_groups[kernel_name], gather_swap(f"gather_swap_{kernel_name}", outs)):
                if key == "vec":
                    vec_full = _unshard_cols(g)
                    self["ffn_norm"] = [[vec_full[2 * l + j][None] for j in range(2)] for l in range(2)]
                    self["rw_mix"] = vec_full[4:10]
                    for i, n in enumerate(("rw_w0", "rw_a0", "rw_kk", "rw_ka", "rw_lnx_g", "rw_lnx_b")):
                        self[n] = vec_full[10 + i][None]
                elif key == "attn_w_in":
                    self[key] = [g[p] for p in range(N_CHIPS)]
                elif key in cols_split:
                    self[key] = _unshard_cols(g)
                elif isinstance(key, str):
                    self[key] = g.reshape(D, -1)
                else:
                    self[key] = g

    buckets = _bucket_maps()
    W = Streamed({"mix_norm": [mix_norm[0:1], mix_norm[1:2]], "attn_q_norm": attn_q_norm, "attn_k_norm": attn_k_norm,
                  "rw_rk": rw_rk[0][:, None, :], "buckets": buckets, "bias_mat": bias_table(rel_bias, buckets)})
    W.arrived("first", exchange("gather_weights", W.ride("first")))

    def slots(key, G):
        if key == "vec":
            rows = [G[("ffn_norm", l, j)] for l in range(2) for j in range(2)] + [G["rw_mix"]] + \
                   [G[n] for n in ("rw_w0", "rw_a0", "rw_kk", "rw_ka", "rw_lnx_g", "rw_lnx_b")]
            return _shard_cols(jnp.concatenate(rows, axis=0))
        if key == "attn_w_in":
            return jnp.stack(G[key])
        if key in cols_split:
            return _shard_cols(G[key])
        if isinstance(key, str):
            return G[key].reshape(N_CHIPS, D // N_CHIPS, -1)
        return G[key]

    g_groups = {"early": ffn_keys(1, 1) + ffn_keys(1, 0) + ffn_keys(0, 1) + list(rw_mats) + ["attn_w_out"],
                "late": ["vec", "attn_w_in"] + ffn_keys(0, 0)}
    wire = lambda keys: [F32 if k == "vec" else BF16 for k in keys]
    part = {}

    def grads_early(G):
        keys = g_groups["early"]
        names_ = [label(k) for k in keys]
        split, swap_pushes = reduce_swap([slots(k, G) for k in keys])

        def swapped(theirs):
            chip_sum, pushes = reduce_sum(names_, split, theirs, wire(keys))
            return pushes, lambda landed: part.update(zip(keys, reduce_end("early", names_, chip_sum, landed)))

        return swap_pushes, swapped

    loss_part, dx, G = _forward_backward(x[0], loss_target[0], W, grads_early)
    loss = lax.psum(loss_part[0, 0], ("x", "y", "c"))
    keys = g_groups["late"]
    chip_sum, pushes = reduce_begin("late", [label(k) for k in keys], [slots(k, G) for k in keys], wire(keys))
    part.update(zip(keys, reduce_end("late", [label(k) for k in keys], chip_sum, exchange("scatter_grads", pushes))))
    for n in ffn3:
        part[n] = jnp.stack([jnp.stack([part[n, l, j] for j in range(2)]) for l in range(2)])

    rep = jnp.concatenate([G[("mix_norm", 0)][0] + G[("mix_norm", 0)][1] + G[("mix_norm", 0)][2] + G[("mix_norm", 0)][3],
                           G[("mix_norm", 1)]], axis=0).reshape(16, 128)
    rep = jnp.concatenate([rep, G["rel_bias"], jnp.pad(G["attn_q_norm"], ((0, 0), (0, 64))),
                           jnp.pad(G["attn_k_norm"], ((0, 0), (0, 64))), G["rw_rk"].reshape(8, 128),
                           jnp.zeros((2, 128), F32)], axis=0)
    rep_sum = sum_slots("sum_replicated", gather_all([rep])[0])
    g_rep = {
        "mix_norm": rep_sum[0:16].reshape(2, D),
        "rel_bias": jnp.transpose(rep_sum[16:28, :N_BUCKETS]),
        "attn_q_norm": rep_sum[28:29, :HEAD], "attn_k_norm": rep_sum[29:30, :HEAD],
        "rw_rk": rep_sum[30:38].reshape(1, RW_H, HEAD),
    }

    out = {}

    def adam(n, ga, gb):
        shp = w[n].shape
        to2 = lambda a: a.reshape(-1, shp[-1])
        res = adam_step(f"adam_{n}", to2(ga), None if gb is None else to2(gb), to2(w[n]), to2(mom[n]), to2(vel[n]))
        out[n] = tuple(r.reshape(shp) for r in res)

    for n in ffn3 + ("attn_w_in", "attn_w_out") + rw_mats:
        adam(n, part[n], None)
    rows = {"ffn_norm": (0, 4), "rw_mix": (4, 10), "rw_w0": (10, 11), "rw_a0": (11, 12), "rw_kk": (12, 13),
            "rw_ka": (13, 14), "rw_lnx_g": (14, 15), "rw_lnx_b": (15, 16)}
    for n, (lo, hi) in rows.items():
        adam(n, part["vec"][lo:hi], None)
    for n, gv in g_rep.items():
        adam(n, gv, None)

    grads = [out[n][0] for n in names]
    deltas = [out[n][1] for n in names]
    new_m = [out[n][2] for n in names]
    new_v = [out[n][3] for n in names]
    return (loss, dx[None], *grads, *deltas, *new_m, *new_v)
```

```python
import functools
import math

import jax
import jax.numpy as jnp
from jax import lax
from jax.experimental import pallas as pl
from jax.experimental.pallas import tpu as pltpu

F32, BF16 = jnp.float32, jnp.bfloat16
HI = lax.Precision.HIGHEST
MESH = pl.DeviceIdType.MESH

D = 1024
HEAD = 64
N_CHIPS = 4
FF_SHARD = 704
SB_W = 256
DL_HEADS = 12
DL_PAIRS = 6
DIL = (1, 4, 16)
QBLK = 128
N_BUCKETS = 32
MAX_DISTANCE = 2048
RW_H = 16
RW_CHUNK = 64
NORM_EPS = 1e-6
GN_EPS = 64e-5
NEG_INF = -1e30
VMEM_LIMIT = 56 * 1024 * 1024

ADAM_LR, ADAM_B1, ADAM_B2, ADAM_EPS, ADAM_WD, ADAM_STEP = 0.001, 0.9, 0.999, 1e-08, 0.01, 10


def _cp(sem):
    return pltpu.CompilerParams(dimension_semantics=sem, vmem_limit_bytes=VMEM_LIMIT)


def _dg(a, b, dims, prec=None):
    return lax.dot_general(a, b, (dims, ((), ())), precision=prec, preferred_element_type=F32)


def _bdot(a, b, dims):
    return _dg(a.astype(BF16), b.astype(BF16), dims)


@jax.custom_vjp
def mm(a, b):
    return _bdot(a, b, ((1,), (0,)))


def _mm_fwd(a, b):
    return _bdot(a, b, ((1,), (0,))), (a, b)


def _mm_bwd(res, g):
    a, b = res
    return _bdot(g, b, ((1,), (1,))), _bdot(a, g, ((0,), (0,)))


mm.defvjp(_mm_fwd, _mm_bwd)


def rms(x, g):
    return x * lax.rsqrt(jnp.mean(x * x, axis=-1, keepdims=True) + NORM_EPS) * g


def _pieces(x):
    x1 = x.astype(BF16)
    r1 = x - x1.astype(F32)
    x2 = r1.astype(BF16)
    return jnp.concatenate([x1, x2, (r1 - x2.astype(F32)).astype(BF16)], axis=-1)


def _group_sum(x, nh):
    w = x.shape[-1]
    e = (lax.broadcasted_iota(jnp.int32, (w, nh), 0) // HEAD == lax.broadcasted_iota(jnp.int32, (w, nh), 1)).astype(BF16)
    s = _dg(_pieces(x), jnp.concatenate([e, e, e], axis=0), ((1,), (0,)))
    return _dg(_pieces(s), jnp.concatenate([e, e, e], axis=1), ((1,), (1,)))


@functools.partial(jax.custom_vjp, nondiff_argnums=(1,))
def group_sum(x, nh):
    return _group_sum(x, nh)


group_sum.defvjp(lambda x, nh: (_group_sum(x, nh), None), lambda nh, _, g: (_group_sum(g, nh),))


def softplus(u):
    return jnp.maximum(u, 0.0) + jnp.log1p(jnp.exp(-jnp.abs(u)))


def to_heads(t, nh=RW_H):
    return jnp.stack([t[:, HEAD * h:HEAD * (h + 1)] for h in range(nh)])


def from_heads(t):
    return jnp.concatenate([t[h] for h in range(t.shape[0])], axis=-1)


def _tile_spec(shape, tm):
    if len(shape) == 2:
        return pl.BlockSpec((tm, shape[1]), lambda t: (t, 0))
    return pl.BlockSpec((shape[0], tm, shape[2]), lambda t: (0, t, 0))


def _full_spec(shape):
    nd = len(shape)
    return pl.BlockSpec(tuple(shape), lambda t: (0,) * nd)


def _rows(a):
    return a.shape[0] if a.ndim == 2 else a.shape[1]


def tile_fwd(f, name, tiles, weights, outs, tm):
    nt, nw = len(tiles), len(weights)

    def body(*refs):
        tv = [r[...] for r in refs[:nt]]
        wv = [r[...].astype(F32) for r in refs[nt:nt + nw]]
        res = f(*tv, *wv)
        if not isinstance(res, (tuple, list)):
            res = (res,)
        for o, v in zip(refs[nt + nw:], res):
            o[...] = v.astype(o.dtype)

    return pl.pallas_call(
        body, name=name, grid=(_rows(tiles[0]) // tm,),
        in_specs=[_tile_spec(a.shape, tm) for a in tiles] + [_full_spec(w.shape) for w in weights],
        out_specs=[_tile_spec(o.shape, tm) for o in outs],
        out_shape=list(outs),
        compiler_params=_cp(("parallel",)),
    )(*tiles, *weights)


def tile_bwd(f, name, tiles, weights, cts, tm, dt, dw, acc=None):
    acc = acc or {}
    groups = [c if isinstance(c, tuple) else (c,) for c in cts]
    cts = [a for grp in groups for a in grp]
    nt, nw, nc = len(tiles), len(weights), len(cts)
    acc_idx = sorted(acc)
    na = len(acc_idx)
    dti = [i for i in range(nt) if dt[i]]
    dwi = [i for i in range(nw) if dw[i]]

    def body(*refs):
        tv = [r[...] for r in refs[:nt]]
        wv = [r[...].astype(F32) for r in refs[nt:nt + nw]]
        crefs = list(refs[nt + nw:nt + nw + nc])
        cv = []
        for grp in groups:
            terms = [crefs.pop(0)[...] for _ in grp]
            cv.append(functools.reduce(lambda a, b: a + b, terms))
        av = {i: r[...] for i, r in zip(acc_idx, refs[nt + nw + nc:nt + nw + nc + na])}
        orefs = refs[nt + nw + nc + na:]

        def g(*diff):
            t2, w2 = list(tv), list(wv)
            for i, v in zip(dti, diff[:len(dti)]):
                t2[i] = v
            for i, v in zip(dwi, diff[len(dti):]):
                w2[i] = v
            res = f(*t2, *w2)
            return tuple(res) if isinstance(res, (tuple, list)) else (res,)

        _, vjp = jax.vjp(g, *[tv[i] for i in dti], *[wv[i] for i in dwi])
        grads = vjp(tuple(cv))
        for k, i in enumerate(dti):
            gt = grads[k]
            if i in av:
                gt = gt + av[i]
            orefs[k][...] = gt
        first = pl.program_id(0) == 0
        for k, i in enumerate(dwi):
            o = orefs[len(dti) + k]
            gw = grads[len(dti) + k]

            @pl.when(first)
            def _(o=o, gw=gw):
                o[...] = gw

            @pl.when(jnp.logical_not(first))
            def _(o=o, gw=gw):
                o[...] += gw

    out_shape = [jax.ShapeDtypeStruct(tiles[i].shape, F32) for i in dti] + \
                [jax.ShapeDtypeStruct(weights[i].shape, F32) for i in dwi]
    res = pl.pallas_call(
        body, name=name, grid=(_rows(tiles[0]) // tm,),
        in_specs=[_tile_spec(a.shape, tm) for a in tiles] + [_full_spec(w.shape) for w in weights] +
                 [_tile_spec(c.shape, tm) for c in cts] + [_tile_spec(tiles[i].shape, tm) for i in acc_idx],
        out_specs=[_tile_spec(tiles[i].shape, tm) for i in dti] + [_full_spec(weights[i].shape) for i in dwi],
        out_shape=out_shape,
        compiler_params=_cp(("arbitrary",)),
    )(*tiles, *weights, *cts, *[acc[i] for i in acc_idx])
    return list(res[:len(dti)]), list(res[len(dti):])


def _ffn_wspec(rows, cols, cfirst):
    if cfirst:
        return pl.BlockSpec((1, rows, cols), lambda c, t: (c, 0, 0))
    return pl.BlockSpec((1, rows, cols), lambda t, c: (c, 0, 0))


def ffn_fwd(x, g, wg, wu, wd, l, j, tm=512):
    S = x.shape[0]

    def body(x_ref, g_ref, wg_ref, wu_ref, wd_ref, o_ref, h_ref, acc_ref):
        c = pl.program_id(1)

        @pl.when(c == 0)
        def _():
            h_ref[...] = rms(x_ref[...], g_ref[...]).astype(BF16)
            acc_ref[...] = jnp.zeros_like(acc_ref)

        h = h_ref[...]
        a = _bdot(h, wg_ref[0], ((1,), (0,)))
        b = _bdot(h, wu_ref[0], ((1,), (0,)))
        y = a * jax.nn.sigmoid(a) * b
        acc_ref[...] += _bdot(y, wd_ref[0], ((1,), (0,)))

        @pl.when(c == N_CHIPS - 1)
        def _():
            o_ref[...] = x_ref[...] + 0.5 * acc_ref[...]

    return pl.pallas_call(
        body, name=f"ffn_fwd_{l}{j}", grid=(S // tm, N_CHIPS),
        in_specs=[pl.BlockSpec((tm, D), lambda t, c: (t, 0)), pl.BlockSpec((1, D), lambda t, c: (0, 0)),
                  _ffn_wspec(D, FF_SHARD, False), _ffn_wspec(D, FF_SHARD, False), _ffn_wspec(FF_SHARD, D, False)],
        out_specs=pl.BlockSpec((tm, D), lambda t, c: (t, 0)),
        out_shape=jax.ShapeDtypeStruct((S, D), F32),
        scratch_shapes=[pltpu.VMEM((tm, D), BF16), pltpu.VMEM((tm, D), F32)],
        compiler_params=_cp(("parallel", "arbitrary")),
    )(x, g, wg, wu, wd)


def ffn_bwd(x, g, wg, wu, wd, dout, l, j, tm=512):
    S = x.shape[0]

    def body(x_ref, g_ref, wg_ref, wu_ref, wd_ref, do_ref, dh_ref, dwg_ref, dwu_ref, dwd_ref):
        t = pl.program_id(1)
        h = rms(x_ref[...], g_ref[...]).astype(BF16)
        wgv, wuv, wdv = wg_ref[0], wu_ref[0], wd_ref[0]
        a = _bdot(h, wgv, ((1,), (0,)))
        b = _bdot(h, wuv, ((1,), (0,)))
        sig = jax.nn.sigmoid(a)
        s = a * sig
        dyd = 0.5 * do_ref[...]
        dy = _bdot(dyd, wdv, ((1,), (1,)))
        dwd = _bdot(s * b, dyd, ((0,), (0,)))
        db = dy * s
        da = dy * b * (sig * (1.0 + a * (1.0 - sig)))
        dwg = _bdot(h, da, ((0,), (0,)))
        dwu = _bdot(h, db, ((0,), (0,)))
        dh_ref[0] = _bdot(da, wgv, ((1,), (1,))) + _bdot(db, wuv, ((1,), (1,)))

        @pl.when(t == 0)
        def _():
            dwg_ref[0] = dwg
            dwu_ref[0] = dwu
            dwd_ref[0] = dwd

        @pl.when(t != 0)
        def _():
            dwg_ref[0] += dwg
            dwu_ref[0] += dwu
            dwd_ref[0] += dwd

    return pl.pallas_call(
        body, name=f"ffn_bwd_{l}{j}", grid=(N_CHIPS, S // tm),
        in_specs=[pl.BlockSpec((tm, D), lambda c, t: (t, 0)), pl.BlockSpec((1, D), lambda c, t: (0, 0)),
                  _ffn_wspec(D, FF_SHARD, True), _ffn_wspec(D, FF_SHARD, True), _ffn_wspec(FF_SHARD, D, True),
                  pl.BlockSpec((tm, D), lambda c, t: (t, 0))],
        out_specs=[pl.BlockSpec((1, tm, D), lambda c, t: (c, t, 0)),
                   _ffn_wspec(D, FF_SHARD, True), _ffn_wspec(D, FF_SHARD, True), _ffn_wspec(FF_SHARD, D, True)],
        out_shape=[jax.ShapeDtypeStruct((N_CHIPS, S, D), F32)] + [jax.ShapeDtypeStruct(a.shape, F32) for a in (wg, wu, wd)],
        compiler_params=_cp(("parallel", "arbitrary")),
    )(x, g, wg, wu, wd, dout)


def norm_bwd(name, x, g, dh_parts, dres, tm=256):
    S = x.shape[0]
    P = dh_parts.shape[0]

    def body(x_ref, g_ref, dh_ref, dr_ref, dx_ref, dg_ref):
        dh = dh_ref[0]
        for p in range(1, P):
            dh = dh + dh_ref[p]
        _, vjp = jax.vjp(rms, x_ref[...], g_ref[...])
        dx, dg = vjp(dh)
        dx_ref[...] = dr_ref[...] + dx

        @pl.when(pl.program_id(0) == 0)
        def _():
            dg_ref[...] = dg

        @pl.when(pl.program_id(0) != 0)
        def _():
            dg_ref[...] += dg

    return pl.pallas_call(
        body, name=name, grid=(S // tm,),
        in_specs=[pl.BlockSpec((tm, D), lambda t: (t, 0)), pl.BlockSpec((1, D), lambda t: (0, 0)),
                  pl.BlockSpec((P, tm, D), lambda t: (0, t, 0)), pl.BlockSpec((tm, D), lambda t: (t, 0))],
        out_specs=[pl.BlockSpec((tm, D), lambda t: (t, 0)), pl.BlockSpec((1, D), lambda t: (0, 0))],
        out_shape=[jax.ShapeDtypeStruct((S, D), F32), jax.ShapeDtypeStruct((1, D), F32)],
        compiler_params=_cp(("arbitrary",)),
    )(x, g, dh_parts, dres)


def f_attn_sb(x, g, w):
    pr = mm(rms(x, g), w)
    return pr[:, :SB_W], pr[:, SB_W:2 * SB_W], pr[:, 2 * SB_W:]


def _pairs(y):
    return jnp.stack([y[:, 128 * j:128 * (j + 1)] for j in range(DL_PAIRS)])


def f_attn_qk(x, g, w, nrm):
    pr = mm(rms(x, g), w)
    ms = group_sum(pr * pr, DL_HEADS) * (1.0 / HEAD)
    return _pairs(pr * lax.rsqrt(ms + NORM_EPS) * jnp.concatenate([nrm] * DL_HEADS, axis=1))


def f_attn_v(x, g, w):
    return _pairs(mm(rms(x, g), w))


def _masked(strict, x):
    return x if strict is None else jnp.where(strict, x, 0.0)


def _head_stack(x):
    nh = x.shape[1] // HEAD
    lane_head = lax.broadcasted_iota(jnp.int32, (1, x.shape[1]), 1) // HEAD
    return jnp.concatenate([jnp.where(lane_head == h, x, 0.0) for h in range(nh)], axis=0).astype(BF16)


def _head_pick(xs):
    nh = xs.shape[1] // HEAD
    rows = xs.shape[0] // nh
    lane_head = lax.broadcasted_iota(jnp.int32, (1, xs.shape[1]), 1) // HEAD
    out = xs[:rows]
    for h in range(1, nh):
        out = jnp.where(lane_head == h, xs[rows * h:rows * (h + 1)], out)
    return out


def _sb_tiles(qs, kblk, strict):
    z = _dg(qs, kblk, ((1,), (1,))) * (HEAD ** -0.5)
    keep = -(jnp.maximum(z, 0.0) + jnp.log(1.0 + jnp.exp(-jnp.abs(z))))
    return z, _masked(strict, keep)


def _tri(n, upper):
    r = lax.broadcasted_iota(jnp.int32, (n, n), 0)
    c = lax.broadcasted_iota(jnp.int32, (n, n), 1)
    return ((r > c) if upper else (r < c)).astype(BF16)


def _tri_sums(x, tri):
    hi, lo = _split2(x)
    return _dg(jnp.concatenate([hi, lo], axis=1), jnp.concatenate([tri, tri], axis=0), ((1,), (0,)))


SB_UNROLL = 4


def _sb_diag(tb, nh):
    r = lax.broadcasted_iota(jnp.int32, (nh * tb, tb), 0)
    return lax.broadcasted_iota(jnp.int32, (nh * tb, tb), 1) < lax.rem(r, tb)


def _sb_sweep(step, first, count, carry, direction, commit=None):
    def run(kbs, c):
        outs = []
        for kb in kbs:
            c, out = step(kb, c)
            outs.append(out)
        if commit is not None:
            for kb, out in zip(kbs, outs):
                commit(kb, out)
        return c

    rem = count % SB_UNROLL
    carry = lax.fori_loop(0, rem, lambda i, c: run([first + direction * i], c), carry)
    return lax.fori_loop(
        0, count // SB_UNROLL,
        lambda g, c: run([first + direction * (rem + SB_UNROLL * g + u) for u in range(SB_UNROLL)], c), carry)


def _riding(ride, refs, n_in, n_out, first, last):
    if ride is None:
        return refs, lambda: None
    n = ride.n
    own = refs[:n_in] + refs[n_in + n:n_in + n + n_out] + refs[n_in + 2 * n + n_out:len(refs) - 2]
    start, wait = ride.ops(refs[n_in:n_in + n], refs[n_in + n + n_out:n_in + 2 * n + n_out], refs[-2], refs[-1])
    pl.when(first)(start)
    return own, lambda: pl.when(last)(wait)


def _ride_specs(ride):
    if ride is None:
        return [], [], [], [], []
    return [_HBM] * ride.n, [_HBM] * ride.n, ride.out_shapes, ride.sem_shapes(), ride.arrays


def sb_fwd(q, k, v, ride=None, tb=QBLK):
    S = q.shape[0]
    nh = SB_W // HEAD
    nb = S // tb
    r_in, r_out, r_shape, r_scr, r_args = _ride_specs(ride)

    def body(*refs):
        qb = pl.program_id(0)
        (q_ref, k_ref, v_ref, o_ref, w_ref), finish = _riding(ride, refs, 3, 2, qb == 0, qb == nb - 1)
        diag = _sb_diag(tb, nh)
        after_mat = _tri(tb, True)
        qs = _head_stack(q_ref[...])

        def step(kb, carry, strict):
            acc, run = carry
            rows = pl.ds(pl.multiple_of(kb * tb, tb), tb)
            z, keep = _sb_tiles(qs, k_ref[rows, :].astype(BF16), strict)
            w = _masked(strict, jnp.exp(z + keep + _tri_sums(keep, after_mat) + run)).astype(BF16)
            w_ref[0, kb] = w
            acc = acc + _dg(w, v_ref[rows, :].astype(BF16), ((1,), (0,)))
            return acc, run + jnp.sum(keep, axis=1, keepdims=True)

        init = (jnp.zeros((nh * tb, SB_W), F32), jnp.zeros((nh * tb, 1), F32))
        carry = step(qb, init, diag)
        acc, _ = _sb_sweep(lambda kb, c: (step(kb, c, None), None), qb - 1, qb, carry, -1)
        o_ref[...] = _head_pick(acc)
        finish()

    return pl.pallas_call(
        body, name="sb_fwd", grid=(S // tb,),
        in_specs=[pl.BlockSpec((tb, SB_W), lambda i: (i, 0)), pl.BlockSpec((S, SB_W), lambda i: (0, 0)),
                  pl.BlockSpec((S, SB_W), lambda i: (0, 0))] + r_in,
        out_specs=[pl.BlockSpec((tb, SB_W), lambda i: (i, 0)),
                   pl.BlockSpec((1, nb, nh * tb, tb), lambda i: (i, 0, 0, 0))] + r_out,
        out_shape=[jax.ShapeDtypeStruct((S, SB_W), F32), jax.ShapeDtypeStruct((nb, nb, nh * tb, tb), BF16)] + r_shape,
        scratch_shapes=r_scr,
        compiler_params=_cp(("arbitrary",)),
    )(q, k, v, *r_args)


def sb_bwd(q, k, v, do, wts, ride=None, tb=QBLK):
    S = q.shape[0]
    nh = SB_W // HEAD
    nb = S // tb
    scale = HEAD ** -0.5
    r_in, r_out, r_shape, r_scr, r_args = _ride_specs(ride)

    def body(*refs):
        qb = pl.program_id(0)
        (q_ref, k_ref, v_ref, do_ref, w_ref, dq_ref, dk_ref, dv_ref, g_scr), finish = _riding(
            ride, refs, 5, 3, qb == 0, qb == nb - 1)

        @pl.when(qb == 0)
        def _():
            dk_ref[...] = jnp.zeros_like(dk_ref)
            dv_ref[...] = jnp.zeros_like(dv_ref)

        diag = _sb_diag(tb, nh)
        before_mat = _tri(tb, False)
        qs = _head_stack(q_ref[...])
        dos = _head_stack(do_ref[...])

        def weights_pass(kb, carry):
            rows = pl.ds(pl.multiple_of(kb * tb, tb), tb)
            w = w_ref[0, kb]
            g_scr[kb] = _dg(dos, v_ref[rows, :].astype(BF16), ((1,), (1,))) * w.astype(F32)
            return carry, _dg(w, dos, ((0,), (0,)))

        def add_rows(ref):
            def commit(kb, val):
                ref[pl.ds(pl.multiple_of(kb * tb, tb), tb), :] += val
            return commit

        zero_run = jnp.zeros((nh * tb, 1), F32)
        _sb_sweep(weights_pass, 0, qb + 1, 0, 1, add_rows(dv_ref))

        def left_to_right(kb, carry, strict):
            dq, run = carry
            rows = pl.ds(pl.multiple_of(kb * tb, tb), tb)
            kblk = k_ref[rows, :].astype(BF16)
            gw = g_scr[kb]
            sig = jax.nn.sigmoid(_dg(qs, kblk, ((1,), (1,))) * scale)
            dkeep = _masked(strict, _tri_sums(gw, before_mat) + run)
            dz = ((gw * (1.0 - sig) - dkeep * sig) * scale).astype(BF16)
            dq = dq + _dg(dz, kblk, ((1,), (0,)))
            return (dq, run + jnp.sum(gw, axis=1, keepdims=True)), _dg(dz, qs, ((0,), (0,)))

        carry = _sb_sweep(lambda kb, c: left_to_right(kb, c, None), 0, qb,
                          (jnp.zeros((nh * tb, SB_W), F32), zero_run), 1, add_rows(dk_ref))
        (dq, _), dk_diag = left_to_right(qb, carry, diag)
        add_rows(dk_ref)(qb, dk_diag)
        dq_ref[...] = _head_pick(dq)
        finish()

    whole = pl.BlockSpec((S, SB_W), lambda i: (0, 0))
    blk = pl.BlockSpec((tb, SB_W), lambda i: (i, 0))
    return pl.pallas_call(
        body, name="sb_bwd", grid=(S // tb,),
        in_specs=[blk, whole, whole, blk, pl.BlockSpec((1, nb, nh * tb, tb), lambda i: (i, 0, 0, 0))] + r_in,
        out_specs=[blk, whole, whole] + r_out,
        out_shape=[jax.ShapeDtypeStruct((S, SB_W), F32)] * 3 + r_shape,
        scratch_shapes=[pltpu.VMEM((S // tb, nh * tb, tb), F32)] + r_scr,
        compiler_params=_cp(("arbitrary",)),
    )(q, k, v, do, wts, *r_args)


def reorder(name, x, groups, inverse):
    P, S, _ = x.shape

    def body(x_ref, o_ref):
        p = pl.program_id(0)
        for gi, r in enumerate(groups):
            @pl.when(p // 2 == gi)
            def _(r=r):
                L = S // r
                if r == 1:
                    o_ref[...] = x_ref[...]
                for c in range(r if r > 1 else 0):
                    if inverse:
                        o_ref[pl.ds(c, L, stride=r), :] = x_ref[c * L:(c + 1) * L, :]
                    else:
                        o_ref[c * L:(c + 1) * L, :] = x_ref[pl.ds(c, L, stride=r), :]

    slab = pl.BlockSpec((None, S, 128), lambda p: (p, 0, 0))
    return pl.pallas_call(
        body, name=name, grid=(P,), in_specs=[slab], out_specs=slab,
        out_shape=jax.ShapeDtypeStruct(x.shape, x.dtype), compiler_params=_cp(("parallel",)),
    )(x)


def _dil_blocks(S):
    return S // QBLK


def _dil_mask(n_in_stream):
    qi = lax.broadcasted_iota(jnp.int32, (QBLK, 2 * QBLK), 0)
    kj = lax.broadcasted_iota(jnp.int32, (QBLK, 2 * QBLK), 1) - QBLK
    dist = qi - kj
    return (dist >= 0) & (dist <= QBLK) & ((n_in_stream > 0) | (kj >= 0))


def _stream_pos(gi, i, S):
    nb = jnp.where(gi == 0, S // (QBLK * DIL[0]), jnp.where(gi == 1, S // (QBLK * DIL[1]), S // (QBLK * DIL[2])))
    return i % nb


def dil_fwd(q, k, v, bias, ride=None):
    S = q.shape[1]
    nblk = _dil_blocks(S)
    r_in, r_out, r_shape, r_scr, r_args = _ride_specs(ride)

    def body(*refs):
        gi, i = pl.program_id(0), pl.program_id(1)
        (q_ref, kc_ref, kp_ref, vc_ref, vp_ref, b_ref, o_ref, l_ref), finish = _riding(
            ride, refs, 6, 2, (gi == 0) & (i == 0), (gi == len(DIL) - 1) & (i == nblk - 1))
        mask = _dil_mask(_stream_pos(gi, i, S))
        for j in range(2):
            q2, kc, kp, vc, vp = q_ref[j], kc_ref[j], kp_ref[j], vc_ref[j], vp_ref[j]
            os_, ls_ = [], []
            for hh in range(2):
                sl = slice(HEAD * hh, HEAD * (hh + 1))
                kw = jnp.concatenate([kp[:, sl], kc[:, sl]], axis=0)
                vw = jnp.concatenate([vp[:, sl], vc[:, sl]], axis=0)
                lg = _bdot(q2[:, sl], kw, ((1,), (1,))) * (HEAD ** -0.5) + b_ref[2 * j + hh]
                lg = jnp.where(mask, lg, NEG_INF)
                m = jnp.max(lg, axis=-1, keepdims=True)
                p = jnp.exp(lg - m)
                den = jnp.sum(p, axis=-1, keepdims=True)
                os_.append(_bdot(p / den, vw, ((1,), (0,))))
                ls_.append(jnp.broadcast_to(m + jnp.log(den), (QBLK, HEAD)))
            o_ref[j] = jnp.concatenate(os_, axis=1)
            l_ref[j] = jnp.concatenate(ls_, axis=1)
        finish()

    cur = pl.BlockSpec((2, QBLK, 128), lambda g, i: (g, i, 0))
    prev = pl.BlockSpec((2, QBLK, 128), lambda g, i: (g, jnp.maximum(i - 1, 0), 0))
    return pl.pallas_call(
        body, name="dil_fwd", grid=(len(DIL), nblk),
        in_specs=[cur, cur, prev, cur, prev, pl.BlockSpec((4, QBLK, 2 * QBLK), lambda g, i: (g, 0, 0))] + r_in,
        out_specs=[cur, cur] + r_out,
        out_shape=[jax.ShapeDtypeStruct(q.shape, F32)] * 2 + r_shape,
        scratch_shapes=r_scr,
        compiler_params=_cp(("arbitrary", "arbitrary")),
    )(q, k, k, v, v, bias, *r_args)


def dil_bwd(q, k, v, bias, o, lse, do, dlse, ride=None):
    S = q.shape[1]
    nblk = _dil_blocks(S)
    r_in, r_out, r_shape, r_scr, r_args = _ride_specs(ride)

    def body(*refs):
        gi, i = pl.program_id(0), pl.program_id(1)
        (q_ref, kc_ref, kp_ref, vc_ref, vp_ref, b_ref, o_ref, l_ref, do_ref, dl_ref,
         dq_ref, dk_ref, dv_ref, ds_ref, dk_car, dv_car), finish = _riding(
            ride, refs, 10, 4, (gi == 0) & (i == 0), (gi == len(DIL) - 1) & (i == nblk))

        @pl.when(i == 0)
        def _():
            ds_ref[...] = jnp.zeros_like(ds_ref)
            dk_car[...] = jnp.zeros_like(dk_car)
            dv_car[...] = jnp.zeros_like(dv_car)

        @pl.when(i < nblk)
        def _():
            mask = _dil_mask(_stream_pos(gi, i, S))
            for j in range(2):
                q2, kc, kp, vc, vp = q_ref[j], kc_ref[j], kp_ref[j], vc_ref[j], vp_ref[j]
                o2, l2, do2, dl2 = o_ref[j], l_ref[j], do_ref[j], dl_ref[j]
                dqs, dkps, dkcs, dvps, dvcs = [], [], [], [], []
                for hh in range(2):
                    sl = slice(HEAD * hh, HEAD * (hh + 1))
                    qh, doh = q2[:, sl], do2[:, sl]
                    kw = jnp.concatenate([kp[:, sl], kc[:, sl]], axis=0)
                    vw = jnp.concatenate([vp[:, sl], vc[:, sl]], axis=0)
                    lg = _bdot(qh, kw, ((1,), (1,))) * (HEAD ** -0.5) + b_ref[2 * j + hh]
                    p = jnp.where(mask, jnp.exp(lg - l2[:, HEAD * hh:HEAD * hh + 1]), 0.0)
                    dp = _bdot(doh, vw, ((1,), (1,)))
                    delta = jnp.sum(doh * o2[:, sl], axis=-1, keepdims=True)
                    dl = jnp.sum(dl2[:, sl], axis=-1, keepdims=True)
                    ds = p * (dp - delta + dl)
                    ds_ref[2 * j + hh] += ds
                    dsq = ds * (HEAD ** -0.5)
                    dqs.append(_bdot(dsq, kw, ((1,), (0,))))
                    dkw = _bdot(dsq, qh, ((0,), (0,)))
                    dvw = _bdot(p, doh, ((0,), (0,)))
                    dkps.append(dkw[:QBLK])
                    dkcs.append(dkw[QBLK:])
                    dvps.append(dvw[:QBLK])
                    dvcs.append(dvw[QBLK:])
                dq_ref[j] = jnp.concatenate(dqs, axis=1)
                dk_ref[j] = dk_car[j] + jnp.concatenate(dkps, axis=1)
                dv_ref[j] = dv_car[j] + jnp.concatenate(dvps, axis=1)
                dk_car[j] = jnp.concatenate(dkcs, axis=1)
                dv_car[j] = jnp.concatenate(dvcs, axis=1)

        @pl.when(i == nblk)
        def _():
            dk_ref[...] = dk_car[...]
            dv_ref[...] = dv_car[...]

        finish()

    cur = pl.BlockSpec((2, QBLK, 128), lambda g, i: (g, jnp.minimum(i, nblk - 1), 0))
    prev = pl.BlockSpec((2, QBLK, 128), lambda g, i: (g, jnp.clip(i - 1, 0, nblk - 1), 0))
    bspec = pl.BlockSpec((4, QBLK, 2 * QBLK), lambda g, i: (g, 0, 0))
    return pl.pallas_call(
        body, name="dil_bwd", grid=(len(DIL), nblk + 1),
        in_specs=[cur, cur, prev, cur, prev, bspec, cur, cur, cur, cur] + r_in,
        out_specs=[cur, prev, prev, bspec] + r_out,
        out_shape=[jax.ShapeDtypeStruct(q.shape, F32)] * 3 + [jax.ShapeDtypeStruct(bias.shape, F32)] + r_shape,
        scratch_shapes=[pltpu.VMEM((2, QBLK, 128), F32), pltpu.VMEM((2, QBLK, 128), F32)] + r_scr,
        compiler_params=_cp(("arbitrary", "arbitrary")),
    )(q, k, k, v, v, bias, o, lse, do, dlse, *r_args)


def _t5_bucket(dist):
    max_exact = N_BUCKETS // 2
    d = jnp.maximum(dist, 1).astype(F32)
    large = max_exact + (jnp.log(d / max_exact) / math.log(MAX_DISTANCE / max_exact)
                         * (N_BUCKETS - max_exact)).astype(jnp.int32)
    large = jnp.minimum(large, N_BUCKETS - 1)
    return jnp.where(dist < max_exact, dist, large)


def _bucket_maps():
    qi = jnp.arange(QBLK)[:, None]
    kj = jnp.arange(2 * QBLK)[None, :] - QBLK
    dist = jnp.maximum(qi - kj, 0)
    return jnp.stack([_t5_bucket(dist * r) for r in DIL])


def bias_table(rel_bias, buckets):
    def body(tbl_ref, bk_ref, o_ref):
        for h in range(DL_HEADS):
            bk = bk_ref[h // 4]

            def step(b, acc):
                return jnp.where(bk == b, tbl_ref[b, h], acc)

            o_ref[h] = lax.fori_loop(0, N_BUCKETS, step, jnp.zeros(bk.shape, F32))

    return pl.pallas_call(
        body, name="bias_table", out_shape=jax.ShapeDtypeStruct((DL_HEADS,) + buckets.shape[1:], F32),
        in_specs=[pl.BlockSpec(memory_space=pltpu.SMEM), pl.BlockSpec(memory_space=pltpu.VMEM)],
        out_specs=pl.BlockSpec(memory_space=pltpu.VMEM),
    )(rel_bias, buckets)


def bias_grad(ds, buckets):
    def body(ds_ref, bk_ref, o_ref):
        lane = lax.broadcasted_iota(jnp.int32, (1, 128), 1)
        for h in range(DL_HEADS):
            dsv = ds_ref[h]
            bk = bk_ref[h // 4]

            def step(b, row):
                return jnp.where(lane == b, jnp.sum(jnp.where(bk == b, dsv, 0.0)), row)

            o_ref[h:h + 1, :] = lax.fori_loop(0, N_BUCKETS, step, jnp.zeros((1, 128), F32))

    return pl.pallas_call(
        body, name="bias_grad", out_shape=jax.ShapeDtypeStruct((DL_HEADS, 128), F32),
        in_specs=[pl.BlockSpec(memory_space=pltpu.VMEM)] * 2, out_specs=pl.BlockSpec(memory_space=pltpu.VMEM),
    )(ds, buckets)


def f_attn_out(x, oa, o, lse, w):
    og = [jnp.concatenate([o[2 * g], o[2 * g + 1]], axis=1) for g in range(3)]
    lg = [jnp.concatenate([lse[2 * g], lse[2 * g + 1]], axis=1) for g in range(3)]
    m = jnp.maximum(jnp.maximum(lg[0], lg[1]), lg[2])
    e = [jnp.exp(l - m) for l in lg]
    den = e[0] + e[1] + e[2]
    ob = (e[0] * og[0] + e[1] * og[1] + e[2] * og[2]) / den
    return x + mm(jnp.concatenate([oa, ob], axis=1), w)


def norm_shift_fwd(x, g, tm=256):
    S = x.shape[0]

    def body(x_ref, xp_ref, g_ref, h_ref, hs_ref):
        h = rms(x_ref[...], g_ref[...])
        hp = rms(xp_ref[7:8, :], g_ref[...])
        hp = jnp.where(pl.program_id(0) == 0, 0.0, hp)
        row = lax.broadcasted_iota(jnp.int32, (tm, D), 0)
        h_ref[...] = h
        hs_ref[...] = jnp.where(row == 0, hp, pltpu.roll(h, 1, 0))

    return pl.pallas_call(
        body, name="rw_norm_shift", grid=(S // tm,),
        in_specs=[pl.BlockSpec((tm, D), lambda t: (t, 0)),
                  pl.BlockSpec((8, D), lambda t: (jnp.maximum(t * (tm // 8) - 1, 0), 0)),
                  pl.BlockSpec((1, D), lambda t: (0, 0))],
        out_specs=[pl.BlockSpec((tm, D), lambda t: (t, 0))] * 2,
        out_shape=[jax.ShapeDtypeStruct((S, D), F32)] * 2,
        compiler_params=_cp(("parallel",)),
    )(x, x, g)


def norm_shift_bwd(x, g, dh, dhs, dres, tm=256):
    S = x.shape[0]
    nt = S // tm

    def body(x_ref, g_ref, dh_ref, dhs_ref, dhn_ref, dr_ref, dx_ref, dg_ref):
        t = pl.program_id(0)
        nxt = jnp.where(t == nt - 1, 0.0, dhn_ref[0:1, :])
        row = lax.broadcasted_iota(jnp.int32, (tm, D), 0)
        tot = dh_ref[...] + jnp.where(row == tm - 1, nxt, pltpu.roll(dhs_ref[...], tm - 1, 0))
        _, vjp = jax.vjp(rms, x_ref[...], g_ref[...])
        dx, dg = vjp(tot)
        dx_ref[...] = dr_ref[...] + dx

        @pl.when(t == 0)
        def _():
            dg_ref[...] = dg

        @pl.when(t != 0)
        def _():
            dg_ref[...] += dg

    tile = pl.BlockSpec((tm, D), lambda t: (t, 0))
    return pl.pallas_call(
        body, name="rw_norm_shift_bwd", grid=(nt,),
        in_specs=[tile, pl.BlockSpec((1, D), lambda t: (0, 0)), tile, tile,
                  pl.BlockSpec((8, D), lambda t: (jnp.minimum((t + 1) * (tm // 8), S // 8 - 1), 0)), tile],
        out_specs=[tile, pl.BlockSpec((1, D), lambda t: (0, 0))],
        out_shape=[jax.ShapeDtypeStruct((S, D), F32), jax.ShapeDtypeStruct((1, D), F32)],
        compiler_params=_cp(("arbitrary",)),
    )(x, g, dh, dhs, dhs, dres)


def f_rw_proj(h, hs, mix, w):
    return mm(h + (hs - h) * mix, w)


def f_rw_mid(h, hs, r, k, v, mix3, w0, a0, kkw, kaw, w1, w2, a1, a2, g1, g2):
    xx = hs - h
    xw, xa, xg = h + xx * mix3[0:1], h + xx * mix3[1:2], h + xx * mix3[2:3]
    w_log = -softplus(-(w0 + mm(jnp.tanh(mm(xw, w1)), w2))) - 0.5
    lw = -jnp.exp(w_log)
    ag = jax.nn.sigmoid(a0 + mm(mm(xa, a1), a2))
    gate = mm(jax.nn.sigmoid(mm(xg, g1)), g2)
    kk = k * kkw
    kk = kk / jnp.maximum(jnp.sqrt(group_sum(kk * kk, RW_H)), 1e-12)
    kmod = k * (1.0 + (ag - 1.0) * kaw)
    return (to_heads(r), to_heads(lw), to_heads(kmod), to_heads(v), to_heads(-kk), to_heads(kk * ag), gate)


def f_rw_post(yh, rh, kh, vh, gate, x, lng, lnb, rk, wo):
    mu = jnp.mean(yh, axis=-1, keepdims=True)
    var = jnp.mean(jnp.square(yh - mu), axis=-1, keepdims=True)
    yn = (yh - mu) * lax.rsqrt(var + GN_EPS)
    bonus = jnp.sum(rh * kh * rk, axis=-1, keepdims=True) * vh
    y = from_heads(yn) * lng + lnb + from_heads(bonus)
    return x + mm(y * gate, wo)


def _split2(x):
    hi = x.astype(BF16)
    return hi, (x - hi.astype(F32)).astype(BF16)


def _b3(x, y, cx, cy):
    xh, xl = _split2(x)
    yh, yl = _split2(y)
    x3 = jnp.concatenate([xh, xh, xl], axis=cx)
    y3 = jnp.concatenate([yh, yl, yh], axis=cy)
    return lax.dot_general(x3, y3, (((cx,), (cy,)), ((0,), (0,))), preferred_element_type=F32)


@jax.custom_vjp
def b_nt(x, y):
    return _b3(x, y, 2, 2)


@jax.custom_vjp
def b_nn(x, y):
    return _b3(x, y, 2, 1)


@jax.custom_vjp
def b_tn(x, y):
    return _b3(x, y, 1, 1)


def _b1(x, y, cx, cy):
    return lax.dot_general(x.astype(BF16), y.astype(BF16), (((cx,), (cy,)), ((0,), (0,))), preferred_element_type=F32)


b_nt.defvjp(lambda x, y: (b_nt(x, y), (x, y)), lambda r, g: (_b1(g, r[1], 2, 1), _b1(g, r[0], 1, 1)))
b_nn.defvjp(lambda x, y: (b_nn(x, y), (x, y)), lambda r, g: (_b1(g, r[1], 2, 2), _b1(r[0], g, 1, 1)))
b_tn.defvjp(lambda x, y: (b_tn(x, y), (x, y)), lambda r, g: (_b1(r[1], g, 2, 2), _b1(r[0], g, 2, 1)))


def _tri_apply(x, lower):
    H, C, _ = x.shape
    ii = lax.broadcasted_iota(jnp.int32, (C, C), 0)
    jj = lax.broadcasted_iota(jnp.int32, (C, C), 1)
    m = jnp.broadcast_to(((jj <= ii) if lower else (jj >= ii)).astype(BF16), (H, C, C))
    x1 = x.astype(BF16)
    r1 = x - x1.astype(F32)
    x2 = r1.astype(BF16)
    x3 = (r1 - x2.astype(F32)).astype(BF16)
    return lax.dot_general(jnp.concatenate([m, m, m], axis=2), jnp.concatenate([x1, x2, x3], axis=1),
                           (((2,), (1,)), ((0,), (0,))), preferred_element_type=F32)


@jax.custom_vjp
def run_sum(x):
    return _tri_apply(x, True)


run_sum.defvjp(lambda x: (run_sum(x), None), lambda _, g: (_tri_apply(g, False),))


def rwkv_chunk(S0, r, lw, k, v, a, b):
    H, C, _ = r.shape
    V = S0.shape[1]
    ii = lax.broadcasted_iota(jnp.int32, (C, C), 0)
    jj = lax.broadcasted_iota(jnp.int32, (C, C), 1)
    strict = jj < ii
    i2 = lax.broadcasted_iota(jnp.int32, (C, 2 * C), 0)
    j2 = lax.broadcasted_iota(jnp.int32, (C, 2 * C), 1)
    incl2 = jnp.where(j2 >= C, j2 - C, j2) <= i2
    g = run_sum(lw)
    ig = jnp.exp(-g)
    ar = jnp.concatenate([a * jnp.exp(g - lw), r * jnp.exp(g)], axis=1)
    bk = jnp.concatenate([b * ig, k * ig], axis=1)
    m = b_nt(ar, bk)
    a_ab = jnp.where(strict, m[:, :C, :C], 0.0)
    a_ak = jnp.where(strict, m[:, :C, C:], 0.0)
    b_r = jnp.where(incl2, m[:, C:, :], 0.0)
    p = b_nt(ar, S0)
    u = p[:, :C] + b_nn(a_ak, v)
    nmat, n = a_ab, 1
    while n < C:
        n *= 2
        if n < C:
            z = b_nn(nmat, jnp.concatenate([u, nmat], axis=2))
            u, nmat = u + z[:, :, :V], z[:, :, V:]
        else:
            u = u + b_nn(nmat, u)
    uv = jnp.concatenate([u, v], axis=1)
    y = p[:, C:] + b_nn(b_r, uv)
    g_end = g[:, C - 1:C, :]
    dec = jnp.exp(g_end - g)
    s_new = S0 * jnp.exp(g_end) + b_tn(uv, jnp.concatenate([b * dec, k * dec], axis=1))
    return y, s_new


def rwkv_fwd(r, lw, k, v, a, b):
    H, S, _ = r.shape
    C = RW_CHUNK

    def body(r_ref, lw_ref, k_ref, v_ref, a_ref, b_ref, y_ref, s_ref, s_scr):
        @pl.when(pl.program_id(0) == 0)
        def _():
            s_scr[...] = jnp.zeros_like(s_scr)

        s0 = s_scr[...]
        s_ref[0] = s0
        y, s1 = rwkv_chunk(s0, r_ref[...], lw_ref[...], k_ref[...], v_ref[...], a_ref[...], b_ref[...])
        y_ref[...] = y
        s_scr[...] = s1

    bs = pl.BlockSpec((H, C, HEAD), lambda c: (0, c, 0))
    return pl.pallas_call(
        body, name="rwkv_fwd", grid=(S // C,), in_specs=[bs] * 6,
        out_specs=[bs, pl.BlockSpec((1, H, HEAD, HEAD), lambda c: (c, 0, 0, 0))],
        out_shape=[jax.ShapeDtypeStruct((H, S, HEAD), F32), jax.ShapeDtypeStruct((S // C, H, HEAD, HEAD), F32)],
        scratch_shapes=[pltpu.VMEM((H, HEAD, HEAD), F32)],
        compiler_params=_cp(("arbitrary",)),
    )(r, lw, k, v, a, b)


def rwkv_bwd(r, lw, k, v, a, b, states, dy):
    H, S, _ = r.shape
    C = RW_CHUNK
    nc = S // C

    def body(r_ref, lw_ref, k_ref, v_ref, a_ref, b_ref, s_ref, dy_ref, dr, dlw, dk, dv, da, db, ds_scr):
        @pl.when(pl.program_id(0) == 0)
        def _():
            ds_scr[...] = jnp.zeros_like(ds_scr)

        _, vjp = jax.vjp(rwkv_chunk, s_ref[0], r_ref[...], lw_ref[...], k_ref[...], v_ref[...], a_ref[...], b_ref[...])
        grads = vjp((dy_ref[...], ds_scr[...]))
        ds_scr[...] = grads[0]
        for o, gv in zip((dr, dlw, dk, dv, da, db), grads[1:]):
            o[...] = gv

    bs = pl.BlockSpec((H, C, HEAD), lambda c: (0, nc - 1 - c, 0))
    return pl.pallas_call(
        body, name="rwkv_bwd", grid=(nc,),
        in_specs=[bs] * 6 + [pl.BlockSpec((1, H, HEAD, HEAD), lambda c: (nc - 1 - c, 0, 0, 0)), bs],
        out_specs=[bs] * 6, out_shape=[jax.ShapeDtypeStruct((H, S, HEAD), F32)] * 6,
        scratch_shapes=[pltpu.VMEM((H, HEAD, HEAD), F32)],
        compiler_params=_cp(("arbitrary",)),
    )(r, lw, k, v, a, b, states, dy)


def loss_head(y, target, tm=512):
    S = y.shape[0]

    def body(y_ref, t_ref, dy_ref, l_ref):
        e = y_ref[...] - t_ref[...]
        dy_ref[...] = e * (1.0 / D)
        part = jnp.broadcast_to(0.5 * jnp.sum(jnp.mean(e * e, axis=-1, keepdims=True)), (1, 128))

        @pl.when(pl.program_id(0) == 0)
        def _():
            l_ref[...] = part

        @pl.when(pl.program_id(0) != 0)
        def _():
            l_ref[...] += part

    tile = pl.BlockSpec((tm, D), lambda t: (t, 0))
    return pl.pallas_call(
        body, name="loss_head", grid=(S // tm,), in_specs=[tile, tile],
        out_specs=[tile, pl.BlockSpec((1, 128), lambda t: (0, 0))],
        out_shape=[jax.ShapeDtypeStruct((S, D), F32), jax.ShapeDtypeStruct((1, 128), F32)],
        compiler_params=_cp(("arbitrary",)),
    )(y, target)


def _row_tile(rows, cols, budget=1 << 19):
    best = None
    for tr in range(8, rows + 1, 8):
        if rows % tr == 0 and tr * cols <= budget:
            best = tr
    return best or rows


def _adam(w, g, m, v):
    m = ADAM_B1 * m + (1.0 - ADAM_B1) * g
    v = ADAM_B2 * v + (1.0 - ADAM_B2) * jnp.square(g)
    m_hat = m / (1.0 - ADAM_B1 ** ADAM_STEP)
    v_hat = v / (1.0 - ADAM_B2 ** ADAM_STEP)
    return -ADAM_LR * (m_hat / (jnp.sqrt(v_hat) + ADAM_EPS) + ADAM_WD * w), m, v


def sum_slots(name, parts, dtype=F32, extras=()):
    n = 0 if parts is None else parts.shape[0]
    R, C = extras[0].shape if parts is None else parts.shape[1:]
    tr = _row_tile(R, C * (n + len(extras)))
    ins = ([] if parts is None else [parts]) + list(extras)

    def body(*refs):
        terms = [] if parts is None else [refs[0][i] for i in range(n)]
        terms += [r[...] for r in refs[len(ins) - len(extras):len(ins)]]
        s = terms[0].astype(F32)
        for t in terms[1:]:
            s = s + t.astype(F32)
        refs[len(ins)][...] = s.astype(dtype)

    tile = pl.BlockSpec((tr, C), lambda t: (t, 0))
    return pl.pallas_call(
        body, name=name, grid=(R // tr,),
        in_specs=([] if parts is None else [pl.BlockSpec((n, tr, C), lambda t: (0, t, 0))]) + [tile] * len(extras),
        out_specs=tile, out_shape=jax.ShapeDtypeStruct((R, C), dtype), compiler_params=_cp(("parallel",)),
    )(*ins)


def sum_own_half(name, split, theirs, c, dtype):
    nq, _, rh, cols = split.shape
    tr = _row_tile(rh, 2 * cols)

    def body(c_ref, a_ref, b_ref, o_ref):
        o_ref[...] = (a_ref[...] + b_ref[...]).astype(dtype)

    tile = pl.BlockSpec((None, tr, cols), lambda q, t, c_ref: (q, t, 0))
    return pl.pallas_call(
        body, name=name,
        grid_spec=pltpu.PrefetchScalarGridSpec(
            num_scalar_prefetch=1, grid=(nq, rh // tr),
            in_specs=[pl.BlockSpec((None, None, tr, cols), lambda q, t, c_ref: (q, c_ref[0], t, 0)), tile],
            out_specs=tile),
        out_shape=jax.ShapeDtypeStruct((nq, rh, cols), dtype), compiler_params=_cp(("parallel", "parallel")),
    )(jnp.reshape(c, (1,)).astype(jnp.int32), split, theirs)


def sum_landed(name, landed, chip_sum, p):
    n, rh, cols = landed.shape
    tr = _row_tile(rh, (n + 1) * cols)

    def body(p_ref, l_ref, own_ref, o_ref):
        s = l_ref[0].astype(F32)
        for i in range(1, n):
            s = s + l_ref[i].astype(F32)
        o_ref[...] = s + own_ref[...].astype(F32)

    return pl.pallas_call(
        body, name=name,
        grid_spec=pltpu.PrefetchScalarGridSpec(
            num_scalar_prefetch=1, grid=(rh // tr,),
            in_specs=[pl.BlockSpec((n, tr, cols), lambda t, p_ref: (0, t, 0)),
                      pl.BlockSpec((None, tr, cols), lambda t, p_ref: (p_ref[0], t, 0))],
            out_specs=pl.BlockSpec((tr, cols), lambda t, p_ref: (t, 0))),
        out_shape=jax.ShapeDtypeStruct((rh, cols), F32), compiler_params=_cp(("parallel",)),
    )(jnp.reshape(p, (1,)).astype(jnp.int32), landed, chip_sum)


def adam_step(name, ga, gb, w, m, v):
    R, C = w.shape
    tr = _row_tile(R, C, 1 << 17)
    ins = [ga] + ([gb] if gb is not None else []) + [w, m, v]

    def body(*refs):
        g = refs[0][...]
        if gb is not None:
            g = g + refs[1][...]
        w_ref, m_ref, v_ref, g_out, d_out, m_out, v_out = refs[len(ins) - 3:]
        d, m2, v2 = _adam(w_ref[...], g, m_ref[...], v_ref[...])
        g_out[...] = g
        d_out[...] = d
        m_out[...] = m2
        v_out[...] = v2

    tile = pl.BlockSpec((tr, C), lambda t: (t, 0))
    return pl.pallas_call(
        body, name=name, grid=(R // tr,), in_specs=[tile] * len(ins), out_specs=[tile] * 4,
        out_shape=[jax.ShapeDtypeStruct((R, C), F32)] * 4, compiler_params=_cp(("parallel",)),
    )(*ins)


def _place():
    return lax.axis_index("x"), lax.axis_index("y"), lax.axis_index("c")


def _flip(me, mask):
    return tuple(1 - v if mk else v for v, mk in zip(me, mask))


CHIP_MASKS = ((1, 0, 0), (0, 1, 0), (1, 1, 0))
ALL_MASKS = tuple((a, b, c) for a in (0, 1) for b in (0, 1) for c in (0, 1) if (a, b, c) != (0, 0, 0))


def _chip(dev):
    return 2 * dev[0] + dev[1]


def _devno(dev):
    return 4 * dev[0] + 2 * dev[1] + dev[2]


class Pushes:
    def __init__(self, arrays, out_shapes, masks, copies, src_of, dst_of, alias=False):
        self.arrays, self.out_shapes, self.masks, self.copies = list(arrays), list(out_shapes), masks, copies
        self.src_of, self.dst_of, self.alias = src_of, dst_of, alias
        self.n = len(self.arrays)

    def sem_shapes(self):
        k = self.n * len(self.masks) * self.copies
        return [pltpu.SemaphoreType.DMA((k,)), pltpu.SemaphoreType.DMA((k,))]

    def ops(self, ins, outs, send_sems, recv_sems):
        me = _place()
        sends, lands = [], []
        for i in range(self.n):
            for j, mk in enumerate(self.masks):
                peer = _flip(me, mk)
                srcs, dsts = self.src_of(ins[i], me, j), self.dst_of(outs[i], me, j)
                here = self.dst_of(outs[i], peer, j)
                for q in range(self.copies):
                    sem = (i * len(self.masks) + j) * self.copies + q
                    sends.append(pltpu.make_async_remote_copy(
                        src_ref=srcs[q], dst_ref=dsts[q], send_sem=send_sems.at[sem], recv_sem=recv_sems.at[sem],
                        device_id=peer, device_id_type=MESH))
                    lands.append(pltpu.make_async_remote_copy(
                        src_ref=here[q], dst_ref=here[q], send_sem=send_sems.at[sem], recv_sem=recv_sems.at[sem],
                        device_id=peer, device_id_type=MESH))

        def start():
            for cp in sends:
                cp.start()

        def wait():
            for cp in lands:
                cp.wait_recv()
            for cp in sends:
                cp.wait_send()

        return start, wait


_HBM = pl.BlockSpec(memory_space=pl.ANY)


def exchange(name, p, local_of=None):
    n = p.n

    def body(*refs):
        ins, outs = refs[:n], refs[n:2 * n]
        start, wait = p.ops(ins, outs, refs[2 * n], refs[2 * n + 1])
        locals_ = []
        if local_of is not None:
            for i in range(n):
                src, dst = local_of(ins[i], outs[i], _place())
                locals_.append(pltpu.make_async_copy(src, dst, refs[2 * n + 2].at[i]))
                locals_[-1].start()
        start()
        wait()
        for cp in locals_:
            cp.wait()

    return pl.pallas_call(
        body, name=name, in_specs=[_HBM] * n, out_specs=[_HBM] * n, out_shape=p.out_shapes,
        scratch_shapes=p.sem_shapes() + ([pltpu.SemaphoreType.DMA((n,))] if local_of is not None else []),
        input_output_aliases={i: i for i in range(n)} if p.alias else {},
    )(*p.arrays)


def _half(c, rows):
    return pl.ds(c * (rows // 2), rows // 2)


def gather_pushes(arrays):
    outs = [jax.ShapeDtypeStruct((N_CHIPS,) + a.shape, a.dtype) for a in arrays]
    sib = len(CHIP_MASKS)
    return Pushes(arrays, outs, CHIP_MASKS + ((0, 0, 1),), 1,
                  src_of=lambda r, me, j: [r] if j == sib else [r.at[_half(me[2], r.shape[0])]],
                  dst_of=lambda o, sender, j: [o.at[_chip(sender)]] if j == sib else
                  [o.at[_chip(sender), _half(sender[2], o.shape[1])]])


def gather_swap(name, got):
    outs = [jax.ShapeDtypeStruct(a.shape, a.dtype) for a in got]
    return exchange(name, Pushes(
        got, outs, ((0, 0, 1),), len(CHIP_MASKS),
        src_of=lambda r, me, j: [r.at[_chip(_flip(me, mk)), _half(me[2], r.shape[1])] for mk in CHIP_MASKS],
        dst_of=lambda o, sender, j: [o.at[_chip(_flip(sender, mk)), _half(sender[2], o.shape[1])] for mk in CHIP_MASKS],
        alias=True))


def reduce_swap(arrays):
    split = [a.reshape(N_CHIPS, 2, a.shape[1] // 2, a.shape[2]) for a in arrays]
    half_shapes = [jax.ShapeDtypeStruct((N_CHIPS,) + a.shape[2:], F32) for a in split]
    return split, Pushes(split, half_shapes, ((0, 0, 1),), 1,
                         src_of=lambda r, me, j: [r.at[:, 1 - me[2]]], dst_of=lambda o, sender, j: [o])


def reduce_begin(tag, names, arrays, wire):
    split, pushes = reduce_swap(arrays)
    return reduce_sum(names, split, exchange(f"grad_pre_swap_{tag}", pushes), wire)


def reduce_sum(names, split, theirs, wire):
    c = lax.axis_index("c")
    chip_sum = [sum_own_half(f"sum2_{nm}", a, t, c, dt) for nm, a, t, dt in zip(names, split, theirs, wire)]
    pushes = Pushes(chip_sum, [jax.ShapeDtypeStruct((len(CHIP_MASKS),) + a.shape[1:], a.dtype) for a in chip_sum],
                    CHIP_MASKS, 1,
                    src_of=lambda r, me, j: [r.at[_chip(_flip(me, CHIP_MASKS[j]))]],
                    dst_of=lambda o, sender, j: [o.at[j]])
    return chip_sum, pushes


def reduce_end(tag, names, chip_sum, landed):
    x, y, c = _place()
    halves = [sum_landed(f"sum4_{nm}", p, a, _chip((x, y, c))) for nm, p, a in zip(names, landed, chip_sum)]
    others = exchange(f"grad_final_swap_{tag}", Pushes(
        halves, [jax.ShapeDtypeStruct(a.shape, F32) for a in halves], ((0, 0, 1),), 1,
        src_of=lambda r, me, j: [r], dst_of=lambda o, sender, j: [o]))
    return [jnp.concatenate([jnp.where(c == 0, h, o), jnp.where(c == 0, o, h)], axis=0) for h, o in zip(halves, others)]


def gather_all(arrays):
    outs = [jax.ShapeDtypeStruct((8,) + a.shape, a.dtype) for a in arrays]
    return exchange("gather_replicated", Pushes(
        arrays, outs, ALL_MASKS, 1, src_of=lambda r, me, j: [r], dst_of=lambda o, sender, j: [o.at[_devno(sender)]]),
        local_of=lambda r, o, me: (r, o.at[_devno(me)]))


def _unshard_cols(g):
    return jnp.transpose(g, (1, 0, 2)).reshape(g.shape[1], -1)


def _shard_cols(a):
    return jnp.transpose(a.reshape(a.shape[0], N_CHIPS, -1), (1, 0, 2))


class Weights(dict):
    def ride(self, kernel_name):
        return None

    def arrived(self, kernel_name, outs):
        pass


def _forward_backward(x, tgt, W, grads_early=None):
    S = x.shape[0]
    G = {}
    sd = jax.ShapeDtypeStruct

    def ffn(xin, l, j):
        return ffn_fwd(xin, W["ffn_norm"][l][j], W["ffn_w_gate", l, j], W["ffn_w_up", l, j], W["ffn_w_down", l, j], l, j)

    def ffn_back(xin, dout, l, j):
        gn = W["ffn_norm"][l][j]
        dh, G["ffn_w_gate", l, j], G["ffn_w_up", l, j], G["ffn_w_down", l, j] = ffn_bwd(
            xin, gn, W["ffn_w_gate", l, j], W["ffn_w_up", l, j], W["ffn_w_down", l, j], dout, l, j)
        dx, G[("ffn_norm", l, j)] = norm_bwd(f"ffn_norm_bwd_{l}{j}", xin, gn, dh, dout)
        return dx

    x0 = x
    x1 = ffn(x0, 0, 0)
    g0 = W["mix_norm"][0]
    sbq, sbk, sbv = tile_fwd(f_attn_sb, "attn_in_sb", [x1], [g0, W["attn_w_in"][0]], [sd((S, SB_W), F32)] * 3, 256)
    dl_shape = sd((DL_PAIRS, S, 128), F32)
    qn, = tile_fwd(f_attn_qk, "attn_in_q", [x1], [g0, W["attn_w_in"][1], W["attn_q_norm"]], [dl_shape], 256)
    kn, = tile_fwd(f_attn_qk, "attn_in_k", [x1], [g0, W["attn_w_in"][2], W["attn_k_norm"]], [dl_shape], 256)
    vv, = tile_fwd(f_attn_v, "attn_in_v", [x1], [g0, W["attn_w_in"][3]], [dl_shape], 256)
    oa, sb_wts, *rode = sb_fwd(sbq, sbk, sbv, W.ride("sb_fwd"))
    W.arrived("sb_fwd", rode)
    qs, ks, vs = (reorder(nm, t, DIL, False) for nm, t in (("sub_q", qn), ("sub_k", kn), ("sub_v", vv)))
    o_s, lse_s, *rode = dil_fwd(qs, ks, vs, W["bias_mat"], W.ride("dil_fwd"))
    W.arrived("dil_fwd", rode)
    o_n, lse_n = reorder("nat_o", o_s, DIL, True), reorder("nat_lse", lse_s, DIL, True)
    x2, = tile_fwd(f_attn_out, "attn_out", [x1, oa, o_n, lse_n], [W["attn_w_out"]], [sd((S, D), F32)], 256)
    x3 = ffn(x2, 0, 1)
    x4 = ffn(x3, 1, 0)
    g1 = W["mix_norm"][1]
    h, hs = norm_shift_fwd(x4, g1)
    mix = W["rw_mix"]
    r, = tile_fwd(f_rw_proj, "rw_proj_r", [h, hs], [mix[0:1], W["rw_wr"]], [sd((S, D), F32)], 256)
    k, = tile_fwd(f_rw_proj, "rw_proj_k", [h, hs], [mix[2:3], W["rw_wk"]], [sd((S, D), F32)], 256)
    v, = tile_fwd(f_rw_proj, "rw_proj_v", [h, hs], [mix[3:4], W["rw_wv"]], [sd((S, D), F32)], 256)
    mix3 = jnp.concatenate([mix[1:2], mix[4:5], mix[5:6]], axis=0)
    mid_w = [mix3, W["rw_w0"], W["rw_a0"], W["rw_kk"], W["rw_ka"], W["rw_w1"], W["rw_w2"], W["rw_a1"], W["rw_a2"],
             W["rw_g1"], W["rw_g2"]]
    hshape = sd((RW_H, S, HEAD), F32)
    mid_tiles = [h, hs, r, k, v]
    rh, lwh, kh, vh, ah, bh, gate = tile_fwd(f_rw_mid, "rw_mid", mid_tiles, mid_w, [hshape] * 6 + [sd((S, D), F32)], 128)
    yh, states = rwkv_fwd(rh, lwh, kh, vh, ah, bh)
    post_w = [W["rw_lnx_g"], W["rw_lnx_b"], W["rw_rk"], W["rw_wo"]]
    post_tiles = [yh, rh, kh, vh, gate, x4]
    x5, = tile_fwd(f_rw_post, "rw_post", post_tiles, post_w, [sd((S, D), F32)], 128)
    x6 = ffn(x5, 1, 1)
    dx6, loss_part = loss_head(x6, tgt)

    dx5 = ffn_back(x5, dx6, 1, 1)
    (dyh, drh, dkh, dvh, dgate, dx4), (d_lng, d_lnb, d_rk, d_wo) = tile_bwd(
        f_rw_post, "rw_post_bwd", post_tiles, post_w, [dx5], 128, [True] * 6, [True] * 4)
    drh2, dlwh, dkh2, dvh2, dah, dbh = rwkv_bwd(rh, lwh, kh, vh, ah, bh, states, dyh)
    mid_cts = [(drh, drh2), dlwh, (dkh, dkh2), (dvh, dvh2), dah, dbh, dgate]
    (dh, dhs, dr, dk, dv), dmid_w = tile_bwd(f_rw_mid, "rw_mid_bwd", mid_tiles, mid_w, mid_cts, 128,
                                             [True] * 5, [True] * len(mid_w))
    dmix = {}
    for nm, ct, row, wname in (("r", dr, 0, "rw_wr"), ("k", dk, 2, "rw_wk"), ("v", dv, 3, "rw_wv")):
        (dh, dhs), (dmix[row], G[wname]) = tile_bwd(
            f_rw_proj, f"rw_proj_{nm}_bwd", [h, hs], [mix[row:row + 1], W[wname]], [ct], 256,
            [True, True], [True, True], acc={0: dh, 1: dhs})
    dx4, G[("mix_norm", 1)] = norm_shift_bwd(x4, g1, dh, dhs, dx4)
    dmix3 = dmid_w[0]
    G["rw_mix"] = jnp.concatenate([dmix[0], dmix3[0:1], dmix[2], dmix[3], dmix3[1:2], dmix3[2:3]], axis=0)
    for nm, gv in zip(("rw_w0", "rw_a0", "rw_kk", "rw_ka", "rw_w1", "rw_w2", "rw_a1", "rw_a2", "rw_g1", "rw_g2"), dmid_w[1:]):
        G[nm] = gv
    G["rw_lnx_g"], G["rw_lnx_b"], G["rw_rk"], G["rw_wo"] = d_lng, d_lnb, d_rk, d_wo
    dx3 = ffn_back(x3, dx4, 1, 0)
    dx2 = ffn_back(x2, dx3, 0, 1)
    (dx1, doa, do_n, dlse_n), (G["attn_w_out"],) = tile_bwd(
        f_attn_out, "attn_out_bwd", [x1, oa, o_n, lse_n], [W["attn_w_out"]], [dx2], 256, [True] * 4, [True])
    do_s, dlse_s = reorder("sub_do", do_n, DIL, False), reorder("sub_dlse", dlse_n, DIL, False)
    ride, swapped = grads_early(G) if grads_early is not None else (None, None)
    dqs, dks, dvs, dsum, *rode = dil_bwd(qs, ks, vs, W["bias_mat"], o_s, lse_s, do_s, dlse_s, ride)
    ride, landed = swapped(rode) if swapped is not None else (None, None)
    G["rel_bias"] = bias_grad(dsum, W["buckets"])
    dqn, dkn, dvv = (reorder(nm, t, DIL, True) for nm, t in (("nat_dq", dqs), ("nat_dk", dks), ("nat_dv", dvs)))
    dsbq, dsbk, dsbv, *rode = sb_bwd(sbq, sbk, sbv, doa, sb_wts, ride)
    if landed is not None:
        landed(rode)
    dg0 = []
    dwin = []
    (dx1,), (dg, dw) = tile_bwd(f_attn_sb, "attn_in_sb_bwd", [x1], [g0, W["attn_w_in"][0]], [dsbq, dsbk, dsbv], 256,
                                [True], [True, True], acc={0: dx1})
    dg0.append(dg), dwin.append(dw)
    (dx1,), (dg, dw, G["attn_q_norm"]) = tile_bwd(f_attn_qk, "attn_in_q_bwd", [x1], [g0, W["attn_w_in"][1], W["attn_q_norm"]],
                                                  [dqn], 256, [True], [True] * 3, acc={0: dx1})
    dg0.append(dg), dwin.append(dw)
    (dx1,), (dg, dw, G["attn_k_norm"]) = tile_bwd(f_attn_qk, "attn_in_k_bwd", [x1], [g0, W["attn_w_in"][2], W["attn_k_norm"]],
                                                  [dkn], 256, [True], [True] * 3, acc={0: dx1})
    dg0.append(dg), dwin.append(dw)
    (dx1,), (dg, dw) = tile_bwd(f_attn_v, "attn_in_v_bwd", [x1], [g0, W["attn_w_in"][3]], [dvv], 256,
                                [True], [True, True], acc={0: dx1})
    dg0.append(dg), dwin.append(dw)
    G[("mix_norm", 0)] = dg0
    G["attn_w_in"] = dwin
    dx0 = ffn_back(x0, dx1, 0, 0)
    return loss_part, dx0, G


VEC_ROWS = ("ffn_norm", "rw_mix", "rw_w0", "rw_a0", "rw_kk", "rw_ka", "rw_lnx_g", "rw_lnx_b")


def kernel(x, ffn_norm, ffn_w_gate, ffn_w_up, ffn_w_down, mix_norm, rel_bias, attn_w_in, attn_q_norm, attn_k_norm, attn_w_out, rw_mix, rw_w0, rw_w1, rw_w2, rw_a0, rw_a1, rw_a2, rw_g1, rw_g2, rw_kk, rw_ka, rw_rk, rw_wr, rw_wk, rw_wv, rw_wo, rw_lnx_g, rw_lnx_b, loss_target, m_ffn_norm, m_ffn_w_gate, m_ffn_w_up, m_ffn_w_down, m_mix_norm, m_rel_bias, m_attn_w_in, m_attn_q_norm, m_attn_k_norm, m_attn_w_out, m_rw_mix, m_rw_w0, m_rw_w1, m_rw_w2, m_rw_a0, m_rw_a1, m_rw_a2, m_rw_g1, m_rw_g2, m_rw_kk, m_rw_ka, m_rw_rk, m_rw_wr, m_rw_wk, m_rw_wv, m_rw_wo, m_rw_lnx_g, m_rw_lnx_b, v_ffn_norm, v_ffn_w_gate, v_ffn_w_up, v_ffn_w_down, v_mix_norm, v_rel_bias, v_attn_w_in, v_attn_q_norm, v_attn_k_norm, v_attn_w_out, v_rw_mix, v_rw_w0, v_rw_w1, v_rw_w2, v_rw_a0, v_rw_a1, v_rw_a2, v_rw_g1, v_rw_g2, v_rw_kk, v_rw_ka, v_rw_rk, v_rw_wr, v_rw_wk, v_rw_wv, v_rw_wo, v_rw_lnx_g, v_rw_lnx_b):
    names = ["ffn_norm", "ffn_w_gate", "ffn_w_up", "ffn_w_down", "mix_norm", "rel_bias", "attn_w_in", "attn_q_norm",
             "attn_k_norm", "attn_w_out", "rw_mix", "rw_w0", "rw_w1", "rw_w2", "rw_a0", "rw_a1", "rw_a2", "rw_g1", "rw_g2",
             "rw_kk", "rw_ka", "rw_rk", "rw_wr", "rw_wk", "rw_wv", "rw_wo", "rw_lnx_g", "rw_lnx_b"]
    loc = locals()
    w = {n: loc[n] for n in names}
    mom = {n: loc["m_" + n] for n in names}
    vel = {n: loc["v_" + n] for n in names}
    S = x.shape[1]

    ffn3 = ("ffn_w_gate", "ffn_w_up", "ffn_w_down")
    rw_mats = ("rw_w1", "rw_w2", "rw_a1", "rw_a2", "rw_g1", "rw_g2", "rw_wr", "rw_wk", "rw_wv", "rw_wo")
    cols_split = ("attn_w_out", "rw_w2", "rw_a2", "rw_g2")
    shard = {"vec": jnp.concatenate([w[n].reshape(-1, 256) for n in VEC_ROWS], axis=0)}
    for n in ffn3:
        for l in range(2):
            for j in range(2):
                shard[n, l, j] = w[n][l, j].astype(BF16)
    for n in ("attn_w_in", "attn_w_out") + rw_mats:
        shard[n] = w[n].reshape(-1, w[n].shape[-1]).astype(BF16)
    ffn_keys = lambda l, j: [(n, l, j) for n in ffn3]
    w_groups = {"first": ["vec"] + ffn_keys(0, 0) + ["attn_w_in", "attn_w_out"],
                "sb_fwd": ffn_keys(0, 1) + ffn_keys(1, 0) + list(rw_mats),
                "dil_fwd": ffn_keys(1, 1)}
    label = lambda key: key if isinstance(key, str) else f"{key[0]}_{key[1]}{key[2]}"

    class Streamed(Weights):
        def ride(self, kernel_name):
            return gather_pushes([shard[k] for k in w_groups[kernel_name]])

        def arrived(self, kernel_name, outs):
            for key, g in zip(w_groups[kernel_name], gather_swap(f"gather_swap_{kernel_name}", outs)):
                if key == "vec":
                    vec_full = _unshard_cols(g)
                    self["ffn_norm"] = [[vec_full[2 * l + j][None] for j in range(2)] for l in range(2)]
                    self["rw_mix"] = vec_full[4:10]
                    for i, n in enumerate(("rw_w0", "rw_a0", "rw_kk", "rw_ka", "rw_lnx_g", "rw_lnx_b")):
                        self[n] = vec_full[10 + i][None]
                elif key == "attn_w_in":
                    self[key] = [g[p] for p in range(N_CHIPS)]
                elif key in cols_split:
                    self[key] = _unshard_cols(g)
                elif isinstance(key, str):
                    self[key] = g.reshape(D, -1)
                else:
                    self[key] = g

    buckets = _bucket_maps()
    W = Streamed({"mix_norm": [mix_norm[0:1], mix_norm[1:2]], "attn_q_norm": attn_q_norm, "attn_k_norm": attn_k_norm,
                  "rw_rk": rw_rk[0][:, None, :], "buckets": buckets, "bias_mat": bias_table(rel_bias, buckets)})
    W.arrived("first", exchange("gather_weights", W.ride("first")))

    def slots(key, G):
        if key == "vec":
            rows = [G[("ffn_norm", l, j)] for l in range(2) for j in range(2)] + [G["rw_mix"]] + \
                   [G[n] for n in ("rw_w0", "rw_a0", "rw_kk", "rw_ka", "rw_lnx_g", "rw_lnx_b")]
            return _shard_cols(jnp.concatenate(rows, axis=0))
        if key == "attn_w_in":
            return jnp.stack(G[key])
        if key in cols_split:
            return _shard_cols(G[key])
        if isinstance(key, str):
            return G[key].reshape(N_CHIPS, D // N_CHIPS, -1)
        return G[key]

    g_groups = {"early": ffn_keys(1, 1) + ffn_keys(1, 0) + ffn_keys(0, 1) + list(rw_mats) + ["attn_w_out"],
                "late": ["vec", "attn_w_in"] + ffn_keys(0, 0)}
    wire = lambda keys: [F32 if k == "vec" else BF16 for k in keys]
    part = {}

    def grads_early(G):
        keys = g_groups["early"]
        names_ = [label(k) for k in keys]
        split, swap_pushes = reduce_swap([slots(k, G) for k in keys])

        def swapped(theirs):
            chip_sum, pushes = reduce_sum(names_, split, theirs, wire(keys))
            return pushes, lambda landed: part.update(zip(keys, reduce_end("early", names_, chip_sum, landed)))

        return swap_pushes, swapped

    loss_part, dx, G = _forward_backward(x[0], loss_target[0], W, grads_early)
    loss = lax.psum(loss_part[0, 0], ("x", "y", "c"))
    keys = g_groups["late"]
    chip_sum, pushes = reduce_begin("late", [label(k) for k in keys], [slots(k, G) for k in keys], wire(keys))
    part.update(zip(keys, reduce_end("late", [label(k) for k in keys], chip_sum, exchange("scatter_grads", pushes))))
    for n in ffn3:
        part[n] = jnp.stack([jnp.stack([part[n, l, j] for j in range(2)]) for l in range(2)])

    rep = jnp.concatenate([G[("mix_norm", 0)][0] + G[("mix_norm", 0)][1] + G[("mix_norm", 0)][2] + G[("mix_norm", 0)][3],
                           G[("mix_norm", 1)]], axis=0).reshape(16, 128)
    rep = jnp.concatenate([rep, G["rel_bias"], jnp.pad(G["attn_q_norm"], ((0, 0), (0, 64))),
                           jnp.pad(G["attn_k_norm"], ((0, 0), (0, 64))), G["rw_rk"].reshape(8, 128),
                           jnp.zeros((2, 128), F32)], axis=0)
    rep_sum = sum_slots("sum_replicated", gather_all([rep])[0])
    g_rep = {
        "mix_norm": rep_sum[0:16].reshape(2, D),
        "rel_bias": jnp.transpose(rep_sum[16:28, :N_BUCKETS]),
        "attn_q_norm": rep_sum[28:29, :HEAD], "attn_k_norm": rep_sum[29:30, :HEAD],
        "rw_rk": rep_sum[30:38].reshape(1, RW_H, HEAD),
    }

    out = {}

    def adam(n, ga, gb):
        shp = w[n].shape
        to2 = lambda a: a.reshape(-1, shp[-1])
        res = adam_step(f"adam_{n}", to2(ga), None if gb is None else to2(gb), to2(w[n]), to2(mom[n]), to2(vel[n]))
        out[n] = tuple(r.reshape(shp) for r in res)

    for n in ffn3 + ("attn_w_in", "attn_w_out") + rw_mats:
        adam(n, part[n], None)
    rows = {"ffn_norm": (0, 4), "rw_mix": (4, 10), "rw_w0": (10, 11), "rw_a0": (11, 12), "rw_kk": (12, 13),
            "rw_ka": (13, 14), "rw_lnx_g": (14, 15), "rw_lnx_b": (15, 16)}
    for n, (lo, hi) in rows.items():
        adam(n, part["vec"][lo:hi], None)
    for n, gv in g_rep.items():
        adam(n, gv, None)

    grads = [out[n][0] for n in names]
    deltas = [out[n][1] for n in names]
    new_m = [out[n][2] for n in names]
    new_v = [out[n][3] for n in names]
    return (loss, dx[None], *grads, *deltas, *new_m, *new_v)
```

```python
import functools
import math

import jax
import jax.numpy as jnp
from jax import lax
from jax.experimental import pallas as pl
from jax.experimental.pallas import tpu as pltpu

F32, BF16 = jnp.float32, jnp.bfloat16
HI = lax.Precision.HIGHEST
MESH = pl.DeviceIdType.MESH

D = 1024
HEAD = 64
N_CHIPS = 4
FF_SHARD = 704
SB_W = 256
DL_HEADS = 12
DL_PAIRS = 6
DIL = (1, 4, 16)
QBLK = 128
N_BUCKETS = 32
MAX_DISTANCE = 2048
RW_H = 16
RW_CHUNK = 64
NORM_EPS = 1e-6
GN_EPS = 64e-5
NEG_INF = -1e30
VMEM_LIMIT = 56 * 1024 * 1024

ADAM_LR, ADAM_B1, ADAM_B2, ADAM_EPS, ADAM_WD, ADAM_STEP = 0.001, 0.9, 0.999, 1e-08, 0.01, 10


def _cp(sem):
    return pltpu.CompilerParams(dimension_semantics=sem, vmem_limit_bytes=VMEM_LIMIT)


def _dg(a, b, dims, prec=None):
    return lax.dot_general(a, b, (dims, ((), ())), precision=prec, preferred_element_type=F32)


def _bdot(a, b, dims):
    return _dg(a.astype(BF16), b.astype(BF16), dims)


@jax.custom_vjp
def mm(a, b):
    return _bdot(a, b, ((1,), (0,)))


def _mm_fwd(a, b):
    return _bdot(a, b, ((1,), (0,))), (a, b)


def _mm_bwd(res, g):
    a, b = res
    return _bdot(g, b, ((1,), (1,))), _bdot(a, g, ((0,), (0,)))


mm.defvjp(_mm_fwd, _mm_bwd)


def rms(x, g):
    return x * lax.rsqrt(jnp.mean(x * x, axis=-1, keepdims=True) + NORM_EPS) * g


def _pieces(x):
    x1 = x.astype(BF16)
    r1 = x - x1.astype(F32)
    x2 = r1.astype(BF16)
    return jnp.concatenate([x1, x2, (r1 - x2.astype(F32)).astype(BF16)], axis=-1)


def _group_sum(x, nh):
    w = x.shape[-1]
    e = (lax.broadcasted_iota(jnp.int32, (w, nh), 0) // HEAD == lax.broadcasted_iota(jnp.int32, (w, nh), 1)).astype(BF16)
    s = _dg(_pieces(x), jnp.concatenate([e, e, e], axis=0), ((1,), (0,)))
    return _dg(_pieces(s), jnp.concatenate([e, e, e], axis=1), ((1,), (1,)))


@functools.partial(jax.custom_vjp, nondiff_argnums=(1,))
def group_sum(x, nh):
    return _group_sum(x, nh)


group_sum.defvjp(lambda x, nh: (_group_sum(x, nh), None), lambda nh, _, g: (_group_sum(g, nh),))


def softplus(u):
    return jnp.maximum(u, 0.0) + jnp.log1p(jnp.exp(-jnp.abs(u)))


def to_heads(t, nh=RW_H):
    return jnp.stack([t[:, HEAD * h:HEAD * (h + 1)] for h in range(nh)])


def from_heads(t):
    return jnp.concatenate([t[h] for h in range(t.shape[0])], axis=-1)


def _tile_spec(shape, tm):
    if len(shape) == 2:
        return pl.BlockSpec((tm, shape[1]), lambda t: (t, 0))
    return pl.BlockSpec((shape[0], tm, shape[2]), lambda t: (0, t, 0))


def _full_spec(shape):
    nd = len(shape)
    return pl.BlockSpec(tuple(shape), lambda t: (0,) * nd)


def _rows(a):
    return a.shape[0] if a.ndim == 2 else a.shape[1]


def tile_fwd(f, name, tiles, weights, outs, tm):
    nt, nw = len(tiles), len(weights)

    def body(*refs):
        tv = [r[...] for r in refs[:nt]]
        wv = [r[...].astype(F32) for r in refs[nt:nt + nw]]
        res = f(*tv, *wv)
        if not isinstance(res, (tuple, list)):
            res = (res,)
        for o, v in zip(refs[nt + nw:], res):
            o[...] = v.astype(o.dtype)

    return pl.pallas_call(
        body, name=name, grid=(_rows(tiles[0]) // tm,),
        in_specs=[_tile_spec(a.shape, tm) for a in tiles] + [_full_spec(w.shape) for w in weights],
        out_specs=[_tile_spec(o.shape, tm) for o in outs],
        out_shape=list(outs),
        compiler_params=_cp(("parallel",)),
    )(*tiles, *weights)


def tile_bwd(f, name, tiles, weights, cts, tm, dt, dw, acc=None):
    acc = acc or {}
    groups = [c if isinstance(c, tuple) else (c,) for c in cts]
    cts = [a for grp in groups for a in grp]
    nt, nw, nc = len(tiles), len(weights), len(cts)
    acc_idx = sorted(acc)
    na = len(acc_idx)
    dti = [i for i in range(nt) if dt[i]]
    dwi = [i for i in range(nw) if dw[i]]

    def body(*refs):
        tv = [r[...] for r in refs[:nt]]
        wv = [r[...].astype(F32) for r in refs[nt:nt + nw]]
        crefs = list(refs[nt + nw:nt + nw + nc])
        cv = []
        for grp in groups:
            terms = [crefs.pop(0)[...] for _ in grp]
            cv.append(functools.reduce(lambda a, b: a + b, terms))
        av = {i: r[...] for i, r in zip(acc_idx, refs[nt + nw + nc:nt + nw + nc + na])}
        orefs = refs[nt + nw + nc + na:]

        def g(*diff):
            t2, w2 = list(tv), list(wv)
            for i, v in zip(dti, diff[:len(dti)]):
                t2[i] = v
            for i, v in zip(dwi, diff[len(dti):]):
                w2[i] = v
            res = f(*t2, *w2)
            return tuple(res) if isinstance(res, (tuple, list)) else (res,)

        _, vjp = jax.vjp(g, *[tv[i] for i in dti], *[wv[i] for i in dwi])
        grads = vjp(tuple(cv))
        for k, i in enumerate(dti):
            gt = grads[k]
            if i in av:
                gt = gt + av[i]
            orefs[k][...] = gt
        first = pl.program_id(0) == 0
        for k, i in enumerate(dwi):
            o = orefs[len(dti) + k]
            gw = grads[len(dti) + k]

            @pl.when(first)
            def _(o=o, gw=gw):
                o[...] = gw

            @pl.when(jnp.logical_not(first))
            def _(o=o, gw=gw):
                o[...] += gw

    out_shape = [jax.ShapeDtypeStruct(tiles[i].shape, F32) for i in dti] + \
                [jax.ShapeDtypeStruct(weights[i].shape, F32) for i in dwi]
    res = pl.pallas_call(
        body, name=name, grid=(_rows(tiles[0]) // tm,),
        in_specs=[_tile_spec(a.shape, tm) for a in tiles] + [_full_spec(w.shape) for w in weights] +
                 [_tile_spec(c.shape, tm) for c in cts] + [_tile_spec(tiles[i].shape, tm) for i in acc_idx],
        out_specs=[_tile_spec(tiles[i].shape, tm) for i in dti] + [_full_spec(weights[i].shape) for i in dwi],
        out_shape=out_shape,
        compiler_params=_cp(("arbitrary",)),
    )(*tiles, *weights, *cts, *[acc[i] for i in acc_idx])
    return list(res[:len(dti)]), list(res[len(dti):])


def _ffn_wspec(rows, cols, cfirst):
    if cfirst:
        return pl.BlockSpec((1, rows, cols), lambda c, t: (c, 0, 0))
    return pl.BlockSpec((1, rows, cols), lambda t, c: (c, 0, 0))


def ffn_fwd(x, g, wg, wu, wd, l, j, tm=512):
    S = x.shape[0]

    def body(x_ref, g_ref, wg_ref, wu_ref, wd_ref, o_ref, a_ref, b_ref, h_ref, acc_ref):
        c = pl.program_id(1)

        @pl.when(c == 0)
        def _():
            h_ref[...] = rms(x_ref[...], g_ref[...]).astype(BF16)
            acc_ref[...] = jnp.zeros_like(acc_ref)

        h = h_ref[...]
        a = _bdot(h, wg_ref[0], ((1,), (0,)))
        b = _bdot(h, wu_ref[0], ((1,), (0,)))
        a_ref[0] = a.astype(BF16)
        b_ref[0] = b.astype(BF16)
        y = a * jax.nn.sigmoid(a) * b
        acc_ref[...] += _bdot(y, wd_ref[0], ((1,), (0,)))

        @pl.when(c == N_CHIPS - 1)
        def _():
            o_ref[...] = x_ref[...] + 0.5 * acc_ref[...]

    hid = pl.BlockSpec((1, tm, FF_SHARD), lambda t, c: (c, t, 0))
    return pl.pallas_call(
        body, name=f"ffn_fwd_{l}{j}", grid=(S // tm, N_CHIPS),
        in_specs=[pl.BlockSpec((tm, D), lambda t, c: (t, 0)), pl.BlockSpec((1, D), lambda t, c: (0, 0)),
                  _ffn_wspec(D, FF_SHARD, False), _ffn_wspec(D, FF_SHARD, False), _ffn_wspec(FF_SHARD, D, False)],
        out_specs=[pl.BlockSpec((tm, D), lambda t, c: (t, 0)), hid, hid],
        out_shape=[jax.ShapeDtypeStruct((S, D), F32)] + [jax.ShapeDtypeStruct((N_CHIPS, S, FF_SHARD), BF16)] * 2,
        scratch_shapes=[pltpu.VMEM((tm, D), BF16), pltpu.VMEM((tm, D), F32)],
        compiler_params=_cp(("parallel", "arbitrary")),
    )(x, g, wg, wu, wd)


def ffn_bwd(x, g, wg, wu, wd, dout, a_sav, b_sav, l, j, tm=512):
    S = x.shape[0]

    def body(x_ref, g_ref, wg_ref, wu_ref, wd_ref, do_ref, a_ref, b_ref, dh_ref, dwg_ref, dwu_ref, dwd_ref):
        t = pl.program_id(1)
        h = rms(x_ref[...], g_ref[...]).astype(BF16)
        wgv, wuv, wdv = wg_ref[0], wu_ref[0], wd_ref[0]
        a = a_ref[0].astype(F32)
        b = b_ref[0].astype(F32)
        sig = jax.nn.sigmoid(a)
        s = a * sig
        dyd = 0.5 * do_ref[...]
        dy = _bdot(dyd, wdv, ((1,), (1,)))
        dwd = _bdot(s * b, dyd, ((0,), (0,)))
        db = dy * s
        da = dy * b * (sig * (1.0 + a * (1.0 - sig)))
        dwg = _bdot(h, da, ((0,), (0,)))
        dwu = _bdot(h, db, ((0,), (0,)))
        dh_ref[0] = _bdot(da, wgv, ((1,), (1,))) + _bdot(db, wuv, ((1,), (1,)))

        @pl.when(t == 0)
        def _():
            dwg_ref[0] = dwg
            dwu_ref[0] = dwu
            dwd_ref[0] = dwd

        @pl.when(t != 0)
        def _():
            dwg_ref[0] += dwg
            dwu_ref[0] += dwu
            dwd_ref[0] += dwd

    return pl.pallas_call(
        body, name=f"ffn_bwd_{l}{j}", grid=(N_CHIPS, S // tm),
        in_specs=[pl.BlockSpec((tm, D), lambda c, t: (t, 0)), pl.BlockSpec((1, D), lambda c, t: (0, 0)),
                  _ffn_wspec(D, FF_SHARD, True), _ffn_wspec(D, FF_SHARD, True), _ffn_wspec(FF_SHARD, D, True),
                  pl.BlockSpec((tm, D), lambda c, t: (t, 0)),
                  pl.BlockSpec((1, tm, FF_SHARD), lambda c, t: (c, t, 0)), pl.BlockSpec((1, tm, FF_SHARD), lambda c, t: (c, t, 0))],
        out_specs=[pl.BlockSpec((1, tm, D), lambda c, t: (c, t, 0)),
                   _ffn_wspec(D, FF_SHARD, True), _ffn_wspec(D, FF_SHARD, True), _ffn_wspec(FF_SHARD, D, True)],
        out_shape=[jax.ShapeDtypeStruct((N_CHIPS, S, D), F32)] + [jax.ShapeDtypeStruct(a.shape, F32) for a in (wg, wu, wd)],
        compiler_params=_cp(("parallel", "arbitrary")),
    )(x, g, wg, wu, wd, dout, a_sav, b_sav)


def norm_bwd(name, x, g, dh_parts, dres, tm=256):
    S = x.shape[0]
    P = dh_parts.shape[0]

    def body(x_ref, g_ref, dh_ref, dr_ref, dx_ref, dg_ref):
        dh = dh_ref[0]
        for p in range(1, P):
            dh = dh + dh_ref[p]
        _, vjp = jax.vjp(rms, x_ref[...], g_ref[...])
        dx, dg = vjp(dh)
        dx_ref[...] = dr_ref[...] + dx

        @pl.when(pl.program_id(0) == 0)
        def _():
            dg_ref[...] = dg

        @pl.when(pl.program_id(0) != 0)
        def _():
            dg_ref[...] += dg

    return pl.pallas_call(
        body, name=name, grid=(S // tm,),
        in_specs=[pl.BlockSpec((tm, D), lambda t: (t, 0)), pl.BlockSpec((1, D), lambda t: (0, 0)),
                  pl.BlockSpec((P, tm, D), lambda t: (0, t, 0)), pl.BlockSpec((tm, D), lambda t: (t, 0))],
        out_specs=[pl.BlockSpec((tm, D), lambda t: (t, 0)), pl.BlockSpec((1, D), lambda t: (0, 0))],
        out_shape=[jax.ShapeDtypeStruct((S, D), F32), jax.ShapeDtypeStruct((1, D), F32)],
        compiler_params=_cp(("arbitrary",)),
    )(x, g, dh_parts, dres)


def f_attn_sb(x, g, w):
    pr = mm(rms(x, g), w)
    return pr[:, :SB_W], pr[:, SB_W:2 * SB_W], pr[:, 2 * SB_W:]


def _pairs(y):
    return jnp.stack([y[:, 128 * j:128 * (j + 1)] for j in range(DL_PAIRS)])


def f_attn_qk(x, g, w, nrm):
    pr = mm(rms(x, g), w)
    ms = group_sum(pr * pr, DL_HEADS) * (1.0 / HEAD)
    return _pairs(pr * lax.rsqrt(ms + NORM_EPS) * jnp.concatenate([nrm] * DL_HEADS, axis=1))


def f_attn_v(x, g, w):
    return _pairs(mm(rms(x, g), w))


def _masked(strict, x):
    return x if strict is None else jnp.where(strict, x, 0.0)


def _head_stack(x):
    nh = x.shape[1] // HEAD
    lane_head = lax.broadcasted_iota(jnp.int32, (1, x.shape[1]), 1) // HEAD
    return jnp.concatenate([jnp.where(lane_head == h, x, 0.0) for h in range(nh)], axis=0).astype(BF16)


def _head_pick(xs):
    nh = xs.shape[1] // HEAD
    rows = xs.shape[0] // nh
    lane_head = lax.broadcasted_iota(jnp.int32, (1, xs.shape[1]), 1) // HEAD
    out = xs[:rows]
    for h in range(1, nh):
        out = jnp.where(lane_head == h, xs[rows * h:rows * (h + 1)], out)
    return out


def _sb_tiles(qs, kblk, strict):
    z = _dg(qs, kblk, ((1,), (1,))) * (HEAD ** -0.5)
    keep = -(jnp.maximum(z, 0.0) + jnp.log(1.0 + jnp.exp(-jnp.abs(z))))
    return z, _masked(strict, keep)


def _tri(n, upper):
    r = lax.broadcasted_iota(jnp.int32, (n, n), 0)
    c = lax.broadcasted_iota(jnp.int32, (n, n), 1)
    return ((r > c) if upper else (r < c)).astype(BF16)


def _tri_sums(x, tri):
    hi, lo = _split2(x)
    return _dg(jnp.concatenate([hi, lo], axis=1), jnp.concatenate([tri, tri], axis=0), ((1,), (0,)))


SB_UNROLL = 4


def _sb_diag(tb, nh):
    r = lax.broadcasted_iota(jnp.int32, (nh * tb, tb), 0)
    return lax.broadcasted_iota(jnp.int32, (nh * tb, tb), 1) < lax.rem(r, tb)


def _sb_sweep(step, first, count, carry, direction, commit=None):
    def run(kbs, c):
        outs = []
        for kb in kbs:
            c, out = step(kb, c)
            outs.append(out)
        if commit is not None:
            for kb, out in zip(kbs, outs):
                commit(kb, out)
        return c

    rem = count % SB_UNROLL
    carry = lax.fori_loop(0, rem, lambda i, c: run([first + direction * i], c), carry)
    return lax.fori_loop(
        0, count // SB_UNROLL,
        lambda g, c: run([first + direction * (rem + SB_UNROLL * g + u) for u in range(SB_UNROLL)], c), carry)


def _riding(ride, refs, n_in, n_out, first, last):
    if ride is None:
        return refs, lambda: None
    n = ride.n
    own = refs[:n_in] + refs[n_in + n:n_in + n + n_out] + refs[n_in + 2 * n + n_out:len(refs) - 2]
    start, wait = ride.ops(refs[n_in:n_in + n], refs[n_in + n + n_out:n_in + 2 * n + n_out], refs[-2], refs[-1])
    pl.when(first)(start)
    return own, lambda: pl.when(last)(wait)


def _ride_specs(ride):
    if ride is None:
        return [], [], [], [], []
    return [_HBM] * ride.n, [_HBM] * ride.n, ride.out_shapes, ride.sem_shapes(), ride.arrays


def sb_fwd(q, k, v, ride=None, tb=QBLK):
    S = q.shape[0]
    nh = SB_W // HEAD
    nb = S // tb
    r_in, r_out, r_shape, r_scr, r_args = _ride_specs(ride)

    def body(*refs):
        qb = pl.program_id(0)
        (q_ref, k_ref, v_ref, o_ref, w_ref), finish = _riding(ride, refs, 3, 2, qb == 0, qb == nb - 1)
        diag = _sb_diag(tb, nh)
        after_mat = _tri(tb, True)
        qs = _head_stack(q_ref[...])

        def step(kb, carry, strict):
            acc, run = carry
            rows = pl.ds(pl.multiple_of(kb * tb, tb), tb)
            z, keep = _sb_tiles(qs, k_ref[rows, :].astype(BF16), strict)
            w = _masked(strict, jnp.exp(z + keep + _tri_sums(keep, after_mat) + run)).astype(BF16)
            w_ref[0, kb] = w
            acc = acc + _dg(w, v_ref[rows, :].astype(BF16), ((1,), (0,)))
            return acc, run + jnp.sum(keep, axis=1, keepdims=True)

        init = (jnp.zeros((nh * tb, SB_W), F32), jnp.zeros((nh * tb, 1), F32))
        carry = step(qb, init, diag)
        acc, _ = _sb_sweep(lambda kb, c: (step(kb, c, None), None), qb - 1, qb, carry, -1)
        o_ref[...] = _head_pick(acc)
        finish()

    return pl.pallas_call(
        body, name="sb_fwd", grid=(S // tb,),
        in_specs=[pl.BlockSpec((tb, SB_W), lambda i: (i, 0)), pl.BlockSpec((S, SB_W), lambda i: (0, 0)),
                  pl.BlockSpec((S, SB_W), lambda i: (0, 0))] + r_in,
        out_specs=[pl.BlockSpec((tb, SB_W), lambda i: (i, 0)),
                   pl.BlockSpec((1, nb, nh * tb, tb), lambda i: (i, 0, 0, 0))] + r_out,
        out_shape=[jax.ShapeDtypeStruct((S, SB_W), F32), jax.ShapeDtypeStruct((nb, nb, nh * tb, tb), BF16)] + r_shape,
        scratch_shapes=r_scr,
        compiler_params=_cp(("arbitrary",)),
    )(q, k, v, *r_args)


def sb_bwd(q, k, v, do, wts, ride=None, tb=QBLK):
    S = q.shape[0]
    nh = SB_W // HEAD
    nb = S // tb
    scale = HEAD ** -0.5
    r_in, r_out, r_shape, r_scr, r_args = _ride_specs(ride)

    def body(*refs):
        qb = pl.program_id(0)
        (q_ref, k_ref, v_ref, do_ref, w_ref, dq_ref, dk_ref, dv_ref, g_scr), finish = _riding(
            ride, refs, 5, 3, qb == 0, qb == nb - 1)

        @pl.when(qb == 0)
        def _():
            dk_ref[...] = jnp.zeros_like(dk_ref)
            dv_ref[...] = jnp.zeros_like(dv_ref)

        diag = _sb_diag(tb, nh)
        before_mat = _tri(tb, False)
        qs = _head_stack(q_ref[...])
        dos = _head_stack(do_ref[...])

        def weights_pass(kb, carry):
            rows = pl.ds(pl.multiple_of(kb * tb, tb), tb)
            w = w_ref[0, kb]
            g_scr[kb] = _dg(dos, v_ref[rows, :].astype(BF16), ((1,), (1,))) * w.astype(F32)
            return carry, _dg(w, dos, ((0,), (0,)))

        def add_rows(ref):
            def commit(kb, val):
                ref[pl.ds(pl.multiple_of(kb * tb, tb), tb), :] += val
            return commit

        zero_run = jnp.zeros((nh * tb, 1), F32)
        _sb_sweep(weights_pass, 0, qb + 1, 0, 1, add_rows(dv_ref))

        def left_to_right(kb, carry, strict):
            dq, run = carry
            rows = pl.ds(pl.multiple_of(kb * tb, tb), tb)
            kblk = k_ref[rows, :].astype(BF16)
            gw = g_scr[kb]
            sig = jax.nn.sigmoid(_dg(qs, kblk, ((1,), (1,))) * scale)
            dkeep = _masked(strict, _tri_sums(gw, before_mat) + run)
            dz = ((gw * (1.0 - sig) - dkeep * sig) * scale).astype(BF16)
            dq = dq + _dg(dz, kblk, ((1,), (0,)))
            return (dq, run + jnp.sum(gw, axis=1, keepdims=True)), _dg(dz, qs, ((0,), (0,)))

        carry = _sb_sweep(lambda kb, c: left_to_right(kb, c, None), 0, qb,
                          (jnp.zeros((nh * tb, SB_W), F32), zero_run), 1, add_rows(dk_ref))
        (dq, _), dk_diag = left_to_right(qb, carry, diag)
        add_rows(dk_ref)(qb, dk_diag)
        dq_ref[...] = _head_pick(dq)
        finish()

    whole = pl.BlockSpec((S, SB_W), lambda i: (0, 0))
    blk = pl.BlockSpec((tb, SB_W), lambda i: (i, 0))
    return pl.pallas_call(
        body, name="sb_bwd", grid=(S // tb,),
        in_specs=[blk, whole, whole, blk, pl.BlockSpec((1, nb, nh * tb, tb), lambda i: (i, 0, 0, 0))] + r_in,
        out_specs=[blk, whole, whole] + r_out,
        out_shape=[jax.ShapeDtypeStruct((S, SB_W), F32)] * 3 + r_shape,
        scratch_shapes=[pltpu.VMEM((S // tb, nh * tb, tb), F32)] + r_scr,
        compiler_params=_cp(("arbitrary",)),
    )(q, k, v, do, wts, *r_args)


def reorder(name, x, groups, inverse):
    P, S, _ = x.shape

    def body(x_ref, o_ref):
        p = pl.program_id(0)
        for gi, r in enumerate(groups):
            @pl.when(p // 2 == gi)
            def _(r=r):
                L = S // r
                if r == 1:
                    o_ref[...] = x_ref[...]
                for c in range(r if r > 1 else 0):
                    if inverse:
                        o_ref[pl.ds(c, L, stride=r), :] = x_ref[c * L:(c + 1) * L, :]
                    else:
                        o_ref[c * L:(c + 1) * L, :] = x_ref[pl.ds(c, L, stride=r), :]

    slab = pl.BlockSpec((None, S, 128), lambda p: (p, 0, 0))
    return pl.pallas_call(
        body, name=name, grid=(P,), in_specs=[slab], out_specs=slab,
        out_shape=jax.ShapeDtypeStruct(x.shape, x.dtype), compiler_params=_cp(("parallel",)),
    )(x)


def _dil_blocks(S):
    return S // QBLK


def _dil_mask(n_in_stream):
    qi = lax.broadcasted_iota(jnp.int32, (QBLK, 2 * QBLK), 0)
    kj = lax.broadcasted_iota(jnp.int32, (QBLK, 2 * QBLK), 1) - QBLK
    dist = qi - kj
    return (dist >= 0) & (dist <= QBLK) & ((n_in_stream > 0) | (kj >= 0))


def _stream_pos(gi, i, S):
    nb = jnp.where(gi == 0, S // (QBLK * DIL[0]), jnp.where(gi == 1, S // (QBLK * DIL[1]), S // (QBLK * DIL[2])))
    return i % nb


def dil_fwd(q, k, v, bias, ride=None):
    S = q.shape[1]
    nblk = _dil_blocks(S)
    r_in, r_out, r_shape, r_scr, r_args = _ride_specs(ride)

    def body(*refs):
        gi, i = pl.program_id(0), pl.program_id(1)
        (q_ref, kc_ref, kp_ref, vc_ref, vp_ref, b_ref, o_ref, l_ref), finish = _riding(
            ride, refs, 6, 2, (gi == 0) & (i == 0), (gi == len(DIL) - 1) & (i == nblk - 1))
        mask = _dil_mask(_stream_pos(gi, i, S))
        for j in range(2):
            q2, kc, kp, vc, vp = q_ref[j], kc_ref[j], kp_ref[j], vc_ref[j], vp_ref[j]
            os_, ls_ = [], []
            for hh in range(2):
                sl = slice(HEAD * hh, HEAD * (hh + 1))
                kw = jnp.concatenate([kp[:, sl], kc[:, sl]], axis=0)
                vw = jnp.concatenate([vp[:, sl], vc[:, sl]], axis=0)
                lg = _bdot(q2[:, sl], kw, ((1,), (1,))) * (HEAD ** -0.5) + b_ref[2 * j + hh]
                lg = jnp.where(mask, lg, NEG_INF)
                m = jnp.max(lg, axis=-1, keepdims=True)
                p = jnp.exp(lg - m)
                den = jnp.sum(p, axis=-1, keepdims=True)
                os_.append(_bdot(p / den, vw, ((1,), (0,))))
                ls_.append(jnp.broadcast_to(m + jnp.log(den), (QBLK, HEAD)))
            o_ref[j] = jnp.concatenate(os_, axis=1)
            l_ref[j] = jnp.concatenate(ls_, axis=1)
        finish()

    cur = pl.BlockSpec((2, QBLK, 128), lambda g, i: (g, i, 0))
    prev = pl.BlockSpec((2, QBLK, 128), lambda g, i: (g, jnp.maximum(i - 1, 0), 0))
    return pl.pallas_call(
        body, name="dil_fwd", grid=(len(DIL), nblk),
        in_specs=[cur, cur, prev, cur, prev, pl.BlockSpec((4, QBLK, 2 * QBLK), lambda g, i: (g, 0, 0))] + r_in,
        out_specs=[cur, cur] + r_out,
        out_shape=[jax.ShapeDtypeStruct(q.shape, F32)] * 2 + r_shape,
        scratch_shapes=r_scr,
        compiler_params=_cp(("arbitrary", "arbitrary")),
    )(q, k, k, v, v, bias, *r_args)


def dil_bwd(q, k, v, bias, o, lse, do, dlse, ride=None):
    S = q.shape[1]
    nblk = _dil_blocks(S)
    r_in, r_out, r_shape, r_scr, r_args = _ride_specs(ride)

    def body(*refs):
        gi, i = pl.program_id(0), pl.program_id(1)
        (q_ref, kc_ref, kp_ref, vc_ref, vp_ref, b_ref, o_ref, l_ref, do_ref, dl_ref,
         dq_ref, dk_ref, dv_ref, ds_ref, dk_car, dv_car), finish = _riding(
            ride, refs, 10, 4, (gi == 0) & (i == 0), (gi == len(DIL) - 1) & (i == nblk))

        @pl.when(i == 0)
        def _():
            ds_ref[...] = jnp.zeros_like(ds_ref)
            dk_car[...] = jnp.zeros_like(dk_car)
            dv_car[...] = jnp.zeros_like(dv_car)

        @pl.when(i < nblk)
        def _():
            mask = _dil_mask(_stream_pos(gi, i, S))
            for j in range(2):
                q2, kc, kp, vc, vp = q_ref[j], kc_ref[j], kp_ref[j], vc_ref[j], vp_ref[j]
                o2, l2, do2, dl2 = o_ref[j], l_ref[j], do_ref[j], dl_ref[j]
                dqs, dkps, dkcs, dvps, dvcs = [], [], [], [], []
                for hh in range(2):
                    sl = slice(HEAD * hh, HEAD * (hh + 1))
                    qh, doh = q2[:, sl], do2[:, sl]
                    kw = jnp.concatenate([kp[:, sl], kc[:, sl]], axis=0)
                    vw = jnp.concatenate([vp[:, sl], vc[:, sl]], axis=0)
                    lg = _bdot(qh, kw, ((1,), (1,))) * (HEAD ** -0.5) + b_ref[2 * j + hh]
                    p = jnp.where(mask, jnp.exp(lg - l2[:, HEAD * hh:HEAD * hh + 1]), 0.0)
                    dp = _bdot(doh, vw, ((1,), (1,)))
                    delta = jnp.sum(doh * o2[:, sl], axis=-1, keepdims=True)
                    dl = jnp.sum(dl2[:, sl], axis=-1, keepdims=True)
                    ds = p * (dp - delta + dl)
                    ds_ref[2 * j + hh] += ds
                    dsq = ds * (HEAD ** -0.5)
                    dqs.append(_bdot(dsq, kw, ((1,), (0,))))
                    dkw = _bdot(dsq, qh, ((0,), (0,)))
                    dvw = _bdot(p, doh, ((0,), (0,)))
                    dkps.append(dkw[:QBLK])
                    dkcs.append(dkw[QBLK:])
                    dvps.append(dvw[:QBLK])
                    dvcs.append(dvw[QBLK:])
                dq_ref[j] = jnp.concatenate(dqs, axis=1)
                dk_ref[j] = dk_car[j] + jnp.concatenate(dkps, axis=1)
                dv_ref[j] = dv_car[j] + jnp.concatenate(dvps, axis=1)
                dk_car[j] = jnp.concatenate(dkcs, axis=1)
                dv_car[j] = jnp.concatenate(dvcs, axis=1)

        @pl.when(i == nblk)
        def _():
            dk_ref[...] = dk_car[...]
            dv_ref[...] = dv_car[...]

        finish()

    cur = pl.BlockSpec((2, QBLK, 128), lambda g, i: (g, jnp.minimum(i, nblk - 1), 0))
    prev = pl.BlockSpec((2, QBLK, 128), lambda g, i: (g, jnp.clip(i - 1, 0, nblk - 1), 0))
    bspec = pl.BlockSpec((4, QBLK, 2 * QBLK), lambda g, i: (g, 0, 0))
    return pl.pallas_call(
        body, name="dil_bwd", grid=(len(DIL), nblk + 1),
        in_specs=[cur, cur, prev, cur, prev, bspec, cur, cur, cur, cur] + r_in,
        out_specs=[cur, prev, prev, bspec] + r_out,
        out_shape=[jax.ShapeDtypeStruct(q.shape, F32)] * 3 + [jax.ShapeDtypeStruct(bias.shape, F32)] + r_shape,
        scratch_shapes=[pltpu.VMEM((2, QBLK, 128), F32), pltpu.VMEM((2, QBLK, 128), F32)] + r_scr,
        compiler_params=_cp(("arbitrary", "arbitrary")),
    )(q, k, k, v, v, bias, o, lse, do, dlse, *r_args)


def _t5_bucket(dist):
    max_exact = N_BUCKETS // 2
    d = jnp.maximum(dist, 1).astype(F32)
    large = max_exact + (jnp.log(d / max_exact) / math.log(MAX_DISTANCE / max_exact)
                         * (N_BUCKETS - max_exact)).astype(jnp.int32)
    large = jnp.minimum(large, N_BUCKETS - 1)
    return jnp.where(dist < max_exact, dist, large)


def _bucket_maps():
    qi = jnp.arange(QBLK)[:, None]
    kj = jnp.arange(2 * QBLK)[None, :] - QBLK
    dist = jnp.maximum(qi - kj, 0)
    return jnp.stack([_t5_bucket(dist * r) for r in DIL])


def bias_table(rel_bias, buckets):
    def body(tbl_ref, bk_ref, o_ref):
        for h in range(DL_HEADS):
            bk = bk_ref[h // 4]

            def step(b, acc):
                return jnp.where(bk == b, tbl_ref[b, h], acc)

            o_ref[h] = lax.fori_loop(0, N_BUCKETS, step, jnp.zeros(bk.shape, F32))

    return pl.pallas_call(
        body, name="bias_table", out_shape=jax.ShapeDtypeStruct((DL_HEADS,) + buckets.shape[1:], F32),
        in_specs=[pl.BlockSpec(memory_space=pltpu.SMEM), pl.BlockSpec(memory_space=pltpu.VMEM)],
        out_specs=pl.BlockSpec(memory_space=pltpu.VMEM),
    )(rel_bias, buckets)


def bias_grad(ds, buckets):
    def body(ds_ref, bk_ref, o_ref):
        lane = lax.broadcasted_iota(jnp.int32, (1, 128), 1)
        for h in range(DL_HEADS):
            dsv = ds_ref[h]
            bk = bk_ref[h // 4]

            def step(b, row):
                return jnp.where(lane == b, jnp.sum(jnp.where(bk == b, dsv, 0.0)), row)

            o_ref[h:h + 1, :] = lax.fori_loop(0, N_BUCKETS, step, jnp.zeros((1, 128), F32))

    return pl.pallas_call(
        body, name="bias_grad", out_shape=jax.ShapeDtypeStruct((DL_HEADS, 128), F32),
        in_specs=[pl.BlockSpec(memory_space=pltpu.VMEM)] * 2, out_specs=pl.BlockSpec(memory_space=pltpu.VMEM),
    )(ds, buckets)


def f_attn_out(x, oa, o, lse, w):
    og = [jnp.concatenate([o[2 * g], o[2 * g + 1]], axis=1) for g in range(3)]
    lg = [jnp.concatenate([lse[2 * g], lse[2 * g + 1]], axis=1) for g in range(3)]
    m = jnp.maximum(jnp.maximum(lg[0], lg[1]), lg[2])
    e = [jnp.exp(l - m) for l in lg]
    den = e[0] + e[1] + e[2]
    ob = (e[0] * og[0] + e[1] * og[1] + e[2] * og[2]) / den
    return x + mm(jnp.concatenate([oa, ob], axis=1), w)


def norm_shift_fwd(x, g, tm=256):
    S = x.shape[0]

    def body(x_ref, xp_ref, g_ref, h_ref, hs_ref):
        h = rms(x_ref[...], g_ref[...])
        hp = rms(xp_ref[7:8, :], g_ref[...])
        hp = jnp.where(pl.program_id(0) == 0, 0.0, hp)
        row = lax.broadcasted_iota(jnp.int32, (tm, D), 0)
        h_ref[...] = h
        hs_ref[...] = jnp.where(row == 0, hp, pltpu.roll(h, 1, 0))

    return pl.pallas_call(
        body, name="rw_norm_shift", grid=(S // tm,),
        in_specs=[pl.BlockSpec((tm, D), lambda t: (t, 0)),
                  pl.BlockSpec((8, D), lambda t: (jnp.maximum(t * (tm // 8) - 1, 0), 0)),
                  pl.BlockSpec((1, D), lambda t: (0, 0))],
        out_specs=[pl.BlockSpec((tm, D), lambda t: (t, 0))] * 2,
        out_shape=[jax.ShapeDtypeStruct((S, D), F32)] * 2,
        compiler_params=_cp(("parallel",)),
    )(x, x, g)


def norm_shift_bwd(x, g, dh, dhs, dres, tm=256):
    S = x.shape[0]
    nt = S // tm

    def body(x_ref, g_ref, dh_ref, dhs_ref, dhn_ref, dr_ref, dx_ref, dg_ref):
        t = pl.program_id(0)
        nxt = jnp.where(t == nt - 1, 0.0, dhn_ref[0:1, :])
        row = lax.broadcasted_iota(jnp.int32, (tm, D), 0)
        tot = dh_ref[...] + jnp.where(row == tm - 1, nxt, pltpu.roll(dhs_ref[...], tm - 1, 0))
        _, vjp = jax.vjp(rms, x_ref[...], g_ref[...])
        dx, dg = vjp(tot)
        dx_ref[...] = dr_ref[...] + dx

        @pl.when(t == 0)
        def _():
            dg_ref[...] = dg

        @pl.when(t != 0)
        def _():
            dg_ref[...] += dg

    tile = pl.BlockSpec((tm, D), lambda t: (t, 0))
    return pl.pallas_call(
        body, name="rw_norm_shift_bwd", grid=(nt,),
        in_specs=[tile, pl.BlockSpec((1, D), lambda t: (0, 0)), tile, tile,
                  pl.BlockSpec((8, D), lambda t: (jnp.minimum((t + 1) * (tm // 8), S // 8 - 1), 0)), tile],
        out_specs=[tile, pl.BlockSpec((1, D), lambda t: (0, 0))],
        out_shape=[jax.ShapeDtypeStruct((S, D), F32), jax.ShapeDtypeStruct((1, D), F32)],
        compiler_params=_cp(("arbitrary",)),
    )(x, g, dh, dhs, dhs, dres)


def f_rw_proj(h, hs, mix, w):
    return mm(h + (hs - h) * mix, w)


def f_rw_mid(h, hs, r, k, v, mix3, w0, a0, kkw, kaw, w1, w2, a1, a2, g1, g2):
    xx = hs - h
    xw, xa, xg = h + xx * mix3[0:1], h + xx * mix3[1:2], h + xx * mix3[2:3]
    w_log = -softplus(-(w0 + mm(jnp.tanh(mm(xw, w1)), w2))) - 0.5
    lw = -jnp.exp(w_log)
    ag = jax.nn.sigmoid(a0 + mm(mm(xa, a1), a2))
    gate = mm(jax.nn.sigmoid(mm(xg, g1)), g2)
    kk = k * kkw
    kk = kk / jnp.maximum(jnp.sqrt(group_sum(kk * kk, RW_H)), 1e-12)
    kmod = k * (1.0 + (ag - 1.0) * kaw)
    return (to_heads(r), to_heads(lw), to_heads(kmod), to_heads(v), to_heads(-kk), to_heads(kk * ag), gate)


def f_rw_post(yh, rh, kh, vh, gate, x, lng, lnb, rk, wo):
    mu = jnp.mean(yh, axis=-1, keepdims=True)
    var = jnp.mean(jnp.square(yh - mu), axis=-1, keepdims=True)
    yn = (yh - mu) * lax.rsqrt(var + GN_EPS)
    bonus = jnp.sum(rh * kh * rk, axis=-1, keepdims=True) * vh
    y = from_heads(yn) * lng + lnb + from_heads(bonus)
    return x + mm(y * gate, wo)


def _split2(x):
    hi = x.astype(BF16)
    return hi, (x - hi.astype(F32)).astype(BF16)


def _b3(x, y, cx, cy):
    xh, xl = _split2(x)
    yh, yl = _split2(y)
    x3 = jnp.concatenate([xh, xh, xl], axis=cx)
    y3 = jnp.concatenate([yh, yl, yh], axis=cy)
    return lax.dot_general(x3, y3, (((cx,), (cy,)), ((0,), (0,))), preferred_element_type=F32)


@jax.custom_vjp
def b_nt(x, y):
    return _b3(x, y, 2, 2)


@jax.custom_vjp
def b_nn(x, y):
    return _b3(x, y, 2, 1)


@jax.custom_vjp
def b_tn(x, y):
    return _b3(x, y, 1, 1)


def _b1(x, y, cx, cy):
    return lax.dot_general(x.astype(BF16), y.astype(BF16), (((cx,), (cy,)), ((0,), (0,))), preferred_element_type=F32)


b_nt.defvjp(lambda x, y: (b_nt(x, y), (x, y)), lambda r, g: (_b1(g, r[1], 2, 1), _b1(g, r[0], 1, 1)))
b_nn.defvjp(lambda x, y: (b_nn(x, y), (x, y)), lambda r, g: (_b1(g, r[1], 2, 2), _b1(r[0], g, 1, 1)))
b_tn.defvjp(lambda x, y: (b_tn(x, y), (x, y)), lambda r, g: (_b1(r[1], g, 2, 2), _b1(r[0], g, 2, 1)))


def _tri_apply(x, lower):
    H, C, _ = x.shape
    ii = lax.broadcasted_iota(jnp.int32, (C, C), 0)
    jj = lax.broadcasted_iota(jnp.int32, (C, C), 1)
    m = jnp.broadcast_to(((jj <= ii) if lower else (jj >= ii)).astype(BF16), (H, C, C))
    x1 = x.astype(BF16)
    r1 = x - x1.astype(F32)
    x2 = r1.astype(BF16)
    x3 = (r1 - x2.astype(F32)).astype(BF16)
    return lax.dot_general(jnp.concatenate([m, m, m], axis=2), jnp.concatenate([x1, x2, x3], axis=1),
                           (((2,), (1,)), ((0,), (0,))), preferred_element_type=F32)


@jax.custom_vjp
def run_sum(x):
    return _tri_apply(x, True)


run_sum.defvjp(lambda x: (run_sum(x), None), lambda _, g: (_tri_apply(g, False),))


def rwkv_chunk(S0, r, lw, k, v, a, b):
    H, C, _ = r.shape
    V = S0.shape[1]
    ii = lax.broadcasted_iota(jnp.int32, (C, C), 0)
    jj = lax.broadcasted_iota(jnp.int32, (C, C), 1)
    strict = jj < ii
    i2 = lax.broadcasted_iota(jnp.int32, (C, 2 * C), 0)
    j2 = lax.broadcasted_iota(jnp.int32, (C, 2 * C), 1)
    incl2 = jnp.where(j2 >= C, j2 - C, j2) <= i2
    g = run_sum(lw)
    ig = jnp.exp(-g)
    ar = jnp.concatenate([a * jnp.exp(g - lw), r * jnp.exp(g)], axis=1)
    bk = jnp.concatenate([b * ig, k * ig], axis=1)
    m = b_nt(ar, bk)
    a_ab = jnp.where(strict, m[:, :C, :C], 0.0)
    a_ak = jnp.where(strict, m[:, :C, C:], 0.0)
    b_r = jnp.where(incl2, m[:, C:, :], 0.0)
    p = b_nt(ar, S0)
    u = p[:, :C] + b_nn(a_ak, v)
    nmat, n = a_ab, 1
    while n < C:
        n *= 2
        if n < C:
            z = b_nn(nmat, jnp.concatenate([u, nmat], axis=2))
            u, nmat = u + z[:, :, :V], z[:, :, V:]
        else:
            u = u + b_nn(nmat, u)
    uv = jnp.concatenate([u, v], axis=1)
    y = p[:, C:] + b_nn(b_r, uv)
    g_end = g[:, C - 1:C, :]
    dec = jnp.exp(g_end - g)
    s_new = S0 * jnp.exp(g_end) + b_tn(uv, jnp.concatenate([b * dec, k * dec], axis=1))
    return y, s_new


def rwkv_fwd(r, lw, k, v, a, b):
    H, S, _ = r.shape
    C = RW_CHUNK

    def body(r_ref, lw_ref, k_ref, v_ref, a_ref, b_ref, y_ref, s_ref, s_scr):
        @pl.when(pl.program_id(0) == 0)
        def _():
            s_scr[...] = jnp.zeros_like(s_scr)

        s0 = s_scr[...]
        s_ref[0] = s0
        y, s1 = rwkv_chunk(s0, r_ref[...], lw_ref[...], k_ref[...], v_ref[...], a_ref[...], b_ref[...])
        y_ref[...] = y
        s_scr[...] = s1

    bs = pl.BlockSpec((H, C, HEAD), lambda c: (0, c, 0))
    return pl.pallas_call(
        body, name="rwkv_fwd", grid=(S // C,), in_specs=[bs] * 6,
        out_specs=[bs, pl.BlockSpec((1, H, HEAD, HEAD), lambda c: (c, 0, 0, 0))],
        out_shape=[jax.ShapeDtypeStruct((H, S, HEAD), F32), jax.ShapeDtypeStruct((S // C, H, HEAD, HEAD), F32)],
        scratch_shapes=[pltpu.VMEM((H, HEAD, HEAD), F32)],
        compiler_params=_cp(("arbitrary",)),
    )(r, lw, k, v, a, b)


def rwkv_bwd(r, lw, k, v, a, b, states, dy):
    H, S, _ = r.shape
    C = RW_CHUNK
    nc = S // C

    def body(r_ref, lw_ref, k_ref, v_ref, a_ref, b_ref, s_ref, dy_ref, dr, dlw, dk, dv, da, db, ds_scr):
        @pl.when(pl.program_id(0) == 0)
        def _():
            ds_scr[...] = jnp.zeros_like(ds_scr)

        _, vjp = jax.vjp(rwkv_chunk, s_ref[0], r_ref[...], lw_ref[...], k_ref[...], v_ref[...], a_ref[...], b_ref[...])
        grads = vjp((dy_ref[...], ds_scr[...]))
        ds_scr[...] = grads[0]
        for o, gv in zip((dr, dlw, dk, dv, da, db), grads[1:]):
            o[...] = gv

    bs = pl.BlockSpec((H, C, HEAD), lambda c: (0, nc - 1 - c, 0))
    return pl.pallas_call(
        body, name="rwkv_bwd", grid=(nc,),
        in_specs=[bs] * 6 + [pl.BlockSpec((1, H, HEAD, HEAD), lambda c: (nc - 1 - c, 0, 0, 0)), bs],
        out_specs=[bs] * 6, out_shape=[jax.ShapeDtypeStruct((H, S, HEAD), F32)] * 6,
        scratch_shapes=[pltpu.VMEM((H, HEAD, HEAD), F32)],
        compiler_params=_cp(("arbitrary",)),
    )(r, lw, k, v, a, b, states, dy)


def loss_head(y, target, tm=512):
    S = y.shape[0]

    def body(y_ref, t_ref, dy_ref, l_ref):
        e = y_ref[...] - t_ref[...]
        dy_ref[...] = e * (1.0 / D)
        part = jnp.broadcast_to(0.5 * jnp.sum(jnp.mean(e * e, axis=-1, keepdims=True)), (1, 128))

        @pl.when(pl.program_id(0) == 0)
        def _():
            l_ref[...] = part

        @pl.when(pl.program_id(0) != 0)
        def _():
            l_ref[...] += part

    tile = pl.BlockSpec((tm, D), lambda t: (t, 0))
    return pl.pallas_call(
        body, name="loss_head", grid=(S // tm,), in_specs=[tile, tile],
        out_specs=[tile, pl.BlockSpec((1, 128), lambda t: (0, 0))],
        out_shape=[jax.ShapeDtypeStruct((S, D), F32), jax.ShapeDtypeStruct((1, 128), F32)],
        compiler_params=_cp(("arbitrary",)),
    )(y, target)


def _row_tile(rows, cols, budget=1 << 19):
    best = None
    for tr in range(8, rows + 1, 8):
        if rows % tr == 0 and tr * cols <= budget:
            best = tr
    return best or rows


def _adam(w, g, m, v):
    m = ADAM_B1 * m + (1.0 - ADAM_B1) * g
    v = ADAM_B2 * v + (1.0 - ADAM_B2) * jnp.square(g)
    m_hat = m / (1.0 - ADAM_B1 ** ADAM_STEP)
    v_hat = v / (1.0 - ADAM_B2 ** ADAM_STEP)
    return -ADAM_LR * (m_hat / (jnp.sqrt(v_hat) + ADAM_EPS) + ADAM_WD * w), m, v


def sum_slots(name, parts, dtype=F32, extras=()):
    n = 0 if parts is None else parts.shape[0]
    R, C = extras[0].shape if parts is None else parts.shape[1:]
    tr = _row_tile(R, C * (n + len(extras)))
    ins = ([] if parts is None else [parts]) + list(extras)

    def body(*refs):
        terms = [] if parts is None else [refs[0][i] for i in range(n)]
        terms += [r[...] for r in refs[len(ins) - len(extras):len(ins)]]
        s = terms[0].astype(F32)
        for t in terms[1:]:
            s = s + t.astype(F32)
        refs[len(ins)][...] = s.astype(dtype)

    tile = pl.BlockSpec((tr, C), lambda t: (t, 0))
    return pl.pallas_call(
        body, name=name, grid=(R // tr,),
        in_specs=([] if parts is None else [pl.BlockSpec((n, tr, C), lambda t: (0, t, 0))]) + [tile] * len(extras),
        out_specs=tile, out_shape=jax.ShapeDtypeStruct((R, C), dtype), compiler_params=_cp(("parallel",)),
    )(*ins)


def sum_own_half(name, split, theirs, c, dtype):
    nq, _, rh, cols = split.shape
    tr = _row_tile(rh, 2 * cols)

    def body(c_ref, a_ref, b_ref, o_ref):
        o_ref[...] = (a_ref[...] + b_ref[...]).astype(dtype)

    tile = pl.BlockSpec((None, tr, cols), lambda q, t, c_ref: (q, t, 0))
    return pl.pallas_call(
        body, name=name,
        grid_spec=pltpu.PrefetchScalarGridSpec(
            num_scalar_prefetch=1, grid=(nq, rh // tr),
            in_specs=[pl.BlockSpec((None, None, tr, cols), lambda q, t, c_ref: (q, c_ref[0], t, 0)), tile],
            out_specs=tile),
        out_shape=jax.ShapeDtypeStruct((nq, rh, cols), dtype), compiler_params=_cp(("parallel", "parallel")),
    )(jnp.reshape(c, (1,)).astype(jnp.int32), split, theirs)


def sum_landed(name, landed, chip_sum, p):
    n, rh, cols = landed.shape
    tr = _row_tile(rh, (n + 1) * cols)

    def body(p_ref, l_ref, own_ref, o_ref):
        s = l_ref[0].astype(F32)
        for i in range(1, n):
            s = s + l_ref[i].astype(F32)
        o_ref[...] = s + own_ref[...].astype(F32)

    return pl.pallas_call(
        body, name=name,
        grid_spec=pltpu.PrefetchScalarGridSpec(
            num_scalar_prefetch=1, grid=(rh // tr,),
            in_specs=[pl.BlockSpec((n, tr, cols), lambda t, p_ref: (0, t, 0)),
                      pl.BlockSpec((None, tr, cols), lambda t, p_ref: (p_ref[0], t, 0))],
            out_specs=pl.BlockSpec((tr, cols), lambda t, p_ref: (t, 0))),
        out_shape=jax.ShapeDtypeStruct((rh, cols), F32), compiler_params=_cp(("parallel",)),
    )(jnp.reshape(p, (1,)).astype(jnp.int32), landed, chip_sum)


def adam_step(name, ga, gb, w, m, v):
    R, C = w.shape
    tr = _row_tile(R, C, 1 << 17)
    ins = [ga] + ([gb] if gb is not None else []) + [w, m, v]

    def body(*refs):
        g = refs[0][...]
        if gb is not None:
            g = g + refs[1][...]
        w_ref, m_ref, v_ref, g_out, d_out, m_out, v_out = refs[len(ins) - 3:]
        d, m2, v2 = _adam(w_ref[...], g, m_ref[...], v_ref[...])
        g_out[...] = g
        d_out[...] = d
        m_out[...] = m2
        v_out[...] = v2

    tile = pl.BlockSpec((tr, C), lambda t: (t, 0))
    return pl.pallas_call(
        body, name=name, grid=(R // tr,), in_specs=[tile] * len(ins), out_specs=[tile] * 4,
        out_shape=[jax.ShapeDtypeStruct((R, C), F32)] * 4, compiler_params=_cp(("parallel",)),
    )(*ins)


def adam_ffn(name, g_pieces, w, m, v):
    _, _, R, C = w.shape
    tr = _row_tile(R, 4 * C, 1 << 17)

    def body(g00, g01, g10, g11, w_ref, m_ref, v_ref, g_out, d_out, m_out, v_out):
        for l, j, g_ref in ((0, 0, g00), (0, 1, g01), (1, 0, g10), (1, 1, g11)):
            g = g_ref[...]
            d, m2, v2 = _adam(w_ref[l, j], g, m_ref[l, j], v_ref[l, j])
            g_out[l, j] = g
            d_out[l, j] = d
            m_out[l, j] = m2
            v_out[l, j] = v2

    piece = pl.BlockSpec((tr, C), lambda t: (t, 0))
    full = pl.BlockSpec((2, 2, tr, C), lambda t: (0, 0, t, 0))
    return pl.pallas_call(
        body, name=name, grid=(R // tr,), in_specs=[piece] * 4 + [full] * 3, out_specs=[full] * 4,
        out_shape=[jax.ShapeDtypeStruct(w.shape, F32)] * 4, compiler_params=_cp(("parallel",)),
    )(*g_pieces, w, m, v)


def _place():
    return lax.axis_index("x"), lax.axis_index("y"), lax.axis_index("c")


def _flip(me, mask):
    return tuple(1 - v if mk else v for v, mk in zip(me, mask))


CHIP_MASKS = ((1, 0, 0), (0, 1, 0), (1, 1, 0))
ALL_MASKS = tuple((a, b, c) for a in (0, 1) for b in (0, 1) for c in (0, 1) if (a, b, c) != (0, 0, 0))


def _chip(dev):
    return 2 * dev[0] + dev[1]


def _devno(dev):
    return 4 * dev[0] + 2 * dev[1] + dev[2]


class Pushes:
    def __init__(self, arrays, out_shapes, masks, copies, src_of, dst_of, alias=False):
        self.arrays, self.out_shapes, self.masks, self.copies = list(arrays), list(out_shapes), masks, copies
        self.src_of, self.dst_of, self.alias = src_of, dst_of, alias
        self.n = len(self.arrays)

    def sem_shapes(self):
        k = self.n * len(self.masks) * self.copies
        return [pltpu.SemaphoreType.DMA((k,)), pltpu.SemaphoreType.DMA((k,))]

    def ops(self, ins, outs, send_sems, recv_sems):
        me = _place()
        sends, lands = [], []
        for i in range(self.n):
            for j, mk in enumerate(self.masks):
                peer = _flip(me, mk)
                srcs, dsts = self.src_of(ins[i], me, j), self.dst_of(outs[i], me, j)
                here = self.dst_of(outs[i], peer, j)
                for q in range(self.copies):
                    sem = (i * len(self.masks) + j) * self.copies + q
                    sends.append(pltpu.make_async_remote_copy(
                        src_ref=srcs[q], dst_ref=dsts[q], send_sem=send_sems.at[sem], recv_sem=recv_sems.at[sem],
                        device_id=peer, device_id_type=MESH))
                    lands.append(pltpu.make_async_remote_copy(
                        src_ref=here[q], dst_ref=here[q], send_sem=send_sems.at[sem], recv_sem=recv_sems.at[sem],
                        device_id=peer, device_id_type=MESH))

        def start():
            for cp in sends:
                cp.start()

        def wait():
            for cp in lands:
                cp.wait_recv()
            for cp in sends:
                cp.wait_send()

        return start, wait


_HBM = pl.BlockSpec(memory_space=pl.ANY)


def exchange(name, p, local_of=None):
    n = p.n

    def body(*refs):
        ins, outs = refs[:n], refs[n:2 * n]
        start, wait = p.ops(ins, outs, refs[2 * n], refs[2 * n + 1])
        locals_ = []
        if local_of is not None:
            for i in range(n):
                src, dst = local_of(ins[i], outs[i], _place())
                locals_.append(pltpu.make_async_copy(src, dst, refs[2 * n + 2].at[i]))
                locals_[-1].start()
        start()
        wait()
        for cp in locals_:
            cp.wait()

    return pl.pallas_call(
        body, name=name, in_specs=[_HBM] * n, out_specs=[_HBM] * n, out_shape=p.out_shapes,
        scratch_shapes=p.sem_shapes() + ([pltpu.SemaphoreType.DMA((n,))] if local_of is not None else []),
        input_output_aliases={i: i for i in range(n)} if p.alias else {},
    )(*p.arrays)


def _half(c, rows):
    return pl.ds(c * (rows // 2), rows // 2)


def gather_pushes(arrays):
    outs = [jax.ShapeDtypeStruct((N_CHIPS,) + a.shape, a.dtype) for a in arrays]
    sib = len(CHIP_MASKS)
    return Pushes(arrays, outs, CHIP_MASKS + ((0, 0, 1),), 1,
                  src_of=lambda r, me, j: [r] if j == sib else [r.at[_half(me[2], r.shape[0])]],
                  dst_of=lambda o, sender, j: [o.at[_chip(sender)]] if j == sib else
                  [o.at[_chip(sender), _half(sender[2], o.shape[1])]])


def gather_swap(name, got):
    outs = [jax.ShapeDtypeStruct(a.shape, a.dtype) for a in got]
    return exchange(name, Pushes(
        got, outs, ((0, 0, 1),), len(CHIP_MASKS),
        src_of=lambda r, me, j: [r.at[_chip(_flip(me, mk)), _half(me[2], r.shape[1])] for mk in CHIP_MASKS],
        dst_of=lambda o, sender, j: [o.at[_chip(_flip(sender, mk)), _half(sender[2], o.shape[1])] for mk in CHIP_MASKS],
        alias=True))


def reduce_swap(arrays):
    split = [a.reshape(N_CHIPS, 2, a.shape[1] // 2, a.shape[2]) for a in arrays]
    half_shapes = [jax.ShapeDtypeStruct((N_CHIPS,) + a.shape[2:], F32) for a in split]
    return split, Pushes(split, half_shapes, ((0, 0, 1),), 1,
                         src_of=lambda r, me, j: [r.at[:, 1 - me[2]]], dst_of=lambda o, sender, j: [o])


def reduce_begin(tag, names, arrays, wire):
    split, pushes = reduce_swap(arrays)
    return reduce_sum(names, split, exchange(f"grad_pre_swap_{tag}", pushes), wire)


def reduce_sum(names, split, theirs, wire):
    c = lax.axis_index("c")
    chip_sum = [sum_own_half(f"sum2_{nm}", a, t, c, dt) for nm, a, t, dt in zip(names, split, theirs, wire)]
    pushes = Pushes(chip_sum, [jax.ShapeDtypeStruct((len(CHIP_MASKS),) + a.shape[1:], a.dtype) for a in chip_sum],
                    CHIP_MASKS, 1,
                    src_of=lambda r, me, j: [r.at[_chip(_flip(me, CHIP_MASKS[j]))]],
                    dst_of=lambda o, sender, j: [o.at[j]])
    return chip_sum, pushes


def reduce_end(tag, names, chip_sum, landed):
    x, y, c = _place()
    halves = [sum_landed(f"sum4_{nm}", p, a, _chip((x, y, c))) for nm, p, a in zip(names, landed, chip_sum)]
    others = exchange(f"grad_final_swap_{tag}", Pushes(
        halves, [jax.ShapeDtypeStruct(a.shape, F32) for a in halves], ((0, 0, 1),), 1,
        src_of=lambda r, me, j: [r], dst_of=lambda o, sender, j: [o]))
    return [jnp.concatenate([jnp.where(c == 0, h, o), jnp.where(c == 0, o, h)], axis=0) for h, o in zip(halves, others)]


def gather_all(arrays):
    outs = [jax.ShapeDtypeStruct((8,) + a.shape, a.dtype) for a in arrays]
    return exchange("gather_replicated", Pushes(
        arrays, outs, ALL_MASKS, 1, src_of=lambda r, me, j: [r], dst_of=lambda o, sender, j: [o.at[_devno(sender)]]),
        local_of=lambda r, o, me: (r, o.at[_devno(me)]))


def _unshard_cols(g):
    return jnp.transpose(g, (1, 0, 2)).reshape(g.shape[1], -1)


def _shard_cols(a):
    return jnp.transpose(a.reshape(a.shape[0], N_CHIPS, -1), (1, 0, 2))


class Weights(dict):
    def ride(self, kernel_name):
        return None

    def arrived(self, kernel_name, outs):
        pass


def _forward_backward(x, tgt, W, grads_early=None):
    S = x.shape[0]
    G = {}
    sd = jax.ShapeDtypeStruct

    hidden = {}

    def ffn(xin, l, j):
        out, *hidden[l, j] = ffn_fwd(xin, W["ffn_norm"][l][j], W["ffn_w_gate", l, j], W["ffn_w_up", l, j],
                                     W["ffn_w_down", l, j], l, j)
        return out

    def ffn_back(xin, dout, l, j):
        gn = W["ffn_norm"][l][j]
        dh, G["ffn_w_gate", l, j], G["ffn_w_up", l, j], G["ffn_w_down", l, j] = ffn_bwd(
            xin, gn, W["ffn_w_gate", l, j], W["ffn_w_up", l, j], W["ffn_w_down", l, j], dout, *hidden[l, j], l, j)
        dx, G[("ffn_norm", l, j)] = norm_bwd(f"ffn_norm_bwd_{l}{j}", xin, gn, dh, dout)
        return dx

    x0 = x
    x1 = ffn(x0, 0, 0)
    g0 = W["mix_norm"][0]
    sbq, sbk, sbv = tile_fwd(f_attn_sb, "attn_in_sb", [x1], [g0, W["attn_w_in"][0]], [sd((S, SB_W), F32)] * 3, 256)
    dl_shape = sd((DL_PAIRS, S, 128), F32)
    qn, = tile_fwd(f_attn_qk, "attn_in_q", [x1], [g0, W["attn_w_in"][1], W["attn_q_norm"]], [dl_shape], 256)
    kn, = tile_fwd(f_attn_qk, "attn_in_k", [x1], [g0, W["attn_w_in"][2], W["attn_k_norm"]], [dl_shape], 256)
    vv, = tile_fwd(f_attn_v, "attn_in_v", [x1], [g0, W["attn_w_in"][3]], [dl_shape], 256)
    oa, sb_wts, *rode = sb_fwd(sbq, sbk, sbv, W.ride("sb_fwd"))
    W.arrived("sb_fwd", rode)
    qs, ks, vs = (reorder(nm, t, DIL, False) for nm, t in (("sub_q", qn), ("sub_k", kn), ("sub_v", vv)))
    o_s, lse_s, *rode = dil_fwd(qs, ks, vs, W["bias_mat"], W.ride("dil_fwd"))
    W.arrived("dil_fwd", rode)
    o_n, lse_n = reorder("nat_o", o_s, DIL, True), reorder("nat_lse", lse_s, DIL, True)
    x2, = tile_fwd(f_attn_out, "attn_out", [x1, oa, o_n, lse_n], [W["attn_w_out"]], [sd((S, D), F32)], 256)
    x3 = ffn(x2, 0, 1)
    x4 = ffn(x3, 1, 0)
    g1 = W["mix_norm"][1]
    h, hs = norm_shift_fwd(x4, g1)
    mix = W["rw_mix"]
    r, = tile_fwd(f_rw_proj, "rw_proj_r", [h, hs], [mix[0:1], W["rw_wr"]], [sd((S, D), F32)], 256)
    k, = tile_fwd(f_rw_proj, "rw_proj_k", [h, hs], [mix[2:3], W["rw_wk"]], [sd((S, D), F32)], 256)
    v, = tile_fwd(f_rw_proj, "rw_proj_v", [h, hs], [mix[3:4], W["rw_wv"]], [sd((S, D), F32)], 256)
    mix3 = jnp.concatenate([mix[1:2], mix[4:5], mix[5:6]], axis=0)
    mid_w = [mix3, W["rw_w0"], W["rw_a0"], W["rw_kk"], W["rw_ka"], W["rw_w1"], W["rw_w2"], W["rw_a1"], W["rw_a2"],
             W["rw_g1"], W["rw_g2"]]
    hshape = sd((RW_H, S, HEAD), F32)
    mid_tiles = [h, hs, r, k, v]
    rh, lwh, kh, vh, ah, bh, gate = tile_fwd(f_rw_mid, "rw_mid", mid_tiles, mid_w, [hshape] * 6 + [sd((S, D), F32)], 128)
    yh, states = rwkv_fwd(rh, lwh, kh, vh, ah, bh)
    post_w = [W["rw_lnx_g"], W["rw_lnx_b"], W["rw_rk"], W["rw_wo"]]
    post_tiles = [yh, rh, kh, vh, gate, x4]
    x5, = tile_fwd(f_rw_post, "rw_post", post_tiles, post_w, [sd((S, D), F32)], 128)
    x6 = ffn(x5, 1, 1)
    dx6, loss_part = loss_head(x6, tgt)

    dx5 = ffn_back(x5, dx6, 1, 1)
    (dyh, drh, dkh, dvh, dgate, dx4), (d_lng, d_lnb, d_rk, d_wo) = tile_bwd(
        f_rw_post, "rw_post_bwd", post_tiles, post_w, [dx5], 128, [True] * 6, [True] * 4)
    drh2, dlwh, dkh2, dvh2, dah, dbh = rwkv_bwd(rh, lwh, kh, vh, ah, bh, states, dyh)
    mid_cts = [(drh, drh2), dlwh, (dkh, dkh2), (dvh, dvh2), dah, dbh, dgate]
    (dh, dhs, dr, dk, dv), dmid_w = tile_bwd(f_rw_mid, "rw_mid_bwd", mid_tiles, mid_w, mid_cts, 128,
                                             [True] * 5, [True] * len(mid_w))
    dmix = {}
    for nm, ct, row, wname in (("r", dr, 0, "rw_wr"), ("k", dk, 2, "rw_wk"), ("v", dv, 3, "rw_wv")):
        (dh, dhs), (dmix[row], G[wname]) = tile_bwd(
            f_rw_proj, f"rw_proj_{nm}_bwd", [h, hs], [mix[row:row + 1], W[wname]], [ct], 256,
            [True, True], [True, True], acc={0: dh, 1: dhs})
    dx4, G[("mix_norm", 1)] = norm_shift_bwd(x4, g1, dh, dhs, dx4)
    dmix3 = dmid_w[0]
    G["rw_mix"] = jnp.concatenate([dmix[0], dmix3[0:1], dmix[2], dmix[3], dmix3[1:2], dmix3[2:3]], axis=0)
    for nm, gv in zip(("rw_w0", "rw_a0", "rw_kk", "rw_ka", "rw_w1", "rw_w2", "rw_a1", "rw_a2", "rw_g1", "rw_g2"), dmid_w[1:]):
        G[nm] = gv
    G["rw_lnx_g"], G["rw_lnx_b"], G["rw_rk"], G["rw_wo"] = d_lng, d_lnb, d_rk, d_wo
    dx3 = ffn_back(x3, dx4, 1, 0)
    dx2 = ffn_back(x2, dx3, 0, 1)
    (dx1, doa, do_n, dlse_n), (G["attn_w_out"],) = tile_bwd(
        f_attn_out, "attn_out_bwd", [x1, oa, o_n, lse_n], [W["attn_w_out"]], [dx2], 256, [True] * 4, [True])
    do_s, dlse_s = reorder("sub_do", do_n, DIL, False), reorder("sub_dlse", dlse_n, DIL, False)
    ride, swapped = grads_early(G) if grads_early is not None else (None, None)
    dqs, dks, dvs, dsum, *rode = dil_bwd(qs, ks, vs, W["bias_mat"], o_s, lse_s, do_s, dlse_s, ride)
    ride, landed = swapped(rode) if swapped is not None else (None, None)
    G["rel_bias"] = bias_grad(dsum, W["buckets"])
    dqn, dkn, dvv = (reorder(nm, t, DIL, True) for nm, t in (("nat_dq", dqs), ("nat_dk", dks), ("nat_dv", dvs)))
    dsbq, dsbk, dsbv, *rode = sb_bwd(sbq, sbk, sbv, doa, sb_wts, ride)
    if landed is not None:
        landed(rode)
    dg0 = []
    dwin = []
    (dx1,), (dg, dw) = tile_bwd(f_attn_sb, "attn_in_sb_bwd", [x1], [g0, W["attn_w_in"][0]], [dsbq, dsbk, dsbv], 256,
                                [True], [True, True], acc={0: dx1})
    dg0.append(dg), dwin.append(dw)
    (dx1,), (dg, dw, G["attn_q_norm"]) = tile_bwd(f_attn_qk, "attn_in_q_bwd", [x1], [g0, W["attn_w_in"][1], W["attn_q_norm"]],
                                                  [dqn], 256, [True], [True] * 3, acc={0: dx1})
    dg0.append(dg), dwin.append(dw)
    (dx1,), (dg, dw, G["attn_k_norm"]) = tile_bwd(f_attn_qk, "attn_in_k_bwd", [x1], [g0, W["attn_w_in"][2], W["attn_k_norm"]],
                                                  [dkn], 256, [True], [True] * 3, acc={0: dx1})
    dg0.append(dg), dwin.append(dw)
    (dx1,), (dg, dw) = tile_bwd(f_attn_v, "attn_in_v_bwd", [x1], [g0, W["attn_w_in"][3]], [dvv], 256,
                                [True], [True, True], acc={0: dx1})
    dg0.append(dg), dwin.append(dw)
    G[("mix_norm", 0)] = dg0
    G["attn_w_in"] = dwin
    dx0 = ffn_back(x0, dx1, 0, 0)
    return loss_part, dx0, G


VEC_ROWS = ("ffn_norm", "rw_mix", "rw_w0", "rw_a0", "rw_kk", "rw_ka", "rw_lnx_g", "rw_lnx_b")


def kernel(x, ffn_norm, ffn_w_gate, ffn_w_up, ffn_w_down, mix_norm, rel_bias, attn_w_in, attn_q_norm, attn_k_norm, attn_w_out, rw_mix, rw_w0, rw_w1, rw_w2, rw_a0, rw_a1, rw_a2, rw_g1, rw_g2, rw_kk, rw_ka, rw_rk, rw_wr, rw_wk, rw_wv, rw_wo, rw_lnx_g, rw_lnx_b, loss_target, m_ffn_norm, m_ffn_w_gate, m_ffn_w_up, m_ffn_w_down, m_mix_norm, m_rel_bias, m_attn_w_in, m_attn_q_norm, m_attn_k_norm, m_attn_w_out, m_rw_mix, m_rw_w0, m_rw_w1, m_rw_w2, m_rw_a0, m_rw_a1, m_rw_a2, m_rw_g1, m_rw_g2, m_rw_kk, m_rw_ka, m_rw_rk, m_rw_wr, m_rw_wk, m_rw_wv, m_rw_wo, m_rw_lnx_g, m_rw_lnx_b, v_ffn_norm, v_ffn_w_gate, v_ffn_w_up, v_ffn_w_down, v_mix_norm, v_rel_bias, v_attn_w_in, v_attn_q_norm, v_attn_k_norm, v_attn_w_out, v_rw_mix, v_rw_w0, v_rw_w1, v_rw_w2, v_rw_a0, v_rw_a1, v_rw_a2, v_rw_g1, v_rw_g2, v_rw_kk, v_rw_ka, v_rw_rk, v_rw_wr, v_rw_wk, v_rw_wv, v_rw_wo, v_rw_lnx_g, v_rw_lnx_b):
    names = ["ffn_norm", "ffn_w_gate", "ffn_w_up", "ffn_w_down", "mix_norm", "rel_bias", "attn_w_in", "attn_q_norm",
             "attn_k_norm", "attn_w_out", "rw_mix", "rw_w0", "rw_w1", "rw_w2", "rw_a0", "rw_a1", "rw_a2", "rw_g1", "rw_g2",
             "rw_kk", "rw_ka", "rw_rk", "rw_wr", "rw_wk", "rw_wv", "rw_wo", "rw_lnx_g", "rw_lnx_b"]
    loc = locals()
    w = {n: loc[n] for n in names}
    mom = {n: loc["m_" + n] for n in names}
    vel = {n: loc["v_" + n] for n in names}
    S = x.shape[1]

    ffn3 = ("ffn_w_gate", "ffn_w_up", "ffn_w_down")
    rw_mats = ("rw_w1", "rw_w2", "rw_a1", "rw_a2", "rw_g1", "rw_g2", "rw_wr", "rw_wk", "rw_wv", "rw_wo")
    cols_split = ("attn_w_out", "rw_w2", "rw_a2", "rw_g2")
    shard = {"vec": jnp.concatenate([w[n].reshape(-1, 256) for n in VEC_ROWS], axis=0)}
    for n in ffn3:
        for l in range(2):
            for j in range(2):
                shard[n, l, j] = w[n][l, j].astype(BF16)
    for n in ("attn_w_in", "attn_w_out") + rw_mats:
        shard[n] = w[n].reshape(-1, w[n].shape[-1]).astype(BF16)
    ffn_keys = lambda l, j: [(n, l, j) for n in ffn3]
    w_groups = {"first": ["vec"] + ffn_keys(0, 0) + ["attn_w_in", "attn_w_out"],
                "sb_fwd": ffn_keys(0, 1) + ffn_keys(1, 0) + list(rw_mats),
                "dil_fwd": ffn_keys(1, 1)}
    label = lambda key: key if isinstance(key, str) else f"{key[0]}_{key[1]}{key[2]}"

    class Streamed(Weights):
        def ride(self, kernel_name):
            return gather_pushes([shard[k] for k in w_groups[kernel_name]])

        def arrived(self, kernel_name, outs):
            for key, g in zip(w_groups[kernel_name], gather_swap(f"gather_swap_{kernel_name}", outs)):
                if key == "vec":
                    vec_full = _unshard_cols(g)
                    self["ffn_norm"] = [[vec_full[2 * l + j][None] for j in range(2)] for l in range(2)]
                    self["rw_mix"] = vec_full[4:10]
                    for i, n in enumerate(("rw_w0", "rw_a0", "rw_kk", "rw_ka", "rw_lnx_g", "rw_lnx_b")):
                        self[n] = vec_full[10 + i][None]
                elif key == "attn_w_in":
                    self[key] = [g[p] for p in range(N_CHIPS)]
                elif key in cols_split:
                    self[key] = _unshard_cols(g)
                elif isinstance(key, str):
                    self[key] = g.reshape(D, -1)
                else:
                    self[key] = g

    buckets = _bucket_maps()
    W = Streamed({"mix_norm": [mix_norm[0:1], mix_norm[1:2]], "attn_q_norm": attn_q_norm, "attn_k_norm": attn_k_norm,
                  "rw_rk": rw_rk[0][:, None, :], "buckets": buckets, "bias_mat": bias_table(rel_bias, buckets)})
    W.arrived("first", exchange("gather_weights", W.ride("first")))

    def slots(key, G):
        if key == "vec":
            rows = [G[("ffn_norm", l, j)] for l in range(2) for j in range(2)] + [G["rw_mix"]] + \
                   [G[n] for n in ("rw_w0", "rw_a0", "rw_kk", "rw_ka", "rw_lnx_g", "rw_lnx_b")]
            return _shard_cols(jnp.concatenate(rows, axis=0))
        if key == "attn_w_in":
            return jnp.stack(G[key])
        if key in cols_split:
            return _shard_cols(G[key])
        if isinstance(key, str):
            return G[key].reshape(N_CHIPS, D // N_CHIPS, -1)
        return G[key]

    g_groups = {"early": ffn_keys(1, 1) + ffn_keys(1, 0) + ffn_keys(0, 1) + list(rw_mats) + ["attn_w_out"],
                "late": ["vec", "attn_w_in"] + ffn_keys(0, 0)}
    wire = lambda keys: [F32 if k == "vec" else BF16 for k in keys]
    part = {}

    def grads_early(G):
        keys = g_groups["early"]
        names_ = [label(k) for k in keys]
        split, swap_pushes = reduce_swap([slots(k, G) for k in keys])

        def swapped(theirs):
            chip_sum, pushes = reduce_sum(names_, split, theirs, wire(keys))
            return pushes, lambda landed: part.update(zip(keys, reduce_end("early", names_, chip_sum, landed)))

        return swap_pushes, swapped

    loss_part, dx, G = _forward_backward(x[0], loss_target[0], W, grads_early)
    loss = lax.psum(loss_part[0, 0], ("x", "y", "c"))
    keys = g_groups["late"]
    chip_sum, pushes = reduce_begin("late", [label(k) for k in keys], [slots(k, G) for k in keys], wire(keys))
    part.update(zip(keys, reduce_end("late", [label(k) for k in keys], chip_sum, exchange("scatter_grads", pushes))))

    rep = jnp.concatenate([G[("mix_norm", 0)][0] + G[("mix_norm", 0)][1] + G[("mix_norm", 0)][2] + G[("mix_norm", 0)][3],
                           G[("mix_norm", 1)]], axis=0).reshape(16, 128)
    rep = jnp.concatenate([rep, G["rel_bias"], jnp.pad(G["attn_q_norm"], ((0, 0), (0, 64))),
                           jnp.pad(G["attn_k_norm"], ((0, 0), (0, 64))), G["rw_rk"].reshape(8, 128),
                           jnp.zeros((2, 128), F32)], axis=0)
    rep_sum = sum_slots("sum_replicated", gather_all([rep])[0])
    g_rep = {
        "mix_norm": rep_sum[0:16].reshape(2, D),
        "rel_bias": jnp.transpose(rep_sum[16:28, :N_BUCKETS]),
        "attn_q_norm": rep_sum[28:29, :HEAD], "attn_k_norm": rep_sum[29:30, :HEAD],
        "rw_rk": rep_sum[30:38].reshape(1, RW_H, HEAD),
    }

    out = {}

    def adam(n, ga, gb):
        shp = w[n].shape
        to2 = lambda a: a.reshape(-1, shp[-1])
        res = adam_step(f"adam_{n}", to2(ga), None if gb is None else to2(gb), to2(w[n]), to2(mom[n]), to2(vel[n]))
        out[n] = tuple(r.reshape(shp) for r in res)

    for n in ffn3:
        out[n] = tuple(adam_ffn(f"adam_{n}", [part[n, l, j] for l in range(2) for j in range(2)], w[n], mom[n], vel[n]))
    for n in ("attn_w_in", "attn_w_out") + rw_mats:
        adam(n, part[n], None)
    rows = {"ffn_norm": (0, 4), "rw_mix": (4, 10), "rw_w0": (10, 11), "rw_a0": (11, 12), "rw_kk": (12, 13),
            "rw_ka": (13, 14), "rw_lnx_g": (14, 15), "rw_lnx_b": (15, 16)}
    for n, (lo, hi) in rows.items():
        adam(n, part["vec"][lo:hi], None)
    for n, gv in g_rep.items():
        adam(n, gv, None)

    grads = [out[n][0] for n in names]
    deltas = [out[n][1] for n in names]
    new_m = [out[n][2] for n in names]
    new_v = [out[n][3] for n in names]
    return (loss, dx[None], *grads, *deltas, *new_m, *new_v)
```

```python
import functools
import math

import jax
import jax.numpy as jnp
from jax import lax
from jax.experimental import pallas as pl
from jax.experimental.pallas import tpu as pltpu

F32, BF16 = jnp.float32, jnp.bfloat16
HI = lax.Precision.HIGHEST
MESH = pl.DeviceIdType.MESH

D = 1024
HEAD = 64
N_CHIPS = 4
FF_SHARD = 704
SB_W = 256
DL_HEADS = 12
DL_PAIRS = 6
DIL = (1, 4, 16)
QBLK = 128
N_BUCKETS = 32
MAX_DISTANCE = 2048
RW_H = 16
RW_CHUNK = 64
NORM_EPS = 1e-6
GN_EPS = 64e-5
NEG_INF = -1e30
VMEM_LIMIT = 56 * 1024 * 1024

ADAM_LR, ADAM_B1, ADAM_B2, ADAM_EPS, ADAM_WD, ADAM_STEP = 0.001, 0.9, 0.999, 1e-08, 0.01, 10


def _cp(sem):
    return pltpu.CompilerParams(dimension_semantics=sem, vmem_limit_bytes=VMEM_LIMIT)


def _dg(a, b, dims, prec=None):
    return lax.dot_general(a, b, (dims, ((), ())), precision=prec, preferred_element_type=F32)


def _bdot(a, b, dims):
    return _dg(a.astype(BF16), b.astype(BF16), dims)


@jax.custom_vjp
def mm(a, b):
    return _bdot(a, b, ((1,), (0,)))


def _mm_fwd(a, b):
    return _bdot(a, b, ((1,), (0,))), (a, b)


def _mm_bwd(res, g):
    a, b = res
    return _bdot(g, b, ((1,), (1,))), _bdot(a, g, ((0,), (0,)))


mm.defvjp(_mm_fwd, _mm_bwd)


def rms(x, g):
    return x * lax.rsqrt(jnp.mean(x * x, axis=-1, keepdims=True) + NORM_EPS) * g


def _pieces(x):
    x1 = x.astype(BF16)
    r1 = x - x1.astype(F32)
    x2 = r1.astype(BF16)
    return jnp.concatenate([x1, x2, (r1 - x2.astype(F32)).astype(BF16)], axis=-1)


def _group_sum(x, nh):
    w = x.shape[-1]
    e = (lax.broadcasted_iota(jnp.int32, (w, nh), 0) // HEAD == lax.broadcasted_iota(jnp.int32, (w, nh), 1)).astype(BF16)
    s = _dg(_pieces(x), jnp.concatenate([e, e, e], axis=0), ((1,), (0,)))
    return _dg(_pieces(s), jnp.concatenate([e, e, e], axis=1), ((1,), (1,)))


@functools.partial(jax.custom_vjp, nondiff_argnums=(1,))
def group_sum(x, nh):
    return _group_sum(x, nh)


group_sum.defvjp(lambda x, nh: (_group_sum(x, nh), None), lambda nh, _, g: (_group_sum(g, nh),))


def softplus(u):
    return jnp.maximum(u, 0.0) + jnp.log1p(jnp.exp(-jnp.abs(u)))


def to_heads(t, nh=RW_H):
    return jnp.stack([t[:, HEAD * h:HEAD * (h + 1)] for h in range(nh)])


def from_heads(t):
    return jnp.concatenate([t[h] for h in range(t.shape[0])], axis=-1)


def _tile_spec(shape, tm):
    if len(shape) == 2:
        return pl.BlockSpec((tm, shape[1]), lambda t: (t, 0))
    return pl.BlockSpec((shape[0], tm, shape[2]), lambda t: (0, t, 0))


def _full_spec(shape):
    nd = len(shape)
    return pl.BlockSpec(tuple(shape), lambda t: (0,) * nd)


def _rows(a):
    return a.shape[0] if a.ndim == 2 else a.shape[1]


def tile_fwd(f, name, tiles, weights, outs, tm):
    nt, nw = len(tiles), len(weights)

    def body(*refs):
        tv = [r[...] for r in refs[:nt]]
        wv = [r[...].astype(F32) for r in refs[nt:nt + nw]]
        res = f(*tv, *wv)
        if not isinstance(res, (tuple, list)):
            res = (res,)
        for o, v in zip(refs[nt + nw:], res):
            o[...] = v.astype(o.dtype)

    return pl.pallas_call(
        body, name=name, grid=(_rows(tiles[0]) // tm,),
        in_specs=[_tile_spec(a.shape, tm) for a in tiles] + [_full_spec(w.shape) for w in weights],
        out_specs=[_tile_spec(o.shape, tm) for o in outs],
        out_shape=list(outs),
        compiler_params=_cp(("parallel",)),
    )(*tiles, *weights)


def tile_bwd(f, name, tiles, weights, cts, tm, dt, dw, acc=None):
    acc = acc or {}
    groups = [c if isinstance(c, tuple) else (c,) for c in cts]
    cts = [a for grp in groups for a in grp]
    nt, nw, nc = len(tiles), len(weights), len(cts)
    acc_idx = sorted(acc)
    na = len(acc_idx)
    dti = [i for i in range(nt) if dt[i]]
    dwi = [i for i in range(nw) if dw[i]]

    def body(*refs):
        tv = [r[...] for r in refs[:nt]]
        wv = [r[...].astype(F32) for r in refs[nt:nt + nw]]
        crefs = list(refs[nt + nw:nt + nw + nc])
        cv = []
        for grp in groups:
            terms = [crefs.pop(0)[...] for _ in grp]
            cv.append(functools.reduce(lambda a, b: a + b, terms))
        av = {i: r[...] for i, r in zip(acc_idx, refs[nt + nw + nc:nt + nw + nc + na])}
        orefs = refs[nt + nw + nc + na:]

        def g(*diff):
            t2, w2 = list(tv), list(wv)
            for i, v in zip(dti, diff[:len(dti)]):
                t2[i] = v
            for i, v in zip(dwi, diff[len(dti):]):
                w2[i] = v
            res = f(*t2, *w2)
            return tuple(res) if isinstance(res, (tuple, list)) else (res,)

        _, vjp = jax.vjp(g, *[tv[i] for i in dti], *[wv[i] for i in dwi])
        grads = vjp(tuple(cv))
        for k, i in enumerate(dti):
            gt = grads[k]
            if i in av:
                gt = gt + av[i]
            orefs[k][...] = gt
        first = pl.program_id(0) == 0
        for k, i in enumerate(dwi):
            o = orefs[len(dti) + k]
            gw = grads[len(dti) + k]

            @pl.when(first)
            def _(o=o, gw=gw):
                o[...] = gw

            @pl.when(jnp.logical_not(first))
            def _(o=o, gw=gw):
                o[...] += gw

    out_shape = [jax.ShapeDtypeStruct(tiles[i].shape, F32) for i in dti] + \
                [jax.ShapeDtypeStruct(weights[i].shape, F32) for i in dwi]
    res = pl.pallas_call(
        body, name=name, grid=(_rows(tiles[0]) // tm,),
        in_specs=[_tile_spec(a.shape, tm) for a in tiles] + [_full_spec(w.shape) for w in weights] +
                 [_tile_spec(c.shape, tm) for c in cts] + [_tile_spec(tiles[i].shape, tm) for i in acc_idx],
        out_specs=[_tile_spec(tiles[i].shape, tm) for i in dti] + [_full_spec(weights[i].shape) for i in dwi],
        out_shape=out_shape,
        compiler_params=_cp(("arbitrary",)),
    )(*tiles, *weights, *cts, *[acc[i] for i in acc_idx])
    return list(res[:len(dti)]), list(res[len(dti):])


def _ffn_wspec(rows, cols, cfirst):
    if cfirst:
        return pl.BlockSpec((1, rows, cols), lambda c, t: (c, 0, 0))
    return pl.BlockSpec((1, rows, cols), lambda t, c: (c, 0, 0))


def ffn_fwd(x, g, wg, wu, wd, l, j, tm=512):
    S = x.shape[0]

    def body(x_ref, g_ref, wg_ref, wu_ref, wd_ref, o_ref, a_ref, b_ref, h_ref, acc_ref):
        c = pl.program_id(1)

        @pl.when(c == 0)
        def _():
            h_ref[...] = rms(x_ref[...], g_ref[...]).astype(BF16)
            acc_ref[...] = jnp.zeros_like(acc_ref)

        h = h_ref[...]
        a = _bdot(h, wg_ref[0], ((1,), (0,)))
        b = _bdot(h, wu_ref[0], ((1,), (0,)))
        a_ref[0] = a.astype(BF16)
        b_ref[0] = b.astype(BF16)
        y = a * jax.nn.sigmoid(a) * b
        acc_ref[...] += _bdot(y, wd_ref[0], ((1,), (0,)))

        @pl.when(c == N_CHIPS - 1)
        def _():
            o_ref[...] = x_ref[...] + 0.5 * acc_ref[...]

    hid = pl.BlockSpec((1, tm, FF_SHARD), lambda t, c: (c, t, 0))
    return pl.pallas_call(
        body, name=f"ffn_fwd_{l}{j}", grid=(S // tm, N_CHIPS),
        in_specs=[pl.BlockSpec((tm, D), lambda t, c: (t, 0)), pl.BlockSpec((1, D), lambda t, c: (0, 0)),
                  _ffn_wspec(D, FF_SHARD, False), _ffn_wspec(D, FF_SHARD, False), _ffn_wspec(FF_SHARD, D, False)],
        out_specs=[pl.BlockSpec((tm, D), lambda t, c: (t, 0)), hid, hid],
        out_shape=[jax.ShapeDtypeStruct((S, D), F32)] + [jax.ShapeDtypeStruct((N_CHIPS, S, FF_SHARD), BF16)] * 2,
        scratch_shapes=[pltpu.VMEM((tm, D), BF16), pltpu.VMEM((tm, D), F32)],
        compiler_params=_cp(("parallel", "arbitrary")),
    )(x, g, wg, wu, wd)


def ffn_bwd(x, g, wg, wu, wd, dout, a_sav, b_sav, l, j, tm=512):
    S = x.shape[0]

    def body(x_ref, g_ref, wg_ref, wu_ref, wd_ref, do_ref, a_ref, b_ref, dh_ref, dwg_ref, dwu_ref, dwd_ref):
        t = pl.program_id(1)
        h = rms(x_ref[...], g_ref[...]).astype(BF16)
        wgv, wuv, wdv = wg_ref[0], wu_ref[0], wd_ref[0]
        a = a_ref[0].astype(F32)
        b = b_ref[0].astype(F32)
        sig = jax.nn.sigmoid(a)
        s = a * sig
        dyd = 0.5 * do_ref[...]
        dy = _bdot(dyd, wdv, ((1,), (1,)))
        dwd = _bdot(s * b, dyd, ((0,), (0,)))
        db = dy * s
        da = dy * b * (sig * (1.0 + a * (1.0 - sig)))
        dwg = _bdot(da, h, ((0,), (0,)))
        dwu = _bdot(db, h, ((0,), (0,)))
        dh_ref[0] = _bdot(da, wgv, ((1,), (1,))) + _bdot(db, wuv, ((1,), (1,)))

        @pl.when(t == 0)
        def _():
            dwg_ref[0] = dwg
            dwu_ref[0] = dwu
            dwd_ref[0] = dwd

        @pl.when(t != 0)
        def _():
            dwg_ref[0] += dwg
            dwu_ref[0] += dwu
            dwd_ref[0] += dwd

    return pl.pallas_call(
        body, name=f"ffn_bwd_{l}{j}", grid=(N_CHIPS, S // tm),
        in_specs=[pl.BlockSpec((tm, D), lambda c, t: (t, 0)), pl.BlockSpec((1, D), lambda c, t: (0, 0)),
                  _ffn_wspec(D, FF_SHARD, True), _ffn_wspec(D, FF_SHARD, True), _ffn_wspec(FF_SHARD, D, True),
                  pl.BlockSpec((tm, D), lambda c, t: (t, 0)),
                  pl.BlockSpec((1, tm, FF_SHARD), lambda c, t: (c, t, 0)), pl.BlockSpec((1, tm, FF_SHARD), lambda c, t: (c, t, 0))],
        out_specs=[pl.BlockSpec((1, tm, D), lambda c, t: (c, t, 0))] + [_ffn_wspec(FF_SHARD, D, True)] * 3,
        out_shape=[jax.ShapeDtypeStruct((N_CHIPS, S, D), F32)] + [jax.ShapeDtypeStruct(wd.shape, F32)] * 3,
        compiler_params=_cp(("parallel", "arbitrary")),
    )(x, g, wg, wu, wd, dout, a_sav, b_sav)


def norm_bwd(name, x, g, dh_parts, dres, tm=256):
    S = x.shape[0]
    P = dh_parts.shape[0]

    def body(x_ref, g_ref, dh_ref, dr_ref, dx_ref, dg_ref):
        dh = dh_ref[0]
        for p in range(1, P):
            dh = dh + dh_ref[p]
        _, vjp = jax.vjp(rms, x_ref[...], g_ref[...])
        dx, dg = vjp(dh)
        dx_ref[...] = dr_ref[...] + dx

        @pl.when(pl.program_id(0) == 0)
        def _():
            dg_ref[...] = dg

        @pl.when(pl.program_id(0) != 0)
        def _():
            dg_ref[...] += dg

    return pl.pallas_call(
        body, name=name, grid=(S // tm,),
        in_specs=[pl.BlockSpec((tm, D), lambda t: (t, 0)), pl.BlockSpec((1, D), lambda t: (0, 0)),
                  pl.BlockSpec((P, tm, D), lambda t: (0, t, 0)), pl.BlockSpec((tm, D), lambda t: (t, 0))],
        out_specs=[pl.BlockSpec((tm, D), lambda t: (t, 0)), pl.BlockSpec((1, D), lambda t: (0, 0))],
        out_shape=[jax.ShapeDtypeStruct((S, D), F32), jax.ShapeDtypeStruct((1, D), F32)],
        compiler_params=_cp(("arbitrary",)),
    )(x, g, dh_parts, dres)


def f_attn_sb(x, g, w):
    pr = mm(rms(x, g), w)
    return pr[:, :SB_W], pr[:, SB_W:2 * SB_W], pr[:, 2 * SB_W:]


def _pairs(y):
    return jnp.stack([y[:, 128 * j:128 * (j + 1)] for j in range(DL_PAIRS)])


def f_attn_qk(x, g, w, nrm):
    pr = mm(rms(x, g), w)
    ms = group_sum(pr * pr, DL_HEADS) * (1.0 / HEAD)
    return _pairs(pr * lax.rsqrt(ms + NORM_EPS) * jnp.concatenate([nrm] * DL_HEADS, axis=1))


def f_attn_v(x, g, w):
    return _pairs(mm(rms(x, g), w))


def _masked(strict, x):
    return x if strict is None else jnp.where(strict, x, 0.0)


def _head_stack(x):
    nh = x.shape[1] // HEAD
    lane_head = lax.broadcasted_iota(jnp.int32, (1, x.shape[1]), 1) // HEAD
    return jnp.concatenate([jnp.where(lane_head == h, x, 0.0) for h in range(nh)], axis=0).astype(BF16)


def _head_pick(xs):
    nh = xs.shape[1] // HEAD
    rows = xs.shape[0] // nh
    lane_head = lax.broadcasted_iota(jnp.int32, (1, xs.shape[1]), 1) // HEAD
    out = xs[:rows]
    for h in range(1, nh):
        out = jnp.where(lane_head == h, xs[rows * h:rows * (h + 1)], out)
    return out


def _sb_tiles(qs, kblk, strict):
    z = _dg(qs, kblk, ((1,), (1,))) * (HEAD ** -0.5)
    keep = -(jnp.maximum(z, 0.0) + jnp.log(1.0 + jnp.exp(-jnp.abs(z))))
    return z, _masked(strict, keep)


def _tri(n, upper):
    r = lax.broadcasted_iota(jnp.int32, (n, n), 0)
    c = lax.broadcasted_iota(jnp.int32, (n, n), 1)
    return ((r > c) if upper else (r < c)).astype(BF16)


def _tri_sums(x, tri):
    hi, lo = _split2(x)
    return _dg(jnp.concatenate([hi, lo], axis=1), jnp.concatenate([tri, tri], axis=0), ((1,), (0,)))


SB_UNROLL = 4


def _sb_diag(tb, nh):
    r = lax.broadcasted_iota(jnp.int32, (nh * tb, tb), 0)
    return lax.broadcasted_iota(jnp.int32, (nh * tb, tb), 1) < lax.rem(r, tb)


def _sb_sweep(step, first, count, carry, direction, commit=None):
    def run(kbs, c):
        outs = []
        for kb in kbs:
            c, out = step(kb, c)
            outs.append(out)
        if commit is not None:
            for kb, out in zip(kbs, outs):
                commit(kb, out)
        return c

    rem = count % SB_UNROLL
    carry = lax.fori_loop(0, rem, lambda i, c: run([first + direction * i], c), carry)
    return lax.fori_loop(
        0, count // SB_UNROLL,
        lambda g, c: run([first + direction * (rem + SB_UNROLL * g + u) for u in range(SB_UNROLL)], c), carry)


def _riding(ride, refs, n_in, n_out, first, last):
    if ride is None:
        return refs, lambda: None
    n = ride.n
    own = refs[:n_in] + refs[n_in + n:n_in + n + n_out] + refs[n_in + 2 * n + n_out:len(refs) - 2]
    start, wait = ride.ops(refs[n_in:n_in + n], refs[n_in + n + n_out:n_in + 2 * n + n_out], refs[-2], refs[-1])
    pl.when(first)(start)
    return own, lambda: pl.when(last)(wait)


def _ride_specs(ride):
    if ride is None:
        return [], [], [], [], []
    return [_HBM] * ride.n, [_HBM] * ride.n, ride.out_shapes, ride.sem_shapes(), ride.arrays


def sb_fwd(q, k, v, ride=None, tb=QBLK):
    S = q.shape[0]
    nh = SB_W // HEAD
    nb = S // tb
    r_in, r_out, r_shape, r_scr, r_args = _ride_specs(ride)

    def body(*refs):
        qb = pl.program_id(0)
        (q_ref, k_ref, v_ref, o_ref, w_ref), finish = _riding(ride, refs, 3, 2, qb == 0, qb == nb - 1)
        diag = _sb_diag(tb, nh)
        after_mat = _tri(tb, True)
        qs = _head_stack(q_ref[...])

        def step(kb, carry, strict):
            acc, run = carry
            rows = pl.ds(pl.multiple_of(kb * tb, tb), tb)
            z, keep = _sb_tiles(qs, k_ref[rows, :].astype(BF16), strict)
            w = _masked(strict, jnp.exp(z + keep + _tri_sums(keep, after_mat) + run)).astype(BF16)
            w_ref[0, kb] = w
            acc = acc + _dg(w, v_ref[rows, :].astype(BF16), ((1,), (0,)))
            return acc, run + jnp.sum(keep, axis=1, keepdims=True)

        init = (jnp.zeros((nh * tb, SB_W), F32), jnp.zeros((nh * tb, 1), F32))
        carry = step(qb, init, diag)
        acc, _ = _sb_sweep(lambda kb, c: (step(kb, c, None), None), qb - 1, qb, carry, -1)
        o_ref[...] = _head_pick(acc)
        finish()

    return pl.pallas_call(
        body, name="sb_fwd", grid=(S // tb,),
        in_specs=[pl.BlockSpec((tb, SB_W), lambda i: (i, 0)), pl.BlockSpec((S, SB_W), lambda i: (0, 0)),
                  pl.BlockSpec((S, SB_W), lambda i: (0, 0))] + r_in,
        out_specs=[pl.BlockSpec((tb, SB_W), lambda i: (i, 0)),
                   pl.BlockSpec((1, nb, nh * tb, tb), lambda i: (i, 0, 0, 0))] + r_out,
        out_shape=[jax.ShapeDtypeStruct((S, SB_W), F32), jax.ShapeDtypeStruct((nb, nb, nh * tb, tb), BF16)] + r_shape,
        scratch_shapes=r_scr,
        compiler_params=_cp(("arbitrary",)),
    )(q, k, v, *r_args)


def sb_bwd(q, k, v, do, wts, ride=None, tb=QBLK):
    S = q.shape[0]
    nh = SB_W // HEAD
    nb = S // tb
    scale = HEAD ** -0.5
    r_in, r_out, r_shape, r_scr, r_args = _ride_specs(ride)

    def body(*refs):
        qb = pl.program_id(0)
        (q_ref, k_ref, v_ref, do_ref, w_ref, dq_ref, dk_ref, dv_ref, g_scr), finish = _riding(
            ride, refs, 5, 3, qb == 0, qb == nb - 1)

        @pl.when(qb == 0)
        def _():
            dk_ref[...] = jnp.zeros_like(dk_ref)
            dv_ref[...] = jnp.zeros_like(dv_ref)

        diag = _sb_diag(tb, nh)
        before_mat = _tri(tb, False)
        qs = _head_stack(q_ref[...])
        dos = _head_stack(do_ref[...])

        def weights_pass(kb, carry):
            rows = pl.ds(pl.multiple_of(kb * tb, tb), tb)
            w = w_ref[0, kb]
            g_scr[kb] = _dg(dos, v_ref[rows, :].astype(BF16), ((1,), (1,))) * w.astype(F32)
            return carry, _dg(w, dos, ((0,), (0,)))

        def add_rows(ref):
            def commit(kb, val):
                ref[pl.ds(pl.multiple_of(kb * tb, tb), tb), :] += val
            return commit

        zero_run = jnp.zeros((nh * tb, 1), F32)
        _sb_sweep(weights_pass, 0, qb + 1, 0, 1, add_rows(dv_ref))

        def left_to_right(kb, carry, strict):
            dq, run = carry
            rows = pl.ds(pl.multiple_of(kb * tb, tb), tb)
            kblk = k_ref[rows, :].astype(BF16)
            gw = g_scr[kb]
            sig = jax.nn.sigmoid(_dg(qs, kblk, ((1,), (1,))) * scale)
            dkeep = _masked(strict, _tri_sums(gw, before_mat) + run)
            dz = ((gw * (1.0 - sig) - dkeep * sig) * scale).astype(BF16)
            dq = dq + _dg(dz, kblk, ((1,), (0,)))
            return (dq, run + jnp.sum(gw, axis=1, keepdims=True)), _dg(dz, qs, ((0,), (0,)))

        carry = _sb_sweep(lambda kb, c: left_to_right(kb, c, None), 0, qb,
                          (jnp.zeros((nh * tb, SB_W), F32), zero_run), 1, add_rows(dk_ref))
        (dq, _), dk_diag = left_to_right(qb, carry, diag)
        add_rows(dk_ref)(qb, dk_diag)
        dq_ref[...] = _head_pick(dq)
        finish()

    whole = pl.BlockSpec((S, SB_W), lambda i: (0, 0))
    blk = pl.BlockSpec((tb, SB_W), lambda i: (i, 0))
    return pl.pallas_call(
        body, name="sb_bwd", grid=(S // tb,),
        in_specs=[blk, whole, whole, blk, pl.BlockSpec((1, nb, nh * tb, tb), lambda i: (i, 0, 0, 0))] + r_in,
        out_specs=[blk, whole, whole] + r_out,
        out_shape=[jax.ShapeDtypeStruct((S, SB_W), F32)] * 3 + r_shape,
        scratch_shapes=[pltpu.VMEM((S // tb, nh * tb, tb), F32)] + r_scr,
        compiler_params=_cp(("arbitrary",)),
    )(q, k, v, do, wts, *r_args)


def reorder(name, x, groups, inverse):
    P, S, _ = x.shape

    def body(x_ref, o_ref):
        p = pl.program_id(0)
        for gi, r in enumerate(groups):
            @pl.when(p // 2 == gi)
            def _(r=r):
                L = S // r
                if r == 1:
                    o_ref[...] = x_ref[...]
                for c in range(r if r > 1 else 0):
                    if inverse:
                        o_ref[pl.ds(c, L, stride=r), :] = x_ref[c * L:(c + 1) * L, :]
                    else:
                        o_ref[c * L:(c + 1) * L, :] = x_ref[pl.ds(c, L, stride=r), :]

    slab = pl.BlockSpec((None, S, 128), lambda p: (p, 0, 0))
    return pl.pallas_call(
        body, name=name, grid=(P,), in_specs=[slab], out_specs=slab,
        out_shape=jax.ShapeDtypeStruct(x.shape, x.dtype), compiler_params=_cp(("parallel",)),
    )(x)


def _dil_blocks(S):
    return S // QBLK


def _dil_mask(n_in_stream):
    qi = lax.broadcasted_iota(jnp.int32, (QBLK, 2 * QBLK), 0)
    kj = lax.broadcasted_iota(jnp.int32, (QBLK, 2 * QBLK), 1) - QBLK
    dist = qi - kj
    return (dist >= 0) & (dist <= QBLK) & ((n_in_stream > 0) | (kj >= 0))


def _stream_pos(gi, i, S):
    nb = jnp.where(gi == 0, S // (QBLK * DIL[0]), jnp.where(gi == 1, S // (QBLK * DIL[1]), S // (QBLK * DIL[2])))
    return i % nb


def dil_fwd(q, k, v, bias, ride=None):
    S = q.shape[1]
    nblk = _dil_blocks(S)
    r_in, r_out, r_shape, r_scr, r_args = _ride_specs(ride)

    def body(*refs):
        gi, i = pl.program_id(0), pl.program_id(1)
        (q_ref, kc_ref, kp_ref, vc_ref, vp_ref, b_ref, o_ref, l_ref), finish = _riding(
            ride, refs, 6, 2, (gi == 0) & (i == 0), (gi == len(DIL) - 1) & (i == nblk - 1))
        mask = _dil_mask(_stream_pos(gi, i, S))
        for j in range(2):
            q2, kc, kp, vc, vp = q_ref[j], kc_ref[j], kp_ref[j], vc_ref[j], vp_ref[j]
            os_, ls_ = [], []
            for hh in range(2):
                sl = slice(HEAD * hh, HEAD * (hh + 1))
                kw = jnp.concatenate([kp[:, sl], kc[:, sl]], axis=0)
                vw = jnp.concatenate([vp[:, sl], vc[:, sl]], axis=0)
                lg = _bdot(q2[:, sl], kw, ((1,), (1,))) * (HEAD ** -0.5) + b_ref[2 * j + hh]
                lg = jnp.where(mask, lg, NEG_INF)
                m = jnp.max(lg, axis=-1, keepdims=True)
                p = jnp.exp(lg - m)
                den = jnp.sum(p, axis=-1, keepdims=True)
                os_.append(_bdot(p / den, vw, ((1,), (0,))))
                ls_.append(jnp.broadcast_to(m + jnp.log(den), (QBLK, HEAD)))
            o_ref[j] = jnp.concatenate(os_, axis=1)
            l_ref[j] = jnp.concatenate(ls_, axis=1)
        finish()

    cur = pl.BlockSpec((2, QBLK, 128), lambda g, i: (g, i, 0))
    prev = pl.BlockSpec((2, QBLK, 128), lambda g, i: (g, jnp.maximum(i - 1, 0), 0))
    return pl.pallas_call(
        body, name="dil_fwd", grid=(len(DIL), nblk),
        in_specs=[cur, cur, prev, cur, prev, pl.BlockSpec((4, QBLK, 2 * QBLK), lambda g, i: (g, 0, 0))] + r_in,
        out_specs=[cur, cur] + r_out,
        out_shape=[jax.ShapeDtypeStruct(q.shape, F32)] * 2 + r_shape,
        scratch_shapes=r_scr,
        compiler_params=_cp(("arbitrary", "arbitrary")),
    )(q, k, k, v, v, bias, *r_args)


def dil_bwd(q, k, v, bias, o, lse, do, dlse, ride=None):
    S = q.shape[1]
    nblk = _dil_blocks(S)
    r_in, r_out, r_shape, r_scr, r_args = _ride_specs(ride)

    def body(*refs):
        gi, i = pl.program_id(0), pl.program_id(1)
        (q_ref, kc_ref, kp_ref, vc_ref, vp_ref, b_ref, o_ref, l_ref, do_ref, dl_ref,
         dq_ref, dk_ref, dv_ref, ds_ref, dk_car, dv_car), finish = _riding(
            ride, refs, 10, 4, (gi == 0) & (i == 0), (gi == len(DIL) - 1) & (i == nblk))

        @pl.when(i == 0)
        def _():
            ds_ref[...] = jnp.zeros_like(ds_ref)
            dk_car[...] = jnp.zeros_like(dk_car)
            dv_car[...] = jnp.zeros_like(dv_car)

        @pl.when(i < nblk)
        def _():
            mask = _dil_mask(_stream_pos(gi, i, S))
            for j in range(2):
                q2, kc, kp, vc, vp = q_ref[j], kc_ref[j], kp_ref[j], vc_ref[j], vp_ref[j]
                o2, l2, do2, dl2 = o_ref[j], l_ref[j], do_ref[j], dl_ref[j]
                dqs, dkps, dkcs, dvps, dvcs = [], [], [], [], []
                for hh in range(2):
                    sl = slice(HEAD * hh, HEAD * (hh + 1))
                    qh, doh = q2[:, sl], do2[:, sl]
                    kw = jnp.concatenate([kp[:, sl], kc[:, sl]], axis=0)
                    vw = jnp.concatenate([vp[:, sl], vc[:, sl]], axis=0)
                    lg = _bdot(qh, kw, ((1,), (1,))) * (HEAD ** -0.5) + b_ref[2 * j + hh]
                    p = jnp.where(mask, jnp.exp(lg - l2[:, HEAD * hh:HEAD * hh + 1]), 0.0)
                    dp = _bdot(doh, vw, ((1,), (1,)))
                    delta = jnp.sum(doh * o2[:, sl], axis=-1, keepdims=True)
                    dl = jnp.sum(dl2[:, sl], axis=-1, keepdims=True)
                    ds = p * (dp - delta + dl)
                    ds_ref[2 * j + hh] += ds
                    dsq = ds * (HEAD ** -0.5)
                    dqs.append(_bdot(dsq, kw, ((1,), (0,))))
                    dkw = _bdot(dsq, qh, ((0,), (0,)))
                    dvw = _bdot(p, doh, ((0,), (0,)))
                    dkps.append(dkw[:QBLK])
                    dkcs.append(dkw[QBLK:])
                    dvps.append(dvw[:QBLK])
                    dvcs.append(dvw[QBLK:])
                dq_ref[j] = jnp.concatenate(dqs, axis=1)
                dk_ref[j] = dk_car[j] + jnp.concatenate(dkps, axis=1)
                dv_ref[j] = dv_car[j] + jnp.concatenate(dvps, axis=1)
                dk_car[j] = jnp.concatenate(dkcs, axis=1)
                dv_car[j] = jnp.concatenate(dvcs, axis=1)

        @pl.when(i == nblk)
        def _():
            dk_ref[...] = dk_car[...]
            dv_ref[...] = dv_car[...]

        finish()

    cur = pl.BlockSpec((2, QBLK, 128), lambda g, i: (g, jnp.minimum(i, nblk - 1), 0))
    prev = pl.BlockSpec((2, QBLK, 128), lambda g, i: (g, jnp.clip(i - 1, 0, nblk - 1), 0))
    bspec = pl.BlockSpec((4, QBLK, 2 * QBLK), lambda g, i: (g, 0, 0))
    return pl.pallas_call(
        body, name="dil_bwd", grid=(len(DIL), nblk + 1),
        in_specs=[cur, cur, prev, cur, prev, bspec, cur, cur, cur, cur] + r_in,
        out_specs=[cur, prev, prev, bspec] + r_out,
        out_shape=[jax.ShapeDtypeStruct(q.shape, F32)] * 3 + [jax.ShapeDtypeStruct(bias.shape, F32)] + r_shape,
        scratch_shapes=[pltpu.VMEM((2, QBLK, 128), F32), pltpu.VMEM((2, QBLK, 128), F32)] + r_scr,
        compiler_params=_cp(("arbitrary", "arbitrary")),
    )(q, k, k, v, v, bias, o, lse, do, dlse, *r_args)


def _t5_bucket(dist):
    max_exact = N_BUCKETS // 2
    d = jnp.maximum(dist, 1).astype(F32)
    large = max_exact + (jnp.log(d / max_exact) / math.log(MAX_DISTANCE / max_exact)
                         * (N_BUCKETS - max_exact)).astype(jnp.int32)
    large = jnp.minimum(large, N_BUCKETS - 1)
    return jnp.where(dist < max_exact, dist, large)


def _bucket_maps():
    qi = jnp.arange(QBLK)[:, None]
    kj = jnp.arange(2 * QBLK)[None, :] - QBLK
    dist = jnp.maximum(qi - kj, 0)
    return jnp.stack([_t5_bucket(dist * r) for r in DIL])


def bias_table(rel_bias, buckets):
    def body(tbl_ref, bk_ref, o_ref):
        for h in range(DL_HEADS):
            bk = bk_ref[h // 4]

            def step(b, acc):
                return jnp.where(bk == b, tbl_ref[b, h], acc)

            o_ref[h] = lax.fori_loop(0, N_BUCKETS, step, jnp.zeros(bk.shape, F32))

    return pl.pallas_call(
        body, name="bias_table", out_shape=jax.ShapeDtypeStruct((DL_HEADS,) + buckets.shape[1:], F32),
        in_specs=[pl.BlockSpec(memory_space=pltpu.SMEM), pl.BlockSpec(memory_space=pltpu.VMEM)],
        out_specs=pl.BlockSpec(memory_space=pltpu.VMEM),
    )(rel_bias, buckets)


def bias_grad(ds, buckets):
    def body(ds_ref, bk_ref, o_ref):
        lane = lax.broadcasted_iota(jnp.int32, (1, 128), 1)
        for h in range(DL_HEADS):
            dsv = ds_ref[h]
            bk = bk_ref[h // 4]

            def step(b, row):
                return jnp.where(lane == b, jnp.sum(jnp.where(bk == b, dsv, 0.0)), row)

            o_ref[h:h + 1, :] = lax.fori_loop(0, N_BUCKETS, step, jnp.zeros((1, 128), F32))

    return pl.pallas_call(
        body, name="bias_grad", out_shape=jax.ShapeDtypeStruct((DL_HEADS, 128), F32),
        in_specs=[pl.BlockSpec(memory_space=pltpu.VMEM)] * 2, out_specs=pl.BlockSpec(memory_space=pltpu.VMEM),
    )(ds, buckets)


def f_attn_out(x, oa, o, lse, w):
    og = [jnp.concatenate([o[2 * g], o[2 * g + 1]], axis=1) for g in range(3)]
    lg = [jnp.concatenate([lse[2 * g], lse[2 * g + 1]], axis=1) for g in range(3)]
    m = jnp.maximum(jnp.maximum(lg[0], lg[1]), lg[2])
    e = [jnp.exp(l - m) for l in lg]
    den = e[0] + e[1] + e[2]
    ob = (e[0] * og[0] + e[1] * og[1] + e[2] * og[2]) / den
    return x + mm(jnp.concatenate([oa, ob], axis=1), w)


def norm_shift_fwd(x, g, tm=256):
    S = x.shape[0]

    def body(x_ref, xp_ref, g_ref, h_ref, hs_ref):
        h = rms(x_ref[...], g_ref[...])
        hp = rms(xp_ref[7:8, :], g_ref[...])
        hp = jnp.where(pl.program_id(0) == 0, 0.0, hp)
        row = lax.broadcasted_iota(jnp.int32, (tm, D), 0)
        h_ref[...] = h
        hs_ref[...] = jnp.where(row == 0, hp, pltpu.roll(h, 1, 0))

    return pl.pallas_call(
        body, name="rw_norm_shift", grid=(S // tm,),
        in_specs=[pl.BlockSpec((tm, D), lambda t: (t, 0)),
                  pl.BlockSpec((8, D), lambda t: (jnp.maximum(t * (tm // 8) - 1, 0), 0)),
                  pl.BlockSpec((1, D), lambda t: (0, 0))],
        out_specs=[pl.BlockSpec((tm, D), lambda t: (t, 0))] * 2,
        out_shape=[jax.ShapeDtypeStruct((S, D), F32)] * 2,
        compiler_params=_cp(("parallel",)),
    )(x, x, g)


def norm_shift_bwd(x, g, dh, dhs, dres, tm=256):
    S = x.shape[0]
    nt = S // tm

    def body(x_ref, g_ref, dh_ref, dhs_ref, dhn_ref, dr_ref, dx_ref, dg_ref):
        t = pl.program_id(0)
        nxt = jnp.where(t == nt - 1, 0.0, dhn_ref[0:1, :])
        row = lax.broadcasted_iota(jnp.int32, (tm, D), 0)
        tot = dh_ref[...] + jnp.where(row == tm - 1, nxt, pltpu.roll(dhs_ref[...], tm - 1, 0))
        _, vjp = jax.vjp(rms, x_ref[...], g_ref[...])
        dx, dg = vjp(tot)
        dx_ref[...] = dr_ref[...] + dx

        @pl.when(t == 0)
        def _():
            dg_ref[...] = dg

        @pl.when(t != 0)
        def _():
            dg_ref[...] += dg

    tile = pl.BlockSpec((tm, D), lambda t: (t, 0))
    return pl.pallas_call(
        body, name="rw_norm_shift_bwd", grid=(nt,),
        in_specs=[tile, pl.BlockSpec((1, D), lambda t: (0, 0)), tile, tile,
                  pl.BlockSpec((8, D), lambda t: (jnp.minimum((t + 1) * (tm // 8), S // 8 - 1), 0)), tile],
        out_specs=[tile, pl.BlockSpec((1, D), lambda t: (0, 0))],
        out_shape=[jax.ShapeDtypeStruct((S, D), F32), jax.ShapeDtypeStruct((1, D), F32)],
        compiler_params=_cp(("arbitrary",)),
    )(x, g, dh, dhs, dhs, dres)


def f_rw_proj(h, hs, mix, w):
    return mm(h + (hs - h) * mix, w)


def f_rw_mid(h, hs, r, k, v, mix3, w0, a0, kkw, kaw, w1, w2, a1, a2, g1, g2):
    xx = hs - h
    xw, xa, xg = h + xx * mix3[0:1], h + xx * mix3[1:2], h + xx * mix3[2:3]
    w_log = -softplus(-(w0 + mm(jnp.tanh(mm(xw, w1)), w2))) - 0.5
    lw = -jnp.exp(w_log)
    ag = jax.nn.sigmoid(a0 + mm(mm(xa, a1), a2))
    gate = mm(jax.nn.sigmoid(mm(xg, g1)), g2)
    kk = k * kkw
    kk = kk / jnp.maximum(jnp.sqrt(group_sum(kk * kk, RW_H)), 1e-12)
    kmod = k * (1.0 + (ag - 1.0) * kaw)
    return (to_heads(r), to_heads(lw), to_heads(kmod), to_heads(v), to_heads(-kk), to_heads(kk * ag), gate)


def f_rw_post(yh, rh, kh, vh, gate, x, lng, lnb, rk, wo):
    mu = jnp.mean(yh, axis=-1, keepdims=True)
    var = jnp.mean(jnp.square(yh - mu), axis=-1, keepdims=True)
    yn = (yh - mu) * lax.rsqrt(var + GN_EPS)
    bonus = jnp.sum(rh * kh * rk, axis=-1, keepdims=True) * vh
    y = from_heads(yn) * lng + lnb + from_heads(bonus)
    return x + mm(y * gate, wo)


def _split2(x):
    hi = x.astype(BF16)
    return hi, (x - hi.astype(F32)).astype(BF16)


def _b3(x, y, cx, cy):
    xh, xl = _split2(x)
    yh, yl = _split2(y)
    x3 = jnp.concatenate([xh, xh, xl], axis=cx)
    y3 = jnp.concatenate([yh, yl, yh], axis=cy)
    return lax.dot_general(x3, y3, (((cx,), (cy,)), ((0,), (0,))), preferred_element_type=F32)


@jax.custom_vjp
def b_nt(x, y):
    return _b3(x, y, 2, 2)


@jax.custom_vjp
def b_nn(x, y):
    return _b3(x, y, 2, 1)


@jax.custom_vjp
def b_tn(x, y):
    return _b3(x, y, 1, 1)


def _b1(x, y, cx, cy):
    return lax.dot_general(x.astype(BF16), y.astype(BF16), (((cx,), (cy,)), ((0,), (0,))), preferred_element_type=F32)


b_nt.defvjp(lambda x, y: (b_nt(x, y), (x, y)), lambda r, g: (_b1(g, r[1], 2, 1), _b1(g, r[0], 1, 1)))
b_nn.defvjp(lambda x, y: (b_nn(x, y), (x, y)), lambda r, g: (_b1(g, r[1], 2, 2), _b1(r[0], g, 1, 1)))
b_tn.defvjp(lambda x, y: (b_tn(x, y), (x, y)), lambda r, g: (_b1(r[1], g, 2, 2), _b1(r[0], g, 2, 1)))


def _tri_apply(x, lower):
    H, C, _ = x.shape
    ii = lax.broadcasted_iota(jnp.int32, (C, C), 0)
    jj = lax.broadcasted_iota(jnp.int32, (C, C), 1)
    m = jnp.broadcast_to(((jj <= ii) if lower else (jj >= ii)).astype(BF16), (H, C, C))
    x1 = x.astype(BF16)
    r1 = x - x1.astype(F32)
    x2 = r1.astype(BF16)
    x3 = (r1 - x2.astype(F32)).astype(BF16)
    return lax.dot_general(jnp.concatenate([m, m, m], axis=2), jnp.concatenate([x1, x2, x3], axis=1),
                           (((2,), (1,)), ((0,), (0,))), preferred_element_type=F32)


@jax.custom_vjp
def run_sum(x):
    return _tri_apply(x, True)


run_sum.defvjp(lambda x: (run_sum(x), None), lambda _, g: (_tri_apply(g, False),))


def rwkv_chunk(S0, r, lw, k, v, a, b):
    H, C, _ = r.shape
    V = S0.shape[1]
    ii = lax.broadcasted_iota(jnp.int32, (C, C), 0)
    jj = lax.broadcasted_iota(jnp.int32, (C, C), 1)
    strict = jj < ii
    i2 = lax.broadcasted_iota(jnp.int32, (C, 2 * C), 0)
    j2 = lax.broadcasted_iota(jnp.int32, (C, 2 * C), 1)
    incl2 = jnp.where(j2 >= C, j2 - C, j2) <= i2
    g = run_sum(lw)
    ig = jnp.exp(-g)
    ar = jnp.concatenate([a * jnp.exp(g - lw), r * jnp.exp(g)], axis=1)
    bk = jnp.concatenate([b * ig, k * ig], axis=1)
    m = b_nt(ar, bk)
    a_ab = jnp.where(strict, m[:, :C, :C], 0.0)
    a_ak = jnp.where(strict, m[:, :C, C:], 0.0)
    b_r = jnp.where(incl2, m[:, C:, :], 0.0)
    p = b_nt(ar, S0)
    u = p[:, :C] + b_nn(a_ak, v)
    nmat, n = a_ab, 1
    while n < C:
        n *= 2
        if n < C:
            z = b_nn(nmat, jnp.concatenate([u, nmat], axis=2))
            u, nmat = u + z[:, :, :V], z[:, :, V:]
        else:
            u = u + b_nn(nmat, u)
    uv = jnp.concatenate([u, v], axis=1)
    y = p[:, C:] + b_nn(b_r, uv)
    g_end = g[:, C - 1:C, :]
    dec = jnp.exp(g_end - g)
    s_new = S0 * jnp.exp(g_end) + b_tn(uv, jnp.concatenate([b * dec, k * dec], axis=1))
    return y, s_new


def rwkv_fwd(r, lw, k, v, a, b):
    H, S, _ = r.shape
    C = RW_CHUNK

    def body(r_ref, lw_ref, k_ref, v_ref, a_ref, b_ref, y_ref, s_ref, s_scr):
        @pl.when(pl.program_id(0) == 0)
        def _():
            s_scr[...] = jnp.zeros_like(s_scr)

        s0 = s_scr[...]
        s_ref[0] = s0
        y, s1 = rwkv_chunk(s0, r_ref[...], lw_ref[...], k_ref[...], v_ref[...], a_ref[...], b_ref[...])
        y_ref[...] = y
        s_scr[...] = s1

    bs = pl.BlockSpec((H, C, HEAD), lambda c: (0, c, 0))
    return pl.pallas_call(
        body, name="rwkv_fwd", grid=(S // C,), in_specs=[bs] * 6,
        out_specs=[bs, pl.BlockSpec((1, H, HEAD, HEAD), lambda c: (c, 0, 0, 0))],
        out_shape=[jax.ShapeDtypeStruct((H, S, HEAD), F32), jax.ShapeDtypeStruct((S // C, H, HEAD, HEAD), F32)],
        scratch_shapes=[pltpu.VMEM((H, HEAD, HEAD), F32)],
        compiler_params=_cp(("arbitrary",)),
    )(r, lw, k, v, a, b)


def rwkv_bwd(r, lw, k, v, a, b, states, dy):
    H, S, _ = r.shape
    C = RW_CHUNK
    nc = S // C

    def body(r_ref, lw_ref, k_ref, v_ref, a_ref, b_ref, s_ref, dy_ref, dr, dlw, dk, dv, da, db, ds_scr):
        @pl.when(pl.program_id(0) == 0)
        def _():
            ds_scr[...] = jnp.zeros_like(ds_scr)

        _, vjp = jax.vjp(rwkv_chunk, s_ref[0], r_ref[...], lw_ref[...], k_ref[...], v_ref[...], a_ref[...], b_ref[...])
        grads = vjp((dy_ref[...], ds_scr[...]))
        ds_scr[...] = grads[0]
        for o, gv in zip((dr, dlw, dk, dv, da, db), grads[1:]):
            o[...] = gv

    bs = pl.BlockSpec((H, C, HEAD), lambda c: (0, nc - 1 - c, 0))
    return pl.pallas_call(
        body, name="rwkv_bwd", grid=(nc,),
        in_specs=[bs] * 6 + [pl.BlockSpec((1, H, HEAD, HEAD), lambda c: (nc - 1 - c, 0, 0, 0)), bs],
        out_specs=[bs] * 6, out_shape=[jax.ShapeDtypeStruct((H, S, HEAD), F32)] * 6,
        scratch_shapes=[pltpu.VMEM((H, HEAD, HEAD), F32)],
        compiler_params=_cp(("arbitrary",)),
    )(r, lw, k, v, a, b, states, dy)


def loss_head(y, target, tm=512):
    S = y.shape[0]

    def body(y_ref, t_ref, dy_ref, l_ref):
        e = y_ref[...] - t_ref[...]
        dy_ref[...] = e * (1.0 / D)
        part = jnp.broadcast_to(0.5 * jnp.sum(jnp.mean(e * e, axis=-1, keepdims=True)), (1, 128))

        @pl.when(pl.program_id(0) == 0)
        def _():
            l_ref[...] = part

        @pl.when(pl.program_id(0) != 0)
        def _():
            l_ref[...] += part

    tile = pl.BlockSpec((tm, D), lambda t: (t, 0))
    return pl.pallas_call(
        body, name="loss_head", grid=(S // tm,), in_specs=[tile, tile],
        out_specs=[tile, pl.BlockSpec((1, 128), lambda t: (0, 0))],
        out_shape=[jax.ShapeDtypeStruct((S, D), F32), jax.ShapeDtypeStruct((1, 128), F32)],
        compiler_params=_cp(("arbitrary",)),
    )(y, target)


def _row_tile(rows, cols, budget=1 << 19):
    best = None
    for tr in range(8, rows + 1, 8):
        if rows % tr == 0 and tr * cols <= budget:
            best = tr
    return best or rows


def _adam(w, g, m, v):
    m = ADAM_B1 * m + (1.0 - ADAM_B1) * g
    v = ADAM_B2 * v + (1.0 - ADAM_B2) * jnp.square(g)
    m_hat = m / (1.0 - ADAM_B1 ** ADAM_STEP)
    v_hat = v / (1.0 - ADAM_B2 ** ADAM_STEP)
    return -ADAM_LR * (m_hat / (jnp.sqrt(v_hat) + ADAM_EPS) + ADAM_WD * w), m, v


def sum_slots(name, parts, dtype=F32, extras=()):
    n = 0 if parts is None else parts.shape[0]
    R, C = extras[0].shape if parts is None else parts.shape[1:]
    tr = _row_tile(R, C * (n + len(extras)))
    ins = ([] if parts is None else [parts]) + list(extras)

    def body(*refs):
        terms = [] if parts is None else [refs[0][i] for i in range(n)]
        terms += [r[...] for r in refs[len(ins) - len(extras):len(ins)]]
        s = terms[0].astype(F32)
        for t in terms[1:]:
            s = s + t.astype(F32)
        refs[len(ins)][...] = s.astype(dtype)

    tile = pl.BlockSpec((tr, C), lambda t: (t, 0))
    return pl.pallas_call(
        body, name=name, grid=(R // tr,),
        in_specs=([] if parts is None else [pl.BlockSpec((n, tr, C), lambda t: (0, t, 0))]) + [tile] * len(extras),
        out_specs=tile, out_shape=jax.ShapeDtypeStruct((R, C), dtype), compiler_params=_cp(("parallel",)),
    )(*ins)


def sum_own_half(name, split, theirs, c, dtype):
    nq, _, rh, cols = split.shape
    tr = _row_tile(rh, 2 * cols)

    def body(c_ref, a_ref, b_ref, o_ref):
        o_ref[...] = (a_ref[...] + b_ref[...]).astype(dtype)

    tile = pl.BlockSpec((None, tr, cols), lambda q, t, c_ref: (q, t, 0))
    return pl.pallas_call(
        body, name=name,
        grid_spec=pltpu.PrefetchScalarGridSpec(
            num_scalar_prefetch=1, grid=(nq, rh // tr),
            in_specs=[pl.BlockSpec((None, None, tr, cols), lambda q, t, c_ref: (q, c_ref[0], t, 0)), tile],
            out_specs=tile),
        out_shape=jax.ShapeDtypeStruct((nq, rh, cols), dtype), compiler_params=_cp(("parallel", "parallel")),
    )(jnp.reshape(c, (1,)).astype(jnp.int32), split, theirs)


def sum_landed(name, landed, chip_sum, p):
    n, rh, cols = landed.shape
    tr = _row_tile(rh, (n + 1) * cols)

    def body(p_ref, l_ref, own_ref, o_ref):
        s = l_ref[0].astype(F32)
        for i in range(1, n):
            s = s + l_ref[i].astype(F32)
        o_ref[...] = s + own_ref[...].astype(F32)

    return pl.pallas_call(
        body, name=name,
        grid_spec=pltpu.PrefetchScalarGridSpec(
            num_scalar_prefetch=1, grid=(rh // tr,),
            in_specs=[pl.BlockSpec((n, tr, cols), lambda t, p_ref: (0, t, 0)),
                      pl.BlockSpec((None, tr, cols), lambda t, p_ref: (p_ref[0], t, 0))],
            out_specs=pl.BlockSpec((tr, cols), lambda t, p_ref: (t, 0))),
        out_shape=jax.ShapeDtypeStruct((rh, cols), F32), compiler_params=_cp(("parallel",)),
    )(jnp.reshape(p, (1,)).astype(jnp.int32), landed, chip_sum)


def adam_step(name, ga, gb, w, m, v):
    R, C = w.shape
    tr = _row_tile(R, C, 1 << 17)
    ins = [ga] + ([gb] if gb is not None else []) + [w, m, v]

    def body(*refs):
        g = refs[0][...]
        if gb is not None:
            g = g + refs[1][...]
        w_ref, m_ref, v_ref, g_out, d_out, m_out, v_out = refs[len(ins) - 3:]
        d, m2, v2 = _adam(w_ref[...], g, m_ref[...], v_ref[...])
        g_out[...] = g
        d_out[...] = d
        m_out[...] = m2
        v_out[...] = v2

    tile = pl.BlockSpec((tr, C), lambda t: (t, 0))
    return pl.pallas_call(
        body, name=name, grid=(R // tr,), in_specs=[tile] * len(ins), out_specs=[tile] * 4,
        out_shape=[jax.ShapeDtypeStruct((R, C), F32)] * 4, compiler_params=_cp(("parallel",)),
    )(*ins)


def adam_ffn(name, g_pieces, w, m, v, transposed=False):
    if transposed:
        res = adam_ffn(name, g_pieces, *(jnp.swapaxes(a, 2, 3) for a in (w, m, v)))
        return [jnp.swapaxes(r, 2, 3) for r in res]
    _, _, R, C = w.shape
    tr = _row_tile(R, 4 * C, 1 << 17)

    def body(g00, g01, g10, g11, w_ref, m_ref, v_ref, g_out, d_out, m_out, v_out):
        for l, j, g_ref in ((0, 0, g00), (0, 1, g01), (1, 0, g10), (1, 1, g11)):
            g = g_ref[...]
            d, m2, v2 = _adam(w_ref[l, j], g, m_ref[l, j], v_ref[l, j])
            g_out[l, j] = g
            d_out[l, j] = d
            m_out[l, j] = m2
            v_out[l, j] = v2

    piece = pl.BlockSpec((tr, C), lambda t: (t, 0))
    full = pl.BlockSpec((2, 2, tr, C), lambda t: (0, 0, t, 0))
    return pl.pallas_call(
        body, name=name, grid=(R // tr,), in_specs=[piece] * 4 + [full] * 3, out_specs=[full] * 4,
        out_shape=[jax.ShapeDtypeStruct(w.shape, F32)] * 4, compiler_params=_cp(("parallel",)),
    )(*g_pieces, w, m, v)


def _place():
    return lax.axis_index("x"), lax.axis_index("y"), lax.axis_index("c")


def _flip(me, mask):
    return tuple(1 - v if mk else v for v, mk in zip(me, mask))


CHIP_MASKS = ((1, 0, 0), (0, 1, 0), (1, 1, 0))
ALL_MASKS = tuple((a, b, c) for a in (0, 1) for b in (0, 1) for c in (0, 1) if (a, b, c) != (0, 0, 0))


def _chip(dev):
    return 2 * dev[0] + dev[1]


def _devno(dev):
    return 4 * dev[0] + 2 * dev[1] + dev[2]


class Pushes:
    def __init__(self, arrays, out_shapes, masks, copies, src_of, dst_of, alias=False):
        self.arrays, self.out_shapes, self.masks, self.copies = list(arrays), list(out_shapes), masks, copies
        self.src_of, self.dst_of, self.alias = src_of, dst_of, alias
        self.n = len(self.arrays)

    def sem_shapes(self):
        k = self.n * len(self.masks) * self.copies
        return [pltpu.SemaphoreType.DMA((k,)), pltpu.SemaphoreType.DMA((k,))]

    def ops(self, ins, outs, send_sems, recv_sems):
        me = _place()
        sends, lands = [], []
        for i in range(self.n):
            for j, mk in enumerate(self.masks):
                peer = _flip(me, mk)
                srcs, dsts = self.src_of(ins[i], me, j), self.dst_of(outs[i], me, j)
                here = self.dst_of(outs[i], peer, j)
                for q in range(self.copies):
                    sem = (i * len(self.masks) + j) * self.copies + q
                    sends.append(pltpu.make_async_remote_copy(
                        src_ref=srcs[q], dst_ref=dsts[q], send_sem=send_sems.at[sem], recv_sem=recv_sems.at[sem],
                        device_id=peer, device_id_type=MESH))
                    lands.append(pltpu.make_async_remote_copy(
                        src_ref=here[q], dst_ref=here[q], send_sem=send_sems.at[sem], recv_sem=recv_sems.at[sem],
                        device_id=peer, device_id_type=MESH))

        def start():
            for cp in sends:
                cp.start()

        def wait():
            for cp in lands:
                cp.wait_recv()
            for cp in sends:
                cp.wait_send()

        return start, wait


_HBM = pl.BlockSpec(memory_space=pl.ANY)


def exchange(name, p, local_of=None):
    n = p.n

    def body(*refs):
        ins, outs = refs[:n], refs[n:2 * n]
        start, wait = p.ops(ins, outs, refs[2 * n], refs[2 * n + 1])
        locals_ = []
        if local_of is not None:
            for i in range(n):
                src, dst = local_of(ins[i], outs[i], _place())
                locals_.append(pltpu.make_async_copy(src, dst, refs[2 * n + 2].at[i]))
                locals_[-1].start()
        start()
        wait()
        for cp in locals_:
            cp.wait()

    return pl.pallas_call(
        body, name=name, in_specs=[_HBM] * n, out_specs=[_HBM] * n, out_shape=p.out_shapes,
        scratch_shapes=p.sem_shapes() + ([pltpu.SemaphoreType.DMA((n,))] if local_of is not None else []),
        input_output_aliases={i: i for i in range(n)} if p.alias else {},
    )(*p.arrays)


def _half(c, rows):
    return pl.ds(c * (rows // 2), rows // 2)


def gather_pushes(arrays):
    outs = [jax.ShapeDtypeStruct((N_CHIPS,) + a.shape, a.dtype) for a in arrays]
    sib = len(CHIP_MASKS)
    return Pushes(arrays, outs, CHIP_MASKS + ((0, 0, 1),), 1,
                  src_of=lambda r, me, j: [r] if j == sib else [r.at[_half(me[2], r.shape[0])]],
                  dst_of=lambda o, sender, j: [o.at[_chip(sender)]] if j == sib else
                  [o.at[_chip(sender), _half(sender[2], o.shape[1])]])


def gather_swap(name, got):
    outs = [jax.ShapeDtypeStruct(a.shape, a.dtype) for a in got]
    return exchange(name, Pushes(
        got, outs, ((0, 0, 1),), len(CHIP_MASKS),
        src_of=lambda r, me, j: [r.at[_chip(_flip(me, mk)), _half(me[2], r.shape[1])] for mk in CHIP_MASKS],
        dst_of=lambda o, sender, j: [o.at[_chip(_flip(sender, mk)), _half(sender[2], o.shape[1])] for mk in CHIP_MASKS],
        alias=True))


def reduce_swap(arrays):
    split = [a.reshape(N_CHIPS, 2, a.shape[1] // 2, a.shape[2]) for a in arrays]
    half_shapes = [jax.ShapeDtypeStruct((N_CHIPS,) + a.shape[2:], F32) for a in split]
    return split, Pushes(split, half_shapes, ((0, 0, 1),), 1,
                         src_of=lambda r, me, j: [r.at[:, 1 - me[2]]], dst_of=lambda o, sender, j: [o])


def reduce_begin(tag, names, arrays, wire):
    split, pushes = reduce_swap(arrays)
    return reduce_sum(names, split, exchange(f"grad_pre_swap_{tag}", pushes), wire)


def reduce_sum(names, split, theirs, wire):
    c = lax.axis_index("c")
    chip_sum = [sum_own_half(f"sum2_{nm}", a, t, c, dt) for nm, a, t, dt in zip(names, split, theirs, wire)]
    pushes = Pushes(chip_sum, [jax.ShapeDtypeStruct((len(CHIP_MASKS),) + a.shape[1:], a.dtype) for a in chip_sum],
                    CHIP_MASKS, 1,
                    src_of=lambda r, me, j: [r.at[_chip(_flip(me, CHIP_MASKS[j]))]],
                    dst_of=lambda o, sender, j: [o.at[j]])
    return chip_sum, pushes


def reduce_end(tag, names, chip_sum, landed):
    x, y, c = _place()
    halves = [sum_landed(f"sum4_{nm}", p, a, _chip((x, y, c))) for nm, p, a in zip(names, landed, chip_sum)]
    others = exchange(f"grad_final_swap_{tag}", Pushes(
        halves, [jax.ShapeDtypeStruct(a.shape, F32) for a in halves], ((0, 0, 1),), 1,
        src_of=lambda r, me, j: [r], dst_of=lambda o, sender, j: [o]))
    return [jnp.concatenate([jnp.where(c == 0, h, o), jnp.where(c == 0, o, h)], axis=0) for h, o in zip(halves, others)]


def gather_all(arrays):
    outs = [jax.ShapeDtypeStruct((8,) + a.shape, a.dtype) for a in arrays]
    return exchange("gather_replicated", Pushes(
        arrays, outs, ALL_MASKS, 1, src_of=lambda r, me, j: [r], dst_of=lambda o, sender, j: [o.at[_devno(sender)]]),
        local_of=lambda r, o, me: (r, o.at[_devno(me)]))


def _unshard_cols(g):
    return jnp.transpose(g, (1, 0, 2)).reshape(g.shape[1], -1)


def _shard_cols(a):
    return jnp.transpose(a.reshape(a.shape[0], N_CHIPS, -1), (1, 0, 2))


class Weights(dict):
    def ride(self, kernel_name):
        return None

    def arrived(self, kernel_name, outs):
        pass


def _forward_backward(x, tgt, W, grads_early=None):
    S = x.shape[0]
    G = {}
    sd = jax.ShapeDtypeStruct

    hidden = {}

    def ffn(xin, l, j):
        out, *hidden[l, j] = ffn_fwd(xin, W["ffn_norm"][l][j], W["ffn_w_gate", l, j], W["ffn_w_up", l, j],
                                     W["ffn_w_down", l, j], l, j)
        return out

    def ffn_back(xin, dout, l, j):
        gn = W["ffn_norm"][l][j]
        dh, G["ffn_w_gate", l, j], G["ffn_w_up", l, j], G["ffn_w_down", l, j] = ffn_bwd(
            xin, gn, W["ffn_w_gate", l, j], W["ffn_w_up", l, j], W["ffn_w_down", l, j], dout, *hidden[l, j], l, j)
        dx, G[("ffn_norm", l, j)] = norm_bwd(f"ffn_norm_bwd_{l}{j}", xin, gn, dh, dout)
        return dx

    x0 = x
    x1 = ffn(x0, 0, 0)
    g0 = W["mix_norm"][0]
    sbq, sbk, sbv = tile_fwd(f_attn_sb, "attn_in_sb", [x1], [g0, W["attn_w_in"][0]], [sd((S, SB_W), F32)] * 3, 256)
    dl_shape = sd((DL_PAIRS, S, 128), F32)
    qn, = tile_fwd(f_attn_qk, "attn_in_q", [x1], [g0, W["attn_w_in"][1], W["attn_q_norm"]], [dl_shape], 256)
    kn, = tile_fwd(f_attn_qk, "attn_in_k", [x1], [g0, W["attn_w_in"][2], W["attn_k_norm"]], [dl_shape], 256)
    vv, = tile_fwd(f_attn_v, "attn_in_v", [x1], [g0, W["attn_w_in"][3]], [dl_shape], 256)
    oa, sb_wts, *rode = sb_fwd(sbq, sbk, sbv, W.ride("sb_fwd"))
    W.arrived("sb_fwd", rode)
    qs, ks, vs = (reorder(nm, t, DIL, False) for nm, t in (("sub_q", qn), ("sub_k", kn), ("sub_v", vv)))
    o_s, lse_s, *rode = dil_fwd(qs, ks, vs, W["bias_mat"], W.ride("dil_fwd"))
    W.arrived("dil_fwd", rode)
    o_n, lse_n = reorder("nat_o", o_s, DIL, True), reorder("nat_lse", lse_s, DIL, True)
    x2, = tile_fwd(f_attn_out, "attn_out", [x1, oa, o_n, lse_n], [W["attn_w_out"]], [sd((S, D), F32)], 256)
    x3 = ffn(x2, 0, 1)
    x4 = ffn(x3, 1, 0)
    g1 = W["mix_norm"][1]
    h, hs = norm_shift_fwd(x4, g1)
    mix = W["rw_mix"]
    r, = tile_fwd(f_rw_proj, "rw_proj_r", [h, hs], [mix[0:1], W["rw_wr"]], [sd((S, D), F32)], 256)
    k, = tile_fwd(f_rw_proj, "rw_proj_k", [h, hs], [mix[2:3], W["rw_wk"]], [sd((S, D), F32)], 256)
    v, = tile_fwd(f_rw_proj, "rw_proj_v", [h, hs], [mix[3:4], W["rw_wv"]], [sd((S, D), F32)], 256)
    mix3 = jnp.concatenate([mix[1:2], mix[4:5], mix[5:6]], axis=0)
    mid_w = [mix3, W["rw_w0"], W["rw_a0"], W["rw_kk"], W["rw_ka"], W["rw_w1"], W["rw_w2"], W["rw_a1"], W["rw_a2"],
             W["rw_g1"], W["rw_g2"]]
    hshape = sd((RW_H, S, HEAD), F32)
    mid_tiles = [h, hs, r, k, v]
    rh, lwh, kh, vh, ah, bh, gate = tile_fwd(f_rw_mid, "rw_mid", mid_tiles, mid_w, [hshape] * 6 + [sd((S, D), F32)], 128)
    yh, states = rwkv_fwd(rh, lwh, kh, vh, ah, bh)
    post_w = [W["rw_lnx_g"], W["rw_lnx_b"], W["rw_rk"], W["rw_wo"]]
    post_tiles = [yh, rh, kh, vh, gate, x4]
    x5, = tile_fwd(f_rw_post, "rw_post", post_tiles, post_w, [sd((S, D), F32)], 128)
    x6 = ffn(x5, 1, 1)
    dx6, loss_part = loss_head(x6, tgt)

    dx5 = ffn_back(x5, dx6, 1, 1)
    (dyh, drh, dkh, dvh, dgate, dx4), (d_lng, d_lnb, d_rk, d_wo) = tile_bwd(
        f_rw_post, "rw_post_bwd", post_tiles, post_w, [dx5], 128, [True] * 6, [True] * 4)
    drh2, dlwh, dkh2, dvh2, dah, dbh = rwkv_bwd(rh, lwh, kh, vh, ah, bh, states, dyh)
    mid_cts = [(drh, drh2), dlwh, (dkh, dkh2), (dvh, dvh2), dah, dbh, dgate]
    (dh, dhs, dr, dk, dv), dmid_w = tile_bwd(f_rw_mid, "rw_mid_bwd", mid_tiles, mid_w, mid_cts, 128,
                                             [True] * 5, [True] * len(mid_w))
    dmix = {}
    for nm, ct, row, wname in (("r", dr, 0, "rw_wr"), ("k", dk, 2, "rw_wk"), ("v", dv, 3, "rw_wv")):
        (dh, dhs), (dmix[row], G[wname]) = tile_bwd(
            f_rw_proj, f"rw_proj_{nm}_bwd", [h, hs], [mix[row:row + 1], W[wname]], [ct], 256,
            [True, True], [True, True], acc={0: dh, 1: dhs})
    dx4, G[("mix_norm", 1)] = norm_shift_bwd(x4, g1, dh, dhs, dx4)
    dmix3 = dmid_w[0]
    G["rw_mix"] = jnp.concatenate([dmix[0], dmix3[0:1], dmix[2], dmix[3], dmix3[1:2], dmix3[2:3]], axis=0)
    for nm, gv in zip(("rw_w0", "rw_a0", "rw_kk", "rw_ka", "rw_w1", "rw_w2", "rw_a1", "rw_a2", "rw_g1", "rw_g2"), dmid_w[1:]):
        G[nm] = gv
    G["rw_lnx_g"], G["rw_lnx_b"], G["rw_rk"], G["rw_wo"] = d_lng, d_lnb, d_rk, d_wo
    dx3 = ffn_back(x3, dx4, 1, 0)
    dx2 = ffn_back(x2, dx3, 0, 1)
    (dx1, doa, do_n, dlse_n), (G["attn_w_out"],) = tile_bwd(
        f_attn_out, "attn_out_bwd", [x1, oa, o_n, lse_n], [W["attn_w_out"]], [dx2], 256, [True] * 4, [True])
    do_s, dlse_s = reorder("sub_do", do_n, DIL, False), reorder("sub_dlse", dlse_n, DIL, False)
    ride, swapped = grads_early(G) if grads_early is not None else (None, None)
    dqs, dks, dvs, dsum, *rode = dil_bwd(qs, ks, vs, W["bias_mat"], o_s, lse_s, do_s, dlse_s, ride)
    ride, landed = swapped(rode) if swapped is not None else (None, None)
    G["rel_bias"] = bias_grad(dsum, W["buckets"])
    dqn, dkn, dvv = (reorder(nm, t, DIL, True) for nm, t in (("nat_dq", dqs), ("nat_dk", dks), ("nat_dv", dvs)))
    dsbq, dsbk, dsbv, *rode = sb_bwd(sbq, sbk, sbv, doa, sb_wts, ride)
    if landed is not None:
        landed(rode)
    dg0 = []
    dwin = []
    (dx1,), (dg, dw) = tile_bwd(f_attn_sb, "attn_in_sb_bwd", [x1], [g0, W["attn_w_in"][0]], [dsbq, dsbk, dsbv], 256,
                                [True], [True, True], acc={0: dx1})
    dg0.append(dg), dwin.append(dw)
    (dx1,), (dg, dw, G["attn_q_norm"]) = tile_bwd(f_attn_qk, "attn_in_q_bwd", [x1], [g0, W["attn_w_in"][1], W["attn_q_norm"]],
                                                  [dqn], 256, [True], [True] * 3, acc={0: dx1})
    dg0.append(dg), dwin.append(dw)
    (dx1,), (dg, dw, G["attn_k_norm"]) = tile_bwd(f_attn_qk, "attn_in_k_bwd", [x1], [g0, W["attn_w_in"][2], W["attn_k_norm"]],
                                                  [dkn], 256, [True], [True] * 3, acc={0: dx1})
    dg0.append(dg), dwin.append(dw)
    (dx1,), (dg, dw) = tile_bwd(f_attn_v, "attn_in_v_bwd", [x1], [g0, W["attn_w_in"][3]], [dvv], 256,
                                [True], [True, True], acc={0: dx1})
    dg0.append(dg), dwin.append(dw)
    G[("mix_norm", 0)] = dg0
    G["attn_w_in"] = dwin
    dx0 = ffn_back(x0, dx1, 0, 0)
    return loss_part, dx0, G


VEC_ROWS = ("ffn_norm", "rw_mix", "rw_w0", "rw_a0", "rw_kk", "rw_ka", "rw_lnx_g", "rw_lnx_b")


def kernel(x, ffn_norm, ffn_w_gate, ffn_w_up, ffn_w_down, mix_norm, rel_bias, attn_w_in, attn_q_norm, attn_k_norm, attn_w_out, rw_mix, rw_w0, rw_w1, rw_w2, rw_a0, rw_a1, rw_a2, rw_g1, rw_g2, rw_kk, rw_ka, rw_rk, rw_wr, rw_wk, rw_wv, rw_wo, rw_lnx_g, rw_lnx_b, loss_target, m_ffn_norm, m_ffn_w_gate, m_ffn_w_up, m_ffn_w_down, m_mix_norm, m_rel_bias, m_attn_w_in, m_attn_q_norm, m_attn_k_norm, m_attn_w_out, m_rw_mix, m_rw_w0, m_rw_w1, m_rw_w2, m_rw_a0, m_rw_a1, m_rw_a2, m_rw_g1, m_rw_g2, m_rw_kk, m_rw_ka, m_rw_rk, m_rw_wr, m_rw_wk, m_rw_wv, m_rw_wo, m_rw_lnx_g, m_rw_lnx_b, v_ffn_norm, v_ffn_w_gate, v_ffn_w_up, v_ffn_w_down, v_mix_norm, v_rel_bias, v_attn_w_in, v_attn_q_norm, v_attn_k_norm, v_attn_w_out, v_rw_mix, v_rw_w0, v_rw_w1, v_rw_w2, v_rw_a0, v_rw_a1, v_rw_a2, v_rw_g1, v_rw_g2, v_rw_kk, v_rw_ka, v_rw_rk, v_rw_wr, v_rw_wk, v_rw_wv, v_rw_wo, v_rw_lnx_g, v_rw_lnx_b):
    names = ["ffn_norm", "ffn_w_gate", "ffn_w_up", "ffn_w_down", "mix_norm", "rel_bias", "attn_w_in", "attn_q_norm",
             "attn_k_norm", "attn_w_out", "rw_mix", "rw_w0", "rw_w1", "rw_w2", "rw_a0", "rw_a1", "rw_a2", "rw_g1", "rw_g2",
             "rw_kk", "rw_ka", "rw_rk", "rw_wr", "rw_wk", "rw_wv", "rw_wo", "rw_lnx_g", "rw_lnx_b"]
    loc = locals()
    w = {n: loc[n] for n in names}
    mom = {n: loc["m_" + n] for n in names}
    vel = {n: loc["v_" + n] for n in names}
    S = x.shape[1]

    ffn3 = ("ffn_w_gate", "ffn_w_up", "ffn_w_down")
    rw_mats = ("rw_w1", "rw_w2", "rw_a1", "rw_a2", "rw_g1", "rw_g2", "rw_wr", "rw_wk", "rw_wv", "rw_wo")
    cols_split = ("attn_w_out", "rw_w2", "rw_a2", "rw_g2")
    shard = {"vec": jnp.concatenate([w[n].reshape(-1, 256) for n in VEC_ROWS], axis=0)}
    for n in ffn3:
        for l in range(2):
            for j in range(2):
                shard[n, l, j] = w[n][l, j].astype(BF16)
    for n in ("attn_w_in", "attn_w_out") + rw_mats:
        shard[n] = w[n].reshape(-1, w[n].shape[-1]).astype(BF16)
    ffn_keys = lambda l, j: [(n, l, j) for n in ffn3]
    w_groups = {"first": ["vec"] + ffn_keys(0, 0) + ["attn_w_in", "attn_w_out"],
                "sb_fwd": ffn_keys(0, 1) + ffn_keys(1, 0) + list(rw_mats),
                "dil_fwd": ffn_keys(1, 1)}
    label = lambda key: key if isinstance(key, str) else f"{key[0]}_{key[1]}{key[2]}"

    class Streamed(Weights):
        def ride(self, kernel_name):
            return gather_pushes([shard[k] for k in w_groups[kernel_name]])

        def arrived(self, kernel_name, outs):
            for key, g in zip(w_groups[kernel_name], gather_swap(f"gather_swap_{kernel_name}", outs)):
                if key == "vec":
                    vec_full = _unshard_cols(g)
                    self["ffn_norm"] = [[vec_full[2 * l + j][None] for j in range(2)] for l in range(2)]
                    self["rw_mix"] = vec_full[4:10]
                    for i, n in enumerate(("rw_w0", "rw_a0", "rw_kk", "rw_ka", "rw_lnx_g", "rw_lnx_b")):
                        self[n] = vec_full[10 + i][None]
                elif key == "attn_w_in":
                    self[key] = [g[p] for p in range(N_CHIPS)]
                elif key in cols_split:
                    self[key] = _unshard_cols(g)
                elif isinstance(key, str):
                    self[key] = g.reshape(D, -1)
                else:
                    self[key] = g

    buckets = _bucket_maps()
    W = Streamed({"mix_norm": [mix_norm[0:1], mix_norm[1:2]], "attn_q_norm": attn_q_norm, "attn_k_norm": attn_k_norm,
                  "rw_rk": rw_rk[0][:, None, :], "buckets": buckets, "bias_mat": bias_table(rel_bias, buckets)})
    W.arrived("first", exchange("gather_weights", W.ride("first")))

    def slots(key, G):
        if key == "vec":
            rows = [G[("ffn_norm", l, j)] for l in range(2) for j in range(2)] + [G["rw_mix"]] + \
                   [G[n] for n in ("rw_w0", "rw_a0", "rw_kk", "rw_ka", "rw_lnx_g", "rw_lnx_b")]
            return _shard_cols(jnp.concatenate(rows, axis=0))
        if key == "attn_w_in":
            return jnp.stack(G[key])
        if key in cols_split:
            return _shard_cols(G[key])
        if isinstance(key, str):
            return G[key].reshape(N_CHIPS, D // N_CHIPS, -1)
        return G[key]

    g_groups = {"early": ffn_keys(1, 1) + ffn_keys(1, 0) + ffn_keys(0, 1) + list(rw_mats) + ["attn_w_out"],
                "late": ["vec", "attn_w_in"] + ffn_keys(0, 0)}
    wire = lambda keys: [F32 if k == "vec" else BF16 for k in keys]
    part = {}

    def grads_early(G):
        keys = g_groups["early"]
        names_ = [label(k) for k in keys]
        split, swap_pushes = reduce_swap([slots(k, G) for k in keys])

        def swapped(theirs):
            chip_sum, pushes = reduce_sum(names_, split, theirs, wire(keys))
            return pushes, lambda landed: part.update(zip(keys, reduce_end("early", names_, chip_sum, landed)))

        return swap_pushes, swapped

    loss_part, dx, G = _forward_backward(x[0], loss_target[0], W, grads_early)
    loss = lax.psum(loss_part[0, 0], ("x", "y", "c"))
    keys = g_groups["late"]
    chip_sum, pushes = reduce_begin("late", [label(k) for k in keys], [slots(k, G) for k in keys], wire(keys))
    part.update(zip(keys, reduce_end("late", [label(k) for k in keys], chip_sum, exchange("scatter_grads", pushes))))

    rep = jnp.concatenate([G[("mix_norm", 0)][0] + G[("mix_norm", 0)][1] + G[("mix_norm", 0)][2] + G[("mix_norm", 0)][3],
                           G[("mix_norm", 1)]], axis=0).reshape(16, 128)
    rep = jnp.concatenate([rep, G["rel_bias"], jnp.pad(G["attn_q_norm"], ((0, 0), (0, 64))),
                           jnp.pad(G["attn_k_norm"], ((0, 0), (0, 64))), G["rw_rk"].reshape(8, 128),
                           jnp.zeros((2, 128), F32)], axis=0)
    rep_sum = sum_slots("sum_replicated", gather_all([rep])[0])
    g_rep = {
        "mix_norm": rep_sum[0:16].reshape(2, D),
        "rel_bias": jnp.transpose(rep_sum[16:28, :N_BUCKETS]),
        "attn_q_norm": rep_sum[28:29, :HEAD], "attn_k_norm": rep_sum[29:30, :HEAD],
        "rw_rk": rep_sum[30:38].reshape(1, RW_H, HEAD),
    }

    out = {}

    def adam(n, ga, gb):
        shp = w[n].shape
        to2 = lambda a: a.reshape(-1, shp[-1])
        res = adam_step(f"adam_{n}", to2(ga), None if gb is None else to2(gb), to2(w[n]), to2(mom[n]), to2(vel[n]))
        out[n] = tuple(r.reshape(shp) for r in res)

    for n in ffn3:
        out[n] = tuple(adam_ffn(f"adam_{n}", [part[n, l, j] for l in range(2) for j in range(2)], w[n], mom[n], vel[n],
                                transposed=n != "ffn_w_down"))
    for n in ("attn_w_in", "attn_w_out") + rw_mats:
        adam(n, part[n], None)
    rows = {"ffn_norm": (0, 4), "rw_mix": (4, 10), "rw_w0": (10, 11), "rw_a0": (11, 12), "rw_kk": (12, 13),
            "rw_ka": (13, 14), "rw_lnx_g": (14, 15), "rw_lnx_b": (15, 16)}
    for n, (lo, hi) in rows.items():
        adam(n, part["vec"][lo:hi], None)
    for n, gv in g_rep.items():
        adam(n, gv, None)

    grads = [out[n][0] for n in names]
    deltas = [out[n][1] for n in names]
    new_m = [out[n][2] for n in names]
    new_v = [out[n][3] for n in names]
    return (loss, dx[None], *grads, *deltas, *new_m, *new_v)
```

```python
import functools
import math

import jax
import jax.numpy as jnp
from jax import lax
from jax.experimental import pallas as pl
from jax.experimental.pallas import tpu as pltpu

F32, BF16 = jnp.float32, jnp.bfloat16
HI = lax.Precision.HIGHEST
MESH = pl.DeviceIdType.MESH

D = 1024
HEAD = 64
N_CHIPS = 4
FF_SHARD = 704
SB_W = 256
DL_HEADS = 12
DL_PAIRS = 6
DIL = (1, 4, 16)
QBLK = 128
N_BUCKETS = 32
MAX_DISTANCE = 2048
RW_H = 16
RW_CHUNK = 64
NORM_EPS = 1e-6
GN_EPS = 64e-5
NEG_INF = -1e30
VMEM_LIMIT = 56 * 1024 * 1024

ADAM_LR, ADAM_B1, ADAM_B2, ADAM_EPS, ADAM_WD, ADAM_STEP = 0.001, 0.9, 0.999, 1e-08, 0.01, 10


def _cp(sem):
    return pltpu.CompilerParams(dimension_semantics=sem, vmem_limit_bytes=VMEM_LIMIT)


def _dg(a, b, dims, prec=None):
    return lax.dot_general(a, b, (dims, ((), ())), precision=prec, preferred_element_type=F32)


def _bdot(a, b, dims):
    return _dg(a.astype(BF16), b.astype(BF16), dims)


@jax.custom_vjp
def mm(a, b):
    return _bdot(a, b, ((1,), (0,)))


def _mm_fwd(a, b):
    return _bdot(a, b, ((1,), (0,))), (a, b)


def _mm_bwd(res, g):
    a, b = res
    return _bdot(g, b, ((1,), (1,))), _bdot(a, g, ((0,), (0,)))


mm.defvjp(_mm_fwd, _mm_bwd)


def rms(x, g):
    return x * lax.rsqrt(jnp.mean(x * x, axis=-1, keepdims=True) + NORM_EPS) * g


def _pieces(x):
    x1 = x.astype(BF16)
    r1 = x - x1.astype(F32)
    x2 = r1.astype(BF16)
    return jnp.concatenate([x1, x2, (r1 - x2.astype(F32)).astype(BF16)], axis=-1)


def _group_sum(x, nh):
    w = x.shape[-1]
    e = (lax.broadcasted_iota(jnp.int32, (w, nh), 0) // HEAD == lax.broadcasted_iota(jnp.int32, (w, nh), 1)).astype(BF16)
    s = _dg(_pieces(x), jnp.concatenate([e, e, e], axis=0), ((1,), (0,)))
    return _dg(_pieces(s), jnp.concatenate([e, e, e], axis=1), ((1,), (1,)))


@functools.partial(jax.custom_vjp, nondiff_argnums=(1,))
def group_sum(x, nh):
    return _group_sum(x, nh)


group_sum.defvjp(lambda x, nh: (_group_sum(x, nh), None), lambda nh, _, g: (_group_sum(g, nh),))


def softplus(u):
    return jnp.maximum(u, 0.0) + jnp.log1p(jnp.exp(-jnp.abs(u)))


def to_heads(t, nh=RW_H):
    return jnp.stack([t[:, HEAD * h:HEAD * (h + 1)] for h in range(nh)])


def from_heads(t):
    return jnp.concatenate([t[h] for h in range(t.shape[0])], axis=-1)


def _tile_spec(shape, tm):
    if len(shape) == 2:
        return pl.BlockSpec((tm, shape[1]), lambda t: (t, 0))
    return pl.BlockSpec((shape[0], tm, shape[2]), lambda t: (0, t, 0))


def _full_spec(shape):
    nd = len(shape)
    return pl.BlockSpec(tuple(shape), lambda t: (0,) * nd)


def _rows(a):
    return a.shape[0] if a.ndim == 2 else a.shape[1]


def tile_fwd(f, name, tiles, weights, outs, tm):
    nt, nw = len(tiles), len(weights)

    def body(*refs):
        tv = [r[...] for r in refs[:nt]]
        wv = [r[...].astype(F32) for r in refs[nt:nt + nw]]
        res = f(*tv, *wv)
        if not isinstance(res, (tuple, list)):
            res = (res,)
        for o, v in zip(refs[nt + nw:], res):
            o[...] = v.astype(o.dtype)

    return pl.pallas_call(
        body, name=name, grid=(_rows(tiles[0]) // tm,),
        in_specs=[_tile_spec(a.shape, tm) for a in tiles] + [_full_spec(w.shape) for w in weights],
        out_specs=[_tile_spec(o.shape, tm) for o in outs],
        out_shape=list(outs),
        compiler_params=_cp(("parallel",)),
    )(*tiles, *weights)


def tile_bwd(f, name, tiles, weights, cts, tm, dt, dw, acc=None):
    acc = acc or {}
    groups = [c if isinstance(c, tuple) else (c,) for c in cts]
    cts = [a for grp in groups for a in grp]
    nt, nw, nc = len(tiles), len(weights), len(cts)
    acc_idx = sorted(acc)
    na = len(acc_idx)
    dti = [i for i in range(nt) if dt[i]]
    dwi = [i for i in range(nw) if dw[i]]

    def body(*refs):
        tv = [r[...] for r in refs[:nt]]
        wv = [r[...].astype(F32) for r in refs[nt:nt + nw]]
        crefs = list(refs[nt + nw:nt + nw + nc])
        cv = []
        for grp in groups:
            terms = [crefs.pop(0)[...] for _ in grp]
            cv.append(functools.reduce(lambda a, b: a + b, terms))
        av = {i: r[...] for i, r in zip(acc_idx, refs[nt + nw + nc:nt + nw + nc + na])}
        orefs = refs[nt + nw + nc + na:]

        def g(*diff):
            t2, w2 = list(tv), list(wv)
            for i, v in zip(dti, diff[:len(dti)]):
                t2[i] = v
            for i, v in zip(dwi, diff[len(dti):]):
                w2[i] = v
            res = f(*t2, *w2)
            return tuple(res) if isinstance(res, (tuple, list)) else (res,)

        _, vjp = jax.vjp(g, *[tv[i] for i in dti], *[wv[i] for i in dwi])
        grads = vjp(tuple(cv))
        for k, i in enumerate(dti):
            gt = grads[k]
            if i in av:
                gt = gt + av[i]
            orefs[k][...] = gt
        first = pl.program_id(0) == 0
        for k, i in enumerate(dwi):
            o = orefs[len(dti) + k]
            gw = grads[len(dti) + k]

            @pl.when(first)
            def _(o=o, gw=gw):
                o[...] = gw

            @pl.when(jnp.logical_not(first))
            def _(o=o, gw=gw):
                o[...] += gw

    out_shape = [jax.ShapeDtypeStruct(tiles[i].shape, F32) for i in dti] + \
                [jax.ShapeDtypeStruct(weights[i].shape, F32) for i in dwi]
    res = pl.pallas_call(
        body, name=name, grid=(_rows(tiles[0]) // tm,),
        in_specs=[_tile_spec(a.shape, tm) for a in tiles] + [_full_spec(w.shape) for w in weights] +
                 [_tile_spec(c.shape, tm) for c in cts] + [_tile_spec(tiles[i].shape, tm) for i in acc_idx],
        out_specs=[_tile_spec(tiles[i].shape, tm) for i in dti] + [_full_spec(weights[i].shape) for i in dwi],
        out_shape=out_shape,
        compiler_params=_cp(("arbitrary",)),
    )(*tiles, *weights, *cts, *[acc[i] for i in acc_idx])
    return list(res[:len(dti)]), list(res[len(dti):])


def _ffn_wspec(rows, cols, cfirst):
    if cfirst:
        return pl.BlockSpec((1, rows, cols), lambda c, t: (c, 0, 0))
    return pl.BlockSpec((1, rows, cols), lambda t, c: (c, 0, 0))


def ffn_fwd(x, g, wg, wu, wd, l, j, tm=512):
    S = x.shape[0]

    def body(x_ref, g_ref, wg_ref, wu_ref, wd_ref, o_ref, a_ref, b_ref, h_ref, acc_ref):
        c = pl.program_id(1)

        @pl.when(c == 0)
        def _():
            h_ref[...] = rms(x_ref[...], g_ref[...]).astype(BF16)
            acc_ref[...] = jnp.zeros_like(acc_ref)

        h = h_ref[...]
        a = _bdot(h, wg_ref[0], ((1,), (0,)))
        b = _bdot(h, wu_ref[0], ((1,), (0,)))
        a_ref[0] = a.astype(BF16)
        b_ref[0] = b.astype(BF16)
        y = a * jax.nn.sigmoid(a) * b
        acc_ref[...] += _bdot(y, wd_ref[0], ((1,), (0,)))

        @pl.when(c == N_CHIPS - 1)
        def _():
            o_ref[...] = x_ref[...] + 0.5 * acc_ref[...]

    hid = pl.BlockSpec((1, tm, FF_SHARD), lambda t, c: (c, t, 0))
    return pl.pallas_call(
        body, name=f"ffn_fwd_{l}{j}", grid=(S // tm, N_CHIPS),
        in_specs=[pl.BlockSpec((tm, D), lambda t, c: (t, 0)), pl.BlockSpec((1, D), lambda t, c: (0, 0)),
                  _ffn_wspec(D, FF_SHARD, False), _ffn_wspec(D, FF_SHARD, False), _ffn_wspec(FF_SHARD, D, False)],
        out_specs=[pl.BlockSpec((tm, D), lambda t, c: (t, 0)), hid, hid],
        out_shape=[jax.ShapeDtypeStruct((S, D), F32)] + [jax.ShapeDtypeStruct((N_CHIPS, S, FF_SHARD), BF16)] * 2,
        scratch_shapes=[pltpu.VMEM((tm, D), BF16), pltpu.VMEM((tm, D), F32)],
        compiler_params=_cp(("parallel", "arbitrary")),
    )(x, g, wg, wu, wd)


def ffn_bwd(x, g, wg, wu, wd, dout, a_sav, b_sav, l, j, tm=512):
    S = x.shape[0]

    def body(x_ref, g_ref, wg_ref, wu_ref, wd_ref, do_ref, a_ref, b_ref, dh_ref, dwg_ref, dwu_ref, dwd_ref):
        t = pl.program_id(1)
        h = rms(x_ref[...], g_ref[...]).astype(BF16)
        wgv, wuv, wdv = wg_ref[0], wu_ref[0], wd_ref[0]
        a = a_ref[0].astype(F32)
        b = b_ref[0].astype(F32)
        sig = jax.nn.sigmoid(a)
        s = a * sig
        dyd = 0.5 * do_ref[...]
        dy = _bdot(dyd, wdv, ((1,), (1,)))
        dwd = _bdot(s * b, dyd, ((0,), (0,)))
        db = dy * s
        da = dy * b * (sig * (1.0 + a * (1.0 - sig)))
        dwg = _bdot(da, h, ((0,), (0,)))
        dwu = _bdot(db, h, ((0,), (0,)))
        dh_ref[0] = (_bdot(da, wgv, ((1,), (1,))) + _bdot(db, wuv, ((1,), (1,)))).astype(dh_ref.dtype)

        @pl.when(t == 0)
        def _():
            dwg_ref[0] = dwg
            dwu_ref[0] = dwu
            dwd_ref[0] = dwd

        @pl.when(t != 0)
        def _():
            dwg_ref[0] += dwg
            dwu_ref[0] += dwu
            dwd_ref[0] += dwd

    return pl.pallas_call(
        body, name=f"ffn_bwd_{l}{j}", grid=(N_CHIPS, S // tm),
        in_specs=[pl.BlockSpec((tm, D), lambda c, t: (t, 0)), pl.BlockSpec((1, D), lambda c, t: (0, 0)),
                  _ffn_wspec(D, FF_SHARD, True), _ffn_wspec(D, FF_SHARD, True), _ffn_wspec(FF_SHARD, D, True),
                  pl.BlockSpec((tm, D), lambda c, t: (t, 0)),
                  pl.BlockSpec((1, tm, FF_SHARD), lambda c, t: (c, t, 0)), pl.BlockSpec((1, tm, FF_SHARD), lambda c, t: (c, t, 0))],
        out_specs=[pl.BlockSpec((1, tm, D), lambda c, t: (c, t, 0))] + [_ffn_wspec(FF_SHARD, D, True)] * 3,
        out_shape=[jax.ShapeDtypeStruct((N_CHIPS, S, D), BF16)] + [jax.ShapeDtypeStruct(wd.shape, F32)] * 3,
        compiler_params=_cp(("parallel", "arbitrary")),
    )(x, g, wg, wu, wd, dout, a_sav, b_sav)


def norm_bwd(name, x, g, dh_parts, dres, tm=256):
    S = x.shape[0]
    P = dh_parts.shape[0]

    def body(x_ref, g_ref, dh_ref, dr_ref, dx_ref, dg_ref):
        dh = dh_ref[0].astype(F32)
        for p in range(1, P):
            dh = dh + dh_ref[p].astype(F32)
        _, vjp = jax.vjp(rms, x_ref[...], g_ref[...])
        dx, dg = vjp(dh)
        dx_ref[...] = dr_ref[...] + dx

        @pl.when(pl.program_id(0) == 0)
        def _():
            dg_ref[...] = dg

        @pl.when(pl.program_id(0) != 0)
        def _():
            dg_ref[...] += dg

    return pl.pallas_call(
        body, name=name, grid=(S // tm,),
        in_specs=[pl.BlockSpec((tm, D), lambda t: (t, 0)), pl.BlockSpec((1, D), lambda t: (0, 0)),
                  pl.BlockSpec((P, tm, D), lambda t: (0, t, 0)), pl.BlockSpec((tm, D), lambda t: (t, 0))],
        out_specs=[pl.BlockSpec((tm, D), lambda t: (t, 0)), pl.BlockSpec((1, D), lambda t: (0, 0))],
        out_shape=[jax.ShapeDtypeStruct((S, D), F32), jax.ShapeDtypeStruct((1, D), F32)],
        compiler_params=_cp(("arbitrary",)),
    )(x, g, dh_parts, dres)


def f_attn_sb(x, g, w):
    pr = mm(rms(x, g), w)
    return pr[:, :SB_W], pr[:, SB_W:2 * SB_W], pr[:, 2 * SB_W:]


def _pairs(y):
    return jnp.stack([y[:, 128 * j:128 * (j + 1)] for j in range(DL_PAIRS)])


def f_attn_qk(x, g, w, nrm):
    pr = mm(rms(x, g), w)
    ms = group_sum(pr * pr, DL_HEADS) * (1.0 / HEAD)
    return _pairs(pr * lax.rsqrt(ms + NORM_EPS) * jnp.concatenate([nrm] * DL_HEADS, axis=1))


def f_attn_v(x, g, w):
    return _pairs(mm(rms(x, g), w))


def _masked(strict, x):
    return x if strict is None else jnp.where(strict, x, 0.0)


def _head_stack(x):
    nh = x.shape[1] // HEAD
    lane_head = lax.broadcasted_iota(jnp.int32, (1, x.shape[1]), 1) // HEAD
    return jnp.concatenate([jnp.where(lane_head == h, x, 0.0) for h in range(nh)], axis=0).astype(BF16)


def _head_pick(xs):
    nh = xs.shape[1] // HEAD
    rows = xs.shape[0] // nh
    lane_head = lax.broadcasted_iota(jnp.int32, (1, xs.shape[1]), 1) // HEAD
    out = xs[:rows]
    for h in range(1, nh):
        out = jnp.where(lane_head == h, xs[rows * h:rows * (h + 1)], out)
    return out


def _sb_tiles(qs, kblk, strict):
    z = _dg(qs, kblk, ((1,), (1,))) * (HEAD ** -0.5)
    keep = -(jnp.maximum(z, 0.0) + jnp.log(1.0 + jnp.exp(-jnp.abs(z))))
    return z, _masked(strict, keep)


def _tri(n, upper):
    r = lax.broadcasted_iota(jnp.int32, (n, n), 0)
    c = lax.broadcasted_iota(jnp.int32, (n, n), 1)
    return ((r > c) if upper else (r < c)).astype(BF16)


def _tri_sums(x, tri):
    hi, lo = _split2(x)
    return _dg(jnp.concatenate([hi, lo], axis=1), jnp.concatenate([tri, tri], axis=0), ((1,), (0,)))


SB_UNROLL = 4


def _sb_diag(tb, nh):
    r = lax.broadcasted_iota(jnp.int32, (nh * tb, tb), 0)
    return lax.broadcasted_iota(jnp.int32, (nh * tb, tb), 1) < lax.rem(r, tb)


def _sb_sweep(step, first, count, carry, direction, commit=None):
    def run(kbs, c):
        outs = []
        for kb in kbs:
            c, out = step(kb, c)
            outs.append(out)
        if commit is not None:
            for kb, out in zip(kbs, outs):
                commit(kb, out)
        return c

    rem = count % SB_UNROLL
    carry = lax.fori_loop(0, rem, lambda i, c: run([first + direction * i], c), carry)
    return lax.fori_loop(
        0, count // SB_UNROLL,
        lambda g, c: run([first + direction * (rem + SB_UNROLL * g + u) for u in range(SB_UNROLL)], c), carry)


def _riding(ride, refs, n_in, n_out, first, last):
    if ride is None:
        return refs, lambda: None
    n = ride.n
    own = refs[:n_in] + refs[n_in + n:n_in + n + n_out] + refs[n_in + 2 * n + n_out:len(refs) - 2]
    start, wait = ride.ops(refs[n_in:n_in + n], refs[n_in + n + n_out:n_in + 2 * n + n_out], refs[-2], refs[-1])
    pl.when(first)(start)
    return own, lambda: pl.when(last)(wait)


def _ride_specs(ride):
    if ride is None:
        return [], [], [], [], []
    return [_HBM] * ride.n, [_HBM] * ride.n, ride.out_shapes, ride.sem_shapes(), ride.arrays


def sb_fwd(q, k, v, ride=None, tb=QBLK):
    S = q.shape[0]
    nh = SB_W // HEAD
    nb = S // tb
    r_in, r_out, r_shape, r_scr, r_args = _ride_specs(ride)

    def body(*refs):
        qb = pl.program_id(0)
        (q_ref, k_ref, v_ref, o_ref, w_ref), finish = _riding(ride, refs, 3, 2, qb == 0, qb == nb - 1)
        diag = _sb_diag(tb, nh)
        after_mat = _tri(tb, True)
        qs = _head_stack(q_ref[...])

        def step(kb, carry, strict):
            acc, run = carry
            rows = pl.ds(pl.multiple_of(kb * tb, tb), tb)
            z, keep = _sb_tiles(qs, k_ref[rows, :].astype(BF16), strict)
            w = _masked(strict, jnp.exp(z + keep + _tri_sums(keep, after_mat) + run)).astype(BF16)
            w_ref[0, kb] = w
            acc = acc + _dg(w, v_ref[rows, :].astype(BF16), ((1,), (0,)))
            return acc, run + jnp.sum(keep, axis=1, keepdims=True)

        init = (jnp.zeros((nh * tb, SB_W), F32), jnp.zeros((nh * tb, 1), F32))
        carry = step(qb, init, diag)
        acc, _ = _sb_sweep(lambda kb, c: (step(kb, c, None), None), qb - 1, qb, carry, -1)
        o_ref[...] = _head_pick(acc)
        finish()

    return pl.pallas_call(
        body, name="sb_fwd", grid=(S // tb,),
        in_specs=[pl.BlockSpec((tb, SB_W), lambda i: (i, 0)), pl.BlockSpec((S, SB_W), lambda i: (0, 0)),
                  pl.BlockSpec((S, SB_W), lambda i: (0, 0))] + r_in,
        out_specs=[pl.BlockSpec((tb, SB_W), lambda i: (i, 0)),
                   pl.BlockSpec((1, nb, nh * tb, tb), lambda i: (i, 0, 0, 0))] + r_out,
        out_shape=[jax.ShapeDtypeStruct((S, SB_W), F32), jax.ShapeDtypeStruct((nb, nb, nh * tb, tb), BF16)] + r_shape,
        scratch_shapes=r_scr,
        compiler_params=_cp(("arbitrary",)),
    )(q, k, v, *r_args)


def sb_bwd(q, k, v, do, wts, ride=None, tb=QBLK):
    S = q.shape[0]
    nh = SB_W // HEAD
    nb = S // tb
    scale = HEAD ** -0.5
    r_in, r_out, r_shape, r_scr, r_args = _ride_specs(ride)

    def body(*refs):
        qb = pl.program_id(0)
        (q_ref, k_ref, v_ref, do_ref, w_ref, dq_ref, dk_ref, dv_ref, g_scr), finish = _riding(
            ride, refs, 5, 3, qb == 0, qb == nb - 1)

        @pl.when(qb == 0)
        def _():
            dk_ref[...] = jnp.zeros_like(dk_ref)
            dv_ref[...] = jnp.zeros_like(dv_ref)

        diag = _sb_diag(tb, nh)
        before_mat = _tri(tb, False)
        qs = _head_stack(q_ref[...])
        dos = _head_stack(do_ref[...])

        def weights_pass(kb, carry):
            rows = pl.ds(pl.multiple_of(kb * tb, tb), tb)
            w = w_ref[0, kb]
            g_scr[kb] = _dg(dos, v_ref[rows, :].astype(BF16), ((1,), (1,))) * w.astype(F32)
            return carry, _dg(w, dos, ((0,), (0,)))

        def add_rows(ref):
            def commit(kb, val):
                ref[pl.ds(pl.multiple_of(kb * tb, tb), tb), :] += val
            return commit

        zero_run = jnp.zeros((nh * tb, 1), F32)
        _sb_sweep(weights_pass, 0, qb + 1, 0, 1, add_rows(dv_ref))

        def left_to_right(kb, carry, strict):
            dq, run = carry
            rows = pl.ds(pl.multiple_of(kb * tb, tb), tb)
            kblk = k_ref[rows, :].astype(BF16)
            gw = g_scr[kb]
            sig = jax.nn.sigmoid(_dg(qs, kblk, ((1,), (1,))) * scale)
            dkeep = _masked(strict, _tri_sums(gw, before_mat) + run)
            dz = ((gw * (1.0 - sig) - dkeep * sig) * scale).astype(BF16)
            dq = dq + _dg(dz, kblk, ((1,), (0,)))
            return (dq, run + jnp.sum(gw, axis=1, keepdims=True)), _dg(dz, qs, ((0,), (0,)))

        carry = _sb_sweep(lambda kb, c: left_to_right(kb, c, None), 0, qb,
                          (jnp.zeros((nh * tb, SB_W), F32), zero_run), 1, add_rows(dk_ref))
        (dq, _), dk_diag = left_to_right(qb, carry, diag)
        add_rows(dk_ref)(qb, dk_diag)
        dq_ref[...] = _head_pick(dq)
        finish()

    whole = pl.BlockSpec((S, SB_W), lambda i: (0, 0))
    blk = pl.BlockSpec((tb, SB_W), lambda i: (i, 0))
    return pl.pallas_call(
        body, name="sb_bwd", grid=(S // tb,),
        in_specs=[blk, whole, whole, blk, pl.BlockSpec((1, nb, nh * tb, tb), lambda i: (i, 0, 0, 0))] + r_in,
        out_specs=[blk, whole, whole] + r_out,
        out_shape=[jax.ShapeDtypeStruct((S, SB_W), F32)] * 3 + r_shape,
        scratch_shapes=[pltpu.VMEM((S // tb, nh * tb, tb), F32)] + r_scr,
        compiler_params=_cp(("arbitrary",)),
    )(q, k, v, do, wts, *r_args)


def reorder(name, x, groups, inverse):
    P, S, _ = x.shape

    def body(x_ref, o_ref):
        p = pl.program_id(0)
        for gi, r in enumerate(groups):
            @pl.when(p // 2 == gi)
            def _(r=r):
                L = S // r
                if r == 1:
                    o_ref[...] = x_ref[...]
                for c in range(r if r > 1 else 0):
                    if inverse:
                        o_ref[pl.ds(c, L, stride=r), :] = x_ref[c * L:(c + 1) * L, :]
                    else:
                        o_ref[c * L:(c + 1) * L, :] = x_ref[pl.ds(c, L, stride=r), :]

    slab = pl.BlockSpec((None, S, 128), lambda p: (p, 0, 0))
    return pl.pallas_call(
        body, name=name, grid=(P,), in_specs=[slab], out_specs=slab,
        out_shape=jax.ShapeDtypeStruct(x.shape, x.dtype), compiler_params=_cp(("parallel",)),
    )(x)


def _dil_blocks(S):
    return S // QBLK


def _dil_mask(n_in_stream):
    qi = lax.broadcasted_iota(jnp.int32, (QBLK, 2 * QBLK), 0)
    kj = lax.broadcasted_iota(jnp.int32, (QBLK, 2 * QBLK), 1) - QBLK
    dist = qi - kj
    return (dist >= 0) & (dist <= QBLK) & ((n_in_stream > 0) | (kj >= 0))


def _stream_pos(gi, i, S):
    nb = jnp.where(gi == 0, S // (QBLK * DIL[0]), jnp.where(gi == 1, S // (QBLK * DIL[1]), S // (QBLK * DIL[2])))
    return i % nb


def dil_fwd(q, k, v, bias, ride=None):
    S = q.shape[1]
    nblk = _dil_blocks(S)
    r_in, r_out, r_shape, r_scr, r_args = _ride_specs(ride)

    def body(*refs):
        gi, i = pl.program_id(0), pl.program_id(1)
        (q_ref, kc_ref, kp_ref, vc_ref, vp_ref, b_ref, o_ref, l_ref), finish = _riding(
            ride, refs, 6, 2, (gi == 0) & (i == 0), (gi == len(DIL) - 1) & (i == nblk - 1))
        mask = _dil_mask(_stream_pos(gi, i, S))
        for j in range(2):
            q2, kc, kp, vc, vp = q_ref[j], kc_ref[j], kp_ref[j], vc_ref[j], vp_ref[j]
            os_, ls_ = [], []
            for hh in range(2):
                sl = slice(HEAD * hh, HEAD * (hh + 1))
                kw = jnp.concatenate([kp[:, sl], kc[:, sl]], axis=0)
                vw = jnp.concatenate([vp[:, sl], vc[:, sl]], axis=0)
                lg = _bdot(q2[:, sl], kw, ((1,), (1,))) * (HEAD ** -0.5) + b_ref[2 * j + hh]
                lg = jnp.where(mask, lg, NEG_INF)
                m = jnp.max(lg, axis=-1, keepdims=True)
                p = jnp.exp(lg - m)
                den = jnp.sum(p, axis=-1, keepdims=True)
                os_.append(_bdot(p / den, vw, ((1,), (0,))))
                ls_.append(jnp.broadcast_to(m + jnp.log(den), (QBLK, HEAD)))
            o_ref[j] = jnp.concatenate(os_, axis=1)
            l_ref[j] = jnp.concatenate(ls_, axis=1)
        finish()

    cur = pl.BlockSpec((2, QBLK, 128), lambda g, i: (g, i, 0))
    prev = pl.BlockSpec((2, QBLK, 128), lambda g, i: (g, jnp.maximum(i - 1, 0), 0))
    return pl.pallas_call(
        body, name="dil_fwd", grid=(len(DIL), nblk),
        in_specs=[cur, cur, prev, cur, prev, pl.BlockSpec((4, QBLK, 2 * QBLK), lambda g, i: (g, 0, 0))] + r_in,
        out_specs=[cur, cur] + r_out,
        out_shape=[jax.ShapeDtypeStruct(q.shape, F32)] * 2 + r_shape,
        scratch_shapes=r_scr,
        compiler_params=_cp(("arbitrary", "arbitrary")),
    )(q, k, k, v, v, bias, *r_args)


def dil_bwd(q, k, v, bias, o, lse, do, dlse, ride=None):
    S = q.shape[1]
    nblk = _dil_blocks(S)
    r_in, r_out, r_shape, r_scr, r_args = _ride_specs(ride)

    def body(*refs):
        gi, i = pl.program_id(0), pl.program_id(1)
        (q_ref, kc_ref, kp_ref, vc_ref, vp_ref, b_ref, o_ref, l_ref, do_ref, dl_ref,
         dq_ref, dk_ref, dv_ref, ds_ref, dk_car, dv_car), finish = _riding(
            ride, refs, 10, 4, (gi == 0) & (i == 0), (gi == len(DIL) - 1) & (i == nblk))

        @pl.when(i == 0)
        def _():
            ds_ref[...] = jnp.zeros_like(ds_ref)
            dk_car[...] = jnp.zeros_like(dk_car)
            dv_car[...] = jnp.zeros_like(dv_car)

        @pl.when(i < nblk)
        def _():
            mask = _dil_mask(_stream_pos(gi, i, S))
            for j in range(2):
                q2, kc, kp, vc, vp = q_ref[j], kc_ref[j], kp_ref[j], vc_ref[j], vp_ref[j]
                o2, l2, do2, dl2 = o_ref[j], l_ref[j], do_ref[j], dl_ref[j]
                dqs, dkps, dkcs, dvps, dvcs = [], [], [], [], []
                for hh in range(2):
                    sl = slice(HEAD * hh, HEAD * (hh + 1))
                    qh, doh = q2[:, sl], do2[:, sl]
                    kw = jnp.concatenate([kp[:, sl], kc[:, sl]], axis=0)
                    vw = jnp.concatenate([vp[:, sl], vc[:, sl]], axis=0)
                    lg = _bdot(qh, kw, ((1,), (1,))) * (HEAD ** -0.5) + b_ref[2 * j + hh]
                    p = jnp.where(mask, jnp.exp(lg - l2[:, HEAD * hh:HEAD * hh + 1]), 0.0)
                    dp = _bdot(doh, vw, ((1,), (1,)))
                    delta = jnp.sum(doh * o2[:, sl], axis=-1, keepdims=True)
                    dl = jnp.sum(dl2[:, sl], axis=-1, keepdims=True)
                    ds = p * (dp - delta + dl)
                    ds_ref[2 * j + hh] += ds
                    dsq = ds * (HEAD ** -0.5)
                    dqs.append(_bdot(dsq, kw, ((1,), (0,))))
                    dkw = _bdot(dsq, qh, ((0,), (0,)))
                    dvw = _bdot(p, doh, ((0,), (0,)))
                    dkps.append(dkw[:QBLK])
                    dkcs.append(dkw[QBLK:])
                    dvps.append(dvw[:QBLK])
                    dvcs.append(dvw[QBLK:])
                dq_ref[j] = jnp.concatenate(dqs, axis=1)
                dk_ref[j] = dk_car[j] + jnp.concatenate(dkps, axis=1)
                dv_ref[j] = dv_car[j] + jnp.concatenate(dvps, axis=1)
                dk_car[j] = jnp.concatenate(dkcs, axis=1)
                dv_car[j] = jnp.concatenate(dvcs, axis=1)

        @pl.when(i == nblk)
        def _():
            dk_ref[...] = dk_car[...]
            dv_ref[...] = dv_car[...]

        finish()

    cur = pl.BlockSpec((2, QBLK, 128), lambda g, i: (g, jnp.minimum(i, nblk - 1), 0))
    prev = pl.BlockSpec((2, QBLK, 128), lambda g, i: (g, jnp.clip(i - 1, 0, nblk - 1), 0))
    bspec = pl.BlockSpec((4, QBLK, 2 * QBLK), lambda g, i: (g, 0, 0))
    return pl.pallas_call(
        body, name="dil_bwd", grid=(len(DIL), nblk + 1),
        in_specs=[cur, cur, prev, cur, prev, bspec, cur, cur, cur, cur] + r_in,
        out_specs=[cur, prev, prev, bspec] + r_out,
        out_shape=[jax.ShapeDtypeStruct(q.shape, F32)] * 3 + [jax.ShapeDtypeStruct(bias.shape, F32)] + r_shape,
        scratch_shapes=[pltpu.VMEM((2, QBLK, 128), F32), pltpu.VMEM((2, QBLK, 128), F32)] + r_scr,
        compiler_params=_cp(("arbitrary", "arbitrary")),
    )(q, k, k, v, v, bias, o, lse, do, dlse, *r_args)


def _t5_bucket(dist):
    max_exact = N_BUCKETS // 2
    d = jnp.maximum(dist, 1).astype(F32)
    large = max_exact + (jnp.log(d / max_exact) / math.log(MAX_DISTANCE / max_exact)
                         * (N_BUCKETS - max_exact)).astype(jnp.int32)
    large = jnp.minimum(large, N_BUCKETS - 1)
    return jnp.where(dist < max_exact, dist, large)


def _bucket_maps():
    qi = jnp.arange(QBLK)[:, None]
    kj = jnp.arange(2 * QBLK)[None, :] - QBLK
    dist = jnp.maximum(qi - kj, 0)
    return jnp.stack([_t5_bucket(dist * r) for r in DIL])


def bias_table(rel_bias, buckets):
    def body(tbl_ref, bk_ref, o_ref):
        for h in range(DL_HEADS):
            bk = bk_ref[h // 4]

            def step(b, acc):
                return jnp.where(bk == b, tbl_ref[b, h], acc)

            o_ref[h] = lax.fori_loop(0, N_BUCKETS, step, jnp.zeros(bk.shape, F32))

    return pl.pallas_call(
        body, name="bias_table", out_shape=jax.ShapeDtypeStruct((DL_HEADS,) + buckets.shape[1:], F32),
        in_specs=[pl.BlockSpec(memory_space=pltpu.SMEM), pl.BlockSpec(memory_space=pltpu.VMEM)],
        out_specs=pl.BlockSpec(memory_space=pltpu.VMEM),
    )(rel_bias, buckets)


def bias_grad(ds, buckets):
    def body(ds_ref, bk_ref, o_ref):
        lane = lax.broadcasted_iota(jnp.int32, (1, 128), 1)
        for h in range(DL_HEADS):
            dsv = ds_ref[h]
            bk = bk_ref[h // 4]

            def step(b, row):
                return jnp.where(lane == b, jnp.sum(jnp.where(bk == b, dsv, 0.0)), row)

            o_ref[h:h + 1, :] = lax.fori_loop(0, N_BUCKETS, step, jnp.zeros((1, 128), F32))

    return pl.pallas_call(
        body, name="bias_grad", out_shape=jax.ShapeDtypeStruct((DL_HEADS, 128), F32),
        in_specs=[pl.BlockSpec(memory_space=pltpu.VMEM)] * 2, out_specs=pl.BlockSpec(memory_space=pltpu.VMEM),
    )(ds, buckets)


def f_attn_out(x, oa, o, lse, w):
    og = [jnp.concatenate([o[2 * g], o[2 * g + 1]], axis=1) for g in range(3)]
    lg = [jnp.concatenate([lse[2 * g], lse[2 * g + 1]], axis=1) for g in range(3)]
    m = jnp.maximum(jnp.maximum(lg[0], lg[1]), lg[2])
    e = [jnp.exp(l - m) for l in lg]
    den = e[0] + e[1] + e[2]
    ob = (e[0] * og[0] + e[1] * og[1] + e[2] * og[2]) / den
    return x + mm(jnp.concatenate([oa, ob], axis=1), w)


def norm_shift_fwd(x, g, tm=256):
    S = x.shape[0]

    def body(x_ref, xp_ref, g_ref, h_ref, hs_ref):
        h = rms(x_ref[...], g_ref[...])
        hp = rms(xp_ref[7:8, :], g_ref[...])
        hp = jnp.where(pl.program_id(0) == 0, 0.0, hp)
        row = lax.broadcasted_iota(jnp.int32, (tm, D), 0)
        h_ref[...] = h
        hs_ref[...] = jnp.where(row == 0, hp, pltpu.roll(h, 1, 0))

    return pl.pallas_call(
        body, name="rw_norm_shift", grid=(S // tm,),
        in_specs=[pl.BlockSpec((tm, D), lambda t: (t, 0)),
                  pl.BlockSpec((8, D), lambda t: (jnp.maximum(t * (tm // 8) - 1, 0), 0)),
                  pl.BlockSpec((1, D), lambda t: (0, 0))],
        out_specs=[pl.BlockSpec((tm, D), lambda t: (t, 0))] * 2,
        out_shape=[jax.ShapeDtypeStruct((S, D), F32)] * 2,
        compiler_params=_cp(("parallel",)),
    )(x, x, g)


def norm_shift_bwd(x, g, dh, dhs, dres, tm=256):
    S = x.shape[0]
    nt = S // tm

    def body(x_ref, g_ref, dh_ref, dhs_ref, dhn_ref, dr_ref, dx_ref, dg_ref):
        t = pl.program_id(0)
        nxt = jnp.where(t == nt - 1, 0.0, dhn_ref[0:1, :])
        row = lax.broadcasted_iota(jnp.int32, (tm, D), 0)
        tot = dh_ref[...] + jnp.where(row == tm - 1, nxt, pltpu.roll(dhs_ref[...], tm - 1, 0))
        _, vjp = jax.vjp(rms, x_ref[...], g_ref[...])
        dx, dg = vjp(tot)
        dx_ref[...] = dr_ref[...] + dx

        @pl.when(t == 0)
        def _():
            dg_ref[...] = dg

        @pl.when(t != 0)
        def _():
            dg_ref[...] += dg

    tile = pl.BlockSpec((tm, D), lambda t: (t, 0))
    return pl.pallas_call(
        body, name="rw_norm_shift_bwd", grid=(nt,),
        in_specs=[tile, pl.BlockSpec((1, D), lambda t: (0, 0)), tile, tile,
                  pl.BlockSpec((8, D), lambda t: (jnp.minimum((t + 1) * (tm // 8), S // 8 - 1), 0)), tile],
        out_specs=[tile, pl.BlockSpec((1, D), lambda t: (0, 0))],
        out_shape=[jax.ShapeDtypeStruct((S, D), F32), jax.ShapeDtypeStruct((1, D), F32)],
        compiler_params=_cp(("arbitrary",)),
    )(x, g, dh, dhs, dhs, dres)


def f_rw_proj(h, hs, mix, w):
    return mm(h + (hs - h) * mix, w)


def f_rw_mid(h, hs, r, k, v, mix3, w0, a0, kkw, kaw, w1, w2, a1, a2, g1, g2):
    xx = hs - h
    xw, xa, xg = h + xx * mix3[0:1], h + xx * mix3[1:2], h + xx * mix3[2:3]
    w_log = -softplus(-(w0 + mm(jnp.tanh(mm(xw, w1)), w2))) - 0.5
    lw = -jnp.exp(w_log)
    ag = jax.nn.sigmoid(a0 + mm(mm(xa, a1), a2))
    gate = mm(jax.nn.sigmoid(mm(xg, g1)), g2)
    kk = k * kkw
    kk = kk / jnp.maximum(jnp.sqrt(group_sum(kk * kk, RW_H)), 1e-12)
    kmod = k * (1.0 + (ag - 1.0) * kaw)
    return (to_heads(r), to_heads(lw), to_heads(kmod), to_heads(v), to_heads(-kk), to_heads(kk * ag), gate)


def f_rw_post(yh, rh, kh, vh, gate, x, lng, lnb, rk, wo):
    mu = jnp.mean(yh, axis=-1, keepdims=True)
    var = jnp.mean(jnp.square(yh - mu), axis=-1, keepdims=True)
    yn = (yh - mu) * lax.rsqrt(var + GN_EPS)
    bonus = jnp.sum(rh * kh * rk, axis=-1, keepdims=True) * vh
    y = from_heads(yn) * lng + lnb + from_heads(bonus)
    return x + mm(y * gate, wo)


def _split2(x):
    hi = x.astype(BF16)
    return hi, (x - hi.astype(F32)).astype(BF16)


def _b3(x, y, cx, cy):
    xh, xl = _split2(x)
    yh, yl = _split2(y)
    x3 = jnp.concatenate([xh, xh, xl], axis=cx)
    y3 = jnp.concatenate([yh, yl, yh], axis=cy)
    return lax.dot_general(x3, y3, (((cx,), (cy,)), ((0,), (0,))), preferred_element_type=F32)


@jax.custom_vjp
def b_nt(x, y):
    return _b3(x, y, 2, 2)


@jax.custom_vjp
def b_nn(x, y):
    return _b3(x, y, 2, 1)


@jax.custom_vjp
def b_tn(x, y):
    return _b3(x, y, 1, 1)


def _b1(x, y, cx, cy):
    return lax.dot_general(x.astype(BF16), y.astype(BF16), (((cx,), (cy,)), ((0,), (0,))), preferred_element_type=F32)


b_nt.defvjp(lambda x, y: (b_nt(x, y), (x, y)), lambda r, g: (_b1(g, r[1], 2, 1), _b1(g, r[0], 1, 1)))
b_nn.defvjp(lambda x, y: (b_nn(x, y), (x, y)), lambda r, g: (_b1(g, r[1], 2, 2), _b1(r[0], g, 1, 1)))
b_tn.defvjp(lambda x, y: (b_tn(x, y), (x, y)), lambda r, g: (_b1(r[1], g, 2, 2), _b1(r[0], g, 2, 1)))


def _tri_apply(x, lower):
    H, C, _ = x.shape
    ii = lax.broadcasted_iota(jnp.int32, (C, C), 0)
    jj = lax.broadcasted_iota(jnp.int32, (C, C), 1)
    m = jnp.broadcast_to(((jj <= ii) if lower else (jj >= ii)).astype(BF16), (H, C, C))
    x1 = x.astype(BF16)
    r1 = x - x1.astype(F32)
    x2 = r1.astype(BF16)
    x3 = (r1 - x2.astype(F32)).astype(BF16)
    return lax.dot_general(jnp.concatenate([m, m, m], axis=2), jnp.concatenate([x1, x2, x3], axis=1),
                           (((2,), (1,)), ((0,), (0,))), preferred_element_type=F32)


@jax.custom_vjp
def run_sum(x):
    return _tri_apply(x, True)


run_sum.defvjp(lambda x: (run_sum(x), None), lambda _, g: (_tri_apply(g, False),))


def rwkv_chunk(S0, r, lw, k, v, a, b):
    H, C, _ = r.shape
    V = S0.shape[1]
    ii = lax.broadcasted_iota(jnp.int32, (C, C), 0)
    jj = lax.broadcasted_iota(jnp.int32, (C, C), 1)
    strict = jj < ii
    i2 = lax.broadcasted_iota(jnp.int32, (C, 2 * C), 0)
    j2 = lax.broadcasted_iota(jnp.int32, (C, 2 * C), 1)
    incl2 = jnp.where(j2 >= C, j2 - C, j2) <= i2
    g = run_sum(lw)
    ig = jnp.exp(-g)
    ar = jnp.concatenate([a * jnp.exp(g - lw), r * jnp.exp(g)], axis=1)
    bk = jnp.concatenate([b * ig, k * ig], axis=1)
    m = b_nt(ar, bk)
    a_ab = jnp.where(strict, m[:, :C, :C], 0.0)
    a_ak = jnp.where(strict, m[:, :C, C:], 0.0)
    b_r = jnp.where(incl2, m[:, C:, :], 0.0)
    p = b_nt(ar, S0)
    u = p[:, :C] + b_nn(a_ak, v)
    nmat, n = a_ab, 1
    while n < C:
        n *= 2
        if n < C:
            z = b_nn(nmat, jnp.concatenate([u, nmat], axis=2))
            u, nmat = u + z[:, :, :V], z[:, :, V:]
        else:
            u = u + b_nn(nmat, u)
    uv = jnp.concatenate([u, v], axis=1)
    y = p[:, C:] + b_nn(b_r, uv)
    g_end = g[:, C - 1:C, :]
    dec = jnp.exp(g_end - g)
    s_new = S0 * jnp.exp(g_end) + b_tn(uv, jnp.concatenate([b * dec, k * dec], axis=1))
    return y, s_new


def rwkv_fwd(r, lw, k, v, a, b):
    H, S, _ = r.shape
    C = RW_CHUNK

    def body(r_ref, lw_ref, k_ref, v_ref, a_ref, b_ref, y_ref, s_ref, s_scr):
        @pl.when(pl.program_id(0) == 0)
        def _():
            s_scr[...] = jnp.zeros_like(s_scr)

        s0 = s_scr[...]
        s_ref[0] = s0
        y, s1 = rwkv_chunk(s0, r_ref[...], lw_ref[...], k_ref[...], v_ref[...], a_ref[...], b_ref[...])
        y_ref[...] = y
        s_scr[...] = s1

    bs = pl.BlockSpec((H, C, HEAD), lambda c: (0, c, 0))
    return pl.pallas_call(
        body, name="rwkv_fwd", grid=(S // C,), in_specs=[bs] * 6,
        out_specs=[bs, pl.BlockSpec((1, H, HEAD, HEAD), lambda c: (c, 0, 0, 0))],
        out_shape=[jax.ShapeDtypeStruct((H, S, HEAD), F32), jax.ShapeDtypeStruct((S // C, H, HEAD, HEAD), F32)],
        scratch_shapes=[pltpu.VMEM((H, HEAD, HEAD), F32)],
        compiler_params=_cp(("arbitrary",)),
    )(r, lw, k, v, a, b)


def rwkv_bwd(r, lw, k, v, a, b, states, dy):
    H, S, _ = r.shape
    C = RW_CHUNK
    nc = S // C

    def body(r_ref, lw_ref, k_ref, v_ref, a_ref, b_ref, s_ref, dy_ref, dr, dlw, dk, dv, da, db, ds_scr):
        @pl.when(pl.program_id(0) == 0)
        def _():
            ds_scr[...] = jnp.zeros_like(ds_scr)

        _, vjp = jax.vjp(rwkv_chunk, s_ref[0], r_ref[...], lw_ref[...], k_ref[...], v_ref[...], a_ref[...], b_ref[...])
        grads = vjp((dy_ref[...], ds_scr[...]))
        ds_scr[...] = grads[0]
        for o, gv in zip((dr, dlw, dk, dv, da, db), grads[1:]):
            o[...] = gv

    bs = pl.BlockSpec((H, C, HEAD), lambda c: (0, nc - 1 - c, 0))
    return pl.pallas_call(
        body, name="rwkv_bwd", grid=(nc,),
        in_specs=[bs] * 6 + [pl.BlockSpec((1, H, HEAD, HEAD), lambda c: (nc - 1 - c, 0, 0, 0)), bs],
        out_specs=[bs] * 6, out_shape=[jax.ShapeDtypeStruct((H, S, HEAD), F32)] * 6,
        scratch_shapes=[pltpu.VMEM((H, HEAD, HEAD), F32)],
        compiler_params=_cp(("arbitrary",)),
    )(r, lw, k, v, a, b, states, dy)


def loss_head(y, target, tm=512):
    S = y.shape[0]

    def body(y_ref, t_ref, dy_ref, l_ref):
        e = y_ref[...] - t_ref[...]
        dy_ref[...] = e * (1.0 / D)
        part = jnp.broadcast_to(0.5 * jnp.sum(jnp.mean(e * e, axis=-1, keepdims=True)), (1, 128))

        @pl.when(pl.program_id(0) == 0)
        def _():
            l_ref[...] = part

        @pl.when(pl.program_id(0) != 0)
        def _():
            l_ref[...] += part

    tile = pl.BlockSpec((tm, D), lambda t: (t, 0))
    return pl.pallas_call(
        body, name="loss_head", grid=(S // tm,), in_specs=[tile, tile],
        out_specs=[tile, pl.BlockSpec((1, 128), lambda t: (0, 0))],
        out_shape=[jax.ShapeDtypeStruct((S, D), F32), jax.ShapeDtypeStruct((1, 128), F32)],
        compiler_params=_cp(("arbitrary",)),
    )(y, target)


def _row_tile(rows, cols, budget=1 << 19):
    best = None
    for tr in range(8, rows + 1, 8):
        if rows % tr == 0 and tr * cols <= budget:
            best = tr
    return best or rows


def _adam(w, g, m, v):
    m = ADAM_B1 * m + (1.0 - ADAM_B1) * g
    v = ADAM_B2 * v + (1.0 - ADAM_B2) * jnp.square(g)
    m_hat = m / (1.0 - ADAM_B1 ** ADAM_STEP)
    v_hat = v / (1.0 - ADAM_B2 ** ADAM_STEP)
    return -ADAM_LR * (m_hat / (jnp.sqrt(v_hat) + ADAM_EPS) + ADAM_WD * w), m, v


def sum_slots(name, parts, dtype=F32, extras=()):
    n = 0 if parts is None else parts.shape[0]
    R, C = extras[0].shape if parts is None else parts.shape[1:]
    tr = _row_tile(R, C * (n + len(extras)))
    ins = ([] if parts is None else [parts]) + list(extras)

    def body(*refs):
        terms = [] if parts is None else [refs[0][i] for i in range(n)]
        terms += [r[...] for r in refs[len(ins) - len(extras):len(ins)]]
        s = terms[0].astype(F32)
        for t in terms[1:]:
            s = s + t.astype(F32)
        refs[len(ins)][...] = s.astype(dtype)

    tile = pl.BlockSpec((tr, C), lambda t: (t, 0))
    return pl.pallas_call(
        body, name=name, grid=(R // tr,),
        in_specs=([] if parts is None else [pl.BlockSpec((n, tr, C), lambda t: (0, t, 0))]) + [tile] * len(extras),
        out_specs=tile, out_shape=jax.ShapeDtypeStruct((R, C), dtype), compiler_params=_cp(("parallel",)),
    )(*ins)


def sum_own_half(name, split, theirs, c, dtype):
    nq, _, rh, cols = split.shape
    tr = _row_tile(rh, 2 * cols)

    def body(c_ref, a_ref, b_ref, o_ref):
        o_ref[...] = (a_ref[...] + b_ref[...]).astype(dtype)

    tile = pl.BlockSpec((None, tr, cols), lambda q, t, c_ref: (q, t, 0))
    return pl.pallas_call(
        body, name=name,
        grid_spec=pltpu.PrefetchScalarGridSpec(
            num_scalar_prefetch=1, grid=(nq, rh // tr),
            in_specs=[pl.BlockSpec((None, None, tr, cols), lambda q, t, c_ref: (q, c_ref[0], t, 0)), tile],
            out_specs=tile),
        out_shape=jax.ShapeDtypeStruct((nq, rh, cols), dtype), compiler_params=_cp(("parallel", "parallel")),
    )(jnp.reshape(c, (1,)).astype(jnp.int32), split, theirs)


def sum_landed(name, landed, chip_sum, p):
    n, rh, cols = landed.shape
    tr = _row_tile(rh, (n + 1) * cols)

    def body(p_ref, l_ref, own_ref, o_ref):
        s = l_ref[0].astype(F32)
        for i in range(1, n):
            s = s + l_ref[i].astype(F32)
        o_ref[...] = s + own_ref[...].astype(F32)

    return pl.pallas_call(
        body, name=name,
        grid_spec=pltpu.PrefetchScalarGridSpec(
            num_scalar_prefetch=1, grid=(rh // tr,),
            in_specs=[pl.BlockSpec((n, tr, cols), lambda t, p_ref: (0, t, 0)),
                      pl.BlockSpec((None, tr, cols), lambda t, p_ref: (p_ref[0], t, 0))],
            out_specs=pl.BlockSpec((tr, cols), lambda t, p_ref: (t, 0))),
        out_shape=jax.ShapeDtypeStruct((rh, cols), F32), compiler_params=_cp(("parallel",)),
    )(jnp.reshape(p, (1,)).astype(jnp.int32), landed, chip_sum)


def adam_step(name, ga, gb, w, m, v):
    R, C = w.shape
    tr = _row_tile(R, C, 1 << 17)
    ins = [ga] + ([gb] if gb is not None else []) + [w, m, v]

    def body(*refs):
        g = refs[0][...]
        if gb is not None:
            g = g + refs[1][...]
        w_ref, m_ref, v_ref, g_out, d_out, m_out, v_out = refs[len(ins) - 3:]
        d, m2, v2 = _adam(w_ref[...], g, m_ref[...], v_ref[...])
        g_out[...] = g
        d_out[...] = d
        m_out[...] = m2
        v_out[...] = v2

    tile = pl.BlockSpec((tr, C), lambda t: (t, 0))
    return pl.pallas_call(
        body, name=name, grid=(R // tr,), in_specs=[tile] * len(ins), out_specs=[tile] * 4,
        out_shape=[jax.ShapeDtypeStruct((R, C), F32)] * 4, compiler_params=_cp(("parallel",)),
    )(*ins)


def adam_ffn(name, g_pieces, w, m, v, transposed=False):
    if transposed:
        res = adam_ffn(name, g_pieces, *(jnp.swapaxes(a, 2, 3) for a in (w, m, v)))
        return [jnp.swapaxes(r, 2, 3) for r in res]
    _, _, R, C = w.shape
    tr = _row_tile(R, 4 * C, 1 << 17)

    def body(g00, g01, g10, g11, w_ref, m_ref, v_ref, g_out, d_out, m_out, v_out):
        for l, j, g_ref in ((0, 0, g00), (0, 1, g01), (1, 0, g10), (1, 1, g11)):
            g = g_ref[...]
            d, m2, v2 = _adam(w_ref[l, j], g, m_ref[l, j], v_ref[l, j])
            g_out[l, j] = g
            d_out[l, j] = d
            m_out[l, j] = m2
            v_out[l, j] = v2

    piece = pl.BlockSpec((tr, C), lambda t: (t, 0))
    full = pl.BlockSpec((2, 2, tr, C), lambda t: (0, 0, t, 0))
    return pl.pallas_call(
        body, name=name, grid=(R // tr,), in_specs=[piece] * 4 + [full] * 3, out_specs=[full] * 4,
        out_shape=[jax.ShapeDtypeStruct(w.shape, F32)] * 4, compiler_params=_cp(("parallel",)),
    )(*g_pieces, w, m, v)


def _place():
    return lax.axis_index("x"), lax.axis_index("y"), lax.axis_index("c")


def _flip(me, mask):
    return tuple(1 - v if mk else v for v, mk in zip(me, mask))


CHIP_MASKS = ((1, 0, 0), (0, 1, 0), (1, 1, 0))
ALL_MASKS = tuple((a, b, c) for a in (0, 1) for b in (0, 1) for c in (0, 1) if (a, b, c) != (0, 0, 0))


def _chip(dev):
    return 2 * dev[0] + dev[1]


def _devno(dev):
    return 4 * dev[0] + 2 * dev[1] + dev[2]


class Pushes:
    def __init__(self, arrays, out_shapes, masks, copies, src_of, dst_of, alias=False):
        self.arrays, self.out_shapes, self.masks, self.copies = list(arrays), list(out_shapes), masks, copies
        self.src_of, self.dst_of, self.alias = src_of, dst_of, alias
        self.n = len(self.arrays)

    def sem_shapes(self):
        k = self.n * len(self.masks) * self.copies
        return [pltpu.SemaphoreType.DMA((k,)), pltpu.SemaphoreType.DMA((k,))]

    def ops(self, ins, outs, send_sems, recv_sems):
        me = _place()
        sends, lands = [], []
        for i in range(self.n):
            for j, mk in enumerate(self.masks):
                peer = _flip(me, mk)
                srcs, dsts = self.src_of(ins[i], me, j), self.dst_of(outs[i], me, j)
                here = self.dst_of(outs[i], peer, j)
                for q in range(self.copies):
                    sem = (i * len(self.masks) + j) * self.copies + q
                    sends.append(pltpu.make_async_remote_copy(
                        src_ref=srcs[q], dst_ref=dsts[q], send_sem=send_sems.at[sem], recv_sem=recv_sems.at[sem],
                        device_id=peer, device_id_type=MESH))
                    lands.append(pltpu.make_async_remote_copy(
                        src_ref=here[q], dst_ref=here[q], send_sem=send_sems.at[sem], recv_sem=recv_sems.at[sem],
                        device_id=peer, device_id_type=MESH))

        def start():
            for cp in sends:
                cp.start()

        def wait():
            for cp in lands:
                cp.wait_recv()
            for cp in sends:
                cp.wait_send()

        return start, wait


_HBM = pl.BlockSpec(memory_space=pl.ANY)


def exchange(name, p, local_of=None):
    n = p.n

    def body(*refs):
        ins, outs = refs[:n], refs[n:2 * n]
        start, wait = p.ops(ins, outs, refs[2 * n], refs[2 * n + 1])
        locals_ = []
        if local_of is not None:
            for i in range(n):
                src, dst = local_of(ins[i], outs[i], _place())
                locals_.append(pltpu.make_async_copy(src, dst, refs[2 * n + 2].at[i]))
                locals_[-1].start()
        start()
        wait()
        for cp in locals_:
            cp.wait()

    return pl.pallas_call(
        body, name=name, in_specs=[_HBM] * n, out_specs=[_HBM] * n, out_shape=p.out_shapes,
        scratch_shapes=p.sem_shapes() + ([pltpu.SemaphoreType.DMA((n,))] if local_of is not None else []),
        input_output_aliases={i: i for i in range(n)} if p.alias else {},
    )(*p.arrays)


def _half(c, rows):
    return pl.ds(c * (rows // 2), rows // 2)


def gather_pushes(arrays):
    outs = [jax.ShapeDtypeStruct((N_CHIPS,) + a.shape, a.dtype) for a in arrays]
    sib = len(CHIP_MASKS)
    return Pushes(arrays, outs, CHIP_MASKS + ((0, 0, 1),), 1,
                  src_of=lambda r, me, j: [r] if j == sib else [r.at[_half(me[2], r.shape[0])]],
                  dst_of=lambda o, sender, j: [o.at[_chip(sender)]] if j == sib else
                  [o.at[_chip(sender), _half(sender[2], o.shape[1])]])


def gather_swap(name, got):
    outs = [jax.ShapeDtypeStruct(a.shape, a.dtype) for a in got]
    return exchange(name, Pushes(
        got, outs, ((0, 0, 1),), len(CHIP_MASKS),
        src_of=lambda r, me, j: [r.at[_chip(_flip(me, mk)), _half(me[2], r.shape[1])] for mk in CHIP_MASKS],
        dst_of=lambda o, sender, j: [o.at[_chip(_flip(sender, mk)), _half(sender[2], o.shape[1])] for mk in CHIP_MASKS],
        alias=True))


def reduce_swap(arrays):
    split = [a.reshape(N_CHIPS, 2, a.shape[1] // 2, a.shape[2]) for a in arrays]
    half_shapes = [jax.ShapeDtypeStruct((N_CHIPS,) + a.shape[2:], F32) for a in split]
    return split, Pushes(split, half_shapes, ((0, 0, 1),), 1,
                         src_of=lambda r, me, j: [r.at[:, 1 - me[2]]], dst_of=lambda o, sender, j: [o])


def reduce_begin(tag, names, arrays, wire):
    split, pushes = reduce_swap(arrays)
    return reduce_sum(names, split, exchange(f"grad_pre_swap_{tag}", pushes), wire)


def reduce_sum(names, split, theirs, wire):
    c = lax.axis_index("c")
    chip_sum = [sum_own_half(f"sum2_{nm}", a, t, c, dt) for nm, a, t, dt in zip(names, split, theirs, wire)]
    pushes = Pushes(chip_sum, [jax.ShapeDtypeStruct((len(CHIP_MASKS),) + a.shape[1:], a.dtype) for a in chip_sum],
                    CHIP_MASKS, 1,
                    src_of=lambda r, me, j: [r.at[_chip(_flip(me, CHIP_MASKS[j]))]],
                    dst_of=lambda o, sender, j: [o.at[j]])
    return chip_sum, pushes


def reduce_end(tag, names, chip_sum, landed):
    x, y, c = _place()
    halves = [sum_landed(f"sum4_{nm}", p, a, _chip((x, y, c))) for nm, p, a in zip(names, landed, chip_sum)]
    others = exchange(f"grad_final_swap_{tag}", Pushes(
        halves, [jax.ShapeDtypeStruct(a.shape, F32) for a in halves], ((0, 0, 1),), 1,
        src_of=lambda r, me, j: [r], dst_of=lambda o, sender, j: [o]))
    return [jnp.concatenate([jnp.where(c == 0, h, o), jnp.where(c == 0, o, h)], axis=0) for h, o in zip(halves, others)]


def gather_all(arrays):
    outs = [jax.ShapeDtypeStruct((8,) + a.shape, a.dtype) for a in arrays]
    return exchange("gather_replicated", Pushes(
        arrays, outs, ALL_MASKS, 1, src_of=lambda r, me, j: [r], dst_of=lambda o, sender, j: [o.at[_devno(sender)]]),
        local_of=lambda r, o, me: (r, o.at[_devno(me)]))


def _unshard_cols(g):
    return jnp.transpose(g, (1, 0, 2)).reshape(g.shape[1], -1)


def _shard_cols(a):
    return jnp.transpose(a.reshape(a.shape[0], N_CHIPS, -1), (1, 0, 2))


class Weights(dict):
    def ride(self, kernel_name):
        return None

    def arrived(self, kernel_name, outs):
        pass


def _forward_backward(x, tgt, W, grads_early=None):
    S = x.shape[0]
    G = {}
    sd = jax.ShapeDtypeStruct

    hidden = {}

    def ffn(xin, l, j):
        out, *hidden[l, j] = ffn_fwd(xin, W["ffn_norm"][l][j], W["ffn_w_gate", l, j], W["ffn_w_up", l, j],
                                     W["ffn_w_down", l, j], l, j)
        return out

    def ffn_back(xin, dout, l, j):
        gn = W["ffn_norm"][l][j]
        dh, G["ffn_w_gate", l, j], G["ffn_w_up", l, j], G["ffn_w_down", l, j] = ffn_bwd(
            xin, gn, W["ffn_w_gate", l, j], W["ffn_w_up", l, j], W["ffn_w_down", l, j], dout, *hidden[l, j], l, j)
        dx, G[("ffn_norm", l, j)] = norm_bwd(f"ffn_norm_bwd_{l}{j}", xin, gn, dh, dout)
        return dx

    x0 = x
    x1 = ffn(x0, 0, 0)
    g0 = W["mix_norm"][0]
    sbq, sbk, sbv = tile_fwd(f_attn_sb, "attn_in_sb", [x1], [g0, W["attn_w_in"][0]], [sd((S, SB_W), F32)] * 3, 256)
    dl_shape = sd((DL_PAIRS, S, 128), F32)
    qn, = tile_fwd(f_attn_qk, "attn_in_q", [x1], [g0, W["attn_w_in"][1], W["attn_q_norm"]], [dl_shape], 256)
    kn, = tile_fwd(f_attn_qk, "attn_in_k", [x1], [g0, W["attn_w_in"][2], W["attn_k_norm"]], [dl_shape], 256)
    vv, = tile_fwd(f_attn_v, "attn_in_v", [x1], [g0, W["attn_w_in"][3]], [dl_shape], 256)
    oa, sb_wts, *rode = sb_fwd(sbq, sbk, sbv, W.ride("sb_fwd"))
    W.arrived("sb_fwd", rode)
    qs, ks, vs = (reorder(nm, t, DIL, False) for nm, t in (("sub_q", qn), ("sub_k", kn), ("sub_v", vv)))
    o_s, lse_s, *rode = dil_fwd(qs, ks, vs, W["bias_mat"], W.ride("dil_fwd"))
    W.arrived("dil_fwd", rode)
    o_n, lse_n = reorder("nat_o", o_s, DIL, True), reorder("nat_lse", lse_s, DIL, True)
    x2, = tile_fwd(f_attn_out, "attn_out", [x1, oa, o_n, lse_n], [W["attn_w_out"]], [sd((S, D), F32)], 256)
    x3 = ffn(x2, 0, 1)
    x4 = ffn(x3, 1, 0)
    g1 = W["mix_norm"][1]
    h, hs = norm_shift_fwd(x4, g1)
    mix = W["rw_mix"]
    r, = tile_fwd(f_rw_proj, "rw_proj_r", [h, hs], [mix[0:1], W["rw_wr"]], [sd((S, D), F32)], 256)
    k, = tile_fwd(f_rw_proj, "rw_proj_k", [h, hs], [mix[2:3], W["rw_wk"]], [sd((S, D), F32)], 256)
    v, = tile_fwd(f_rw_proj, "rw_proj_v", [h, hs], [mix[3:4], W["rw_wv"]], [sd((S, D), F32)], 256)
    mix3 = jnp.concatenate([mix[1:2], mix[4:5], mix[5:6]], axis=0)
    mid_w = [mix3, W["rw_w0"], W["rw_a0"], W["rw_kk"], W["rw_ka"], W["rw_w1"], W["rw_w2"], W["rw_a1"], W["rw_a2"],
             W["rw_g1"], W["rw_g2"]]
    hshape = sd((RW_H, S, HEAD), F32)
    mid_tiles = [h, hs, r, k, v]
    rh, lwh, kh, vh, ah, bh, gate = tile_fwd(f_rw_mid, "rw_mid", mid_tiles, mid_w, [hshape] * 6 + [sd((S, D), F32)], 128)
    yh, states = rwkv_fwd(rh, lwh, kh, vh, ah, bh)
    post_w = [W["rw_lnx_g"], W["rw_lnx_b"], W["rw_rk"], W["rw_wo"]]
    post_tiles = [yh, rh, kh, vh, gate, x4]
    x5, = tile_fwd(f_rw_post, "rw_post", post_tiles, post_w, [sd((S, D), F32)], 128)
    x6 = ffn(x5, 1, 1)
    dx6, loss_part = loss_head(x6, tgt)

    dx5 = ffn_back(x5, dx6, 1, 1)
    (dyh, drh, dkh, dvh, dgate, dx4), (d_lng, d_lnb, d_rk, d_wo) = tile_bwd(
        f_rw_post, "rw_post_bwd", post_tiles, post_w, [dx5], 128, [True] * 6, [True] * 4)
    drh2, dlwh, dkh2, dvh2, dah, dbh = rwkv_bwd(rh, lwh, kh, vh, ah, bh, states, dyh)
    mid_cts = [(drh, drh2), dlwh, (dkh, dkh2), (dvh, dvh2), dah, dbh, dgate]
    (dh, dhs, dr, dk, dv), dmid_w = tile_bwd(f_rw_mid, "rw_mid_bwd", mid_tiles, mid_w, mid_cts, 128,
                                             [True] * 5, [True] * len(mid_w))
    dmix = {}
    for nm, ct, row, wname in (("r", dr, 0, "rw_wr"), ("k", dk, 2, "rw_wk"), ("v", dv, 3, "rw_wv")):
        (dh, dhs), (dmix[row], G[wname]) = tile_bwd(
            f_rw_proj, f"rw_proj_{nm}_bwd", [h, hs], [mix[row:row + 1], W[wname]], [ct], 256,
            [True, True], [True, True], acc={0: dh, 1: dhs})
    dx4, G[("mix_norm", 1)] = norm_shift_bwd(x4, g1, dh, dhs, dx4)
    dmix3 = dmid_w[0]
    G["rw_mix"] = jnp.concatenate([dmix[0], dmix3[0:1], dmix[2], dmix[3], dmix3[1:2], dmix3[2:3]], axis=0)
    for nm, gv in zip(("rw_w0", "rw_a0", "rw_kk", "rw_ka", "rw_w1", "rw_w2", "rw_a1", "rw_a2", "rw_g1", "rw_g2"), dmid_w[1:]):
        G[nm] = gv
    G["rw_lnx_g"], G["rw_lnx_b"], G["rw_rk"], G["rw_wo"] = d_lng, d_lnb, d_rk, d_wo
    dx3 = ffn_back(x3, dx4, 1, 0)
    dx2 = ffn_back(x2, dx3, 0, 1)
    (dx1, doa, do_n, dlse_n), (G["attn_w_out"],) = tile_bwd(
        f_attn_out, "attn_out_bwd", [x1, oa, o_n, lse_n], [W["attn_w_out"]], [dx2], 256, [True] * 4, [True])
    do_s, dlse_s = reorder("sub_do", do_n, DIL, False), reorder("sub_dlse", dlse_n, DIL, False)
    ride, swapped = grads_early(G) if grads_early is not None else (None, None)
    dqs, dks, dvs, dsum, *rode = dil_bwd(qs, ks, vs, W["bias_mat"], o_s, lse_s, do_s, dlse_s, ride)
    ride, landed = swapped(rode) if swapped is not None else (None, None)
    G["rel_bias"] = bias_grad(dsum, W["buckets"])
    dqn, dkn, dvv = (reorder(nm, t, DIL, True) for nm, t in (("nat_dq", dqs), ("nat_dk", dks), ("nat_dv", dvs)))
    dsbq, dsbk, dsbv, *rode = sb_bwd(sbq, sbk, sbv, doa, sb_wts, ride)
    if landed is not None:
        landed(rode)
    dg0 = []
    dwin = []
    (dx1,), (dg, dw) = tile_bwd(f_attn_sb, "attn_in_sb_bwd", [x1], [g0, W["attn_w_in"][0]], [dsbq, dsbk, dsbv], 256,
                                [True], [True, True], acc={0: dx1})
    dg0.append(dg), dwin.append(dw)
    (dx1,), (dg, dw, G["attn_q_norm"]) = tile_bwd(f_attn_qk, "attn_in_q_bwd", [x1], [g0, W["attn_w_in"][1], W["attn_q_norm"]],
                                                  [dqn], 256, [True], [True] * 3, acc={0: dx1})
    dg0.append(dg), dwin.append(dw)
    (dx1,), (dg, dw, G["attn_k_norm"]) = tile_bwd(f_attn_qk, "attn_in_k_bwd", [x1], [g0, W["attn_w_in"][2], W["attn_k_norm"]],
                                                  [dkn], 256, [True], [True] * 3, acc={0: dx1})
    dg0.append(dg), dwin.append(dw)
    (dx1,), (dg, dw) = tile_bwd(f_attn_v, "attn_in_v_bwd", [x1], [g0, W["attn_w_in"][3]], [dvv], 256,
                                [True], [True, True], acc={0: dx1})
    dg0.append(dg), dwin.append(dw)
    G[("mix_norm", 0)] = dg0
    G["attn_w_in"] = dwin
    dx0 = ffn_back(x0, dx1, 0, 0)
    return loss_part, dx0, G


VEC_ROWS = ("ffn_norm", "rw_mix", "rw_w0", "rw_a0", "rw_kk", "rw_ka", "rw_lnx_g", "rw_lnx_b")


def kernel(x, ffn_norm, ffn_w_gate, ffn_w_up, ffn_w_down, mix_norm, rel_bias, attn_w_in, attn_q_norm, attn_k_norm, attn_w_out, rw_mix, rw_w0, rw_w1, rw_w2, rw_a0, rw_a1, rw_a2, rw_g1, rw_g2, rw_kk, rw_ka, rw_rk, rw_wr, rw_wk, rw_wv, rw_wo, rw_lnx_g, rw_lnx_b, loss_target, m_ffn_norm, m_ffn_w_gate, m_ffn_w_up, m_ffn_w_down, m_mix_norm, m_rel_bias, m_attn_w_in, m_attn_q_norm, m_attn_k_norm, m_attn_w_out, m_rw_mix, m_rw_w0, m_rw_w1, m_rw_w2, m_rw_a0, m_rw_a1, m_rw_a2, m_rw_g1, m_rw_g2, m_rw_kk, m_rw_ka, m_rw_rk, m_rw_wr, m_rw_wk, m_rw_wv, m_rw_wo, m_rw_lnx_g, m_rw_lnx_b, v_ffn_norm, v_ffn_w_gate, v_ffn_w_up, v_ffn_w_down, v_mix_norm, v_rel_bias, v_attn_w_in, v_attn_q_norm, v_attn_k_norm, v_attn_w_out, v_rw_mix, v_rw_w0, v_rw_w1, v_rw_w2, v_rw_a0, v_rw_a1, v_rw_a2, v_rw_g1, v_rw_g2, v_rw_kk, v_rw_ka, v_rw_rk, v_rw_wr, v_rw_wk, v_rw_wv, v_rw_wo, v_rw_lnx_g, v_rw_lnx_b):
    names = ["ffn_norm", "ffn_w_gate", "ffn_w_up", "ffn_w_down", "mix_norm", "rel_bias", "attn_w_in", "attn_q_norm",
             "attn_k_norm", "attn_w_out", "rw_mix", "rw_w0", "rw_w1", "rw_w2", "rw_a0", "rw_a1", "rw_a2", "rw_g1", "rw_g2",
             "rw_kk", "rw_ka", "rw_rk", "rw_wr", "rw_wk", "rw_wv", "rw_wo", "rw_lnx_g", "rw_lnx_b"]
    loc = locals()
    w = {n: loc[n] for n in names}
    mom = {n: loc["m_" + n] for n in names}
    vel = {n: loc["v_" + n] for n in names}
    S = x.shape[1]

    ffn3 = ("ffn_w_gate", "ffn_w_up", "ffn_w_down")
    rw_mats = ("rw_w1", "rw_w2", "rw_a1", "rw_a2", "rw_g1", "rw_g2", "rw_wr", "rw_wk", "rw_wv", "rw_wo")
    cols_split = ("attn_w_out", "rw_w2", "rw_a2", "rw_g2")
    shard = {"vec": jnp.concatenate([w[n].reshape(-1, 256) for n in VEC_ROWS], axis=0)}
    for n in ffn3:
        for l in range(2):
            for j in range(2):
                shard[n, l, j] = w[n][l, j].astype(BF16)
    for n in ("attn_w_in", "attn_w_out") + rw_mats:
        shard[n] = w[n].reshape(-1, w[n].shape[-1]).astype(BF16)
    ffn_keys = lambda l, j: [(n, l, j) for n in ffn3]
    w_groups = {"first": ["vec"] + ffn_keys(0, 0) + ["attn_w_in", "attn_w_out"],
                "sb_fwd": ffn_keys(0, 1) + ffn_keys(1, 0) + list(rw_mats),
                "dil_fwd": ffn_keys(1, 1)}
    label = lambda key: key if isinstance(key, str) else f"{key[0]}_{key[1]}{key[2]}"

    class Streamed(Weights):
        def ride(self, kernel_name):
            return gather_pushes([shard[k] for k in w_groups[kernel_name]])

        def arrived(self, kernel_name, outs):
            for key, g in zip(w_groups[kernel_name], gather_swap(f"gather_swap_{kernel_name}", outs)):
                if key == "vec":
                    vec_full = _unshard_cols(g)
                    self["ffn_norm"] = [[vec_full[2 * l + j][None] for j in range(2)] for l in range(2)]
                    self["rw_mix"] = vec_full[4:10]
                    for i, n in enumerate(("rw_w0", "rw_a0", "rw_kk", "rw_ka", "rw_lnx_g", "rw_lnx_b")):
                        self[n] = vec_full[10 + i][None]
                elif key == "attn_w_in":
                    self[key] = [g[p] for p in range(N_CHIPS)]
                elif key in cols_split:
                    self[key] = _unshard_cols(g)
                elif isinstance(key, str):
                    self[key] = g.reshape(D, -1)
                else:
                    self[key] = g

    buckets = _bucket_maps()
    W = Streamed({"mix_norm": [mix_norm[0:1], mix_norm[1:2]], "attn_q_norm": attn_q_norm, "attn_k_norm": attn_k_norm,
                  "rw_rk": rw_rk[0][:, None, :], "buckets": buckets, "bias_mat": bias_table(rel_bias, buckets)})
    W.arrived("first", exchange("gather_weights", W.ride("first")))

    def slots(key, G):
        if key == "vec":
            rows = [G[("ffn_norm", l, j)] for l in range(2) for j in range(2)] + [G["rw_mix"]] + \
                   [G[n] for n in ("rw_w0", "rw_a0", "rw_kk", "rw_ka", "rw_lnx_g", "rw_lnx_b")]
            return _shard_cols(jnp.concatenate(rows, axis=0))
        if key == "attn_w_in":
            return jnp.stack(G[key])
        if key in cols_split:
            return _shard_cols(G[key])
        if isinstance(key, str):
            return G[key].reshape(N_CHIPS, D // N_CHIPS, -1)
        return G[key]

    g_groups = {"early": ffn_keys(1, 1) + ffn_keys(1, 0) + ffn_keys(0, 1) + list(rw_mats) + ["attn_w_out"],
                "late": ["vec", "attn_w_in"] + ffn_keys(0, 0)}
    wire = lambda keys: [F32 if k == "vec" else BF16 for k in keys]
    part = {}

    def grads_early(G):
        keys = g_groups["early"]
        names_ = [label(k) for k in keys]
        split, swap_pushes = reduce_swap([slots(k, G) for k in keys])

        def swapped(theirs):
            chip_sum, pushes = reduce_sum(names_, split, theirs, wire(keys))
            return pushes, lambda landed: part.update(zip(keys, reduce_end("early", names_, chip_sum, landed)))

        return swap_pushes, swapped

    loss_part, dx, G = _forward_backward(x[0], loss_target[0], W, grads_early)
    loss = lax.psum(loss_part[0, 0], ("x", "y", "c"))
    keys = g_groups["late"]
    chip_sum, pushes = reduce_begin("late", [label(k) for k in keys], [slots(k, G) for k in keys], wire(keys))
    part.update(zip(keys, reduce_end("late", [label(k) for k in keys], chip_sum, exchange("scatter_grads", pushes))))

    rep = jnp.concatenate([G[("mix_norm", 0)][0] + G[("mix_norm", 0)][1] + G[("mix_norm", 0)][2] + G[("mix_norm", 0)][3],
                           G[("mix_norm", 1)]], axis=0).reshape(16, 128)
    rep = jnp.concatenate([rep, G["rel_bias"], jnp.pad(G["attn_q_norm"], ((0, 0), (0, 64))),
                           jnp.pad(G["attn_k_norm"], ((0, 0), (0, 64))), G["rw_rk"].reshape(8, 128),
                           jnp.zeros((2, 128), F32)], axis=0)
    rep_sum = sum_slots("sum_replicated", gather_all([rep])[0])
    g_rep = {
        "mix_norm": rep_sum[0:16].reshape(2, D),
        "rel_bias": jnp.transpose(rep_sum[16:28, :N_BUCKETS]),
        "attn_q_norm": rep_sum[28:29, :HEAD], "attn_k_norm": rep_sum[29:30, :HEAD],
        "rw_rk": rep_sum[30:38].reshape(1, RW_H, HEAD),
    }

    out = {}

    def adam(n, ga, gb):
        shp = w[n].shape
        to2 = lambda a: a.reshape(-1, shp[-1])
        res = adam_step(f"adam_{n}", to2(ga), None if gb is None else to2(gb), to2(w[n]), to2(mom[n]), to2(vel[n]))
        out[n] = tuple(r.reshape(shp) for r in res)

    for n in ffn3:
        out[n] = tuple(adam_ffn(f"adam_{n}", [part[n, l, j] for l in range(2) for j in range(2)], w[n], mom[n], vel[n],
                                transposed=n != "ffn_w_down"))
    for n in ("attn_w_in", "attn_w_out") + rw_mats:
        adam(n, part[n], None)
    rows = {"ffn_norm": (0, 4), "rw_mix": (4, 10), "rw_w0": (10, 11), "rw_a0": (11, 12), "rw_kk": (12, 13),
            "rw_ka": (13, 14), "rw_lnx_g": (14, 15), "rw_lnx_b": (15, 16)}
    for n, (lo, hi) in rows.items():
        adam(n, part["vec"][lo:hi], None)
    for n, gv in g_rep.items():
        adam(n, gv, None)

    grads = [out[n][0] for n in names]
    deltas = [out[n][1] for n in names]
    new_m = [out[n][2] for n in names]
    new_v = [out[n][3] for n in names]
    return (loss, dx[None], *grads, *deltas, *new_m, *new_v)
```

```python
import functools
import math

import jax
import jax.numpy as jnp
from jax import lax
from jax.experimental import pallas as pl
from jax.experimental.pallas import tpu as pltpu

F32, BF16 = jnp.float32, jnp.bfloat16
HI = lax.Precision.HIGHEST
MESH = pl.DeviceIdType.MESH

D = 1024
HEAD = 64
N_CHIPS = 4
FF_SHARD = 704
SB_W = 256
DL_HEADS = 12
DL_PAIRS = 6
DIL = (1, 4, 16)
QBLK = 128
N_BUCKETS = 32
MAX_DISTANCE = 2048
RW_H = 16
RW_CHUNK = 64
NORM_EPS = 1e-6
GN_EPS = 64e-5
NEG_INF = -1e30
VMEM_LIMIT = 56 * 1024 * 1024

ADAM_LR, ADAM_B1, ADAM_B2, ADAM_EPS, ADAM_WD, ADAM_STEP = 0.001, 0.9, 0.999, 1e-08, 0.01, 10


def _cp(sem):
    return pltpu.CompilerParams(dimension_semantics=sem, vmem_limit_bytes=VMEM_LIMIT)


def _dg(a, b, dims, prec=None):
    return lax.dot_general(a, b, (dims, ((), ())), precision=prec, preferred_element_type=F32)


def _bdot(a, b, dims):
    return _dg(a.astype(BF16), b.astype(BF16), dims)


@jax.custom_vjp
def mm(a, b):
    return _bdot(a, b, ((1,), (0,)))


def _mm_fwd(a, b):
    return _bdot(a, b, ((1,), (0,))), (a, b)


def _mm_bwd(res, g):
    a, b = res
    return _bdot(g, b, ((1,), (1,))), _bdot(a, g, ((0,), (0,)))


mm.defvjp(_mm_fwd, _mm_bwd)


def rms(x, g):
    return x * lax.rsqrt(jnp.mean(x * x, axis=-1, keepdims=True) + NORM_EPS) * g


def _pieces(x):
    x1 = x.astype(BF16)
    r1 = x - x1.astype(F32)
    x2 = r1.astype(BF16)
    return jnp.concatenate([x1, x2, (r1 - x2.astype(F32)).astype(BF16)], axis=-1)


def _group_sum(x, nh):
    w = x.shape[-1]
    e = (lax.broadcasted_iota(jnp.int32, (w, nh), 0) // HEAD == lax.broadcasted_iota(jnp.int32, (w, nh), 1)).astype(BF16)
    s = _dg(_pieces(x), jnp.concatenate([e, e, e], axis=0), ((1,), (0,)))
    return _dg(_pieces(s), jnp.concatenate([e, e, e], axis=1), ((1,), (1,)))


@functools.partial(jax.custom_vjp, nondiff_argnums=(1,))
def group_sum(x, nh):
    return _group_sum(x, nh)


group_sum.defvjp(lambda x, nh: (_group_sum(x, nh), None), lambda nh, _, g: (_group_sum(g, nh),))


def softplus(u):
    return jnp.maximum(u, 0.0) + jnp.log1p(jnp.exp(-jnp.abs(u)))


def to_heads(t, nh=RW_H):
    return jnp.stack([t[:, HEAD * h:HEAD * (h + 1)] for h in range(nh)])


def from_heads(t):
    return jnp.concatenate([t[h] for h in range(t.shape[0])], axis=-1)


def _tile_spec(shape, tm):
    if len(shape) == 2:
        return pl.BlockSpec((tm, shape[1]), lambda t: (t, 0))
    return pl.BlockSpec((shape[0], tm, shape[2]), lambda t: (0, t, 0))


def _full_spec(shape):
    nd = len(shape)
    return pl.BlockSpec(tuple(shape), lambda t: (0,) * nd)


def _rows(a):
    return a.shape[0] if a.ndim == 2 else a.shape[1]


def tile_fwd(f, name, tiles, weights, outs, tm):
    nt, nw = len(tiles), len(weights)

    def body(*refs):
        tv = [r[...] for r in refs[:nt]]
        wv = [r[...].astype(F32) for r in refs[nt:nt + nw]]
        res = f(*tv, *wv)
        if not isinstance(res, (tuple, list)):
            res = (res,)
        for o, v in zip(refs[nt + nw:], res):
            o[...] = v.astype(o.dtype)

    return pl.pallas_call(
        body, name=name, grid=(_rows(tiles[0]) // tm,),
        in_specs=[_tile_spec(a.shape, tm) for a in tiles] + [_full_spec(w.shape) for w in weights],
        out_specs=[_tile_spec(o.shape, tm) for o in outs],
        out_shape=list(outs),
        compiler_params=_cp(("parallel",)),
    )(*tiles, *weights)


def tile_bwd(f, name, tiles, weights, cts, tm, dt, dw, acc=None):
    acc = acc or {}
    groups = [c if isinstance(c, tuple) else (c,) for c in cts]
    cts = [a for grp in groups for a in grp]
    nt, nw, nc = len(tiles), len(weights), len(cts)
    acc_idx = sorted(acc)
    na = len(acc_idx)
    dti = [i for i in range(nt) if dt[i]]
    dwi = [i for i in range(nw) if dw[i]]

    def body(*refs):
        tv = [r[...] for r in refs[:nt]]
        wv = [r[...].astype(F32) for r in refs[nt:nt + nw]]
        crefs = list(refs[nt + nw:nt + nw + nc])
        cv = []
        for grp in groups:
            terms = [crefs.pop(0)[...] for _ in grp]
            cv.append(functools.reduce(lambda a, b: a + b, terms))
        av = {i: r[...] for i, r in zip(acc_idx, refs[nt + nw + nc:nt + nw + nc + na])}
        orefs = refs[nt + nw + nc + na:]

        def g(*diff):
            t2, w2 = list(tv), list(wv)
            for i, v in zip(dti, diff[:len(dti)]):
                t2[i] = v
            for i, v in zip(dwi, diff[len(dti):]):
                w2[i] = v
            res = f(*t2, *w2)
            return tuple(res) if isinstance(res, (tuple, list)) else (res,)

        _, vjp = jax.vjp(g, *[tv[i] for i in dti], *[wv[i] for i in dwi])
        grads = vjp(tuple(cv))
        for k, i in enumerate(dti):
            gt = grads[k]
            if i in av:
                gt = gt + av[i]
            orefs[k][...] = gt
        first = pl.program_id(0) == 0
        for k, i in enumerate(dwi):
            o = orefs[len(dti) + k]
            gw = grads[len(dti) + k]

            @pl.when(first)
            def _(o=o, gw=gw):
                o[...] = gw

            @pl.when(jnp.logical_not(first))
            def _(o=o, gw=gw):
                o[...] += gw

    out_shape = [jax.ShapeDtypeStruct(tiles[i].shape, F32) for i in dti] + \
                [jax.ShapeDtypeStruct(weights[i].shape, F32) for i in dwi]
    res = pl.pallas_call(
        body, name=name, grid=(_rows(tiles[0]) // tm,),
        in_specs=[_tile_spec(a.shape, tm) for a in tiles] + [_full_spec(w.shape) for w in weights] +
                 [_tile_spec(c.shape, tm) for c in cts] + [_tile_spec(tiles[i].shape, tm) for i in acc_idx],
        out_specs=[_tile_spec(tiles[i].shape, tm) for i in dti] + [_full_spec(weights[i].shape) for i in dwi],
        out_shape=out_shape,
        compiler_params=_cp(("arbitrary",)),
    )(*tiles, *weights, *cts, *[acc[i] for i in acc_idx])
    return list(res[:len(dti)]), list(res[len(dti):])


def _ffn_wspec(rows, cols, cfirst):
    if cfirst:
        return pl.BlockSpec((1, rows, cols), lambda c, t: (c, 0, 0))
    return pl.BlockSpec((1, rows, cols), lambda t, c: (c, 0, 0))


def ffn_fwd(x, g, wg, wu, wd, l, j, tm=512):
    S = x.shape[0]

    def body(x_ref, g_ref, wg_ref, wu_ref, wd_ref, o_ref, a_ref, b_ref, h_ref, acc_ref):
        c = pl.program_id(1)

        @pl.when(c == 0)
        def _():
            h_ref[...] = rms(x_ref[...], g_ref[...]).astype(BF16)
            acc_ref[...] = jnp.zeros_like(acc_ref)

        h = h_ref[...]
        a = _bdot(h, wg_ref[0], ((1,), (0,)))
        b = _bdot(h, wu_ref[0], ((1,), (0,)))
        a_ref[0] = a.astype(BF16)
        b_ref[0] = b.astype(BF16)
        y = a * jax.nn.sigmoid(a) * b
        acc_ref[...] += _bdot(y, wd_ref[0], ((1,), (0,)))

        @pl.when(c == N_CHIPS - 1)
        def _():
            o_ref[...] = x_ref[...] + 0.5 * acc_ref[...]

    hid = pl.BlockSpec((1, tm, FF_SHARD), lambda t, c: (c, t, 0))
    return pl.pallas_call(
        body, name=f"ffn_fwd_{l}{j}", grid=(S // tm, N_CHIPS),
        in_specs=[pl.BlockSpec((tm, D), lambda t, c: (t, 0)), pl.BlockSpec((1, D), lambda t, c: (0, 0)),
                  _ffn_wspec(D, FF_SHARD, False), _ffn_wspec(D, FF_SHARD, False), _ffn_wspec(FF_SHARD, D, False)],
        out_specs=[pl.BlockSpec((tm, D), lambda t, c: (t, 0)), hid, hid],
        out_shape=[jax.ShapeDtypeStruct((S, D), F32)] + [jax.ShapeDtypeStruct((N_CHIPS, S, FF_SHARD), BF16)] * 2,
        scratch_shapes=[pltpu.VMEM((tm, D), BF16), pltpu.VMEM((tm, D), F32)],
        compiler_params=_cp(("parallel", "arbitrary")),
    )(x, g, wg, wu, wd)


def ffn_bwd(x, g, wg, wu, wd, dout, a_sav, b_sav, l, j, tm=512):
    S = x.shape[0]

    def body(x_ref, g_ref, wg_ref, wu_ref, wd_ref, do_ref, a_ref, b_ref, dh_ref, dwg_ref, dwu_ref, dwd_ref):
        t = pl.program_id(1)
        h = rms(x_ref[...], g_ref[...]).astype(BF16)
        wgv, wuv, wdv = wg_ref[0], wu_ref[0], wd_ref[0]
        a = a_ref[0].astype(F32)
        b = b_ref[0].astype(F32)
        sig = jax.nn.sigmoid(a)
        s = a * sig
        dyd = 0.5 * do_ref[...]
        dy = _bdot(dyd, wdv, ((1,), (1,)))
        dwd = _bdot(s * b, dyd, ((0,), (0,)))
        db = dy * s
        da = dy * b * (sig * (1.0 + a * (1.0 - sig)))
        dwg = _bdot(da, h, ((0,), (0,)))
        dwu = _bdot(db, h, ((0,), (0,)))
        dh_ref[0] = (_bdot(da, wgv, ((1,), (1,))) + _bdot(db, wuv, ((1,), (1,)))).astype(dh_ref.dtype)

        @pl.when(t == 0)
        def _():
            dwg_ref[0] = dwg
            dwu_ref[0] = dwu
            dwd_ref[0] = dwd

        @pl.when(t != 0)
        def _():
            dwg_ref[0] += dwg
            dwu_ref[0] += dwu
            dwd_ref[0] += dwd

    return pl.pallas_call(
        body, name=f"ffn_bwd_{l}{j}", grid=(N_CHIPS, S // tm),
        in_specs=[pl.BlockSpec((tm, D), lambda c, t: (t, 0)), pl.BlockSpec((1, D), lambda c, t: (0, 0)),
                  _ffn_wspec(D, FF_SHARD, True), _ffn_wspec(D, FF_SHARD, True), _ffn_wspec(FF_SHARD, D, True),
                  pl.BlockSpec((tm, D), lambda c, t: (t, 0)),
                  pl.BlockSpec((1, tm, FF_SHARD), lambda c, t: (c, t, 0)), pl.BlockSpec((1, tm, FF_SHARD), lambda c, t: (c, t, 0))],
        out_specs=[pl.BlockSpec((1, tm, D), lambda c, t: (c, t, 0))] + [_ffn_wspec(FF_SHARD, D, True)] * 3,
        out_shape=[jax.ShapeDtypeStruct((N_CHIPS, S, D), BF16)] + [jax.ShapeDtypeStruct(wd.shape, F32)] * 3,
        compiler_params=_cp(("parallel", "arbitrary")),
    )(x, g, wg, wu, wd, dout, a_sav, b_sav)


def norm_bwd(name, x, g, dh_parts, dres, tm=256):
    S = x.shape[0]
    P = dh_parts.shape[0]

    def body(x_ref, g_ref, dh_ref, dr_ref, dx_ref, dg_ref):
        dh = dh_ref[0].astype(F32)
        for p in range(1, P):
            dh = dh + dh_ref[p].astype(F32)
        _, vjp = jax.vjp(rms, x_ref[...], g_ref[...])
        dx, dg = vjp(dh)
        dx_ref[...] = dr_ref[...] + dx

        @pl.when(pl.program_id(0) == 0)
        def _():
            dg_ref[...] = dg

        @pl.when(pl.program_id(0) != 0)
        def _():
            dg_ref[...] += dg

    return pl.pallas_call(
        body, name=name, grid=(S // tm,),
        in_specs=[pl.BlockSpec((tm, D), lambda t: (t, 0)), pl.BlockSpec((1, D), lambda t: (0, 0)),
                  pl.BlockSpec((P, tm, D), lambda t: (0, t, 0)), pl.BlockSpec((tm, D), lambda t: (t, 0))],
        out_specs=[pl.BlockSpec((tm, D), lambda t: (t, 0)), pl.BlockSpec((1, D), lambda t: (0, 0))],
        out_shape=[jax.ShapeDtypeStruct((S, D), F32), jax.ShapeDtypeStruct((1, D), F32)],
        compiler_params=_cp(("arbitrary",)),
    )(x, g, dh_parts, dres)


def f_attn_sb(x, g, w):
    pr = mm(rms(x, g), w)
    return pr[:, :SB_W], pr[:, SB_W:2 * SB_W], pr[:, 2 * SB_W:]


def _pairs(y):
    return jnp.stack([y[:, 128 * j:128 * (j + 1)] for j in range(DL_PAIRS)])


def f_attn_qk(x, g, w, nrm):
    pr = mm(rms(x, g), w)
    ms = group_sum(pr * pr, DL_HEADS) * (1.0 / HEAD)
    return _pairs(pr * lax.rsqrt(ms + NORM_EPS) * jnp.concatenate([nrm] * DL_HEADS, axis=1))


def f_attn_v(x, g, w):
    return _pairs(mm(rms(x, g), w))


def _masked(strict, x):
    return x if strict is None else jnp.where(strict, x, 0.0)


def _head_stack(x, dtype=BF16):
    nh = x.shape[1] // HEAD
    lane_head = lax.broadcasted_iota(jnp.int32, (1, x.shape[1]), 1) // HEAD
    return jnp.concatenate([jnp.where(lane_head == h, x, 0.0) for h in range(nh)], axis=0).astype(dtype)


def _head_pick(xs):
    nh = xs.shape[1] // HEAD
    rows = xs.shape[0] // nh
    lane_head = lax.broadcasted_iota(jnp.int32, (1, xs.shape[1]), 1) // HEAD
    out = xs[:rows]
    for h in range(1, nh):
        out = jnp.where(lane_head == h, xs[rows * h:rows * (h + 1)], out)
    return out


def _sb_tiles(qs, kblk, strict):
    z = _dg(qs, kblk, ((1,), (1,))) * (HEAD ** -0.5)
    keep = -(jnp.maximum(z, 0.0) + jnp.log(1.0 + jnp.exp(-jnp.abs(z))))
    return z, _masked(strict, keep)


def _tri(n, upper):
    r = lax.broadcasted_iota(jnp.int32, (n, n), 0)
    c = lax.broadcasted_iota(jnp.int32, (n, n), 1)
    return ((r > c) if upper else (r < c)).astype(BF16)


def _tri_sums(x, tri):
    hi, lo = _split2(x)
    return _dg(jnp.concatenate([hi, lo], axis=1), jnp.concatenate([tri, tri], axis=0), ((1,), (0,)))


SB_UNROLL = 4


def _sb_diag(tb, nh):
    r = lax.broadcasted_iota(jnp.int32, (nh * tb, tb), 0)
    return lax.broadcasted_iota(jnp.int32, (nh * tb, tb), 1) < lax.rem(r, tb)


def _sb_sweep(step, first, count, carry, direction, commit=None):
    def run(kbs, c):
        outs = []
        for kb in kbs:
            c, out = step(kb, c)
            outs.append(out)
        if commit is not None:
            for kb, out in zip(kbs, outs):
                commit(kb, out)
        return c

    rem = count % SB_UNROLL
    carry = lax.fori_loop(0, rem, lambda i, c: run([first + direction * i], c), carry)
    return lax.fori_loop(
        0, count // SB_UNROLL,
        lambda g, c: run([first + direction * (rem + SB_UNROLL * g + u) for u in range(SB_UNROLL)], c), carry)


def _riding(ride, refs, n_in, n_out, first, last):
    if ride is None:
        return refs, lambda: None
    n = ride.n
    own = refs[:n_in] + refs[n_in + n:n_in + n + n_out] + refs[n_in + 2 * n + n_out:len(refs) - 2]
    start, wait = ride.ops(refs[n_in:n_in + n], refs[n_in + n + n_out:n_in + 2 * n + n_out], refs[-2], refs[-1])
    pl.when(first)(start)
    return own, lambda: pl.when(last)(wait)


def _ride_specs(ride):
    if ride is None:
        return [], [], [], [], []
    return [_HBM] * ride.n, [_HBM] * ride.n, ride.out_shapes, ride.sem_shapes(), ride.arrays


def sb_fwd(q, k, v, ride=None, tb=QBLK):
    S = q.shape[0]
    nh = SB_W // HEAD
    nb = S // tb
    r_in, r_out, r_shape, r_scr, r_args = _ride_specs(ride)

    def body(*refs):
        qb = pl.program_id(0)
        (q_ref, k_ref, v_ref, o_ref, w_ref), finish = _riding(ride, refs, 3, 2, qb == 0, qb == nb - 1)
        diag = _sb_diag(tb, nh)
        after_mat = _tri(tb, True)
        qs = _head_stack(q_ref[...])

        def step(kb, carry, strict):
            acc, run = carry
            rows = pl.ds(pl.multiple_of(kb * tb, tb), tb)
            z, keep = _sb_tiles(qs, k_ref[rows, :].astype(BF16), strict)
            w = _masked(strict, jnp.exp(z + keep + _tri_sums(keep, after_mat) + run)).astype(BF16)
            w_ref[0, kb] = w
            acc = acc + _dg(w, v_ref[rows, :].astype(BF16), ((1,), (0,)))
            return acc, run + jnp.sum(keep, axis=1, keepdims=True)

        init = (jnp.zeros((nh * tb, SB_W), F32), jnp.zeros((nh * tb, 1), F32))
        carry = step(qb, init, diag)
        acc, _ = _sb_sweep(lambda kb, c: (step(kb, c, None), None), qb - 1, qb, carry, -1)
        o_ref[...] = _head_pick(acc)
        finish()

    return pl.pallas_call(
        body, name="sb_fwd", grid=(S // tb,),
        in_specs=[pl.BlockSpec((tb, SB_W), lambda i: (i, 0)), pl.BlockSpec((S, SB_W), lambda i: (0, 0)),
                  pl.BlockSpec((S, SB_W), lambda i: (0, 0))] + r_in,
        out_specs=[pl.BlockSpec((tb, SB_W), lambda i: (i, 0)),
                   pl.BlockSpec((1, nb, nh * tb, tb), lambda i: (i, 0, 0, 0))] + r_out,
        out_shape=[jax.ShapeDtypeStruct((S, SB_W), F32), jax.ShapeDtypeStruct((nb, nb, nh * tb, tb), BF16)] + r_shape,
        scratch_shapes=r_scr,
        compiler_params=_cp(("arbitrary",)),
    )(q, k, v, *r_args)


def sb_bwd(q, k, v, do, wts, ride=None, tb=QBLK):
    S = q.shape[0]
    nh = SB_W // HEAD
    nb = S // tb
    scale = HEAD ** -0.5
    r_in, r_out, r_shape, r_scr, r_args = _ride_specs(ride)

    def body(*refs):
        qb = pl.program_id(0)
        (q_ref, k_ref, v_ref, do_ref, w_ref, dq_ref, dk_ref, dv_ref, g_scr), finish = _riding(
            ride, refs, 5, 3, qb == 0, qb == nb - 1)

        @pl.when(qb == 0)
        def _():
            dk_ref[...] = jnp.zeros_like(dk_ref)
            dv_ref[...] = jnp.zeros_like(dv_ref)

        diag = _sb_diag(tb, nh)
        before_mat = _tri(tb, False)
        qs = _head_stack(q_ref[...])
        dos = _head_stack(do_ref[...])

        def weights_pass(kb, carry):
            rows = pl.ds(pl.multiple_of(kb * tb, tb), tb)
            w = w_ref[0, kb]
            g_scr[kb] = _dg(dos, v_ref[rows, :].astype(BF16), ((1,), (1,))) * w.astype(F32)
            return carry, _dg(w, dos, ((0,), (0,)))

        def add_rows(ref):
            def commit(kb, val):
                ref[pl.ds(pl.multiple_of(kb * tb, tb), tb), :] += val
            return commit

        zero_run = jnp.zeros((nh * tb, 1), F32)
        _sb_sweep(weights_pass, 0, qb + 1, 0, 1, add_rows(dv_ref))

        def left_to_right(kb, carry, strict):
            dq, run = carry
            rows = pl.ds(pl.multiple_of(kb * tb, tb), tb)
            kblk = k_ref[rows, :].astype(BF16)
            gw = g_scr[kb]
            sig = jax.nn.sigmoid(_dg(qs, kblk, ((1,), (1,))) * scale)
            dkeep = _masked(strict, _tri_sums(gw, before_mat) + run)
            dz = ((gw * (1.0 - sig) - dkeep * sig) * scale).astype(BF16)
            dq = dq + _dg(dz, kblk, ((1,), (0,)))
            return (dq, run + jnp.sum(gw, axis=1, keepdims=True)), _dg(dz, qs, ((0,), (0,)))

        carry = _sb_sweep(lambda kb, c: left_to_right(kb, c, None), 0, qb,
                          (jnp.zeros((nh * tb, SB_W), F32), zero_run), 1, add_rows(dk_ref))
        (dq, _), dk_diag = left_to_right(qb, carry, diag)
        add_rows(dk_ref)(qb, dk_diag)
        dq_ref[...] = _head_pick(dq)
        finish()

    whole = pl.BlockSpec((S, SB_W), lambda i: (0, 0))
    blk = pl.BlockSpec((tb, SB_W), lambda i: (i, 0))
    return pl.pallas_call(
        body, name="sb_bwd", grid=(S // tb,),
        in_specs=[blk, whole, whole, blk, pl.BlockSpec((1, nb, nh * tb, tb), lambda i: (i, 0, 0, 0))] + r_in,
        out_specs=[blk, whole, whole] + r_out,
        out_shape=[jax.ShapeDtypeStruct((S, SB_W), F32)] * 3 + r_shape,
        scratch_shapes=[pltpu.VMEM((S // tb, nh * tb, tb), F32)] + r_scr,
        compiler_params=_cp(("arbitrary",)),
    )(q, k, v, do, wts, *r_args)


def reorder(name, x, groups, inverse):
    P, S, _ = x.shape

    def body(x_ref, o_ref):
        p = pl.program_id(0)
        for gi, r in enumerate(groups):
            @pl.when(p // 2 == gi)
            def _(r=r):
                L = S // r
                if r == 1:
                    o_ref[...] = x_ref[...]
                for c in range(r if r > 1 else 0):
                    if inverse:
                        o_ref[pl.ds(c, L, stride=r), :] = x_ref[c * L:(c + 1) * L, :]
                    else:
                        o_ref[c * L:(c + 1) * L, :] = x_ref[pl.ds(c, L, stride=r), :]

    slab = pl.BlockSpec((None, S, 128), lambda p: (p, 0, 0))
    return pl.pallas_call(
        body, name=name, grid=(P,), in_specs=[slab], out_specs=slab,
        out_shape=jax.ShapeDtypeStruct(x.shape, x.dtype), compiler_params=_cp(("parallel",)),
    )(x)


def _dil_blocks(S):
    return S // QBLK


def _dil_mask4(n_in_stream):
    qi = lax.rem(lax.broadcasted_iota(jnp.int32, (4 * QBLK, 2 * QBLK), 0), QBLK)
    kj = lax.broadcasted_iota(jnp.int32, (4 * QBLK, 2 * QBLK), 1) - QBLK
    dist = qi - kj
    return (dist >= 0) & (dist <= QBLK) & ((n_in_stream > 0) | (kj >= 0))


def _dil_lanes(ref):
    return jnp.concatenate([ref[0], ref[1]], axis=1)


def _dil_window(prev_ref, cur_ref):
    return jnp.concatenate([_dil_lanes(prev_ref), _dil_lanes(cur_ref)], axis=0).astype(BF16)


def _stream_pos(gi, i, S):
    nb = jnp.where(gi == 0, S // (QBLK * DIL[0]), jnp.where(gi == 1, S // (QBLK * DIL[1]), S // (QBLK * DIL[2])))
    return i % nb


def dil_fwd(q, k, v, bias, ride=None):
    S = q.shape[1]
    nblk = _dil_blocks(S)
    r_in, r_out, r_shape, r_scr, r_args = _ride_specs(ride)

    def body(*refs):
        gi, i = pl.program_id(0), pl.program_id(1)
        (q_ref, kc_ref, kp_ref, vc_ref, vp_ref, b_ref, o_ref, l_ref), finish = _riding(
            ride, refs, 6, 2, (gi == 0) & (i == 0), (gi == len(DIL) - 1) & (i == nblk - 1))
        mask = _dil_mask4(_stream_pos(gi, i, S))
        kw, vw = _dil_window(kp_ref, kc_ref), _dil_window(vp_ref, vc_ref)
        lg = _dg(_head_stack(_dil_lanes(q_ref)), kw, ((1,), (1,))) * (HEAD ** -0.5) + \
            b_ref[...].reshape(4 * QBLK, 2 * QBLK)
        lg = jnp.where(mask, lg, NEG_INF)
        m = jnp.max(lg, axis=-1, keepdims=True)
        p = jnp.exp(lg - m)
        den = jnp.sum(p, axis=-1, keepdims=True)
        o = _head_pick(_dg((p / den).astype(BF16), vw, ((1,), (0,))))
        lse = _head_pick(jnp.broadcast_to(m + jnp.log(den), (4 * QBLK, 4 * HEAD)))
        for j in range(2):
            o_ref[j] = o[:, 128 * j:128 * (j + 1)]
            l_ref[j] = lse[:, 128 * j:128 * (j + 1)]
        finish()

    cur = pl.BlockSpec((2, QBLK, 128), lambda g, i: (g, i, 0))
    prev = pl.BlockSpec((2, QBLK, 128), lambda g, i: (g, jnp.maximum(i - 1, 0), 0))
    return pl.pallas_call(
        body, name="dil_fwd", grid=(len(DIL), nblk),
        in_specs=[cur, cur, prev, cur, prev, pl.BlockSpec((4, QBLK, 2 * QBLK), lambda g, i: (g, 0, 0))] + r_in,
        out_specs=[cur, cur] + r_out,
        out_shape=[jax.ShapeDtypeStruct(q.shape, F32)] * 2 + r_shape,
        scratch_shapes=r_scr,
        compiler_params=_cp(("arbitrary", "arbitrary")),
    )(q, k, k, v, v, bias, *r_args)


def dil_bwd(q, k, v, bias, o, lse, do, dlse, ride=None):
    S = q.shape[1]
    nblk = _dil_blocks(S)
    r_in, r_out, r_shape, r_scr, r_args = _ride_specs(ride)

    def body(*refs):
        gi, i = pl.program_id(0), pl.program_id(1)
        (q_ref, kc_ref, kp_ref, vc_ref, vp_ref, b_ref, o_ref, l_ref, do_ref, dl_ref,
         dq_ref, dk_ref, dv_ref, ds_ref, dk_car, dv_car), finish = _riding(
            ride, refs, 10, 4, (gi == 0) & (i == 0), (gi == len(DIL) - 1) & (i == nblk))

        @pl.when(i == 0)
        def _():
            ds_ref[...] = jnp.zeros_like(ds_ref)
            dk_car[...] = jnp.zeros_like(dk_car)
            dv_car[...] = jnp.zeros_like(dv_car)

        @pl.when(i < nblk)
        def _():
            mask = _dil_mask4(_stream_pos(gi, i, S))
            kw, vw = _dil_window(kp_ref, kc_ref), _dil_window(vp_ref, vc_ref)
            qs = _head_stack(_dil_lanes(q_ref))
            do_nat = _dil_lanes(do_ref)
            dos = _head_stack(do_nat, F32)
            lse = jnp.sum(_head_stack(_dil_lanes(l_ref), F32), axis=-1, keepdims=True) * (1.0 / HEAD)
            lg = _dg(qs, kw, ((1,), (1,))) * (HEAD ** -0.5) + b_ref[...].reshape(4 * QBLK, 2 * QBLK)
            p = jnp.where(mask, jnp.exp(lg - lse), 0.0)
            dp = _dg(dos.astype(BF16), vw, ((1,), (1,)))
            four = lambda t: jnp.concatenate([t] * 4, axis=0)
            delta = jnp.sum(dos * four(_dil_lanes(o_ref)), axis=-1, keepdims=True)
            dl = jnp.sum(_head_stack(_dil_lanes(dl_ref), F32), axis=-1, keepdims=True)
            ds = p * (dp - delta + dl)
            ds_ref[...] += ds.reshape(4, QBLK, 2 * QBLK)
            dsq = (ds * (HEAD ** -0.5)).astype(BF16)
            dq = _head_pick(_dg(dsq, kw, ((1,), (0,))))
            dkw = _dg(dsq, qs, ((0,), (0,)))
            dvw = _dg(p.astype(BF16), dos.astype(BF16), ((0,), (0,)))
            for j in range(2):
                lanes = slice(128 * j, 128 * (j + 1))
                dq_ref[j] = dq[:, lanes]
                dk_ref[j] = dk_car[j] + dkw[:QBLK, lanes]
                dv_ref[j] = dv_car[j] + dvw[:QBLK, lanes]
                dk_car[j] = dkw[QBLK:, lanes]
                dv_car[j] = dvw[QBLK:, lanes]

        @pl.when(i == nblk)
        def _():
            dk_ref[...] = dk_car[...]
            dv_ref[...] = dv_car[...]

        finish()

    cur = pl.BlockSpec((2, QBLK, 128), lambda g, i: (g, jnp.minimum(i, nblk - 1), 0))
    prev = pl.BlockSpec((2, QBLK, 128), lambda g, i: (g, jnp.clip(i - 1, 0, nblk - 1), 0))
    bspec = pl.BlockSpec((4, QBLK, 2 * QBLK), lambda g, i: (g, 0, 0))
    return pl.pallas_call(
        body, name="dil_bwd", grid=(len(DIL), nblk + 1),
        in_specs=[cur, cur, prev, cur, prev, bspec, cur, cur, cur, cur] + r_in,
        out_specs=[cur, prev, prev, bspec] + r_out,
        out_shape=[jax.ShapeDtypeStruct(q.shape, F32)] * 3 + [jax.ShapeDtypeStruct(bias.shape, F32)] + r_shape,
        scratch_shapes=[pltpu.VMEM((2, QBLK, 128), F32), pltpu.VMEM((2, QBLK, 128), F32)] + r_scr,
        compiler_params=_cp(("arbitrary", "arbitrary")),
    )(q, k, k, v, v, bias, o, lse, do, dlse, *r_args)


def _t5_bucket(dist):
    max_exact = N_BUCKETS // 2
    d = jnp.maximum(dist, 1).astype(F32)
    large = max_exact + (jnp.log(d / max_exact) / math.log(MAX_DISTANCE / max_exact)
                         * (N_BUCKETS - max_exact)).astype(jnp.int32)
    large = jnp.minimum(large, N_BUCKETS - 1)
    return jnp.where(dist < max_exact, dist, large)


def _bucket_maps():
    qi = jnp.arange(QBLK)[:, None]
    kj = jnp.arange(2 * QBLK)[None, :] - QBLK
    dist = jnp.maximum(qi - kj, 0)
    return jnp.stack([_t5_bucket(dist * r) for r in DIL])


def bias_table(rel_bias, buckets):
    def body(tbl_ref, bk_ref, o_ref):
        for h in range(DL_HEADS):
            bk = bk_ref[h // 4]

            def step(b, acc):
                return jnp.where(bk == b, tbl_ref[b, h], acc)

            o_ref[h] = lax.fori_loop(0, N_BUCKETS, step, jnp.zeros(bk.shape, F32))

    return pl.pallas_call(
        body, name="bias_table", out_shape=jax.ShapeDtypeStruct((DL_HEADS,) + buckets.shape[1:], F32),
        in_specs=[pl.BlockSpec(memory_space=pltpu.SMEM), pl.BlockSpec(memory_space=pltpu.VMEM)],
        out_specs=pl.BlockSpec(memory_space=pltpu.VMEM),
    )(rel_bias, buckets)


def bias_grad(ds, buckets):
    def body(ds_ref, bk_ref, o_ref):
        lane = lax.broadcasted_iota(jnp.int32, (1, 128), 1)
        for h in range(DL_HEADS):
            dsv = ds_ref[h]
            bk = bk_ref[h // 4]

            def step(b, row):
                return jnp.where(lane == b, jnp.sum(jnp.where(bk == b, dsv, 0.0)), row)

            o_ref[h:h + 1, :] = lax.fori_loop(0, N_BUCKETS, step, jnp.zeros((1, 128), F32))

    return pl.pallas_call(
        body, name="bias_grad", out_shape=jax.ShapeDtypeStruct((DL_HEADS, 128), F32),
        in_specs=[pl.BlockSpec(memory_space=pltpu.VMEM)] * 2, out_specs=pl.BlockSpec(memory_space=pltpu.VMEM),
    )(ds, buckets)


def f_attn_out(x, oa, o, lse, w):
    og = [jnp.concatenate([o[2 * g], o[2 * g + 1]], axis=1) for g in range(3)]
    lg = [jnp.concatenate([lse[2 * g], lse[2 * g + 1]], axis=1) for g in range(3)]
    m = jnp.maximum(jnp.maximum(lg[0], lg[1]), lg[2])
    e = [jnp.exp(l - m) for l in lg]
    den = e[0] + e[1] + e[2]
    ob = (e[0] * og[0] + e[1] * og[1] + e[2] * og[2]) / den
    return x + mm(jnp.concatenate([oa, ob], axis=1), w)


def norm_shift_fwd(x, g, tm=256):
    S = x.shape[0]

    def body(x_ref, xp_ref, g_ref, h_ref, hs_ref):
        h = rms(x_ref[...], g_ref[...])
        hp = rms(xp_ref[7:8, :], g_ref[...])
        hp = jnp.where(pl.program_id(0) == 0, 0.0, hp)
        row = lax.broadcasted_iota(jnp.int32, (tm, D), 0)
        h_ref[...] = h
        hs_ref[...] = jnp.where(row == 0, hp, pltpu.roll(h, 1, 0))

    return pl.pallas_call(
        body, name="rw_norm_shift", grid=(S // tm,),
        in_specs=[pl.BlockSpec((tm, D), lambda t: (t, 0)),
                  pl.BlockSpec((8, D), lambda t: (jnp.maximum(t * (tm // 8) - 1, 0), 0)),
                  pl.BlockSpec((1, D), lambda t: (0, 0))],
        out_specs=[pl.BlockSpec((tm, D), lambda t: (t, 0))] * 2,
        out_shape=[jax.ShapeDtypeStruct((S, D), F32)] * 2,
        compiler_params=_cp(("parallel",)),
    )(x, x, g)


def norm_shift_bwd(x, g, dh, dhs, dres, tm=256):
    S = x.shape[0]
    nt = S // tm

    def body(x_ref, g_ref, dh_ref, dhs_ref, dhn_ref, dr_ref, dx_ref, dg_ref):
        t = pl.program_id(0)
        nxt = jnp.where(t == nt - 1, 0.0, dhn_ref[0:1, :])
        row = lax.broadcasted_iota(jnp.int32, (tm, D), 0)
        tot = dh_ref[...] + jnp.where(row == tm - 1, nxt, pltpu.roll(dhs_ref[...], tm - 1, 0))
        _, vjp = jax.vjp(rms, x_ref[...], g_ref[...])
        dx, dg = vjp(tot)
        dx_ref[...] = dr_ref[...] + dx

        @pl.when(t == 0)
        def _():
            dg_ref[...] = dg

        @pl.when(t != 0)
        def _():
            dg_ref[...] += dg

    tile = pl.BlockSpec((tm, D), lambda t: (t, 0))
    return pl.pallas_call(
        body, name="rw_norm_shift_bwd", grid=(nt,),
        in_specs=[tile, pl.BlockSpec((1, D), lambda t: (0, 0)), tile, tile,
                  pl.BlockSpec((8, D), lambda t: (jnp.minimum((t + 1) * (tm // 8), S // 8 - 1), 0)), tile],
        out_specs=[tile, pl.BlockSpec((1, D), lambda t: (0, 0))],
        out_shape=[jax.ShapeDtypeStruct((S, D), F32), jax.ShapeDtypeStruct((1, D), F32)],
        compiler_params=_cp(("arbitrary",)),
    )(x, g, dh, dhs, dhs, dres)


def f_rw_proj(h, hs, mix, w):
    return mm(h + (hs - h) * mix, w)


def f_rw_mid(h, hs, r, k, v, mix3, w0, a0, kkw, kaw, w1, w2, a1, a2, g1, g2):
    xx = hs - h
    xw, xa, xg = h + xx * mix3[0:1], h + xx * mix3[1:2], h + xx * mix3[2:3]
    w_log = -softplus(-(w0 + mm(jnp.tanh(mm(xw, w1)), w2))) - 0.5
    lw = -jnp.exp(w_log)
    ag = jax.nn.sigmoid(a0 + mm(mm(xa, a1), a2))
    gate = mm(jax.nn.sigmoid(mm(xg, g1)), g2)
    kk = k * kkw
    kk = kk / jnp.maximum(jnp.sqrt(group_sum(kk * kk, RW_H)), 1e-12)
    kmod = k * (1.0 + (ag - 1.0) * kaw)
    return (to_heads(r), to_heads(lw), to_heads(kmod), to_heads(v), to_heads(-kk), to_heads(kk * ag), gate)


def f_rw_post(yh, rh, kh, vh, gate, x, lng, lnb, rk, wo):
    mu = jnp.mean(yh, axis=-1, keepdims=True)
    var = jnp.mean(jnp.square(yh - mu), axis=-1, keepdims=True)
    yn = (yh - mu) * lax.rsqrt(var + GN_EPS)
    bonus = jnp.sum(rh * kh * rk, axis=-1, keepdims=True) * vh
    y = from_heads(yn) * lng + lnb + from_heads(bonus)
    return x + mm(y * gate, wo)


def _split2(x):
    hi = x.astype(BF16)
    return hi, (x - hi.astype(F32)).astype(BF16)


def _b3(x, y, cx, cy):
    xh, xl = _split2(x)
    yh, yl = _split2(y)
    x3 = jnp.concatenate([xh, xh, xl], axis=cx)
    y3 = jnp.concatenate([yh, yl, yh], axis=cy)
    return lax.dot_general(x3, y3, (((cx,), (cy,)), ((0,), (0,))), preferred_element_type=F32)


@jax.custom_vjp
def b_nt(x, y):
    return _b3(x, y, 2, 2)


@jax.custom_vjp
def b_nn(x, y):
    return _b3(x, y, 2, 1)


@jax.custom_vjp
def b_tn(x, y):
    return _b3(x, y, 1, 1)


def _b1(x, y, cx, cy):
    return lax.dot_general(x.astype(BF16), y.astype(BF16), (((cx,), (cy,)), ((0,), (0,))), preferred_element_type=F32)


b_nt.defvjp(lambda x, y: (b_nt(x, y), (x, y)), lambda r, g: (_b1(g, r[1], 2, 1), _b1(g, r[0], 1, 1)))
b_nn.defvjp(lambda x, y: (b_nn(x, y), (x, y)), lambda r, g: (_b1(g, r[1], 2, 2), _b1(r[0], g, 1, 1)))
b_tn.defvjp(lambda x, y: (b_tn(x, y), (x, y)), lambda r, g: (_b1(r[1], g, 2, 2), _b1(r[0], g, 2, 1)))


def _tri_apply(x, lower):
    H, C, _ = x.shape
    ii = lax.broadcasted_iota(jnp.int32, (C, C), 0)
    jj = lax.broadcasted_iota(jnp.int32, (C, C), 1)
    m = jnp.broadcast_to(((jj <= ii) if lower else (jj >= ii)).astype(BF16), (H, C, C))
    x1 = x.astype(BF16)
    r1 = x - x1.astype(F32)
    x2 = r1.astype(BF16)
    x3 = (r1 - x2.astype(F32)).astype(BF16)
    return lax.dot_general(jnp.concatenate([m, m, m], axis=2), jnp.concatenate([x1, x2, x3], axis=1),
                           (((2,), (1,)), ((0,), (0,))), preferred_element_type=F32)


@jax.custom_vjp
def run_sum(x):
    return _tri_apply(x, True)


run_sum.defvjp(lambda x: (run_sum(x), None), lambda _, g: (_tri_apply(g, False),))


def rwkv_chunk(S0, r, lw, k, v, a, b):
    H, C, _ = r.shape
    V = S0.shape[1]
    ii = lax.broadcasted_iota(jnp.int32, (C, C), 0)
    jj = lax.broadcasted_iota(jnp.int32, (C, C), 1)
    strict = jj < ii
    i2 = lax.broadcasted_iota(jnp.int32, (C, 2 * C), 0)
    j2 = lax.broadcasted_iota(jnp.int32, (C, 2 * C), 1)
    incl2 = jnp.where(j2 >= C, j2 - C, j2) <= i2
    g = run_sum(lw)
    ig = jnp.exp(-g)
    ar = jnp.concatenate([a * jnp.exp(g - lw), r * jnp.exp(g)], axis=1)
    bk = jnp.concatenate([b * ig, k * ig], axis=1)
    m = b_nt(ar, bk)
    a_ab = jnp.where(strict, m[:, :C, :C], 0.0)
    a_ak = jnp.where(strict, m[:, :C, C:], 0.0)
    b_r = jnp.where(incl2, m[:, C:, :], 0.0)
    p = b_nt(ar, S0)
    u = p[:, :C] + b_nn(a_ak, v)
    nmat, n = a_ab, 1
    while n < C:
        n *= 2
        if n < C:
            z = b_nn(nmat, jnp.concatenate([u, nmat], axis=2))
            u, nmat = u + z[:, :, :V], z[:, :, V:]
        else:
            u = u + b_nn(nmat, u)
    uv = jnp.concatenate([u, v], axis=1)
    y = p[:, C:] + b_nn(b_r, uv)
    g_end = g[:, C - 1:C, :]
    dec = jnp.exp(g_end - g)
    s_new = S0 * jnp.exp(g_end) + b_tn(uv, jnp.concatenate([b * dec, k * dec], axis=1))
    return y, s_new


def rwkv_fwd(r, lw, k, v, a, b):
    H, S, _ = r.shape
    C = RW_CHUNK

    def body(r_ref, lw_ref, k_ref, v_ref, a_ref, b_ref, y_ref, s_ref, s_scr):
        @pl.when(pl.program_id(0) == 0)
        def _():
            s_scr[...] = jnp.zeros_like(s_scr)

        s0 = s_scr[...]
        s_ref[0] = s0
        y, s1 = rwkv_chunk(s0, r_ref[...], lw_ref[...], k_ref[...], v_ref[...], a_ref[...], b_ref[...])
        y_ref[...] = y
        s_scr[...] = s1

    bs = pl.BlockSpec((H, C, HEAD), lambda c: (0, c, 0))
    return pl.pallas_call(
        body, name="rwkv_fwd", grid=(S // C,), in_specs=[bs] * 6,
        out_specs=[bs, pl.BlockSpec((1, H, HEAD, HEAD), lambda c: (c, 0, 0, 0))],
        out_shape=[jax.ShapeDtypeStruct((H, S, HEAD), F32), jax.ShapeDtypeStruct((S // C, H, HEAD, HEAD), F32)],
        scratch_shapes=[pltpu.VMEM((H, HEAD, HEAD), F32)],
        compiler_params=_cp(("arbitrary",)),
    )(r, lw, k, v, a, b)


def rwkv_bwd(r, lw, k, v, a, b, states, dy):
    H, S, _ = r.shape
    C = RW_CHUNK
    nc = S // C

    def body(r_ref, lw_ref, k_ref, v_ref, a_ref, b_ref, s_ref, dy_ref, dr, dlw, dk, dv, da, db, ds_scr):
        @pl.when(pl.program_id(0) == 0)
        def _():
            ds_scr[...] = jnp.zeros_like(ds_scr)

        _, vjp = jax.vjp(rwkv_chunk, s_ref[0], r_ref[...], lw_ref[...], k_ref[...], v_ref[...], a_ref[...], b_ref[...])
        grads = vjp((dy_ref[...], ds_scr[...]))
        ds_scr[...] = grads[0]
        for o, gv in zip((dr, dlw, dk, dv, da, db), grads[1:]):
            o[...] = gv

    bs = pl.BlockSpec((H, C, HEAD), lambda c: (0, nc - 1 - c, 0))
    return pl.pallas_call(
        body, name="rwkv_bwd", grid=(nc,),
        in_specs=[bs] * 6 + [pl.BlockSpec((1, H, HEAD, HEAD), lambda c: (nc - 1 - c, 0, 0, 0)), bs],
        out_specs=[bs] * 6, out_shape=[jax.ShapeDtypeStruct((H, S, HEAD), F32)] * 6,
        scratch_shapes=[pltpu.VMEM((H, HEAD, HEAD), F32)],
        compiler_params=_cp(("arbitrary",)),
    )(r, lw, k, v, a, b, states, dy)


def loss_head(y, target, tm=512):
    S = y.shape[0]

    def body(y_ref, t_ref, dy_ref, l_ref):
        e = y_ref[...] - t_ref[...]
        dy_ref[...] = e * (1.0 / D)
        part = jnp.broadcast_to(0.5 * jnp.sum(jnp.mean(e * e, axis=-1, keepdims=True)), (1, 128))

        @pl.when(pl.program_id(0) == 0)
        def _():
            l_ref[...] = part

        @pl.when(pl.program_id(0) != 0)
        def _():
            l_ref[...] += part

    tile = pl.BlockSpec((tm, D), lambda t: (t, 0))
    return pl.pallas_call(
        body, name="loss_head", grid=(S // tm,), in_specs=[tile, tile],
        out_specs=[tile, pl.BlockSpec((1, 128), lambda t: (0, 0))],
        out_shape=[jax.ShapeDtypeStruct((S, D), F32), jax.ShapeDtypeStruct((1, 128), F32)],
        compiler_params=_cp(("arbitrary",)),
    )(y, target)


def _row_tile(rows, cols, budget=1 << 19):
    best = None
    for tr in range(8, rows + 1, 8):
        if rows % tr == 0 and tr * cols <= budget:
            best = tr
    return best or rows


def _adam(w, g, m, v):
    m = ADAM_B1 * m + (1.0 - ADAM_B1) * g
    v = ADAM_B2 * v + (1.0 - ADAM_B2) * jnp.square(g)
    m_hat = m / (1.0 - ADAM_B1 ** ADAM_STEP)
    v_hat = v / (1.0 - ADAM_B2 ** ADAM_STEP)
    return -ADAM_LR * (m_hat / (jnp.sqrt(v_hat) + ADAM_EPS) + ADAM_WD * w), m, v


def sum_slots(name, parts, dtype=F32, extras=()):
    n = 0 if parts is None else parts.shape[0]
    R, C = extras[0].shape if parts is None else parts.shape[1:]
    tr = _row_tile(R, C * (n + len(extras)))
    ins = ([] if parts is None else [parts]) + list(extras)

    def body(*refs):
        terms = [] if parts is None else [refs[0][i] for i in range(n)]
        terms += [r[...] for r in refs[len(ins) - len(extras):len(ins)]]
        s = terms[0].astype(F32)
        for t in terms[1:]:
            s = s + t.astype(F32)
        refs[len(ins)][...] = s.astype(dtype)

    tile = pl.BlockSpec((tr, C), lambda t: (t, 0))
    return pl.pallas_call(
        body, name=name, grid=(R // tr,),
        in_specs=([] if parts is None else [pl.BlockSpec((n, tr, C), lambda t: (0, t, 0))]) + [tile] * len(extras),
        out_specs=tile, out_shape=jax.ShapeDtypeStruct((R, C), dtype), compiler_params=_cp(("parallel",)),
    )(*ins)


def sum_own_half(name, split, theirs, c, dtype):
    nq, _, rh, cols = split.shape
    tr = _row_tile(rh, 2 * cols)

    def body(c_ref, a_ref, b_ref, o_ref):
        o_ref[...] = (a_ref[...] + b_ref[...]).astype(dtype)

    tile = pl.BlockSpec((None, tr, cols), lambda q, t, c_ref: (q, t, 0))
    return pl.pallas_call(
        body, name=name,
        grid_spec=pltpu.PrefetchScalarGridSpec(
            num_scalar_prefetch=1, grid=(nq, rh // tr),
            in_specs=[pl.BlockSpec((None, None, tr, cols), lambda q, t, c_ref: (q, c_ref[0], t, 0)), tile],
            out_specs=tile),
        out_shape=jax.ShapeDtypeStruct((nq, rh, cols), dtype), compiler_params=_cp(("parallel", "parallel")),
    )(jnp.reshape(c, (1,)).astype(jnp.int32), split, theirs)


def sum_landed(name, landed, chip_sum, p):
    n, rh, cols = landed.shape
    tr = _row_tile(rh, (n + 1) * cols)

    def body(p_ref, l_ref, own_ref, o_ref):
        s = l_ref[0].astype(F32)
        for i in range(1, n):
            s = s + l_ref[i].astype(F32)
        o_ref[...] = s + own_ref[...].astype(F32)

    return pl.pallas_call(
        body, name=name,
        grid_spec=pltpu.PrefetchScalarGridSpec(
            num_scalar_prefetch=1, grid=(rh // tr,),
            in_specs=[pl.BlockSpec((n, tr, cols), lambda t, p_ref: (0, t, 0)),
                      pl.BlockSpec((None, tr, cols), lambda t, p_ref: (p_ref[0], t, 0))],
            out_specs=pl.BlockSpec((tr, cols), lambda t, p_ref: (t, 0))),
        out_shape=jax.ShapeDtypeStruct((rh, cols), F32), compiler_params=_cp(("parallel",)),
    )(jnp.reshape(p, (1,)).astype(jnp.int32), landed, chip_sum)


def adam_step(name, ga, gb, w, m, v):
    R, C = w.shape
    tr = _row_tile(R, C, 1 << 17)
    ins = [ga] + ([gb] if gb is not None else []) + [w, m, v]

    def body(*refs):
        g = refs[0][...]
        if gb is not None:
            g = g + refs[1][...]
        w_ref, m_ref, v_ref, g_out, d_out, m_out, v_out = refs[len(ins) - 3:]
        d, m2, v2 = _adam(w_ref[...], g, m_ref[...], v_ref[...])
        g_out[...] = g
        d_out[...] = d
        m_out[...] = m2
        v_out[...] = v2

    tile = pl.BlockSpec((tr, C), lambda t: (t, 0))
    return pl.pallas_call(
        body, name=name, grid=(R // tr,), in_specs=[tile] * len(ins), out_specs=[tile] * 4,
        out_shape=[jax.ShapeDtypeStruct((R, C), F32)] * 4, compiler_params=_cp(("parallel",)),
    )(*ins)


def adam_ffn(name, g_pieces, w, m, v, transposed=False):
    if transposed:
        res = adam_ffn(name, g_pieces, *(jnp.swapaxes(a, 2, 3) for a in (w, m, v)))
        return [jnp.swapaxes(r, 2, 3) for r in res]
    _, _, R, C = w.shape
    tr = _row_tile(R, 4 * C, 1 << 17)

    def body(g00, g01, g10, g11, w_ref, m_ref, v_ref, g_out, d_out, m_out, v_out):
        for l, j, g_ref in ((0, 0, g00), (0, 1, g01), (1, 0, g10), (1, 1, g11)):
            g = g_ref[...]
            d, m2, v2 = _adam(w_ref[l, j], g, m_ref[l, j], v_ref[l, j])
            g_out[l, j] = g
            d_out[l, j] = d
            m_out[l, j] = m2
            v_out[l, j] = v2

    piece = pl.BlockSpec((tr, C), lambda t: (t, 0))
    full = pl.BlockSpec((2, 2, tr, C), lambda t: (0, 0, t, 0))
    return pl.pallas_call(
        body, name=name, grid=(R // tr,), in_specs=[piece] * 4 + [full] * 3, out_specs=[full] * 4,
        out_shape=[jax.ShapeDtypeStruct(w.shape, F32)] * 4, compiler_params=_cp(("parallel",)),
    )(*g_pieces, w, m, v)


def _place():
    return lax.axis_index("x"), lax.axis_index("y"), lax.axis_index("c")


def _flip(me, mask):
    return tuple(1 - v if mk else v for v, mk in zip(me, mask))


CHIP_MASKS = ((1, 0, 0), (0, 1, 0), (1, 1, 0))
ALL_MASKS = tuple((a, b, c) for a in (0, 1) for b in (0, 1) for c in (0, 1) if (a, b, c) != (0, 0, 0))


def _chip(dev):
    return 2 * dev[0] + dev[1]


def _devno(dev):
    return 4 * dev[0] + 2 * dev[1] + dev[2]


class Pushes:
    def __init__(self, arrays, out_shapes, masks, copies, src_of, dst_of, alias=False):
        self.arrays, self.out_shapes, self.masks, self.copies = list(arrays), list(out_shapes), masks, copies
        self.src_of, self.dst_of, self.alias = src_of, dst_of, alias
        self.n = len(self.arrays)

    def sem_shapes(self):
        k = self.n * len(self.masks) * self.copies
        return [pltpu.SemaphoreType.DMA((k,)), pltpu.SemaphoreType.DMA((k,))]

    def ops(self, ins, outs, send_sems, recv_sems):
        me = _place()
        sends, lands = [], []
        for i in range(self.n):
            for j, mk in enumerate(self.masks):
                peer = _flip(me, mk)
                srcs, dsts = self.src_of(ins[i], me, j), self.dst_of(outs[i], me, j)
                here = self.dst_of(outs[i], peer, j)
                for q in range(self.copies):
                    sem = (i * len(self.masks) + j) * self.copies + q
                    sends.append(pltpu.make_async_remote_copy(
                        src_ref=srcs[q], dst_ref=dsts[q], send_sem=send_sems.at[sem], recv_sem=recv_sems.at[sem],
                        device_id=peer, device_id_type=MESH))
                    lands.append(pltpu.make_async_remote_copy(
                        src_ref=here[q], dst_ref=here[q], send_sem=send_sems.at[sem], recv_sem=recv_sems.at[sem],
                        device_id=peer, device_id_type=MESH))

        def start():
            for cp in sends:
                cp.start()

        def wait():
            for cp in lands:
                cp.wait_recv()
            for cp in sends:
                cp.wait_send()

        return start, wait


_HBM = pl.BlockSpec(memory_space=pl.ANY)


def exchange(name, p, local_of=None):
    n = p.n

    def body(*refs):
        ins, outs = refs[:n], refs[n:2 * n]
        start, wait = p.ops(ins, outs, refs[2 * n], refs[2 * n + 1])
        locals_ = []
        if local_of is not None:
            for i in range(n):
                src, dst = local_of(ins[i], outs[i], _place())
                locals_.append(pltpu.make_async_copy(src, dst, refs[2 * n + 2].at[i]))
                locals_[-1].start()
        start()
        wait()
        for cp in locals_:
            cp.wait()

    return pl.pallas_call(
        body, name=name, in_specs=[_HBM] * n, out_specs=[_HBM] * n, out_shape=p.out_shapes,
        scratch_shapes=p.sem_shapes() + ([pltpu.SemaphoreType.DMA((n,))] if local_of is not None else []),
        input_output_aliases={i: i for i in range(n)} if p.alias else {},
    )(*p.arrays)


def _half(c, rows):
    return pl.ds(c * (rows // 2), rows // 2)


def gather_pushes(arrays):
    outs = [jax.ShapeDtypeStruct((N_CHIPS,) + a.shape, a.dtype) for a in arrays]
    sib = len(CHIP_MASKS)
    return Pushes(arrays, outs, CHIP_MASKS + ((0, 0, 1),), 1,
                  src_of=lambda r, me, j: [r] if j == sib else [r.at[_half(me[2], r.shape[0])]],
                  dst_of=lambda o, sender, j: [o.at[_chip(sender)]] if j == sib else
                  [o.at[_chip(sender), _half(sender[2], o.shape[1])]])


def gather_swap(name, got):
    outs = [jax.ShapeDtypeStruct(a.shape, a.dtype) for a in got]
    return exchange(name, Pushes(
        got, outs, ((0, 0, 1),), len(CHIP_MASKS),
        src_of=lambda r, me, j: [r.at[_chip(_flip(me, mk)), _half(me[2], r.shape[1])] for mk in CHIP_MASKS],
        dst_of=lambda o, sender, j: [o.at[_chip(_flip(sender, mk)), _half(sender[2], o.shape[1])] for mk in CHIP_MASKS],
        alias=True))


def reduce_swap(arrays):
    split = [a.reshape(N_CHIPS, 2, a.shape[1] // 2, a.shape[2]) for a in arrays]
    half_shapes = [jax.ShapeDtypeStruct((N_CHIPS,) + a.shape[2:], F32) for a in split]
    return split, Pushes(split, half_shapes, ((0, 0, 1),), 1,
                         src_of=lambda r, me, j: [r.at[:, 1 - me[2]]], dst_of=lambda o, sender, j: [o])


def reduce_begin(tag, names, arrays, wire):
    split, pushes = reduce_swap(arrays)
    return reduce_sum(names, split, exchange(f"grad_pre_swap_{tag}", pushes), wire)


def reduce_sum(names, split, theirs, wire):
    c = lax.axis_index("c")
    chip_sum = [sum_own_half(f"sum2_{nm}", a, t, c, dt) for nm, a, t, dt in zip(names, split, theirs, wire)]
    pushes = Pushes(chip_sum, [jax.ShapeDtypeStruct((len(CHIP_MASKS),) + a.shape[1:], a.dtype) for a in chip_sum],
                    CHIP_MASKS, 1,
                    src_of=lambda r, me, j: [r.at[_chip(_flip(me, CHIP_MASKS[j]))]],
                    dst_of=lambda o, sender, j: [o.at[j]])
    return chip_sum, pushes


def reduce_end(tag, names, chip_sum, landed):
    x, y, c = _place()
    halves = [sum_landed(f"sum4_{nm}", p, a, _chip((x, y, c))) for nm, p, a in zip(names, landed, chip_sum)]
    others = exchange(f"grad_final_swap_{tag}", Pushes(
        halves, [jax.ShapeDtypeStruct(a.shape, F32) for a in halves], ((0, 0, 1),), 1,
        src_of=lambda r, me, j: [r], dst_of=lambda o, sender, j: [o]))
    return [jnp.concatenate([jnp.where(c == 0, h, o), jnp.where(c == 0, o, h)], axis=0) for h, o in zip(halves, others)]


def gather_all(arrays):
    outs = [jax.ShapeDtypeStruct((8,) + a.shape, a.dtype) for a in arrays]
    return exchange("gather_replicated", Pushes(
        arrays, outs, ALL_MASKS, 1, src_of=lambda r, me, j: [r], dst_of=lambda o, sender, j: [o.at[_devno(sender)]]),
        local_of=lambda r, o, me: (r, o.at[_devno(me)]))


def _unshard_cols(g):
    return jnp.transpose(g, (1, 0, 2)).reshape(g.shape[1], -1)


def _shard_cols(a):
    return jnp.transpose(a.reshape(a.shape[0], N_CHIPS, -1), (1, 0, 2))


class Weights(dict):
    def ride(self, kernel_name):
        return None

    def arrived(self, kernel_name, outs):
        pass


def _forward_backward(x, tgt, W, grads_early=None):
    S = x.shape[0]
    G = {}
    sd = jax.ShapeDtypeStruct

    hidden = {}

    def ffn(xin, l, j):
        out, *hidden[l, j] = ffn_fwd(xin, W["ffn_norm"][l][j], W["ffn_w_gate", l, j], W["ffn_w_up", l, j],
                                     W["ffn_w_down", l, j], l, j)
        return out

    def ffn_back(xin, dout, l, j):
        gn = W["ffn_norm"][l][j]
        dh, G["ffn_w_gate", l, j], G["ffn_w_up", l, j], G["ffn_w_down", l, j] = ffn_bwd(
            xin, gn, W["ffn_w_gate", l, j], W["ffn_w_up", l, j], W["ffn_w_down", l, j], dout, *hidden[l, j], l, j)
        dx, G[("ffn_norm", l, j)] = norm_bwd(f"ffn_norm_bwd_{l}{j}", xin, gn, dh, dout)
        return dx

    x0 = x
    x1 = ffn(x0, 0, 0)
    g0 = W["mix_norm"][0]
    sbq, sbk, sbv = tile_fwd(f_attn_sb, "attn_in_sb", [x1], [g0, W["attn_w_in"][0]], [sd((S, SB_W), F32)] * 3, 256)
    dl_shape = sd((DL_PAIRS, S, 128), F32)
    qn, = tile_fwd(f_attn_qk, "attn_in_q", [x1], [g0, W["attn_w_in"][1], W["attn_q_norm"]], [dl_shape], 256)
    kn, = tile_fwd(f_attn_qk, "attn_in_k", [x1], [g0, W["attn_w_in"][2], W["attn_k_norm"]], [dl_shape], 256)
    vv, = tile_fwd(f_attn_v, "attn_in_v", [x1], [g0, W["attn_w_in"][3]], [dl_shape], 256)
    oa, sb_wts, *rode = sb_fwd(sbq, sbk, sbv, W.ride("sb_fwd"))
    W.arrived("sb_fwd", rode)
    qs, ks, vs = (reorder(nm, t, DIL, False) for nm, t in (("sub_q", qn), ("sub_k", kn), ("sub_v", vv)))
    o_s, lse_s, *rode = dil_fwd(qs, ks, vs, W["bias_mat"], W.ride("dil_fwd"))
    W.arrived("dil_fwd", rode)
    o_n, lse_n = reorder("nat_o", o_s, DIL, True), reorder("nat_lse", lse_s, DIL, True)
    x2, = tile_fwd(f_attn_out, "attn_out", [x1, oa, o_n, lse_n], [W["attn_w_out"]], [sd((S, D), F32)], 256)
    x3 = ffn(x2, 0, 1)
    x4 = ffn(x3, 1, 0)
    g1 = W["mix_norm"][1]
    h, hs = norm_shift_fwd(x4, g1)
    mix = W["rw_mix"]
    r, = tile_fwd(f_rw_proj, "rw_proj_r", [h, hs], [mix[0:1], W["rw_wr"]], [sd((S, D), F32)], 256)
    k, = tile_fwd(f_rw_proj, "rw_proj_k", [h, hs], [mix[2:3], W["rw_wk"]], [sd((S, D), F32)], 256)
    v, = tile_fwd(f_rw_proj, "rw_proj_v", [h, hs], [mix[3:4], W["rw_wv"]], [sd((S, D), F32)], 256)
    mix3 = jnp.concatenate([mix[1:2], mix[4:5], mix[5:6]], axis=0)
    mid_w = [mix3, W["rw_w0"], W["rw_a0"], W["rw_kk"], W["rw_ka"], W["rw_w1"], W["rw_w2"], W["rw_a1"], W["rw_a2"],
             W["rw_g1"], W["rw_g2"]]
    hshape = sd((RW_H, S, HEAD), F32)
    mid_tiles = [h, hs, r, k, v]
    rh, lwh, kh, vh, ah, bh, gate = tile_fwd(f_rw_mid, "rw_mid", mid_tiles, mid_w, [hshape] * 6 + [sd((S, D), F32)], 128)
    yh, states = rwkv_fwd(rh, lwh, kh, vh, ah, bh)
    post_w = [W["rw_lnx_g"], W["rw_lnx_b"], W["rw_rk"], W["rw_wo"]]
    post_tiles = [yh, rh, kh, vh, gate, x4]
    x5, = tile_fwd(f_rw_post, "rw_post", post_tiles, post_w, [sd((S, D), F32)], 128)
    x6 = ffn(x5, 1, 1)
    dx6, loss_part = loss_head(x6, tgt)

    dx5 = ffn_back(x5, dx6, 1, 1)
    (dyh, drh, dkh, dvh, dgate, dx4), (d_lng, d_lnb, d_rk, d_wo) = tile_bwd(
        f_rw_post, "rw_post_bwd", post_tiles, post_w, [dx5], 128, [True] * 6, [True] * 4)
    drh2, dlwh, dkh2, dvh2, dah, dbh = rwkv_bwd(rh, lwh, kh, vh, ah, bh, states, dyh)
    mid_cts = [(drh, drh2), dlwh, (dkh, dkh2), (dvh, dvh2), dah, dbh, dgate]
    (dh, dhs, dr, dk, dv), dmid_w = tile_bwd(f_rw_mid, "rw_mid_bwd", mid_tiles, mid_w, mid_cts, 128,
                                             [True] * 5, [True] * len(mid_w))
    dmix = {}
    for nm, ct, row, wname in (("r", dr, 0, "rw_wr"), ("k", dk, 2, "rw_wk"), ("v", dv, 3, "rw_wv")):
        (dh, dhs), (dmix[row], G[wname]) = tile_bwd(
            f_rw_proj, f"rw_proj_{nm}_bwd", [h, hs], [mix[row:row + 1], W[wname]], [ct], 256,
            [True, True], [True, True], acc={0: dh, 1: dhs})
    dx4, G[("mix_norm", 1)] = norm_shift_bwd(x4, g1, dh, dhs, dx4)
    dmix3 = dmid_w[0]
    G["rw_mix"] = jnp.concatenate([dmix[0], dmix3[0:1], dmix[2], dmix[3], dmix3[1:2], dmix3[2:3]], axis=0)
    for nm, gv in zip(("rw_w0", "rw_a0", "rw_kk", "rw_ka", "rw_w1", "rw_w2", "rw_a1", "rw_a2", "rw_g1", "rw_g2"), dmid_w[1:]):
        G[nm] = gv
    G["rw_lnx_g"], G["rw_lnx_b"], G["rw_rk"], G["rw_wo"] = d_lng, d_lnb, d_rk, d_wo
    dx3 = ffn_back(x3, dx4, 1, 0)
    dx2 = ffn_back(x2, dx3, 0, 1)
    (dx1, doa, do_n, dlse_n), (G["attn_w_out"],) = tile_bwd(
        f_attn_out, "attn_out_bwd", [x1, oa, o_n, lse_n], [W["attn_w_out"]], [dx2], 256, [True] * 4, [True])
    do_s, dlse_s = reorder("sub_do", do_n, DIL, False), reorder("sub_dlse", dlse_n, DIL, False)
    ride, swapped = grads_early(G) if grads_early is not None else (None, None)
    dqs, dks, dvs, dsum, *rode = dil_bwd(qs, ks, vs, W["bias_mat"], o_s, lse_s, do_s, dlse_s, ride)
    ride, landed = swapped(rode) if swapped is not None else (None, None)
    G["rel_bias"] = bias_grad(dsum, W["buckets"])
    dqn, dkn, dvv = (reorder(nm, t, DIL, True) for nm, t in (("nat_dq", dqs), ("nat_dk", dks), ("nat_dv", dvs)))
    dsbq, dsbk, dsbv, *rode = sb_bwd(sbq, sbk, sbv, doa, sb_wts, ride)
    if landed is not None:
        landed(rode)
    dg0 = []
    dwin = []
    (dx1,), (dg, dw) = tile_bwd(f_attn_sb, "attn_in_sb_bwd", [x1], [g0, W["attn_w_in"][0]], [dsbq, dsbk, dsbv], 256,
                                [True], [True, True], acc={0: dx1})
    dg0.append(dg), dwin.append(dw)
    (dx1,), (dg, dw, G["attn_q_norm"]) = tile_bwd(f_attn_qk, "attn_in_q_bwd", [x1], [g0, W["attn_w_in"][1], W["attn_q_norm"]],
                                                  [dqn], 256, [True], [True] * 3, acc={0: dx1})
    dg0.append(dg), dwin.append(dw)
    (dx1,), (dg, dw, G["attn_k_norm"]) = tile_bwd(f_attn_qk, "attn_in_k_bwd", [x1], [g0, W["attn_w_in"][2], W["attn_k_norm"]],
                                                  [dkn], 256, [True], [True] * 3, acc={0: dx1})
    dg0.append(dg), dwin.append(dw)
    (dx1,), (dg, dw) = tile_bwd(f_attn_v, "attn_in_v_bwd", [x1], [g0, W["attn_w_in"][3]], [dvv], 256,
                                [True], [True, True], acc={0: dx1})
    dg0.append(dg), dwin.append(dw)
    G[("mix_norm", 0)] = dg0
    G["attn_w_in"] = dwin
    dx0 = ffn_back(x0, dx1, 0, 0)
    return loss_part, dx0, G


VEC_ROWS = ("ffn_norm", "rw_mix", "rw_w0", "rw_a0", "rw_kk", "rw_ka", "rw_lnx_g", "rw_lnx_b")


def kernel(x, ffn_norm, ffn_w_gate, ffn_w_up, ffn_w_down, mix_norm, rel_bias, attn_w_in, attn_q_norm, attn_k_norm, attn_w_out, rw_mix, rw_w0, rw_w1, rw_w2, rw_a0, rw_a1, rw_a2, rw_g1, rw_g2, rw_kk, rw_ka, rw_rk, rw_wr, rw_wk, rw_wv, rw_wo, rw_lnx_g, rw_lnx_b, loss_target, m_ffn_norm, m_ffn_w_gate, m_ffn_w_up, m_ffn_w_down, m_mix_norm, m_rel_bias, m_attn_w_in, m_attn_q_norm, m_attn_k_norm, m_attn_w_out, m_rw_mix, m_rw_w0, m_rw_w1, m_rw_w2, m_rw_a0, m_rw_a1, m_rw_a2, m_rw_g1, m_rw_g2, m_rw_kk, m_rw_ka, m_rw_rk, m_rw_wr, m_rw_wk, m_rw_wv, m_rw_wo, m_rw_lnx_g, m_rw_lnx_b, v_ffn_norm, v_ffn_w_gate, v_ffn_w_up, v_ffn_w_down, v_mix_norm, v_rel_bias, v_attn_w_in, v_attn_q_norm, v_attn_k_norm, v_attn_w_out, v_rw_mix, v_rw_w0, v_rw_w1, v_rw_w2, v_rw_a0, v_rw_a1, v_rw_a2, v_rw_g1, v_rw_g2, v_rw_kk, v_rw_ka, v_rw_rk, v_rw_wr, v_rw_wk, v_rw_wv, v_rw_wo, v_rw_lnx_g, v_rw_lnx_b):
    names = ["ffn_norm", "ffn_w_gate", "ffn_w_up", "ffn_w_down", "mix_norm", "rel_bias", "attn_w_in", "attn_q_norm",
             "attn_k_norm", "attn_w_out", "rw_mix", "rw_w0", "rw_w1", "rw_w2", "rw_a0", "rw_a1", "rw_a2", "rw_g1", "rw_g2",
             "rw_kk", "rw_ka", "rw_rk", "rw_wr", "rw_wk", "rw_wv", "rw_wo", "rw_lnx_g", "rw_lnx_b"]
    loc = locals()
    w = {n: loc[n] for n in names}
    mom = {n: loc["m_" + n] for n in names}
    vel = {n: loc["v_" + n] for n in names}
    S = x.shape[1]

    ffn3 = ("ffn_w_gate", "ffn_w_up", "ffn_w_down")
    rw_mats = ("rw_w1", "rw_w2", "rw_a1", "rw_a2", "rw_g1", "rw_g2", "rw_wr", "rw_wk", "rw_wv", "rw_wo")
    cols_split = ("attn_w_out", "rw_w2", "rw_a2", "rw_g2")
    shard = {"vec": jnp.concatenate([w[n].reshape(-1, 256) for n in VEC_ROWS], axis=0)}
    for n in ffn3:
        for l in range(2):
            for j in range(2):
                shard[n, l, j] = w[n][l, j].astype(BF16)
    for n in ("attn_w_in", "attn_w_out") + rw_mats:
        shard[n] = w[n].reshape(-1, w[n].shape[-1]).astype(BF16)
    ffn_keys = lambda l, j: [(n, l, j) for n in ffn3]
    w_groups = {"first": ["vec"] + ffn_keys(0, 0) + ["attn_w_in", "attn_w_out"],
                "sb_fwd": ffn_keys(0, 1) + ffn_keys(1, 0) + list(rw_mats),
                "dil_fwd": ffn_keys(1, 1)}
    label = lambda key: key if isinstance(key, str) else f"{key[0]}_{key[1]}{key[2]}"

    class Streamed(Weights):
        def ride(self, kernel_name):
            return gather_pushes([shard[k] for k in w_groups[kernel_name]])

        def arrived(self, kernel_name, outs):
            for key, g in zip(w_groups[kernel_name], gather_swap(f"gather_swap_{kernel_name}", outs)):
                if key == "vec":
                    vec_full = _unshard_cols(g)
                    self["ffn_norm"] = [[vec_full[2 * l + j][None] for j in range(2)] for l in range(2)]
                    self["rw_mix"] = vec_full[4:10]
                    for i, n in enumerate(("rw_w0", "rw_a0", "rw_kk", "rw_ka", "rw_lnx_g", "rw_lnx_b")):
                        self[n] = vec_full[10 + i][None]
                elif key == "attn_w_in":
                    self[key] = [g[p] for p in range(N_CHIPS)]
                elif key in cols_split:
                    self[key] = _unshard_cols(g)
                elif isinstance(key, str):
                    self[key] = g.reshape(D, -1)
                else:
                    self[key] = g

    buckets = _bucket_maps()
    W = Streamed({"mix_norm": [mix_norm[0:1], mix_norm[1:2]], "attn_q_norm": attn_q_norm, "attn_k_norm": attn_k_norm,
                  "rw_rk": rw_rk[0][:, None, :], "buckets": buckets, "bias_mat": bias_table(rel_bias, buckets)})
    W.arrived("first", exchange("gather_weights", W.ride("first")))

    def slots(key, G):
        if key == "vec":
            rows = [G[("ffn_norm", l, j)] for l in range(2) for j in range(2)] + [G["rw_mix"]] + \
                   [G[n] for n in ("rw_w0", "rw_a0", "rw_kk", "rw_ka", "rw_lnx_g", "rw_lnx_b")]
            return _shard_cols(jnp.concatenate(rows, axis=0))
        if key == "attn_w_in":
            return jnp.stack(G[key])
        if key in cols_split:
            return _shard_cols(G[key])
        if isinstance(key, str):
            return G[key].reshape(N_CHIPS, D // N_CHIPS, -1)
        return G[key]

    g_groups = {"early": ffn_keys(1, 1) + ffn_keys(1, 0) + ffn_keys(0, 1) + list(rw_mats) + ["attn_w_out"],
                "late": ["vec", "attn_w_in"] + ffn_keys(0, 0)}
    wire = lambda keys: [F32 if k == "vec" else BF16 for k in keys]
    part = {}

    def grads_early(G):
        keys = g_groups["early"]
        names_ = [label(k) for k in keys]
        split, swap_pushes = reduce_swap([slots(k, G) for k in keys])

        def swapped(theirs):
            chip_sum, pushes = reduce_sum(names_, split, theirs, wire(keys))
            return pushes, lambda landed: part.update(zip(keys, reduce_end("early", names_, chip_sum, landed)))

        return swap_pushes, swapped

    loss_part, dx, G = _forward_backward(x[0], loss_target[0], W, grads_early)
    loss = lax.psum(loss_part[0, 0], ("x", "y", "c"))
    keys = g_groups["late"]
    chip_sum, pushes = reduce_begin("late", [label(k) for k in keys], [slots(k, G) for k in keys], wire(keys))
    part.update(zip(keys, reduce_end("late", [label(k) for k in keys], chip_sum, exchange("scatter_grads", pushes))))

    rep = jnp.concatenate([G[("mix_norm", 0)][0] + G[("mix_norm", 0)][1] + G[("mix_norm", 0)][2] + G[("mix_norm", 0)][3],
                           G[("mix_norm", 1)]], axis=0).reshape(16, 128)
    rep = jnp.concatenate([rep, G["rel_bias"], jnp.pad(G["attn_q_norm"], ((0, 0), (0, 64))),
                           jnp.pad(G["attn_k_norm"], ((0, 0), (0, 64))), G["rw_rk"].reshape(8, 128),
                           jnp.zeros((2, 128), F32)], axis=0)
    rep_sum = sum_slots("sum_replicated", gather_all([rep])[0])
    g_rep = {
        "mix_norm": rep_sum[0:16].reshape(2, D),
        "rel_bias": jnp.transpose(rep_sum[16:28, :N_BUCKETS]),
        "attn_q_norm": rep_sum[28:29, :HEAD], "attn_k_norm": rep_sum[29:30, :HEAD],
        "rw_rk": rep_sum[30:38].reshape(1, RW_H, HEAD),
    }

    out = {}

    def adam(n, ga, gb):
        shp = w[n].shape
        to2 = lambda a: a.reshape(-1, shp[-1])
        res = adam_step(f"adam_{n}", to2(ga), None if gb is None else to2(gb), to2(w[n]), to2(mom[n]), to2(vel[n]))
        out[n] = tuple(r.reshape(shp) for r in res)

    for n in ffn3:
        out[n] = tuple(adam_ffn(f"adam_{n}", [part[n, l, j] for l in range(2) for j in range(2)], w[n], mom[n], vel[n],
                                transposed=n != "ffn_w_down"))
    for n in ("attn_w_in", "attn_w_out") + rw_mats:
        adam(n, part[n], None)
    rows = {"ffn_norm": (0, 4), "rw_mix": (4, 10), "rw_w0": (10, 11), "rw_a0": (11, 12), "rw_kk": (12, 13),
            "rw_ka": (13, 14), "rw_lnx_g": (14, 15), "rw_lnx_b": (15, 16)}
    for n, (lo, hi) in rows.items():
        adam(n, part["vec"][lo:hi], None)
    for n, gv in g_rep.items():
        adam(n, gv, None)

    grads = [out[n][0] for n in names]
    deltas = [out[n][1] for n in names]
    new_m = [out[n][2] for n in names]
    new_v = [out[n][3] for n in names]
    return (loss, dx[None], *grads, *deltas, *new_m, *new_v)
```

```python
import functools
import math

import jax
import jax.numpy as jnp
from jax import lax
from jax.experimental import pallas as pl
from jax.experimental.pallas import tpu as pltpu

F32, BF16 = jnp.float32, jnp.bfloat16
HI = lax.Precision.HIGHEST
MESH = pl.DeviceIdType.MESH

D = 1024
HEAD = 64
N_CHIPS = 4
FF_SHARD = 704
SB_W = 256
DL_HEADS = 12
DL_PAIRS = 6
DIL = (1, 4, 16)
QBLK = 128
N_BUCKETS = 32
MAX_DISTANCE = 2048
RW_H = 16
RW_CHUNK = 64
NORM_EPS = 1e-6
GN_EPS = 64e-5
NEG_INF = -1e30
VMEM_LIMIT = 56 * 1024 * 1024

ADAM_LR, ADAM_B1, ADAM_B2, ADAM_EPS, ADAM_WD, ADAM_STEP = 0.001, 0.9, 0.999, 1e-08, 0.01, 10


def _cp(sem):
    return pltpu.CompilerParams(dimension_semantics=sem, vmem_limit_bytes=VMEM_LIMIT)


def _dg(a, b, dims, prec=None):
    return lax.dot_general(a, b, (dims, ((), ())), precision=prec, preferred_element_type=F32)


def _bdot(a, b, dims):
    return _dg(a.astype(BF16), b.astype(BF16), dims)


@jax.custom_vjp
def mm(a, b):
    return _bdot(a, b, ((1,), (0,)))


def _mm_fwd(a, b):
    return _bdot(a, b, ((1,), (0,))), (a, b)


def _mm_bwd(res, g):
    a, b = res
    return _bdot(g, b, ((1,), (1,))), _bdot(a, g, ((0,), (0,)))


mm.defvjp(_mm_fwd, _mm_bwd)


def rms(x, g):
    return x * lax.rsqrt(jnp.mean(x * x, axis=-1, keepdims=True) + NORM_EPS) * g


def _pieces(x):
    x1 = x.astype(BF16)
    r1 = x - x1.astype(F32)
    x2 = r1.astype(BF16)
    return jnp.concatenate([x1, x2, (r1 - x2.astype(F32)).astype(BF16)], axis=-1)


def _group_sum(x, nh):
    w = x.shape[-1]
    e = (lax.broadcasted_iota(jnp.int32, (w, nh), 0) // HEAD == lax.broadcasted_iota(jnp.int32, (w, nh), 1)).astype(BF16)
    s = _dg(_pieces(x), jnp.concatenate([e, e, e], axis=0), ((1,), (0,)))
    return _dg(_pieces(s), jnp.concatenate([e, e, e], axis=1), ((1,), (1,)))


@functools.partial(jax.custom_vjp, nondiff_argnums=(1,))
def group_sum(x, nh):
    return _group_sum(x, nh)


group_sum.defvjp(lambda x, nh: (_group_sum(x, nh), None), lambda nh, _, g: (_group_sum(g, nh),))


def softplus(u):
    return jnp.maximum(u, 0.0) + jnp.log1p(jnp.exp(-jnp.abs(u)))


def to_heads(t, nh=RW_H):
    return jnp.stack([t[:, HEAD * h:HEAD * (h + 1)] for h in range(nh)])


def from_heads(t):
    return jnp.concatenate([t[h] for h in range(t.shape[0])], axis=-1)


def _tile_spec(shape, tm):
    if len(shape) == 2:
        return pl.BlockSpec((tm, shape[1]), lambda t: (t, 0))
    return pl.BlockSpec((shape[0], tm, shape[2]), lambda t: (0, t, 0))


def _full_spec(shape):
    nd = len(shape)
    return pl.BlockSpec(tuple(shape), lambda t: (0,) * nd)


def _rows(a):
    return a.shape[0] if a.ndim == 2 else a.shape[1]


def tile_fwd(f, name, tiles, weights, outs, tm):
    nt, nw = len(tiles), len(weights)

    def body(*refs):
        tv = [r[...] for r in refs[:nt]]
        wv = [r[...].astype(F32) for r in refs[nt:nt + nw]]
        res = f(*tv, *wv)
        if not isinstance(res, (tuple, list)):
            res = (res,)
        for o, v in zip(refs[nt + nw:], res):
            o[...] = v.astype(o.dtype)

    return pl.pallas_call(
        body, name=name, grid=(_rows(tiles[0]) // tm,),
        in_specs=[_tile_spec(a.shape, tm) for a in tiles] + [_full_spec(w.shape) for w in weights],
        out_specs=[_tile_spec(o.shape, tm) for o in outs],
        out_shape=list(outs),
        compiler_params=_cp(("parallel",)),
    )(*tiles, *weights)


def tile_bwd(f, name, tiles, weights, cts, tm, dt, dw, acc=None):
    acc = acc or {}
    groups = [c if isinstance(c, tuple) else (c,) for c in cts]
    cts = [a for grp in groups for a in grp]
    nt, nw, nc = len(tiles), len(weights), len(cts)
    acc_idx = sorted(acc)
    na = len(acc_idx)
    dti = [i for i in range(nt) if dt[i]]
    dwi = [i for i in range(nw) if dw[i]]

    def body(*refs):
        tv = [r[...] for r in refs[:nt]]
        wv = [r[...].astype(F32) for r in refs[nt:nt + nw]]
        crefs = list(refs[nt + nw:nt + nw + nc])
        cv = []
        for grp in groups:
            terms = [crefs.pop(0)[...] for _ in grp]
            cv.append(functools.reduce(lambda a, b: a + b, terms))
        av = {i: r[...] for i, r in zip(acc_idx, refs[nt + nw + nc:nt + nw + nc + na])}
        orefs = refs[nt + nw + nc + na:]

        def g(*diff):
            t2, w2 = list(tv), list(wv)
            for i, v in zip(dti, diff[:len(dti)]):
                t2[i] = v
            for i, v in zip(dwi, diff[len(dti):]):
                w2[i] = v
            res = f(*t2, *w2)
            return tuple(res) if isinstance(res, (tuple, list)) else (res,)

        _, vjp = jax.vjp(g, *[tv[i] for i in dti], *[wv[i] for i in dwi])
        grads = vjp(tuple(cv))
        for k, i in enumerate(dti):
            gt = grads[k]
            if i in av:
                gt = gt + av[i]
            orefs[k][...] = gt
        first = pl.program_id(0) == 0
        for k, i in enumerate(dwi):
            o = orefs[len(dti) + k]
            gw = grads[len(dti) + k]

            @pl.when(first)
            def _(o=o, gw=gw):
                o[...] = gw

            @pl.when(jnp.logical_not(first))
            def _(o=o, gw=gw):
                o[...] += gw

    out_shape = [jax.ShapeDtypeStruct(tiles[i].shape, F32) for i in dti] + \
                [jax.ShapeDtypeStruct(weights[i].shape, F32) for i in dwi]
    res = pl.pallas_call(
        body, name=name, grid=(_rows(tiles[0]) // tm,),
        in_specs=[_tile_spec(a.shape, tm) for a in tiles] + [_full_spec(w.shape) for w in weights] +
                 [_tile_spec(c.shape, tm) for c in cts] + [_tile_spec(tiles[i].shape, tm) for i in acc_idx],
        out_specs=[_tile_spec(tiles[i].shape, tm) for i in dti] + [_full_spec(weights[i].shape) for i in dwi],
        out_shape=out_shape,
        compiler_params=_cp(("arbitrary",)),
    )(*tiles, *weights, *cts, *[acc[i] for i in acc_idx])
    return list(res[:len(dti)]), list(res[len(dti):])


def _ffn_wspec(rows, cols, cfirst):
    if cfirst:
        return pl.BlockSpec((1, rows, cols), lambda c, t: (c, 0, 0))
    return pl.BlockSpec((1, rows, cols), lambda t, c: (c, 0, 0))


def ffn_fwd(x, g, wg, wu, wd, l, j, tm=1024):
    S = x.shape[0]

    def body(x_ref, g_ref, wg_ref, wu_ref, wd_ref, o_ref, a_ref, b_ref, h_ref, acc_ref):
        c = pl.program_id(1)

        @pl.when(c == 0)
        def _():
            h_ref[...] = rms(x_ref[...], g_ref[...]).astype(BF16)
            acc_ref[...] = jnp.zeros_like(acc_ref)

        h = h_ref[...]
        a = _bdot(h, wg_ref[0], ((1,), (0,)))
        b = _bdot(h, wu_ref[0], ((1,), (0,)))
        a_ref[0] = a.astype(BF16)
        b_ref[0] = b.astype(BF16)
        y = a * jax.nn.sigmoid(a) * b
        acc_ref[...] += _bdot(y, wd_ref[0], ((1,), (0,)))

        @pl.when(c == N_CHIPS - 1)
        def _():
            o_ref[...] = x_ref[...] + 0.5 * acc_ref[...]

    hid = pl.BlockSpec((1, tm, FF_SHARD), lambda t, c: (c, t, 0))
    return pl.pallas_call(
        body, name=f"ffn_fwd_{l}{j}", grid=(S // tm, N_CHIPS),
        in_specs=[pl.BlockSpec((tm, D), lambda t, c: (t, 0)), pl.BlockSpec((1, D), lambda t, c: (0, 0)),
                  _ffn_wspec(D, FF_SHARD, False), _ffn_wspec(D, FF_SHARD, False), _ffn_wspec(FF_SHARD, D, False)],
        out_specs=[pl.BlockSpec((tm, D), lambda t, c: (t, 0)), hid, hid],
        out_shape=[jax.ShapeDtypeStruct((S, D), F32)] + [jax.ShapeDtypeStruct((N_CHIPS, S, FF_SHARD), BF16)] * 2,
        scratch_shapes=[pltpu.VMEM((tm, D), BF16), pltpu.VMEM((tm, D), F32)],
        compiler_params=_cp(("parallel", "arbitrary")),
    )(x, g, wg, wu, wd)


def ffn_bwd(x, g, wg, wu, wd, dout, a_sav, b_sav, l, j, tm=512):
    S = x.shape[0]

    def body(x_ref, g_ref, wg_ref, wu_ref, wd_ref, do_ref, a_ref, b_ref, dh_ref, dwg_ref, dwu_ref, dwd_ref):
        t = pl.program_id(1)
        h = rms(x_ref[...], g_ref[...]).astype(BF16)
        wgv, wuv, wdv = wg_ref[0], wu_ref[0], wd_ref[0]
        a = a_ref[0].astype(F32)
        b = b_ref[0].astype(F32)
        sig = jax.nn.sigmoid(a)
        s = a * sig
        dyd = 0.5 * do_ref[...]
        dy = _bdot(dyd, wdv, ((1,), (1,)))
        dwd = _bdot(s * b, dyd, ((0,), (0,)))
        db = dy * s
        da = dy * b * (sig * (1.0 + a * (1.0 - sig)))
        dwg = _bdot(da, h, ((0,), (0,)))
        dwu = _bdot(db, h, ((0,), (0,)))
        dh_ref[0] = (_bdot(da, wgv, ((1,), (1,))) + _bdot(db, wuv, ((1,), (1,)))).astype(dh_ref.dtype)

        @pl.when(t == 0)
        def _():
            dwg_ref[0] = dwg
            dwu_ref[0] = dwu
            dwd_ref[0] = dwd

        @pl.when(t != 0)
        def _():
            dwg_ref[0] += dwg
            dwu_ref[0] += dwu
            dwd_ref[0] += dwd

    return pl.pallas_call(
        body, name=f"ffn_bwd_{l}{j}", grid=(N_CHIPS, S // tm),
        in_specs=[pl.BlockSpec((tm, D), lambda c, t: (t, 0)), pl.BlockSpec((1, D), lambda c, t: (0, 0)),
                  _ffn_wspec(D, FF_SHARD, True), _ffn_wspec(D, FF_SHARD, True), _ffn_wspec(FF_SHARD, D, True),
                  pl.BlockSpec((tm, D), lambda c, t: (t, 0)),
                  pl.BlockSpec((1, tm, FF_SHARD), lambda c, t: (c, t, 0)), pl.BlockSpec((1, tm, FF_SHARD), lambda c, t: (c, t, 0))],
        out_specs=[pl.BlockSpec((1, tm, D), lambda c, t: (c, t, 0))] + [_ffn_wspec(FF_SHARD, D, True)] * 3,
        out_shape=[jax.ShapeDtypeStruct((N_CHIPS, S, D), BF16)] + [jax.ShapeDtypeStruct(wd.shape, F32)] * 3,
        compiler_params=_cp(("parallel", "arbitrary")),
    )(x, g, wg, wu, wd, dout, a_sav, b_sav)


def norm_bwd(name, x, g, dh_parts, dres, tm=256):
    S = x.shape[0]
    P = dh_parts.shape[0]

    def body(x_ref, g_ref, dh_ref, dr_ref, dx_ref, dg_ref):
        dh = dh_ref[0].astype(F32)
        for p in range(1, P):
            dh = dh + dh_ref[p].astype(F32)
        _, vjp = jax.vjp(rms, x_ref[...], g_ref[...])
        dx, dg = vjp(dh)
        dx_ref[...] = dr_ref[...] + dx

        @pl.when(pl.program_id(0) == 0)
        def _():
            dg_ref[...] = dg

        @pl.when(pl.program_id(0) != 0)
        def _():
            dg_ref[...] += dg

    return pl.pallas_call(
        body, name=name, grid=(S // tm,),
        in_specs=[pl.BlockSpec((tm, D), lambda t: (t, 0)), pl.BlockSpec((1, D), lambda t: (0, 0)),
                  pl.BlockSpec((P, tm, D), lambda t: (0, t, 0)), pl.BlockSpec((tm, D), lambda t: (t, 0))],
        out_specs=[pl.BlockSpec((tm, D), lambda t: (t, 0)), pl.BlockSpec((1, D), lambda t: (0, 0))],
        out_shape=[jax.ShapeDtypeStruct((S, D), F32), jax.ShapeDtypeStruct((1, D), F32)],
        compiler_params=_cp(("arbitrary",)),
    )(x, g, dh_parts, dres)


def f_attn_sb(x, g, w):
    pr = mm(rms(x, g), w)
    return pr[:, :SB_W], pr[:, SB_W:2 * SB_W], pr[:, 2 * SB_W:]


def _pairs(y):
    return jnp.stack([y[:, 128 * j:128 * (j + 1)] for j in range(DL_PAIRS)])


def f_attn_qk(x, g, w, nrm):
    pr = mm(rms(x, g), w)
    ms = group_sum(pr * pr, DL_HEADS) * (1.0 / HEAD)
    return _pairs(pr * lax.rsqrt(ms + NORM_EPS) * jnp.concatenate([nrm] * DL_HEADS, axis=1))


def f_attn_v(x, g, w):
    return _pairs(mm(rms(x, g), w))


def _masked(strict, x):
    return x if strict is None else jnp.where(strict, x, 0.0)


def _head_stack(x, dtype=BF16):
    nh = x.shape[1] // HEAD
    lane_head = lax.broadcasted_iota(jnp.int32, (1, x.shape[1]), 1) // HEAD
    return jnp.concatenate([jnp.where(lane_head == h, x, 0.0) for h in range(nh)], axis=0).astype(dtype)


def _head_pick(xs):
    nh = xs.shape[1] // HEAD
    rows = xs.shape[0] // nh
    lane_head = lax.broadcasted_iota(jnp.int32, (1, xs.shape[1]), 1) // HEAD
    out = xs[:rows]
    for h in range(1, nh):
        out = jnp.where(lane_head == h, xs[rows * h:rows * (h + 1)], out)
    return out


def _sb_tiles(qs, kblk, strict):
    z = _dg(qs, kblk, ((1,), (1,))) * (HEAD ** -0.5)
    keep = -(jnp.maximum(z, 0.0) + jnp.log(1.0 + jnp.exp(-jnp.abs(z))))
    return z, _masked(strict, keep)


def _tri(n, upper):
    r = lax.broadcasted_iota(jnp.int32, (n, n), 0)
    c = lax.broadcasted_iota(jnp.int32, (n, n), 1)
    return ((r > c) if upper else (r < c)).astype(BF16)


def _tri_sums(x, tri):
    hi, lo = _split2(x)
    return _dg(jnp.concatenate([hi, lo], axis=1), jnp.concatenate([tri, tri], axis=0), ((1,), (0,)))


SB_UNROLL = 8


def _sb_diag(tb, nh):
    r = lax.broadcasted_iota(jnp.int32, (nh * tb, tb), 0)
    return lax.broadcasted_iota(jnp.int32, (nh * tb, tb), 1) < lax.rem(r, tb)


def _sb_sweep(step, first, count, carry, direction, commit=None):
    def run(kbs, c):
        outs = []
        for kb in kbs:
            c, out = step(kb, c)
            outs.append(out)
        if commit is not None:
            for kb, out in zip(kbs, outs):
                commit(kb, out)
        return c

    pos, size = first, 1
    while size < SB_UNROLL:
        n = (count // size) % 2
        carry = lax.fori_loop(
            0, n, lambda i, c, pos=pos, size=size: run([pos + direction * u for u in range(size)], c), carry)
        pos, size = pos + direction * size * n, 2 * size
    return lax.fori_loop(
        0, count // SB_UNROLL,
        lambda g, c: run([pos + direction * (SB_UNROLL * g + u) for u in range(SB_UNROLL)], c), carry)


def _riding(ride, refs, n_in, n_out, first, last):
    if ride is None:
        return refs, lambda: None
    n = ride.n
    own = refs[:n_in] + refs[n_in + n:n_in + n + n_out] + refs[n_in + 2 * n + n_out:len(refs) - 2]
    start, wait = ride.ops(refs[n_in:n_in + n], refs[n_in + n + n_out:n_in + 2 * n + n_out], refs[-2], refs[-1])
    pl.when(first)(start)
    return own, lambda: pl.when(last)(wait)


def _ride_specs(ride):
    if ride is None:
        return [], [], [], [], []
    return [_HBM] * ride.n, [_HBM] * ride.n, ride.out_shapes, ride.sem_shapes(), ride.arrays


def sb_fwd(q, k, v, ride=None, tb=QBLK):
    S = q.shape[0]
    nh = SB_W // HEAD
    nb = S // tb
    r_in, r_out, r_shape, r_scr, r_args = _ride_specs(ride)

    def body(*refs):
        qb = pl.program_id(0)
        (q_ref, k_ref, v_ref, o_ref, w_ref), finish = _riding(ride, refs, 3, 2, qb == 0, qb == nb - 1)
        diag = _sb_diag(tb, nh)
        after_mat = _tri(tb, True)
        qs = _head_stack(q_ref[...])

        def step(kb, carry, strict):
            acc, run = carry
            rows = pl.ds(pl.multiple_of(kb * tb, tb), tb)
            z, keep = _sb_tiles(qs, k_ref[rows, :].astype(BF16), strict)
            w = _masked(strict, jnp.exp(z + keep + _tri_sums(keep, after_mat) + run)).astype(BF16)
            w_ref[0, kb] = w
            acc = acc + _dg(w, v_ref[rows, :].astype(BF16), ((1,), (0,)))
            return acc, run + jnp.sum(keep, axis=1, keepdims=True)

        init = (jnp.zeros((nh * tb, SB_W), F32), jnp.zeros((nh * tb, 1), F32))
        carry = step(qb, init, diag)
        acc, _ = _sb_sweep(lambda kb, c: (step(kb, c, None), None), qb - 1, qb, carry, -1)
        o_ref[...] = _head_pick(acc)
        finish()

    return pl.pallas_call(
        body, name="sb_fwd", grid=(S // tb,),
        in_specs=[pl.BlockSpec((tb, SB_W), lambda i: (i, 0)), pl.BlockSpec((S, SB_W), lambda i: (0, 0)),
                  pl.BlockSpec((S, SB_W), lambda i: (0, 0))] + r_in,
        out_specs=[pl.BlockSpec((tb, SB_W), lambda i: (i, 0)),
                   pl.BlockSpec((1, nb, nh * tb, tb), lambda i: (i, 0, 0, 0))] + r_out,
        out_shape=[jax.ShapeDtypeStruct((S, SB_W), F32), jax.ShapeDtypeStruct((nb, nb, nh * tb, tb), BF16)] + r_shape,
        scratch_shapes=r_scr,
        compiler_params=_cp(("arbitrary",)),
    )(q, k, v, *r_args)


def sb_bwd(q, k, v, do, wts, ride=None, tb=QBLK):
    S = q.shape[0]
    nh = SB_W // HEAD
    nb = S // tb
    scale = HEAD ** -0.5
    r_in, r_out, r_shape, r_scr, r_args = _ride_specs(ride)

    def body(*refs):
        qb = pl.program_id(0)
        (q_ref, k_ref, v_ref, do_ref, w_ref, dq_ref, dk_ref, dv_ref, g_scr), finish = _riding(
            ride, refs, 5, 3, qb == 0, qb == nb - 1)

        @pl.when(qb == 0)
        def _():
            dk_ref[...] = jnp.zeros_like(dk_ref)
            dv_ref[...] = jnp.zeros_like(dv_ref)

        diag = _sb_diag(tb, nh)
        before_mat = _tri(tb, False)
        qs = _head_stack(q_ref[...])
        dos = _head_stack(do_ref[...])

        def weights_pass(kb, carry):
            rows = pl.ds(pl.multiple_of(kb * tb, tb), tb)
            w = w_ref[0, kb]
            g_scr[kb] = _dg(dos, v_ref[rows, :].astype(BF16), ((1,), (1,))) * w.astype(F32)
            return carry, _dg(w, dos, ((0,), (0,)))

        def add_rows(ref):
            def commit(kb, val):
                ref[pl.ds(pl.multiple_of(kb * tb, tb), tb), :] += val
            return commit

        zero_run = jnp.zeros((nh * tb, 1), F32)
        _sb_sweep(weights_pass, 0, qb + 1, 0, 1, add_rows(dv_ref))

        def left_to_right(kb, carry, strict):
            dq, run = carry
            rows = pl.ds(pl.multiple_of(kb * tb, tb), tb)
            kblk = k_ref[rows, :].astype(BF16)
            gw = g_scr[kb]
            sig = jax.nn.sigmoid(_dg(qs, kblk, ((1,), (1,))) * scale)
            dkeep = _masked(strict, _tri_sums(gw, before_mat) + run)
            dz = ((gw * (1.0 - sig) - dkeep * sig) * scale).astype(BF16)
            dq = dq + _dg(dz, kblk, ((1,), (0,)))
            return (dq, run + jnp.sum(gw, axis=1, keepdims=True)), _dg(dz, qs, ((0,), (0,)))

        carry = _sb_sweep(lambda kb, c: left_to_right(kb, c, None), 0, qb,
                          (jnp.zeros((nh * tb, SB_W), F32), zero_run), 1, add_rows(dk_ref))
        (dq, _), dk_diag = left_to_right(qb, carry, diag)
        add_rows(dk_ref)(qb, dk_diag)
        dq_ref[...] = _head_pick(dq)
        finish()

    whole = pl.BlockSpec((S, SB_W), lambda i: (0, 0))
    blk = pl.BlockSpec((tb, SB_W), lambda i: (i, 0))
    return pl.pallas_call(
        body, name="sb_bwd", grid=(S // tb,),
        in_specs=[blk, whole, whole, blk, pl.BlockSpec((1, nb, nh * tb, tb), lambda i: (i, 0, 0, 0))] + r_in,
        out_specs=[blk, whole, whole] + r_out,
        out_shape=[jax.ShapeDtypeStruct((S, SB_W), F32)] * 3 + r_shape,
        scratch_shapes=[pltpu.VMEM((S // tb, nh * tb, tb), F32)] + r_scr,
        compiler_params=_cp(("arbitrary",)),
    )(q, k, v, do, wts, *r_args)


def reorder(name, x, groups, inverse):
    P, S, _ = x.shape

    def body(x_ref, o_ref):
        p = pl.program_id(0)
        for gi, r in enumerate(groups):
            @pl.when(p // 2 == gi)
            def _(r=r):
                L = S // r
                if r == 1:
                    o_ref[...] = x_ref[...]
                for c in range(r if r > 1 else 0):
                    if inverse:
                        o_ref[pl.ds(c, L, stride=r), :] = x_ref[c * L:(c + 1) * L, :]
                    else:
                        o_ref[c * L:(c + 1) * L, :] = x_ref[pl.ds(c, L, stride=r), :]

    slab = pl.BlockSpec((None, S, 128), lambda p: (p, 0, 0))
    return pl.pallas_call(
        body, name=name, grid=(P,), in_specs=[slab], out_specs=slab,
        out_shape=jax.ShapeDtypeStruct(x.shape, x.dtype), compiler_params=_cp(("parallel",)),
    )(x)


def _dil_blocks(S):
    return S // QBLK


def _dil_mask4(n_in_stream):
    qi = lax.rem(lax.broadcasted_iota(jnp.int32, (4 * QBLK, 2 * QBLK), 0), QBLK)
    kj = lax.broadcasted_iota(jnp.int32, (4 * QBLK, 2 * QBLK), 1) - QBLK
    dist = qi - kj
    return (dist >= 0) & (dist <= QBLK) & ((n_in_stream > 0) | (kj >= 0))


def _dil_lanes(ref):
    return jnp.concatenate([ref[0], ref[1]], axis=1)


def _dil_window(prev_ref, cur_ref):
    return jnp.concatenate([_dil_lanes(prev_ref), _dil_lanes(cur_ref)], axis=0).astype(BF16)


def _stream_pos(gi, i, S):
    nb = jnp.where(gi == 0, S // (QBLK * DIL[0]), jnp.where(gi == 1, S // (QBLK * DIL[1]), S // (QBLK * DIL[2])))
    return i % nb


def dil_fwd(q, k, v, bias, ride=None):
    S = q.shape[1]
    nblk = _dil_blocks(S)
    r_in, r_out, r_shape, r_scr, r_args = _ride_specs(ride)

    def body(*refs):
        gi, i = pl.program_id(0), pl.program_id(1)
        (q_ref, kc_ref, kp_ref, vc_ref, vp_ref, b_ref, o_ref, l_ref), finish = _riding(
            ride, refs, 6, 2, (gi == 0) & (i == 0), (gi == len(DIL) - 1) & (i == nblk - 1))
        mask = _dil_mask4(_stream_pos(gi, i, S))
        kw, vw = _dil_window(kp_ref, kc_ref), _dil_window(vp_ref, vc_ref)
        lg = _dg(_head_stack(_dil_lanes(q_ref)), kw, ((1,), (1,))) * (HEAD ** -0.5) + \
            b_ref[...].reshape(4 * QBLK, 2 * QBLK)
        lg = jnp.where(mask, lg, NEG_INF)
        m = jnp.max(lg, axis=-1, keepdims=True)
        p = jnp.exp(lg - m)
        den = jnp.sum(p, axis=-1, keepdims=True)
        o = _head_pick(_dg((p / den).astype(BF16), vw, ((1,), (0,))))
        lse = _head_pick(jnp.broadcast_to(m + jnp.log(den), (4 * QBLK, 4 * HEAD)))
        for j in range(2):
            o_ref[j] = o[:, 128 * j:128 * (j + 1)]
            l_ref[j] = lse[:, 128 * j:128 * (j + 1)]
        finish()

    cur = pl.BlockSpec((2, QBLK, 128), lambda g, i: (g, i, 0))
    prev = pl.BlockSpec((2, QBLK, 128), lambda g, i: (g, jnp.maximum(i - 1, 0), 0))
    return pl.pallas_call(
        body, name="dil_fwd", grid=(len(DIL), nblk),
        in_specs=[cur, cur, prev, cur, prev, pl.BlockSpec((4, QBLK, 2 * QBLK), lambda g, i: (g, 0, 0))] + r_in,
        out_specs=[cur, cur] + r_out,
        out_shape=[jax.ShapeDtypeStruct(q.shape, F32)] * 2 + r_shape,
        scratch_shapes=r_scr,
        compiler_params=_cp(("arbitrary", "arbitrary")),
    )(q, k, k, v, v, bias, *r_args)


def dil_bwd(q, k, v, bias, o, lse, do, dlse, ride=None):
    S = q.shape[1]
    nblk = _dil_blocks(S)
    r_in, r_out, r_shape, r_scr, r_args = _ride_specs(ride)

    def body(*refs):
        gi, i = pl.program_id(0), pl.program_id(1)
        (q_ref, kc_ref, kp_ref, vc_ref, vp_ref, b_ref, o_ref, l_ref, do_ref, dl_ref,
         dq_ref, dk_ref, dv_ref, ds_ref, dk_car, dv_car), finish = _riding(
            ride, refs, 10, 4, (gi == 0) & (i == 0), (gi == len(DIL) - 1) & (i == nblk))

        @pl.when(i == 0)
        def _():
            ds_ref[...] = jnp.zeros_like(ds_ref)
            dk_car[...] = jnp.zeros_like(dk_car)
            dv_car[...] = jnp.zeros_like(dv_car)

        @pl.when(i < nblk)
        def _():
            mask = _dil_mask4(_stream_pos(gi, i, S))
            kw, vw = _dil_window(kp_ref, kc_ref), _dil_window(vp_ref, vc_ref)
            qs = _head_stack(_dil_lanes(q_ref))
            do_nat = _dil_lanes(do_ref)
            dos = _head_stack(do_nat, F32)
            lse = jnp.sum(_head_stack(_dil_lanes(l_ref), F32), axis=-1, keepdims=True) * (1.0 / HEAD)
            lg = _dg(qs, kw, ((1,), (1,))) * (HEAD ** -0.5) + b_ref[...].reshape(4 * QBLK, 2 * QBLK)
            p = jnp.where(mask, jnp.exp(lg - lse), 0.0)
            dp = _dg(dos.astype(BF16), vw, ((1,), (1,)))
            four = lambda t: jnp.concatenate([t] * 4, axis=0)
            delta = jnp.sum(dos * four(_dil_lanes(o_ref)), axis=-1, keepdims=True)
            dl = jnp.sum(_head_stack(_dil_lanes(dl_ref), F32), axis=-1, keepdims=True)
            ds = p * (dp - delta + dl)
            ds_ref[...] += ds.reshape(4, QBLK, 2 * QBLK)
            dsq = (ds * (HEAD ** -0.5)).astype(BF16)
            dq = _head_pick(_dg(dsq, kw, ((1,), (0,))))
            dkw = _dg(dsq, qs, ((0,), (0,)))
            dvw = _dg(p.astype(BF16), dos.astype(BF16), ((0,), (0,)))
            for j in range(2):
                lanes = slice(128 * j, 128 * (j + 1))
                dq_ref[j] = dq[:, lanes]
                dk_ref[j] = dk_car[j] + dkw[:QBLK, lanes]
                dv_ref[j] = dv_car[j] + dvw[:QBLK, lanes]
                dk_car[j] = dkw[QBLK:, lanes]
                dv_car[j] = dvw[QBLK:, lanes]

        @pl.when(i == nblk)
        def _():
            dk_ref[...] = dk_car[...]
            dv_ref[...] = dv_car[...]

        finish()

    cur = pl.BlockSpec((2, QBLK, 128), lambda g, i: (g, jnp.minimum(i, nblk - 1), 0))
    prev = pl.BlockSpec((2, QBLK, 128), lambda g, i: (g, jnp.clip(i - 1, 0, nblk - 1), 0))
    bspec = pl.BlockSpec((4, QBLK, 2 * QBLK), lambda g, i: (g, 0, 0))
    return pl.pallas_call(
        body, name="dil_bwd", grid=(len(DIL), nblk + 1),
        in_specs=[cur, cur, prev, cur, prev, bspec, cur, cur, cur, cur] + r_in,
        out_specs=[cur, prev, prev, bspec] + r_out,
        out_shape=[jax.ShapeDtypeStruct(q.shape, F32)] * 3 + [jax.ShapeDtypeStruct(bias.shape, F32)] + r_shape,
        scratch_shapes=[pltpu.VMEM((2, QBLK, 128), F32), pltpu.VMEM((2, QBLK, 128), F32)] + r_scr,
        compiler_params=_cp(("arbitrary", "arbitrary")),
    )(q, k, k, v, v, bias, o, lse, do, dlse, *r_args)


def _t5_bucket(dist):
    max_exact = N_BUCKETS // 2
    d = jnp.maximum(dist, 1).astype(F32)
    large = max_exact + (jnp.log(d / max_exact) / math.log(MAX_DISTANCE / max_exact)
                         * (N_BUCKETS - max_exact)).astype(jnp.int32)
    large = jnp.minimum(large, N_BUCKETS - 1)
    return jnp.where(dist < max_exact, dist, large)


def _bucket_maps():
    qi = jnp.arange(QBLK)[:, None]
    kj = jnp.arange(2 * QBLK)[None, :] - QBLK
    dist = jnp.maximum(qi - kj, 0)
    return jnp.stack([_t5_bucket(dist * r) for r in DIL])


def bias_table(rel_bias, buckets):
    def body(tbl_ref, bk_ref, o_ref):
        for h in range(DL_HEADS):
            bk = bk_ref[h // 4]

            def step(b, acc):
                return jnp.where(bk == b, tbl_ref[b, h], acc)

            o_ref[h] = lax.fori_loop(0, N_BUCKETS, step, jnp.zeros(bk.shape, F32))

    return pl.pallas_call(
        body, name="bias_table", out_shape=jax.ShapeDtypeStruct((DL_HEADS,) + buckets.shape[1:], F32),
        in_specs=[pl.BlockSpec(memory_space=pltpu.SMEM), pl.BlockSpec(memory_space=pltpu.VMEM)],
        out_specs=pl.BlockSpec(memory_space=pltpu.VMEM),
    )(rel_bias, buckets)


def bias_grad(ds, buckets):
    def body(ds_ref, bk_ref, o_ref):
        lane = lax.broadcasted_iota(jnp.int32, (1, 128), 1)
        for h in range(DL_HEADS):
            dsv = ds_ref[h]
            bk = bk_ref[h // 4]

            def step(b, row):
                return jnp.where(lane == b, jnp.sum(jnp.where(bk == b, dsv, 0.0)), row)

            o_ref[h:h + 1, :] = lax.fori_loop(0, N_BUCKETS, step, jnp.zeros((1, 128), F32))

    return pl.pallas_call(
        body, name="bias_grad", out_shape=jax.ShapeDtypeStruct((DL_HEADS, 128), F32),
        in_specs=[pl.BlockSpec(memory_space=pltpu.VMEM)] * 2, out_specs=pl.BlockSpec(memory_space=pltpu.VMEM),
    )(ds, buckets)


def f_attn_out(x, oa, o, lse, w):
    og = [jnp.concatenate([o[2 * g], o[2 * g + 1]], axis=1) for g in range(3)]
    lg = [jnp.concatenate([lse[2 * g], lse[2 * g + 1]], axis=1) for g in range(3)]
    m = jnp.maximum(jnp.maximum(lg[0], lg[1]), lg[2])
    e = [jnp.exp(l - m) for l in lg]
    den = e[0] + e[1] + e[2]
    ob = (e[0] * og[0] + e[1] * og[1] + e[2] * og[2]) / den
    return x + mm(jnp.concatenate([oa, ob], axis=1), w)


def norm_shift_fwd(x, g, tm=256):
    S = x.shape[0]

    def body(x_ref, xp_ref, g_ref, h_ref, hs_ref):
        h = rms(x_ref[...], g_ref[...])
        hp = rms(xp_ref[7:8, :], g_ref[...])
        hp = jnp.where(pl.program_id(0) == 0, 0.0, hp)
        row = lax.broadcasted_iota(jnp.int32, (tm, D), 0)
        h_ref[...] = h
        hs_ref[...] = jnp.where(row == 0, hp, pltpu.roll(h, 1, 0))

    return pl.pallas_call(
        body, name="rw_norm_shift", grid=(S // tm,),
        in_specs=[pl.BlockSpec((tm, D), lambda t: (t, 0)),
                  pl.BlockSpec((8, D), lambda t: (jnp.maximum(t * (tm // 8) - 1, 0), 0)),
                  pl.BlockSpec((1, D), lambda t: (0, 0))],
        out_specs=[pl.BlockSpec((tm, D), lambda t: (t, 0))] * 2,
        out_shape=[jax.ShapeDtypeStruct((S, D), F32)] * 2,
        compiler_params=_cp(("parallel",)),
    )(x, x, g)


def norm_shift_bwd(x, g, dh, dhs, dres, tm=256):
    S = x.shape[0]
    nt = S // tm

    def body(x_ref, g_ref, dh_ref, dhs_ref, dhn_ref, dr_ref, dx_ref, dg_ref):
        t = pl.program_id(0)
        nxt = jnp.where(t == nt - 1, 0.0, dhn_ref[0:1, :])
        row = lax.broadcasted_iota(jnp.int32, (tm, D), 0)
        tot = dh_ref[...] + jnp.where(row == tm - 1, nxt, pltpu.roll(dhs_ref[...], tm - 1, 0))
        _, vjp = jax.vjp(rms, x_ref[...], g_ref[...])
        dx, dg = vjp(tot)
        dx_ref[...] = dr_ref[...] + dx

        @pl.when(t == 0)
        def _():
            dg_ref[...] = dg

        @pl.when(t != 0)
        def _():
            dg_ref[...] += dg

    tile = pl.BlockSpec((tm, D), lambda t: (t, 0))
    return pl.pallas_call(
        body, name="rw_norm_shift_bwd", grid=(nt,),
        in_specs=[tile, pl.BlockSpec((1, D), lambda t: (0, 0)), tile, tile,
                  pl.BlockSpec((8, D), lambda t: (jnp.minimum((t + 1) * (tm // 8), S // 8 - 1), 0)), tile],
        out_specs=[tile, pl.BlockSpec((1, D), lambda t: (0, 0))],
        out_shape=[jax.ShapeDtypeStruct((S, D), F32), jax.ShapeDtypeStruct((1, D), F32)],
        compiler_params=_cp(("arbitrary",)),
    )(x, g, dh, dhs, dhs, dres)


def f_rw_proj(h, hs, mix, w):
    return mm(h + (hs - h) * mix, w)


def f_rw_mid(h, hs, r, k, v, mix3, w0, a0, kkw, kaw, w1, w2, a1, a2, g1, g2):
    xx = hs - h
    xw, xa, xg = h + xx * mix3[0:1], h + xx * mix3[1:2], h + xx * mix3[2:3]
    w_log = -softplus(-(w0 + mm(jnp.tanh(mm(xw, w1)), w2))) - 0.5
    lw = -jnp.exp(w_log)
    ag = jax.nn.sigmoid(a0 + mm(mm(xa, a1), a2))
    gate = mm(jax.nn.sigmoid(mm(xg, g1)), g2)
    kk = k * kkw
    kk = kk / jnp.maximum(jnp.sqrt(group_sum(kk * kk, RW_H)), 1e-12)
    kmod = k * (1.0 + (ag - 1.0) * kaw)
    return (to_heads(r), to_heads(lw), to_heads(kmod), to_heads(v), to_heads(-kk), to_heads(kk * ag), gate)


def f_rw_post(yh, rh, kh, vh, gate, x, lng, lnb, rk, wo):
    mu = jnp.mean(yh, axis=-1, keepdims=True)
    var = jnp.mean(jnp.square(yh - mu), axis=-1, keepdims=True)
    yn = (yh - mu) * lax.rsqrt(var + GN_EPS)
    bonus = jnp.sum(rh * kh * rk, axis=-1, keepdims=True) * vh
    y = from_heads(yn) * lng + lnb + from_heads(bonus)
    return x + mm(y * gate, wo)


def _split2(x):
    hi = x.astype(BF16)
    return hi, (x - hi.astype(F32)).astype(BF16)


def _b3(x, y, cx, cy):
    xh, xl = _split2(x)
    yh, yl = _split2(y)
    x3 = jnp.concatenate([xh, xh, xl], axis=cx)
    y3 = jnp.concatenate([yh, yl, yh], axis=cy)
    return lax.dot_general(x3, y3, (((cx,), (cy,)), ((0,), (0,))), preferred_element_type=F32)


@jax.custom_vjp
def b_nt(x, y):
    return _b3(x, y, 2, 2)


@jax.custom_vjp
def b_nn(x, y):
    return _b3(x, y, 2, 1)


@jax.custom_vjp
def b_tn(x, y):
    return _b3(x, y, 1, 1)


def _b1(x, y, cx, cy):
    return lax.dot_general(x.astype(BF16), y.astype(BF16), (((cx,), (cy,)), ((0,), (0,))), preferred_element_type=F32)


b_nt.defvjp(lambda x, y: (b_nt(x, y), (x, y)), lambda r, g: (_b1(g, r[1], 2, 1), _b1(g, r[0], 1, 1)))
b_nn.defvjp(lambda x, y: (b_nn(x, y), (x, y)), lambda r, g: (_b1(g, r[1], 2, 2), _b1(r[0], g, 1, 1)))
b_tn.defvjp(lambda x, y: (b_tn(x, y), (x, y)), lambda r, g: (_b1(r[1], g, 2, 2), _b1(r[0], g, 2, 1)))


def _tri_apply(x, lower):
    H, C, _ = x.shape
    ii = lax.broadcasted_iota(jnp.int32, (C, C), 0)
    jj = lax.broadcasted_iota(jnp.int32, (C, C), 1)
    m = jnp.broadcast_to(((jj <= ii) if lower else (jj >= ii)).astype(BF16), (H, C, C))
    x1 = x.astype(BF16)
    r1 = x - x1.astype(F32)
    x2 = r1.astype(BF16)
    x3 = (r1 - x2.astype(F32)).astype(BF16)
    return lax.dot_general(jnp.concatenate([m, m, m], axis=2), jnp.concatenate([x1, x2, x3], axis=1),
                           (((2,), (1,)), ((0,), (0,))), preferred_element_type=F32)


@jax.custom_vjp
def run_sum(x):
    return _tri_apply(x, True)


run_sum.defvjp(lambda x: (run_sum(x), None), lambda _, g: (_tri_apply(g, False),))


def rwkv_chunk(S0, r, lw, k, v, a, b):
    H, C, _ = r.shape
    V = S0.shape[1]
    ii = lax.broadcasted_iota(jnp.int32, (C, C), 0)
    jj = lax.broadcasted_iota(jnp.int32, (C, C), 1)
    strict = jj < ii
    i2 = lax.broadcasted_iota(jnp.int32, (C, 2 * C), 0)
    j2 = lax.broadcasted_iota(jnp.int32, (C, 2 * C), 1)
    incl2 = jnp.where(j2 >= C, j2 - C, j2) <= i2
    g = run_sum(lw)
    ig = jnp.exp(-g)
    ar = jnp.concatenate([a * jnp.exp(g - lw), r * jnp.exp(g)], axis=1)
    bk = jnp.concatenate([b * ig, k * ig], axis=1)
    m = b_nt(ar, bk)
    a_ab = jnp.where(strict, m[:, :C, :C], 0.0)
    a_ak = jnp.where(strict, m[:, :C, C:], 0.0)
    b_r = jnp.where(incl2, m[:, C:, :], 0.0)
    p = b_nt(ar, S0)
    u = p[:, :C] + b_nn(a_ak, v)
    nmat, n = a_ab, 1
    while n < C:
        n *= 2
        if n < C:
            z = b_nn(nmat, jnp.concatenate([u, nmat], axis=2))
            u, nmat = u + z[:, :, :V], z[:, :, V:]
        else:
            u = u + b_nn(nmat, u)
    uv = jnp.concatenate([u, v], axis=1)
    y = p[:, C:] + b_nn(b_r, uv)
    g_end = g[:, C - 1:C, :]
    dec = jnp.exp(g_end - g)
    s_new = S0 * jnp.exp(g_end) + b_tn(uv, jnp.concatenate([b * dec, k * dec], axis=1))
    return y, s_new


def rwkv_fwd(r, lw, k, v, a, b):
    H, S, _ = r.shape
    C = RW_CHUNK

    def body(r_ref, lw_ref, k_ref, v_ref, a_ref, b_ref, y_ref, s_ref, s_scr):
        @pl.when(pl.program_id(0) == 0)
        def _():
            s_scr[...] = jnp.zeros_like(s_scr)

        s0 = s_scr[...]
        s_ref[0] = s0
        y, s1 = rwkv_chunk(s0, r_ref[...], lw_ref[...], k_ref[...], v_ref[...], a_ref[...], b_ref[...])
        y_ref[...] = y
        s_scr[...] = s1

    bs = pl.BlockSpec((H, C, HEAD), lambda c: (0, c, 0))
    return pl.pallas_call(
        body, name="rwkv_fwd", grid=(S // C,), in_specs=[bs] * 6,
        out_specs=[bs, pl.BlockSpec((1, H, HEAD, HEAD), lambda c: (c, 0, 0, 0))],
        out_shape=[jax.ShapeDtypeStruct((H, S, HEAD), F32), jax.ShapeDtypeStruct((S // C, H, HEAD, HEAD), F32)],
        scratch_shapes=[pltpu.VMEM((H, HEAD, HEAD), F32)],
        compiler_params=_cp(("arbitrary",)),
    )(r, lw, k, v, a, b)


def rwkv_bwd(r, lw, k, v, a, b, states, dy):
    H, S, _ = r.shape
    C = RW_CHUNK
    nc = S // C

    def body(r_ref, lw_ref, k_ref, v_ref, a_ref, b_ref, s_ref, dy_ref, dr, dlw, dk, dv, da, db, ds_scr):
        @pl.when(pl.program_id(0) == 0)
        def _():
            ds_scr[...] = jnp.zeros_like(ds_scr)

        _, vjp = jax.vjp(rwkv_chunk, s_ref[0], r_ref[...], lw_ref[...], k_ref[...], v_ref[...], a_ref[...], b_ref[...])
        grads = vjp((dy_ref[...], ds_scr[...]))
        ds_scr[...] = grads[0]
        for o, gv in zip((dr, dlw, dk, dv, da, db), grads[1:]):
            o[...] = gv

    bs = pl.BlockSpec((H, C, HEAD), lambda c: (0, nc - 1 - c, 0))
    return pl.pallas_call(
        body, name="rwkv_bwd", grid=(nc,),
        in_specs=[bs] * 6 + [pl.BlockSpec((1, H, HEAD, HEAD), lambda c: (nc - 1 - c, 0, 0, 0)), bs],
        out_specs=[bs] * 6, out_shape=[jax.ShapeDtypeStruct((H, S, HEAD), F32)] * 6,
        scratch_shapes=[pltpu.VMEM((H, HEAD, HEAD), F32)],
        compiler_params=_cp(("arbitrary",)),
    )(r, lw, k, v, a, b, states, dy)


def loss_head(y, target, tm=512):
    S = y.shape[0]

    def body(y_ref, t_ref, dy_ref, l_ref):
        e = y_ref[...] - t_ref[...]
        dy_ref[...] = e * (1.0 / D)
        part = jnp.broadcast_to(0.5 * jnp.sum(jnp.mean(e * e, axis=-1, keepdims=True)), (1, 128))

        @pl.when(pl.program_id(0) == 0)
        def _():
            l_ref[...] = part

        @pl.when(pl.program_id(0) != 0)
        def _():
            l_ref[...] += part

    tile = pl.BlockSpec((tm, D), lambda t: (t, 0))
    return pl.pallas_call(
        body, name="loss_head", grid=(S // tm,), in_specs=[tile, tile],
        out_specs=[tile, pl.BlockSpec((1, 128), lambda t: (0, 0))],
        out_shape=[jax.ShapeDtypeStruct((S, D), F32), jax.ShapeDtypeStruct((1, 128), F32)],
        compiler_params=_cp(("arbitrary",)),
    )(y, target)


def _row_tile(rows, cols, budget=1 << 19):
    best = None
    for tr in range(8, rows + 1, 8):
        if rows % tr == 0 and tr * cols <= budget:
            best = tr
    return best or rows


def _adam(w, g, m, v):
    m = ADAM_B1 * m + (1.0 - ADAM_B1) * g
    v = ADAM_B2 * v + (1.0 - ADAM_B2) * jnp.square(g)
    m_hat = m / (1.0 - ADAM_B1 ** ADAM_STEP)
    v_hat = v / (1.0 - ADAM_B2 ** ADAM_STEP)
    return -ADAM_LR * (m_hat / (jnp.sqrt(v_hat) + ADAM_EPS) + ADAM_WD * w), m, v


def sum_slots(name, parts, dtype=F32, extras=()):
    n = 0 if parts is None else parts.shape[0]
    R, C = extras[0].shape if parts is None else parts.shape[1:]
    tr = _row_tile(R, C * (n + len(extras)))
    ins = ([] if parts is None else [parts]) + list(extras)

    def body(*refs):
        terms = [] if parts is None else [refs[0][i] for i in range(n)]
        terms += [r[...] for r in refs[len(ins) - len(extras):len(ins)]]
        s = terms[0].astype(F32)
        for t in terms[1:]:
            s = s + t.astype(F32)
        refs[len(ins)][...] = s.astype(dtype)

    tile = pl.BlockSpec((tr, C), lambda t: (t, 0))
    return pl.pallas_call(
        body, name=name, grid=(R // tr,),
        in_specs=([] if parts is None else [pl.BlockSpec((n, tr, C), lambda t: (0, t, 0))]) + [tile] * len(extras),
        out_specs=tile, out_shape=jax.ShapeDtypeStruct((R, C), dtype), compiler_params=_cp(("parallel",)),
    )(*ins)


def sum_own_half(name, split, theirs, c, dtype):
    nq, _, rh, cols = split.shape
    tr = _row_tile(rh, 2 * cols)

    def body(c_ref, a_ref, b_ref, o_ref):
        o_ref[...] = (a_ref[...] + b_ref[...]).astype(dtype)

    tile = pl.BlockSpec((None, tr, cols), lambda q, t, c_ref: (q, t, 0))
    return pl.pallas_call(
        body, name=name,
        grid_spec=pltpu.PrefetchScalarGridSpec(
            num_scalar_prefetch=1, grid=(nq, rh // tr),
            in_specs=[pl.BlockSpec((None, None, tr, cols), lambda q, t, c_ref: (q, c_ref[0], t, 0)), tile],
            out_specs=tile),
        out_shape=jax.ShapeDtypeStruct((nq, rh, cols), dtype), compiler_params=_cp(("parallel", "parallel")),
    )(jnp.reshape(c, (1,)).astype(jnp.int32), split, theirs)


def sum_landed(name, landed, chip_sum, p):
    n, rh, cols = landed.shape
    tr = _row_tile(rh, (n + 1) * cols)

    def body(p_ref, l_ref, own_ref, o_ref):
        s = l_ref[0].astype(F32)
        for i in range(1, n):
            s = s + l_ref[i].astype(F32)
        o_ref[...] = s + own_ref[...].astype(F32)

    return pl.pallas_call(
        body, name=name,
        grid_spec=pltpu.PrefetchScalarGridSpec(
            num_scalar_prefetch=1, grid=(rh // tr,),
            in_specs=[pl.BlockSpec((n, tr, cols), lambda t, p_ref: (0, t, 0)),
                      pl.BlockSpec((None, tr, cols), lambda t, p_ref: (p_ref[0], t, 0))],
            out_specs=pl.BlockSpec((tr, cols), lambda t, p_ref: (t, 0))),
        out_shape=jax.ShapeDtypeStruct((rh, cols), F32), compiler_params=_cp(("parallel",)),
    )(jnp.reshape(p, (1,)).astype(jnp.int32), landed, chip_sum)


def adam_step(name, ga, gb, w, m, v):
    R, C = w.shape
    tr = _row_tile(R, C, 1 << 17)
    ins = [ga] + ([gb] if gb is not None else []) + [w, m, v]

    def body(*refs):
        g = refs[0][...]
        if gb is not None:
            g = g + refs[1][...]
        w_ref, m_ref, v_ref, g_out, d_out, m_out, v_out = refs[len(ins) - 3:]
        d, m2, v2 = _adam(w_ref[...], g, m_ref[...], v_ref[...])
        g_out[...] = g
        d_out[...] = d
        m_out[...] = m2
        v_out[...] = v2

    tile = pl.BlockSpec((tr, C), lambda t: (t, 0))
    return pl.pallas_call(
        body, name=name, grid=(R // tr,), in_specs=[tile] * len(ins), out_specs=[tile] * 4,
        out_shape=[jax.ShapeDtypeStruct((R, C), F32)] * 4, compiler_params=_cp(("parallel",)),
    )(*ins)


def adam_ffn(name, g_pieces, w, m, v, transposed=False):
    if transposed:
        res = adam_ffn(name, g_pieces, *(jnp.swapaxes(a, 2, 3) for a in (w, m, v)))
        return [jnp.swapaxes(r, 2, 3) for r in res]
    _, _, R, C = w.shape
    tr = _row_tile(R, 4 * C, 1 << 17)

    def body(g00, g01, g10, g11, w_ref, m_ref, v_ref, g_out, d_out, m_out, v_out):
        for l, j, g_ref in ((0, 0, g00), (0, 1, g01), (1, 0, g10), (1, 1, g11)):
            g = g_ref[...]
            d, m2, v2 = _adam(w_ref[l, j], g, m_ref[l, j], v_ref[l, j])
            g_out[l, j] = g
            d_out[l, j] = d
            m_out[l, j] = m2
            v_out[l, j] = v2

    piece = pl.BlockSpec((tr, C), lambda t: (t, 0))
    full = pl.BlockSpec((2, 2, tr, C), lambda t: (0, 0, t, 0))
    return pl.pallas_call(
        body, name=name, grid=(R // tr,), in_specs=[piece] * 4 + [full] * 3, out_specs=[full] * 4,
        out_shape=[jax.ShapeDtypeStruct(w.shape, F32)] * 4, compiler_params=_cp(("parallel",)),
    )(*g_pieces, w, m, v)


def _place():
    return lax.axis_index("x"), lax.axis_index("y"), lax.axis_index("c")


def _flip(me, mask):
    return tuple(1 - v if mk else v for v, mk in zip(me, mask))


CHIP_MASKS = ((1, 0, 0), (0, 1, 0), (1, 1, 0))
ALL_MASKS = tuple((a, b, c) for a in (0, 1) for b in (0, 1) for c in (0, 1) if (a, b, c) != (0, 0, 0))


def _chip(dev):
    return 2 * dev[0] + dev[1]


def _devno(dev):
    return 4 * dev[0] + 2 * dev[1] + dev[2]


class Pushes:
    def __init__(self, arrays, out_shapes, masks, copies, src_of, dst_of, alias=False):
        self.arrays, self.out_shapes, self.masks, self.copies = list(arrays), list(out_shapes), masks, copies
        self.src_of, self.dst_of, self.alias = src_of, dst_of, alias
        self.n = len(self.arrays)

    def sem_shapes(self):
        k = self.n * len(self.masks) * self.copies
        return [pltpu.SemaphoreType.DMA((k,)), pltpu.SemaphoreType.DMA((k,))]

    def ops(self, ins, outs, send_sems, recv_sems):
        me = _place()
        sends, lands = [], []
        for i in range(self.n):
            for j, mk in enumerate(self.masks):
                peer = _flip(me, mk)
                srcs, dsts = self.src_of(ins[i], me, j), self.dst_of(outs[i], me, j)
                here = self.dst_of(outs[i], peer, j)
                for q in range(self.copies):
                    sem = (i * len(self.masks) + j) * self.copies + q
                    sends.append(pltpu.make_async_remote_copy(
                        src_ref=srcs[q], dst_ref=dsts[q], send_sem=send_sems.at[sem], recv_sem=recv_sems.at[sem],
                        device_id=peer, device_id_type=MESH))
                    lands.append(pltpu.make_async_remote_copy(
                        src_ref=here[q], dst_ref=here[q], send_sem=send_sems.at[sem], recv_sem=recv_sems.at[sem],
                        device_id=peer, device_id_type=MESH))

        def start():
            for cp in sends:
                cp.start()

        def wait():
            for cp in lands:
                cp.wait_recv()
            for cp in sends:
                cp.wait_send()

        return start, wait


_HBM = pl.BlockSpec(memory_space=pl.ANY)


def exchange(name, p, local_of=None):
    n = p.n

    def body(*refs):
        ins, outs = refs[:n], refs[n:2 * n]
        start, wait = p.ops(ins, outs, refs[2 * n], refs[2 * n + 1])
        locals_ = []
        if local_of is not None:
            for i in range(n):
                src, dst = local_of(ins[i], outs[i], _place())
                locals_.append(pltpu.make_async_copy(src, dst, refs[2 * n + 2].at[i]))
                locals_[-1].start()
        start()
        wait()
        for cp in locals_:
            cp.wait()

    return pl.pallas_call(
        body, name=name, in_specs=[_HBM] * n, out_specs=[_HBM] * n, out_shape=p.out_shapes,
        scratch_shapes=p.sem_shapes() + ([pltpu.SemaphoreType.DMA((n,))] if local_of is not None else []),
        input_output_aliases={i: i for i in range(n)} if p.alias else {},
    )(*p.arrays)


def _half(c, rows):
    return pl.ds(c * (rows // 2), rows // 2)


def gather_pushes(arrays):
    outs = [jax.ShapeDtypeStruct((N_CHIPS,) + a.shape, a.dtype) for a in arrays]
    sib = len(CHIP_MASKS)
    return Pushes(arrays, outs, CHIP_MASKS + ((0, 0, 1),), 1,
                  src_of=lambda r, me, j: [r] if j == sib else [r.at[_half(me[2], r.shape[0])]],
                  dst_of=lambda o, sender, j: [o.at[_chip(sender)]] if j == sib else
                  [o.at[_chip(sender), _half(sender[2], o.shape[1])]])


def gather_swap(name, got):
    outs = [jax.ShapeDtypeStruct(a.shape, a.dtype) for a in got]
    return exchange(name, Pushes(
        got, outs, ((0, 0, 1),), len(CHIP_MASKS),
        src_of=lambda r, me, j: [r.at[_chip(_flip(me, mk)), _half(me[2], r.shape[1])] for mk in CHIP_MASKS],
        dst_of=lambda o, sender, j: [o.at[_chip(_flip(sender, mk)), _half(sender[2], o.shape[1])] for mk in CHIP_MASKS],
        alias=True))


def reduce_swap(arrays):
    split = [a.reshape(N_CHIPS, 2, a.shape[1] // 2, a.shape[2]) for a in arrays]
    half_shapes = [jax.ShapeDtypeStruct((N_CHIPS,) + a.shape[2:], F32) for a in split]
    return split, Pushes(split, half_shapes, ((0, 0, 1),), 1,
                         src_of=lambda r, me, j: [r.at[:, 1 - me[2]]], dst_of=lambda o, sender, j: [o])


def reduce_begin(tag, names, arrays, wire):
    split, pushes = reduce_swap(arrays)
    return reduce_sum(names, split, exchange(f"grad_pre_swap_{tag}", pushes), wire)


def reduce_sum(names, split, theirs, wire):
    c = lax.axis_index("c")
    chip_sum = [sum_own_half(f"sum2_{nm}", a, t, c, dt) for nm, a, t, dt in zip(names, split, theirs, wire)]
    pushes = Pushes(chip_sum, [jax.ShapeDtypeStruct((len(CHIP_MASKS),) + a.shape[1:], a.dtype) for a in chip_sum],
                    CHIP_MASKS, 1,
                    src_of=lambda r, me, j: [r.at[_chip(_flip(me, CHIP_MASKS[j]))]],
                    dst_of=lambda o, sender, j: [o.at[j]])
    return chip_sum, pushes


def reduce_end(tag, names, chip_sum, landed):
    x, y, c = _place()
    halves = [sum_landed(f"sum4_{nm}", p, a, _chip((x, y, c))) for nm, p, a in zip(names, landed, chip_sum)]
    others = exchange(f"grad_final_swap_{tag}", Pushes(
        halves, [jax.ShapeDtypeStruct(a.shape, F32) for a in halves], ((0, 0, 1),), 1,
        src_of=lambda r, me, j: [r], dst_of=lambda o, sender, j: [o]))
    return [jnp.concatenate([jnp.where(c == 0, h, o), jnp.where(c == 0, o, h)], axis=0) for h, o in zip(halves, others)]


def gather_all(arrays):
    outs = [jax.ShapeDtypeStruct((8,) + a.shape, a.dtype) for a in arrays]
    return exchange("gather_replicated", Pushes(
        arrays, outs, ALL_MASKS, 1, src_of=lambda r, me, j: [r], dst_of=lambda o, sender, j: [o.at[_devno(sender)]]),
        local_of=lambda r, o, me: (r, o.at[_devno(me)]))


def _unshard_cols(g):
    return jnp.transpose(g, (1, 0, 2)).reshape(g.shape[1], -1)


def _shard_cols(a):
    return jnp.transpose(a.reshape(a.shape[0], N_CHIPS, -1), (1, 0, 2))


class Weights(dict):
    def ride(self, kernel_name):
        return None

    def arrived(self, kernel_name, outs):
        pass


def _forward_backward(x, tgt, W, grads_early=None):
    S = x.shape[0]
    G = {}
    sd = jax.ShapeDtypeStruct

    hidden = {}

    def ffn(xin, l, j):
        out, *hidden[l, j] = ffn_fwd(xin, W["ffn_norm"][l][j], W["ffn_w_gate", l, j], W["ffn_w_up", l, j],
                                     W["ffn_w_down", l, j], l, j)
        return out

    def ffn_back(xin, dout, l, j):
        gn = W["ffn_norm"][l][j]
        dh, G["ffn_w_gate", l, j], G["ffn_w_up", l, j], G["ffn_w_down", l, j] = ffn_bwd(
            xin, gn, W["ffn_w_gate", l, j], W["ffn_w_up", l, j], W["ffn_w_down", l, j], dout, *hidden[l, j], l, j)
        dx, G[("ffn_norm", l, j)] = norm_bwd(f"ffn_norm_bwd_{l}{j}", xin, gn, dh, dout)
        return dx

    x0 = x
    x1 = ffn(x0, 0, 0)
    g0 = W["mix_norm"][0]
    sbq, sbk, sbv = tile_fwd(f_attn_sb, "attn_in_sb", [x1], [g0, W["attn_w_in"][0]], [sd((S, SB_W), F32)] * 3, 256)
    dl_shape = sd((DL_PAIRS, S, 128), F32)
    qn, = tile_fwd(f_attn_qk, "attn_in_q", [x1], [g0, W["attn_w_in"][1], W["attn_q_norm"]], [dl_shape], 256)
    kn, = tile_fwd(f_attn_qk, "attn_in_k", [x1], [g0, W["attn_w_in"][2], W["attn_k_norm"]], [dl_shape], 256)
    vv, = tile_fwd(f_attn_v, "attn_in_v", [x1], [g0, W["attn_w_in"][3]], [dl_shape], 256)
    oa, sb_wts, *rode = sb_fwd(sbq, sbk, sbv, W.ride("sb_fwd"))
    W.arrived("sb_fwd", rode)
    qs, ks, vs = (reorder(nm, t, DIL, False) for nm, t in (("sub_q", qn), ("sub_k", kn), ("sub_v", vv)))
    o_s, lse_s, *rode = dil_fwd(qs, ks, vs, W["bias_mat"], W.ride("dil_fwd"))
    W.arrived("dil_fwd", rode)
    o_n, lse_n = reorder("nat_o", o_s, DIL, True), reorder("nat_lse", lse_s, DIL, True)
    x2, = tile_fwd(f_attn_out, "attn_out", [x1, oa, o_n, lse_n], [W["attn_w_out"]], [sd((S, D), F32)], 256)
    x3 = ffn(x2, 0, 1)
    x4 = ffn(x3, 1, 0)
    g1 = W["mix_norm"][1]
    h, hs = norm_shift_fwd(x4, g1)
    mix = W["rw_mix"]
    r, = tile_fwd(f_rw_proj, "rw_proj_r", [h, hs], [mix[0:1], W["rw_wr"]], [sd((S, D), F32)], 256)
    k, = tile_fwd(f_rw_proj, "rw_proj_k", [h, hs], [mix[2:3], W["rw_wk"]], [sd((S, D), F32)], 256)
    v, = tile_fwd(f_rw_proj, "rw_proj_v", [h, hs], [mix[3:4], W["rw_wv"]], [sd((S, D), F32)], 256)
    mix3 = jnp.concatenate([mix[1:2], mix[4:5], mix[5:6]], axis=0)
    mid_w = [mix3, W["rw_w0"], W["rw_a0"], W["rw_kk"], W["rw_ka"], W["rw_w1"], W["rw_w2"], W["rw_a1"], W["rw_a2"],
             W["rw_g1"], W["rw_g2"]]
    hshape = sd((RW_H, S, HEAD), F32)
    mid_tiles = [h, hs, r, k, v]
    rh, lwh, kh, vh, ah, bh, gate = tile_fwd(f_rw_mid, "rw_mid", mid_tiles, mid_w, [hshape] * 6 + [sd((S, D), F32)], 128)
    yh, states = rwkv_fwd(rh, lwh, kh, vh, ah, bh)
    post_w = [W["rw_lnx_g"], W["rw_lnx_b"], W["rw_rk"], W["rw_wo"]]
    post_tiles = [yh, rh, kh, vh, gate, x4]
    x5, = tile_fwd(f_rw_post, "rw_post", post_tiles, post_w, [sd((S, D), F32)], 128)
    x6 = ffn(x5, 1, 1)
    dx6, loss_part = loss_head(x6, tgt)

    dx5 = ffn_back(x5, dx6, 1, 1)
    (dyh, drh, dkh, dvh, dgate, dx4), (d_lng, d_lnb, d_rk, d_wo) = tile_bwd(
        f_rw_post, "rw_post_bwd", post_tiles, post_w, [dx5], 128, [True] * 6, [True] * 4)
    drh2, dlwh, dkh2, dvh2, dah, dbh = rwkv_bwd(rh, lwh, kh, vh, ah, bh, states, dyh)
    mid_cts = [(drh, drh2), dlwh, (dkh, dkh2), (dvh, dvh2), dah, dbh, dgate]
    (dh, dhs, dr, dk, dv), dmid_w = tile_bwd(f_rw_mid, "rw_mid_bwd", mid_tiles, mid_w, mid_cts, 128,
                                             [True] * 5, [True] * len(mid_w))
    dmix = {}
    for nm, ct, row, wname in (("r", dr, 0, "rw_wr"), ("k", dk, 2, "rw_wk"), ("v", dv, 3, "rw_wv")):
        (dh, dhs), (dmix[row], G[wname]) = tile_bwd(
            f_rw_proj, f"rw_proj_{nm}_bwd", [h, hs], [mix[row:row + 1], W[wname]], [ct], 256,
            [True, True], [True, True], acc={0: dh, 1: dhs})
    dx4, G[("mix_norm", 1)] = norm_shift_bwd(x4, g1, dh, dhs, dx4)
    dmix3 = dmid_w[0]
    G["rw_mix"] = jnp.concatenate([dmix[0], dmix3[0:1], dmix[2], dmix[3], dmix3[1:2], dmix3[2:3]], axis=0)
    for nm, gv in zip(("rw_w0", "rw_a0", "rw_kk", "rw_ka", "rw_w1", "rw_w2", "rw_a1", "rw_a2", "rw_g1", "rw_g2"), dmid_w[1:]):
        G[nm] = gv
    G["rw_lnx_g"], G["rw_lnx_b"], G["rw_rk"], G["rw_wo"] = d_lng, d_lnb, d_rk, d_wo
    dx3 = ffn_back(x3, dx4, 1, 0)
    dx2 = ffn_back(x2, dx3, 0, 1)
    (dx1, doa, do_n, dlse_n), (G["attn_w_out"],) = tile_bwd(
        f_attn_out, "attn_out_bwd", [x1, oa, o_n, lse_n], [W["attn_w_out"]], [dx2], 256, [True] * 4, [True])
    do_s, dlse_s = reorder("sub_do", do_n, DIL, False), reorder("sub_dlse", dlse_n, DIL, False)
    ride, swapped = grads_early(G) if grads_early is not None else (None, None)
    dqs, dks, dvs, dsum, *rode = dil_bwd(qs, ks, vs, W["bias_mat"], o_s, lse_s, do_s, dlse_s, ride)
    ride, landed = swapped(rode) if swapped is not None else (None, None)
    G["rel_bias"] = bias_grad(dsum, W["buckets"])
    dqn, dkn, dvv = (reorder(nm, t, DIL, True) for nm, t in (("nat_dq", dqs), ("nat_dk", dks), ("nat_dv", dvs)))
    dsbq, dsbk, dsbv, *rode = sb_bwd(sbq, sbk, sbv, doa, sb_wts, ride)
    if landed is not None:
        landed(rode)
    dg0 = []
    dwin = []
    (dx1,), (dg, dw) = tile_bwd(f_attn_sb, "attn_in_sb_bwd", [x1], [g0, W["attn_w_in"][0]], [dsbq, dsbk, dsbv], 256,
                                [True], [True, True], acc={0: dx1})
    dg0.append(dg), dwin.append(dw)
    (dx1,), (dg, dw, G["attn_q_norm"]) = tile_bwd(f_attn_qk, "attn_in_q_bwd", [x1], [g0, W["attn_w_in"][1], W["attn_q_norm"]],
                                                  [dqn], 256, [True], [True] * 3, acc={0: dx1})
    dg0.append(dg), dwin.append(dw)
    (dx1,), (dg, dw, G["attn_k_norm"]) = tile_bwd(f_attn_qk, "attn_in_k_bwd", [x1], [g0, W["attn_w_in"][2], W["attn_k_norm"]],
                                                  [dkn], 256, [True], [True] * 3, acc={0: dx1})
    dg0.append(dg), dwin.append(dw)
    (dx1,), (dg, dw) = tile_bwd(f_attn_v, "attn_in_v_bwd", [x1], [g0, W["attn_w_in"][3]], [dvv], 256,
                                [True], [True, True], acc={0: dx1})
    dg0.append(dg), dwin.append(dw)
    G[("mix_norm", 0)] = dg0
    G["attn_w_in"] = dwin
    dx0 = ffn_back(x0, dx1, 0, 0)
    return loss_part, dx0, G


VEC_ROWS = ("ffn_norm", "rw_mix", "rw_w0", "rw_a0", "rw_kk", "rw_ka", "rw_lnx_g", "rw_lnx_b")


def kernel(x, ffn_norm, ffn_w_gate, ffn_w_up, ffn_w_down, mix_norm, rel_bias, attn_w_in, attn_q_norm, attn_k_norm, attn_w_out, rw_mix, rw_w0, rw_w1, rw_w2, rw_a0, rw_a1, rw_a2, rw_g1, rw_g2, rw_kk, rw_ka, rw_rk, rw_wr, rw_wk, rw_wv, rw_wo, rw_lnx_g, rw_lnx_b, loss_target, m_ffn_norm, m_ffn_w_gate, m_ffn_w_up, m_ffn_w_down, m_mix_norm, m_rel_bias, m_attn_w_in, m_attn_q_norm, m_attn_k_norm, m_attn_w_out, m_rw_mix, m_rw_w0, m_rw_w1, m_rw_w2, m_rw_a0, m_rw_a1, m_rw_a2, m_rw_g1, m_rw_g2, m_rw_kk, m_rw_ka, m_rw_rk, m_rw_wr, m_rw_wk, m_rw_wv, m_rw_wo, m_rw_lnx_g, m_rw_lnx_b, v_ffn_norm, v_ffn_w_gate, v_ffn_w_up, v_ffn_w_down, v_mix_norm, v_rel_bias, v_attn_w_in, v_attn_q_norm, v_attn_k_norm, v_attn_w_out, v_rw_mix, v_rw_w0, v_rw_w1, v_rw_w2, v_rw_a0, v_rw_a1, v_rw_a2, v_rw_g1, v_rw_g2, v_rw_kk, v_rw_ka, v_rw_rk, v_rw_wr, v_rw_wk, v_rw_wv, v_rw_wo, v_rw_lnx_g, v_rw_lnx_b):
    names = ["ffn_norm", "ffn_w_gate", "ffn_w_up", "ffn_w_down", "mix_norm", "rel_bias", "attn_w_in", "attn_q_norm",
             "attn_k_norm", "attn_w_out", "rw_mix", "rw_w0", "rw_w1", "rw_w2", "rw_a0", "rw_a1", "rw_a2", "rw_g1", "rw_g2",
             "rw_kk", "rw_ka", "rw_rk", "rw_wr", "rw_wk", "rw_wv", "rw_wo", "rw_lnx_g", "rw_lnx_b"]
    loc = locals()
    w = {n: loc[n] for n in names}
    mom = {n: loc["m_" + n] for n in names}
    vel = {n: loc["v_" + n] for n in names}
    S = x.shape[1]

    ffn3 = ("ffn_w_gate", "ffn_w_up", "ffn_w_down")
    rw_mats = ("rw_w1", "rw_w2", "rw_a1", "rw_a2", "rw_g1", "rw_g2", "rw_wr", "rw_wk", "rw_wv", "rw_wo")
    cols_split = ("attn_w_out", "rw_w2", "rw_a2", "rw_g2")
    shard = {"vec": jnp.concatenate([w[n].reshape(-1, 256) for n in VEC_ROWS], axis=0)}
    for n in ffn3:
        for l in range(2):
            for j in range(2):
                shard[n, l, j] = w[n][l, j].astype(BF16)
    for n in ("attn_w_in", "attn_w_out") + rw_mats:
        shard[n] = w[n].reshape(-1, w[n].shape[-1]).astype(BF16)
    ffn_keys = lambda l, j: [(n, l, j) for n in ffn3]
    w_groups = {"first": ["vec"] + ffn_keys(0, 0) + ["attn_w_in", "attn_w_out"],
                "sb_fwd": ffn_keys(0, 1) + ffn_keys(1, 0) + list(rw_mats),
                "dil_fwd": ffn_keys(1, 1)}
    label = lambda key: key if isinstance(key, str) else f"{key[0]}_{key[1]}{key[2]}"

    class Streamed(Weights):
        def ride(self, kernel_name):
            return gather_pushes([shard[k] for k in w_groups[kernel_name]])

        def arrived(self, kernel_name, outs):
            for key, g in zip(w_groups[kernel_name], gather_swap(f"gather_swap_{kernel_name}", outs)):
                if key == "vec":
                    vec_full = _unshard_cols(g)
                    self["ffn_norm"] = [[vec_full[2 * l + j][None] for j in range(2)] for l in range(2)]
                    self["rw_mix"] = vec_full[4:10]
                    for i, n in enumerate(("rw_w0", "rw_a0", "rw_kk", "rw_ka", "rw_lnx_g", "rw_lnx_b")):
                        self[n] = vec_full[10 + i][None]
                elif key == "attn_w_in":
                    self[key] = [g[p] for p in range(N_CHIPS)]
                elif key in cols_split:
                    self[key] = _unshard_cols(g)
                elif isinstance(key, str):
                    self[key] = g.reshape(D, -1)
                else:
                    self[key] = g

    buckets = _bucket_maps()
    W = Streamed({"mix_norm": [mix_norm[0:1], mix_norm[1:2]], "attn_q_norm": attn_q_norm, "attn_k_norm": attn_k_norm,
                  "rw_rk": rw_rk[0][:, None, :], "buckets": buckets, "bias_mat": bias_table(rel_bias, buckets)})
    W.arrived("first", exchange("gather_weights", W.ride("first")))

    def slots(key, G):
        if key == "vec":
            rows = [G[("ffn_norm", l, j)] for l in range(2) for j in range(2)] + [G["rw_mix"]] + \
                   [G[n] for n in ("rw_w0", "rw_a0", "rw_kk", "rw_ka", "rw_lnx_g", "rw_lnx_b")]
            return _shard_cols(jnp.concatenate(rows, axis=0))
        if key == "attn_w_in":
            return jnp.stack(G[key])
        if key in cols_split:
            return _shard_cols(G[key])
        if isinstance(key, str):
            return G[key].reshape(N_CHIPS, D // N_CHIPS, -1)
        return G[key]

    g_groups = {"early": ffn_keys(1, 1) + ffn_keys(1, 0) + ffn_keys(0, 1) + list(rw_mats) + ["attn_w_out"],
                "late": ["vec", "attn_w_in"] + ffn_keys(0, 0)}
    wire = lambda keys: [F32 if k == "vec" else BF16 for k in keys]
    part = {}

    def grads_early(G):
        keys = g_groups["early"]
        names_ = [label(k) for k in keys]
        split, swap_pushes = reduce_swap([slots(k, G) for k in keys])

        def swapped(theirs):
            chip_sum, pushes = reduce_sum(names_, split, theirs, wire(keys))
            return pushes, lambda landed: part.update(zip(keys, reduce_end("early", names_, chip_sum, landed)))

        return swap_pushes, swapped

    loss_part, dx, G = _forward_backward(x[0], loss_target[0], W, grads_early)
    loss = lax.psum(loss_part[0, 0], ("x", "y", "c"))
    keys = g_groups["late"]
    chip_sum, pushes = reduce_begin("late", [label(k) for k in keys], [slots(k, G) for k in keys], wire(keys))
    part.update(zip(keys, reduce_end("late", [label(k) for k in keys], chip_sum, exchange("scatter_grads", pushes))))

    rep = jnp.concatenate([G[("mix_norm", 0)][0] + G[("mix_norm", 0)][1] + G[("mix_norm", 0)][2] + G[("mix_norm", 0)][3],
                           G[("mix_norm", 1)]], axis=0).reshape(16, 128)
    rep = jnp.concatenate([rep, G["rel_bias"], jnp.pad(G["attn_q_norm"], ((0, 0), (0, 64))),
                           jnp.pad(G["attn_k_norm"], ((0, 0), (0, 64))), G["rw_rk"].reshape(8, 128),
                           jnp.zeros((2, 128), F32)], axis=0)
    rep_sum = sum_slots("sum_replicated", gather_all([rep])[0])
    g_rep = {
        "mix_norm": rep_sum[0:16].reshape(2, D),
        "rel_bias": jnp.transpose(rep_sum[16:28, :N_BUCKETS]),
        "attn_q_norm": rep_sum[28:29, :HEAD], "attn_k_norm": rep_sum[29:30, :HEAD],
        "rw_rk": rep_sum[30:38].reshape(1, RW_H, HEAD),
    }

    out = {}

    def adam(n, ga, gb):
        shp = w[n].shape
        to2 = lambda a: a.reshape(-1, shp[-1])
        res = adam_step(f"adam_{n}", to2(ga), None if gb is None else to2(gb), to2(w[n]), to2(mom[n]), to2(vel[n]))
        out[n] = tuple(r.reshape(shp) for r in res)

    for n in ffn3:
        out[n] = tuple(adam_ffn(f"adam_{n}", [part[n, l, j] for l in range(2) for j in range(2)], w[n], mom[n], vel[n],
                                transposed=n != "ffn_w_down"))
    for n in ("attn_w_in", "attn_w_out") + rw_mats:
        adam(n, part[n], None)
    rows = {"ffn_norm": (0, 4), "rw_mix": (4, 10), "rw_w0": (10, 11), "rw_a0": (11, 12), "rw_kk": (12, 13),
            "rw_ka": (13, 14), "rw_lnx_g": (14, 15), "rw_lnx_b": (15, 16)}
    for n, (lo, hi) in rows.items():
        adam(n, part["vec"][lo:hi], None)
    for n, gv in g_rep.items():
        adam(n, gv, None)

    grads = [out[n][0] for n in names]
    deltas = [out[n][1] for n in names]
    new_m = [out[n][2] for n in names]
    new_v = [out[n][3] for n in names]
    return (loss, dx[None], *grads, *deltas, *new_m, *new_v)
```

```python
import functools
import math

import jax
import jax.numpy as jnp
from jax import lax
from jax.experimental import pallas as pl
from jax.experimental.pallas import tpu as pltpu

F32, BF16 = jnp.float32, jnp.bfloat16
HI = lax.Precision.HIGHEST
MESH = pl.DeviceIdType.MESH

D = 1024
HEAD = 64
N_CHIPS = 4
FF_SHARD = 704
SB_W = 256
DL_HEADS = 12
DL_PAIRS = 6
DIL = (1, 4, 16)
QBLK = 128
N_BUCKETS = 32
MAX_DISTANCE = 2048
RW_H = 16
RW_CHUNK = 64
NORM_EPS = 1e-6
GN_EPS = 64e-5
NEG_INF = -1e30
VMEM_LIMIT = 56 * 1024 * 1024

ADAM_LR, ADAM_B1, ADAM_B2, ADAM_EPS, ADAM_WD, ADAM_STEP = 0.001, 0.9, 0.999, 1e-08, 0.01, 10


def _cp(sem):
    return pltpu.CompilerParams(dimension_semantics=sem, vmem_limit_bytes=VMEM_LIMIT)


def _dg(a, b, dims, prec=None):
    return lax.dot_general(a, b, (dims, ((), ())), precision=prec, preferred_element_type=F32)


def _bdot(a, b, dims):
    return _dg(a.astype(BF16), b.astype(BF16), dims)


@jax.custom_vjp
def mm(a, b):
    return _bdot(a, b, ((1,), (0,)))


def _mm_fwd(a, b):
    return _bdot(a, b, ((1,), (0,))), (a, b)


def _mm_bwd(res, g):
    a, b = res
    return _bdot(g, b, ((1,), (1,))), _bdot(a, g, ((0,), (0,)))


mm.defvjp(_mm_fwd, _mm_bwd)


def rms(x, g):
    return x * lax.rsqrt(jnp.mean(x * x, axis=-1, keepdims=True) + NORM_EPS) * g


def _pieces(x):
    x1 = x.astype(BF16)
    r1 = x - x1.astype(F32)
    x2 = r1.astype(BF16)
    return jnp.concatenate([x1, x2, (r1 - x2.astype(F32)).astype(BF16)], axis=-1)


def _group_sum(x, nh):
    w = x.shape[-1]
    e = (lax.broadcasted_iota(jnp.int32, (w, nh), 0) // HEAD == lax.broadcasted_iota(jnp.int32, (w, nh), 1)).astype(BF16)
    s = _dg(_pieces(x), jnp.concatenate([e, e, e], axis=0), ((1,), (0,)))
    return _dg(_pieces(s), jnp.concatenate([e, e, e], axis=1), ((1,), (1,)))


@functools.partial(jax.custom_vjp, nondiff_argnums=(1,))
def group_sum(x, nh):
    return _group_sum(x, nh)


group_sum.defvjp(lambda x, nh: (_group_sum(x, nh), None), lambda nh, _, g: (_group_sum(g, nh),))


def softplus(u):
    return jnp.maximum(u, 0.0) + jnp.log1p(jnp.exp(-jnp.abs(u)))


def to_heads(t, nh=RW_H):
    return jnp.stack([t[:, HEAD * h:HEAD * (h + 1)] for h in range(nh)])


def from_heads(t):
    return jnp.concatenate([t[h] for h in range(t.shape[0])], axis=-1)


def _tile_spec(shape, tm):
    if len(shape) == 2:
        return pl.BlockSpec((tm, shape[1]), lambda t: (t, 0))
    return pl.BlockSpec((shape[0], tm, shape[2]), lambda t: (0, t, 0))


def _full_spec(shape):
    nd = len(shape)
    return pl.BlockSpec(tuple(shape), lambda t: (0,) * nd)


def _rows(a):
    return a.shape[0] if a.ndim == 2 else a.shape[1]


def tile_fwd(f, name, tiles, weights, outs, tm):
    nt, nw = len(tiles), len(weights)

    def body(*refs):
        tv = [r[...] for r in refs[:nt]]
        wv = [r[...].astype(F32) for r in refs[nt:nt + nw]]
        res = f(*tv, *wv)
        if not isinstance(res, (tuple, list)):
            res = (res,)
        for o, v in zip(refs[nt + nw:], res):
            o[...] = v.astype(o.dtype)

    return pl.pallas_call(
        body, name=name, grid=(_rows(tiles[0]) // tm,),
        in_specs=[_tile_spec(a.shape, tm) for a in tiles] + [_full_spec(w.shape) for w in weights],
        out_specs=[_tile_spec(o.shape, tm) for o in outs],
        out_shape=list(outs),
        compiler_params=_cp(("parallel",)),
    )(*tiles, *weights)


def tile_bwd(f, name, tiles, weights, cts, tm, dt, dw, acc=None):
    acc = acc or {}
    groups = [c if isinstance(c, tuple) else (c,) for c in cts]
    cts = [a for grp in groups for a in grp]
    nt, nw, nc = len(tiles), len(weights), len(cts)
    acc_idx = sorted(acc)
    na = len(acc_idx)
    dti = [i for i in range(nt) if dt[i]]
    dwi = [i for i in range(nw) if dw[i]]

    def body(*refs):
        tv = [r[...] for r in refs[:nt]]
        wv = [r[...].astype(F32) for r in refs[nt:nt + nw]]
        crefs = list(refs[nt + nw:nt + nw + nc])
        cv = []
        for grp in groups:
            terms = [crefs.pop(0)[...] for _ in grp]
            cv.append(functools.reduce(lambda a, b: a + b, terms))
        av = {i: r[...] for i, r in zip(acc_idx, refs[nt + nw + nc:nt + nw + nc + na])}
        orefs = refs[nt + nw + nc + na:]

        def g(*diff):
            t2, w2 = list(tv), list(wv)
            for i, v in zip(dti, diff[:len(dti)]):
                t2[i] = v
            for i, v in zip(dwi, diff[len(dti):]):
                w2[i] = v
            res = f(*t2, *w2)
            return tuple(res) if isinstance(res, (tuple, list)) else (res,)

        _, vjp = jax.vjp(g, *[tv[i] for i in dti], *[wv[i] for i in dwi])
        grads = vjp(tuple(cv))
        for k, i in enumerate(dti):
            gt = grads[k]
            if i in av:
                gt = gt + av[i]
            orefs[k][...] = gt
        first = pl.program_id(0) == 0
        for k, i in enumerate(dwi):
            o = orefs[len(dti) + k]
            gw = grads[len(dti) + k]

            @pl.when(first)
            def _(o=o, gw=gw):
                o[...] = gw

            @pl.when(jnp.logical_not(first))
            def _(o=o, gw=gw):
                o[...] += gw

    out_shape = [jax.ShapeDtypeStruct(tiles[i].shape, F32) for i in dti] + \
                [jax.ShapeDtypeStruct(weights[i].shape, F32) for i in dwi]
    res = pl.pallas_call(
        body, name=name, grid=(_rows(tiles[0]) // tm,),
        in_specs=[_tile_spec(a.shape, tm) for a in tiles] + [_full_spec(w.shape) for w in weights] +
                 [_tile_spec(c.shape, tm) for c in cts] + [_tile_spec(tiles[i].shape, tm) for i in acc_idx],
        out_specs=[_tile_spec(tiles[i].shape, tm) for i in dti] + [_full_spec(weights[i].shape) for i in dwi],
        out_shape=out_shape,
        compiler_params=_cp(("arbitrary",)),
    )(*tiles, *weights, *cts, *[acc[i] for i in acc_idx])
    return list(res[:len(dti)]), list(res[len(dti):])


def _ffn_wspec(rows, cols, cfirst):
    if cfirst:
        return pl.BlockSpec((1, rows, cols), lambda c, t: (c, 0, 0))
    return pl.BlockSpec((1, rows, cols), lambda t, c: (c, 0, 0))


def ffn_fwd(x, g, wg, wu, wd, l, j, ride=None, tm=1024):
    S = x.shape[0]
    r_in, r_out, r_shape, r_scr, r_args = _ride_specs(ride)

    def body(*refs):
        t, c = pl.program_id(0), pl.program_id(1)
        (x_ref, g_ref, wg_ref, wu_ref, wd_ref, o_ref, a_ref, b_ref, h_ref, acc_ref), finish = _riding(
            ride, refs, 5, 3, (t == 0) & (c == 0), (t == S // tm - 1) & (c == N_CHIPS - 1))

        @pl.when(c == 0)
        def _():
            h_ref[...] = rms(x_ref[...], g_ref[...]).astype(BF16)
            acc_ref[...] = jnp.zeros_like(acc_ref)

        h = h_ref[...]
        a = _bdot(h, wg_ref[0], ((1,), (0,)))
        b = _bdot(h, wu_ref[0], ((1,), (0,)))
        a_ref[0] = a.astype(BF16)
        b_ref[0] = b.astype(BF16)
        y = a * jax.nn.sigmoid(a) * b
        acc_ref[...] += _bdot(y, wd_ref[0], ((1,), (0,)))

        @pl.when(c == N_CHIPS - 1)
        def _():
            o_ref[...] = x_ref[...] + 0.5 * acc_ref[...]

        finish()

    hid = pl.BlockSpec((1, tm, FF_SHARD), lambda t, c: (c, t, 0))
    return pl.pallas_call(
        body, name=f"ffn_fwd_{l}{j}", grid=(S // tm, N_CHIPS),
        in_specs=[pl.BlockSpec((tm, D), lambda t, c: (t, 0)), pl.BlockSpec((1, D), lambda t, c: (0, 0)),
                  _ffn_wspec(D, FF_SHARD, False), _ffn_wspec(D, FF_SHARD, False), _ffn_wspec(FF_SHARD, D, False)] + r_in,
        out_specs=[pl.BlockSpec((tm, D), lambda t, c: (t, 0)), hid, hid] + r_out,
        out_shape=[jax.ShapeDtypeStruct((S, D), F32)] + [jax.ShapeDtypeStruct((N_CHIPS, S, FF_SHARD), BF16)] * 2 + r_shape,
        scratch_shapes=[pltpu.VMEM((tm, D), BF16), pltpu.VMEM((tm, D), F32)] + r_scr,
        compiler_params=_cp(("arbitrary", "arbitrary")),
    )(x, g, wg, wu, wd, *r_args)


def ffn_bwd(x, g, wg, wu, wd, dout, a_sav, b_sav, l, j, tm=512):
    S = x.shape[0]

    def body(x_ref, g_ref, wg_ref, wu_ref, wd_ref, do_ref, a_ref, b_ref, dh_ref, dwg_ref, dwu_ref, dwd_ref):
        t = pl.program_id(1)
        h = rms(x_ref[...], g_ref[...]).astype(BF16)
        wgv, wuv, wdv = wg_ref[0], wu_ref[0], wd_ref[0]
        a = a_ref[0].astype(F32)
        b = b_ref[0].astype(F32)
        sig = jax.nn.sigmoid(a)
        s = a * sig
        dyd = 0.5 * do_ref[...]
        dy = _bdot(dyd, wdv, ((1,), (1,)))
        dwd = _bdot(s * b, dyd, ((0,), (0,)))
        db = dy * s
        da = dy * b * (sig * (1.0 + a * (1.0 - sig)))
        dwg = _bdot(da, h, ((0,), (0,)))
        dwu = _bdot(db, h, ((0,), (0,)))
        dh_ref[0] = (_bdot(da, wgv, ((1,), (1,))) + _bdot(db, wuv, ((1,), (1,)))).astype(dh_ref.dtype)

        @pl.when(t == 0)
        def _():
            dwg_ref[0] = dwg
            dwu_ref[0] = dwu
            dwd_ref[0] = dwd

        @pl.when(t != 0)
        def _():
            dwg_ref[0] += dwg
            dwu_ref[0] += dwu
            dwd_ref[0] += dwd

    return pl.pallas_call(
        body, name=f"ffn_bwd_{l}{j}", grid=(N_CHIPS, S // tm),
        in_specs=[pl.BlockSpec((tm, D), lambda c, t: (t, 0)), pl.BlockSpec((1, D), lambda c, t: (0, 0)),
                  _ffn_wspec(D, FF_SHARD, True), _ffn_wspec(D, FF_SHARD, True), _ffn_wspec(FF_SHARD, D, True),
                  pl.BlockSpec((tm, D), lambda c, t: (t, 0)),
                  pl.BlockSpec((1, tm, FF_SHARD), lambda c, t: (c, t, 0)), pl.BlockSpec((1, tm, FF_SHARD), lambda c, t: (c, t, 0))],
        out_specs=[pl.BlockSpec((1, tm, D), lambda c, t: (c, t, 0))] + [_ffn_wspec(FF_SHARD, D, True)] * 3,
        out_shape=[jax.ShapeDtypeStruct((N_CHIPS, S, D), BF16)] + [jax.ShapeDtypeStruct(wd.shape, F32)] * 3,
        compiler_params=_cp(("parallel", "arbitrary")),
    )(x, g, wg, wu, wd, dout, a_sav, b_sav)


def norm_bwd(name, x, g, dh_parts, dres, tm=256):
    S = x.shape[0]
    P = dh_parts.shape[0]

    def body(x_ref, g_ref, dh_ref, dr_ref, dx_ref, dg_ref):
        dh = dh_ref[0].astype(F32)
        for p in range(1, P):
            dh = dh + dh_ref[p].astype(F32)
        _, vjp = jax.vjp(rms, x_ref[...], g_ref[...])
        dx, dg = vjp(dh)
        dx_ref[...] = dr_ref[...] + dx

        @pl.when(pl.program_id(0) == 0)
        def _():
            dg_ref[...] = dg

        @pl.when(pl.program_id(0) != 0)
        def _():
            dg_ref[...] += dg

    return pl.pallas_call(
        body, name=name, grid=(S // tm,),
        in_specs=[pl.BlockSpec((tm, D), lambda t: (t, 0)), pl.BlockSpec((1, D), lambda t: (0, 0)),
                  pl.BlockSpec((P, tm, D), lambda t: (0, t, 0)), pl.BlockSpec((tm, D), lambda t: (t, 0))],
        out_specs=[pl.BlockSpec((tm, D), lambda t: (t, 0)), pl.BlockSpec((1, D), lambda t: (0, 0))],
        out_shape=[jax.ShapeDtypeStruct((S, D), F32), jax.ShapeDtypeStruct((1, D), F32)],
        compiler_params=_cp(("arbitrary",)),
    )(x, g, dh_parts, dres)


def f_attn_sb(x, g, w):
    pr = mm(rms(x, g), w)
    return pr[:, :SB_W], pr[:, SB_W:2 * SB_W], pr[:, 2 * SB_W:]


def _pairs(y):
    return jnp.stack([y[:, 128 * j:128 * (j + 1)] for j in range(DL_PAIRS)])


def f_attn_qk(x, g, w, nrm):
    pr = mm(rms(x, g), w)
    ms = group_sum(pr * pr, DL_HEADS) * (1.0 / HEAD)
    return _pairs(pr * lax.rsqrt(ms + NORM_EPS) * jnp.concatenate([nrm] * DL_HEADS, axis=1))


def f_attn_v(x, g, w):
    return _pairs(mm(rms(x, g), w))


def _masked(strict, x):
    return x if strict is None else jnp.where(strict, x, 0.0)


def _head_stack(x, dtype=BF16):
    nh = x.shape[1] // HEAD
    lane_head = lax.broadcasted_iota(jnp.int32, (1, x.shape[1]), 1) // HEAD
    return jnp.concatenate([jnp.where(lane_head == h, x, 0.0) for h in range(nh)], axis=0).astype(dtype)


def _head_pick(xs):
    nh = xs.shape[1] // HEAD
    rows = xs.shape[0] // nh
    lane_head = lax.broadcasted_iota(jnp.int32, (1, xs.shape[1]), 1) // HEAD
    out = xs[:rows]
    for h in range(1, nh):
        out = jnp.where(lane_head == h, xs[rows * h:rows * (h + 1)], out)
    return out


def _sb_tiles(qs, kblk, strict):
    z = _dg(qs, kblk, ((1,), (1,))) * (HEAD ** -0.5)
    keep = -(jnp.maximum(z, 0.0) + jnp.log(1.0 + jnp.exp(-jnp.abs(z))))
    return z, _masked(strict, keep)


def _tri(n, upper):
    r = lax.broadcasted_iota(jnp.int32, (n, n), 0)
    c = lax.broadcasted_iota(jnp.int32, (n, n), 1)
    return ((r > c) if upper else (r < c)).astype(BF16)


def _tri_sums(x, tri):
    hi, lo = _split2(x)
    return _dg(jnp.concatenate([hi, lo], axis=1), jnp.concatenate([tri, tri], axis=0), ((1,), (0,)))


SB_UNROLL = 8


def _sb_diag(tb, nh):
    r = lax.broadcasted_iota(jnp.int32, (nh * tb, tb), 0)
    return lax.broadcasted_iota(jnp.int32, (nh * tb, tb), 1) < lax.rem(r, tb)


def _sb_sweep(step, first, count, carry, direction, commit=None):
    def run(kbs, c):
        outs = []
        for kb in kbs:
            c, out = step(kb, c)
            outs.append(out)
        if commit is not None:
            for kb, out in zip(kbs, outs):
                commit(kb, out)
        return c

    pos, size = first, 1
    while size < SB_UNROLL:
        n = (count // size) % 2
        carry = lax.fori_loop(
            0, n, lambda i, c, pos=pos, size=size: run([pos + direction * u for u in range(size)], c), carry)
        pos, size = pos + direction * size * n, 2 * size
    return lax.fori_loop(
        0, count // SB_UNROLL,
        lambda g, c: run([pos + direction * (SB_UNROLL * g + u) for u in range(SB_UNROLL)], c), carry)


def _riding(ride, refs, n_in, n_out, first, last):
    if ride is None:
        return refs, lambda: None
    n = ride.n
    own = refs[:n_in] + refs[n_in + n:n_in + n + n_out] + refs[n_in + 2 * n + n_out:len(refs) - 2]
    start, wait = ride.ops(refs[n_in:n_in + n], refs[n_in + n + n_out:n_in + 2 * n + n_out], refs[-2], refs[-1])
    pl.when(first)(start)
    return own, lambda: pl.when(last)(wait)


def _ride_specs(ride):
    if ride is None:
        return [], [], [], [], []
    return [_HBM] * ride.n, [_HBM] * ride.n, ride.out_shapes, ride.sem_shapes(), ride.arrays


def sb_fwd(q, k, v, ride=None, tb=QBLK):
    S = q.shape[0]
    nh = SB_W // HEAD
    nb = S // tb
    r_in, r_out, r_shape, r_scr, r_args = _ride_specs(ride)

    def body(*refs):
        qb = pl.program_id(0)
        (q_ref, k_ref, v_ref, o_ref, w_ref), finish = _riding(ride, refs, 3, 2, qb == 0, qb == nb - 1)
        diag = _sb_diag(tb, nh)
        after_mat = _tri(tb, True)
        qs = _head_stack(q_ref[...])

        def step(kb, carry, strict):
            acc, run = carry
            rows = pl.ds(pl.multiple_of(kb * tb, tb), tb)
            z, keep = _sb_tiles(qs, k_ref[rows, :].astype(BF16), strict)
            w = _masked(strict, jnp.exp(z + keep + _tri_sums(keep, after_mat) + run)).astype(BF16)
            w_ref[0, kb] = w
            acc = acc + _dg(w, v_ref[rows, :].astype(BF16), ((1,), (0,)))
            return acc, run + jnp.sum(keep, axis=1, keepdims=True)

        init = (jnp.zeros((nh * tb, SB_W), F32), jnp.zeros((nh * tb, 1), F32))
        carry = step(qb, init, diag)
        acc, _ = _sb_sweep(lambda kb, c: (step(kb, c, None), None), qb - 1, qb, carry, -1)
        o_ref[...] = _head_pick(acc)
        finish()

    return pl.pallas_call(
        body, name="sb_fwd", grid=(S // tb,),
        in_specs=[pl.BlockSpec((tb, SB_W), lambda i: (i, 0)), pl.BlockSpec((S, SB_W), lambda i: (0, 0)),
                  pl.BlockSpec((S, SB_W), lambda i: (0, 0))] + r_in,
        out_specs=[pl.BlockSpec((tb, SB_W), lambda i: (i, 0)),
                   pl.BlockSpec((1, nb, nh * tb, tb), lambda i: (i, 0, 0, 0))] + r_out,
        out_shape=[jax.ShapeDtypeStruct((S, SB_W), F32), jax.ShapeDtypeStruct((nb, nb, nh * tb, tb), BF16)] + r_shape,
        scratch_shapes=r_scr,
        compiler_params=_cp(("arbitrary",)),
    )(q, k, v, *r_args)


def sb_bwd(q, k, v, do, wts, ride=None, tb=QBLK):
    S = q.shape[0]
    nh = SB_W // HEAD
    nb = S // tb
    scale = HEAD ** -0.5
    r_in, r_out, r_shape, r_scr, r_args = _ride_specs(ride)

    def body(*refs):
        qb = pl.program_id(0)
        (q_ref, k_ref, v_ref, do_ref, w_ref, dq_ref, dk_ref, dv_ref, g_scr), finish = _riding(
            ride, refs, 5, 3, qb == 0, qb == nb - 1)

        @pl.when(qb == 0)
        def _():
            dk_ref[...] = jnp.zeros_like(dk_ref)
            dv_ref[...] = jnp.zeros_like(dv_ref)

        diag = _sb_diag(tb, nh)
        before_mat = _tri(tb, False)
        qs = _head_stack(q_ref[...])
        dos = _head_stack(do_ref[...])

        def weights_pass(kb, carry):
            rows = pl.ds(pl.multiple_of(kb * tb, tb), tb)
            w = w_ref[0, kb]
            g_scr[kb] = _dg(dos, v_ref[rows, :].astype(BF16), ((1,), (1,))) * w.astype(F32)
            return carry, _dg(w, dos, ((0,), (0,)))

        def add_rows(ref):
            def commit(kb, val):
                ref[pl.ds(pl.multiple_of(kb * tb, tb), tb), :] += val
            return commit

        zero_run = jnp.zeros((nh * tb, 1), F32)
        _sb_sweep(weights_pass, 0, qb + 1, 0, 1, add_rows(dv_ref))

        def left_to_right(kb, carry, strict):
            dq, run = carry
            rows = pl.ds(pl.multiple_of(kb * tb, tb), tb)
            kblk = k_ref[rows, :].astype(BF16)
            gw = g_scr[kb]
            sig = jax.nn.sigmoid(_dg(qs, kblk, ((1,), (1,))) * scale)
            dkeep = _masked(strict, _tri_sums(gw, before_mat) + run)
            dz = ((gw * (1.0 - sig) - dkeep * sig) * scale).astype(BF16)
            dq = dq + _dg(dz, kblk, ((1,), (0,)))
            return (dq, run + jnp.sum(gw, axis=1, keepdims=True)), _dg(dz, qs, ((0,), (0,)))

        carry = _sb_sweep(lambda kb, c: left_to_right(kb, c, None), 0, qb,
                          (jnp.zeros((nh * tb, SB_W), F32), zero_run), 1, add_rows(dk_ref))
        (dq, _), dk_diag = left_to_right(qb, carry, diag)
        add_rows(dk_ref)(qb, dk_diag)
        dq_ref[...] = _head_pick(dq)
        finish()

    whole = pl.BlockSpec((S, SB_W), lambda i: (0, 0))
    blk = pl.BlockSpec((tb, SB_W), lambda i: (i, 0))
    return pl.pallas_call(
        body, name="sb_bwd", grid=(S // tb,),
        in_specs=[blk, whole, whole, blk, pl.BlockSpec((1, nb, nh * tb, tb), lambda i: (i, 0, 0, 0))] + r_in,
        out_specs=[blk, whole, whole] + r_out,
        out_shape=[jax.ShapeDtypeStruct((S, SB_W), F32)] * 3 + r_shape,
        scratch_shapes=[pltpu.VMEM((S // tb, nh * tb, tb), F32)] + r_scr,
        compiler_params=_cp(("arbitrary",)),
    )(q, k, v, do, wts, *r_args)


def reorder(name, x, groups, inverse):
    P, S, _ = x.shape

    def body(x_ref, o_ref):
        p = pl.program_id(0)
        for gi, r in enumerate(groups):
            @pl.when(p // 2 == gi)
            def _(r=r):
                L = S // r
                if r == 1:
                    o_ref[...] = x_ref[...]
                for c in range(r if r > 1 else 0):
                    if inverse:
                        o_ref[pl.ds(c, L, stride=r), :] = x_ref[c * L:(c + 1) * L, :]
                    else:
                        o_ref[c * L:(c + 1) * L, :] = x_ref[pl.ds(c, L, stride=r), :]

    slab = pl.BlockSpec((None, S, 128), lambda p: (p, 0, 0))
    return pl.pallas_call(
        body, name=name, grid=(P,), in_specs=[slab], out_specs=slab,
        out_shape=jax.ShapeDtypeStruct(x.shape, x.dtype), compiler_params=_cp(("parallel",)),
    )(x)


def _dil_blocks(S):
    return S // QBLK


def _dil_mask4(n_in_stream):
    qi = lax.rem(lax.broadcasted_iota(jnp.int32, (4 * QBLK, 2 * QBLK), 0), QBLK)
    kj = lax.broadcasted_iota(jnp.int32, (4 * QBLK, 2 * QBLK), 1) - QBLK
    dist = qi - kj
    return (dist >= 0) & (dist <= QBLK) & ((n_in_stream > 0) | (kj >= 0))


def _dil_lanes(ref):
    return jnp.concatenate([ref[0], ref[1]], axis=1)


def _dil_window(prev_ref, cur_ref):
    return jnp.concatenate([_dil_lanes(prev_ref), _dil_lanes(cur_ref)], axis=0).astype(BF16)


def _stream_pos(gi, i, S):
    nb = jnp.where(gi == 0, S // (QBLK * DIL[0]), jnp.where(gi == 1, S // (QBLK * DIL[1]), S // (QBLK * DIL[2])))
    return i % nb


def dil_fwd(q, k, v, bias, ride=None):
    S = q.shape[1]
    nblk = _dil_blocks(S)
    r_in, r_out, r_shape, r_scr, r_args = _ride_specs(ride)

    def body(*refs):
        gi, i = pl.program_id(0), pl.program_id(1)
        (q_ref, kc_ref, kp_ref, vc_ref, vp_ref, b_ref, o_ref, l_ref), finish = _riding(
            ride, refs, 6, 2, (gi == 0) & (i == 0), (gi == len(DIL) - 1) & (i == nblk - 1))
        mask = _dil_mask4(_stream_pos(gi, i, S))
        kw, vw = _dil_window(kp_ref, kc_ref), _dil_window(vp_ref, vc_ref)
        lg = _dg(_head_stack(_dil_lanes(q_ref)), kw, ((1,), (1,))) * (HEAD ** -0.5) + \
            b_ref[...].reshape(4 * QBLK, 2 * QBLK)
        lg = jnp.where(mask, lg, NEG_INF)
        m = jnp.max(lg, axis=-1, keepdims=True)
        p = jnp.exp(lg - m)
        den = jnp.sum(p, axis=-1, keepdims=True)
        o = _head_pick(_dg((p / den).astype(BF16), vw, ((1,), (0,))))
        lse = _head_pick(jnp.broadcast_to(m + jnp.log(den), (4 * QBLK, 4 * HEAD)))
        for j in range(2):
            o_ref[j] = o[:, 128 * j:128 * (j + 1)]
            l_ref[j] = lse[:, 128 * j:128 * (j + 1)]
        finish()

    cur = pl.BlockSpec((2, QBLK, 128), lambda g, i: (g, i, 0))
    prev = pl.BlockSpec((2, QBLK, 128), lambda g, i: (g, jnp.maximum(i - 1, 0), 0))
    return pl.pallas_call(
        body, name="dil_fwd", grid=(len(DIL), nblk),
        in_specs=[cur, cur, prev, cur, prev, pl.BlockSpec((4, QBLK, 2 * QBLK), lambda g, i: (g, 0, 0))] + r_in,
        out_specs=[cur, cur] + r_out,
        out_shape=[jax.ShapeDtypeStruct(q.shape, F32)] * 2 + r_shape,
        scratch_shapes=r_scr,
        compiler_params=_cp(("arbitrary", "arbitrary")),
    )(q, k, k, v, v, bias, *r_args)


def dil_bwd(q, k, v, bias, o, lse, do, dlse, ride=None):
    S = q.shape[1]
    nblk = _dil_blocks(S)
    r_in, r_out, r_shape, r_scr, r_args = _ride_specs(ride)

    def body(*refs):
        gi, i = pl.program_id(0), pl.program_id(1)
        (q_ref, kc_ref, kp_ref, vc_ref, vp_ref, b_ref, o_ref, l_ref, do_ref, dl_ref,
         dq_ref, dk_ref, dv_ref, ds_ref, dk_car, dv_car), finish = _riding(
            ride, refs, 10, 4, (gi == 0) & (i == 0), (gi == len(DIL) - 1) & (i == nblk))

        @pl.when(i == 0)
        def _():
            ds_ref[...] = jnp.zeros_like(ds_ref)
            dk_car[...] = jnp.zeros_like(dk_car)
            dv_car[...] = jnp.zeros_like(dv_car)

        @pl.when(i < nblk)
        def _():
            mask = _dil_mask4(_stream_pos(gi, i, S))
            kw, vw = _dil_window(kp_ref, kc_ref), _dil_window(vp_ref, vc_ref)
            qs = _head_stack(_dil_lanes(q_ref))
            do_nat = _dil_lanes(do_ref)
            dos = _head_stack(do_nat, F32)
            lse = jnp.sum(_head_stack(_dil_lanes(l_ref), F32), axis=-1, keepdims=True) * (1.0 / HEAD)
            lg = _dg(qs, kw, ((1,), (1,))) * (HEAD ** -0.5) + b_ref[...].reshape(4 * QBLK, 2 * QBLK)
            p = jnp.where(mask, jnp.exp(lg - lse), 0.0)
            dp = _dg(dos.astype(BF16), vw, ((1,), (1,)))
            four = lambda t: jnp.concatenate([t] * 4, axis=0)
            delta = jnp.sum(dos * four(_dil_lanes(o_ref)), axis=-1, keepdims=True)
            dl = jnp.sum(_head_stack(_dil_lanes(dl_ref), F32), axis=-1, keepdims=True)
            ds = p * (dp - delta + dl)
            ds_ref[...] += ds.reshape(4, QBLK, 2 * QBLK)
            dsq = (ds * (HEAD ** -0.5)).astype(BF16)
            dq = _head_pick(_dg(dsq, kw, ((1,), (0,))))
            dkw = _dg(dsq, qs, ((0,), (0,)))
            dvw = _dg(p.astype(BF16), dos.astype(BF16), ((0,), (0,)))
            for j in range(2):
                lanes = slice(128 * j, 128 * (j + 1))
                dq_ref[j] = dq[:, lanes]
                dk_ref[j] = dk_car[j] + dkw[:QBLK, lanes]
                dv_ref[j] = dv_car[j] + dvw[:QBLK, lanes]
                dk_car[j] = dkw[QBLK:, lanes]
                dv_car[j] = dvw[QBLK:, lanes]

        @pl.when(i == nblk)
        def _():
            dk_ref[...] = dk_car[...]
            dv_ref[...] = dv_car[...]

        finish()

    cur = pl.BlockSpec((2, QBLK, 128), lambda g, i: (g, jnp.minimum(i, nblk - 1), 0))
    prev = pl.BlockSpec((2, QBLK, 128), lambda g, i: (g, jnp.clip(i - 1, 0, nblk - 1), 0))
    bspec = pl.BlockSpec((4, QBLK, 2 * QBLK), lambda g, i: (g, 0, 0))
    return pl.pallas_call(
        body, name="dil_bwd", grid=(len(DIL), nblk + 1),
        in_specs=[cur, cur, prev, cur, prev, bspec, cur, cur, cur, cur] + r_in,
        out_specs=[cur, prev, prev, bspec] + r_out,
        out_shape=[jax.ShapeDtypeStruct(q.shape, F32)] * 3 + [jax.ShapeDtypeStruct(bias.shape, F32)] + r_shape,
        scratch_shapes=[pltpu.VMEM((2, QBLK, 128), F32), pltpu.VMEM((2, QBLK, 128), F32)] + r_scr,
        compiler_params=_cp(("arbitrary", "arbitrary")),
    )(q, k, k, v, v, bias, o, lse, do, dlse, *r_args)


def _t5_bucket(dist):
    max_exact = N_BUCKETS // 2
    d = jnp.maximum(dist, 1).astype(F32)
    large = max_exact + (jnp.log(d / max_exact) / math.log(MAX_DISTANCE / max_exact)
                         * (N_BUCKETS - max_exact)).astype(jnp.int32)
    large = jnp.minimum(large, N_BUCKETS - 1)
    return jnp.where(dist < max_exact, dist, large)


def _bucket_maps():
    qi = jnp.arange(QBLK)[:, None]
    kj = jnp.arange(2 * QBLK)[None, :] - QBLK
    dist = jnp.maximum(qi - kj, 0)
    return jnp.stack([_t5_bucket(dist * r) for r in DIL])


def bias_table(rel_bias, buckets):
    def body(tbl_ref, bk_ref, o_ref):
        for h in range(DL_HEADS):
            bk = bk_ref[h // 4]

            def step(b, acc):
                return jnp.where(bk == b, tbl_ref[b, h], acc)

            o_ref[h] = lax.fori_loop(0, N_BUCKETS, step, jnp.zeros(bk.shape, F32))

    return pl.pallas_call(
        body, name="bias_table", out_shape=jax.ShapeDtypeStruct((DL_HEADS,) + buckets.shape[1:], F32),
        in_specs=[pl.BlockSpec(memory_space=pltpu.SMEM), pl.BlockSpec(memory_space=pltpu.VMEM)],
        out_specs=pl.BlockSpec(memory_space=pltpu.VMEM),
    )(rel_bias, buckets)


def bias_grad(ds, buckets):
    def body(ds_ref, bk_ref, o_ref):
        lane = lax.broadcasted_iota(jnp.int32, (1, 128), 1)
        for h in range(DL_HEADS):
            dsv = ds_ref[h]
            bk = bk_ref[h // 4]

            def step(b, row):
                return jnp.where(lane == b, jnp.sum(jnp.where(bk == b, dsv, 0.0)), row)

            o_ref[h:h + 1, :] = lax.fori_loop(0, N_BUCKETS, step, jnp.zeros((1, 128), F32))

    return pl.pallas_call(
        body, name="bias_grad", out_shape=jax.ShapeDtypeStruct((DL_HEADS, 128), F32),
        in_specs=[pl.BlockSpec(memory_space=pltpu.VMEM)] * 2, out_specs=pl.BlockSpec(memory_space=pltpu.VMEM),
    )(ds, buckets)


def f_attn_out(x, oa, o, lse, w):
    og = [jnp.concatenate([o[2 * g], o[2 * g + 1]], axis=1) for g in range(3)]
    lg = [jnp.concatenate([lse[2 * g], lse[2 * g + 1]], axis=1) for g in range(3)]
    m = jnp.maximum(jnp.maximum(lg[0], lg[1]), lg[2])
    e = [jnp.exp(l - m) for l in lg]
    den = e[0] + e[1] + e[2]
    ob = (e[0] * og[0] + e[1] * og[1] + e[2] * og[2]) / den
    return x + mm(jnp.concatenate([oa, ob], axis=1), w)


def norm_shift_fwd(x, g, tm=256):
    S = x.shape[0]

    def body(x_ref, xp_ref, g_ref, h_ref, hs_ref):
        h = rms(x_ref[...], g_ref[...])
        hp = rms(xp_ref[7:8, :], g_ref[...])
        hp = jnp.where(pl.program_id(0) == 0, 0.0, hp)
        row = lax.broadcasted_iota(jnp.int32, (tm, D), 0)
        h_ref[...] = h
        hs_ref[...] = jnp.where(row == 0, hp, pltpu.roll(h, 1, 0))

    return pl.pallas_call(
        body, name="rw_norm_shift", grid=(S // tm,),
        in_specs=[pl.BlockSpec((tm, D), lambda t: (t, 0)),
                  pl.BlockSpec((8, D), lambda t: (jnp.maximum(t * (tm // 8) - 1, 0), 0)),
                  pl.BlockSpec((1, D), lambda t: (0, 0))],
        out_specs=[pl.BlockSpec((tm, D), lambda t: (t, 0))] * 2,
        out_shape=[jax.ShapeDtypeStruct((S, D), F32)] * 2,
        compiler_params=_cp(("parallel",)),
    )(x, x, g)


def norm_shift_bwd(x, g, dh, dhs, dres, tm=256):
    S = x.shape[0]
    nt = S // tm

    def body(x_ref, g_ref, dh_ref, dhs_ref, dhn_ref, dr_ref, dx_ref, dg_ref):
        t = pl.program_id(0)
        nxt = jnp.where(t == nt - 1, 0.0, dhn_ref[0:1, :])
        row = lax.broadcasted_iota(jnp.int32, (tm, D), 0)
        tot = dh_ref[...] + jnp.where(row == tm - 1, nxt, pltpu.roll(dhs_ref[...], tm - 1, 0))
        _, vjp = jax.vjp(rms, x_ref[...], g_ref[...])
        dx, dg = vjp(tot)
        dx_ref[...] = dr_ref[...] + dx

        @pl.when(t == 0)
        def _():
            dg_ref[...] = dg

        @pl.when(t != 0)
        def _():
            dg_ref[...] += dg

    tile = pl.BlockSpec((tm, D), lambda t: (t, 0))
    return pl.pallas_call(
        body, name="rw_norm_shift_bwd", grid=(nt,),
        in_specs=[tile, pl.BlockSpec((1, D), lambda t: (0, 0)), tile, tile,
                  pl.BlockSpec((8, D), lambda t: (jnp.minimum((t + 1) * (tm // 8), S // 8 - 1), 0)), tile],
        out_specs=[tile, pl.BlockSpec((1, D), lambda t: (0, 0))],
        out_shape=[jax.ShapeDtypeStruct((S, D), F32), jax.ShapeDtypeStruct((1, D), F32)],
        compiler_params=_cp(("arbitrary",)),
    )(x, g, dh, dhs, dhs, dres)


def f_rw_proj(h, hs, mix, w):
    return mm(h + (hs - h) * mix, w)


def f_rw_mid(h, hs, r, k, v, mix3, w0, a0, kkw, kaw, w1, w2, a1, a2, g1, g2):
    xx = hs - h
    xw, xa, xg = h + xx * mix3[0:1], h + xx * mix3[1:2], h + xx * mix3[2:3]
    w_log = -softplus(-(w0 + mm(jnp.tanh(mm(xw, w1)), w2))) - 0.5
    lw = -jnp.exp(w_log)
    ag = jax.nn.sigmoid(a0 + mm(mm(xa, a1), a2))
    gate = mm(jax.nn.sigmoid(mm(xg, g1)), g2)
    kk = k * kkw
    kk = kk / jnp.maximum(jnp.sqrt(group_sum(kk * kk, RW_H)), 1e-12)
    kmod = k * (1.0 + (ag - 1.0) * kaw)
    return (to_heads(r), to_heads(lw), to_heads(kmod), to_heads(v), to_heads(-kk), to_heads(kk * ag), gate)


def f_rw_post(yh, rh, kh, vh, gate, x, lng, lnb, rk, wo):
    mu = jnp.mean(yh, axis=-1, keepdims=True)
    var = jnp.mean(jnp.square(yh - mu), axis=-1, keepdims=True)
    yn = (yh - mu) * lax.rsqrt(var + GN_EPS)
    bonus = jnp.sum(rh * kh * rk, axis=-1, keepdims=True) * vh
    y = from_heads(yn) * lng + lnb + from_heads(bonus)
    return x + mm(y * gate, wo)


def _split2(x):
    hi = x.astype(BF16)
    return hi, (x - hi.astype(F32)).astype(BF16)


def _b3(x, y, cx, cy):
    xh, xl = _split2(x)
    yh, yl = _split2(y)
    x3 = jnp.concatenate([xh, xh, xl], axis=cx)
    y3 = jnp.concatenate([yh, yl, yh], axis=cy)
    return lax.dot_general(x3, y3, (((cx,), (cy,)), ((0,), (0,))), preferred_element_type=F32)


@jax.custom_vjp
def b_nt(x, y):
    return _b3(x, y, 2, 2)


@jax.custom_vjp
def b_nn(x, y):
    return _b3(x, y, 2, 1)


@jax.custom_vjp
def b_tn(x, y):
    return _b3(x, y, 1, 1)


def _b1(x, y, cx, cy):
    return lax.dot_general(x.astype(BF16), y.astype(BF16), (((cx,), (cy,)), ((0,), (0,))), preferred_element_type=F32)


b_nt.defvjp(lambda x, y: (b_nt(x, y), (x, y)), lambda r, g: (_b1(g, r[1], 2, 1), _b1(g, r[0], 1, 1)))
b_nn.defvjp(lambda x, y: (b_nn(x, y), (x, y)), lambda r, g: (_b1(g, r[1], 2, 2), _b1(r[0], g, 1, 1)))
b_tn.defvjp(lambda x, y: (b_tn(x, y), (x, y)), lambda r, g: (_b1(r[1], g, 2, 2), _b1(r[0], g, 2, 1)))


def _tri_apply(x, lower):
    H, C, _ = x.shape
    ii = lax.broadcasted_iota(jnp.int32, (C, C), 0)
    jj = lax.broadcasted_iota(jnp.int32, (C, C), 1)
    m = jnp.broadcast_to(((jj <= ii) if lower else (jj >= ii)).astype(BF16), (H, C, C))
    x1 = x.astype(BF16)
    r1 = x - x1.astype(F32)
    x2 = r1.astype(BF16)
    x3 = (r1 - x2.astype(F32)).astype(BF16)
    return lax.dot_general(jnp.concatenate([m, m, m], axis=2), jnp.concatenate([x1, x2, x3], axis=1),
                           (((2,), (1,)), ((0,), (0,))), preferred_element_type=F32)


@jax.custom_vjp
def run_sum(x):
    return _tri_apply(x, True)


run_sum.defvjp(lambda x: (run_sum(x), None), lambda _, g: (_tri_apply(g, False),))


def rwkv_chunk(S0, r, lw, k, v, a, b):
    H, C, _ = r.shape
    V = S0.shape[1]
    ii = lax.broadcasted_iota(jnp.int32, (C, C), 0)
    jj = lax.broadcasted_iota(jnp.int32, (C, C), 1)
    strict = jj < ii
    i2 = lax.broadcasted_iota(jnp.int32, (C, 2 * C), 0)
    j2 = lax.broadcasted_iota(jnp.int32, (C, 2 * C), 1)
    incl2 = jnp.where(j2 >= C, j2 - C, j2) <= i2
    g = run_sum(lw)
    ig = jnp.exp(-g)
    ar = jnp.concatenate([a * jnp.exp(g - lw), r * jnp.exp(g)], axis=1)
    bk = jnp.concatenate([b * ig, k * ig], axis=1)
    m = b_nt(ar, bk)
    a_ab = jnp.where(strict, m[:, :C, :C], 0.0)
    a_ak = jnp.where(strict, m[:, :C, C:], 0.0)
    b_r = jnp.where(incl2, m[:, C:, :], 0.0)
    p = b_nt(ar, S0)
    u = p[:, :C] + b_nn(a_ak, v)
    nmat, n = a_ab, 1
    while n < C:
        n *= 2
        if n < C:
            z = b_nn(nmat, jnp.concatenate([u, nmat], axis=2))
            u, nmat = u + z[:, :, :V], z[:, :, V:]
        else:
            u = u + b_nn(nmat, u)
    uv = jnp.concatenate([u, v], axis=1)
    y = p[:, C:] + b_nn(b_r, uv)
    g_end = g[:, C - 1:C, :]
    dec = jnp.exp(g_end - g)
    s_new = S0 * jnp.exp(g_end) + b_tn(uv, jnp.concatenate([b * dec, k * dec], axis=1))
    return y, s_new


def rwkv_fwd(r, lw, k, v, a, b):
    H, S, _ = r.shape
    C = RW_CHUNK

    def body(r_ref, lw_ref, k_ref, v_ref, a_ref, b_ref, y_ref, s_ref, s_scr):
        @pl.when(pl.program_id(0) == 0)
        def _():
            s_scr[...] = jnp.zeros_like(s_scr)

        s0 = s_scr[...]
        s_ref[0] = s0
        y, s1 = rwkv_chunk(s0, r_ref[...], lw_ref[...], k_ref[...], v_ref[...], a_ref[...], b_ref[...])
        y_ref[...] = y
        s_scr[...] = s1

    bs = pl.BlockSpec((H, C, HEAD), lambda c: (0, c, 0))
    return pl.pallas_call(
        body, name="rwkv_fwd", grid=(S // C,), in_specs=[bs] * 6,
        out_specs=[bs, pl.BlockSpec((1, H, HEAD, HEAD), lambda c: (c, 0, 0, 0))],
        out_shape=[jax.ShapeDtypeStruct((H, S, HEAD), F32), jax.ShapeDtypeStruct((S // C, H, HEAD, HEAD), F32)],
        scratch_shapes=[pltpu.VMEM((H, HEAD, HEAD), F32)],
        compiler_params=_cp(("arbitrary",)),
    )(r, lw, k, v, a, b)


def rwkv_bwd(r, lw, k, v, a, b, states, dy):
    H, S, _ = r.shape
    C = RW_CHUNK
    nc = S // C

    def body(r_ref, lw_ref, k_ref, v_ref, a_ref, b_ref, s_ref, dy_ref, dr, dlw, dk, dv, da, db, ds_scr):
        @pl.when(pl.program_id(0) == 0)
        def _():
            ds_scr[...] = jnp.zeros_like(ds_scr)

        _, vjp = jax.vjp(rwkv_chunk, s_ref[0], r_ref[...], lw_ref[...], k_ref[...], v_ref[...], a_ref[...], b_ref[...])
        grads = vjp((dy_ref[...], ds_scr[...]))
        ds_scr[...] = grads[0]
        for o, gv in zip((dr, dlw, dk, dv, da, db), grads[1:]):
            o[...] = gv

    bs = pl.BlockSpec((H, C, HEAD), lambda c: (0, nc - 1 - c, 0))
    return pl.pallas_call(
        body, name="rwkv_bwd", grid=(nc,),
        in_specs=[bs] * 6 + [pl.BlockSpec((1, H, HEAD, HEAD), lambda c: (nc - 1 - c, 0, 0, 0)), bs],
        out_specs=[bs] * 6, out_shape=[jax.ShapeDtypeStruct((H, S, HEAD), F32)] * 6,
        scratch_shapes=[pltpu.VMEM((H, HEAD, HEAD), F32)],
        compiler_params=_cp(("arbitrary",)),
    )(r, lw, k, v, a, b, states, dy)


def loss_head(y, target, tm=512):
    S = y.shape[0]

    def body(y_ref, t_ref, dy_ref, l_ref):
        e = y_ref[...] - t_ref[...]
        dy_ref[...] = e * (1.0 / D)
        part = jnp.broadcast_to(0.5 * jnp.sum(jnp.mean(e * e, axis=-1, keepdims=True)), (1, 128))

        @pl.when(pl.program_id(0) == 0)
        def _():
            l_ref[...] = part

        @pl.when(pl.program_id(0) != 0)
        def _():
            l_ref[...] += part

    tile = pl.BlockSpec((tm, D), lambda t: (t, 0))
    return pl.pallas_call(
        body, name="loss_head", grid=(S // tm,), in_specs=[tile, tile],
        out_specs=[tile, pl.BlockSpec((1, 128), lambda t: (0, 0))],
        out_shape=[jax.ShapeDtypeStruct((S, D), F32), jax.ShapeDtypeStruct((1, 128), F32)],
        compiler_params=_cp(("arbitrary",)),
    )(y, target)


def _row_tile(rows, cols, budget=1 << 19):
    best = None
    for tr in range(8, rows + 1, 8):
        if rows % tr == 0 and tr * cols <= budget:
            best = tr
    return best or rows


def _adam(w, g, m, v):
    m = ADAM_B1 * m + (1.0 - ADAM_B1) * g
    v = ADAM_B2 * v + (1.0 - ADAM_B2) * jnp.square(g)
    m_hat = m / (1.0 - ADAM_B1 ** ADAM_STEP)
    v_hat = v / (1.0 - ADAM_B2 ** ADAM_STEP)
    return -ADAM_LR * (m_hat / (jnp.sqrt(v_hat) + ADAM_EPS) + ADAM_WD * w), m, v


def sum_slots(name, parts, dtype=F32, extras=()):
    n = 0 if parts is None else parts.shape[0]
    R, C = extras[0].shape if parts is None else parts.shape[1:]
    tr = _row_tile(R, C * (n + len(extras)))
    ins = ([] if parts is None else [parts]) + list(extras)

    def body(*refs):
        terms = [] if parts is None else [refs[0][i] for i in range(n)]
        terms += [r[...] for r in refs[len(ins) - len(extras):len(ins)]]
        s = terms[0].astype(F32)
        for t in terms[1:]:
            s = s + t.astype(F32)
        refs[len(ins)][...] = s.astype(dtype)

    tile = pl.BlockSpec((tr, C), lambda t: (t, 0))
    return pl.pallas_call(
        body, name=name, grid=(R // tr,),
        in_specs=([] if parts is None else [pl.BlockSpec((n, tr, C), lambda t: (0, t, 0))]) + [tile] * len(extras),
        out_specs=tile, out_shape=jax.ShapeDtypeStruct((R, C), dtype), compiler_params=_cp(("parallel",)),
    )(*ins)


def sum_own_half(name, split, theirs, c, dtype):
    nq, _, rh, cols = split.shape
    tr = _row_tile(rh, 2 * cols)

    def body(c_ref, a_ref, b_ref, o_ref):
        o_ref[...] = (a_ref[...] + b_ref[...]).astype(dtype)

    tile = pl.BlockSpec((None, tr, cols), lambda q, t, c_ref: (q, t, 0))
    return pl.pallas_call(
        body, name=name,
        grid_spec=pltpu.PrefetchScalarGridSpec(
            num_scalar_prefetch=1, grid=(nq, rh // tr),
            in_specs=[pl.BlockSpec((None, None, tr, cols), lambda q, t, c_ref: (q, c_ref[0], t, 0)), tile],
            out_specs=tile),
        out_shape=jax.ShapeDtypeStruct((nq, rh, cols), dtype), compiler_params=_cp(("parallel", "parallel")),
    )(jnp.reshape(c, (1,)).astype(jnp.int32), split, theirs)


def sum_landed(name, landed, chip_sum, p):
    n, rh, cols = landed.shape
    tr = _row_tile(rh, (n + 1) * cols)

    def body(p_ref, l_ref, own_ref, o_ref):
        s = l_ref[0].astype(F32)
        for i in range(1, n):
            s = s + l_ref[i].astype(F32)
        o_ref[...] = s + own_ref[...].astype(F32)

    return pl.pallas_call(
        body, name=name,
        grid_spec=pltpu.PrefetchScalarGridSpec(
            num_scalar_prefetch=1, grid=(rh // tr,),
            in_specs=[pl.BlockSpec((n, tr, cols), lambda t, p_ref: (0, t, 0)),
                      pl.BlockSpec((None, tr, cols), lambda t, p_ref: (p_ref[0], t, 0))],
            out_specs=pl.BlockSpec((tr, cols), lambda t, p_ref: (t, 0))),
        out_shape=jax.ShapeDtypeStruct((rh, cols), F32), compiler_params=_cp(("parallel",)),
    )(jnp.reshape(p, (1,)).astype(jnp.int32), landed, chip_sum)


def adam_step(name, ga, gb, w, m, v):
    R, C = w.shape
    tr = _row_tile(R, C, 1 << 17)
    ins = [ga] + ([gb] if gb is not None else []) + [w, m, v]

    def body(*refs):
        g = refs[0][...]
        if gb is not None:
            g = g + refs[1][...]
        w_ref, m_ref, v_ref, g_out, d_out, m_out, v_out = refs[len(ins) - 3:]
        d, m2, v2 = _adam(w_ref[...], g, m_ref[...], v_ref[...])
        g_out[...] = g
        d_out[...] = d
        m_out[...] = m2
        v_out[...] = v2

    tile = pl.BlockSpec((tr, C), lambda t: (t, 0))
    return pl.pallas_call(
        body, name=name, grid=(R // tr,), in_specs=[tile] * len(ins), out_specs=[tile] * 4,
        out_shape=[jax.ShapeDtypeStruct((R, C), F32)] * 4, compiler_params=_cp(("parallel",)),
    )(*ins)


def adam_ffn(name, g_pieces, w, m, v, transposed=False):
    if transposed:
        res = adam_ffn(name, g_pieces, *(jnp.swapaxes(a, 2, 3) for a in (w, m, v)))
        return [jnp.swapaxes(r, 2, 3) for r in res]
    _, _, R, C = w.shape
    tr = _row_tile(R, 4 * C, 1 << 17)

    def body(g00, g01, g10, g11, w_ref, m_ref, v_ref, g_out, d_out, m_out, v_out):
        for l, j, g_ref in ((0, 0, g00), (0, 1, g01), (1, 0, g10), (1, 1, g11)):
            g = g_ref[...]
            d, m2, v2 = _adam(w_ref[l, j], g, m_ref[l, j], v_ref[l, j])
            g_out[l, j] = g
            d_out[l, j] = d
            m_out[l, j] = m2
            v_out[l, j] = v2

    piece = pl.BlockSpec((tr, C), lambda t: (t, 0))
    full = pl.BlockSpec((2, 2, tr, C), lambda t: (0, 0, t, 0))
    return pl.pallas_call(
        body, name=name, grid=(R // tr,), in_specs=[piece] * 4 + [full] * 3, out_specs=[full] * 4,
        out_shape=[jax.ShapeDtypeStruct(w.shape, F32)] * 4, compiler_params=_cp(("parallel",)),
    )(*g_pieces, w, m, v)


def _place():
    return lax.axis_index("x"), lax.axis_index("y"), lax.axis_index("c")


def _flip(me, mask):
    return tuple(1 - v if mk else v for v, mk in zip(me, mask))


CHIP_MASKS = ((1, 0, 0), (0, 1, 0), (1, 1, 0))
ALL_MASKS = tuple((a, b, c) for a in (0, 1) for b in (0, 1) for c in (0, 1) if (a, b, c) != (0, 0, 0))


def _chip(dev):
    return 2 * dev[0] + dev[1]


def _devno(dev):
    return 4 * dev[0] + 2 * dev[1] + dev[2]


class Pushes:
    def __init__(self, arrays, out_shapes, masks, copies, src_of, dst_of, alias=False):
        self.arrays, self.out_shapes, self.masks, self.copies = list(arrays), list(out_shapes), masks, copies
        self.src_of, self.dst_of, self.alias = src_of, dst_of, alias
        self.n = len(self.arrays)

    def sem_shapes(self):
        k = self.n * len(self.masks) * self.copies
        return [pltpu.SemaphoreType.DMA((k,)), pltpu.SemaphoreType.DMA((k,))]

    def ops(self, ins, outs, send_sems, recv_sems):
        me = _place()
        sends, lands = [], []
        for i in range(self.n):
            for j, mk in enumerate(self.masks):
                peer = _flip(me, mk)
                srcs, dsts = self.src_of(ins[i], me, j), self.dst_of(outs[i], me, j)
                here = self.dst_of(outs[i], peer, j)
                for q in range(self.copies):
                    sem = (i * len(self.masks) + j) * self.copies + q
                    sends.append(pltpu.make_async_remote_copy(
                        src_ref=srcs[q], dst_ref=dsts[q], send_sem=send_sems.at[sem], recv_sem=recv_sems.at[sem],
                        device_id=peer, device_id_type=MESH))
                    lands.append(pltpu.make_async_remote_copy(
                        src_ref=here[q], dst_ref=here[q], send_sem=send_sems.at[sem], recv_sem=recv_sems.at[sem],
                        device_id=peer, device_id_type=MESH))

        def start():
            for cp in sends:
                cp.start()

        def wait():
            for cp in lands:
                cp.wait_recv()
            for cp in sends:
                cp.wait_send()

        return start, wait


_HBM = pl.BlockSpec(memory_space=pl.ANY)


def exchange(name, p, local_of=None):
    n = p.n

    def body(*refs):
        ins, outs = refs[:n], refs[n:2 * n]
        start, wait = p.ops(ins, outs, refs[2 * n], refs[2 * n + 1])
        locals_ = []
        if local_of is not None:
            for i in range(n):
                src, dst = local_of(ins[i], outs[i], _place())
                locals_.append(pltpu.make_async_copy(src, dst, refs[2 * n + 2].at[i]))
                locals_[-1].start()
        start()
        wait()
        for cp in locals_:
            cp.wait()

    return pl.pallas_call(
        body, name=name, in_specs=[_HBM] * n, out_specs=[_HBM] * n, out_shape=p.out_shapes,
        scratch_shapes=p.sem_shapes() + ([pltpu.SemaphoreType.DMA((n,))] if local_of is not None else []),
        input_output_aliases={i: i for i in range(n)} if p.alias else {},
    )(*p.arrays)


def _half(c, rows):
    return pl.ds(c * (rows // 2), rows // 2)


def gather_pushes(arrays):
    outs = [jax.ShapeDtypeStruct((N_CHIPS,) + a.shape, a.dtype) for a in arrays]
    sib = len(CHIP_MASKS)
    return Pushes(arrays, outs, CHIP_MASKS + ((0, 0, 1),), 1,
                  src_of=lambda r, me, j: [r] if j == sib else [r.at[_half(me[2], r.shape[0])]],
                  dst_of=lambda o, sender, j: [o.at[_chip(sender)]] if j == sib else
                  [o.at[_chip(sender), _half(sender[2], o.shape[1])]])


def gather_swap(name, got):
    outs = [jax.ShapeDtypeStruct(a.shape, a.dtype) for a in got]
    return exchange(name, Pushes(
        got, outs, ((0, 0, 1),), len(CHIP_MASKS),
        src_of=lambda r, me, j: [r.at[_chip(_flip(me, mk)), _half(me[2], r.shape[1])] for mk in CHIP_MASKS],
        dst_of=lambda o, sender, j: [o.at[_chip(_flip(sender, mk)), _half(sender[2], o.shape[1])] for mk in CHIP_MASKS],
        alias=True))


def reduce_swap(arrays):
    split = [a.reshape(N_CHIPS, 2, a.shape[1] // 2, a.shape[2]) for a in arrays]
    half_shapes = [jax.ShapeDtypeStruct((N_CHIPS,) + a.shape[2:], F32) for a in split]
    return split, Pushes(split, half_shapes, ((0, 0, 1),), 1,
                         src_of=lambda r, me, j: [r.at[:, 1 - me[2]]], dst_of=lambda o, sender, j: [o])


def reduce_begin(tag, names, arrays, wire):
    split, pushes = reduce_swap(arrays)
    return reduce_sum(names, split, exchange(f"grad_pre_swap_{tag}", pushes), wire)


def reduce_sum(names, split, theirs, wire):
    c = lax.axis_index("c")
    chip_sum = [sum_own_half(f"sum2_{nm}", a, t, c, dt) for nm, a, t, dt in zip(names, split, theirs, wire)]
    pushes = Pushes(chip_sum, [jax.ShapeDtypeStruct((len(CHIP_MASKS),) + a.shape[1:], a.dtype) for a in chip_sum],
                    CHIP_MASKS, 1,
                    src_of=lambda r, me, j: [r.at[_chip(_flip(me, CHIP_MASKS[j]))]],
                    dst_of=lambda o, sender, j: [o.at[j]])
    return chip_sum, pushes


def reduce_end(tag, names, chip_sum, landed):
    x, y, c = _place()
    halves = [sum_landed(f"sum4_{nm}", p, a, _chip((x, y, c))) for nm, p, a in zip(names, landed, chip_sum)]
    others = exchange(f"grad_final_swap_{tag}", Pushes(
        halves, [jax.ShapeDtypeStruct(a.shape, F32) for a in halves], ((0, 0, 1),), 1,
        src_of=lambda r, me, j: [r], dst_of=lambda o, sender, j: [o]))
    return [jnp.concatenate([jnp.where(c == 0, h, o), jnp.where(c == 0, o, h)], axis=0) for h, o in zip(halves, others)]


def gather_all(arrays):
    outs = [jax.ShapeDtypeStruct((8,) + a.shape, a.dtype) for a in arrays]
    return exchange("gather_replicated", Pushes(
        arrays, outs, ALL_MASKS, 1, src_of=lambda r, me, j: [r], dst_of=lambda o, sender, j: [o.at[_devno(sender)]]),
        local_of=lambda r, o, me: (r, o.at[_devno(me)]))


def _unshard_cols(g):
    return jnp.transpose(g, (1, 0, 2)).reshape(g.shape[1], -1)


def _shard_cols(a):
    return jnp.transpose(a.reshape(a.shape[0], N_CHIPS, -1), (1, 0, 2))


class Weights(dict):
    def ride(self, kernel_name):
        return None

    def arrived(self, kernel_name, outs):
        pass


def _forward_backward(x, tgt, W, grads_early=None):
    S = x.shape[0]
    G = {}
    sd = jax.ShapeDtypeStruct

    hidden = {}

    def ffn(xin, l, j):
        out, a_sav, b_sav, *rode = ffn_fwd(xin, W["ffn_norm"][l][j], W["ffn_w_gate", l, j], W["ffn_w_up", l, j],
                                           W["ffn_w_down", l, j], l, j, W.ride(f"ffn_fwd_{l}{j}"))
        hidden[l, j] = [a_sav, b_sav]
        W.arrived(f"ffn_fwd_{l}{j}", rode)
        return out

    def ffn_back(xin, dout, l, j):
        gn = W["ffn_norm"][l][j]
        dh, G["ffn_w_gate", l, j], G["ffn_w_up", l, j], G["ffn_w_down", l, j] = ffn_bwd(
            xin, gn, W["ffn_w_gate", l, j], W["ffn_w_up", l, j], W["ffn_w_down", l, j], dout, *hidden[l, j], l, j)
        dx, G[("ffn_norm", l, j)] = norm_bwd(f"ffn_norm_bwd_{l}{j}", xin, gn, dh, dout)
        return dx

    x0 = x
    x1 = ffn(x0, 0, 0)
    g0 = W["mix_norm"][0]
    sbq, sbk, sbv = tile_fwd(f_attn_sb, "attn_in_sb", [x1], [g0, W["attn_w_in"][0]], [sd((S, SB_W), F32)] * 3, 256)
    dl_shape = sd((DL_PAIRS, S, 128), F32)
    qn, = tile_fwd(f_attn_qk, "attn_in_q", [x1], [g0, W["attn_w_in"][1], W["attn_q_norm"]], [dl_shape], 256)
    kn, = tile_fwd(f_attn_qk, "attn_in_k", [x1], [g0, W["attn_w_in"][2], W["attn_k_norm"]], [dl_shape], 256)
    vv, = tile_fwd(f_attn_v, "attn_in_v", [x1], [g0, W["attn_w_in"][3]], [dl_shape], 256)
    oa, sb_wts, *rode = sb_fwd(sbq, sbk, sbv, W.ride("sb_fwd"))
    W.arrived("sb_fwd", rode)
    qs, ks, vs = (reorder(nm, t, DIL, False) for nm, t in (("sub_q", qn), ("sub_k", kn), ("sub_v", vv)))
    o_s, lse_s, *rode = dil_fwd(qs, ks, vs, W["bias_mat"], W.ride("dil_fwd"))
    W.arrived("dil_fwd", rode)
    o_n, lse_n = reorder("nat_o", o_s, DIL, True), reorder("nat_lse", lse_s, DIL, True)
    x2, = tile_fwd(f_attn_out, "attn_out", [x1, oa, o_n, lse_n], [W["attn_w_out"]], [sd((S, D), F32)], 256)
    x3 = ffn(x2, 0, 1)
    x4 = ffn(x3, 1, 0)
    g1 = W["mix_norm"][1]
    h, hs = norm_shift_fwd(x4, g1)
    mix = W["rw_mix"]
    r, = tile_fwd(f_rw_proj, "rw_proj_r", [h, hs], [mix[0:1], W["rw_wr"]], [sd((S, D), F32)], 256)
    k, = tile_fwd(f_rw_proj, "rw_proj_k", [h, hs], [mix[2:3], W["rw_wk"]], [sd((S, D), F32)], 256)
    v, = tile_fwd(f_rw_proj, "rw_proj_v", [h, hs], [mix[3:4], W["rw_wv"]], [sd((S, D), F32)], 256)
    mix3 = jnp.concatenate([mix[1:2], mix[4:5], mix[5:6]], axis=0)
    mid_w = [mix3, W["rw_w0"], W["rw_a0"], W["rw_kk"], W["rw_ka"], W["rw_w1"], W["rw_w2"], W["rw_a1"], W["rw_a2"],
             W["rw_g1"], W["rw_g2"]]
    hshape = sd((RW_H, S, HEAD), F32)
    mid_tiles = [h, hs, r, k, v]
    rh, lwh, kh, vh, ah, bh, gate = tile_fwd(f_rw_mid, "rw_mid", mid_tiles, mid_w, [hshape] * 6 + [sd((S, D), F32)], 128)
    yh, states = rwkv_fwd(rh, lwh, kh, vh, ah, bh)
    post_w = [W["rw_lnx_g"], W["rw_lnx_b"], W["rw_rk"], W["rw_wo"]]
    post_tiles = [yh, rh, kh, vh, gate, x4]
    x5, = tile_fwd(f_rw_post, "rw_post", post_tiles, post_w, [sd((S, D), F32)], 128)
    x6 = ffn(x5, 1, 1)
    dx6, loss_part = loss_head(x6, tgt)

    dx5 = ffn_back(x5, dx6, 1, 1)
    (dyh, drh, dkh, dvh, dgate, dx4), (d_lng, d_lnb, d_rk, d_wo) = tile_bwd(
        f_rw_post, "rw_post_bwd", post_tiles, post_w, [dx5], 128, [True] * 6, [True] * 4)
    drh2, dlwh, dkh2, dvh2, dah, dbh = rwkv_bwd(rh, lwh, kh, vh, ah, bh, states, dyh)
    mid_cts = [(drh, drh2), dlwh, (dkh, dkh2), (dvh, dvh2), dah, dbh, dgate]
    (dh, dhs, dr, dk, dv), dmid_w = tile_bwd(f_rw_mid, "rw_mid_bwd", mid_tiles, mid_w, mid_cts, 128,
                                             [True] * 5, [True] * len(mid_w))
    dmix = {}
    for nm, ct, row, wname in (("r", dr, 0, "rw_wr"), ("k", dk, 2, "rw_wk"), ("v", dv, 3, "rw_wv")):
        (dh, dhs), (dmix[row], G[wname]) = tile_bwd(
            f_rw_proj, f"rw_proj_{nm}_bwd", [h, hs], [mix[row:row + 1], W[wname]], [ct], 256,
            [True, True], [True, True], acc={0: dh, 1: dhs})
    dx4, G[("mix_norm", 1)] = norm_shift_bwd(x4, g1, dh, dhs, dx4)
    dmix3 = dmid_w[0]
    G["rw_mix"] = jnp.concatenate([dmix[0], dmix3[0:1], dmix[2], dmix[3], dmix3[1:2], dmix3[2:3]], axis=0)
    for nm, gv in zip(("rw_w0", "rw_a0", "rw_kk", "rw_ka", "rw_w1", "rw_w2", "rw_a1", "rw_a2", "rw_g1", "rw_g2"), dmid_w[1:]):
        G[nm] = gv
    G["rw_lnx_g"], G["rw_lnx_b"], G["rw_rk"], G["rw_wo"] = d_lng, d_lnb, d_rk, d_wo
    dx3 = ffn_back(x3, dx4, 1, 0)
    dx2 = ffn_back(x2, dx3, 0, 1)
    (dx1, doa, do_n, dlse_n), (G["attn_w_out"],) = tile_bwd(
        f_attn_out, "attn_out_bwd", [x1, oa, o_n, lse_n], [W["attn_w_out"]], [dx2], 256, [True] * 4, [True])
    do_s, dlse_s = reorder("sub_do", do_n, DIL, False), reorder("sub_dlse", dlse_n, DIL, False)
    ride, swapped = grads_early(G) if grads_early is not None else (None, None)
    dqs, dks, dvs, dsum, *rode = dil_bwd(qs, ks, vs, W["bias_mat"], o_s, lse_s, do_s, dlse_s, ride)
    ride, landed = swapped(rode) if swapped is not None else (None, None)
    G["rel_bias"] = bias_grad(dsum, W["buckets"])
    dqn, dkn, dvv = (reorder(nm, t, DIL, True) for nm, t in (("nat_dq", dqs), ("nat_dk", dks), ("nat_dv", dvs)))
    dsbq, dsbk, dsbv, *rode = sb_bwd(sbq, sbk, sbv, doa, sb_wts, ride)
    if landed is not None:
        landed(rode)
    dg0 = []
    dwin = []
    (dx1,), (dg, dw) = tile_bwd(f_attn_sb, "attn_in_sb_bwd", [x1], [g0, W["attn_w_in"][0]], [dsbq, dsbk, dsbv], 256,
                                [True], [True, True], acc={0: dx1})
    dg0.append(dg), dwin.append(dw)
    (dx1,), (dg, dw, G["attn_q_norm"]) = tile_bwd(f_attn_qk, "attn_in_q_bwd", [x1], [g0, W["attn_w_in"][1], W["attn_q_norm"]],
                                                  [dqn], 256, [True], [True] * 3, acc={0: dx1})
    dg0.append(dg), dwin.append(dw)
    (dx1,), (dg, dw, G["attn_k_norm"]) = tile_bwd(f_attn_qk, "attn_in_k_bwd", [x1], [g0, W["attn_w_in"][2], W["attn_k_norm"]],
                                                  [dkn], 256, [True], [True] * 3, acc={0: dx1})
    dg0.append(dg), dwin.append(dw)
    (dx1,), (dg, dw) = tile_bwd(f_attn_v, "attn_in_v_bwd", [x1], [g0, W["attn_w_in"][3]], [dvv], 256,
                                [True], [True, True], acc={0: dx1})
    dg0.append(dg), dwin.append(dw)
    G[("mix_norm", 0)] = dg0
    G["attn_w_in"] = dwin
    dx0 = ffn_back(x0, dx1, 0, 0)
    return loss_part, dx0, G


VEC_ROWS = ("ffn_norm", "rw_mix", "rw_w0", "rw_a0", "rw_kk", "rw_ka", "rw_lnx_g", "rw_lnx_b")


def kernel(x, ffn_norm, ffn_w_gate, ffn_w_up, ffn_w_down, mix_norm, rel_bias, attn_w_in, attn_q_norm, attn_k_norm, attn_w_out, rw_mix, rw_w0, rw_w1, rw_w2, rw_a0, rw_a1, rw_a2, rw_g1, rw_g2, rw_kk, rw_ka, rw_rk, rw_wr, rw_wk, rw_wv, rw_wo, rw_lnx_g, rw_lnx_b, loss_target, m_ffn_norm, m_ffn_w_gate, m_ffn_w_up, m_ffn_w_down, m_mix_norm, m_rel_bias, m_attn_w_in, m_attn_q_norm, m_attn_k_norm, m_attn_w_out, m_rw_mix, m_rw_w0, m_rw_w1, m_rw_w2, m_rw_a0, m_rw_a1, m_rw_a2, m_rw_g1, m_rw_g2, m_rw_kk, m_rw_ka, m_rw_rk, m_rw_wr, m_rw_wk, m_rw_wv, m_rw_wo, m_rw_lnx_g, m_rw_lnx_b, v_ffn_norm, v_ffn_w_gate, v_ffn_w_up, v_ffn_w_down, v_mix_norm, v_rel_bias, v_attn_w_in, v_attn_q_norm, v_attn_k_norm, v_attn_w_out, v_rw_mix, v_rw_w0, v_rw_w1, v_rw_w2, v_rw_a0, v_rw_a1, v_rw_a2, v_rw_g1, v_rw_g2, v_rw_kk, v_rw_ka, v_rw_rk, v_rw_wr, v_rw_wk, v_rw_wv, v_rw_wo, v_rw_lnx_g, v_rw_lnx_b):
    names = ["ffn_norm", "ffn_w_gate", "ffn_w_up", "ffn_w_down", "mix_norm", "rel_bias", "attn_w_in", "attn_q_norm",
             "attn_k_norm", "attn_w_out", "rw_mix", "rw_w0", "rw_w1", "rw_w2", "rw_a0", "rw_a1", "rw_a2", "rw_g1", "rw_g2",
             "rw_kk", "rw_ka", "rw_rk", "rw_wr", "rw_wk", "rw_wv", "rw_wo", "rw_lnx_g", "rw_lnx_b"]
    loc = locals()
    w = {n: loc[n] for n in names}
    mom = {n: loc["m_" + n] for n in names}
    vel = {n: loc["v_" + n] for n in names}
    S = x.shape[1]

    ffn3 = ("ffn_w_gate", "ffn_w_up", "ffn_w_down")
    rw_mats = ("rw_w1", "rw_w2", "rw_a1", "rw_a2", "rw_g1", "rw_g2", "rw_wr", "rw_wk", "rw_wv", "rw_wo")
    cols_split = ("attn_w_out", "rw_w2", "rw_a2", "rw_g2")
    shard = {"vec": jnp.concatenate([w[n].reshape(-1, 256) for n in VEC_ROWS], axis=0)}
    for n in ffn3:
        for l in range(2):
            for j in range(2):
                shard[n, l, j] = w[n][l, j].astype(BF16)
    for n in ("attn_w_in", "attn_w_out") + rw_mats:
        shard[n] = w[n].reshape(-1, w[n].shape[-1]).astype(BF16)
    ffn_keys = lambda l, j: [(n, l, j) for n in ffn3]
    w_groups = {"first": ["vec"] + ffn_keys(0, 0) + ["attn_w_in", "attn_w_out"],
                "ffn_fwd_00": ffn_keys(0, 1), "sb_fwd": ffn_keys(1, 0) + list(rw_mats), "dil_fwd": ffn_keys(1, 1)}
    label = lambda key: key if isinstance(key, str) else f"{key[0]}_{key[1]}{key[2]}"

    class Streamed(Weights):
        def ride(self, kernel_name):
            keys_ = w_groups.get(kernel_name)
            return gather_pushes([shard[k] for k in keys_]) if keys_ else None

        def arrived(self, kernel_name, outs):
            if not outs:
                return
            for key, g in zip(w_groups[kernel_name], gather_swap(f"gather_swap_{kernel_name}", outs)):
                if key == "vec":
                    vec_full = _unshard_cols(g)
                    self["ffn_norm"] = [[vec_full[2 * l + j][None] for j in range(2)] for l in range(2)]
                    self["rw_mix"] = vec_full[4:10]
                    for i, n in enumerate(("rw_w0", "rw_a0", "rw_kk", "rw_ka", "rw_lnx_g", "rw_lnx_b")):
                        self[n] = vec_full[10 + i][None]
                elif key == "attn_w_in":
                    self[key] = [g[p] for p in range(N_CHIPS)]
                elif key in cols_split:
                    self[key] = _unshard_cols(g)
                elif isinstance(key, str):
                    self[key] = g.reshape(D, -1)
                else:
                    self[key] = g

    buckets = _bucket_maps()
    W = Streamed({"mix_norm": [mix_norm[0:1], mix_norm[1:2]], "attn_q_norm": attn_q_norm, "attn_k_norm": attn_k_norm,
                  "rw_rk": rw_rk[0][:, None, :], "buckets": buckets, "bias_mat": bias_table(rel_bias, buckets)})
    W.arrived("first", exchange("gather_weights", W.ride("first")))

    def slots(key, G):
        if key == "vec":
            rows = [G[("ffn_norm", l, j)] for l in range(2) for j in range(2)] + [G["rw_mix"]] + \
                   [G[n] for n in ("rw_w0", "rw_a0", "rw_kk", "rw_ka", "rw_lnx_g", "rw_lnx_b")]
            return _shard_cols(jnp.concatenate(rows, axis=0))
        if key == "attn_w_in":
            return jnp.stack(G[key])
        if key in cols_split:
            return _shard_cols(G[key])
        if isinstance(key, str):
            return G[key].reshape(N_CHIPS, D // N_CHIPS, -1)
        return G[key]

    g_groups = {"early": ffn_keys(1, 1) + ffn_keys(1, 0) + ffn_keys(0, 1) + list(rw_mats) + ["attn_w_out"],
                "late": ["vec", "attn_w_in"] + ffn_keys(0, 0)}
    wire = lambda keys: [F32 if k == "vec" else BF16 for k in keys]
    part = {}

    def grads_early(G):
        keys = g_groups["early"]
        names_ = [label(k) for k in keys]
        split, swap_pushes = reduce_swap([slots(k, G) for k in keys])

        def swapped(theirs):
            chip_sum, pushes = reduce_sum(names_, split, theirs, wire(keys))
            return pushes, lambda landed: part.update(zip(keys, reduce_end("early", names_, chip_sum, landed)))

        return swap_pushes, swapped

    loss_part, dx, G = _forward_backward(x[0], loss_target[0], W, grads_early)
    loss = lax.psum(loss_part[0, 0], ("x", "y", "c"))
    keys = g_groups["late"]
    chip_sum, pushes = reduce_begin("late", [label(k) for k in keys], [slots(k, G) for k in keys], wire(keys))
    part.update(zip(keys, reduce_end("late", [label(k) for k in keys], chip_sum, exchange("scatter_grads", pushes))))

    rep = jnp.concatenate([G[("mix_norm", 0)][0] + G[("mix_norm", 0)][1] + G[("mix_norm", 0)][2] + G[("mix_norm", 0)][3],
                           G[("mix_norm", 1)]], axis=0).reshape(16, 128)
    rep = jnp.concatenate([rep, G["rel_bias"], jnp.pad(G["attn_q_norm"], ((0, 0), (0, 64))),
                           jnp.pad(G["attn_k_norm"], ((0, 0), (0, 64))), G["rw_rk"].reshape(8, 128),
                           jnp.zeros((2, 128), F32)], axis=0)
    rep_sum = sum_slots("sum_replicated", gather_all([rep])[0])
    g_rep = {
        "mix_norm": rep_sum[0:16].reshape(2, D),
        "rel_bias": jnp.transpose(rep_sum[16:28, :N_BUCKETS]),
        "attn_q_norm": rep_sum[28:29, :HEAD], "attn_k_norm": rep_sum[29:30, :HEAD],
        "rw_rk": rep_sum[30:38].reshape(1, RW_H, HEAD),
    }

    out = {}

    def adam(n, ga, gb):
        shp = w[n].shape
        to2 = lambda a: a.reshape(-1, shp[-1])
        res = adam_step(f"adam_{n}", to2(ga), None if gb is None else to2(gb), to2(w[n]), to2(mom[n]), to2(vel[n]))
        out[n] = tuple(r.reshape(shp) for r in res)

    for n in ffn3:
        out[n] = tuple(adam_ffn(f"adam_{n}", [part[n, l, j] for l in range(2) for j in range(2)], w[n], mom[n], vel[n],
                                transposed=n != "ffn_w_down"))
    for n in ("attn_w_in", "attn_w_out") + rw_mats:
        adam(n, part[n], None)
    rows = {"ffn_norm": (0, 4), "rw_mix": (4, 10), "rw_w0": (10, 11), "rw_a0": (11, 12), "rw_kk": (12, 13),
            "rw_ka": (13, 14), "rw_lnx_g": (14, 15), "rw_lnx_b": (15, 16)}
    for n, (lo, hi) in rows.items():
        adam(n, part["vec"][lo:hi], None)
    for n, gv in g_rep.items():
        adam(n, gv, None)

    grads = [out[n][0] for n in names]
    deltas = [out[n][1] for n in names]
    new_m = [out[n][2] for n in names]
    new_v = [out[n][3] for n in names]
    return (loss, dx[None], *grads, *deltas, *new_m, *new_v)
```

```python
import functools
import math

import jax
import jax.numpy as jnp
from jax import lax
from jax.experimental import pallas as pl
from jax.experimental.pallas import tpu as pltpu

F32, BF16 = jnp.float32, jnp.bfloat16
HI = lax.Precision.HIGHEST
MESH = pl.DeviceIdType.MESH

D = 1024
HEAD = 64
N_CHIPS = 4
FF_SHARD = 704
SB_W = 256
DL_HEADS = 12
DL_PAIRS = 6
DIL = (1, 4, 16)
QBLK = 128
N_BUCKETS = 32
MAX_DISTANCE = 2048
RW_H = 16
RW_CHUNK = 64
NORM_EPS = 1e-6
GN_EPS = 64e-5
NEG_INF = -1e30
VMEM_LIMIT = 56 * 1024 * 1024

ADAM_LR, ADAM_B1, ADAM_B2, ADAM_EPS, ADAM_WD, ADAM_STEP = 0.001, 0.9, 0.999, 1e-08, 0.01, 10


def _cp(sem):
    return pltpu.CompilerParams(dimension_semantics=sem, vmem_limit_bytes=VMEM_LIMIT)


def _dg(a, b, dims, prec=None):
    return lax.dot_general(a, b, (dims, ((), ())), precision=prec, preferred_element_type=F32)


def _bdot(a, b, dims):
    return _dg(a.astype(BF16), b.astype(BF16), dims)


@jax.custom_vjp
def mm(a, b):
    return _bdot(a, b, ((1,), (0,)))


def _mm_fwd(a, b):
    return _bdot(a, b, ((1,), (0,))), (a, b)


def _mm_bwd(res, g):
    a, b = res
    return _bdot(g, b, ((1,), (1,))), _bdot(a, g, ((0,), (0,)))


mm.defvjp(_mm_fwd, _mm_bwd)


def rms(x, g):
    return x * lax.rsqrt(jnp.mean(x * x, axis=-1, keepdims=True) + NORM_EPS) * g


def _pieces(x):
    x1 = x.astype(BF16)
    r1 = x - x1.astype(F32)
    x2 = r1.astype(BF16)
    return jnp.concatenate([x1, x2, (r1 - x2.astype(F32)).astype(BF16)], axis=-1)


def _group_sum(x, nh):
    w = x.shape[-1]
    e = (lax.broadcasted_iota(jnp.int32, (w, nh), 0) // HEAD == lax.broadcasted_iota(jnp.int32, (w, nh), 1)).astype(BF16)
    s = _dg(_pieces(x), jnp.concatenate([e, e, e], axis=0), ((1,), (0,)))
    return _dg(_pieces(s), jnp.concatenate([e, e, e], axis=1), ((1,), (1,)))


@functools.partial(jax.custom_vjp, nondiff_argnums=(1,))
def group_sum(x, nh):
    return _group_sum(x, nh)


group_sum.defvjp(lambda x, nh: (_group_sum(x, nh), None), lambda nh, _, g: (_group_sum(g, nh),))


def softplus(u):
    return jnp.maximum(u, 0.0) + jnp.log1p(jnp.exp(-jnp.abs(u)))


def to_heads(t, nh=RW_H):
    return jnp.stack([t[:, HEAD * h:HEAD * (h + 1)] for h in range(nh)])


def from_heads(t):
    return jnp.concatenate([t[h] for h in range(t.shape[0])], axis=-1)


def _tile_spec(shape, tm):
    if len(shape) == 2:
        return pl.BlockSpec((tm, shape[1]), lambda t: (t, 0))
    return pl.BlockSpec((shape[0], tm, shape[2]), lambda t: (0, t, 0))


def _full_spec(shape):
    nd = len(shape)
    return pl.BlockSpec(tuple(shape), lambda t: (0,) * nd)


def _rows(a):
    return a.shape[0] if a.ndim == 2 else a.shape[1]


def tile_fwd(f, name, tiles, weights, outs, tm):
    nt, nw = len(tiles), len(weights)

    def body(*refs):
        tv = [r[...] for r in refs[:nt]]
        wv = [r[...].astype(F32) for r in refs[nt:nt + nw]]
        res = f(*tv, *wv)
        if not isinstance(res, (tuple, list)):
            res = (res,)
        for o, v in zip(refs[nt + nw:], res):
            o[...] = v.astype(o.dtype)

    return pl.pallas_call(
        body, name=name, grid=(_rows(tiles[0]) // tm,),
        in_specs=[_tile_spec(a.shape, tm) for a in tiles] + [_full_spec(w.shape) for w in weights],
        out_specs=[_tile_spec(o.shape, tm) for o in outs],
        out_shape=list(outs),
        compiler_params=_cp(("parallel",)),
    )(*tiles, *weights)


def tile_bwd(f, name, tiles, weights, cts, tm, dt, dw, acc=None):
    acc = acc or {}
    groups = [c if isinstance(c, tuple) else (c,) for c in cts]
    cts = [a for grp in groups for a in grp]
    nt, nw, nc = len(tiles), len(weights), len(cts)
    acc_idx = sorted(acc)
    na = len(acc_idx)
    dti = [i for i in range(nt) if dt[i]]
    dwi = [i for i in range(nw) if dw[i]]

    def body(*refs):
        tv = [r[...] for r in refs[:nt]]
        wv = [r[...].astype(F32) for r in refs[nt:nt + nw]]
        crefs = list(refs[nt + nw:nt + nw + nc])
        cv = []
        for grp in groups:
            terms = [crefs.pop(0)[...] for _ in grp]
            cv.append(functools.reduce(lambda a, b: a + b, terms))
        av = {i: r[...] for i, r in zip(acc_idx, refs[nt + nw + nc:nt + nw + nc + na])}
        orefs = refs[nt + nw + nc + na:]

        def g(*diff):
            t2, w2 = list(tv), list(wv)
            for i, v in zip(dti, diff[:len(dti)]):
                t2[i] = v
            for i, v in zip(dwi, diff[len(dti):]):
                w2[i] = v
            res = f(*t2, *w2)
            return tuple(res) if isinstance(res, (tuple, list)) else (res,)

        _, vjp = jax.vjp(g, *[tv[i] for i in dti], *[wv[i] for i in dwi])
        grads = vjp(tuple(cv))
        for k, i in enumerate(dti):
            gt = grads[k]
            if i in av:
                gt = gt + av[i]
            orefs[k][...] = gt
        first = pl.program_id(0) == 0
        for k, i in enumerate(dwi):
            o = orefs[len(dti) + k]
            gw = grads[len(dti) + k]

            @pl.when(first)
            def _(o=o, gw=gw):
                o[...] = gw

            @pl.when(jnp.logical_not(first))
            def _(o=o, gw=gw):
                o[...] += gw

    out_shape = [jax.ShapeDtypeStruct(tiles[i].shape, F32) for i in dti] + \
                [jax.ShapeDtypeStruct(weights[i].shape, F32) for i in dwi]
    res = pl.pallas_call(
        body, name=name, grid=(_rows(tiles[0]) // tm,),
        in_specs=[_tile_spec(a.shape, tm) for a in tiles] + [_full_spec(w.shape) for w in weights] +
                 [_tile_spec(c.shape, tm) for c in cts] + [_tile_spec(tiles[i].shape, tm) for i in acc_idx],
        out_specs=[_tile_spec(tiles[i].shape, tm) for i in dti] + [_full_spec(weights[i].shape) for i in dwi],
        out_shape=out_shape,
        compiler_params=_cp(("arbitrary",)),
    )(*tiles, *weights, *cts, *[acc[i] for i in acc_idx])
    return list(res[:len(dti)]), list(res[len(dti):])


def _ffn_wspec(rows, cols, cfirst):
    if cfirst:
        return pl.BlockSpec((1, rows, cols), lambda c, t: (c, 0, 0))
    return pl.BlockSpec((1, rows, cols), lambda t, c: (c, 0, 0))


def ffn_fwd(x, g, wg, wu, wd, l, j, ride=None, tm=1024):
    S = x.shape[0]
    r_in, r_out, r_shape, r_scr, r_args = _ride_specs(ride)

    def body(*refs):
        t, c = pl.program_id(0), pl.program_id(1)
        (x_ref, g_ref, wg_ref, wu_ref, wd_ref, o_ref, a_ref, b_ref, h_ref, acc_ref), finish = _riding(
            ride, refs, 5, 3, (t == 0) & (c == 0), (t == S // tm - 1) & (c == N_CHIPS - 1))

        @pl.when(c == 0)
        def _():
            h_ref[...] = rms(x_ref[...], g_ref[...]).astype(BF16)
            acc_ref[...] = jnp.zeros_like(acc_ref)

        h = h_ref[...]
        a = _bdot(h, wg_ref[0], ((1,), (0,)))
        b = _bdot(h, wu_ref[0], ((1,), (0,)))
        a_ref[0] = a.astype(BF16)
        b_ref[0] = b.astype(BF16)
        y = a * jax.nn.sigmoid(a) * b
        acc_ref[...] += _bdot(y, wd_ref[0], ((1,), (0,)))

        @pl.when(c == N_CHIPS - 1)
        def _():
            o_ref[...] = x_ref[...] + 0.5 * acc_ref[...]

        finish()

    hid = pl.BlockSpec((1, tm, FF_SHARD), lambda t, c: (c, t, 0))
    return pl.pallas_call(
        body, name=f"ffn_fwd_{l}{j}", grid=(S // tm, N_CHIPS),
        in_specs=[pl.BlockSpec((tm, D), lambda t, c: (t, 0)), pl.BlockSpec((1, D), lambda t, c: (0, 0)),
                  _ffn_wspec(D, FF_SHARD, False), _ffn_wspec(D, FF_SHARD, False), _ffn_wspec(FF_SHARD, D, False)] + r_in,
        out_specs=[pl.BlockSpec((tm, D), lambda t, c: (t, 0)), hid, hid] + r_out,
        out_shape=[jax.ShapeDtypeStruct((S, D), F32)] + [jax.ShapeDtypeStruct((N_CHIPS, S, FF_SHARD), BF16)] * 2 + r_shape,
        scratch_shapes=[pltpu.VMEM((tm, D), BF16), pltpu.VMEM((tm, D), F32)] + r_scr,
        compiler_params=_cp(("arbitrary", "arbitrary")),
    )(x, g, wg, wu, wd, *r_args)


def ffn_bwd(x, g, wg, wu, wd, dout, a_sav, b_sav, l, j, tm=512):
    S = x.shape[0]

    def body(x_ref, g_ref, wg_ref, wu_ref, wd_ref, do_ref, a_ref, b_ref, dh_ref, dwg_ref, dwu_ref, dwd_ref):
        t = pl.program_id(1)
        h = rms(x_ref[...], g_ref[...]).astype(BF16)
        wgv, wuv, wdv = wg_ref[0], wu_ref[0], wd_ref[0]
        a = a_ref[0].astype(F32)
        b = b_ref[0].astype(F32)
        sig = jax.nn.sigmoid(a)
        s = a * sig
        dyd = 0.5 * do_ref[...]
        dy = _bdot(dyd, wdv, ((1,), (1,)))
        dwd = _bdot(s * b, dyd, ((0,), (0,)))
        db = dy * s
        da = dy * b * (sig * (1.0 + a * (1.0 - sig)))
        dwg = _bdot(da, h, ((0,), (0,)))
        dwu = _bdot(db, h, ((0,), (0,)))
        dh_ref[0] = (_bdot(da, wgv, ((1,), (1,))) + _bdot(db, wuv, ((1,), (1,)))).astype(dh_ref.dtype)

        @pl.when(t == 0)
        def _():
            dwg_ref[0] = dwg
            dwu_ref[0] = dwu
            dwd_ref[0] = dwd

        @pl.when(t != 0)
        def _():
            dwg_ref[0] += dwg
            dwu_ref[0] += dwu
            dwd_ref[0] += dwd

    return pl.pallas_call(
        body, name=f"ffn_bwd_{l}{j}", grid=(N_CHIPS, S // tm),
        in_specs=[pl.BlockSpec((tm, D), lambda c, t: (t, 0)), pl.BlockSpec((1, D), lambda c, t: (0, 0)),
                  _ffn_wspec(D, FF_SHARD, True), _ffn_wspec(D, FF_SHARD, True), _ffn_wspec(FF_SHARD, D, True),
                  pl.BlockSpec((tm, D), lambda c, t: (t, 0)),
                  pl.BlockSpec((1, tm, FF_SHARD), lambda c, t: (c, t, 0)), pl.BlockSpec((1, tm, FF_SHARD), lambda c, t: (c, t, 0))],
        out_specs=[pl.BlockSpec((1, tm, D), lambda c, t: (c, t, 0))] + [_ffn_wspec(FF_SHARD, D, True)] * 3,
        out_shape=[jax.ShapeDtypeStruct((N_CHIPS, S, D), BF16)] + [jax.ShapeDtypeStruct(wd.shape, F32)] * 3,
        compiler_params=_cp(("parallel", "arbitrary")),
    )(x, g, wg, wu, wd, dout, a_sav, b_sav)


def norm_bwd(name, x, g, dh_parts, dres, tm=512):
    S = x.shape[0]
    P = dh_parts.shape[0]

    def body(x_ref, g_ref, dh_ref, dr_ref, dx_ref, dg_ref):
        dh = dh_ref[0].astype(F32)
        for p in range(1, P):
            dh = dh + dh_ref[p].astype(F32)
        _, vjp = jax.vjp(rms, x_ref[...], g_ref[...])
        dx, dg = vjp(dh)
        dx_ref[...] = dr_ref[...] + dx

        @pl.when(pl.program_id(0) == 0)
        def _():
            dg_ref[...] = dg

        @pl.when(pl.program_id(0) != 0)
        def _():
            dg_ref[...] += dg

    return pl.pallas_call(
        body, name=name, grid=(S // tm,),
        in_specs=[pl.BlockSpec((tm, D), lambda t: (t, 0)), pl.BlockSpec((1, D), lambda t: (0, 0)),
                  pl.BlockSpec((P, tm, D), lambda t: (0, t, 0)), pl.BlockSpec((tm, D), lambda t: (t, 0))],
        out_specs=[pl.BlockSpec((tm, D), lambda t: (t, 0)), pl.BlockSpec((1, D), lambda t: (0, 0))],
        out_shape=[jax.ShapeDtypeStruct((S, D), F32), jax.ShapeDtypeStruct((1, D), F32)],
        compiler_params=_cp(("arbitrary",)),
    )(x, g, dh_parts, dres)


def f_attn_sb(x, g, w):
    pr = mm(rms(x, g), w)
    return pr[:, :SB_W], pr[:, SB_W:2 * SB_W], pr[:, 2 * SB_W:]


def _pairs(y):
    return jnp.stack([y[:, 128 * j:128 * (j + 1)] for j in range(DL_PAIRS)])


def f_attn_qk(x, g, w, nrm):
    pr = mm(rms(x, g), w)
    ms = group_sum(pr * pr, DL_HEADS) * (1.0 / HEAD)
    return _pairs(pr * lax.rsqrt(ms + NORM_EPS) * jnp.concatenate([nrm] * DL_HEADS, axis=1))


def f_attn_v(x, g, w):
    return _pairs(mm(rms(x, g), w))


def _masked(strict, x):
    return x if strict is None else jnp.where(strict, x, 0.0)


def _head_stack(x, dtype=BF16):
    nh = x.shape[1] // HEAD
    lane_head = lax.broadcasted_iota(jnp.int32, (1, x.shape[1]), 1) // HEAD
    return jnp.concatenate([jnp.where(lane_head == h, x, 0.0) for h in range(nh)], axis=0).astype(dtype)


def _head_pick(xs):
    nh = xs.shape[1] // HEAD
    rows = xs.shape[0] // nh
    lane_head = lax.broadcasted_iota(jnp.int32, (1, xs.shape[1]), 1) // HEAD
    out = xs[:rows]
    for h in range(1, nh):
        out = jnp.where(lane_head == h, xs[rows * h:rows * (h + 1)], out)
    return out


def _sb_tiles(qs, kblk, strict):
    z = _dg(qs, kblk, ((1,), (1,))) * (HEAD ** -0.5)
    keep = -(jnp.maximum(z, 0.0) + jnp.log(1.0 + jnp.exp(-jnp.abs(z))))
    return z, _masked(strict, keep)


def _tri(n, upper):
    r = lax.broadcasted_iota(jnp.int32, (n, n), 0)
    c = lax.broadcasted_iota(jnp.int32, (n, n), 1)
    return ((r > c) if upper else (r < c)).astype(BF16)


def _tri_sums(x, tri):
    hi, lo = _split2(x)
    return _dg(jnp.concatenate([hi, lo], axis=1), jnp.concatenate([tri, tri], axis=0), ((1,), (0,)))


SB_UNROLL = 8


def _sb_diag(tb, nh):
    r = lax.broadcasted_iota(jnp.int32, (nh * tb, tb), 0)
    return lax.broadcasted_iota(jnp.int32, (nh * tb, tb), 1) < lax.rem(r, tb)


def _sb_sweep(step, first, count, carry, direction, commit=None):
    def run(kbs, c):
        outs = []
        for kb in kbs:
            c, out = step(kb, c)
            outs.append(out)
        if commit is not None:
            for kb, out in zip(kbs, outs):
                commit(kb, out)
        return c

    pos, size = first, 1
    while size < SB_UNROLL:
        n = (count // size) % 2
        carry = lax.fori_loop(
            0, n, lambda i, c, pos=pos, size=size: run([pos + direction * u for u in range(size)], c), carry)
        pos, size = pos + direction * size * n, 2 * size
    return lax.fori_loop(
        0, count // SB_UNROLL,
        lambda g, c: run([pos + direction * (SB_UNROLL * g + u) for u in range(SB_UNROLL)], c), carry)


def _riding(ride, refs, n_in, n_out, first, last):
    if ride is None:
        return refs, lambda: None
    n = ride.n
    own = refs[:n_in] + refs[n_in + n:n_in + n + n_out] + refs[n_in + 2 * n + n_out:len(refs) - 2]
    start, wait = ride.ops(refs[n_in:n_in + n], refs[n_in + n + n_out:n_in + 2 * n + n_out], refs[-2], refs[-1])
    pl.when(first)(start)
    return own, lambda: pl.when(last)(wait)


def _ride_specs(ride):
    if ride is None:
        return [], [], [], [], []
    return [_HBM] * ride.n, [_HBM] * ride.n, ride.out_shapes, ride.sem_shapes(), ride.arrays


def sb_fwd(q, k, v, ride=None, tb=QBLK):
    S = q.shape[0]
    nh = SB_W // HEAD
    nb = S // tb
    r_in, r_out, r_shape, r_scr, r_args = _ride_specs(ride)

    def body(*refs):
        qb = pl.program_id(0)
        (q_ref, k_ref, v_ref, o_ref, w_ref), finish = _riding(ride, refs, 3, 2, qb == 0, qb == nb - 1)
        diag = _sb_diag(tb, nh)
        after_mat = _tri(tb, True)
        qs = _head_stack(q_ref[...])

        def step(kb, carry, strict):
            acc, run = carry
            rows = pl.ds(pl.multiple_of(kb * tb, tb), tb)
            z, keep = _sb_tiles(qs, k_ref[rows, :].astype(BF16), strict)
            w = _masked(strict, jnp.exp(z + keep + _tri_sums(keep, after_mat) + run)).astype(BF16)
            w_ref[0, kb] = w
            acc = acc + _dg(w, v_ref[rows, :].astype(BF16), ((1,), (0,)))
            return acc, run + jnp.sum(keep, axis=1, keepdims=True)

        init = (jnp.zeros((nh * tb, SB_W), F32), jnp.zeros((nh * tb, 1), F32))
        carry = step(qb, init, diag)
        acc, _ = _sb_sweep(lambda kb, c: (step(kb, c, None), None), qb - 1, qb, carry, -1)
        o_ref[...] = _head_pick(acc)
        finish()

    return pl.pallas_call(
        body, name="sb_fwd", grid=(S // tb,),
        in_specs=[pl.BlockSpec((tb, SB_W), lambda i: (i, 0)), pl.BlockSpec((S, SB_W), lambda i: (0, 0)),
                  pl.BlockSpec((S, SB_W), lambda i: (0, 0))] + r_in,
        out_specs=[pl.BlockSpec((tb, SB_W), lambda i: (i, 0)),
                   pl.BlockSpec((1, nb, nh * tb, tb), lambda i: (i, 0, 0, 0))] + r_out,
        out_shape=[jax.ShapeDtypeStruct((S, SB_W), F32), jax.ShapeDtypeStruct((nb, nb, nh * tb, tb), BF16)] + r_shape,
        scratch_shapes=r_scr,
        compiler_params=_cp(("arbitrary",)),
    )(q, k, v, *r_args)


def sb_bwd(q, k, v, do, wts, ride=None, tb=QBLK):
    S = q.shape[0]
    nh = SB_W // HEAD
    nb = S // tb
    scale = HEAD ** -0.5
    r_in, r_out, r_shape, r_scr, r_args = _ride_specs(ride)

    def body(*refs):
        qb = pl.program_id(0)
        (q_ref, k_ref, v_ref, do_ref, w_ref, dq_ref, dk_ref, dv_ref, g_scr), finish = _riding(
            ride, refs, 5, 3, qb == 0, qb == nb - 1)

        @pl.when(qb == 0)
        def _():
            dk_ref[...] = jnp.zeros_like(dk_ref)
            dv_ref[...] = jnp.zeros_like(dv_ref)

        diag = _sb_diag(tb, nh)
        before_mat = _tri(tb, False)
        qs = _head_stack(q_ref[...])
        dos = _head_stack(do_ref[...])

        def weights_pass(kb, carry):
            rows = pl.ds(pl.multiple_of(kb * tb, tb), tb)
            w = w_ref[0, kb]
            g_scr[kb] = _dg(dos, v_ref[rows, :].astype(BF16), ((1,), (1,))) * w.astype(F32)
            return carry, _dg(w, dos, ((0,), (0,)))

        def add_rows(ref):
            def commit(kb, val):
                ref[pl.ds(pl.multiple_of(kb * tb, tb), tb), :] += val
            return commit

        zero_run = jnp.zeros((nh * tb, 1), F32)
        _sb_sweep(weights_pass, 0, qb + 1, 0, 1, add_rows(dv_ref))

        def left_to_right(kb, carry, strict):
            dq, run = carry
            rows = pl.ds(pl.multiple_of(kb * tb, tb), tb)
            kblk = k_ref[rows, :].astype(BF16)
            gw = g_scr[kb]
            sig = jax.nn.sigmoid(_dg(qs, kblk, ((1,), (1,))) * scale)
            dkeep = _masked(strict, _tri_sums(gw, before_mat) + run)
            dz = ((gw * (1.0 - sig) - dkeep * sig) * scale).astype(BF16)
            dq = dq + _dg(dz, kblk, ((1,), (0,)))
            return (dq, run + jnp.sum(gw, axis=1, keepdims=True)), _dg(dz, qs, ((0,), (0,)))

        carry = _sb_sweep(lambda kb, c: left_to_right(kb, c, None), 0, qb,
                          (jnp.zeros((nh * tb, SB_W), F32), zero_run), 1, add_rows(dk_ref))
        (dq, _), dk_diag = left_to_right(qb, carry, diag)
        add_rows(dk_ref)(qb, dk_diag)
        dq_ref[...] = _head_pick(dq)
        finish()

    whole = pl.BlockSpec((S, SB_W), lambda i: (0, 0))
    blk = pl.BlockSpec((tb, SB_W), lambda i: (i, 0))
    return pl.pallas_call(
        body, name="sb_bwd", grid=(S // tb,),
        in_specs=[blk, whole, whole, blk, pl.BlockSpec((1, nb, nh * tb, tb), lambda i: (i, 0, 0, 0))] + r_in,
        out_specs=[blk, whole, whole] + r_out,
        out_shape=[jax.ShapeDtypeStruct((S, SB_W), F32)] * 3 + r_shape,
        scratch_shapes=[pltpu.VMEM((S // tb, nh * tb, tb), F32)] + r_scr,
        compiler_params=_cp(("arbitrary",)),
    )(q, k, v, do, wts, *r_args)


def reorder(name, x, groups, inverse):
    P, S, _ = x.shape

    def body(x_ref, o_ref):
        p = pl.program_id(0)
        for gi, r in enumerate(groups):
            @pl.when(p // 2 == gi)
            def _(r=r):
                L = S // r
                if r == 1:
                    o_ref[...] = x_ref[...]
                for c in range(r if r > 1 else 0):
                    if inverse:
                        o_ref[pl.ds(c, L, stride=r), :] = x_ref[c * L:(c + 1) * L, :]
                    else:
                        o_ref[c * L:(c + 1) * L, :] = x_ref[pl.ds(c, L, stride=r), :]

    slab = pl.BlockSpec((None, S, 128), lambda p: (p, 0, 0))
    return pl.pallas_call(
        body, name=name, grid=(P,), in_specs=[slab], out_specs=slab,
        out_shape=jax.ShapeDtypeStruct(x.shape, x.dtype), compiler_params=_cp(("parallel",)),
    )(x)


def _dil_blocks(S):
    return S // QBLK


def _dil_mask4(n_in_stream):
    qi = lax.rem(lax.broadcasted_iota(jnp.int32, (4 * QBLK, 2 * QBLK), 0), QBLK)
    kj = lax.broadcasted_iota(jnp.int32, (4 * QBLK, 2 * QBLK), 1) - QBLK
    dist = qi - kj
    return (dist >= 0) & (dist <= QBLK) & ((n_in_stream > 0) | (kj >= 0))


def _dil_lanes(ref):
    return jnp.concatenate([ref[0], ref[1]], axis=1)


def _dil_window(prev_ref, cur_ref):
    return jnp.concatenate([_dil_lanes(prev_ref), _dil_lanes(cur_ref)], axis=0).astype(BF16)


def _stream_pos(gi, i, S):
    nb = jnp.where(gi == 0, S // (QBLK * DIL[0]), jnp.where(gi == 1, S // (QBLK * DIL[1]), S // (QBLK * DIL[2])))
    return i % nb


def dil_fwd(q, k, v, bias, ride=None):
    S = q.shape[1]
    nblk = _dil_blocks(S)
    r_in, r_out, r_shape, r_scr, r_args = _ride_specs(ride)

    def body(*refs):
        gi, i = pl.program_id(0), pl.program_id(1)
        (q_ref, kc_ref, kp_ref, vc_ref, vp_ref, b_ref, o_ref, l_ref), finish = _riding(
            ride, refs, 6, 2, (gi == 0) & (i == 0), (gi == len(DIL) - 1) & (i == nblk - 1))
        mask = _dil_mask4(_stream_pos(gi, i, S))
        kw, vw = _dil_window(kp_ref, kc_ref), _dil_window(vp_ref, vc_ref)
        lg = _dg(_head_stack(_dil_lanes(q_ref)), kw, ((1,), (1,))) * (HEAD ** -0.5) + \
            b_ref[...].reshape(4 * QBLK, 2 * QBLK)
        lg = jnp.where(mask, lg, NEG_INF)
        m = jnp.max(lg, axis=-1, keepdims=True)
        p = jnp.exp(lg - m)
        den = jnp.sum(p, axis=-1, keepdims=True)
        o = _head_pick(_dg((p / den).astype(BF16), vw, ((1,), (0,))))
        lse = _head_pick(jnp.broadcast_to(m + jnp.log(den), (4 * QBLK, 4 * HEAD)))
        for j in range(2):
            o_ref[j] = o[:, 128 * j:128 * (j + 1)]
            l_ref[j] = lse[:, 128 * j:128 * (j + 1)]
        finish()

    cur = pl.BlockSpec((2, QBLK, 128), lambda g, i: (g, i, 0))
    prev = pl.BlockSpec((2, QBLK, 128), lambda g, i: (g, jnp.maximum(i - 1, 0), 0))
    return pl.pallas_call(
        body, name="dil_fwd", grid=(len(DIL), nblk),
        in_specs=[cur, cur, prev, cur, prev, pl.BlockSpec((4, QBLK, 2 * QBLK), lambda g, i: (g, 0, 0))] + r_in,
        out_specs=[cur, cur] + r_out,
        out_shape=[jax.ShapeDtypeStruct(q.shape, F32)] * 2 + r_shape,
        scratch_shapes=r_scr,
        compiler_params=_cp(("arbitrary", "arbitrary")),
    )(q, k, k, v, v, bias, *r_args)


def dil_bwd(q, k, v, bias, o, lse, do, dlse, ride=None):
    S = q.shape[1]
    nblk = _dil_blocks(S)
    r_in, r_out, r_shape, r_scr, r_args = _ride_specs(ride)

    def body(*refs):
        gi, i = pl.program_id(0), pl.program_id(1)
        (q_ref, kc_ref, kp_ref, vc_ref, vp_ref, b_ref, o_ref, l_ref, do_ref, dl_ref,
         dq_ref, dk_ref, dv_ref, ds_ref, dk_car, dv_car), finish = _riding(
            ride, refs, 10, 4, (gi == 0) & (i == 0), (gi == len(DIL) - 1) & (i == nblk))

        @pl.when(i == 0)
        def _():
            ds_ref[...] = jnp.zeros_like(ds_ref)
            dk_car[...] = jnp.zeros_like(dk_car)
            dv_car[...] = jnp.zeros_like(dv_car)

        @pl.when(i < nblk)
        def _():
            mask = _dil_mask4(_stream_pos(gi, i, S))
            kw, vw = _dil_window(kp_ref, kc_ref), _dil_window(vp_ref, vc_ref)
            qs = _head_stack(_dil_lanes(q_ref))
            do_nat = _dil_lanes(do_ref)
            dos = _head_stack(do_nat, F32)
            lse = jnp.sum(_head_stack(_dil_lanes(l_ref), F32), axis=-1, keepdims=True) * (1.0 / HEAD)
            lg = _dg(qs, kw, ((1,), (1,))) * (HEAD ** -0.5) + b_ref[...].reshape(4 * QBLK, 2 * QBLK)
            p = jnp.where(mask, jnp.exp(lg - lse), 0.0)
            dp = _dg(dos.astype(BF16), vw, ((1,), (1,)))
            four = lambda t: jnp.concatenate([t] * 4, axis=0)
            delta = jnp.sum(dos * four(_dil_lanes(o_ref)), axis=-1, keepdims=True)
            dl = jnp.sum(_head_stack(_dil_lanes(dl_ref), F32), axis=-1, keepdims=True)
            ds = p * (dp - delta + dl)
            ds_ref[...] += ds.reshape(4, QBLK, 2 * QBLK)
            dsq = (ds * (HEAD ** -0.5)).astype(BF16)
            dq = _head_pick(_dg(dsq, kw, ((1,), (0,))))
            dkw = _dg(dsq, qs, ((0,), (0,)))
            dvw = _dg(p.astype(BF16), dos.astype(BF16), ((0,), (0,)))
            for j in range(2):
                lanes = slice(128 * j, 128 * (j + 1))
                dq_ref[j] = dq[:, lanes]
                dk_ref[j] = dk_car[j] + dkw[:QBLK, lanes]
                dv_ref[j] = dv_car[j] + dvw[:QBLK, lanes]
                dk_car[j] = dkw[QBLK:, lanes]
                dv_car[j] = dvw[QBLK:, lanes]

        @pl.when(i == nblk)
        def _():
            dk_ref[...] = dk_car[...]
            dv_ref[...] = dv_car[...]

        finish()

    cur = pl.BlockSpec((2, QBLK, 128), lambda g, i: (g, jnp.minimum(i, nblk - 1), 0))
    prev = pl.BlockSpec((2, QBLK, 128), lambda g, i: (g, jnp.clip(i - 1, 0, nblk - 1), 0))
    bspec = pl.BlockSpec((4, QBLK, 2 * QBLK), lambda g, i: (g, 0, 0))
    return pl.pallas_call(
        body, name="dil_bwd", grid=(len(DIL), nblk + 1),
        in_specs=[cur, cur, prev, cur, prev, bspec, cur, cur, cur, cur] + r_in,
        out_specs=[cur, prev, prev, bspec] + r_out,
        out_shape=[jax.ShapeDtypeStruct(q.shape, F32)] * 3 + [jax.ShapeDtypeStruct(bias.shape, F32)] + r_shape,
        scratch_shapes=[pltpu.VMEM((2, QBLK, 128), F32), pltpu.VMEM((2, QBLK, 128), F32)] + r_scr,
        compiler_params=_cp(("arbitrary", "arbitrary")),
    )(q, k, k, v, v, bias, o, lse, do, dlse, *r_args)


def _t5_bucket(dist):
    max_exact = N_BUCKETS // 2
    d = jnp.maximum(dist, 1).astype(F32)
    large = max_exact + (jnp.log(d / max_exact) / math.log(MAX_DISTANCE / max_exact)
                         * (N_BUCKETS - max_exact)).astype(jnp.int32)
    large = jnp.minimum(large, N_BUCKETS - 1)
    return jnp.where(dist < max_exact, dist, large)


def _bucket_maps():
    qi = jnp.arange(QBLK)[:, None]
    kj = jnp.arange(2 * QBLK)[None, :] - QBLK
    dist = jnp.maximum(qi - kj, 0)
    return jnp.stack([_t5_bucket(dist * r) for r in DIL])


def bias_table(rel_bias, buckets):
    def body(tbl_ref, bk_ref, o_ref):
        for h in range(DL_HEADS):
            bk = bk_ref[h // 4]

            def step(b, acc):
                return jnp.where(bk == b, tbl_ref[b, h], acc)

            o_ref[h] = lax.fori_loop(0, N_BUCKETS, step, jnp.zeros(bk.shape, F32))

    return pl.pallas_call(
        body, name="bias_table", out_shape=jax.ShapeDtypeStruct((DL_HEADS,) + buckets.shape[1:], F32),
        in_specs=[pl.BlockSpec(memory_space=pltpu.SMEM), pl.BlockSpec(memory_space=pltpu.VMEM)],
        out_specs=pl.BlockSpec(memory_space=pltpu.VMEM),
    )(rel_bias, buckets)


def bias_grad(ds, buckets):
    def body(ds_ref, bk_ref, o_ref):
        lane = lax.broadcasted_iota(jnp.int32, (1, 128), 1)
        for h in range(DL_HEADS):
            dsv = ds_ref[h]
            bk = bk_ref[h // 4]

            def step(b, row):
                return jnp.where(lane == b, jnp.sum(jnp.where(bk == b, dsv, 0.0)), row)

            o_ref[h:h + 1, :] = lax.fori_loop(0, N_BUCKETS, step, jnp.zeros((1, 128), F32))

    return pl.pallas_call(
        body, name="bias_grad", out_shape=jax.ShapeDtypeStruct((DL_HEADS, 128), F32),
        in_specs=[pl.BlockSpec(memory_space=pltpu.VMEM)] * 2, out_specs=pl.BlockSpec(memory_space=pltpu.VMEM),
    )(ds, buckets)


def f_attn_out(x, oa, o, lse, w):
    og = [jnp.concatenate([o[2 * g], o[2 * g + 1]], axis=1) for g in range(3)]
    lg = [jnp.concatenate([lse[2 * g], lse[2 * g + 1]], axis=1) for g in range(3)]
    m = jnp.maximum(jnp.maximum(lg[0], lg[1]), lg[2])
    e = [jnp.exp(l - m) for l in lg]
    den = e[0] + e[1] + e[2]
    ob = (e[0] * og[0] + e[1] * og[1] + e[2] * og[2]) / den
    return x + mm(jnp.concatenate([oa, ob], axis=1), w)


def norm_shift_fwd(x, g, tm=256):
    S = x.shape[0]

    def body(x_ref, xp_ref, g_ref, h_ref, hs_ref):
        h = rms(x_ref[...], g_ref[...])
        hp = rms(xp_ref[7:8, :], g_ref[...])
        hp = jnp.where(pl.program_id(0) == 0, 0.0, hp)
        row = lax.broadcasted_iota(jnp.int32, (tm, D), 0)
        h_ref[...] = h
        hs_ref[...] = jnp.where(row == 0, hp, pltpu.roll(h, 1, 0))

    return pl.pallas_call(
        body, name="rw_norm_shift", grid=(S // tm,),
        in_specs=[pl.BlockSpec((tm, D), lambda t: (t, 0)),
                  pl.BlockSpec((8, D), lambda t: (jnp.maximum(t * (tm // 8) - 1, 0), 0)),
                  pl.BlockSpec((1, D), lambda t: (0, 0))],
        out_specs=[pl.BlockSpec((tm, D), lambda t: (t, 0))] * 2,
        out_shape=[jax.ShapeDtypeStruct((S, D), F32)] * 2,
        compiler_params=_cp(("parallel",)),
    )(x, x, g)


def norm_shift_bwd(x, g, dh, dhs, dres, tm=256):
    S = x.shape[0]
    nt = S // tm

    def body(x_ref, g_ref, dh_ref, dhs_ref, dhn_ref, dr_ref, dx_ref, dg_ref):
        t = pl.program_id(0)
        nxt = jnp.where(t == nt - 1, 0.0, dhn_ref[0:1, :])
        row = lax.broadcasted_iota(jnp.int32, (tm, D), 0)
        tot = dh_ref[...] + jnp.where(row == tm - 1, nxt, pltpu.roll(dhs_ref[...], tm - 1, 0))
        _, vjp = jax.vjp(rms, x_ref[...], g_ref[...])
        dx, dg = vjp(tot)
        dx_ref[...] = dr_ref[...] + dx

        @pl.when(t == 0)
        def _():
            dg_ref[...] = dg

        @pl.when(t != 0)
        def _():
            dg_ref[...] += dg

    tile = pl.BlockSpec((tm, D), lambda t: (t, 0))
    return pl.pallas_call(
        body, name="rw_norm_shift_bwd", grid=(nt,),
        in_specs=[tile, pl.BlockSpec((1, D), lambda t: (0, 0)), tile, tile,
                  pl.BlockSpec((8, D), lambda t: (jnp.minimum((t + 1) * (tm // 8), S // 8 - 1), 0)), tile],
        out_specs=[tile, pl.BlockSpec((1, D), lambda t: (0, 0))],
        out_shape=[jax.ShapeDtypeStruct((S, D), F32), jax.ShapeDtypeStruct((1, D), F32)],
        compiler_params=_cp(("arbitrary",)),
    )(x, g, dh, dhs, dhs, dres)


def f_rw_proj(h, hs, mix, w):
    return mm(h + (hs - h) * mix, w)


def f_rw_mid(h, hs, r, k, v, mix3, w0, a0, kkw, kaw, w1, w2, a1, a2, g1, g2):
    xx = hs - h
    xw, xa, xg = h + xx * mix3[0:1], h + xx * mix3[1:2], h + xx * mix3[2:3]
    w_log = -softplus(-(w0 + mm(jnp.tanh(mm(xw, w1)), w2))) - 0.5
    lw = -jnp.exp(w_log)
    ag = jax.nn.sigmoid(a0 + mm(mm(xa, a1), a2))
    gate = mm(jax.nn.sigmoid(mm(xg, g1)), g2)
    kk = k * kkw
    kk = kk / jnp.maximum(jnp.sqrt(group_sum(kk * kk, RW_H)), 1e-12)
    kmod = k * (1.0 + (ag - 1.0) * kaw)
    return (to_heads(r), to_heads(lw), to_heads(kmod), to_heads(v), to_heads(-kk), to_heads(kk * ag), gate)


def f_rw_post(yh, rh, kh, vh, gate, x, lng, lnb, rk, wo):
    mu = jnp.mean(yh, axis=-1, keepdims=True)
    var = jnp.mean(jnp.square(yh - mu), axis=-1, keepdims=True)
    yn = (yh - mu) * lax.rsqrt(var + GN_EPS)
    bonus = jnp.sum(rh * kh * rk, axis=-1, keepdims=True) * vh
    y = from_heads(yn) * lng + lnb + from_heads(bonus)
    return x + mm(y * gate, wo)


def _split2(x):
    hi = x.astype(BF16)
    return hi, (x - hi.astype(F32)).astype(BF16)


def _b3(x, y, cx, cy):
    xh, xl = _split2(x)
    yh, yl = _split2(y)
    x3 = jnp.concatenate([xh, xh, xl], axis=cx)
    y3 = jnp.concatenate([yh, yl, yh], axis=cy)
    return lax.dot_general(x3, y3, (((cx,), (cy,)), ((0,), (0,))), preferred_element_type=F32)


@jax.custom_vjp
def b_nt(x, y):
    return _b3(x, y, 2, 2)


@jax.custom_vjp
def b_nn(x, y):
    return _b3(x, y, 2, 1)


@jax.custom_vjp
def b_tn(x, y):
    return _b3(x, y, 1, 1)


def _b1(x, y, cx, cy):
    return lax.dot_general(x.astype(BF16), y.astype(BF16), (((cx,), (cy,)), ((0,), (0,))), preferred_element_type=F32)


b_nt.defvjp(lambda x, y: (b_nt(x, y), (x, y)), lambda r, g: (_b1(g, r[1], 2, 1), _b1(g, r[0], 1, 1)))
b_nn.defvjp(lambda x, y: (b_nn(x, y), (x, y)), lambda r, g: (_b1(g, r[1], 2, 2), _b1(r[0], g, 1, 1)))
b_tn.defvjp(lambda x, y: (b_tn(x, y), (x, y)), lambda r, g: (_b1(r[1], g, 2, 2), _b1(r[0], g, 2, 1)))


def _tri_apply(x, lower):
    H, C, _ = x.shape
    ii = lax.broadcasted_iota(jnp.int32, (C, C), 0)
    jj = lax.broadcasted_iota(jnp.int32, (C, C), 1)
    m = jnp.broadcast_to(((jj <= ii) if lower else (jj >= ii)).astype(BF16), (H, C, C))
    x1 = x.astype(BF16)
    r1 = x - x1.astype(F32)
    x2 = r1.astype(BF16)
    x3 = (r1 - x2.astype(F32)).astype(BF16)
    return lax.dot_general(jnp.concatenate([m, m, m], axis=2), jnp.concatenate([x1, x2, x3], axis=1),
                           (((2,), (1,)), ((0,), (0,))), preferred_element_type=F32)


@jax.custom_vjp
def run_sum(x):
    return _tri_apply(x, True)


run_sum.defvjp(lambda x: (run_sum(x), None), lambda _, g: (_tri_apply(g, False),))


def rwkv_chunk(S0, r, lw, k, v, a, b):
    H, C, _ = r.shape
    V = S0.shape[1]
    ii = lax.broadcasted_iota(jnp.int32, (C, C), 0)
    jj = lax.broadcasted_iota(jnp.int32, (C, C), 1)
    strict = jj < ii
    i2 = lax.broadcasted_iota(jnp.int32, (C, 2 * C), 0)
    j2 = lax.broadcasted_iota(jnp.int32, (C, 2 * C), 1)
    incl2 = jnp.where(j2 >= C, j2 - C, j2) <= i2
    g = run_sum(lw)
    ig = jnp.exp(-g)
    ar = jnp.concatenate([a * jnp.exp(g - lw), r * jnp.exp(g)], axis=1)
    bk = jnp.concatenate([b * ig, k * ig], axis=1)
    m = b_nt(ar, bk)
    a_ab = jnp.where(strict, m[:, :C, :C], 0.0)
    a_ak = jnp.where(strict, m[:, :C, C:], 0.0)
    b_r = jnp.where(incl2, m[:, C:, :], 0.0)
    p = b_nt(ar, S0)
    u = p[:, :C] + b_nn(a_ak, v)
    nmat, n = a_ab, 1
    while n < C:
        n *= 2
        if n < C:
            z = b_nn(nmat, jnp.concatenate([u, nmat], axis=2))
            u, nmat = u + z[:, :, :V], z[:, :, V:]
        else:
            u = u + b_nn(nmat, u)
    uv = jnp.concatenate([u, v], axis=1)
    y = p[:, C:] + b_nn(b_r, uv)
    g_end = g[:, C - 1:C, :]
    dec = jnp.exp(g_end - g)
    s_new = S0 * jnp.exp(g_end) + b_tn(uv, jnp.concatenate([b * dec, k * dec], axis=1))
    return y, s_new


def rwkv_fwd(r, lw, k, v, a, b):
    H, S, _ = r.shape
    C = RW_CHUNK

    def body(r_ref, lw_ref, k_ref, v_ref, a_ref, b_ref, y_ref, s_ref, s_scr):
        @pl.when(pl.program_id(0) == 0)
        def _():
            s_scr[...] = jnp.zeros_like(s_scr)

        s0 = s_scr[...]
        s_ref[0] = s0
        y, s1 = rwkv_chunk(s0, r_ref[...], lw_ref[...], k_ref[...], v_ref[...], a_ref[...], b_ref[...])
        y_ref[...] = y
        s_scr[...] = s1

    bs = pl.BlockSpec((H, C, HEAD), lambda c: (0, c, 0))
    return pl.pallas_call(
        body, name="rwkv_fwd", grid=(S // C,), in_specs=[bs] * 6,
        out_specs=[bs, pl.BlockSpec((1, H, HEAD, HEAD), lambda c: (c, 0, 0, 0))],
        out_shape=[jax.ShapeDtypeStruct((H, S, HEAD), F32), jax.ShapeDtypeStruct((S // C, H, HEAD, HEAD), F32)],
        scratch_shapes=[pltpu.VMEM((H, HEAD, HEAD), F32)],
        compiler_params=_cp(("arbitrary",)),
    )(r, lw, k, v, a, b)


def rwkv_bwd(r, lw, k, v, a, b, states, dy):
    H, S, _ = r.shape
    C = RW_CHUNK
    nc = S // C

    def body(r_ref, lw_ref, k_ref, v_ref, a_ref, b_ref, s_ref, dy_ref, dr, dlw, dk, dv, da, db, ds_scr):
        @pl.when(pl.program_id(0) == 0)
        def _():
            ds_scr[...] = jnp.zeros_like(ds_scr)

        _, vjp = jax.vjp(rwkv_chunk, s_ref[0], r_ref[...], lw_ref[...], k_ref[...], v_ref[...], a_ref[...], b_ref[...])
        grads = vjp((dy_ref[...], ds_scr[...]))
        ds_scr[...] = grads[0]
        for o, gv in zip((dr, dlw, dk, dv, da, db), grads[1:]):
            o[...] = gv

    bs = pl.BlockSpec((H, C, HEAD), lambda c: (0, nc - 1 - c, 0))
    return pl.pallas_call(
        body, name="rwkv_bwd", grid=(nc,),
        in_specs=[bs] * 6 + [pl.BlockSpec((1, H, HEAD, HEAD), lambda c: (nc - 1 - c, 0, 0, 0)), bs],
        out_specs=[bs] * 6, out_shape=[jax.ShapeDtypeStruct((H, S, HEAD), F32)] * 6,
        scratch_shapes=[pltpu.VMEM((H, HEAD, HEAD), F32)],
        compiler_params=_cp(("arbitrary",)),
    )(r, lw, k, v, a, b, states, dy)


def loss_head(y, target, tm=512):
    S = y.shape[0]

    def body(y_ref, t_ref, dy_ref, l_ref):
        e = y_ref[...] - t_ref[...]
        dy_ref[...] = e * (1.0 / D)
        part = jnp.broadcast_to(0.5 * jnp.sum(jnp.mean(e * e, axis=-1, keepdims=True)), (1, 128))

        @pl.when(pl.program_id(0) == 0)
        def _():
            l_ref[...] = part

        @pl.when(pl.program_id(0) != 0)
        def _():
            l_ref[...] += part

    tile = pl.BlockSpec((tm, D), lambda t: (t, 0))
    return pl.pallas_call(
        body, name="loss_head", grid=(S // tm,), in_specs=[tile, tile],
        out_specs=[tile, pl.BlockSpec((1, 128), lambda t: (0, 0))],
        out_shape=[jax.ShapeDtypeStruct((S, D), F32), jax.ShapeDtypeStruct((1, 128), F32)],
        compiler_params=_cp(("arbitrary",)),
    )(y, target)


def _row_tile(rows, cols, budget=1 << 19):
    best = None
    for tr in range(8, rows + 1, 8):
        if rows % tr == 0 and tr * cols <= budget:
            best = tr
    return best or rows


def _adam(w, g, m, v):
    m = ADAM_B1 * m + (1.0 - ADAM_B1) * g
    v = ADAM_B2 * v + (1.0 - ADAM_B2) * jnp.square(g)
    m_hat = m / (1.0 - ADAM_B1 ** ADAM_STEP)
    v_hat = v / (1.0 - ADAM_B2 ** ADAM_STEP)
    return -ADAM_LR * (m_hat / (jnp.sqrt(v_hat) + ADAM_EPS) + ADAM_WD * w), m, v


def sum_slots(name, parts, dtype=F32, extras=()):
    n = 0 if parts is None else parts.shape[0]
    R, C = extras[0].shape if parts is None else parts.shape[1:]
    tr = _row_tile(R, C * (n + len(extras)))
    ins = ([] if parts is None else [parts]) + list(extras)

    def body(*refs):
        terms = [] if parts is None else [refs[0][i] for i in range(n)]
        terms += [r[...] for r in refs[len(ins) - len(extras):len(ins)]]
        s = terms[0].astype(F32)
        for t in terms[1:]:
            s = s + t.astype(F32)
        refs[len(ins)][...] = s.astype(dtype)

    tile = pl.BlockSpec((tr, C), lambda t: (t, 0))
    return pl.pallas_call(
        body, name=name, grid=(R // tr,),
        in_specs=([] if parts is None else [pl.BlockSpec((n, tr, C), lambda t: (0, t, 0))]) + [tile] * len(extras),
        out_specs=tile, out_shape=jax.ShapeDtypeStruct((R, C), dtype), compiler_params=_cp(("parallel",)),
    )(*ins)


def sum_own_half(name, split, theirs, c, dtype):
    nq, _, rh, cols = split.shape
    tr = _row_tile(rh, 2 * cols)

    def body(c_ref, a_ref, b_ref, o_ref):
        o_ref[...] = (a_ref[...] + b_ref[...]).astype(dtype)

    tile = pl.BlockSpec((None, tr, cols), lambda q, t, c_ref: (q, t, 0))
    return pl.pallas_call(
        body, name=name,
        grid_spec=pltpu.PrefetchScalarGridSpec(
            num_scalar_prefetch=1, grid=(nq, rh // tr),
            in_specs=[pl.BlockSpec((None, None, tr, cols), lambda q, t, c_ref: (q, c_ref[0], t, 0)), tile],
            out_specs=tile),
        out_shape=jax.ShapeDtypeStruct((nq, rh, cols), dtype), compiler_params=_cp(("parallel", "parallel")),
    )(jnp.reshape(c, (1,)).astype(jnp.int32), split, theirs)


def sum_landed(name, landed, chip_sum, p):
    n, rh, cols = landed.shape
    tr = _row_tile(rh, (n + 1) * cols)

    def body(p_ref, l_ref, own_ref, o_ref):
        s = l_ref[0].astype(F32)
        for i in range(1, n):
            s = s + l_ref[i].astype(F32)
        o_ref[...] = s + own_ref[...].astype(F32)

    return pl.pallas_call(
        body, name=name,
        grid_spec=pltpu.PrefetchScalarGridSpec(
            num_scalar_prefetch=1, grid=(rh // tr,),
            in_specs=[pl.BlockSpec((n, tr, cols), lambda t, p_ref: (0, t, 0)),
                      pl.BlockSpec((None, tr, cols), lambda t, p_ref: (p_ref[0], t, 0))],
            out_specs=pl.BlockSpec((tr, cols), lambda t, p_ref: (t, 0))),
        out_shape=jax.ShapeDtypeStruct((rh, cols), F32), compiler_params=_cp(("parallel",)),
    )(jnp.reshape(p, (1,)).astype(jnp.int32), landed, chip_sum)


def adam_step(name, ga, gb, w, m, v):
    R, C = w.shape
    tr = _row_tile(R, C, 1 << 17)
    ins = [ga] + ([gb] if gb is not None else []) + [w, m, v]

    def body(*refs):
        g = refs[0][...]
        if gb is not None:
            g = g + refs[1][...]
        w_ref, m_ref, v_ref, g_out, d_out, m_out, v_out = refs[len(ins) - 3:]
        d, m2, v2 = _adam(w_ref[...], g, m_ref[...], v_ref[...])
        g_out[...] = g
        d_out[...] = d
        m_out[...] = m2
        v_out[...] = v2

    tile = pl.BlockSpec((tr, C), lambda t: (t, 0))
    return pl.pallas_call(
        body, name=name, grid=(R // tr,), in_specs=[tile] * len(ins), out_specs=[tile] * 4,
        out_shape=[jax.ShapeDtypeStruct((R, C), F32)] * 4, compiler_params=_cp(("parallel",)),
    )(*ins)


def adam_ffn(name, g_pieces, w, m, v, transposed=False):
    if transposed:
        res = adam_ffn(name, g_pieces, *(jnp.swapaxes(a, 2, 3) for a in (w, m, v)))
        return [jnp.swapaxes(r, 2, 3) for r in res]
    _, _, R, C = w.shape
    tr = _row_tile(R, 4 * C, 1 << 17)

    def body(g00, g01, g10, g11, w_ref, m_ref, v_ref, g_out, d_out, m_out, v_out):
        for l, j, g_ref in ((0, 0, g00), (0, 1, g01), (1, 0, g10), (1, 1, g11)):
            g = g_ref[...]
            d, m2, v2 = _adam(w_ref[l, j], g, m_ref[l, j], v_ref[l, j])
            g_out[l, j] = g
            d_out[l, j] = d
            m_out[l, j] = m2
            v_out[l, j] = v2

    piece = pl.BlockSpec((tr, C), lambda t: (t, 0))
    full = pl.BlockSpec((2, 2, tr, C), lambda t: (0, 0, t, 0))
    return pl.pallas_call(
        body, name=name, grid=(R // tr,), in_specs=[piece] * 4 + [full] * 3, out_specs=[full] * 4,
        out_shape=[jax.ShapeDtypeStruct(w.shape, F32)] * 4, compiler_params=_cp(("parallel",)),
    )(*g_pieces, w, m, v)


def _place():
    return lax.axis_index("x"), lax.axis_index("y"), lax.axis_index("c")


def _flip(me, mask):
    return tuple(1 - v if mk else v for v, mk in zip(me, mask))


CHIP_MASKS = ((1, 0, 0), (0, 1, 0), (1, 1, 0))
ALL_MASKS = tuple((a, b, c) for a in (0, 1) for b in (0, 1) for c in (0, 1) if (a, b, c) != (0, 0, 0))


def _chip(dev):
    return 2 * dev[0] + dev[1]


def _devno(dev):
    return 4 * dev[0] + 2 * dev[1] + dev[2]


class Pushes:
    def __init__(self, arrays, out_shapes, masks, copies, src_of, dst_of, alias=False):
        self.arrays, self.out_shapes, self.masks, self.copies = list(arrays), list(out_shapes), masks, copies
        self.src_of, self.dst_of, self.alias = src_of, dst_of, alias
        self.n = len(self.arrays)

    def sem_shapes(self):
        k = self.n * len(self.masks) * self.copies
        return [pltpu.SemaphoreType.DMA((k,)), pltpu.SemaphoreType.DMA((k,))]

    def ops(self, ins, outs, send_sems, recv_sems):
        me = _place()
        sends, lands = [], []
        for i in range(self.n):
            for j, mk in enumerate(self.masks):
                peer = _flip(me, mk)
                srcs, dsts = self.src_of(ins[i], me, j), self.dst_of(outs[i], me, j)
                here = self.dst_of(outs[i], peer, j)
                for q in range(self.copies):
                    sem = (i * len(self.masks) + j) * self.copies + q
                    sends.append(pltpu.make_async_remote_copy(
                        src_ref=srcs[q], dst_ref=dsts[q], send_sem=send_sems.at[sem], recv_sem=recv_sems.at[sem],
                        device_id=peer, device_id_type=MESH))
                    lands.append(pltpu.make_async_remote_copy(
                        src_ref=here[q], dst_ref=here[q], send_sem=send_sems.at[sem], recv_sem=recv_sems.at[sem],
                        device_id=peer, device_id_type=MESH))

        def start():
            for cp in sends:
                cp.start()

        def wait():
            for cp in lands:
                cp.wait_recv()
            for cp in sends:
                cp.wait_send()

        return start, wait


_HBM = pl.BlockSpec(memory_space=pl.ANY)


def exchange(name, p, local_of=None):
    n = p.n

    def body(*refs):
        ins, outs = refs[:n], refs[n:2 * n]
        start, wait = p.ops(ins, outs, refs[2 * n], refs[2 * n + 1])
        locals_ = []
        if local_of is not None:
            for i in range(n):
                src, dst = local_of(ins[i], outs[i], _place())
                locals_.append(pltpu.make_async_copy(src, dst, refs[2 * n + 2].at[i]))
                locals_[-1].start()
        start()
        wait()
        for cp in locals_:
            cp.wait()

    return pl.pallas_call(
        body, name=name, in_specs=[_HBM] * n, out_specs=[_HBM] * n, out_shape=p.out_shapes,
        scratch_shapes=p.sem_shapes() + ([pltpu.SemaphoreType.DMA((n,))] if local_of is not None else []),
        input_output_aliases={i: i for i in range(n)} if p.alias else {},
    )(*p.arrays)


def _half(c, rows):
    return pl.ds(c * (rows // 2), rows // 2)


def gather_pushes(arrays):
    outs = [jax.ShapeDtypeStruct((N_CHIPS,) + a.shape, a.dtype) for a in arrays]
    sib = len(CHIP_MASKS)
    return Pushes(arrays, outs, CHIP_MASKS + ((0, 0, 1),), 1,
                  src_of=lambda r, me, j: [r] if j == sib else [r.at[_half(me[2], r.shape[0])]],
                  dst_of=lambda o, sender, j: [o.at[_chip(sender)]] if j == sib else
                  [o.at[_chip(sender), _half(sender[2], o.shape[1])]])


def gather_swap(name, got):
    outs = [jax.ShapeDtypeStruct(a.shape, a.dtype) for a in got]
    return exchange(name, Pushes(
        got, outs, ((0, 0, 1),), len(CHIP_MASKS),
        src_of=lambda r, me, j: [r.at[_chip(_flip(me, mk)), _half(me[2], r.shape[1])] for mk in CHIP_MASKS],
        dst_of=lambda o, sender, j: [o.at[_chip(_flip(sender, mk)), _half(sender[2], o.shape[1])] for mk in CHIP_MASKS],
        alias=True))


def reduce_swap(arrays):
    split = [a.reshape(N_CHIPS, 2, a.shape[1] // 2, a.shape[2]) for a in arrays]
    half_shapes = [jax.ShapeDtypeStruct((N_CHIPS,) + a.shape[2:], F32) for a in split]
    return split, Pushes(split, half_shapes, ((0, 0, 1),), 1,
                         src_of=lambda r, me, j: [r.at[:, 1 - me[2]]], dst_of=lambda o, sender, j: [o])


def reduce_begin(tag, names, arrays, wire):
    split, pushes = reduce_swap(arrays)
    return reduce_sum(names, split, exchange(f"grad_pre_swap_{tag}", pushes), wire)


def reduce_sum(names, split, theirs, wire):
    c = lax.axis_index("c")
    chip_sum = [sum_own_half(f"sum2_{nm}", a, t, c, dt) for nm, a, t, dt in zip(names, split, theirs, wire)]
    pushes = Pushes(chip_sum, [jax.ShapeDtypeStruct((len(CHIP_MASKS),) + a.shape[1:], a.dtype) for a in chip_sum],
                    CHIP_MASKS, 1,
                    src_of=lambda r, me, j: [r.at[_chip(_flip(me, CHIP_MASKS[j]))]],
                    dst_of=lambda o, sender, j: [o.at[j]])
    return chip_sum, pushes


def reduce_end(tag, names, chip_sum, landed):
    x, y, c = _place()
    halves = [sum_landed(f"sum4_{nm}", p, a, _chip((x, y, c))) for nm, p, a in zip(names, landed, chip_sum)]
    others = exchange(f"grad_final_swap_{tag}", Pushes(
        halves, [jax.ShapeDtypeStruct(a.shape, F32) for a in halves], ((0, 0, 1),), 1,
        src_of=lambda r, me, j: [r], dst_of=lambda o, sender, j: [o]))
    return [jnp.concatenate([jnp.where(c == 0, h, o), jnp.where(c == 0, o, h)], axis=0) for h, o in zip(halves, others)]


def gather_all(arrays):
    outs = [jax.ShapeDtypeStruct((8,) + a.shape, a.dtype) for a in arrays]
    return exchange("gather_replicated", Pushes(
        arrays, outs, ALL_MASKS, 1, src_of=lambda r, me, j: [r], dst_of=lambda o, sender, j: [o.at[_devno(sender)]]),
        local_of=lambda r, o, me: (r, o.at[_devno(me)]))


def _unshard_cols(g):
    return jnp.transpose(g, (1, 0, 2)).reshape(g.shape[1], -1)


def _shard_cols(a):
    return jnp.transpose(a.reshape(a.shape[0], N_CHIPS, -1), (1, 0, 2))


class Weights(dict):
    def ride(self, kernel_name):
        return None

    def arrived(self, kernel_name, outs):
        pass


def _forward_backward(x, tgt, W, grads_early=None):
    S = x.shape[0]
    G = {}
    sd = jax.ShapeDtypeStruct

    hidden = {}

    def ffn(xin, l, j):
        out, a_sav, b_sav, *rode = ffn_fwd(xin, W["ffn_norm"][l][j], W["ffn_w_gate", l, j], W["ffn_w_up", l, j],
                                           W["ffn_w_down", l, j], l, j, W.ride(f"ffn_fwd_{l}{j}"))
        hidden[l, j] = [a_sav, b_sav]
        W.arrived(f"ffn_fwd_{l}{j}", rode)
        return out

    def ffn_back(xin, dout, l, j):
        gn = W["ffn_norm"][l][j]
        dh, G["ffn_w_gate", l, j], G["ffn_w_up", l, j], G["ffn_w_down", l, j] = ffn_bwd(
            xin, gn, W["ffn_w_gate", l, j], W["ffn_w_up", l, j], W["ffn_w_down", l, j], dout, *hidden[l, j], l, j)
        dx, G[("ffn_norm", l, j)] = norm_bwd(f"ffn_norm_bwd_{l}{j}", xin, gn, dh, dout)
        return dx

    x0 = x
    x1 = ffn(x0, 0, 0)
    g0 = W["mix_norm"][0]
    sbq, sbk, sbv = tile_fwd(f_attn_sb, "attn_in_sb", [x1], [g0, W["attn_w_in"][0]], [sd((S, SB_W), F32)] * 3, 256)
    dl_shape = sd((DL_PAIRS, S, 128), F32)
    qn, = tile_fwd(f_attn_qk, "attn_in_q", [x1], [g0, W["attn_w_in"][1], W["attn_q_norm"]], [dl_shape], 256)
    kn, = tile_fwd(f_attn_qk, "attn_in_k", [x1], [g0, W["attn_w_in"][2], W["attn_k_norm"]], [dl_shape], 256)
    vv, = tile_fwd(f_attn_v, "attn_in_v", [x1], [g0, W["attn_w_in"][3]], [dl_shape], 256)
    oa, sb_wts, *rode = sb_fwd(sbq, sbk, sbv, W.ride("sb_fwd"))
    W.arrived("sb_fwd", rode)
    qs, ks, vs = (reorder(nm, t, DIL, False) for nm, t in (("sub_q", qn), ("sub_k", kn), ("sub_v", vv)))
    o_s, lse_s, *rode = dil_fwd(qs, ks, vs, W["bias_mat"], W.ride("dil_fwd"))
    W.arrived("dil_fwd", rode)
    o_n, lse_n = reorder("nat_o", o_s, DIL, True), reorder("nat_lse", lse_s, DIL, True)
    x2, = tile_fwd(f_attn_out, "attn_out", [x1, oa, o_n, lse_n], [W["attn_w_out"]], [sd((S, D), F32)], 256)
    x3 = ffn(x2, 0, 1)
    x4 = ffn(x3, 1, 0)
    g1 = W["mix_norm"][1]
    h, hs = norm_shift_fwd(x4, g1)
    mix = W["rw_mix"]
    r, = tile_fwd(f_rw_proj, "rw_proj_r", [h, hs], [mix[0:1], W["rw_wr"]], [sd((S, D), F32)], 256)
    k, = tile_fwd(f_rw_proj, "rw_proj_k", [h, hs], [mix[2:3], W["rw_wk"]], [sd((S, D), F32)], 256)
    v, = tile_fwd(f_rw_proj, "rw_proj_v", [h, hs], [mix[3:4], W["rw_wv"]], [sd((S, D), F32)], 256)
    mix3 = jnp.concatenate([mix[1:2], mix[4:5], mix[5:6]], axis=0)
    mid_w = [mix3, W["rw_w0"], W["rw_a0"], W["rw_kk"], W["rw_ka"], W["rw_w1"], W["rw_w2"], W["rw_a1"], W["rw_a2"],
             W["rw_g1"], W["rw_g2"]]
    hshape = sd((RW_H, S, HEAD), F32)
    mid_tiles = [h, hs, r, k, v]
    rh, lwh, kh, vh, ah, bh, gate = tile_fwd(f_rw_mid, "rw_mid", mid_tiles, mid_w, [hshape] * 6 + [sd((S, D), F32)], 128)
    yh, states = rwkv_fwd(rh, lwh, kh, vh, ah, bh)
    post_w = [W["rw_lnx_g"], W["rw_lnx_b"], W["rw_rk"], W["rw_wo"]]
    post_tiles = [yh, rh, kh, vh, gate, x4]
    x5, = tile_fwd(f_rw_post, "rw_post", post_tiles, post_w, [sd((S, D), F32)], 128)
    x6 = ffn(x5, 1, 1)
    dx6, loss_part = loss_head(x6, tgt)

    dx5 = ffn_back(x5, dx6, 1, 1)
    (dyh, drh, dkh, dvh, dgate, dx4), (d_lng, d_lnb, d_rk, d_wo) = tile_bwd(
        f_rw_post, "rw_post_bwd", post_tiles, post_w, [dx5], 128, [True] * 6, [True] * 4)
    drh2, dlwh, dkh2, dvh2, dah, dbh = rwkv_bwd(rh, lwh, kh, vh, ah, bh, states, dyh)
    mid_cts = [(drh, drh2), dlwh, (dkh, dkh2), (dvh, dvh2), dah, dbh, dgate]
    (dh, dhs, dr, dk, dv), dmid_w = tile_bwd(f_rw_mid, "rw_mid_bwd", mid_tiles, mid_w, mid_cts, 128,
                                             [True] * 5, [True] * len(mid_w))
    dmix = {}
    for nm, ct, row, wname in (("r", dr, 0, "rw_wr"), ("k", dk, 2, "rw_wk"), ("v", dv, 3, "rw_wv")):
        (dh, dhs), (dmix[row], G[wname]) = tile_bwd(
            f_rw_proj, f"rw_proj_{nm}_bwd", [h, hs], [mix[row:row + 1], W[wname]], [ct], 256,
            [True, True], [True, True], acc={0: dh, 1: dhs})
    dx4, G[("mix_norm", 1)] = norm_shift_bwd(x4, g1, dh, dhs, dx4)
    dmix3 = dmid_w[0]
    G["rw_mix"] = jnp.concatenate([dmix[0], dmix3[0:1], dmix[2], dmix[3], dmix3[1:2], dmix3[2:3]], axis=0)
    for nm, gv in zip(("rw_w0", "rw_a0", "rw_kk", "rw_ka", "rw_w1", "rw_w2", "rw_a1", "rw_a2", "rw_g1", "rw_g2"), dmid_w[1:]):
        G[nm] = gv
    G["rw_lnx_g"], G["rw_lnx_b"], G["rw_rk"], G["rw_wo"] = d_lng, d_lnb, d_rk, d_wo
    dx3 = ffn_back(x3, dx4, 1, 0)
    dx2 = ffn_back(x2, dx3, 0, 1)
    (dx1, doa, do_n, dlse_n), (G["attn_w_out"],) = tile_bwd(
        f_attn_out, "attn_out_bwd", [x1, oa, o_n, lse_n], [W["attn_w_out"]], [dx2], 256, [True] * 4, [True])
    do_s, dlse_s = reorder("sub_do", do_n, DIL, False), reorder("sub_dlse", dlse_n, DIL, False)
    ride, swapped = grads_early(G) if grads_early is not None else (None, None)
    dqs, dks, dvs, dsum, *rode = dil_bwd(qs, ks, vs, W["bias_mat"], o_s, lse_s, do_s, dlse_s, ride)
    ride, landed = swapped(rode) if swapped is not None else (None, None)
    G["rel_bias"] = bias_grad(dsum, W["buckets"])
    dqn, dkn, dvv = (reorder(nm, t, DIL, True) for nm, t in (("nat_dq", dqs), ("nat_dk", dks), ("nat_dv", dvs)))
    dsbq, dsbk, dsbv, *rode = sb_bwd(sbq, sbk, sbv, doa, sb_wts, ride)
    if landed is not None:
        landed(rode)
    dg0 = []
    dwin = []
    (dx1,), (dg, dw) = tile_bwd(f_attn_sb, "attn_in_sb_bwd", [x1], [g0, W["attn_w_in"][0]], [dsbq, dsbk, dsbv], 256,
                                [True], [True, True], acc={0: dx1})
    dg0.append(dg), dwin.append(dw)
    (dx1,), (dg, dw, G["attn_q_norm"]) = tile_bwd(f_attn_qk, "attn_in_q_bwd", [x1], [g0, W["attn_w_in"][1], W["attn_q_norm"]],
                                                  [dqn], 256, [True], [True] * 3, acc={0: dx1})
    dg0.append(dg), dwin.append(dw)
    (dx1,), (dg, dw, G["attn_k_norm"]) = tile_bwd(f_attn_qk, "attn_in_k_bwd", [x1], [g0, W["attn_w_in"][2], W["attn_k_norm"]],
                                                  [dkn], 256, [True], [True] * 3, acc={0: dx1})
    dg0.append(dg), dwin.append(dw)
    (dx1,), (dg, dw) = tile_bwd(f_attn_v, "attn_in_v_bwd", [x1], [g0, W["attn_w_in"][3]], [dvv], 256,
                                [True], [True, True], acc={0: dx1})
    dg0.append(dg), dwin.append(dw)
    G[("mix_norm", 0)] = dg0
    G["attn_w_in"] = dwin
    dx0 = ffn_back(x0, dx1, 0, 0)
    return loss_part, dx0, G


VEC_ROWS = ("ffn_norm", "rw_mix", "rw_w0", "rw_a0", "rw_kk", "rw_ka", "rw_lnx_g", "rw_lnx_b")


def kernel(x, ffn_norm, ffn_w_gate, ffn_w_up, ffn_w_down, mix_norm, rel_bias, attn_w_in, attn_q_norm, attn_k_norm, attn_w_out, rw_mix, rw_w0, rw_w1, rw_w2, rw_a0, rw_a1, rw_a2, rw_g1, rw_g2, rw_kk, rw_ka, rw_rk, rw_wr, rw_wk, rw_wv, rw_wo, rw_lnx_g, rw_lnx_b, loss_target, m_ffn_norm, m_ffn_w_gate, m_ffn_w_up, m_ffn_w_down, m_mix_norm, m_rel_bias, m_attn_w_in, m_attn_q_norm, m_attn_k_norm, m_attn_w_out, m_rw_mix, m_rw_w0, m_rw_w1, m_rw_w2, m_rw_a0, m_rw_a1, m_rw_a2, m_rw_g1, m_rw_g2, m_rw_kk, m_rw_ka, m_rw_rk, m_rw_wr, m_rw_wk, m_rw_wv, m_rw_wo, m_rw_lnx_g, m_rw_lnx_b, v_ffn_norm, v_ffn_w_gate, v_ffn_w_up, v_ffn_w_down, v_mix_norm, v_rel_bias, v_attn_w_in, v_attn_q_norm, v_attn_k_norm, v_attn_w_out, v_rw_mix, v_rw_w0, v_rw_w1, v_rw_w2, v_rw_a0, v_rw_a1, v_rw_a2, v_rw_g1, v_rw_g2, v_rw_kk, v_rw_ka, v_rw_rk, v_rw_wr, v_rw_wk, v_rw_wv, v_rw_wo, v_rw_lnx_g, v_rw_lnx_b):
    names = ["ffn_norm", "ffn_w_gate", "ffn_w_up", "ffn_w_down", "mix_norm", "rel_bias", "attn_w_in", "attn_q_norm",
             "attn_k_norm", "attn_w_out", "rw_mix", "rw_w0", "rw_w1", "rw_w2", "rw_a0", "rw_a1", "rw_a2", "rw_g1", "rw_g2",
             "rw_kk", "rw_ka", "rw_rk", "rw_wr", "rw_wk", "rw_wv", "rw_wo", "rw_lnx_g", "rw_lnx_b"]
    loc = locals()
    w = {n: loc[n] for n in names}
    mom = {n: loc["m_" + n] for n in names}
    vel = {n: loc["v_" + n] for n in names}
    S = x.shape[1]

    ffn3 = ("ffn_w_gate", "ffn_w_up", "ffn_w_down")
    rw_mats = ("rw_w1", "rw_w2", "rw_a1", "rw_a2", "rw_g1", "rw_g2", "rw_wr", "rw_wk", "rw_wv", "rw_wo")
    cols_split = ("attn_w_out", "rw_w2", "rw_a2", "rw_g2")
    shard = {"vec": jnp.concatenate([w[n].reshape(-1, 256) for n in VEC_ROWS], axis=0)}
    for n in ffn3:
        for l in range(2):
            for j in range(2):
                shard[n, l, j] = w[n][l, j].astype(BF16)
    for n in ("attn_w_in", "attn_w_out") + rw_mats:
        shard[n] = w[n].reshape(-1, w[n].shape[-1]).astype(BF16)
    ffn_keys = lambda l, j: [(n, l, j) for n in ffn3]
    w_groups = {"first": ["vec"] + ffn_keys(0, 0) + ["attn_w_in", "attn_w_out"],
                "ffn_fwd_00": ffn_keys(0, 1), "sb_fwd": ffn_keys(1, 0) + list(rw_mats) + [("ffn_w_down", 1, 1)],
                "dil_fwd": [("ffn_w_gate", 1, 1), ("ffn_w_up", 1, 1)]}
    label = lambda key: key if isinstance(key, str) else f"{key[0]}_{key[1]}{key[2]}"

    class Streamed(Weights):
        def ride(self, kernel_name):
            keys_ = w_groups.get(kernel_name)
            return gather_pushes([shard[k] for k in keys_]) if keys_ else None

        def arrived(self, kernel_name, outs):
            if not outs:
                return
            for key, g in zip(w_groups[kernel_name], gather_swap(f"gather_swap_{kernel_name}", outs)):
                if key == "vec":
                    vec_full = _unshard_cols(g)
                    self["ffn_norm"] = [[vec_full[2 * l + j][None] for j in range(2)] for l in range(2)]
                    self["rw_mix"] = vec_full[4:10]
                    for i, n in enumerate(("rw_w0", "rw_a0", "rw_kk", "rw_ka", "rw_lnx_g", "rw_lnx_b")):
                        self[n] = vec_full[10 + i][None]
                elif key == "attn_w_in":
                    self[key] = [g[p] for p in range(N_CHIPS)]
                elif key in cols_split:
                    self[key] = _unshard_cols(g)
                elif isinstance(key, str):
                    self[key] = g.reshape(D, -1)
                else:
                    self[key] = g

    buckets = _bucket_maps()
    W = Streamed({"mix_norm": [mix_norm[0:1], mix_norm[1:2]], "attn_q_norm": attn_q_norm, "attn_k_norm": attn_k_norm,
                  "rw_rk": rw_rk[0][:, None, :], "buckets": buckets, "bias_mat": bias_table(rel_bias, buckets)})
    W.arrived("first", exchange("gather_weights", W.ride("first")))

    def slots(key, G):
        if key == "vec":
            rows = [G[("ffn_norm", l, j)] for l in range(2) for j in range(2)] + [G["rw_mix"]] + \
                   [G[n] for n in ("rw_w0", "rw_a0", "rw_kk", "rw_ka", "rw_lnx_g", "rw_lnx_b")]
            return _shard_cols(jnp.concatenate(rows, axis=0))
        if key == "attn_w_in":
            return jnp.stack(G[key])
        if key in cols_split:
            return _shard_cols(G[key])
        if isinstance(key, str):
            return G[key].reshape(N_CHIPS, D // N_CHIPS, -1)
        return G[key]

    g_groups = {"early": ffn_keys(1, 1) + ffn_keys(1, 0) + ffn_keys(0, 1) + list(rw_mats) + ["attn_w_out"],
                "late": ["vec", "attn_w_in"] + ffn_keys(0, 0)}
    wire = lambda keys: [F32 if k == "vec" else BF16 for k in keys]
    part = {}

    def grads_early(G):
        keys = g_groups["early"]
        names_ = [label(k) for k in keys]
        split, swap_pushes = reduce_swap([slots(k, G) for k in keys])

        def swapped(theirs):
            chip_sum, pushes = reduce_sum(names_, split, theirs, wire(keys))
            return pushes, lambda landed: part.update(zip(keys, reduce_end("early", names_, chip_sum, landed)))

        return swap_pushes, swapped

    loss_part, dx, G = _forward_backward(x[0], loss_target[0], W, grads_early)
    loss = lax.psum(loss_part[0, 0], ("x", "y", "c"))
    keys = g_groups["late"]
    chip_sum, pushes = reduce_begin("late", [label(k) for k in keys], [slots(k, G) for k in keys], wire(keys))
    part.update(zip(keys, reduce_end("late", [label(k) for k in keys], chip_sum, exchange("scatter_grads", pushes))))

    rep = jnp.concatenate([G[("mix_norm", 0)][0] + G[("mix_norm", 0)][1] + G[("mix_norm", 0)][2] + G[("mix_norm", 0)][3],
                           G[("mix_norm", 1)]], axis=0).reshape(16, 128)
    rep = jnp.concatenate([rep, G["rel_bias"], jnp.pad(G["attn_q_norm"], ((0, 0), (0, 64))),
                           jnp.pad(G["attn_k_norm"], ((0, 0), (0, 64))), G["rw_rk"].reshape(8, 128),
                           jnp.zeros((2, 128), F32)], axis=0)
    rep_sum = sum_slots("sum_replicated", gather_all([rep])[0])
    g_rep = {
        "mix_norm": rep_sum[0:16].reshape(2, D),
        "rel_bias": jnp.transpose(rep_sum[16:28, :N_BUCKETS]),
        "attn_q_norm": rep_sum[28:29, :HEAD], "attn_k_norm": rep_sum[29:30, :HEAD],
        "rw_rk": rep_sum[30:38].reshape(1, RW_H, HEAD),
    }

    out = {}

    def adam(n, ga, gb):
        shp = w[n].shape
        to2 = lambda a: a.reshape(-1, shp[-1])
        res = adam_step(f"adam_{n}", to2(ga), None if gb is None else to2(gb), to2(w[n]), to2(mom[n]), to2(vel[n]))
        out[n] = tuple(r.reshape(shp) for r in res)

    for n in ffn3:
        out[n] = tuple(adam_ffn(f"adam_{n}", [part[n, l, j] for l in range(2) for j in range(2)], w[n], mom[n], vel[n],
                                transposed=n != "ffn_w_down"))
    for n in ("attn_w_in", "attn_w_out") + rw_mats:
        adam(n, part[n], None)
    rows = {"ffn_norm": (0, 4), "rw_mix": (4, 10), "rw_w0": (10, 11), "rw_a0": (11, 12), "rw_kk": (12, 13),
            "rw_ka": (13, 14), "rw_lnx_g": (14, 15), "rw_lnx_b": (15, 16)}
    for n, (lo, hi) in rows.items():
        adam(n, part["vec"][lo:hi], None)
    for n, gv in g_rep.items():
        adam(n, gv, None)

    grads = [out[n][0] for n in names]
    deltas = [out[n][1] for n in names]
    new_m = [out[n][2] for n in names]
    new_v = [out[n][3] for n in names]
    return (loss, dx[None], *grads, *deltas, *new_m, *new_v)
```

```python
import functools
import math

import jax
import jax.numpy as jnp
from jax import lax
from jax.experimental import pallas as pl
from jax.experimental.pallas import tpu as pltpu

F32, BF16 = jnp.float32, jnp.bfloat16
HI = lax.Precision.HIGHEST
MESH = pl.DeviceIdType.MESH

D = 1024
HEAD = 64
N_CHIPS = 4
FF_SHARD = 704
SB_W = 256
DL_HEADS = 12
DL_PAIRS = 6
DIL = (1, 4, 16)
QBLK = 128
N_BUCKETS = 32
MAX_DISTANCE = 2048
RW_H = 16
RW_CHUNK = 64
NORM_EPS = 1e-6
GN_EPS = 64e-5
NEG_INF = -1e30
VMEM_LIMIT = 56 * 1024 * 1024

ADAM_LR, ADAM_B1, ADAM_B2, ADAM_EPS, ADAM_WD, ADAM_STEP = 0.001, 0.9, 0.999, 1e-08, 0.01, 10


def _cp(sem):
    return pltpu.CompilerParams(dimension_semantics=sem, vmem_limit_bytes=VMEM_LIMIT)


def _dg(a, b, dims, prec=None):
    return lax.dot_general(a, b, (dims, ((), ())), precision=prec, preferred_element_type=F32)


def _bdot(a, b, dims):
    return _dg(a.astype(BF16), b.astype(BF16), dims)


@jax.custom_vjp
def mm(a, b):
    return _bdot(a, b, ((1,), (0,)))


def _mm_fwd(a, b):
    return _bdot(a, b, ((1,), (0,))), (a, b)


def _mm_bwd(res, g):
    a, b = res
    return _bdot(g, b, ((1,), (1,))), _bdot(a, g, ((0,), (0,)))


mm.defvjp(_mm_fwd, _mm_bwd)


def rms(x, g):
    return x * lax.rsqrt(jnp.mean(x * x, axis=-1, keepdims=True) + NORM_EPS) * g


def _pieces(x):
    x1 = x.astype(BF16)
    r1 = x - x1.astype(F32)
    x2 = r1.astype(BF16)
    return jnp.concatenate([x1, x2, (r1 - x2.astype(F32)).astype(BF16)], axis=-1)


def _group_sum(x, nh):
    w = x.shape[-1]
    e = (lax.broadcasted_iota(jnp.int32, (w, nh), 0) // HEAD == lax.broadcasted_iota(jnp.int32, (w, nh), 1)).astype(BF16)
    s = _dg(_pieces(x), jnp.concatenate([e, e, e], axis=0), ((1,), (0,)))
    return _dg(_pieces(s), jnp.concatenate([e, e, e], axis=1), ((1,), (1,)))


@functools.partial(jax.custom_vjp, nondiff_argnums=(1,))
def group_sum(x, nh):
    return _group_sum(x, nh)


group_sum.defvjp(lambda x, nh: (_group_sum(x, nh), None), lambda nh, _, g: (_group_sum(g, nh),))


def softplus(u):
    return jnp.maximum(u, 0.0) + jnp.log1p(jnp.exp(-jnp.abs(u)))


def to_heads(t, nh=RW_H):
    return jnp.stack([t[:, HEAD * h:HEAD * (h + 1)] for h in range(nh)])


def from_heads(t):
    return jnp.concatenate([t[h] for h in range(t.shape[0])], axis=-1)


def _tile_spec(shape, tm):
    if len(shape) == 2:
        return pl.BlockSpec((tm, shape[1]), lambda t: (t, 0))
    return pl.BlockSpec((shape[0], tm, shape[2]), lambda t: (0, t, 0))


def _full_spec(shape):
    nd = len(shape)
    return pl.BlockSpec(tuple(shape), lambda t: (0,) * nd)


def _rows(a):
    return a.shape[0] if a.ndim == 2 else a.shape[1]


def tile_fwd(f, name, tiles, weights, outs, tm):
    nt, nw = len(tiles), len(weights)

    def body(*refs):
        tv = [r[...] for r in refs[:nt]]
        wv = [r[...].astype(F32) for r in refs[nt:nt + nw]]
        res = f(*tv, *wv)
        if not isinstance(res, (tuple, list)):
            res = (res,)
        for o, v in zip(refs[nt + nw:], res):
            o[...] = v.astype(o.dtype)

    return pl.pallas_call(
        body, name=name, grid=(_rows(tiles[0]) // tm,),
        in_specs=[_tile_spec(a.shape, tm) for a in tiles] + [_full_spec(w.shape) for w in weights],
        out_specs=[_tile_spec(o.shape, tm) for o in outs],
        out_shape=list(outs),
        compiler_params=_cp(("parallel",)),
    )(*tiles, *weights)


def tile_bwd(f, name, tiles, weights, cts, tm, dt, dw, acc=None):
    acc = acc or {}
    groups = [c if isinstance(c, tuple) else (c,) for c in cts]
    cts = [a for grp in groups for a in grp]
    nt, nw, nc = len(tiles), len(weights), len(cts)
    acc_idx = sorted(acc)
    na = len(acc_idx)
    dti = [i for i in range(nt) if dt[i]]
    dwi = [i for i in range(nw) if dw[i]]

    def body(*refs):
        tv = [r[...] for r in refs[:nt]]
        wv = [r[...].astype(F32) for r in refs[nt:nt + nw]]
        crefs = list(refs[nt + nw:nt + nw + nc])
        cv = []
        for grp in groups:
            terms = [crefs.pop(0)[...] for _ in grp]
            cv.append(functools.reduce(lambda a, b: a + b, terms))
        av = {i: r[...] for i, r in zip(acc_idx, refs[nt + nw + nc:nt + nw + nc + na])}
        orefs = refs[nt + nw + nc + na:]

        def g(*diff):
            t2, w2 = list(tv), list(wv)
            for i, v in zip(dti, diff[:len(dti)]):
                t2[i] = v
            for i, v in zip(dwi, diff[len(dti):]):
                w2[i] = v
            res = f(*t2, *w2)
            return tuple(res) if isinstance(res, (tuple, list)) else (res,)

        _, vjp = jax.vjp(g, *[tv[i] for i in dti], *[wv[i] for i in dwi])
        grads = vjp(tuple(cv))
        for k, i in enumerate(dti):
            gt = grads[k]
            if i in av:
                gt = gt + av[i]
            orefs[k][...] = gt
        first = pl.program_id(0) == 0
        for k, i in enumerate(dwi):
            o = orefs[len(dti) + k]
            gw = grads[len(dti) + k]

            @pl.when(first)
            def _(o=o, gw=gw):
                o[...] = gw

            @pl.when(jnp.logical_not(first))
            def _(o=o, gw=gw):
                o[...] += gw

    out_shape = [jax.ShapeDtypeStruct(tiles[i].shape, F32) for i in dti] + \
                [jax.ShapeDtypeStruct(weights[i].shape, F32) for i in dwi]
    res = pl.pallas_call(
        body, name=name, grid=(_rows(tiles[0]) // tm,),
        in_specs=[_tile_spec(a.shape, tm) for a in tiles] + [_full_spec(w.shape) for w in weights] +
                 [_tile_spec(c.shape, tm) for c in cts] + [_tile_spec(tiles[i].shape, tm) for i in acc_idx],
        out_specs=[_tile_spec(tiles[i].shape, tm) for i in dti] + [_full_spec(weights[i].shape) for i in dwi],
        out_shape=out_shape,
        compiler_params=_cp(("arbitrary",)),
    )(*tiles, *weights, *cts, *[acc[i] for i in acc_idx])
    return list(res[:len(dti)]), list(res[len(dti):])


def _ffn_wspec(rows, cols, cfirst):
    if cfirst:
        return pl.BlockSpec((1, rows, cols), lambda c, t: (c, 0, 0))
    return pl.BlockSpec((1, rows, cols), lambda t, c: (c, 0, 0))


def ffn_fwd(x, g, wg, wu, wd, l, j, ride=None, tm=1024):
    S = x.shape[0]
    r_in, r_out, r_shape, r_scr, r_args = _ride_specs(ride)

    def body(*refs):
        t, c = pl.program_id(0), pl.program_id(1)
        (x_ref, g_ref, wg_ref, wu_ref, wd_ref, o_ref, a_ref, b_ref, h_ref, acc_ref), finish = _riding(
            ride, refs, 5, 3, (t == 0) & (c == 0), (t == S // tm - 1) & (c == N_CHIPS - 1))

        @pl.when(c == 0)
        def _():
            h_ref[...] = rms(x_ref[...], g_ref[...]).astype(BF16)
            acc_ref[...] = jnp.zeros_like(acc_ref)

        h = h_ref[...]
        a = _bdot(h, wg_ref[0], ((1,), (0,)))
        b = _bdot(h, wu_ref[0], ((1,), (0,)))
        a_ref[0] = a.astype(BF16)
        b_ref[0] = b.astype(BF16)
        y = a * jax.nn.sigmoid(a) * b
        acc_ref[...] += _bdot(y, wd_ref[0], ((1,), (0,)))

        @pl.when(c == N_CHIPS - 1)
        def _():
            o_ref[...] = x_ref[...] + 0.5 * acc_ref[...]

        finish()

    hid = pl.BlockSpec((1, tm, FF_SHARD), lambda t, c: (c, t, 0))
    return pl.pallas_call(
        body, name=f"ffn_fwd_{l}{j}", grid=(S // tm, N_CHIPS),
        in_specs=[pl.BlockSpec((tm, D), lambda t, c: (t, 0)), pl.BlockSpec((1, D), lambda t, c: (0, 0)),
                  _ffn_wspec(D, FF_SHARD, False), _ffn_wspec(D, FF_SHARD, False), _ffn_wspec(FF_SHARD, D, False)] + r_in,
        out_specs=[pl.BlockSpec((tm, D), lambda t, c: (t, 0)), hid, hid] + r_out,
        out_shape=[jax.ShapeDtypeStruct((S, D), F32)] + [jax.ShapeDtypeStruct((N_CHIPS, S, FF_SHARD), BF16)] * 2 + r_shape,
        scratch_shapes=[pltpu.VMEM((tm, D), BF16), pltpu.VMEM((tm, D), F32)] + r_scr,
        compiler_params=_cp(("arbitrary", "arbitrary")),
    )(x, g, wg, wu, wd, *r_args)


def ffn_bwd(x, g, wg, wu, wd, dout, a_sav, b_sav, l, j, tm=512):
    S = x.shape[0]

    def body(x_ref, g_ref, wg_ref, wu_ref, wd_ref, do_ref, a_ref, b_ref, dh_ref, dwg_ref, dwu_ref, dwd_ref):
        t = pl.program_id(1)
        h = rms(x_ref[...], g_ref[...]).astype(BF16)
        wgv, wuv, wdv = wg_ref[0], wu_ref[0], wd_ref[0]
        a = a_ref[0].astype(F32)
        b = b_ref[0].astype(F32)
        sig = jax.nn.sigmoid(a)
        s = a * sig
        dyd = 0.5 * do_ref[...]
        dy = _bdot(dyd, wdv, ((1,), (1,)))
        dwd = _bdot(s * b, dyd, ((0,), (0,)))
        db = dy * s
        da = dy * b * (sig * (1.0 + a * (1.0 - sig)))
        dwg = _bdot(da, h, ((0,), (0,)))
        dwu = _bdot(db, h, ((0,), (0,)))
        dh_ref[0] = (_bdot(da, wgv, ((1,), (1,))) + _bdot(db, wuv, ((1,), (1,)))).astype(dh_ref.dtype)

        @pl.when(t == 0)
        def _():
            dwg_ref[0] = dwg
            dwu_ref[0] = dwu
            dwd_ref[0] = dwd

        @pl.when(t != 0)
        def _():
            dwg_ref[0] += dwg
            dwu_ref[0] += dwu
            dwd_ref[0] += dwd

    return pl.pallas_call(
        body, name=f"ffn_bwd_{l}{j}", grid=(N_CHIPS, S // tm),
        in_specs=[pl.BlockSpec((tm, D), lambda c, t: (t, 0)), pl.BlockSpec((1, D), lambda c, t: (0, 0)),
                  _ffn_wspec(D, FF_SHARD, True), _ffn_wspec(D, FF_SHARD, True), _ffn_wspec(FF_SHARD, D, True),
                  pl.BlockSpec((tm, D), lambda c, t: (t, 0)),
                  pl.BlockSpec((1, tm, FF_SHARD), lambda c, t: (c, t, 0)), pl.BlockSpec((1, tm, FF_SHARD), lambda c, t: (c, t, 0))],
        out_specs=[pl.BlockSpec((1, tm, D), lambda c, t: (c, t, 0))] + [_ffn_wspec(FF_SHARD, D, True)] * 3,
        out_shape=[jax.ShapeDtypeStruct((N_CHIPS, S, D), BF16)] + [jax.ShapeDtypeStruct(wd.shape, F32)] * 3,
        compiler_params=_cp(("parallel", "arbitrary")),
    )(x, g, wg, wu, wd, dout, a_sav, b_sav)


def norm_bwd(name, x, g, dh_parts, dres, tm=512):
    S = x.shape[0]
    P = dh_parts.shape[0]

    def body(x_ref, g_ref, dh_ref, dr_ref, dx_ref, dg_ref):
        dh = dh_ref[0].astype(F32)
        for p in range(1, P):
            dh = dh + dh_ref[p].astype(F32)
        _, vjp = jax.vjp(rms, x_ref[...], g_ref[...])
        dx, dg = vjp(dh)
        dx_ref[...] = dr_ref[...] + dx

        @pl.when(pl.program_id(0) == 0)
        def _():
            dg_ref[...] = dg

        @pl.when(pl.program_id(0) != 0)
        def _():
            dg_ref[...] += dg

    return pl.pallas_call(
        body, name=name, grid=(S // tm,),
        in_specs=[pl.BlockSpec((tm, D), lambda t: (t, 0)), pl.BlockSpec((1, D), lambda t: (0, 0)),
                  pl.BlockSpec((P, tm, D), lambda t: (0, t, 0)), pl.BlockSpec((tm, D), lambda t: (t, 0))],
        out_specs=[pl.BlockSpec((tm, D), lambda t: (t, 0)), pl.BlockSpec((1, D), lambda t: (0, 0))],
        out_shape=[jax.ShapeDtypeStruct((S, D), F32), jax.ShapeDtypeStruct((1, D), F32)],
        compiler_params=_cp(("arbitrary",)),
    )(x, g, dh_parts, dres)


def f_attn_sb(x, g, w):
    pr = mm(rms(x, g), w)
    return pr[:, :SB_W], pr[:, SB_W:2 * SB_W], pr[:, 2 * SB_W:]


def _pairs(y):
    return jnp.stack([y[:, 128 * j:128 * (j + 1)] for j in range(DL_PAIRS)])


def f_attn_qk(x, g, w, nrm):
    pr = mm(rms(x, g), w)
    ms = group_sum(pr * pr, DL_HEADS) * (1.0 / HEAD)
    return _pairs(pr * lax.rsqrt(ms + NORM_EPS) * jnp.concatenate([nrm] * DL_HEADS, axis=1))


def f_attn_v(x, g, w):
    return _pairs(mm(rms(x, g), w))


def _masked(strict, x):
    return x if strict is None else jnp.where(strict, x, 0.0)


def _head_stack(x, dtype=BF16):
    nh = x.shape[1] // HEAD
    lane_head = lax.broadcasted_iota(jnp.int32, (1, x.shape[1]), 1) // HEAD
    return jnp.concatenate([jnp.where(lane_head == h, x, 0.0) for h in range(nh)], axis=0).astype(dtype)


def _head_pick(xs):
    nh = xs.shape[1] // HEAD
    rows = xs.shape[0] // nh
    lane_head = lax.broadcasted_iota(jnp.int32, (1, xs.shape[1]), 1) // HEAD
    out = xs[:rows]
    for h in range(1, nh):
        out = jnp.where(lane_head == h, xs[rows * h:rows * (h + 1)], out)
    return out


def _sb_tiles(qs, kblk, strict):
    z = _dg(qs, kblk, ((1,), (1,))) * (HEAD ** -0.5)
    keep = -(jnp.maximum(z, 0.0) + jnp.log(1.0 + jnp.exp(-jnp.abs(z))))
    return z, _masked(strict, keep)


def _tri(n, upper):
    r = lax.broadcasted_iota(jnp.int32, (n, n), 0)
    c = lax.broadcasted_iota(jnp.int32, (n, n), 1)
    return ((r > c) if upper else (r < c)).astype(BF16)


def _tri_sums(x, tri):
    hi, lo = _split2(x)
    return _dg(jnp.concatenate([hi, lo], axis=1), jnp.concatenate([tri, tri], axis=0), ((1,), (0,)))


SB_UNROLL = 8


def _sb_diag(tb, nh):
    r = lax.broadcasted_iota(jnp.int32, (nh * tb, tb), 0)
    return lax.broadcasted_iota(jnp.int32, (nh * tb, tb), 1) < lax.rem(r, tb)


def _sb_sweep(step, first, count, carry, direction, commit=None):
    def run(kbs, c):
        outs = []
        for kb in kbs:
            c, out = step(kb, c)
            outs.append(out)
        if commit is not None:
            for kb, out in zip(kbs, outs):
                commit(kb, out)
        return c

    pos, size = first, 1
    while size < SB_UNROLL:
        n = (count // size) % 2
        carry = lax.fori_loop(
            0, n, lambda i, c, pos=pos, size=size: run([pos + direction * u for u in range(size)], c), carry)
        pos, size = pos + direction * size * n, 2 * size
    return lax.fori_loop(
        0, count // SB_UNROLL,
        lambda g, c: run([pos + direction * (SB_UNROLL * g + u) for u in range(SB_UNROLL)], c), carry)


def _riding(ride, refs, n_in, n_out, first, last):
    if ride is None:
        return refs, lambda: None
    n = ride.n
    own = refs[:n_in] + refs[n_in + n:n_in + n + n_out] + refs[n_in + 2 * n + n_out:len(refs) - 2]
    start, wait = ride.ops(refs[n_in:n_in + n], refs[n_in + n + n_out:n_in + 2 * n + n_out], refs[-2], refs[-1])
    pl.when(first)(start)
    return own, lambda: pl.when(last)(wait)


def _ride_specs(ride):
    if ride is None:
        return [], [], [], [], []
    return [_HBM] * ride.n, [_HBM] * ride.n, ride.out_shapes, ride.sem_shapes(), ride.arrays


def sb_fwd(q, k, v, ride=None, tb=QBLK):
    S = q.shape[0]
    nh = SB_W // HEAD
    nb = S // tb
    r_in, r_out, r_shape, r_scr, r_args = _ride_specs(ride)

    def body(*refs):
        qb = pl.program_id(0)
        (q_ref, k_ref, v_ref, o_ref, w_ref), finish = _riding(ride, refs, 3, 2, qb == 0, qb == nb - 1)
        diag = _sb_diag(tb, nh)
        after_mat = _tri(tb, True)
        qs = _head_stack(q_ref[...])

        def step(kb, carry, strict):
            acc, run = carry
            rows = pl.ds(pl.multiple_of(kb * tb, tb), tb)
            z, keep = _sb_tiles(qs, k_ref[rows, :].astype(BF16), strict)
            w = _masked(strict, jnp.exp(z + keep + _tri_sums(keep, after_mat) + run)).astype(BF16)
            w_ref[0, kb] = w
            acc = acc + _dg(w, v_ref[rows, :].astype(BF16), ((1,), (0,)))
            return acc, run + jnp.sum(keep, axis=1, keepdims=True)

        init = (jnp.zeros((nh * tb, SB_W), F32), jnp.zeros((nh * tb, 1), F32))
        carry = step(qb, init, diag)
        acc, _ = _sb_sweep(lambda kb, c: (step(kb, c, None), None), qb - 1, qb, carry, -1)
        o_ref[...] = _head_pick(acc)
        finish()

    return pl.pallas_call(
        body, name="sb_fwd", grid=(S // tb,),
        in_specs=[pl.BlockSpec((tb, SB_W), lambda i: (i, 0)), pl.BlockSpec((S, SB_W), lambda i: (0, 0)),
                  pl.BlockSpec((S, SB_W), lambda i: (0, 0))] + r_in,
        out_specs=[pl.BlockSpec((tb, SB_W), lambda i: (i, 0)),
                   pl.BlockSpec((1, nb, nh * tb, tb), lambda i: (i, 0, 0, 0))] + r_out,
        out_shape=[jax.ShapeDtypeStruct((S, SB_W), F32), jax.ShapeDtypeStruct((nb, nb, nh * tb, tb), BF16)] + r_shape,
        scratch_shapes=r_scr,
        compiler_params=_cp(("arbitrary",)),
    )(q, k, v, *r_args)


def sb_bwd(q, k, v, do, wts, ride=None, tb=QBLK):
    S = q.shape[0]
    nh = SB_W // HEAD
    nb = S // tb
    scale = HEAD ** -0.5
    r_in, r_out, r_shape, r_scr, r_args = _ride_specs(ride)

    def body(*refs):
        qb = pl.program_id(0)
        (q_ref, k_ref, v_ref, do_ref, w_ref, dq_ref, dk_ref, dv_ref, g_scr), finish = _riding(
            ride, refs, 5, 3, qb == 0, qb == nb - 1)

        @pl.when(qb == 0)
        def _():
            dk_ref[...] = jnp.zeros_like(dk_ref)
            dv_ref[...] = jnp.zeros_like(dv_ref)

        diag = _sb_diag(tb, nh)
        before_mat = _tri(tb, False)
        qs = _head_stack(q_ref[...])
        dos = _head_stack(do_ref[...])

        def weights_pass(kb, carry):
            rows = pl.ds(pl.multiple_of(kb * tb, tb), tb)
            w = w_ref[0, kb]
            g_scr[kb] = _dg(dos, v_ref[rows, :].astype(BF16), ((1,), (1,))) * w.astype(F32)
            return carry, _dg(w, dos, ((0,), (0,)))

        def add_rows(ref):
            def commit(kb, val):
                ref[pl.ds(pl.multiple_of(kb * tb, tb), tb), :] += val
            return commit

        zero_run = jnp.zeros((nh * tb, 1), F32)
        _sb_sweep(weights_pass, 0, qb + 1, 0, 1, add_rows(dv_ref))

        def left_to_right(kb, carry, strict):
            dq, run = carry
            rows = pl.ds(pl.multiple_of(kb * tb, tb), tb)
            kblk = k_ref[rows, :].astype(BF16)
            gw = g_scr[kb]
            sig = jax.nn.sigmoid(_dg(qs, kblk, ((1,), (1,))) * scale)
            dkeep = _masked(strict, _dg(gw.astype(BF16), before_mat, ((1,), (0,))) + run)
            dz = ((gw * (1.0 - sig) - dkeep * sig) * scale).astype(BF16)
            dq = dq + _dg(dz, kblk, ((1,), (0,)))
            return (dq, run + jnp.sum(gw, axis=1, keepdims=True)), _dg(dz, qs, ((0,), (0,)))

        carry = _sb_sweep(lambda kb, c: left_to_right(kb, c, None), 0, qb,
                          (jnp.zeros((nh * tb, SB_W), F32), zero_run), 1, add_rows(dk_ref))
        (dq, _), dk_diag = left_to_right(qb, carry, diag)
        add_rows(dk_ref)(qb, dk_diag)
        dq_ref[...] = _head_pick(dq)
        finish()

    whole = pl.BlockSpec((S, SB_W), lambda i: (0, 0))
    blk = pl.BlockSpec((tb, SB_W), lambda i: (i, 0))
    return pl.pallas_call(
        body, name="sb_bwd", grid=(S // tb,),
        in_specs=[blk, whole, whole, blk, pl.BlockSpec((1, nb, nh * tb, tb), lambda i: (i, 0, 0, 0))] + r_in,
        out_specs=[blk, whole, whole] + r_out,
        out_shape=[jax.ShapeDtypeStruct((S, SB_W), F32)] * 3 + r_shape,
        scratch_shapes=[pltpu.VMEM((S // tb, nh * tb, tb), F32)] + r_scr,
        compiler_params=_cp(("arbitrary",)),
    )(q, k, v, do, wts, *r_args)


def reorder(name, x, groups, inverse):
    P, S, _ = x.shape

    def body(x_ref, o_ref):
        p = pl.program_id(0)
        for gi, r in enumerate(groups):
            @pl.when(p // 2 == gi)
            def _(r=r):
                L = S // r
                if r == 1:
                    o_ref[...] = x_ref[...]
                for c in range(r if r > 1 else 0):
                    if inverse:
                        o_ref[pl.ds(c, L, stride=r), :] = x_ref[c * L:(c + 1) * L, :]
                    else:
                        o_ref[c * L:(c + 1) * L, :] = x_ref[pl.ds(c, L, stride=r), :]

    slab = pl.BlockSpec((None, S, 128), lambda p: (p, 0, 0))
    return pl.pallas_call(
        body, name=name, grid=(P,), in_specs=[slab], out_specs=slab,
        out_shape=jax.ShapeDtypeStruct(x.shape, x.dtype), compiler_params=_cp(("parallel",)),
    )(x)


def _dil_blocks(S):
    return S // QBLK


def _dil_mask4(n_in_stream):
    qi = lax.rem(lax.broadcasted_iota(jnp.int32, (4 * QBLK, 2 * QBLK), 0), QBLK)
    kj = lax.broadcasted_iota(jnp.int32, (4 * QBLK, 2 * QBLK), 1) - QBLK
    dist = qi - kj
    return (dist >= 0) & (dist <= QBLK) & ((n_in_stream > 0) | (kj >= 0))


def _dil_lanes(ref):
    return jnp.concatenate([ref[0], ref[1]], axis=1)


def _dil_window(prev_ref, cur_ref):
    return jnp.concatenate([_dil_lanes(prev_ref), _dil_lanes(cur_ref)], axis=0).astype(BF16)


def _stream_pos(gi, i, S):
    nb = jnp.where(gi == 0, S // (QBLK * DIL[0]), jnp.where(gi == 1, S // (QBLK * DIL[1]), S // (QBLK * DIL[2])))
    return i % nb


def dil_fwd(q, k, v, bias, ride=None):
    S = q.shape[1]
    nblk = _dil_blocks(S)
    r_in, r_out, r_shape, r_scr, r_args = _ride_specs(ride)

    def body(*refs):
        gi, i = pl.program_id(0), pl.program_id(1)
        (q_ref, kc_ref, kp_ref, vc_ref, vp_ref, b_ref, o_ref, l_ref), finish = _riding(
            ride, refs, 6, 2, (gi == 0) & (i == 0), (gi == len(DIL) - 1) & (i == nblk - 1))
        mask = _dil_mask4(_stream_pos(gi, i, S))
        kw, vw = _dil_window(kp_ref, kc_ref), _dil_window(vp_ref, vc_ref)
        lg = _dg(_head_stack(_dil_lanes(q_ref)), kw, ((1,), (1,))) * (HEAD ** -0.5) + \
            b_ref[...].reshape(4 * QBLK, 2 * QBLK)
        lg = jnp.where(mask, lg, NEG_INF)
        m = jnp.max(lg, axis=-1, keepdims=True)
        p = jnp.exp(lg - m)
        den = jnp.sum(p, axis=-1, keepdims=True)
        o = _head_pick(_dg((p / den).astype(BF16), vw, ((1,), (0,))))
        lse = _head_pick(jnp.broadcast_to(m + jnp.log(den), (4 * QBLK, 4 * HEAD)))
        for j in range(2):
            o_ref[j] = o[:, 128 * j:128 * (j + 1)]
            l_ref[j] = lse[:, 128 * j:128 * (j + 1)]
        finish()

    cur = pl.BlockSpec((2, QBLK, 128), lambda g, i: (g, i, 0))
    prev = pl.BlockSpec((2, QBLK, 128), lambda g, i: (g, jnp.maximum(i - 1, 0), 0))
    return pl.pallas_call(
        body, name="dil_fwd", grid=(len(DIL), nblk),
        in_specs=[cur, cur, prev, cur, prev, pl.BlockSpec((4, QBLK, 2 * QBLK), lambda g, i: (g, 0, 0))] + r_in,
        out_specs=[cur, cur] + r_out,
        out_shape=[jax.ShapeDtypeStruct(q.shape, F32)] * 2 + r_shape,
        scratch_shapes=r_scr,
        compiler_params=_cp(("arbitrary", "arbitrary")),
    )(q, k, k, v, v, bias, *r_args)


def dil_bwd(q, k, v, bias, o, lse, do, dlse, ride=None):
    S = q.shape[1]
    nblk = _dil_blocks(S)
    r_in, r_out, r_shape, r_scr, r_args = _ride_specs(ride)

    def body(*refs):
        gi, i = pl.program_id(0), pl.program_id(1)
        (q_ref, kc_ref, kp_ref, vc_ref, vp_ref, b_ref, o_ref, l_ref, do_ref, dl_ref,
         dq_ref, dk_ref, dv_ref, ds_ref, dk_car, dv_car), finish = _riding(
            ride, refs, 10, 4, (gi == 0) & (i == 0), (gi == len(DIL) - 1) & (i == nblk))

        @pl.when(i == 0)
        def _():
            ds_ref[...] = jnp.zeros_like(ds_ref)
            dk_car[...] = jnp.zeros_like(dk_car)
            dv_car[...] = jnp.zeros_like(dv_car)

        @pl.when(i < nblk)
        def _():
            mask = _dil_mask4(_stream_pos(gi, i, S))
            kw, vw = _dil_window(kp_ref, kc_ref), _dil_window(vp_ref, vc_ref)
            qs = _head_stack(_dil_lanes(q_ref))
            do_nat = _dil_lanes(do_ref)
            dos = _head_stack(do_nat, F32)
            lse = jnp.sum(_head_stack(_dil_lanes(l_ref), F32), axis=-1, keepdims=True) * (1.0 / HEAD)
            lg = _dg(qs, kw, ((1,), (1,))) * (HEAD ** -0.5) + b_ref[...].reshape(4 * QBLK, 2 * QBLK)
            p = jnp.where(mask, jnp.exp(lg - lse), 0.0)
            dp = _dg(dos.astype(BF16), vw, ((1,), (1,)))
            four = lambda t: jnp.concatenate([t] * 4, axis=0)
            delta = jnp.sum(dos * four(_dil_lanes(o_ref)), axis=-1, keepdims=True)
            dl = jnp.sum(_head_stack(_dil_lanes(dl_ref), F32), axis=-1, keepdims=True)
            ds = p * (dp - delta + dl)
            ds_ref[...] += ds.reshape(4, QBLK, 2 * QBLK)
            dsq = (ds * (HEAD ** -0.5)).astype(BF16)
            dq = _head_pick(_dg(dsq, kw, ((1,), (0,))))
            dkw = _dg(dsq, qs, ((0,), (0,)))
            dvw = _dg(p.astype(BF16), dos.astype(BF16), ((0,), (0,)))
            for j in range(2):
                lanes = slice(128 * j, 128 * (j + 1))
                dq_ref[j] = dq[:, lanes]
                dk_ref[j] = dk_car[j] + dkw[:QBLK, lanes]
                dv_ref[j] = dv_car[j] + dvw[:QBLK, lanes]
                dk_car[j] = dkw[QBLK:, lanes]
                dv_car[j] = dvw[QBLK:, lanes]

        @pl.when(i == nblk)
        def _():
            dk_ref[...] = dk_car[...]
            dv_ref[...] = dv_car[...]

        finish()

    cur = pl.BlockSpec((2, QBLK, 128), lambda g, i: (g, jnp.minimum(i, nblk - 1), 0))
    prev = pl.BlockSpec((2, QBLK, 128), lambda g, i: (g, jnp.clip(i - 1, 0, nblk - 1), 0))
    bspec = pl.BlockSpec((4, QBLK, 2 * QBLK), lambda g, i: (g, 0, 0))
    return pl.pallas_call(
        body, name="dil_bwd", grid=(len(DIL), nblk + 1),
        in_specs=[cur, cur, prev, cur, prev, bspec, cur, cur, cur, cur] + r_in,
        out_specs=[cur, prev, prev, bspec] + r_out,
        out_shape=[jax.ShapeDtypeStruct(q.shape, F32)] * 3 + [jax.ShapeDtypeStruct(bias.shape, F32)] + r_shape,
        scratch_shapes=[pltpu.VMEM((2, QBLK, 128), F32), pltpu.VMEM((2, QBLK, 128), F32)] + r_scr,
        compiler_params=_cp(("arbitrary", "arbitrary")),
    )(q, k, k, v, v, bias, o, lse, do, dlse, *r_args)


def _t5_bucket(dist):
    max_exact = N_BUCKETS // 2
    d = jnp.maximum(dist, 1).astype(F32)
    large = max_exact + (jnp.log(d / max_exact) / math.log(MAX_DISTANCE / max_exact)
                         * (N_BUCKETS - max_exact)).astype(jnp.int32)
    large = jnp.minimum(large, N_BUCKETS - 1)
    return jnp.where(dist < max_exact, dist, large)


def _bucket_maps():
    qi = jnp.arange(QBLK)[:, None]
    kj = jnp.arange(2 * QBLK)[None, :] - QBLK
    dist = jnp.maximum(qi - kj, 0)
    return jnp.stack([_t5_bucket(dist * r) for r in DIL])


def bias_table(rel_bias, buckets):
    def body(tbl_ref, bk_ref, o_ref):
        for h in range(DL_HEADS):
            bk = bk_ref[h // 4]

            def step(b, acc):
                return jnp.where(bk == b, tbl_ref[b, h], acc)

            o_ref[h] = lax.fori_loop(0, N_BUCKETS, step, jnp.zeros(bk.shape, F32))

    return pl.pallas_call(
        body, name="bias_table", out_shape=jax.ShapeDtypeStruct((DL_HEADS,) + buckets.shape[1:], F32),
        in_specs=[pl.BlockSpec(memory_space=pltpu.SMEM), pl.BlockSpec(memory_space=pltpu.VMEM)],
        out_specs=pl.BlockSpec(memory_space=pltpu.VMEM),
    )(rel_bias, buckets)


def bias_grad(ds, buckets):
    def body(ds_ref, bk_ref, o_ref):
        lane = lax.broadcasted_iota(jnp.int32, (1, 128), 1)
        for h in range(DL_HEADS):
            dsv = ds_ref[h]
            bk = bk_ref[h // 4]

            def step(b, row):
                return jnp.where(lane == b, jnp.sum(jnp.where(bk == b, dsv, 0.0)), row)

            o_ref[h:h + 1, :] = lax.fori_loop(0, N_BUCKETS, step, jnp.zeros((1, 128), F32))

    return pl.pallas_call(
        body, name="bias_grad", out_shape=jax.ShapeDtypeStruct((DL_HEADS, 128), F32),
        in_specs=[pl.BlockSpec(memory_space=pltpu.VMEM)] * 2, out_specs=pl.BlockSpec(memory_space=pltpu.VMEM),
    )(ds, buckets)


def f_attn_out(x, oa, o, lse, w):
    og = [jnp.concatenate([o[2 * g], o[2 * g + 1]], axis=1) for g in range(3)]
    lg = [jnp.concatenate([lse[2 * g], lse[2 * g + 1]], axis=1) for g in range(3)]
    m = jnp.maximum(jnp.maximum(lg[0], lg[1]), lg[2])
    e = [jnp.exp(l - m) for l in lg]
    den = e[0] + e[1] + e[2]
    ob = (e[0] * og[0] + e[1] * og[1] + e[2] * og[2]) / den
    return x + mm(jnp.concatenate([oa, ob], axis=1), w)


def norm_shift_fwd(x, g, tm=256):
    S = x.shape[0]

    def body(x_ref, xp_ref, g_ref, h_ref, hs_ref):
        h = rms(x_ref[...], g_ref[...])
        hp = rms(xp_ref[7:8, :], g_ref[...])
        hp = jnp.where(pl.program_id(0) == 0, 0.0, hp)
        row = lax.broadcasted_iota(jnp.int32, (tm, D), 0)
        h_ref[...] = h
        hs_ref[...] = jnp.where(row == 0, hp, pltpu.roll(h, 1, 0))

    return pl.pallas_call(
        body, name="rw_norm_shift", grid=(S // tm,),
        in_specs=[pl.BlockSpec((tm, D), lambda t: (t, 0)),
                  pl.BlockSpec((8, D), lambda t: (jnp.maximum(t * (tm // 8) - 1, 0), 0)),
                  pl.BlockSpec((1, D), lambda t: (0, 0))],
        out_specs=[pl.BlockSpec((tm, D), lambda t: (t, 0))] * 2,
        out_shape=[jax.ShapeDtypeStruct((S, D), F32)] * 2,
        compiler_params=_cp(("parallel",)),
    )(x, x, g)


def norm_shift_bwd(x, g, dh, dhs, dres, tm=256):
    S = x.shape[0]
    nt = S // tm

    def body(x_ref, g_ref, dh_ref, dhs_ref, dhn_ref, dr_ref, dx_ref, dg_ref):
        t = pl.program_id(0)
        nxt = jnp.where(t == nt - 1, 0.0, dhn_ref[0:1, :])
        row = lax.broadcasted_iota(jnp.int32, (tm, D), 0)
        tot = dh_ref[...] + jnp.where(row == tm - 1, nxt, pltpu.roll(dhs_ref[...], tm - 1, 0))
        _, vjp = jax.vjp(rms, x_ref[...], g_ref[...])
        dx, dg = vjp(tot)
        dx_ref[...] = dr_ref[...] + dx

        @pl.when(t == 0)
        def _():
            dg_ref[...] = dg

        @pl.when(t != 0)
        def _():
            dg_ref[...] += dg

    tile = pl.BlockSpec((tm, D), lambda t: (t, 0))
    return pl.pallas_call(
        body, name="rw_norm_shift_bwd", grid=(nt,),
        in_specs=[tile, pl.BlockSpec((1, D), lambda t: (0, 0)), tile, tile,
                  pl.BlockSpec((8, D), lambda t: (jnp.minimum((t + 1) * (tm // 8), S // 8 - 1), 0)), tile],
        out_specs=[tile, pl.BlockSpec((1, D), lambda t: (0, 0))],
        out_shape=[jax.ShapeDtypeStruct((S, D), F32), jax.ShapeDtypeStruct((1, D), F32)],
        compiler_params=_cp(("arbitrary",)),
    )(x, g, dh, dhs, dhs, dres)


def f_rw_proj(h, hs, mix, w):
    return mm(h + (hs - h) * mix, w)


def f_rw_mid(h, hs, r, k, v, mix3, w0, a0, kkw, kaw, w1, w2, a1, a2, g1, g2):
    xx = hs - h
    xw, xa, xg = h + xx * mix3[0:1], h + xx * mix3[1:2], h + xx * mix3[2:3]
    w_log = -softplus(-(w0 + mm(jnp.tanh(mm(xw, w1)), w2))) - 0.5
    lw = -jnp.exp(w_log)
    ag = jax.nn.sigmoid(a0 + mm(mm(xa, a1), a2))
    gate = mm(jax.nn.sigmoid(mm(xg, g1)), g2)
    kk = k * kkw
    kk = kk / jnp.maximum(jnp.sqrt(group_sum(kk * kk, RW_H)), 1e-12)
    kmod = k * (1.0 + (ag - 1.0) * kaw)
    return (to_heads(r), to_heads(lw), to_heads(kmod), to_heads(v), to_heads(-kk), to_heads(kk * ag), gate)


def f_rw_post(yh, rh, kh, vh, gate, x, lng, lnb, rk, wo):
    mu = jnp.mean(yh, axis=-1, keepdims=True)
    var = jnp.mean(jnp.square(yh - mu), axis=-1, keepdims=True)
    yn = (yh - mu) * lax.rsqrt(var + GN_EPS)
    bonus = jnp.sum(rh * kh * rk, axis=-1, keepdims=True) * vh
    y = from_heads(yn) * lng + lnb + from_heads(bonus)
    return x + mm(y * gate, wo)


def _split2(x):
    hi = x.astype(BF16)
    return hi, (x - hi.astype(F32)).astype(BF16)


def _b3(x, y, cx, cy):
    xh, xl = _split2(x)
    yh, yl = _split2(y)
    x3 = jnp.concatenate([xh, xh, xl], axis=cx)
    y3 = jnp.concatenate([yh, yl, yh], axis=cy)
    return lax.dot_general(x3, y3, (((cx,), (cy,)), ((0,), (0,))), preferred_element_type=F32)


@jax.custom_vjp
def b_nt(x, y):
    return _b3(x, y, 2, 2)


@jax.custom_vjp
def b_nn(x, y):
    return _b3(x, y, 2, 1)


@jax.custom_vjp
def b_tn(x, y):
    return _b3(x, y, 1, 1)


def _b1(x, y, cx, cy):
    return lax.dot_general(x.astype(BF16), y.astype(BF16), (((cx,), (cy,)), ((0,), (0,))), preferred_element_type=F32)


b_nt.defvjp(lambda x, y: (b_nt(x, y), (x, y)), lambda r, g: (_b1(g, r[1], 2, 1), _b1(g, r[0], 1, 1)))
b_nn.defvjp(lambda x, y: (b_nn(x, y), (x, y)), lambda r, g: (_b1(g, r[1], 2, 2), _b1(r[0], g, 1, 1)))
b_tn.defvjp(lambda x, y: (b_tn(x, y), (x, y)), lambda r, g: (_b1(r[1], g, 2, 2), _b1(r[0], g, 2, 1)))


def _tri_apply(x, lower):
    H, C, _ = x.shape
    ii = lax.broadcasted_iota(jnp.int32, (C, C), 0)
    jj = lax.broadcasted_iota(jnp.int32, (C, C), 1)
    m = jnp.broadcast_to(((jj <= ii) if lower else (jj >= ii)).astype(BF16), (H, C, C))
    x1 = x.astype(BF16)
    r1 = x - x1.astype(F32)
    x2 = r1.astype(BF16)
    x3 = (r1 - x2.astype(F32)).astype(BF16)
    return lax.dot_general(jnp.concatenate([m, m, m], axis=2), jnp.concatenate([x1, x2, x3], axis=1),
                           (((2,), (1,)), ((0,), (0,))), preferred_element_type=F32)


@jax.custom_vjp
def run_sum(x):
    return _tri_apply(x, True)


run_sum.defvjp(lambda x: (run_sum(x), None), lambda _, g: (_tri_apply(g, False),))


def rwkv_chunk(S0, r, lw, k, v, a, b):
    H, C, _ = r.shape
    V = S0.shape[1]
    ii = lax.broadcasted_iota(jnp.int32, (C, C), 0)
    jj = lax.broadcasted_iota(jnp.int32, (C, C), 1)
    strict = jj < ii
    i2 = lax.broadcasted_iota(jnp.int32, (C, 2 * C), 0)
    j2 = lax.broadcasted_iota(jnp.int32, (C, 2 * C), 1)
    incl2 = jnp.where(j2 >= C, j2 - C, j2) <= i2
    g = run_sum(lw)
    ig = jnp.exp(-g)
    ar = jnp.concatenate([a * jnp.exp(g - lw), r * jnp.exp(g)], axis=1)
    bk = jnp.concatenate([b * ig, k * ig], axis=1)
    m = b_nt(ar, bk)
    a_ab = jnp.where(strict, m[:, :C, :C], 0.0)
    a_ak = jnp.where(strict, m[:, :C, C:], 0.0)
    b_r = jnp.where(incl2, m[:, C:, :], 0.0)
    p = b_nt(ar, S0)
    u = p[:, :C] + b_nn(a_ak, v)
    nmat, n = a_ab, 1
    while n < C:
        n *= 2
        if n < C:
            z = b_nn(nmat, jnp.concatenate([u, nmat], axis=2))
            u, nmat = u + z[:, :, :V], z[:, :, V:]
        else:
            u = u + b_nn(nmat, u)
    uv = jnp.concatenate([u, v], axis=1)
    y = p[:, C:] + b_nn(b_r, uv)
    g_end = g[:, C - 1:C, :]
    dec = jnp.exp(g_end - g)
    s_new = S0 * jnp.exp(g_end) + b_tn(uv, jnp.concatenate([b * dec, k * dec], axis=1))
    return y, s_new


def rwkv_fwd(r, lw, k, v, a, b):
    H, S, _ = r.shape
    C = RW_CHUNK

    def body(r_ref, lw_ref, k_ref, v_ref, a_ref, b_ref, y_ref, s_ref, s_scr):
        @pl.when(pl.program_id(0) == 0)
        def _():
            s_scr[...] = jnp.zeros_like(s_scr)

        s0 = s_scr[...]
        s_ref[0] = s0
        y, s1 = rwkv_chunk(s0, r_ref[...], lw_ref[...], k_ref[...], v_ref[...], a_ref[...], b_ref[...])
        y_ref[...] = y
        s_scr[...] = s1

    bs = pl.BlockSpec((H, C, HEAD), lambda c: (0, c, 0))
    return pl.pallas_call(
        body, name="rwkv_fwd", grid=(S // C,), in_specs=[bs] * 6,
        out_specs=[bs, pl.BlockSpec((1, H, HEAD, HEAD), lambda c: (c, 0, 0, 0))],
        out_shape=[jax.ShapeDtypeStruct((H, S, HEAD), F32), jax.ShapeDtypeStruct((S // C, H, HEAD, HEAD), F32)],
        scratch_shapes=[pltpu.VMEM((H, HEAD, HEAD), F32)],
        compiler_params=_cp(("arbitrary",)),
    )(r, lw, k, v, a, b)


def rwkv_bwd(r, lw, k, v, a, b, states, dy):
    H, S, _ = r.shape
    C = RW_CHUNK
    nc = S // C

    def body(r_ref, lw_ref, k_ref, v_ref, a_ref, b_ref, s_ref, dy_ref, dr, dlw, dk, dv, da, db, ds_scr):
        @pl.when(pl.program_id(0) == 0)
        def _():
            ds_scr[...] = jnp.zeros_like(ds_scr)

        _, vjp = jax.vjp(rwkv_chunk, s_ref[0], r_ref[...], lw_ref[...], k_ref[...], v_ref[...], a_ref[...], b_ref[...])
        grads = vjp((dy_ref[...], ds_scr[...]))
        ds_scr[...] = grads[0]
        for o, gv in zip((dr, dlw, dk, dv, da, db), grads[1:]):
            o[...] = gv

    bs = pl.BlockSpec((H, C, HEAD), lambda c: (0, nc - 1 - c, 0))
    return pl.pallas_call(
        body, name="rwkv_bwd", grid=(nc,),
        in_specs=[bs] * 6 + [pl.BlockSpec((1, H, HEAD, HEAD), lambda c: (nc - 1 - c, 0, 0, 0)), bs],
        out_specs=[bs] * 6, out_shape=[jax.ShapeDtypeStruct((H, S, HEAD), F32)] * 6,
        scratch_shapes=[pltpu.VMEM((H, HEAD, HEAD), F32)],
        compiler_params=_cp(("arbitrary",)),
    )(r, lw, k, v, a, b, states, dy)


def loss_head(y, target, tm=512):
    S = y.shape[0]

    def body(y_ref, t_ref, dy_ref, l_ref):
        e = y_ref[...] - t_ref[...]
        dy_ref[...] = e * (1.0 / D)
        part = jnp.broadcast_to(0.5 * jnp.sum(jnp.mean(e * e, axis=-1, keepdims=True)), (1, 128))

        @pl.when(pl.program_id(0) == 0)
        def _():
            l_ref[...] = part

        @pl.when(pl.program_id(0) != 0)
        def _():
            l_ref[...] += part

    tile = pl.BlockSpec((tm, D), lambda t: (t, 0))
    return pl.pallas_call(
        body, name="loss_head", grid=(S // tm,), in_specs=[tile, tile],
        out_specs=[tile, pl.BlockSpec((1, 128), lambda t: (0, 0))],
        out_shape=[jax.ShapeDtypeStruct((S, D), F32), jax.ShapeDtypeStruct((1, 128), F32)],
        compiler_params=_cp(("arbitrary",)),
    )(y, target)


def _row_tile(rows, cols, budget=1 << 19):
    best = None
    for tr in range(8, rows + 1, 8):
        if rows % tr == 0 and tr * cols <= budget:
            best = tr
    return best or rows


def _adam(w, g, m, v):
    m = ADAM_B1 * m + (1.0 - ADAM_B1) * g
    v = ADAM_B2 * v + (1.0 - ADAM_B2) * jnp.square(g)
    m_hat = m / (1.0 - ADAM_B1 ** ADAM_STEP)
    v_hat = v / (1.0 - ADAM_B2 ** ADAM_STEP)
    return -ADAM_LR * (m_hat / (jnp.sqrt(v_hat) + ADAM_EPS) + ADAM_WD * w), m, v


def sum_slots(name, parts, dtype=F32, extras=()):
    n = 0 if parts is None else parts.shape[0]
    R, C = extras[0].shape if parts is None else parts.shape[1:]
    tr = _row_tile(R, C * (n + len(extras)))
    ins = ([] if parts is None else [parts]) + list(extras)

    def body(*refs):
        terms = [] if parts is None else [refs[0][i] for i in range(n)]
        terms += [r[...] for r in refs[len(ins) - len(extras):len(ins)]]
        s = terms[0].astype(F32)
        for t in terms[1:]:
            s = s + t.astype(F32)
        refs[len(ins)][...] = s.astype(dtype)

    tile = pl.BlockSpec((tr, C), lambda t: (t, 0))
    return pl.pallas_call(
        body, name=name, grid=(R // tr,),
        in_specs=([] if parts is None else [pl.BlockSpec((n, tr, C), lambda t: (0, t, 0))]) + [tile] * len(extras),
        out_specs=tile, out_shape=jax.ShapeDtypeStruct((R, C), dtype), compiler_params=_cp(("parallel",)),
    )(*ins)


def sum_own_half(name, split, theirs, c, dtype):
    nq, _, rh, cols = split.shape
    tr = _row_tile(rh, 2 * cols)

    def body(c_ref, a_ref, b_ref, o_ref):
        o_ref[...] = (a_ref[...] + b_ref[...]).astype(dtype)

    tile = pl.BlockSpec((None, tr, cols), lambda q, t, c_ref: (q, t, 0))
    return pl.pallas_call(
        body, name=name,
        grid_spec=pltpu.PrefetchScalarGridSpec(
            num_scalar_prefetch=1, grid=(nq, rh // tr),
            in_specs=[pl.BlockSpec((None, None, tr, cols), lambda q, t, c_ref: (q, c_ref[0], t, 0)), tile],
            out_specs=tile),
        out_shape=jax.ShapeDtypeStruct((nq, rh, cols), dtype), compiler_params=_cp(("parallel", "parallel")),
    )(jnp.reshape(c, (1,)).astype(jnp.int32), split, theirs)


def sum_landed(name, landed, chip_sum, p):
    n, rh, cols = landed.shape
    tr = _row_tile(rh, (n + 1) * cols)

    def body(p_ref, l_ref, own_ref, o_ref):
        s = l_ref[0].astype(F32)
        for i in range(1, n):
            s = s + l_ref[i].astype(F32)
        o_ref[...] = s + own_ref[...].astype(F32)

    return pl.pallas_call(
        body, name=name,
        grid_spec=pltpu.PrefetchScalarGridSpec(
            num_scalar_prefetch=1, grid=(rh // tr,),
            in_specs=[pl.BlockSpec((n, tr, cols), lambda t, p_ref: (0, t, 0)),
                      pl.BlockSpec((None, tr, cols), lambda t, p_ref: (p_ref[0], t, 0))],
            out_specs=pl.BlockSpec((tr, cols), lambda t, p_ref: (t, 0))),
        out_shape=jax.ShapeDtypeStruct((rh, cols), F32), compiler_params=_cp(("parallel",)),
    )(jnp.reshape(p, (1,)).astype(jnp.int32), landed, chip_sum)


def adam_step(name, ga, gb, w, m, v):
    R, C = w.shape
    tr = _row_tile(R, C, 1 << 17)
    ins = [ga] + ([gb] if gb is not None else []) + [w, m, v]

    def body(*refs):
        g = refs[0][...]
        if gb is not None:
            g = g + refs[1][...]
        w_ref, m_ref, v_ref, g_out, d_out, m_out, v_out = refs[len(ins) - 3:]
        d, m2, v2 = _adam(w_ref[...], g, m_ref[...], v_ref[...])
        g_out[...] = g
        d_out[...] = d
        m_out[...] = m2
        v_out[...] = v2

    tile = pl.BlockSpec((tr, C), lambda t: (t, 0))
    return pl.pallas_call(
        body, name=name, grid=(R // tr,), in_specs=[tile] * len(ins), out_specs=[tile] * 4,
        out_shape=[jax.ShapeDtypeStruct((R, C), F32)] * 4, compiler_params=_cp(("parallel",)),
    )(*ins)


def adam_ffn(name, g_pieces, w, m, v, transposed=False):
    if transposed:
        res = adam_ffn(name, g_pieces, *(jnp.swapaxes(a, 2, 3) for a in (w, m, v)))
        return [jnp.swapaxes(r, 2, 3) for r in res]
    _, _, R, C = w.shape
    tr = _row_tile(R, 4 * C, 1 << 17)

    def body(g00, g01, g10, g11, w_ref, m_ref, v_ref, g_out, d_out, m_out, v_out):
        for l, j, g_ref in ((0, 0, g00), (0, 1, g01), (1, 0, g10), (1, 1, g11)):
            g = g_ref[...]
            d, m2, v2 = _adam(w_ref[l, j], g, m_ref[l, j], v_ref[l, j])
            g_out[l, j] = g
            d_out[l, j] = d
            m_out[l, j] = m2
            v_out[l, j] = v2

    piece = pl.BlockSpec((tr, C), lambda t: (t, 0))
    full = pl.BlockSpec((2, 2, tr, C), lambda t: (0, 0, t, 0))
    return pl.pallas_call(
        body, name=name, grid=(R // tr,), in_specs=[piece] * 4 + [full] * 3, out_specs=[full] * 4,
        out_shape=[jax.ShapeDtypeStruct(w.shape, F32)] * 4, compiler_params=_cp(("parallel",)),
    )(*g_pieces, w, m, v)


def _place():
    return lax.axis_index("x"), lax.axis_index("y"), lax.axis_index("c")


def _flip(me, mask):
    return tuple(1 - v if mk else v for v, mk in zip(me, mask))


CHIP_MASKS = ((1, 0, 0), (0, 1, 0), (1, 1, 0))
ALL_MASKS = tuple((a, b, c) for a in (0, 1) for b in (0, 1) for c in (0, 1) if (a, b, c) != (0, 0, 0))


def _chip(dev):
    return 2 * dev[0] + dev[1]


def _devno(dev):
    return 4 * dev[0] + 2 * dev[1] + dev[2]


class Pushes:
    def __init__(self, arrays, out_shapes, masks, copies, src_of, dst_of, alias=False):
        self.arrays, self.out_shapes, self.masks, self.copies = list(arrays), list(out_shapes), masks, copies
        self.src_of, self.dst_of, self.alias = src_of, dst_of, alias
        self.n = len(self.arrays)

    def sem_shapes(self):
        k = self.n * len(self.masks) * self.copies
        return [pltpu.SemaphoreType.DMA((k,)), pltpu.SemaphoreType.DMA((k,))]

    def ops(self, ins, outs, send_sems, recv_sems):
        me = _place()
        sends, lands = [], []
        for i in range(self.n):
            for j, mk in enumerate(self.masks):
                peer = _flip(me, mk)
                srcs, dsts = self.src_of(ins[i], me, j), self.dst_of(outs[i], me, j)
                here = self.dst_of(outs[i], peer, j)
                for q in range(self.copies):
                    sem = (i * len(self.masks) + j) * self.copies + q
                    sends.append(pltpu.make_async_remote_copy(
                        src_ref=srcs[q], dst_ref=dsts[q], send_sem=send_sems.at[sem], recv_sem=recv_sems.at[sem],
                        device_id=peer, device_id_type=MESH))
                    lands.append(pltpu.make_async_remote_copy(
                        src_ref=here[q], dst_ref=here[q], send_sem=send_sems.at[sem], recv_sem=recv_sems.at[sem],
                        device_id=peer, device_id_type=MESH))

        def start():
            for cp in sends:
                cp.start()

        def wait():
            for cp in lands:
                cp.wait_recv()
            for cp in sends:
                cp.wait_send()

        return start, wait


_HBM = pl.BlockSpec(memory_space=pl.ANY)


def exchange(name, p, local_of=None):
    n = p.n

    def body(*refs):
        ins, outs = refs[:n], refs[n:2 * n]
        start, wait = p.ops(ins, outs, refs[2 * n], refs[2 * n + 1])
        locals_ = []
        if local_of is not None:
            for i in range(n):
                src, dst = local_of(ins[i], outs[i], _place())
                locals_.append(pltpu.make_async_copy(src, dst, refs[2 * n + 2].at[i]))
                locals_[-1].start()
        start()
        wait()
        for cp in locals_:
            cp.wait()

    return pl.pallas_call(
        body, name=name, in_specs=[_HBM] * n, out_specs=[_HBM] * n, out_shape=p.out_shapes,
        scratch_shapes=p.sem_shapes() + ([pltpu.SemaphoreType.DMA((n,))] if local_of is not None else []),
        input_output_aliases={i: i for i in range(n)} if p.alias else {},
    )(*p.arrays)


def _half(c, rows):
    return pl.ds(c * (rows // 2), rows // 2)


def gather_pushes(arrays):
    outs = [jax.ShapeDtypeStruct((N_CHIPS,) + a.shape, a.dtype) for a in arrays]
    sib = len(CHIP_MASKS)
    return Pushes(arrays, outs, CHIP_MASKS + ((0, 0, 1),), 1,
                  src_of=lambda r, me, j: [r] if j == sib else [r.at[_half(me[2], r.shape[0])]],
                  dst_of=lambda o, sender, j: [o.at[_chip(sender)]] if j == sib else
                  [o.at[_chip(sender), _half(sender[2], o.shape[1])]])


def gather_swap(name, got):
    outs = [jax.ShapeDtypeStruct(a.shape, a.dtype) for a in got]
    return exchange(name, Pushes(
        got, outs, ((0, 0, 1),), len(CHIP_MASKS),
        src_of=lambda r, me, j: [r.at[_chip(_flip(me, mk)), _half(me[2], r.shape[1])] for mk in CHIP_MASKS],
        dst_of=lambda o, sender, j: [o.at[_chip(_flip(sender, mk)), _half(sender[2], o.shape[1])] for mk in CHIP_MASKS],
        alias=True))


def reduce_swap(arrays):
    split = [a.reshape(N_CHIPS, 2, a.shape[1] // 2, a.shape[2]) for a in arrays]
    half_shapes = [jax.ShapeDtypeStruct((N_CHIPS,) + a.shape[2:], F32) for a in split]
    return split, Pushes(split, half_shapes, ((0, 0, 1),), 1,
                         src_of=lambda r, me, j: [r.at[:, 1 - me[2]]], dst_of=lambda o, sender, j: [o])


def reduce_begin(tag, names, arrays, wire):
    split, pushes = reduce_swap(arrays)
    return reduce_sum(names, split, exchange(f"grad_pre_swap_{tag}", pushes), wire)


def reduce_sum(names, split, theirs, wire):
    c = lax.axis_index("c")
    chip_sum = [sum_own_half(f"sum2_{nm}", a, t, c, dt) for nm, a, t, dt in zip(names, split, theirs, wire)]
    pushes = Pushes(chip_sum, [jax.ShapeDtypeStruct((len(CHIP_MASKS),) + a.shape[1:], a.dtype) for a in chip_sum],
                    CHIP_MASKS, 1,
                    src_of=lambda r, me, j: [r.at[_chip(_flip(me, CHIP_MASKS[j]))]],
                    dst_of=lambda o, sender, j: [o.at[j]])
    return chip_sum, pushes


def reduce_end(tag, names, chip_sum, landed):
    x, y, c = _place()
    halves = [sum_landed(f"sum4_{nm}", p, a, _chip((x, y, c))) for nm, p, a in zip(names, landed, chip_sum)]
    others = exchange(f"grad_final_swap_{tag}", Pushes(
        halves, [jax.ShapeDtypeStruct(a.shape, F32) for a in halves], ((0, 0, 1),), 1,
        src_of=lambda r, me, j: [r], dst_of=lambda o, sender, j: [o]))
    return [jnp.concatenate([jnp.where(c == 0, h, o), jnp.where(c == 0, o, h)], axis=0) for h, o in zip(halves, others)]


def gather_all(arrays):
    outs = [jax.ShapeDtypeStruct((8,) + a.shape, a.dtype) for a in arrays]
    return exchange("gather_replicated", Pushes(
        arrays, outs, ALL_MASKS, 1, src_of=lambda r, me, j: [r], dst_of=lambda o, sender, j: [o.at[_devno(sender)]]),
        local_of=lambda r, o, me: (r, o.at[_devno(me)]))


def _unshard_cols(g):
    return jnp.transpose(g, (1, 0, 2)).reshape(g.shape[1], -1)


def _shard_cols(a):
    return jnp.transpose(a.reshape(a.shape[0], N_CHIPS, -1), (1, 0, 2))


class Weights(dict):
    def ride(self, kernel_name):
        return None

    def arrived(self, kernel_name, outs):
        pass


def _forward_backward(x, tgt, W, grads_early=None):
    S = x.shape[0]
    G = {}
    sd = jax.ShapeDtypeStruct

    hidden = {}

    def ffn(xin, l, j):
        out, a_sav, b_sav, *rode = ffn_fwd(xin, W["ffn_norm"][l][j], W["ffn_w_gate", l, j], W["ffn_w_up", l, j],
                                           W["ffn_w_down", l, j], l, j, W.ride(f"ffn_fwd_{l}{j}"))
        hidden[l, j] = [a_sav, b_sav]
        W.arrived(f"ffn_fwd_{l}{j}", rode)
        return out

    def ffn_back(xin, dout, l, j):
        gn = W["ffn_norm"][l][j]
        dh, G["ffn_w_gate", l, j], G["ffn_w_up", l, j], G["ffn_w_down", l, j] = ffn_bwd(
            xin, gn, W["ffn_w_gate", l, j], W["ffn_w_up", l, j], W["ffn_w_down", l, j], dout, *hidden[l, j], l, j)
        dx, G[("ffn_norm", l, j)] = norm_bwd(f"ffn_norm_bwd_{l}{j}", xin, gn, dh, dout)
        return dx

    x0 = x
    x1 = ffn(x0, 0, 0)
    g0 = W["mix_norm"][0]
    sbq, sbk, sbv = tile_fwd(f_attn_sb, "attn_in_sb", [x1], [g0, W["attn_w_in"][0]], [sd((S, SB_W), F32)] * 3, 256)
    dl_shape = sd((DL_PAIRS, S, 128), F32)
    qn, = tile_fwd(f_attn_qk, "attn_in_q", [x1], [g0, W["attn_w_in"][1], W["attn_q_norm"]], [dl_shape], 256)
    kn, = tile_fwd(f_attn_qk, "attn_in_k", [x1], [g0, W["attn_w_in"][2], W["attn_k_norm"]], [dl_shape], 256)
    vv, = tile_fwd(f_attn_v, "attn_in_v", [x1], [g0, W["attn_w_in"][3]], [dl_shape], 256)
    oa, sb_wts, *rode = sb_fwd(sbq, sbk, sbv, W.ride("sb_fwd"))
    W.arrived("sb_fwd", rode)
    qs, ks, vs = (reorder(nm, t, DIL, False) for nm, t in (("sub_q", qn), ("sub_k", kn), ("sub_v", vv)))
    o_s, lse_s, *rode = dil_fwd(qs, ks, vs, W["bias_mat"], W.ride("dil_fwd"))
    W.arrived("dil_fwd", rode)
    o_n, lse_n = reorder("nat_o", o_s, DIL, True), reorder("nat_lse", lse_s, DIL, True)
    x2, = tile_fwd(f_attn_out, "attn_out", [x1, oa, o_n, lse_n], [W["attn_w_out"]], [sd((S, D), F32)], 256)
    x3 = ffn(x2, 0, 1)
    x4 = ffn(x3, 1, 0)
    g1 = W["mix_norm"][1]
    h, hs = norm_shift_fwd(x4, g1)
    mix = W["rw_mix"]
    r, = tile_fwd(f_rw_proj, "rw_proj_r", [h, hs], [mix[0:1], W["rw_wr"]], [sd((S, D), F32)], 256)
    k, = tile_fwd(f_rw_proj, "rw_proj_k", [h, hs], [mix[2:3], W["rw_wk"]], [sd((S, D), F32)], 256)
    v, = tile_fwd(f_rw_proj, "rw_proj_v", [h, hs], [mix[3:4], W["rw_wv"]], [sd((S, D), F32)], 256)
    mix3 = jnp.concatenate([mix[1:2], mix[4:5], mix[5:6]], axis=0)
    mid_w = [mix3, W["rw_w0"], W["rw_a0"], W["rw_kk"], W["rw_ka"], W["rw_w1"], W["rw_w2"], W["rw_a1"], W["rw_a2"],
             W["rw_g1"], W["rw_g2"]]
    hshape = sd((RW_H, S, HEAD), F32)
    mid_tiles = [h, hs, r, k, v]
    rh, lwh, kh, vh, ah, bh, gate = tile_fwd(f_rw_mid, "rw_mid", mid_tiles, mid_w, [hshape] * 6 + [sd((S, D), F32)], 128)
    yh, states = rwkv_fwd(rh, lwh, kh, vh, ah, bh)
    post_w = [W["rw_lnx_g"], W["rw_lnx_b"], W["rw_rk"], W["rw_wo"]]
    post_tiles = [yh, rh, kh, vh, gate, x4]
    x5, = tile_fwd(f_rw_post, "rw_post", post_tiles, post_w, [sd((S, D), F32)], 128)
    x6 = ffn(x5, 1, 1)
    dx6, loss_part = loss_head(x6, tgt)

    dx5 = ffn_back(x5, dx6, 1, 1)
    (dyh, drh, dkh, dvh, dgate, dx4), (d_lng, d_lnb, d_rk, d_wo) = tile_bwd(
        f_rw_post, "rw_post_bwd", post_tiles, post_w, [dx5], 128, [True] * 6, [True] * 4)
    drh2, dlwh, dkh2, dvh2, dah, dbh = rwkv_bwd(rh, lwh, kh, vh, ah, bh, states, dyh)
    mid_cts = [(drh, drh2), dlwh, (dkh, dkh2), (dvh, dvh2), dah, dbh, dgate]
    (dh, dhs, dr, dk, dv), dmid_w = tile_bwd(f_rw_mid, "rw_mid_bwd", mid_tiles, mid_w, mid_cts, 128,
                                             [True] * 5, [True] * len(mid_w))
    dmix = {}
    for nm, ct, row, wname in (("r", dr, 0, "rw_wr"), ("k", dk, 2, "rw_wk"), ("v", dv, 3, "rw_wv")):
        (dh, dhs), (dmix[row], G[wname]) = tile_bwd(
            f_rw_proj, f"rw_proj_{nm}_bwd", [h, hs], [mix[row:row + 1], W[wname]], [ct], 256,
            [True, True], [True, True], acc={0: dh, 1: dhs})
    dx4, G[("mix_norm", 1)] = norm_shift_bwd(x4, g1, dh, dhs, dx4)
    dmix3 = dmid_w[0]
    G["rw_mix"] = jnp.concatenate([dmix[0], dmix3[0:1], dmix[2], dmix[3], dmix3[1:2], dmix3[2:3]], axis=0)
    for nm, gv in zip(("rw_w0", "rw_a0", "rw_kk", "rw_ka", "rw_w1", "rw_w2", "rw_a1", "rw_a2", "rw_g1", "rw_g2"), dmid_w[1:]):
        G[nm] = gv
    G["rw_lnx_g"], G["rw_lnx_b"], G["rw_rk"], G["rw_wo"] = d_lng, d_lnb, d_rk, d_wo
    dx3 = ffn_back(x3, dx4, 1, 0)
    dx2 = ffn_back(x2, dx3, 0, 1)
    (dx1, doa, do_n, dlse_n), (G["attn_w_out"],) = tile_bwd(
        f_attn_out, "attn_out_bwd", [x1, oa, o_n, lse_n], [W["attn_w_out"]], [dx2], 256, [True] * 4, [True])
    do_s, dlse_s = reorder("sub_do", do_n, DIL, False), reorder("sub_dlse", dlse_n, DIL, False)
    ride, swapped = grads_early(G) if grads_early is not None else (None, None)
    dqs, dks, dvs, dsum, *rode = dil_bwd(qs, ks, vs, W["bias_mat"], o_s, lse_s, do_s, dlse_s, ride)
    ride, landed = swapped(rode) if swapped is not None else (None, None)
    G["rel_bias"] = bias_grad(dsum, W["buckets"])
    dqn, dkn, dvv = (reorder(nm, t, DIL, True) for nm, t in (("nat_dq", dqs), ("nat_dk", dks), ("nat_dv", dvs)))
    dsbq, dsbk, dsbv, *rode = sb_bwd(sbq, sbk, sbv, doa, sb_wts, ride)
    if landed is not None:
        landed(rode)
    dg0 = []
    dwin = []
    (dx1,), (dg, dw) = tile_bwd(f_attn_sb, "attn_in_sb_bwd", [x1], [g0, W["attn_w_in"][0]], [dsbq, dsbk, dsbv], 256,
                                [True], [True, True], acc={0: dx1})
    dg0.append(dg), dwin.append(dw)
    (dx1,), (dg, dw, G["attn_q_norm"]) = tile_bwd(f_attn_qk, "attn_in_q_bwd", [x1], [g0, W["attn_w_in"][1], W["attn_q_norm"]],
                                                  [dqn], 256, [True], [True] * 3, acc={0: dx1})
    dg0.append(dg), dwin.append(dw)
    (dx1,), (dg, dw, G["attn_k_norm"]) = tile_bwd(f_attn_qk, "attn_in_k_bwd", [x1], [g0, W["attn_w_in"][2], W["attn_k_norm"]],
                                                  [dkn], 256, [True], [True] * 3, acc={0: dx1})
    dg0.append(dg), dwin.append(dw)
    (dx1,), (dg, dw) = tile_bwd(f_attn_v, "attn_in_v_bwd", [x1], [g0, W["attn_w_in"][3]], [dvv], 256,
                                [True], [True, True], acc={0: dx1})
    dg0.append(dg), dwin.append(dw)
    G[("mix_norm", 0)] = dg0
    G["attn_w_in"] = dwin
    dx0 = ffn_back(x0, dx1, 0, 0)
    return loss_part, dx0, G


VEC_ROWS = ("ffn_norm", "rw_mix", "rw_w0", "rw_a0", "rw_kk", "rw_ka", "rw_lnx_g", "rw_lnx_b")


def kernel(x, ffn_norm, ffn_w_gate, ffn_w_up, ffn_w_down, mix_norm, rel_bias, attn_w_in, attn_q_norm, attn_k_norm, attn_w_out, rw_mix, rw_w0, rw_w1, rw_w2, rw_a0, rw_a1, rw_a2, rw_g1, rw_g2, rw_kk, rw_ka, rw_rk, rw_wr, rw_wk, rw_wv, rw_wo, rw_lnx_g, rw_lnx_b, loss_target, m_ffn_norm, m_ffn_w_gate, m_ffn_w_up, m_ffn_w_down, m_mix_norm, m_rel_bias, m_attn_w_in, m_attn_q_norm, m_attn_k_norm, m_attn_w_out, m_rw_mix, m_rw_w0, m_rw_w1, m_rw_w2, m_rw_a0, m_rw_a1, m_rw_a2, m_rw_g1, m_rw_g2, m_rw_kk, m_rw_ka, m_rw_rk, m_rw_wr, m_rw_wk, m_rw_wv, m_rw_wo, m_rw_lnx_g, m_rw_lnx_b, v_ffn_norm, v_ffn_w_gate, v_ffn_w_up, v_ffn_w_down, v_mix_norm, v_rel_bias, v_attn_w_in, v_attn_q_norm, v_attn_k_norm, v_attn_w_out, v_rw_mix, v_rw_w0, v_rw_w1, v_rw_w2, v_rw_a0, v_rw_a1, v_rw_a2, v_rw_g1, v_rw_g2, v_rw_kk, v_rw_ka, v_rw_rk, v_rw_wr, v_rw_wk, v_rw_wv, v_rw_wo, v_rw_lnx_g, v_rw_lnx_b):
    names = ["ffn_norm", "ffn_w_gate", "ffn_w_up", "ffn_w_down", "mix_norm", "rel_bias", "attn_w_in", "attn_q_norm",
             "attn_k_norm", "attn_w_out", "rw_mix", "rw_w0", "rw_w1", "rw_w2", "rw_a0", "rw_a1", "rw_a2", "rw_g1", "rw_g2",
             "rw_kk", "rw_ka", "rw_rk", "rw_wr", "rw_wk", "rw_wv", "rw_wo", "rw_lnx_g", "rw_lnx_b"]
    loc = locals()
    w = {n: loc[n] for n in names}
    mom = {n: loc["m_" + n] for n in names}
    vel = {n: loc["v_" + n] for n in names}
    S = x.shape[1]

    ffn3 = ("ffn_w_gate", "ffn_w_up", "ffn_w_down")
    rw_mats = ("rw_w1", "rw_w2", "rw_a1", "rw_a2", "rw_g1", "rw_g2", "rw_wr", "rw_wk", "rw_wv", "rw_wo")
    cols_split = ("attn_w_out", "rw_w2", "rw_a2", "rw_g2")
    shard = {"vec": jnp.concatenate([w[n].reshape(-1, 256) for n in VEC_ROWS], axis=0)}
    for n in ffn3:
        for l in range(2):
            for j in range(2):
                shard[n, l, j] = w[n][l, j].astype(BF16)
    for n in ("attn_w_in", "attn_w_out") + rw_mats:
        shard[n] = w[n].reshape(-1, w[n].shape[-1]).astype(BF16)
    ffn_keys = lambda l, j: [(n, l, j) for n in ffn3]
    w_groups = {"first": ["vec"] + ffn_keys(0, 0) + ["attn_w_in", "attn_w_out"],
                "ffn_fwd_00": [("ffn_w_gate", 0, 1), ("ffn_w_up", 0, 1)],
                "sb_fwd": ffn_keys(1, 0) + list(rw_mats) + [("ffn_w_down", 0, 1), ("ffn_w_down", 1, 1)],
                "dil_fwd": [("ffn_w_gate", 1, 1), ("ffn_w_up", 1, 1)]}
    label = lambda key: key if isinstance(key, str) else f"{key[0]}_{key[1]}{key[2]}"

    class Streamed(Weights):
        def ride(self, kernel_name):
            keys_ = w_groups.get(kernel_name)
            return gather_pushes([shard[k] for k in keys_]) if keys_ else None

        def arrived(self, kernel_name, outs):
            if not outs:
                return
            for key, g in zip(w_groups[kernel_name], gather_swap(f"gather_swap_{kernel_name}", outs)):
                if key == "vec":
                    vec_full = _unshard_cols(g)
                    self["ffn_norm"] = [[vec_full[2 * l + j][None] for j in range(2)] for l in range(2)]
                    self["rw_mix"] = vec_full[4:10]
                    for i, n in enumerate(("rw_w0", "rw_a0", "rw_kk", "rw_ka", "rw_lnx_g", "rw_lnx_b")):
                        self[n] = vec_full[10 + i][None]
                elif key == "attn_w_in":
                    self[key] = [g[p] for p in range(N_CHIPS)]
                elif key in cols_split:
                    self[key] = _unshard_cols(g)
                elif isinstance(key, str):
                    self[key] = g.reshape(D, -1)
                else:
                    self[key] = g

    buckets = _bucket_maps()
    W = Streamed({"mix_norm": [mix_norm[0:1], mix_norm[1:2]], "attn_q_norm": attn_q_norm, "attn_k_norm": attn_k_norm,
                  "rw_rk": rw_rk[0][:, None, :], "buckets": buckets, "bias_mat": bias_table(rel_bias, buckets)})
    W.arrived("first", exchange("gather_weights", W.ride("first")))

    def slots(key, G):
        if key == "vec":
            rows = [G[("ffn_norm", l, j)] for l in range(2) for j in range(2)] + [G["rw_mix"]] + \
                   [G[n] for n in ("rw_w0", "rw_a0", "rw_kk", "rw_ka", "rw_lnx_g", "rw_lnx_b")]
            return _shard_cols(jnp.concatenate(rows, axis=0))
        if key == "attn_w_in":
            return jnp.stack(G[key])
        if key in cols_split:
            return _shard_cols(G[key])
        if isinstance(key, str):
            return G[key].reshape(N_CHIPS, D // N_CHIPS, -1)
        return G[key]

    g_groups = {"early": ffn_keys(1, 1) + ffn_keys(1, 0) + ffn_keys(0, 1) + list(rw_mats) + ["attn_w_out"],
                "late": ["vec", "attn_w_in"] + ffn_keys(0, 0)}
    wire = lambda keys: [F32 if k == "vec" else BF16 for k in keys]
    part = {}

    def grads_early(G):
        keys = g_groups["early"]
        names_ = [label(k) for k in keys]
        split, swap_pushes = reduce_swap([slots(k, G) for k in keys])

        def swapped(theirs):
            chip_sum, pushes = reduce_sum(names_, split, theirs, wire(keys))
            return pushes, lambda landed: part.update(zip(keys, reduce_end("early", names_, chip_sum, landed)))

        return swap_pushes, swapped

    loss_part, dx, G = _forward_backward(x[0], loss_target[0], W, grads_early)
    loss = lax.psum(loss_part[0, 0], ("x", "y", "c"))
    keys = g_groups["late"]
    chip_sum, pushes = reduce_begin("late", [label(k) for k in keys], [slots(k, G) for k in keys], wire(keys))
    part.update(zip(keys, reduce_end("late", [label(k) for k in keys], chip_sum, exchange("scatter_grads", pushes))))

    rep = jnp.concatenate([G[("mix_norm", 0)][0] + G[("mix_norm", 0)][1] + G[("mix_norm", 0)][2] + G[("mix_norm", 0)][3],
                           G[("mix_norm", 1)]], axis=0).reshape(16, 128)
    rep = jnp.concatenate([rep, G["rel_bias"], jnp.pad(G["attn_q_norm"], ((0, 0), (0, 64))),
                           jnp.pad(G["attn_k_norm"], ((0, 0), (0, 64))), G["rw_rk"].reshape(8, 128),
                           jnp.zeros((2, 128), F32)], axis=0)
    rep_sum = sum_slots("sum_replicated", gather_all([rep])[0])
    g_rep = {
        "mix_norm": rep_sum[0:16].reshape(2, D),
        "rel_bias": jnp.transpose(rep_sum[16:28, :N_BUCKETS]),
        "attn_q_norm": rep_sum[28:29, :HEAD], "attn_k_norm": rep_sum[29:30, :HEAD],
        "rw_rk": rep_sum[30:38].reshape(1, RW_H, HEAD),
    }

    out = {}

    def adam(n, ga, gb):
        shp = w[n].shape
        to2 = lambda a: a.reshape(-1, shp[-1])
        res = adam_step(f"adam_{n}", to2(ga), None if gb is None else to2(gb), to2(w[n]), to2(mom[n]), to2(vel[n]))
        out[n] = tuple(r.reshape(shp) for r in res)

    for n in ffn3:
        out[n] = tuple(adam_ffn(f"adam_{n}", [part[n, l, j] for l in range(2) for j in range(2)], w[n], mom[n], vel[n],
                                transposed=n != "ffn_w_down"))
    for n in ("attn_w_in", "attn_w_out") + rw_mats:
        adam(n, part[n], None)
    rows = {"ffn_norm": (0, 4), "rw_mix": (4, 10), "rw_w0": (10, 11), "rw_a0": (11, 12), "rw_kk": (12, 13),
            "rw_ka": (13, 14), "rw_lnx_g": (14, 15), "rw_lnx_b": (15, 16)}
    for n, (lo, hi) in rows.items():
        adam(n, part["vec"][lo:hi], None)
    for n, gv in g_rep.items():
        adam(n, gv, None)

    grads = [out[n][0] for n in names]
    deltas = [out[n][1] for n in names]
    new_m = [out[n][2] for n in names]
    new_v = [out[n][3] for n in names]
    return (loss, dx[None], *grads, *deltas, *new_m, *new_v)
```

```python
import functools
import math

import jax
import jax.numpy as jnp
from jax import lax
from jax.experimental import pallas as pl
from jax.experimental.pallas import tpu as pltpu

F32, BF16 = jnp.float32, jnp.bfloat16
MESH = pl.DeviceIdType.MESH

D = 1024
HEAD = 64
N_CHIPS = 4
FF_SHARD = 704
SB_W = 256
DL_HEADS = 12
DL_PAIRS = 6
DIL = (1, 4, 16)
QBLK = 128
N_BUCKETS = 32
MAX_DISTANCE = 2048
RW_H = 16
RW_CHUNK = 64
NORM_EPS = 1e-6
GN_EPS = 64e-5
NEG_INF = -1e30
VMEM_LIMIT = 56 * 1024 * 1024

ADAM_LR, ADAM_B1, ADAM_B2, ADAM_EPS, ADAM_WD, ADAM_STEP = 0.001, 0.9, 0.999, 1e-08, 0.01, 10


def _cp(sem):
    return pltpu.CompilerParams(dimension_semantics=sem, vmem_limit_bytes=VMEM_LIMIT)


def _dg(a, b, dims, prec=None):
    return lax.dot_general(a, b, (dims, ((), ())), precision=prec, preferred_element_type=F32)


def _bdot(a, b, dims):
    return _dg(a.astype(BF16), b.astype(BF16), dims)


@jax.custom_vjp
def mm(a, b):
    return _bdot(a, b, ((1,), (0,)))


def _mm_fwd(a, b):
    return _bdot(a, b, ((1,), (0,))), (a, b)


def _mm_bwd(res, g):
    a, b = res
    return _bdot(g, b, ((1,), (1,))), _bdot(a, g, ((0,), (0,)))


mm.defvjp(_mm_fwd, _mm_bwd)


def rms(x, g):
    return x * lax.rsqrt(jnp.mean(x * x, axis=-1, keepdims=True) + NORM_EPS) * g


def _pieces(x):
    x1 = x.astype(BF16)
    r1 = x - x1.astype(F32)
    x2 = r1.astype(BF16)
    return jnp.concatenate([x1, x2, (r1 - x2.astype(F32)).astype(BF16)], axis=-1)


def _group_sum(x, nh):
    w = x.shape[-1]
    e = (lax.broadcasted_iota(jnp.int32, (w, nh), 0) // HEAD == lax.broadcasted_iota(jnp.int32, (w, nh), 1)).astype(BF16)
    s = _dg(_pieces(x), jnp.concatenate([e, e, e], axis=0), ((1,), (0,)))
    return _dg(_pieces(s), jnp.concatenate([e, e, e], axis=1), ((1,), (1,)))


@functools.partial(jax.custom_vjp, nondiff_argnums=(1,))
def group_sum(x, nh):
    return _group_sum(x, nh)


group_sum.defvjp(lambda x, nh: (_group_sum(x, nh), None), lambda nh, _, g: (_group_sum(g, nh),))


def softplus(u):
    return jnp.maximum(u, 0.0) + jnp.log1p(jnp.exp(-jnp.abs(u)))


def to_heads(t, nh=RW_H):
    return jnp.stack([t[:, HEAD * h:HEAD * (h + 1)] for h in range(nh)])


def from_heads(t):
    return jnp.concatenate([t[h] for h in range(t.shape[0])], axis=-1)


def _tile_spec(shape, tm):
    if len(shape) == 2:
        return pl.BlockSpec((tm, shape[1]), lambda t: (t, 0))
    return pl.BlockSpec((shape[0], tm, shape[2]), lambda t: (0, t, 0))


def _full_spec(shape):
    nd = len(shape)
    return pl.BlockSpec(tuple(shape), lambda t: (0,) * nd)


def _rows(a):
    return a.shape[0] if a.ndim == 2 else a.shape[1]


def tile_fwd(f, name, tiles, weights, outs, tm):
    nt, nw = len(tiles), len(weights)

    def body(*refs):
        tv = [r[...] for r in refs[:nt]]
        wv = [r[...].astype(F32) for r in refs[nt:nt + nw]]
        res = f(*tv, *wv)
        if not isinstance(res, (tuple, list)):
            res = (res,)
        for o, v in zip(refs[nt + nw:], res):
            o[...] = v.astype(o.dtype)

    return pl.pallas_call(
        body, name=name, grid=(_rows(tiles[0]) // tm,),
        in_specs=[_tile_spec(a.shape, tm) for a in tiles] + [_full_spec(w.shape) for w in weights],
        out_specs=[_tile_spec(o.shape, tm) for o in outs],
        out_shape=list(outs),
        compiler_params=_cp(("parallel",)),
    )(*tiles, *weights)


def tile_bwd(f, name, tiles, weights, cts, tm, dt, dw, acc=None):
    acc = acc or {}
    groups = [c if isinstance(c, tuple) else (c,) for c in cts]
    cts = [a for grp in groups for a in grp]
    nt, nw, nc = len(tiles), len(weights), len(cts)
    acc_idx = sorted(acc)
    na = len(acc_idx)
    dti = [i for i in range(nt) if dt[i]]
    dwi = [i for i in range(nw) if dw[i]]

    def body(*refs):
        tv = [r[...] for r in refs[:nt]]
        wv = [r[...].astype(F32) for r in refs[nt:nt + nw]]
        crefs = list(refs[nt + nw:nt + nw + nc])
        cv = []
        for grp in groups:
            terms = [crefs.pop(0)[...] for _ in grp]
            cv.append(functools.reduce(lambda a, b: a + b, terms))
        av = {i: r[...] for i, r in zip(acc_idx, refs[nt + nw + nc:nt + nw + nc + na])}
        orefs = refs[nt + nw + nc + na:]

        def g(*diff):
            t2, w2 = list(tv), list(wv)
            for i, v in zip(dti, diff[:len(dti)]):
                t2[i] = v
            for i, v in zip(dwi, diff[len(dti):]):
                w2[i] = v
            res = f(*t2, *w2)
            return tuple(res) if isinstance(res, (tuple, list)) else (res,)

        _, vjp = jax.vjp(g, *[tv[i] for i in dti], *[wv[i] for i in dwi])
        grads = vjp(tuple(cv))
        for k, i in enumerate(dti):
            gt = grads[k]
            if i in av:
                gt = gt + av[i]
            orefs[k][...] = gt
        first = pl.program_id(0) == 0
        for k, i in enumerate(dwi):
            o = orefs[len(dti) + k]
            gw = grads[len(dti) + k]

            @pl.when(first)
            def _(o=o, gw=gw):
                o[...] = gw

            @pl.when(jnp.logical_not(first))
            def _(o=o, gw=gw):
                o[...] += gw

    out_shape = [jax.ShapeDtypeStruct(tiles[i].shape, F32) for i in dti] + \
                [jax.ShapeDtypeStruct(weights[i].shape, F32) for i in dwi]
    res = pl.pallas_call(
        body, name=name, grid=(_rows(tiles[0]) // tm,),
        in_specs=[_tile_spec(a.shape, tm) for a in tiles] + [_full_spec(w.shape) for w in weights] +
                 [_tile_spec(c.shape, tm) for c in cts] + [_tile_spec(tiles[i].shape, tm) for i in acc_idx],
        out_specs=[_tile_spec(tiles[i].shape, tm) for i in dti] + [_full_spec(weights[i].shape) for i in dwi],
        out_shape=out_shape,
        compiler_params=_cp(("arbitrary",)),
    )(*tiles, *weights, *cts, *[acc[i] for i in acc_idx])
    return list(res[:len(dti)]), list(res[len(dti):])


def _ffn_wspec(rows, cols, cfirst):
    if cfirst:
        return pl.BlockSpec((1, rows, cols), lambda c, t: (c, 0, 0))
    return pl.BlockSpec((1, rows, cols), lambda t, c: (c, 0, 0))


def ffn_fwd(x, g, wg, wu, wd, l, j, ride=None, tm=1024):
    S = x.shape[0]
    r_in, r_out, r_shape, r_scr, r_args = _ride_specs(ride)

    def body(*refs):
        t, c = pl.program_id(0), pl.program_id(1)
        (x_ref, g_ref, wg_ref, wu_ref, wd_ref, o_ref, a_ref, b_ref, h_ref, acc_ref), finish = _riding(
            ride, refs, 5, 3, (t == 0) & (c == 0), (t == S // tm - 1) & (c == N_CHIPS - 1))

        @pl.when(c == 0)
        def _():
            h_ref[...] = rms(x_ref[...], g_ref[...]).astype(BF16)
            acc_ref[...] = jnp.zeros_like(acc_ref)

        h = h_ref[...]
        a = _bdot(h, wg_ref[0], ((1,), (0,)))
        b = _bdot(h, wu_ref[0], ((1,), (0,)))
        a_ref[0] = a.astype(BF16)
        b_ref[0] = b.astype(BF16)
        y = a * jax.nn.sigmoid(a) * b
        acc_ref[...] += _bdot(y, wd_ref[0], ((1,), (0,)))

        @pl.when(c == N_CHIPS - 1)
        def _():
            o_ref[...] = x_ref[...] + 0.5 * acc_ref[...]

        finish()

    hid = pl.BlockSpec((1, tm, FF_SHARD), lambda t, c: (c, t, 0))
    return pl.pallas_call(
        body, name=f"ffn_fwd_{l}{j}", grid=(S // tm, N_CHIPS),
        in_specs=[pl.BlockSpec((tm, D), lambda t, c: (t, 0)), pl.BlockSpec((1, D), lambda t, c: (0, 0)),
                  _ffn_wspec(D, FF_SHARD, False), _ffn_wspec(D, FF_SHARD, False), _ffn_wspec(FF_SHARD, D, False)] + r_in,
        out_specs=[pl.BlockSpec((tm, D), lambda t, c: (t, 0)), hid, hid] + r_out,
        out_shape=[jax.ShapeDtypeStruct((S, D), F32)] + [jax.ShapeDtypeStruct((N_CHIPS, S, FF_SHARD), BF16)] * 2 + r_shape,
        scratch_shapes=[pltpu.VMEM((tm, D), BF16), pltpu.VMEM((tm, D), F32)] + r_scr,
        compiler_params=_cp(("arbitrary", "arbitrary")),
    )(x, g, wg, wu, wd, *r_args)


def ffn_bwd(x, g, wg, wu, wd, dout, a_sav, b_sav, l, j, tm=512):
    S = x.shape[0]

    def body(x_ref, g_ref, wg_ref, wu_ref, wd_ref, do_ref, a_ref, b_ref, dh_ref, dwg_ref, dwu_ref, dwd_ref):
        t = pl.program_id(1)
        h = rms(x_ref[...], g_ref[...]).astype(BF16)
        wgv, wuv, wdv = wg_ref[0], wu_ref[0], wd_ref[0]
        a = a_ref[0].astype(F32)
        b = b_ref[0].astype(F32)
        sig = jax.nn.sigmoid(a)
        s = a * sig
        dyd = 0.5 * do_ref[...]
        dy = _bdot(dyd, wdv, ((1,), (1,)))
        dwd = _bdot(s * b, dyd, ((0,), (0,)))
        db = dy * s
        da = dy * b * (sig * (1.0 + a * (1.0 - sig)))
        dwg = _bdot(da, h, ((0,), (0,)))
        dwu = _bdot(db, h, ((0,), (0,)))
        dh_ref[0] = (_bdot(da, wgv, ((1,), (1,))) + _bdot(db, wuv, ((1,), (1,)))).astype(dh_ref.dtype)

        @pl.when(t == 0)
        def _():
            dwg_ref[0] = dwg
            dwu_ref[0] = dwu
            dwd_ref[0] = dwd

        @pl.when(t != 0)
        def _():
            dwg_ref[0] += dwg
            dwu_ref[0] += dwu
            dwd_ref[0] += dwd

    return pl.pallas_call(
        body, name=f"ffn_bwd_{l}{j}", grid=(N_CHIPS, S // tm),
        in_specs=[pl.BlockSpec((tm, D), lambda c, t: (t, 0)), pl.BlockSpec((1, D), lambda c, t: (0, 0)),
                  _ffn_wspec(D, FF_SHARD, True), _ffn_wspec(D, FF_SHARD, True), _ffn_wspec(FF_SHARD, D, True),
                  pl.BlockSpec((tm, D), lambda c, t: (t, 0)),
                  pl.BlockSpec((1, tm, FF_SHARD), lambda c, t: (c, t, 0)), pl.BlockSpec((1, tm, FF_SHARD), lambda c, t: (c, t, 0))],
        out_specs=[pl.BlockSpec((1, tm, D), lambda c, t: (c, t, 0))] + [_ffn_wspec(FF_SHARD, D, True)] * 3,
        out_shape=[jax.ShapeDtypeStruct((N_CHIPS, S, D), BF16)] + [jax.ShapeDtypeStruct(wd.shape, F32)] * 3,
        compiler_params=_cp(("parallel", "arbitrary")),
    )(x, g, wg, wu, wd, dout, a_sav, b_sav)


def norm_bwd(name, x, g, dh_parts, dres, tm=512):
    S = x.shape[0]
    P = dh_parts.shape[0]

    def body(x_ref, g_ref, dh_ref, dr_ref, dx_ref, dg_ref):
        dh = dh_ref[0].astype(F32)
        for p in range(1, P):
            dh = dh + dh_ref[p].astype(F32)
        _, vjp = jax.vjp(rms, x_ref[...], g_ref[...])
        dx, dg = vjp(dh)
        dx_ref[...] = dr_ref[...] + dx

        @pl.when(pl.program_id(0) == 0)
        def _():
            dg_ref[...] = dg

        @pl.when(pl.program_id(0) != 0)
        def _():
            dg_ref[...] += dg

    return pl.pallas_call(
        body, name=name, grid=(S // tm,),
        in_specs=[pl.BlockSpec((tm, D), lambda t: (t, 0)), pl.BlockSpec((1, D), lambda t: (0, 0)),
                  pl.BlockSpec((P, tm, D), lambda t: (0, t, 0)), pl.BlockSpec((tm, D), lambda t: (t, 0))],
        out_specs=[pl.BlockSpec((tm, D), lambda t: (t, 0)), pl.BlockSpec((1, D), lambda t: (0, 0))],
        out_shape=[jax.ShapeDtypeStruct((S, D), F32), jax.ShapeDtypeStruct((1, D), F32)],
        compiler_params=_cp(("arbitrary",)),
    )(x, g, dh_parts, dres)


def f_attn_sb(x, g, w):
    pr = mm(rms(x, g), w)
    return pr[:, :SB_W], pr[:, SB_W:2 * SB_W], pr[:, 2 * SB_W:]


def _pairs(y):
    return jnp.stack([y[:, 128 * j:128 * (j + 1)] for j in range(DL_PAIRS)])


def f_attn_qk(x, g, w, nrm):
    pr = mm(rms(x, g), w)
    ms = group_sum(pr * pr, DL_HEADS) * (1.0 / HEAD)
    return _pairs(pr * lax.rsqrt(ms + NORM_EPS) * jnp.concatenate([nrm] * DL_HEADS, axis=1))


def f_attn_v(x, g, w):
    return _pairs(mm(rms(x, g), w))


def _masked(strict, x):
    return x if strict is None else jnp.where(strict, x, 0.0)


def _head_stack(x, dtype=BF16):
    nh = x.shape[1] // HEAD
    lane_head = lax.broadcasted_iota(jnp.int32, (1, x.shape[1]), 1) // HEAD
    return jnp.concatenate([jnp.where(lane_head == h, x, 0.0) for h in range(nh)], axis=0).astype(dtype)


def _head_pick(xs):
    nh = xs.shape[1] // HEAD
    rows = xs.shape[0] // nh
    lane_head = lax.broadcasted_iota(jnp.int32, (1, xs.shape[1]), 1) // HEAD
    out = xs[:rows]
    for h in range(1, nh):
        out = jnp.where(lane_head == h, xs[rows * h:rows * (h + 1)], out)
    return out


def _sb_tiles(qs, kblk, strict):
    z = _dg(qs, kblk, ((1,), (1,))) * (HEAD ** -0.5)
    keep = -(jnp.maximum(z, 0.0) + jnp.log(1.0 + jnp.exp(-jnp.abs(z))))
    return z, _masked(strict, keep)


def _tri(n, upper):
    r = lax.broadcasted_iota(jnp.int32, (n, n), 0)
    c = lax.broadcasted_iota(jnp.int32, (n, n), 1)
    return ((r > c) if upper else (r < c)).astype(BF16)


def _tri_sums(x, tri):
    hi, lo = _split2(x)
    return _dg(jnp.concatenate([hi, lo], axis=1), jnp.concatenate([tri, tri], axis=0), ((1,), (0,)))


SB_UNROLL = 8


def _sb_diag(tb, nh):
    r = lax.broadcasted_iota(jnp.int32, (nh * tb, tb), 0)
    return lax.broadcasted_iota(jnp.int32, (nh * tb, tb), 1) < lax.rem(r, tb)


def _sb_sweep(step, first, count, carry, direction, commit=None):
    def run(kbs, c):
        outs = []
        for kb in kbs:
            c, out = step(kb, c)
            outs.append(out)
        if commit is not None:
            for kb, out in zip(kbs, outs):
                commit(kb, out)
        return c

    pos, size = first, 1
    while size < SB_UNROLL:
        n = (count // size) % 2
        carry = lax.fori_loop(
            0, n, lambda i, c, pos=pos, size=size: run([pos + direction * u for u in range(size)], c), carry)
        pos, size = pos + direction * size * n, 2 * size
    return lax.fori_loop(
        0, count // SB_UNROLL,
        lambda g, c: run([pos + direction * (SB_UNROLL * g + u) for u in range(SB_UNROLL)], c), carry)


def _riding(ride, refs, n_in, n_out, first, last):
    if ride is None:
        return refs, lambda: None
    n = ride.n
    own = refs[:n_in] + refs[n_in + n:n_in + n + n_out] + refs[n_in + 2 * n + n_out:len(refs) - 2]
    start, wait = ride.ops(refs[n_in:n_in + n], refs[n_in + n + n_out:n_in + 2 * n + n_out], refs[-2], refs[-1])
    pl.when(first)(start)
    return own, lambda: pl.when(last)(wait)


def _ride_specs(ride):
    if ride is None:
        return [], [], [], [], []
    return [_HBM] * ride.n, [_HBM] * ride.n, ride.out_shapes, ride.sem_shapes(), ride.arrays


def sb_fwd(q, k, v, ride=None, tb=QBLK):
    S = q.shape[0]
    nh = SB_W // HEAD
    nb = S // tb
    r_in, r_out, r_shape, r_scr, r_args = _ride_specs(ride)

    def body(*refs):
        qb = pl.program_id(0)
        (q_ref, k_ref, v_ref, o_ref, w_ref), finish = _riding(ride, refs, 3, 2, qb == 0, qb == nb - 1)
        diag = _sb_diag(tb, nh)
        after_mat = _tri(tb, True)
        qs = _head_stack(q_ref[...])

        def step(kb, carry, strict):
            acc, run = carry
            rows = pl.ds(pl.multiple_of(kb * tb, tb), tb)
            z, keep = _sb_tiles(qs, k_ref[rows, :].astype(BF16), strict)
            w = _masked(strict, jnp.exp(z + keep + _tri_sums(keep, after_mat) + run)).astype(BF16)
            w_ref[0, kb] = w
            acc = acc + _dg(w, v_ref[rows, :].astype(BF16), ((1,), (0,)))
            return acc, run + jnp.sum(keep, axis=1, keepdims=True)

        init = (jnp.zeros((nh * tb, SB_W), F32), jnp.zeros((nh * tb, 1), F32))
        carry = step(qb, init, diag)
        acc, _ = _sb_sweep(lambda kb, c: (step(kb, c, None), None), qb - 1, qb, carry, -1)
        o_ref[...] = _head_pick(acc)
        finish()

    return pl.pallas_call(
        body, name="sb_fwd", grid=(S // tb,),
        in_specs=[pl.BlockSpec((tb, SB_W), lambda i: (i, 0)), pl.BlockSpec((S, SB_W), lambda i: (0, 0)),
                  pl.BlockSpec((S, SB_W), lambda i: (0, 0))] + r_in,
        out_specs=[pl.BlockSpec((tb, SB_W), lambda i: (i, 0)),
                   pl.BlockSpec((1, nb, nh * tb, tb), lambda i: (i, 0, 0, 0))] + r_out,
        out_shape=[jax.ShapeDtypeStruct((S, SB_W), F32), jax.ShapeDtypeStruct((nb, nb, nh * tb, tb), BF16)] + r_shape,
        scratch_shapes=r_scr,
        compiler_params=_cp(("arbitrary",)),
    )(q, k, v, *r_args)


def sb_bwd(q, k, v, do, wts, ride=None, tb=QBLK):
    S = q.shape[0]
    nh = SB_W // HEAD
    nb = S // tb
    scale = HEAD ** -0.5
    r_in, r_out, r_shape, r_scr, r_args = _ride_specs(ride)

    def body(*refs):
        qb = pl.program_id(0)
        (q_ref, k_ref, v_ref, do_ref, w_ref, dq_ref, dk_ref, dv_ref, g_scr), finish = _riding(
            ride, refs, 5, 3, qb == 0, qb == nb - 1)

        @pl.when(qb == 0)
        def _():
            dk_ref[...] = jnp.zeros_like(dk_ref)
            dv_ref[...] = jnp.zeros_like(dv_ref)

        diag = _sb_diag(tb, nh)
        before_mat = _tri(tb, False)
        qs = _head_stack(q_ref[...])
        dos = _head_stack(do_ref[...])

        def weights_pass(kb, carry):
            rows = pl.ds(pl.multiple_of(kb * tb, tb), tb)
            w = w_ref[0, kb]
            g_scr[kb] = _dg(dos, v_ref[rows, :].astype(BF16), ((1,), (1,))) * w.astype(F32)
            return carry, _dg(w, dos, ((0,), (0,)))

        def add_rows(ref):
            def commit(kb, val):
                ref[pl.ds(pl.multiple_of(kb * tb, tb), tb), :] += val
            return commit

        zero_run = jnp.zeros((nh * tb, 1), F32)
        _sb_sweep(weights_pass, 0, qb + 1, 0, 1, add_rows(dv_ref))

        def left_to_right(kb, carry, strict):
            dq, run = carry
            rows = pl.ds(pl.multiple_of(kb * tb, tb), tb)
            kblk = k_ref[rows, :].astype(BF16)
            gw = g_scr[kb]
            sig = jax.nn.sigmoid(_dg(qs, kblk, ((1,), (1,))) * scale)
            dkeep = _masked(strict, _dg(gw.astype(BF16), before_mat, ((1,), (0,))) + run)
            dz = ((gw * (1.0 - sig) - dkeep * sig) * scale).astype(BF16)
            dq = dq + _dg(dz, kblk, ((1,), (0,)))
            return (dq, run + jnp.sum(gw, axis=1, keepdims=True)), _dg(dz, qs, ((0,), (0,)))

        carry = _sb_sweep(lambda kb, c: left_to_right(kb, c, None), 0, qb,
                          (jnp.zeros((nh * tb, SB_W), F32), zero_run), 1, add_rows(dk_ref))
        (dq, _), dk_diag = left_to_right(qb, carry, diag)
        add_rows(dk_ref)(qb, dk_diag)
        dq_ref[...] = _head_pick(dq)
        finish()

    whole = pl.BlockSpec((S, SB_W), lambda i: (0, 0))
    blk = pl.BlockSpec((tb, SB_W), lambda i: (i, 0))
    return pl.pallas_call(
        body, name="sb_bwd", grid=(S // tb,),
        in_specs=[blk, whole, whole, blk, pl.BlockSpec((1, nb, nh * tb, tb), lambda i: (i, 0, 0, 0))] + r_in,
        out_specs=[blk, whole, whole] + r_out,
        out_shape=[jax.ShapeDtypeStruct((S, SB_W), F32)] * 3 + r_shape,
        scratch_shapes=[pltpu.VMEM((S // tb, nh * tb, tb), F32)] + r_scr,
        compiler_params=_cp(("arbitrary",)),
    )(q, k, v, do, wts, *r_args)


def reorder(name, x, groups, inverse):
    P, S, _ = x.shape

    def body(x_ref, o_ref):
        p = pl.program_id(0)
        for gi, r in enumerate(groups):
            @pl.when(p // 2 == gi)
            def _(r=r):
                L = S // r
                if r == 1:
                    o_ref[...] = x_ref[...]
                for c in range(r if r > 1 else 0):
                    if inverse:
                        o_ref[pl.ds(c, L, stride=r), :] = x_ref[c * L:(c + 1) * L, :]
                    else:
                        o_ref[c * L:(c + 1) * L, :] = x_ref[pl.ds(c, L, stride=r), :]

    slab = pl.BlockSpec((None, S, 128), lambda p: (p, 0, 0))
    return pl.pallas_call(
        body, name=name, grid=(P,), in_specs=[slab], out_specs=slab,
        out_shape=jax.ShapeDtypeStruct(x.shape, x.dtype), compiler_params=_cp(("parallel",)),
    )(x)


def _dil_blocks(S):
    return S // QBLK


def _dil_mask4(n_in_stream):
    qi = lax.rem(lax.broadcasted_iota(jnp.int32, (4 * QBLK, 2 * QBLK), 0), QBLK)
    kj = lax.broadcasted_iota(jnp.int32, (4 * QBLK, 2 * QBLK), 1) - QBLK
    dist = qi - kj
    return (dist >= 0) & (dist <= QBLK) & ((n_in_stream > 0) | (kj >= 0))


def _dil_lanes(ref):
    return jnp.concatenate([ref[0], ref[1]], axis=1)


def _dil_window(prev_ref, cur_ref):
    return jnp.concatenate([_dil_lanes(prev_ref), _dil_lanes(cur_ref)], axis=0).astype(BF16)


def _stream_pos(gi, i, S):
    nb = jnp.where(gi == 0, S // (QBLK * DIL[0]), jnp.where(gi == 1, S // (QBLK * DIL[1]), S // (QBLK * DIL[2])))
    return i % nb


def dil_fwd(q, k, v, bias, ride=None):
    S = q.shape[1]
    nblk = _dil_blocks(S)
    r_in, r_out, r_shape, r_scr, r_args = _ride_specs(ride)

    def body(*refs):
        gi, i = pl.program_id(0), pl.program_id(1)
        (q_ref, kc_ref, kp_ref, vc_ref, vp_ref, b_ref, o_ref, l_ref), finish = _riding(
            ride, refs, 6, 2, (gi == 0) & (i == 0), (gi == len(DIL) - 1) & (i == nblk - 1))
        mask = _dil_mask4(_stream_pos(gi, i, S))
        kw, vw = _dil_window(kp_ref, kc_ref), _dil_window(vp_ref, vc_ref)
        lg = _dg(_head_stack(_dil_lanes(q_ref)), kw, ((1,), (1,))) * (HEAD ** -0.5) + \
            b_ref[...].reshape(4 * QBLK, 2 * QBLK)
        lg = jnp.where(mask, lg, NEG_INF)
        m = jnp.max(lg, axis=-1, keepdims=True)
        p = jnp.exp(lg - m)
        den = jnp.sum(p, axis=-1, keepdims=True)
        o = _head_pick(_dg((p / den).astype(BF16), vw, ((1,), (0,))))
        lse = _head_pick(jnp.broadcast_to(m + jnp.log(den), (4 * QBLK, 4 * HEAD)))
        for j in range(2):
            o_ref[j] = o[:, 128 * j:128 * (j + 1)]
            l_ref[j] = lse[:, 128 * j:128 * (j + 1)]
        finish()

    cur = pl.BlockSpec((2, QBLK, 128), lambda g, i: (g, i, 0))
    prev = pl.BlockSpec((2, QBLK, 128), lambda g, i: (g, jnp.maximum(i - 1, 0), 0))
    return pl.pallas_call(
        body, name="dil_fwd", grid=(len(DIL), nblk),
        in_specs=[cur, cur, prev, cur, prev, pl.BlockSpec((4, QBLK, 2 * QBLK), lambda g, i: (g, 0, 0))] + r_in,
        out_specs=[cur, cur] + r_out,
        out_shape=[jax.ShapeDtypeStruct(q.shape, F32)] * 2 + r_shape,
        scratch_shapes=r_scr,
        compiler_params=_cp(("arbitrary", "arbitrary")),
    )(q, k, k, v, v, bias, *r_args)


def dil_bwd(q, k, v, bias, o, lse, do, dlse, ride=None):
    S = q.shape[1]
    nblk = _dil_blocks(S)
    r_in, r_out, r_shape, r_scr, r_args = _ride_specs(ride)

    def body(*refs):
        gi, i = pl.program_id(0), pl.program_id(1)
        (q_ref, kc_ref, kp_ref, vc_ref, vp_ref, b_ref, o_ref, l_ref, do_ref, dl_ref,
         dq_ref, dk_ref, dv_ref, ds_ref, dk_car, dv_car), finish = _riding(
            ride, refs, 10, 4, (gi == 0) & (i == 0), (gi == len(DIL) - 1) & (i == nblk))

        @pl.when(i == 0)
        def _():
            ds_ref[...] = jnp.zeros_like(ds_ref)
            dk_car[...] = jnp.zeros_like(dk_car)
            dv_car[...] = jnp.zeros_like(dv_car)

        @pl.when(i < nblk)
        def _():
            mask = _dil_mask4(_stream_pos(gi, i, S))
            kw, vw = _dil_window(kp_ref, kc_ref), _dil_window(vp_ref, vc_ref)
            qs = _head_stack(_dil_lanes(q_ref))
            do_nat = _dil_lanes(do_ref)
            dos = _head_stack(do_nat, F32)
            lse = jnp.sum(_head_stack(_dil_lanes(l_ref), F32), axis=-1, keepdims=True) * (1.0 / HEAD)
            lg = _dg(qs, kw, ((1,), (1,))) * (HEAD ** -0.5) + b_ref[...].reshape(4 * QBLK, 2 * QBLK)
            p = jnp.where(mask, jnp.exp(lg - lse), 0.0)
            dp = _dg(dos.astype(BF16), vw, ((1,), (1,)))
            four = lambda t: jnp.concatenate([t] * 4, axis=0)
            delta = jnp.sum(dos * four(_dil_lanes(o_ref)), axis=-1, keepdims=True)
            dl = jnp.sum(_head_stack(_dil_lanes(dl_ref), F32), axis=-1, keepdims=True)
            ds = p * (dp - delta + dl)
            ds_ref[...] += ds.reshape(4, QBLK, 2 * QBLK)
            dsq = (ds * (HEAD ** -0.5)).astype(BF16)
            dq = _head_pick(_dg(dsq, kw, ((1,), (0,))))
            dkw = _dg(dsq, qs, ((0,), (0,)))
            dvw = _dg(p.astype(BF16), dos.astype(BF16), ((0,), (0,)))
            for j in range(2):
                lanes = slice(128 * j, 128 * (j + 1))
                dq_ref[j] = dq[:, lanes]
                dk_ref[j] = dk_car[j] + dkw[:QBLK, lanes]
                dv_ref[j] = dv_car[j] + dvw[:QBLK, lanes]
                dk_car[j] = dkw[QBLK:, lanes]
                dv_car[j] = dvw[QBLK:, lanes]

        @pl.when(i == nblk)
        def _():
            dk_ref[...] = dk_car[...]
            dv_ref[...] = dv_car[...]

        finish()

    cur = pl.BlockSpec((2, QBLK, 128), lambda g, i: (g, jnp.minimum(i, nblk - 1), 0))
    prev = pl.BlockSpec((2, QBLK, 128), lambda g, i: (g, jnp.clip(i - 1, 0, nblk - 1), 0))
    bspec = pl.BlockSpec((4, QBLK, 2 * QBLK), lambda g, i: (g, 0, 0))
    return pl.pallas_call(
        body, name="dil_bwd", grid=(len(DIL), nblk + 1),
        in_specs=[cur, cur, prev, cur, prev, bspec, cur, cur, cur, cur] + r_in,
        out_specs=[cur, prev, prev, bspec] + r_out,
        out_shape=[jax.ShapeDtypeStruct(q.shape, F32)] * 3 + [jax.ShapeDtypeStruct(bias.shape, F32)] + r_shape,
        scratch_shapes=[pltpu.VMEM((2, QBLK, 128), F32), pltpu.VMEM((2, QBLK, 128), F32)] + r_scr,
        compiler_params=_cp(("arbitrary", "arbitrary")),
    )(q, k, k, v, v, bias, o, lse, do, dlse, *r_args)


def _t5_bucket(dist):
    max_exact = N_BUCKETS // 2
    d = jnp.maximum(dist, 1).astype(F32)
    large = max_exact + (jnp.log(d / max_exact) / math.log(MAX_DISTANCE / max_exact)
                         * (N_BUCKETS - max_exact)).astype(jnp.int32)
    large = jnp.minimum(large, N_BUCKETS - 1)
    return jnp.where(dist < max_exact, dist, large)


def _bucket_maps():
    qi = jnp.arange(QBLK)[:, None]
    kj = jnp.arange(2 * QBLK)[None, :] - QBLK
    dist = jnp.maximum(qi - kj, 0)
    return jnp.stack([_t5_bucket(dist * r) for r in DIL])


def bias_table(rel_bias, buckets):
    def body(tbl_ref, bk_ref, o_ref):
        for h in range(DL_HEADS):
            bk = bk_ref[h // 4]

            def step(b, acc):
                return jnp.where(bk == b, tbl_ref[b, h], acc)

            o_ref[h] = lax.fori_loop(0, N_BUCKETS, step, jnp.zeros(bk.shape, F32))

    return pl.pallas_call(
        body, name="bias_table", out_shape=jax.ShapeDtypeStruct((DL_HEADS,) + buckets.shape[1:], F32),
        in_specs=[pl.BlockSpec(memory_space=pltpu.SMEM), pl.BlockSpec(memory_space=pltpu.VMEM)],
        out_specs=pl.BlockSpec(memory_space=pltpu.VMEM),
    )(rel_bias, buckets)


def bias_grad(ds, buckets):
    def body(ds_ref, bk_ref, o_ref):
        lane = lax.broadcasted_iota(jnp.int32, (1, 128), 1)
        for h in range(DL_HEADS):
            dsv = ds_ref[h]
            bk = bk_ref[h // 4]

            def step(b, row):
                return jnp.where(lane == b, jnp.sum(jnp.where(bk == b, dsv, 0.0)), row)

            o_ref[h:h + 1, :] = lax.fori_loop(0, N_BUCKETS, step, jnp.zeros((1, 128), F32))

    return pl.pallas_call(
        body, name="bias_grad", out_shape=jax.ShapeDtypeStruct((DL_HEADS, 128), F32),
        in_specs=[pl.BlockSpec(memory_space=pltpu.VMEM)] * 2, out_specs=pl.BlockSpec(memory_space=pltpu.VMEM),
    )(ds, buckets)


def f_attn_out(x, oa, o, lse, w):
    og = [jnp.concatenate([o[2 * g], o[2 * g + 1]], axis=1) for g in range(3)]
    lg = [jnp.concatenate([lse[2 * g], lse[2 * g + 1]], axis=1) for g in range(3)]
    m = jnp.maximum(jnp.maximum(lg[0], lg[1]), lg[2])
    e = [jnp.exp(l - m) for l in lg]
    den = e[0] + e[1] + e[2]
    ob = (e[0] * og[0] + e[1] * og[1] + e[2] * og[2]) / den
    return x + mm(jnp.concatenate([oa, ob], axis=1), w)


def norm_shift_fwd(x, g, tm=256):
    S = x.shape[0]

    def body(x_ref, xp_ref, g_ref, h_ref, hs_ref):
        h = rms(x_ref[...], g_ref[...])
        hp = rms(xp_ref[7:8, :], g_ref[...])
        hp = jnp.where(pl.program_id(0) == 0, 0.0, hp)
        row = lax.broadcasted_iota(jnp.int32, (tm, D), 0)
        h_ref[...] = h
        hs_ref[...] = jnp.where(row == 0, hp, pltpu.roll(h, 1, 0))

    return pl.pallas_call(
        body, name="rw_norm_shift", grid=(S // tm,),
        in_specs=[pl.BlockSpec((tm, D), lambda t: (t, 0)),
                  pl.BlockSpec((8, D), lambda t: (jnp.maximum(t * (tm // 8) - 1, 0), 0)),
                  pl.BlockSpec((1, D), lambda t: (0, 0))],
        out_specs=[pl.BlockSpec((tm, D), lambda t: (t, 0))] * 2,
        out_shape=[jax.ShapeDtypeStruct((S, D), F32)] * 2,
        compiler_params=_cp(("parallel",)),
    )(x, x, g)


def norm_shift_bwd(x, g, dh, dhs, dres, tm=256):
    S = x.shape[0]
    nt = S // tm

    def body(x_ref, g_ref, dh_ref, dhs_ref, dhn_ref, dr_ref, dx_ref, dg_ref):
        t = pl.program_id(0)
        nxt = jnp.where(t == nt - 1, 0.0, dhn_ref[0:1, :])
        row = lax.broadcasted_iota(jnp.int32, (tm, D), 0)
        tot = dh_ref[...] + jnp.where(row == tm - 1, nxt, pltpu.roll(dhs_ref[...], tm - 1, 0))
        _, vjp = jax.vjp(rms, x_ref[...], g_ref[...])
        dx, dg = vjp(tot)
        dx_ref[...] = dr_ref[...] + dx

        @pl.when(t == 0)
        def _():
            dg_ref[...] = dg

        @pl.when(t != 0)
        def _():
            dg_ref[...] += dg

    tile = pl.BlockSpec((tm, D), lambda t: (t, 0))
    return pl.pallas_call(
        body, name="rw_norm_shift_bwd", grid=(nt,),
        in_specs=[tile, pl.BlockSpec((1, D), lambda t: (0, 0)), tile, tile,
                  pl.BlockSpec((8, D), lambda t: (jnp.minimum((t + 1) * (tm // 8), S // 8 - 1), 0)), tile],
        out_specs=[tile, pl.BlockSpec((1, D), lambda t: (0, 0))],
        out_shape=[jax.ShapeDtypeStruct((S, D), F32), jax.ShapeDtypeStruct((1, D), F32)],
        compiler_params=_cp(("arbitrary",)),
    )(x, g, dh, dhs, dhs, dres)


def f_rw_proj(h, hs, mix, w):
    return mm(h + (hs - h) * mix, w)


def f_rw_proj3(h, hs, mix_r, mix_k, mix_v, wr, wk, wv):
    return f_rw_proj(h, hs, mix_r, wr), f_rw_proj(h, hs, mix_k, wk), f_rw_proj(h, hs, mix_v, wv)


def f_rw_mid(h, hs, r, k, v, mix3, w0, a0, kkw, kaw, w1, w2, a1, a2, g1, g2):
    xx = hs - h
    xw, xa, xg = h + xx * mix3[0:1], h + xx * mix3[1:2], h + xx * mix3[2:3]
    w_log = -softplus(-(w0 + mm(jnp.tanh(mm(xw, w1)), w2))) - 0.5
    lw = -jnp.exp(w_log)
    ag = jax.nn.sigmoid(a0 + mm(mm(xa, a1), a2))
    gate = mm(jax.nn.sigmoid(mm(xg, g1)), g2)
    kk = k * kkw
    kk = kk / jnp.maximum(jnp.sqrt(group_sum(kk * kk, RW_H)), 1e-12)
    kmod = k * (1.0 + (ag - 1.0) * kaw)
    return (to_heads(r), to_heads(lw), to_heads(kmod), to_heads(v), to_heads(-kk), to_heads(kk * ag), gate)


def f_rw_post(yh, rh, kh, vh, gate, x, lng, lnb, rk, wo):
    mu = jnp.mean(yh, axis=-1, keepdims=True)
    var = jnp.mean(jnp.square(yh - mu), axis=-1, keepdims=True)
    yn = (yh - mu) * lax.rsqrt(var + GN_EPS)
    bonus = jnp.sum(rh * kh * rk, axis=-1, keepdims=True) * vh
    y = from_heads(yn) * lng + lnb + from_heads(bonus)
    return x + mm(y * gate, wo)


def _split2(x):
    hi = x.astype(BF16)
    return hi, (x - hi.astype(F32)).astype(BF16)


def _b3(x, y, cx, cy):
    xh, xl = _split2(x)
    yh, yl = _split2(y)
    x3 = jnp.concatenate([xh, xh, xl], axis=cx)
    y3 = jnp.concatenate([yh, yl, yh], axis=cy)
    return lax.dot_general(x3, y3, (((cx,), (cy,)), ((0,), (0,))), preferred_element_type=F32)


@jax.custom_vjp
def b_nt(x, y):
    return _b3(x, y, 2, 2)


@jax.custom_vjp
def b_nn(x, y):
    return _b3(x, y, 2, 1)


@jax.custom_vjp
def b_tn(x, y):
    return _b3(x, y, 1, 1)


def _b1(x, y, cx, cy):
    return lax.dot_general(x.astype(BF16), y.astype(BF16), (((cx,), (cy,)), ((0,), (0,))), preferred_element_type=F32)


b_nt.defvjp(lambda x, y: (b_nt(x, y), (x, y)), lambda r, g: (_b1(g, r[1], 2, 1), _b1(g, r[0], 1, 1)))
b_nn.defvjp(lambda x, y: (b_nn(x, y), (x, y)), lambda r, g: (_b1(g, r[1], 2, 2), _b1(r[0], g, 1, 1)))
b_tn.defvjp(lambda x, y: (b_tn(x, y), (x, y)), lambda r, g: (_b1(r[1], g, 2, 2), _b1(r[0], g, 2, 1)))


def _tri_apply(x, lower):
    H, C, _ = x.shape
    ii = lax.broadcasted_iota(jnp.int32, (C, C), 0)
    jj = lax.broadcasted_iota(jnp.int32, (C, C), 1)
    m = jnp.broadcast_to(((jj <= ii) if lower else (jj >= ii)).astype(BF16), (H, C, C))
    x1 = x.astype(BF16)
    r1 = x - x1.astype(F32)
    x2 = r1.astype(BF16)
    x3 = (r1 - x2.astype(F32)).astype(BF16)
    return lax.dot_general(jnp.concatenate([m, m, m], axis=2), jnp.concatenate([x1, x2, x3], axis=1),
                           (((2,), (1,)), ((0,), (0,))), preferred_element_type=F32)


@jax.custom_vjp
def run_sum(x):
    return _tri_apply(x, True)


run_sum.defvjp(lambda x: (run_sum(x), None), lambda _, g: (_tri_apply(g, False),))


def rwkv_chunk(S0, r, lw, k, v, a, b):
    H, C, _ = r.shape
    V = S0.shape[1]
    ii = lax.broadcasted_iota(jnp.int32, (C, C), 0)
    jj = lax.broadcasted_iota(jnp.int32, (C, C), 1)
    strict = jj < ii
    i2 = lax.broadcasted_iota(jnp.int32, (C, 2 * C), 0)
    j2 = lax.broadcasted_iota(jnp.int32, (C, 2 * C), 1)
    incl2 = jnp.where(j2 >= C, j2 - C, j2) <= i2
    g = run_sum(lw)
    ig = jnp.exp(-g)
    ar = jnp.concatenate([a * jnp.exp(g - lw), r * jnp.exp(g)], axis=1)
    bk = jnp.concatenate([b * ig, k * ig], axis=1)
    m = b_nt(ar, bk)
    a_ab = jnp.where(strict, m[:, :C, :C], 0.0)
    a_ak = jnp.where(strict, m[:, :C, C:], 0.0)
    b_r = jnp.where(incl2, m[:, C:, :], 0.0)
    p = b_nt(ar, S0)
    u = p[:, :C] + b_nn(a_ak, v)
    nmat, n = a_ab, 1
    while n < C:
        n *= 2
        if n < C:
            z = b_nn(nmat, jnp.concatenate([u, nmat], axis=2))
            u, nmat = u + z[:, :, :V], z[:, :, V:]
        else:
            u = u + b_nn(nmat, u)
    uv = jnp.concatenate([u, v], axis=1)
    y = p[:, C:] + b_nn(b_r, uv)
    g_end = g[:, C - 1:C, :]
    dec = jnp.exp(g_end - g)
    s_new = S0 * jnp.exp(g_end) + b_tn(uv, jnp.concatenate([b * dec, k * dec], axis=1))
    return y, s_new


def rwkv_fwd(r, lw, k, v, a, b):
    H, S, _ = r.shape
    C = RW_CHUNK

    def body(r_ref, lw_ref, k_ref, v_ref, a_ref, b_ref, y_ref, s_ref, s_scr):
        @pl.when(pl.program_id(0) == 0)
        def _():
            s_scr[...] = jnp.zeros_like(s_scr)

        s0 = s_scr[...]
        s_ref[0] = s0
        y, s1 = rwkv_chunk(s0, r_ref[...], lw_ref[...], k_ref[...], v_ref[...], a_ref[...], b_ref[...])
        y_ref[...] = y
        s_scr[...] = s1

    bs = pl.BlockSpec((H, C, HEAD), lambda c: (0, c, 0))
    return pl.pallas_call(
        body, name="rwkv_fwd", grid=(S // C,), in_specs=[bs] * 6,
        out_specs=[bs, pl.BlockSpec((1, H, HEAD, HEAD), lambda c: (c, 0, 0, 0))],
        out_shape=[jax.ShapeDtypeStruct((H, S, HEAD), F32), jax.ShapeDtypeStruct((S // C, H, HEAD, HEAD), F32)],
        scratch_shapes=[pltpu.VMEM((H, HEAD, HEAD), F32)],
        compiler_params=_cp(("arbitrary",)),
    )(r, lw, k, v, a, b)


def rwkv_bwd(r, lw, k, v, a, b, states, dy):
    H, S, _ = r.shape
    C = RW_CHUNK
    nc = S // C

    def body(r_ref, lw_ref, k_ref, v_ref, a_ref, b_ref, s_ref, dy_ref, dr, dlw, dk, dv, da, db, ds_scr):
        @pl.when(pl.program_id(0) == 0)
        def _():
            ds_scr[...] = jnp.zeros_like(ds_scr)

        _, vjp = jax.vjp(rwkv_chunk, s_ref[0], r_ref[...], lw_ref[...], k_ref[...], v_ref[...], a_ref[...], b_ref[...])
        grads = vjp((dy_ref[...], ds_scr[...]))
        ds_scr[...] = grads[0]
        for o, gv in zip((dr, dlw, dk, dv, da, db), grads[1:]):
            o[...] = gv

    bs = pl.BlockSpec((H, C, HEAD), lambda c: (0, nc - 1 - c, 0))
    return pl.pallas_call(
        body, name="rwkv_bwd", grid=(nc,),
        in_specs=[bs] * 6 + [pl.BlockSpec((1, H, HEAD, HEAD), lambda c: (nc - 1 - c, 0, 0, 0)), bs],
        out_specs=[bs] * 6, out_shape=[jax.ShapeDtypeStruct((H, S, HEAD), F32)] * 6,
        scratch_shapes=[pltpu.VMEM((H, HEAD, HEAD), F32)],
        compiler_params=_cp(("arbitrary",)),
    )(r, lw, k, v, a, b, states, dy)


def loss_head(y, target, tm=512):
    S = y.shape[0]

    def body(y_ref, t_ref, dy_ref, l_ref):
        e = y_ref[...] - t_ref[...]
        dy_ref[...] = e * (1.0 / D)
        part = jnp.broadcast_to(0.5 * jnp.sum(jnp.mean(e * e, axis=-1, keepdims=True)), (1, 128))

        @pl.when(pl.program_id(0) == 0)
        def _():
            l_ref[...] = part

        @pl.when(pl.program_id(0) != 0)
        def _():
            l_ref[...] += part

    tile = pl.BlockSpec((tm, D), lambda t: (t, 0))
    return pl.pallas_call(
        body, name="loss_head", grid=(S // tm,), in_specs=[tile, tile],
        out_specs=[tile, pl.BlockSpec((1, 128), lambda t: (0, 0))],
        out_shape=[jax.ShapeDtypeStruct((S, D), F32), jax.ShapeDtypeStruct((1, 128), F32)],
        compiler_params=_cp(("arbitrary",)),
    )(y, target)


def _row_tile(rows, cols, budget=1 << 19):
    best = None
    for tr in range(8, rows + 1, 8):
        if rows % tr == 0 and tr * cols <= budget:
            best = tr
    return best or rows


def _adam(w, g, m, v):
    m = ADAM_B1 * m + (1.0 - ADAM_B1) * g
    v = ADAM_B2 * v + (1.0 - ADAM_B2) * jnp.square(g)
    m_hat = m / (1.0 - ADAM_B1 ** ADAM_STEP)
    v_hat = v / (1.0 - ADAM_B2 ** ADAM_STEP)
    return -ADAM_LR * (m_hat / (jnp.sqrt(v_hat) + ADAM_EPS) + ADAM_WD * w), m, v


def sum_slots(name, parts, dtype=F32, extras=()):
    n = 0 if parts is None else parts.shape[0]
    R, C = extras[0].shape if parts is None else parts.shape[1:]
    tr = _row_tile(R, C * (n + len(extras)))
    ins = ([] if parts is None else [parts]) + list(extras)

    def body(*refs):
        terms = [] if parts is None else [refs[0][i] for i in range(n)]
        terms += [r[...] for r in refs[len(ins) - len(extras):len(ins)]]
        s = terms[0].astype(F32)
        for t in terms[1:]:
            s = s + t.astype(F32)
        refs[len(ins)][...] = s.astype(dtype)

    tile = pl.BlockSpec((tr, C), lambda t: (t, 0))
    return pl.pallas_call(
        body, name=name, grid=(R // tr,),
        in_specs=([] if parts is None else [pl.BlockSpec((n, tr, C), lambda t: (0, t, 0))]) + [tile] * len(extras),
        out_specs=tile, out_shape=jax.ShapeDtypeStruct((R, C), dtype), compiler_params=_cp(("parallel",)),
    )(*ins)


def sum_own_half(name, split, theirs, c, dtype):
    nq, _, rh, cols = split.shape
    tr = _row_tile(rh, 2 * cols)

    def body(c_ref, a_ref, b_ref, o_ref):
        o_ref[...] = (a_ref[...] + b_ref[...]).astype(dtype)

    tile = pl.BlockSpec((None, tr, cols), lambda q, t, c_ref: (q, t, 0))
    return pl.pallas_call(
        body, name=name,
        grid_spec=pltpu.PrefetchScalarGridSpec(
            num_scalar_prefetch=1, grid=(nq, rh // tr),
            in_specs=[pl.BlockSpec((None, None, tr, cols), lambda q, t, c_ref: (q, c_ref[0], t, 0)), tile],
            out_specs=tile),
        out_shape=jax.ShapeDtypeStruct((nq, rh, cols), dtype), compiler_params=_cp(("parallel", "parallel")),
    )(jnp.reshape(c, (1,)).astype(jnp.int32), split, theirs)


def sum_landed(name, landed, chip_sum, p):
    n, rh, cols = landed.shape
    tr = _row_tile(rh, (n + 1) * cols)

    def body(p_ref, l_ref, own_ref, o_ref):
        s = l_ref[0].astype(F32)
        for i in range(1, n):
            s = s + l_ref[i].astype(F32)
        o_ref[...] = s + own_ref[...].astype(F32)

    return pl.pallas_call(
        body, name=name,
        grid_spec=pltpu.PrefetchScalarGridSpec(
            num_scalar_prefetch=1, grid=(rh // tr,),
            in_specs=[pl.BlockSpec((n, tr, cols), lambda t, p_ref: (0, t, 0)),
                      pl.BlockSpec((None, tr, cols), lambda t, p_ref: (p_ref[0], t, 0))],
            out_specs=pl.BlockSpec((tr, cols), lambda t, p_ref: (t, 0))),
        out_shape=jax.ShapeDtypeStruct((rh, cols), F32), compiler_params=_cp(("parallel",)),
    )(jnp.reshape(p, (1,)).astype(jnp.int32), landed, chip_sum)


def adam_step(name, ga, gb, w, m, v):
    R, C = w.shape
    tr = _row_tile(R, C, 1 << 17)
    ins = [ga] + ([gb] if gb is not None else []) + [w, m, v]

    def body(*refs):
        g = refs[0][...]
        if gb is not None:
            g = g + refs[1][...]
        w_ref, m_ref, v_ref, g_out, d_out, m_out, v_out = refs[len(ins) - 3:]
        d, m2, v2 = _adam(w_ref[...], g, m_ref[...], v_ref[...])
        g_out[...] = g
        d_out[...] = d
        m_out[...] = m2
        v_out[...] = v2

    tile = pl.BlockSpec((tr, C), lambda t: (t, 0))
    return pl.pallas_call(
        body, name=name, grid=(R // tr,), in_specs=[tile] * len(ins), out_specs=[tile] * 4,
        out_shape=[jax.ShapeDtypeStruct((R, C), F32)] * 4, compiler_params=_cp(("parallel",)),
    )(*ins)


def adam_ffn(name, g_pieces, w, m, v, transposed=False):
    if transposed:
        res = adam_ffn(name, g_pieces, *(jnp.swapaxes(a, 2, 3) for a in (w, m, v)))
        return [jnp.swapaxes(r, 2, 3) for r in res]
    _, _, R, C = w.shape
    tr = _row_tile(R, 4 * C, 1 << 17)

    def body(g00, g01, g10, g11, w_ref, m_ref, v_ref, g_out, d_out, m_out, v_out):
        for l, j, g_ref in ((0, 0, g00), (0, 1, g01), (1, 0, g10), (1, 1, g11)):
            g = g_ref[...]
            d, m2, v2 = _adam(w_ref[l, j], g, m_ref[l, j], v_ref[l, j])
            g_out[l, j] = g
            d_out[l, j] = d
            m_out[l, j] = m2
            v_out[l, j] = v2

    piece = pl.BlockSpec((tr, C), lambda t: (t, 0))
    full = pl.BlockSpec((2, 2, tr, C), lambda t: (0, 0, t, 0))
    return pl.pallas_call(
        body, name=name, grid=(R // tr,), in_specs=[piece] * 4 + [full] * 3, out_specs=[full] * 4,
        out_shape=[jax.ShapeDtypeStruct(w.shape, F32)] * 4, compiler_params=_cp(("parallel",)),
    )(*g_pieces, w, m, v)


def _place():
    return lax.axis_index("x"), lax.axis_index("y"), lax.axis_index("c")


def _flip(me, mask):
    return tuple(1 - v if mk else v for v, mk in zip(me, mask))


CHIP_MASKS = ((1, 0, 0), (0, 1, 0), (1, 1, 0))
ALL_MASKS = tuple((a, b, c) for a in (0, 1) for b in (0, 1) for c in (0, 1) if (a, b, c) != (0, 0, 0))


def _chip(dev):
    return 2 * dev[0] + dev[1]


def _devno(dev):
    return 4 * dev[0] + 2 * dev[1] + dev[2]


class Pushes:
    def __init__(self, arrays, out_shapes, masks, copies, src_of, dst_of, alias=False):
        self.arrays, self.out_shapes, self.masks, self.copies = list(arrays), list(out_shapes), masks, copies
        self.src_of, self.dst_of, self.alias = src_of, dst_of, alias
        self.n = len(self.arrays)

    def sem_shapes(self):
        k = self.n * len(self.masks) * self.copies
        return [pltpu.SemaphoreType.DMA((k,)), pltpu.SemaphoreType.DMA((k,))]

    def ops(self, ins, outs, send_sems, recv_sems):
        me = _place()
        sends, lands = [], []
        for i in range(self.n):
            for j, mk in enumerate(self.masks):
                peer = _flip(me, mk)
                srcs, dsts = self.src_of(ins[i], me, j), self.dst_of(outs[i], me, j)
                here = self.dst_of(outs[i], peer, j)
                for q in range(self.copies):
                    sem = (i * len(self.masks) + j) * self.copies + q
                    sends.append(pltpu.make_async_remote_copy(
                        src_ref=srcs[q], dst_ref=dsts[q], send_sem=send_sems.at[sem], recv_sem=recv_sems.at[sem],
                        device_id=peer, device_id_type=MESH))
                    lands.append(pltpu.make_async_remote_copy(
                        src_ref=here[q], dst_ref=here[q], send_sem=send_sems.at[sem], recv_sem=recv_sems.at[sem],
                        device_id=peer, device_id_type=MESH))

        def start():
            for cp in sends:
                cp.start()

        def wait():
            for cp in lands:
                cp.wait_recv()
            for cp in sends:
                cp.wait_send()

        return start, wait


_HBM = pl.BlockSpec(memory_space=pl.ANY)


def exchange(name, p, local_of=None):
    n = p.n

    def body(*refs):
        ins, outs = refs[:n], refs[n:2 * n]
        start, wait = p.ops(ins, outs, refs[2 * n], refs[2 * n + 1])
        locals_ = []
        if local_of is not None:
            for i in range(n):
                src, dst = local_of(ins[i], outs[i], _place())
                locals_.append(pltpu.make_async_copy(src, dst, refs[2 * n + 2].at[i]))
                locals_[-1].start()
        start()
        wait()
        for cp in locals_:
            cp.wait()

    return pl.pallas_call(
        body, name=name, in_specs=[_HBM] * n, out_specs=[_HBM] * n, out_shape=p.out_shapes,
        scratch_shapes=p.sem_shapes() + ([pltpu.SemaphoreType.DMA((n,))] if local_of is not None else []),
        input_output_aliases={i: i for i in range(n)} if p.alias else {},
    )(*p.arrays)


def _half(c, rows):
    return pl.ds(c * (rows // 2), rows // 2)


def gather_pushes(arrays):
    outs = [jax.ShapeDtypeStruct((N_CHIPS,) + a.shape, a.dtype) for a in arrays]
    sib = len(CHIP_MASKS)
    return Pushes(arrays, outs, CHIP_MASKS + ((0, 0, 1),), 1,
                  src_of=lambda r, me, j: [r] if j == sib else [r.at[_half(me[2], r.shape[0])]],
                  dst_of=lambda o, sender, j: [o.at[_chip(sender)]] if j == sib else
                  [o.at[_chip(sender), _half(sender[2], o.shape[1])]])


def gather_swap(name, got):
    outs = [jax.ShapeDtypeStruct(a.shape, a.dtype) for a in got]
    return exchange(name, Pushes(
        got, outs, ((0, 0, 1),), len(CHIP_MASKS),
        src_of=lambda r, me, j: [r.at[_chip(_flip(me, mk)), _half(me[2], r.shape[1])] for mk in CHIP_MASKS],
        dst_of=lambda o, sender, j: [o.at[_chip(_flip(sender, mk)), _half(sender[2], o.shape[1])] for mk in CHIP_MASKS],
        alias=True))


def reduce_swap(arrays):
    split = [a.reshape(N_CHIPS, 2, a.shape[1] // 2, a.shape[2]) for a in arrays]
    half_shapes = [jax.ShapeDtypeStruct((N_CHIPS,) + a.shape[2:], F32) for a in split]
    return split, Pushes(split, half_shapes, ((0, 0, 1),), 1,
                         src_of=lambda r, me, j: [r.at[:, 1 - me[2]]], dst_of=lambda o, sender, j: [o])


def reduce_begin(tag, names, arrays, wire):
    split, pushes = reduce_swap(arrays)
    return reduce_sum(names, split, exchange(f"grad_pre_swap_{tag}", pushes), wire)


def reduce_sum(names, split, theirs, wire):
    c = lax.axis_index("c")
    chip_sum = [sum_own_half(f"sum2_{nm}", a, t, c, dt) for nm, a, t, dt in zip(names, split, theirs, wire)]
    pushes = Pushes(chip_sum, [jax.ShapeDtypeStruct((len(CHIP_MASKS),) + a.shape[1:], a.dtype) for a in chip_sum],
                    CHIP_MASKS, 1,
                    src_of=lambda r, me, j: [r.at[_chip(_flip(me, CHIP_MASKS[j]))]],
                    dst_of=lambda o, sender, j: [o.at[j]])
    return chip_sum, pushes


def reduce_end(tag, names, chip_sum, landed):
    x, y, c = _place()
    halves = [sum_landed(f"sum4_{nm}", p, a, _chip((x, y, c))) for nm, p, a in zip(names, landed, chip_sum)]
    others = exchange(f"grad_final_swap_{tag}", Pushes(
        halves, [jax.ShapeDtypeStruct(a.shape, F32) for a in halves], ((0, 0, 1),), 1,
        src_of=lambda r, me, j: [r], dst_of=lambda o, sender, j: [o]))
    return [jnp.concatenate([jnp.where(c == 0, h, o), jnp.where(c == 0, o, h)], axis=0) for h, o in zip(halves, others)]


def gather_all(arrays):
    outs = [jax.ShapeDtypeStruct((8,) + a.shape, a.dtype) for a in arrays]
    return exchange("gather_replicated", Pushes(
        arrays, outs, ALL_MASKS, 1, src_of=lambda r, me, j: [r], dst_of=lambda o, sender, j: [o.at[_devno(sender)]]),
        local_of=lambda r, o, me: (r, o.at[_devno(me)]))


def _unshard_cols(g):
    return jnp.transpose(g, (1, 0, 2)).reshape(g.shape[1], -1)


def _shard_cols(a):
    return jnp.transpose(a.reshape(a.shape[0], N_CHIPS, -1), (1, 0, 2))


class Weights(dict):
    def ride(self, kernel_name):
        return None

    def arrived(self, kernel_name, outs):
        pass


def _forward_backward(x, tgt, W, grads_early=None):
    S = x.shape[0]
    G = {}
    sd = jax.ShapeDtypeStruct

    hidden = {}

    def ffn(xin, l, j):
        out, a_sav, b_sav, *rode = ffn_fwd(xin, W["ffn_norm"][l][j], W["ffn_w_gate", l, j], W["ffn_w_up", l, j],
                                           W["ffn_w_down", l, j], l, j, W.ride(f"ffn_fwd_{l}{j}"))
        hidden[l, j] = [a_sav, b_sav]
        W.arrived(f"ffn_fwd_{l}{j}", rode)
        return out

    def ffn_back(xin, dout, l, j):
        gn = W["ffn_norm"][l][j]
        dh, G["ffn_w_gate", l, j], G["ffn_w_up", l, j], G["ffn_w_down", l, j] = ffn_bwd(
            xin, gn, W["ffn_w_gate", l, j], W["ffn_w_up", l, j], W["ffn_w_down", l, j], dout, *hidden[l, j], l, j)
        dx, G[("ffn_norm", l, j)] = norm_bwd(f"ffn_norm_bwd_{l}{j}", xin, gn, dh, dout)
        return dx

    x0 = x
    x1 = ffn(x0, 0, 0)
    g0 = W["mix_norm"][0]
    sbq, sbk, sbv = tile_fwd(f_attn_sb, "attn_in_sb", [x1], [g0, W["attn_w_in"][0]], [sd((S, SB_W), F32)] * 3, 256)
    dl_shape = sd((DL_PAIRS, S, 128), F32)
    qn, = tile_fwd(f_attn_qk, "attn_in_q", [x1], [g0, W["attn_w_in"][1], W["attn_q_norm"]], [dl_shape], 256)
    kn, = tile_fwd(f_attn_qk, "attn_in_k", [x1], [g0, W["attn_w_in"][2], W["attn_k_norm"]], [dl_shape], 256)
    vv, = tile_fwd(f_attn_v, "attn_in_v", [x1], [g0, W["attn_w_in"][3]], [dl_shape], 256)
    oa, sb_wts, *rode = sb_fwd(sbq, sbk, sbv, W.ride("sb_fwd"))
    W.arrived("sb_fwd", rode)
    qs, ks, vs = (reorder(nm, t, DIL, False) for nm, t in (("sub_q", qn), ("sub_k", kn), ("sub_v", vv)))
    o_s, lse_s, *rode = dil_fwd(qs, ks, vs, W["bias_mat"], W.ride("dil_fwd"))
    W.arrived("dil_fwd", rode)
    o_n, lse_n = reorder("nat_o", o_s, DIL, True), reorder("nat_lse", lse_s, DIL, True)
    x2, = tile_fwd(f_attn_out, "attn_out", [x1, oa, o_n, lse_n], [W["attn_w_out"]], [sd((S, D), F32)], 256)
    x3 = ffn(x2, 0, 1)
    x4 = ffn(x3, 1, 0)
    g1 = W["mix_norm"][1]
    h, hs = norm_shift_fwd(x4, g1)
    mix = W["rw_mix"]
    r, k, v = tile_fwd(f_rw_proj3, "rw_proj_rkv", [h, hs],
                       [mix[0:1], mix[2:3], mix[3:4], W["rw_wr"], W["rw_wk"], W["rw_wv"]], [sd((S, D), F32)] * 3, 256)
    mix3 = jnp.concatenate([mix[1:2], mix[4:5], mix[5:6]], axis=0)
    mid_w = [mix3, W["rw_w0"], W["rw_a0"], W["rw_kk"], W["rw_ka"], W["rw_w1"], W["rw_w2"], W["rw_a1"], W["rw_a2"],
             W["rw_g1"], W["rw_g2"]]
    hshape = sd((RW_H, S, HEAD), F32)
    mid_tiles = [h, hs, r, k, v]
    rh, lwh, kh, vh, ah, bh, gate = tile_fwd(f_rw_mid, "rw_mid", mid_tiles, mid_w, [hshape] * 6 + [sd((S, D), F32)], 128)
    yh, states = rwkv_fwd(rh, lwh, kh, vh, ah, bh)
    post_w = [W["rw_lnx_g"], W["rw_lnx_b"], W["rw_rk"], W["rw_wo"]]
    post_tiles = [yh, rh, kh, vh, gate, x4]
    x5, = tile_fwd(f_rw_post, "rw_post", post_tiles, post_w, [sd((S, D), F32)], 128)
    x6 = ffn(x5, 1, 1)
    dx6, loss_part = loss_head(x6, tgt)

    dx5 = ffn_back(x5, dx6, 1, 1)
    (dyh, drh, dkh, dvh, dgate, dx4), (d_lng, d_lnb, d_rk, d_wo) = tile_bwd(
        f_rw_post, "rw_post_bwd", post_tiles, post_w, [dx5], 128, [True] * 6, [True] * 4)
    drh2, dlwh, dkh2, dvh2, dah, dbh = rwkv_bwd(rh, lwh, kh, vh, ah, bh, states, dyh)
    mid_cts = [(drh, drh2), dlwh, (dkh, dkh2), (dvh, dvh2), dah, dbh, dgate]
    (dh, dhs, dr, dk, dv), dmid_w = tile_bwd(f_rw_mid, "rw_mid_bwd", mid_tiles, mid_w, mid_cts, 128,
                                             [True] * 5, [True] * len(mid_w))
    dmix = {}
    for nm, ct, row, wname in (("r", dr, 0, "rw_wr"), ("k", dk, 2, "rw_wk"), ("v", dv, 3, "rw_wv")):
        (dh, dhs), (dmix[row], G[wname]) = tile_bwd(
            f_rw_proj, f"rw_proj_{nm}_bwd", [h, hs], [mix[row:row + 1], W[wname]], [ct], 256,
            [True, True], [True, True], acc={0: dh, 1: dhs})
    dx4, G[("mix_norm", 1)] = norm_shift_bwd(x4, g1, dh, dhs, dx4)
    dmix3 = dmid_w[0]
    G["rw_mix"] = jnp.concatenate([dmix[0], dmix3[0:1], dmix[2], dmix[3], dmix3[1:2], dmix3[2:3]], axis=0)
    for nm, gv in zip(("rw_w0", "rw_a0", "rw_kk", "rw_ka", "rw_w1", "rw_w2", "rw_a1", "rw_a2", "rw_g1", "rw_g2"), dmid_w[1:]):
        G[nm] = gv
    G["rw_lnx_g"], G["rw_lnx_b"], G["rw_rk"], G["rw_wo"] = d_lng, d_lnb, d_rk, d_wo
    dx3 = ffn_back(x3, dx4, 1, 0)
    dx2 = ffn_back(x2, dx3, 0, 1)
    (dx1, doa, do_n, dlse_n), (G["attn_w_out"],) = tile_bwd(
        f_attn_out, "attn_out_bwd", [x1, oa, o_n, lse_n], [W["attn_w_out"]], [dx2], 256, [True] * 4, [True])
    do_s, dlse_s = reorder("sub_do", do_n, DIL, False), reorder("sub_dlse", dlse_n, DIL, False)
    ride, swapped = grads_early(G) if grads_early is not None else (None, None)
    dqs, dks, dvs, dsum, *rode = dil_bwd(qs, ks, vs, W["bias_mat"], o_s, lse_s, do_s, dlse_s, ride)
    ride, landed = swapped(rode) if swapped is not None else (None, None)
    G["rel_bias"] = bias_grad(dsum, W["buckets"])
    dqn, dkn, dvv = (reorder(nm, t, DIL, True) for nm, t in (("nat_dq", dqs), ("nat_dk", dks), ("nat_dv", dvs)))
    dsbq, dsbk, dsbv, *rode = sb_bwd(sbq, sbk, sbv, doa, sb_wts, ride)
    if landed is not None:
        landed(rode)
    dg0 = []
    dwin = []
    (dx1,), (dg, dw) = tile_bwd(f_attn_sb, "attn_in_sb_bwd", [x1], [g0, W["attn_w_in"][0]], [dsbq, dsbk, dsbv], 256,
                                [True], [True, True], acc={0: dx1})
    dg0.append(dg), dwin.append(dw)
    (dx1,), (dg, dw, G["attn_q_norm"]) = tile_bwd(f_attn_qk, "attn_in_q_bwd", [x1], [g0, W["attn_w_in"][1], W["attn_q_norm"]],
                                                  [dqn], 256, [True], [True] * 3, acc={0: dx1})
    dg0.append(dg), dwin.append(dw)
    (dx1,), (dg, dw, G["attn_k_norm"]) = tile_bwd(f_attn_qk, "attn_in_k_bwd", [x1], [g0, W["attn_w_in"][2], W["attn_k_norm"]],
                                                  [dkn], 256, [True], [True] * 3, acc={0: dx1})
    dg0.append(dg), dwin.append(dw)
    (dx1,), (dg, dw) = tile_bwd(f_attn_v, "attn_in_v_bwd", [x1], [g0, W["attn_w_in"][3]], [dvv], 256,
                                [True], [True, True], acc={0: dx1})
    dg0.append(dg), dwin.append(dw)
    G[("mix_norm", 0)] = dg0
    G["attn_w_in"] = dwin
    dx0 = ffn_back(x0, dx1, 0, 0)
    return loss_part, dx0, G


VEC_ROWS = ("ffn_norm", "rw_mix", "rw_w0", "rw_a0", "rw_kk", "rw_ka", "rw_lnx_g", "rw_lnx_b")


def kernel(x, ffn_norm, ffn_w_gate, ffn_w_up, ffn_w_down, mix_norm, rel_bias, attn_w_in, attn_q_norm, attn_k_norm, attn_w_out, rw_mix, rw_w0, rw_w1, rw_w2, rw_a0, rw_a1, rw_a2, rw_g1, rw_g2, rw_kk, rw_ka, rw_rk, rw_wr, rw_wk, rw_wv, rw_wo, rw_lnx_g, rw_lnx_b, loss_target, m_ffn_norm, m_ffn_w_gate, m_ffn_w_up, m_ffn_w_down, m_mix_norm, m_rel_bias, m_attn_w_in, m_attn_q_norm, m_attn_k_norm, m_attn_w_out, m_rw_mix, m_rw_w0, m_rw_w1, m_rw_w2, m_rw_a0, m_rw_a1, m_rw_a2, m_rw_g1, m_rw_g2, m_rw_kk, m_rw_ka, m_rw_rk, m_rw_wr, m_rw_wk, m_rw_wv, m_rw_wo, m_rw_lnx_g, m_rw_lnx_b, v_ffn_norm, v_ffn_w_gate, v_ffn_w_up, v_ffn_w_down, v_mix_norm, v_rel_bias, v_attn_w_in, v_attn_q_norm, v_attn_k_norm, v_attn_w_out, v_rw_mix, v_rw_w0, v_rw_w1, v_rw_w2, v_rw_a0, v_rw_a1, v_rw_a2, v_rw_g1, v_rw_g2, v_rw_kk, v_rw_ka, v_rw_rk, v_rw_wr, v_rw_wk, v_rw_wv, v_rw_wo, v_rw_lnx_g, v_rw_lnx_b):
    names = ["ffn_norm", "ffn_w_gate", "ffn_w_up", "ffn_w_down", "mix_norm", "rel_bias", "attn_w_in", "attn_q_norm",
             "attn_k_norm", "attn_w_out", "rw_mix", "rw_w0", "rw_w1", "rw_w2", "rw_a0", "rw_a1", "rw_a2", "rw_g1", "rw_g2",
             "rw_kk", "rw_ka", "rw_rk", "rw_wr", "rw_wk", "rw_wv", "rw_wo", "rw_lnx_g", "rw_lnx_b"]
    loc = locals()
    w = {n: loc[n] for n in names}
    mom = {n: loc["m_" + n] for n in names}
    vel = {n: loc["v_" + n] for n in names}
    S = x.shape[1]

    ffn3 = ("ffn_w_gate", "ffn_w_up", "ffn_w_down")
    rw_mats = ("rw_w1", "rw_w2", "rw_a1", "rw_a2", "rw_g1", "rw_g2", "rw_wr", "rw_wk", "rw_wv", "rw_wo")
    cols_split = ("attn_w_out", "rw_w2", "rw_a2", "rw_g2")
    shard = {"vec": jnp.concatenate([w[n].reshape(-1, 256) for n in VEC_ROWS], axis=0)}
    for n in ffn3:
        for l in range(2):
            for j in range(2):
                shard[n, l, j] = w[n][l, j].astype(BF16)
    for n in ("attn_w_in", "attn_w_out") + rw_mats:
        shard[n] = w[n].reshape(-1, w[n].shape[-1]).astype(BF16)
    ffn_keys = lambda l, j: [(n, l, j) for n in ffn3]
    w_groups = {"first": ["vec"] + ffn_keys(0, 0) + ["attn_w_in", "attn_w_out"],
                "ffn_fwd_00": [("ffn_w_gate", 0, 1), ("ffn_w_up", 0, 1)],
                "sb_fwd": ffn_keys(1, 0) + list(rw_mats) + [("ffn_w_down", 0, 1), ("ffn_w_down", 1, 1)],
                "dil_fwd": [("ffn_w_gate", 1, 1), ("ffn_w_up", 1, 1)]}
    label = lambda key: key if isinstance(key, str) else f"{key[0]}_{key[1]}{key[2]}"

    class Streamed(Weights):
        def ride(self, kernel_name):
            keys_ = w_groups.get(kernel_name)
            return gather_pushes([shard[k] for k in keys_]) if keys_ else None

        def arrived(self, kernel_name, outs):
            if not outs:
                return
            for key, g in zip(w_groups[kernel_name], gather_swap(f"gather_swap_{kernel_name}", outs)):
                if key == "vec":
                    vec_full = _unshard_cols(g)
                    self["ffn_norm"] = [[vec_full[2 * l + j][None] for j in range(2)] for l in range(2)]
                    self["rw_mix"] = vec_full[4:10]
                    for i, n in enumerate(("rw_w0", "rw_a0", "rw_kk", "rw_ka", "rw_lnx_g", "rw_lnx_b")):
                        self[n] = vec_full[10 + i][None]
                elif key == "attn_w_in":
                    self[key] = [g[p] for p in range(N_CHIPS)]
                elif key in cols_split:
                    self[key] = _unshard_cols(g)
                elif isinstance(key, str):
                    self[key] = g.reshape(D, -1)
                else:
                    self[key] = g

    buckets = _bucket_maps()
    W = Streamed({"mix_norm": [mix_norm[0:1], mix_norm[1:2]], "attn_q_norm": attn_q_norm, "attn_k_norm": attn_k_norm,
                  "rw_rk": rw_rk[0][:, None, :], "buckets": buckets, "bias_mat": bias_table(rel_bias, buckets)})
    W.arrived("first", exchange("gather_weights", W.ride("first")))

    def slots(key, G):
        if key == "vec":
            rows = [G[("ffn_norm", l, j)] for l in range(2) for j in range(2)] + [G["rw_mix"]] + \
                   [G[n] for n in ("rw_w0", "rw_a0", "rw_kk", "rw_ka", "rw_lnx_g", "rw_lnx_b")]
            return _shard_cols(jnp.concatenate(rows, axis=0))
        if key == "attn_w_in":
            return jnp.stack(G[key])
        if key in cols_split:
            return _shard_cols(G[key])
        if isinstance(key, str):
            return G[key].reshape(N_CHIPS, D // N_CHIPS, -1)
        return G[key]

    g_groups = {"early": ffn_keys(1, 1) + ffn_keys(1, 0) + ffn_keys(0, 1) + list(rw_mats) + ["attn_w_out"],
                "late": ["vec", "attn_w_in"] + ffn_keys(0, 0)}
    wire = lambda keys: [F32 if k == "vec" else BF16 for k in keys]
    part = {}

    def grads_early(G):
        keys = g_groups["early"]
        names_ = [label(k) for k in keys]
        split, swap_pushes = reduce_swap([slots(k, G) for k in keys])

        def swapped(theirs):
            chip_sum, pushes = reduce_sum(names_, split, theirs, wire(keys))
            return pushes, lambda landed: part.update(zip(keys, reduce_end("early", names_, chip_sum, landed)))

        return swap_pushes, swapped

    loss_part, dx, G = _forward_backward(x[0], loss_target[0], W, grads_early)
    loss = lax.psum(loss_part[0, 0], ("x", "y", "c"))
    keys = g_groups["late"]
    chip_sum, pushes = reduce_begin("late", [label(k) for k in keys], [slots(k, G) for k in keys], wire(keys))
    part.update(zip(keys, reduce_end("late", [label(k) for k in keys], chip_sum, exchange("scatter_grads", pushes))))

    rep = jnp.concatenate([G[("mix_norm", 0)][0] + G[("mix_norm", 0)][1] + G[("mix_norm", 0)][2] + G[("mix_norm", 0)][3],
                           G[("mix_norm", 1)]], axis=0).reshape(16, 128)
    rep = jnp.concatenate([rep, G["rel_bias"], jnp.pad(G["attn_q_norm"], ((0, 0), (0, 64))),
                           jnp.pad(G["attn_k_norm"], ((0, 0), (0, 64))), G["rw_rk"].reshape(8, 128),
                           jnp.zeros((2, 128), F32)], axis=0)
    rep_sum = sum_slots("sum_replicated", gather_all([rep])[0])
    g_rep = {
        "mix_norm": rep_sum[0:16].reshape(2, D),
        "rel_bias": jnp.transpose(rep_sum[16:28, :N_BUCKETS]),
        "attn_q_norm": rep_sum[28:29, :HEAD], "attn_k_norm": rep_sum[29:30, :HEAD],
        "rw_rk": rep_sum[30:38].reshape(1, RW_H, HEAD),
    }

    out = {}

    def adam(n, ga, gb):
        shp = w[n].shape
        to2 = lambda a: a.reshape(-1, shp[-1])
        res = adam_step(f"adam_{n}", to2(ga), None if gb is None else to2(gb), to2(w[n]), to2(mom[n]), to2(vel[n]))
        out[n] = tuple(r.reshape(shp) for r in res)

    for n in ffn3:
        out[n] = tuple(adam_ffn(f"adam_{n}", [part[n, l, j] for l in range(2) for j in range(2)], w[n], mom[n], vel[n],
                                transposed=n != "ffn_w_down"))
    for n in ("attn_w_in", "attn_w_out") + rw_mats:
        adam(n, part[n], None)
    rows = {"ffn_norm": (0, 4), "rw_mix": (4, 10), "rw_w0": (10, 11), "rw_a0": (11, 12), "rw_kk": (12, 13),
            "rw_ka": (13, 14), "rw_lnx_g": (14, 15), "rw_lnx_b": (15, 16)}
    for n, (lo, hi) in rows.items():
        adam(n, part["vec"][lo:hi], None)
    for n, gv in g_rep.items():
        adam(n, gv, None)

    grads = [out[n][0] for n in names]
    deltas = [out[n][1] for n in names]
    new_m = [out[n][2] for n in names]
    new_v = [out[n][3] for n in names]
    return (loss, dx[None], *grads, *deltas, *new_m, *new_v)
```

```python
import functools
import math

import jax
import jax.numpy as jnp
from jax import lax
from jax.experimental import pallas as pl
from jax.experimental.pallas import tpu as pltpu

F32, BF16 = jnp.float32, jnp.bfloat16
MESH = pl.DeviceIdType.MESH

D = 1024
HEAD = 64
N_CHIPS = 4
FF_SHARD = 704
SB_W = 256
DL_HEADS = 12
DL_PAIRS = 6
DIL = (1, 4, 16)
QBLK = 128
N_BUCKETS = 32
MAX_DISTANCE = 2048
RW_H = 16
RW_CHUNK = 64
NORM_EPS = 1e-6
GN_EPS = 64e-5
NEG_INF = -1e30
VMEM_LIMIT = 56 * 1024 * 1024

ADAM_LR, ADAM_B1, ADAM_B2, ADAM_EPS, ADAM_WD, ADAM_STEP = 0.001, 0.9, 0.999, 1e-08, 0.01, 10


def _cp(sem):
    return pltpu.CompilerParams(dimension_semantics=sem, vmem_limit_bytes=VMEM_LIMIT)


def _dg(a, b, dims, prec=None):
    return lax.dot_general(a, b, (dims, ((), ())), precision=prec, preferred_element_type=F32)


def _bdot(a, b, dims):
    return _dg(a.astype(BF16), b.astype(BF16), dims)


@jax.custom_vjp
def mm(a, b):
    return _bdot(a, b, ((1,), (0,)))


def _mm_fwd(a, b):
    return _bdot(a, b, ((1,), (0,))), (a, b)


def _mm_bwd(res, g):
    a, b = res
    return _bdot(g, b, ((1,), (1,))), _bdot(a, g, ((0,), (0,)))


mm.defvjp(_mm_fwd, _mm_bwd)


def rms(x, g):
    return x * lax.rsqrt(jnp.mean(x * x, axis=-1, keepdims=True) + NORM_EPS) * g


def _pieces(x):
    x1 = x.astype(BF16)
    r1 = x - x1.astype(F32)
    x2 = r1.astype(BF16)
    return jnp.concatenate([x1, x2, (r1 - x2.astype(F32)).astype(BF16)], axis=-1)


def _group_sum(x, nh):
    w = x.shape[-1]
    e = (lax.broadcasted_iota(jnp.int32, (w, nh), 0) // HEAD == lax.broadcasted_iota(jnp.int32, (w, nh), 1)).astype(BF16)
    s = _dg(_pieces(x), jnp.concatenate([e, e, e], axis=0), ((1,), (0,)))
    return _dg(_pieces(s), jnp.concatenate([e, e, e], axis=1), ((1,), (1,)))


@functools.partial(jax.custom_vjp, nondiff_argnums=(1,))
def group_sum(x, nh):
    return _group_sum(x, nh)


group_sum.defvjp(lambda x, nh: (_group_sum(x, nh), None), lambda nh, _, g: (_group_sum(g, nh),))


def softplus(u):
    return jnp.maximum(u, 0.0) + jnp.log1p(jnp.exp(-jnp.abs(u)))


def to_heads(t, nh=RW_H):
    return jnp.stack([t[:, HEAD * h:HEAD * (h + 1)] for h in range(nh)])


def from_heads(t):
    return jnp.concatenate([t[h] for h in range(t.shape[0])], axis=-1)


def _tile_spec(shape, tm):
    if len(shape) == 2:
        return pl.BlockSpec((tm, shape[1]), lambda t: (t, 0))
    return pl.BlockSpec((shape[0], tm, shape[2]), lambda t: (0, t, 0))


def _full_spec(shape):
    nd = len(shape)
    return pl.BlockSpec(tuple(shape), lambda t: (0,) * nd)


def _rows(a):
    return a.shape[0] if a.ndim == 2 else a.shape[1]


def tile_fwd(f, name, tiles, weights, outs, tm):
    nt, nw = len(tiles), len(weights)

    def body(*refs):
        tv = [r[...] for r in refs[:nt]]
        wv = [r[...].astype(F32) for r in refs[nt:nt + nw]]
        res = f(*tv, *wv)
        if not isinstance(res, (tuple, list)):
            res = (res,)
        for o, v in zip(refs[nt + nw:], res):
            o[...] = v.astype(o.dtype)

    return pl.pallas_call(
        body, name=name, grid=(_rows(tiles[0]) // tm,),
        in_specs=[_tile_spec(a.shape, tm) for a in tiles] + [_full_spec(w.shape) for w in weights],
        out_specs=[_tile_spec(o.shape, tm) for o in outs],
        out_shape=list(outs),
        compiler_params=_cp(("parallel",)),
    )(*tiles, *weights)


def tile_bwd(f, name, tiles, weights, cts, tm, dt, dw, acc=None):
    acc = acc or {}
    groups = [c if isinstance(c, tuple) else (c,) for c in cts]
    cts = [a for grp in groups for a in grp]
    nt, nw, nc = len(tiles), len(weights), len(cts)
    acc_idx = sorted(acc)
    na = len(acc_idx)
    dti = [i for i in range(nt) if dt[i]]
    dwi = [i for i in range(nw) if dw[i]]

    def body(*refs):
        tv = [r[...] for r in refs[:nt]]
        wv = [r[...].astype(F32) for r in refs[nt:nt + nw]]
        crefs = list(refs[nt + nw:nt + nw + nc])
        cv = []
        for grp in groups:
            terms = [crefs.pop(0)[...] for _ in grp]
            cv.append(functools.reduce(lambda a, b: a + b, terms))
        av = {i: r[...] for i, r in zip(acc_idx, refs[nt + nw + nc:nt + nw + nc + na])}
        orefs = refs[nt + nw + nc + na:]

        def g(*diff):
            t2, w2 = list(tv), list(wv)
            for i, v in zip(dti, diff[:len(dti)]):
                t2[i] = v
            for i, v in zip(dwi, diff[len(dti):]):
                w2[i] = v
            res = f(*t2, *w2)
            return tuple(res) if isinstance(res, (tuple, list)) else (res,)

        _, vjp = jax.vjp(g, *[tv[i] for i in dti], *[wv[i] for i in dwi])
        grads = vjp(tuple(cv))
        for k, i in enumerate(dti):
            gt = grads[k]
            if i in av:
                gt = gt + av[i]
            orefs[k][...] = gt
        first = pl.program_id(0) == 0
        for k, i in enumerate(dwi):
            o = orefs[len(dti) + k]
            gw = grads[len(dti) + k]

            @pl.when(first)
            def _(o=o, gw=gw):
                o[...] = gw

            @pl.when(jnp.logical_not(first))
            def _(o=o, gw=gw):
                o[...] += gw

    out_shape = [jax.ShapeDtypeStruct(tiles[i].shape, F32) for i in dti] + \
                [jax.ShapeDtypeStruct(weights[i].shape, F32) for i in dwi]
    res = pl.pallas_call(
        body, name=name, grid=(_rows(tiles[0]) // tm,),
        in_specs=[_tile_spec(a.shape, tm) for a in tiles] + [_full_spec(w.shape) for w in weights] +
                 [_tile_spec(c.shape, tm) for c in cts] + [_tile_spec(tiles[i].shape, tm) for i in acc_idx],
        out_specs=[_tile_spec(tiles[i].shape, tm) for i in dti] + [_full_spec(weights[i].shape) for i in dwi],
        out_shape=out_shape,
        compiler_params=_cp(("arbitrary",)),
    )(*tiles, *weights, *cts, *[acc[i] for i in acc_idx])
    return list(res[:len(dti)]), list(res[len(dti):])


def _ffn_wspec(rows, cols, cfirst):
    if cfirst:
        return pl.BlockSpec((1, rows, cols), lambda c, t: (c, 0, 0))
    return pl.BlockSpec((1, rows, cols), lambda t, c: (c, 0, 0))


def ffn_fwd(x, g, wg, wu, wd, l, j, ride=None, tm=1024):
    S = x.shape[0]
    r_in, r_out, r_shape, r_scr, r_args = _ride_specs(ride)

    def body(*refs):
        t, c = pl.program_id(0), pl.program_id(1)
        (x_ref, g_ref, wg_ref, wu_ref, wd_ref, o_ref, a_ref, b_ref, h_ref, acc_ref), finish = _riding(
            ride, refs, 5, 3, (t == 0) & (c == 0), (t == S // tm - 1) & (c == N_CHIPS - 1))

        @pl.when(c == 0)
        def _():
            h_ref[...] = rms(x_ref[...], g_ref[...]).astype(BF16)
            acc_ref[...] = jnp.zeros_like(acc_ref)

        h = h_ref[...]
        a = _bdot(h, wg_ref[0], ((1,), (0,)))
        b = _bdot(h, wu_ref[0], ((1,), (0,)))
        a_ref[0] = a.astype(BF16)
        b_ref[0] = b.astype(BF16)
        y = a * jax.nn.sigmoid(a) * b
        acc_ref[...] += _bdot(y, wd_ref[0], ((1,), (0,)))

        @pl.when(c == N_CHIPS - 1)
        def _():
            o_ref[...] = x_ref[...] + 0.5 * acc_ref[...]

        finish()

    hid = pl.BlockSpec((1, tm, FF_SHARD), lambda t, c: (c, t, 0))
    return pl.pallas_call(
        body, name=f"ffn_fwd_{l}{j}", grid=(S // tm, N_CHIPS),
        in_specs=[pl.BlockSpec((tm, D), lambda t, c: (t, 0)), pl.BlockSpec((1, D), lambda t, c: (0, 0)),
                  _ffn_wspec(D, FF_SHARD, False), _ffn_wspec(D, FF_SHARD, False), _ffn_wspec(FF_SHARD, D, False)] + r_in,
        out_specs=[pl.BlockSpec((tm, D), lambda t, c: (t, 0)), hid, hid] + r_out,
        out_shape=[jax.ShapeDtypeStruct((S, D), F32)] + [jax.ShapeDtypeStruct((N_CHIPS, S, FF_SHARD), BF16)] * 2 + r_shape,
        scratch_shapes=[pltpu.VMEM((tm, D), BF16), pltpu.VMEM((tm, D), F32)] + r_scr,
        compiler_params=_cp(("arbitrary", "arbitrary")),
    )(x, g, wg, wu, wd, *r_args)


def ffn_bwd(x, g, wg, wu, wd, dout, a_sav, b_sav, l, j, tm=512):
    S = x.shape[0]

    def body(x_ref, g_ref, wg_ref, wu_ref, wd_ref, do_ref, a_ref, b_ref, dh_ref, dwg_ref, dwu_ref, dwd_ref):
        t = pl.program_id(1)
        h = rms(x_ref[...], g_ref[...]).astype(BF16)
        wgv, wuv, wdv = wg_ref[0], wu_ref[0], wd_ref[0]
        a = a_ref[0].astype(F32)
        b = b_ref[0].astype(F32)
        sig = jax.nn.sigmoid(a)
        s = a * sig
        dyd = 0.5 * do_ref[...]
        dy = _bdot(dyd, wdv, ((1,), (1,)))
        dwd = _bdot(s * b, dyd, ((0,), (0,)))
        db = dy * s
        da = dy * b * (sig * (1.0 + a * (1.0 - sig)))
        dwg = _bdot(da, h, ((0,), (0,)))
        dwu = _bdot(db, h, ((0,), (0,)))
        dh_ref[0] = (_bdot(da, wgv, ((1,), (1,))) + _bdot(db, wuv, ((1,), (1,)))).astype(dh_ref.dtype)

        @pl.when(t == 0)
        def _():
            dwg_ref[0] = dwg
            dwu_ref[0] = dwu
            dwd_ref[0] = dwd

        @pl.when(t != 0)
        def _():
            dwg_ref[0] += dwg
            dwu_ref[0] += dwu
            dwd_ref[0] += dwd

    return pl.pallas_call(
        body, name=f"ffn_bwd_{l}{j}", grid=(N_CHIPS, S // tm),
        in_specs=[pl.BlockSpec((tm, D), lambda c, t: (t, 0)), pl.BlockSpec((1, D), lambda c, t: (0, 0)),
                  _ffn_wspec(D, FF_SHARD, True), _ffn_wspec(D, FF_SHARD, True), _ffn_wspec(FF_SHARD, D, True),
                  pl.BlockSpec((tm, D), lambda c, t: (t, 0)),
                  pl.BlockSpec((1, tm, FF_SHARD), lambda c, t: (c, t, 0)), pl.BlockSpec((1, tm, FF_SHARD), lambda c, t: (c, t, 0))],
        out_specs=[pl.BlockSpec((1, tm, D), lambda c, t: (c, t, 0))] + [_ffn_wspec(FF_SHARD, D, True)] * 3,
        out_shape=[jax.ShapeDtypeStruct((N_CHIPS, S, D), BF16)] + [jax.ShapeDtypeStruct(wd.shape, F32)] * 3,
        compiler_params=_cp(("parallel", "arbitrary")),
    )(x, g, wg, wu, wd, dout, a_sav, b_sav)


def norm_bwd(name, x, g, dh_parts, dres, tm=512):
    S = x.shape[0]
    P = dh_parts.shape[0]

    def body(x_ref, g_ref, dh_ref, dr_ref, dx_ref, dg_ref):
        dh = dh_ref[0].astype(F32)
        for p in range(1, P):
            dh = dh + dh_ref[p].astype(F32)
        _, vjp = jax.vjp(rms, x_ref[...], g_ref[...])
        dx, dg = vjp(dh)
        dx_ref[...] = dr_ref[...] + dx

        @pl.when(pl.program_id(0) == 0)
        def _():
            dg_ref[...] = dg

        @pl.when(pl.program_id(0) != 0)
        def _():
            dg_ref[...] += dg

    return pl.pallas_call(
        body, name=name, grid=(S // tm,),
        in_specs=[pl.BlockSpec((tm, D), lambda t: (t, 0)), pl.BlockSpec((1, D), lambda t: (0, 0)),
                  pl.BlockSpec((P, tm, D), lambda t: (0, t, 0)), pl.BlockSpec((tm, D), lambda t: (t, 0))],
        out_specs=[pl.BlockSpec((tm, D), lambda t: (t, 0)), pl.BlockSpec((1, D), lambda t: (0, 0))],
        out_shape=[jax.ShapeDtypeStruct((S, D), F32), jax.ShapeDtypeStruct((1, D), F32)],
        compiler_params=_cp(("arbitrary",)),
    )(x, g, dh_parts, dres)


def f_attn_sb(x, g, w):
    pr = mm(rms(x, g), w)
    return pr[:, :SB_W], pr[:, SB_W:2 * SB_W], pr[:, 2 * SB_W:]


def _pairs(y):
    return jnp.stack([y[:, 128 * j:128 * (j + 1)] for j in range(DL_PAIRS)])


def f_attn_qk(x, g, w, nrm):
    pr = mm(rms(x, g), w)
    ms = group_sum(pr * pr, DL_HEADS) * (1.0 / HEAD)
    return _pairs(pr * lax.rsqrt(ms + NORM_EPS) * jnp.concatenate([nrm] * DL_HEADS, axis=1))


def f_attn_v(x, g, w):
    return _pairs(mm(rms(x, g), w))


def f_attn_in(x, g, w0, w1, w2, w3, qn, kn):
    return (*f_attn_sb(x, g, w0), f_attn_qk(x, g, w1, qn), f_attn_qk(x, g, w2, kn), f_attn_v(x, g, w3))


def _masked(strict, x):
    return x if strict is None else jnp.where(strict, x, 0.0)


def _head_stack(x, dtype=BF16):
    nh = x.shape[1] // HEAD
    lane_head = lax.broadcasted_iota(jnp.int32, (1, x.shape[1]), 1) // HEAD
    return jnp.concatenate([jnp.where(lane_head == h, x, 0.0) for h in range(nh)], axis=0).astype(dtype)


def _head_pick(xs):
    nh = xs.shape[1] // HEAD
    rows = xs.shape[0] // nh
    lane_head = lax.broadcasted_iota(jnp.int32, (1, xs.shape[1]), 1) // HEAD
    out = xs[:rows]
    for h in range(1, nh):
        out = jnp.where(lane_head == h, xs[rows * h:rows * (h + 1)], out)
    return out


def _sb_tiles(qs, kblk, strict):
    z = _dg(qs, kblk, ((1,), (1,))) * (HEAD ** -0.5)
    keep = -(jnp.maximum(z, 0.0) + jnp.log(1.0 + jnp.exp(-jnp.abs(z))))
    return z, _masked(strict, keep)


def _tri(n, upper):
    r = lax.broadcasted_iota(jnp.int32, (n, n), 0)
    c = lax.broadcasted_iota(jnp.int32, (n, n), 1)
    return ((r > c) if upper else (r < c)).astype(BF16)


def _tri_sums(x, tri):
    hi, lo = _split2(x)
    return _dg(jnp.concatenate([hi, lo], axis=1), jnp.concatenate([tri, tri], axis=0), ((1,), (0,)))


SB_UNROLL = 8


def _sb_diag(tb, nh):
    r = lax.broadcasted_iota(jnp.int32, (nh * tb, tb), 0)
    return lax.broadcasted_iota(jnp.int32, (nh * tb, tb), 1) < lax.rem(r, tb)


def _sb_sweep(step, first, count, carry, direction, commit=None):
    def run(kbs, c):
        outs = []
        for kb in kbs:
            c, out = step(kb, c)
            outs.append(out)
        if commit is not None:
            for kb, out in zip(kbs, outs):
                commit(kb, out)
        return c

    pos, size = first, 1
    while size < SB_UNROLL:
        n = (count // size) % 2
        carry = lax.fori_loop(
            0, n, lambda i, c, pos=pos, size=size: run([pos + direction * u for u in range(size)], c), carry)
        pos, size = pos + direction * size * n, 2 * size
    return lax.fori_loop(
        0, count // SB_UNROLL,
        lambda g, c: run([pos + direction * (SB_UNROLL * g + u) for u in range(SB_UNROLL)], c), carry)


def _riding(ride, refs, n_in, n_out, first, last):
    if ride is None:
        return refs, lambda: None
    n = ride.n
    own = refs[:n_in] + refs[n_in + n:n_in + n + n_out] + refs[n_in + 2 * n + n_out:len(refs) - 2]
    start, wait = ride.ops(refs[n_in:n_in + n], refs[n_in + n + n_out:n_in + 2 * n + n_out], refs[-2], refs[-1])
    pl.when(first)(start)
    return own, lambda: pl.when(last)(wait)


def _ride_specs(ride):
    if ride is None:
        return [], [], [], [], []
    return [_HBM] * ride.n, [_HBM] * ride.n, ride.out_shapes, ride.sem_shapes(), ride.arrays


def sb_fwd(q, k, v, ride=None, tb=QBLK):
    S = q.shape[0]
    nh = SB_W // HEAD
    nb = S // tb
    r_in, r_out, r_shape, r_scr, r_args = _ride_specs(ride)

    def body(*refs):
        qb = pl.program_id(0)
        (q_ref, k_ref, v_ref, o_ref, w_ref), finish = _riding(ride, refs, 3, 2, qb == 0, qb == nb - 1)
        diag = _sb_diag(tb, nh)
        after_mat = _tri(tb, True)
        qs = _head_stack(q_ref[...])

        def step(kb, carry, strict):
            acc, run = carry
            rows = pl.ds(pl.multiple_of(kb * tb, tb), tb)
            z, keep = _sb_tiles(qs, k_ref[rows, :].astype(BF16), strict)
            w = _masked(strict, jnp.exp(z + keep + _tri_sums(keep, after_mat) + run)).astype(BF16)
            w_ref[0, kb] = w
            acc = acc + _dg(w, v_ref[rows, :].astype(BF16), ((1,), (0,)))
            return acc, run + jnp.sum(keep, axis=1, keepdims=True)

        init = (jnp.zeros((nh * tb, SB_W), F32), jnp.zeros((nh * tb, 1), F32))
        carry = step(qb, init, diag)
        acc, _ = _sb_sweep(lambda kb, c: (step(kb, c, None), None), qb - 1, qb, carry, -1)
        o_ref[...] = _head_pick(acc)
        finish()

    return pl.pallas_call(
        body, name="sb_fwd", grid=(S // tb,),
        in_specs=[pl.BlockSpec((tb, SB_W), lambda i: (i, 0)), pl.BlockSpec((S, SB_W), lambda i: (0, 0)),
                  pl.BlockSpec((S, SB_W), lambda i: (0, 0))] + r_in,
        out_specs=[pl.BlockSpec((tb, SB_W), lambda i: (i, 0)),
                   pl.BlockSpec((1, nb, nh * tb, tb), lambda i: (i, 0, 0, 0))] + r_out,
        out_shape=[jax.ShapeDtypeStruct((S, SB_W), F32), jax.ShapeDtypeStruct((nb, nb, nh * tb, tb), BF16)] + r_shape,
        scratch_shapes=r_scr,
        compiler_params=_cp(("arbitrary",)),
    )(q, k, v, *r_args)


def sb_bwd(q, k, v, do, wts, ride=None, tb=QBLK):
    S = q.shape[0]
    nh = SB_W // HEAD
    nb = S // tb
    scale = HEAD ** -0.5
    r_in, r_out, r_shape, r_scr, r_args = _ride_specs(ride)

    def body(*refs):
        qb = pl.program_id(0)
        (q_ref, k_ref, v_ref, do_ref, w_ref, dq_ref, dk_ref, dv_ref, g_scr), finish = _riding(
            ride, refs, 5, 3, qb == 0, qb == nb - 1)

        @pl.when(qb == 0)
        def _():
            dk_ref[...] = jnp.zeros_like(dk_ref)
            dv_ref[...] = jnp.zeros_like(dv_ref)

        diag = _sb_diag(tb, nh)
        before_mat = _tri(tb, False)
        qs = _head_stack(q_ref[...])
        dos = _head_stack(do_ref[...])

        def weights_pass(kb, carry):
            rows = pl.ds(pl.multiple_of(kb * tb, tb), tb)
            w = w_ref[0, kb]
            g_scr[kb] = _dg(dos, v_ref[rows, :].astype(BF16), ((1,), (1,))) * w.astype(F32)
            return carry, _dg(w, dos, ((0,), (0,)))

        def add_rows(ref):
            def commit(kb, val):
                ref[pl.ds(pl.multiple_of(kb * tb, tb), tb), :] += val
            return commit

        zero_run = jnp.zeros((nh * tb, 1), F32)
        _sb_sweep(weights_pass, 0, qb + 1, 0, 1, add_rows(dv_ref))

        def left_to_right(kb, carry, strict):
            dq, run = carry
            rows = pl.ds(pl.multiple_of(kb * tb, tb), tb)
            kblk = k_ref[rows, :].astype(BF16)
            gw = g_scr[kb]
            sig = jax.nn.sigmoid(_dg(qs, kblk, ((1,), (1,))) * scale)
            dkeep = _masked(strict, _dg(gw.astype(BF16), before_mat, ((1,), (0,))) + run)
            dz = ((gw * (1.0 - sig) - dkeep * sig) * scale).astype(BF16)
            dq = dq + _dg(dz, kblk, ((1,), (0,)))
            return (dq, run + jnp.sum(gw, axis=1, keepdims=True)), _dg(dz, qs, ((0,), (0,)))

        carry = _sb_sweep(lambda kb, c: left_to_right(kb, c, None), 0, qb,
                          (jnp.zeros((nh * tb, SB_W), F32), zero_run), 1, add_rows(dk_ref))
        (dq, _), dk_diag = left_to_right(qb, carry, diag)
        add_rows(dk_ref)(qb, dk_diag)
        dq_ref[...] = _head_pick(dq)
        finish()

    whole = pl.BlockSpec((S, SB_W), lambda i: (0, 0))
    blk = pl.BlockSpec((tb, SB_W), lambda i: (i, 0))
    return pl.pallas_call(
        body, name="sb_bwd", grid=(S // tb,),
        in_specs=[blk, whole, whole, blk, pl.BlockSpec((1, nb, nh * tb, tb), lambda i: (i, 0, 0, 0))] + r_in,
        out_specs=[blk, whole, whole] + r_out,
        out_shape=[jax.ShapeDtypeStruct((S, SB_W), F32)] * 3 + r_shape,
        scratch_shapes=[pltpu.VMEM((S // tb, nh * tb, tb), F32)] + r_scr,
        compiler_params=_cp(("arbitrary",)),
    )(q, k, v, do, wts, *r_args)


def reorder(name, x, groups, inverse):
    P, S, _ = x.shape

    def body(x_ref, o_ref):
        p = pl.program_id(0)
        for gi, r in enumerate(groups):
            @pl.when(p // 2 == gi)
            def _(r=r):
                L = S // r
                if r == 1:
                    o_ref[...] = x_ref[...]
                for c in range(r if r > 1 else 0):
                    if inverse:
                        o_ref[pl.ds(c, L, stride=r), :] = x_ref[c * L:(c + 1) * L, :]
                    else:
                        o_ref[c * L:(c + 1) * L, :] = x_ref[pl.ds(c, L, stride=r), :]

    slab = pl.BlockSpec((None, S, 128), lambda p: (p, 0, 0))
    return pl.pallas_call(
        body, name=name, grid=(P,), in_specs=[slab], out_specs=slab,
        out_shape=jax.ShapeDtypeStruct(x.shape, x.dtype), compiler_params=_cp(("parallel",)),
    )(x)


def _dil_blocks(S):
    return S // QBLK


def _dil_mask4(n_in_stream):
    qi = lax.rem(lax.broadcasted_iota(jnp.int32, (4 * QBLK, 2 * QBLK), 0), QBLK)
    kj = lax.broadcasted_iota(jnp.int32, (4 * QBLK, 2 * QBLK), 1) - QBLK
    dist = qi - kj
    return (dist >= 0) & (dist <= QBLK) & ((n_in_stream > 0) | (kj >= 0))


def _dil_lanes(ref):
    return jnp.concatenate([ref[0], ref[1]], axis=1)


def _dil_window(prev_ref, cur_ref):
    return jnp.concatenate([_dil_lanes(prev_ref), _dil_lanes(cur_ref)], axis=0).astype(BF16)


def _stream_pos(gi, i, S):
    nb = jnp.where(gi == 0, S // (QBLK * DIL[0]), jnp.where(gi == 1, S // (QBLK * DIL[1]), S // (QBLK * DIL[2])))
    return i % nb


def dil_fwd(q, k, v, bias, ride=None):
    S = q.shape[1]
    nblk = _dil_blocks(S)
    r_in, r_out, r_shape, r_scr, r_args = _ride_specs(ride)

    def body(*refs):
        gi, i = pl.program_id(0), pl.program_id(1)
        (q_ref, kc_ref, kp_ref, vc_ref, vp_ref, b_ref, o_ref, l_ref), finish = _riding(
            ride, refs, 6, 2, (gi == 0) & (i == 0), (gi == len(DIL) - 1) & (i == nblk - 1))
        mask = _dil_mask4(_stream_pos(gi, i, S))
        kw, vw = _dil_window(kp_ref, kc_ref), _dil_window(vp_ref, vc_ref)
        lg = _dg(_head_stack(_dil_lanes(q_ref)), kw, ((1,), (1,))) * (HEAD ** -0.5) + \
            b_ref[...].reshape(4 * QBLK, 2 * QBLK)
        lg = jnp.where(mask, lg, NEG_INF)
        m = jnp.max(lg, axis=-1, keepdims=True)
        p = jnp.exp(lg - m)
        den = jnp.sum(p, axis=-1, keepdims=True)
        o = _head_pick(_dg((p / den).astype(BF16), vw, ((1,), (0,))))
        lse = _head_pick(jnp.broadcast_to(m + jnp.log(den), (4 * QBLK, 4 * HEAD)))
        for j in range(2):
            o_ref[j] = o[:, 128 * j:128 * (j + 1)]
            l_ref[j] = lse[:, 128 * j:128 * (j + 1)]
        finish()

    cur = pl.BlockSpec((2, QBLK, 128), lambda g, i: (g, i, 0))
    prev = pl.BlockSpec((2, QBLK, 128), lambda g, i: (g, jnp.maximum(i - 1, 0), 0))
    return pl.pallas_call(
        body, name="dil_fwd", grid=(len(DIL), nblk),
        in_specs=[cur, cur, prev, cur, prev, pl.BlockSpec((4, QBLK, 2 * QBLK), lambda g, i: (g, 0, 0))] + r_in,
        out_specs=[cur, cur] + r_out,
        out_shape=[jax.ShapeDtypeStruct(q.shape, F32)] * 2 + r_shape,
        scratch_shapes=r_scr,
        compiler_params=_cp(("arbitrary", "arbitrary")),
    )(q, k, k, v, v, bias, *r_args)


def dil_bwd(q, k, v, bias, o, lse, do, dlse, ride=None):
    S = q.shape[1]
    nblk = _dil_blocks(S)
    r_in, r_out, r_shape, r_scr, r_args = _ride_specs(ride)

    def body(*refs):
        gi, i = pl.program_id(0), pl.program_id(1)
        (q_ref, kc_ref, kp_ref, vc_ref, vp_ref, b_ref, o_ref, l_ref, do_ref, dl_ref,
         dq_ref, dk_ref, dv_ref, ds_ref, dk_car, dv_car), finish = _riding(
            ride, refs, 10, 4, (gi == 0) & (i == 0), (gi == len(DIL) - 1) & (i == nblk))

        @pl.when(i == 0)
        def _():
            ds_ref[...] = jnp.zeros_like(ds_ref)
            dk_car[...] = jnp.zeros_like(dk_car)
            dv_car[...] = jnp.zeros_like(dv_car)

        @pl.when(i < nblk)
        def _():
            mask = _dil_mask4(_stream_pos(gi, i, S))
            kw, vw = _dil_window(kp_ref, kc_ref), _dil_window(vp_ref, vc_ref)
            qs = _head_stack(_dil_lanes(q_ref))
            do_nat = _dil_lanes(do_ref)
            dos = _head_stack(do_nat, F32)
            lse = jnp.sum(_head_stack(_dil_lanes(l_ref), F32), axis=-1, keepdims=True) * (1.0 / HEAD)
            lg = _dg(qs, kw, ((1,), (1,))) * (HEAD ** -0.5) + b_ref[...].reshape(4 * QBLK, 2 * QBLK)
            p = jnp.where(mask, jnp.exp(lg - lse), 0.0)
            dp = _dg(dos.astype(BF16), vw, ((1,), (1,)))
            four = lambda t: jnp.concatenate([t] * 4, axis=0)
            delta = jnp.sum(dos * four(_dil_lanes(o_ref)), axis=-1, keepdims=True)
            dl = jnp.sum(_head_stack(_dil_lanes(dl_ref), F32), axis=-1, keepdims=True)
            ds = p * (dp - delta + dl)
            ds_ref[...] += ds.reshape(4, QBLK, 2 * QBLK)
            dsq = (ds * (HEAD ** -0.5)).astype(BF16)
            dq = _head_pick(_dg(dsq, kw, ((1,), (0,))))
            dkw = _dg(dsq, qs, ((0,), (0,)))
            dvw = _dg(p.astype(BF16), dos.astype(BF16), ((0,), (0,)))
            for j in range(2):
                lanes = slice(128 * j, 128 * (j + 1))
                dq_ref[j] = dq[:, lanes]
                dk_ref[j] = dk_car[j] + dkw[:QBLK, lanes]
                dv_ref[j] = dv_car[j] + dvw[:QBLK, lanes]
                dk_car[j] = dkw[QBLK:, lanes]
                dv_car[j] = dvw[QBLK:, lanes]

        @pl.when(i == nblk)
        def _():
            dk_ref[...] = dk_car[...]
            dv_ref[...] = dv_car[...]

        finish()

    cur = pl.BlockSpec((2, QBLK, 128), lambda g, i: (g, jnp.minimum(i, nblk - 1), 0))
    prev = pl.BlockSpec((2, QBLK, 128), lambda g, i: (g, jnp.clip(i - 1, 0, nblk - 1), 0))
    bspec = pl.BlockSpec((4, QBLK, 2 * QBLK), lambda g, i: (g, 0, 0))
    return pl.pallas_call(
        body, name="dil_bwd", grid=(len(DIL), nblk + 1),
        in_specs=[cur, cur, prev, cur, prev, bspec, cur, cur, cur, cur] + r_in,
        out_specs=[cur, prev, prev, bspec] + r_out,
        out_shape=[jax.ShapeDtypeStruct(q.shape, F32)] * 3 + [jax.ShapeDtypeStruct(bias.shape, F32)] + r_shape,
        scratch_shapes=[pltpu.VMEM((2, QBLK, 128), F32), pltpu.VMEM((2, QBLK, 128), F32)] + r_scr,
        compiler_params=_cp(("arbitrary", "arbitrary")),
    )(q, k, k, v, v, bias, o, lse, do, dlse, *r_args)


def _t5_bucket(dist):
    max_exact = N_BUCKETS // 2
    d = jnp.maximum(dist, 1).astype(F32)
    large = max_exact + (jnp.log(d / max_exact) / math.log(MAX_DISTANCE / max_exact)
                         * (N_BUCKETS - max_exact)).astype(jnp.int32)
    large = jnp.minimum(large, N_BUCKETS - 1)
    return jnp.where(dist < max_exact, dist, large)


def _bucket_maps():
    qi = jnp.arange(QBLK)[:, None]
    kj = jnp.arange(2 * QBLK)[None, :] - QBLK
    dist = jnp.maximum(qi - kj, 0)
    return jnp.stack([_t5_bucket(dist * r) for r in DIL])


def bias_table(rel_bias, buckets):
    def body(tbl_ref, bk_ref, o_ref):
        for h in range(DL_HEADS):
            bk = bk_ref[h // 4]

            def step(b, acc):
                return jnp.where(bk == b, tbl_ref[b, h], acc)

            o_ref[h] = lax.fori_loop(0, N_BUCKETS, step, jnp.zeros(bk.shape, F32))

    return pl.pallas_call(
        body, name="bias_table", out_shape=jax.ShapeDtypeStruct((DL_HEADS,) + buckets.shape[1:], F32),
        in_specs=[pl.BlockSpec(memory_space=pltpu.SMEM), pl.BlockSpec(memory_space=pltpu.VMEM)],
        out_specs=pl.BlockSpec(memory_space=pltpu.VMEM),
    )(rel_bias, buckets)


def bias_grad(ds, buckets):
    def body(ds_ref, bk_ref, o_ref):
        lane = lax.broadcasted_iota(jnp.int32, (1, 128), 1)
        for h in range(DL_HEADS):
            dsv = ds_ref[h]
            bk = bk_ref[h // 4]

            def step(b, row):
                return jnp.where(lane == b, jnp.sum(jnp.where(bk == b, dsv, 0.0)), row)

            o_ref[h:h + 1, :] = lax.fori_loop(0, N_BUCKETS, step, jnp.zeros((1, 128), F32))

    return pl.pallas_call(
        body, name="bias_grad", out_shape=jax.ShapeDtypeStruct((DL_HEADS, 128), F32),
        in_specs=[pl.BlockSpec(memory_space=pltpu.VMEM)] * 2, out_specs=pl.BlockSpec(memory_space=pltpu.VMEM),
    )(ds, buckets)


def f_attn_out(x, oa, o, lse, w):
    og = [jnp.concatenate([o[2 * g], o[2 * g + 1]], axis=1) for g in range(3)]
    lg = [jnp.concatenate([lse[2 * g], lse[2 * g + 1]], axis=1) for g in range(3)]
    m = jnp.maximum(jnp.maximum(lg[0], lg[1]), lg[2])
    e = [jnp.exp(l - m) for l in lg]
    den = e[0] + e[1] + e[2]
    ob = (e[0] * og[0] + e[1] * og[1] + e[2] * og[2]) / den
    return x + mm(jnp.concatenate([oa, ob], axis=1), w)


def norm_shift_fwd(x, g, tm=256):
    S = x.shape[0]

    def body(x_ref, xp_ref, g_ref, h_ref, hs_ref):
        h = rms(x_ref[...], g_ref[...])
        hp = rms(xp_ref[7:8, :], g_ref[...])
        hp = jnp.where(pl.program_id(0) == 0, 0.0, hp)
        row = lax.broadcasted_iota(jnp.int32, (tm, D), 0)
        h_ref[...] = h
        hs_ref[...] = jnp.where(row == 0, hp, pltpu.roll(h, 1, 0))

    return pl.pallas_call(
        body, name="rw_norm_shift", grid=(S // tm,),
        in_specs=[pl.BlockSpec((tm, D), lambda t: (t, 0)),
                  pl.BlockSpec((8, D), lambda t: (jnp.maximum(t * (tm // 8) - 1, 0), 0)),
                  pl.BlockSpec((1, D), lambda t: (0, 0))],
        out_specs=[pl.BlockSpec((tm, D), lambda t: (t, 0))] * 2,
        out_shape=[jax.ShapeDtypeStruct((S, D), F32)] * 2,
        compiler_params=_cp(("parallel",)),
    )(x, x, g)


def norm_shift_bwd(x, g, dh, dhs, dres, tm=256):
    S = x.shape[0]
    nt = S // tm

    def body(x_ref, g_ref, dh_ref, dhs_ref, dhn_ref, dr_ref, dx_ref, dg_ref):
        t = pl.program_id(0)
        nxt = jnp.where(t == nt - 1, 0.0, dhn_ref[0:1, :])
        row = lax.broadcasted_iota(jnp.int32, (tm, D), 0)
        tot = dh_ref[...] + jnp.where(row == tm - 1, nxt, pltpu.roll(dhs_ref[...], tm - 1, 0))
        _, vjp = jax.vjp(rms, x_ref[...], g_ref[...])
        dx, dg = vjp(tot)
        dx_ref[...] = dr_ref[...] + dx

        @pl.when(t == 0)
        def _():
            dg_ref[...] = dg

        @pl.when(t != 0)
        def _():
            dg_ref[...] += dg

    tile = pl.BlockSpec((tm, D), lambda t: (t, 0))
    return pl.pallas_call(
        body, name="rw_norm_shift_bwd", grid=(nt,),
        in_specs=[tile, pl.BlockSpec((1, D), lambda t: (0, 0)), tile, tile,
                  pl.BlockSpec((8, D), lambda t: (jnp.minimum((t + 1) * (tm // 8), S // 8 - 1), 0)), tile],
        out_specs=[tile, pl.BlockSpec((1, D), lambda t: (0, 0))],
        out_shape=[jax.ShapeDtypeStruct((S, D), F32), jax.ShapeDtypeStruct((1, D), F32)],
        compiler_params=_cp(("arbitrary",)),
    )(x, g, dh, dhs, dhs, dres)


def f_rw_proj(h, hs, mix, w):
    return mm(h + (hs - h) * mix, w)


def f_rw_proj3(h, hs, mix_r, mix_k, mix_v, wr, wk, wv):
    return f_rw_proj(h, hs, mix_r, wr), f_rw_proj(h, hs, mix_k, wk), f_rw_proj(h, hs, mix_v, wv)


def f_rw_mid(h, hs, r, k, v, mix3, w0, a0, kkw, kaw, w1, w2, a1, a2, g1, g2):
    xx = hs - h
    xw, xa, xg = h + xx * mix3[0:1], h + xx * mix3[1:2], h + xx * mix3[2:3]
    w_log = -softplus(-(w0 + mm(jnp.tanh(mm(xw, w1)), w2))) - 0.5
    lw = -jnp.exp(w_log)
    ag = jax.nn.sigmoid(a0 + mm(mm(xa, a1), a2))
    gate = mm(jax.nn.sigmoid(mm(xg, g1)), g2)
    kk = k * kkw
    kk = kk / jnp.maximum(jnp.sqrt(group_sum(kk * kk, RW_H)), 1e-12)
    kmod = k * (1.0 + (ag - 1.0) * kaw)
    return (to_heads(r), to_heads(lw), to_heads(kmod), to_heads(v), to_heads(-kk), to_heads(kk * ag), gate)


def f_rw_post(yh, rh, kh, vh, gate, x, lng, lnb, rk, wo):
    mu = jnp.mean(yh, axis=-1, keepdims=True)
    var = jnp.mean(jnp.square(yh - mu), axis=-1, keepdims=True)
    yn = (yh - mu) * lax.rsqrt(var + GN_EPS)
    bonus = jnp.sum(rh * kh * rk, axis=-1, keepdims=True) * vh
    y = from_heads(yn) * lng + lnb + from_heads(bonus)
    return x + mm(y * gate, wo)


def _split2(x):
    hi = x.astype(BF16)
    return hi, (x - hi.astype(F32)).astype(BF16)


def _b3(x, y, cx, cy):
    xh, xl = _split2(x)
    yh, yl = _split2(y)
    x3 = jnp.concatenate([xh, xh, xl], axis=cx)
    y3 = jnp.concatenate([yh, yl, yh], axis=cy)
    return lax.dot_general(x3, y3, (((cx,), (cy,)), ((0,), (0,))), preferred_element_type=F32)


@jax.custom_vjp
def b_nt(x, y):
    return _b3(x, y, 2, 2)


@jax.custom_vjp
def b_nn(x, y):
    return _b3(x, y, 2, 1)


@jax.custom_vjp
def b_tn(x, y):
    return _b3(x, y, 1, 1)


def _b1(x, y, cx, cy):
    return lax.dot_general(x.astype(BF16), y.astype(BF16), (((cx,), (cy,)), ((0,), (0,))), preferred_element_type=F32)


b_nt.defvjp(lambda x, y: (b_nt(x, y), (x, y)), lambda r, g: (_b1(g, r[1], 2, 1), _b1(g, r[0], 1, 1)))
b_nn.defvjp(lambda x, y: (b_nn(x, y), (x, y)), lambda r, g: (_b1(g, r[1], 2, 2), _b1(r[0], g, 1, 1)))
b_tn.defvjp(lambda x, y: (b_tn(x, y), (x, y)), lambda r, g: (_b1(r[1], g, 2, 2), _b1(r[0], g, 2, 1)))


def _tri_apply(x, lower):
    H, C, _ = x.shape
    ii = lax.broadcasted_iota(jnp.int32, (C, C), 0)
    jj = lax.broadcasted_iota(jnp.int32, (C, C), 1)
    m = jnp.broadcast_to(((jj <= ii) if lower else (jj >= ii)).astype(BF16), (H, C, C))
    x1 = x.astype(BF16)
    r1 = x - x1.astype(F32)
    x2 = r1.astype(BF16)
    x3 = (r1 - x2.astype(F32)).astype(BF16)
    return lax.dot_general(jnp.concatenate([m, m, m], axis=2), jnp.concatenate([x1, x2, x3], axis=1),
                           (((2,), (1,)), ((0,), (0,))), preferred_element_type=F32)


@jax.custom_vjp
def run_sum(x):
    return _tri_apply(x, True)


run_sum.defvjp(lambda x: (run_sum(x), None), lambda _, g: (_tri_apply(g, False),))


def rwkv_chunk(S0, r, lw, k, v, a, b):
    H, C, _ = r.shape
    V = S0.shape[1]
    ii = lax.broadcasted_iota(jnp.int32, (C, C), 0)
    jj = lax.broadcasted_iota(jnp.int32, (C, C), 1)
    strict = jj < ii
    i2 = lax.broadcasted_iota(jnp.int32, (C, 2 * C), 0)
    j2 = lax.broadcasted_iota(jnp.int32, (C, 2 * C), 1)
    incl2 = jnp.where(j2 >= C, j2 - C, j2) <= i2
    g = run_sum(lw)
    ig = jnp.exp(-g)
    ar = jnp.concatenate([a * jnp.exp(g - lw), r * jnp.exp(g)], axis=1)
    bk = jnp.concatenate([b * ig, k * ig], axis=1)
    m = b_nt(ar, bk)
    a_ab = jnp.where(strict, m[:, :C, :C], 0.0)
    a_ak = jnp.where(strict, m[:, :C, C:], 0.0)
    b_r = jnp.where(incl2, m[:, C:, :], 0.0)
    p = b_nt(ar, S0)
    u = p[:, :C] + b_nn(a_ak, v)
    nmat, n = a_ab, 1
    while n < C:
        n *= 2
        if n < C:
            z = b_nn(nmat, jnp.concatenate([u, nmat], axis=2))
            u, nmat = u + z[:, :, :V], z[:, :, V:]
        else:
            u = u + b_nn(nmat, u)
    uv = jnp.concatenate([u, v], axis=1)
    y = p[:, C:] + b_nn(b_r, uv)
    g_end = g[:, C - 1:C, :]
    dec = jnp.exp(g_end - g)
    s_new = S0 * jnp.exp(g_end) + b_tn(uv, jnp.concatenate([b * dec, k * dec], axis=1))
    return y, s_new


def rwkv_fwd(r, lw, k, v, a, b):
    H, S, _ = r.shape
    C = RW_CHUNK

    def body(r_ref, lw_ref, k_ref, v_ref, a_ref, b_ref, y_ref, s_ref, s_scr):
        @pl.when(pl.program_id(0) == 0)
        def _():
            s_scr[...] = jnp.zeros_like(s_scr)

        s0 = s_scr[...]
        s_ref[0] = s0
        y, s1 = rwkv_chunk(s0, r_ref[...], lw_ref[...], k_ref[...], v_ref[...], a_ref[...], b_ref[...])
        y_ref[...] = y
        s_scr[...] = s1

    bs = pl.BlockSpec((H, C, HEAD), lambda c: (0, c, 0))
    return pl.pallas_call(
        body, name="rwkv_fwd", grid=(S // C,), in_specs=[bs] * 6,
        out_specs=[bs, pl.BlockSpec((1, H, HEAD, HEAD), lambda c: (c, 0, 0, 0))],
        out_shape=[jax.ShapeDtypeStruct((H, S, HEAD), F32), jax.ShapeDtypeStruct((S // C, H, HEAD, HEAD), F32)],
        scratch_shapes=[pltpu.VMEM((H, HEAD, HEAD), F32)],
        compiler_params=_cp(("arbitrary",)),
    )(r, lw, k, v, a, b)


def rwkv_bwd(r, lw, k, v, a, b, states, dy):
    H, S, _ = r.shape
    C = RW_CHUNK
    nc = S // C

    def body(r_ref, lw_ref, k_ref, v_ref, a_ref, b_ref, s_ref, dy_ref, dr, dlw, dk, dv, da, db, ds_scr):
        @pl.when(pl.program_id(0) == 0)
        def _():
            ds_scr[...] = jnp.zeros_like(ds_scr)

        _, vjp = jax.vjp(rwkv_chunk, s_ref[0], r_ref[...], lw_ref[...], k_ref[...], v_ref[...], a_ref[...], b_ref[...])
        grads = vjp((dy_ref[...], ds_scr[...]))
        ds_scr[...] = grads[0]
        for o, gv in zip((dr, dlw, dk, dv, da, db), grads[1:]):
            o[...] = gv

    bs = pl.BlockSpec((H, C, HEAD), lambda c: (0, nc - 1 - c, 0))
    return pl.pallas_call(
        body, name="rwkv_bwd", grid=(nc,),
        in_specs=[bs] * 6 + [pl.BlockSpec((1, H, HEAD, HEAD), lambda c: (nc - 1 - c, 0, 0, 0)), bs],
        out_specs=[bs] * 6, out_shape=[jax.ShapeDtypeStruct((H, S, HEAD), F32)] * 6,
        scratch_shapes=[pltpu.VMEM((H, HEAD, HEAD), F32)],
        compiler_params=_cp(("arbitrary",)),
    )(r, lw, k, v, a, b, states, dy)


def loss_head(y, target, tm=512):
    S = y.shape[0]

    def body(y_ref, t_ref, dy_ref, l_ref):
        e = y_ref[...] - t_ref[...]
        dy_ref[...] = e * (1.0 / D)
        part = jnp.broadcast_to(0.5 * jnp.sum(jnp.mean(e * e, axis=-1, keepdims=True)), (1, 128))

        @pl.when(pl.program_id(0) == 0)
        def _():
            l_ref[...] = part

        @pl.when(pl.program_id(0) != 0)
        def _():
            l_ref[...] += part

    tile = pl.BlockSpec((tm, D), lambda t: (t, 0))
    return pl.pallas_call(
        body, name="loss_head", grid=(S // tm,), in_specs=[tile, tile],
        out_specs=[tile, pl.BlockSpec((1, 128), lambda t: (0, 0))],
        out_shape=[jax.ShapeDtypeStruct((S, D), F32), jax.ShapeDtypeStruct((1, 128), F32)],
        compiler_params=_cp(("arbitrary",)),
    )(y, target)


def _row_tile(rows, cols, budget=1 << 19):
    best = None
    for tr in range(8, rows + 1, 8):
        if rows % tr == 0 and tr * cols <= budget:
            best = tr
    return best or rows


def _adam(w, g, m, v):
    m = ADAM_B1 * m + (1.0 - ADAM_B1) * g
    v = ADAM_B2 * v + (1.0 - ADAM_B2) * jnp.square(g)
    m_hat = m / (1.0 - ADAM_B1 ** ADAM_STEP)
    v_hat = v / (1.0 - ADAM_B2 ** ADAM_STEP)
    return -ADAM_LR * (m_hat / (jnp.sqrt(v_hat) + ADAM_EPS) + ADAM_WD * w), m, v


def sum_slots(name, parts, dtype=F32, extras=()):
    n = 0 if parts is None else parts.shape[0]
    R, C = extras[0].shape if parts is None else parts.shape[1:]
    tr = _row_tile(R, C * (n + len(extras)))
    ins = ([] if parts is None else [parts]) + list(extras)

    def body(*refs):
        terms = [] if parts is None else [refs[0][i] for i in range(n)]
        terms += [r[...] for r in refs[len(ins) - len(extras):len(ins)]]
        s = terms[0].astype(F32)
        for t in terms[1:]:
            s = s + t.astype(F32)
        refs[len(ins)][...] = s.astype(dtype)

    tile = pl.BlockSpec((tr, C), lambda t: (t, 0))
    return pl.pallas_call(
        body, name=name, grid=(R // tr,),
        in_specs=([] if parts is None else [pl.BlockSpec((n, tr, C), lambda t: (0, t, 0))]) + [tile] * len(extras),
        out_specs=tile, out_shape=jax.ShapeDtypeStruct((R, C), dtype), compiler_params=_cp(("parallel",)),
    )(*ins)


def sum_own_half(name, split, theirs, c, dtype):
    nq, _, rh, cols = split.shape
    tr = _row_tile(rh, 2 * cols)

    def body(c_ref, a_ref, b_ref, o_ref):
        o_ref[...] = (a_ref[...] + b_ref[...]).astype(dtype)

    tile = pl.BlockSpec((None, tr, cols), lambda q, t, c_ref: (q, t, 0))
    return pl.pallas_call(
        body, name=name,
        grid_spec=pltpu.PrefetchScalarGridSpec(
            num_scalar_prefetch=1, grid=(nq, rh // tr),
            in_specs=[pl.BlockSpec((None, None, tr, cols), lambda q, t, c_ref: (q, c_ref[0], t, 0)), tile],
            out_specs=tile),
        out_shape=jax.ShapeDtypeStruct((nq, rh, cols), dtype), compiler_params=_cp(("parallel", "parallel")),
    )(jnp.reshape(c, (1,)).astype(jnp.int32), split, theirs)


def sum_landed(name, landed, chip_sum, p):
    n, rh, cols = landed.shape
    tr = _row_tile(rh, (n + 1) * cols)

    def body(p_ref, l_ref, own_ref, o_ref):
        s = l_ref[0].astype(F32)
        for i in range(1, n):
            s = s + l_ref[i].astype(F32)
        o_ref[...] = s + own_ref[...].astype(F32)

    return pl.pallas_call(
        body, name=name,
        grid_spec=pltpu.PrefetchScalarGridSpec(
            num_scalar_prefetch=1, grid=(rh // tr,),
            in_specs=[pl.BlockSpec((n, tr, cols), lambda t, p_ref: (0, t, 0)),
                      pl.BlockSpec((None, tr, cols), lambda t, p_ref: (p_ref[0], t, 0))],
            out_specs=pl.BlockSpec((tr, cols), lambda t, p_ref: (t, 0))),
        out_shape=jax.ShapeDtypeStruct((rh, cols), F32), compiler_params=_cp(("parallel",)),
    )(jnp.reshape(p, (1,)).astype(jnp.int32), landed, chip_sum)


def adam_step(name, ga, gb, w, m, v):
    R, C = w.shape
    tr = _row_tile(R, C, 1 << 17)
    ins = [ga] + ([gb] if gb is not None else []) + [w, m, v]

    def body(*refs):
        g = refs[0][...]
        if gb is not None:
            g = g + refs[1][...]
        w_ref, m_ref, v_ref, g_out, d_out, m_out, v_out = refs[len(ins) - 3:]
        d, m2, v2 = _adam(w_ref[...], g, m_ref[...], v_ref[...])
        g_out[...] = g
        d_out[...] = d
        m_out[...] = m2
        v_out[...] = v2

    tile = pl.BlockSpec((tr, C), lambda t: (t, 0))
    return pl.pallas_call(
        body, name=name, grid=(R // tr,), in_specs=[tile] * len(ins), out_specs=[tile] * 4,
        out_shape=[jax.ShapeDtypeStruct((R, C), F32)] * 4, compiler_params=_cp(("parallel",)),
    )(*ins)


def adam_ffn(name, g_pieces, w, m, v, transposed=False):
    if transposed:
        res = adam_ffn(name, g_pieces, *(jnp.swapaxes(a, 2, 3) for a in (w, m, v)))
        return [jnp.swapaxes(r, 2, 3) for r in res]
    _, _, R, C = w.shape
    tr = _row_tile(R, 4 * C, 1 << 17)

    def body(g00, g01, g10, g11, w_ref, m_ref, v_ref, g_out, d_out, m_out, v_out):
        for l, j, g_ref in ((0, 0, g00), (0, 1, g01), (1, 0, g10), (1, 1, g11)):
            g = g_ref[...]
            d, m2, v2 = _adam(w_ref[l, j], g, m_ref[l, j], v_ref[l, j])
            g_out[l, j] = g
            d_out[l, j] = d
            m_out[l, j] = m2
            v_out[l, j] = v2

    piece = pl.BlockSpec((tr, C), lambda t: (t, 0))
    full = pl.BlockSpec((2, 2, tr, C), lambda t: (0, 0, t, 0))
    return pl.pallas_call(
        body, name=name, grid=(R // tr,), in_specs=[piece] * 4 + [full] * 3, out_specs=[full] * 4,
        out_shape=[jax.ShapeDtypeStruct(w.shape, F32)] * 4, compiler_params=_cp(("parallel",)),
    )(*g_pieces, w, m, v)


def _place():
    return lax.axis_index("x"), lax.axis_index("y"), lax.axis_index("c")


def _flip(me, mask):
    return tuple(1 - v if mk else v for v, mk in zip(me, mask))


CHIP_MASKS = ((1, 0, 0), (0, 1, 0), (1, 1, 0))
ALL_MASKS = tuple((a, b, c) for a in (0, 1) for b in (0, 1) for c in (0, 1) if (a, b, c) != (0, 0, 0))


def _chip(dev):
    return 2 * dev[0] + dev[1]


def _devno(dev):
    return 4 * dev[0] + 2 * dev[1] + dev[2]


class Pushes:
    def __init__(self, arrays, out_shapes, masks, copies, src_of, dst_of, alias=False):
        self.arrays, self.out_shapes, self.masks, self.copies = list(arrays), list(out_shapes), masks, copies
        self.src_of, self.dst_of, self.alias = src_of, dst_of, alias
        self.n = len(self.arrays)

    def sem_shapes(self):
        k = self.n * len(self.masks) * self.copies
        return [pltpu.SemaphoreType.DMA((k,)), pltpu.SemaphoreType.DMA((k,))]

    def ops(self, ins, outs, send_sems, recv_sems):
        me = _place()
        sends, lands = [], []
        for i in range(self.n):
            for j, mk in enumerate(self.masks):
                peer = _flip(me, mk)
                srcs, dsts = self.src_of(ins[i], me, j), self.dst_of(outs[i], me, j)
                here = self.dst_of(outs[i], peer, j)
                for q in range(self.copies):
                    sem = (i * len(self.masks) + j) * self.copies + q
                    sends.append(pltpu.make_async_remote_copy(
                        src_ref=srcs[q], dst_ref=dsts[q], send_sem=send_sems.at[sem], recv_sem=recv_sems.at[sem],
                        device_id=peer, device_id_type=MESH))
                    lands.append(pltpu.make_async_remote_copy(
                        src_ref=here[q], dst_ref=here[q], send_sem=send_sems.at[sem], recv_sem=recv_sems.at[sem],
                        device_id=peer, device_id_type=MESH))

        def start():
            for cp in sends:
                cp.start()

        def wait():
            for cp in lands:
                cp.wait_recv()
            for cp in sends:
                cp.wait_send()

        return start, wait


_HBM = pl.BlockSpec(memory_space=pl.ANY)


def exchange(name, p, local_of=None):
    n = p.n

    def body(*refs):
        ins, outs = refs[:n], refs[n:2 * n]
        start, wait = p.ops(ins, outs, refs[2 * n], refs[2 * n + 1])
        locals_ = []
        if local_of is not None:
            for i in range(n):
                src, dst = local_of(ins[i], outs[i], _place())
                locals_.append(pltpu.make_async_copy(src, dst, refs[2 * n + 2].at[i]))
                locals_[-1].start()
        start()
        wait()
        for cp in locals_:
            cp.wait()

    return pl.pallas_call(
        body, name=name, in_specs=[_HBM] * n, out_specs=[_HBM] * n, out_shape=p.out_shapes,
        scratch_shapes=p.sem_shapes() + ([pltpu.SemaphoreType.DMA((n,))] if local_of is not None else []),
        input_output_aliases={i: i for i in range(n)} if p.alias else {},
    )(*p.arrays)


def _half(c, rows):
    return pl.ds(c * (rows // 2), rows // 2)


def gather_pushes(arrays):
    outs = [jax.ShapeDtypeStruct((N_CHIPS,) + a.shape, a.dtype) for a in arrays]
    sib = len(CHIP_MASKS)
    return Pushes(arrays, outs, CHIP_MASKS + ((0, 0, 1),), 1,
                  src_of=lambda r, me, j: [r] if j == sib else [r.at[_half(me[2], r.shape[0])]],
                  dst_of=lambda o, sender, j: [o.at[_chip(sender)]] if j == sib else
                  [o.at[_chip(sender), _half(sender[2], o.shape[1])]])


def gather_swap(name, got):
    outs = [jax.ShapeDtypeStruct(a.shape, a.dtype) for a in got]
    return exchange(name, Pushes(
        got, outs, ((0, 0, 1),), len(CHIP_MASKS),
        src_of=lambda r, me, j: [r.at[_chip(_flip(me, mk)), _half(me[2], r.shape[1])] for mk in CHIP_MASKS],
        dst_of=lambda o, sender, j: [o.at[_chip(_flip(sender, mk)), _half(sender[2], o.shape[1])] for mk in CHIP_MASKS],
        alias=True))


def reduce_swap(arrays):
    split = [a.reshape(N_CHIPS, 2, a.shape[1] // 2, a.shape[2]) for a in arrays]
    half_shapes = [jax.ShapeDtypeStruct((N_CHIPS,) + a.shape[2:], F32) for a in split]
    return split, Pushes(split, half_shapes, ((0, 0, 1),), 1,
                         src_of=lambda r, me, j: [r.at[:, 1 - me[2]]], dst_of=lambda o, sender, j: [o])


def reduce_begin(tag, names, arrays, wire):
    split, pushes = reduce_swap(arrays)
    return reduce_sum(names, split, exchange(f"grad_pre_swap_{tag}", pushes), wire)


def reduce_sum(names, split, theirs, wire):
    c = lax.axis_index("c")
    chip_sum = [sum_own_half(f"sum2_{nm}", a, t, c, dt) for nm, a, t, dt in zip(names, split, theirs, wire)]
    pushes = Pushes(chip_sum, [jax.ShapeDtypeStruct((len(CHIP_MASKS),) + a.shape[1:], a.dtype) for a in chip_sum],
                    CHIP_MASKS, 1,
                    src_of=lambda r, me, j: [r.at[_chip(_flip(me, CHIP_MASKS[j]))]],
                    dst_of=lambda o, sender, j: [o.at[j]])
    return chip_sum, pushes


def reduce_end(tag, names, chip_sum, landed):
    x, y, c = _place()
    halves = [sum_landed(f"sum4_{nm}", p, a, _chip((x, y, c))) for nm, p, a in zip(names, landed, chip_sum)]
    others = exchange(f"grad_final_swap_{tag}", Pushes(
        halves, [jax.ShapeDtypeStruct(a.shape, F32) for a in halves], ((0, 0, 1),), 1,
        src_of=lambda r, me, j: [r], dst_of=lambda o, sender, j: [o]))
    return [jnp.concatenate([jnp.where(c == 0, h, o), jnp.where(c == 0, o, h)], axis=0) for h, o in zip(halves, others)]


def gather_all(arrays):
    outs = [jax.ShapeDtypeStruct((8,) + a.shape, a.dtype) for a in arrays]
    return exchange("gather_replicated", Pushes(
        arrays, outs, ALL_MASKS, 1, src_of=lambda r, me, j: [r], dst_of=lambda o, sender, j: [o.at[_devno(sender)]]),
        local_of=lambda r, o, me: (r, o.at[_devno(me)]))


def _unshard_cols(g):
    return jnp.transpose(g, (1, 0, 2)).reshape(g.shape[1], -1)


def _shard_cols(a):
    return jnp.transpose(a.reshape(a.shape[0], N_CHIPS, -1), (1, 0, 2))


class Weights(dict):
    def ride(self, kernel_name):
        return None

    def arrived(self, kernel_name, outs):
        pass


def _forward_backward(x, tgt, W, grads_early=None):
    S = x.shape[0]
    G = {}
    sd = jax.ShapeDtypeStruct

    hidden = {}

    def ffn(xin, l, j):
        out, a_sav, b_sav, *rode = ffn_fwd(xin, W["ffn_norm"][l][j], W["ffn_w_gate", l, j], W["ffn_w_up", l, j],
                                           W["ffn_w_down", l, j], l, j, W.ride(f"ffn_fwd_{l}{j}"))
        hidden[l, j] = [a_sav, b_sav]
        W.arrived(f"ffn_fwd_{l}{j}", rode)
        return out

    def ffn_back(xin, dout, l, j):
        gn = W["ffn_norm"][l][j]
        dh, G["ffn_w_gate", l, j], G["ffn_w_up", l, j], G["ffn_w_down", l, j] = ffn_bwd(
            xin, gn, W["ffn_w_gate", l, j], W["ffn_w_up", l, j], W["ffn_w_down", l, j], dout, *hidden[l, j], l, j)
        dx, G[("ffn_norm", l, j)] = norm_bwd(f"ffn_norm_bwd_{l}{j}", xin, gn, dh, dout)
        return dx

    x0 = x
    x1 = ffn(x0, 0, 0)
    g0 = W["mix_norm"][0]
    dl_shape = sd((DL_PAIRS, S, 128), F32)
    sbq, sbk, sbv, qn, kn, vv = tile_fwd(
        f_attn_in, "attn_in", [x1], [g0, *W["attn_w_in"], W["attn_q_norm"], W["attn_k_norm"]],
        [sd((S, SB_W), F32)] * 3 + [dl_shape] * 3, 256)
    oa, sb_wts, *rode = sb_fwd(sbq, sbk, sbv, W.ride("sb_fwd"))
    W.arrived("sb_fwd", rode)
    qs, ks, vs = (reorder(nm, t, DIL, False) for nm, t in (("sub_q", qn), ("sub_k", kn), ("sub_v", vv)))
    o_s, lse_s, *rode = dil_fwd(qs, ks, vs, W["bias_mat"], W.ride("dil_fwd"))
    W.arrived("dil_fwd", rode)
    o_n, lse_n = reorder("nat_o", o_s, DIL, True), reorder("nat_lse", lse_s, DIL, True)
    x2, = tile_fwd(f_attn_out, "attn_out", [x1, oa, o_n, lse_n], [W["attn_w_out"]], [sd((S, D), F32)], 256)
    x3 = ffn(x2, 0, 1)
    x4 = ffn(x3, 1, 0)
    g1 = W["mix_norm"][1]
    h, hs = norm_shift_fwd(x4, g1)
    mix = W["rw_mix"]
    r, k, v = tile_fwd(f_rw_proj3, "rw_proj_rkv", [h, hs],
                       [mix[0:1], mix[2:3], mix[3:4], W["rw_wr"], W["rw_wk"], W["rw_wv"]], [sd((S, D), F32)] * 3, 256)
    mix3 = jnp.concatenate([mix[1:2], mix[4:5], mix[5:6]], axis=0)
    mid_w = [mix3, W["rw_w0"], W["rw_a0"], W["rw_kk"], W["rw_ka"], W["rw_w1"], W["rw_w2"], W["rw_a1"], W["rw_a2"],
             W["rw_g1"], W["rw_g2"]]
    hshape = sd((RW_H, S, HEAD), F32)
    mid_tiles = [h, hs, r, k, v]
    rh, lwh, kh, vh, ah, bh, gate = tile_fwd(f_rw_mid, "rw_mid", mid_tiles, mid_w, [hshape] * 6 + [sd((S, D), F32)], 128)
    yh, states = rwkv_fwd(rh, lwh, kh, vh, ah, bh)
    post_w = [W["rw_lnx_g"], W["rw_lnx_b"], W["rw_rk"], W["rw_wo"]]
    post_tiles = [yh, rh, kh, vh, gate, x4]
    x5, = tile_fwd(f_rw_post, "rw_post", post_tiles, post_w, [sd((S, D), F32)], 128)
    x6 = ffn(x5, 1, 1)
    dx6, loss_part = loss_head(x6, tgt)

    dx5 = ffn_back(x5, dx6, 1, 1)
    (dyh, drh, dkh, dvh, dgate, dx4), (d_lng, d_lnb, d_rk, d_wo) = tile_bwd(
        f_rw_post, "rw_post_bwd", post_tiles, post_w, [dx5], 128, [True] * 6, [True] * 4)
    drh2, dlwh, dkh2, dvh2, dah, dbh = rwkv_bwd(rh, lwh, kh, vh, ah, bh, states, dyh)
    mid_cts = [(drh, drh2), dlwh, (dkh, dkh2), (dvh, dvh2), dah, dbh, dgate]
    (dh, dhs, dr, dk, dv), dmid_w = tile_bwd(f_rw_mid, "rw_mid_bwd", mid_tiles, mid_w, mid_cts, 128,
                                             [True] * 5, [True] * len(mid_w))
    dmix = {}
    for nm, ct, row, wname in (("r", dr, 0, "rw_wr"), ("k", dk, 2, "rw_wk"), ("v", dv, 3, "rw_wv")):
        (dh, dhs), (dmix[row], G[wname]) = tile_bwd(
            f_rw_proj, f"rw_proj_{nm}_bwd", [h, hs], [mix[row:row + 1], W[wname]], [ct], 256,
            [True, True], [True, True], acc={0: dh, 1: dhs})
    dx4, G[("mix_norm", 1)] = norm_shift_bwd(x4, g1, dh, dhs, dx4)
    dmix3 = dmid_w[0]
    G["rw_mix"] = jnp.concatenate([dmix[0], dmix3[0:1], dmix[2], dmix[3], dmix3[1:2], dmix3[2:3]], axis=0)
    for nm, gv in zip(("rw_w0", "rw_a0", "rw_kk", "rw_ka", "rw_w1", "rw_w2", "rw_a1", "rw_a2", "rw_g1", "rw_g2"), dmid_w[1:]):
        G[nm] = gv
    G["rw_lnx_g"], G["rw_lnx_b"], G["rw_rk"], G["rw_wo"] = d_lng, d_lnb, d_rk, d_wo
    dx3 = ffn_back(x3, dx4, 1, 0)
    dx2 = ffn_back(x2, dx3, 0, 1)
    (dx1, doa, do_n, dlse_n), (G["attn_w_out"],) = tile_bwd(
        f_attn_out, "attn_out_bwd", [x1, oa, o_n, lse_n], [W["attn_w_out"]], [dx2], 256, [True] * 4, [True])
    do_s, dlse_s = reorder("sub_do", do_n, DIL, False), reorder("sub_dlse", dlse_n, DIL, False)
    ride, swapped = grads_early(G) if grads_early is not None else (None, None)
    dqs, dks, dvs, dsum, *rode = dil_bwd(qs, ks, vs, W["bias_mat"], o_s, lse_s, do_s, dlse_s, ride)
    ride, landed = swapped(rode) if swapped is not None else (None, None)
    G["rel_bias"] = bias_grad(dsum, W["buckets"])
    dqn, dkn, dvv = (reorder(nm, t, DIL, True) for nm, t in (("nat_dq", dqs), ("nat_dk", dks), ("nat_dv", dvs)))
    dsbq, dsbk, dsbv, *rode = sb_bwd(sbq, sbk, sbv, doa, sb_wts, ride)
    if landed is not None:
        landed(rode)
    dg0 = []
    dwin = []
    (dx1,), (dg, dw) = tile_bwd(f_attn_sb, "attn_in_sb_bwd", [x1], [g0, W["attn_w_in"][0]], [dsbq, dsbk, dsbv], 256,
                                [True], [True, True], acc={0: dx1})
    dg0.append(dg), dwin.append(dw)
    (dx1,), (dg, dw, G["attn_q_norm"]) = tile_bwd(f_attn_qk, "attn_in_q_bwd", [x1], [g0, W["attn_w_in"][1], W["attn_q_norm"]],
                                                  [dqn], 256, [True], [True] * 3, acc={0: dx1})
    dg0.append(dg), dwin.append(dw)
    (dx1,), (dg, dw, G["attn_k_norm"]) = tile_bwd(f_attn_qk, "attn_in_k_bwd", [x1], [g0, W["attn_w_in"][2], W["attn_k_norm"]],
                                                  [dkn], 256, [True], [True] * 3, acc={0: dx1})
    dg0.append(dg), dwin.append(dw)
    (dx1,), (dg, dw) = tile_bwd(f_attn_v, "attn_in_v_bwd", [x1], [g0, W["attn_w_in"][3]], [dvv], 256,
                                [True], [True, True], acc={0: dx1})
    dg0.append(dg), dwin.append(dw)
    G[("mix_norm", 0)] = dg0
    G["attn_w_in"] = dwin
    dx0 = ffn_back(x0, dx1, 0, 0)
    return loss_part, dx0, G


VEC_ROWS = ("ffn_norm", "rw_mix", "rw_w0", "rw_a0", "rw_kk", "rw_ka", "rw_lnx_g", "rw_lnx_b")


def kernel(x, ffn_norm, ffn_w_gate, ffn_w_up, ffn_w_down, mix_norm, rel_bias, attn_w_in, attn_q_norm, attn_k_norm, attn_w_out, rw_mix, rw_w0, rw_w1, rw_w2, rw_a0, rw_a1, rw_a2, rw_g1, rw_g2, rw_kk, rw_ka, rw_rk, rw_wr, rw_wk, rw_wv, rw_wo, rw_lnx_g, rw_lnx_b, loss_target, m_ffn_norm, m_ffn_w_gate, m_ffn_w_up, m_ffn_w_down, m_mix_norm, m_rel_bias, m_attn_w_in, m_attn_q_norm, m_attn_k_norm, m_attn_w_out, m_rw_mix, m_rw_w0, m_rw_w1, m_rw_w2, m_rw_a0, m_rw_a1, m_rw_a2, m_rw_g1, m_rw_g2, m_rw_kk, m_rw_ka, m_rw_rk, m_rw_wr, m_rw_wk, m_rw_wv, m_rw_wo, m_rw_lnx_g, m_rw_lnx_b, v_ffn_norm, v_ffn_w_gate, v_ffn_w_up, v_ffn_w_down, v_mix_norm, v_rel_bias, v_attn_w_in, v_attn_q_norm, v_attn_k_norm, v_attn_w_out, v_rw_mix, v_rw_w0, v_rw_w1, v_rw_w2, v_rw_a0, v_rw_a1, v_rw_a2, v_rw_g1, v_rw_g2, v_rw_kk, v_rw_ka, v_rw_rk, v_rw_wr, v_rw_wk, v_rw_wv, v_rw_wo, v_rw_lnx_g, v_rw_lnx_b):
    names = ["ffn_norm", "ffn_w_gate", "ffn_w_up", "ffn_w_down", "mix_norm", "rel_bias", "attn_w_in", "attn_q_norm",
             "attn_k_norm", "attn_w_out", "rw_mix", "rw_w0", "rw_w1", "rw_w2", "rw_a0", "rw_a1", "rw_a2", "rw_g1", "rw_g2",
             "rw_kk", "rw_ka", "rw_rk", "rw_wr", "rw_wk", "rw_wv", "rw_wo", "rw_lnx_g", "rw_lnx_b"]
    loc = locals()
    w = {n: loc[n] for n in names}
    mom = {n: loc["m_" + n] for n in names}
    vel = {n: loc["v_" + n] for n in names}
    S = x.shape[1]

    ffn3 = ("ffn_w_gate", "ffn_w_up", "ffn_w_down")
    rw_mats = ("rw_w1", "rw_w2", "rw_a1", "rw_a2", "rw_g1", "rw_g2", "rw_wr", "rw_wk", "rw_wv", "rw_wo")
    cols_split = ("attn_w_out", "rw_w2", "rw_a2", "rw_g2")
    shard = {"vec": jnp.concatenate([w[n].reshape(-1, 256) for n in VEC_ROWS], axis=0)}
    for n in ffn3:
        for l in range(2):
            for j in range(2):
                shard[n, l, j] = w[n][l, j].astype(BF16)
    for n in ("attn_w_in", "attn_w_out") + rw_mats:
        shard[n] = w[n].reshape(-1, w[n].shape[-1]).astype(BF16)
    ffn_keys = lambda l, j: [(n, l, j) for n in ffn3]
    w_groups = {"first": ["vec"] + ffn_keys(0, 0) + ["attn_w_in", "attn_w_out"],
                "ffn_fwd_00": [("ffn_w_gate", 0, 1), ("ffn_w_up", 0, 1)],
                "sb_fwd": ffn_keys(1, 0) + list(rw_mats) + [("ffn_w_down", 0, 1), ("ffn_w_down", 1, 1)],
                "dil_fwd": [("ffn_w_gate", 1, 1), ("ffn_w_up", 1, 1)]}
    label = lambda key: key if isinstance(key, str) else f"{key[0]}_{key[1]}{key[2]}"

    class Streamed(Weights):
        def ride(self, kernel_name):
            keys_ = w_groups.get(kernel_name)
            return gather_pushes([shard[k] for k in keys_]) if keys_ else None

        def arrived(self, kernel_name, outs):
            if not outs:
                return
            for key, g in zip(w_groups[kernel_name], gather_swap(f"gather_swap_{kernel_name}", outs)):
                if key == "vec":
                    vec_full = _unshard_cols(g)
                    self["ffn_norm"] = [[vec_full[2 * l + j][None] for j in range(2)] for l in range(2)]
                    self["rw_mix"] = vec_full[4:10]
                    for i, n in enumerate(("rw_w0", "rw_a0", "rw_kk", "rw_ka", "rw_lnx_g", "rw_lnx_b")):
                        self[n] = vec_full[10 + i][None]
                elif key == "attn_w_in":
                    self[key] = [g[p] for p in range(N_CHIPS)]
                elif key in cols_split:
                    self[key] = _unshard_cols(g)
                elif isinstance(key, str):
                    self[key] = g.reshape(D, -1)
                else:
                    self[key] = g

    buckets = _bucket_maps()
    W = Streamed({"mix_norm": [mix_norm[0:1], mix_norm[1:2]], "attn_q_norm": attn_q_norm, "attn_k_norm": attn_k_norm,
                  "rw_rk": rw_rk[0][:, None, :], "buckets": buckets, "bias_mat": bias_table(rel_bias, buckets)})
    W.arrived("first", exchange("gather_weights", W.ride("first")))

    def slots(key, G):
        if key == "vec":
            rows = [G[("ffn_norm", l, j)] for l in range(2) for j in range(2)] + [G["rw_mix"]] + \
                   [G[n] for n in ("rw_w0", "rw_a0", "rw_kk", "rw_ka", "rw_lnx_g", "rw_lnx_b")]
            return _shard_cols(jnp.concatenate(rows, axis=0))
        if key == "attn_w_in":
            return jnp.stack(G[key])
        if key in cols_split:
            return _shard_cols(G[key])
        if isinstance(key, str):
            return G[key].reshape(N_CHIPS, D // N_CHIPS, -1)
        return G[key]

    g_groups = {"early": ffn_keys(1, 1) + ffn_keys(1, 0) + ffn_keys(0, 1) + list(rw_mats) + ["attn_w_out"],
                "late": ["vec", "attn_w_in"] + ffn_keys(0, 0)}
    wire = lambda keys: [F32 if k == "vec" else BF16 for k in keys]
    part = {}

    def grads_early(G):
        keys = g_groups["early"]
        names_ = [label(k) for k in keys]
        split, swap_pushes = reduce_swap([slots(k, G) for k in keys])

        def swapped(theirs):
            chip_sum, pushes = reduce_sum(names_, split, theirs, wire(keys))
            return pushes, lambda landed: part.update(zip(keys, reduce_end("early", names_, chip_sum, landed)))

        return swap_pushes, swapped

    loss_part, dx, G = _forward_backward(x[0], loss_target[0], W, grads_early)
    loss = lax.psum(loss_part[0, 0], ("x", "y", "c"))
    keys = g_groups["late"]
    chip_sum, pushes = reduce_begin("late", [label(k) for k in keys], [slots(k, G) for k in keys], wire(keys))
    part.update(zip(keys, reduce_end("late", [label(k) for k in keys], chip_sum, exchange("scatter_grads", pushes))))

    rep = jnp.concatenate([G[("mix_norm", 0)][0] + G[("mix_norm", 0)][1] + G[("mix_norm", 0)][2] + G[("mix_norm", 0)][3],
                           G[("mix_norm", 1)]], axis=0).reshape(16, 128)
    rep = jnp.concatenate([rep, G["rel_bias"], jnp.pad(G["attn_q_norm"], ((0, 0), (0, 64))),
                           jnp.pad(G["attn_k_norm"], ((0, 0), (0, 64))), G["rw_rk"].reshape(8, 128),
                           jnp.zeros((2, 128), F32)], axis=0)
    rep_sum = sum_slots("sum_replicated", gather_all([rep])[0])
    g_rep = {
        "mix_norm": rep_sum[0:16].reshape(2, D),
        "rel_bias": jnp.transpose(rep_sum[16:28, :N_BUCKETS]),
        "attn_q_norm": rep_sum[28:29, :HEAD], "attn_k_norm": rep_sum[29:30, :HEAD],
        "rw_rk": rep_sum[30:38].reshape(1, RW_H, HEAD),
    }

    out = {}

    def adam(n, ga, gb):
        shp = w[n].shape
        to2 = lambda a: a.reshape(-1, shp[-1])
        res = adam_step(f"adam_{n}", to2(ga), None if gb is None else to2(gb), to2(w[n]), to2(mom[n]), to2(vel[n]))
        out[n] = tuple(r.reshape(shp) for r in res)

    for n in ffn3:
        out[n] = tuple(adam_ffn(f"adam_{n}", [part[n, l, j] for l in range(2) for j in range(2)], w[n], mom[n], vel[n],
                                transposed=n != "ffn_w_down"))
    for n in ("attn_w_in", "attn_w_out") + rw_mats:
        adam(n, part[n], None)
    rows = {"ffn_norm": (0, 4), "rw_mix": (4, 10), "rw_w0": (10, 11), "rw_a0": (11, 12), "rw_kk": (12, 13),
            "rw_ka": (13, 14), "rw_lnx_g": (14, 15), "rw_lnx_b": (15, 16)}
    for n, (lo, hi) in rows.items():
        adam(n, part["vec"][lo:hi], None)
    for n, gv in g_rep.items():
        adam(n, gv, None)

    grads = [out[n][0] for n in names]
    deltas = [out[n][1] for n in names]
    new_m = [out[n][2] for n in names]
    new_v = [out[n][3] for n in names]
    return (loss, dx[None], *grads, *deltas, *new_m, *new_v)
```
